```python
import math
import jax, jax.numpy as jnp
from jax import lax
import numpy as np

D_MODEL = 1024
BATCH = 8
SEQ = 4096
DEPTH = 1

MLA_HEADS = 8
MLA_NOPE_DIM = 64
MLA_ROPE_DIM = 32
MLA_V_DIM = 64
MLA_Q_RANK = 384
MLA_KV_RANK = 128
MLA_WIDTH = MLA_HEADS * MLA_V_DIM
MLA_QK_DIM = MLA_NOPE_DIM + MLA_ROPE_DIM
ROPE_THETA = 10000.0
Q_BLOCK = 128
HGRN_HEADS = 4
HGRN_HEAD_DIM = 128
HGRN_WIDTH = HGRN_HEADS * HGRN_HEAD_DIM
HGRN_CHUNK = 64
D_MIX = MLA_WIDTH + HGRN_WIDTH
D_IN_PROJ = MLA_Q_RANK + MLA_KV_RANK + MLA_ROPE_DIM + 4 * HGRN_WIDTH
D_FF = -(-8 * D_MODEL // (3 * 256)) * 256
EPS = 1e-6

kernel_name = 'hymba_mla_hgrn2_sandwich_block'


def rmsnorm(x, w):
    x32 = x.astype(jnp.float32)
    inv = lax.rsqrt(jnp.mean(x32 * x32, axis=-1, keepdims=True) + EPS)
    return (x32 * inv).astype(x.dtype) * w


def rope_cos_sin(positions, dtype):
    inv_freq = 1.0 / (ROPE_THETA ** (jnp.arange(0, MLA_ROPE_DIM, 2, dtype=jnp.float32) / MLA_ROPE_DIM))
    ang = positions.astype(jnp.float32)[..., None] * inv_freq
    return jnp.cos(ang).astype(dtype), jnp.sin(ang).astype(dtype)


def apply_rope(x, cos, sin):
    x1, x2 = jnp.split(x, 2, axis=-1)
    return jnp.concatenate([x1 * cos - x2 * sin, x1 * sin + x2 * cos], axis=-1)


def mla_mixer(c_q, c_kv, k_rope_raw, positions, q_norm_w, w_uq, kv_norm_w, w_ukv, out_norm_w):
    B, S, _ = c_q.shape
    nb = S // Q_BLOCK
    q = jnp.einsum('bsr,rhd->bshd', rmsnorm(c_q, q_norm_w), w_uq)
    q_nope, q_rope = q[..., :MLA_NOPE_DIM], q[..., MLA_NOPE_DIM:]
    kv = jnp.einsum('bsr,rhd->bshd', rmsnorm(c_kv, kv_norm_w), w_ukv)
    k_nope, v = kv[..., :MLA_NOPE_DIM], kv[..., MLA_NOPE_DIM:]
    cos, sin = rope_cos_sin(positions, q.dtype)
    q_rope = apply_rope(q_rope, cos[:, :, None, :], sin[:, :, None, :])
    k_rope = apply_rope(k_rope_raw, cos, sin)
    scale = MLA_QK_DIM ** -0.5
    qn_b = q_nope.reshape(B, nb, Q_BLOCK, MLA_HEADS, MLA_NOPE_DIM).transpose(1, 0, 2, 3, 4)
    qr_b = q_rope.reshape(B, nb, Q_BLOCK, MLA_HEADS, MLA_ROPE_DIM).transpose(1, 0, 2, 3, 4)
    key_idx = jnp.arange(S)

    def block(args):
        qn, qr, blk = args
        s = (jnp.einsum('bqhd,bkhd->bhqk', qn, k_nope)
             + jnp.einsum('bqhd,bkd->bhqk', qr, k_rope)).astype(jnp.float32) * scale
        q_idx = blk * Q_BLOCK + jnp.arange(Q_BLOCK)
        mask = key_idx[None, :] <= q_idx[:, None]
        p = jax.nn.softmax(jnp.where(mask, s, -jnp.inf), axis=-1).astype(v.dtype)
        return jnp.einsum('bhqk,bkhd->bqhd', p, v)

    o = lax.map(block, (qn_b, qr_b, jnp.arange(nb)))
    o = o.transpose(1, 0, 2, 3, 4).reshape(B, S, MLA_HEADS, MLA_V_DIM)
    o = rmsnorm(o, out_norm_w.reshape(MLA_HEADS, MLA_V_DIM))
    return o.reshape(B, S, MLA_WIDTH)


def hgrn2_mixer(q_raw, f_raw, i_raw, g_raw, lb, out_norm_w):
    B, S, _ = q_raw.shape
    H, D, C = HGRN_HEADS, HGRN_HEAD_DIM, HGRN_CHUNK
    nc = S // C
    lb32 = lb.astype(jnp.float32)
    f = lb32 + (1.0 - lb32) * jax.nn.sigmoid(f_raw.astype(jnp.float32))
    log_f = jnp.log(f)
    k = 1.0 - f
    q = jax.nn.silu(q_raw.astype(jnp.float32))
    v = i_raw.astype(jnp.float32)

    def to_chunks(t):
        return t.reshape(B, nc, C, H, D).transpose(1, 0, 3, 2, 4)

    causal = jnp.tril(jnp.ones((C, C), dtype=bool))[:, :, None]

    def step(state, inp):
        qc, kc, vc, lfc = inp
        b = jnp.cumsum(lfc, axis=2)
        o_inter = jnp.einsum('bhtk,bhkv->bhtv', qc * jnp.exp(b), state)
        diff = b[:, :, :, None, :] - b[:, :, None, :, :]
        decay = jnp.exp(jnp.where(causal, diff, -jnp.inf))
        a = jnp.einsum('bhtk,bhtsk,bhsk->bhts', qc, decay, kc)
        o_intra = jnp.einsum('bhts,bhsv->bhtv', a, vc)
        b_last = b[:, :, -1:, :]
        k_dec = kc * jnp.exp(b_last - b)
        state = jnp.exp(b_last[:, :, 0, :])[..., None] * state + jnp.einsum('bhsk,bhsv->bhkv', k_dec, vc)
        return state, o_inter + o_intra

    s0 = jnp.zeros((B, H, D, D), jnp.float32)
    _, o = lax.scan(step, s0, (to_chunks(q), to_chunks(k), to_chunks(v), to_chunks(log_f)))
    o = o.transpose(1, 0, 3, 2, 4).reshape(B, S, H, D)
    o = rmsnorm(o, out_norm_w.reshape(H, D).astype(jnp.float32))
    o = o.reshape(B, S, HGRN_WIDTH) * jax.nn.silu(g_raw.astype(jnp.float32))
    return o.astype(q_raw.dtype)


def _fwd_setup_inputs(seed: int = 0) -> dict:
    key = jax.random.key(seed)
    ks = jax.random.split(key, 24)
    nrm = lambda k, shape, fan_in: jax.random.normal(k, shape, jnp.float32) * fan_in ** -0.5
    gain = lambda k, shape: 1.0 + 0.02 * jax.random.normal(k, shape, jnp.float32)
    x = jax.random.normal(ks[0], (BATCH, SEQ, D_MODEL), jnp.float32)
    offset = jax.random.randint(ks[1], (BATCH, 1), 0, 2048, dtype=jnp.int32)
    positions = (offset + jnp.arange(SEQ, dtype=jnp.int32)[None, :]).astype(jnp.int32)
    lb_base = jnp.concatenate([-jnp.ones((1, HGRN_WIDTH), jnp.float32),
                               jnp.ones((DEPTH, HGRN_WIDTH), jnp.float32)], axis=0)
    hgrn_lb_logits = lb_base + 0.1 * jax.random.normal(ks[2], (DEPTH + 1, HGRN_WIDTH), jnp.float32)
    return {
        'x': x,
        'positions': positions,
        'attn_pre_norm': gain(ks[3], (DEPTH, D_MODEL)),
        'w_in': nrm(ks[4], (DEPTH, D_MODEL, D_IN_PROJ), D_MODEL),
        'mla_q_norm': gain(ks[5], (DEPTH, MLA_Q_RANK)),
        'mla_w_uq': nrm(ks[6], (DEPTH, MLA_Q_RANK, MLA_HEADS, MLA_QK_DIM), MLA_Q_RANK),
        'mla_kv_norm': gain(ks[7], (DEPTH, MLA_KV_RANK)),
        'mla_w_ukv': nrm(ks[8], (DEPTH, MLA_KV_RANK, MLA_HEADS, MLA_NOPE_DIM + MLA_V_DIM), MLA_KV_RANK),
        'mla_out_norm': gain(ks[9], (DEPTH, MLA_WIDTH)),
        'hgrn_lb_logits': hgrn_lb_logits,
        'hgrn_out_norm': gain(ks[10], (DEPTH, HGRN_WIDTH)),
        'w_out': nrm(ks[11], (DEPTH, D_MIX, D_MODEL), D_MIX),
        'attn_post_norm': gain(ks[12], (DEPTH, D_MODEL)),
        'ffn_pre_norm': gain(ks[13], (DEPTH, D_MODEL)),
        'w_gate': nrm(ks[14], (DEPTH, D_MODEL, D_FF), D_MODEL),
        'w_up': nrm(ks[15], (DEPTH, D_MODEL, D_FF), D_MODEL),
        'w_down': nrm(ks[16], (DEPTH, D_FF, D_MODEL), D_FF),
        'ffn_post_norm': gain(ks[17], (DEPTH, D_MODEL)),
    }


def _fwd_reference(x, positions, attn_pre_norm, w_in, mla_q_norm, mla_w_uq, mla_kv_norm, mla_w_ukv,
              mla_out_norm, hgrn_lb_logits, hgrn_out_norm, w_out, attn_post_norm, ffn_pre_norm,
              w_gate, w_up, w_down, ffn_post_norm):
    lb_all = jnp.cumsum(jax.nn.softmax(hgrn_lb_logits.astype(jnp.float32), axis=0), axis=0)[:DEPTH]
    s1 = MLA_Q_RANK
    s2 = s1 + MLA_KV_RANK
    s3 = s2 + MLA_ROPE_DIM
    s4 = s3 + HGRN_WIDTH
    s5 = s4 + HGRN_WIDTH
    s6 = s5 + HGRN_WIDTH
    h = x
    for l in range(DEPTH):
        u = rmsnorm(h, attn_pre_norm[l])
        xp = jnp.einsum('bsd,de->bse', u, w_in[l])
        c_q, c_kv, k_rope_raw = xp[..., :s1], xp[..., s1:s2], xp[..., s2:s3]
        hq, hf, hi, hg = xp[..., s3:s4], xp[..., s4:s5], xp[..., s5:s6], xp[..., s6:]
        o_mla = mla_mixer(c_q, c_kv, k_rope_raw, positions, mla_q_norm[l], mla_w_uq[l],
                          mla_kv_norm[l], mla_w_ukv[l], mla_out_norm[l])
        o_hgrn = hgrn2_mixer(hq, hf, hi, hg, lb_all[l], hgrn_out_norm[l])
        mix = jnp.concatenate([o_mla, o_hgrn.astype(o_mla.dtype)], axis=-1)
        h = h + rmsnorm(jnp.einsum('bse,ed->bsd', mix, w_out[l]), attn_post_norm[l])
        z = rmsnorm(h, ffn_pre_norm[l])
        ff = jax.nn.silu(jnp.einsum('bsd,df->bsf', z, w_gate[l])) * jnp.einsum('bsd,df->bsf', z, w_up[l])
        h = h + rmsnorm(jnp.einsum('bsf,fd->bsd', ff, w_down[l]), ffn_post_norm[l])
    return h


import jax as _jax
import jax.numpy as _jnp

TWIN_FORMAT = 'train_step'
FWD_PARAMS = ['x', 'positions', 'attn_pre_norm', 'w_in', 'mla_q_norm', 'mla_w_uq', 'mla_kv_norm', 'mla_w_ukv', 'mla_out_norm', 'hgrn_lb_logits', 'hgrn_out_norm', 'w_out', 'attn_post_norm', 'ffn_pre_norm', 'w_gate', 'w_up', 'w_down', 'ffn_post_norm']
TWIN_WEIGHTS = ['attn_pre_norm', 'w_in', 'mla_q_norm', 'mla_w_uq', 'mla_kv_norm', 'mla_w_ukv', 'mla_out_norm', 'hgrn_lb_logits', 'hgrn_out_norm', 'w_out', 'attn_post_norm', 'ffn_pre_norm', 'w_gate', 'w_up', 'w_down', 'ffn_post_norm']
TWIN_DIFF_INPUT = 'x'
TWIN_INPUTS = ['x', 'positions', 'attn_pre_norm', 'w_in', 'mla_q_norm', 'mla_w_uq', 'mla_kv_norm', 'mla_w_ukv', 'mla_out_norm', 'hgrn_lb_logits', 'hgrn_out_norm', 'w_out', 'attn_post_norm', 'ffn_pre_norm', 'w_gate', 'w_up', 'w_down', 'ffn_post_norm', 'loss_target', 'm_attn_pre_norm', 'm_w_in', 'm_mla_q_norm', 'm_mla_w_uq', 'm_mla_kv_norm', 'm_mla_w_ukv', 'm_mla_out_norm', 'm_hgrn_lb_logits', 'm_hgrn_out_norm', 'm_w_out', 'm_attn_post_norm', 'm_ffn_pre_norm', 'm_w_gate', 'm_w_up', 'm_w_down', 'm_ffn_post_norm', 'v_attn_pre_norm', 'v_w_in', 'v_mla_q_norm', 'v_mla_w_uq', 'v_mla_kv_norm', 'v_mla_w_ukv', 'v_mla_out_norm', 'v_hgrn_lb_logits', 'v_hgrn_out_norm', 'v_w_out', 'v_attn_post_norm', 'v_ffn_pre_norm', 'v_w_gate', 'v_w_up', 'v_w_down', 'v_ffn_post_norm']
TWIN_OUTPUTS = ['loss', 'grad_x', 'grad_attn_pre_norm', 'grad_w_in', 'grad_mla_q_norm', 'grad_mla_w_uq', 'grad_mla_kv_norm', 'grad_mla_w_ukv', 'grad_mla_out_norm', 'grad_hgrn_lb_logits', 'grad_hgrn_out_norm', 'grad_w_out', 'grad_attn_post_norm', 'grad_ffn_pre_norm', 'grad_w_gate', 'grad_w_up', 'grad_w_down', 'grad_ffn_post_norm', 'delta_attn_pre_norm', 'delta_w_in', 'delta_mla_q_norm', 'delta_mla_w_uq', 'delta_mla_kv_norm', 'delta_mla_w_ukv', 'delta_mla_out_norm', 'delta_hgrn_lb_logits', 'delta_hgrn_out_norm', 'delta_w_out', 'delta_attn_post_norm', 'delta_ffn_pre_norm', 'delta_w_gate', 'delta_w_up', 'delta_w_down', 'delta_ffn_post_norm', 'new_m_attn_pre_norm', 'new_m_w_in', 'new_m_mla_q_norm', 'new_m_mla_w_uq', 'new_m_mla_kv_norm', 'new_m_mla_w_ukv', 'new_m_mla_out_norm', 'new_m_hgrn_lb_logits', 'new_m_hgrn_out_norm', 'new_m_w_out', 'new_m_attn_post_norm', 'new_m_ffn_pre_norm', 'new_m_w_gate', 'new_m_w_up', 'new_m_w_down', 'new_m_ffn_post_norm', 'new_v_attn_pre_norm', 'new_v_w_in', 'new_v_mla_q_norm', 'new_v_mla_w_uq', 'new_v_mla_kv_norm', 'new_v_mla_w_ukv', 'new_v_mla_out_norm', 'new_v_hgrn_lb_logits', 'new_v_hgrn_out_norm', 'new_v_w_out', 'new_v_attn_post_norm', 'new_v_ffn_pre_norm', 'new_v_w_gate', 'new_v_w_up', 'new_v_w_down', 'new_v_ffn_post_norm']
TWIN_LEAF_KINDS = {'loss': 'loss', 'grad_x': 'grad_x', 'grad_attn_pre_norm': 'grad_w', 'grad_w_in': 'grad_w', 'grad_mla_q_norm': 'grad_w', 'grad_mla_w_uq': 'grad_w', 'grad_mla_kv_norm': 'grad_w', 'grad_mla_w_ukv': 'grad_w', 'grad_mla_out_norm': 'grad_w', 'grad_hgrn_lb_logits': 'grad_w', 'grad_hgrn_out_norm': 'grad_w', 'grad_w_out': 'grad_w', 'grad_attn_post_norm': 'grad_w', 'grad_ffn_pre_norm': 'grad_w', 'grad_w_gate': 'grad_w', 'grad_w_up': 'grad_w', 'grad_w_down': 'grad_w', 'grad_ffn_post_norm': 'grad_w', 'delta_attn_pre_norm': 'delta_w', 'delta_w_in': 'delta_w', 'delta_mla_q_norm': 'delta_w', 'delta_mla_w_uq': 'delta_w', 'delta_mla_kv_norm': 'delta_w', 'delta_mla_w_ukv': 'delta_w', 'delta_mla_out_norm': 'delta_w', 'delta_hgrn_lb_logits': 'delta_w', 'delta_hgrn_out_norm': 'delta_w', 'delta_w_out': 'delta_w', 'delta_attn_post_norm': 'delta_w', 'delta_ffn_pre_norm': 'delta_w', 'delta_w_gate': 'delta_w', 'delta_w_up': 'delta_w', 'delta_w_down': 'delta_w', 'delta_ffn_post_norm': 'delta_w', 'new_m_attn_pre_norm': 'new_m', 'new_m_w_in': 'new_m', 'new_m_mla_q_norm': 'new_m', 'new_m_mla_w_uq': 'new_m', 'new_m_mla_kv_norm': 'new_m', 'new_m_mla_w_ukv': 'new_m', 'new_m_mla_out_norm': 'new_m', 'new_m_hgrn_lb_logits': 'new_m', 'new_m_hgrn_out_norm': 'new_m', 'new_m_w_out': 'new_m', 'new_m_attn_post_norm': 'new_m', 'new_m_ffn_pre_norm': 'new_m', 'new_m_w_gate': 'new_m', 'new_m_w_up': 'new_m', 'new_m_w_down': 'new_m', 'new_m_ffn_post_norm': 'new_m', 'new_v_attn_pre_norm': 'new_v', 'new_v_w_in': 'new_v', 'new_v_mla_q_norm': 'new_v', 'new_v_mla_w_uq': 'new_v', 'new_v_mla_kv_norm': 'new_v', 'new_v_mla_w_ukv': 'new_v', 'new_v_mla_out_norm': 'new_v', 'new_v_hgrn_lb_logits': 'new_v', 'new_v_hgrn_out_norm': 'new_v', 'new_v_w_out': 'new_v', 'new_v_attn_post_norm': 'new_v', 'new_v_ffn_pre_norm': 'new_v', 'new_v_w_gate': 'new_v', 'new_v_w_up': 'new_v', 'new_v_w_down': 'new_v', 'new_v_ffn_post_norm': 'new_v'}


def _forward(args):
    return _fwd_reference(*[args[k] for k in FWD_PARAMS])


def _output_shape():
    out = _jax.eval_shape(lambda: _forward(_fwd_setup_inputs(0)))
    return out.shape, out.dtype

N_MICROBATCH = 1
ADAM_LR = 0.001
ADAM_B1 = 0.9
ADAM_B2 = 0.999
ADAM_EPS = 1e-08
ADAM_WD = 0.01
ADAM_STEP = 10
PER_EXAMPLE_BATCH_AXIS = {'x': 0, 'positions': 0, 'loss_target': 0}
SHARED_INPUTS = []
_WEIGHT_DTYPES = {'attn_pre_norm': _jnp.float32, 'w_in': _jnp.float32, 'mla_q_norm': _jnp.float32, 'mla_w_uq': _jnp.float32, 'mla_kv_norm': _jnp.float32, 'mla_w_ukv': _jnp.float32, 'mla_out_norm': _jnp.float32, 'hgrn_lb_logits': _jnp.float32, 'hgrn_out_norm': _jnp.float32, 'w_out': _jnp.float32, 'attn_post_norm': _jnp.float32, 'ffn_pre_norm': _jnp.float32, 'w_gate': _jnp.float32, 'w_up': _jnp.float32, 'w_down': _jnp.float32, 'ffn_post_norm': _jnp.float32}
MOMENT_SCALE = {'attn_pre_norm': 8.244043e-01, 'w_in': 5.075470e-01, 'mla_q_norm': 1.129345e+00, 'mla_w_uq': 7.082488e-01, 'mla_kv_norm': 3.712269e+00, 'mla_w_ukv': 9.424115e-01, 'mla_out_norm': 1.022415e+00, 'hgrn_lb_logits': 8.685582e-03, 'hgrn_out_norm': 4.331149e-01, 'w_out': 7.760781e-01, 'attn_post_norm': 3.187725e+01, 'ffn_pre_norm': 6.919823e-01, 'w_gate': 2.212100e-01, 'w_up': 3.428690e-01, 'w_down': 5.617646e-01, 'ffn_post_norm': 3.183107e+01}


def _to_microbatches(a, axis):
    t = _jnp.moveaxis(a, axis, 0)
    t = t.reshape((N_MICROBATCH, t.shape[0] // N_MICROBATCH) + t.shape[1:])
    return _jnp.moveaxis(t, 1, axis + 1)


def setup_inputs(seed: int = 0) -> dict:
    inp = _fwd_setup_inputs(seed)
    key = _jax.random.fold_in(_jax.random.key(seed), 7919)
    shape, _ = _output_shape()
    out = dict(inp)
    out["loss_target"] = _jax.random.normal(_jax.random.fold_in(key, 0), shape, _jnp.float32)
    for i, name in enumerate(TWIN_WEIGHTS):
        w = inp[name].astype(_jnp.float32)
        if MOMENT_SCALE is None:
            s = _jnp.sqrt(_jnp.mean(_jnp.square(w)) + 1e-30)
        else:
            s = MOMENT_SCALE[name]
        km, kv = _jax.random.split(_jax.random.fold_in(key, i + 1))
        out[name] = w
        out["m_" + name] = s * _jax.random.normal(km, w.shape, _jnp.float32)
        out["v_" + name] = (s * s) * _jax.random.uniform(kv, w.shape, _jnp.float32, 0.5, 1.5)
    if N_MICROBATCH > 1:
        for name, axis in PER_EXAMPLE_BATCH_AXIS.items():
            out[name] = _to_microbatches(out[name], axis)
    return {'x': out['x'], 'positions': out['positions'], 'attn_pre_norm': out['attn_pre_norm'], 'w_in': out['w_in'], 'mla_q_norm': out['mla_q_norm'], 'mla_w_uq': out['mla_w_uq'], 'mla_kv_norm': out['mla_kv_norm'], 'mla_w_ukv': out['mla_w_ukv'], 'mla_out_norm': out['mla_out_norm'], 'hgrn_lb_logits': out['hgrn_lb_logits'], 'hgrn_out_norm': out['hgrn_out_norm'], 'w_out': out['w_out'], 'attn_post_norm': out['attn_post_norm'], 'ffn_pre_norm': out['ffn_pre_norm'], 'w_gate': out['w_gate'], 'w_up': out['w_up'], 'w_down': out['w_down'], 'ffn_post_norm': out['ffn_post_norm'], 'loss_target': out['loss_target'], 'm_attn_pre_norm': out['m_attn_pre_norm'], 'm_w_in': out['m_w_in'], 'm_mla_q_norm': out['m_mla_q_norm'], 'm_mla_w_uq': out['m_mla_w_uq'], 'm_mla_kv_norm': out['m_mla_kv_norm'], 'm_mla_w_ukv': out['m_mla_w_ukv'], 'm_mla_out_norm': out['m_mla_out_norm'], 'm_hgrn_lb_logits': out['m_hgrn_lb_logits'], 'm_hgrn_out_norm': out['m_hgrn_out_norm'], 'm_w_out': out['m_w_out'], 'm_attn_post_norm': out['m_attn_post_norm'], 'm_ffn_pre_norm': out['m_ffn_pre_norm'], 'm_w_gate': out['m_w_gate'], 'm_w_up': out['m_w_up'], 'm_w_down': out['m_w_down'], 'm_ffn_post_norm': out['m_ffn_post_norm'], 'v_attn_pre_norm': out['v_attn_pre_norm'], 'v_w_in': out['v_w_in'], 'v_mla_q_norm': out['v_mla_q_norm'], 'v_mla_w_uq': out['v_mla_w_uq'], 'v_mla_kv_norm': out['v_mla_kv_norm'], 'v_mla_w_ukv': out['v_mla_w_ukv'], 'v_mla_out_norm': out['v_mla_out_norm'], 'v_hgrn_lb_logits': out['v_hgrn_lb_logits'], 'v_hgrn_out_norm': out['v_hgrn_out_norm'], 'v_w_out': out['v_w_out'], 'v_attn_post_norm': out['v_attn_post_norm'], 'v_ffn_pre_norm': out['v_ffn_pre_norm'], 'v_w_gate': out['v_w_gate'], 'v_w_up': out['v_w_up'], 'v_w_down': out['v_w_down'], 'v_ffn_post_norm': out['v_ffn_post_norm']}


def _loss(weights, diff, rest, loss_target):
    with _jax.named_scope("forward"):
        args = {**rest, TWIN_DIFF_INPUT: diff, **{k: w.astype(_WEIGHT_DTYPES[k]) for k, w in weights.items()}}
        y = _forward(args)
    with _jax.named_scope("loss_head"):
        err = _jnp.square(y.astype(_jnp.float32) - loss_target)
        return 0.5 * _jnp.sum(_jnp.mean(err, axis=-1)) if err.ndim else 0.5 * err


def _adamw(w, g, m, v):
    m = ADAM_B1 * m + (1.0 - ADAM_B1) * g
    v = ADAM_B2 * v + (1.0 - ADAM_B2) * _jnp.square(g)
    m_hat = m / (1.0 - ADAM_B1 ** ADAM_STEP)
    v_hat = v / (1.0 - ADAM_B2 ** ADAM_STEP)
    delta = -ADAM_LR * (m_hat / (_jnp.sqrt(v_hat) + ADAM_EPS) + ADAM_WD * w)
    return delta, m, v


def reference(x, positions, attn_pre_norm, w_in, mla_q_norm, mla_w_uq, mla_kv_norm, mla_w_ukv, mla_out_norm, hgrn_lb_logits, hgrn_out_norm, w_out, attn_post_norm, ffn_pre_norm, w_gate, w_up, w_down, ffn_post_norm, loss_target, m_attn_pre_norm, m_w_in, m_mla_q_norm, m_mla_w_uq, m_mla_kv_norm, m_mla_w_ukv, m_mla_out_norm, m_hgrn_lb_logits, m_hgrn_out_norm, m_w_out, m_attn_post_norm, m_ffn_pre_norm, m_w_gate, m_w_up, m_w_down, m_ffn_post_norm, v_attn_pre_norm, v_w_in, v_mla_q_norm, v_mla_w_uq, v_mla_kv_norm, v_mla_w_ukv, v_mla_out_norm, v_hgrn_lb_logits, v_hgrn_out_norm, v_w_out, v_attn_post_norm, v_ffn_pre_norm, v_w_gate, v_w_up, v_w_down, v_ffn_post_norm):
    given = dict(x=x, positions=positions, attn_pre_norm=attn_pre_norm, w_in=w_in, mla_q_norm=mla_q_norm, mla_w_uq=mla_w_uq, mla_kv_norm=mla_kv_norm, mla_w_ukv=mla_w_ukv, mla_out_norm=mla_out_norm, hgrn_lb_logits=hgrn_lb_logits, hgrn_out_norm=hgrn_out_norm, w_out=w_out, attn_post_norm=attn_post_norm, ffn_pre_norm=ffn_pre_norm, w_gate=w_gate, w_up=w_up, w_down=w_down, ffn_post_norm=ffn_post_norm, loss_target=loss_target, m_attn_pre_norm=m_attn_pre_norm, m_w_in=m_w_in, m_mla_q_norm=m_mla_q_norm, m_mla_w_uq=m_mla_w_uq, m_mla_kv_norm=m_mla_kv_norm, m_mla_w_ukv=m_mla_w_ukv, m_mla_out_norm=m_mla_out_norm, m_hgrn_lb_logits=m_hgrn_lb_logits, m_hgrn_out_norm=m_hgrn_out_norm, m_w_out=m_w_out, m_attn_post_norm=m_attn_post_norm, m_ffn_pre_norm=m_ffn_pre_norm, m_w_gate=m_w_gate, m_w_up=m_w_up, m_w_down=m_w_down, m_ffn_post_norm=m_ffn_post_norm, v_attn_pre_norm=v_attn_pre_norm, v_w_in=v_w_in, v_mla_q_norm=v_mla_q_norm, v_mla_w_uq=v_mla_w_uq, v_mla_kv_norm=v_mla_kv_norm, v_mla_w_ukv=v_mla_w_ukv, v_mla_out_norm=v_mla_out_norm, v_hgrn_lb_logits=v_hgrn_lb_logits, v_hgrn_out_norm=v_hgrn_out_norm, v_w_out=v_w_out, v_attn_post_norm=v_attn_post_norm, v_ffn_pre_norm=v_ffn_pre_norm, v_w_gate=v_w_gate, v_w_up=v_w_up, v_w_down=v_w_down, v_ffn_post_norm=v_ffn_post_norm)
    weights = {n: given[n] for n in TWIN_WEIGHTS}
    shared = {n: given[n] for n in SHARED_INPUTS}
    per_example = {n: given[n] for n in ['x', 'positions']}
    grad_fn = _jax.value_and_grad(_loss, argnums=(0, 1))

    def one_microbatch(ex, loss_target):
        ex = dict(ex)
        diff = ex.pop(TWIN_DIFF_INPUT)
        return grad_fn(weights, diff, {**shared, **ex}, loss_target)

    if N_MICROBATCH == 1:
        loss, (grad_w, grad_x) = one_microbatch(per_example, given["loss_target"])
    else:
        def body(carry, xs):
            loss_sum, grad_sum = carry
            l_k, (gw_k, gx_k) = one_microbatch(xs[0], xs[1])
            with _jax.named_scope("update"):
                return (loss_sum + l_k, _jax.tree.map(_jnp.add, grad_sum, gw_k)), gx_k

        init = (_jnp.zeros((), _jnp.float32), _jax.tree.map(_jnp.zeros_like, weights))
        (loss, grad_w), grad_x = _jax.lax.scan(body, init, (per_example, given["loss_target"]))
    with _jax.named_scope("update"):
        delta_w, new_m, new_v = {}, {}, {}
        for n in TWIN_WEIGHTS:
            delta_w[n], new_m[n], new_v[n] = _adamw(weights[n], grad_w[n], given["m_" + n], given["v_" + n])
    return (loss, grad_x, *[grad_w[n] for n in TWIN_WEIGHTS], *[delta_w[n] for n in TWIN_WEIGHTS],
            *[new_m[n] for n in TWIN_WEIGHTS], *[new_v[n] for n in TWIN_WEIGHTS])
```

```python
import functools

import jax
import jax.numpy as jnp
from jax import lax
from jax.experimental import pallas as pl
from jax.experimental.pallas import tpu as pltpu

F32 = jnp.float32
BF16 = jnp.bfloat16
MXU_DTYPE = BF16

D_MODEL = 1024
MLA_HEADS = 8
MLA_NOPE = 64
MLA_ROPE = 32
MLA_V = 64
MLA_QK = MLA_NOPE + MLA_ROPE
Q_RANK = 384
KV_RANK = 128
MLA_WIDTH = MLA_HEADS * MLA_V
HEAD_PAD = 128
HGRN_HEADS = 4
HGRN_DIM = 128
HGRN_WIDTH = HGRN_HEADS * HGRN_DIM
CHUNK = 64
SUB = 16
D_IN = Q_RANK + KV_RANK + MLA_ROPE + 4 * HGRN_WIDTH
D_IN_ARR = Q_RANK + KV_RANK + HEAD_PAD + 4 * HGRN_WIDTH
D_FF = 2816
N_CHIPS = 4
FF_SHARD = D_FF // N_CHIPS
EPS = 1e-6
ROPE_THETA = 10000.0
ATTN_SCALE = MLA_QK ** -0.5
NEG_BIG = -1e30

ADAM_LR = 0.001
ADAM_B1 = 0.9
ADAM_B2 = 0.999
ADAM_EPS = 1e-08
ADAM_WD = 0.01
ADAM_STEP = 10

VMEM_LIMIT = 56 * 1024 * 1024

ROWS_WIN, ROWS_WUQ, ROWS_WOUT, ROWS_FF = 648, 72, 256, 704
PACK_ROWS = 3136
PACK_HALF = PACK_ROWS // 2
PACK_BLOCK = PACK_HALF // 2
SMALL_ROWS = 144
SMALL_SIZES = (1024, 384, 128, 131072, 512, 1024, 512, 1024, 1024, 1024)

MESH = pl.DeviceIdType.MESH
ANY = pl.BlockSpec(memory_space=pl.ANY)


def _dot(a, b, dims, exact):
    if exact:
        return lax.dot_general(a.astype(F32), b.astype(F32), (dims, ((), ())), precision=lax.Precision.HIGHEST,
                               preferred_element_type=F32)
    return lax.dot_general(a.astype(MXU_DTYPE), b.astype(MXU_DTYPE), (dims, ((), ())), preferred_element_type=F32)


def _mm(a, b, exact=False):
    return _dot(a, b, ((1,), (0,)), exact)


def _mm_nt(a, b, exact=False):
    return _dot(a, b, ((1,), (1,)), exact)


def _mm_tn(a, b, exact=False):
    return _dot(a, b, ((0,), (0,)), exact)


def _rms_fwd(x, w):
    r = lax.rsqrt(jnp.mean(x * x, axis=-1, keepdims=True) + EPS)
    xn = x * r
    return xn * w, xn, r


def _rms_bwd(dy, xn, r, w):
    dxn = dy * w
    dx = r * (dxn - xn * jnp.mean(dxn * xn, axis=-1, keepdims=True))
    dw = jnp.sum(dy * xn, axis=0, keepdims=True)
    return dx, dw


def _group_sums(v, gs):
    t, n = v.shape
    lane = lax.broadcasted_iota(jnp.int32, (t, 128), 1)
    out = []
    for p in range(n // 128):
        vb = v[:, 128 * p:128 * (p + 1)]
        if gs == 128:
            out.append(jnp.sum(vb, axis=-1, keepdims=True))
        else:
            out.append(jnp.sum(jnp.where(lane < 64, vb, 0.0), axis=-1, keepdims=True))
            out.append(jnp.sum(jnp.where(lane >= 64, vb, 0.0), axis=-1, keepdims=True))
    return out


def _group_bcast(sums, gs, t):
    lane = lax.broadcasted_iota(jnp.int32, (t, 128), 1)
    if gs == 128:
        return jnp.concatenate([jnp.broadcast_to(s, (t, 128)) for s in sums], axis=-1)
    return jnp.concatenate([jnp.where(lane < 64, sums[2 * p], sums[2 * p + 1]) for p in range(len(sums) // 2)],
                           axis=-1)


def _grms_fwd(x, w, gs):
    t = x.shape[0]
    r = lax.rsqrt(_group_bcast(_group_sums(x * x, gs), gs, t) * (1.0 / gs) + EPS)
    xn = x * r
    return xn * w, xn, r


def _grms_bwd(dy, xn, r, w, gs):
    t = dy.shape[0]
    dxn = dy * w
    dx = r * (dxn - xn * (_group_bcast(_group_sums(dxn * xn, gs), gs, t) * (1.0 / gs)))
    dw = jnp.sum(dy * xn, axis=0, keepdims=True)
    return dx, dw


def _rope_tables(c_tab, s_tab):
    lane = lax.broadcasted_iota(jnp.int32, c_tab.shape, 1)
    first = (lane >= MLA_NOPE) & (lane < MLA_NOPE + MLA_ROPE // 2)
    second = (lane >= MLA_NOPE + MLA_ROPE // 2) & (lane < MLA_QK)
    return c_tab, jnp.where(first, -s_tab, 0.0), jnp.where(second, s_tab, 0.0)


def _rope(v, c, sa, sb):
    return v * c + pltpu.roll(v, HEAD_PAD - MLA_ROPE // 2, 1) * sa + pltpu.roll(v, MLA_ROPE // 2, 1) * sb


def _rope_bwd(d, c, sa, sb):
    return d * c - pltpu.roll(d, HEAD_PAD - MLA_ROPE // 2, 1) * sa - pltpu.roll(d, MLA_ROPE // 2, 1) * sb


def _params(sem, vmem=VMEM_LIMIT):
    return pltpu.CompilerParams(dimension_semantics=sem, vmem_limit_bytes=vmem)


def _in_fwd(x, pos, invf, w_pre, win, qnw, wq, kvnw, wk, wv, tt=256):
    T = x.shape[0]

    def body(x_ref, pos_ref, invf_ref, wpre_ref, win_ref, qnw_ref, wq_ref, kvnw_ref, wk_ref, wv_ref,
             cq_ref, ckv_ref, xph_ref, q_ref, k_ref, v_ref, rc_ref, rs_ref):
        u, _, _ = _rms_fwd(x_ref[...], wpre_ref[...])
        xp = _mm(u, win_ref[...])
        cq = xp[:, :Q_RANK]
        ckv = xp[:, Q_RANK:Q_RANK + KV_RANK]
        kr = xp[:, Q_RANK + KV_RANK:Q_RANK + KV_RANK + HEAD_PAD]
        cq_ref[...] = cq
        ckv_ref[...] = ckv
        xph_ref[...] = xp[:, Q_RANK + KV_RANK + HEAD_PAD:]
        ang = pos_ref[...].astype(F32) * invf_ref[...]
        c_tab = jnp.cos(ang)
        s_tab = jnp.sin(ang)
        rc_ref[...] = c_tab
        rs_ref[...] = s_tab
        c, sa, sb = _rope_tables(c_tab, s_tab)
        qn, _, _ = _rms_fwd(cq, qnw_ref[...])
        q = _mm(qn, wq_ref[...])
        kvn, _, _ = _rms_fwd(ckv, kvnw_ref[...])
        kn = _mm(kvn, wk_ref[...])
        v_ref[...] = _mm(kvn, wv_ref[...]).astype(v_ref.dtype)
        krr = _rope(kr, c, sa, sb)
        for h in range(MLA_HEADS):
            sl = slice(HEAD_PAD * h, HEAD_PAD * (h + 1))
            q_ref[:, sl] = _rope(q[:, sl], c, sa, sb).astype(q_ref.dtype)
            k_ref[:, sl] = (kn[:, sl] + krr).astype(k_ref.dtype)

    row = lambda w: pl.BlockSpec((tt, w), lambda i: (i, 0))
    full = lambda a: pl.BlockSpec(a.shape, lambda i: (0,) * a.ndim)
    qk_w = MLA_HEADS * HEAD_PAD
    return pl.pallas_call(
        body, name="in_fwd", grid=(T // tt,),
        in_specs=[row(D_MODEL), row(1), full(invf), full(w_pre), full(win), full(qnw), full(wq), full(kvnw),
                  full(wk), full(wv)],
        out_specs=[row(Q_RANK), row(KV_RANK), row(4 * HGRN_WIDTH), row(qk_w), row(qk_w), row(MLA_WIDTH),
                   row(HEAD_PAD), row(HEAD_PAD)],
        out_shape=[jax.ShapeDtypeStruct((T, Q_RANK), F32), jax.ShapeDtypeStruct((T, KV_RANK), F32),
                   jax.ShapeDtypeStruct((T, 4 * HGRN_WIDTH), F32), jax.ShapeDtypeStruct((T, qk_w), MXU_DTYPE),
                   jax.ShapeDtypeStruct((T, qk_w), MXU_DTYPE), jax.ShapeDtypeStruct((T, MLA_WIDTH), MXU_DTYPE),
                   jax.ShapeDtypeStruct((T, HEAD_PAD), F32), jax.ShapeDtypeStruct((T, HEAD_PAD), F32)],
        compiler_params=_params(("arbitrary",)),
    )(x, pos, invf, w_pre, win, qnw, wq, kvnw, wk, wv)


def _attn_fwd(qb, kb, vb, tq=256):
    T = qb.shape[0]
    nq = T // tq

    def body(q_ref, k_ref, v_ref, o_ref, lse_ref):
        qi = pl.program_id(1)
        lane = lax.broadcasted_iota(jnp.int32, (tq, 128), 1)
        rr = lax.broadcasted_iota(jnp.int32, (tq, tq), 0)
        cc = lax.broadcasted_iota(jnp.int32, (tq, tq), 1)
        outs = []
        for a in range(2):
            q = q_ref[:, HEAD_PAD * a:HEAD_PAD * (a + 1)]

            def step(j, carry, masked):
                m, l, acc = carry
                start = pl.multiple_of(j * tq, tq)
                kj = k_ref[pl.ds(start, tq), HEAD_PAD * a:HEAD_PAD * (a + 1)]
                vj = v_ref[pl.ds(start, tq), :]
                s = _mm_nt(q, kj) * ATTN_SCALE
                if masked:
                    s = jnp.where(cc <= rr, s, NEG_BIG)
                m_new = jnp.maximum(m, jnp.max(s, axis=-1, keepdims=True))
                alpha = jnp.exp(m - m_new)
                p = jnp.exp(s - m_new)
                l = l * alpha + jnp.sum(p, axis=-1, keepdims=True)
                acc = acc * alpha + _mm(p, vj)
                return m_new, l, acc

            init = (jnp.full((tq, 1), NEG_BIG, F32), jnp.zeros((tq, 1), F32), jnp.zeros((tq, 128), F32))
            carry = lax.fori_loop(0, qi, functools.partial(step, masked=False), init)
            m, l, acc = step(qi, carry, True)
            outs.append(acc / l)
            lse_ref[a] = m + jnp.log(l)
        o_ref[...] = jnp.where(lane < MLA_V, outs[0], outs[1])

    return pl.pallas_call(
        body, name="attn_fwd", grid=(MLA_HEADS // 2, nq),
        in_specs=[pl.BlockSpec((tq, 2 * HEAD_PAD), lambda p, i: (i, p)),
                  pl.BlockSpec((T, 2 * HEAD_PAD), lambda p, i: (0, p)),
                  pl.BlockSpec((T, 2 * MLA_V), lambda p, i: (0, p))],
        out_specs=[pl.BlockSpec((tq, 2 * MLA_V), lambda p, i: (i, p)),
                   pl.BlockSpec((2, tq, 1), lambda p, i: (p, i, 0))],
        out_shape=[jax.ShapeDtypeStruct((T, MLA_WIDTH), F32), jax.ShapeDtypeStruct((MLA_HEADS, T, 1), F32)],
        compiler_params=_params(("arbitrary", "arbitrary")),
    )(qb, kb, vb)


def _attn_bwd(qb, kb, vb, do, lse, dvec, tq=256):
    T = qb.shape[0]
    nq = T // tq

    def body(q_ref, k_ref, v_ref, do_ref, lse_ref, d_ref, dq_ref, dk_ref, dv_ref):
        j = pl.program_id(1)

        @pl.when(j == 0)
        def _():
            dq_ref[...] = jnp.zeros_like(dq_ref)

        lane = lax.broadcasted_iota(jnp.int32, (tq, 128), 1)
        rr = lax.broadcasted_iota(jnp.int32, (tq, tq), 0)
        cc = lax.broadcasted_iota(jnp.int32, (tq, tq), 1)
        vpair = v_ref[...]
        dvs = []
        for a in range(2):
            hs = slice(HEAD_PAD * a, HEAD_PAD * (a + 1))
            kj = k_ref[:, hs]
            in_head = (lane >= MLA_V * a) & (lane < MLA_V * (a + 1))
            va = jnp.where(in_head, vpair, jnp.zeros_like(vpair))

            def step(i, carry, masked):
                dk_acc, dv_acc = carry
                start = pl.multiple_of(i * tq, tq)
                qi = q_ref[pl.ds(start, tq), hs]
                doi = do_ref[pl.ds(start, tq), :]
                s = _mm_nt(qi, kj) * ATTN_SCALE
                p = jnp.exp(s - lse_ref[a, pl.ds(start, tq), :])
                if masked:
                    p = jnp.where(cc <= rr, p, 0.0)
                dv_acc = dv_acc + _mm_tn(p, doi)
                dp = _mm_nt(doi, va)
                ds = p * (dp - d_ref[a, pl.ds(start, tq), :]) * ATTN_SCALE
                dk_acc = dk_acc + _mm_tn(ds, qi)
                dq_ref[pl.ds(start, tq), hs] += _mm(ds, kj)
                return dk_acc, dv_acc

            init = (jnp.zeros((tq, HEAD_PAD), F32), jnp.zeros((tq, 128), F32))
            carry = step(j, init, True)
            dk_acc, dv_acc = lax.fori_loop(j + 1, nq, functools.partial(step, masked=False), carry)
            dk_ref[:, hs] = dk_acc
            dvs.append(dv_acc)
        dv_ref[...] = jnp.where(lane < MLA_V, dvs[0], dvs[1])

    return pl.pallas_call(
        body, name="attn_bwd", grid=(MLA_HEADS // 2, nq),
        in_specs=[pl.BlockSpec((T, 2 * HEAD_PAD), lambda p, j: (0, p)),
                  pl.BlockSpec((tq, 2 * HEAD_PAD), lambda p, j: (j, p)),
                  pl.BlockSpec((tq, 2 * MLA_V), lambda p, j: (j, p)),
                  pl.BlockSpec((T, 2 * MLA_V), lambda p, j: (0, p)),
                  pl.BlockSpec((2, T, 1), lambda p, j: (p, 0, 0)),
                  pl.BlockSpec((2, T, 1), lambda p, j: (p, 0, 0))],
        out_specs=[pl.BlockSpec((T, 2 * HEAD_PAD), lambda p, j: (0, p)),
                   pl.BlockSpec((tq, 2 * HEAD_PAD), lambda p, j: (j, p)),
                   pl.BlockSpec((tq, 2 * MLA_V), lambda p, j: (j, p))],
        out_shape=[jax.ShapeDtypeStruct((T, MLA_HEADS * HEAD_PAD), F32),
                   jax.ShapeDtypeStruct((T, MLA_HEADS * HEAD_PAD), F32),
                   jax.ShapeDtypeStruct((T, MLA_WIDTH), F32)],
        compiler_params=_params(("arbitrary", "arbitrary")),
    )(qb, kb, vb, do, lse, dvec)


def _cumsum_rows(x):
    n = x.shape[0]
    row = lax.broadcasted_iota(jnp.int32, x.shape, 0)
    s = 1
    while s < n:
        x = x + jnp.where(row >= s, pltpu.roll(x, s, 0), 0.0)
        s *= 2
    return x


def _rev_cumsum_rows(x):
    n = x.shape[0]
    row = lax.broadcasted_iota(jnp.int32, x.shape, 0)
    s = 1
    while s < n:
        x = x + jnp.where(row < n - s, pltpu.roll(x, n - s, 0), 0.0)
        s *= 2
    return x


def _lb_from_logits(l):
    l0, l1 = l[0:1, :], l[1:2, :]
    m = jnp.maximum(l0, l1)
    e0, e1 = jnp.exp(l0 - m), jnp.exp(l1 - m)
    return e0 / (e0 + e1)


def _hgrn_gates(hq, hf, lb):
    sig_f = jax.nn.sigmoid(hf)
    f = lb + (1.0 - lb) * sig_f
    sig_q = jax.nn.sigmoid(hq)
    return sig_f, f, jnp.log(f), 1.0 - f, sig_q, hq * sig_q


def _hgrn_intra(q, kk, b, exact=False):
    row = lax.broadcasted_iota(jnp.int32, b.shape, 0)
    qs, ks, eqs, eks, a_rows = [], [], [], [], []
    for i in range(CHUNK // SUB):
        ref = b[SUB * i:SUB * i + 1, :]
        eq = jnp.exp(b[SUB * i:SUB * (i + 1), :] - ref)
        ek = jnp.exp(jnp.where(row < SUB * (i + 1), ref - b, NEG_BIG))
        qi = q[SUB * i:SUB * (i + 1), :] * eq
        ki = kk * ek
        a_rows.append(_mm_nt(qi, ki, exact))
        qs.append(qi), ks.append(ki), eqs.append(eq), eks.append(ek)
    tt = lax.broadcasted_iota(jnp.int32, (CHUNK, CHUNK), 0)
    ss = lax.broadcasted_iota(jnp.int32, (CHUNK, CHUNK), 1)
    causal = ss <= tt
    a = jnp.where(causal, jnp.concatenate(a_rows, axis=0), 0.0)
    return a, causal, qs, ks, eqs, eks


def _hgrn_fwd(xph, lbl, tg=512):
    T = xph.shape[0]
    ng, ncg = T // tg, tg // CHUNK

    def body(lbl_ref, hq_ref, hf_ref, hi_ref, o_ref, st_ref, s_scr):
        @pl.when(pl.program_id(1) == 0)
        def _():
            s_scr[...] = jnp.zeros_like(s_scr)

        lb = _lb_from_logits(lbl_ref[...])

        def chunk(c, _):
            rows = pl.ds(pl.multiple_of(c * CHUNK, CHUNK), CHUNK)
            _, _, lf, kk, _, q = _hgrn_gates(hq_ref[rows, :], hf_ref[rows, :], lb)
            v = hi_ref[rows, :]
            st = s_scr[...]
            st_ref[0, c] = st
            b = _cumsum_rows(lf)
            a = _hgrn_intra(q, kk, b)[0]
            o_ref[rows, :] = _mm_nt(q * jnp.exp(b), st) + _mm(a, v)
            b_last = b[CHUNK - 1:CHUNK, :]
            s_scr[...] = st * jnp.exp(b_last) + _mm_tn(v, kk * jnp.exp(b_last - b))
            return 0

        lax.fori_loop(0, ncg, chunk, 0)

    col = lambda k: pl.BlockSpec((tg, HGRN_DIM), lambda h, g: (g, k * HGRN_HEADS + h))
    return pl.pallas_call(
        body, name="hgrn_fwd", grid=(HGRN_HEADS, ng),
        in_specs=[pl.BlockSpec((2, HGRN_DIM), lambda h, g: (0, h)), col(0), col(1), col(2)],
        out_specs=[pl.BlockSpec((tg, HGRN_DIM), lambda h, g: (g, h)),
                   pl.BlockSpec((1, ncg, HGRN_DIM, HGRN_DIM), lambda h, g: (h, g, 0, 0))],
        out_shape=[jax.ShapeDtypeStruct((T, HGRN_WIDTH), F32),
                   jax.ShapeDtypeStruct((HGRN_HEADS, T // CHUNK, HGRN_DIM, HGRN_DIM), F32)],
        scratch_shapes=[pltpu.VMEM((HGRN_DIM, HGRN_DIM), F32)],
        compiler_params=_params(("arbitrary", "arbitrary")),
    )(lbl, xph, xph, xph)


def _hgrn_bwd(xph, lbl, states, d_o, tg=512):
    T = xph.shape[0]
    ng, ncg = T // tg, tg // CHUNK

    def body(lbl_ref, hq_ref, hf_ref, hi_ref, st_ref, do_ref, dhq_ref, dhf_ref, dhi_ref, dlg_ref, ds_scr, dlb_scr):
        g = pl.program_id(1)

        @pl.when(g == 0)
        def _():
            ds_scr[...] = jnp.zeros_like(ds_scr)
            dlb_scr[...] = jnp.zeros_like(dlb_scr)

        lb = _lb_from_logits(lbl_ref[...])

        def chunk(ci, _):
            c = ncg - 1 - ci
            rows = pl.ds(pl.multiple_of(c * CHUNK, CHUNK), CHUNK)
            hq = hq_ref[rows, :]
            sig_f, f, lf, kk, sig_q, q = _hgrn_gates(hq, hf_ref[rows, :], lb)
            v = hi_ref[rows, :]
            do = do_ref[rows, :]
            st = st_ref[0, c]
            dst = ds_scr[...]
            b = _cumsum_rows(lf)
            eb = jnp.exp(b)
            qe = q * eb
            a, causal, qs, ks, eqs, eks = _hgrn_intra(q, kk, b, exact=True)
            b_last = b[CHUNK - 1:CHUNK, :]
            ebl = jnp.exp(b_last)
            el = jnp.exp(b_last - b)
            dv = _mm_tn(a, do, True) + _mm_nt(kk * el, dst, True)
            da = jnp.where(causal, _mm_nt(do, v, True), 0.0)
            dq_rows = []
            dk = jnp.zeros_like(q)
            for i in range(CHUNK // SUB):
                dai = da[SUB * i:SUB * (i + 1), :]
                dq_rows.append(_mm(dai, ks[i], True) * eqs[i])
                dk = dk + _mm_tn(dai, qs[i], True) * eks[i]
            dq = _mm(do, st, True) * eb + jnp.concatenate(dq_rows, axis=0)
            dk_state = _mm(v, dst, True) * el
            dk = dk + dk_state
            e_last = (ebl * jnp.sum(st * dst, axis=0, keepdims=True)
                      + jnp.sum(kk * dk_state, axis=0, keepdims=True))
            dlf = _rev_cumsum_rows(q * dq - kk * dk) + e_last
            ds_scr[...] = dst * ebl + _mm_tn(do, qe, True)
            df = dlf / f - dk
            dhf_ref[rows, :] = df * (1.0 - lb) * sig_f * (1.0 - sig_f)
            dlb_scr[...] += jnp.sum(df * (1.0 - sig_f), axis=0, keepdims=True)
            dhq_ref[rows, :] = dq * sig_q * (1.0 + hq * (1.0 - sig_q))
            dhi_ref[rows, :] = dv
            return 0

        lax.fori_loop(0, ncg, chunk, 0)

        @pl.when(g == ng - 1)
        def _():
            dl0 = dlb_scr[...] * lb * (1.0 - lb)
            dlg_ref[...] = jnp.concatenate([dl0, -dl0], axis=0)

    col = lambda k: pl.BlockSpec((tg, HGRN_DIM), lambda h, g: (ng - 1 - g, k * HGRN_HEADS + h))
    ocol = pl.BlockSpec((tg, HGRN_DIM), lambda h, g: (ng - 1 - g, h))
    big = jax.ShapeDtypeStruct((T, HGRN_WIDTH), F32)
    return pl.pallas_call(
        body, name="hgrn_bwd", grid=(HGRN_HEADS, ng),
        in_specs=[pl.BlockSpec((2, HGRN_DIM), lambda h, g: (0, h)), col(0), col(1), col(2),
                  pl.BlockSpec((1, ncg, HGRN_DIM, HGRN_DIM), lambda h, g: (h, ng - 1 - g, 0, 0)), ocol],
        out_specs=[ocol, ocol, ocol, pl.BlockSpec((2, HGRN_DIM), lambda h, g: (0, h))],
        out_shape=[big, big, big, jax.ShapeDtypeStruct((2, HGRN_WIDTH), F32)],
        scratch_shapes=[pltpu.VMEM((HGRN_DIM, HGRN_DIM), F32), pltpu.VMEM((1, HGRN_DIM), F32)],
        compiler_params=_params(("arbitrary", "arbitrary")),
    )(lbl, xph, xph, xph, states, d_o)


def _ffn_fwd(x, o_raw, oh_raw, xph, tgt, wout, w_mla, w_hg, w_post, w_fpre, w_fpost, wg, wu, wd, tt=256):
    T = x.shape[0]
    nj = N_CHIPS

    def body(x_ref, o_ref, oh_ref, hg_ref, tgt_ref, wout_ref, wmla_ref, whg_ref, wpost_ref, wfpre_ref, wfpost_ref,
             wg_ref, wu_ref, wd_ref,
             h1_ref, y1_ref, z_ref, mix_ref, g_ref, up_ref, dy2_ref, dh2_ref, loss_ref, dwf_ref,
             z_scr, y2_scr):
        i, j = pl.program_id(0), pl.program_id(1)

        @pl.when((i == 0) & (j == 0))
        def _():
            loss_ref[...] = jnp.zeros_like(loss_ref)
            dwf_ref[...] = jnp.zeros_like(dwf_ref)

        @pl.when(j == 0)
        def _():
            om, _, _ = _grms_fwd(o_ref[...], wmla_ref[...], MLA_V)
            hg = hg_ref[...]
            ohn, _, _ = _grms_fwd(oh_ref[...], whg_ref[...], HGRN_DIM)
            mix = jnp.concatenate([om, ohn * (hg * jax.nn.sigmoid(hg))], axis=-1)
            mix_ref[...] = mix.astype(mix_ref.dtype)
            y1 = _mm(mix, wout_ref[...])
            y1_ref[...] = y1
            h1 = x_ref[...] + _rms_fwd(y1, wpost_ref[...])[0]
            h1_ref[...] = h1
            z = _rms_fwd(h1, wfpre_ref[...])[0].astype(z_scr.dtype)
            z_scr[...] = z
            z_ref[...] = z
            y2_scr[...] = jnp.zeros_like(y2_scr)

        z = z_scr[...]
        g = _mm(z, wg_ref[0])
        up = _mm(z, wu_ref[0])
        g_ref[0] = g
        up_ref[0] = up
        y2_scr[...] += _mm(g * jax.nn.sigmoid(g) * up, wd_ref[0])

        @pl.when(j == nj - 1)
        def _():
            w = wfpost_ref[...]
            y2s, y2n, r2 = _rms_fwd(y2_scr[...], w)
            e = h1_ref[...] + y2s - tgt_ref[...]
            loss_ref[...] += jnp.sum(e * e, axis=0, keepdims=True)
            dh2 = e * (1.0 / D_MODEL)
            dh2_ref[...] = dh2
            dy2, dwf = _rms_bwd(dh2, y2n, r2, w)
            dy2_ref[...] = dy2.astype(dy2_ref.dtype)
            dwf_ref[...] += dwf

    row = lambda w: pl.BlockSpec((tt, w), lambda i, j: (i, 0))
    full = lambda a: pl.BlockSpec(a.shape, lambda i, j: (0,) * a.ndim)
    vec = pl.BlockSpec((1, D_MODEL), lambda i, j: (0, 0))
    return pl.pallas_call(
        body, name="ffn_fwd", grid=(T // tt, nj),
        in_specs=[row(D_MODEL), row(MLA_WIDTH), row(HGRN_WIDTH),
                  pl.BlockSpec((tt, HGRN_WIDTH), lambda i, j: (i, 3)), row(D_MODEL), full(wout), full(w_mla),
                  full(w_hg), vec, vec, vec,
                  pl.BlockSpec((1, D_MODEL, FF_SHARD), lambda i, j: (j, 0, 0)),
                  pl.BlockSpec((1, D_MODEL, FF_SHARD), lambda i, j: (j, 0, 0)),
                  pl.BlockSpec((1, FF_SHARD, D_MODEL), lambda i, j: (j, 0, 0))],
        out_specs=[row(D_MODEL), row(D_MODEL), row(D_MODEL), row(D_MODEL),
                   pl.BlockSpec((1, tt, FF_SHARD), lambda i, j: (j, i, 0)),
                   pl.BlockSpec((1, tt, FF_SHARD), lambda i, j: (j, i, 0)),
                   row(D_MODEL), row(D_MODEL), vec, vec],
        out_shape=[jax.ShapeDtypeStruct((T, D_MODEL), F32), jax.ShapeDtypeStruct((T, D_MODEL), F32),
                   jax.ShapeDtypeStruct((T, D_MODEL), MXU_DTYPE), jax.ShapeDtypeStruct((T, D_MODEL), MXU_DTYPE),
                   jax.ShapeDtypeStruct((nj, T, FF_SHARD), F32), jax.ShapeDtypeStruct((nj, T, FF_SHARD), F32),
                   jax.ShapeDtypeStruct((T, D_MODEL), MXU_DTYPE), jax.ShapeDtypeStruct((T, D_MODEL), F32),
                   jax.ShapeDtypeStruct((1, D_MODEL), F32), jax.ShapeDtypeStruct((1, D_MODEL), F32)],
        scratch_shapes=[pltpu.VMEM((tt, D_MODEL), MXU_DTYPE), pltpu.VMEM((tt, D_MODEL), F32)],
        compiler_params=_params(("arbitrary", "arbitrary")),
    )(x, o_raw, oh_raw, xph, tgt, wout, w_mla, w_hg, w_post, w_fpre, w_fpost, wg, wu, wd)


def _ffn_bwd(zb, g, up, dy2b, wg, wu, wd, tt=512):
    T = zb.shape[0]
    nj = N_CHIPS

    def body(z_ref, g_ref, up_ref, dy2_ref, wg_ref, wu_ref, wd_ref, dwg_ref, dwu_ref, dwd_ref, dz_ref):
        @pl.when(pl.program_id(1) == 0)
        def _():
            dwg_ref[...] = jnp.zeros_like(dwg_ref)
            dwu_ref[...] = jnp.zeros_like(dwu_ref)
            dwd_ref[...] = jnp.zeros_like(dwd_ref)

        z, g_, up_, dy2 = z_ref[...], g_ref[0], up_ref[0], dy2_ref[...]
        sg = jax.nn.sigmoid(g_)
        act = g_ * sg
        dff = _mm_nt(dy2, wd_ref[0])
        dwd_ref[0] += _mm_tn(act * up_, dy2)
        dg = dff * up_ * sg * (1.0 + g_ * (1.0 - sg))
        dup = dff * act
        dwg_ref[0] += _mm_tn(z, dg)
        dwu_ref[0] += _mm_tn(z, dup)
        dz_ref[0] = _mm_nt(dg, wg_ref[0]) + _mm_nt(dup, wu_ref[0])

    row = pl.BlockSpec((tt, D_MODEL), lambda j, i: (i, 0))
    act = pl.BlockSpec((1, tt, FF_SHARD), lambda j, i: (j, i, 0))
    w_in = pl.BlockSpec((1, D_MODEL, FF_SHARD), lambda j, i: (j, 0, 0))
    w_dn = pl.BlockSpec((1, FF_SHARD, D_MODEL), lambda j, i: (j, 0, 0))
    return pl.pallas_call(
        body, name="ffn_bwd", grid=(nj, T // tt),
        in_specs=[row, act, act, row, w_in, w_in, w_dn],
        out_specs=[w_in, w_in, w_dn, pl.BlockSpec((1, tt, D_MODEL), lambda j, i: (j, i, 0))],
        out_shape=[jax.ShapeDtypeStruct((nj, D_MODEL, FF_SHARD), F32), jax.ShapeDtypeStruct((nj, D_MODEL, FF_SHARD), F32),
                   jax.ShapeDtypeStruct((nj, FF_SHARD, D_MODEL), F32), jax.ShapeDtypeStruct((nj, T, D_MODEL), F32)],
        compiler_params=_params(("arbitrary", "arbitrary")),
    )(zb, g, up, dy2b, wg, wu, wd)


def _mid_bwd(dzp, dh2, h1, y1, mixb, o_raw, oh_raw, xph, wout, w_fpre, w_post, w_mla, w_hg, tt=256):
    T = dh2.shape[0]

    def body(dzp_ref, dh2_ref, h1_ref, y1_ref, mix_ref, o_ref, oh_ref, hg_ref, wout_ref, wfpre_ref, wpost_ref,
             wmla_ref, whg_ref,
             dh1_ref, dwout_ref, do_ref, doh_ref, dhg_ref, dvec_ref, dwfpre_ref, dwpost_ref, dwmla_ref, dwhg_ref):
        @pl.when(pl.program_id(0) == 0)
        def _():
            for r in (dwout_ref, dwfpre_ref, dwpost_ref, dwmla_ref, dwhg_ref):
                r[...] = jnp.zeros_like(r)

        dz = dzp_ref[0] + dzp_ref[1] + dzp_ref[2] + dzp_ref[3]
        wfpre = wfpre_ref[...]
        _, h1n, r = _rms_fwd(h1_ref[...], wfpre)
        dh1_z, dwfpre = _rms_bwd(dz, h1n, r, wfpre)
        dwfpre_ref[...] += dwfpre
        dh1 = dh2_ref[...] + dh1_z
        dh1_ref[...] = dh1
        wpost = wpost_ref[...]
        _, y1n, r1 = _rms_fwd(y1_ref[...], wpost)
        dy1, dwpost = _rms_bwd(dh1, y1n, r1, wpost)
        dwpost_ref[...] += dwpost
        dmix = _mm_nt(dy1, wout_ref[...])
        dwout_ref[...] += _mm_tn(mix_ref[...], dy1)
        wmla = wmla_ref[...]
        o = o_ref[...]
        _, on, ro = _grms_fwd(o, wmla, MLA_V)
        d_o, dwmla = _grms_bwd(dmix[:, :MLA_WIDTH], on, ro, wmla, MLA_V)
        dwmla_ref[...] += dwmla
        do_ref[...] = d_o
        for h, s in enumerate(_group_sums(d_o * o, MLA_V)):
            dvec_ref[h] = s
        whg = whg_ref[...]
        hg = hg_ref[...]
        sg = jax.nn.sigmoid(hg)
        _, ohn, rh = _grms_fwd(oh_ref[...], whg, HGRN_DIM)
        dmh = dmix[:, MLA_WIDTH:]
        dhg_ref[...] = dmh * ohn * whg * sg * (1.0 + hg * (1.0 - sg))
        d_oh, dwhg = _grms_bwd(dmh * (hg * sg), ohn, rh, whg, HGRN_DIM)
        dwhg_ref[...] += dwhg
        doh_ref[...] = d_oh

    row = lambda w: pl.BlockSpec((tt, w), lambda i: (i, 0))
    full = lambda a: pl.BlockSpec(a.shape, lambda i: (0,) * a.ndim)
    vec = lambda w: pl.BlockSpec((1, w), lambda i: (0, 0))
    sds = jax.ShapeDtypeStruct
    return pl.pallas_call(
        body, name="mid_bwd", grid=(T // tt,),
        in_specs=[pl.BlockSpec((N_CHIPS, tt, D_MODEL), lambda i: (0, i, 0)), row(D_MODEL), row(D_MODEL), row(D_MODEL),
                  row(D_MODEL), row(MLA_WIDTH), row(HGRN_WIDTH), pl.BlockSpec((tt, HGRN_WIDTH), lambda i: (i, 3)),
                  full(wout), vec(D_MODEL), vec(D_MODEL), vec(MLA_WIDTH), vec(HGRN_WIDTH)],
        out_specs=[row(D_MODEL), full(wout), row(MLA_WIDTH), row(HGRN_WIDTH), row(HGRN_WIDTH),
                   pl.BlockSpec((MLA_HEADS, tt, 1), lambda i: (0, i, 0)),
                   vec(D_MODEL), vec(D_MODEL), vec(MLA_WIDTH), vec(HGRN_WIDTH)],
        out_shape=[sds((T, D_MODEL), F32), sds(wout.shape, F32), sds((T, MLA_WIDTH), F32), sds((T, HGRN_WIDTH), F32),
                   sds((T, HGRN_WIDTH), F32), sds((MLA_HEADS, T, 1), F32),
                   sds((1, D_MODEL), F32), sds((1, D_MODEL), F32), sds((1, MLA_WIDTH), F32), sds((1, HGRN_WIDTH), F32)],
        compiler_params=_params(("arbitrary",)),
    )(dzp, dh2, h1, y1, mixb, o_raw, oh_raw, xph, wout, w_fpre, w_post, w_mla, w_hg)


def _in_bwd(x, dh1, cq, ckv, dq, dk, dv, dhq, dhf, dhi, dhg, rc, rs, w_pre, win, qnw, wq, kvnw, wk, wv, tt=256):
    T = x.shape[0]

    def body(x_ref, dh1_ref, cq_ref, ckv_ref, dq_ref, dk_ref, dv_ref, dhq_ref, dhf_ref, dhi_ref, dhg_ref, rc_ref, rs_ref,
             wpre_ref, win_ref, qnw_ref, wq_ref, kvnw_ref, wk_ref, wv_ref,
             dx_ref, dwin_ref, dwq_ref, dwk_ref, dwv_ref, dwpre_ref, dqnw_ref, dkvnw_ref):
        @pl.when(pl.program_id(0) == 0)
        def _():
            for r in (dwin_ref, dwq_ref, dwk_ref, dwv_ref, dwpre_ref, dqnw_ref, dkvnw_ref):
                r[...] = jnp.zeros_like(r)

        c, sa, sb = _rope_tables(rc_ref[...], rs_ref[...])
        lane = lax.broadcasted_iota(jnp.int32, (tt, HEAD_PAD), 1)
        dk_all = dk_ref[...]
        dq_lin = []
        dkr = jnp.zeros((tt, HEAD_PAD), F32)
        for h in range(MLA_HEADS):
            sl = slice(HEAD_PAD * h, HEAD_PAD * (h + 1))
            dq_lin.append(_rope_bwd(dq_ref[:, sl], c, sa, sb))
            dkr = dkr + dk_all[:, sl]
        dq_lin = jnp.concatenate(dq_lin, axis=-1)
        dkr = jnp.where((lane >= MLA_NOPE) & (lane < MLA_QK), _rope_bwd(dkr, c, sa, sb), 0.0)
        qnw = qnw_ref[...]
        qn, cqn, rq = _rms_fwd(cq_ref[...], qnw)
        dwq_ref[...] += _mm_tn(qn, dq_lin)
        dcq, dqnw = _rms_bwd(_mm_nt(dq_lin, wq_ref[...]), cqn, rq, qnw)
        dqnw_ref[...] += dqnw
        kvnw = kvnw_ref[...]
        kvn, ckvn, rkv = _rms_fwd(ckv_ref[...], kvnw)
        dv_ = dv_ref[...]
        dwk_ref[...] += _mm_tn(kvn, dk_all)
        dwv_ref[...] += _mm_tn(kvn, dv_)
        dckv, dkvnw = _rms_bwd(_mm_nt(dk_all, wk_ref[...]) + _mm_nt(dv_, wv_ref[...]), ckvn, rkv, kvnw)
        dkvnw_ref[...] += dkvnw
        dxp = jnp.concatenate([dcq, dckv, dkr, dhq_ref[...], dhf_ref[...], dhi_ref[...], dhg_ref[...]], axis=-1)
        wpre = wpre_ref[...]
        u, xn, rx = _rms_fwd(x_ref[...], wpre)
        dwin_ref[...] += _mm_tn(u, dxp)
        dx_u, dwpre = _rms_bwd(_mm_nt(dxp, win_ref[...]), xn, rx, wpre)
        dwpre_ref[...] += dwpre
        dx_ref[...] = dh1_ref[...] + dx_u

    row = lambda w: pl.BlockSpec((tt, w), lambda i: (i, 0))
    full = lambda a: pl.BlockSpec(a.shape, lambda i: (0,) * a.ndim)
    sds = jax.ShapeDtypeStruct
    qk_w = MLA_HEADS * HEAD_PAD
    return pl.pallas_call(
        body, name="in_bwd", grid=(T // tt,),
        in_specs=[row(D_MODEL), row(D_MODEL), row(Q_RANK), row(KV_RANK), row(qk_w), row(qk_w), row(MLA_WIDTH),
                  row(HGRN_WIDTH), row(HGRN_WIDTH), row(HGRN_WIDTH), row(HGRN_WIDTH), row(HEAD_PAD), row(HEAD_PAD),
                  full(w_pre), full(win), full(qnw), full(wq), full(kvnw), full(wk), full(wv)],
        out_specs=[row(D_MODEL), full(win), full(wq), full(wk), full(wv), full(w_pre), full(qnw), full(kvnw)],
        out_shape=[sds((T, D_MODEL), F32), sds(win.shape, F32), sds(wq.shape, F32), sds(wk.shape, F32),
                   sds(wv.shape, F32), sds(w_pre.shape, F32), sds(qnw.shape, F32), sds(kvnw.shape, F32)],
        compiler_params=_params(("arbitrary",)),
    )(x, dh1, cq, ckv, dq, dk, dv, dhq, dhf, dhi, dhg, rc, rs, w_pre, win, qnw, wq, kvnw, wk, wv)


def _arrange_weights(win_full, wuq_full, wukv):
    dt = win_full.dtype
    z = lambda n: jnp.zeros((D_MODEL, n), dt)
    s2 = Q_RANK + KV_RANK
    half = MLA_ROPE // 2
    win_arr = jnp.concatenate([win_full[:, :s2], z(MLA_NOPE), win_full[:, s2:s2 + MLA_ROPE],
                               z(HEAD_PAD - MLA_QK), win_full[:, s2 + MLA_ROPE:]], axis=1)
    del half
    wq_arr = jnp.pad(wuq_full, ((0, 0), (0, 0), (0, HEAD_PAD - MLA_QK))).reshape(Q_RANK, MLA_HEADS * HEAD_PAD)
    wk_arr = jnp.pad(wukv[:, :, :MLA_NOPE], ((0, 0), (0, 0), (0, HEAD_PAD - MLA_NOPE))).reshape(
        KV_RANK, MLA_HEADS * HEAD_PAD)
    wv_arr = wukv[:, :, MLA_NOPE:].reshape(KV_RANK, MLA_WIDTH)
    return win_arr, wq_arr, wk_arr, wv_arr


def _unarrange_grads(dwin_arr, dwq_arr, dwk_arr, dwv_arr):
    s2 = Q_RANK + KV_RANK
    dwin = jnp.concatenate([dwin_arr[:, :s2], dwin_arr[:, s2 + MLA_NOPE:s2 + MLA_QK], dwin_arr[:, s2 + HEAD_PAD:]],
                           axis=1)
    dwuq = dwq_arr.reshape(Q_RANK, MLA_HEADS, HEAD_PAD)[:, :, :MLA_QK]
    dwukv = jnp.concatenate([dwk_arr.reshape(KV_RANK, MLA_HEADS, HEAD_PAD)[:, :, :MLA_NOPE],
                             dwv_arr.reshape(KV_RANK, MLA_HEADS, MLA_V)], axis=-1)
    return dwin, dwuq, dwukv


def _rope_inv_freq():
    inv = 1.0 / (ROPE_THETA ** (jnp.arange(0, MLA_ROPE, 2, dtype=F32) / MLA_ROPE))
    z = lambda n: jnp.zeros((n,), F32)
    return jnp.concatenate([z(MLA_NOPE), inv, inv, z(HEAD_PAD - MLA_QK)]).reshape(1, HEAD_PAD)


def _local_step(x, pos, tgt, small, win_arr, wq_arr, wk_arr, wv_arr, wout, wg, wu, wd):
    invf = _rope_inv_freq()
    cq, ckv, xph, qb, kb, vb, rc, rs = _in_fwd(x, pos, invf, small["attn_pre_norm"], win_arr, small["mla_q_norm"],
                                               wq_arr, small["mla_kv_norm"], wk_arr, wv_arr)
    o_raw, lse = _attn_fwd(qb, kb, vb)
    oh_raw, states = _hgrn_fwd(xph, small["hgrn_lb_logits"])
    h1, y1, zb, mixb, g, up, dy2b, dh2, loss_acc, d_fpost = _ffn_fwd(
        x, o_raw, oh_raw, xph, tgt, wout, small["mla_out_norm"], small["hgrn_out_norm"], small["attn_post_norm"],
        small["ffn_pre_norm"], small["ffn_post_norm"], wg, wu, wd)
    dwg, dwu, dwd, dzp = _ffn_bwd(zb, g, up, dy2b, wg, wu, wd)
    dh1, dwout, d_o, d_oh, dhg, dvec, d_fpre, d_post, d_mla, d_hg = _mid_bwd(
        dzp, dh2, h1, y1, mixb, o_raw, oh_raw, xph, wout, small["ffn_pre_norm"], small["attn_post_norm"],
        small["mla_out_norm"], small["hgrn_out_norm"])
    dq, dk, dv = _attn_bwd(qb, kb, vb, d_o, lse, dvec)
    dhq, dhf, dhi, d_lbl = _hgrn_bwd(xph, small["hgrn_lb_logits"], states, d_oh)
    dx, dwin_arr, dwq_arr, dwk_arr, dwv_arr, d_pre, d_qn, d_kvn = _in_bwd(
        x, dh1, cq, ckv, dq, dk, dv, dhq, dhf, dhi, dhg, rc, rs, small["attn_pre_norm"], win_arr,
        small["mla_q_norm"], wq_arr, small["mla_kv_norm"], wk_arr, wv_arr)
    dwin, dwuq, dwukv = _unarrange_grads(dwin_arr, dwq_arr, dwk_arr, dwv_arr)
    loss = 0.5 * jnp.sum(loss_acc) * (1.0 / D_MODEL)
    grads = dict(attn_pre_norm=d_pre, w_in=dwin, mla_q_norm=d_qn, mla_w_uq=dwuq, mla_kv_norm=d_kvn, mla_w_ukv=dwukv,
                 mla_out_norm=d_mla, hgrn_lb_logits=d_lbl, hgrn_out_norm=d_hg, w_out=dwout, attn_post_norm=d_post,
                 ffn_pre_norm=d_fpre, w_gate=dwg, w_up=dwu, w_down=dwd, ffn_post_norm=d_fpost)
    return loss, dx, grads


def _place():
    x, y, c = lax.axis_index("x"), lax.axis_index("y"), lax.axis_index("c")
    others = [(1 - x, y), (x, 1 - y), (1 - x, 1 - y)]
    return x, y, c, 2 * x + y, (x, y, 1 - c), others


def _half(ref, c, rows):
    return ref.at[pl.ds(pl.multiple_of(c * rows, 8), rows)]


def _gather4(arrs, name):
    n = len(arrs)
    halves = [a.shape[0] // 2 for a in arrs]

    def body(*refs):
        ins, outs = refs[:n], refs[n:2 * n]
        send, recv, lsem = refs[2 * n:]
        x, y, c, me, sib, others = _place()

        def rcopy(k, src, dst, to):
            return pltpu.make_async_remote_copy(src_ref=src, dst_ref=dst, send_sem=send.at[k], recv_sem=recv.at[k],
                                                device_id=to, device_id_type=MESH)

        local = [pltpu.make_async_copy(ins[a], outs[a].at[me], lsem.at[a]) for a in range(n)]
        for cp in local:
            cp.start()
        first = []
        for j, (px, py) in enumerate(others):
            for a in range(n):
                first.append(rcopy(j * n + a, _half(ins[a], c, halves[a]), _half(outs[a].at[me], c, halves[a]),
                                   (px, py, c)))
        for cp in first:
            cp.start()
        passed = []
        for j, (px, py) in enumerate(others):
            chip = 2 * px + py
            for a in range(n):
                part = _half(outs[a].at[chip], c, halves[a])
                rcopy(j * n + a, part, part, (px, py, c)).wait_recv()
                fw = rcopy(3 * n + j * n + a, part, part, sib)
                fw.start()
                passed.append(fw)
        for j, (px, py) in enumerate(others):
            chip = 2 * px + py
            for a in range(n):
                part = _half(outs[a].at[chip], 1 - c, halves[a])
                rcopy(3 * n + j * n + a, part, part, sib).wait_recv()
        for cp in first + passed:
            cp.wait_send()
        for cp in local:
            cp.wait()

    return pl.pallas_call(
        body, name=name,
        in_specs=[ANY] * n, out_specs=[ANY] * n,
        out_shape=[jax.ShapeDtypeStruct((N_CHIPS,) + a.shape, a.dtype) for a in arrs],
        scratch_shapes=[pltpu.SemaphoreType.DMA((6 * n,)), pltpu.SemaphoreType.DMA((6 * n,)),
                        pltpu.SemaphoreType.DMA((n,))],
    )(*arrs)


def _pair_exchange(gb, sm):
    def body(gb_ref, sm_ref, rsib_ref, ssib_ref, send, recv):
        x, y, c, me, sib, others = _place()
        big = pltpu.make_async_remote_copy(
            src_ref=gb_ref.at[:, pl.ds(pl.multiple_of((1 - c) * PACK_HALF, 8), PACK_HALF)], dst_ref=rsib_ref,
            send_sem=send.at[0], recv_sem=recv.at[0], device_id=sib, device_id_type=MESH)
        small = pltpu.make_async_remote_copy(src_ref=sm_ref, dst_ref=ssib_ref, send_sem=send.at[1], recv_sem=recv.at[1],
                                             device_id=sib, device_id_type=MESH)
        big.start()
        small.start()
        big.wait()
        small.wait()

    return pl.pallas_call(
        body, name="pair_exchange", in_specs=[ANY, ANY], out_specs=[ANY, ANY],
        out_shape=[jax.ShapeDtypeStruct((N_CHIPS, PACK_HALF, D_MODEL), gb.dtype), jax.ShapeDtypeStruct(sm.shape, sm.dtype)],
        scratch_shapes=[pltpu.SemaphoreType.DMA((2,)), pltpu.SemaphoreType.DMA((2,))],
    )(gb, sm)


def _pair_add(place, g32, rsib, sm, ssib):
    nb = PACK_HALF // PACK_BLOCK

    def body(place_ref, g_ref, r_ref, sm_ref, ss_ref, pb_ref, own_ref, pair_ref):
        i, k = pl.program_id(0), pl.program_id(1)
        s = g_ref[0] + r_ref[0].astype(F32)
        pb_ref[0] = s.astype(pb_ref.dtype)

        @pl.when(k == place_ref[1])
        def _():
            own_ref[...] = s

        @pl.when((i == 0) & (k == 0))
        def _():
            pair_ref[...] = sm_ref[...] + ss_ref[...]

    blk = (1, PACK_BLOCK, D_MODEL)
    sm_spec = pl.BlockSpec(sm.shape, lambda i, k, p: (0, 0))
    return pl.pallas_call(
        body, name="pair_add",
        grid_spec=pltpu.PrefetchScalarGridSpec(
            num_scalar_prefetch=1, grid=(nb, N_CHIPS),
            in_specs=[pl.BlockSpec(blk, lambda i, k, p: (k, p[0] * nb + i, 0)),
                      pl.BlockSpec(blk, lambda i, k, p: (k, i, 0)), sm_spec, sm_spec],
            out_specs=[pl.BlockSpec(blk, lambda i, k, p: (k, i, 0)),
                       pl.BlockSpec((PACK_BLOCK, D_MODEL), lambda i, k, p: (i, 0)), sm_spec]),
        out_shape=[jax.ShapeDtypeStruct((N_CHIPS, PACK_HALF, D_MODEL), rsib.dtype),
                   jax.ShapeDtypeStruct((PACK_HALF, D_MODEL), F32), jax.ShapeDtypeStruct(sm.shape, F32)],
        compiler_params=_params(("arbitrary", "arbitrary")),
    )(place, g32, rsib, sm, ssib)


def _chip_exchange(pb, pair):
    hs = SMALL_ROWS // 2

    def body(pb_ref, pair_ref, rici_ref, sm4_ref, send, recv, lsem):
        x, y, c, me, sib, others = _place()
        local = pltpu.make_async_copy(pair_ref, sm4_ref.at[me], lsem.at[0])
        local.start()
        copies = []
        for j, (px, py) in enumerate(others):
            chip = 2 * px + py
            copies.append(pltpu.make_async_remote_copy(
                src_ref=pb_ref.at[chip], dst_ref=rici_ref.at[j], send_sem=send.at[j], recv_sem=recv.at[j],
                device_id=(px, py, c), device_id_type=MESH))
            copies.append(pltpu.make_async_remote_copy(
                src_ref=_half(pair_ref, c, hs), dst_ref=_half(sm4_ref.at[me], c, hs), send_sem=send.at[3 + j],
                recv_sem=recv.at[3 + j], device_id=(px, py, c), device_id_type=MESH))
        for cp in copies:
            cp.start()
        for j, (px, py) in enumerate(others):
            chip = 2 * px + py
            pltpu.make_async_remote_copy(src_ref=pb_ref.at[chip], dst_ref=rici_ref.at[j], send_sem=send.at[j],
                                         recv_sem=recv.at[j], device_id=(px, py, c), device_id_type=MESH).wait_recv()
            part = _half(sm4_ref.at[chip], c, hs)
            pltpu.make_async_remote_copy(src_ref=part, dst_ref=part, send_sem=send.at[3 + j], recv_sem=recv.at[3 + j],
                                         device_id=(px, py, c), device_id_type=MESH).wait_recv()
        for cp in copies:
            cp.wait_send()
        local.wait()

    return pl.pallas_call(
        body, name="chip_exchange", in_specs=[ANY, ANY], out_specs=[ANY, ANY],
        out_shape=[jax.ShapeDtypeStruct((3, PACK_HALF, D_MODEL), pb.dtype),
                   jax.ShapeDtypeStruct((N_CHIPS,) + pair.shape, pair.dtype)],
        scratch_shapes=[pltpu.SemaphoreType.DMA((6,)), pltpu.SemaphoreType.DMA((6,)), pltpu.SemaphoreType.DMA((1,))],
    )(pb, pair)


def _chip_add(own, rici):
    def body(own_ref, r_ref, out_ref):
        out_ref[...] = own_ref[...] + r_ref[0].astype(F32) + r_ref[1].astype(F32) + r_ref[2].astype(F32)

    return pl.pallas_call(
        body, name="chip_add", grid=(PACK_HALF // PACK_BLOCK,),
        in_specs=[pl.BlockSpec((PACK_BLOCK, D_MODEL), lambda i: (i, 0)),
                  pl.BlockSpec((3, PACK_BLOCK, D_MODEL), lambda i: (0, i, 0))],
        out_specs=pl.BlockSpec((PACK_BLOCK, D_MODEL), lambda i: (i, 0)),
        out_shape=jax.ShapeDtypeStruct(own.shape, F32),
        compiler_params=_params(("arbitrary",)),
    )(own, rici)


def _pair_share(gfin_half, sm4):
    hs = SMALL_ROWS // 2

    def body(g_ref, sm4_ref, gfull_ref, smf_ref, send, recv, lsem):
        x, y, c, me, sib, others = _place()
        mine = _half(gfull_ref, c, PACK_HALF)
        loc = [pltpu.make_async_copy(g_ref, mine, lsem.at[0]), pltpu.make_async_copy(sm4_ref.at[me], smf_ref.at[me], lsem.at[1])]
        copies = [pltpu.make_async_remote_copy(src_ref=g_ref, dst_ref=mine, send_sem=send.at[0], recv_sem=recv.at[0],
                                               device_id=sib, device_id_type=MESH)]
        for j, (px, py) in enumerate(others):
            chip = 2 * px + py
            src = _half(sm4_ref.at[chip], c, hs)
            dst = _half(smf_ref.at[chip], c, hs)
            loc.append(pltpu.make_async_copy(src, dst, lsem.at[2 + j]))
            copies.append(pltpu.make_async_remote_copy(src_ref=src, dst_ref=dst, send_sem=send.at[1 + j],
                                                       recv_sem=recv.at[1 + j], device_id=sib, device_id_type=MESH))
        for cp in loc + copies:
            cp.start()
        theirs = _half(gfull_ref, 1 - c, PACK_HALF)
        pltpu.make_async_remote_copy(src_ref=g_ref, dst_ref=theirs, send_sem=send.at[0], recv_sem=recv.at[0],
                                     device_id=sib, device_id_type=MESH).wait_recv()
        for j, (px, py) in enumerate(others):
            part = _half(smf_ref.at[2 * px + py], 1 - c, hs)
            pltpu.make_async_remote_copy(src_ref=part, dst_ref=part, send_sem=send.at[1 + j], recv_sem=recv.at[1 + j],
                                         device_id=sib, device_id_type=MESH).wait_recv()
        for cp in copies:
            cp.wait_send()
        for cp in loc:
            cp.wait()

    return pl.pallas_call(
        body, name="pair_share", in_specs=[ANY, ANY], out_specs=[ANY, ANY],
        out_shape=[jax.ShapeDtypeStruct((PACK_ROWS, D_MODEL), F32), jax.ShapeDtypeStruct(sm4.shape, sm4.dtype)],
        scratch_shapes=[pltpu.SemaphoreType.DMA((4,)), pltpu.SemaphoreType.DMA((4,)), pltpu.SemaphoreType.DMA((5,))],
    )(gfin_half, sm4)


def _adamw_math(w, g, m, v):
    m = ADAM_B1 * m + (1.0 - ADAM_B1) * g
    v = ADAM_B2 * v + (1.0 - ADAM_B2) * (g * g)
    m_hat = m / (1.0 - ADAM_B1 ** ADAM_STEP)
    v_hat = v / (1.0 - ADAM_B2 ** ADAM_STEP)
    return -ADAM_LR * (m_hat / (jnp.sqrt(v_hat) + ADAM_EPS) + ADAM_WD * w), m, v


def _adamw(w, g, m, v, rb, name):
    rows, cols = w.shape

    def body(w_ref, g_ref, m_ref, v_ref, d_ref, mo_ref, vo_ref):
        d, mo, vo = _adamw_math(w_ref[...], g_ref[...], m_ref[...], v_ref[...])
        d_ref[...] = d
        mo_ref[...] = mo
        vo_ref[...] = vo

    spec = pl.BlockSpec((rb, cols), lambda i: (i, 0))
    return pl.pallas_call(
        body, name=name, grid=(rows // rb,), in_specs=[spec] * 4, out_specs=[spec] * 3,
        out_shape=[jax.ShapeDtypeStruct(w.shape, F32)] * 3,
        compiler_params=_params(("arbitrary",)),
    )(w, g, m, v)


def _adamw_small(sm4, w, m, v):
    def body(sm4_ref, w_ref, m_ref, v_ref, g_ref, d_ref, mo_ref, vo_ref):
        g = ((sm4_ref[0] + sm4_ref[1]) + sm4_ref[2]) + sm4_ref[3]
        g_ref[...] = g
        d, mo, vo = _adamw_math(w_ref[...], g, m_ref[...], v_ref[...])
        d_ref[...] = d
        mo_ref[...] = mo
        vo_ref[...] = vo

    return pl.pallas_call(
        body, name="adamw_small", out_shape=[jax.ShapeDtypeStruct(w.shape, F32)] * 4,
        compiler_params=pltpu.CompilerParams(vmem_limit_bytes=VMEM_LIMIT),
    )(sm4, w, m, v)


SMALL_NAMES = ("attn_pre_norm", "mla_q_norm", "mla_kv_norm", "mla_w_ukv", "mla_out_norm", "hgrn_lb_logits",
               "hgrn_out_norm", "attn_post_norm", "ffn_pre_norm", "ffn_post_norm")
BIG_NAMES = ("w_in", "mla_w_uq", "w_out", "w_gate", "w_up", "w_down")
WEIGHT_NAMES = ("attn_pre_norm", "w_in", "mla_q_norm", "mla_w_uq", "mla_kv_norm", "mla_w_ukv", "mla_out_norm",
                "hgrn_lb_logits", "hgrn_out_norm", "w_out", "attn_post_norm", "ffn_pre_norm", "w_gate", "w_up", "w_down",
                "ffn_post_norm")


def _pack_small(vals):
    flat = jnp.concatenate([vals[n].reshape(-1) for n in SMALL_NAMES])
    return jnp.pad(flat, (0, SMALL_ROWS * D_MODEL - flat.shape[0])).reshape(SMALL_ROWS, D_MODEL)


def _unpack_small(buf, shapes):
    flat = buf.reshape(-1)
    out, off = {}, 0
    for n, size in zip(SMALL_NAMES, SMALL_SIZES):
        out[n] = flat[off:off + size].reshape(shapes[n])
        off += size
    return out


def kernel(x, positions, attn_pre_norm, w_in, mla_q_norm, mla_w_uq, mla_kv_norm, mla_w_ukv, mla_out_norm, hgrn_lb_logits, hgrn_out_norm, w_out, attn_post_norm, ffn_pre_norm, w_gate, w_up, w_down, ffn_post_norm, loss_target, m_attn_pre_norm, m_w_in, m_mla_q_norm, m_mla_w_uq, m_mla_kv_norm, m_mla_w_ukv, m_mla_out_norm, m_hgrn_lb_logits, m_hgrn_out_norm, m_w_out, m_attn_post_norm, m_ffn_pre_norm, m_w_gate, m_w_up, m_w_down, m_ffn_post_norm, v_attn_pre_norm, v_w_in, v_mla_q_norm, v_mla_w_uq, v_mla_kv_norm, v_mla_w_ukv, v_mla_out_norm, v_hgrn_lb_logits, v_hgrn_out_norm, v_w_out, v_attn_post_norm, v_ffn_pre_norm, v_w_gate, v_w_up, v_w_down, v_ffn_post_norm):
    args = locals()
    W = {n: args[n] for n in WEIGHT_NAMES}
    M = {n: args["m_" + n] for n in WEIGHT_NAMES}
    V = {n: args["v_" + n] for n in WEIGHT_NAMES}
    T = x.shape[1]
    cx, cy, cc = lax.axis_index("x"), lax.axis_index("y"), lax.axis_index("c")

    shard2d = {"w_in": (D_MODEL, D_IN // N_CHIPS), "mla_w_uq": (Q_RANK // N_CHIPS, MLA_HEADS * MLA_QK),
               "w_out": (D_MODEL // N_CHIPS, D_MODEL), "w_gate": (D_MODEL, FF_SHARD), "w_up": (D_MODEL, FF_SHARD),
               "w_down": (FF_SHARD, D_MODEL)}
    local_b = [W[n].reshape(shard2d[n]).astype(BF16) for n in BIG_NAMES]
    win4, wuq4, wout4, wg4, wu4, wd4 = _gather4(local_b, "gather_weights")
    win_full = win4.transpose(1, 0, 2).reshape(D_MODEL, D_IN)
    wuq_full = wuq4.reshape(Q_RANK, MLA_HEADS, MLA_QK)
    win_arr, wq_arr, wk_arr, wv_arr = _arrange_weights(win_full, wuq_full, mla_w_ukv[0].astype(BF16))
    small = {n: W[n][0] if n == "mla_w_ukv" else W[n].reshape(-1, W[n].shape[-1]) for n in SMALL_NAMES}

    loss_local, dx, grads = _local_step(x[0], positions.reshape(T, 1), loss_target[0], small, win_arr, wq_arr, wk_arr,
                                        wv_arr, wout4.reshape(D_MODEL, D_MODEL), wg4, wu4, wd4)

    dwin4 = grads["w_in"].reshape(D_MODEL, N_CHIPS, D_IN // N_CHIPS).transpose(1, 0, 2)
    pieces = [dwin4.reshape(N_CHIPS, ROWS_WIN, D_MODEL), grads["mla_w_uq"].reshape(N_CHIPS, ROWS_WUQ, D_MODEL),
              grads["w_out"].reshape(N_CHIPS, ROWS_WOUT, D_MODEL), grads["w_gate"].reshape(N_CHIPS, ROWS_FF, D_MODEL),
              grads["w_up"].reshape(N_CHIPS, ROWS_FF, D_MODEL), grads["w_down"]]
    used = ROWS_WIN + ROWS_WUQ + ROWS_WOUT + 3 * ROWS_FF
    g32 = jnp.concatenate(pieces + [jnp.zeros((N_CHIPS, PACK_ROWS - used, D_MODEL), F32)], axis=1)
    sm = _pack_small(grads)
    rsib, ssib = _pair_exchange(g32.astype(BF16), sm)
    place = jnp.stack([cc, 2 * cx + cy]).astype(jnp.int32)
    pb, own, pair = _pair_add(place, g32, rsib, sm, ssib)
    rici, sm4 = _chip_exchange(pb, pair)
    gfin_half = _chip_add(own, rici)
    gfin, smf = _pair_share(gfin_half, sm4)

    offs = [0, ROWS_WIN, ROWS_WIN + ROWS_WUQ, ROWS_WIN + ROWS_WUQ + ROWS_WOUT]
    offs += [offs[3] + ROWS_FF, offs[3] + 2 * ROWS_FF, offs[3] + 3 * ROWS_FF]
    row_blocks = {"w_in": 256, "mla_w_uq": 96, "w_out": 256, "w_gate": 256, "w_up": 256, "w_down": 352}
    G, DW, NM, NV = {}, {}, {}, {}
    for k, n in enumerate(BIG_NAMES):
        g2 = gfin[offs[k]:offs[k + 1]].reshape(shard2d[n])
        d, mo, vo = _adamw(W[n].reshape(shard2d[n]), g2, M[n].reshape(shard2d[n]), V[n].reshape(shard2d[n]),
                           row_blocks[n], "adamw_" + n)
        G[n], DW[n], NM[n], NV[n] = (t.reshape(W[n].shape) for t in (g2, d, mo, vo))
    gs, ds, ms, vs = _adamw_small(smf, _pack_small(W), _pack_small(M), _pack_small(V))
    shapes = {n: W[n].shape for n in SMALL_NAMES}
    for dst, buf in ((G, gs), (DW, ds), (NM, ms), (NV, vs)):
        dst.update(_unpack_small(buf, shapes))

    loss = lax.psum(loss_local, ("x", "y", "c"))
    return (loss, dx[None], *[G[n] for n in WEIGHT_NAMES], *[DW[n] for n in WEIGHT_NAMES],
            *[NM[n] for n in WEIGHT_NAMES], *[NV[n] for n in WEIGHT_NAMES])
```

```python
import functools

import jax
import jax.numpy as jnp
from jax import lax
from jax.experimental import pallas as pl
from jax.experimental.pallas import tpu as pltpu

F32 = jnp.float32
BF16 = jnp.bfloat16
MXU_DTYPE = BF16

D_MODEL = 1024
MLA_HEADS = 8
MLA_NOPE = 64
MLA_ROPE = 32
MLA_V = 64
MLA_QK = MLA_NOPE + MLA_ROPE
Q_RANK = 384
KV_RANK = 128
MLA_WIDTH = MLA_HEADS * MLA_V
HEAD_PAD = 128
HGRN_HEADS = 4
HGRN_DIM = 128
HGRN_WIDTH = HGRN_HEADS * HGRN_DIM
CHUNK = 64
SUB = 16
D_IN = Q_RANK + KV_RANK + MLA_ROPE + 4 * HGRN_WIDTH
D_IN_ARR = Q_RANK + KV_RANK + HEAD_PAD + 4 * HGRN_WIDTH
D_FF = 2816
N_CHIPS = 4
FF_SHARD = D_FF // N_CHIPS
EPS = 1e-6
ROPE_THETA = 10000.0
ATTN_SCALE = MLA_QK ** -0.5
NEG_BIG = -1e30

ADAM_LR = 0.001
ADAM_B1 = 0.9
ADAM_B2 = 0.999
ADAM_EPS = 1e-08
ADAM_WD = 0.01
ADAM_STEP = 10

VMEM_LIMIT = 56 * 1024 * 1024

SMALL_ROWS = 144
SMALL_SIZES = (1024, 384, 128, 131072, 512, 1024, 512, 1024, 1024, 1024)

MESH = pl.DeviceIdType.MESH
ANY = pl.BlockSpec(memory_space=pl.ANY)


def _dot(a, b, dims, exact):
    if exact:
        return lax.dot_general(a.astype(F32), b.astype(F32), (dims, ((), ())), precision=lax.Precision.HIGHEST,
                               preferred_element_type=F32)
    return lax.dot_general(a.astype(MXU_DTYPE), b.astype(MXU_DTYPE), (dims, ((), ())), preferred_element_type=F32)


def _mm(a, b, exact=False):
    return _dot(a, b, ((1,), (0,)), exact)


def _mm_nt(a, b, exact=False):
    return _dot(a, b, ((1,), (1,)), exact)


def _mm_tn(a, b, exact=False):
    return _dot(a, b, ((0,), (0,)), exact)


def _rms_fwd(x, w):
    r = lax.rsqrt(jnp.mean(x * x, axis=-1, keepdims=True) + EPS)
    xn = x * r
    return xn * w, xn, r


def _rms_bwd(dy, xn, r, w):
    dxn = dy * w
    dx = r * (dxn - xn * jnp.mean(dxn * xn, axis=-1, keepdims=True))
    dw = jnp.sum(dy * xn, axis=0, keepdims=True)
    return dx, dw


def _group_sums(v, gs):
    t, n = v.shape
    lane = lax.broadcasted_iota(jnp.int32, (t, 128), 1)
    out = []
    for p in range(n // 128):
        vb = v[:, 128 * p:128 * (p + 1)]
        if gs == 128:
            out.append(jnp.sum(vb, axis=-1, keepdims=True))
        else:
            out.append(jnp.sum(jnp.where(lane < 64, vb, 0.0), axis=-1, keepdims=True))
            out.append(jnp.sum(jnp.where(lane >= 64, vb, 0.0), axis=-1, keepdims=True))
    return out


def _group_bcast(sums, gs, t):
    lane = lax.broadcasted_iota(jnp.int32, (t, 128), 1)
    if gs == 128:
        return jnp.concatenate([jnp.broadcast_to(s, (t, 128)) for s in sums], axis=-1)
    return jnp.concatenate([jnp.where(lane < 64, sums[2 * p], sums[2 * p + 1]) for p in range(len(sums) // 2)],
                           axis=-1)


def _grms_fwd(x, w, gs):
    t = x.shape[0]
    r = lax.rsqrt(_group_bcast(_group_sums(x * x, gs), gs, t) * (1.0 / gs) + EPS)
    xn = x * r
    return xn * w, xn, r


def _grms_bwd(dy, xn, r, w, gs):
    t = dy.shape[0]
    dxn = dy * w
    dx = r * (dxn - xn * (_group_bcast(_group_sums(dxn * xn, gs), gs, t) * (1.0 / gs)))
    dw = jnp.sum(dy * xn, axis=0, keepdims=True)
    return dx, dw


def _rope_tables(c_tab, s_tab):
    lane = lax.broadcasted_iota(jnp.int32, c_tab.shape, 1)
    first = (lane >= MLA_NOPE) & (lane < MLA_NOPE + MLA_ROPE // 2)
    second = (lane >= MLA_NOPE + MLA_ROPE // 2) & (lane < MLA_QK)
    return c_tab, jnp.where(first, -s_tab, 0.0), jnp.where(second, s_tab, 0.0)


def _rope(v, c, sa, sb):
    return v * c + pltpu.roll(v, HEAD_PAD - MLA_ROPE // 2, 1) * sa + pltpu.roll(v, MLA_ROPE // 2, 1) * sb


def _rope_bwd(d, c, sa, sb):
    return d * c - pltpu.roll(d, HEAD_PAD - MLA_ROPE // 2, 1) * sa - pltpu.roll(d, MLA_ROPE // 2, 1) * sb


def _params(sem, vmem=VMEM_LIMIT):
    return pltpu.CompilerParams(dimension_semantics=sem, vmem_limit_bytes=vmem)


def _in_fwd(x, pos, invf, w_pre, win, qnw, wq, kvnw, wk, wv, tt=256):
    T = x.shape[0]

    def body(x_ref, pos_ref, invf_ref, wpre_ref, win_ref, qnw_ref, wq_ref, kvnw_ref, wk_ref, wv_ref,
             cq_ref, ckv_ref, xph_ref, q_ref, k_ref, v_ref, rc_ref, rs_ref):
        u, _, _ = _rms_fwd(x_ref[...], wpre_ref[...])
        xp = _mm(u, win_ref[...])
        cq = xp[:, :Q_RANK]
        ckv = xp[:, Q_RANK:Q_RANK + KV_RANK]
        kr = xp[:, Q_RANK + KV_RANK:Q_RANK + KV_RANK + HEAD_PAD]
        cq_ref[...] = cq
        ckv_ref[...] = ckv
        xph_ref[...] = xp[:, Q_RANK + KV_RANK + HEAD_PAD:]
        ang = pos_ref[...].astype(F32) * invf_ref[...]
        c_tab = jnp.cos(ang)
        s_tab = jnp.sin(ang)
        rc_ref[...] = c_tab
        rs_ref[...] = s_tab
        c, sa, sb = _rope_tables(c_tab, s_tab)
        qn, _, _ = _rms_fwd(cq, qnw_ref[...])
        q = _mm(qn, wq_ref[...])
        kvn, _, _ = _rms_fwd(ckv, kvnw_ref[...])
        kn = _mm(kvn, wk_ref[...])
        v_ref[...] = _mm(kvn, wv_ref[...]).astype(v_ref.dtype)
        krr = _rope(kr, c, sa, sb)
        for h in range(MLA_HEADS):
            sl = slice(HEAD_PAD * h, HEAD_PAD * (h + 1))
            q_ref[:, sl] = _rope(q[:, sl], c, sa, sb).astype(q_ref.dtype)
            k_ref[:, sl] = (kn[:, sl] + krr).astype(k_ref.dtype)

    row = lambda w: pl.BlockSpec((tt, w), lambda i: (i, 0))
    full = lambda a: pl.BlockSpec(a.shape, lambda i: (0,) * a.ndim)
    qk_w = MLA_HEADS * HEAD_PAD
    return pl.pallas_call(
        body, name="in_fwd", grid=(T // tt,),
        in_specs=[row(D_MODEL), row(1), full(invf), full(w_pre), full(win), full(qnw), full(wq), full(kvnw),
                  full(wk), full(wv)],
        out_specs=[row(Q_RANK), row(KV_RANK), row(4 * HGRN_WIDTH), row(qk_w), row(qk_w), row(MLA_WIDTH),
                   row(HEAD_PAD), row(HEAD_PAD)],
        out_shape=[jax.ShapeDtypeStruct((T, Q_RANK), F32), jax.ShapeDtypeStruct((T, KV_RANK), F32),
                   jax.ShapeDtypeStruct((T, 4 * HGRN_WIDTH), F32), jax.ShapeDtypeStruct((T, qk_w), MXU_DTYPE),
                   jax.ShapeDtypeStruct((T, qk_w), MXU_DTYPE), jax.ShapeDtypeStruct((T, MLA_WIDTH), MXU_DTYPE),
                   jax.ShapeDtypeStruct((T, HEAD_PAD), F32), jax.ShapeDtypeStruct((T, HEAD_PAD), F32)],
        compiler_params=_params(("arbitrary",)),
    )(x, pos, invf, w_pre, win, qnw, wq, kvnw, wk, wv)


def _attn_fwd(qb, kb, vb, tq=512):
    T = qb.shape[0]
    nq = T // tq

    def body(q_ref, k_ref, v_ref, o_ref, lse_ref, m_scr, l_scr, acc_scr):
        qi = pl.program_id(1)
        lane = lax.broadcasted_iota(jnp.int32, (tq, 128), 1)
        heads = [slice(HEAD_PAD * a, HEAD_PAD * (a + 1)) for a in range(2)]
        m_scr[...] = jnp.full_like(m_scr, NEG_BIG)
        l_scr[...] = jnp.zeros_like(l_scr)
        acc_scr[...] = jnp.zeros_like(acc_scr)

        def step(j, masked):
            start = pl.multiple_of(j * tq, tq)
            vj = v_ref[pl.ds(start, tq), :]
            for a in range(2):
                s = _mm_nt(q_ref[:, heads[a]], k_ref[pl.ds(start, tq), heads[a]]) * ATTN_SCALE
                if masked:
                    rr = lax.broadcasted_iota(jnp.int32, (tq, tq), 0)
                    cc = lax.broadcasted_iota(jnp.int32, (tq, tq), 1)
                    s = jnp.where(cc <= rr, s, NEG_BIG)
                m = m_scr[a]
                m_new = jnp.maximum(m, jnp.max(s, axis=-1, keepdims=True))
                alpha = jnp.exp(m - m_new)
                p = jnp.exp(s - m_new)
                m_scr[a] = m_new
                l_scr[a] = l_scr[a] * alpha + jnp.sum(p, axis=-1, keepdims=True)
                acc_scr[a] = acc_scr[a] * alpha + _mm(p, vj)

        def loop_body(j, _):
            step(j, False)
            return 0

        lax.fori_loop(0, qi, loop_body, 0)
        step(qi, True)
        outs = []
        for a in range(2):
            l = l_scr[a]
            outs.append(acc_scr[a] / l)
            lse_ref[a] = m_scr[a] + jnp.log(l)
        o_ref[...] = jnp.where(lane < MLA_V, outs[0], outs[1])

    return pl.pallas_call(
        body, name="attn_fwd", grid=(MLA_HEADS // 2, nq),
        in_specs=[pl.BlockSpec((tq, 2 * HEAD_PAD), lambda p, i: (i, p)),
                  pl.BlockSpec((T, 2 * HEAD_PAD), lambda p, i: (0, p)),
                  pl.BlockSpec((T, 2 * MLA_V), lambda p, i: (0, p))],
        out_specs=[pl.BlockSpec((tq, 2 * MLA_V), lambda p, i: (i, p)),
                   pl.BlockSpec((2, tq, 1), lambda p, i: (p, i, 0))],
        out_shape=[jax.ShapeDtypeStruct((T, MLA_WIDTH), F32), jax.ShapeDtypeStruct((MLA_HEADS, T, 1), F32)],
        scratch_shapes=[pltpu.VMEM((2, tq, 1), F32), pltpu.VMEM((2, tq, 1), F32), pltpu.VMEM((2, tq, 128), F32)],
        compiler_params=_params(("arbitrary", "arbitrary")),
    )(qb, kb, vb)


def _attn_bwd(qb, kb, vb, do, lse, dvec, tq=256):
    T = qb.shape[0]
    nq = T // tq

    def body(q_ref, k_ref, v_ref, do_ref, lse_ref, d_ref, dq_ref, dk_ref, dv_ref, va_scr, dv_scr):
        j = pl.program_id(1)

        @pl.when(j == 0)
        def _():
            dq_ref[...] = jnp.zeros_like(dq_ref)

        lane = lax.broadcasted_iota(jnp.int32, (tq, 128), 1)
        heads = [slice(HEAD_PAD * a, HEAD_PAD * (a + 1)) for a in range(2)]
        vpair = v_ref[...]
        va_scr[0] = jnp.where(lane < MLA_V, vpair, jnp.zeros_like(vpair))
        va_scr[1] = jnp.where(lane >= MLA_V, vpair, jnp.zeros_like(vpair))
        dk_ref[...] = jnp.zeros_like(dk_ref)
        dv_scr[...] = jnp.zeros_like(dv_scr)

        def step(i, masked):
            start = pl.multiple_of(i * tq, tq)
            doi = do_ref[pl.ds(start, tq), :]
            for a in range(2):
                qi = q_ref[pl.ds(start, tq), heads[a]]
                kj = k_ref[:, heads[a]]
                s = _mm_nt(qi, kj) * ATTN_SCALE
                p = jnp.exp(s - lse_ref[a, pl.ds(start, tq), :])
                if masked:
                    rr = lax.broadcasted_iota(jnp.int32, (tq, tq), 0)
                    cc = lax.broadcasted_iota(jnp.int32, (tq, tq), 1)
                    p = jnp.where(cc <= rr, p, 0.0)
                dv_scr[a] += _mm_tn(p, doi)
                dp = _mm_nt(doi, va_scr[a])
                ds = p * (dp - d_ref[a, pl.ds(start, tq), :]) * ATTN_SCALE
                dk_ref[:, heads[a]] += _mm_tn(ds, qi)
                dq_ref[pl.ds(start, tq), heads[a]] += _mm(ds, kj)

        def loop_body(i, _):
            step(i, False)
            return 0

        step(j, True)
        lax.fori_loop(j + 1, nq, loop_body, 0)
        dv_ref[...] = jnp.where(lane < MLA_V, dv_scr[0], dv_scr[1])

    return pl.pallas_call(
        body, name="attn_bwd", grid=(MLA_HEADS // 2, nq),
        in_specs=[pl.BlockSpec((T, 2 * HEAD_PAD), lambda p, j: (0, p)),
                  pl.BlockSpec((tq, 2 * HEAD_PAD), lambda p, j: (j, p)),
                  pl.BlockSpec((tq, 2 * MLA_V), lambda p, j: (j, p)),
                  pl.BlockSpec((T, 2 * MLA_V), lambda p, j: (0, p)),
                  pl.BlockSpec((2, T, 1), lambda p, j: (p, 0, 0)),
                  pl.BlockSpec((2, T, 1), lambda p, j: (p, 0, 0))],
        out_specs=[pl.BlockSpec((T, 2 * HEAD_PAD), lambda p, j: (0, p)),
                   pl.BlockSpec((tq, 2 * HEAD_PAD), lambda p, j: (j, p)),
                   pl.BlockSpec((tq, 2 * MLA_V), lambda p, j: (j, p))],
        out_shape=[jax.ShapeDtypeStruct((T, MLA_HEADS * HEAD_PAD), F32),
                   jax.ShapeDtypeStruct((T, MLA_HEADS * HEAD_PAD), F32),
                   jax.ShapeDtypeStruct((T, MLA_WIDTH), F32)],
        scratch_shapes=[pltpu.VMEM((2, tq, 2 * MLA_V), vb.dtype), pltpu.VMEM((2, tq, 2 * MLA_V), F32)],
        compiler_params=_params(("arbitrary", "arbitrary")),
    )(qb, kb, vb, do, lse, dvec)


def _cumsum_rows(x):
    n = x.shape[0]
    row = lax.broadcasted_iota(jnp.int32, x.shape, 0)
    s = 1
    while s < n:
        x = x + jnp.where(row >= s, pltpu.roll(x, s, 0), 0.0)
        s *= 2
    return x


def _rev_cumsum_rows(x):
    n = x.shape[0]
    row = lax.broadcasted_iota(jnp.int32, x.shape, 0)
    s = 1
    while s < n:
        x = x + jnp.where(row < n - s, pltpu.roll(x, n - s, 0), 0.0)
        s *= 2
    return x


def _lb_from_logits(l):
    l0, l1 = l[0:1, :], l[1:2, :]
    m = jnp.maximum(l0, l1)
    e0, e1 = jnp.exp(l0 - m), jnp.exp(l1 - m)
    return e0 / (e0 + e1)


def _hgrn_gates(hq, hf, lb):
    sig_f = jax.nn.sigmoid(hf)
    f = lb + (1.0 - lb) * sig_f
    sig_q = jax.nn.sigmoid(hq)
    return sig_f, f, jnp.log(f), 1.0 - f, sig_q, hq * sig_q


def _hgrn_intra(q, kk, b, exact=False):
    row = lax.broadcasted_iota(jnp.int32, b.shape, 0)
    qs, ks, eqs, eks, a_rows = [], [], [], [], []
    for i in range(CHUNK // SUB):
        ref = b[SUB * i:SUB * i + 1, :]
        eq = jnp.exp(b[SUB * i:SUB * (i + 1), :] - ref)
        ek = jnp.exp(jnp.where(row < SUB * (i + 1), ref - b, NEG_BIG))
        qi = q[SUB * i:SUB * (i + 1), :] * eq
        ki = kk * ek
        a_rows.append(_mm_nt(qi, ki, exact))
        qs.append(qi), ks.append(ki), eqs.append(eq), eks.append(ek)
    tt = lax.broadcasted_iota(jnp.int32, (CHUNK, CHUNK), 0)
    ss = lax.broadcasted_iota(jnp.int32, (CHUNK, CHUNK), 1)
    causal = ss <= tt
    a = jnp.where(causal, jnp.concatenate(a_rows, axis=0), 0.0)
    return a, causal, qs, ks, eqs, eks


def _hgrn_fwd(xph, lbl, tg=512):
    T = xph.shape[0]
    ng, ncg = T // tg, tg // CHUNK

    def body(lbl_ref, hq_ref, hf_ref, hi_ref, o_ref, st_ref, s_scr):
        @pl.when(pl.program_id(1) == 0)
        def _():
            s_scr[...] = jnp.zeros_like(s_scr)

        lb = _lb_from_logits(lbl_ref[...])

        def chunk(c, _):
            rows = pl.ds(pl.multiple_of(c * CHUNK, CHUNK), CHUNK)
            _, _, lf, kk, _, q = _hgrn_gates(hq_ref[rows, :], hf_ref[rows, :], lb)
            v = hi_ref[rows, :]
            st = s_scr[...]
            st_ref[0, c] = st
            b = _cumsum_rows(lf)
            a = _hgrn_intra(q, kk, b)[0]
            o_ref[rows, :] = _mm_nt(q * jnp.exp(b), st) + _mm(a, v)
            b_last = b[CHUNK - 1:CHUNK, :]
            s_scr[...] = st * jnp.exp(b_last) + _mm_tn(v, kk * jnp.exp(b_last - b))
            return 0

        lax.fori_loop(0, ncg, chunk, 0)

    col = lambda k: pl.BlockSpec((tg, HGRN_DIM), lambda h, g: (g, k * HGRN_HEADS + h))
    return pl.pallas_call(
        body, name="hgrn_fwd", grid=(HGRN_HEADS, ng),
        in_specs=[pl.BlockSpec((2, HGRN_DIM), lambda h, g: (0, h)), col(0), col(1), col(2)],
        out_specs=[pl.BlockSpec((tg, HGRN_DIM), lambda h, g: (g, h)),
                   pl.BlockSpec((1, ncg, HGRN_DIM, HGRN_DIM), lambda h, g: (h, g, 0, 0))],
        out_shape=[jax.ShapeDtypeStruct((T, HGRN_WIDTH), F32),
                   jax.ShapeDtypeStruct((HGRN_HEADS, T // CHUNK, HGRN_DIM, HGRN_DIM), F32)],
        scratch_shapes=[pltpu.VMEM((HGRN_DIM, HGRN_DIM), F32)],
        compiler_params=_params(("arbitrary", "arbitrary")),
    )(lbl, xph, xph, xph)


def _hgrn_bwd(xph, lbl, states, d_o, tg=512):
    T = xph.shape[0]
    ng, ncg = T // tg, tg // CHUNK

    def body(lbl_ref, hq_ref, hf_ref, hi_ref, st_ref, do_ref, dhq_ref, dhf_ref, dhi_ref, dlg_ref, ds_scr, dlb_scr):
        g = pl.program_id(1)

        @pl.when(g == 0)
        def _():
            ds_scr[...] = jnp.zeros_like(ds_scr)
            dlb_scr[...] = jnp.zeros_like(dlb_scr)

        lb = _lb_from_logits(lbl_ref[...])

        def chunk(ci, _):
            c = ncg - 1 - ci
            rows = pl.ds(pl.multiple_of(c * CHUNK, CHUNK), CHUNK)
            hq = hq_ref[rows, :]
            sig_f, f, lf, kk, sig_q, q = _hgrn_gates(hq, hf_ref[rows, :], lb)
            v = hi_ref[rows, :]
            do = do_ref[rows, :]
            st = st_ref[0, c]
            dst = ds_scr[...]
            b = _cumsum_rows(lf)
            eb = jnp.exp(b)
            qe = q * eb
            a, causal, qs, ks, eqs, eks = _hgrn_intra(q, kk, b, exact=True)
            b_last = b[CHUNK - 1:CHUNK, :]
            ebl = jnp.exp(b_last)
            el = jnp.exp(b_last - b)
            dv = _mm_tn(a, do, True) + _mm_nt(kk * el, dst, True)
            da = jnp.where(causal, _mm_nt(do, v, True), 0.0)
            dq_rows = []
            dk = jnp.zeros_like(q)
            for i in range(CHUNK // SUB):
                dai = da[SUB * i:SUB * (i + 1), :]
                dq_rows.append(_mm(dai, ks[i], True) * eqs[i])
                dk = dk + _mm_tn(dai, qs[i], True) * eks[i]
            dq = _mm(do, st, True) * eb + jnp.concatenate(dq_rows, axis=0)
            dk_state = _mm(v, dst, True) * el
            dk = dk + dk_state
            e_last = (ebl * jnp.sum(st * dst, axis=0, keepdims=True)
                      + jnp.sum(kk * dk_state, axis=0, keepdims=True))
            dlf = _rev_cumsum_rows(q * dq - kk * dk) + e_last
            ds_scr[...] = dst * ebl + _mm_tn(do, qe, True)
            df = dlf / f - dk
            dhf_ref[rows, :] = df * (1.0 - lb) * sig_f * (1.0 - sig_f)
            dlb_scr[...] += jnp.sum(df * (1.0 - sig_f), axis=0, keepdims=True)
            dhq_ref[rows, :] = dq * sig_q * (1.0 + hq * (1.0 - sig_q))
            dhi_ref[rows, :] = dv
            return 0

        lax.fori_loop(0, ncg, chunk, 0)

        @pl.when(g == ng - 1)
        def _():
            dl0 = dlb_scr[...] * lb * (1.0 - lb)
            dlg_ref[...] = jnp.concatenate([dl0, -dl0], axis=0)

    col = lambda k: pl.BlockSpec((tg, HGRN_DIM), lambda h, g: (ng - 1 - g, k * HGRN_HEADS + h))
    ocol = pl.BlockSpec((tg, HGRN_DIM), lambda h, g: (ng - 1 - g, h))
    big = jax.ShapeDtypeStruct((T, HGRN_WIDTH), F32)
    return pl.pallas_call(
        body, name="hgrn_bwd", grid=(HGRN_HEADS, ng),
        in_specs=[pl.BlockSpec((2, HGRN_DIM), lambda h, g: (0, h)), col(0), col(1), col(2),
                  pl.BlockSpec((1, ncg, HGRN_DIM, HGRN_DIM), lambda h, g: (h, ng - 1 - g, 0, 0)), ocol],
        out_specs=[ocol, ocol, ocol, pl.BlockSpec((2, HGRN_DIM), lambda h, g: (0, h))],
        out_shape=[big, big, big, jax.ShapeDtypeStruct((2, HGRN_WIDTH), F32)],
        scratch_shapes=[pltpu.VMEM((HGRN_DIM, HGRN_DIM), F32), pltpu.VMEM((1, HGRN_DIM), F32)],
        compiler_params=_params(("arbitrary", "arbitrary")),
    )(lbl, xph, xph, xph, states, d_o)


def _ffn_fwd(x, o_raw, oh_raw, xph, tgt, wout, w_mla, w_hg, w_post, w_fpre, w_fpost, wg, wu, wd, tt=256):
    T = x.shape[0]
    nj = N_CHIPS

    def body(x_ref, o_ref, oh_ref, hg_ref, tgt_ref, wout_ref, wmla_ref, whg_ref, wpost_ref, wfpre_ref, wfpost_ref,
             wg_ref, wu_ref, wd_ref,
             h1_ref, y1_ref, z_ref, mix_ref, g_ref, up_ref, dy2_ref, dh2_ref, loss_ref, dwf_ref,
             z_scr, y2_scr):
        i, j = pl.program_id(0), pl.program_id(1)

        @pl.when((i == 0) & (j == 0))
        def _():
            loss_ref[...] = jnp.zeros_like(loss_ref)
            dwf_ref[...] = jnp.zeros_like(dwf_ref)

        @pl.when(j == 0)
        def _():
            om, _, _ = _grms_fwd(o_ref[...], wmla_ref[...], MLA_V)
            hg = hg_ref[...]
            ohn, _, _ = _grms_fwd(oh_ref[...], whg_ref[...], HGRN_DIM)
            mix = jnp.concatenate([om, ohn * (hg * jax.nn.sigmoid(hg))], axis=-1)
            mix_ref[...] = mix.astype(mix_ref.dtype)
            y1 = _mm(mix, wout_ref[...])
            y1_ref[...] = y1
            h1 = x_ref[...] + _rms_fwd(y1, wpost_ref[...])[0]
            h1_ref[...] = h1
            z = _rms_fwd(h1, wfpre_ref[...])[0].astype(z_scr.dtype)
            z_scr[...] = z
            z_ref[...] = z
            y2_scr[...] = jnp.zeros_like(y2_scr)

        z = z_scr[...]
        g = _mm(z, wg_ref[0])
        up = _mm(z, wu_ref[0])
        g_ref[0] = g
        up_ref[0] = up
        y2_scr[...] += _mm(g * jax.nn.sigmoid(g) * up, wd_ref[0])

        @pl.when(j == nj - 1)
        def _():
            w = wfpost_ref[...]
            y2s, y2n, r2 = _rms_fwd(y2_scr[...], w)
            e = h1_ref[...] + y2s - tgt_ref[...]
            loss_ref[...] += jnp.sum(e * e, axis=0, keepdims=True)
            dh2 = e * (1.0 / D_MODEL)
            dh2_ref[...] = dh2
            dy2, dwf = _rms_bwd(dh2, y2n, r2, w)
            dy2_ref[...] = dy2.astype(dy2_ref.dtype)
            dwf_ref[...] += dwf

    row = lambda w: pl.BlockSpec((tt, w), lambda i, j: (i, 0))
    full = lambda a: pl.BlockSpec(a.shape, lambda i, j: (0,) * a.ndim)
    vec = pl.BlockSpec((1, D_MODEL), lambda i, j: (0, 0))
    return pl.pallas_call(
        body, name="ffn_fwd", grid=(T // tt, nj),
        in_specs=[row(D_MODEL), row(MLA_WIDTH), row(HGRN_WIDTH),
                  pl.BlockSpec((tt, HGRN_WIDTH), lambda i, j: (i, 3)), row(D_MODEL), full(wout), full(w_mla),
                  full(w_hg), vec, vec, vec,
                  pl.BlockSpec((1, D_MODEL, FF_SHARD), lambda i, j: (j, 0, 0)),
                  pl.BlockSpec((1, D_MODEL, FF_SHARD), lambda i, j: (j, 0, 0)),
                  pl.BlockSpec((1, FF_SHARD, D_MODEL), lambda i, j: (j, 0, 0))],
        out_specs=[row(D_MODEL), row(D_MODEL), row(D_MODEL), row(D_MODEL),
                   pl.BlockSpec((1, tt, FF_SHARD), lambda i, j: (j, i, 0)),
                   pl.BlockSpec((1, tt, FF_SHARD), lambda i, j: (j, i, 0)),
                   row(D_MODEL), row(D_MODEL), vec, vec],
        out_shape=[jax.ShapeDtypeStruct((T, D_MODEL), F32), jax.ShapeDtypeStruct((T, D_MODEL), F32),
                   jax.ShapeDtypeStruct((T, D_MODEL), MXU_DTYPE), jax.ShapeDtypeStruct((T, D_MODEL), MXU_DTYPE),
                   jax.ShapeDtypeStruct((nj, T, FF_SHARD), F32), jax.ShapeDtypeStruct((nj, T, FF_SHARD), F32),
                   jax.ShapeDtypeStruct((T, D_MODEL), MXU_DTYPE), jax.ShapeDtypeStruct((T, D_MODEL), F32),
                   jax.ShapeDtypeStruct((1, D_MODEL), F32), jax.ShapeDtypeStruct((1, D_MODEL), F32)],
        scratch_shapes=[pltpu.VMEM((tt, D_MODEL), MXU_DTYPE), pltpu.VMEM((tt, D_MODEL), F32)],
        compiler_params=_params(("arbitrary", "arbitrary")),
    )(x, o_raw, oh_raw, xph, tgt, wout, w_mla, w_hg, w_post, w_fpre, w_fpost, wg, wu, wd)


def _ffn_bwd(zb, g, up, dy2b, wg, wu, wd, tt=512):
    T = zb.shape[0]
    nj = N_CHIPS

    def body(z_ref, g_ref, up_ref, dy2_ref, wg_ref, wu_ref, wd_ref, dwg_ref, dwu_ref, dwd_ref, dz_ref):
        @pl.when(pl.program_id(1) == 0)
        def _():
            dwg_ref[...] = jnp.zeros_like(dwg_ref)
            dwu_ref[...] = jnp.zeros_like(dwu_ref)
            dwd_ref[...] = jnp.zeros_like(dwd_ref)

        z, g_, up_, dy2 = z_ref[...], g_ref[0], up_ref[0], dy2_ref[...]
        sg = jax.nn.sigmoid(g_)
        act = g_ * sg
        dff = _mm_nt(dy2, wd_ref[0])
        dwd_ref[0] += _mm_tn(act * up_, dy2)
        dg = dff * up_ * sg * (1.0 + g_ * (1.0 - sg))
        dup = dff * act
        dwg_ref[0] += _mm_tn(z, dg)
        dwu_ref[0] += _mm_tn(z, dup)
        dz_ref[0] = _mm_nt(dg, wg_ref[0]) + _mm_nt(dup, wu_ref[0])

    row = pl.BlockSpec((tt, D_MODEL), lambda j, i: (i, 0))
    act = pl.BlockSpec((1, tt, FF_SHARD), lambda j, i: (j, i, 0))
    w_in = pl.BlockSpec((1, D_MODEL, FF_SHARD), lambda j, i: (j, 0, 0))
    w_dn = pl.BlockSpec((1, FF_SHARD, D_MODEL), lambda j, i: (j, 0, 0))
    return pl.pallas_call(
        body, name="ffn_bwd", grid=(nj, T // tt),
        in_specs=[row, act, act, row, w_in, w_in, w_dn],
        out_specs=[w_in, w_in, w_dn, pl.BlockSpec((1, tt, D_MODEL), lambda j, i: (j, i, 0))],
        out_shape=[jax.ShapeDtypeStruct((nj, D_MODEL, FF_SHARD), F32), jax.ShapeDtypeStruct((nj, D_MODEL, FF_SHARD), F32),
                   jax.ShapeDtypeStruct((nj, FF_SHARD, D_MODEL), F32), jax.ShapeDtypeStruct((nj, T, D_MODEL), F32)],
        compiler_params=_params(("arbitrary", "arbitrary")),
    )(zb, g, up, dy2b, wg, wu, wd)


def _mid_bwd(dzp, dh2, h1, y1, mixb, o_raw, oh_raw, xph, wout, w_fpre, w_post, w_mla, w_hg, tt=256):
    T = dh2.shape[0]

    def body(dzp_ref, dh2_ref, h1_ref, y1_ref, mix_ref, o_ref, oh_ref, hg_ref, wout_ref, wfpre_ref, wpost_ref,
             wmla_ref, whg_ref,
             dh1_ref, dwout_ref, do_ref, doh_ref, dhg_ref, dvec_ref, dwfpre_ref, dwpost_ref, dwmla_ref, dwhg_ref):
        @pl.when(pl.program_id(0) == 0)
        def _():
            for r in (dwout_ref, dwfpre_ref, dwpost_ref, dwmla_ref, dwhg_ref):
                r[...] = jnp.zeros_like(r)

        dz = dzp_ref[0] + dzp_ref[1] + dzp_ref[2] + dzp_ref[3]
        wfpre = wfpre_ref[...]
        _, h1n, r = _rms_fwd(h1_ref[...], wfpre)
        dh1_z, dwfpre = _rms_bwd(dz, h1n, r, wfpre)
        dwfpre_ref[...] += dwfpre
        dh1 = dh2_ref[...] + dh1_z
        dh1_ref[...] = dh1
        wpost = wpost_ref[...]
        _, y1n, r1 = _rms_fwd(y1_ref[...], wpost)
        dy1, dwpost = _rms_bwd(dh1, y1n, r1, wpost)
        dwpost_ref[...] += dwpost
        dmix = _mm_nt(dy1, wout_ref[...])
        dwout_ref[...] += _mm_tn(mix_ref[...], dy1)
        wmla = wmla_ref[...]
        o = o_ref[...]
        _, on, ro = _grms_fwd(o, wmla, MLA_V)
        d_o, dwmla = _grms_bwd(dmix[:, :MLA_WIDTH], on, ro, wmla, MLA_V)
        dwmla_ref[...] += dwmla
        do_ref[...] = d_o
        for h, s in enumerate(_group_sums(d_o * o, MLA_V)):
            dvec_ref[h] = s
        whg = whg_ref[...]
        hg = hg_ref[...]
        sg = jax.nn.sigmoid(hg)
        _, ohn, rh = _grms_fwd(oh_ref[...], whg, HGRN_DIM)
        dmh = dmix[:, MLA_WIDTH:]
        dhg_ref[...] = dmh * ohn * whg * sg * (1.0 + hg * (1.0 - sg))
        d_oh, dwhg = _grms_bwd(dmh * (hg * sg), ohn, rh, whg, HGRN_DIM)
        dwhg_ref[...] += dwhg
        doh_ref[...] = d_oh

    row = lambda w: pl.BlockSpec((tt, w), lambda i: (i, 0))
    full = lambda a: pl.BlockSpec(a.shape, lambda i: (0,) * a.ndim)
    vec = lambda w: pl.BlockSpec((1, w), lambda i: (0, 0))
    sds = jax.ShapeDtypeStruct
    return pl.pallas_call(
        body, name="mid_bwd", grid=(T // tt,),
        in_specs=[pl.BlockSpec((N_CHIPS, tt, D_MODEL), lambda i: (0, i, 0)), row(D_MODEL), row(D_MODEL), row(D_MODEL),
                  row(D_MODEL), row(MLA_WIDTH), row(HGRN_WIDTH), pl.BlockSpec((tt, HGRN_WIDTH), lambda i: (i, 3)),
                  full(wout), vec(D_MODEL), vec(D_MODEL), vec(MLA_WIDTH), vec(HGRN_WIDTH)],
        out_specs=[row(D_MODEL), full(wout), row(MLA_WIDTH), row(HGRN_WIDTH), row(HGRN_WIDTH),
                   pl.BlockSpec((MLA_HEADS, tt, 1), lambda i: (0, i, 0)),
                   vec(D_MODEL), vec(D_MODEL), vec(MLA_WIDTH), vec(HGRN_WIDTH)],
        out_shape=[sds((T, D_MODEL), F32), sds(wout.shape, F32), sds((T, MLA_WIDTH), F32), sds((T, HGRN_WIDTH), F32),
                   sds((T, HGRN_WIDTH), F32), sds((MLA_HEADS, T, 1), F32),
                   sds((1, D_MODEL), F32), sds((1, D_MODEL), F32), sds((1, MLA_WIDTH), F32), sds((1, HGRN_WIDTH), F32)],
        compiler_params=_params(("arbitrary",)),
    )(dzp, dh2, h1, y1, mixb, o_raw, oh_raw, xph, wout, w_fpre, w_post, w_mla, w_hg)


def _in_bwd(x, dh1, cq, ckv, dq, dk, dv, dhq, dhf, dhi, dhg, rc, rs, w_pre, win, qnw, wq, kvnw, wk, wv, tt=256):
    T = x.shape[0]

    def body(x_ref, dh1_ref, cq_ref, ckv_ref, dq_ref, dk_ref, dv_ref, dhq_ref, dhf_ref, dhi_ref, dhg_ref, rc_ref, rs_ref,
             wpre_ref, win_ref, qnw_ref, wq_ref, kvnw_ref, wk_ref, wv_ref,
             dx_ref, dwin_ref, dwq_ref, dwk_ref, dwv_ref, dwpre_ref, dqnw_ref, dkvnw_ref):
        @pl.when(pl.program_id(0) == 0)
        def _():
            for r in (dwin_ref, dwq_ref, dwk_ref, dwv_ref, dwpre_ref, dqnw_ref, dkvnw_ref):
                r[...] = jnp.zeros_like(r)

        c, sa, sb = _rope_tables(rc_ref[...], rs_ref[...])
        lane = lax.broadcasted_iota(jnp.int32, (tt, HEAD_PAD), 1)
        dk_all = dk_ref[...]
        dq_lin = []
        dkr = jnp.zeros((tt, HEAD_PAD), F32)
        for h in range(MLA_HEADS):
            sl = slice(HEAD_PAD * h, HEAD_PAD * (h + 1))
            dq_lin.append(_rope_bwd(dq_ref[:, sl], c, sa, sb))
            dkr = dkr + dk_all[:, sl]
        dq_lin = jnp.concatenate(dq_lin, axis=-1)
        dkr = jnp.where((lane >= MLA_NOPE) & (lane < MLA_QK), _rope_bwd(dkr, c, sa, sb), 0.0)
        qnw = qnw_ref[...]
        qn, cqn, rq = _rms_fwd(cq_ref[...], qnw)
        dwq_ref[...] += _mm_tn(qn, dq_lin)
        dcq, dqnw = _rms_bwd(_mm_nt(dq_lin, wq_ref[...]), cqn, rq, qnw)
        dqnw_ref[...] += dqnw
        kvnw = kvnw_ref[...]
        kvn, ckvn, rkv = _rms_fwd(ckv_ref[...], kvnw)
        dv_ = dv_ref[...]
        dwk_ref[...] += _mm_tn(kvn, dk_all)
        dwv_ref[...] += _mm_tn(kvn, dv_)
        dckv, dkvnw = _rms_bwd(_mm_nt(dk_all, wk_ref[...]) + _mm_nt(dv_, wv_ref[...]), ckvn, rkv, kvnw)
        dkvnw_ref[...] += dkvnw
        dxp = jnp.concatenate([dcq, dckv, dkr, dhq_ref[...], dhf_ref[...], dhi_ref[...], dhg_ref[...]], axis=-1)
        wpre = wpre_ref[...]
        u, xn, rx = _rms_fwd(x_ref[...], wpre)
        dwin_ref[...] += _mm_tn(u, dxp)
        dx_u, dwpre = _rms_bwd(_mm_nt(dxp, win_ref[...]), xn, rx, wpre)
        dwpre_ref[...] += dwpre
        dx_ref[...] = dh1_ref[...] + dx_u

    row = lambda w: pl.BlockSpec((tt, w), lambda i: (i, 0))
    full = lambda a: pl.BlockSpec(a.shape, lambda i: (0,) * a.ndim)
    sds = jax.ShapeDtypeStruct
    qk_w = MLA_HEADS * HEAD_PAD
    return pl.pallas_call(
        body, name="in_bwd", grid=(T // tt,),
        in_specs=[row(D_MODEL), row(D_MODEL), row(Q_RANK), row(KV_RANK), row(qk_w), row(qk_w), row(MLA_WIDTH),
                  row(HGRN_WIDTH), row(HGRN_WIDTH), row(HGRN_WIDTH), row(HGRN_WIDTH), row(HEAD_PAD), row(HEAD_PAD),
                  full(w_pre), full(win), full(qnw), full(wq), full(kvnw), full(wk), full(wv)],
        out_specs=[row(D_MODEL), full(win), full(wq), full(wk), full(wv), full(w_pre), full(qnw), full(kvnw)],
        out_shape=[sds((T, D_MODEL), F32), sds(win.shape, F32), sds(wq.shape, F32), sds(wk.shape, F32),
                   sds(wv.shape, F32), sds(w_pre.shape, F32), sds(qnw.shape, F32), sds(kvnw.shape, F32)],
        compiler_params=_params(("arbitrary",)),
    )(x, dh1, cq, ckv, dq, dk, dv, dhq, dhf, dhi, dhg, rc, rs, w_pre, win, qnw, wq, kvnw, wk, wv)


def _arrange_weights(win_full, wuq_full, wukv):
    dt = win_full.dtype
    z = lambda n: jnp.zeros((D_MODEL, n), dt)
    s2 = Q_RANK + KV_RANK
    half = MLA_ROPE // 2
    win_arr = jnp.concatenate([win_full[:, :s2], z(MLA_NOPE), win_full[:, s2:s2 + MLA_ROPE],
                               z(HEAD_PAD - MLA_QK), win_full[:, s2 + MLA_ROPE:]], axis=1)
    del half
    wq_arr = jnp.pad(wuq_full, ((0, 0), (0, 0), (0, HEAD_PAD - MLA_QK))).reshape(Q_RANK, MLA_HEADS * HEAD_PAD)
    wk_arr = jnp.pad(wukv[:, :, :MLA_NOPE], ((0, 0), (0, 0), (0, HEAD_PAD - MLA_NOPE))).reshape(
        KV_RANK, MLA_HEADS * HEAD_PAD)
    wv_arr = wukv[:, :, MLA_NOPE:].reshape(KV_RANK, MLA_WIDTH)
    return win_arr, wq_arr, wk_arr, wv_arr


def _unarrange_grads(dwin_arr, dwq_arr, dwk_arr, dwv_arr):
    s2 = Q_RANK + KV_RANK
    dwin = jnp.concatenate([dwin_arr[:, :s2], dwin_arr[:, s2 + MLA_NOPE:s2 + MLA_QK], dwin_arr[:, s2 + HEAD_PAD:]],
                           axis=1)
    dwuq = dwq_arr.reshape(Q_RANK, MLA_HEADS, HEAD_PAD)[:, :, :MLA_QK]
    dwukv = jnp.concatenate([dwk_arr.reshape(KV_RANK, MLA_HEADS, HEAD_PAD)[:, :, :MLA_NOPE],
                             dwv_arr.reshape(KV_RANK, MLA_HEADS, MLA_V)], axis=-1)
    return dwin, dwuq, dwukv


def _rope_inv_freq():
    inv = 1.0 / (ROPE_THETA ** (jnp.arange(0, MLA_ROPE, 2, dtype=F32) / MLA_ROPE))
    z = lambda n: jnp.zeros((n,), F32)
    return jnp.concatenate([z(MLA_NOPE), inv, inv, z(HEAD_PAD - MLA_QK)]).reshape(1, HEAD_PAD)


def _local_step(x, pos, tgt, small, win_arr, wq_arr, wk_arr, wv_arr, wout, wg, wu, wd):
    invf = _rope_inv_freq()
    cq, ckv, xph, qb, kb, vb, rc, rs = _in_fwd(x, pos, invf, small["attn_pre_norm"], win_arr, small["mla_q_norm"],
                                               wq_arr, small["mla_kv_norm"], wk_arr, wv_arr)
    o_raw, lse = _attn_fwd(qb, kb, vb)
    oh_raw, states = _hgrn_fwd(xph, small["hgrn_lb_logits"])
    h1, y1, zb, mixb, g, up, dy2b, dh2, loss_acc, d_fpost = _ffn_fwd(
        x, o_raw, oh_raw, xph, tgt, wout, small["mla_out_norm"], small["hgrn_out_norm"], small["attn_post_norm"],
        small["ffn_pre_norm"], small["ffn_post_norm"], wg, wu, wd)
    dwg, dwu, dwd, dzp = _ffn_bwd(zb, g, up, dy2b, wg, wu, wd)
    dh1, dwout, d_o, d_oh, dhg, dvec, d_fpre, d_post, d_mla, d_hg = _mid_bwd(
        dzp, dh2, h1, y1, mixb, o_raw, oh_raw, xph, wout, small["ffn_pre_norm"], small["attn_post_norm"],
        small["mla_out_norm"], small["hgrn_out_norm"])
    dq, dk, dv = _attn_bwd(qb, kb, vb, d_o, lse, dvec)
    dhq, dhf, dhi, d_lbl = _hgrn_bwd(xph, small["hgrn_lb_logits"], states, d_oh)
    dx, dwin_arr, dwq_arr, dwk_arr, dwv_arr, d_pre, d_qn, d_kvn = _in_bwd(
        x, dh1, cq, ckv, dq, dk, dv, dhq, dhf, dhi, dhg, rc, rs, small["attn_pre_norm"], win_arr,
        small["mla_q_norm"], wq_arr, small["mla_kv_norm"], wk_arr, wv_arr)
    dwin, dwuq, dwukv = _unarrange_grads(dwin_arr, dwq_arr, dwk_arr, dwv_arr)
    loss = 0.5 * jnp.sum(loss_acc) * (1.0 / D_MODEL)
    grads = dict(attn_pre_norm=d_pre, w_in=dwin, mla_q_norm=d_qn, mla_w_uq=dwuq, mla_kv_norm=d_kvn, mla_w_ukv=dwukv,
                 mla_out_norm=d_mla, hgrn_lb_logits=d_lbl, hgrn_out_norm=d_hg, w_out=dwout, attn_post_norm=d_post,
                 ffn_pre_norm=d_fpre, w_gate=dwg, w_up=dwu, w_down=dwd, ffn_post_norm=d_fpost)
    return loss, dx, grads


def _place():
    x, y, c = lax.axis_index("x"), lax.axis_index("y"), lax.axis_index("c")
    others = [(1 - x, y), (x, 1 - y), (1 - x, 1 - y)]
    return x, y, c, 2 * x + y, (x, y, 1 - c), others


def _half(ref, c, rows):
    return ref.at[pl.ds(pl.multiple_of(c * rows, 8), rows)]


def _gather_chips(arrs, name):
    n = len(arrs)
    halves = [a.shape[0] // 2 for a in arrs]

    def body(*refs):
        ins, outs = refs[:n], refs[n:2 * n]
        send, recv = refs[2 * n:]
        x, y, c, me, sib, others = _place()

        def rcopy(k, src, dst, to):
            return pltpu.make_async_remote_copy(src_ref=src, dst_ref=dst, send_sem=send.at[k], recv_sem=recv.at[k],
                                                device_id=to, device_id_type=MESH)

        first = []
        for j, (px, py) in enumerate(others):
            for a in range(n):
                first.append(rcopy(j * n + a, _half(ins[a], c, halves[a]), _half(outs[a].at[me], c, halves[a]),
                                   (px, py, c)))
        for cp in first:
            cp.start()
        passed = []
        for j, (px, py) in enumerate(others):
            chip = 2 * px + py
            for a in range(n):
                part = _half(outs[a].at[chip], c, halves[a])
                rcopy(j * n + a, part, part, (px, py, c)).wait_recv()
                fw = rcopy(3 * n + j * n + a, part, part, sib)
                fw.start()
                passed.append(fw)
        for j, (px, py) in enumerate(others):
            chip = 2 * px + py
            for a in range(n):
                part = _half(outs[a].at[chip], 1 - c, halves[a])
                rcopy(3 * n + j * n + a, part, part, sib).wait_recv()
        for cp in first + passed:
            cp.wait_send()

    return pl.pallas_call(
        body, name=name,
        in_specs=[ANY] * n, out_specs=[ANY] * n,
        out_shape=[jax.ShapeDtypeStruct((N_CHIPS,) + a.shape, a.dtype) for a in arrs],
        scratch_shapes=[pltpu.SemaphoreType.DMA((6 * n,)), pltpu.SemaphoreType.DMA((6 * n,))],
    )(*arrs)


GRAD_BLOCKS = 2


def _pair_swap(gs, sm):
    n = len(gs)

    def body(*refs):
        g_refs, sm_ref = refs[:n], refs[n]
        r_refs, ssib_ref = refs[n + 1:2 * n + 1], refs[2 * n + 1]
        send, recv = refs[2 * n + 2:]
        x, y, c, me, sib, others = _place()
        copies = []
        for a in range(n):
            h = gs[a].shape[1] // 2
            copies.append(pltpu.make_async_remote_copy(
                src_ref=g_refs[a].at[:, pl.ds(pl.multiple_of((1 - c) * h, 8), h)], dst_ref=r_refs[a],
                send_sem=send.at[a], recv_sem=recv.at[a], device_id=sib, device_id_type=MESH))
        copies.append(pltpu.make_async_remote_copy(src_ref=sm_ref, dst_ref=ssib_ref, send_sem=send.at[n],
                                                   recv_sem=recv.at[n], device_id=sib, device_id_type=MESH))
        for cp in copies:
            cp.start()
        for cp in copies:
            cp.wait()

    return pl.pallas_call(
        body, name="pair_swap", in_specs=[ANY] * (n + 1), out_specs=[ANY] * (n + 1),
        out_shape=[jax.ShapeDtypeStruct((N_CHIPS, g.shape[1] // 2, g.shape[2]), g.dtype) for g in gs]
        + [jax.ShapeDtypeStruct(sm.shape, sm.dtype)],
        scratch_shapes=[pltpu.SemaphoreType.DMA((n + 1,)), pltpu.SemaphoreType.DMA((n + 1,))],
    )(*gs, sm)


def _pair_sum(place, gs, rs, sm, ssib):
    n = len(gs)
    nb = GRAD_BLOCKS

    def body(place_ref, *refs):
        g_refs, r_refs = refs[:n], refs[n:2 * n]
        sm_ref, ss_ref = refs[2 * n], refs[2 * n + 1]
        p_refs, pair_ref = refs[2 * n + 2:3 * n + 2], refs[3 * n + 2]
        for a in range(n):
            p_refs[a][0] = (g_refs[a][0] + r_refs[a][0]).astype(p_refs[a].dtype)

        @pl.when((pl.program_id(0) == 0) & (pl.program_id(1) == 0))
        def _():
            pair_ref[...] = sm_ref[...] + ss_ref[...]

    in_specs, out_specs, out_shape = [], [], []
    for g in gs:
        blk = (1, g.shape[1] // 2 // nb, g.shape[2])
        in_specs.append(pl.BlockSpec(blk, lambda i, k, p: (k, p[0] * nb + i, 0)))
    for g in gs:
        blk = (1, g.shape[1] // 2 // nb, g.shape[2])
        in_specs.append(pl.BlockSpec(blk, lambda i, k, p: (k, i, 0)))
        out_specs.append(pl.BlockSpec(blk, lambda i, k, p: (k, i, 0)))
        out_shape.append(jax.ShapeDtypeStruct((N_CHIPS, g.shape[1] // 2, g.shape[2]), BF16))
    sm_spec = pl.BlockSpec(sm.shape, lambda i, k, p: (0, 0))
    return pl.pallas_call(
        body, name="pair_sum",
        grid_spec=pltpu.PrefetchScalarGridSpec(num_scalar_prefetch=1, grid=(nb, N_CHIPS),
                                               in_specs=in_specs + [sm_spec, sm_spec],
                                               out_specs=out_specs + [sm_spec]),
        out_shape=out_shape + [jax.ShapeDtypeStruct(sm.shape, F32)],
        compiler_params=_params(("arbitrary", "arbitrary")),
    )(place, *gs, *rs, sm, ssib)


def _chip_swap(ps, pair):
    n = len(ps)
    hs = SMALL_ROWS // 2

    def body(*refs):
        p_refs, pair_ref = refs[:n], refs[n]
        ri_refs, sm4_ref = refs[n + 1:2 * n + 1], refs[2 * n + 1]
        send, recv, lsem = refs[2 * n + 2:]
        x, y, c, me, sib, others = _place()
        local = pltpu.make_async_copy(pair_ref, sm4_ref.at[me], lsem.at[0])
        local.start()
        copies = []
        for j, (px, py) in enumerate(others):
            chip = 2 * px + py
            for a in range(n):
                copies.append(pltpu.make_async_remote_copy(
                    src_ref=p_refs[a].at[chip], dst_ref=ri_refs[a].at[j], send_sem=send.at[j * (n + 1) + a],
                    recv_sem=recv.at[j * (n + 1) + a], device_id=(px, py, c), device_id_type=MESH))
            copies.append(pltpu.make_async_remote_copy(
                src_ref=_half(pair_ref, c, hs), dst_ref=_half(sm4_ref.at[me], c, hs), send_sem=send.at[j * (n + 1) + n],
                recv_sem=recv.at[j * (n + 1) + n], device_id=(px, py, c), device_id_type=MESH))
        for cp in copies:
            cp.start()
        for j, (px, py) in enumerate(others):
            chip = 2 * px + py
            for a in range(n):
                pltpu.make_async_remote_copy(
                    src_ref=p_refs[a].at[chip], dst_ref=ri_refs[a].at[j], send_sem=send.at[j * (n + 1) + a],
                    recv_sem=recv.at[j * (n + 1) + a], device_id=(px, py, c), device_id_type=MESH).wait_recv()
            part = _half(sm4_ref.at[chip], c, hs)
            pltpu.make_async_remote_copy(src_ref=part, dst_ref=part, send_sem=send.at[j * (n + 1) + n],
                                         recv_sem=recv.at[j * (n + 1) + n], device_id=(px, py, c),
                                         device_id_type=MESH).wait_recv()
        for cp in copies:
            cp.wait_send()
        local.wait()

    k = 3 * (n + 1)
    return pl.pallas_call(
        body, name="chip_swap", in_specs=[ANY] * (n + 1), out_specs=[ANY] * (n + 1),
        out_shape=[jax.ShapeDtypeStruct((3,) + p.shape[1:], p.dtype) for p in ps]
        + [jax.ShapeDtypeStruct((N_CHIPS,) + pair.shape, pair.dtype)],
        scratch_shapes=[pltpu.SemaphoreType.DMA((k,)), pltpu.SemaphoreType.DMA((k,)), pltpu.SemaphoreType.DMA((1,))],
    )(*ps, pair)


def _chip_sum(place, gs, rs, ris):
    n = len(gs)
    nb = GRAD_BLOCKS

    def body(place_ref, *refs):
        g_refs, r_refs, ri_refs, o_refs = refs[:n], refs[n:2 * n], refs[2 * n:3 * n], refs[3 * n:]
        for a in range(n):
            ri = ri_refs[a]
            o_refs[a][...] = (g_refs[a][0] + r_refs[a][0]) + ri[0].astype(F32) + ri[1].astype(F32) + ri[2].astype(F32)

    in_specs, out_specs, out_shape = [], [], []
    for g in gs:
        blk = (1, g.shape[1] // 2 // nb, g.shape[2])
        in_specs.append(pl.BlockSpec(blk, lambda i, p: (p[1], p[0] * nb + i, 0)))
    for g in gs:
        blk = (1, g.shape[1] // 2 // nb, g.shape[2])
        in_specs.append(pl.BlockSpec(blk, lambda i, p: (p[1], i, 0)))
    for g in gs:
        rb = g.shape[1] // 2 // nb
        in_specs.append(pl.BlockSpec((3, rb, g.shape[2]), lambda i, p: (0, i, 0)))
        out_specs.append(pl.BlockSpec((rb, g.shape[2]), lambda i, p: (p[0] * nb + i, 0)))
        out_shape.append(jax.ShapeDtypeStruct(g.shape[1:], F32))
    return pl.pallas_call(
        body, name="chip_sum",
        grid_spec=pltpu.PrefetchScalarGridSpec(num_scalar_prefetch=1, grid=(nb,), in_specs=in_specs, out_specs=out_specs),
        out_shape=out_shape,
        compiler_params=_params(("arbitrary",)),
    )(place, *gs, *rs, *ris)


def _pair_fill(gfs, sm4):
    n = len(gfs)
    hs = SMALL_ROWS // 2

    def body(*refs):
        g_refs, sm4_ref = refs[n + 1:2 * n + 1], refs[2 * n + 1]
        send, recv = refs[2 * n + 2:]
        x, y, c, me, sib, others = _place()
        copies, waits = [], []
        for a in range(n):
            h = gfs[a].shape[0] // 2
            mine, theirs = _half(g_refs[a], c, h), _half(g_refs[a], 1 - c, h)
            copies.append(pltpu.make_async_remote_copy(src_ref=mine, dst_ref=mine, send_sem=send.at[a],
                                                       recv_sem=recv.at[a], device_id=sib, device_id_type=MESH))
            waits.append(pltpu.make_async_remote_copy(src_ref=theirs, dst_ref=theirs, send_sem=send.at[a],
                                                      recv_sem=recv.at[a], device_id=sib, device_id_type=MESH))
        for j, (px, py) in enumerate(others):
            chip = 2 * px + py
            mine, theirs = _half(sm4_ref.at[chip], c, hs), _half(sm4_ref.at[chip], 1 - c, hs)
            copies.append(pltpu.make_async_remote_copy(src_ref=mine, dst_ref=mine, send_sem=send.at[n + j],
                                                       recv_sem=recv.at[n + j], device_id=sib, device_id_type=MESH))
            waits.append(pltpu.make_async_remote_copy(src_ref=theirs, dst_ref=theirs, send_sem=send.at[n + j],
                                                      recv_sem=recv.at[n + j], device_id=sib, device_id_type=MESH))
        for cp in copies:
            cp.start()
        for w in waits:
            w.wait_recv()
        for cp in copies:
            cp.wait_send()

    return pl.pallas_call(
        body, name="pair_fill", in_specs=[ANY] * (n + 1), out_specs=[ANY] * (n + 1),
        out_shape=[jax.ShapeDtypeStruct(g.shape, g.dtype) for g in gfs] + [jax.ShapeDtypeStruct(sm4.shape, sm4.dtype)],
        input_output_aliases={i: i for i in range(n + 1)},
        scratch_shapes=[pltpu.SemaphoreType.DMA((n + 3,)), pltpu.SemaphoreType.DMA((n + 3,))],
    )(*gfs, sm4)


def _adamw_math(w, g, m, v):
    m = ADAM_B1 * m + (1.0 - ADAM_B1) * g
    v = ADAM_B2 * v + (1.0 - ADAM_B2) * (g * g)
    m_hat = m / (1.0 - ADAM_B1 ** ADAM_STEP)
    v_hat = v / (1.0 - ADAM_B2 ** ADAM_STEP)
    return -ADAM_LR * (m_hat / (jnp.sqrt(v_hat) + ADAM_EPS) + ADAM_WD * w), m, v


def _adamw(w, g, m, v, rb, name):
    rows, cols = w.shape

    def body(w_ref, g_ref, m_ref, v_ref, d_ref, mo_ref, vo_ref):
        d, mo, vo = _adamw_math(w_ref[...], g_ref[...], m_ref[...], v_ref[...])
        d_ref[...] = d
        mo_ref[...] = mo
        vo_ref[...] = vo

    spec = pl.BlockSpec((rb, cols), lambda i: (i, 0))
    return pl.pallas_call(
        body, name=name, grid=(rows // rb,), in_specs=[spec] * 4, out_specs=[spec] * 3,
        out_shape=[jax.ShapeDtypeStruct(w.shape, F32)] * 3,
        compiler_params=_params(("arbitrary",)),
    )(w, g, m, v)


def _adamw_small(sm4, w, m, v):
    def body(sm4_ref, w_ref, m_ref, v_ref, g_ref, d_ref, mo_ref, vo_ref):
        g = ((sm4_ref[0] + sm4_ref[1]) + sm4_ref[2]) + sm4_ref[3]
        g_ref[...] = g
        d, mo, vo = _adamw_math(w_ref[...], g, m_ref[...], v_ref[...])
        d_ref[...] = d
        mo_ref[...] = mo
        vo_ref[...] = vo

    return pl.pallas_call(
        body, name="adamw_small", out_shape=[jax.ShapeDtypeStruct(w.shape, F32)] * 4,
        compiler_params=pltpu.CompilerParams(vmem_limit_bytes=VMEM_LIMIT),
    )(sm4, w, m, v)


SMALL_NAMES = ("attn_pre_norm", "mla_q_norm", "mla_kv_norm", "mla_w_ukv", "mla_out_norm", "hgrn_lb_logits",
               "hgrn_out_norm", "attn_post_norm", "ffn_pre_norm", "ffn_post_norm")
BIG_NAMES = ("w_in", "mla_w_uq", "w_out", "w_gate", "w_up", "w_down")
WEIGHT_NAMES = ("attn_pre_norm", "w_in", "mla_q_norm", "mla_w_uq", "mla_kv_norm", "mla_w_ukv", "mla_out_norm",
                "hgrn_lb_logits", "hgrn_out_norm", "w_out", "attn_post_norm", "ffn_pre_norm", "w_gate", "w_up", "w_down",
                "ffn_post_norm")


UQ_COMM_SHAPE = (192, 384)


def _pack_small(vals, extra=None):
    parts = [vals[n].reshape(-1) for n in SMALL_NAMES]
    if extra is not None:
        parts.append(extra.reshape(1))
    flat = jnp.concatenate(parts)
    return jnp.pad(flat, (0, SMALL_ROWS * D_MODEL - flat.shape[0])).reshape(SMALL_ROWS, D_MODEL)


def _unpack_small(buf, shapes):
    flat = buf.reshape(-1)
    out, off = {}, 0
    for n, size in zip(SMALL_NAMES, SMALL_SIZES):
        out[n] = flat[off:off + size].reshape(shapes[n])
        off += size
    return out


def kernel(x, positions, attn_pre_norm, w_in, mla_q_norm, mla_w_uq, mla_kv_norm, mla_w_ukv, mla_out_norm, hgrn_lb_logits, hgrn_out_norm, w_out, attn_post_norm, ffn_pre_norm, w_gate, w_up, w_down, ffn_post_norm, loss_target, m_attn_pre_norm, m_w_in, m_mla_q_norm, m_mla_w_uq, m_mla_kv_norm, m_mla_w_ukv, m_mla_out_norm, m_hgrn_lb_logits, m_hgrn_out_norm, m_w_out, m_attn_post_norm, m_ffn_pre_norm, m_w_gate, m_w_up, m_w_down, m_ffn_post_norm, v_attn_pre_norm, v_w_in, v_mla_q_norm, v_mla_w_uq, v_mla_kv_norm, v_mla_w_ukv, v_mla_out_norm, v_hgrn_lb_logits, v_hgrn_out_norm, v_w_out, v_attn_post_norm, v_ffn_pre_norm, v_w_gate, v_w_up, v_w_down, v_ffn_post_norm):
    args = locals()
    W = {n: args[n] for n in WEIGHT_NAMES}
    M = {n: args["m_" + n] for n in WEIGHT_NAMES}
    V = {n: args["v_" + n] for n in WEIGHT_NAMES}
    T = x.shape[1]
    cx, cy, cc = lax.axis_index("x"), lax.axis_index("y"), lax.axis_index("c")

    shard2d = {"w_in": (D_MODEL, D_IN // N_CHIPS), "mla_w_uq": (Q_RANK // N_CHIPS, MLA_HEADS * MLA_QK),
               "w_out": (D_MODEL // N_CHIPS, D_MODEL), "w_gate": (D_MODEL, FF_SHARD), "w_up": (D_MODEL, FF_SHARD),
               "w_down": (FF_SHARD, D_MODEL)}
    me = 2 * cx + cy
    local_b = [W[n].reshape(shard2d[n]).astype(BF16) for n in BIG_NAMES]
    stacks = _gather_chips(local_b, "gather_weights")
    win4, wuq4, wout4, wg4, wu4, wd4 = [lax.dynamic_update_slice(s, l[None], (me, 0, 0))
                                        for s, l in zip(stacks, local_b)]
    win_full = win4.transpose(1, 0, 2).reshape(D_MODEL, D_IN)
    wuq_full = wuq4.reshape(Q_RANK, MLA_HEADS, MLA_QK)
    win_arr, wq_arr, wk_arr, wv_arr = _arrange_weights(win_full, wuq_full, mla_w_ukv[0].astype(BF16))
    small = {n: W[n][0] if n == "mla_w_ukv" else W[n].reshape(-1, W[n].shape[-1]) for n in SMALL_NAMES}

    loss_local, dx, grads = _local_step(x[0], positions.reshape(T, 1), loss_target[0], small, win_arr, wq_arr, wk_arr,
                                        wv_arr, wout4.reshape(D_MODEL, D_MODEL), wg4, wu4, wd4)

    dwin4 = grads["w_in"].reshape(D_MODEL, N_CHIPS, D_IN // N_CHIPS).transpose(1, 0, 2)
    gs = [dwin4, grads["mla_w_uq"].reshape((N_CHIPS,) + UQ_COMM_SHAPE), grads["w_out"].reshape((N_CHIPS,) + shard2d["w_out"]),
          grads["w_gate"], grads["w_up"], grads["w_down"]]
    sm = _pack_small(grads, loss_local)
    *rs, ssib = _pair_swap(gs, sm)
    place = jnp.stack([cc, me]).astype(jnp.int32)
    *ps, pair = _pair_sum(place, gs, rs, sm, ssib)
    *ris, sm4 = _chip_swap(ps, pair)
    gfs = _chip_sum(place, gs, rs, ris)
    *gfin, smf = _pair_fill(gfs, sm4)

    row_blocks = {"w_in": 256, "mla_w_uq": 96, "w_out": 256, "w_gate": 256, "w_up": 256, "w_down": 352}
    G, DW, NM, NV = {}, {}, {}, {}
    for k, n in enumerate(BIG_NAMES):
        g2 = gfin[k].reshape(shard2d[n])
        d, mo, vo = _adamw(W[n].reshape(shard2d[n]), g2, M[n].reshape(shard2d[n]), V[n].reshape(shard2d[n]),
                           row_blocks[n], "adamw_" + n)
        G[n], DW[n], NM[n], NV[n] = (t.reshape(W[n].shape) for t in (g2, d, mo, vo))
    gs_buf, ds, ms, vs = _adamw_small(smf, _pack_small(W), _pack_small(M), _pack_small(V))
    shapes = {n: W[n].shape for n in SMALL_NAMES}
    for dst, buf in ((G, gs_buf), (DW, ds), (NM, ms), (NV, vs)):
        dst.update(_unpack_small(buf, shapes))

    loss = gs_buf.reshape(-1)[sum(SMALL_SIZES)]
    return (loss, dx[None], *[G[n] for n in WEIGHT_NAMES], *[DW[n] for n in WEIGHT_NAMES],
            *[NM[n] for n in WEIGHT_NAMES], *[NV[n] for n in WEIGHT_NAMES])
```

```python
import jax
import jax.numpy as jnp
from jax import lax
from jax.experimental import pallas as pl
from jax.experimental.pallas import tpu as pltpu

F32 = jnp.float32
BF16 = jnp.bfloat16
MXU_DTYPE = BF16

D_MODEL = 1024
MLA_HEADS = 8
MLA_NOPE = 64
MLA_ROPE = 32
MLA_V = 64
MLA_QK = MLA_NOPE + MLA_ROPE
Q_RANK = 384
KV_RANK = 128
MLA_WIDTH = MLA_HEADS * MLA_V
HEAD_PAD = 128
HGRN_HEADS = 4
HGRN_DIM = 128
HGRN_WIDTH = HGRN_HEADS * HGRN_DIM
CHUNK = 64
SUB = 16
D_IN = Q_RANK + KV_RANK + MLA_ROPE + 4 * HGRN_WIDTH
D_IN_ARR = Q_RANK + KV_RANK + HEAD_PAD + 4 * HGRN_WIDTH
D_FF = 2816
N_CHIPS = 4
FF_SHARD = D_FF // N_CHIPS
EPS = 1e-6
ROPE_THETA = 10000.0
ATTN_SCALE = MLA_QK ** -0.5
NEG_BIG = -1e30

ADAM_LR = 0.001
ADAM_B1 = 0.9
ADAM_B2 = 0.999
ADAM_EPS = 1e-08
ADAM_WD = 0.01
ADAM_STEP = 10

VMEM_LIMIT = 56 * 1024 * 1024

SMALL_ROWS = 144
SMALL_SIZES = (1024, 384, 128, 131072, 512, 1024, 512, 1024, 1024, 1024)

MESH = pl.DeviceIdType.MESH
ANY = pl.BlockSpec(memory_space=pl.ANY)


def _dot(a, b, dims, exact):
    if exact:
        return lax.dot_general(a.astype(F32), b.astype(F32), (dims, ((), ())), precision=lax.Precision.HIGH,
                               preferred_element_type=F32)
    return lax.dot_general(a.astype(MXU_DTYPE), b.astype(MXU_DTYPE), (dims, ((), ())), preferred_element_type=F32)


def _mm(a, b, exact=False):
    return _dot(a, b, ((1,), (0,)), exact)


def _mm_nt(a, b, exact=False):
    return _dot(a, b, ((1,), (1,)), exact)


def _mm_tn(a, b, exact=False):
    return _dot(a, b, ((0,), (0,)), exact)


def _rms_fwd(x, w):
    r = lax.rsqrt(jnp.mean(x * x, axis=-1, keepdims=True) + EPS)
    xn = x * r
    return xn * w, xn, r


def _rms_bwd(dy, xn, r, w):
    dxn = dy * w
    dx = r * (dxn - xn * jnp.mean(dxn * xn, axis=-1, keepdims=True))
    dw = jnp.sum(dy * xn, axis=0, keepdims=True)
    return dx, dw


def _group_sums(v, gs):
    t, n = v.shape
    lane = lax.broadcasted_iota(jnp.int32, (t, 128), 1)
    out = []
    for p in range(n // 128):
        vb = v[:, 128 * p:128 * (p + 1)]
        if gs == 128:
            out.append(jnp.sum(vb, axis=-1, keepdims=True))
        else:
            out.append(jnp.sum(jnp.where(lane < 64, vb, 0.0), axis=-1, keepdims=True))
            out.append(jnp.sum(jnp.where(lane >= 64, vb, 0.0), axis=-1, keepdims=True))
    return out


def _group_bcast(sums, gs, t):
    lane = lax.broadcasted_iota(jnp.int32, (t, 128), 1)
    if gs == 128:
        return jnp.concatenate([jnp.broadcast_to(s, (t, 128)) for s in sums], axis=-1)
    return jnp.concatenate([jnp.where(lane < 64, sums[2 * p], sums[2 * p + 1]) for p in range(len(sums) // 2)],
                           axis=-1)


def _grms_fwd(x, w, gs):
    t = x.shape[0]
    r = lax.rsqrt(_group_bcast(_group_sums(x * x, gs), gs, t) * (1.0 / gs) + EPS)
    xn = x * r
    return xn * w, xn, r


def _grms_bwd(dy, xn, r, w, gs):
    t = dy.shape[0]
    dxn = dy * w
    dx = r * (dxn - xn * (_group_bcast(_group_sums(dxn * xn, gs), gs, t) * (1.0 / gs)))
    dw = jnp.sum(dy * xn, axis=0, keepdims=True)
    return dx, dw


def _rope_tables(c_tab, s_tab):
    lane = lax.broadcasted_iota(jnp.int32, c_tab.shape, 1)
    first = (lane >= MLA_NOPE) & (lane < MLA_NOPE + MLA_ROPE // 2)
    second = (lane >= MLA_NOPE + MLA_ROPE // 2) & (lane < MLA_QK)
    return c_tab, jnp.where(first, -s_tab, 0.0), jnp.where(second, s_tab, 0.0)


def _rope(v, c, sa, sb):
    return v * c + pltpu.roll(v, HEAD_PAD - MLA_ROPE // 2, 1) * sa + pltpu.roll(v, MLA_ROPE // 2, 1) * sb


def _rope_bwd(d, c, sa, sb):
    return d * c - pltpu.roll(d, HEAD_PAD - MLA_ROPE // 2, 1) * sa - pltpu.roll(d, MLA_ROPE // 2, 1) * sb


def _params(sem, vmem=VMEM_LIMIT):
    return pltpu.CompilerParams(dimension_semantics=sem, vmem_limit_bytes=vmem)


def _in_fwd(x, pos, invf, w_pre, win, qnw, wq, kvnw, wk, wv, tt=256):
    T = x.shape[0]

    def body(x_ref, pos_ref, invf_ref, wpre_ref, win_ref, qnw_ref, wq_ref, kvnw_ref, wk_ref, wv_ref,
             cq_ref, ckv_ref, xph_ref, q_ref, k_ref, v_ref, kt_ref, vt_ref, rc_ref, rs_ref):
        u, _, _ = _rms_fwd(x_ref[...], wpre_ref[...])
        xp = _mm(u, win_ref[...])
        cq = xp[:, :Q_RANK]
        ckv = xp[:, Q_RANK:Q_RANK + KV_RANK]
        kr = xp[:, Q_RANK + KV_RANK:Q_RANK + KV_RANK + HEAD_PAD]
        cq_ref[...] = cq
        ckv_ref[...] = ckv
        xph_ref[...] = xp[:, Q_RANK + KV_RANK + HEAD_PAD:]
        ang = pos_ref[...].astype(F32) * invf_ref[...]
        c_tab = jnp.cos(ang)
        s_tab = jnp.sin(ang)
        rc_ref[...] = c_tab
        rs_ref[...] = s_tab
        c, sa, sb = _rope_tables(c_tab, s_tab)
        qn, _, _ = _rms_fwd(cq, qnw_ref[...])
        q = _mm(qn, wq_ref[...])
        kvn, _, _ = _rms_fwd(ckv, kvnw_ref[...])
        kn = _mm(kvn, wk_ref[...])
        v = _mm(kvn, wv_ref[...])
        v_ref[...] = v.astype(v_ref.dtype)
        vt_ref[...] = v.T.astype(vt_ref.dtype)
        krr = _rope(kr, c, sa, sb)
        for h in range(MLA_HEADS):
            sl = slice(HEAD_PAD * h, HEAD_PAD * (h + 1))
            q_ref[:, sl] = _rope(q[:, sl], c, sa, sb).astype(q_ref.dtype)
            kh = kn[:, sl] + krr
            k_ref[:, sl] = kh.astype(k_ref.dtype)
            kt_ref[sl, :] = kh.T.astype(kt_ref.dtype)

    row = lambda w: pl.BlockSpec((tt, w), lambda i: (i, 0))
    full = lambda a: pl.BlockSpec(a.shape, lambda i: (0,) * a.ndim)
    qk_w = MLA_HEADS * HEAD_PAD
    return pl.pallas_call(
        body, name="in_fwd", grid=(T // tt,),
        in_specs=[row(D_MODEL), row(1), full(invf), full(w_pre), full(win), full(qnw), full(wq), full(kvnw),
                  full(wk), full(wv)],
        out_specs=[row(Q_RANK), row(KV_RANK), row(4 * HGRN_WIDTH), row(qk_w), row(qk_w), row(MLA_WIDTH),
                   pl.BlockSpec((qk_w, tt), lambda i: (0, i)), pl.BlockSpec((MLA_WIDTH, tt), lambda i: (0, i)),
                   row(HEAD_PAD), row(HEAD_PAD)],
        out_shape=[jax.ShapeDtypeStruct((T, Q_RANK), F32), jax.ShapeDtypeStruct((T, KV_RANK), F32),
                   jax.ShapeDtypeStruct((T, 4 * HGRN_WIDTH), F32), jax.ShapeDtypeStruct((T, qk_w), MXU_DTYPE),
                   jax.ShapeDtypeStruct((T, qk_w), MXU_DTYPE), jax.ShapeDtypeStruct((T, MLA_WIDTH), MXU_DTYPE),
                   jax.ShapeDtypeStruct((qk_w, T), MXU_DTYPE), jax.ShapeDtypeStruct((MLA_WIDTH, T), MXU_DTYPE),
                   jax.ShapeDtypeStruct((T, HEAD_PAD), F32), jax.ShapeDtypeStruct((T, HEAD_PAD), F32)],
        compiler_params=_params(("arbitrary",)),
    )(x, pos, invf, w_pre, win, qnw, wq, kvnw, wk, wv)


def _attn_fwd_t(qb, kb, vt, tq=256, hps=8):
    T = qb.shape[0]
    nq = T // tq

    def body(q_ref, k_ref, vt_ref, o_ref, lse_ref, acc_scr):
        qi = pl.program_id(1)
        heads = [slice(HEAD_PAD * a, HEAD_PAD * (a + 1)) for a in range(hps)]
        acc_scr[...] = jnp.zeros_like(acc_scr)

        def step(j, carry, masked):
            start = pl.multiple_of(j * tq, tq)
            scores = [_mm_nt(k_ref[pl.ds(start, tq), heads[a]], q_ref[:, heads[a]]) for a in range(hps)]
            new = []
            for a in range(hps):
                m, l = carry[a]
                s = scores[a] * ATTN_SCALE
                if masked:
                    kk = lax.broadcasted_iota(jnp.int32, (tq, tq), 0)
                    qq = lax.broadcasted_iota(jnp.int32, (tq, tq), 1)
                    s = jnp.where(kk <= qq, s, NEG_BIG)
                m_new = jnp.maximum(m, jnp.max(s, axis=0, keepdims=True))
                alpha = jnp.exp(m - m_new)
                p = jnp.exp(s - m_new)
                l = l * alpha + jnp.sum(p, axis=0, keepdims=True)
                vtj = vt_ref[2 * MLA_V * (a // 2):2 * MLA_V * (a // 2 + 1), pl.ds(start, tq)]
                acc_scr[a] = acc_scr[a] * alpha + _mm(vtj, p)
                new.append((m_new, l))
            return tuple(new)

        init = tuple((jnp.full((1, tq), NEG_BIG, F32), jnp.zeros((1, tq), F32)) for _ in range(hps))
        carry = lax.fori_loop(0, qi, lambda j, c: step(j, c, False), init)
        carry = step(qi, carry, True)
        row = lax.broadcasted_iota(jnp.int32, (2 * MLA_V, tq), 0)
        for pr in range(hps // 2):
            (m0, l0), (m1, l1) = carry[2 * pr], carry[2 * pr + 1]
            ot = jnp.where(row < MLA_V, acc_scr[2 * pr] / l0, acc_scr[2 * pr + 1] / l1)
            o_ref[:, 2 * MLA_V * pr:2 * MLA_V * (pr + 1)] = ot.T
            lse_ref[pr, 0:1, :] = m0 + jnp.log(l0)
            lse_ref[pr, 1:2, :] = m1 + jnp.log(l1)

    return pl.pallas_call(
        body, name="attn_fwd", grid=(MLA_HEADS // hps, nq),
        in_specs=[pl.BlockSpec((tq, hps * HEAD_PAD), lambda g, i: (i, g)),
                  pl.BlockSpec((T, hps * HEAD_PAD), lambda g, i: (0, g)),
                  pl.BlockSpec((hps * MLA_V, T), lambda g, i: (g, 0))],
        out_specs=[pl.BlockSpec((tq, hps * MLA_V), lambda g, i: (i, g)),
                   pl.BlockSpec((hps // 2, 2, tq), lambda g, i: (g, 0, i))],
        out_shape=[jax.ShapeDtypeStruct((T, MLA_WIDTH), F32), jax.ShapeDtypeStruct((MLA_HEADS // 2, 2, T), F32)],
        scratch_shapes=[pltpu.VMEM((hps, 2 * MLA_V, tq), F32)],
        compiler_params=_params(("arbitrary", "arbitrary")),
    )(qb, kb, vt)


def _attn_bwd_t(qb, kb, kt, vb, dob, lse, dvec, tq=256, hps=4):
    T = qb.shape[0]
    nq = T // tq

    def body(q_ref, k_ref, kt_ref, v_ref, do_ref, lse_ref, d_ref, dqt_ref, dk_ref, dv_ref, va_scr, dv_scr):
        j = pl.program_id(1)

        @pl.when(j == 0)
        def _():
            dqt_ref[...] = jnp.zeros_like(dqt_ref)

        lane = lax.broadcasted_iota(jnp.int32, (tq, 2 * MLA_V), 1)
        heads = [slice(HEAD_PAD * a, HEAD_PAD * (a + 1)) for a in range(hps)]
        pairs = [slice(2 * MLA_V * p, 2 * MLA_V * (p + 1)) for p in range(hps // 2)]
        for pr in range(hps // 2):
            vpair = v_ref[:, pairs[pr]]
            va_scr[2 * pr] = jnp.where(lane < MLA_V, vpair, jnp.zeros_like(vpair))
            va_scr[2 * pr + 1] = jnp.where(lane >= MLA_V, vpair, jnp.zeros_like(vpair))
        dk_ref[...] = jnp.zeros_like(dk_ref)
        dv_scr[...] = jnp.zeros_like(dv_scr)

        def step(i, masked):
            start = pl.multiple_of(i * tq, tq)
            rows = pl.ds(start, tq)
            scores = [_mm_nt(k_ref[:, heads[a]], q_ref[rows, heads[a]]) for a in range(hps)]
            dps = [_mm_nt(va_scr[a], do_ref[rows, pairs[a // 2]]) for a in range(hps)]
            for a in range(hps):
                pr, r = a // 2, a % 2
                p = jnp.exp(scores[a] * ATTN_SCALE - lse_ref[pr, r:r + 1, rows])
                if masked:
                    kk = lax.broadcasted_iota(jnp.int32, (tq, tq), 0)
                    qq = lax.broadcasted_iota(jnp.int32, (tq, tq), 1)
                    p = jnp.where(kk <= qq, p, 0.0)
                ds = p * (dps[a] - d_ref[pr, r:r + 1, rows]) * ATTN_SCALE
                dv_scr[a] += _mm(p, do_ref[rows, pairs[pr]])
                dk_ref[:, heads[a]] += _mm(ds, q_ref[rows, heads[a]])
                dqt_ref[heads[a], rows] += _mm(kt_ref[heads[a], :], ds)

        def loop_body(i, _):
            step(i, False)
            return 0

        step(j, True)
        lax.fori_loop(j + 1, nq, loop_body, 0)
        for pr in range(hps // 2):
            dv_ref[:, pairs[pr]] = jnp.where(lane < MLA_V, dv_scr[2 * pr], dv_scr[2 * pr + 1])

    stat = pl.BlockSpec((hps // 2, 2, T), lambda g, j: (g, 0, 0))
    return pl.pallas_call(
        body, name="attn_bwd", grid=(MLA_HEADS // hps, nq),
        in_specs=[pl.BlockSpec((T, hps * HEAD_PAD), lambda g, j: (0, g)),
                  pl.BlockSpec((tq, hps * HEAD_PAD), lambda g, j: (j, g)),
                  pl.BlockSpec((hps * HEAD_PAD, tq), lambda g, j: (g, j)),
                  pl.BlockSpec((tq, hps * MLA_V), lambda g, j: (j, g)),
                  pl.BlockSpec((T, hps * MLA_V), lambda g, j: (0, g)), stat, stat],
        out_specs=[pl.BlockSpec((hps * HEAD_PAD, T), lambda g, j: (g, 0)),
                   pl.BlockSpec((tq, hps * HEAD_PAD), lambda g, j: (j, g)),
                   pl.BlockSpec((tq, hps * MLA_V), lambda g, j: (j, g))],
        out_shape=[jax.ShapeDtypeStruct((MLA_HEADS * HEAD_PAD, T), F32),
                   jax.ShapeDtypeStruct((T, MLA_HEADS * HEAD_PAD), F32),
                   jax.ShapeDtypeStruct((T, MLA_WIDTH), F32)],
        scratch_shapes=[pltpu.VMEM((hps, tq, 2 * MLA_V), vb.dtype), pltpu.VMEM((hps, tq, 2 * MLA_V), F32)],
        compiler_params=_params(("arbitrary", "arbitrary")),
    )(qb, kb, kt, vb, dob, lse, dvec)


def _cumsum_rows(x):
    n = x.shape[0]
    row = lax.broadcasted_iota(jnp.int32, x.shape, 0)
    s = 1
    while s < n:
        x = x + jnp.where(row >= s, pltpu.roll(x, s, 0), 0.0)
        s *= 2
    return x


def _rev_cumsum_rows(x):
    n = x.shape[0]
    row = lax.broadcasted_iota(jnp.int32, x.shape, 0)
    s = 1
    while s < n:
        x = x + jnp.where(row < n - s, pltpu.roll(x, n - s, 0), 0.0)
        s *= 2
    return x


def _lb_from_logits(l):
    l0, l1 = l[0:1, :], l[1:2, :]
    m = jnp.maximum(l0, l1)
    e0, e1 = jnp.exp(l0 - m), jnp.exp(l1 - m)
    return e0 / (e0 + e1)


def _hgrn_gates(hq, hf, lb):
    sig_f = jax.nn.sigmoid(hf)
    f = lb + (1.0 - lb) * sig_f
    sig_q = jax.nn.sigmoid(hq)
    return sig_f, f, jnp.log(f), 1.0 - f, sig_q, hq * sig_q


def _hgrn_intra(q, kk, b, exact=False):
    row = lax.broadcasted_iota(jnp.int32, b.shape, 0)
    qs, ks, eqs, eks, a_rows = [], [], [], [], []
    for i in range(CHUNK // SUB):
        ref = b[SUB * i:SUB * i + 1, :]
        eq = jnp.exp(b[SUB * i:SUB * (i + 1), :] - ref)
        ek = jnp.exp(jnp.where(row < SUB * (i + 1), ref - b, NEG_BIG))
        qi = q[SUB * i:SUB * (i + 1), :] * eq
        ki = kk * ek
        a_rows.append(_mm_nt(qi, ki, exact))
        qs.append(qi), ks.append(ki), eqs.append(eq), eks.append(ek)
    tt = lax.broadcasted_iota(jnp.int32, (CHUNK, CHUNK), 0)
    ss = lax.broadcasted_iota(jnp.int32, (CHUNK, CHUNK), 1)
    causal = ss <= tt
    a = jnp.where(causal, jnp.concatenate(a_rows, axis=0), 0.0)
    return a, causal, qs, ks, eqs, eks


def _hgrn_fwd(xph, lbl, tg=512):
    T = xph.shape[0]
    ng, ncg = T // tg, tg // CHUNK
    cols = [slice(HGRN_DIM * h, HGRN_DIM * (h + 1)) for h in range(HGRN_HEADS)]

    def body(lbl_ref, hq_ref, hf_ref, hi_ref, o_ref, st_ref, s_scr):
        @pl.when(pl.program_id(0) == 0)
        def _():
            s_scr[...] = jnp.zeros_like(s_scr)

        lb = _lb_from_logits(lbl_ref[...])

        def chunk(c, _):
            rows = pl.ds(pl.multiple_of(c * CHUNK, CHUNK), CHUNK)
            pre = []
            for cs in cols:
                _, _, lf, kk, _, q = _hgrn_gates(hq_ref[rows, cs], hf_ref[rows, cs], lb[:, cs])
                v = hi_ref[rows, cs]
                b = _cumsum_rows(lf)
                a = _hgrn_intra(q, kk, b)[0]
                b_last = b[CHUNK - 1:CHUNK, :]
                pre.append((q * jnp.exp(b), a, v, jnp.exp(b_last), _mm_tn(v, kk * jnp.exp(b_last - b))))
            for h, cs in enumerate(cols):
                qe, a, v, ebl, upd = pre[h]
                st = s_scr[h]
                st_ref[h, c] = st
                o_ref[rows, cs] = _mm_nt(qe, st) + _mm(a, v)
                s_scr[h] = st * ebl + upd
            return 0

        lax.fori_loop(0, ncg, chunk, 0)

    col = lambda k: pl.BlockSpec((tg, HGRN_WIDTH), lambda g: (g, k))
    return pl.pallas_call(
        body, name="hgrn_fwd", grid=(ng,),
        in_specs=[pl.BlockSpec((2, HGRN_WIDTH), lambda g: (0, 0)), col(0), col(1), col(2)],
        out_specs=[col(0), pl.BlockSpec((HGRN_HEADS, ncg, HGRN_DIM, HGRN_DIM), lambda g: (0, g, 0, 0))],
        out_shape=[jax.ShapeDtypeStruct((T, HGRN_WIDTH), F32),
                   jax.ShapeDtypeStruct((HGRN_HEADS, T // CHUNK, HGRN_DIM, HGRN_DIM), F32)],
        scratch_shapes=[pltpu.VMEM((HGRN_HEADS, HGRN_DIM, HGRN_DIM), F32)],
        compiler_params=_params(("arbitrary",)),
    )(lbl, xph, xph, xph)


def _hgrn_bwd(xph, lbl, states, d_o, tg=512):
    T = xph.shape[0]
    ng, ncg = T // tg, tg // CHUNK
    cols = [slice(HGRN_DIM * h, HGRN_DIM * (h + 1)) for h in range(HGRN_HEADS)]
    nsub = CHUNK // SUB

    def body(lbl_ref, hq_ref, hf_ref, hi_ref, st_ref, do_ref, dhq_ref, dhf_ref, dhi_ref, dlg_ref, ds_scr, dlb_scr):
        g = pl.program_id(0)

        @pl.when(g == 0)
        def _():
            ds_scr[...] = jnp.zeros_like(ds_scr)
            dlb_scr[...] = jnp.zeros_like(dlb_scr)

        lb = _lb_from_logits(lbl_ref[...])

        def chunk(ci, _):
            c = ncg - 1 - ci
            rows = pl.ds(pl.multiple_of(c * CHUNK, CHUNK), CHUNK)
            pre = []
            for h, cs in enumerate(cols):
                hq = hq_ref[rows, cs]
                sig_f, f, lf, kk, sig_q, q = _hgrn_gates(hq, hf_ref[rows, cs], lb[:, cs])
                v = hi_ref[rows, cs]
                do = do_ref[rows, cs]
                b = _cumsum_rows(lf)
                eb = jnp.exp(b)
                a, causal, qs, ks, eqs, eks = _hgrn_intra(q, kk, b, exact=True)
                b_last = b[CHUNK - 1:CHUNK, :]
                st = st_ref[h, c]
                pre.append(dict(hq=hq, sig_f=sig_f, f=f, kk=kk, sig_q=sig_q, q=q, v=v, eb=eb, qs=qs, ks=ks, eqs=eqs,
                                eks=eks, ebl=jnp.exp(b_last), el=jnp.exp(b_last - b), st=st,
                                da=jnp.where(causal, _mm_nt(do, v, True), 0.0), dq=_mm(do, st, True) * eb,
                                dv=_mm_tn(a, do, True), dsu=_mm_tn(do, q * eb, True)))
            for w in pre:
                dq_rows = []
                dk = jnp.zeros_like(w["q"])
                for i in range(nsub):
                    dai = w["da"][SUB * i:SUB * (i + 1), :]
                    dq_rows.append(_mm(dai, w["ks"][i], True) * w["eqs"][i])
                    dk = dk + _mm_tn(dai, w["qs"][i], True) * w["eks"][i]
                w["dq"] = w["dq"] + jnp.concatenate(dq_rows, axis=0)
                w["dk"] = dk
            for h, cs in enumerate(cols):
                w = pre[h]
                kk, el, ebl, dst = w["kk"], w["el"], w["ebl"], ds_scr[h]
                dk_state = _mm(w["v"], dst, True) * el
                dk = w["dk"] + dk_state
                e_last = (ebl * jnp.sum(w["st"] * dst, axis=0, keepdims=True)
                          + jnp.sum(kk * dk_state, axis=0, keepdims=True))
                dlf = _rev_cumsum_rows(w["q"] * w["dq"] - kk * dk) + e_last
                ds_scr[h] = dst * ebl + w["dsu"]
                df = dlf / w["f"] - dk
                sig_f, sig_q = w["sig_f"], w["sig_q"]
                dhf_ref[rows, cs] = df * (1.0 - lb[:, cs]) * sig_f * (1.0 - sig_f)
                dlb_scr[:, cs] += jnp.sum(df * (1.0 - sig_f), axis=0, keepdims=True)
                dhq_ref[rows, cs] = w["dq"] * sig_q * (1.0 + w["hq"] * (1.0 - sig_q))
                dhi_ref[rows, cs] = w["dv"] + _mm_nt(kk * el, dst, True)
            return 0

        lax.fori_loop(0, ncg, chunk, 0)

        @pl.when(g == ng - 1)
        def _():
            dl0 = dlb_scr[...] * lb * (1.0 - lb)
            dlg_ref[...] = jnp.concatenate([dl0, -dl0], axis=0)

    col = lambda k: pl.BlockSpec((tg, HGRN_WIDTH), lambda g: (ng - 1 - g, k))
    logits = pl.BlockSpec((2, HGRN_WIDTH), lambda g: (0, 0))
    big = jax.ShapeDtypeStruct((T, HGRN_WIDTH), F32)
    return pl.pallas_call(
        body, name="hgrn_bwd", grid=(ng,),
        in_specs=[logits, col(0), col(1), col(2),
                  pl.BlockSpec((HGRN_HEADS, ncg, HGRN_DIM, HGRN_DIM), lambda g: (0, ng - 1 - g, 0, 0)), col(0)],
        out_specs=[col(0), col(0), col(0), logits],
        out_shape=[big, big, big, jax.ShapeDtypeStruct((2, HGRN_WIDTH), F32)],
        scratch_shapes=[pltpu.VMEM((HGRN_HEADS, HGRN_DIM, HGRN_DIM), F32), pltpu.VMEM((1, HGRN_WIDTH), F32)],
        compiler_params=_params(("arbitrary",)),
    )(lbl, xph, xph, xph, states, d_o)


def _ffn_fwd(x, o_raw, oh_raw, xph, tgt, wout, w_mla, w_hg, w_post, w_fpre, w_fpost, wg, wu, wd, tt=256):
    T = x.shape[0]
    nj = N_CHIPS

    def body(x_ref, o_ref, oh_ref, hg_ref, tgt_ref, wout_ref, wmla_ref, whg_ref, wpost_ref, wfpre_ref, wfpost_ref,
             wg_ref, wu_ref, wd_ref,
             h1_ref, y1_ref, z_ref, mix_ref, g_ref, up_ref, dy2_ref, dh2_ref, loss_ref, dwf_ref,
             z_scr, y2_scr):
        i, j = pl.program_id(0), pl.program_id(1)

        @pl.when((i == 0) & (j == 0))
        def _():
            loss_ref[...] = jnp.zeros_like(loss_ref)
            dwf_ref[...] = jnp.zeros_like(dwf_ref)

        @pl.when(j == 0)
        def _():
            om, _, _ = _grms_fwd(o_ref[...], wmla_ref[...], MLA_V)
            hg = hg_ref[...]
            ohn, _, _ = _grms_fwd(oh_ref[...], whg_ref[...], HGRN_DIM)
            mix = jnp.concatenate([om, ohn * (hg * jax.nn.sigmoid(hg))], axis=-1)
            mix_ref[...] = mix.astype(mix_ref.dtype)
            y1 = _mm(mix, wout_ref[...])
            y1_ref[...] = y1
            h1 = x_ref[...] + _rms_fwd(y1, wpost_ref[...])[0]
            h1_ref[...] = h1
            z = _rms_fwd(h1, wfpre_ref[...])[0].astype(z_scr.dtype)
            z_scr[...] = z
            z_ref[...] = z
            y2_scr[...] = jnp.zeros_like(y2_scr)

        z = z_scr[...]
        g = _mm(z, wg_ref[0])
        up = _mm(z, wu_ref[0])
        g_ref[0] = g
        up_ref[0] = up
        y2_scr[...] += _mm(g * jax.nn.sigmoid(g) * up, wd_ref[0])

        @pl.when(j == nj - 1)
        def _():
            w = wfpost_ref[...]
            y2s, y2n, r2 = _rms_fwd(y2_scr[...], w)
            e = h1_ref[...] + y2s - tgt_ref[...]
            loss_ref[...] += jnp.sum(e * e, axis=0, keepdims=True)
            dh2 = e * (1.0 / D_MODEL)
            dh2_ref[...] = dh2
            dy2, dwf = _rms_bwd(dh2, y2n, r2, w)
            dy2_ref[...] = dy2.astype(dy2_ref.dtype)
            dwf_ref[...] += dwf

    row = lambda w: pl.BlockSpec((tt, w), lambda i, j: (i, 0))
    full = lambda a: pl.BlockSpec(a.shape, lambda i, j: (0,) * a.ndim)
    vec = pl.BlockSpec((1, D_MODEL), lambda i, j: (0, 0))
    return pl.pallas_call(
        body, name="ffn_fwd", grid=(T // tt, nj),
        in_specs=[row(D_MODEL), row(MLA_WIDTH), row(HGRN_WIDTH),
                  pl.BlockSpec((tt, HGRN_WIDTH), lambda i, j: (i, 3)), row(D_MODEL), full(wout), full(w_mla),
                  full(w_hg), vec, vec, vec,
                  pl.BlockSpec((1, D_MODEL, FF_SHARD), lambda i, j: (j, 0, 0)),
                  pl.BlockSpec((1, D_MODEL, FF_SHARD), lambda i, j: (j, 0, 0)),
                  pl.BlockSpec((1, FF_SHARD, D_MODEL), lambda i, j: (j, 0, 0))],
        out_specs=[row(D_MODEL), row(D_MODEL), row(D_MODEL), row(D_MODEL),
                   pl.BlockSpec((1, tt, FF_SHARD), lambda i, j: (j, i, 0)),
                   pl.BlockSpec((1, tt, FF_SHARD), lambda i, j: (j, i, 0)),
                   row(D_MODEL), row(D_MODEL), vec, vec],
        out_shape=[jax.ShapeDtypeStruct((T, D_MODEL), F32), jax.ShapeDtypeStruct((T, D_MODEL), F32),
                   jax.ShapeDtypeStruct((T, D_MODEL), MXU_DTYPE), jax.ShapeDtypeStruct((T, D_MODEL), MXU_DTYPE),
                   jax.ShapeDtypeStruct((nj, T, FF_SHARD), F32), jax.ShapeDtypeStruct((nj, T, FF_SHARD), F32),
                   jax.ShapeDtypeStruct((T, D_MODEL), MXU_DTYPE), jax.ShapeDtypeStruct((T, D_MODEL), F32),
                   jax.ShapeDtypeStruct((1, D_MODEL), F32), jax.ShapeDtypeStruct((1, D_MODEL), F32)],
        scratch_shapes=[pltpu.VMEM((tt, D_MODEL), MXU_DTYPE), pltpu.VMEM((tt, D_MODEL), F32)],
        compiler_params=_params(("arbitrary", "arbitrary")),
    )(x, o_raw, oh_raw, xph, tgt, wout, w_mla, w_hg, w_post, w_fpre, w_fpost, wg, wu, wd)


def _ffn_bwd(zb, g, up, dy2b, wg, wu, wd, tt=512):
    T = zb.shape[0]
    nj = N_CHIPS

    def body(z_ref, g_ref, up_ref, dy2_ref, wg_ref, wu_ref, wd_ref, dwg_ref, dwu_ref, dwd_ref, dz_ref):
        @pl.when(pl.program_id(1) == 0)
        def _():
            dwg_ref[...] = jnp.zeros_like(dwg_ref)
            dwu_ref[...] = jnp.zeros_like(dwu_ref)
            dwd_ref[...] = jnp.zeros_like(dwd_ref)

        z, g_, up_, dy2 = z_ref[...], g_ref[0], up_ref[0], dy2_ref[...]
        sg = jax.nn.sigmoid(g_)
        act = g_ * sg
        dff = _mm_nt(dy2, wd_ref[0])
        dwd_ref[0] += _mm_tn(act * up_, dy2)
        dg = dff * up_ * sg * (1.0 + g_ * (1.0 - sg))
        dup = dff * act
        dwg_ref[0] += _mm_tn(z, dg)
        dwu_ref[0] += _mm_tn(z, dup)
        dz_ref[0] = _mm_nt(dg, wg_ref[0]) + _mm_nt(dup, wu_ref[0])

    row = pl.BlockSpec((tt, D_MODEL), lambda j, i: (i, 0))
    act = pl.BlockSpec((1, tt, FF_SHARD), lambda j, i: (j, i, 0))
    w_in = pl.BlockSpec((1, D_MODEL, FF_SHARD), lambda j, i: (j, 0, 0))
    w_dn = pl.BlockSpec((1, FF_SHARD, D_MODEL), lambda j, i: (j, 0, 0))
    return pl.pallas_call(
        body, name="ffn_bwd", grid=(nj, T // tt),
        in_specs=[row, act, act, row, w_in, w_in, w_dn],
        out_specs=[w_in, w_in, w_dn, pl.BlockSpec((1, tt, D_MODEL), lambda j, i: (j, i, 0))],
        out_shape=[jax.ShapeDtypeStruct((nj, D_MODEL, FF_SHARD), F32), jax.ShapeDtypeStruct((nj, D_MODEL, FF_SHARD), F32),
                   jax.ShapeDtypeStruct((nj, FF_SHARD, D_MODEL), F32), jax.ShapeDtypeStruct((nj, T, D_MODEL), F32)],
        compiler_params=_params(("arbitrary", "arbitrary")),
    )(zb, g, up, dy2b, wg, wu, wd)


def _mid_bwd(dzp, dh2, h1, y1, mixb, o_raw, oh_raw, xph, wout, w_fpre, w_post, w_mla, w_hg, tt=256):
    T = dh2.shape[0]

    def body(dzp_ref, dh2_ref, h1_ref, y1_ref, mix_ref, o_ref, oh_ref, hg_ref, wout_ref, wfpre_ref, wpost_ref,
             wmla_ref, whg_ref,
             dh1_ref, dwout_ref, do_ref, doh_ref, dhg_ref, dvec_ref, dwfpre_ref, dwpost_ref, dwmla_ref, dwhg_ref):
        @pl.when(pl.program_id(0) == 0)
        def _():
            for r in (dwout_ref, dwfpre_ref, dwpost_ref, dwmla_ref, dwhg_ref):
                r[...] = jnp.zeros_like(r)

        dz = dzp_ref[0] + dzp_ref[1] + dzp_ref[2] + dzp_ref[3]
        wfpre = wfpre_ref[...]
        _, h1n, r = _rms_fwd(h1_ref[...], wfpre)
        dh1_z, dwfpre = _rms_bwd(dz, h1n, r, wfpre)
        dwfpre_ref[...] += dwfpre
        dh1 = dh2_ref[...] + dh1_z
        dh1_ref[...] = dh1
        wpost = wpost_ref[...]
        _, y1n, r1 = _rms_fwd(y1_ref[...], wpost)
        dy1, dwpost = _rms_bwd(dh1, y1n, r1, wpost)
        dwpost_ref[...] += dwpost
        dmix = _mm_nt(dy1, wout_ref[...])
        dwout_ref[...] += _mm_tn(mix_ref[...], dy1)
        wmla = wmla_ref[...]
        o = o_ref[...]
        _, on, ro = _grms_fwd(o, wmla, MLA_V)
        d_o, dwmla = _grms_bwd(dmix[:, :MLA_WIDTH], on, ro, wmla, MLA_V)
        dwmla_ref[...] += dwmla
        do_ref[...] = d_o.astype(do_ref.dtype)
        hh = lax.broadcasted_iota(jnp.int32, (MLA_HEADS, MLA_WIDTH), 0)
        ll = lax.broadcasted_iota(jnp.int32, (MLA_HEADS, MLA_WIDTH), 1)
        sel = jnp.where((ll >= hh * MLA_V) & (ll < (hh + 1) * MLA_V), 1.0, 0.0)
        dvec_ref[...] = _mm_nt(sel, d_o * o, True)
        whg = whg_ref[...]
        hg = hg_ref[...]
        sg = jax.nn.sigmoid(hg)
        _, ohn, rh = _grms_fwd(oh_ref[...], whg, HGRN_DIM)
        dmh = dmix[:, MLA_WIDTH:]
        dhg_ref[...] = dmh * ohn * whg * sg * (1.0 + hg * (1.0 - sg))
        d_oh, dwhg = _grms_bwd(dmh * (hg * sg), ohn, rh, whg, HGRN_DIM)
        dwhg_ref[...] += dwhg
        doh_ref[...] = d_oh

    row = lambda w: pl.BlockSpec((tt, w), lambda i: (i, 0))
    full = lambda a: pl.BlockSpec(a.shape, lambda i: (0,) * a.ndim)
    vec = lambda w: pl.BlockSpec((1, w), lambda i: (0, 0))
    sds = jax.ShapeDtypeStruct
    return pl.pallas_call(
        body, name="mid_bwd", grid=(T // tt,),
        in_specs=[pl.BlockSpec((N_CHIPS, tt, D_MODEL), lambda i: (0, i, 0)), row(D_MODEL), row(D_MODEL), row(D_MODEL),
                  row(D_MODEL), row(MLA_WIDTH), row(HGRN_WIDTH), pl.BlockSpec((tt, HGRN_WIDTH), lambda i: (i, 3)),
                  full(wout), vec(D_MODEL), vec(D_MODEL), vec(MLA_WIDTH), vec(HGRN_WIDTH)],
        out_specs=[row(D_MODEL), full(wout), row(MLA_WIDTH), row(HGRN_WIDTH), row(HGRN_WIDTH),
                   pl.BlockSpec((MLA_HEADS, tt), lambda i: (0, i)),
                   vec(D_MODEL), vec(D_MODEL), vec(MLA_WIDTH), vec(HGRN_WIDTH)],
        out_shape=[sds((T, D_MODEL), F32), sds(wout.shape, F32), sds((T, MLA_WIDTH), MXU_DTYPE), sds((T, HGRN_WIDTH), F32),
                   sds((T, HGRN_WIDTH), F32), sds((MLA_HEADS, T), F32),
                   sds((1, D_MODEL), F32), sds((1, D_MODEL), F32), sds((1, MLA_WIDTH), F32), sds((1, HGRN_WIDTH), F32)],
        compiler_params=_params(("arbitrary",)),
    )(dzp, dh2, h1, y1, mixb, o_raw, oh_raw, xph, wout, w_fpre, w_post, w_mla, w_hg)


def _in_bwd(x, dh1, cq, ckv, dq, dk, dv, dhq, dhf, dhi, dhg, rc, rs, w_pre, win, qnw, wq, kvnw, wk, wv, tt=256):
    T = x.shape[0]

    def body(x_ref, dh1_ref, cq_ref, ckv_ref, dq_ref, dk_ref, dv_ref, dhq_ref, dhf_ref, dhi_ref, dhg_ref, rc_ref, rs_ref,
             wpre_ref, win_ref, qnw_ref, wq_ref, kvnw_ref, wk_ref, wv_ref,
             dx_ref, dwin_ref, dwq_ref, dwk_ref, dwv_ref, dwpre_ref, dqnw_ref, dkvnw_ref):
        @pl.when(pl.program_id(0) == 0)
        def _():
            for r in (dwin_ref, dwq_ref, dwk_ref, dwv_ref, dwpre_ref, dqnw_ref, dkvnw_ref):
                r[...] = jnp.zeros_like(r)

        c, sa, sb = _rope_tables(rc_ref[...], rs_ref[...])
        lane = lax.broadcasted_iota(jnp.int32, (tt, HEAD_PAD), 1)
        dk_all = dk_ref[...]
        dq_lin = []
        dkr = jnp.zeros((tt, HEAD_PAD), F32)
        for h in range(MLA_HEADS):
            sl = slice(HEAD_PAD * h, HEAD_PAD * (h + 1))
            dq_lin.append(_rope_bwd(dq_ref[:, sl], c, sa, sb))
            dkr = dkr + dk_all[:, sl]
        dq_lin = jnp.concatenate(dq_lin, axis=-1)
        dkr = jnp.where((lane >= MLA_NOPE) & (lane < MLA_QK), _rope_bwd(dkr, c, sa, sb), 0.0)
        qnw = qnw_ref[...]
        qn, cqn, rq = _rms_fwd(cq_ref[...], qnw)
        dwq_ref[...] += _mm_tn(qn, dq_lin)
        dcq, dqnw = _rms_bwd(_mm_nt(dq_lin, wq_ref[...]), cqn, rq, qnw)
        dqnw_ref[...] += dqnw
        kvnw = kvnw_ref[...]
        kvn, ckvn, rkv = _rms_fwd(ckv_ref[...], kvnw)
        dv_ = dv_ref[...]
        dwk_ref[...] += _mm_tn(kvn, dk_all)
        dwv_ref[...] += _mm_tn(kvn, dv_)
        dckv, dkvnw = _rms_bwd(_mm_nt(dk_all, wk_ref[...]) + _mm_nt(dv_, wv_ref[...]), ckvn, rkv, kvnw)
        dkvnw_ref[...] += dkvnw
        dxp = jnp.concatenate([dcq, dckv, dkr, dhq_ref[...], dhf_ref[...], dhi_ref[...], dhg_ref[...]], axis=-1)
        wpre = wpre_ref[...]
        u, xn, rx = _rms_fwd(x_ref[...], wpre)
        dwin_ref[...] += _mm_tn(u, dxp)
        dx_u, dwpre = _rms_bwd(_mm_nt(dxp, win_ref[...]), xn, rx, wpre)
        dwpre_ref[...] += dwpre
        dx_ref[...] = dh1_ref[...] + dx_u

    row = lambda w: pl.BlockSpec((tt, w), lambda i: (i, 0))
    full = lambda a: pl.BlockSpec(a.shape, lambda i: (0,) * a.ndim)
    sds = jax.ShapeDtypeStruct
    qk_w = MLA_HEADS * HEAD_PAD
    return pl.pallas_call(
        body, name="in_bwd", grid=(T // tt,),
        in_specs=[row(D_MODEL), row(D_MODEL), row(Q_RANK), row(KV_RANK), row(qk_w), row(qk_w), row(MLA_WIDTH),
                  row(HGRN_WIDTH), row(HGRN_WIDTH), row(HGRN_WIDTH), row(HGRN_WIDTH), row(HEAD_PAD), row(HEAD_PAD),
                  full(w_pre), full(win), full(qnw), full(wq), full(kvnw), full(wk), full(wv)],
        out_specs=[row(D_MODEL), full(win), full(wq), full(wk), full(wv), full(w_pre), full(qnw), full(kvnw)],
        out_shape=[sds((T, D_MODEL), F32), sds(win.shape, F32), sds(wq.shape, F32), sds(wk.shape, F32),
                   sds(wv.shape, F32), sds(w_pre.shape, F32), sds(qnw.shape, F32), sds(kvnw.shape, F32)],
        compiler_params=_params(("arbitrary",)),
    )(x, dh1, cq, ckv, dq, dk, dv, dhq, dhf, dhi, dhg, rc, rs, w_pre, win, qnw, wq, kvnw, wk, wv)


def _arrange_weights(win_full, wuq_full, wukv):
    dt = win_full.dtype
    z = lambda n: jnp.zeros((D_MODEL, n), dt)
    s2 = Q_RANK + KV_RANK
    half = MLA_ROPE // 2
    win_arr = jnp.concatenate([win_full[:, :s2], z(MLA_NOPE), win_full[:, s2:s2 + MLA_ROPE],
                               z(HEAD_PAD - MLA_QK), win_full[:, s2 + MLA_ROPE:]], axis=1)
    del half
    wq_arr = jnp.pad(wuq_full, ((0, 0), (0, 0), (0, HEAD_PAD - MLA_QK))).reshape(Q_RANK, MLA_HEADS * HEAD_PAD)
    wk_arr = jnp.pad(wukv[:, :, :MLA_NOPE], ((0, 0), (0, 0), (0, HEAD_PAD - MLA_NOPE))).reshape(
        KV_RANK, MLA_HEADS * HEAD_PAD)
    wv_arr = wukv[:, :, MLA_NOPE:].reshape(KV_RANK, MLA_WIDTH)
    return win_arr, wq_arr, wk_arr, wv_arr


def _unarrange_grads(dwin_arr, dwq_arr, dwk_arr, dwv_arr):
    s2 = Q_RANK + KV_RANK
    dwin = jnp.concatenate([dwin_arr[:, :s2], dwin_arr[:, s2 + MLA_NOPE:s2 + MLA_QK], dwin_arr[:, s2 + HEAD_PAD:]],
                           axis=1)
    dwuq = dwq_arr.reshape(Q_RANK, MLA_HEADS, HEAD_PAD)[:, :, :MLA_QK]
    dwukv = jnp.concatenate([dwk_arr.reshape(KV_RANK, MLA_HEADS, HEAD_PAD)[:, :, :MLA_NOPE],
                             dwv_arr.reshape(KV_RANK, MLA_HEADS, MLA_V)], axis=-1)
    return dwin, dwuq, dwukv


def _rope_inv_freq():
    inv = 1.0 / (ROPE_THETA ** (jnp.arange(0, MLA_ROPE, 2, dtype=F32) / MLA_ROPE))
    z = lambda n: jnp.zeros((n,), F32)
    return jnp.concatenate([z(MLA_NOPE), inv, inv, z(HEAD_PAD - MLA_QK)]).reshape(1, HEAD_PAD)


def _local_step(x, pos, tgt, small, win_arr, wq_arr, wk_arr, wv_arr, wout, wg, wu, wd):
    invf = _rope_inv_freq()
    cq, ckv, xph, qb, kb, vb, kt, vt, rc, rs = _in_fwd(x, pos, invf, small["attn_pre_norm"], win_arr, small["mla_q_norm"],
                                               wq_arr, small["mla_kv_norm"], wk_arr, wv_arr)
    o_raw, lse = _attn_fwd_t(qb, kb, vt)
    oh_raw, states = _hgrn_fwd(xph, small["hgrn_lb_logits"])
    h1, y1, zb, mixb, g, up, dy2b, dh2, loss_acc, d_fpost = _ffn_fwd(
        x, o_raw, oh_raw, xph, tgt, wout, small["mla_out_norm"], small["hgrn_out_norm"], small["attn_post_norm"],
        small["ffn_pre_norm"], small["ffn_post_norm"], wg, wu, wd)
    dwg, dwu, dwd, dzp = _ffn_bwd(zb, g, up, dy2b, wg, wu, wd)
    dh1, dwout, d_o, d_oh, dhg, dvec, d_fpre, d_post, d_mla, d_hg = _mid_bwd(
        dzp, dh2, h1, y1, mixb, o_raw, oh_raw, xph, wout, small["ffn_pre_norm"], small["attn_post_norm"],
        small["mla_out_norm"], small["hgrn_out_norm"])
    dqt, dk, dv = _attn_bwd_t(qb, kb, kt, vb, d_o, lse, dvec.reshape(lse.shape))
    dq = dqt.T
    dhq, dhf, dhi, d_lbl = _hgrn_bwd(xph, small["hgrn_lb_logits"], states, d_oh)
    dx, dwin_arr, dwq_arr, dwk_arr, dwv_arr, d_pre, d_qn, d_kvn = _in_bwd(
        x, dh1, cq, ckv, dq, dk, dv, dhq, dhf, dhi, dhg, rc, rs, small["attn_pre_norm"], win_arr,
        small["mla_q_norm"], wq_arr, small["mla_kv_norm"], wk_arr, wv_arr)
    dwin, dwuq, dwukv = _unarrange_grads(dwin_arr, dwq_arr, dwk_arr, dwv_arr)
    loss = 0.5 * jnp.sum(loss_acc) * (1.0 / D_MODEL)
    grads = dict(attn_pre_norm=d_pre, w_in=dwin, mla_q_norm=d_qn, mla_w_uq=dwuq, mla_kv_norm=d_kvn, mla_w_ukv=dwukv,
                 mla_out_norm=d_mla, hgrn_lb_logits=d_lbl, hgrn_out_norm=d_hg, w_out=dwout, attn_post_norm=d_post,
                 ffn_pre_norm=d_fpre, w_gate=dwg, w_up=dwu, w_down=dwd, ffn_post_norm=d_fpost)
    return loss, dx, grads


def _place():
    x, y, c = lax.axis_index("x"), lax.axis_index("y"), lax.axis_index("c")
    others = [(1 - x, y), (x, 1 - y), (1 - x, 1 - y)]
    return x, y, c, 2 * x + y, (x, y, 1 - c), others


def _half(ref, c, rows):
    return ref.at[pl.ds(pl.multiple_of(c * rows, 8), rows)]


def _gather_chips(arrs, name):
    n = len(arrs)
    halves = [a.shape[0] // 2 for a in arrs]

    def body(*refs):
        ins, outs = refs[:n], refs[n:2 * n]
        send, recv = refs[2 * n:]
        x, y, c, me, sib, others = _place()

        def rcopy(k, src, dst, to):
            return pltpu.make_async_remote_copy(src_ref=src, dst_ref=dst, send_sem=send.at[k], recv_sem=recv.at[k],
                                                device_id=to, device_id_type=MESH)

        first = []
        for j, (px, py) in enumerate(others):
            for a in range(n):
                first.append(rcopy(j * n + a, _half(ins[a], c, halves[a]), _half(outs[a].at[me], c, halves[a]),
                                   (px, py, c)))
        for cp in first:
            cp.start()
        passed = []
        for j, (px, py) in enumerate(others):
            chip = 2 * px + py
            for a in range(n):
                part = _half(outs[a].at[chip], c, halves[a])
                rcopy(j * n + a, part, part, (px, py, c)).wait_recv()
                fw = rcopy(3 * n + j * n + a, part, part, sib)
                fw.start()
                passed.append(fw)
        for j, (px, py) in enumerate(others):
            chip = 2 * px + py
            for a in range(n):
                part = _half(outs[a].at[chip], 1 - c, halves[a])
                rcopy(3 * n + j * n + a, part, part, sib).wait_recv()
        for cp in first + passed:
            cp.wait_send()

    return pl.pallas_call(
        body, name=name,
        in_specs=[ANY] * n, out_specs=[ANY] * n,
        out_shape=[jax.ShapeDtypeStruct((N_CHIPS,) + a.shape, a.dtype) for a in arrs],
        scratch_shapes=[pltpu.SemaphoreType.DMA((6 * n,)), pltpu.SemaphoreType.DMA((6 * n,))],
    )(*arrs)


GRAD_BLOCKS = 2


def _pair_swap(gs, sm):
    n = len(gs)

    def body(*refs):
        g_refs, sm_ref = refs[:n], refs[n]
        r_refs, ssib_ref = refs[n + 1:2 * n + 1], refs[2 * n + 1]
        send, recv = refs[2 * n + 2:]
        x, y, c, me, sib, others = _place()
        copies = []
        for a in range(n):
            h = gs[a].shape[1] // 2
            copies.append(pltpu.make_async_remote_copy(
                src_ref=g_refs[a].at[:, pl.ds(pl.multiple_of((1 - c) * h, 8), h)], dst_ref=r_refs[a],
                send_sem=send.at[a], recv_sem=recv.at[a], device_id=sib, device_id_type=MESH))
        copies.append(pltpu.make_async_remote_copy(src_ref=sm_ref, dst_ref=ssib_ref, send_sem=send.at[n],
                                                   recv_sem=recv.at[n], device_id=sib, device_id_type=MESH))
        for cp in copies:
            cp.start()
        for cp in copies:
            cp.wait()

    return pl.pallas_call(
        body, name="pair_swap", in_specs=[ANY] * (n + 1), out_specs=[ANY] * (n + 1),
        out_shape=[jax.ShapeDtypeStruct((N_CHIPS, g.shape[1] // 2, g.shape[2]), g.dtype) for g in gs]
        + [jax.ShapeDtypeStruct(sm.shape, sm.dtype)],
        scratch_shapes=[pltpu.SemaphoreType.DMA((n + 1,)), pltpu.SemaphoreType.DMA((n + 1,))],
    )(*gs, sm)


def _pair_sum(place, gs, rs, sm, ssib):
    n = len(gs)
    nb = GRAD_BLOCKS

    def body(place_ref, *refs):
        g_refs, r_refs = refs[:n], refs[n:2 * n]
        sm_ref, ss_ref = refs[2 * n], refs[2 * n + 1]
        p_refs, pair_ref = refs[2 * n + 2:3 * n + 2], refs[3 * n + 2]
        for a in range(n):
            p_refs[a][0] = (g_refs[a][0] + r_refs[a][0]).astype(p_refs[a].dtype)

        @pl.when((pl.program_id(0) == 0) & (pl.program_id(1) == 0))
        def _():
            pair_ref[...] = sm_ref[...] + ss_ref[...]

    in_specs, out_specs, out_shape = [], [], []
    for g in gs:
        blk = (1, g.shape[1] // 2 // nb, g.shape[2])
        in_specs.append(pl.BlockSpec(blk, lambda i, k, p: (k, p[0] * nb + i, 0)))
    for g in gs:
        blk = (1, g.shape[1] // 2 // nb, g.shape[2])
        in_specs.append(pl.BlockSpec(blk, lambda i, k, p: (k, i, 0)))
        out_specs.append(pl.BlockSpec(blk, lambda i, k, p: (k, i, 0)))
        out_shape.append(jax.ShapeDtypeStruct((N_CHIPS, g.shape[1] // 2, g.shape[2]), BF16))
    sm_spec = pl.BlockSpec(sm.shape, lambda i, k, p: (0, 0))
    return pl.pallas_call(
        body, name="pair_sum",
        grid_spec=pltpu.PrefetchScalarGridSpec(num_scalar_prefetch=1, grid=(nb, N_CHIPS),
                                               in_specs=in_specs + [sm_spec, sm_spec],
                                               out_specs=out_specs + [sm_spec]),
        out_shape=out_shape + [jax.ShapeDtypeStruct(sm.shape, F32)],
        compiler_params=_params(("arbitrary", "arbitrary")),
    )(place, *gs, *rs, sm, ssib)


def _chip_swap(ps, pair):
    n = len(ps)
    hs = SMALL_ROWS // 2

    def body(*refs):
        p_refs, pair_ref = refs[:n], refs[n]
        ri_refs, sm4_ref = refs[n + 1:2 * n + 1], refs[2 * n + 1]
        send, recv, lsem = refs[2 * n + 2:]
        x, y, c, me, sib, others = _place()
        local = pltpu.make_async_copy(pair_ref, sm4_ref.at[me], lsem.at[0])
        local.start()
        copies = []
        for j, (px, py) in enumerate(others):
            chip = 2 * px + py
            for a in range(n):
                copies.append(pltpu.make_async_remote_copy(
                    src_ref=p_refs[a].at[chip], dst_ref=ri_refs[a].at[j], send_sem=send.at[j * (n + 1) + a],
                    recv_sem=recv.at[j * (n + 1) + a], device_id=(px, py, c), device_id_type=MESH))
            copies.append(pltpu.make_async_remote_copy(
                src_ref=_half(pair_ref, c, hs), dst_ref=_half(sm4_ref.at[me], c, hs), send_sem=send.at[j * (n + 1) + n],
                recv_sem=recv.at[j * (n + 1) + n], device_id=(px, py, c), device_id_type=MESH))
        for cp in copies:
            cp.start()
        for j, (px, py) in enumerate(others):
            chip = 2 * px + py
            for a in range(n):
                pltpu.make_async_remote_copy(
                    src_ref=p_refs[a].at[chip], dst_ref=ri_refs[a].at[j], send_sem=send.at[j * (n + 1) + a],
                    recv_sem=recv.at[j * (n + 1) + a], device_id=(px, py, c), device_id_type=MESH).wait_recv()
            part = _half(sm4_ref.at[chip], c, hs)
            pltpu.make_async_remote_copy(src_ref=part, dst_ref=part, send_sem=send.at[j * (n + 1) + n],
                                         recv_sem=recv.at[j * (n + 1) + n], device_id=(px, py, c),
                                         device_id_type=MESH).wait_recv()
        for cp in copies:
            cp.wait_send()
        local.wait()

    k = 3 * (n + 1)
    return pl.pallas_call(
        body, name="chip_swap", in_specs=[ANY] * (n + 1), out_specs=[ANY] * (n + 1),
        out_shape=[jax.ShapeDtypeStruct((3,) + p.shape[1:], p.dtype) for p in ps]
        + [jax.ShapeDtypeStruct((N_CHIPS,) + pair.shape, pair.dtype)],
        scratch_shapes=[pltpu.SemaphoreType.DMA((k,)), pltpu.SemaphoreType.DMA((k,)), pltpu.SemaphoreType.DMA((1,))],
    )(*ps, pair)


def _chip_sum(place, gs, rs, ris):
    n = len(gs)
    nb = GRAD_BLOCKS

    def body(place_ref, *refs):
        g_refs, r_refs, ri_refs, o_refs = refs[:n], refs[n:2 * n], refs[2 * n:3 * n], refs[3 * n:]
        for a in range(n):
            ri = ri_refs[a]
            o_refs[a][...] = (g_refs[a][0] + r_refs[a][0]) + ri[0].astype(F32) + ri[1].astype(F32) + ri[2].astype(F32)

    in_specs, out_specs, out_shape = [], [], []
    for g in gs:
        blk = (1, g.shape[1] // 2 // nb, g.shape[2])
        in_specs.append(pl.BlockSpec(blk, lambda i, p: (p[1], p[0] * nb + i, 0)))
    for g in gs:
        blk = (1, g.shape[1] // 2 // nb, g.shape[2])
        in_specs.append(pl.BlockSpec(blk, lambda i, p: (p[1], i, 0)))
    for g in gs:
        rb = g.shape[1] // 2 // nb
        in_specs.append(pl.BlockSpec((3, rb, g.shape[2]), lambda i, p: (0, i, 0)))
        out_specs.append(pl.BlockSpec((rb, g.shape[2]), lambda i, p: (p[0] * nb + i, 0)))
        out_shape.append(jax.ShapeDtypeStruct(g.shape[1:], F32))
    return pl.pallas_call(
        body, name="chip_sum",
        grid_spec=pltpu.PrefetchScalarGridSpec(num_scalar_prefetch=1, grid=(nb,), in_specs=in_specs, out_specs=out_specs),
        out_shape=out_shape,
        compiler_params=_params(("arbitrary",)),
    )(place, *gs, *rs, *ris)


def _pair_fill(gfs, sm4):
    n = len(gfs)
    hs = SMALL_ROWS // 2

    def body(*refs):
        g_refs, sm4_ref = refs[n + 1:2 * n + 1], refs[2 * n + 1]
        send, recv = refs[2 * n + 2:]
        x, y, c, me, sib, others = _place()
        copies, waits = [], []
        for a in range(n):
            h = gfs[a].shape[0] // 2
            mine, theirs = _half(g_refs[a], c, h), _half(g_refs[a], 1 - c, h)
            copies.append(pltpu.make_async_remote_copy(src_ref=mine, dst_ref=mine, send_sem=send.at[a],
                                                       recv_sem=recv.at[a], device_id=sib, device_id_type=MESH))
            waits.append(pltpu.make_async_remote_copy(src_ref=theirs, dst_ref=theirs, send_sem=send.at[a],
                                                      recv_sem=recv.at[a], device_id=sib, device_id_type=MESH))
        for j, (px, py) in enumerate(others):
            chip = 2 * px + py
            mine, theirs = _half(sm4_ref.at[chip], c, hs), _half(sm4_ref.at[chip], 1 - c, hs)
            copies.append(pltpu.make_async_remote_copy(src_ref=mine, dst_ref=mine, send_sem=send.at[n + j],
                                                       recv_sem=recv.at[n + j], device_id=sib, device_id_type=MESH))
            waits.append(pltpu.make_async_remote_copy(src_ref=theirs, dst_ref=theirs, send_sem=send.at[n + j],
                                                      recv_sem=recv.at[n + j], device_id=sib, device_id_type=MESH))
        for cp in copies:
            cp.start()
        for w in waits:
            w.wait_recv()
        for cp in copies:
            cp.wait_send()

    return pl.pallas_call(
        body, name="pair_fill", in_specs=[ANY] * (n + 1), out_specs=[ANY] * (n + 1),
        out_shape=[jax.ShapeDtypeStruct(g.shape, g.dtype) for g in gfs] + [jax.ShapeDtypeStruct(sm4.shape, sm4.dtype)],
        input_output_aliases={i: i for i in range(n + 1)},
        scratch_shapes=[pltpu.SemaphoreType.DMA((n + 3,)), pltpu.SemaphoreType.DMA((n + 3,))],
    )(*gfs, sm4)


def _adamw_math(w, g, m, v):
    m = ADAM_B1 * m + (1.0 - ADAM_B1) * g
    v = ADAM_B2 * v + (1.0 - ADAM_B2) * (g * g)
    m_hat = m / (1.0 - ADAM_B1 ** ADAM_STEP)
    v_hat = v / (1.0 - ADAM_B2 ** ADAM_STEP)
    return -ADAM_LR * (m_hat / (jnp.sqrt(v_hat) + ADAM_EPS) + ADAM_WD * w), m, v


def _adamw(w, g, m, v, rb, name):
    rows, cols = w.shape

    def body(w_ref, g_ref, m_ref, v_ref, d_ref, mo_ref, vo_ref):
        d, mo, vo = _adamw_math(w_ref[...], g_ref[...], m_ref[...], v_ref[...])
        d_ref[...] = d
        mo_ref[...] = mo
        vo_ref[...] = vo

    spec = pl.BlockSpec((rb, cols), lambda i: (i, 0))
    return pl.pallas_call(
        body, name=name, grid=(rows // rb,), in_specs=[spec] * 4, out_specs=[spec] * 3,
        out_shape=[jax.ShapeDtypeStruct(w.shape, F32)] * 3,
        compiler_params=_params(("arbitrary",)),
    )(w, g, m, v)


def _adamw_small(sm4, w, m, v):
    def body(sm4_ref, w_ref, m_ref, v_ref, g_ref, d_ref, mo_ref, vo_ref):
        g = ((sm4_ref[0] + sm4_ref[1]) + sm4_ref[2]) + sm4_ref[3]
        g_ref[...] = g
        d, mo, vo = _adamw_math(w_ref[...], g, m_ref[...], v_ref[...])
        d_ref[...] = d
        mo_ref[...] = mo
        vo_ref[...] = vo

    return pl.pallas_call(
        body, name="adamw_small", out_shape=[jax.ShapeDtypeStruct(w.shape, F32)] * 4,
        compiler_params=pltpu.CompilerParams(vmem_limit_bytes=VMEM_LIMIT),
    )(sm4, w, m, v)


SMALL_NAMES = ("attn_pre_norm", "mla_q_norm", "mla_kv_norm", "mla_w_ukv", "mla_out_norm", "hgrn_lb_logits",
               "hgrn_out_norm", "attn_post_norm", "ffn_pre_norm", "ffn_post_norm")
BIG_NAMES = ("w_in", "mla_w_uq", "w_out", "w_gate", "w_up", "w_down")
WEIGHT_NAMES = ("attn_pre_norm", "w_in", "mla_q_norm", "mla_w_uq", "mla_kv_norm", "mla_w_ukv", "mla_out_norm",
                "hgrn_lb_logits", "hgrn_out_norm", "w_out", "attn_post_norm", "ffn_pre_norm", "w_gate", "w_up", "w_down",
                "ffn_post_norm")


UQ_COMM_SHAPE = (192, 384)


def _pack_small(vals, extra=None):
    parts = [vals[n].reshape(-1) for n in SMALL_NAMES]
    if extra is not None:
        parts.append(extra.reshape(1))
    flat = jnp.concatenate(parts)
    return jnp.pad(flat, (0, SMALL_ROWS * D_MODEL - flat.shape[0])).reshape(SMALL_ROWS, D_MODEL)


def _unpack_small(buf, shapes):
    flat = buf.reshape(-1)
    out, off = {}, 0
    for n, size in zip(SMALL_NAMES, SMALL_SIZES):
        out[n] = flat[off:off + size].reshape(shapes[n])
        off += size
    return out


def kernel(x, positions, attn_pre_norm, w_in, mla_q_norm, mla_w_uq, mla_kv_norm, mla_w_ukv, mla_out_norm, hgrn_lb_logits, hgrn_out_norm, w_out, attn_post_norm, ffn_pre_norm, w_gate, w_up, w_down, ffn_post_norm, loss_target, m_attn_pre_norm, m_w_in, m_mla_q_norm, m_mla_w_uq, m_mla_kv_norm, m_mla_w_ukv, m_mla_out_norm, m_hgrn_lb_logits, m_hgrn_out_norm, m_w_out, m_attn_post_norm, m_ffn_pre_norm, m_w_gate, m_w_up, m_w_down, m_ffn_post_norm, v_attn_pre_norm, v_w_in, v_mla_q_norm, v_mla_w_uq, v_mla_kv_norm, v_mla_w_ukv, v_mla_out_norm, v_hgrn_lb_logits, v_hgrn_out_norm, v_w_out, v_attn_post_norm, v_ffn_pre_norm, v_w_gate, v_w_up, v_w_down, v_ffn_post_norm):
    args = locals()
    W = {n: args[n] for n in WEIGHT_NAMES}
    M = {n: args["m_" + n] for n in WEIGHT_NAMES}
    V = {n: args["v_" + n] for n in WEIGHT_NAMES}
    T = x.shape[1]
    cx, cy, cc = lax.axis_index("x"), lax.axis_index("y"), lax.axis_index("c")

    shard2d = {"w_in": (D_MODEL, D_IN // N_CHIPS), "mla_w_uq": (Q_RANK // N_CHIPS, MLA_HEADS * MLA_QK),
               "w_out": (D_MODEL // N_CHIPS, D_MODEL), "w_gate": (D_MODEL, FF_SHARD), "w_up": (D_MODEL, FF_SHARD),
               "w_down": (FF_SHARD, D_MODEL)}
    me = 2 * cx + cy
    local_b = [W[n].reshape(shard2d[n]).astype(BF16) for n in BIG_NAMES]
    stacks = _gather_chips(local_b, "gather_weights")
    win4, wuq4, wout4, wg4, wu4, wd4 = [lax.dynamic_update_slice(s, l[None], (me, 0, 0))
                                        for s, l in zip(stacks, local_b)]
    win_full = win4.transpose(1, 0, 2).reshape(D_MODEL, D_IN)
    wuq_full = wuq4.reshape(Q_RANK, MLA_HEADS, MLA_QK)
    win_arr, wq_arr, wk_arr, wv_arr = _arrange_weights(win_full, wuq_full, mla_w_ukv[0].astype(BF16))
    small = {n: W[n][0] if n == "mla_w_ukv" else W[n].reshape(-1, W[n].shape[-1]) for n in SMALL_NAMES}

    loss_local, dx, grads = _local_step(x[0], positions.reshape(T, 1), loss_target[0], small, win_arr, wq_arr, wk_arr,
                                        wv_arr, wout4.reshape(D_MODEL, D_MODEL), wg4, wu4, wd4)

    dwin4 = grads["w_in"].reshape(D_MODEL, N_CHIPS, D_IN // N_CHIPS).transpose(1, 0, 2)
    gs = [dwin4, grads["mla_w_uq"].reshape((N_CHIPS,) + UQ_COMM_SHAPE), grads["w_out"].reshape((N_CHIPS,) + shard2d["w_out"]),
          grads["w_gate"], grads["w_up"], grads["w_down"]]
    sm = _pack_small(grads, loss_local)
    *rs, ssib = _pair_swap(gs, sm)
    place = jnp.stack([cc, me]).astype(jnp.int32)
    *ps, pair = _pair_sum(place, gs, rs, sm, ssib)
    *ris, sm4 = _chip_swap(ps, pair)
    gfs = _chip_sum(place, gs, rs, ris)
    *gfin, smf = _pair_fill(gfs, sm4)

    row_blocks = {"w_in": 256, "mla_w_uq": 96, "w_out": 256, "w_gate": 256, "w_up": 256, "w_down": 352}
    G, DW, NM, NV = {}, {}, {}, {}
    for k, n in enumerate(BIG_NAMES):
        g2 = gfin[k].reshape(shard2d[n])
        d, mo, vo = _adamw(W[n].reshape(shard2d[n]), g2, M[n].reshape(shard2d[n]), V[n].reshape(shard2d[n]),
                           row_blocks[n], "adamw_" + n)
        G[n], DW[n], NM[n], NV[n] = (t.reshape(W[n].shape) for t in (g2, d, mo, vo))
    gs_buf, ds, ms, vs = _adamw_small(smf, _pack_small(W), _pack_small(M), _pack_small(V))
    shapes = {n: W[n].shape for n in SMALL_NAMES}
    for dst, buf in ((G, gs_buf), (DW, ds), (NM, ms), (NV, vs)):
        dst.update(_unpack_small(buf, shapes))

    loss = gs_buf.reshape(-1)[sum(SMALL_SIZES)]
    return (loss, dx[None], *[G[n] for n in WEIGHT_NAMES], *[DW[n] for n in WEIGHT_NAMES],
            *[NM[n] for n in WEIGHT_NAMES], *[NV[n] for n in WEIGHT_NAMES])
```

```python
import jax
import jax.numpy as jnp
from jax import lax
from jax.experimental import pallas as pl
from jax.experimental.pallas import tpu as pltpu

F32 = jnp.float32
BF16 = jnp.bfloat16
MXU_DTYPE = BF16

D_MODEL = 1024
MLA_HEADS = 8
MLA_NOPE = 64
MLA_ROPE = 32
MLA_V = 64
MLA_QK = MLA_NOPE + MLA_ROPE
Q_RANK = 384
KV_RANK = 128
MLA_WIDTH = MLA_HEADS * MLA_V
HEAD_PAD = 128
HGRN_HEADS = 4
HGRN_DIM = 128
HGRN_WIDTH = HGRN_HEADS * HGRN_DIM
CHUNK = 64
SUB = 16
D_IN = Q_RANK + KV_RANK + MLA_ROPE + 4 * HGRN_WIDTH
D_IN_ARR = Q_RANK + KV_RANK + HEAD_PAD + 4 * HGRN_WIDTH
D_FF = 2816
N_CHIPS = 4
FF_SHARD = D_FF // N_CHIPS
EPS = 1e-6
ROPE_THETA = 10000.0
ATTN_SCALE = MLA_QK ** -0.5
NEG_BIG = -1e30

ADAM_LR = 0.001
ADAM_B1 = 0.9
ADAM_B2 = 0.999
ADAM_EPS = 1e-08
ADAM_WD = 0.01
ADAM_STEP = 10

VMEM_LIMIT = 56 * 1024 * 1024

SMALL_ROWS = 144
SMALL_SIZES = (1024, 384, 128, 131072, 512, 1024, 512, 1024, 1024, 1024)

MESH = pl.DeviceIdType.MESH
ANY = pl.BlockSpec(memory_space=pl.ANY)


def _dot(a, b, dims, exact):
    if exact:
        return lax.dot_general(a.astype(F32), b.astype(F32), (dims, ((), ())), precision=lax.Precision.HIGH,
                               preferred_element_type=F32)
    return lax.dot_general(a.astype(MXU_DTYPE), b.astype(MXU_DTYPE), (dims, ((), ())), preferred_element_type=F32)


def _mm(a, b, exact=False):
    return _dot(a, b, ((1,), (0,)), exact)


def _mm_nt(a, b, exact=False):
    return _dot(a, b, ((1,), (1,)), exact)


def _mm_tn(a, b, exact=False):
    return _dot(a, b, ((0,), (0,)), exact)


def _rms_fwd(x, w):
    r = lax.rsqrt(jnp.mean(x * x, axis=-1, keepdims=True) + EPS)
    xn = x * r
    return xn * w, xn, r


def _rms_bwd(dy, xn, r, w):
    dxn = dy * w
    dx = r * (dxn - xn * jnp.mean(dxn * xn, axis=-1, keepdims=True))
    dw = jnp.sum(dy * xn, axis=0, keepdims=True)
    return dx, dw


def _group_sums(v, gs):
    t, n = v.shape
    lane = lax.broadcasted_iota(jnp.int32, (t, 128), 1)
    out = []
    for p in range(n // 128):
        vb = v[:, 128 * p:128 * (p + 1)]
        if gs == 128:
            out.append(jnp.sum(vb, axis=-1, keepdims=True))
        else:
            out.append(jnp.sum(jnp.where(lane < 64, vb, 0.0), axis=-1, keepdims=True))
            out.append(jnp.sum(jnp.where(lane >= 64, vb, 0.0), axis=-1, keepdims=True))
    return out


def _group_bcast(sums, gs, t):
    lane = lax.broadcasted_iota(jnp.int32, (t, 128), 1)
    if gs == 128:
        return jnp.concatenate([jnp.broadcast_to(s, (t, 128)) for s in sums], axis=-1)
    return jnp.concatenate([jnp.where(lane < 64, sums[2 * p], sums[2 * p + 1]) for p in range(len(sums) // 2)],
                           axis=-1)


def _grms_fwd(x, w, gs):
    t = x.shape[0]
    r = lax.rsqrt(_group_bcast(_group_sums(x * x, gs), gs, t) * (1.0 / gs) + EPS)
    xn = x * r
    return xn * w, xn, r


def _grms_bwd(dy, xn, r, w, gs):
    t = dy.shape[0]
    dxn = dy * w
    dx = r * (dxn - xn * (_group_bcast(_group_sums(dxn * xn, gs), gs, t) * (1.0 / gs)))
    dw = jnp.sum(dy * xn, axis=0, keepdims=True)
    return dx, dw


def _rope_tables(c_tab, s_tab):
    lane = lax.broadcasted_iota(jnp.int32, c_tab.shape, 1)
    first = (lane >= MLA_NOPE) & (lane < MLA_NOPE + MLA_ROPE // 2)
    second = (lane >= MLA_NOPE + MLA_ROPE // 2) & (lane < MLA_QK)
    return c_tab, jnp.where(first, -s_tab, 0.0), jnp.where(second, s_tab, 0.0)


def _rope(v, c, sa, sb):
    return v * c + pltpu.roll(v, HEAD_PAD - MLA_ROPE // 2, 1) * sa + pltpu.roll(v, MLA_ROPE // 2, 1) * sb


def _rope_bwd(d, c, sa, sb):
    return d * c - pltpu.roll(d, HEAD_PAD - MLA_ROPE // 2, 1) * sa - pltpu.roll(d, MLA_ROPE // 2, 1) * sb


def _params(sem, vmem=VMEM_LIMIT):
    return pltpu.CompilerParams(dimension_semantics=sem, vmem_limit_bytes=vmem)


def _in_fwd(x, pos, invf, w_pre, win, qnw, wq, kvnw, wk, wv, tt=256):
    T = x.shape[0]

    def body(x_ref, pos_ref, invf_ref, wpre_ref, win_ref, qnw_ref, wq_ref, kvnw_ref, wk_ref, wv_ref,
             cq_ref, ckv_ref, xph_ref, q_ref, k_ref, v_ref, kt_ref, vt_ref, rc_ref, rs_ref):
        u, _, _ = _rms_fwd(x_ref[...], wpre_ref[...])
        xp = _mm_nt(u, win_ref[...])
        cq = xp[:, :Q_RANK]
        ckv = xp[:, Q_RANK:Q_RANK + KV_RANK]
        kr = xp[:, Q_RANK + KV_RANK:Q_RANK + KV_RANK + HEAD_PAD]
        cq_ref[...] = cq
        ckv_ref[...] = ckv
        xph_ref[...] = xp[:, Q_RANK + KV_RANK + HEAD_PAD:]
        ang = pos_ref[...].astype(F32) * invf_ref[...]
        c_tab = jnp.cos(ang)
        s_tab = jnp.sin(ang)
        rc_ref[...] = c_tab
        rs_ref[...] = s_tab
        c, sa, sb = _rope_tables(c_tab, s_tab)
        qn, _, _ = _rms_fwd(cq, qnw_ref[...])
        q = _mm(qn, wq_ref[...])
        kvn, _, _ = _rms_fwd(ckv, kvnw_ref[...])
        kn = _mm(kvn, wk_ref[...])
        v = _mm(kvn, wv_ref[...])
        v_ref[...] = v.astype(v_ref.dtype)
        vt_ref[...] = v.T.astype(vt_ref.dtype)
        krr = _rope(kr, c, sa, sb)
        for h in range(MLA_HEADS):
            sl = slice(HEAD_PAD * h, HEAD_PAD * (h + 1))
            q_ref[:, sl] = _rope(q[:, sl], c, sa, sb).astype(q_ref.dtype)
            kh = kn[:, sl] + krr
            k_ref[:, sl] = kh.astype(k_ref.dtype)
            kt_ref[sl, :] = kh.T.astype(kt_ref.dtype)

    row = lambda w: pl.BlockSpec((tt, w), lambda i: (i, 0))
    full = lambda a: pl.BlockSpec(a.shape, lambda i: (0,) * a.ndim)
    qk_w = MLA_HEADS * HEAD_PAD
    return pl.pallas_call(
        body, name="in_fwd", grid=(T // tt,),
        in_specs=[row(D_MODEL), row(1), full(invf), full(w_pre), full(win), full(qnw), full(wq), full(kvnw),
                  full(wk), full(wv)],
        out_specs=[row(Q_RANK), row(KV_RANK), row(4 * HGRN_WIDTH), row(qk_w), row(qk_w), row(MLA_WIDTH),
                   pl.BlockSpec((qk_w, tt), lambda i: (0, i)), pl.BlockSpec((MLA_WIDTH, tt), lambda i: (0, i)),
                   row(HEAD_PAD), row(HEAD_PAD)],
        out_shape=[jax.ShapeDtypeStruct((T, Q_RANK), F32), jax.ShapeDtypeStruct((T, KV_RANK), F32),
                   jax.ShapeDtypeStruct((T, 4 * HGRN_WIDTH), F32), jax.ShapeDtypeStruct((T, qk_w), MXU_DTYPE),
                   jax.ShapeDtypeStruct((T, qk_w), MXU_DTYPE), jax.ShapeDtypeStruct((T, MLA_WIDTH), MXU_DTYPE),
                   jax.ShapeDtypeStruct((qk_w, T), MXU_DTYPE), jax.ShapeDtypeStruct((MLA_WIDTH, T), MXU_DTYPE),
                   jax.ShapeDtypeStruct((T, HEAD_PAD), F32), jax.ShapeDtypeStruct((T, HEAD_PAD), F32)],
        compiler_params=_params(("arbitrary",)),
    )(x, pos, invf, w_pre, win, qnw, wq, kvnw, wk, wv)


def _attn_fwd_t(qb, kb, vt, tq=256, hps=8):
    T = qb.shape[0]
    nq = T // tq

    def body(q_ref, k_ref, vt_ref, o_ref, lse_ref, acc_scr):
        qi = pl.program_id(1)
        heads = [slice(HEAD_PAD * a, HEAD_PAD * (a + 1)) for a in range(hps)]
        acc_scr[...] = jnp.zeros_like(acc_scr)

        def step(j, carry, masked):
            start = pl.multiple_of(j * tq, tq)
            scores = [_mm_nt(k_ref[pl.ds(start, tq), heads[a]], q_ref[:, heads[a]]) for a in range(hps)]
            new = []
            for a in range(hps):
                m, l = carry[a]
                s = scores[a] * ATTN_SCALE
                if masked:
                    kk = lax.broadcasted_iota(jnp.int32, (tq, tq), 0)
                    qq = lax.broadcasted_iota(jnp.int32, (tq, tq), 1)
                    s = jnp.where(kk <= qq, s, NEG_BIG)
                m_new = jnp.maximum(m, jnp.max(s, axis=0, keepdims=True))
                alpha = jnp.exp(m - m_new)
                p = jnp.exp(s - m_new)
                l = l * alpha + jnp.sum(p, axis=0, keepdims=True)
                vtj = vt_ref[2 * MLA_V * (a // 2):2 * MLA_V * (a // 2 + 1), pl.ds(start, tq)]
                acc_scr[a] = acc_scr[a] * alpha + _mm(vtj, p)
                new.append((m_new, l))
            return tuple(new)

        init = tuple((jnp.full((1, tq), NEG_BIG, F32), jnp.zeros((1, tq), F32)) for _ in range(hps))
        carry = lax.fori_loop(0, qi, lambda j, c: step(j, c, False), init)
        carry = step(qi, carry, True)
        row = lax.broadcasted_iota(jnp.int32, (2 * MLA_V, tq), 0)
        for pr in range(hps // 2):
            (m0, l0), (m1, l1) = carry[2 * pr], carry[2 * pr + 1]
            ot = jnp.where(row < MLA_V, acc_scr[2 * pr] / l0, acc_scr[2 * pr + 1] / l1)
            o_ref[:, 2 * MLA_V * pr:2 * MLA_V * (pr + 1)] = ot.T
            lse_ref[pr, 0:1, :] = m0 + jnp.log(l0)
            lse_ref[pr, 1:2, :] = m1 + jnp.log(l1)

    return pl.pallas_call(
        body, name="attn_fwd", grid=(MLA_HEADS // hps, nq),
        in_specs=[pl.BlockSpec((tq, hps * HEAD_PAD), lambda g, i: (i, g)),
                  pl.BlockSpec((T, hps * HEAD_PAD), lambda g, i: (0, g)),
                  pl.BlockSpec((hps * MLA_V, T), lambda g, i: (g, 0))],
        out_specs=[pl.BlockSpec((tq, hps * MLA_V), lambda g, i: (i, g)),
                   pl.BlockSpec((hps // 2, 2, tq), lambda g, i: (g, 0, i))],
        out_shape=[jax.ShapeDtypeStruct((T, MLA_WIDTH), F32), jax.ShapeDtypeStruct((MLA_HEADS // 2, 2, T), F32)],
        scratch_shapes=[pltpu.VMEM((hps, 2 * MLA_V, tq), F32)],
        compiler_params=_params(("arbitrary", "arbitrary")),
    )(qb, kb, vt)


def _attn_bwd_t(qb, kb, kt, vb, dob, lse, dvec, tq=256, hps=4):
    T = qb.shape[0]
    nq = T // tq

    def body(q_ref, k_ref, kt_ref, v_ref, do_ref, lse_ref, d_ref, dqt_ref, dk_ref, dv_ref, va_scr, dv_scr):
        j = pl.program_id(1)

        @pl.when(j == 0)
        def _():
            dqt_ref[...] = jnp.zeros_like(dqt_ref)

        lane = lax.broadcasted_iota(jnp.int32, (tq, 2 * MLA_V), 1)
        heads = [slice(HEAD_PAD * a, HEAD_PAD * (a + 1)) for a in range(hps)]
        pairs = [slice(2 * MLA_V * p, 2 * MLA_V * (p + 1)) for p in range(hps // 2)]
        for pr in range(hps // 2):
            vpair = v_ref[:, pairs[pr]]
            va_scr[2 * pr] = jnp.where(lane < MLA_V, vpair, jnp.zeros_like(vpair))
            va_scr[2 * pr + 1] = jnp.where(lane >= MLA_V, vpair, jnp.zeros_like(vpair))
        dk_ref[...] = jnp.zeros_like(dk_ref)
        dv_scr[...] = jnp.zeros_like(dv_scr)

        def step(i, masked):
            start = pl.multiple_of(i * tq, tq)
            rows = pl.ds(start, tq)
            scores = [_mm_nt(k_ref[:, heads[a]], q_ref[rows, heads[a]]) for a in range(hps)]
            dps = [_mm_nt(va_scr[a], do_ref[rows, pairs[a // 2]]) for a in range(hps)]
            for a in range(hps):
                pr, r = a // 2, a % 2
                p = jnp.exp(scores[a] * ATTN_SCALE - lse_ref[pr, r:r + 1, rows])
                if masked:
                    kk = lax.broadcasted_iota(jnp.int32, (tq, tq), 0)
                    qq = lax.broadcasted_iota(jnp.int32, (tq, tq), 1)
                    p = jnp.where(kk <= qq, p, 0.0)
                ds = p * (dps[a] - d_ref[pr, r:r + 1, rows]) * ATTN_SCALE
                dv_scr[a] += _mm(p, do_ref[rows, pairs[pr]])
                dk_ref[:, heads[a]] += _mm(ds, q_ref[rows, heads[a]])
                dqt_ref[heads[a], rows] += _mm(kt_ref[heads[a], :], ds)

        def loop_body(i, _):
            step(i, False)
            return 0

        step(j, True)
        lax.fori_loop(j + 1, nq, loop_body, 0)
        for pr in range(hps // 2):
            dv_ref[:, pairs[pr]] = jnp.where(lane < MLA_V, dv_scr[2 * pr], dv_scr[2 * pr + 1])

    stat = pl.BlockSpec((hps // 2, 2, T), lambda g, j: (g, 0, 0))
    return pl.pallas_call(
        body, name="attn_bwd", grid=(MLA_HEADS // hps, nq),
        in_specs=[pl.BlockSpec((T, hps * HEAD_PAD), lambda g, j: (0, g)),
                  pl.BlockSpec((tq, hps * HEAD_PAD), lambda g, j: (j, g)),
                  pl.BlockSpec((hps * HEAD_PAD, tq), lambda g, j: (g, j)),
                  pl.BlockSpec((tq, hps * MLA_V), lambda g, j: (j, g)),
                  pl.BlockSpec((T, hps * MLA_V), lambda g, j: (0, g)), stat, stat],
        out_specs=[pl.BlockSpec((hps * HEAD_PAD, T), lambda g, j: (g, 0)),
                   pl.BlockSpec((tq, hps * HEAD_PAD), lambda g, j: (j, g)),
                   pl.BlockSpec((tq, hps * MLA_V), lambda g, j: (j, g))],
        out_shape=[jax.ShapeDtypeStruct((MLA_HEADS * HEAD_PAD, T), F32),
                   jax.ShapeDtypeStruct((T, MLA_HEADS * HEAD_PAD), F32),
                   jax.ShapeDtypeStruct((T, MLA_WIDTH), F32)],
        scratch_shapes=[pltpu.VMEM((hps, tq, 2 * MLA_V), vb.dtype), pltpu.VMEM((hps, tq, 2 * MLA_V), F32)],
        compiler_params=_params(("arbitrary", "arbitrary")),
    )(qb, kb, kt, vb, dob, lse, dvec)


def _cumsum_rows(x):
    n = x.shape[0]
    row = lax.broadcasted_iota(jnp.int32, x.shape, 0)
    s = 1
    while s < n:
        x = x + jnp.where(row >= s, pltpu.roll(x, s, 0), 0.0)
        s *= 2
    return x


def _rev_cumsum_rows(x):
    n = x.shape[0]
    row = lax.broadcasted_iota(jnp.int32, x.shape, 0)
    s = 1
    while s < n:
        x = x + jnp.where(row < n - s, pltpu.roll(x, n - s, 0), 0.0)
        s *= 2
    return x


def _lb_from_logits(l):
    l0, l1 = l[0:1, :], l[1:2, :]
    m = jnp.maximum(l0, l1)
    e0, e1 = jnp.exp(l0 - m), jnp.exp(l1 - m)
    return e0 / (e0 + e1)


def _hgrn_gates(hq, hf, lb):
    sig_f = jax.nn.sigmoid(hf)
    f = lb + (1.0 - lb) * sig_f
    sig_q = jax.nn.sigmoid(hq)
    return sig_f, f, jnp.log(f), 1.0 - f, sig_q, hq * sig_q


def _hgrn_intra(q, kk, b, exact=False):
    row = lax.broadcasted_iota(jnp.int32, b.shape, 0)
    qs, ks, eqs, eks, a_rows = [], [], [], [], []
    for i in range(CHUNK // SUB):
        ref = b[SUB * i:SUB * i + 1, :]
        eq = jnp.exp(b[SUB * i:SUB * (i + 1), :] - ref)
        ek = jnp.exp(jnp.where(row < SUB * (i + 1), ref - b, NEG_BIG))
        qi = q[SUB * i:SUB * (i + 1), :] * eq
        ki = kk * ek
        a_rows.append(_mm_nt(qi, ki, exact))
        qs.append(qi), ks.append(ki), eqs.append(eq), eks.append(ek)
    tt = lax.broadcasted_iota(jnp.int32, (CHUNK, CHUNK), 0)
    ss = lax.broadcasted_iota(jnp.int32, (CHUNK, CHUNK), 1)
    causal = ss <= tt
    a = jnp.where(causal, jnp.concatenate(a_rows, axis=0), 0.0)
    return a, causal, qs, ks, eqs, eks


def _hgrn_fwd(xph, lbl, tg=512):
    T = xph.shape[0]
    ng, ncg = T // tg, tg // CHUNK
    cols = [slice(HGRN_DIM * h, HGRN_DIM * (h + 1)) for h in range(HGRN_HEADS)]

    def body(lbl_ref, hq_ref, hf_ref, hi_ref, o_ref, st_ref, s_scr):
        @pl.when(pl.program_id(0) == 0)
        def _():
            s_scr[...] = jnp.zeros_like(s_scr)

        lb = _lb_from_logits(lbl_ref[...])

        def chunk(c, _):
            rows = pl.ds(pl.multiple_of(c * CHUNK, CHUNK), CHUNK)
            pre = []
            for cs in cols:
                _, _, lf, kk, _, q = _hgrn_gates(hq_ref[rows, cs], hf_ref[rows, cs], lb[:, cs])
                v = hi_ref[rows, cs]
                b = _cumsum_rows(lf)
                a = _hgrn_intra(q, kk, b)[0]
                b_last = b[CHUNK - 1:CHUNK, :]
                pre.append((q * jnp.exp(b), a, v, jnp.exp(b_last), _mm_tn(v, kk * jnp.exp(b_last - b))))
            for h, cs in enumerate(cols):
                qe, a, v, ebl, upd = pre[h]
                st = s_scr[h]
                st_ref[h, c] = st
                o_ref[rows, cs] = _mm_nt(qe, st) + _mm(a, v)
                s_scr[h] = st * ebl + upd
            return 0

        lax.fori_loop(0, ncg, chunk, 0)

    col = lambda k: pl.BlockSpec((tg, HGRN_WIDTH), lambda g: (g, k))
    return pl.pallas_call(
        body, name="hgrn_fwd", grid=(ng,),
        in_specs=[pl.BlockSpec((2, HGRN_WIDTH), lambda g: (0, 0)), col(0), col(1), col(2)],
        out_specs=[col(0), pl.BlockSpec((HGRN_HEADS, ncg, HGRN_DIM, HGRN_DIM), lambda g: (0, g, 0, 0))],
        out_shape=[jax.ShapeDtypeStruct((T, HGRN_WIDTH), F32),
                   jax.ShapeDtypeStruct((HGRN_HEADS, T // CHUNK, HGRN_DIM, HGRN_DIM), F32)],
        scratch_shapes=[pltpu.VMEM((HGRN_HEADS, HGRN_DIM, HGRN_DIM), F32)],
        compiler_params=_params(("arbitrary",)),
    )(lbl, xph, xph, xph)


def _hgrn_bwd(xph, lbl, states, d_o, tg=512):
    T = xph.shape[0]
    ng, ncg = T // tg, tg // CHUNK
    cols = [slice(HGRN_DIM * h, HGRN_DIM * (h + 1)) for h in range(HGRN_HEADS)]
    nsub = CHUNK // SUB

    def body(lbl_ref, hq_ref, hf_ref, hi_ref, st_ref, do_ref, dhq_ref, dhf_ref, dhi_ref, dlg_ref, ds_scr, dlb_scr):
        g = pl.program_id(0)

        @pl.when(g == 0)
        def _():
            ds_scr[...] = jnp.zeros_like(ds_scr)
            dlb_scr[...] = jnp.zeros_like(dlb_scr)

        lb = _lb_from_logits(lbl_ref[...])

        def chunk(ci, _):
            c = ncg - 1 - ci
            rows = pl.ds(pl.multiple_of(c * CHUNK, CHUNK), CHUNK)
            pre = []
            for h, cs in enumerate(cols):
                hq = hq_ref[rows, cs]
                sig_f, f, lf, kk, sig_q, q = _hgrn_gates(hq, hf_ref[rows, cs], lb[:, cs])
                v = hi_ref[rows, cs]
                do = do_ref[rows, cs]
                b = _cumsum_rows(lf)
                eb = jnp.exp(b)
                a, causal, qs, ks, eqs, eks = _hgrn_intra(q, kk, b, exact=True)
                b_last = b[CHUNK - 1:CHUNK, :]
                st = st_ref[h, c]
                pre.append(dict(hq=hq, sig_f=sig_f, f=f, kk=kk, sig_q=sig_q, q=q, v=v, eb=eb, qs=qs, ks=ks, eqs=eqs,
                                eks=eks, ebl=jnp.exp(b_last), el=jnp.exp(b_last - b), st=st,
                                da=jnp.where(causal, _mm_nt(do, v, True), 0.0), dq=_mm(do, st, True) * eb,
                                dv=_mm_tn(a, do, True), dsu=_mm_tn(do, q * eb, True)))
            for w in pre:
                dq_rows = []
                dk = jnp.zeros_like(w["q"])
                for i in range(nsub):
                    dai = w["da"][SUB * i:SUB * (i + 1), :]
                    dq_rows.append(_mm(dai, w["ks"][i], True) * w["eqs"][i])
                    dk = dk + _mm_tn(dai, w["qs"][i], True) * w["eks"][i]
                w["dq"] = w["dq"] + jnp.concatenate(dq_rows, axis=0)
                w["dk"] = dk
            for h, cs in enumerate(cols):
                w = pre[h]
                kk, el, ebl, dst = w["kk"], w["el"], w["ebl"], ds_scr[h]
                dk_state = _mm(w["v"], dst, True) * el
                dk = w["dk"] + dk_state
                e_last = (ebl * jnp.sum(w["st"] * dst, axis=0, keepdims=True)
                          + jnp.sum(kk * dk_state, axis=0, keepdims=True))
                dlf = _rev_cumsum_rows(w["q"] * w["dq"] - kk * dk) + e_last
                ds_scr[h] = dst * ebl + w["dsu"]
                df = dlf / w["f"] - dk
                sig_f, sig_q = w["sig_f"], w["sig_q"]
                dhf_ref[rows, cs] = df * (1.0 - lb[:, cs]) * sig_f * (1.0 - sig_f)
                dlb_scr[:, cs] += jnp.sum(df * (1.0 - sig_f), axis=0, keepdims=True)
                dhq_ref[rows, cs] = w["dq"] * sig_q * (1.0 + w["hq"] * (1.0 - sig_q))
                dhi_ref[rows, cs] = w["dv"] + _mm_nt(kk * el, dst, True)
            return 0

        lax.fori_loop(0, ncg, chunk, 0)

        @pl.when(g == ng - 1)
        def _():
            dl0 = dlb_scr[...] * lb * (1.0 - lb)
            dlg_ref[...] = jnp.concatenate([dl0, -dl0], axis=0)

    col = lambda k: pl.BlockSpec((tg, HGRN_WIDTH), lambda g: (ng - 1 - g, k))
    logits = pl.BlockSpec((2, HGRN_WIDTH), lambda g: (0, 0))
    big = jax.ShapeDtypeStruct((T, HGRN_WIDTH), F32)
    return pl.pallas_call(
        body, name="hgrn_bwd", grid=(ng,),
        in_specs=[logits, col(0), col(1), col(2),
                  pl.BlockSpec((HGRN_HEADS, ncg, HGRN_DIM, HGRN_DIM), lambda g: (0, ng - 1 - g, 0, 0)), col(0)],
        out_specs=[col(0), col(0), col(0), logits],
        out_shape=[big, big, big, jax.ShapeDtypeStruct((2, HGRN_WIDTH), F32)],
        scratch_shapes=[pltpu.VMEM((HGRN_HEADS, HGRN_DIM, HGRN_DIM), F32), pltpu.VMEM((1, HGRN_WIDTH), F32)],
        compiler_params=_params(("arbitrary",)),
    )(lbl, xph, xph, xph, states, d_o)


def _ffn_fwd(x, o_raw, oh_raw, xph, tgt, wout, w_mla, w_hg, w_post, w_fpre, w_fpost, wg, wu, wd, tt=256):
    T = x.shape[0]
    nj = N_CHIPS

    def body(x_ref, o_ref, oh_ref, hg_ref, tgt_ref, wout_ref, wmla_ref, whg_ref, wpost_ref, wfpre_ref, wfpost_ref,
             wg_ref, wu_ref, wd_ref,
             h1_ref, y1_ref, z_ref, mix_ref, g_ref, up_ref, dy2_ref, dh2_ref, loss_ref, dwf_ref,
             z_scr, y2_scr):
        i, j = pl.program_id(0), pl.program_id(1)

        @pl.when((i == 0) & (j == 0))
        def _():
            loss_ref[...] = jnp.zeros_like(loss_ref)
            dwf_ref[...] = jnp.zeros_like(dwf_ref)

        @pl.when(j == 0)
        def _():
            om, _, _ = _grms_fwd(o_ref[...], wmla_ref[...], MLA_V)
            hg = hg_ref[...]
            ohn, _, _ = _grms_fwd(oh_ref[...], whg_ref[...], HGRN_DIM)
            mix = jnp.concatenate([om, ohn * (hg * jax.nn.sigmoid(hg))], axis=-1)
            mix_ref[...] = mix.astype(mix_ref.dtype)
            y1 = _mm(mix, wout_ref[...])
            y1_ref[...] = y1
            h1 = x_ref[...] + _rms_fwd(y1, wpost_ref[...])[0]
            h1_ref[...] = h1
            z = _rms_fwd(h1, wfpre_ref[...])[0].astype(z_scr.dtype)
            z_scr[...] = z
            z_ref[...] = z
            y2_scr[...] = jnp.zeros_like(y2_scr)

        z = z_scr[...]
        g = _mm_nt(z, wg_ref[0])
        up = _mm_nt(z, wu_ref[0])
        g_ref[0] = g
        up_ref[0] = up
        y2_scr[...] += _mm(g * jax.nn.sigmoid(g) * up, wd_ref[0])

        @pl.when(j == nj - 1)
        def _():
            w = wfpost_ref[...]
            y2s, y2n, r2 = _rms_fwd(y2_scr[...], w)
            e = h1_ref[...] + y2s - tgt_ref[...]
            loss_ref[...] += jnp.sum(e * e, axis=0, keepdims=True)
            dh2 = e * (1.0 / D_MODEL)
            dh2_ref[...] = dh2
            dy2, dwf = _rms_bwd(dh2, y2n, r2, w)
            dy2_ref[...] = dy2.astype(dy2_ref.dtype)
            dwf_ref[...] += dwf

    row = lambda w: pl.BlockSpec((tt, w), lambda i, j: (i, 0))
    full = lambda a: pl.BlockSpec(a.shape, lambda i, j: (0,) * a.ndim)
    vec = pl.BlockSpec((1, D_MODEL), lambda i, j: (0, 0))
    return pl.pallas_call(
        body, name="ffn_fwd", grid=(T // tt, nj),
        in_specs=[row(D_MODEL), row(MLA_WIDTH), row(HGRN_WIDTH),
                  pl.BlockSpec((tt, HGRN_WIDTH), lambda i, j: (i, 3)), row(D_MODEL), full(wout), full(w_mla),
                  full(w_hg), vec, vec, vec,
                  pl.BlockSpec((1, FF_SHARD, D_MODEL), lambda i, j: (j, 0, 0)),
                  pl.BlockSpec((1, FF_SHARD, D_MODEL), lambda i, j: (j, 0, 0)),
                  pl.BlockSpec((1, FF_SHARD, D_MODEL), lambda i, j: (j, 0, 0))],
        out_specs=[row(D_MODEL), row(D_MODEL), row(D_MODEL), row(D_MODEL),
                   pl.BlockSpec((1, tt, FF_SHARD), lambda i, j: (j, i, 0)),
                   pl.BlockSpec((1, tt, FF_SHARD), lambda i, j: (j, i, 0)),
                   row(D_MODEL), row(D_MODEL), vec, vec],
        out_shape=[jax.ShapeDtypeStruct((T, D_MODEL), F32), jax.ShapeDtypeStruct((T, D_MODEL), F32),
                   jax.ShapeDtypeStruct((T, D_MODEL), MXU_DTYPE), jax.ShapeDtypeStruct((T, D_MODEL), MXU_DTYPE),
                   jax.ShapeDtypeStruct((nj, T, FF_SHARD), F32), jax.ShapeDtypeStruct((nj, T, FF_SHARD), F32),
                   jax.ShapeDtypeStruct((T, D_MODEL), MXU_DTYPE), jax.ShapeDtypeStruct((T, D_MODEL), F32),
                   jax.ShapeDtypeStruct((1, D_MODEL), F32), jax.ShapeDtypeStruct((1, D_MODEL), F32)],
        scratch_shapes=[pltpu.VMEM((tt, D_MODEL), MXU_DTYPE), pltpu.VMEM((tt, D_MODEL), F32)],
        compiler_params=_params(("arbitrary", "arbitrary")),
    )(x, o_raw, oh_raw, xph, tgt, wout, w_mla, w_hg, w_post, w_fpre, w_fpost, wg, wu, wd)


def _ffn_bwd(zb, g, up, dy2b, wg, wu, wd, tt=512):
    T = zb.shape[0]
    nj = N_CHIPS

    def body(z_ref, g_ref, up_ref, dy2_ref, wg_ref, wu_ref, wd_ref, dwg_ref, dwu_ref, dwd_ref, dz_ref):
        @pl.when(pl.program_id(1) == 0)
        def _():
            dwg_ref[...] = jnp.zeros_like(dwg_ref)
            dwu_ref[...] = jnp.zeros_like(dwu_ref)
            dwd_ref[...] = jnp.zeros_like(dwd_ref)

        z, g_, up_, dy2 = z_ref[...], g_ref[0], up_ref[0], dy2_ref[...]
        sg = jax.nn.sigmoid(g_)
        act = g_ * sg
        dff = _mm_nt(dy2, wd_ref[0])
        dwd_ref[0] += _mm_tn(act * up_, dy2)
        dg = dff * up_ * sg * (1.0 + g_ * (1.0 - sg))
        dup = dff * act
        dwg_ref[0] += _mm_tn(dg, z)
        dwu_ref[0] += _mm_tn(dup, z)
        dz_ref[0] = _mm(dg, wg_ref[0]) + _mm(dup, wu_ref[0])

    row = pl.BlockSpec((tt, D_MODEL), lambda j, i: (i, 0))
    act = pl.BlockSpec((1, tt, FF_SHARD), lambda j, i: (j, i, 0))
    w_sh = pl.BlockSpec((1, FF_SHARD, D_MODEL), lambda j, i: (j, 0, 0))
    w_grad = jax.ShapeDtypeStruct((nj, FF_SHARD, D_MODEL), F32)
    return pl.pallas_call(
        body, name="ffn_bwd", grid=(nj, T // tt),
        in_specs=[row, act, act, row, w_sh, w_sh, w_sh],
        out_specs=[w_sh, w_sh, w_sh, pl.BlockSpec((1, tt, D_MODEL), lambda j, i: (j, i, 0))],
        out_shape=[w_grad, w_grad, w_grad, jax.ShapeDtypeStruct((nj, T, D_MODEL), F32)],
        compiler_params=_params(("arbitrary", "arbitrary")),
    )(zb, g, up, dy2b, wg, wu, wd)


def _mid_bwd(dzp, dh2, h1, y1, mixb, o_raw, oh_raw, xph, wout, w_fpre, w_post, w_mla, w_hg, tt=256):
    T = dh2.shape[0]

    def body(dzp_ref, dh2_ref, h1_ref, y1_ref, mix_ref, o_ref, oh_ref, hg_ref, wout_ref, wfpre_ref, wpost_ref,
             wmla_ref, whg_ref,
             dh1_ref, dwout_ref, do_ref, doh_ref, dhg_ref, dvec_ref, dwfpre_ref, dwpost_ref, dwmla_ref, dwhg_ref):
        @pl.when(pl.program_id(0) == 0)
        def _():
            for r in (dwout_ref, dwfpre_ref, dwpost_ref, dwmla_ref, dwhg_ref):
                r[...] = jnp.zeros_like(r)

        dz = dzp_ref[0] + dzp_ref[1] + dzp_ref[2] + dzp_ref[3]
        wfpre = wfpre_ref[...]
        _, h1n, r = _rms_fwd(h1_ref[...], wfpre)
        dh1_z, dwfpre = _rms_bwd(dz, h1n, r, wfpre)
        dwfpre_ref[...] += dwfpre
        dh1 = dh2_ref[...] + dh1_z
        dh1_ref[...] = dh1
        wpost = wpost_ref[...]
        _, y1n, r1 = _rms_fwd(y1_ref[...], wpost)
        dy1, dwpost = _rms_bwd(dh1, y1n, r1, wpost)
        dwpost_ref[...] += dwpost
        dmix = _mm_nt(dy1, wout_ref[...])
        dwout_ref[...] += _mm_tn(mix_ref[...], dy1)
        wmla = wmla_ref[...]
        o = o_ref[...]
        _, on, ro = _grms_fwd(o, wmla, MLA_V)
        d_o, dwmla = _grms_bwd(dmix[:, :MLA_WIDTH], on, ro, wmla, MLA_V)
        dwmla_ref[...] += dwmla
        do_ref[...] = d_o.astype(do_ref.dtype)
        hh = lax.broadcasted_iota(jnp.int32, (MLA_HEADS, MLA_WIDTH), 0)
        ll = lax.broadcasted_iota(jnp.int32, (MLA_HEADS, MLA_WIDTH), 1)
        sel = jnp.where((ll >= hh * MLA_V) & (ll < (hh + 1) * MLA_V), 1.0, 0.0)
        dvec_ref[...] = _mm_nt(sel, d_o * o, True)
        whg = whg_ref[...]
        hg = hg_ref[...]
        sg = jax.nn.sigmoid(hg)
        _, ohn, rh = _grms_fwd(oh_ref[...], whg, HGRN_DIM)
        dmh = dmix[:, MLA_WIDTH:]
        dhg_ref[...] = dmh * ohn * whg * sg * (1.0 + hg * (1.0 - sg))
        d_oh, dwhg = _grms_bwd(dmh * (hg * sg), ohn, rh, whg, HGRN_DIM)
        dwhg_ref[...] += dwhg
        doh_ref[...] = d_oh

    row = lambda w: pl.BlockSpec((tt, w), lambda i: (i, 0))
    full = lambda a: pl.BlockSpec(a.shape, lambda i: (0,) * a.ndim)
    vec = lambda w: pl.BlockSpec((1, w), lambda i: (0, 0))
    sds = jax.ShapeDtypeStruct
    return pl.pallas_call(
        body, name="mid_bwd", grid=(T // tt,),
        in_specs=[pl.BlockSpec((N_CHIPS, tt, D_MODEL), lambda i: (0, i, 0)), row(D_MODEL), row(D_MODEL), row(D_MODEL),
                  row(D_MODEL), row(MLA_WIDTH), row(HGRN_WIDTH), pl.BlockSpec((tt, HGRN_WIDTH), lambda i: (i, 3)),
                  full(wout), vec(D_MODEL), vec(D_MODEL), vec(MLA_WIDTH), vec(HGRN_WIDTH)],
        out_specs=[row(D_MODEL), full(wout), row(MLA_WIDTH), row(HGRN_WIDTH), row(HGRN_WIDTH),
                   pl.BlockSpec((MLA_HEADS, tt), lambda i: (0, i)),
                   vec(D_MODEL), vec(D_MODEL), vec(MLA_WIDTH), vec(HGRN_WIDTH)],
        out_shape=[sds((T, D_MODEL), F32), sds(wout.shape, F32), sds((T, MLA_WIDTH), MXU_DTYPE), sds((T, HGRN_WIDTH), F32),
                   sds((T, HGRN_WIDTH), F32), sds((MLA_HEADS, T), F32),
                   sds((1, D_MODEL), F32), sds((1, D_MODEL), F32), sds((1, MLA_WIDTH), F32), sds((1, HGRN_WIDTH), F32)],
        compiler_params=_params(("arbitrary",)),
    )(dzp, dh2, h1, y1, mixb, o_raw, oh_raw, xph, wout, w_fpre, w_post, w_mla, w_hg)


def _in_bwd(x, dh1, cq, ckv, dq, dk, dv, dhq, dhf, dhi, dhg, rc, rs, w_pre, win, qnw, wq, kvnw, wk, wv, tt=256):
    T = x.shape[0]

    def body(x_ref, dh1_ref, cq_ref, ckv_ref, dq_ref, dk_ref, dv_ref, dhq_ref, dhf_ref, dhi_ref, dhg_ref, rc_ref, rs_ref,
             wpre_ref, win_ref, qnw_ref, wq_ref, kvnw_ref, wk_ref, wv_ref,
             dx_ref, dwin_ref, dwq_ref, dwk_ref, dwv_ref, dwpre_ref, dqnw_ref, dkvnw_ref):
        @pl.when(pl.program_id(0) == 0)
        def _():
            for r in (dwin_ref, dwq_ref, dwk_ref, dwv_ref, dwpre_ref, dqnw_ref, dkvnw_ref):
                r[...] = jnp.zeros_like(r)

        c, sa, sb = _rope_tables(rc_ref[...], rs_ref[...])
        lane = lax.broadcasted_iota(jnp.int32, (tt, HEAD_PAD), 1)
        dk_all = dk_ref[...]
        dq_lin = []
        dkr = jnp.zeros((tt, HEAD_PAD), F32)
        for h in range(MLA_HEADS):
            sl = slice(HEAD_PAD * h, HEAD_PAD * (h + 1))
            dq_lin.append(_rope_bwd(dq_ref[sl, :].T, c, sa, sb))
            dkr = dkr + dk_all[:, sl]
        dq_lin = jnp.concatenate(dq_lin, axis=-1)
        dkr = jnp.where((lane >= MLA_NOPE) & (lane < MLA_QK), _rope_bwd(dkr, c, sa, sb), 0.0)
        qnw = qnw_ref[...]
        qn, cqn, rq = _rms_fwd(cq_ref[...], qnw)
        dwq_ref[...] += _mm_tn(qn, dq_lin)
        dcq, dqnw = _rms_bwd(_mm_nt(dq_lin, wq_ref[...]), cqn, rq, qnw)
        dqnw_ref[...] += dqnw
        kvnw = kvnw_ref[...]
        kvn, ckvn, rkv = _rms_fwd(ckv_ref[...], kvnw)
        dv_ = dv_ref[...]
        dwk_ref[...] += _mm_tn(kvn, dk_all)
        dwv_ref[...] += _mm_tn(kvn, dv_)
        dckv, dkvnw = _rms_bwd(_mm_nt(dk_all, wk_ref[...]) + _mm_nt(dv_, wv_ref[...]), ckvn, rkv, kvnw)
        dkvnw_ref[...] += dkvnw
        dxp = jnp.concatenate([dcq, dckv, dkr, dhq_ref[...], dhf_ref[...], dhi_ref[...], dhg_ref[...]], axis=-1)
        wpre = wpre_ref[...]
        u, xn, rx = _rms_fwd(x_ref[...], wpre)
        dwin_ref[...] += _mm_tn(dxp, u)
        dx_u, dwpre = _rms_bwd(_mm(dxp, win_ref[...]), xn, rx, wpre)
        dwpre_ref[...] += dwpre
        dx_ref[...] = dh1_ref[...] + dx_u

    row = lambda w: pl.BlockSpec((tt, w), lambda i: (i, 0))
    full = lambda a: pl.BlockSpec(a.shape, lambda i: (0,) * a.ndim)
    sds = jax.ShapeDtypeStruct
    qk_w = MLA_HEADS * HEAD_PAD
    return pl.pallas_call(
        body, name="in_bwd", grid=(T // tt,),
        in_specs=[row(D_MODEL), row(D_MODEL), row(Q_RANK), row(KV_RANK), pl.BlockSpec((qk_w, tt), lambda i: (0, i)),
                  row(qk_w), row(MLA_WIDTH),
                  row(HGRN_WIDTH), row(HGRN_WIDTH), row(HGRN_WIDTH), row(HGRN_WIDTH), row(HEAD_PAD), row(HEAD_PAD),
                  full(w_pre), full(win), full(qnw), full(wq), full(kvnw), full(wk), full(wv)],
        out_specs=[row(D_MODEL), full(win), full(wq), full(wk), full(wv), full(w_pre), full(qnw), full(kvnw)],
        out_shape=[sds((T, D_MODEL), F32), sds(win.shape, F32), sds(wq.shape, F32), sds(wk.shape, F32),
                   sds(wv.shape, F32), sds(w_pre.shape, F32), sds(qnw.shape, F32), sds(kvnw.shape, F32)],
        compiler_params=_params(("arbitrary",)),
    )(x, dh1, cq, ckv, dq, dk, dv, dhq, dhf, dhi, dhg, rc, rs, w_pre, win, qnw, wq, kvnw, wk, wv)


def _arrange_weights(win_t, wuq_full, wukv):
    dt = win_t.dtype
    z = lambda n: jnp.zeros((n, D_MODEL), dt)
    s2 = Q_RANK + KV_RANK
    win_arr = jnp.concatenate([win_t[:s2], z(MLA_NOPE), win_t[s2:s2 + MLA_ROPE], z(HEAD_PAD - MLA_QK),
                               win_t[s2 + MLA_ROPE:]], axis=0)
    wq_arr = jnp.pad(wuq_full, ((0, 0), (0, 0), (0, HEAD_PAD - MLA_QK))).reshape(Q_RANK, MLA_HEADS * HEAD_PAD)
    wk_arr = jnp.pad(wukv[:, :, :MLA_NOPE], ((0, 0), (0, 0), (0, HEAD_PAD - MLA_NOPE))).reshape(
        KV_RANK, MLA_HEADS * HEAD_PAD)
    wv_arr = wukv[:, :, MLA_NOPE:].reshape(KV_RANK, MLA_WIDTH)
    return win_arr, wq_arr, wk_arr, wv_arr


def _unarrange_grads(dwin_arr, dwq_arr, dwk_arr, dwv_arr):
    s2 = Q_RANK + KV_RANK
    dwin = jnp.concatenate([dwin_arr[:s2], dwin_arr[s2 + MLA_NOPE:s2 + MLA_QK], dwin_arr[s2 + HEAD_PAD:]], axis=0)
    dwuq = dwq_arr.reshape(Q_RANK, MLA_HEADS, HEAD_PAD)[:, :, :MLA_QK]
    dwukv = jnp.concatenate([dwk_arr.reshape(KV_RANK, MLA_HEADS, HEAD_PAD)[:, :, :MLA_NOPE],
                             dwv_arr.reshape(KV_RANK, MLA_HEADS, MLA_V)], axis=-1)
    return dwin, dwuq, dwukv


def _rope_inv_freq():
    inv = 1.0 / (ROPE_THETA ** (jnp.arange(0, MLA_ROPE, 2, dtype=F32) / MLA_ROPE))
    z = lambda n: jnp.zeros((n,), F32)
    return jnp.concatenate([z(MLA_NOPE), inv, inv, z(HEAD_PAD - MLA_QK)]).reshape(1, HEAD_PAD)


def _local_step(x, pos, tgt, small, win_arr, wq_arr, wk_arr, wv_arr, wout, wg, wu, wd):
    invf = _rope_inv_freq()
    cq, ckv, xph, qb, kb, vb, kt, vt, rc, rs = _in_fwd(x, pos, invf, small["attn_pre_norm"], win_arr, small["mla_q_norm"],
                                               wq_arr, small["mla_kv_norm"], wk_arr, wv_arr)
    o_raw, lse = _attn_fwd_t(qb, kb, vt)
    oh_raw, states = _hgrn_fwd(xph, small["hgrn_lb_logits"])
    h1, y1, zb, mixb, g, up, dy2b, dh2, loss_acc, d_fpost = _ffn_fwd(
        x, o_raw, oh_raw, xph, tgt, wout, small["mla_out_norm"], small["hgrn_out_norm"], small["attn_post_norm"],
        small["ffn_pre_norm"], small["ffn_post_norm"], wg, wu, wd)
    dwg, dwu, dwd, dzp = _ffn_bwd(zb, g, up, dy2b, wg, wu, wd)
    dh1, dwout, d_o, d_oh, dhg, dvec, d_fpre, d_post, d_mla, d_hg = _mid_bwd(
        dzp, dh2, h1, y1, mixb, o_raw, oh_raw, xph, wout, small["ffn_pre_norm"], small["attn_post_norm"],
        small["mla_out_norm"], small["hgrn_out_norm"])
    dq, dk, dv = _attn_bwd_t(qb, kb, kt, vb, d_o, lse, dvec.reshape(lse.shape))
    dhq, dhf, dhi, d_lbl = _hgrn_bwd(xph, small["hgrn_lb_logits"], states, d_oh)
    dx, dwin_arr, dwq_arr, dwk_arr, dwv_arr, d_pre, d_qn, d_kvn = _in_bwd(
        x, dh1, cq, ckv, dq, dk, dv, dhq, dhf, dhi, dhg, rc, rs, small["attn_pre_norm"], win_arr,
        small["mla_q_norm"], wq_arr, small["mla_kv_norm"], wk_arr, wv_arr)
    dwin, dwuq, dwukv = _unarrange_grads(dwin_arr, dwq_arr, dwk_arr, dwv_arr)
    loss = 0.5 * jnp.sum(loss_acc) * (1.0 / D_MODEL)
    grads = dict(attn_pre_norm=d_pre, w_in=dwin, mla_q_norm=d_qn, mla_w_uq=dwuq, mla_kv_norm=d_kvn, mla_w_ukv=dwukv,
                 mla_out_norm=d_mla, hgrn_lb_logits=d_lbl, hgrn_out_norm=d_hg, w_out=dwout, attn_post_norm=d_post,
                 ffn_pre_norm=d_fpre, w_gate=dwg, w_up=dwu, w_down=dwd, ffn_post_norm=d_fpost)
    return loss, dx, grads


def _place():
    x, y, c = lax.axis_index("x"), lax.axis_index("y"), lax.axis_index("c")
    others = [(1 - x, y), (x, 1 - y), (1 - x, 1 - y)]
    return x, y, c, 2 * x + y, (x, y, 1 - c), others


def _half(ref, c, rows):
    return ref.at[pl.ds(pl.multiple_of(c * rows, 8), rows)]


def _gather_chips(arrs, name):
    n = len(arrs)
    halves = [a.shape[0] // 2 for a in arrs]

    def body(*refs):
        ins, outs = refs[:n], refs[n:2 * n]
        send, recv = refs[2 * n:]
        x, y, c, me, sib, others = _place()

        def rcopy(k, src, dst, to):
            return pltpu.make_async_remote_copy(src_ref=src, dst_ref=dst, send_sem=send.at[k], recv_sem=recv.at[k],
                                                device_id=to, device_id_type=MESH)

        first = []
        for j, (px, py) in enumerate(others):
            for a in range(n):
                first.append(rcopy(j * n + a, _half(ins[a], c, halves[a]), _half(outs[a].at[me], c, halves[a]),
                                   (px, py, c)))
        for cp in first:
            cp.start()
        passed = []
        for j, (px, py) in enumerate(others):
            chip = 2 * px + py
            for a in range(n):
                part = _half(outs[a].at[chip], c, halves[a])
                rcopy(j * n + a, part, part, (px, py, c)).wait_recv()
                fw = rcopy(3 * n + j * n + a, part, part, sib)
                fw.start()
                passed.append(fw)
        for j, (px, py) in enumerate(others):
            chip = 2 * px + py
            for a in range(n):
                part = _half(outs[a].at[chip], 1 - c, halves[a])
                rcopy(3 * n + j * n + a, part, part, sib).wait_recv()
        for cp in first + passed:
            cp.wait_send()

    return pl.pallas_call(
        body, name=name,
        in_specs=[ANY] * n, out_specs=[ANY] * n,
        out_shape=[jax.ShapeDtypeStruct((N_CHIPS,) + a.shape, a.dtype) for a in arrs],
        scratch_shapes=[pltpu.SemaphoreType.DMA((6 * n,)), pltpu.SemaphoreType.DMA((6 * n,))],
    )(*arrs)


GRAD_BLOCKS = 2


def _pair_swap(gs, sm):
    n = len(gs)

    def body(*refs):
        g_refs, sm_ref = refs[:n], refs[n]
        r_refs, ssib_ref = refs[n + 1:2 * n + 1], refs[2 * n + 1]
        send, recv = refs[2 * n + 2:]
        x, y, c, me, sib, others = _place()
        copies = []
        for a in range(n):
            h = gs[a].shape[1] // 2
            copies.append(pltpu.make_async_remote_copy(
                src_ref=g_refs[a].at[:, pl.ds(pl.multiple_of((1 - c) * h, 8), h)], dst_ref=r_refs[a],
                send_sem=send.at[a], recv_sem=recv.at[a], device_id=sib, device_id_type=MESH))
        copies.append(pltpu.make_async_remote_copy(src_ref=sm_ref, dst_ref=ssib_ref, send_sem=send.at[n],
                                                   recv_sem=recv.at[n], device_id=sib, device_id_type=MESH))
        for cp in copies:
            cp.start()
        for cp in copies:
            cp.wait()

    return pl.pallas_call(
        body, name="pair_swap", in_specs=[ANY] * (n + 1), out_specs=[ANY] * (n + 1),
        out_shape=[jax.ShapeDtypeStruct((N_CHIPS, g.shape[1] // 2, g.shape[2]), g.dtype) for g in gs]
        + [jax.ShapeDtypeStruct(sm.shape, sm.dtype)],
        scratch_shapes=[pltpu.SemaphoreType.DMA((n + 1,)), pltpu.SemaphoreType.DMA((n + 1,))],
    )(*gs, sm)


def _pair_sum(place, gs, rs, sm, ssib):
    n = len(gs)
    nb = GRAD_BLOCKS

    def body(place_ref, *refs):
        g_refs, r_refs = refs[:n], refs[n:2 * n]
        sm_ref, ss_ref = refs[2 * n], refs[2 * n + 1]
        p_refs, pair_ref = refs[2 * n + 2:3 * n + 2], refs[3 * n + 2]
        for a in range(n):
            p_refs[a][0] = (g_refs[a][0] + r_refs[a][0]).astype(p_refs[a].dtype)

        @pl.when((pl.program_id(0) == 0) & (pl.program_id(1) == 0))
        def _():
            pair_ref[...] = sm_ref[...] + ss_ref[...]

    in_specs, out_specs, out_shape = [], [], []
    for g in gs:
        blk = (1, g.shape[1] // 2 // nb, g.shape[2])
        in_specs.append(pl.BlockSpec(blk, lambda i, k, p: (k, p[0] * nb + i, 0)))
    for g in gs:
        blk = (1, g.shape[1] // 2 // nb, g.shape[2])
        in_specs.append(pl.BlockSpec(blk, lambda i, k, p: (k, i, 0)))
        out_specs.append(pl.BlockSpec(blk, lambda i, k, p: (k, i, 0)))
        out_shape.append(jax.ShapeDtypeStruct((N_CHIPS, g.shape[1] // 2, g.shape[2]), BF16))
    sm_spec = pl.BlockSpec(sm.shape, lambda i, k, p: (0, 0))
    return pl.pallas_call(
        body, name="pair_sum",
        grid_spec=pltpu.PrefetchScalarGridSpec(num_scalar_prefetch=1, grid=(nb, N_CHIPS),
                                               in_specs=in_specs + [sm_spec, sm_spec],
                                               out_specs=out_specs + [sm_spec]),
        out_shape=out_shape + [jax.ShapeDtypeStruct(sm.shape, F32)],
        compiler_params=_params(("arbitrary", "arbitrary")),
    )(place, *gs, *rs, sm, ssib)


def _chip_swap(ps, pair):
    n = len(ps)
    hs = SMALL_ROWS // 2

    def body(*refs):
        p_refs, pair_ref = refs[:n], refs[n]
        ri_refs, sm4_ref = refs[n + 1:2 * n + 1], refs[2 * n + 1]
        send, recv, lsem = refs[2 * n + 2:]
        x, y, c, me, sib, others = _place()
        local = pltpu.make_async_copy(pair_ref, sm4_ref.at[me], lsem.at[0])
        local.start()
        copies = []
        for j, (px, py) in enumerate(others):
            chip = 2 * px + py
            for a in range(n):
                copies.append(pltpu.make_async_remote_copy(
                    src_ref=p_refs[a].at[chip], dst_ref=ri_refs[a].at[j], send_sem=send.at[j * (n + 1) + a],
                    recv_sem=recv.at[j * (n + 1) + a], device_id=(px, py, c), device_id_type=MESH))
            copies.append(pltpu.make_async_remote_copy(
                src_ref=_half(pair_ref, c, hs), dst_ref=_half(sm4_ref.at[me], c, hs), send_sem=send.at[j * (n + 1) + n],
                recv_sem=recv.at[j * (n + 1) + n], device_id=(px, py, c), device_id_type=MESH))
        for cp in copies:
            cp.start()
        for j, (px, py) in enumerate(others):
            chip = 2 * px + py
            for a in range(n):
                pltpu.make_async_remote_copy(
                    src_ref=p_refs[a].at[chip], dst_ref=ri_refs[a].at[j], send_sem=send.at[j * (n + 1) + a],
                    recv_sem=recv.at[j * (n + 1) + a], device_id=(px, py, c), device_id_type=MESH).wait_recv()
            part = _half(sm4_ref.at[chip], c, hs)
            pltpu.make_async_remote_copy(src_ref=part, dst_ref=part, send_sem=send.at[j * (n + 1) + n],
                                         recv_sem=recv.at[j * (n + 1) + n], device_id=(px, py, c),
                                         device_id_type=MESH).wait_recv()
        for cp in copies:
            cp.wait_send()
        local.wait()

    k = 3 * (n + 1)
    return pl.pallas_call(
        body, name="chip_swap", in_specs=[ANY] * (n + 1), out_specs=[ANY] * (n + 1),
        out_shape=[jax.ShapeDtypeStruct((3,) + p.shape[1:], p.dtype) for p in ps]
        + [jax.ShapeDtypeStruct((N_CHIPS,) + pair.shape, pair.dtype)],
        scratch_shapes=[pltpu.SemaphoreType.DMA((k,)), pltpu.SemaphoreType.DMA((k,)), pltpu.SemaphoreType.DMA((1,))],
    )(*ps, pair)


def _chip_sum(place, gs, rs, ris):
    n = len(gs)
    nb = GRAD_BLOCKS

    def body(place_ref, *refs):
        g_refs, r_refs, ri_refs, o_refs = refs[:n], refs[n:2 * n], refs[2 * n:3 * n], refs[3 * n:]
        for a in range(n):
            ri = ri_refs[a]
            o_refs[a][...] = (g_refs[a][0] + r_refs[a][0]) + ri[0].astype(F32) + ri[1].astype(F32) + ri[2].astype(F32)

    in_specs, out_specs, out_shape = [], [], []
    for g in gs:
        blk = (1, g.shape[1] // 2 // nb, g.shape[2])
        in_specs.append(pl.BlockSpec(blk, lambda i, p: (p[1], p[0] * nb + i, 0)))
    for g in gs:
        blk = (1, g.shape[1] // 2 // nb, g.shape[2])
        in_specs.append(pl.BlockSpec(blk, lambda i, p: (p[1], i, 0)))
    for g in gs:
        rb = g.shape[1] // 2 // nb
        in_specs.append(pl.BlockSpec((3, rb, g.shape[2]), lambda i, p: (0, i, 0)))
        out_specs.append(pl.BlockSpec((rb, g.shape[2]), lambda i, p: (p[0] * nb + i, 0)))
        out_shape.append(jax.ShapeDtypeStruct(g.shape[1:], F32))
    return pl.pallas_call(
        body, name="chip_sum",
        grid_spec=pltpu.PrefetchScalarGridSpec(num_scalar_prefetch=1, grid=(nb,), in_specs=in_specs, out_specs=out_specs),
        out_shape=out_shape,
        compiler_params=_params(("arbitrary",)),
    )(place, *gs, *rs, *ris)


def _pair_fill(gfs, sm4):
    n = len(gfs)
    hs = SMALL_ROWS // 2

    def body(*refs):
        g_refs, sm4_ref = refs[n + 1:2 * n + 1], refs[2 * n + 1]
        send, recv = refs[2 * n + 2:]
        x, y, c, me, sib, others = _place()
        copies, waits = [], []
        for a in range(n):
            h = gfs[a].shape[0] // 2
            mine, theirs = _half(g_refs[a], c, h), _half(g_refs[a], 1 - c, h)
            copies.append(pltpu.make_async_remote_copy(src_ref=mine, dst_ref=mine, send_sem=send.at[a],
                                                       recv_sem=recv.at[a], device_id=sib, device_id_type=MESH))
            waits.append(pltpu.make_async_remote_copy(src_ref=theirs, dst_ref=theirs, send_sem=send.at[a],
                                                      recv_sem=recv.at[a], device_id=sib, device_id_type=MESH))
        for j, (px, py) in enumerate(others):
            chip = 2 * px + py
            mine, theirs = _half(sm4_ref.at[chip], c, hs), _half(sm4_ref.at[chip], 1 - c, hs)
            copies.append(pltpu.make_async_remote_copy(src_ref=mine, dst_ref=mine, send_sem=send.at[n + j],
                                                       recv_sem=recv.at[n + j], device_id=sib, device_id_type=MESH))
            waits.append(pltpu.make_async_remote_copy(src_ref=theirs, dst_ref=theirs, send_sem=send.at[n + j],
                                                      recv_sem=recv.at[n + j], device_id=sib, device_id_type=MESH))
        for cp in copies:
            cp.start()
        for w in waits:
            w.wait_recv()
        for cp in copies:
            cp.wait_send()

    return pl.pallas_call(
        body, name="pair_fill", in_specs=[ANY] * (n + 1), out_specs=[ANY] * (n + 1),
        out_shape=[jax.ShapeDtypeStruct(g.shape, g.dtype) for g in gfs] + [jax.ShapeDtypeStruct(sm4.shape, sm4.dtype)],
        input_output_aliases={i: i for i in range(n + 1)},
        scratch_shapes=[pltpu.SemaphoreType.DMA((n + 3,)), pltpu.SemaphoreType.DMA((n + 3,))],
    )(*gfs, sm4)


def _adamw_math(w, g, m, v):
    m = ADAM_B1 * m + (1.0 - ADAM_B1) * g
    v = ADAM_B2 * v + (1.0 - ADAM_B2) * (g * g)
    m_hat = m / (1.0 - ADAM_B1 ** ADAM_STEP)
    v_hat = v / (1.0 - ADAM_B2 ** ADAM_STEP)
    return -ADAM_LR * (m_hat / (jnp.sqrt(v_hat) + ADAM_EPS) + ADAM_WD * w), m, v


def _adamw(w, g, m, v, rb, name):
    rows, cols = w.shape

    def body(w_ref, g_ref, m_ref, v_ref, d_ref, mo_ref, vo_ref):
        d, mo, vo = _adamw_math(w_ref[...], g_ref[...], m_ref[...], v_ref[...])
        d_ref[...] = d
        mo_ref[...] = mo
        vo_ref[...] = vo

    spec = pl.BlockSpec((rb, cols), lambda i: (i, 0))
    return pl.pallas_call(
        body, name=name, grid=(rows // rb,), in_specs=[spec] * 4, out_specs=[spec] * 3,
        out_shape=[jax.ShapeDtypeStruct(w.shape, F32)] * 3,
        compiler_params=_params(("arbitrary",)),
    )(w, g, m, v)


def _adamw_small(sm4, w, m, v):
    def body(sm4_ref, w_ref, m_ref, v_ref, g_ref, d_ref, mo_ref, vo_ref):
        g = ((sm4_ref[0] + sm4_ref[1]) + sm4_ref[2]) + sm4_ref[3]
        g_ref[...] = g
        d, mo, vo = _adamw_math(w_ref[...], g, m_ref[...], v_ref[...])
        d_ref[...] = d
        mo_ref[...] = mo
        vo_ref[...] = vo

    return pl.pallas_call(
        body, name="adamw_small", out_shape=[jax.ShapeDtypeStruct(w.shape, F32)] * 4,
        compiler_params=pltpu.CompilerParams(vmem_limit_bytes=VMEM_LIMIT),
    )(sm4, w, m, v)


SMALL_NAMES = ("attn_pre_norm", "mla_q_norm", "mla_kv_norm", "mla_w_ukv", "mla_out_norm", "hgrn_lb_logits",
               "hgrn_out_norm", "attn_post_norm", "ffn_pre_norm", "ffn_post_norm")
BIG_NAMES = ("w_in", "mla_w_uq", "w_out", "w_gate", "w_up", "w_down")
WEIGHT_NAMES = ("attn_pre_norm", "w_in", "mla_q_norm", "mla_w_uq", "mla_kv_norm", "mla_w_ukv", "mla_out_norm",
                "hgrn_lb_logits", "hgrn_out_norm", "w_out", "attn_post_norm", "ffn_pre_norm", "w_gate", "w_up", "w_down",
                "ffn_post_norm")


UQ_COMM_SHAPE = (192, 384)


def _pack_small(vals, extra=None):
    parts = [vals[n].reshape(-1) for n in SMALL_NAMES]
    if extra is not None:
        parts.append(extra.reshape(1))
    flat = jnp.concatenate(parts)
    return jnp.pad(flat, (0, SMALL_ROWS * D_MODEL - flat.shape[0])).reshape(SMALL_ROWS, D_MODEL)


def _unpack_small(buf, shapes):
    flat = buf.reshape(-1)
    out, off = {}, 0
    for n, size in zip(SMALL_NAMES, SMALL_SIZES):
        out[n] = flat[off:off + size].reshape(shapes[n])
        off += size
    return out


def kernel(x, positions, attn_pre_norm, w_in, mla_q_norm, mla_w_uq, mla_kv_norm, mla_w_ukv, mla_out_norm, hgrn_lb_logits, hgrn_out_norm, w_out, attn_post_norm, ffn_pre_norm, w_gate, w_up, w_down, ffn_post_norm, loss_target, m_attn_pre_norm, m_w_in, m_mla_q_norm, m_mla_w_uq, m_mla_kv_norm, m_mla_w_ukv, m_mla_out_norm, m_hgrn_lb_logits, m_hgrn_out_norm, m_w_out, m_attn_post_norm, m_ffn_pre_norm, m_w_gate, m_w_up, m_w_down, m_ffn_post_norm, v_attn_pre_norm, v_w_in, v_mla_q_norm, v_mla_w_uq, v_mla_kv_norm, v_mla_w_ukv, v_mla_out_norm, v_hgrn_lb_logits, v_hgrn_out_norm, v_w_out, v_attn_post_norm, v_ffn_pre_norm, v_w_gate, v_w_up, v_w_down, v_ffn_post_norm):
    args = locals()
    W = {n: args[n] for n in WEIGHT_NAMES}
    M = {n: args["m_" + n] for n in WEIGHT_NAMES}
    V = {n: args["v_" + n] for n in WEIGHT_NAMES}
    T = x.shape[1]
    cx, cy, cc = lax.axis_index("x"), lax.axis_index("y"), lax.axis_index("c")

    win_rows = D_IN // N_CHIPS
    shard2d = {"w_in": (win_rows, D_MODEL), "mla_w_uq": (Q_RANK // N_CHIPS, MLA_HEADS * MLA_QK),
               "w_out": (D_MODEL // N_CHIPS, D_MODEL), "w_gate": (FF_SHARD, D_MODEL), "w_up": (FF_SHARD, D_MODEL),
               "w_down": (FF_SHARD, D_MODEL)}
    transposed = ("w_in", "w_gate", "w_up")
    to2d = lambda n, a: a[0].T if n in transposed else a.reshape(shard2d[n])
    from2d = lambda n, t: t.T[None] if n in transposed else t.reshape(W[n].shape)
    me = 2 * cx + cy
    local_b = [to2d(n, W[n]).astype(BF16) for n in BIG_NAMES]
    local_b[0] = jnp.pad(local_b[0], ((0, FF_SHARD - win_rows), (0, 0)))
    stacks = _gather_chips(local_b, "gather_weights")
    win4, wuq4, wout4, wg4, wu4, wd4 = [lax.dynamic_update_slice(s, l[None], (me, 0, 0))
                                        for s, l in zip(stacks, local_b)]
    win_t = win4[:, :win_rows].reshape(D_IN, D_MODEL)
    wuq_full = wuq4.reshape(Q_RANK, MLA_HEADS, MLA_QK)
    win_arr, wq_arr, wk_arr, wv_arr = _arrange_weights(win_t, wuq_full, mla_w_ukv[0].astype(BF16))
    small = {n: W[n][0] if n == "mla_w_ukv" else W[n].reshape(-1, W[n].shape[-1]) for n in SMALL_NAMES}

    loss_local, dx, grads = _local_step(x[0], positions.reshape(T, 1), loss_target[0], small, win_arr, wq_arr, wk_arr,
                                        wv_arr, wout4.reshape(D_MODEL, D_MODEL), wg4, wu4, wd4)

    dwin4 = jnp.pad(grads["w_in"].reshape(N_CHIPS, win_rows, D_MODEL), ((0, 0), (0, FF_SHARD - win_rows), (0, 0)))
    gs = [dwin4, grads["mla_w_uq"].reshape((N_CHIPS,) + UQ_COMM_SHAPE), grads["w_out"].reshape((N_CHIPS,) + shard2d["w_out"]),
          grads["w_gate"], grads["w_up"], grads["w_down"]]
    sm = _pack_small(grads, loss_local)
    *rs, ssib = _pair_swap(gs, sm)
    place = jnp.stack([cc, me]).astype(jnp.int32)
    *ps, pair = _pair_sum(place, gs, rs, sm, ssib)
    *ris, sm4 = _chip_swap(ps, pair)
    gfs = _chip_sum(place, gs, rs, ris)
    *gfin, smf = _pair_fill(gfs, sm4)

    row_blocks = {"w_in": 216, "mla_w_uq": 96, "w_out": 256, "w_gate": 352, "w_up": 352, "w_down": 352}
    G, DW, NM, NV = {}, {}, {}, {}
    for k, n in enumerate(BIG_NAMES):
        g2 = gfin[k][:win_rows] if n == "w_in" else gfin[k].reshape(shard2d[n])
        d, mo, vo = _adamw(to2d(n, W[n]), g2, to2d(n, M[n]), to2d(n, V[n]), row_blocks[n], "adamw_" + n)
        G[n], DW[n], NM[n], NV[n] = (from2d(n, t) for t in (g2, d, mo, vo))
    gs_buf, ds, ms, vs = _adamw_small(smf, _pack_small(W), _pack_small(M), _pack_small(V))
    shapes = {n: W[n].shape for n in SMALL_NAMES}
    for dst, buf in ((G, gs_buf), (DW, ds), (NM, ms), (NV, vs)):
        dst.update(_unpack_small(buf, shapes))

    loss = gs_buf.reshape(-1)[sum(SMALL_SIZES)]
    return (loss, dx[None], *[G[n] for n in WEIGHT_NAMES], *[DW[n] for n in WEIGHT_NAMES],
            *[NM[n] for n in WEIGHT_NAMES], *[NV[n] for n in WEIGHT_NAMES])
```

```python
import jax
import jax.numpy as jnp
from jax import lax
from jax.experimental import pallas as pl
from jax.experimental.pallas import tpu as pltpu

F32 = jnp.float32
BF16 = jnp.bfloat16
MXU_DTYPE = BF16

D_MODEL = 1024
MLA_HEADS = 8
MLA_NOPE = 64
MLA_ROPE = 32
MLA_V = 64
MLA_QK = MLA_NOPE + MLA_ROPE
Q_RANK = 384
KV_RANK = 128
MLA_WIDTH = MLA_HEADS * MLA_V
HEAD_PAD = 128
HGRN_HEADS = 4
HGRN_DIM = 128
HGRN_WIDTH = HGRN_HEADS * HGRN_DIM
CHUNK = 64
SUB = 16
D_IN = Q_RANK + KV_RANK + MLA_ROPE + 4 * HGRN_WIDTH
D_IN_ARR = Q_RANK + KV_RANK + HEAD_PAD + 4 * HGRN_WIDTH
D_FF = 2816
N_CHIPS = 4
FF_SHARD = D_FF // N_CHIPS
EPS = 1e-6
ROPE_THETA = 10000.0
ATTN_SCALE = MLA_QK ** -0.5
NEG_BIG = -1e30

ADAM_LR = 0.001
ADAM_B1 = 0.9
ADAM_B2 = 0.999
ADAM_EPS = 1e-08
ADAM_WD = 0.01
ADAM_STEP = 10

VMEM_LIMIT = 56 * 1024 * 1024

SMALL_ROWS = 144
SMALL_SIZES = (1024, 384, 128, 131072, 512, 1024, 512, 1024, 1024, 1024)

MESH = pl.DeviceIdType.MESH
ANY = pl.BlockSpec(memory_space=pl.ANY)


def _dot(a, b, dims, exact):
    if exact:
        return lax.dot_general(a.astype(F32), b.astype(F32), (dims, ((), ())), precision=lax.Precision.HIGH,
                               preferred_element_type=F32)
    return lax.dot_general(a.astype(MXU_DTYPE), b.astype(MXU_DTYPE), (dims, ((), ())), preferred_element_type=F32)


def _mm(a, b, exact=False):
    return _dot(a, b, ((1,), (0,)), exact)


def _mm_nt(a, b, exact=False):
    return _dot(a, b, ((1,), (1,)), exact)


def _mm_tn(a, b, exact=False):
    return _dot(a, b, ((0,), (0,)), exact)


def _rms_fwd(x, w):
    r = lax.rsqrt(jnp.mean(x * x, axis=-1, keepdims=True) + EPS)
    xn = x * r
    return xn * w, xn, r


def _rms_bwd(dy, xn, r, w):
    dxn = dy * w
    dx = r * (dxn - xn * jnp.mean(dxn * xn, axis=-1, keepdims=True))
    dw = jnp.sum(dy * xn, axis=0, keepdims=True)
    return dx, dw


def _group_sums(v, gs):
    t, n = v.shape
    lane = lax.broadcasted_iota(jnp.int32, (t, 128), 1)
    out = []
    for p in range(n // 128):
        vb = v[:, 128 * p:128 * (p + 1)]
        if gs == 128:
            out.append(jnp.sum(vb, axis=-1, keepdims=True))
        else:
            out.append(jnp.sum(jnp.where(lane < 64, vb, 0.0), axis=-1, keepdims=True))
            out.append(jnp.sum(jnp.where(lane >= 64, vb, 0.0), axis=-1, keepdims=True))
    return out


def _group_bcast(sums, gs, t):
    lane = lax.broadcasted_iota(jnp.int32, (t, 128), 1)
    if gs == 128:
        return jnp.concatenate([jnp.broadcast_to(s, (t, 128)) for s in sums], axis=-1)
    return jnp.concatenate([jnp.where(lane < 64, sums[2 * p], sums[2 * p + 1]) for p in range(len(sums) // 2)],
                           axis=-1)


def _grms_fwd(x, w, gs):
    t = x.shape[0]
    r = lax.rsqrt(_group_bcast(_group_sums(x * x, gs), gs, t) * (1.0 / gs) + EPS)
    xn = x * r
    return xn * w, xn, r


def _grms_bwd(dy, xn, r, w, gs):
    t = dy.shape[0]
    dxn = dy * w
    dx = r * (dxn - xn * (_group_bcast(_group_sums(dxn * xn, gs), gs, t) * (1.0 / gs)))
    dw = jnp.sum(dy * xn, axis=0, keepdims=True)
    return dx, dw


def _rope_tables(c_tab, s_tab):
    lane = lax.broadcasted_iota(jnp.int32, c_tab.shape, 1)
    first = (lane >= MLA_NOPE) & (lane < MLA_NOPE + MLA_ROPE // 2)
    second = (lane >= MLA_NOPE + MLA_ROPE // 2) & (lane < MLA_QK)
    return c_tab, jnp.where(first, -s_tab, 0.0), jnp.where(second, s_tab, 0.0)


def _rope(v, c, sa, sb):
    return v * c + pltpu.roll(v, HEAD_PAD - MLA_ROPE // 2, 1) * sa + pltpu.roll(v, MLA_ROPE // 2, 1) * sb


def _rope_bwd(d, c, sa, sb):
    return d * c - pltpu.roll(d, HEAD_PAD - MLA_ROPE // 2, 1) * sa - pltpu.roll(d, MLA_ROPE // 2, 1) * sb


def _params(sem, vmem=VMEM_LIMIT):
    return pltpu.CompilerParams(dimension_semantics=sem, vmem_limit_bytes=vmem)


def _in_fwd(x, pos, invf, w_pre, win, qnw, wq, kvnw, wk, wv, tt=256):
    T = x.shape[0]

    def body(x_ref, pos_ref, invf_ref, wpre_ref, win_ref, qnw_ref, wq_ref, kvnw_ref, wk_ref, wv_ref,
             cq_ref, ckv_ref, xph_ref, q_ref, k_ref, v_ref, kt_ref, vt_ref, rc_ref, rs_ref):
        u, _, _ = _rms_fwd(x_ref[...], wpre_ref[...])
        xp = _mm_nt(u, win_ref[...])
        cq = xp[:, :Q_RANK]
        ckv = xp[:, Q_RANK:Q_RANK + KV_RANK]
        kr = xp[:, Q_RANK + KV_RANK:Q_RANK + KV_RANK + HEAD_PAD]
        cq_ref[...] = cq
        ckv_ref[...] = ckv
        xph_ref[...] = xp[:, Q_RANK + KV_RANK + HEAD_PAD:]
        ang = pos_ref[...].astype(F32) * invf_ref[...]
        c_tab = jnp.cos(ang)
        s_tab = jnp.sin(ang)
        rc_ref[...] = c_tab
        rs_ref[...] = s_tab
        c, sa, sb = _rope_tables(c_tab, s_tab)
        qn, _, _ = _rms_fwd(cq, qnw_ref[...])
        q = _mm(qn, wq_ref[...])
        kvn, _, _ = _rms_fwd(ckv, kvnw_ref[...])
        kn = _mm(kvn, wk_ref[...])
        v = _mm(kvn, wv_ref[...])
        v_ref[...] = v.astype(v_ref.dtype)
        vt_ref[...] = v.T.astype(vt_ref.dtype)
        krr = _rope(kr, c, sa, sb)
        for h in range(MLA_HEADS):
            sl = slice(HEAD_PAD * h, HEAD_PAD * (h + 1))
            q_ref[:, sl] = _rope(q[:, sl], c, sa, sb).astype(q_ref.dtype)
            kh = kn[:, sl] + krr
            k_ref[:, sl] = kh.astype(k_ref.dtype)
            kt_ref[sl, :] = kh.T.astype(kt_ref.dtype)

    row = lambda w: pl.BlockSpec((tt, w), lambda i: (i, 0))
    full = lambda a: pl.BlockSpec(a.shape, lambda i: (0,) * a.ndim)
    qk_w = MLA_HEADS * HEAD_PAD
    return pl.pallas_call(
        body, name="in_fwd", grid=(T // tt,),
        in_specs=[row(D_MODEL), row(1), full(invf), full(w_pre), full(win), full(qnw), full(wq), full(kvnw),
                  full(wk), full(wv)],
        out_specs=[row(Q_RANK), row(KV_RANK), row(4 * HGRN_WIDTH), row(qk_w), row(qk_w), row(MLA_WIDTH),
                   pl.BlockSpec((qk_w, tt), lambda i: (0, i)), pl.BlockSpec((MLA_WIDTH, tt), lambda i: (0, i)),
                   row(HEAD_PAD), row(HEAD_PAD)],
        out_shape=[jax.ShapeDtypeStruct((T, Q_RANK), F32), jax.ShapeDtypeStruct((T, KV_RANK), F32),
                   jax.ShapeDtypeStruct((T, 4 * HGRN_WIDTH), F32), jax.ShapeDtypeStruct((T, qk_w), MXU_DTYPE),
                   jax.ShapeDtypeStruct((T, qk_w), MXU_DTYPE), jax.ShapeDtypeStruct((T, MLA_WIDTH), MXU_DTYPE),
                   jax.ShapeDtypeStruct((qk_w, T), MXU_DTYPE), jax.ShapeDtypeStruct((MLA_WIDTH, T), MXU_DTYPE),
                   jax.ShapeDtypeStruct((T, HEAD_PAD), F32), jax.ShapeDtypeStruct((T, HEAD_PAD), F32)],
        compiler_params=_params(("arbitrary",)),
    )(x, pos, invf, w_pre, win, qnw, wq, kvnw, wk, wv)


def _attn_fwd_t(qb, kb, vt, gather=(), tq=256, hps=8):
    T = qb.shape[0]
    nq = T // tq
    ng = len(gather)
    steps = (MLA_HEADS // hps) * nq
    pass_on = steps - 3

    def body(q_ref, k_ref, vt_ref, *rest):
        o_ref, lse_ref = rest[ng:ng + 2]
        acc_scr = rest[2 * ng + 2]
        qi = pl.program_id(1)
        step_no = pl.program_id(0) * nq + qi
        if ng:
            gat = _Gather(rest[:ng], rest[ng + 2:2 * ng + 2], *rest[2 * ng + 3:])

            @pl.when(step_no == 0)
            def _():
                for cp in gat.sends():
                    cp.start()

            @pl.when(step_no == pass_on)
            def _():
                for arrival in gat.arrivals():
                    arrival.wait_recv()
                for cp in gat.forwards():
                    cp.start()

        heads = [slice(HEAD_PAD * a, HEAD_PAD * (a + 1)) for a in range(hps)]
        acc_scr[...] = jnp.zeros_like(acc_scr)

        def step(j, carry, masked):
            start = pl.multiple_of(j * tq, tq)
            scores = [_mm_nt(k_ref[pl.ds(start, tq), heads[a]], q_ref[:, heads[a]]) for a in range(hps)]
            new = []
            for a in range(hps):
                m, l = carry[a]
                s = scores[a] * ATTN_SCALE
                if masked:
                    kk = lax.broadcasted_iota(jnp.int32, (tq, tq), 0)
                    qq = lax.broadcasted_iota(jnp.int32, (tq, tq), 1)
                    s = jnp.where(kk <= qq, s, NEG_BIG)
                m_new = jnp.maximum(m, jnp.max(s, axis=0, keepdims=True))
                alpha = jnp.exp(m - m_new)
                p = jnp.exp(s - m_new)
                l = l * alpha + jnp.sum(p, axis=0, keepdims=True)
                vtj = vt_ref[2 * MLA_V * (a // 2):2 * MLA_V * (a // 2 + 1), pl.ds(start, tq)]
                acc_scr[a] = acc_scr[a] * alpha + _mm(vtj, p)
                new.append((m_new, l))
            return tuple(new)

        init = tuple((jnp.full((1, tq), NEG_BIG, F32), jnp.zeros((1, tq), F32)) for _ in range(hps))
        carry = lax.fori_loop(0, qi, lambda j, c: step(j, c, False), init)
        carry = step(qi, carry, True)
        row = lax.broadcasted_iota(jnp.int32, (2 * MLA_V, tq), 0)
        for pr in range(hps // 2):
            (m0, l0), (m1, l1) = carry[2 * pr], carry[2 * pr + 1]
            ot = jnp.where(row < MLA_V, acc_scr[2 * pr] / l0, acc_scr[2 * pr + 1] / l1)
            o_ref[:, 2 * MLA_V * pr:2 * MLA_V * (pr + 1)] = ot.T
            lse_ref[pr, 0:1, :] = m0 + jnp.log(l0)
            lse_ref[pr, 1:2, :] = m1 + jnp.log(l1)

        if ng:
            @pl.when(step_no == steps - 1)
            def _():
                for arrival in gat.forward_arrivals():
                    arrival.wait_recv()
                for cp in gat.sends() + gat.forwards():
                    cp.wait_send()

    return pl.pallas_call(
        body, name="attn_fwd", grid=(MLA_HEADS // hps, nq),
        in_specs=[pl.BlockSpec((tq, hps * HEAD_PAD), lambda g, i: (i, g)),
                  pl.BlockSpec((T, hps * HEAD_PAD), lambda g, i: (0, g)),
                  pl.BlockSpec((hps * MLA_V, T), lambda g, i: (g, 0))] + [ANY] * ng,
        out_specs=[pl.BlockSpec((tq, hps * MLA_V), lambda g, i: (i, g)),
                   pl.BlockSpec((hps // 2, 2, tq), lambda g, i: (g, 0, i))] + [ANY] * ng,
        out_shape=[jax.ShapeDtypeStruct((T, MLA_WIDTH), F32), jax.ShapeDtypeStruct((MLA_HEADS // 2, 2, T), F32)]
        + _Gather.out_shapes(gather),
        scratch_shapes=[pltpu.VMEM((hps, 2 * MLA_V, tq), F32)] + (_Gather.semaphores(gather) if ng else []),
        compiler_params=_params(("arbitrary", "arbitrary")),
    )(qb, kb, vt, *gather)


def _attn_bwd_t(qb, kb, kt, vb, dob, lse, dvec, send=(), tq=256, hps=4):
    T = qb.shape[0]
    nq = T // tq
    ns = len(send)
    steps = (MLA_HEADS // hps) * nq

    def body(q_ref, k_ref, kt_ref, v_ref, do_ref, lse_ref, d_ref, *rest):
        dqt_ref, dk_ref, dv_ref = rest[ns:ns + 3]
        va_scr, dv_scr = rest[2 * ns + 3:2 * ns + 5]
        j = pl.program_id(1)
        step_no = pl.program_id(0) * nq + j
        if ns:
            @pl.when(step_no == 0)
            def _():
                for cp in _chip_swap_copies(rest[:ns], rest[ns + 3:2 * ns + 3], *rest[2 * ns + 5:]):
                    cp.start()

        @pl.when(j == 0)
        def _():
            dqt_ref[...] = jnp.zeros_like(dqt_ref)

        lane = lax.broadcasted_iota(jnp.int32, (tq, 2 * MLA_V), 1)
        heads = [slice(HEAD_PAD * a, HEAD_PAD * (a + 1)) for a in range(hps)]
        pairs = [slice(2 * MLA_V * p, 2 * MLA_V * (p + 1)) for p in range(hps // 2)]
        for pr in range(hps // 2):
            vpair = v_ref[:, pairs[pr]]
            va_scr[2 * pr] = jnp.where(lane < MLA_V, vpair, jnp.zeros_like(vpair))
            va_scr[2 * pr + 1] = jnp.where(lane >= MLA_V, vpair, jnp.zeros_like(vpair))
        dk_ref[...] = jnp.zeros_like(dk_ref)
        dv_scr[...] = jnp.zeros_like(dv_scr)

        def step(i, masked):
            start = pl.multiple_of(i * tq, tq)
            rows = pl.ds(start, tq)
            scores = [_mm_nt(k_ref[:, heads[a]], q_ref[rows, heads[a]]) for a in range(hps)]
            dps = [_mm_nt(va_scr[a], do_ref[rows, pairs[a // 2]]) for a in range(hps)]
            for a in range(hps):
                pr, r = a // 2, a % 2
                p = jnp.exp(scores[a] * ATTN_SCALE - lse_ref[pr, r:r + 1, rows])
                if masked:
                    kk = lax.broadcasted_iota(jnp.int32, (tq, tq), 0)
                    qq = lax.broadcasted_iota(jnp.int32, (tq, tq), 1)
                    p = jnp.where(kk <= qq, p, 0.0)
                ds = p * (dps[a] - d_ref[pr, r:r + 1, rows]) * ATTN_SCALE
                dv_scr[a] += _mm(p, do_ref[rows, pairs[pr]])
                dk_ref[:, heads[a]] += _mm(ds, q_ref[rows, heads[a]])
                dqt_ref[heads[a], rows] += _mm(kt_ref[heads[a], :], ds)

        def loop_body(i, _):
            step(i, False)
            return 0

        step(j, True)
        lax.fori_loop(j + 1, nq, loop_body, 0)
        for pr in range(hps // 2):
            dv_ref[:, pairs[pr]] = jnp.where(lane < MLA_V, dv_scr[2 * pr], dv_scr[2 * pr + 1])

        if ns:
            @pl.when(step_no == steps - 1)
            def _():
                for cp in _chip_swap_copies(rest[:ns], rest[ns + 3:2 * ns + 3], *rest[2 * ns + 5:]):
                    cp.wait()

    stat = pl.BlockSpec((hps // 2, 2, T), lambda g, j: (g, 0, 0))
    return pl.pallas_call(
        body, name="attn_bwd", grid=(MLA_HEADS // hps, nq),
        in_specs=[pl.BlockSpec((T, hps * HEAD_PAD), lambda g, j: (0, g)),
                  pl.BlockSpec((tq, hps * HEAD_PAD), lambda g, j: (j, g)),
                  pl.BlockSpec((hps * HEAD_PAD, tq), lambda g, j: (g, j)),
                  pl.BlockSpec((tq, hps * MLA_V), lambda g, j: (j, g)),
                  pl.BlockSpec((T, hps * MLA_V), lambda g, j: (0, g)), stat, stat] + [ANY] * ns,
        out_specs=[pl.BlockSpec((hps * HEAD_PAD, T), lambda g, j: (g, 0)),
                   pl.BlockSpec((tq, hps * HEAD_PAD), lambda g, j: (j, g)),
                   pl.BlockSpec((tq, hps * MLA_V), lambda g, j: (j, g))] + [ANY] * ns,
        out_shape=[jax.ShapeDtypeStruct((MLA_HEADS * HEAD_PAD, T), F32),
                   jax.ShapeDtypeStruct((T, MLA_HEADS * HEAD_PAD), F32),
                   jax.ShapeDtypeStruct((T, MLA_WIDTH), F32)] + _chip_swap_shapes(send),
        scratch_shapes=[pltpu.VMEM((hps, tq, 2 * MLA_V), vb.dtype), pltpu.VMEM((hps, tq, 2 * MLA_V), F32)]
        + ([pltpu.SemaphoreType.DMA((3 * ns,)), pltpu.SemaphoreType.DMA((3 * ns,))] if ns else []),
        compiler_params=_params(("arbitrary", "arbitrary")),
    )(qb, kb, kt, vb, dob, lse, dvec, *send)


def _cumsum_rows(x):
    n = x.shape[0]
    row = lax.broadcasted_iota(jnp.int32, x.shape, 0)
    s = 1
    while s < n:
        x = x + jnp.where(row >= s, pltpu.roll(x, s, 0), 0.0)
        s *= 2
    return x


def _rev_cumsum_rows(x):
    n = x.shape[0]
    row = lax.broadcasted_iota(jnp.int32, x.shape, 0)
    s = 1
    while s < n:
        x = x + jnp.where(row < n - s, pltpu.roll(x, n - s, 0), 0.0)
        s *= 2
    return x


def _lb_from_logits(l):
    l0, l1 = l[0:1, :], l[1:2, :]
    m = jnp.maximum(l0, l1)
    e0, e1 = jnp.exp(l0 - m), jnp.exp(l1 - m)
    return e0 / (e0 + e1)


def _hgrn_gates(hq, hf, lb):
    sig_f = jax.nn.sigmoid(hf)
    f = lb + (1.0 - lb) * sig_f
    sig_q = jax.nn.sigmoid(hq)
    return sig_f, f, jnp.log(f), 1.0 - f, sig_q, hq * sig_q


def _hgrn_intra(q, kk, b, exact=False):
    row = lax.broadcasted_iota(jnp.int32, b.shape, 0)
    qs, ks, eqs, eks, a_rows = [], [], [], [], []
    for i in range(CHUNK // SUB):
        ref = b[SUB * i:SUB * i + 1, :]
        eq = jnp.exp(b[SUB * i:SUB * (i + 1), :] - ref)
        ek = jnp.exp(jnp.where(row < SUB * (i + 1), ref - b, NEG_BIG))
        qi = q[SUB * i:SUB * (i + 1), :] * eq
        ki = kk * ek
        a_rows.append(_mm_nt(qi, ki, exact))
        qs.append(qi), ks.append(ki), eqs.append(eq), eks.append(ek)
    tt = lax.broadcasted_iota(jnp.int32, (CHUNK, CHUNK), 0)
    ss = lax.broadcasted_iota(jnp.int32, (CHUNK, CHUNK), 1)
    causal = ss <= tt
    a = jnp.where(causal, jnp.concatenate(a_rows, axis=0), 0.0)
    return a, causal, qs, ks, eqs, eks


def _hgrn_fwd(xph, lbl, tg=512):
    T = xph.shape[0]
    ng, ncg = T // tg, tg // CHUNK
    cols = [slice(HGRN_DIM * h, HGRN_DIM * (h + 1)) for h in range(HGRN_HEADS)]

    def body(lbl_ref, hq_ref, hf_ref, hi_ref, o_ref, st_ref, s_scr):
        @pl.when(pl.program_id(0) == 0)
        def _():
            s_scr[...] = jnp.zeros_like(s_scr)

        lb = _lb_from_logits(lbl_ref[...])

        def chunk(c, _):
            rows = pl.ds(pl.multiple_of(c * CHUNK, CHUNK), CHUNK)
            pre = []
            for cs in cols:
                _, _, lf, kk, _, q = _hgrn_gates(hq_ref[rows, cs], hf_ref[rows, cs], lb[:, cs])
                v = hi_ref[rows, cs]
                b = _cumsum_rows(lf)
                a = _hgrn_intra(q, kk, b)[0]
                b_last = b[CHUNK - 1:CHUNK, :]
                pre.append((q * jnp.exp(b), a, v, jnp.exp(b_last), _mm_tn(v, kk * jnp.exp(b_last - b))))
            for h, cs in enumerate(cols):
                qe, a, v, ebl, upd = pre[h]
                st = s_scr[h]
                st_ref[h, c] = st
                o_ref[rows, cs] = _mm_nt(qe, st) + _mm(a, v)
                s_scr[h] = st * ebl + upd
            return 0

        lax.fori_loop(0, ncg, chunk, 0)

    col = lambda k: pl.BlockSpec((tg, HGRN_WIDTH), lambda g: (g, k))
    return pl.pallas_call(
        body, name="hgrn_fwd", grid=(ng,),
        in_specs=[pl.BlockSpec((2, HGRN_WIDTH), lambda g: (0, 0)), col(0), col(1), col(2)],
        out_specs=[col(0), pl.BlockSpec((HGRN_HEADS, ncg, HGRN_DIM, HGRN_DIM), lambda g: (0, g, 0, 0))],
        out_shape=[jax.ShapeDtypeStruct((T, HGRN_WIDTH), F32),
                   jax.ShapeDtypeStruct((HGRN_HEADS, T // CHUNK, HGRN_DIM, HGRN_DIM), F32)],
        scratch_shapes=[pltpu.VMEM((HGRN_HEADS, HGRN_DIM, HGRN_DIM), F32)],
        compiler_params=_params(("arbitrary",)),
    )(lbl, xph, xph, xph)


def _hgrn_bwd(xph, lbl, states, d_o, tg=512):
    T = xph.shape[0]
    ng, ncg = T // tg, tg // CHUNK
    cols = [slice(HGRN_DIM * h, HGRN_DIM * (h + 1)) for h in range(HGRN_HEADS)]
    nsub = CHUNK // SUB

    def body(lbl_ref, hq_ref, hf_ref, hi_ref, st_ref, do_ref, dhq_ref, dhf_ref, dhi_ref, dlg_ref, ds_scr, dlb_scr):
        g = pl.program_id(0)

        @pl.when(g == 0)
        def _():
            ds_scr[...] = jnp.zeros_like(ds_scr)
            dlb_scr[...] = jnp.zeros_like(dlb_scr)

        lb = _lb_from_logits(lbl_ref[...])

        def chunk(ci, _):
            c = ncg - 1 - ci
            rows = pl.ds(pl.multiple_of(c * CHUNK, CHUNK), CHUNK)
            pre = []
            for h, cs in enumerate(cols):
                hq = hq_ref[rows, cs]
                sig_f, f, lf, kk, sig_q, q = _hgrn_gates(hq, hf_ref[rows, cs], lb[:, cs])
                v = hi_ref[rows, cs]
                do = do_ref[rows, cs]
                b = _cumsum_rows(lf)
                eb = jnp.exp(b)
                a, causal, qs, ks, eqs, eks = _hgrn_intra(q, kk, b, exact=True)
                b_last = b[CHUNK - 1:CHUNK, :]
                st = st_ref[h, c]
                pre.append(dict(hq=hq, sig_f=sig_f, f=f, kk=kk, sig_q=sig_q, q=q, v=v, eb=eb, qs=qs, ks=ks, eqs=eqs,
                                eks=eks, ebl=jnp.exp(b_last), el=jnp.exp(b_last - b), st=st,
                                da=jnp.where(causal, _mm_nt(do, v, True), 0.0), dq=_mm(do, st, True) * eb,
                                dv=_mm_tn(a, do, True), dsu=_mm_tn(do, q * eb, True)))
            for w in pre:
                dq_rows = []
                dk = jnp.zeros_like(w["q"])
                for i in range(nsub):
                    dai = w["da"][SUB * i:SUB * (i + 1), :]
                    dq_rows.append(_mm(dai, w["ks"][i], True) * w["eqs"][i])
                    dk = dk + _mm_tn(dai, w["qs"][i], True) * w["eks"][i]
                w["dq"] = w["dq"] + jnp.concatenate(dq_rows, axis=0)
                w["dk"] = dk
            for h, cs in enumerate(cols):
                w = pre[h]
                kk, el, ebl, dst = w["kk"], w["el"], w["ebl"], ds_scr[h]
                dk_state = _mm(w["v"], dst, True) * el
                dk = w["dk"] + dk_state
                e_last = (ebl * jnp.sum(w["st"] * dst, axis=0, keepdims=True)
                          + jnp.sum(kk * dk_state, axis=0, keepdims=True))
                dlf = _rev_cumsum_rows(w["q"] * w["dq"] - kk * dk) + e_last
                ds_scr[h] = dst * ebl + w["dsu"]
                df = dlf / w["f"] - dk
                sig_f, sig_q = w["sig_f"], w["sig_q"]
                dhf_ref[rows, cs] = df * (1.0 - lb[:, cs]) * sig_f * (1.0 - sig_f)
                dlb_scr[:, cs] += jnp.sum(df * (1.0 - sig_f), axis=0, keepdims=True)
                dhq_ref[rows, cs] = w["dq"] * sig_q * (1.0 + w["hq"] * (1.0 - sig_q))
                dhi_ref[rows, cs] = w["dv"] + _mm_nt(kk * el, dst, True)
            return 0

        lax.fori_loop(0, ncg, chunk, 0)

        @pl.when(g == ng - 1)
        def _():
            dl0 = dlb_scr[...] * lb * (1.0 - lb)
            dlg_ref[...] = jnp.concatenate([dl0, -dl0], axis=0)

    col = lambda k: pl.BlockSpec((tg, HGRN_WIDTH), lambda g: (ng - 1 - g, k))
    logits = pl.BlockSpec((2, HGRN_WIDTH), lambda g: (0, 0))
    big = jax.ShapeDtypeStruct((T, HGRN_WIDTH), F32)
    return pl.pallas_call(
        body, name="hgrn_bwd", grid=(ng,),
        in_specs=[logits, col(0), col(1), col(2),
                  pl.BlockSpec((HGRN_HEADS, ncg, HGRN_DIM, HGRN_DIM), lambda g: (0, ng - 1 - g, 0, 0)), col(0)],
        out_specs=[col(0), col(0), col(0), logits],
        out_shape=[big, big, big, jax.ShapeDtypeStruct((2, HGRN_WIDTH), F32)],
        scratch_shapes=[pltpu.VMEM((HGRN_HEADS, HGRN_DIM, HGRN_DIM), F32), pltpu.VMEM((1, HGRN_WIDTH), F32)],
        compiler_params=_params(("arbitrary",)),
    )(lbl, xph, xph, xph, states, d_o)


def _ffn_fwd(x, o_raw, oh_raw, xph, tgt, wout, w_mla, w_hg, w_post, w_fpre, w_fpost, wg, wu, wd, tt=256):
    T = x.shape[0]
    nj = N_CHIPS

    def body(x_ref, o_ref, oh_ref, hg_ref, tgt_ref, wout_ref, wmla_ref, whg_ref, wpost_ref, wfpre_ref, wfpost_ref,
             wg_ref, wu_ref, wd_ref,
             h1_ref, y1_ref, z_ref, mix_ref, g_ref, up_ref, dy2_ref, dh2_ref, loss_ref, dwf_ref,
             z_scr, y2_scr):
        i, j = pl.program_id(0), pl.program_id(1)

        @pl.when((i == 0) & (j == 0))
        def _():
            loss_ref[...] = jnp.zeros_like(loss_ref)
            dwf_ref[...] = jnp.zeros_like(dwf_ref)

        @pl.when(j == 0)
        def _():
            om, _, _ = _grms_fwd(o_ref[...], wmla_ref[...], MLA_V)
            hg = hg_ref[...]
            ohn, _, _ = _grms_fwd(oh_ref[...], whg_ref[...], HGRN_DIM)
            mix = jnp.concatenate([om, ohn * (hg * jax.nn.sigmoid(hg))], axis=-1)
            mix_ref[...] = mix.astype(mix_ref.dtype)
            y1 = _mm(mix, wout_ref[...])
            y1_ref[...] = y1
            h1 = x_ref[...] + _rms_fwd(y1, wpost_ref[...])[0]
            h1_ref[...] = h1
            z = _rms_fwd(h1, wfpre_ref[...])[0].astype(z_scr.dtype)
            z_scr[...] = z
            z_ref[...] = z
            y2_scr[...] = jnp.zeros_like(y2_scr)

        z = z_scr[...]
        g = _mm_nt(z, wg_ref[0])
        up = _mm_nt(z, wu_ref[0])
        g_ref[0] = g
        up_ref[0] = up
        y2_scr[...] += _mm(g * jax.nn.sigmoid(g) * up, wd_ref[0])

        @pl.when(j == nj - 1)
        def _():
            w = wfpost_ref[...]
            y2s, y2n, r2 = _rms_fwd(y2_scr[...], w)
            e = h1_ref[...] + y2s - tgt_ref[...]
            loss_ref[...] += jnp.sum(e * e, axis=0, keepdims=True)
            dh2 = e * (1.0 / D_MODEL)
            dh2_ref[...] = dh2
            dy2, dwf = _rms_bwd(dh2, y2n, r2, w)
            dy2_ref[...] = dy2.astype(dy2_ref.dtype)
            dwf_ref[...] += dwf

    row = lambda w: pl.BlockSpec((tt, w), lambda i, j: (i, 0))
    full = lambda a: pl.BlockSpec(a.shape, lambda i, j: (0,) * a.ndim)
    vec = pl.BlockSpec((1, D_MODEL), lambda i, j: (0, 0))
    return pl.pallas_call(
        body, name="ffn_fwd", grid=(T // tt, nj),
        in_specs=[row(D_MODEL), row(MLA_WIDTH), row(HGRN_WIDTH),
                  pl.BlockSpec((tt, HGRN_WIDTH), lambda i, j: (i, 3)), row(D_MODEL), full(wout), full(w_mla),
                  full(w_hg), vec, vec, vec,
                  pl.BlockSpec((1, FF_SHARD, D_MODEL), lambda i, j: (j, 0, 0)),
                  pl.BlockSpec((1, FF_SHARD, D_MODEL), lambda i, j: (j, 0, 0)),
                  pl.BlockSpec((1, FF_SHARD, D_MODEL), lambda i, j: (j, 0, 0))],
        out_specs=[row(D_MODEL), row(D_MODEL), row(D_MODEL), row(D_MODEL),
                   pl.BlockSpec((1, tt, FF_SHARD), lambda i, j: (j, i, 0)),
                   pl.BlockSpec((1, tt, FF_SHARD), lambda i, j: (j, i, 0)),
                   row(D_MODEL), row(D_MODEL), vec, vec],
        out_shape=[jax.ShapeDtypeStruct((T, D_MODEL), F32), jax.ShapeDtypeStruct((T, D_MODEL), F32),
                   jax.ShapeDtypeStruct((T, D_MODEL), MXU_DTYPE), jax.ShapeDtypeStruct((T, D_MODEL), MXU_DTYPE),
                   jax.ShapeDtypeStruct((nj, T, FF_SHARD), F32), jax.ShapeDtypeStruct((nj, T, FF_SHARD), F32),
                   jax.ShapeDtypeStruct((T, D_MODEL), MXU_DTYPE), jax.ShapeDtypeStruct((T, D_MODEL), F32),
                   jax.ShapeDtypeStruct((1, D_MODEL), F32), jax.ShapeDtypeStruct((1, D_MODEL), F32)],
        scratch_shapes=[pltpu.VMEM((tt, D_MODEL), MXU_DTYPE), pltpu.VMEM((tt, D_MODEL), F32)],
        compiler_params=_params(("arbitrary", "arbitrary")),
    )(x, o_raw, oh_raw, xph, tgt, wout, w_mla, w_hg, w_post, w_fpre, w_fpost, wg, wu, wd)


def _ffn_bwd(zb, g, up, dy2b, wg, wu, wd, tt=512):
    T = zb.shape[0]
    nj = N_CHIPS

    def body(z_ref, g_ref, up_ref, dy2_ref, wg_ref, wu_ref, wd_ref, dwg_ref, dwu_ref, dwd_ref, dz_ref):
        @pl.when(pl.program_id(1) == 0)
        def _():
            dwg_ref[...] = jnp.zeros_like(dwg_ref)
            dwu_ref[...] = jnp.zeros_like(dwu_ref)
            dwd_ref[...] = jnp.zeros_like(dwd_ref)

        z, g_, up_, dy2 = z_ref[...], g_ref[0], up_ref[0], dy2_ref[...]
        sg = jax.nn.sigmoid(g_)
        act = g_ * sg
        dff = _mm_nt(dy2, wd_ref[0])
        dwd_ref[0] += _mm_tn(act * up_, dy2)
        dg = dff * up_ * sg * (1.0 + g_ * (1.0 - sg))
        dup = dff * act
        dwg_ref[0] += _mm_tn(dg, z)
        dwu_ref[0] += _mm_tn(dup, z)
        dz_ref[0] = _mm(dg, wg_ref[0]) + _mm(dup, wu_ref[0])

    row = pl.BlockSpec((tt, D_MODEL), lambda j, i: (i, 0))
    act = pl.BlockSpec((1, tt, FF_SHARD), lambda j, i: (j, i, 0))
    w_sh = pl.BlockSpec((1, FF_SHARD, D_MODEL), lambda j, i: (j, 0, 0))
    w_grad = jax.ShapeDtypeStruct((nj, FF_SHARD, D_MODEL), F32)
    return pl.pallas_call(
        body, name="ffn_bwd", grid=(nj, T // tt),
        in_specs=[row, act, act, row, w_sh, w_sh, w_sh],
        out_specs=[w_sh, w_sh, w_sh, pl.BlockSpec((1, tt, D_MODEL), lambda j, i: (j, i, 0))],
        out_shape=[w_grad, w_grad, w_grad, jax.ShapeDtypeStruct((nj, T, D_MODEL), F32)],
        compiler_params=_params(("arbitrary", "arbitrary")),
    )(zb, g, up, dy2b, wg, wu, wd)


def _mid_bwd(dzp, dh2, h1, y1, mixb, o_raw, oh_raw, xph, wout, w_fpre, w_post, w_mla, w_hg, swap=(), tt=256):
    T = dh2.shape[0]
    nsw = len(swap)
    n_in, n_out = 13, 10

    def body(*refs):
        (dzp_ref, dh2_ref, h1_ref, y1_ref, mix_ref, o_ref, oh_ref, hg_ref, wout_ref, wfpre_ref, wpost_ref,
         wmla_ref, whg_ref) = refs[:n_in]
        (dh1_ref, dwout_ref, do_ref, doh_ref, dhg_ref, dvec_ref, dwfpre_ref, dwpost_ref, dwmla_ref,
         dwhg_ref) = refs[n_in + nsw:n_in + nsw + n_out]
        swap_copies = lambda: _pair_swap_copies(refs[n_in:n_in + nsw], refs[n_in + nsw + n_out:n_in + 2 * nsw + n_out],
                                                *refs[n_in + 2 * nsw + n_out:])

        @pl.when(pl.program_id(0) == 0)
        def _():
            for r in (dwout_ref, dwfpre_ref, dwpost_ref, dwmla_ref, dwhg_ref):
                r[...] = jnp.zeros_like(r)
            for cp in (swap_copies() if nsw else ()):
                cp.start()

        dz = dzp_ref[0] + dzp_ref[1] + dzp_ref[2] + dzp_ref[3]
        wfpre = wfpre_ref[...]
        _, h1n, r = _rms_fwd(h1_ref[...], wfpre)
        dh1_z, dwfpre = _rms_bwd(dz, h1n, r, wfpre)
        dwfpre_ref[...] += dwfpre
        dh1 = dh2_ref[...] + dh1_z
        dh1_ref[...] = dh1
        wpost = wpost_ref[...]
        _, y1n, r1 = _rms_fwd(y1_ref[...], wpost)
        dy1, dwpost = _rms_bwd(dh1, y1n, r1, wpost)
        dwpost_ref[...] += dwpost
        dmix = _mm_nt(dy1, wout_ref[...])
        dwout_ref[...] += _mm_tn(mix_ref[...], dy1)
        wmla = wmla_ref[...]
        o = o_ref[...]
        _, on, ro = _grms_fwd(o, wmla, MLA_V)
        d_o, dwmla = _grms_bwd(dmix[:, :MLA_WIDTH], on, ro, wmla, MLA_V)
        dwmla_ref[...] += dwmla
        do_ref[...] = d_o.astype(do_ref.dtype)
        hh = lax.broadcasted_iota(jnp.int32, (MLA_HEADS, MLA_WIDTH), 0)
        ll = lax.broadcasted_iota(jnp.int32, (MLA_HEADS, MLA_WIDTH), 1)
        sel = jnp.where((ll >= hh * MLA_V) & (ll < (hh + 1) * MLA_V), 1.0, 0.0)
        dvec_ref[...] = _mm_nt(sel, d_o * o, True)
        whg = whg_ref[...]
        hg = hg_ref[...]
        sg = jax.nn.sigmoid(hg)
        _, ohn, rh = _grms_fwd(oh_ref[...], whg, HGRN_DIM)
        dmh = dmix[:, MLA_WIDTH:]
        dhg_ref[...] = dmh * ohn * whg * sg * (1.0 + hg * (1.0 - sg))
        d_oh, dwhg = _grms_bwd(dmh * (hg * sg), ohn, rh, whg, HGRN_DIM)
        dwhg_ref[...] += dwhg
        doh_ref[...] = d_oh

        if nsw:
            @pl.when(pl.program_id(0) == T // tt - 1)
            def _():
                for cp in swap_copies():
                    cp.wait()

    row = lambda w: pl.BlockSpec((tt, w), lambda i: (i, 0))
    full = lambda a: pl.BlockSpec(a.shape, lambda i: (0,) * a.ndim)
    vec = lambda w: pl.BlockSpec((1, w), lambda i: (0, 0))
    sds = jax.ShapeDtypeStruct
    return pl.pallas_call(
        body, name="mid_bwd", grid=(T // tt,),
        in_specs=[pl.BlockSpec((N_CHIPS, tt, D_MODEL), lambda i: (0, i, 0)), row(D_MODEL), row(D_MODEL), row(D_MODEL),
                  row(D_MODEL), row(MLA_WIDTH), row(HGRN_WIDTH), pl.BlockSpec((tt, HGRN_WIDTH), lambda i: (i, 3)),
                  full(wout), vec(D_MODEL), vec(D_MODEL), vec(MLA_WIDTH), vec(HGRN_WIDTH)] + [ANY] * nsw,
        out_specs=[row(D_MODEL), full(wout), row(MLA_WIDTH), row(HGRN_WIDTH), row(HGRN_WIDTH),
                   pl.BlockSpec((MLA_HEADS, tt), lambda i: (0, i)),
                   vec(D_MODEL), vec(D_MODEL), vec(MLA_WIDTH), vec(HGRN_WIDTH)] + [ANY] * nsw,
        out_shape=[sds((T, D_MODEL), F32), sds(wout.shape, F32), sds((T, MLA_WIDTH), MXU_DTYPE), sds((T, HGRN_WIDTH), F32),
                   sds((T, HGRN_WIDTH), F32), sds((MLA_HEADS, T), F32),
                   sds((1, D_MODEL), F32), sds((1, D_MODEL), F32), sds((1, MLA_WIDTH), F32), sds((1, HGRN_WIDTH), F32)]
        + _half_stack_shapes(swap),
        scratch_shapes=[pltpu.SemaphoreType.DMA((nsw,)), pltpu.SemaphoreType.DMA((nsw,))] if nsw else [],
        compiler_params=_params(("arbitrary",)),
    )(dzp, dh2, h1, y1, mixb, o_raw, oh_raw, xph, wout, w_fpre, w_post, w_mla, w_hg, *swap)


def _in_bwd(x, dh1, cq, ckv, dq, dk, dv, dhq, dhf, dhi, dhg, rc, rs, w_pre, win, qnw, wq, kvnw, wk, wv, tt=256):
    T = x.shape[0]

    def body(x_ref, dh1_ref, cq_ref, ckv_ref, dq_ref, dk_ref, dv_ref, dhq_ref, dhf_ref, dhi_ref, dhg_ref, rc_ref, rs_ref,
             wpre_ref, win_ref, qnw_ref, wq_ref, kvnw_ref, wk_ref, wv_ref,
             dx_ref, dwin_ref, dwq_ref, dwk_ref, dwv_ref, dwpre_ref, dqnw_ref, dkvnw_ref):
        @pl.when(pl.program_id(0) == 0)
        def _():
            for r in (dwin_ref, dwq_ref, dwk_ref, dwv_ref, dwpre_ref, dqnw_ref, dkvnw_ref):
                r[...] = jnp.zeros_like(r)

        c, sa, sb = _rope_tables(rc_ref[...], rs_ref[...])
        lane = lax.broadcasted_iota(jnp.int32, (tt, HEAD_PAD), 1)
        dk_all = dk_ref[...]
        dq_lin = []
        dkr = jnp.zeros((tt, HEAD_PAD), F32)
        for h in range(MLA_HEADS):
            sl = slice(HEAD_PAD * h, HEAD_PAD * (h + 1))
            dq_lin.append(_rope_bwd(dq_ref[sl, :].T, c, sa, sb))
            dkr = dkr + dk_all[:, sl]
        dq_lin = jnp.concatenate(dq_lin, axis=-1)
        dkr = jnp.where((lane >= MLA_NOPE) & (lane < MLA_QK), _rope_bwd(dkr, c, sa, sb), 0.0)
        qnw = qnw_ref[...]
        qn, cqn, rq = _rms_fwd(cq_ref[...], qnw)
        dwq_ref[...] += _mm_tn(qn, dq_lin)
        dcq, dqnw = _rms_bwd(_mm_nt(dq_lin, wq_ref[...]), cqn, rq, qnw)
        dqnw_ref[...] += dqnw
        kvnw = kvnw_ref[...]
        kvn, ckvn, rkv = _rms_fwd(ckv_ref[...], kvnw)
        dv_ = dv_ref[...]
        dwk_ref[...] += _mm_tn(kvn, dk_all)
        dwv_ref[...] += _mm_tn(kvn, dv_)
        dckv, dkvnw = _rms_bwd(_mm_nt(dk_all, wk_ref[...]) + _mm_nt(dv_, wv_ref[...]), ckvn, rkv, kvnw)
        dkvnw_ref[...] += dkvnw
        dxp = jnp.concatenate([dcq, dckv, dkr, dhq_ref[...], dhf_ref[...], dhi_ref[...], dhg_ref[...]], axis=-1)
        wpre = wpre_ref[...]
        u, xn, rx = _rms_fwd(x_ref[...], wpre)
        dwin_ref[...] += _mm_tn(dxp, u)
        dx_u, dwpre = _rms_bwd(_mm(dxp, win_ref[...]), xn, rx, wpre)
        dwpre_ref[...] += dwpre
        dx_ref[...] = dh1_ref[...] + dx_u

    row = lambda w: pl.BlockSpec((tt, w), lambda i: (i, 0))
    full = lambda a: pl.BlockSpec(a.shape, lambda i: (0,) * a.ndim)
    sds = jax.ShapeDtypeStruct
    qk_w = MLA_HEADS * HEAD_PAD
    return pl.pallas_call(
        body, name="in_bwd", grid=(T // tt,),
        in_specs=[row(D_MODEL), row(D_MODEL), row(Q_RANK), row(KV_RANK), pl.BlockSpec((qk_w, tt), lambda i: (0, i)),
                  row(qk_w), row(MLA_WIDTH),
                  row(HGRN_WIDTH), row(HGRN_WIDTH), row(HGRN_WIDTH), row(HGRN_WIDTH), row(HEAD_PAD), row(HEAD_PAD),
                  full(w_pre), full(win), full(qnw), full(wq), full(kvnw), full(wk), full(wv)],
        out_specs=[row(D_MODEL), full(win), full(wq), full(wk), full(wv), full(w_pre), full(qnw), full(kvnw)],
        out_shape=[sds((T, D_MODEL), F32), sds(win.shape, F32), sds(wq.shape, F32), sds(wk.shape, F32),
                   sds(wv.shape, F32), sds(w_pre.shape, F32), sds(qnw.shape, F32), sds(kvnw.shape, F32)],
        compiler_params=_params(("arbitrary",)),
    )(x, dh1, cq, ckv, dq, dk, dv, dhq, dhf, dhi, dhg, rc, rs, w_pre, win, qnw, wq, kvnw, wk, wv)


def _arrange_weights(win_t, wuq_full, wukv):
    dt = win_t.dtype
    z = lambda n: jnp.zeros((n, D_MODEL), dt)
    s2 = Q_RANK + KV_RANK
    win_arr = jnp.concatenate([win_t[:s2], z(MLA_NOPE), win_t[s2:s2 + MLA_ROPE], z(HEAD_PAD - MLA_QK),
                               win_t[s2 + MLA_ROPE:]], axis=0)
    wq_arr = jnp.pad(wuq_full, ((0, 0), (0, 0), (0, HEAD_PAD - MLA_QK))).reshape(Q_RANK, MLA_HEADS * HEAD_PAD)
    wk_arr = jnp.pad(wukv[:, :, :MLA_NOPE], ((0, 0), (0, 0), (0, HEAD_PAD - MLA_NOPE))).reshape(
        KV_RANK, MLA_HEADS * HEAD_PAD)
    wv_arr = wukv[:, :, MLA_NOPE:].reshape(KV_RANK, MLA_WIDTH)
    return win_arr, wq_arr, wk_arr, wv_arr


def _unarrange_grads(dwin_arr, dwq_arr, dwk_arr, dwv_arr):
    s2 = Q_RANK + KV_RANK
    dwin = jnp.concatenate([dwin_arr[:s2], dwin_arr[s2 + MLA_NOPE:s2 + MLA_QK], dwin_arr[s2 + HEAD_PAD:]], axis=0)
    dwuq = dwq_arr.reshape(Q_RANK, MLA_HEADS, HEAD_PAD)[:, :, :MLA_QK]
    dwukv = jnp.concatenate([dwk_arr.reshape(KV_RANK, MLA_HEADS, HEAD_PAD)[:, :, :MLA_NOPE],
                             dwv_arr.reshape(KV_RANK, MLA_HEADS, MLA_V)], axis=-1)
    return dwin, dwuq, dwukv


def _rope_inv_freq():
    inv = 1.0 / (ROPE_THETA ** (jnp.arange(0, MLA_ROPE, 2, dtype=F32) / MLA_ROPE))
    z = lambda n: jnp.zeros((n,), F32)
    return jnp.concatenate([z(MLA_NOPE), inv, inv, z(HEAD_PAD - MLA_QK)]).reshape(1, HEAD_PAD)


def _local_step(x, pos, tgt, small, win_arr, wq_arr, wk_arr, wv_arr, late, place=None):
    invf = _rope_inv_freq()
    cq, ckv, xph, qb, kb, vb, kt, vt, rc, rs = _in_fwd(x, pos, invf, small["attn_pre_norm"], win_arr, small["mla_q_norm"],
                                               wq_arr, small["mla_kv_norm"], wk_arr, wv_arr)
    if place is None:
        o_raw, lse = _attn_fwd_t(qb, kb, vt)
        wout, wg, wu, wd = late
    else:
        o_raw, lse, *stacks = _attn_fwd_t(qb, kb, vt, gather=late)
        wout, wg, wu, wd = [lax.dynamic_update_slice(s, l[None], (place[1], 0, 0)) for s, l in zip(stacks, late)]
        wout = wout.reshape(D_MODEL, D_MODEL)
    oh_raw, states = _hgrn_fwd(xph, small["hgrn_lb_logits"])
    h1, y1, zb, mixb, g, up, dy2b, dh2, loss_acc, d_fpost = _ffn_fwd(
        x, o_raw, oh_raw, xph, tgt, wout, small["mla_out_norm"], small["hgrn_out_norm"], small["attn_post_norm"],
        small["ffn_pre_norm"], small["ffn_post_norm"], wg, wu, wd)
    dwg, dwu, dwd, dzp = _ffn_bwd(zb, g, up, dy2b, wg, wu, wd)
    ffn_grads = [] if place is None else [dwg, dwu, dwd]
    dh1, dwout, d_o, d_oh, dhg, dvec, d_fpre, d_post, d_mla, d_hg, *ffn_rs = _mid_bwd(
        dzp, dh2, h1, y1, mixb, o_raw, oh_raw, xph, wout, small["ffn_pre_norm"], small["attn_post_norm"],
        small["mla_out_norm"], small["hgrn_out_norm"], swap=ffn_grads)
    ffn_ps = _pair_sum(place, ffn_grads, ffn_rs, name="pair_sum_ffn") if ffn_grads else []
    dq, dk, dv, *ffn_ris = _attn_bwd_t(qb, kb, kt, vb, d_o, lse, dvec.reshape(lse.shape), send=ffn_ps)
    dhq, dhf, dhi, d_lbl = _hgrn_bwd(xph, small["hgrn_lb_logits"], states, d_oh)
    dx, dwin_arr, dwq_arr, dwk_arr, dwv_arr, d_pre, d_qn, d_kvn = _in_bwd(
        x, dh1, cq, ckv, dq, dk, dv, dhq, dhf, dhi, dhg, rc, rs, small["attn_pre_norm"], win_arr,
        small["mla_q_norm"], wq_arr, small["mla_kv_norm"], wk_arr, wv_arr)
    dwin, dwuq, dwukv = _unarrange_grads(dwin_arr, dwq_arr, dwk_arr, dwv_arr)
    loss = 0.5 * jnp.sum(loss_acc) * (1.0 / D_MODEL)
    grads = dict(attn_pre_norm=d_pre, w_in=dwin, mla_q_norm=d_qn, mla_w_uq=dwuq, mla_kv_norm=d_kvn, mla_w_ukv=dwukv,
                 mla_out_norm=d_mla, hgrn_lb_logits=d_lbl, hgrn_out_norm=d_hg, w_out=dwout, attn_post_norm=d_post,
                 ffn_pre_norm=d_fpre, w_gate=dwg, w_up=dwu, w_down=dwd, ffn_post_norm=d_fpost)
    if place is None:
        return loss, dx, grads
    return loss, dx, grads, (ffn_rs, ffn_ris)


def _place():
    x, y, c = lax.axis_index("x"), lax.axis_index("y"), lax.axis_index("c")
    others = [(1 - x, y), (x, 1 - y), (1 - x, 1 - y)]
    return x, y, c, 2 * x + y, (x, y, 1 - c), others


def _half(ref, c, rows):
    return ref.at[pl.ds(pl.multiple_of(c * rows, 8), rows)]


def _rcopy(src, dst, send, recv, k, to):
    return pltpu.make_async_remote_copy(src_ref=src, dst_ref=dst, send_sem=send.at[k], recv_sem=recv.at[k],
                                        device_id=to, device_id_type=MESH)


class _Gather:
    def __init__(self, ins, outs, send, recv):
        self.ins, self.outs, self.send, self.recv = ins, outs, send, recv
        self.n = len(ins)
        self.halves = [r.shape[0] // 2 for r in ins]
        _, _, self.c, self.me, self.sib, self.others = _place()

    def _each(self):
        for j, (px, py) in enumerate(self.others):
            for a in range(self.n):
                yield j * self.n + a, a, 2 * px + py, (px, py, self.c)

    def sends(self):
        return [_rcopy(_half(self.ins[a], self.c, self.halves[a]), _half(self.outs[a].at[self.me], self.c, self.halves[a]),
                       self.send, self.recv, k, to) for k, a, _, to in self._each()]

    def arrivals(self):
        parts = [(k, _half(self.outs[a].at[chip], self.c, self.halves[a]), to) for k, a, chip, to in self._each()]
        return [_rcopy(p, p, self.send, self.recv, k, to) for k, p, to in parts]

    def forwards(self):
        parts = [(k, _half(self.outs[a].at[chip], self.c, self.halves[a])) for k, a, chip, _ in self._each()]
        return [_rcopy(p, p, self.send, self.recv, 3 * self.n + k, self.sib) for k, p in parts]

    def forward_arrivals(self):
        parts = [(k, _half(self.outs[a].at[chip], 1 - self.c, self.halves[a])) for k, a, chip, _ in self._each()]
        return [_rcopy(p, p, self.send, self.recv, 3 * self.n + k, self.sib) for k, p in parts]

    @staticmethod
    def out_shapes(arrs):
        return [jax.ShapeDtypeStruct((N_CHIPS,) + a.shape, a.dtype) for a in arrs]

    @staticmethod
    def semaphores(arrs):
        return [pltpu.SemaphoreType.DMA((6 * len(arrs),)), pltpu.SemaphoreType.DMA((6 * len(arrs),))]


def _gather_chips(arrs, name):
    n = len(arrs)

    def body(*refs):
        gat = _Gather(refs[:n], refs[n:2 * n], *refs[2 * n:])
        sends, forwards = gat.sends(), gat.forwards()
        for cp in sends:
            cp.start()
        for arrival, fw in zip(gat.arrivals(), forwards):
            arrival.wait_recv()
            fw.start()
        for arrival in gat.forward_arrivals():
            arrival.wait_recv()
        for cp in sends + forwards:
            cp.wait_send()

    return pl.pallas_call(body, name=name, in_specs=[ANY] * n, out_specs=[ANY] * n, out_shape=_Gather.out_shapes(arrs),
                          scratch_shapes=_Gather.semaphores(arrs))(*arrs)


GRAD_BLOCKS = 2


def _pair_swap_copies(g_refs, r_refs, send, recv):
    _, _, c, _, sib, _ = _place()
    copies = []
    for a, (g, r) in enumerate(zip(g_refs, r_refs)):
        h = g.shape[1] // 2
        copies.append(_rcopy(g.at[:, pl.ds(pl.multiple_of((1 - c) * h, 8), h)], r, send, recv, a, sib))
    return copies


def _half_stack_shapes(gs, dtype=None):
    return [jax.ShapeDtypeStruct((N_CHIPS, g.shape[1] // 2, g.shape[2]), dtype or g.dtype) for g in gs]


def _pair_swap(gs, sm):
    n = len(gs)

    def body(*refs):
        g_refs, sm_ref = refs[:n], refs[n]
        r_refs, ssib_ref = refs[n + 1:2 * n + 1], refs[2 * n + 1]
        send, recv = refs[2 * n + 2:]
        copies = _pair_swap_copies(g_refs, r_refs, send, recv)
        copies.append(_rcopy(sm_ref, ssib_ref, send, recv, n, _place()[4]))
        for cp in copies:
            cp.start()
        for cp in copies:
            cp.wait()

    return pl.pallas_call(
        body, name="pair_swap", in_specs=[ANY] * (n + 1), out_specs=[ANY] * (n + 1),
        out_shape=_half_stack_shapes(gs) + [jax.ShapeDtypeStruct(sm.shape, sm.dtype)],
        scratch_shapes=[pltpu.SemaphoreType.DMA((n + 1,)), pltpu.SemaphoreType.DMA((n + 1,))],
    )(*gs, sm)


def _pair_sum(place, gs, rs, small=None, name="pair_sum"):
    n = len(gs)
    nb = GRAD_BLOCKS

    def body(place_ref, *refs):
        g_refs, r_refs, p_refs = refs[:n], refs[n:2 * n], refs[-n - 1:-1] if small else refs[-n:]
        for a in range(n):
            p_refs[a][0] = (g_refs[a][0] + r_refs[a][0]).astype(p_refs[a].dtype)
        if small:
            @pl.when((pl.program_id(0) == 0) & (pl.program_id(1) == 0))
            def _():
                refs[-1][...] = refs[2 * n][...] + refs[2 * n + 1][...]

    in_specs, out_specs = [], []
    for g in gs:
        blk = (1, g.shape[1] // 2 // nb, g.shape[2])
        in_specs.append(pl.BlockSpec(blk, lambda i, k, p: (k, p[0] * nb + i, 0)))
    for g in gs:
        blk = (1, g.shape[1] // 2 // nb, g.shape[2])
        in_specs.append(pl.BlockSpec(blk, lambda i, k, p: (k, i, 0)))
        out_specs.append(pl.BlockSpec(blk, lambda i, k, p: (k, i, 0)))
    out_shape = _half_stack_shapes(gs, BF16)
    if small:
        sm_spec = pl.BlockSpec(small[0].shape, lambda i, k, p: (0, 0))
        in_specs += [sm_spec, sm_spec]
        out_specs.append(sm_spec)
        out_shape.append(jax.ShapeDtypeStruct(small[0].shape, F32))
    return pl.pallas_call(
        body, name=name,
        grid_spec=pltpu.PrefetchScalarGridSpec(num_scalar_prefetch=1, grid=(nb, N_CHIPS), in_specs=in_specs,
                                               out_specs=out_specs),
        out_shape=out_shape,
        compiler_params=_params(("arbitrary", "arbitrary")),
    )(place, *gs, *rs, *(small or ()))


def _chip_swap_copies(p_refs, ri_refs, send, recv):
    _, _, c, _, _, others = _place()
    n = len(p_refs)
    return [_rcopy(p_refs[a].at[2 * px + py], ri_refs[a].at[j], send, recv, j * n + a, (px, py, c))
            for j, (px, py) in enumerate(others) for a in range(n)]


def _chip_swap_shapes(ps):
    return [jax.ShapeDtypeStruct((3,) + p.shape[1:], p.dtype) for p in ps]


def _chip_swap(ps, pair):
    n = len(ps)
    hs = SMALL_ROWS // 2

    def body(*refs):
        p_refs, pair_ref = refs[:n], refs[n]
        ri_refs, sm4_ref = refs[n + 1:2 * n + 1], refs[2 * n + 1]
        send, recv, lsem = refs[2 * n + 2:]
        x, y, c, me, sib, others = _place()
        local = pltpu.make_async_copy(pair_ref, sm4_ref.at[me], lsem.at[0])
        local.start()
        copies = _chip_swap_copies(p_refs, ri_refs, send, recv)
        arrivals = list(copies)
        for j, (px, py) in enumerate(others):
            copies.append(_rcopy(_half(pair_ref, c, hs), _half(sm4_ref.at[me], c, hs), send, recv, 3 * n + j, (px, py, c)))
            part = _half(sm4_ref.at[2 * px + py], c, hs)
            arrivals.append(_rcopy(part, part, send, recv, 3 * n + j, (px, py, c)))
        for cp in copies:
            cp.start()
        for arrival in arrivals:
            arrival.wait_recv()
        for cp in copies:
            cp.wait_send()
        local.wait()

    k = 3 * (n + 1)
    return pl.pallas_call(
        body, name="chip_swap", in_specs=[ANY] * (n + 1), out_specs=[ANY] * (n + 1),
        out_shape=_chip_swap_shapes(ps) + [jax.ShapeDtypeStruct((N_CHIPS,) + pair.shape, pair.dtype)],
        scratch_shapes=[pltpu.SemaphoreType.DMA((k,)), pltpu.SemaphoreType.DMA((k,)), pltpu.SemaphoreType.DMA((1,))],
    )(*ps, pair)


def _chip_sum(place, gs, rs, ris):
    n = len(gs)
    nb = GRAD_BLOCKS

    def body(place_ref, *refs):
        g_refs, r_refs, ri_refs, o_refs = refs[:n], refs[n:2 * n], refs[2 * n:3 * n], refs[3 * n:]
        for a in range(n):
            ri = ri_refs[a]
            o_refs[a][...] = (g_refs[a][0] + r_refs[a][0]) + ri[0].astype(F32) + ri[1].astype(F32) + ri[2].astype(F32)

    in_specs, out_specs, out_shape = [], [], []
    for g in gs:
        blk = (1, g.shape[1] // 2 // nb, g.shape[2])
        in_specs.append(pl.BlockSpec(blk, lambda i, p: (p[1], p[0] * nb + i, 0)))
    for g in gs:
        blk = (1, g.shape[1] // 2 // nb, g.shape[2])
        in_specs.append(pl.BlockSpec(blk, lambda i, p: (p[1], i, 0)))
    for g in gs:
        rb = g.shape[1] // 2 // nb
        in_specs.append(pl.BlockSpec((3, rb, g.shape[2]), lambda i, p: (0, i, 0)))
        out_specs.append(pl.BlockSpec((rb, g.shape[2]), lambda i, p: (p[0] * nb + i, 0)))
        out_shape.append(jax.ShapeDtypeStruct(g.shape[1:], F32))
    return pl.pallas_call(
        body, name="chip_sum",
        grid_spec=pltpu.PrefetchScalarGridSpec(num_scalar_prefetch=1, grid=(nb,), in_specs=in_specs, out_specs=out_specs),
        out_shape=out_shape,
        compiler_params=_params(("arbitrary",)),
    )(place, *gs, *rs, *ris)


def _pair_fill(gfs, sm4):
    n = len(gfs)
    hs = SMALL_ROWS // 2

    def body(*refs):
        g_refs, sm4_ref = refs[n + 1:2 * n + 1], refs[2 * n + 1]
        send, recv = refs[2 * n + 2:]
        x, y, c, me, sib, others = _place()
        copies, waits = [], []
        for a in range(n):
            h = gfs[a].shape[0] // 2
            mine, theirs = _half(g_refs[a], c, h), _half(g_refs[a], 1 - c, h)
            copies.append(pltpu.make_async_remote_copy(src_ref=mine, dst_ref=mine, send_sem=send.at[a],
                                                       recv_sem=recv.at[a], device_id=sib, device_id_type=MESH))
            waits.append(pltpu.make_async_remote_copy(src_ref=theirs, dst_ref=theirs, send_sem=send.at[a],
                                                      recv_sem=recv.at[a], device_id=sib, device_id_type=MESH))
        for j, (px, py) in enumerate(others):
            chip = 2 * px + py
            mine, theirs = _half(sm4_ref.at[chip], c, hs), _half(sm4_ref.at[chip], 1 - c, hs)
            copies.append(pltpu.make_async_remote_copy(src_ref=mine, dst_ref=mine, send_sem=send.at[n + j],
                                                       recv_sem=recv.at[n + j], device_id=sib, device_id_type=MESH))
            waits.append(pltpu.make_async_remote_copy(src_ref=theirs, dst_ref=theirs, send_sem=send.at[n + j],
                                                      recv_sem=recv.at[n + j], device_id=sib, device_id_type=MESH))
        for cp in copies:
            cp.start()
        for w in waits:
            w.wait_recv()
        for cp in copies:
            cp.wait_send()

    return pl.pallas_call(
        body, name="pair_fill", in_specs=[ANY] * (n + 1), out_specs=[ANY] * (n + 1),
        out_shape=[jax.ShapeDtypeStruct(g.shape, g.dtype) for g in gfs] + [jax.ShapeDtypeStruct(sm4.shape, sm4.dtype)],
        input_output_aliases={i: i for i in range(n + 1)},
        scratch_shapes=[pltpu.SemaphoreType.DMA((n + 3,)), pltpu.SemaphoreType.DMA((n + 3,))],
    )(*gfs, sm4)


def _adamw_math(w, g, m, v):
    m = ADAM_B1 * m + (1.0 - ADAM_B1) * g
    v = ADAM_B2 * v + (1.0 - ADAM_B2) * (g * g)
    m_hat = m / (1.0 - ADAM_B1 ** ADAM_STEP)
    v_hat = v / (1.0 - ADAM_B2 ** ADAM_STEP)
    return -ADAM_LR * (m_hat / (jnp.sqrt(v_hat) + ADAM_EPS) + ADAM_WD * w), m, v


def _adamw(w, g, m, v, rb, name):
    rows, cols = w.shape

    def body(w_ref, g_ref, m_ref, v_ref, d_ref, mo_ref, vo_ref):
        d, mo, vo = _adamw_math(w_ref[...], g_ref[...], m_ref[...], v_ref[...])
        d_ref[...] = d
        mo_ref[...] = mo
        vo_ref[...] = vo

    spec = pl.BlockSpec((rb, cols), lambda i: (i, 0))
    return pl.pallas_call(
        body, name=name, grid=(rows // rb,), in_specs=[spec] * 4, out_specs=[spec] * 3,
        out_shape=[jax.ShapeDtypeStruct(w.shape, F32)] * 3,
        compiler_params=_params(("arbitrary",)),
    )(w, g, m, v)


def _adamw_small(sm4, w, m, v):
    def body(sm4_ref, w_ref, m_ref, v_ref, g_ref, d_ref, mo_ref, vo_ref):
        g = ((sm4_ref[0] + sm4_ref[1]) + sm4_ref[2]) + sm4_ref[3]
        g_ref[...] = g
        d, mo, vo = _adamw_math(w_ref[...], g, m_ref[...], v_ref[...])
        d_ref[...] = d
        mo_ref[...] = mo
        vo_ref[...] = vo

    return pl.pallas_call(
        body, name="adamw_small", out_shape=[jax.ShapeDtypeStruct(w.shape, F32)] * 4,
        compiler_params=pltpu.CompilerParams(vmem_limit_bytes=VMEM_LIMIT),
    )(sm4, w, m, v)


SMALL_NAMES = ("attn_pre_norm", "mla_q_norm", "mla_kv_norm", "mla_w_ukv", "mla_out_norm", "hgrn_lb_logits",
               "hgrn_out_norm", "attn_post_norm", "ffn_pre_norm", "ffn_post_norm")
BIG_NAMES = ("w_in", "mla_w_uq", "w_out", "w_gate", "w_up", "w_down")
WEIGHT_NAMES = ("attn_pre_norm", "w_in", "mla_q_norm", "mla_w_uq", "mla_kv_norm", "mla_w_ukv", "mla_out_norm",
                "hgrn_lb_logits", "hgrn_out_norm", "w_out", "attn_post_norm", "ffn_pre_norm", "w_gate", "w_up", "w_down",
                "ffn_post_norm")


UQ_COMM_SHAPE = (192, 384)


def _pack_small(vals, extra=None):
    parts = [vals[n].reshape(-1) for n in SMALL_NAMES]
    if extra is not None:
        parts.append(extra.reshape(1))
    flat = jnp.concatenate(parts)
    return jnp.pad(flat, (0, SMALL_ROWS * D_MODEL - flat.shape[0])).reshape(SMALL_ROWS, D_MODEL)


def _unpack_small(buf, shapes):
    flat = buf.reshape(-1)
    out, off = {}, 0
    for n, size in zip(SMALL_NAMES, SMALL_SIZES):
        out[n] = flat[off:off + size].reshape(shapes[n])
        off += size
    return out


def kernel(x, positions, attn_pre_norm, w_in, mla_q_norm, mla_w_uq, mla_kv_norm, mla_w_ukv, mla_out_norm, hgrn_lb_logits, hgrn_out_norm, w_out, attn_post_norm, ffn_pre_norm, w_gate, w_up, w_down, ffn_post_norm, loss_target, m_attn_pre_norm, m_w_in, m_mla_q_norm, m_mla_w_uq, m_mla_kv_norm, m_mla_w_ukv, m_mla_out_norm, m_hgrn_lb_logits, m_hgrn_out_norm, m_w_out, m_attn_post_norm, m_ffn_pre_norm, m_w_gate, m_w_up, m_w_down, m_ffn_post_norm, v_attn_pre_norm, v_w_in, v_mla_q_norm, v_mla_w_uq, v_mla_kv_norm, v_mla_w_ukv, v_mla_out_norm, v_hgrn_lb_logits, v_hgrn_out_norm, v_w_out, v_attn_post_norm, v_ffn_pre_norm, v_w_gate, v_w_up, v_w_down, v_ffn_post_norm):
    args = locals()
    W = {n: args[n] for n in WEIGHT_NAMES}
    M = {n: args["m_" + n] for n in WEIGHT_NAMES}
    V = {n: args["v_" + n] for n in WEIGHT_NAMES}
    T = x.shape[1]
    cx, cy, cc = lax.axis_index("x"), lax.axis_index("y"), lax.axis_index("c")

    win_rows = D_IN // N_CHIPS
    shard2d = {"w_in": (win_rows, D_MODEL), "mla_w_uq": (Q_RANK // N_CHIPS, MLA_HEADS * MLA_QK),
               "w_out": (D_MODEL // N_CHIPS, D_MODEL), "w_gate": (FF_SHARD, D_MODEL), "w_up": (FF_SHARD, D_MODEL),
               "w_down": (FF_SHARD, D_MODEL)}
    transposed = ("w_in", "w_gate", "w_up")
    to2d = lambda n, a: a[0].T if n in transposed else a.reshape(shard2d[n])
    from2d = lambda n, t: t.T[None] if n in transposed else t.reshape(W[n].shape)
    me = 2 * cx + cy
    place = jnp.stack([cc, me]).astype(jnp.int32)
    local_b = [to2d(n, W[n]).astype(BF16) for n in BIG_NAMES]
    local_b[0] = jnp.pad(local_b[0], ((0, FF_SHARD - win_rows), (0, 0)))
    stacks = _gather_chips(local_b[:2], "gather_weights")
    win4, wuq4 = [lax.dynamic_update_slice(s, l[None], (me, 0, 0)) for s, l in zip(stacks, local_b)]
    win_t = win4[:, :win_rows].reshape(D_IN, D_MODEL)
    wuq_full = wuq4.reshape(Q_RANK, MLA_HEADS, MLA_QK)
    win_arr, wq_arr, wk_arr, wv_arr = _arrange_weights(win_t, wuq_full, mla_w_ukv[0].astype(BF16))
    small = {n: W[n][0] if n == "mla_w_ukv" else W[n].reshape(-1, W[n].shape[-1]) for n in SMALL_NAMES}

    loss_local, dx, grads, (ffn_rs, ffn_ris) = _local_step(x[0], positions.reshape(T, 1), loss_target[0], small, win_arr,
                                                           wq_arr, wk_arr, wv_arr, local_b[2:], place)

    dwin4 = jnp.pad(grads["w_in"].reshape(N_CHIPS, win_rows, D_MODEL), ((0, 0), (0, FF_SHARD - win_rows), (0, 0)))
    gs = [dwin4, grads["mla_w_uq"].reshape((N_CHIPS,) + UQ_COMM_SHAPE), grads["w_out"].reshape((N_CHIPS,) + shard2d["w_out"])]
    ffn_gs = [grads["w_gate"], grads["w_up"], grads["w_down"]]
    sm = _pack_small(grads, loss_local)
    *rs, ssib = _pair_swap(gs, sm)
    *ps, pair = _pair_sum(place, gs, rs, small=(sm, ssib))
    *ris, sm4 = _chip_swap(ps, pair)
    gfs = _chip_sum(place, gs + ffn_gs, rs + ffn_rs, ris + ffn_ris)
    *gfin, smf = _pair_fill(gfs, sm4)

    row_blocks = {"w_in": 216, "mla_w_uq": 96, "w_out": 256, "w_gate": 352, "w_up": 352, "w_down": 352}
    G, DW, NM, NV = {}, {}, {}, {}
    for k, n in enumerate(BIG_NAMES):
        g2 = gfin[k][:win_rows] if n == "w_in" else gfin[k].reshape(shard2d[n])
        d, mo, vo = _adamw(to2d(n, W[n]), g2, to2d(n, M[n]), to2d(n, V[n]), row_blocks[n], "adamw_" + n)
        G[n], DW[n], NM[n], NV[n] = (from2d(n, t) for t in (g2, d, mo, vo))
    gs_buf, ds, ms, vs = _adamw_small(smf, _pack_small(W), _pack_small(M), _pack_small(V))
    shapes = {n: W[n].shape for n in SMALL_NAMES}
    for dst, buf in ((G, gs_buf), (DW, ds), (NM, ms), (NV, vs)):
        dst.update(_unpack_small(buf, shapes))

    loss = gs_buf.reshape(-1)[sum(SMALL_SIZES)]
    return (loss, dx[None], *[G[n] for n in WEIGHT_NAMES], *[DW[n] for n in WEIGHT_NAMES],
            *[NM[n] for n in WEIGHT_NAMES], *[NV[n] for n in WEIGHT_NAMES])
```

```python
import jax
import jax.numpy as jnp
from jax import lax
from jax.experimental import pallas as pl
from jax.experimental.pallas import tpu as pltpu

F32 = jnp.float32
BF16 = jnp.bfloat16
MXU_DTYPE = BF16

D_MODEL = 1024
MLA_HEADS = 8
MLA_NOPE = 64
MLA_ROPE = 32
MLA_V = 64
MLA_QK = MLA_NOPE + MLA_ROPE
Q_RANK = 384
KV_RANK = 128
MLA_WIDTH = MLA_HEADS * MLA_V
HEAD_PAD = 128
HGRN_HEADS = 4
HGRN_DIM = 128
HGRN_WIDTH = HGRN_HEADS * HGRN_DIM
CHUNK = 64
SUB = 16
D_IN = Q_RANK + KV_RANK + MLA_ROPE + 4 * HGRN_WIDTH
D_IN_ARR = Q_RANK + KV_RANK + HEAD_PAD + 4 * HGRN_WIDTH
D_FF = 2816
N_CHIPS = 4
FF_SHARD = D_FF // N_CHIPS
EPS = 1e-6
ROPE_THETA = 10000.0
ATTN_SCALE = MLA_QK ** -0.5
NEG_BIG = -1e30

ADAM_LR = 0.001
ADAM_B1 = 0.9
ADAM_B2 = 0.999
ADAM_EPS = 1e-08
ADAM_WD = 0.01
ADAM_STEP = 10

VMEM_LIMIT = 56 * 1024 * 1024

SMALL_ROWS = 144
SMALL_SIZES = (1024, 384, 128, 131072, 512, 1024, 512, 1024, 1024, 1024)

MESH = pl.DeviceIdType.MESH
ANY = pl.BlockSpec(memory_space=pl.ANY)


def _dot(a, b, dims, exact):
    if exact:
        return lax.dot_general(a.astype(F32), b.astype(F32), (dims, ((), ())), precision=lax.Precision.HIGH,
                               preferred_element_type=F32)
    return lax.dot_general(a.astype(MXU_DTYPE), b.astype(MXU_DTYPE), (dims, ((), ())), preferred_element_type=F32)


def _mm(a, b, exact=False):
    return _dot(a, b, ((1,), (0,)), exact)


def _mm_nt(a, b, exact=False):
    return _dot(a, b, ((1,), (1,)), exact)


def _mm_tn(a, b, exact=False):
    return _dot(a, b, ((0,), (0,)), exact)


def _rms_fwd(x, w):
    r = lax.rsqrt(jnp.mean(x * x, axis=-1, keepdims=True) + EPS)
    xn = x * r
    return xn * w, xn, r


def _rms_bwd(dy, xn, r, w):
    dxn = dy * w
    dx = r * (dxn - xn * jnp.mean(dxn * xn, axis=-1, keepdims=True))
    dw = jnp.sum(dy * xn, axis=0, keepdims=True)
    return dx, dw


def _group_sums(v, gs):
    t, n = v.shape
    lane = lax.broadcasted_iota(jnp.int32, (t, 128), 1)
    out = []
    for p in range(n // 128):
        vb = v[:, 128 * p:128 * (p + 1)]
        if gs == 128:
            out.append(jnp.sum(vb, axis=-1, keepdims=True))
        else:
            out.append(jnp.sum(jnp.where(lane < 64, vb, 0.0), axis=-1, keepdims=True))
            out.append(jnp.sum(jnp.where(lane >= 64, vb, 0.0), axis=-1, keepdims=True))
    return out


def _group_bcast(sums, gs, t):
    lane = lax.broadcasted_iota(jnp.int32, (t, 128), 1)
    if gs == 128:
        return jnp.concatenate([jnp.broadcast_to(s, (t, 128)) for s in sums], axis=-1)
    return jnp.concatenate([jnp.where(lane < 64, sums[2 * p], sums[2 * p + 1]) for p in range(len(sums) // 2)],
                           axis=-1)


def _grms_fwd(x, w, gs):
    t = x.shape[0]
    r = lax.rsqrt(_group_bcast(_group_sums(x * x, gs), gs, t) * (1.0 / gs) + EPS)
    xn = x * r
    return xn * w, xn, r


def _grms_bwd(dy, xn, r, w, gs):
    t = dy.shape[0]
    dxn = dy * w
    dx = r * (dxn - xn * (_group_bcast(_group_sums(dxn * xn, gs), gs, t) * (1.0 / gs)))
    dw = jnp.sum(dy * xn, axis=0, keepdims=True)
    return dx, dw


def _rope_tables(c_tab, s_tab):
    lane = lax.broadcasted_iota(jnp.int32, c_tab.shape, 1)
    first = (lane >= MLA_NOPE) & (lane < MLA_NOPE + MLA_ROPE // 2)
    second = (lane >= MLA_NOPE + MLA_ROPE // 2) & (lane < MLA_QK)
    return c_tab, jnp.where(first, -s_tab, 0.0), jnp.where(second, s_tab, 0.0)


def _rope(v, c, sa, sb):
    return v * c + pltpu.roll(v, HEAD_PAD - MLA_ROPE // 2, 1) * sa + pltpu.roll(v, MLA_ROPE // 2, 1) * sb


def _rope_bwd(d, c, sa, sb):
    return d * c - pltpu.roll(d, HEAD_PAD - MLA_ROPE // 2, 1) * sa - pltpu.roll(d, MLA_ROPE // 2, 1) * sb


def _params(sem, vmem=VMEM_LIMIT):
    return pltpu.CompilerParams(dimension_semantics=sem, vmem_limit_bytes=vmem)


def _in_fwd(x, pos, invf, w_pre, win, qnw, wq, kvnw, wk, wv, tt=256):
    T = x.shape[0]

    def body(x_ref, pos_ref, invf_ref, wpre_ref, win_ref, qnw_ref, wq_ref, kvnw_ref, wk_ref, wv_ref,
             cq_ref, ckv_ref, xph_ref, q_ref, k_ref, v_ref, kt_ref, vt_ref, rc_ref, rs_ref):
        u, _, _ = _rms_fwd(x_ref[...], wpre_ref[...])
        lo = Q_RANK + KV_RANK + HEAD_PAD
        xp = _mm_nt(u, win_ref[:lo, :])
        xph_ref[...] = _mm_nt(u, win_ref[lo:, :])
        cq = xp[:, :Q_RANK]
        ckv = xp[:, Q_RANK:Q_RANK + KV_RANK]
        kr = xp[:, Q_RANK + KV_RANK:]
        cq_ref[...] = cq
        ckv_ref[...] = ckv
        ang = pos_ref[...].astype(F32) * invf_ref[...]
        c_tab = jnp.cos(ang)
        s_tab = jnp.sin(ang)
        rc_ref[...] = c_tab
        rs_ref[...] = s_tab
        c, sa, sb = _rope_tables(c_tab, s_tab)
        qn, _, _ = _rms_fwd(cq, qnw_ref[...])
        q = _mm(qn, wq_ref[...])
        kvn, _, _ = _rms_fwd(ckv, kvnw_ref[...])
        kn = _mm(kvn, wk_ref[...])
        v = _mm(kvn, wv_ref[...])
        v_ref[...] = v.astype(v_ref.dtype)
        vt_ref[...] = v.T.astype(vt_ref.dtype)
        krr = _rope(kr, c, sa, sb)
        for h in range(MLA_HEADS):
            sl = slice(HEAD_PAD * h, HEAD_PAD * (h + 1))
            q_ref[:, sl] = _rope(q[:, sl], c, sa, sb).astype(q_ref.dtype)
            kh = kn[:, sl] + krr
            k_ref[:, sl] = kh.astype(k_ref.dtype)
            kt_ref[sl, :] = kh.T.astype(kt_ref.dtype)

    row = lambda w: pl.BlockSpec((tt, w), lambda i: (i, 0))
    full = lambda a: pl.BlockSpec(a.shape, lambda i: (0,) * a.ndim)
    qk_w = MLA_HEADS * HEAD_PAD
    return pl.pallas_call(
        body, name="in_fwd", grid=(T // tt,),
        in_specs=[row(D_MODEL), row(1), full(invf), full(w_pre), full(win), full(qnw), full(wq), full(kvnw),
                  full(wk), full(wv)],
        out_specs=[row(Q_RANK), row(KV_RANK), row(4 * HGRN_WIDTH), row(qk_w), row(qk_w), row(MLA_WIDTH),
                   pl.BlockSpec((qk_w, tt), lambda i: (0, i)), pl.BlockSpec((MLA_WIDTH, tt), lambda i: (0, i)),
                   row(HEAD_PAD), row(HEAD_PAD)],
        out_shape=[jax.ShapeDtypeStruct((T, Q_RANK), F32), jax.ShapeDtypeStruct((T, KV_RANK), F32),
                   jax.ShapeDtypeStruct((T, 4 * HGRN_WIDTH), F32), jax.ShapeDtypeStruct((T, qk_w), MXU_DTYPE),
                   jax.ShapeDtypeStruct((T, qk_w), MXU_DTYPE), jax.ShapeDtypeStruct((T, MLA_WIDTH), MXU_DTYPE),
                   jax.ShapeDtypeStruct((qk_w, T), MXU_DTYPE), jax.ShapeDtypeStruct((MLA_WIDTH, T), MXU_DTYPE),
                   jax.ShapeDtypeStruct((T, HEAD_PAD), F32), jax.ShapeDtypeStruct((T, HEAD_PAD), F32)],
        compiler_params=_params(("arbitrary",)),
    )(x, pos, invf, w_pre, win, qnw, wq, kvnw, wk, wv)


def _attn_fwd_t(qb, kb, vt, gather=(), tq=256, hps=8):
    T = qb.shape[0]
    nq = T // tq
    ng = len(gather)
    steps = (MLA_HEADS // hps) * nq
    pass_on = steps - 3

    def body(q_ref, k_ref, vt_ref, *rest):
        o_ref, lse_ref = rest[ng:ng + 2]
        acc_scr = rest[2 * ng + 2]
        qi = pl.program_id(1)
        step_no = pl.program_id(0) * nq + qi
        if ng:
            gat = _Gather(rest[:ng], rest[ng + 2:2 * ng + 2], *rest[2 * ng + 3:])

            @pl.when(step_no == 0)
            def _():
                for cp in gat.sends():
                    cp.start()

            @pl.when(step_no == pass_on)
            def _():
                for arrival in gat.arrivals():
                    arrival.wait_recv()
                for cp in gat.forwards():
                    cp.start()

        heads = [slice(HEAD_PAD * a, HEAD_PAD * (a + 1)) for a in range(hps)]
        acc_scr[...] = jnp.zeros_like(acc_scr)

        def step(j, carry, masked):
            start = pl.multiple_of(j * tq, tq)
            scores = [_mm_nt(k_ref[pl.ds(start, tq), heads[a]], q_ref[:, heads[a]]) for a in range(hps)]
            new = []
            for a in range(hps):
                m, l = carry[a]
                s = scores[a] * ATTN_SCALE
                if masked:
                    kk = lax.broadcasted_iota(jnp.int32, (tq, tq), 0)
                    qq = lax.broadcasted_iota(jnp.int32, (tq, tq), 1)
                    s = jnp.where(kk <= qq, s, NEG_BIG)
                m_new = jnp.maximum(m, jnp.max(s, axis=0, keepdims=True))
                alpha = jnp.exp(m - m_new)
                p = jnp.exp(s - m_new)
                l = l * alpha + jnp.sum(p, axis=0, keepdims=True)
                vtj = vt_ref[2 * MLA_V * (a // 2):2 * MLA_V * (a // 2 + 1), pl.ds(start, tq)]
                acc_scr[a] = acc_scr[a] * alpha + _mm(vtj, p)
                new.append((m_new, l))
            return tuple(new)

        init = tuple((jnp.full((1, tq), NEG_BIG, F32), jnp.zeros((1, tq), F32)) for _ in range(hps))
        carry = lax.fori_loop(0, qi, lambda j, c: step(j, c, False), init)
        carry = step(qi, carry, True)
        row = lax.broadcasted_iota(jnp.int32, (2 * MLA_V, tq), 0)
        for pr in range(hps // 2):
            (m0, l0), (m1, l1) = carry[2 * pr], carry[2 * pr + 1]
            ot = jnp.where(row < MLA_V, acc_scr[2 * pr] / l0, acc_scr[2 * pr + 1] / l1)
            o_ref[:, 2 * MLA_V * pr:2 * MLA_V * (pr + 1)] = ot.T
            lse_ref[pr, 0:1, :] = m0 + jnp.log(l0)
            lse_ref[pr, 1:2, :] = m1 + jnp.log(l1)

        if ng:
            @pl.when(step_no == steps - 1)
            def _():
                for arrival in gat.forward_arrivals():
                    arrival.wait_recv()
                for cp in gat.sends() + gat.forwards():
                    cp.wait_send()

    return pl.pallas_call(
        body, name="attn_fwd", grid=(MLA_HEADS // hps, nq),
        in_specs=[pl.BlockSpec((tq, hps * HEAD_PAD), lambda g, i: (i, g)),
                  pl.BlockSpec((T, hps * HEAD_PAD), lambda g, i: (0, g)),
                  pl.BlockSpec((hps * MLA_V, T), lambda g, i: (g, 0))] + [ANY] * ng,
        out_specs=[pl.BlockSpec((tq, hps * MLA_V), lambda g, i: (i, g)),
                   pl.BlockSpec((hps // 2, 2, tq), lambda g, i: (g, 0, i))] + [ANY] * ng,
        out_shape=[jax.ShapeDtypeStruct((T, MLA_WIDTH), F32), jax.ShapeDtypeStruct((MLA_HEADS // 2, 2, T), F32)]
        + _Gather.out_shapes(gather),
        scratch_shapes=[pltpu.VMEM((hps, 2 * MLA_V, tq), F32)] + (_Gather.semaphores(gather) if ng else []),
        compiler_params=_params(("arbitrary", "arbitrary")),
    )(qb, kb, vt, *gather)


def _attn_bwd_t(qb, kb, kt, vb, dob, lse, dvec, send=(), tq=256, hps=4):
    T = qb.shape[0]
    nq = T // tq
    ns = len(send)
    steps = (MLA_HEADS // hps) * nq

    def body(q_ref, k_ref, kt_ref, v_ref, do_ref, lse_ref, d_ref, *rest):
        dqt_ref, dk_ref, dv_ref = rest[ns:ns + 3]
        va_scr, dv_scr = rest[2 * ns + 3:2 * ns + 5]
        j = pl.program_id(1)
        step_no = pl.program_id(0) * nq + j
        if ns:
            @pl.when(step_no == 0)
            def _():
                for cp in _chip_swap_copies(rest[:ns], rest[ns + 3:2 * ns + 3], *rest[2 * ns + 5:]):
                    cp.start()

        @pl.when(j == 0)
        def _():
            dqt_ref[...] = jnp.zeros_like(dqt_ref)

        lane = lax.broadcasted_iota(jnp.int32, (tq, 2 * MLA_V), 1)
        heads = [slice(HEAD_PAD * a, HEAD_PAD * (a + 1)) for a in range(hps)]
        pairs = [slice(2 * MLA_V * p, 2 * MLA_V * (p + 1)) for p in range(hps // 2)]
        for pr in range(hps // 2):
            vpair = v_ref[:, pairs[pr]]
            va_scr[2 * pr] = jnp.where(lane < MLA_V, vpair, jnp.zeros_like(vpair))
            va_scr[2 * pr + 1] = jnp.where(lane >= MLA_V, vpair, jnp.zeros_like(vpair))
        dk_ref[...] = jnp.zeros_like(dk_ref)
        dv_scr[...] = jnp.zeros_like(dv_scr)

        def step(i, masked):
            start = pl.multiple_of(i * tq, tq)
            rows = pl.ds(start, tq)
            scores = [_mm_nt(k_ref[:, heads[a]], q_ref[rows, heads[a]]) for a in range(hps)]
            dps = [_mm_nt(va_scr[a], do_ref[rows, pairs[a // 2]]) for a in range(hps)]
            for a in range(hps):
                pr, r = a // 2, a % 2
                p = jnp.exp(scores[a] * ATTN_SCALE - lse_ref[pr, r:r + 1, rows])
                if masked:
                    kk = lax.broadcasted_iota(jnp.int32, (tq, tq), 0)
                    qq = lax.broadcasted_iota(jnp.int32, (tq, tq), 1)
                    p = jnp.where(kk <= qq, p, 0.0)
                ds = p * (dps[a] - d_ref[pr, r:r + 1, rows]) * ATTN_SCALE
                dv_scr[a] += _mm(p, do_ref[rows, pairs[pr]])
                dk_ref[:, heads[a]] += _mm(ds, q_ref[rows, heads[a]])
                dqt_ref[heads[a], rows] += _mm(kt_ref[heads[a], :], ds)

        def loop_body(i, _):
            step(i, False)
            return 0

        step(j, True)
        lax.fori_loop(j + 1, nq, loop_body, 0)
        for pr in range(hps // 2):
            dv_ref[:, pairs[pr]] = jnp.where(lane < MLA_V, dv_scr[2 * pr], dv_scr[2 * pr + 1])

        if ns:
            @pl.when(step_no == steps - 1)
            def _():
                for cp in _chip_swap_copies(rest[:ns], rest[ns + 3:2 * ns + 3], *rest[2 * ns + 5:]):
                    cp.wait()

    stat = pl.BlockSpec((hps // 2, 2, T), lambda g, j: (g, 0, 0))
    return pl.pallas_call(
        body, name="attn_bwd", grid=(MLA_HEADS // hps, nq),
        in_specs=[pl.BlockSpec((T, hps * HEAD_PAD), lambda g, j: (0, g)),
                  pl.BlockSpec((tq, hps * HEAD_PAD), lambda g, j: (j, g)),
                  pl.BlockSpec((hps * HEAD_PAD, tq), lambda g, j: (g, j)),
                  pl.BlockSpec((tq, hps * MLA_V), lambda g, j: (j, g)),
                  pl.BlockSpec((T, hps * MLA_V), lambda g, j: (0, g)), stat, stat] + [ANY] * ns,
        out_specs=[pl.BlockSpec((hps * HEAD_PAD, T), lambda g, j: (g, 0)),
                   pl.BlockSpec((tq, hps * HEAD_PAD), lambda g, j: (j, g)),
                   pl.BlockSpec((tq, hps * MLA_V), lambda g, j: (j, g))] + [ANY] * ns,
        out_shape=[jax.ShapeDtypeStruct((MLA_HEADS * HEAD_PAD, T), F32),
                   jax.ShapeDtypeStruct((T, MLA_HEADS * HEAD_PAD), F32),
                   jax.ShapeDtypeStruct((T, MLA_WIDTH), F32)] + _chip_swap_shapes(send),
        scratch_shapes=[pltpu.VMEM((hps, tq, 2 * MLA_V), vb.dtype), pltpu.VMEM((hps, tq, 2 * MLA_V), F32)]
        + ([pltpu.SemaphoreType.DMA((3 * ns,)), pltpu.SemaphoreType.DMA((3 * ns,))] if ns else []),
        compiler_params=_params(("arbitrary", "arbitrary")),
    )(qb, kb, kt, vb, dob, lse, dvec, *send)


def _cumsum_rows(x):
    n = x.shape[0]
    row = lax.broadcasted_iota(jnp.int32, x.shape, 0)
    s = 1
    while s < n:
        x = x + jnp.where(row >= s, pltpu.roll(x, s, 0), 0.0)
        s *= 2
    return x


def _rev_cumsum_rows(x):
    n = x.shape[0]
    row = lax.broadcasted_iota(jnp.int32, x.shape, 0)
    s = 1
    while s < n:
        x = x + jnp.where(row < n - s, pltpu.roll(x, n - s, 0), 0.0)
        s *= 2
    return x


def _lb_from_logits(l):
    l0, l1 = l[0:1, :], l[1:2, :]
    m = jnp.maximum(l0, l1)
    e0, e1 = jnp.exp(l0 - m), jnp.exp(l1 - m)
    return e0 / (e0 + e1)


def _hgrn_gates(hq, hf, lb):
    sig_f = jax.nn.sigmoid(hf)
    f = lb + (1.0 - lb) * sig_f
    sig_q = jax.nn.sigmoid(hq)
    return sig_f, f, jnp.log(f), 1.0 - f, sig_q, hq * sig_q


def _hgrn_intra(q, kk, b, exact=False):
    row = lax.broadcasted_iota(jnp.int32, b.shape, 0)
    qs, ks, eqs, eks, a_rows = [], [], [], [], []
    for i in range(CHUNK // SUB):
        ref = b[SUB * i:SUB * i + 1, :]
        eq = jnp.exp(b[SUB * i:SUB * (i + 1), :] - ref)
        ek = jnp.exp(jnp.where(row < SUB * (i + 1), ref - b, NEG_BIG))
        qi = q[SUB * i:SUB * (i + 1), :] * eq
        ki = kk * ek
        a_rows.append(_mm_nt(qi, ki, exact))
        qs.append(qi), ks.append(ki), eqs.append(eq), eks.append(ek)
    tt = lax.broadcasted_iota(jnp.int32, (CHUNK, CHUNK), 0)
    ss = lax.broadcasted_iota(jnp.int32, (CHUNK, CHUNK), 1)
    causal = ss <= tt
    a = jnp.where(causal, jnp.concatenate(a_rows, axis=0), 0.0)
    return a, causal, qs, ks, eqs, eks


def _hgrn_fwd(xph, lbl, tg=512):
    T = xph.shape[0]
    ng, ncg = T // tg, tg // CHUNK
    cols = [slice(HGRN_DIM * h, HGRN_DIM * (h + 1)) for h in range(HGRN_HEADS)]

    def body(lbl_ref, hq_ref, hf_ref, hi_ref, o_ref, st_ref, s_scr):
        @pl.when(pl.program_id(0) == 0)
        def _():
            s_scr[...] = jnp.zeros_like(s_scr)

        lb = _lb_from_logits(lbl_ref[...])

        def chunk(c, _):
            rows = pl.ds(pl.multiple_of(c * CHUNK, CHUNK), CHUNK)
            pre = []
            for cs in cols:
                _, _, lf, kk, _, q = _hgrn_gates(hq_ref[rows, cs], hf_ref[rows, cs], lb[:, cs])
                v = hi_ref[rows, cs]
                b = _cumsum_rows(lf)
                a = _hgrn_intra(q, kk, b)[0]
                b_last = b[CHUNK - 1:CHUNK, :]
                pre.append((q * jnp.exp(b), a, v, jnp.exp(b_last), _mm_tn(v, kk * jnp.exp(b_last - b))))
            for h, cs in enumerate(cols):
                qe, a, v, ebl, upd = pre[h]
                st = s_scr[h]
                st_ref[h, c] = st
                o_ref[rows, cs] = _mm_nt(qe, st) + _mm(a, v)
                s_scr[h] = st * ebl + upd
            return 0

        lax.fori_loop(0, ncg, chunk, 0)

    col = lambda k: pl.BlockSpec((tg, HGRN_WIDTH), lambda g: (g, k))
    return pl.pallas_call(
        body, name="hgrn_fwd", grid=(ng,),
        in_specs=[pl.BlockSpec((2, HGRN_WIDTH), lambda g: (0, 0)), col(0), col(1), col(2)],
        out_specs=[col(0), pl.BlockSpec((HGRN_HEADS, ncg, HGRN_DIM, HGRN_DIM), lambda g: (0, g, 0, 0))],
        out_shape=[jax.ShapeDtypeStruct((T, HGRN_WIDTH), F32),
                   jax.ShapeDtypeStruct((HGRN_HEADS, T // CHUNK, HGRN_DIM, HGRN_DIM), F32)],
        scratch_shapes=[pltpu.VMEM((HGRN_HEADS, HGRN_DIM, HGRN_DIM), F32)],
        compiler_params=_params(("arbitrary",)),
    )(lbl, xph, xph, xph)


def _hgrn_bwd(xph, lbl, states, d_o, tg=512):
    T = xph.shape[0]
    ng, ncg = T // tg, tg // CHUNK
    cols = [slice(HGRN_DIM * h, HGRN_DIM * (h + 1)) for h in range(HGRN_HEADS)]
    nsub = CHUNK // SUB

    def body(lbl_ref, hq_ref, hf_ref, hi_ref, st_ref, do_ref, dhq_ref, dhf_ref, dhi_ref, dlg_ref, ds_scr, dlb_scr):
        g = pl.program_id(0)

        @pl.when(g == 0)
        def _():
            ds_scr[...] = jnp.zeros_like(ds_scr)
            dlb_scr[...] = jnp.zeros_like(dlb_scr)

        lb = _lb_from_logits(lbl_ref[...])

        def chunk(ci, _):
            c = ncg - 1 - ci
            rows = pl.ds(pl.multiple_of(c * CHUNK, CHUNK), CHUNK)
            pre = []
            for h, cs in enumerate(cols):
                hq = hq_ref[rows, cs]
                sig_f, f, lf, kk, sig_q, q = _hgrn_gates(hq, hf_ref[rows, cs], lb[:, cs])
                v = hi_ref[rows, cs]
                do = do_ref[rows, cs]
                b = _cumsum_rows(lf)
                eb = jnp.exp(b)
                a, causal, qs, ks, eqs, eks = _hgrn_intra(q, kk, b, exact=True)
                b_last = b[CHUNK - 1:CHUNK, :]
                st = st_ref[h, c]
                pre.append(dict(hq=hq, sig_f=sig_f, f=f, kk=kk, sig_q=sig_q, q=q, v=v, eb=eb, qs=qs, ks=ks, eqs=eqs,
                                eks=eks, ebl=jnp.exp(b_last), el=jnp.exp(b_last - b), st=st,
                                da=jnp.where(causal, _mm_nt(do, v, True), 0.0), dq=_mm(do, st, True) * eb,
                                dv=_mm_tn(a, do, True), dsu=_mm_tn(do, q * eb, True)))
            for w in pre:
                dq_rows = []
                dk = jnp.zeros_like(w["q"])
                for i in range(nsub):
                    dai = w["da"][SUB * i:SUB * (i + 1), :]
                    dq_rows.append(_mm(dai, w["ks"][i], True) * w["eqs"][i])
                    dk = dk + _mm_tn(dai, w["qs"][i], True) * w["eks"][i]
                w["dq"] = w["dq"] + jnp.concatenate(dq_rows, axis=0)
                w["dk"] = dk
            for h, cs in enumerate(cols):
                w = pre[h]
                kk, el, ebl, dst = w["kk"], w["el"], w["ebl"], ds_scr[h]
                dk_state = _mm(w["v"], dst, True) * el
                dk = w["dk"] + dk_state
                e_last = (ebl * jnp.sum(w["st"] * dst, axis=0, keepdims=True)
                          + jnp.sum(kk * dk_state, axis=0, keepdims=True))
                dlf = _rev_cumsum_rows(w["q"] * w["dq"] - kk * dk) + e_last
                ds_scr[h] = dst * ebl + w["dsu"]
                df = dlf / w["f"] - dk
                sig_f, sig_q = w["sig_f"], w["sig_q"]
                dhf_ref[rows, cs] = df * (1.0 - lb[:, cs]) * sig_f * (1.0 - sig_f)
                dlb_scr[:, cs] += jnp.sum(df * (1.0 - sig_f), axis=0, keepdims=True)
                dhq_ref[rows, cs] = w["dq"] * sig_q * (1.0 + w["hq"] * (1.0 - sig_q))
                dhi_ref[rows, cs] = w["dv"] + _mm_nt(kk * el, dst, True)
            return 0

        lax.fori_loop(0, ncg, chunk, 0)

        @pl.when(g == ng - 1)
        def _():
            dl0 = dlb_scr[...] * lb * (1.0 - lb)
            dlg_ref[...] = jnp.concatenate([dl0, -dl0], axis=0)

    col = lambda k: pl.BlockSpec((tg, HGRN_WIDTH), lambda g: (ng - 1 - g, k))
    logits = pl.BlockSpec((2, HGRN_WIDTH), lambda g: (0, 0))
    big = jax.ShapeDtypeStruct((T, HGRN_WIDTH), F32)
    return pl.pallas_call(
        body, name="hgrn_bwd", grid=(ng,),
        in_specs=[logits, col(0), col(1), col(2),
                  pl.BlockSpec((HGRN_HEADS, ncg, HGRN_DIM, HGRN_DIM), lambda g: (0, ng - 1 - g, 0, 0)), col(0)],
        out_specs=[col(0), col(0), col(0), logits],
        out_shape=[big, big, big, jax.ShapeDtypeStruct((2, HGRN_WIDTH), F32)],
        scratch_shapes=[pltpu.VMEM((HGRN_HEADS, HGRN_DIM, HGRN_DIM), F32), pltpu.VMEM((1, HGRN_WIDTH), F32)],
        compiler_params=_params(("arbitrary",)),
    )(lbl, xph, xph, xph, states, d_o)


def _proj_fwd(x, o_raw, oh_raw, xph, wout, w_mla, w_hg, w_post, w_fpre, tt=512):
    T = x.shape[0]

    def body(x_ref, o_ref, oh_ref, hg_ref, wout_ref, wmla_ref, whg_ref, wpost_ref, wfpre_ref,
             h1_ref, y1_ref, z_ref, mix_ref):
        om, _, _ = _grms_fwd(o_ref[...], wmla_ref[...], MLA_V)
        hg = hg_ref[...]
        ohn, _, _ = _grms_fwd(oh_ref[...], whg_ref[...], HGRN_DIM)
        mix = jnp.concatenate([om, ohn * (hg * jax.nn.sigmoid(hg))], axis=-1)
        mix_ref[...] = mix.astype(mix_ref.dtype)
        y1 = _mm(mix, wout_ref[...])
        y1_ref[...] = y1
        h1 = x_ref[...] + _rms_fwd(y1, wpost_ref[...])[0]
        h1_ref[...] = h1
        z_ref[...] = _rms_fwd(h1, wfpre_ref[...])[0].astype(z_ref.dtype)

    row = lambda w: pl.BlockSpec((tt, w), lambda i: (i, 0))
    full = lambda a: pl.BlockSpec(a.shape, lambda i: (0,) * a.ndim)
    sds = jax.ShapeDtypeStruct
    return pl.pallas_call(
        body, name="proj_fwd", grid=(T // tt,),
        in_specs=[row(D_MODEL), row(MLA_WIDTH), row(HGRN_WIDTH), pl.BlockSpec((tt, HGRN_WIDTH), lambda i: (i, 3)),
                  full(wout), full(w_mla), full(w_hg), full(w_post), full(w_fpre)],
        out_specs=[row(D_MODEL)] * 4,
        out_shape=[sds((T, D_MODEL), F32), sds((T, D_MODEL), F32), sds((T, D_MODEL), MXU_DTYPE),
                   sds((T, D_MODEL), MXU_DTYPE)],
        compiler_params=_params(("arbitrary",)),
    )(x, o_raw, oh_raw, xph, wout, w_mla, w_hg, w_post, w_fpre)


def _ffn_fwd(zb, h1, tgt, w_fpost, wg, wu, wd, tt=256):
    T = zb.shape[0]
    nj = N_CHIPS

    def body(z_ref, h1_ref, tgt_ref, wfpost_ref, wg_ref, wu_ref, wd_ref, g_ref, up_ref, dy2_ref, dh2_ref, loss_ref, dwf_ref):
        @pl.when(pl.program_id(0) == 0)
        def _():
            loss_ref[...] = jnp.zeros_like(loss_ref)
            dwf_ref[...] = jnp.zeros_like(dwf_ref)

        z = z_ref[...]
        gs = [_mm_nt(z, wg_ref[j]) for j in range(nj)]
        ups = [_mm_nt(z, wu_ref[j]) for j in range(nj)]
        y2 = jnp.zeros((tt, D_MODEL), F32)
        for j in range(nj):
            g_ref[j] = gs[j]
            up_ref[j] = ups[j]
            y2 = y2 + _mm(gs[j] * jax.nn.sigmoid(gs[j]) * ups[j], wd_ref[j])
        w = wfpost_ref[...]
        y2s, y2n, r2 = _rms_fwd(y2, w)
        e = h1_ref[...] + y2s - tgt_ref[...]
        loss_ref[...] += jnp.sum(e * e, axis=0, keepdims=True)
        dh2 = e * (1.0 / D_MODEL)
        dh2_ref[...] = dh2
        dy2, dwf = _rms_bwd(dh2, y2n, r2, w)
        dy2_ref[...] = dy2.astype(dy2_ref.dtype)
        dwf_ref[...] += dwf

    row = pl.BlockSpec((tt, D_MODEL), lambda i: (i, 0))
    vec = pl.BlockSpec((1, D_MODEL), lambda i: (0, 0))
    resident = pl.BlockSpec((nj, FF_SHARD, D_MODEL), lambda i: (0, 0, 0), pipeline_mode=pl.Buffered(1))
    act = pl.BlockSpec((nj, tt, FF_SHARD), lambda i: (0, i, 0))
    sds = jax.ShapeDtypeStruct
    return pl.pallas_call(
        body, name="ffn_fwd", grid=(T // tt,),
        in_specs=[row, row, row, vec, resident, resident, resident],
        out_specs=[act, act, row, row, vec, vec],
        out_shape=[sds((nj, T, FF_SHARD), F32), sds((nj, T, FF_SHARD), F32), sds((T, D_MODEL), MXU_DTYPE),
                   sds((T, D_MODEL), F32), sds((1, D_MODEL), F32), sds((1, D_MODEL), F32)],
        compiler_params=_params(("arbitrary",)),
    )(zb, h1, tgt, w_fpost, wg, wu, wd)


def _ffn_bwd(zb, g, up, dy2b, wg, wu, wd, tt=512):
    T = zb.shape[0]
    nj = N_CHIPS

    def body(z_ref, g_ref, up_ref, dy2_ref, wg_ref, wu_ref, wd_ref, dwg_ref, dwu_ref, dwd_ref, dz_ref):
        @pl.when(pl.program_id(1) == 0)
        def _():
            dwg_ref[...] = jnp.zeros_like(dwg_ref)
            dwu_ref[...] = jnp.zeros_like(dwu_ref)
            dwd_ref[...] = jnp.zeros_like(dwd_ref)

        z, g_, up_, dy2 = z_ref[...], g_ref[0], up_ref[0], dy2_ref[...]
        sg = jax.nn.sigmoid(g_)
        act = g_ * sg
        dff = _mm_nt(dy2, wd_ref[0])
        dwd_ref[0] += _mm_tn(act * up_, dy2)
        dg = dff * up_ * sg * (1.0 + g_ * (1.0 - sg))
        dup = dff * act
        dwg_ref[0] += _mm_tn(dg, z)
        dwu_ref[0] += _mm_tn(dup, z)
        dz_ref[0] = _mm(dg, wg_ref[0]) + _mm(dup, wu_ref[0])

    row = pl.BlockSpec((tt, D_MODEL), lambda j, i: (i, 0))
    act = pl.BlockSpec((1, tt, FF_SHARD), lambda j, i: (j, i, 0))
    w_sh = pl.BlockSpec((1, FF_SHARD, D_MODEL), lambda j, i: (j, 0, 0))
    w_grad = jax.ShapeDtypeStruct((nj, FF_SHARD, D_MODEL), F32)
    return pl.pallas_call(
        body, name="ffn_bwd", grid=(nj, T // tt),
        in_specs=[row, act, act, row, w_sh, w_sh, w_sh],
        out_specs=[w_sh, w_sh, w_sh, pl.BlockSpec((1, tt, D_MODEL), lambda j, i: (j, i, 0))],
        out_shape=[w_grad, w_grad, w_grad, jax.ShapeDtypeStruct((nj, T, D_MODEL), F32)],
        compiler_params=_params(("arbitrary", "arbitrary")),
    )(zb, g, up, dy2b, wg, wu, wd)


def _mid_bwd(dzp, dh2, h1, y1, mixb, o_raw, oh_raw, xph, wout, w_fpre, w_post, w_mla, w_hg, swap=(), tt=256):
    T = dh2.shape[0]
    nsw = len(swap)
    n_in, n_out = 13, 10

    def body(*refs):
        (dzp_ref, dh2_ref, h1_ref, y1_ref, mix_ref, o_ref, oh_ref, hg_ref, wout_ref, wfpre_ref, wpost_ref,
         wmla_ref, whg_ref) = refs[:n_in]
        (dh1_ref, dwout_ref, do_ref, doh_ref, dhg_ref, dvec_ref, dwfpre_ref, dwpost_ref, dwmla_ref,
         dwhg_ref) = refs[n_in + nsw:n_in + nsw + n_out]
        swap_copies = lambda: _pair_swap_copies(refs[n_in:n_in + nsw], refs[n_in + nsw + n_out:n_in + 2 * nsw + n_out],
                                                *refs[n_in + 2 * nsw + n_out:])

        @pl.when(pl.program_id(0) == 0)
        def _():
            for r in (dwout_ref, dwfpre_ref, dwpost_ref, dwmla_ref, dwhg_ref):
                r[...] = jnp.zeros_like(r)
            for cp in (swap_copies() if nsw else ()):
                cp.start()

        dz = dzp_ref[0] + dzp_ref[1] + dzp_ref[2] + dzp_ref[3]
        wfpre = wfpre_ref[...]
        _, h1n, r = _rms_fwd(h1_ref[...], wfpre)
        dh1_z, dwfpre = _rms_bwd(dz, h1n, r, wfpre)
        dwfpre_ref[...] += dwfpre
        dh1 = dh2_ref[...] + dh1_z
        dh1_ref[...] = dh1
        wpost = wpost_ref[...]
        _, y1n, r1 = _rms_fwd(y1_ref[...], wpost)
        dy1, dwpost = _rms_bwd(dh1, y1n, r1, wpost)
        dwpost_ref[...] += dwpost
        dmix = _mm_nt(dy1, wout_ref[...])
        dwout_ref[...] += _mm_tn(mix_ref[...], dy1)
        wmla = wmla_ref[...]
        o = o_ref[...]
        _, on, ro = _grms_fwd(o, wmla, MLA_V)
        d_o, dwmla = _grms_bwd(dmix[:, :MLA_WIDTH], on, ro, wmla, MLA_V)
        dwmla_ref[...] += dwmla
        do_ref[...] = d_o.astype(do_ref.dtype)
        hh = lax.broadcasted_iota(jnp.int32, (MLA_HEADS, MLA_WIDTH), 0)
        ll = lax.broadcasted_iota(jnp.int32, (MLA_HEADS, MLA_WIDTH), 1)
        sel = jnp.where((ll >= hh * MLA_V) & (ll < (hh + 1) * MLA_V), 1.0, 0.0)
        dvec_ref[...] = _mm_nt(sel, d_o * o, True)
        whg = whg_ref[...]
        hg = hg_ref[...]
        sg = jax.nn.sigmoid(hg)
        _, ohn, rh = _grms_fwd(oh_ref[...], whg, HGRN_DIM)
        dmh = dmix[:, MLA_WIDTH:]
        dhg_ref[...] = dmh * ohn * whg * sg * (1.0 + hg * (1.0 - sg))
        d_oh, dwhg = _grms_bwd(dmh * (hg * sg), ohn, rh, whg, HGRN_DIM)
        dwhg_ref[...] += dwhg
        doh_ref[...] = d_oh

        if nsw:
            @pl.when(pl.program_id(0) == T // tt - 1)
            def _():
                for cp in swap_copies():
                    cp.wait()

    row = lambda w: pl.BlockSpec((tt, w), lambda i: (i, 0))
    full = lambda a: pl.BlockSpec(a.shape, lambda i: (0,) * a.ndim)
    vec = lambda w: pl.BlockSpec((1, w), lambda i: (0, 0))
    sds = jax.ShapeDtypeStruct
    return pl.pallas_call(
        body, name="mid_bwd", grid=(T // tt,),
        in_specs=[pl.BlockSpec((N_CHIPS, tt, D_MODEL), lambda i: (0, i, 0)), row(D_MODEL), row(D_MODEL), row(D_MODEL),
                  row(D_MODEL), row(MLA_WIDTH), row(HGRN_WIDTH), pl.BlockSpec((tt, HGRN_WIDTH), lambda i: (i, 3)),
                  full(wout), vec(D_MODEL), vec(D_MODEL), vec(MLA_WIDTH), vec(HGRN_WIDTH)] + [ANY] * nsw,
        out_specs=[row(D_MODEL), full(wout), row(MLA_WIDTH), row(HGRN_WIDTH), row(HGRN_WIDTH),
                   pl.BlockSpec((MLA_HEADS, tt), lambda i: (0, i)),
                   vec(D_MODEL), vec(D_MODEL), vec(MLA_WIDTH), vec(HGRN_WIDTH)] + [ANY] * nsw,
        out_shape=[sds((T, D_MODEL), F32), sds(wout.shape, F32), sds((T, MLA_WIDTH), MXU_DTYPE), sds((T, HGRN_WIDTH), F32),
                   sds((T, HGRN_WIDTH), F32), sds((MLA_HEADS, T), F32),
                   sds((1, D_MODEL), F32), sds((1, D_MODEL), F32), sds((1, MLA_WIDTH), F32), sds((1, HGRN_WIDTH), F32)]
        + _half_stack_shapes(swap),
        scratch_shapes=[pltpu.SemaphoreType.DMA((nsw,)), pltpu.SemaphoreType.DMA((nsw,))] if nsw else [],
        compiler_params=_params(("arbitrary",)),
    )(dzp, dh2, h1, y1, mixb, o_raw, oh_raw, xph, wout, w_fpre, w_post, w_mla, w_hg, *swap)


def _in_bwd(x, dh1, cq, ckv, dq, dk, dv, dhq, dhf, dhi, dhg, rc, rs, w_pre, win, qnw, wq, kvnw, wk, wv, tt=256):
    T = x.shape[0]

    def body(x_ref, dh1_ref, cq_ref, ckv_ref, dq_ref, dk_ref, dv_ref, dhq_ref, dhf_ref, dhi_ref, dhg_ref, rc_ref, rs_ref,
             wpre_ref, win_ref, qnw_ref, wq_ref, kvnw_ref, wk_ref, wv_ref,
             dx_ref, dwin_ref, dwq_ref, dwk_ref, dwv_ref, dwpre_ref, dqnw_ref, dkvnw_ref):
        @pl.when(pl.program_id(0) == 0)
        def _():
            for r in (dwin_ref, dwq_ref, dwk_ref, dwv_ref, dwpre_ref, dqnw_ref, dkvnw_ref):
                r[...] = jnp.zeros_like(r)

        lo = Q_RANK + KV_RANK + HEAD_PAD
        wpre = wpre_ref[...]
        u, xn, rx = _rms_fwd(x_ref[...], wpre)
        dxp_h = jnp.concatenate([dhq_ref[...], dhf_ref[...], dhi_ref[...], dhg_ref[...]], axis=-1)
        dwin_ref[lo:, :] += _mm_tn(dxp_h, u)
        du = _mm(dxp_h, win_ref[lo:, :])
        c, sa, sb = _rope_tables(rc_ref[...], rs_ref[...])
        lane = lax.broadcasted_iota(jnp.int32, (tt, HEAD_PAD), 1)
        dk_all = dk_ref[...]
        dq_lin = []
        dkr = jnp.zeros((tt, HEAD_PAD), F32)
        for h in range(MLA_HEADS):
            sl = slice(HEAD_PAD * h, HEAD_PAD * (h + 1))
            dq_lin.append(_rope_bwd(dq_ref[sl, :].T, c, sa, sb))
            dkr = dkr + dk_all[:, sl]
        dq_lin = jnp.concatenate(dq_lin, axis=-1)
        dkr = jnp.where((lane >= MLA_NOPE) & (lane < MLA_QK), _rope_bwd(dkr, c, sa, sb), 0.0)
        qnw = qnw_ref[...]
        qn, cqn, rq = _rms_fwd(cq_ref[...], qnw)
        dwq_ref[...] += _mm_tn(qn, dq_lin)
        dcq, dqnw = _rms_bwd(_mm_nt(dq_lin, wq_ref[...]), cqn, rq, qnw)
        dqnw_ref[...] += dqnw
        kvnw = kvnw_ref[...]
        kvn, ckvn, rkv = _rms_fwd(ckv_ref[...], kvnw)
        dv_ = dv_ref[...]
        dwk_ref[...] += _mm_tn(kvn, dk_all)
        dwv_ref[...] += _mm_tn(kvn, dv_)
        dckv, dkvnw = _rms_bwd(_mm_nt(dk_all, wk_ref[...]) + _mm_nt(dv_, wv_ref[...]), ckvn, rkv, kvnw)
        dkvnw_ref[...] += dkvnw
        dxp_a = jnp.concatenate([dcq, dckv, dkr], axis=-1)
        dwin_ref[:lo, :] += _mm_tn(dxp_a, u)
        dx_u, dwpre = _rms_bwd(du + _mm(dxp_a, win_ref[:lo, :]), xn, rx, wpre)
        dwpre_ref[...] += dwpre
        dx_ref[...] = dh1_ref[...] + dx_u

    row = lambda w: pl.BlockSpec((tt, w), lambda i: (i, 0))
    full = lambda a: pl.BlockSpec(a.shape, lambda i: (0,) * a.ndim)
    sds = jax.ShapeDtypeStruct
    qk_w = MLA_HEADS * HEAD_PAD
    return pl.pallas_call(
        body, name="in_bwd", grid=(T // tt,),
        in_specs=[row(D_MODEL), row(D_MODEL), row(Q_RANK), row(KV_RANK), pl.BlockSpec((qk_w, tt), lambda i: (0, i)),
                  row(qk_w), row(MLA_WIDTH),
                  row(HGRN_WIDTH), row(HGRN_WIDTH), row(HGRN_WIDTH), row(HGRN_WIDTH), row(HEAD_PAD), row(HEAD_PAD),
                  full(w_pre), full(win), full(qnw), full(wq), full(kvnw), full(wk), full(wv)],
        out_specs=[row(D_MODEL), full(win), full(wq), full(wk), full(wv), full(w_pre), full(qnw), full(kvnw)],
        out_shape=[sds((T, D_MODEL), F32), sds(win.shape, F32), sds(wq.shape, F32), sds(wk.shape, F32),
                   sds(wv.shape, F32), sds(w_pre.shape, F32), sds(qnw.shape, F32), sds(kvnw.shape, F32)],
        compiler_params=_params(("arbitrary",)),
    )(x, dh1, cq, ckv, dq, dk, dv, dhq, dhf, dhi, dhg, rc, rs, w_pre, win, qnw, wq, kvnw, wk, wv)


def _arrange_weights(win_t, wuq_full, wukv):
    dt = win_t.dtype
    z = lambda n: jnp.zeros((n, D_MODEL), dt)
    s2 = Q_RANK + KV_RANK
    win_arr = jnp.concatenate([win_t[:s2], z(MLA_NOPE), win_t[s2:s2 + MLA_ROPE], z(HEAD_PAD - MLA_QK),
                               win_t[s2 + MLA_ROPE:]], axis=0)
    wq_arr = jnp.pad(wuq_full, ((0, 0), (0, 0), (0, HEAD_PAD - MLA_QK))).reshape(Q_RANK, MLA_HEADS * HEAD_PAD)
    wk_arr = jnp.pad(wukv[:, :, :MLA_NOPE], ((0, 0), (0, 0), (0, HEAD_PAD - MLA_NOPE))).reshape(
        KV_RANK, MLA_HEADS * HEAD_PAD)
    wv_arr = wukv[:, :, MLA_NOPE:].reshape(KV_RANK, MLA_WIDTH)
    return win_arr, wq_arr, wk_arr, wv_arr


def _unarrange_grads(dwin_arr, dwq_arr, dwk_arr, dwv_arr):
    s2 = Q_RANK + KV_RANK
    dwin = jnp.concatenate([dwin_arr[:s2], dwin_arr[s2 + MLA_NOPE:s2 + MLA_QK], dwin_arr[s2 + HEAD_PAD:]], axis=0)
    dwuq = dwq_arr.reshape(Q_RANK, MLA_HEADS, HEAD_PAD)[:, :, :MLA_QK]
    dwukv = jnp.concatenate([dwk_arr.reshape(KV_RANK, MLA_HEADS, HEAD_PAD)[:, :, :MLA_NOPE],
                             dwv_arr.reshape(KV_RANK, MLA_HEADS, MLA_V)], axis=-1)
    return dwin, dwuq, dwukv


def _rope_inv_freq():
    inv = 1.0 / (ROPE_THETA ** (jnp.arange(0, MLA_ROPE, 2, dtype=F32) / MLA_ROPE))
    z = lambda n: jnp.zeros((n,), F32)
    return jnp.concatenate([z(MLA_NOPE), inv, inv, z(HEAD_PAD - MLA_QK)]).reshape(1, HEAD_PAD)


def _local_step(x, pos, tgt, small, win_arr, wq_arr, wk_arr, wv_arr, late, place=None):
    invf = _rope_inv_freq()
    cq, ckv, xph, qb, kb, vb, kt, vt, rc, rs = _in_fwd(x, pos, invf, small["attn_pre_norm"], win_arr, small["mla_q_norm"],
                                               wq_arr, small["mla_kv_norm"], wk_arr, wv_arr)
    if place is None:
        o_raw, lse = _attn_fwd_t(qb, kb, vt)
        wout, wg, wu, wd = late
    else:
        o_raw, lse, *stacks = _attn_fwd_t(qb, kb, vt, gather=late)
        wout, wg, wu, wd = [lax.dynamic_update_slice(s, l[None], (place[1], 0, 0)) for s, l in zip(stacks, late)]
        wout = wout.reshape(D_MODEL, D_MODEL)
    oh_raw, states = _hgrn_fwd(xph, small["hgrn_lb_logits"])
    h1, y1, zb, mixb = _proj_fwd(x, o_raw, oh_raw, xph, wout, small["mla_out_norm"], small["hgrn_out_norm"],
                                 small["attn_post_norm"], small["ffn_pre_norm"])
    g, up, dy2b, dh2, loss_acc, d_fpost = _ffn_fwd(zb, h1, tgt, small["ffn_post_norm"], wg, wu, wd)
    dwg, dwu, dwd, dzp = _ffn_bwd(zb, g, up, dy2b, wg, wu, wd)
    ffn_grads = [] if place is None else [dwg, dwu, dwd]
    dh1, dwout, d_o, d_oh, dhg, dvec, d_fpre, d_post, d_mla, d_hg, *ffn_rs = _mid_bwd(
        dzp, dh2, h1, y1, mixb, o_raw, oh_raw, xph, wout, small["ffn_pre_norm"], small["attn_post_norm"],
        small["mla_out_norm"], small["hgrn_out_norm"], swap=ffn_grads)
    ffn_ps = _pair_sum(place, ffn_grads, ffn_rs, name="pair_sum_ffn") if ffn_grads else []
    dq, dk, dv, *ffn_ris = _attn_bwd_t(qb, kb, kt, vb, d_o, lse, dvec.reshape(lse.shape), send=ffn_ps)
    dhq, dhf, dhi, d_lbl = _hgrn_bwd(xph, small["hgrn_lb_logits"], states, d_oh)
    dx, dwin_arr, dwq_arr, dwk_arr, dwv_arr, d_pre, d_qn, d_kvn = _in_bwd(
        x, dh1, cq, ckv, dq, dk, dv, dhq, dhf, dhi, dhg, rc, rs, small["attn_pre_norm"], win_arr,
        small["mla_q_norm"], wq_arr, small["mla_kv_norm"], wk_arr, wv_arr)
    dwin, dwuq, dwukv = _unarrange_grads(dwin_arr, dwq_arr, dwk_arr, dwv_arr)
    loss = 0.5 * jnp.sum(loss_acc) * (1.0 / D_MODEL)
    grads = dict(attn_pre_norm=d_pre, w_in=dwin, mla_q_norm=d_qn, mla_w_uq=dwuq, mla_kv_norm=d_kvn, mla_w_ukv=dwukv,
                 mla_out_norm=d_mla, hgrn_lb_logits=d_lbl, hgrn_out_norm=d_hg, w_out=dwout, attn_post_norm=d_post,
                 ffn_pre_norm=d_fpre, w_gate=dwg, w_up=dwu, w_down=dwd, ffn_post_norm=d_fpost)
    if place is None:
        return loss, dx, grads
    return loss, dx, grads, (ffn_rs, ffn_ris)


def _place():
    x, y, c = lax.axis_index("x"), lax.axis_index("y"), lax.axis_index("c")
    others = [(1 - x, y), (x, 1 - y), (1 - x, 1 - y)]
    return x, y, c, 2 * x + y, (x, y, 1 - c), others


def _half(ref, c, rows):
    return ref.at[pl.ds(pl.multiple_of(c * rows, 8), rows)]


def _rcopy(src, dst, send, recv, k, to):
    return pltpu.make_async_remote_copy(src_ref=src, dst_ref=dst, send_sem=send.at[k], recv_sem=recv.at[k],
                                        device_id=to, device_id_type=MESH)


class _Gather:
    def __init__(self, ins, outs, send, recv):
        self.ins, self.outs, self.send, self.recv = ins, outs, send, recv
        self.n = len(ins)
        self.halves = [r.shape[0] // 2 for r in ins]
        _, _, self.c, self.me, self.sib, self.others = _place()

    def _each(self):
        for j, (px, py) in enumerate(self.others):
            for a in range(self.n):
                yield j * self.n + a, a, 2 * px + py, (px, py, self.c)

    def sends(self):
        return [_rcopy(_half(self.ins[a], self.c, self.halves[a]), _half(self.outs[a].at[self.me], self.c, self.halves[a]),
                       self.send, self.recv, k, to) for k, a, _, to in self._each()]

    def arrivals(self):
        parts = [(k, _half(self.outs[a].at[chip], self.c, self.halves[a]), to) for k, a, chip, to in self._each()]
        return [_rcopy(p, p, self.send, self.recv, k, to) for k, p, to in parts]

    def forwards(self):
        parts = [(k, _half(self.outs[a].at[chip], self.c, self.halves[a])) for k, a, chip, _ in self._each()]
        return [_rcopy(p, p, self.send, self.recv, 3 * self.n + k, self.sib) for k, p in parts]

    def forward_arrivals(self):
        parts = [(k, _half(self.outs[a].at[chip], 1 - self.c, self.halves[a])) for k, a, chip, _ in self._each()]
        return [_rcopy(p, p, self.send, self.recv, 3 * self.n + k, self.sib) for k, p in parts]

    @staticmethod
    def out_shapes(arrs):
        return [jax.ShapeDtypeStruct((N_CHIPS,) + a.shape, a.dtype) for a in arrs]

    @staticmethod
    def semaphores(arrs):
        return [pltpu.SemaphoreType.DMA((6 * len(arrs),)), pltpu.SemaphoreType.DMA((6 * len(arrs),))]


def _gather_chips(arrs, name):
    n = len(arrs)

    def body(*refs):
        gat = _Gather(refs[:n], refs[n:2 * n], *refs[2 * n:])
        sends, forwards = gat.sends(), gat.forwards()
        for cp in sends:
            cp.start()
        for arrival, fw in zip(gat.arrivals(), forwards):
            arrival.wait_recv()
            fw.start()
        for arrival in gat.forward_arrivals():
            arrival.wait_recv()
        for cp in sends + forwards:
            cp.wait_send()

    return pl.pallas_call(body, name=name, in_specs=[ANY] * n, out_specs=[ANY] * n, out_shape=_Gather.out_shapes(arrs),
                          scratch_shapes=_Gather.semaphores(arrs))(*arrs)


GRAD_BLOCKS = 2


def _pair_swap_copies(g_refs, r_refs, send, recv):
    _, _, c, _, sib, _ = _place()
    copies = []
    for a, (g, r) in enumerate(zip(g_refs, r_refs)):
        h = g.shape[1] // 2
        copies.append(_rcopy(g.at[:, pl.ds(pl.multiple_of((1 - c) * h, 8), h)], r, send, recv, a, sib))
    return copies


def _half_stack_shapes(gs, dtype=None):
    return [jax.ShapeDtypeStruct((N_CHIPS, g.shape[1] // 2, g.shape[2]), dtype or g.dtype) for g in gs]


def _pair_swap(gs, sm):
    n = len(gs)

    def body(*refs):
        g_refs, sm_ref = refs[:n], refs[n]
        r_refs, ssib_ref = refs[n + 1:2 * n + 1], refs[2 * n + 1]
        send, recv = refs[2 * n + 2:]
        copies = _pair_swap_copies(g_refs, r_refs, send, recv)
        copies.append(_rcopy(sm_ref, ssib_ref, send, recv, n, _place()[4]))
        for cp in copies:
            cp.start()
        for cp in copies:
            cp.wait()

    return pl.pallas_call(
        body, name="pair_swap", in_specs=[ANY] * (n + 1), out_specs=[ANY] * (n + 1),
        out_shape=_half_stack_shapes(gs) + [jax.ShapeDtypeStruct(sm.shape, sm.dtype)],
        scratch_shapes=[pltpu.SemaphoreType.DMA((n + 1,)), pltpu.SemaphoreType.DMA((n + 1,))],
    )(*gs, sm)


def _pair_sum(place, gs, rs, small=None, name="pair_sum"):
    n = len(gs)
    nb = GRAD_BLOCKS

    def body(place_ref, *refs):
        g_refs, r_refs, p_refs = refs[:n], refs[n:2 * n], refs[-n - 1:-1] if small else refs[-n:]
        for a in range(n):
            p_refs[a][0] = (g_refs[a][0] + r_refs[a][0]).astype(p_refs[a].dtype)
        if small:
            @pl.when((pl.program_id(0) == 0) & (pl.program_id(1) == 0))
            def _():
                refs[-1][...] = refs[2 * n][...] + refs[2 * n + 1][...]

    in_specs, out_specs = [], []
    for g in gs:
        blk = (1, g.shape[1] // 2 // nb, g.shape[2])
        in_specs.append(pl.BlockSpec(blk, lambda i, k, p: (k, p[0] * nb + i, 0)))
    for g in gs:
        blk = (1, g.shape[1] // 2 // nb, g.shape[2])
        in_specs.append(pl.BlockSpec(blk, lambda i, k, p: (k, i, 0)))
        out_specs.append(pl.BlockSpec(blk, lambda i, k, p: (k, i, 0)))
    out_shape = _half_stack_shapes(gs, BF16)
    if small:
        sm_spec = pl.BlockSpec(small[0].shape, lambda i, k, p: (0, 0))
        in_specs += [sm_spec, sm_spec]
        out_specs.append(sm_spec)
        out_shape.append(jax.ShapeDtypeStruct(small[0].shape, F32))
    return pl.pallas_call(
        body, name=name,
        grid_spec=pltpu.PrefetchScalarGridSpec(num_scalar_prefetch=1, grid=(nb, N_CHIPS), in_specs=in_specs,
                                               out_specs=out_specs),
        out_shape=out_shape,
        compiler_params=_params(("arbitrary", "arbitrary")),
    )(place, *gs, *rs, *(small or ()))


def _chip_swap_copies(p_refs, ri_refs, send, recv):
    _, _, c, _, _, others = _place()
    n = len(p_refs)
    return [_rcopy(p_refs[a].at[2 * px + py], ri_refs[a].at[j], send, recv, j * n + a, (px, py, c))
            for j, (px, py) in enumerate(others) for a in range(n)]


def _chip_swap_shapes(ps):
    return [jax.ShapeDtypeStruct((3,) + p.shape[1:], p.dtype) for p in ps]


def _chip_swap(ps, pair):
    n = len(ps)
    hs = SMALL_ROWS // 2

    def body(*refs):
        p_refs, pair_ref = refs[:n], refs[n]
        ri_refs, sm4_ref = refs[n + 1:2 * n + 1], refs[2 * n + 1]
        send, recv, lsem = refs[2 * n + 2:]
        x, y, c, me, sib, others = _place()
        local = pltpu.make_async_copy(pair_ref, sm4_ref.at[me], lsem.at[0])
        local.start()
        copies = _chip_swap_copies(p_refs, ri_refs, send, recv)
        arrivals = list(copies)
        for j, (px, py) in enumerate(others):
            copies.append(_rcopy(_half(pair_ref, c, hs), _half(sm4_ref.at[me], c, hs), send, recv, 3 * n + j, (px, py, c)))
            part = _half(sm4_ref.at[2 * px + py], c, hs)
            arrivals.append(_rcopy(part, part, send, recv, 3 * n + j, (px, py, c)))
        for cp in copies:
            cp.start()
        for arrival in arrivals:
            arrival.wait_recv()
        for cp in copies:
            cp.wait_send()
        local.wait()

    k = 3 * (n + 1)
    return pl.pallas_call(
        body, name="chip_swap", in_specs=[ANY] * (n + 1), out_specs=[ANY] * (n + 1),
        out_shape=_chip_swap_shapes(ps) + [jax.ShapeDtypeStruct((N_CHIPS,) + pair.shape, pair.dtype)],
        scratch_shapes=[pltpu.SemaphoreType.DMA((k,)), pltpu.SemaphoreType.DMA((k,)), pltpu.SemaphoreType.DMA((1,))],
    )(*ps, pair)


def _chip_sum(place, gs, rs, ris):
    n = len(gs)
    nb = GRAD_BLOCKS

    def body(place_ref, *refs):
        g_refs, r_refs, ri_refs, o_refs = refs[:n], refs[n:2 * n], refs[2 * n:3 * n], refs[3 * n:]
        for a in range(n):
            ri = ri_refs[a]
            o_refs[a][...] = (g_refs[a][0] + r_refs[a][0]) + ri[0].astype(F32) + ri[1].astype(F32) + ri[2].astype(F32)

    in_specs, out_specs, out_shape = [], [], []
    for g in gs:
        blk = (1, g.shape[1] // 2 // nb, g.shape[2])
        in_specs.append(pl.BlockSpec(blk, lambda i, p: (p[1], p[0] * nb + i, 0)))
    for g in gs:
        blk = (1, g.shape[1] // 2 // nb, g.shape[2])
        in_specs.append(pl.BlockSpec(blk, lambda i, p: (p[1], i, 0)))
    for g in gs:
        rb = g.shape[1] // 2 // nb
        in_specs.append(pl.BlockSpec((3, rb, g.shape[2]), lambda i, p: (0, i, 0)))
        out_specs.append(pl.BlockSpec((rb, g.shape[2]), lambda i, p: (p[0] * nb + i, 0)))
        out_shape.append(jax.ShapeDtypeStruct(g.shape[1:], F32))
    return pl.pallas_call(
        body, name="chip_sum",
        grid_spec=pltpu.PrefetchScalarGridSpec(num_scalar_prefetch=1, grid=(nb,), in_specs=in_specs, out_specs=out_specs),
        out_shape=out_shape,
        compiler_params=_params(("arbitrary",)),
    )(place, *gs, *rs, *ris)


def _pair_fill(gfs, sm4):
    n = len(gfs)
    hs = SMALL_ROWS // 2

    def body(*refs):
        g_refs, sm4_ref = refs[n + 1:2 * n + 1], refs[2 * n + 1]
        send, recv = refs[2 * n + 2:]
        x, y, c, me, sib, others = _place()
        copies, waits = [], []
        for a in range(n):
            h = gfs[a].shape[0] // 2
            mine, theirs = _half(g_refs[a], c, h), _half(g_refs[a], 1 - c, h)
            copies.append(pltpu.make_async_remote_copy(src_ref=mine, dst_ref=mine, send_sem=send.at[a],
                                                       recv_sem=recv.at[a], device_id=sib, device_id_type=MESH))
            waits.append(pltpu.make_async_remote_copy(src_ref=theirs, dst_ref=theirs, send_sem=send.at[a],
                                                      recv_sem=recv.at[a], device_id=sib, device_id_type=MESH))
        for j, (px, py) in enumerate(others):
            chip = 2 * px + py
            mine, theirs = _half(sm4_ref.at[chip], c, hs), _half(sm4_ref.at[chip], 1 - c, hs)
            copies.append(pltpu.make_async_remote_copy(src_ref=mine, dst_ref=mine, send_sem=send.at[n + j],
                                                       recv_sem=recv.at[n + j], device_id=sib, device_id_type=MESH))
            waits.append(pltpu.make_async_remote_copy(src_ref=theirs, dst_ref=theirs, send_sem=send.at[n + j],
                                                      recv_sem=recv.at[n + j], device_id=sib, device_id_type=MESH))
        for cp in copies:
            cp.start()
        for w in waits:
            w.wait_recv()
        for cp in copies:
            cp.wait_send()

    return pl.pallas_call(
        body, name="pair_fill", in_specs=[ANY] * (n + 1), out_specs=[ANY] * (n + 1),
        out_shape=[jax.ShapeDtypeStruct(g.shape, g.dtype) for g in gfs] + [jax.ShapeDtypeStruct(sm4.shape, sm4.dtype)],
        input_output_aliases={i: i for i in range(n + 1)},
        scratch_shapes=[pltpu.SemaphoreType.DMA((n + 3,)), pltpu.SemaphoreType.DMA((n + 3,))],
    )(*gfs, sm4)


def _adamw_math(w, g, m, v):
    m = ADAM_B1 * m + (1.0 - ADAM_B1) * g
    v = ADAM_B2 * v + (1.0 - ADAM_B2) * (g * g)
    m_hat = m / (1.0 - ADAM_B1 ** ADAM_STEP)
    v_hat = v / (1.0 - ADAM_B2 ** ADAM_STEP)
    return -ADAM_LR * (m_hat / (jnp.sqrt(v_hat) + ADAM_EPS) + ADAM_WD * w), m, v


def _adamw(w, g, m, v, rb, name):
    rows, cols = w.shape

    def body(w_ref, g_ref, m_ref, v_ref, d_ref, mo_ref, vo_ref):
        d, mo, vo = _adamw_math(w_ref[...], g_ref[...], m_ref[...], v_ref[...])
        d_ref[...] = d
        mo_ref[...] = mo
        vo_ref[...] = vo

    spec = pl.BlockSpec((rb, cols), lambda i: (i, 0))
    return pl.pallas_call(
        body, name=name, grid=(rows // rb,), in_specs=[spec] * 4, out_specs=[spec] * 3,
        out_shape=[jax.ShapeDtypeStruct(w.shape, F32)] * 3,
        compiler_params=_params(("arbitrary",)),
    )(w, g, m, v)


def _adamw_small(sm4, w, m, v):
    def body(sm4_ref, w_ref, m_ref, v_ref, g_ref, d_ref, mo_ref, vo_ref):
        g = ((sm4_ref[0] + sm4_ref[1]) + sm4_ref[2]) + sm4_ref[3]
        g_ref[...] = g
        d, mo, vo = _adamw_math(w_ref[...], g, m_ref[...], v_ref[...])
        d_ref[...] = d
        mo_ref[...] = mo
        vo_ref[...] = vo

    return pl.pallas_call(
        body, name="adamw_small", out_shape=[jax.ShapeDtypeStruct(w.shape, F32)] * 4,
        compiler_params=pltpu.CompilerParams(vmem_limit_bytes=VMEM_LIMIT),
    )(sm4, w, m, v)


SMALL_NAMES = ("attn_pre_norm", "mla_q_norm", "mla_kv_norm", "mla_w_ukv", "mla_out_norm", "hgrn_lb_logits",
               "hgrn_out_norm", "attn_post_norm", "ffn_pre_norm", "ffn_post_norm")
BIG_NAMES = ("w_in", "mla_w_uq", "w_out", "w_gate", "w_up", "w_down")
WEIGHT_NAMES = ("attn_pre_norm", "w_in", "mla_q_norm", "mla_w_uq", "mla_kv_norm", "mla_w_ukv", "mla_out_norm",
                "hgrn_lb_logits", "hgrn_out_norm", "w_out", "attn_post_norm", "ffn_pre_norm", "w_gate", "w_up", "w_down",
                "ffn_post_norm")


UQ_COMM_SHAPE = (192, 384)


def _pack_small(vals, extra=None):
    parts = [vals[n].reshape(-1) for n in SMALL_NAMES]
    if extra is not None:
        parts.append(extra.reshape(1))
    flat = jnp.concatenate(parts)
    return jnp.pad(flat, (0, SMALL_ROWS * D_MODEL - flat.shape[0])).reshape(SMALL_ROWS, D_MODEL)


def _unpack_small(buf, shapes):
    flat = buf.reshape(-1)
    out, off = {}, 0
    for n, size in zip(SMALL_NAMES, SMALL_SIZES):
        out[n] = flat[off:off + size].reshape(shapes[n])
        off += size
    return out


def kernel(x, positions, attn_pre_norm, w_in, mla_q_norm, mla_w_uq, mla_kv_norm, mla_w_ukv, mla_out_norm, hgrn_lb_logits, hgrn_out_norm, w_out, attn_post_norm, ffn_pre_norm, w_gate, w_up, w_down, ffn_post_norm, loss_target, m_attn_pre_norm, m_w_in, m_mla_q_norm, m_mla_w_uq, m_mla_kv_norm, m_mla_w_ukv, m_mla_out_norm, m_hgrn_lb_logits, m_hgrn_out_norm, m_w_out, m_attn_post_norm, m_ffn_pre_norm, m_w_gate, m_w_up, m_w_down, m_ffn_post_norm, v_attn_pre_norm, v_w_in, v_mla_q_norm, v_mla_w_uq, v_mla_kv_norm, v_mla_w_ukv, v_mla_out_norm, v_hgrn_lb_logits, v_hgrn_out_norm, v_w_out, v_attn_post_norm, v_ffn_pre_norm, v_w_gate, v_w_up, v_w_down, v_ffn_post_norm):
    args = locals()
    W = {n: args[n] for n in WEIGHT_NAMES}
    M = {n: args["m_" + n] for n in WEIGHT_NAMES}
    V = {n: args["v_" + n] for n in WEIGHT_NAMES}
    T = x.shape[1]
    cx, cy, cc = lax.axis_index("x"), lax.axis_index("y"), lax.axis_index("c")

    win_rows = D_IN // N_CHIPS
    shard2d = {"w_in": (win_rows, D_MODEL), "mla_w_uq": (Q_RANK // N_CHIPS, MLA_HEADS * MLA_QK),
               "w_out": (D_MODEL // N_CHIPS, D_MODEL), "w_gate": (FF_SHARD, D_MODEL), "w_up": (FF_SHARD, D_MODEL),
               "w_down": (FF_SHARD, D_MODEL)}
    transposed = ("w_in", "w_gate", "w_up")
    to2d = lambda n, a: a[0].T if n in transposed else a.reshape(shard2d[n])
    from2d = lambda n, t: t.T[None] if n in transposed else t.reshape(W[n].shape)
    me = 2 * cx + cy
    place = jnp.stack([cc, me]).astype(jnp.int32)
    local_b = [to2d(n, W[n]).astype(BF16) for n in BIG_NAMES]
    local_b[0] = jnp.pad(local_b[0], ((0, FF_SHARD - win_rows), (0, 0)))
    stacks = _gather_chips(local_b[:2], "gather_weights")
    win4, wuq4 = [lax.dynamic_update_slice(s, l[None], (me, 0, 0)) for s, l in zip(stacks, local_b)]
    win_t = win4[:, :win_rows].reshape(D_IN, D_MODEL)
    wuq_full = wuq4.reshape(Q_RANK, MLA_HEADS, MLA_QK)
    win_arr, wq_arr, wk_arr, wv_arr = _arrange_weights(win_t, wuq_full, mla_w_ukv[0].astype(BF16))
    small = {n: W[n][0] if n == "mla_w_ukv" else W[n].reshape(-1, W[n].shape[-1]) for n in SMALL_NAMES}

    loss_local, dx, grads, (ffn_rs, ffn_ris) = _local_step(x[0], positions.reshape(T, 1), loss_target[0], small, win_arr,
                                                           wq_arr, wk_arr, wv_arr, local_b[2:], place)

    dwin4 = jnp.pad(grads["w_in"].reshape(N_CHIPS, win_rows, D_MODEL), ((0, 0), (0, FF_SHARD - win_rows), (0, 0)))
    gs = [dwin4, grads["mla_w_uq"].reshape((N_CHIPS,) + UQ_COMM_SHAPE), grads["w_out"].reshape((N_CHIPS,) + shard2d["w_out"])]
    ffn_gs = [grads["w_gate"], grads["w_up"], grads["w_down"]]
    sm = _pack_small(grads, loss_local)
    *rs, ssib = _pair_swap(gs, sm)
    *ps, pair = _pair_sum(place, gs, rs, small=(sm, ssib))
    *ris, sm4 = _chip_swap(ps, pair)
    gfs = _chip_sum(place, gs + ffn_gs, rs + ffn_rs, ris + ffn_ris)
    *gfin, smf = _pair_fill(gfs, sm4)

    row_blocks = {"w_in": 216, "mla_w_uq": 96, "w_out": 256, "w_gate": 352, "w_up": 352, "w_down": 352}
    G, DW, NM, NV = {}, {}, {}, {}
    for k, n in enumerate(BIG_NAMES):
        g2 = gfin[k][:win_rows] if n == "w_in" else gfin[k].reshape(shard2d[n])
        d, mo, vo = _adamw(to2d(n, W[n]), g2, to2d(n, M[n]), to2d(n, V[n]), row_blocks[n], "adamw_" + n)
        G[n], DW[n], NM[n], NV[n] = (from2d(n, t) for t in (g2, d, mo, vo))
    gs_buf, ds, ms, vs = _adamw_small(smf, _pack_small(W), _pack_small(M), _pack_small(V))
    shapes = {n: W[n].shape for n in SMALL_NAMES}
    for dst, buf in ((G, gs_buf), (DW, ds), (NM, ms), (NV, vs)):
        dst.update(_unpack_small(buf, shapes))

    loss = gs_buf.reshape(-1)[sum(SMALL_SIZES)]
    return (loss, dx[None], *[G[n] for n in WEIGHT_NAMES], *[DW[n] for n in WEIGHT_NAMES],
            *[NM[n] for n in WEIGHT_NAMES], *[NV[n] for n in WEIGHT_NAMES])
```

```python
import jax
import jax.numpy as jnp
from jax import lax
from jax.experimental import pallas as pl
from jax.experimental.pallas import tpu as pltpu

F32 = jnp.float32
BF16 = jnp.bfloat16
MXU_DTYPE = BF16

D_MODEL = 1024
MLA_HEADS = 8
MLA_NOPE = 64
MLA_ROPE = 32
MLA_V = 64
MLA_QK = MLA_NOPE + MLA_ROPE
Q_RANK = 384
KV_RANK = 128
MLA_WIDTH = MLA_HEADS * MLA_V
HEAD_PAD = 128
HGRN_HEADS = 4
HGRN_DIM = 128
HGRN_WIDTH = HGRN_HEADS * HGRN_DIM
CHUNK = 64
SUB = 16
D_IN = Q_RANK + KV_RANK + MLA_ROPE + 4 * HGRN_WIDTH
D_IN_ARR = Q_RANK + KV_RANK + HEAD_PAD + 4 * HGRN_WIDTH
D_FF = 2816
N_CHIPS = 4
FF_SHARD = D_FF // N_CHIPS
EPS = 1e-6
ROPE_THETA = 10000.0
ATTN_SCALE = MLA_QK ** -0.5
NEG_BIG = -1e30

ADAM_LR = 0.001
ADAM_B1 = 0.9
ADAM_B2 = 0.999
ADAM_EPS = 1e-08
ADAM_WD = 0.01
ADAM_STEP = 10

VMEM_LIMIT = 56 * 1024 * 1024

SMALL_ROWS = 144
SMALL_SIZES = (1024, 384, 128, 131072, 512, 1024, 512, 1024, 1024, 1024)

MESH = pl.DeviceIdType.MESH
ANY = pl.BlockSpec(memory_space=pl.ANY)


def _dot(a, b, dims, exact):
    if exact:
        return lax.dot_general(a.astype(F32), b.astype(F32), (dims, ((), ())), precision=lax.Precision.HIGH,
                               preferred_element_type=F32)
    return lax.dot_general(a.astype(MXU_DTYPE), b.astype(MXU_DTYPE), (dims, ((), ())), preferred_element_type=F32)


def _mm(a, b, exact=False):
    return _dot(a, b, ((1,), (0,)), exact)


def _mm_nt(a, b, exact=False):
    return _dot(a, b, ((1,), (1,)), exact)


def _mm_tn(a, b, exact=False):
    return _dot(a, b, ((0,), (0,)), exact)


def _rms_fwd(x, w):
    r = lax.rsqrt(jnp.mean(x * x, axis=-1, keepdims=True) + EPS)
    xn = x * r
    return xn * w, xn, r


def _rms_bwd(dy, xn, r, w):
    dxn = dy * w
    dx = r * (dxn - xn * jnp.mean(dxn * xn, axis=-1, keepdims=True))
    dw = jnp.sum(dy * xn, axis=0, keepdims=True)
    return dx, dw


def _group_sums(v, gs):
    t, n = v.shape
    lane = lax.broadcasted_iota(jnp.int32, (t, 128), 1)
    out = []
    for p in range(n // 128):
        vb = v[:, 128 * p:128 * (p + 1)]
        if gs == 128:
            out.append(jnp.sum(vb, axis=-1, keepdims=True))
        else:
            out.append(jnp.sum(jnp.where(lane < 64, vb, 0.0), axis=-1, keepdims=True))
            out.append(jnp.sum(jnp.where(lane >= 64, vb, 0.0), axis=-1, keepdims=True))
    return out


def _group_bcast(sums, gs, t):
    lane = lax.broadcasted_iota(jnp.int32, (t, 128), 1)
    if gs == 128:
        return jnp.concatenate([jnp.broadcast_to(s, (t, 128)) for s in sums], axis=-1)
    return jnp.concatenate([jnp.where(lane < 64, sums[2 * p], sums[2 * p + 1]) for p in range(len(sums) // 2)],
                           axis=-1)


def _grms_fwd(x, w, gs):
    t = x.shape[0]
    r = lax.rsqrt(_group_bcast(_group_sums(x * x, gs), gs, t) * (1.0 / gs) + EPS)
    xn = x * r
    return xn * w, xn, r


def _grms_bwd(dy, xn, r, w, gs):
    t = dy.shape[0]
    dxn = dy * w
    dx = r * (dxn - xn * (_group_bcast(_group_sums(dxn * xn, gs), gs, t) * (1.0 / gs)))
    dw = jnp.sum(dy * xn, axis=0, keepdims=True)
    return dx, dw


def _rope_tables(c_tab, s_tab):
    lane = lax.broadcasted_iota(jnp.int32, c_tab.shape, 1)
    first = (lane >= MLA_NOPE) & (lane < MLA_NOPE + MLA_ROPE // 2)
    second = (lane >= MLA_NOPE + MLA_ROPE // 2) & (lane < MLA_QK)
    return c_tab, jnp.where(first, -s_tab, 0.0), jnp.where(second, s_tab, 0.0)


def _rope(v, c, sa, sb):
    return v * c + pltpu.roll(v, HEAD_PAD - MLA_ROPE // 2, 1) * sa + pltpu.roll(v, MLA_ROPE // 2, 1) * sb


def _rope_bwd(d, c, sa, sb):
    return d * c - pltpu.roll(d, HEAD_PAD - MLA_ROPE // 2, 1) * sa - pltpu.roll(d, MLA_ROPE // 2, 1) * sb


def _params(sem, vmem=VMEM_LIMIT):
    return pltpu.CompilerParams(dimension_semantics=sem, vmem_limit_bytes=vmem)


def _in_fwd(x, pos, invf, w_pre, win, qnw, wq, kvnw, wk, wv, tt=256):
    T = x.shape[0]

    def body(x_ref, pos_ref, invf_ref, wpre_ref, win_ref, qnw_ref, wq_ref, kvnw_ref, wk_ref, wv_ref,
             cq_ref, ckv_ref, xph_ref, q_ref, k_ref, v_ref, kt_ref, vt_ref, rc_ref, rs_ref):
        u, _, _ = _rms_fwd(x_ref[...], wpre_ref[...])
        lo = Q_RANK + KV_RANK + HEAD_PAD
        xp = _mm_nt(u, win_ref[:lo, :])
        xph_ref[...] = _mm_nt(u, win_ref[lo:, :])
        cq = xp[:, :Q_RANK]
        ckv = xp[:, Q_RANK:Q_RANK + KV_RANK]
        kr = xp[:, Q_RANK + KV_RANK:]
        cq_ref[...] = cq
        ckv_ref[...] = ckv
        ang = pos_ref[...].astype(F32) * invf_ref[...]
        c_tab = jnp.cos(ang)
        s_tab = jnp.sin(ang)
        rc_ref[...] = c_tab
        rs_ref[...] = s_tab
        c, sa, sb = _rope_tables(c_tab, s_tab)
        qn, _, _ = _rms_fwd(cq, qnw_ref[...])
        q = _mm(qn, wq_ref[...])
        kvn, _, _ = _rms_fwd(ckv, kvnw_ref[...])
        kn = _mm(kvn, wk_ref[...])
        v = _mm(kvn, wv_ref[...])
        v_ref[...] = v.astype(v_ref.dtype)
        vt_ref[...] = v.T.astype(vt_ref.dtype)
        krr = _rope(kr, c, sa, sb)
        for h in range(MLA_HEADS):
            sl = slice(HEAD_PAD * h, HEAD_PAD * (h + 1))
            q_ref[:, sl] = _rope(q[:, sl], c, sa, sb).astype(q_ref.dtype)
            kh = kn[:, sl] + krr
            k_ref[:, sl] = kh.astype(k_ref.dtype)
            kt_ref[sl, :] = kh.T.astype(kt_ref.dtype)

    row = lambda w: pl.BlockSpec((tt, w), lambda i: (i, 0))
    full = lambda a: pl.BlockSpec(a.shape, lambda i: (0,) * a.ndim)
    qk_w = MLA_HEADS * HEAD_PAD
    return pl.pallas_call(
        body, name="in_fwd", grid=(T // tt,),
        in_specs=[row(D_MODEL), row(1), full(invf), full(w_pre), full(win), full(qnw), full(wq), full(kvnw),
                  full(wk), full(wv)],
        out_specs=[row(Q_RANK), row(KV_RANK), row(4 * HGRN_WIDTH), row(qk_w), row(qk_w), row(MLA_WIDTH),
                   pl.BlockSpec((qk_w, tt), lambda i: (0, i)), pl.BlockSpec((MLA_WIDTH, tt), lambda i: (0, i)),
                   row(HEAD_PAD), row(HEAD_PAD)],
        out_shape=[jax.ShapeDtypeStruct((T, Q_RANK), F32), jax.ShapeDtypeStruct((T, KV_RANK), F32),
                   jax.ShapeDtypeStruct((T, 4 * HGRN_WIDTH), F32), jax.ShapeDtypeStruct((T, qk_w), MXU_DTYPE),
                   jax.ShapeDtypeStruct((T, qk_w), MXU_DTYPE), jax.ShapeDtypeStruct((T, MLA_WIDTH), MXU_DTYPE),
                   jax.ShapeDtypeStruct((qk_w, T), MXU_DTYPE), jax.ShapeDtypeStruct((MLA_WIDTH, T), MXU_DTYPE),
                   jax.ShapeDtypeStruct((T, HEAD_PAD), F32), jax.ShapeDtypeStruct((T, HEAD_PAD), F32)],
        compiler_params=_params(("arbitrary",)),
    )(x, pos, invf, w_pre, win, qnw, wq, kvnw, wk, wv)


def _attn_fwd_t(qb, kb, vt, gather=(), tq=256, hps=8):
    T = qb.shape[0]
    nq = T // tq
    ng = len(gather)
    steps = (MLA_HEADS // hps) * nq
    pass_on = steps - 3

    def body(q_ref, k_ref, vt_ref, *rest):
        o_ref, lse_ref = rest[ng:ng + 2]
        acc_scr = rest[2 * ng + 2]
        qi = pl.program_id(1)
        step_no = pl.program_id(0) * nq + qi
        if ng:
            gat = _Gather(rest[:ng], rest[ng + 2:2 * ng + 2], *rest[2 * ng + 3:])

            @pl.when(step_no == 0)
            def _():
                for cp in gat.sends():
                    cp.start()

            @pl.when(step_no == pass_on)
            def _():
                for arrival in gat.arrivals():
                    arrival.wait_recv()
                for cp in gat.forwards():
                    cp.start()

        heads = [slice(HEAD_PAD * a, HEAD_PAD * (a + 1)) for a in range(hps)]
        acc_scr[...] = jnp.zeros_like(acc_scr)

        def step(j, carry, masked):
            start = pl.multiple_of(j * tq, tq)
            scores = [_mm_nt(k_ref[pl.ds(start, tq), heads[a]], q_ref[:, heads[a]]) for a in range(hps)]
            new = []
            for a in range(hps):
                m, l = carry[a]
                s = scores[a] * ATTN_SCALE
                if masked:
                    kk = lax.broadcasted_iota(jnp.int32, (tq, tq), 0)
                    qq = lax.broadcasted_iota(jnp.int32, (tq, tq), 1)
                    s = jnp.where(kk <= qq, s, NEG_BIG)
                m_new = jnp.maximum(m, jnp.max(s, axis=0, keepdims=True))
                alpha = jnp.exp(m - m_new)
                p = jnp.exp(s - m_new)
                l = l * alpha + jnp.sum(p, axis=0, keepdims=True)
                vtj = vt_ref[2 * MLA_V * (a // 2):2 * MLA_V * (a // 2 + 1), pl.ds(start, tq)]
                acc_scr[a] = acc_scr[a] * alpha + _mm(vtj, p)
                new.append((m_new, l))
            return tuple(new)

        init = tuple((jnp.full((1, tq), NEG_BIG, F32), jnp.zeros((1, tq), F32)) for _ in range(hps))
        carry = lax.fori_loop(0, qi, lambda j, c: step(j, c, False), init)
        carry = step(qi, carry, True)
        row = lax.broadcasted_iota(jnp.int32, (2 * MLA_V, tq), 0)
        for pr in range(hps // 2):
            (m0, l0), (m1, l1) = carry[2 * pr], carry[2 * pr + 1]
            ot = jnp.where(row < MLA_V, acc_scr[2 * pr] / l0, acc_scr[2 * pr + 1] / l1)
            o_ref[:, 2 * MLA_V * pr:2 * MLA_V * (pr + 1)] = ot.T
            lse_ref[pr, 0:1, :] = m0 + jnp.log(l0)
            lse_ref[pr, 1:2, :] = m1 + jnp.log(l1)

        if ng:
            @pl.when(step_no == steps - 1)
            def _():
                for arrival in gat.forward_arrivals():
                    arrival.wait_recv()
                for cp in gat.sends() + gat.forwards():
                    cp.wait_send()

    return pl.pallas_call(
        body, name="attn_fwd", grid=(MLA_HEADS // hps, nq),
        in_specs=[pl.BlockSpec((tq, hps * HEAD_PAD), lambda g, i: (i, g)),
                  pl.BlockSpec((T, hps * HEAD_PAD), lambda g, i: (0, g)),
                  pl.BlockSpec((hps * MLA_V, T), lambda g, i: (g, 0))] + [ANY] * ng,
        out_specs=[pl.BlockSpec((tq, hps * MLA_V), lambda g, i: (i, g)),
                   pl.BlockSpec((hps // 2, 2, tq), lambda g, i: (g, 0, i))] + [ANY] * ng,
        out_shape=[jax.ShapeDtypeStruct((T, MLA_WIDTH), F32), jax.ShapeDtypeStruct((MLA_HEADS // 2, 2, T), F32)]
        + _Gather.out_shapes(gather),
        scratch_shapes=[pltpu.VMEM((hps, 2 * MLA_V, tq), F32)] + (_Gather.semaphores(gather) if ng else []),
        compiler_params=_params(("arbitrary", "arbitrary")),
    )(qb, kb, vt, *gather)


def _attn_bwd_t(qb, kb, kt, vb, dob, lse, dvec, send=(), tq=256, hps=4):
    T = qb.shape[0]
    nq = T // tq
    ns = len(send)
    steps = (MLA_HEADS // hps) * nq

    def body(q_ref, k_ref, kt_ref, v_ref, do_ref, lse_ref, d_ref, *rest):
        dqt_ref, dk_ref, dv_ref = rest[ns:ns + 3]
        va_scr, dv_scr = rest[2 * ns + 3:2 * ns + 5]
        j = pl.program_id(1)
        step_no = pl.program_id(0) * nq + j
        if ns:
            @pl.when(step_no == 0)
            def _():
                for cp in _chip_swap_copies(rest[:ns], rest[ns + 3:2 * ns + 3], *rest[2 * ns + 5:]):
                    cp.start()

        @pl.when(j == 0)
        def _():
            dqt_ref[...] = jnp.zeros_like(dqt_ref)

        lane = lax.broadcasted_iota(jnp.int32, (tq, 2 * MLA_V), 1)
        heads = [slice(HEAD_PAD * a, HEAD_PAD * (a + 1)) for a in range(hps)]
        pairs = [slice(2 * MLA_V * p, 2 * MLA_V * (p + 1)) for p in range(hps // 2)]
        for pr in range(hps // 2):
            vpair = v_ref[:, pairs[pr]]
            va_scr[2 * pr] = jnp.where(lane < MLA_V, vpair, jnp.zeros_like(vpair))
            va_scr[2 * pr + 1] = jnp.where(lane >= MLA_V, vpair, jnp.zeros_like(vpair))
        dk_ref[...] = jnp.zeros_like(dk_ref)
        dv_scr[...] = jnp.zeros_like(dv_scr)

        def step(i, masked):
            start = pl.multiple_of(i * tq, tq)
            rows = pl.ds(start, tq)
            scores = [_mm_nt(k_ref[:, heads[a]], q_ref[rows, heads[a]]) for a in range(hps)]
            dps = [_mm_nt(va_scr[a], do_ref[rows, pairs[a // 2]]) for a in range(hps)]
            for a in range(hps):
                pr, r = a // 2, a % 2
                p = jnp.exp(scores[a] * ATTN_SCALE - lse_ref[pr, r:r + 1, rows])
                if masked:
                    kk = lax.broadcasted_iota(jnp.int32, (tq, tq), 0)
                    qq = lax.broadcasted_iota(jnp.int32, (tq, tq), 1)
                    p = jnp.where(kk <= qq, p, 0.0)
                ds = p * (dps[a] - d_ref[pr, r:r + 1, rows]) * ATTN_SCALE
                dv_scr[a] += _mm(p, do_ref[rows, pairs[pr]])
                dk_ref[:, heads[a]] += _mm(ds, q_ref[rows, heads[a]])
                dqt_ref[heads[a], rows] += _mm(kt_ref[heads[a], :], ds)

        def loop_body(i, _):
            step(i, False)
            return 0

        step(j, True)
        lax.fori_loop(j + 1, nq, loop_body, 0)
        for pr in range(hps // 2):
            dv_ref[:, pairs[pr]] = jnp.where(lane < MLA_V, dv_scr[2 * pr], dv_scr[2 * pr + 1])

        if ns:
            @pl.when(step_no == steps - 1)
            def _():
                for cp in _chip_swap_copies(rest[:ns], rest[ns + 3:2 * ns + 3], *rest[2 * ns + 5:]):
                    cp.wait()

    stat = pl.BlockSpec((hps // 2, 2, T), lambda g, j: (g, 0, 0))
    return pl.pallas_call(
        body, name="attn_bwd", grid=(MLA_HEADS // hps, nq),
        in_specs=[pl.BlockSpec((T, hps * HEAD_PAD), lambda g, j: (0, g)),
                  pl.BlockSpec((tq, hps * HEAD_PAD), lambda g, j: (j, g)),
                  pl.BlockSpec((hps * HEAD_PAD, tq), lambda g, j: (g, j)),
                  pl.BlockSpec((tq, hps * MLA_V), lambda g, j: (j, g)),
                  pl.BlockSpec((T, hps * MLA_V), lambda g, j: (0, g)), stat, stat] + [ANY] * ns,
        out_specs=[pl.BlockSpec((hps * HEAD_PAD, T), lambda g, j: (g, 0)),
                   pl.BlockSpec((tq, hps * HEAD_PAD), lambda g, j: (j, g)),
                   pl.BlockSpec((tq, hps * MLA_V), lambda g, j: (j, g))] + [ANY] * ns,
        out_shape=[jax.ShapeDtypeStruct((MLA_HEADS * HEAD_PAD, T), F32),
                   jax.ShapeDtypeStruct((T, MLA_HEADS * HEAD_PAD), F32),
                   jax.ShapeDtypeStruct((T, MLA_WIDTH), F32)] + _chip_swap_shapes(send),
        scratch_shapes=[pltpu.VMEM((hps, tq, 2 * MLA_V), vb.dtype), pltpu.VMEM((hps, tq, 2 * MLA_V), F32)]
        + ([pltpu.SemaphoreType.DMA((3 * ns,)), pltpu.SemaphoreType.DMA((3 * ns,))] if ns else []),
        compiler_params=_params(("arbitrary", "arbitrary")),
    )(qb, kb, kt, vb, dob, lse, dvec, *send)


def _cumsum_rows(x):
    n = x.shape[0]
    row = lax.broadcasted_iota(jnp.int32, x.shape, 0)
    s = 1
    while s < n:
        x = x + jnp.where(row >= s, pltpu.roll(x, s, 0), 0.0)
        s *= 2
    return x


def _rev_cumsum_rows(x):
    n = x.shape[0]
    row = lax.broadcasted_iota(jnp.int32, x.shape, 0)
    s = 1
    while s < n:
        x = x + jnp.where(row < n - s, pltpu.roll(x, n - s, 0), 0.0)
        s *= 2
    return x


def _lb_from_logits(l):
    l0, l1 = l[0:1, :], l[1:2, :]
    m = jnp.maximum(l0, l1)
    e0, e1 = jnp.exp(l0 - m), jnp.exp(l1 - m)
    return e0 / (e0 + e1)


def _hgrn_gates(hq, hf, lb):
    sig_f = jax.nn.sigmoid(hf)
    f = lb + (1.0 - lb) * sig_f
    sig_q = jax.nn.sigmoid(hq)
    return sig_f, f, jnp.log(f), 1.0 - f, sig_q, hq * sig_q


def _hgrn_intra(q, kk, b, exact=False):
    row = lax.broadcasted_iota(jnp.int32, b.shape, 0)
    qs, ks, eqs, eks, a_rows = [], [], [], [], []
    for i in range(CHUNK // SUB):
        ref = b[SUB * i:SUB * i + 1, :]
        eq = jnp.exp(b[SUB * i:SUB * (i + 1), :] - ref)
        ek = jnp.exp(jnp.where(row < SUB * (i + 1), ref - b, NEG_BIG))
        qi = q[SUB * i:SUB * (i + 1), :] * eq
        ki = kk * ek
        a_rows.append(_mm_nt(qi, ki, exact))
        qs.append(qi), ks.append(ki), eqs.append(eq), eks.append(ek)
    tt = lax.broadcasted_iota(jnp.int32, (CHUNK, CHUNK), 0)
    ss = lax.broadcasted_iota(jnp.int32, (CHUNK, CHUNK), 1)
    causal = ss <= tt
    a = jnp.where(causal, jnp.concatenate(a_rows, axis=0), 0.0)
    return a, causal, qs, ks, eqs, eks


def _hgrn_fwd(xph, lbl, tg=512):
    T = xph.shape[0]
    ng, ncg = T // tg, tg // CHUNK
    cols = [slice(HGRN_DIM * h, HGRN_DIM * (h + 1)) for h in range(HGRN_HEADS)]

    def body(lbl_ref, hq_ref, hf_ref, hi_ref, o_ref, st_ref, s_scr):
        @pl.when(pl.program_id(0) == 0)
        def _():
            s_scr[...] = jnp.zeros_like(s_scr)

        lb = _lb_from_logits(lbl_ref[...])

        def chunk(c, _):
            rows = pl.ds(pl.multiple_of(c * CHUNK, CHUNK), CHUNK)
            pre = []
            for cs in cols:
                _, _, lf, kk, _, q = _hgrn_gates(hq_ref[rows, cs], hf_ref[rows, cs], lb[:, cs])
                v = hi_ref[rows, cs]
                b = _cumsum_rows(lf)
                a = _hgrn_intra(q, kk, b)[0]
                b_last = b[CHUNK - 1:CHUNK, :]
                pre.append((q * jnp.exp(b), a, v, jnp.exp(b_last), _mm_tn(v, kk * jnp.exp(b_last - b))))
            for h, cs in enumerate(cols):
                qe, a, v, ebl, upd = pre[h]
                st = s_scr[h]
                st_ref[h, c] = st
                o_ref[rows, cs] = _mm_nt(qe, st) + _mm(a, v)
                s_scr[h] = st * ebl + upd
            return 0

        lax.fori_loop(0, ncg, chunk, 0)

    col = lambda k: pl.BlockSpec((tg, HGRN_WIDTH), lambda g: (g, k))
    return pl.pallas_call(
        body, name="hgrn_fwd", grid=(ng,),
        in_specs=[pl.BlockSpec((2, HGRN_WIDTH), lambda g: (0, 0)), col(0), col(1), col(2)],
        out_specs=[col(0), pl.BlockSpec((HGRN_HEADS, ncg, HGRN_DIM, HGRN_DIM), lambda g: (0, g, 0, 0))],
        out_shape=[jax.ShapeDtypeStruct((T, HGRN_WIDTH), F32),
                   jax.ShapeDtypeStruct((HGRN_HEADS, T // CHUNK, HGRN_DIM, HGRN_DIM), F32)],
        scratch_shapes=[pltpu.VMEM((HGRN_HEADS, HGRN_DIM, HGRN_DIM), F32)],
        compiler_params=_params(("arbitrary",)),
    )(lbl, xph, xph, xph)


def _hgrn_bwd(xph, lbl, states, d_o, tg=512):
    T = xph.shape[0]
    ng, ncg = T // tg, tg // CHUNK
    cols = [slice(HGRN_DIM * h, HGRN_DIM * (h + 1)) for h in range(HGRN_HEADS)]
    nsub = CHUNK // SUB

    def body(lbl_ref, hq_ref, hf_ref, hi_ref, st_ref, do_ref, dhq_ref, dhf_ref, dhi_ref, dlg_ref, ds_scr, dlb_scr):
        g = pl.program_id(0)

        @pl.when(g == 0)
        def _():
            ds_scr[...] = jnp.zeros_like(ds_scr)
            dlb_scr[...] = jnp.zeros_like(dlb_scr)

        lb = _lb_from_logits(lbl_ref[...])

        def chunk(ci, _):
            c = ncg - 1 - ci
            rows = pl.ds(pl.multiple_of(c * CHUNK, CHUNK), CHUNK)
            pre = []
            for h, cs in enumerate(cols):
                hq = hq_ref[rows, cs]
                sig_f, f, lf, kk, sig_q, q = _hgrn_gates(hq, hf_ref[rows, cs], lb[:, cs])
                v = hi_ref[rows, cs]
                do = do_ref[rows, cs]
                b = _cumsum_rows(lf)
                eb = jnp.exp(b)
                a, causal, qs, ks, eqs, eks = _hgrn_intra(q, kk, b, exact=True)
                b_last = b[CHUNK - 1:CHUNK, :]
                st = st_ref[h, c]
                pre.append(dict(hq=hq, sig_f=sig_f, f=f, kk=kk, sig_q=sig_q, q=q, v=v, eb=eb, qs=qs, ks=ks, eqs=eqs,
                                eks=eks, ebl=jnp.exp(b_last), el=jnp.exp(b_last - b), st=st,
                                da=jnp.where(causal, _mm_nt(do, v, True), 0.0), dq=_mm(do, st, True) * eb,
                                dv=_mm_tn(a, do, True), dsu=_mm_tn(do, q * eb, True)))
            for w in pre:
                dq_rows = []
                dk = jnp.zeros_like(w["q"])
                for i in range(nsub):
                    dai = w["da"][SUB * i:SUB * (i + 1), :]
                    dq_rows.append(_mm(dai, w["ks"][i], True) * w["eqs"][i])
                    dk = dk + _mm_tn(dai, w["qs"][i], True) * w["eks"][i]
                w["dq"] = w["dq"] + jnp.concatenate(dq_rows, axis=0)
                w["dk"] = dk
            for h, cs in enumerate(cols):
                w = pre[h]
                kk, el, ebl, dst = w["kk"], w["el"], w["ebl"], ds_scr[h]
                dk_state = _mm(w["v"], dst, True) * el
                dk = w["dk"] + dk_state
                e_last = (ebl * jnp.sum(w["st"] * dst, axis=0, keepdims=True)
                          + jnp.sum(kk * dk_state, axis=0, keepdims=True))
                dlf = _rev_cumsum_rows(w["q"] * w["dq"] - kk * dk) + e_last
                ds_scr[h] = dst * ebl + w["dsu"]
                df = dlf / w["f"] - dk
                sig_f, sig_q = w["sig_f"], w["sig_q"]
                dhf_ref[rows, cs] = df * (1.0 - lb[:, cs]) * sig_f * (1.0 - sig_f)
                dlb_scr[:, cs] += jnp.sum(df * (1.0 - sig_f), axis=0, keepdims=True)
                dhq_ref[rows, cs] = w["dq"] * sig_q * (1.0 + w["hq"] * (1.0 - sig_q))
                dhi_ref[rows, cs] = w["dv"] + _mm_nt(kk * el, dst, True)
            return 0

        lax.fori_loop(0, ncg, chunk, 0)

        @pl.when(g == ng - 1)
        def _():
            dl0 = dlb_scr[...] * lb * (1.0 - lb)
            dlg_ref[...] = jnp.concatenate([dl0, -dl0], axis=0)

    col = lambda k: pl.BlockSpec((tg, HGRN_WIDTH), lambda g: (ng - 1 - g, k))
    logits = pl.BlockSpec((2, HGRN_WIDTH), lambda g: (0, 0))
    big = jax.ShapeDtypeStruct((T, HGRN_WIDTH), F32)
    return pl.pallas_call(
        body, name="hgrn_bwd", grid=(ng,),
        in_specs=[logits, col(0), col(1), col(2),
                  pl.BlockSpec((HGRN_HEADS, ncg, HGRN_DIM, HGRN_DIM), lambda g: (0, ng - 1 - g, 0, 0)), col(0)],
        out_specs=[col(0), col(0), col(0), logits],
        out_shape=[big, big, big, jax.ShapeDtypeStruct((2, HGRN_WIDTH), F32)],
        scratch_shapes=[pltpu.VMEM((HGRN_HEADS, HGRN_DIM, HGRN_DIM), F32), pltpu.VMEM((1, HGRN_WIDTH), F32)],
        compiler_params=_params(("arbitrary",)),
    )(lbl, xph, xph, xph, states, d_o)


def _proj_fwd(x, o_raw, oh_raw, xph, wout, w_mla, w_hg, w_post, w_fpre, tt=512):
    T = x.shape[0]

    def body(x_ref, o_ref, oh_ref, hg_ref, wout_ref, wmla_ref, whg_ref, wpost_ref, wfpre_ref,
             h1_ref, y1_ref, z_ref, mix_ref):
        om, _, _ = _grms_fwd(o_ref[...], wmla_ref[...], MLA_V)
        hg = hg_ref[...]
        ohn, _, _ = _grms_fwd(oh_ref[...], whg_ref[...], HGRN_DIM)
        mix = jnp.concatenate([om, ohn * (hg * jax.nn.sigmoid(hg))], axis=-1)
        mix_ref[...] = mix.astype(mix_ref.dtype)
        y1 = _mm(mix, wout_ref[...])
        y1_ref[...] = y1
        h1 = x_ref[...] + _rms_fwd(y1, wpost_ref[...])[0]
        h1_ref[...] = h1
        z_ref[...] = _rms_fwd(h1, wfpre_ref[...])[0].astype(z_ref.dtype)

    row = lambda w: pl.BlockSpec((tt, w), lambda i: (i, 0))
    full = lambda a: pl.BlockSpec(a.shape, lambda i: (0,) * a.ndim)
    sds = jax.ShapeDtypeStruct
    return pl.pallas_call(
        body, name="proj_fwd", grid=(T // tt,),
        in_specs=[row(D_MODEL), row(MLA_WIDTH), row(HGRN_WIDTH), pl.BlockSpec((tt, HGRN_WIDTH), lambda i: (i, 3)),
                  full(wout), full(w_mla), full(w_hg), full(w_post), full(w_fpre)],
        out_specs=[row(D_MODEL)] * 4,
        out_shape=[sds((T, D_MODEL), F32), sds((T, D_MODEL), F32), sds((T, D_MODEL), MXU_DTYPE),
                   sds((T, D_MODEL), MXU_DTYPE)],
        compiler_params=_params(("arbitrary",)),
    )(x, o_raw, oh_raw, xph, wout, w_mla, w_hg, w_post, w_fpre)


def _ffn_fwd(zb, h1, tgt, w_fpost, wg, wu, wd, tt=256):
    T = zb.shape[0]
    nj = N_CHIPS

    def body(z_ref, h1_ref, tgt_ref, wfpost_ref, wg_ref, wu_ref, wd_ref, g_ref, up_ref, dy2_ref, dh2_ref, loss_ref, dwf_ref):
        @pl.when(pl.program_id(0) == 0)
        def _():
            loss_ref[...] = jnp.zeros_like(loss_ref)
            dwf_ref[...] = jnp.zeros_like(dwf_ref)

        z = z_ref[...]
        gs = [_mm_nt(z, wg_ref[j]) for j in range(nj)]
        ups = [_mm_nt(z, wu_ref[j]) for j in range(nj)]
        y2 = jnp.zeros((tt, D_MODEL), F32)
        for j in range(nj):
            g_ref[j] = gs[j]
            up_ref[j] = ups[j]
            y2 = y2 + _mm(gs[j] * jax.nn.sigmoid(gs[j]) * ups[j], wd_ref[j])
        w = wfpost_ref[...]
        y2s, y2n, r2 = _rms_fwd(y2, w)
        e = h1_ref[...] + y2s - tgt_ref[...]
        loss_ref[...] += jnp.sum(e * e, axis=0, keepdims=True)
        dh2 = e * (1.0 / D_MODEL)
        dh2_ref[...] = dh2
        dy2, dwf = _rms_bwd(dh2, y2n, r2, w)
        dy2_ref[...] = dy2.astype(dy2_ref.dtype)
        dwf_ref[...] += dwf

    row = pl.BlockSpec((tt, D_MODEL), lambda i: (i, 0))
    vec = pl.BlockSpec((1, D_MODEL), lambda i: (0, 0))
    resident = pl.BlockSpec((nj, FF_SHARD, D_MODEL), lambda i: (0, 0, 0), pipeline_mode=pl.Buffered(1))
    act = pl.BlockSpec((nj, tt, FF_SHARD), lambda i: (0, i, 0))
    sds = jax.ShapeDtypeStruct
    return pl.pallas_call(
        body, name="ffn_fwd", grid=(T // tt,),
        in_specs=[row, row, row, vec, resident, resident, resident],
        out_specs=[act, act, row, row, vec, vec],
        out_shape=[sds((nj, T, FF_SHARD), F32), sds((nj, T, FF_SHARD), F32), sds((T, D_MODEL), MXU_DTYPE),
                   sds((T, D_MODEL), F32), sds((1, D_MODEL), F32), sds((1, D_MODEL), F32)],
        compiler_params=_params(("arbitrary",)),
    )(zb, h1, tgt, w_fpost, wg, wu, wd)


def _ffn_bwd(zb, g, up, dy2b, wg, wu, wd, tt=512):
    T = zb.shape[0]
    nj = N_CHIPS

    def body(z_ref, g_ref, up_ref, dy2_ref, wg_ref, wu_ref, wd_ref, dwg_ref, dwu_ref, dwd_ref, dz_ref):
        @pl.when(pl.program_id(1) == 0)
        def _():
            dwg_ref[...] = jnp.zeros_like(dwg_ref)
            dwu_ref[...] = jnp.zeros_like(dwu_ref)
            dwd_ref[...] = jnp.zeros_like(dwd_ref)

        z, g_, up_, dy2 = z_ref[...], g_ref[0], up_ref[0], dy2_ref[...]
        sg = jax.nn.sigmoid(g_)
        act = g_ * sg
        dff = _mm_nt(dy2, wd_ref[0])
        dwd_ref[0] += _mm_tn(act * up_, dy2)
        dg = dff * up_ * sg * (1.0 + g_ * (1.0 - sg))
        dup = dff * act
        dwg_ref[0] += _mm_tn(dg, z)
        dwu_ref[0] += _mm_tn(dup, z)
        dz_ref[0] = _mm(dg, wg_ref[0]) + _mm(dup, wu_ref[0])

    row = pl.BlockSpec((tt, D_MODEL), lambda j, i: (i, 0))
    act = pl.BlockSpec((1, tt, FF_SHARD), lambda j, i: (j, i, 0))
    w_sh = pl.BlockSpec((1, FF_SHARD, D_MODEL), lambda j, i: (j, 0, 0))
    w_grad = jax.ShapeDtypeStruct((nj, FF_SHARD, D_MODEL), F32)
    return pl.pallas_call(
        body, name="ffn_bwd", grid=(nj, T // tt),
        in_specs=[row, act, act, row, w_sh, w_sh, w_sh],
        out_specs=[w_sh, w_sh, w_sh, pl.BlockSpec((1, tt, D_MODEL), lambda j, i: (j, i, 0))],
        out_shape=[w_grad, w_grad, w_grad, jax.ShapeDtypeStruct((nj, T, D_MODEL), F32)],
        compiler_params=_params(("arbitrary", "arbitrary")),
    )(zb, g, up, dy2b, wg, wu, wd)


def _mid_bwd(dzp, dh2, h1, y1, mixb, o_raw, oh_raw, xph, wout, w_fpre, w_post, w_mla, w_hg, swap=(), tt=256):
    T = dh2.shape[0]
    nsw = len(swap)
    n_in, n_out = 13, 10

    def body(*refs):
        (dzp_ref, dh2_ref, h1_ref, y1_ref, mix_ref, o_ref, oh_ref, hg_ref, wout_ref, wfpre_ref, wpost_ref,
         wmla_ref, whg_ref) = refs[:n_in]
        (dh1_ref, dwout_ref, do_ref, doh_ref, dhg_ref, dvec_ref, dwfpre_ref, dwpost_ref, dwmla_ref,
         dwhg_ref) = refs[n_in + nsw:n_in + nsw + n_out]
        swap_copies = lambda: _pair_swap_copies(refs[n_in:n_in + nsw], refs[n_in + nsw + n_out:n_in + 2 * nsw + n_out],
                                                *refs[n_in + 2 * nsw + n_out:])

        @pl.when(pl.program_id(0) == 0)
        def _():
            for r in (dwout_ref, dwfpre_ref, dwpost_ref, dwmla_ref, dwhg_ref):
                r[...] = jnp.zeros_like(r)
            for cp in (swap_copies() if nsw else ()):
                cp.start()

        dz = dzp_ref[0] + dzp_ref[1] + dzp_ref[2] + dzp_ref[3]
        wfpre = wfpre_ref[...]
        _, h1n, r = _rms_fwd(h1_ref[...], wfpre)
        dh1_z, dwfpre = _rms_bwd(dz, h1n, r, wfpre)
        dwfpre_ref[...] += dwfpre
        dh1 = dh2_ref[...] + dh1_z
        dh1_ref[...] = dh1
        wpost = wpost_ref[...]
        _, y1n, r1 = _rms_fwd(y1_ref[...], wpost)
        dy1, dwpost = _rms_bwd(dh1, y1n, r1, wpost)
        dwpost_ref[...] += dwpost
        dmix = _mm_nt(dy1, wout_ref[...])
        dwout_ref[...] += _mm_tn(mix_ref[...], dy1)
        wmla = wmla_ref[...]
        o = o_ref[...]
        _, on, ro = _grms_fwd(o, wmla, MLA_V)
        d_o, dwmla = _grms_bwd(dmix[:, :MLA_WIDTH], on, ro, wmla, MLA_V)
        dwmla_ref[...] += dwmla
        do_ref[...] = d_o.astype(do_ref.dtype)
        hh = lax.broadcasted_iota(jnp.int32, (MLA_HEADS, MLA_WIDTH), 0)
        ll = lax.broadcasted_iota(jnp.int32, (MLA_HEADS, MLA_WIDTH), 1)
        sel = jnp.where((ll >= hh * MLA_V) & (ll < (hh + 1) * MLA_V), 1.0, 0.0)
        dvec_ref[...] = _mm_nt(sel, d_o * o, True)
        whg = whg_ref[...]
        hg = hg_ref[...]
        sg = jax.nn.sigmoid(hg)
        _, ohn, rh = _grms_fwd(oh_ref[...], whg, HGRN_DIM)
        dmh = dmix[:, MLA_WIDTH:]
        dhg_ref[...] = dmh * ohn * whg * sg * (1.0 + hg * (1.0 - sg))
        d_oh, dwhg = _grms_bwd(dmh * (hg * sg), ohn, rh, whg, HGRN_DIM)
        dwhg_ref[...] += dwhg
        doh_ref[...] = d_oh

        if nsw:
            @pl.when(pl.program_id(0) == T // tt - 1)
            def _():
                for cp in swap_copies():
                    cp.wait()

    row = lambda w: pl.BlockSpec((tt, w), lambda i: (i, 0))
    full = lambda a: pl.BlockSpec(a.shape, lambda i: (0,) * a.ndim)
    vec = lambda w: pl.BlockSpec((1, w), lambda i: (0, 0))
    sds = jax.ShapeDtypeStruct
    return pl.pallas_call(
        body, name="mid_bwd", grid=(T // tt,),
        in_specs=[pl.BlockSpec((N_CHIPS, tt, D_MODEL), lambda i: (0, i, 0)), row(D_MODEL), row(D_MODEL), row(D_MODEL),
                  row(D_MODEL), row(MLA_WIDTH), row(HGRN_WIDTH), pl.BlockSpec((tt, HGRN_WIDTH), lambda i: (i, 3)),
                  full(wout), vec(D_MODEL), vec(D_MODEL), vec(MLA_WIDTH), vec(HGRN_WIDTH)] + [ANY] * nsw,
        out_specs=[row(D_MODEL), full(wout), row(MLA_WIDTH), row(HGRN_WIDTH), row(HGRN_WIDTH),
                   pl.BlockSpec((MLA_HEADS, tt), lambda i: (0, i)),
                   vec(D_MODEL), vec(D_MODEL), vec(MLA_WIDTH), vec(HGRN_WIDTH)] + [ANY] * nsw,
        out_shape=[sds((T, D_MODEL), F32), sds(wout.shape, F32), sds((T, MLA_WIDTH), MXU_DTYPE), sds((T, HGRN_WIDTH), F32),
                   sds((T, HGRN_WIDTH), F32), sds((MLA_HEADS, T), F32),
                   sds((1, D_MODEL), F32), sds((1, D_MODEL), F32), sds((1, MLA_WIDTH), F32), sds((1, HGRN_WIDTH), F32)]
        + _half_stack_shapes(swap),
        scratch_shapes=[pltpu.SemaphoreType.DMA((nsw,)), pltpu.SemaphoreType.DMA((nsw,))] if nsw else [],
        compiler_params=_params(("arbitrary",)),
    )(dzp, dh2, h1, y1, mixb, o_raw, oh_raw, xph, wout, w_fpre, w_post, w_mla, w_hg, *swap)


def _in_bwd(x, dh1, cq, ckv, dq, dk, dv, dhq, dhf, dhi, dhg, rc, rs, w_pre, win, qnw, wq, kvnw, wk, wv, tt=256):
    T = x.shape[0]

    def body(x_ref, dh1_ref, cq_ref, ckv_ref, dq_ref, dk_ref, dv_ref, dhq_ref, dhf_ref, dhi_ref, dhg_ref, rc_ref, rs_ref,
             wpre_ref, win_ref, qnw_ref, wq_ref, kvnw_ref, wk_ref, wv_ref,
             dx_ref, dwin_ref, dwq_ref, dwk_ref, dwv_ref, dwpre_ref, dqnw_ref, dkvnw_ref):
        @pl.when(pl.program_id(0) == 0)
        def _():
            for r in (dwin_ref, dwq_ref, dwk_ref, dwv_ref, dwpre_ref, dqnw_ref, dkvnw_ref):
                r[...] = jnp.zeros_like(r)

        def add_win_grad(r, first):
            for arr0, n, chip, row0 in _win_grad_segments():
                if first <= arr0 and arr0 + n <= first + r.shape[0]:
                    dwin_ref[chip, row0:row0 + n, :] += r[arr0 - first:arr0 - first + n]

        lo = Q_RANK + KV_RANK + HEAD_PAD
        wpre = wpre_ref[...]
        u, xn, rx = _rms_fwd(x_ref[...], wpre)
        dxp_h = jnp.concatenate([dhq_ref[...], dhf_ref[...], dhi_ref[...], dhg_ref[...]], axis=-1)
        add_win_grad(_mm_tn(dxp_h, u), lo)
        du = _mm(dxp_h, win_ref[lo:, :])
        c, sa, sb = _rope_tables(rc_ref[...], rs_ref[...])
        lane = lax.broadcasted_iota(jnp.int32, (tt, HEAD_PAD), 1)
        dk_all = dk_ref[...]
        dq_lin = []
        dkr = jnp.zeros((tt, HEAD_PAD), F32)
        for h in range(MLA_HEADS):
            sl = slice(HEAD_PAD * h, HEAD_PAD * (h + 1))
            dq_lin.append(_rope_bwd(dq_ref[sl, :].T, c, sa, sb))
            dkr = dkr + dk_all[:, sl]
        dq_lin = jnp.concatenate(dq_lin, axis=-1)
        dkr = jnp.where((lane >= MLA_NOPE) & (lane < MLA_QK), _rope_bwd(dkr, c, sa, sb), 0.0)
        qnw = qnw_ref[...]
        qn, cqn, rq = _rms_fwd(cq_ref[...], qnw)
        dwq_ref[...] += _mm_tn(qn, dq_lin)
        dcq, dqnw = _rms_bwd(_mm_nt(dq_lin, wq_ref[...]), cqn, rq, qnw)
        dqnw_ref[...] += dqnw
        kvnw = kvnw_ref[...]
        kvn, ckvn, rkv = _rms_fwd(ckv_ref[...], kvnw)
        dv_ = dv_ref[...]
        dwk_ref[...] += _mm_tn(kvn, dk_all)
        dwv_ref[...] += _mm_tn(kvn, dv_)
        dckv, dkvnw = _rms_bwd(_mm_nt(dk_all, wk_ref[...]) + _mm_nt(dv_, wv_ref[...]), ckvn, rkv, kvnw)
        dkvnw_ref[...] += dkvnw
        dxp_a = jnp.concatenate([dcq, dckv, dkr], axis=-1)
        add_win_grad(_mm_tn(dxp_a, u), 0)
        dx_u, dwpre = _rms_bwd(du + _mm(dxp_a, win_ref[:lo, :]), xn, rx, wpre)
        dwpre_ref[...] += dwpre
        dx_ref[...] = dh1_ref[...] + dx_u

    row = lambda w: pl.BlockSpec((tt, w), lambda i: (i, 0))
    full = lambda a: pl.BlockSpec(a.shape, lambda i: (0,) * a.ndim)
    sds = jax.ShapeDtypeStruct
    qk_w = MLA_HEADS * HEAD_PAD
    return pl.pallas_call(
        body, name="in_bwd", grid=(T // tt,),
        in_specs=[row(D_MODEL), row(D_MODEL), row(Q_RANK), row(KV_RANK), pl.BlockSpec((qk_w, tt), lambda i: (0, i)),
                  row(qk_w), row(MLA_WIDTH),
                  row(HGRN_WIDTH), row(HGRN_WIDTH), row(HGRN_WIDTH), row(HGRN_WIDTH), row(HEAD_PAD), row(HEAD_PAD),
                  full(w_pre), full(win), full(qnw), full(wq), full(kvnw), full(wk), full(wv)],
        out_specs=[row(D_MODEL), pl.BlockSpec(WIN_COMM_SHAPE, lambda i: (0, 0, 0)), full(wq), full(wk), full(wv),
                   full(w_pre), full(qnw), full(kvnw)],
        out_shape=[sds((T, D_MODEL), F32), sds(WIN_COMM_SHAPE, F32), sds(wq.shape, F32), sds(wk.shape, F32),
                   sds(wv.shape, F32), sds(w_pre.shape, F32), sds(qnw.shape, F32), sds(kvnw.shape, F32)],
        compiler_params=_params(("arbitrary",)),
    )(x, dh1, cq, ckv, dq, dk, dv, dhq, dhf, dhi, dhg, rc, rs, w_pre, win, qnw, wq, kvnw, wk, wv)


def _arrange_weights(win_t, wuq_full, wukv):
    dt = win_t.dtype
    z = lambda n: jnp.zeros((n, D_MODEL), dt)
    s2 = Q_RANK + KV_RANK
    win_arr = jnp.concatenate([win_t[:s2], z(MLA_NOPE), win_t[s2:s2 + MLA_ROPE], z(HEAD_PAD - MLA_QK),
                               win_t[s2 + MLA_ROPE:]], axis=0)
    wq_arr = jnp.pad(wuq_full, ((0, 0), (0, 0), (0, HEAD_PAD - MLA_QK))).reshape(Q_RANK, MLA_HEADS * HEAD_PAD)
    wk_arr = jnp.pad(wukv[:, :, :MLA_NOPE], ((0, 0), (0, 0), (0, HEAD_PAD - MLA_NOPE))).reshape(
        KV_RANK, MLA_HEADS * HEAD_PAD)
    wv_arr = wukv[:, :, MLA_NOPE:].reshape(KV_RANK, MLA_WIDTH)
    return win_arr, wq_arr, wk_arr, wv_arr


WIN_COMM_SHAPE = (N_CHIPS, FF_SHARD, D_MODEL)


def _win_grad_segments():
    s2 = Q_RANK + KV_RANK
    runs = [(0, s2, 0), (s2, s2 + MLA_ROPE, MLA_NOPE), (s2 + MLA_ROPE, D_IN, HEAD_PAD - MLA_ROPE)]
    per = D_IN // N_CHIPS
    segs = []
    for lo, hi, shift in runs:
        for k in range(N_CHIPS):
            a, b = max(lo, per * k), min(hi, per * (k + 1))
            if a < b:
                segs.append((a + shift, b - a, k, a - per * k))
    return segs


def _unarrange_grads(dwq_arr, dwk_arr, dwv_arr):
    dwuq = dwq_arr.reshape(Q_RANK, MLA_HEADS, HEAD_PAD)[:, :, :MLA_QK]
    dwukv = jnp.concatenate([dwk_arr.reshape(KV_RANK, MLA_HEADS, HEAD_PAD)[:, :, :MLA_NOPE],
                             dwv_arr.reshape(KV_RANK, MLA_HEADS, MLA_V)], axis=-1)
    return dwuq, dwukv


def _rope_inv_freq():
    inv = 1.0 / (ROPE_THETA ** (jnp.arange(0, MLA_ROPE, 2, dtype=F32) / MLA_ROPE))
    z = lambda n: jnp.zeros((n,), F32)
    return jnp.concatenate([z(MLA_NOPE), inv, inv, z(HEAD_PAD - MLA_QK)]).reshape(1, HEAD_PAD)


def _local_step(x, pos, tgt, small, win_arr, wq_arr, wk_arr, wv_arr, late, place=None):
    invf = _rope_inv_freq()
    cq, ckv, xph, qb, kb, vb, kt, vt, rc, rs = _in_fwd(x, pos, invf, small["attn_pre_norm"], win_arr, small["mla_q_norm"],
                                               wq_arr, small["mla_kv_norm"], wk_arr, wv_arr)
    if place is None:
        o_raw, lse = _attn_fwd_t(qb, kb, vt)
        wout, wg, wu, wd = late
    else:
        o_raw, lse, *stacks = _attn_fwd_t(qb, kb, vt, gather=late)
        wout, wg, wu, wd = [lax.dynamic_update_slice(s, l[None], (place[1], 0, 0)) for s, l in zip(stacks, late)]
        wout = wout.reshape(D_MODEL, D_MODEL)
    oh_raw, states = _hgrn_fwd(xph, small["hgrn_lb_logits"])
    h1, y1, zb, mixb = _proj_fwd(x, o_raw, oh_raw, xph, wout, small["mla_out_norm"], small["hgrn_out_norm"],
                                 small["attn_post_norm"], small["ffn_pre_norm"])
    g, up, dy2b, dh2, loss_acc, d_fpost = _ffn_fwd(zb, h1, tgt, small["ffn_post_norm"], wg, wu, wd)
    dwg, dwu, dwd, dzp = _ffn_bwd(zb, g, up, dy2b, wg, wu, wd)
    ffn_grads = [] if place is None else [dwg, dwu, dwd]
    dh1, dwout, d_o, d_oh, dhg, dvec, d_fpre, d_post, d_mla, d_hg, *ffn_rs = _mid_bwd(
        dzp, dh2, h1, y1, mixb, o_raw, oh_raw, xph, wout, small["ffn_pre_norm"], small["attn_post_norm"],
        small["mla_out_norm"], small["hgrn_out_norm"], swap=ffn_grads)
    ffn_ps = _pair_sum(place, ffn_grads, ffn_rs, name="pair_sum_ffn") if ffn_grads else []
    dq, dk, dv, *ffn_ris = _attn_bwd_t(qb, kb, kt, vb, d_o, lse, dvec.reshape(lse.shape), send=ffn_ps)
    dhq, dhf, dhi, d_lbl = _hgrn_bwd(xph, small["hgrn_lb_logits"], states, d_oh)
    dx, dwin4, dwq_arr, dwk_arr, dwv_arr, d_pre, d_qn, d_kvn = _in_bwd(
        x, dh1, cq, ckv, dq, dk, dv, dhq, dhf, dhi, dhg, rc, rs, small["attn_pre_norm"], win_arr,
        small["mla_q_norm"], wq_arr, small["mla_kv_norm"], wk_arr, wv_arr)
    dwuq, dwukv = _unarrange_grads(dwq_arr, dwk_arr, dwv_arr)
    loss = 0.5 * jnp.sum(loss_acc) * (1.0 / D_MODEL)
    grads = dict(attn_pre_norm=d_pre, w_in=dwin4, mla_q_norm=d_qn, mla_w_uq=dwuq, mla_kv_norm=d_kvn, mla_w_ukv=dwukv,
                 mla_out_norm=d_mla, hgrn_lb_logits=d_lbl, hgrn_out_norm=d_hg, w_out=dwout, attn_post_norm=d_post,
                 ffn_pre_norm=d_fpre, w_gate=dwg, w_up=dwu, w_down=dwd, ffn_post_norm=d_fpost)
    if place is None:
        return loss, dx, grads
    return loss, dx, grads, (ffn_rs, ffn_ris)


def _place():
    x, y, c = lax.axis_index("x"), lax.axis_index("y"), lax.axis_index("c")
    others = [(1 - x, y), (x, 1 - y), (1 - x, 1 - y)]
    return x, y, c, 2 * x + y, (x, y, 1 - c), others


def _half(ref, c, rows):
    return ref.at[pl.ds(pl.multiple_of(c * rows, 8), rows)]


def _rcopy(src, dst, send, recv, k, to):
    return pltpu.make_async_remote_copy(src_ref=src, dst_ref=dst, send_sem=send.at[k], recv_sem=recv.at[k],
                                        device_id=to, device_id_type=MESH)


class _Gather:
    def __init__(self, ins, outs, send, recv):
        self.ins, self.outs, self.send, self.recv = ins, outs, send, recv
        self.n = len(ins)
        self.halves = [r.shape[0] // 2 for r in ins]
        _, _, self.c, self.me, self.sib, self.others = _place()

    def _each(self):
        for j, (px, py) in enumerate(self.others):
            for a in range(self.n):
                yield j * self.n + a, a, 2 * px + py, (px, py, self.c)

    def sends(self):
        return [_rcopy(_half(self.ins[a], self.c, self.halves[a]), _half(self.outs[a].at[self.me], self.c, self.halves[a]),
                       self.send, self.recv, k, to) for k, a, _, to in self._each()]

    def arrivals(self):
        parts = [(k, _half(self.outs[a].at[chip], self.c, self.halves[a]), to) for k, a, chip, to in self._each()]
        return [_rcopy(p, p, self.send, self.recv, k, to) for k, p, to in parts]

    def forwards(self):
        parts = [(k, _half(self.outs[a].at[chip], self.c, self.halves[a])) for k, a, chip, _ in self._each()]
        return [_rcopy(p, p, self.send, self.recv, 3 * self.n + k, self.sib) for k, p in parts]

    def forward_arrivals(self):
        parts = [(k, _half(self.outs[a].at[chip], 1 - self.c, self.halves[a])) for k, a, chip, _ in self._each()]
        return [_rcopy(p, p, self.send, self.recv, 3 * self.n + k, self.sib) for k, p in parts]

    @staticmethod
    def out_shapes(arrs):
        return [jax.ShapeDtypeStruct((N_CHIPS,) + a.shape, a.dtype) for a in arrs]

    @staticmethod
    def semaphores(arrs):
        return [pltpu.SemaphoreType.DMA((6 * len(arrs),)), pltpu.SemaphoreType.DMA((6 * len(arrs),))]


def _gather_chips(arrs, name):
    n = len(arrs)

    def body(*refs):
        gat = _Gather(refs[:n], refs[n:2 * n], *refs[2 * n:])
        sends, forwards = gat.sends(), gat.forwards()
        for cp in sends:
            cp.start()
        for arrival, fw in zip(gat.arrivals(), forwards):
            arrival.wait_recv()
            fw.start()
        for arrival in gat.forward_arrivals():
            arrival.wait_recv()
        for cp in sends + forwards:
            cp.wait_send()

    return pl.pallas_call(body, name=name, in_specs=[ANY] * n, out_specs=[ANY] * n, out_shape=_Gather.out_shapes(arrs),
                          scratch_shapes=_Gather.semaphores(arrs))(*arrs)


GRAD_BLOCKS = 2


def _pair_swap_copies(g_refs, r_refs, send, recv):
    _, _, c, _, sib, _ = _place()
    copies = []
    for a, (g, r) in enumerate(zip(g_refs, r_refs)):
        h = g.shape[1] // 2
        copies.append(_rcopy(g.at[:, pl.ds(pl.multiple_of((1 - c) * h, 8), h)], r, send, recv, a, sib))
    return copies


def _half_stack_shapes(gs, dtype=None):
    return [jax.ShapeDtypeStruct((N_CHIPS, g.shape[1] // 2, g.shape[2]), dtype or g.dtype) for g in gs]


def _pair_swap(gs, sm):
    n = len(gs)

    def body(*refs):
        g_refs, sm_ref = refs[:n], refs[n]
        r_refs, ssib_ref = refs[n + 1:2 * n + 1], refs[2 * n + 1]
        send, recv = refs[2 * n + 2:]
        copies = _pair_swap_copies(g_refs, r_refs, send, recv)
        copies.append(_rcopy(sm_ref, ssib_ref, send, recv, n, _place()[4]))
        for cp in copies:
            cp.start()
        for cp in copies:
            cp.wait()

    return pl.pallas_call(
        body, name="pair_swap", in_specs=[ANY] * (n + 1), out_specs=[ANY] * (n + 1),
        out_shape=_half_stack_shapes(gs) + [jax.ShapeDtypeStruct(sm.shape, sm.dtype)],
        scratch_shapes=[pltpu.SemaphoreType.DMA((n + 1,)), pltpu.SemaphoreType.DMA((n + 1,))],
    )(*gs, sm)


def _pair_sum(place, gs, rs, small=None, name="pair_sum"):
    n = len(gs)
    nb = GRAD_BLOCKS

    def body(place_ref, *refs):
        g_refs, r_refs, p_refs = refs[:n], refs[n:2 * n], refs[-n - 1:-1] if small else refs[-n:]
        for a in range(n):
            p_refs[a][0] = (g_refs[a][0] + r_refs[a][0]).astype(p_refs[a].dtype)
        if small:
            @pl.when((pl.program_id(0) == 0) & (pl.program_id(1) == 0))
            def _():
                refs[-1][...] = refs[2 * n][...] + refs[2 * n + 1][...]

    in_specs, out_specs = [], []
    for g in gs:
        blk = (1, g.shape[1] // 2 // nb, g.shape[2])
        in_specs.append(pl.BlockSpec(blk, lambda i, k, p: (k, p[0] * nb + i, 0)))
    for g in gs:
        blk = (1, g.shape[1] // 2 // nb, g.shape[2])
        in_specs.append(pl.BlockSpec(blk, lambda i, k, p: (k, i, 0)))
        out_specs.append(pl.BlockSpec(blk, lambda i, k, p: (k, i, 0)))
    out_shape = _half_stack_shapes(gs, BF16)
    if small:
        sm_spec = pl.BlockSpec(small[0].shape, lambda i, k, p: (0, 0))
        in_specs += [sm_spec, sm_spec]
        out_specs.append(sm_spec)
        out_shape.append(jax.ShapeDtypeStruct(small[0].shape, F32))
    return pl.pallas_call(
        body, name=name,
        grid_spec=pltpu.PrefetchScalarGridSpec(num_scalar_prefetch=1, grid=(nb, N_CHIPS), in_specs=in_specs,
                                               out_specs=out_specs),
        out_shape=out_shape,
        compiler_params=_params(("arbitrary", "arbitrary")),
    )(place, *gs, *rs, *(small or ()))


def _chip_swap_copies(p_refs, ri_refs, send, recv):
    _, _, c, _, _, others = _place()
    n = len(p_refs)
    return [_rcopy(p_refs[a].at[2 * px + py], ri_refs[a].at[j], send, recv, j * n + a, (px, py, c))
            for j, (px, py) in enumerate(others) for a in range(n)]


def _chip_swap_shapes(ps):
    return [jax.ShapeDtypeStruct((3,) + p.shape[1:], p.dtype) for p in ps]


def _chip_swap(ps, pair):
    n = len(ps)
    hs = SMALL_ROWS // 2

    def body(*refs):
        p_refs, pair_ref = refs[:n], refs[n]
        ri_refs, sm4_ref = refs[n + 1:2 * n + 1], refs[2 * n + 1]
        send, recv, lsem = refs[2 * n + 2:]
        x, y, c, me, sib, others = _place()
        local = pltpu.make_async_copy(pair_ref, sm4_ref.at[me], lsem.at[0])
        local.start()
        copies = _chip_swap_copies(p_refs, ri_refs, send, recv)
        arrivals = list(copies)
        for j, (px, py) in enumerate(others):
            copies.append(_rcopy(_half(pair_ref, c, hs), _half(sm4_ref.at[me], c, hs), send, recv, 3 * n + j, (px, py, c)))
            part = _half(sm4_ref.at[2 * px + py], c, hs)
            arrivals.append(_rcopy(part, part, send, recv, 3 * n + j, (px, py, c)))
        for cp in copies:
            cp.start()
        for arrival in arrivals:
            arrival.wait_recv()
        for cp in copies:
            cp.wait_send()
        local.wait()

    k = 3 * (n + 1)
    return pl.pallas_call(
        body, name="chip_swap", in_specs=[ANY] * (n + 1), out_specs=[ANY] * (n + 1),
        out_shape=_chip_swap_shapes(ps) + [jax.ShapeDtypeStruct((N_CHIPS,) + pair.shape, pair.dtype)],
        scratch_shapes=[pltpu.SemaphoreType.DMA((k,)), pltpu.SemaphoreType.DMA((k,)), pltpu.SemaphoreType.DMA((1,))],
    )(*ps, pair)


def _chip_sum(place, gs, rs, ris):
    n = len(gs)
    nb = GRAD_BLOCKS

    def body(place_ref, *refs):
        g_refs, r_refs, ri_refs, o_refs = refs[:n], refs[n:2 * n], refs[2 * n:3 * n], refs[3 * n:]
        for a in range(n):
            ri = ri_refs[a]
            o_refs[a][...] = (g_refs[a][0] + r_refs[a][0]) + ri[0].astype(F32) + ri[1].astype(F32) + ri[2].astype(F32)

    in_specs, out_specs, out_shape = [], [], []
    for g in gs:
        blk = (1, g.shape[1] // 2 // nb, g.shape[2])
        in_specs.append(pl.BlockSpec(blk, lambda i, p: (p[1], p[0] * nb + i, 0)))
    for g in gs:
        blk = (1, g.shape[1] // 2 // nb, g.shape[2])
        in_specs.append(pl.BlockSpec(blk, lambda i, p: (p[1], i, 0)))
    for g in gs:
        rb = g.shape[1] // 2 // nb
        in_specs.append(pl.BlockSpec((3, rb, g.shape[2]), lambda i, p: (0, i, 0)))
        out_specs.append(pl.BlockSpec((rb, g.shape[2]), lambda i, p: (p[0] * nb + i, 0)))
        out_shape.append(jax.ShapeDtypeStruct(g.shape[1:], F32))
    return pl.pallas_call(
        body, name="chip_sum",
        grid_spec=pltpu.PrefetchScalarGridSpec(num_scalar_prefetch=1, grid=(nb,), in_specs=in_specs, out_specs=out_specs),
        out_shape=out_shape,
        compiler_params=_params(("arbitrary",)),
    )(place, *gs, *rs, *ris)


def _pair_fill(gfs, sm4):
    n = len(gfs)
    hs = SMALL_ROWS // 2

    def body(*refs):
        g_refs, sm4_ref = refs[n + 1:2 * n + 1], refs[2 * n + 1]
        send, recv = refs[2 * n + 2:]
        x, y, c, me, sib, others = _place()
        copies, waits = [], []
        for a in range(n):
            h = gfs[a].shape[0] // 2
            mine, theirs = _half(g_refs[a], c, h), _half(g_refs[a], 1 - c, h)
            copies.append(pltpu.make_async_remote_copy(src_ref=mine, dst_ref=mine, send_sem=send.at[a],
                                                       recv_sem=recv.at[a], device_id=sib, device_id_type=MESH))
            waits.append(pltpu.make_async_remote_copy(src_ref=theirs, dst_ref=theirs, send_sem=send.at[a],
                                                      recv_sem=recv.at[a], device_id=sib, device_id_type=MESH))
        for j, (px, py) in enumerate(others):
            chip = 2 * px + py
            mine, theirs = _half(sm4_ref.at[chip], c, hs), _half(sm4_ref.at[chip], 1 - c, hs)
            copies.append(pltpu.make_async_remote_copy(src_ref=mine, dst_ref=mine, send_sem=send.at[n + j],
                                                       recv_sem=recv.at[n + j], device_id=sib, device_id_type=MESH))
            waits.append(pltpu.make_async_remote_copy(src_ref=theirs, dst_ref=theirs, send_sem=send.at[n + j],
                                                      recv_sem=recv.at[n + j], device_id=sib, device_id_type=MESH))
        for cp in copies:
            cp.start()
        for w in waits:
            w.wait_recv()
        for cp in copies:
            cp.wait_send()

    return pl.pallas_call(
        body, name="pair_fill", in_specs=[ANY] * (n + 1), out_specs=[ANY] * (n + 1),
        out_shape=[jax.ShapeDtypeStruct(g.shape, g.dtype) for g in gfs] + [jax.ShapeDtypeStruct(sm4.shape, sm4.dtype)],
        input_output_aliases={i: i for i in range(n + 1)},
        scratch_shapes=[pltpu.SemaphoreType.DMA((n + 3,)), pltpu.SemaphoreType.DMA((n + 3,))],
    )(*gfs, sm4)


def _adamw_math(w, g, m, v):
    m = ADAM_B1 * m + (1.0 - ADAM_B1) * g
    v = ADAM_B2 * v + (1.0 - ADAM_B2) * (g * g)
    m_hat = m / (1.0 - ADAM_B1 ** ADAM_STEP)
    v_hat = v / (1.0 - ADAM_B2 ** ADAM_STEP)
    return -ADAM_LR * (m_hat / (jnp.sqrt(v_hat) + ADAM_EPS) + ADAM_WD * w), m, v


def _adamw(items, steps, name):
    n = len(items)

    def body(*refs):
        for a in range(n):
            d, mo, vo = _adamw_math(*(r[...] for r in refs[4 * a:4 * a + 4]))
            for out, val in zip(refs[4 * n + 3 * a:4 * n + 3 * a + 3], (d, mo, vo)):
                out[...] = val

    spec = lambda w: pl.BlockSpec((w.shape[0] // steps, w.shape[1]), lambda i: (i, 0))
    flat = pl.pallas_call(
        body, name=name, grid=(steps,), in_specs=[spec(it[0]) for it in items for _ in range(4)],
        out_specs=[spec(it[0]) for it in items for _ in range(3)],
        out_shape=[jax.ShapeDtypeStruct(it[0].shape, F32) for it in items for _ in range(3)],
        compiler_params=_params(("arbitrary",)),
    )(*[a for it in items for a in it])
    return [flat[3 * a:3 * a + 3] for a in range(n)]


def _adamw_small(sm4, w, m, v):
    def body(sm4_ref, w_ref, m_ref, v_ref, g_ref, d_ref, mo_ref, vo_ref):
        g = ((sm4_ref[0] + sm4_ref[1]) + sm4_ref[2]) + sm4_ref[3]
        g_ref[...] = g
        d, mo, vo = _adamw_math(w_ref[...], g, m_ref[...], v_ref[...])
        d_ref[...] = d
        mo_ref[...] = mo
        vo_ref[...] = vo

    return pl.pallas_call(
        body, name="adamw_small", out_shape=[jax.ShapeDtypeStruct(w.shape, F32)] * 4,
        compiler_params=pltpu.CompilerParams(vmem_limit_bytes=VMEM_LIMIT),
    )(sm4, w, m, v)


SMALL_NAMES = ("attn_pre_norm", "mla_q_norm", "mla_kv_norm", "mla_w_ukv", "mla_out_norm", "hgrn_lb_logits",
               "hgrn_out_norm", "attn_post_norm", "ffn_pre_norm", "ffn_post_norm")
BIG_NAMES = ("w_in", "mla_w_uq", "w_out", "w_gate", "w_up", "w_down")
WEIGHT_NAMES = ("attn_pre_norm", "w_in", "mla_q_norm", "mla_w_uq", "mla_kv_norm", "mla_w_ukv", "mla_out_norm",
                "hgrn_lb_logits", "hgrn_out_norm", "w_out", "attn_post_norm", "ffn_pre_norm", "w_gate", "w_up", "w_down",
                "ffn_post_norm")


UQ_COMM_SHAPE = (192, 384)


def _pack_small(vals, extra=None):
    parts = [vals[n].reshape(-1) for n in SMALL_NAMES]
    if extra is not None:
        parts.append(extra.reshape(1))
    flat = jnp.concatenate(parts)
    return jnp.pad(flat, (0, SMALL_ROWS * D_MODEL - flat.shape[0])).reshape(SMALL_ROWS, D_MODEL)


def _unpack_small(buf, shapes):
    flat = buf.reshape(-1)
    out, off = {}, 0
    for n, size in zip(SMALL_NAMES, SMALL_SIZES):
        out[n] = flat[off:off + size].reshape(shapes[n])
        off += size
    return out


def kernel(x, positions, attn_pre_norm, w_in, mla_q_norm, mla_w_uq, mla_kv_norm, mla_w_ukv, mla_out_norm, hgrn_lb_logits, hgrn_out_norm, w_out, attn_post_norm, ffn_pre_norm, w_gate, w_up, w_down, ffn_post_norm, loss_target, m_attn_pre_norm, m_w_in, m_mla_q_norm, m_mla_w_uq, m_mla_kv_norm, m_mla_w_ukv, m_mla_out_norm, m_hgrn_lb_logits, m_hgrn_out_norm, m_w_out, m_attn_post_norm, m_ffn_pre_norm, m_w_gate, m_w_up, m_w_down, m_ffn_post_norm, v_attn_pre_norm, v_w_in, v_mla_q_norm, v_mla_w_uq, v_mla_kv_norm, v_mla_w_ukv, v_mla_out_norm, v_hgrn_lb_logits, v_hgrn_out_norm, v_w_out, v_attn_post_norm, v_ffn_pre_norm, v_w_gate, v_w_up, v_w_down, v_ffn_post_norm):
    args = locals()
    W = {n: args[n] for n in WEIGHT_NAMES}
    M = {n: args["m_" + n] for n in WEIGHT_NAMES}
    V = {n: args["v_" + n] for n in WEIGHT_NAMES}
    T = x.shape[1]
    cx, cy, cc = lax.axis_index("x"), lax.axis_index("y"), lax.axis_index("c")

    win_rows = D_IN // N_CHIPS
    shard2d = {"w_in": (win_rows, D_MODEL), "mla_w_uq": (Q_RANK // N_CHIPS, MLA_HEADS * MLA_QK),
               "w_out": (D_MODEL // N_CHIPS, D_MODEL), "w_gate": (FF_SHARD, D_MODEL), "w_up": (FF_SHARD, D_MODEL),
               "w_down": (FF_SHARD, D_MODEL)}
    transposed = ("w_in", "w_gate", "w_up")
    to2d = lambda n, a: a[0].T if n in transposed else a.reshape(shard2d[n])
    from2d = lambda n, t: t.T[None] if n in transposed else t.reshape(W[n].shape)
    me = 2 * cx + cy
    place = jnp.stack([cc, me]).astype(jnp.int32)
    local_b = [to2d(n, W[n]).astype(BF16) for n in BIG_NAMES]
    local_b[0] = jnp.pad(local_b[0], ((0, FF_SHARD - win_rows), (0, 0)))
    stacks = _gather_chips(local_b[:2], "gather_weights")
    win4, wuq4 = [lax.dynamic_update_slice(s, l[None], (me, 0, 0)) for s, l in zip(stacks, local_b)]
    win_t = win4[:, :win_rows].reshape(D_IN, D_MODEL)
    wuq_full = wuq4.reshape(Q_RANK, MLA_HEADS, MLA_QK)
    win_arr, wq_arr, wk_arr, wv_arr = _arrange_weights(win_t, wuq_full, mla_w_ukv[0].astype(BF16))
    small = {n: W[n][0] if n == "mla_w_ukv" else W[n].reshape(-1, W[n].shape[-1]) for n in SMALL_NAMES}

    loss_local, dx, grads, (ffn_rs, ffn_ris) = _local_step(x[0], positions.reshape(T, 1), loss_target[0], small, win_arr,
                                                           wq_arr, wk_arr, wv_arr, local_b[2:], place)

    gs = [grads["w_in"], grads["mla_w_uq"].reshape((N_CHIPS,) + UQ_COMM_SHAPE), grads["w_out"].reshape((N_CHIPS,) + shard2d["w_out"])]
    ffn_gs = [grads["w_gate"], grads["w_up"], grads["w_down"]]
    sm = _pack_small(grads, loss_local)
    *rs, ssib = _pair_swap(gs, sm)
    *ps, pair = _pair_sum(place, gs, rs, small=(sm, ssib))
    *ris, sm4 = _chip_swap(ps, pair)
    gfs = _chip_sum(place, gs + ffn_gs, rs + ffn_rs, ris + ffn_ris)
    *gfin, smf = _pair_fill(gfs, sm4)

    G, DW, NM, NV = {}, {}, {}, {}
    g2d = {n: gfin[k].reshape((-1,) + shard2d[n][1:]) for k, n in enumerate(BIG_NAMES)}
    for names_, steps in ((("w_in", "mla_w_uq"), 3), (("w_out", "w_gate", "w_up", "w_down"), 8)):
        res = _adamw([(to2d(n, W[n]), g2d[n], to2d(n, M[n]), to2d(n, V[n])) for n in names_], steps, "adamw_" + names_[0])
        for n, (d, mo, vo) in zip(names_, res):
            G[n] = from2d(n, g2d[n][:shard2d[n][0]])
            DW[n], NM[n], NV[n] = (from2d(n, t) for t in (d, mo, vo))
    gs_buf, ds, ms, vs = _adamw_small(smf, _pack_small(W), _pack_small(M), _pack_small(V))
    shapes = {n: W[n].shape for n in SMALL_NAMES}
    for dst, buf in ((G, gs_buf), (DW, ds), (NM, ms), (NV, vs)):
        dst.update(_unpack_small(buf, shapes))

    loss = gs_buf.reshape(-1)[sum(SMALL_SIZES)]
    return (loss, dx[None], *[G[n] for n in WEIGHT_NAMES], *[DW[n] for n in WEIGHT_NAMES],
            *[NM[n] for n in WEIGHT_NAMES], *[NV[n] for n in WEIGHT_NAMES])
```

```python
import jax
import jax.numpy as jnp
from jax import lax
from jax.experimental import pallas as pl
from jax.experimental.pallas import tpu as pltpu

F32 = jnp.float32
BF16 = jnp.bfloat16
MXU_DTYPE = BF16

D_MODEL = 1024
MLA_HEADS = 8
MLA_NOPE = 64
MLA_ROPE = 32
MLA_V = 64
MLA_QK = MLA_NOPE + MLA_ROPE
Q_RANK = 384
KV_RANK = 128
MLA_WIDTH = MLA_HEADS * MLA_V
HEAD_PAD = 128
HGRN_HEADS = 4
HGRN_DIM = 128
HGRN_WIDTH = HGRN_HEADS * HGRN_DIM
CHUNK = 64
SUB = 16
HGRN_CPI = 2
D_IN = Q_RANK + KV_RANK + MLA_ROPE + 4 * HGRN_WIDTH
D_IN_ARR = Q_RANK + KV_RANK + HEAD_PAD + 4 * HGRN_WIDTH
D_FF = 2816
N_CHIPS = 4
FF_SHARD = D_FF // N_CHIPS
EPS = 1e-6
ROPE_THETA = 10000.0
ATTN_SCALE = MLA_QK ** -0.5
ATTN_SCALE_LOG2 = ATTN_SCALE * 1.4426950408889634
NEG_BIG = -1e30

ADAM_LR = 0.001
ADAM_B1 = 0.9
ADAM_B2 = 0.999
ADAM_EPS = 1e-08
ADAM_WD = 0.01
ADAM_STEP = 10

VMEM_LIMIT = 56 * 1024 * 1024

SMALL_ROWS = 144
SMALL_SIZES = (1024, 384, 128, 131072, 512, 1024, 512, 1024, 1024, 1024)

MESH = pl.DeviceIdType.MESH
ANY = pl.BlockSpec(memory_space=pl.ANY)


def _dot(a, b, dims, exact):
    if exact:
        return lax.dot_general(a.astype(F32), b.astype(F32), (dims, ((), ())), precision=lax.Precision.HIGH,
                               preferred_element_type=F32)
    return lax.dot_general(a.astype(MXU_DTYPE), b.astype(MXU_DTYPE), (dims, ((), ())), preferred_element_type=F32)


def _mm(a, b, exact=False):
    return _dot(a, b, ((1,), (0,)), exact)


def _mm_nt(a, b, exact=False):
    return _dot(a, b, ((1,), (1,)), exact)


def _mm_tn(a, b, exact=False):
    return _dot(a, b, ((0,), (0,)), exact)


def _rms_fwd(x, w):
    r = lax.rsqrt(jnp.mean(x * x, axis=-1, keepdims=True) + EPS)
    xn = x * r
    return xn * w, xn, r


def _rms_bwd(dy, xn, r, w):
    dxn = dy * w
    dx = r * (dxn - xn * jnp.mean(dxn * xn, axis=-1, keepdims=True))
    dw = jnp.sum(dy * xn, axis=0, keepdims=True)
    return dx, dw


def _group_sums(v, gs):
    t, n = v.shape
    lane = lax.broadcasted_iota(jnp.int32, (t, 128), 1)
    out = []
    for p in range(n // 128):
        vb = v[:, 128 * p:128 * (p + 1)]
        if gs == 128:
            out.append(jnp.sum(vb, axis=-1, keepdims=True))
        else:
            out.append(jnp.sum(jnp.where(lane < 64, vb, 0.0), axis=-1, keepdims=True))
            out.append(jnp.sum(jnp.where(lane >= 64, vb, 0.0), axis=-1, keepdims=True))
    return out


def _group_bcast(sums, gs, t):
    lane = lax.broadcasted_iota(jnp.int32, (t, 128), 1)
    if gs == 128:
        return jnp.concatenate([jnp.broadcast_to(s, (t, 128)) for s in sums], axis=-1)
    return jnp.concatenate([jnp.where(lane < 64, sums[2 * p], sums[2 * p + 1]) for p in range(len(sums) // 2)],
                           axis=-1)


def _grms_fwd(x, w, gs):
    t = x.shape[0]
    r = lax.rsqrt(_group_bcast(_group_sums(x * x, gs), gs, t) * (1.0 / gs) + EPS)
    xn = x * r
    return xn * w, xn, r


def _grms_bwd(dy, xn, r, w, gs):
    t = dy.shape[0]
    dxn = dy * w
    dx = r * (dxn - xn * (_group_bcast(_group_sums(dxn * xn, gs), gs, t) * (1.0 / gs)))
    dw = jnp.sum(dy * xn, axis=0, keepdims=True)
    return dx, dw


def _rope_tables(c_tab, s_tab):
    lane = lax.broadcasted_iota(jnp.int32, c_tab.shape, 1)
    first = (lane >= MLA_NOPE) & (lane < MLA_NOPE + MLA_ROPE // 2)
    second = (lane >= MLA_NOPE + MLA_ROPE // 2) & (lane < MLA_QK)
    return c_tab, jnp.where(first, -s_tab, 0.0), jnp.where(second, s_tab, 0.0)


def _rope(v, c, sa, sb):
    return v * c + pltpu.roll(v, HEAD_PAD - MLA_ROPE // 2, 1) * sa + pltpu.roll(v, MLA_ROPE // 2, 1) * sb


def _rope_bwd(d, c, sa, sb):
    return d * c - pltpu.roll(d, HEAD_PAD - MLA_ROPE // 2, 1) * sa - pltpu.roll(d, MLA_ROPE // 2, 1) * sb


def _params(sem, vmem=VMEM_LIMIT):
    return pltpu.CompilerParams(dimension_semantics=sem, vmem_limit_bytes=vmem)


def _in_fwd(x, pos, invf, w_pre, win, qnw, wq, kvnw, wk, wv, tt=256):
    T = x.shape[0]

    def body(x_ref, pos_ref, invf_ref, wpre_ref, win_ref, qnw_ref, wq_ref, kvnw_ref, wk_ref, wv_ref,
             cq_ref, ckv_ref, xph_ref, q_ref, k_ref, v_ref, kt_ref, vt_ref, rc_ref, rs_ref):
        u, _, _ = _rms_fwd(x_ref[...], wpre_ref[...])
        lo = Q_RANK + KV_RANK + HEAD_PAD
        xp = _mm_nt(u, win_ref[:lo, :])
        xph_ref[...] = _mm_nt(u, win_ref[lo:, :])
        cq = xp[:, :Q_RANK]
        ckv = xp[:, Q_RANK:Q_RANK + KV_RANK]
        kr = xp[:, Q_RANK + KV_RANK:]
        cq_ref[...] = cq
        ckv_ref[...] = ckv
        ang = pos_ref[...].astype(F32) * invf_ref[...]
        c_tab = jnp.cos(ang)
        s_tab = jnp.sin(ang)
        rc_ref[...] = c_tab
        rs_ref[...] = s_tab
        c, sa, sb = _rope_tables(c_tab, s_tab)
        qn, _, _ = _rms_fwd(cq, qnw_ref[...])
        q = _mm(qn, wq_ref[...])
        kvn, _, _ = _rms_fwd(ckv, kvnw_ref[...])
        kn = _mm(kvn, wk_ref[...])
        v = _mm(kvn, wv_ref[...])
        v_ref[...] = v.astype(v_ref.dtype)
        vt_ref[...] = v.T.astype(vt_ref.dtype)
        krr = _rope(kr, c, sa, sb)
        for h in range(MLA_HEADS):
            sl = slice(HEAD_PAD * h, HEAD_PAD * (h + 1))
            q_ref[:, sl] = _rope(q[:, sl], c, sa, sb).astype(q_ref.dtype)
            kh = kn[:, sl] + krr
            k_ref[:, sl] = kh.astype(k_ref.dtype)
            kt_ref[sl, :] = kh.T.astype(kt_ref.dtype)

    row = lambda w: pl.BlockSpec((tt, w), lambda i: (i, 0))
    full = lambda a: pl.BlockSpec(a.shape, lambda i: (0,) * a.ndim)
    qk_w = MLA_HEADS * HEAD_PAD
    return pl.pallas_call(
        body, name="in_fwd", grid=(T // tt,),
        in_specs=[row(D_MODEL), row(1), full(invf), full(w_pre), full(win), full(qnw), full(wq), full(kvnw),
                  full(wk), full(wv)],
        out_specs=[row(Q_RANK), row(KV_RANK), row(4 * HGRN_WIDTH), row(qk_w), row(qk_w), row(MLA_WIDTH),
                   pl.BlockSpec((qk_w, tt), lambda i: (0, i)), pl.BlockSpec((MLA_WIDTH, tt), lambda i: (0, i)),
                   row(HEAD_PAD), row(HEAD_PAD)],
        out_shape=[jax.ShapeDtypeStruct((T, Q_RANK), F32), jax.ShapeDtypeStruct((T, KV_RANK), F32),
                   jax.ShapeDtypeStruct((T, 4 * HGRN_WIDTH), F32), jax.ShapeDtypeStruct((T, qk_w), MXU_DTYPE),
                   jax.ShapeDtypeStruct((T, qk_w), MXU_DTYPE), jax.ShapeDtypeStruct((T, MLA_WIDTH), MXU_DTYPE),
                   jax.ShapeDtypeStruct((qk_w, T), MXU_DTYPE), jax.ShapeDtypeStruct((MLA_WIDTH, T), MXU_DTYPE),
                   jax.ShapeDtypeStruct((T, HEAD_PAD), F32), jax.ShapeDtypeStruct((T, HEAD_PAD), F32)],
        compiler_params=_params(("arbitrary",)),
    )(x, pos, invf, w_pre, win, qnw, wq, kvnw, wk, wv)


def _attn_fwd_t(qb, kb, vt, gather=(), tq=256, hps=8):
    T = qb.shape[0]
    nq = T // tq
    ng = len(gather)
    steps = (MLA_HEADS // hps) * nq
    pass_on = steps - 3

    def body(q_ref, k_ref, vt_ref, *rest):
        o_ref, lse_ref = rest[ng:ng + 2]
        acc_scr = rest[2 * ng + 2]
        qi = pl.program_id(1)
        step_no = pl.program_id(0) * nq + qi
        if ng:
            gat = _Gather(rest[:ng], rest[ng + 2:2 * ng + 2], *rest[2 * ng + 3:])

            @pl.when(step_no == 0)
            def _():
                for cp in gat.sends():
                    cp.start()

            @pl.when(step_no == pass_on)
            def _():
                for arrival in gat.arrivals():
                    arrival.wait_recv()
                for cp in gat.forwards():
                    cp.start()

        heads = [slice(HEAD_PAD * a, HEAD_PAD * (a + 1)) for a in range(hps)]
        acc_scr[...] = jnp.zeros_like(acc_scr)

        def step(j, carry, masked):
            start = pl.multiple_of(j * tq, tq)
            scores = [_mm_nt(k_ref[pl.ds(start, tq), heads[a]], q_ref[:, heads[a]]) for a in range(hps)]
            new = []
            for a in range(hps):
                m, l = carry[a]
                s = scores[a] * ATTN_SCALE_LOG2
                if masked:
                    kk = lax.broadcasted_iota(jnp.int32, (tq, tq), 0)
                    qq = lax.broadcasted_iota(jnp.int32, (tq, tq), 1)
                    s = jnp.where(kk <= qq, s, NEG_BIG)
                m_new = jnp.maximum(m, jnp.max(s, axis=0, keepdims=True))
                alpha = jnp.exp2(m - m_new)
                p = jnp.exp2(s - m_new)
                l = l * alpha + jnp.sum(p, axis=0, keepdims=True)
                vtj = vt_ref[2 * MLA_V * (a // 2):2 * MLA_V * (a // 2 + 1), pl.ds(start, tq)]
                acc_scr[a] = acc_scr[a] * alpha + _mm(vtj, p)
                new.append((m_new, l))
            return tuple(new)

        init = tuple((jnp.full((1, tq), NEG_BIG, F32), jnp.zeros((1, tq), F32)) for _ in range(hps))
        carry = lax.fori_loop(0, qi, lambda j, c: step(j, c, False), init)
        carry = step(qi, carry, True)
        row = lax.broadcasted_iota(jnp.int32, (2 * MLA_V, tq), 0)
        for pr in range(hps // 2):
            (m0, l0), (m1, l1) = carry[2 * pr], carry[2 * pr + 1]
            ot = jnp.where(row < MLA_V, acc_scr[2 * pr] / l0, acc_scr[2 * pr + 1] / l1)
            o_ref[:, 2 * MLA_V * pr:2 * MLA_V * (pr + 1)] = ot.T
            lse_ref[pr, 0:1, :] = m0 + jnp.log2(l0)
            lse_ref[pr, 1:2, :] = m1 + jnp.log2(l1)

        if ng:
            @pl.when(step_no == steps - 1)
            def _():
                for arrival in gat.forward_arrivals():
                    arrival.wait_recv()
                for cp in gat.sends() + gat.forwards():
                    cp.wait_send()

    return pl.pallas_call(
        body, name="attn_fwd", grid=(MLA_HEADS // hps, nq),
        in_specs=[pl.BlockSpec((tq, hps * HEAD_PAD), lambda g, i: (i, g)),
                  pl.BlockSpec((T, hps * HEAD_PAD), lambda g, i: (0, g)),
                  pl.BlockSpec((hps * MLA_V, T), lambda g, i: (g, 0))] + [ANY] * ng,
        out_specs=[pl.BlockSpec((tq, hps * MLA_V), lambda g, i: (i, g)),
                   pl.BlockSpec((hps // 2, 2, tq), lambda g, i: (g, 0, i))] + [ANY] * ng,
        out_shape=[jax.ShapeDtypeStruct((T, MLA_WIDTH), F32), jax.ShapeDtypeStruct((MLA_HEADS // 2, 2, T), F32)]
        + _Gather.out_shapes(gather),
        scratch_shapes=[pltpu.VMEM((hps, 2 * MLA_V, tq), F32)] + (_Gather.semaphores(gather) if ng else []),
        compiler_params=_params(("arbitrary", "arbitrary")),
    )(qb, kb, vt, *gather)


def _attn_bwd_t(qb, kb, kt, vb, dob, lse, dvec, send=(), tq=256, hps=4):
    T = qb.shape[0]
    nq = T // tq
    ns = len(send)
    steps = (MLA_HEADS // hps) * nq

    def body(q_ref, k_ref, kt_ref, v_ref, do_ref, lse_ref, d_ref, *rest):
        dqt_ref, dk_ref, dv_ref = rest[ns:ns + 3]
        va_scr, dv_scr = rest[2 * ns + 3:2 * ns + 5]
        j = pl.program_id(1)
        step_no = pl.program_id(0) * nq + j
        if ns:
            @pl.when(step_no == 0)
            def _():
                for cp in _chip_swap_copies(rest[:ns], rest[ns + 3:2 * ns + 3], *rest[2 * ns + 5:]):
                    cp.start()

        @pl.when(j == 0)
        def _():
            dqt_ref[...] = jnp.zeros_like(dqt_ref)

        lane = lax.broadcasted_iota(jnp.int32, (tq, 2 * MLA_V), 1)
        heads = [slice(HEAD_PAD * a, HEAD_PAD * (a + 1)) for a in range(hps)]
        pairs = [slice(2 * MLA_V * p, 2 * MLA_V * (p + 1)) for p in range(hps // 2)]
        for pr in range(hps // 2):
            vpair = v_ref[:, pairs[pr]]
            va_scr[2 * pr] = jnp.where(lane < MLA_V, vpair, jnp.zeros_like(vpair))
            va_scr[2 * pr + 1] = jnp.where(lane >= MLA_V, vpair, jnp.zeros_like(vpair))
        dk_ref[...] = jnp.zeros_like(dk_ref)
        dv_scr[...] = jnp.zeros_like(dv_scr)

        def step(i, masked):
            start = pl.multiple_of(i * tq, tq)
            rows = pl.ds(start, tq)
            scores = [_mm_nt(k_ref[:, heads[a]], q_ref[rows, heads[a]]) for a in range(hps)]
            dps = [_mm_nt(va_scr[a], do_ref[rows, pairs[a // 2]]) for a in range(hps)]
            for a in range(hps):
                pr, r = a // 2, a % 2
                p = jnp.exp2(scores[a] * ATTN_SCALE_LOG2 - lse_ref[pr, r:r + 1, rows])
                if masked:
                    kk = lax.broadcasted_iota(jnp.int32, (tq, tq), 0)
                    qq = lax.broadcasted_iota(jnp.int32, (tq, tq), 1)
                    p = jnp.where(kk <= qq, p, 0.0)
                ds = p * (dps[a] - d_ref[pr, r:r + 1, rows]) * ATTN_SCALE
                dv_scr[a] += _mm(p, do_ref[rows, pairs[pr]])
                dk_ref[:, heads[a]] += _mm(ds, q_ref[rows, heads[a]])
                dqt_ref[heads[a], rows] += _mm(kt_ref[heads[a], :], ds)

        def loop_body(i, _):
            step(i, False)
            return 0

        step(j, True)
        lax.fori_loop(j + 1, nq, loop_body, 0)
        for pr in range(hps // 2):
            dv_ref[:, pairs[pr]] = jnp.where(lane < MLA_V, dv_scr[2 * pr], dv_scr[2 * pr + 1])

        if ns:
            @pl.when(step_no == steps - 1)
            def _():
                for cp in _chip_swap_copies(rest[:ns], rest[ns + 3:2 * ns + 3], *rest[2 * ns + 5:]):
                    cp.wait()

    stat = pl.BlockSpec((hps // 2, 2, T), lambda g, j: (g, 0, 0))
    return pl.pallas_call(
        body, name="attn_bwd", grid=(MLA_HEADS // hps, nq),
        in_specs=[pl.BlockSpec((T, hps * HEAD_PAD), lambda g, j: (0, g)),
                  pl.BlockSpec((tq, hps * HEAD_PAD), lambda g, j: (j, g)),
                  pl.BlockSpec((hps * HEAD_PAD, tq), lambda g, j: (g, j)),
                  pl.BlockSpec((tq, hps * MLA_V), lambda g, j: (j, g)),
                  pl.BlockSpec((T, hps * MLA_V), lambda g, j: (0, g)), stat, stat] + [ANY] * ns,
        out_specs=[pl.BlockSpec((hps * HEAD_PAD, T), lambda g, j: (g, 0)),
                   pl.BlockSpec((tq, hps * HEAD_PAD), lambda g, j: (j, g)),
                   pl.BlockSpec((tq, hps * MLA_V), lambda g, j: (j, g))] + [ANY] * ns,
        out_shape=[jax.ShapeDtypeStruct((MLA_HEADS * HEAD_PAD, T), F32),
                   jax.ShapeDtypeStruct((T, MLA_HEADS * HEAD_PAD), F32),
                   jax.ShapeDtypeStruct((T, MLA_WIDTH), F32)] + _chip_swap_shapes(send),
        scratch_shapes=[pltpu.VMEM((hps, tq, 2 * MLA_V), vb.dtype), pltpu.VMEM((hps, tq, 2 * MLA_V), F32)]
        + ([pltpu.SemaphoreType.DMA((3 * ns,)), pltpu.SemaphoreType.DMA((3 * ns,))] if ns else []),
        compiler_params=_params(("arbitrary", "arbitrary")),
    )(qb, kb, kt, vb, dob, lse, dvec, *send)


def _cumsum_rows(x):
    n = x.shape[0]
    row = lax.broadcasted_iota(jnp.int32, x.shape, 0)
    s = 1
    while s < n:
        x = x + jnp.where(row >= s, pltpu.roll(x, s, 0), 0.0)
        s *= 2
    return x


def _rev_cumsum_rows(x):
    n = x.shape[0]
    row = lax.broadcasted_iota(jnp.int32, x.shape, 0)
    s = 1
    while s < n:
        x = x + jnp.where(row < n - s, pltpu.roll(x, n - s, 0), 0.0)
        s *= 2
    return x


def _lb_from_logits(l):
    l0, l1 = l[0:1, :], l[1:2, :]
    m = jnp.maximum(l0, l1)
    e0, e1 = jnp.exp(l0 - m), jnp.exp(l1 - m)
    return e0 / (e0 + e1)


def _hgrn_gates(hq, hf, lb):
    sig_f = jax.nn.sigmoid(hf)
    f = lb + (1.0 - lb) * sig_f
    sig_q = jax.nn.sigmoid(hq)
    return sig_f, f, jnp.log(f), 1.0 - f, sig_q, hq * sig_q


def _hgrn_intra(q, kk, b, exact=False):
    row = lax.broadcasted_iota(jnp.int32, b.shape, 0)
    qs, ks, eqs, eks, a_rows = [], [], [], [], []
    for i in range(CHUNK // SUB):
        ref = b[SUB * i:SUB * i + 1, :]
        eq = jnp.exp(b[SUB * i:SUB * (i + 1), :] - ref)
        ek = jnp.exp(jnp.where(row < SUB * (i + 1), ref - b, NEG_BIG))
        qi = q[SUB * i:SUB * (i + 1), :] * eq
        ki = kk * ek
        a_rows.append(_mm_nt(qi, ki, exact))
        qs.append(qi), ks.append(ki), eqs.append(eq), eks.append(ek)
    tt = lax.broadcasted_iota(jnp.int32, (CHUNK, CHUNK), 0)
    ss = lax.broadcasted_iota(jnp.int32, (CHUNK, CHUNK), 1)
    causal = ss <= tt
    a = jnp.where(causal, jnp.concatenate(a_rows, axis=0), 0.0)
    return a, causal, qs, ks, eqs, eks


def _hgrn_fwd(xph, lbl, tg=512):
    T = xph.shape[0]
    ng, ncg = T // tg, tg // CHUNK
    cols = [slice(HGRN_DIM * h, HGRN_DIM * (h + 1)) for h in range(HGRN_HEADS)]

    def body(lbl_ref, hq_ref, hf_ref, hi_ref, o_ref, st_ref, s_scr):
        @pl.when(pl.program_id(0) == 0)
        def _():
            s_scr[...] = jnp.zeros_like(s_scr)

        lb = _lb_from_logits(lbl_ref[...])

        def chunks(it, _):
            pre = []
            for k in range(HGRN_CPI):
                c = it * HGRN_CPI + k
                rows = pl.ds(pl.multiple_of(c * CHUNK, CHUNK), CHUNK)
                for cs in cols:
                    _, _, lf, kk, _, q = _hgrn_gates(hq_ref[rows, cs], hf_ref[rows, cs], lb[:, cs])
                    v = hi_ref[rows, cs]
                    b = _cumsum_rows(lf)
                    a = _hgrn_intra(q, kk, b)[0]
                    b_last = b[CHUNK - 1:CHUNK, :]
                    pre.append((c, rows, q * jnp.exp(b), a, v, jnp.exp(b_last), _mm_tn(v, kk * jnp.exp(b_last - b))))
            for i, (c, rows, qe, a, v, ebl, upd) in enumerate(pre):
                h = i % HGRN_HEADS
                st = s_scr[h]
                st_ref[h, c] = st
                o_ref[rows, cols[h]] = _mm_nt(qe, st) + _mm(a, v)
                s_scr[h] = st * ebl + upd
            return 0

        lax.fori_loop(0, ncg // HGRN_CPI, chunks, 0)

    col = lambda k: pl.BlockSpec((tg, HGRN_WIDTH), lambda g: (g, k))
    return pl.pallas_call(
        body, name="hgrn_fwd", grid=(ng,),
        in_specs=[pl.BlockSpec((2, HGRN_WIDTH), lambda g: (0, 0)), col(0), col(1), col(2)],
        out_specs=[col(0), pl.BlockSpec((HGRN_HEADS, ncg, HGRN_DIM, HGRN_DIM), lambda g: (0, g, 0, 0))],
        out_shape=[jax.ShapeDtypeStruct((T, HGRN_WIDTH), F32),
                   jax.ShapeDtypeStruct((HGRN_HEADS, T // CHUNK, HGRN_DIM, HGRN_DIM), F32)],
        scratch_shapes=[pltpu.VMEM((HGRN_HEADS, HGRN_DIM, HGRN_DIM), F32)],
        compiler_params=_params(("arbitrary",)),
    )(lbl, xph, xph, xph)


def _hgrn_bwd(xph, lbl, states, d_o, tg=512):
    T = xph.shape[0]
    ng, ncg = T // tg, tg // CHUNK
    cols = [slice(HGRN_DIM * h, HGRN_DIM * (h + 1)) for h in range(HGRN_HEADS)]
    nsub = CHUNK // SUB

    def body(lbl_ref, hq_ref, hf_ref, hi_ref, st_ref, do_ref, dhq_ref, dhf_ref, dhi_ref, dlg_ref, ds_scr, dlb_scr):
        g = pl.program_id(0)

        @pl.when(g == 0)
        def _():
            ds_scr[...] = jnp.zeros_like(ds_scr)
            dlb_scr[...] = jnp.zeros_like(dlb_scr)

        lb = _lb_from_logits(lbl_ref[...])

        def chunks(it, _):
            pre = []
            for k, h in ((k, h) for k in range(HGRN_CPI) for h in range(HGRN_HEADS)):
                cs = cols[h]
                c = ncg - 1 - (it * HGRN_CPI + k)
                rows = pl.ds(pl.multiple_of(c * CHUNK, CHUNK), CHUNK)
                hq = hq_ref[rows, cs]
                sig_f, f, lf, kk, sig_q, q = _hgrn_gates(hq, hf_ref[rows, cs], lb[:, cs])
                v = hi_ref[rows, cs]
                do = do_ref[rows, cs]
                b = _cumsum_rows(lf)
                eb = jnp.exp(b)
                a, causal, qs, ks, eqs, eks = _hgrn_intra(q, kk, b)
                b_last = b[CHUNK - 1:CHUNK, :]
                st = st_ref[h, c]
                pre.append(dict(h=h, cs=cs, rows=rows, hq=hq, sig_f=sig_f, f=f, kk=kk, sig_q=sig_q, q=q, v=v, eb=eb, qs=qs,
                                ks=ks, eqs=eqs,
                                eks=eks, ebl=jnp.exp(b_last), el=jnp.exp(b_last - b), st=st,
                                da=jnp.where(causal, _mm_nt(do, v, True), 0.0), dq=_mm(do, st, True) * eb,
                                dv=_mm_tn(a, do), dsu=_mm_tn(do, q * eb, True)))
            for w in pre:
                dq_rows = []
                dk = jnp.zeros_like(w["q"])
                for i in range(nsub):
                    dai = w["da"][SUB * i:SUB * (i + 1), :]
                    dq_rows.append(_mm(dai, w["ks"][i], True) * w["eqs"][i])
                    dk = dk + _mm_tn(dai, w["qs"][i], True) * w["eks"][i]
                w["dq"] = w["dq"] + jnp.concatenate(dq_rows, axis=0)
                w["dk"] = dk
            for w in pre:
                h, cs, rows = w["h"], w["cs"], w["rows"]
                kk, el, ebl, dst = w["kk"], w["el"], w["ebl"], ds_scr[h]
                dk_state = _mm(w["v"], dst, True) * el
                dk = w["dk"] + dk_state
                e_last = (ebl * jnp.sum(w["st"] * dst, axis=0, keepdims=True)
                          + jnp.sum(kk * dk_state, axis=0, keepdims=True))
                dlf = _rev_cumsum_rows(w["q"] * w["dq"] - kk * dk) + e_last
                ds_scr[h] = dst * ebl + w["dsu"]
                df = dlf / w["f"] - dk
                sig_f, sig_q = w["sig_f"], w["sig_q"]
                dhf_ref[rows, cs] = df * (1.0 - lb[:, cs]) * sig_f * (1.0 - sig_f)
                dlb_scr[:, cs] += jnp.sum(df * (1.0 - sig_f), axis=0, keepdims=True)
                dhq_ref[rows, cs] = w["dq"] * sig_q * (1.0 + w["hq"] * (1.0 - sig_q))
                dhi_ref[rows, cs] = w["dv"] + _mm_nt(kk * el, dst)
            return 0

        lax.fori_loop(0, ncg // HGRN_CPI, chunks, 0)

        @pl.when(g == ng - 1)
        def _():
            dl0 = dlb_scr[...] * lb * (1.0 - lb)
            dlg_ref[...] = jnp.concatenate([dl0, -dl0], axis=0)

    col = lambda k: pl.BlockSpec((tg, HGRN_WIDTH), lambda g: (ng - 1 - g, k))
    logits = pl.BlockSpec((2, HGRN_WIDTH), lambda g: (0, 0))
    big = jax.ShapeDtypeStruct((T, HGRN_WIDTH), F32)
    return pl.pallas_call(
        body, name="hgrn_bwd", grid=(ng,),
        in_specs=[logits, col(0), col(1), col(2),
                  pl.BlockSpec((HGRN_HEADS, ncg, HGRN_DIM, HGRN_DIM), lambda g: (0, ng - 1 - g, 0, 0)), col(0)],
        out_specs=[col(0), col(0), col(0), logits],
        out_shape=[big, big, big, jax.ShapeDtypeStruct((2, HGRN_WIDTH), F32)],
        scratch_shapes=[pltpu.VMEM((HGRN_HEADS, HGRN_DIM, HGRN_DIM), F32), pltpu.VMEM((1, HGRN_WIDTH), F32)],
        compiler_params=_params(("arbitrary",)),
    )(lbl, xph, xph, xph, states, d_o)


def _proj_fwd(x, o_raw, oh_raw, xph, wout, w_mla, w_hg, w_post, w_fpre, tt=512):
    T = x.shape[0]

    def body(x_ref, o_ref, oh_ref, hg_ref, wout_ref, wmla_ref, whg_ref, wpost_ref, wfpre_ref,
             h1_ref, y1_ref, z_ref, mix_ref):
        om, _, _ = _grms_fwd(o_ref[...], wmla_ref[...], MLA_V)
        hg = hg_ref[...]
        ohn, _, _ = _grms_fwd(oh_ref[...], whg_ref[...], HGRN_DIM)
        mix = jnp.concatenate([om, ohn * (hg * jax.nn.sigmoid(hg))], axis=-1)
        mix_ref[...] = mix.astype(mix_ref.dtype)
        y1 = _mm(mix, wout_ref[...])
        y1_ref[...] = y1
        h1 = x_ref[...] + _rms_fwd(y1, wpost_ref[...])[0]
        h1_ref[...] = h1
        z_ref[...] = _rms_fwd(h1, wfpre_ref[...])[0].astype(z_ref.dtype)

    row = lambda w: pl.BlockSpec((tt, w), lambda i: (i, 0))
    full = lambda a: pl.BlockSpec(a.shape, lambda i: (0,) * a.ndim)
    sds = jax.ShapeDtypeStruct
    return pl.pallas_call(
        body, name="proj_fwd", grid=(T // tt,),
        in_specs=[row(D_MODEL), row(MLA_WIDTH), row(HGRN_WIDTH), pl.BlockSpec((tt, HGRN_WIDTH), lambda i: (i, 3)),
                  full(wout), full(w_mla), full(w_hg), full(w_post), full(w_fpre)],
        out_specs=[row(D_MODEL)] * 4,
        out_shape=[sds((T, D_MODEL), F32), sds((T, D_MODEL), F32), sds((T, D_MODEL), MXU_DTYPE),
                   sds((T, D_MODEL), MXU_DTYPE)],
        compiler_params=_params(("arbitrary",)),
    )(x, o_raw, oh_raw, xph, wout, w_mla, w_hg, w_post, w_fpre)


def _ffn_fwd(zb, h1, tgt, w_fpost, wg, wu, wd, tt=256):
    T = zb.shape[0]
    nj = N_CHIPS

    def body(z_ref, h1_ref, tgt_ref, wfpost_ref, wg_ref, wu_ref, wd_ref, g_ref, up_ref, dy2_ref, dh2_ref, loss_ref, dwf_ref):
        @pl.when(pl.program_id(0) == 0)
        def _():
            loss_ref[...] = jnp.zeros_like(loss_ref)
            dwf_ref[...] = jnp.zeros_like(dwf_ref)

        z = z_ref[...]
        gs = [_mm_nt(z, wg_ref[j]) for j in range(nj)]
        ups = [_mm_nt(z, wu_ref[j]) for j in range(nj)]
        y2 = jnp.zeros((tt, D_MODEL), F32)
        for j in range(nj):
            g_ref[j] = gs[j]
            up_ref[j] = ups[j]
            y2 = y2 + _mm(gs[j] * jax.nn.sigmoid(gs[j]) * ups[j], wd_ref[j])
        w = wfpost_ref[...]
        y2s, y2n, r2 = _rms_fwd(y2, w)
        e = h1_ref[...] + y2s - tgt_ref[...]
        loss_ref[...] += jnp.sum(e * e, axis=0, keepdims=True)
        dh2 = e * (1.0 / D_MODEL)
        dh2_ref[...] = dh2
        dy2, dwf = _rms_bwd(dh2, y2n, r2, w)
        dy2_ref[...] = dy2.astype(dy2_ref.dtype)
        dwf_ref[...] += dwf

    row = pl.BlockSpec((tt, D_MODEL), lambda i: (i, 0))
    vec = pl.BlockSpec((1, D_MODEL), lambda i: (0, 0))
    resident = pl.BlockSpec((nj, FF_SHARD, D_MODEL), lambda i: (0, 0, 0), pipeline_mode=pl.Buffered(1))
    act = pl.BlockSpec((nj, tt, FF_SHARD), lambda i: (0, i, 0))
    sds = jax.ShapeDtypeStruct
    return pl.pallas_call(
        body, name="ffn_fwd", grid=(T // tt,),
        in_specs=[row, row, row, vec, resident, resident, resident],
        out_specs=[act, act, row, row, vec, vec],
        out_shape=[sds((nj, T, FF_SHARD), F32), sds((nj, T, FF_SHARD), F32), sds((T, D_MODEL), MXU_DTYPE),
                   sds((T, D_MODEL), F32), sds((1, D_MODEL), F32), sds((1, D_MODEL), F32)],
        compiler_params=_params(("arbitrary",)),
    )(zb, h1, tgt, w_fpost, wg, wu, wd)


def _ffn_bwd(zb, g, up, dy2b, wg, wu, wd, tt=512):
    T = zb.shape[0]
    nj = N_CHIPS

    def body(z_ref, g_ref, up_ref, dy2_ref, wg_ref, wu_ref, wd_ref, dwg_ref, dwu_ref, dwd_ref, dz_ref):
        @pl.when(pl.program_id(1) == 0)
        def _():
            dwg_ref[...] = jnp.zeros_like(dwg_ref)
            dwu_ref[...] = jnp.zeros_like(dwu_ref)
            dwd_ref[...] = jnp.zeros_like(dwd_ref)

        z, g_, up_, dy2 = z_ref[...], g_ref[0], up_ref[0], dy2_ref[...]
        sg = jax.nn.sigmoid(g_)
        act = g_ * sg
        dff = _mm_nt(dy2, wd_ref[0])
        dwd_ref[0] += _mm_tn(act * up_, dy2)
        dg = dff * up_ * sg * (1.0 + g_ * (1.0 - sg))
        dup = dff * act
        dwg_ref[0] += _mm_tn(dg, z)
        dwu_ref[0] += _mm_tn(dup, z)
        dz_ref[0] = _mm(dg, wg_ref[0]) + _mm(dup, wu_ref[0])

    row = pl.BlockSpec((tt, D_MODEL), lambda j, i: (i, 0))
    act = pl.BlockSpec((1, tt, FF_SHARD), lambda j, i: (j, i, 0))
    w_sh = pl.BlockSpec((1, FF_SHARD, D_MODEL), lambda j, i: (j, 0, 0))
    w_grad = jax.ShapeDtypeStruct((nj, FF_SHARD, D_MODEL), F32)
    return pl.pallas_call(
        body, name="ffn_bwd", grid=(nj, T // tt),
        in_specs=[row, act, act, row, w_sh, w_sh, w_sh],
        out_specs=[w_sh, w_sh, w_sh, pl.BlockSpec((1, tt, D_MODEL), lambda j, i: (j, i, 0))],
        out_shape=[w_grad, w_grad, w_grad, jax.ShapeDtypeStruct((nj, T, D_MODEL), F32)],
        compiler_params=_params(("arbitrary", "arbitrary")),
    )(zb, g, up, dy2b, wg, wu, wd)


def _mid_bwd(dzp, dh2, h1, y1, mixb, o_raw, oh_raw, xph, wout, w_fpre, w_post, w_mla, w_hg, swap=(), tt=256):
    T = dh2.shape[0]
    nsw = len(swap)
    n_in, n_out = 13, 10

    def body(*refs):
        (dzp_ref, dh2_ref, h1_ref, y1_ref, mix_ref, o_ref, oh_ref, hg_ref, wout_ref, wfpre_ref, wpost_ref,
         wmla_ref, whg_ref) = refs[:n_in]
        (dh1_ref, dwout_ref, do_ref, doh_ref, dhg_ref, dvec_ref, dwfpre_ref, dwpost_ref, dwmla_ref,
         dwhg_ref) = refs[n_in + nsw:n_in + nsw + n_out]
        swap_copies = lambda: _pair_swap_copies(refs[n_in:n_in + nsw], refs[n_in + nsw + n_out:n_in + 2 * nsw + n_out],
                                                *refs[n_in + 2 * nsw + n_out:])

        @pl.when(pl.program_id(0) == 0)
        def _():
            for r in (dwout_ref, dwfpre_ref, dwpost_ref, dwmla_ref, dwhg_ref):
                r[...] = jnp.zeros_like(r)
            for cp in (swap_copies() if nsw else ()):
                cp.start()

        dz = dzp_ref[0] + dzp_ref[1] + dzp_ref[2] + dzp_ref[3]
        wfpre = wfpre_ref[...]
        _, h1n, r = _rms_fwd(h1_ref[...], wfpre)
        dh1_z, dwfpre = _rms_bwd(dz, h1n, r, wfpre)
        dwfpre_ref[...] += dwfpre
        dh1 = dh2_ref[...] + dh1_z
        dh1_ref[...] = dh1
        wpost = wpost_ref[...]
        _, y1n, r1 = _rms_fwd(y1_ref[...], wpost)
        dy1, dwpost = _rms_bwd(dh1, y1n, r1, wpost)
        dwpost_ref[...] += dwpost
        dmix = _mm_nt(dy1, wout_ref[...])
        dwout_ref[...] += _mm_tn(mix_ref[...], dy1)
        wmla = wmla_ref[...]
        o = o_ref[...]
        _, on, ro = _grms_fwd(o, wmla, MLA_V)
        d_o, dwmla = _grms_bwd(dmix[:, :MLA_WIDTH], on, ro, wmla, MLA_V)
        dwmla_ref[...] += dwmla
        do_ref[...] = d_o.astype(do_ref.dtype)
        hh = lax.broadcasted_iota(jnp.int32, (MLA_HEADS, MLA_WIDTH), 0)
        ll = lax.broadcasted_iota(jnp.int32, (MLA_HEADS, MLA_WIDTH), 1)
        sel = jnp.where((ll >= hh * MLA_V) & (ll < (hh + 1) * MLA_V), 1.0, 0.0)
        dvec_ref[...] = _mm_nt(sel, d_o * o, True)
        whg = whg_ref[...]
        hg = hg_ref[...]
        sg = jax.nn.sigmoid(hg)
        _, ohn, rh = _grms_fwd(oh_ref[...], whg, HGRN_DIM)
        dmh = dmix[:, MLA_WIDTH:]
        dhg_ref[...] = dmh * ohn * whg * sg * (1.0 + hg * (1.0 - sg))
        d_oh, dwhg = _grms_bwd(dmh * (hg * sg), ohn, rh, whg, HGRN_DIM)
        dwhg_ref[...] += dwhg
        doh_ref[...] = d_oh

        if nsw:
            @pl.when(pl.program_id(0) == T // tt - 1)
            def _():
                for cp in swap_copies():
                    cp.wait()

    row = lambda w: pl.BlockSpec((tt, w), lambda i: (i, 0))
    full = lambda a: pl.BlockSpec(a.shape, lambda i: (0,) * a.ndim)
    vec = lambda w: pl.BlockSpec((1, w), lambda i: (0, 0))
    sds = jax.ShapeDtypeStruct
    return pl.pallas_call(
        body, name="mid_bwd", grid=(T // tt,),
        in_specs=[pl.BlockSpec((N_CHIPS, tt, D_MODEL), lambda i: (0, i, 0)), row(D_MODEL), row(D_MODEL), row(D_MODEL),
                  row(D_MODEL), row(MLA_WIDTH), row(HGRN_WIDTH), pl.BlockSpec((tt, HGRN_WIDTH), lambda i: (i, 3)),
                  full(wout), vec(D_MODEL), vec(D_MODEL), vec(MLA_WIDTH), vec(HGRN_WIDTH)] + [ANY] * nsw,
        out_specs=[row(D_MODEL), full(wout), row(MLA_WIDTH), row(HGRN_WIDTH), row(HGRN_WIDTH),
                   pl.BlockSpec((MLA_HEADS, tt), lambda i: (0, i)),
                   vec(D_MODEL), vec(D_MODEL), vec(MLA_WIDTH), vec(HGRN_WIDTH)] + [ANY] * nsw,
        out_shape=[sds((T, D_MODEL), F32), sds(wout.shape, F32), sds((T, MLA_WIDTH), MXU_DTYPE), sds((T, HGRN_WIDTH), F32),
                   sds((T, HGRN_WIDTH), F32), sds((MLA_HEADS, T), F32),
                   sds((1, D_MODEL), F32), sds((1, D_MODEL), F32), sds((1, MLA_WIDTH), F32), sds((1, HGRN_WIDTH), F32)]
        + _half_stack_shapes(swap),
        scratch_shapes=[pltpu.SemaphoreType.DMA((nsw,)), pltpu.SemaphoreType.DMA((nsw,))] if nsw else [],
        compiler_params=_params(("arbitrary",)),
    )(dzp, dh2, h1, y1, mixb, o_raw, oh_raw, xph, wout, w_fpre, w_post, w_mla, w_hg, *swap)


def _in_bwd(x, dh1, cq, ckv, dq, dk, dv, dhq, dhf, dhi, dhg, rc, rs, w_pre, win, qnw, wq, kvnw, wk, wv, tt=256):
    T = x.shape[0]

    def body(x_ref, dh1_ref, cq_ref, ckv_ref, dq_ref, dk_ref, dv_ref, dhq_ref, dhf_ref, dhi_ref, dhg_ref, rc_ref, rs_ref,
             wpre_ref, win_ref, qnw_ref, wq_ref, kvnw_ref, wk_ref, wv_ref,
             dx_ref, dwin_ref, dwq_ref, dwk_ref, dwv_ref, dwpre_ref, dqnw_ref, dkvnw_ref):
        @pl.when(pl.program_id(0) == 0)
        def _():
            for r in (dwin_ref, dwq_ref, dwk_ref, dwv_ref, dwpre_ref, dqnw_ref, dkvnw_ref):
                r[...] = jnp.zeros_like(r)

        def add_win_grad(r, first):
            for arr0, n, chip, row0 in _win_grad_segments():
                if first <= arr0 and arr0 + n <= first + r.shape[0]:
                    dwin_ref[chip, row0:row0 + n, :] += r[arr0 - first:arr0 - first + n]

        lo = Q_RANK + KV_RANK + HEAD_PAD
        wpre = wpre_ref[...]
        u, xn, rx = _rms_fwd(x_ref[...], wpre)
        dxp_h = jnp.concatenate([dhq_ref[...], dhf_ref[...], dhi_ref[...], dhg_ref[...]], axis=-1)
        add_win_grad(_mm_tn(dxp_h, u), lo)
        du = _mm(dxp_h, win_ref[lo:, :])
        c, sa, sb = _rope_tables(rc_ref[...], rs_ref[...])
        lane = lax.broadcasted_iota(jnp.int32, (tt, HEAD_PAD), 1)
        dk_all = dk_ref[...]
        dq_lin = []
        dkr = jnp.zeros((tt, HEAD_PAD), F32)
        for h in range(MLA_HEADS):
            sl = slice(HEAD_PAD * h, HEAD_PAD * (h + 1))
            dq_lin.append(_rope_bwd(dq_ref[sl, :].T, c, sa, sb))
            dkr = dkr + dk_all[:, sl]
        dq_lin = jnp.concatenate(dq_lin, axis=-1)
        dkr = jnp.where((lane >= MLA_NOPE) & (lane < MLA_QK), _rope_bwd(dkr, c, sa, sb), 0.0)
        qnw = qnw_ref[...]
        qn, cqn, rq = _rms_fwd(cq_ref[...], qnw)
        dwq_ref[...] += _mm_tn(qn, dq_lin)
        dcq, dqnw = _rms_bwd(_mm_nt(dq_lin, wq_ref[...]), cqn, rq, qnw)
        dqnw_ref[...] += dqnw
        kvnw = kvnw_ref[...]
        kvn, ckvn, rkv = _rms_fwd(ckv_ref[...], kvnw)
        dv_ = dv_ref[...]
        dwk_ref[...] += _mm_tn(kvn, dk_all)
        dwv_ref[...] += _mm_tn(kvn, dv_)
        dckv, dkvnw = _rms_bwd(_mm_nt(dk_all, wk_ref[...]) + _mm_nt(dv_, wv_ref[...]), ckvn, rkv, kvnw)
        dkvnw_ref[...] += dkvnw
        dxp_a = jnp.concatenate([dcq, dckv, dkr], axis=-1)
        add_win_grad(_mm_tn(dxp_a, u), 0)
        dx_u, dwpre = _rms_bwd(du + _mm(dxp_a, win_ref[:lo, :]), xn, rx, wpre)
        dwpre_ref[...] += dwpre
        dx_ref[...] = dh1_ref[...] + dx_u

    row = lambda w: pl.BlockSpec((tt, w), lambda i: (i, 0))
    full = lambda a: pl.BlockSpec(a.shape, lambda i: (0,) * a.ndim)
    sds = jax.ShapeDtypeStruct
    qk_w = MLA_HEADS * HEAD_PAD
    return pl.pallas_call(
        body, name="in_bwd", grid=(T // tt,),
        in_specs=[row(D_MODEL), row(D_MODEL), row(Q_RANK), row(KV_RANK), pl.BlockSpec((qk_w, tt), lambda i: (0, i)),
                  row(qk_w), row(MLA_WIDTH),
                  row(HGRN_WIDTH), row(HGRN_WIDTH), row(HGRN_WIDTH), row(HGRN_WIDTH), row(HEAD_PAD), row(HEAD_PAD),
                  full(w_pre), full(win), full(qnw), full(wq), full(kvnw), full(wk), full(wv)],
        out_specs=[row(D_MODEL), pl.BlockSpec(WIN_COMM_SHAPE, lambda i: (0, 0, 0)), full(wq), full(wk), full(wv),
                   full(w_pre), full(qnw), full(kvnw)],
        out_shape=[sds((T, D_MODEL), F32), sds(WIN_COMM_SHAPE, F32), sds(wq.shape, F32), sds(wk.shape, F32),
                   sds(wv.shape, F32), sds(w_pre.shape, F32), sds(qnw.shape, F32), sds(kvnw.shape, F32)],
        compiler_params=_params(("arbitrary",)),
    )(x, dh1, cq, ckv, dq, dk, dv, dhq, dhf, dhi, dhg, rc, rs, w_pre, win, qnw, wq, kvnw, wk, wv)


def _arrange_weights(win_t, wuq_full, wukv):
    dt = win_t.dtype
    z = lambda n: jnp.zeros((n, D_MODEL), dt)
    s2 = Q_RANK + KV_RANK
    win_arr = jnp.concatenate([win_t[:s2], z(MLA_NOPE), win_t[s2:s2 + MLA_ROPE], z(HEAD_PAD - MLA_QK),
                               win_t[s2 + MLA_ROPE:]], axis=0)
    wq_arr = jnp.pad(wuq_full, ((0, 0), (0, 0), (0, HEAD_PAD - MLA_QK))).reshape(Q_RANK, MLA_HEADS * HEAD_PAD)
    wk_arr = jnp.pad(wukv[:, :, :MLA_NOPE], ((0, 0), (0, 0), (0, HEAD_PAD - MLA_NOPE))).reshape(
        KV_RANK, MLA_HEADS * HEAD_PAD)
    wv_arr = wukv[:, :, MLA_NOPE:].reshape(KV_RANK, MLA_WIDTH)
    return win_arr, wq_arr, wk_arr, wv_arr


WIN_COMM_SHAPE = (N_CHIPS, FF_SHARD, D_MODEL)


def _win_grad_segments():
    s2 = Q_RANK + KV_RANK
    runs = [(0, s2, 0), (s2, s2 + MLA_ROPE, MLA_NOPE), (s2 + MLA_ROPE, D_IN, HEAD_PAD - MLA_ROPE)]
    per = D_IN // N_CHIPS
    segs = []
    for lo, hi, shift in runs:
        for k in range(N_CHIPS):
            a, b = max(lo, per * k), min(hi, per * (k + 1))
            if a < b:
                segs.append((a + shift, b - a, k, a - per * k))
    return segs


def _unarrange_grads(dwq_arr, dwk_arr, dwv_arr):
    dwuq = dwq_arr.reshape(Q_RANK, MLA_HEADS, HEAD_PAD)[:, :, :MLA_QK]
    dwukv = jnp.concatenate([dwk_arr.reshape(KV_RANK, MLA_HEADS, HEAD_PAD)[:, :, :MLA_NOPE],
                             dwv_arr.reshape(KV_RANK, MLA_HEADS, MLA_V)], axis=-1)
    return dwuq, dwukv


def _rope_inv_freq():
    inv = 1.0 / (ROPE_THETA ** (jnp.arange(0, MLA_ROPE, 2, dtype=F32) / MLA_ROPE))
    z = lambda n: jnp.zeros((n,), F32)
    return jnp.concatenate([z(MLA_NOPE), inv, inv, z(HEAD_PAD - MLA_QK)]).reshape(1, HEAD_PAD)


def _local_step(x, pos, tgt, small, win_arr, wq_arr, wk_arr, wv_arr, late, place=None):
    invf = _rope_inv_freq()
    cq, ckv, xph, qb, kb, vb, kt, vt, rc, rs = _in_fwd(x, pos, invf, small["attn_pre_norm"], win_arr, small["mla_q_norm"],
                                               wq_arr, small["mla_kv_norm"], wk_arr, wv_arr)
    if place is None:
        o_raw, lse = _attn_fwd_t(qb, kb, vt)
        wout, wg, wu, wd = late
    else:
        o_raw, lse, *stacks = _attn_fwd_t(qb, kb, vt, gather=late)
        wout, wg, wu, wd = [lax.dynamic_update_slice(s, l[None], (place[1], 0, 0)) for s, l in zip(stacks, late)]
        wout = wout.reshape(D_MODEL, D_MODEL)
    oh_raw, states = _hgrn_fwd(xph, small["hgrn_lb_logits"])
    h1, y1, zb, mixb = _proj_fwd(x, o_raw, oh_raw, xph, wout, small["mla_out_norm"], small["hgrn_out_norm"],
                                 small["attn_post_norm"], small["ffn_pre_norm"])
    g, up, dy2b, dh2, loss_acc, d_fpost = _ffn_fwd(zb, h1, tgt, small["ffn_post_norm"], wg, wu, wd)
    dwg, dwu, dwd, dzp = _ffn_bwd(zb, g, up, dy2b, wg, wu, wd)
    ffn_grads = [] if place is None else [dwg, dwu, dwd]
    dh1, dwout, d_o, d_oh, dhg, dvec, d_fpre, d_post, d_mla, d_hg, *ffn_rs = _mid_bwd(
        dzp, dh2, h1, y1, mixb, o_raw, oh_raw, xph, wout, small["ffn_pre_norm"], small["attn_post_norm"],
        small["mla_out_norm"], small["hgrn_out_norm"], swap=ffn_grads)
    ffn_ps = _pair_sum(place, ffn_grads, ffn_rs, name="pair_sum_ffn") if ffn_grads else []
    dq, dk, dv, *ffn_ris = _attn_bwd_t(qb, kb, kt, vb, d_o, lse, dvec.reshape(lse.shape), send=ffn_ps)
    dhq, dhf, dhi, d_lbl = _hgrn_bwd(xph, small["hgrn_lb_logits"], states, d_oh)
    dx, dwin4, dwq_arr, dwk_arr, dwv_arr, d_pre, d_qn, d_kvn = _in_bwd(
        x, dh1, cq, ckv, dq, dk, dv, dhq, dhf, dhi, dhg, rc, rs, small["attn_pre_norm"], win_arr,
        small["mla_q_norm"], wq_arr, small["mla_kv_norm"], wk_arr, wv_arr)
    dwuq, dwukv = _unarrange_grads(dwq_arr, dwk_arr, dwv_arr)
    loss = 0.5 * jnp.sum(loss_acc) * (1.0 / D_MODEL)
    grads = dict(attn_pre_norm=d_pre, w_in=dwin4, mla_q_norm=d_qn, mla_w_uq=dwuq, mla_kv_norm=d_kvn, mla_w_ukv=dwukv,
                 mla_out_norm=d_mla, hgrn_lb_logits=d_lbl, hgrn_out_norm=d_hg, w_out=dwout, attn_post_norm=d_post,
                 ffn_pre_norm=d_fpre, w_gate=dwg, w_up=dwu, w_down=dwd, ffn_post_norm=d_fpost)
    if place is None:
        return loss, dx, grads
    return loss, dx, grads, (ffn_rs, ffn_ris)


def _place():
    x, y, c = lax.axis_index("x"), lax.axis_index("y"), lax.axis_index("c")
    others = [(1 - x, y), (x, 1 - y), (1 - x, 1 - y)]
    return x, y, c, 2 * x + y, (x, y, 1 - c), others


def _half(ref, c, rows):
    return ref.at[pl.ds(pl.multiple_of(c * rows, 8), rows)]


def _rcopy(src, dst, send, recv, k, to):
    return pltpu.make_async_remote_copy(src_ref=src, dst_ref=dst, send_sem=send.at[k], recv_sem=recv.at[k],
                                        device_id=to, device_id_type=MESH)


class _Gather:
    def __init__(self, ins, outs, send, recv):
        self.ins, self.outs, self.send, self.recv = ins, outs, send, recv
        self.n = len(ins)
        self.halves = [r.shape[0] // 2 for r in ins]
        _, _, self.c, self.me, self.sib, self.others = _place()

    def _each(self):
        for j, (px, py) in enumerate(self.others):
            for a in range(self.n):
                yield j * self.n + a, a, 2 * px + py, (px, py, self.c)

    def sends(self):
        return [_rcopy(_half(self.ins[a], self.c, self.halves[a]), _half(self.outs[a].at[self.me], self.c, self.halves[a]),
                       self.send, self.recv, k, to) for k, a, _, to in self._each()]

    def arrivals(self):
        parts = [(k, _half(self.outs[a].at[chip], self.c, self.halves[a]), to) for k, a, chip, to in self._each()]
        return [_rcopy(p, p, self.send, self.recv, k, to) for k, p, to in parts]

    def forwards(self):
        parts = [(k, _half(self.outs[a].at[chip], self.c, self.halves[a])) for k, a, chip, _ in self._each()]
        return [_rcopy(p, p, self.send, self.recv, 3 * self.n + k, self.sib) for k, p in parts]

    def forward_arrivals(self):
        parts = [(k, _half(self.outs[a].at[chip], 1 - self.c, self.halves[a])) for k, a, chip, _ in self._each()]
        return [_rcopy(p, p, self.send, self.recv, 3 * self.n + k, self.sib) for k, p in parts]

    @staticmethod
    def out_shapes(arrs):
        return [jax.ShapeDtypeStruct((N_CHIPS,) + a.shape, a.dtype) for a in arrs]

    @staticmethod
    def semaphores(arrs):
        return [pltpu.SemaphoreType.DMA((6 * len(arrs),)), pltpu.SemaphoreType.DMA((6 * len(arrs),))]


def _gather_chips(arrs, name):
    n = len(arrs)

    def body(*refs):
        gat = _Gather(refs[:n], refs[n:2 * n], *refs[2 * n:])
        sends, forwards = gat.sends(), gat.forwards()
        for cp in sends:
            cp.start()
        for arrival, fw in zip(gat.arrivals(), forwards):
            arrival.wait_recv()
            fw.start()
        for arrival in gat.forward_arrivals():
            arrival.wait_recv()
        for cp in sends + forwards:
            cp.wait_send()

    return pl.pallas_call(body, name=name, in_specs=[ANY] * n, out_specs=[ANY] * n, out_shape=_Gather.out_shapes(arrs),
                          scratch_shapes=_Gather.semaphores(arrs))(*arrs)


GRAD_BLOCKS = 2


def _pair_swap_copies(g_refs, r_refs, send, recv):
    _, _, c, _, sib, _ = _place()
    copies = []
    for a, (g, r) in enumerate(zip(g_refs, r_refs)):
        h = g.shape[1] // 2
        copies.append(_rcopy(g.at[:, pl.ds(pl.multiple_of((1 - c) * h, 8), h)], r, send, recv, a, sib))
    return copies


def _half_stack_shapes(gs, dtype=None):
    return [jax.ShapeDtypeStruct((N_CHIPS, g.shape[1] // 2, g.shape[2]), dtype or g.dtype) for g in gs]


def _pair_swap(gs, sm):
    n = len(gs)

    def body(*refs):
        g_refs, sm_ref = refs[:n], refs[n]
        r_refs, ssib_ref = refs[n + 1:2 * n + 1], refs[2 * n + 1]
        send, recv = refs[2 * n + 2:]
        copies = _pair_swap_copies(g_refs, r_refs, send, recv)
        copies.append(_rcopy(sm_ref, ssib_ref, send, recv, n, _place()[4]))
        for cp in copies:
            cp.start()
        for cp in copies:
            cp.wait()

    return pl.pallas_call(
        body, name="pair_swap", in_specs=[ANY] * (n + 1), out_specs=[ANY] * (n + 1),
        out_shape=_half_stack_shapes(gs) + [jax.ShapeDtypeStruct(sm.shape, sm.dtype)],
        scratch_shapes=[pltpu.SemaphoreType.DMA((n + 1,)), pltpu.SemaphoreType.DMA((n + 1,))],
    )(*gs, sm)


def _pair_sum(place, gs, rs, small=None, name="pair_sum"):
    n = len(gs)
    nb = GRAD_BLOCKS

    def body(place_ref, *refs):
        g_refs, r_refs, p_refs = refs[:n], refs[n:2 * n], refs[-n - 1:-1] if small else refs[-n:]
        for a in range(n):
            p_refs[a][0] = (g_refs[a][0] + r_refs[a][0]).astype(p_refs[a].dtype)
        if small:
            @pl.when((pl.program_id(0) == 0) & (pl.program_id(1) == 0))
            def _():
                refs[-1][...] = refs[2 * n][...] + refs[2 * n + 1][...]

    in_specs, out_specs = [], []
    for g in gs:
        blk = (1, g.shape[1] // 2 // nb, g.shape[2])
        in_specs.append(pl.BlockSpec(blk, lambda i, k, p: (k, p[0] * nb + i, 0)))
    for g in gs:
        blk = (1, g.shape[1] // 2 // nb, g.shape[2])
        in_specs.append(pl.BlockSpec(blk, lambda i, k, p: (k, i, 0)))
        out_specs.append(pl.BlockSpec(blk, lambda i, k, p: (k, i, 0)))
    out_shape = _half_stack_shapes(gs, BF16)
    if small:
        sm_spec = pl.BlockSpec(small[0].shape, lambda i, k, p: (0, 0))
        in_specs += [sm_spec, sm_spec]
        out_specs.append(sm_spec)
        out_shape.append(jax.ShapeDtypeStruct(small[0].shape, F32))
    return pl.pallas_call(
        body, name=name,
        grid_spec=pltpu.PrefetchScalarGridSpec(num_scalar_prefetch=1, grid=(nb, N_CHIPS), in_specs=in_specs,
                                               out_specs=out_specs),
        out_shape=out_shape,
        compiler_params=_params(("arbitrary", "arbitrary")),
    )(place, *gs, *rs, *(small or ()))


def _chip_swap_copies(p_refs, ri_refs, send, recv):
    _, _, c, _, _, others = _place()
    n = len(p_refs)
    return [_rcopy(p_refs[a].at[2 * px + py], ri_refs[a].at[j], send, recv, j * n + a, (px, py, c))
            for j, (px, py) in enumerate(others) for a in range(n)]


def _chip_swap_shapes(ps):
    return [jax.ShapeDtypeStruct((3,) + p.shape[1:], p.dtype) for p in ps]


def _chip_swap(ps, pair):
    n = len(ps)
    hs = SMALL_ROWS // 2

    def body(*refs):
        p_refs, pair_ref = refs[:n], refs[n]
        ri_refs, sm4_ref = refs[n + 1:2 * n + 1], refs[2 * n + 1]
        send, recv, lsem = refs[2 * n + 2:]
        x, y, c, me, sib, others = _place()
        local = pltpu.make_async_copy(pair_ref, sm4_ref.at[me], lsem.at[0])
        local.start()
        copies = _chip_swap_copies(p_refs, ri_refs, send, recv)
        arrivals = list(copies)
        for j, (px, py) in enumerate(others):
            copies.append(_rcopy(_half(pair_ref, c, hs), _half(sm4_ref.at[me], c, hs), send, recv, 3 * n + j, (px, py, c)))
            part = _half(sm4_ref.at[2 * px + py], c, hs)
            arrivals.append(_rcopy(part, part, send, recv, 3 * n + j, (px, py, c)))
        for cp in copies:
            cp.start()
        for arrival in arrivals:
            arrival.wait_recv()
        for cp in copies:
            cp.wait_send()
        local.wait()

    k = 3 * (n + 1)
    return pl.pallas_call(
        body, name="chip_swap", in_specs=[ANY] * (n + 1), out_specs=[ANY] * (n + 1),
        out_shape=_chip_swap_shapes(ps) + [jax.ShapeDtypeStruct((N_CHIPS,) + pair.shape, pair.dtype)],
        scratch_shapes=[pltpu.SemaphoreType.DMA((k,)), pltpu.SemaphoreType.DMA((k,)), pltpu.SemaphoreType.DMA((1,))],
    )(*ps, pair)


def _chip_sum(place, gs, rs, ris):
    n = len(gs)
    nb = GRAD_BLOCKS

    def body(place_ref, *refs):
        g_refs, r_refs, ri_refs, o_refs = refs[:n], refs[n:2 * n], refs[2 * n:3 * n], refs[3 * n:]
        for a in range(n):
            ri = ri_refs[a]
            o_refs[a][...] = (g_refs[a][0] + r_refs[a][0]) + ri[0].astype(F32) + ri[1].astype(F32) + ri[2].astype(F32)

    in_specs, out_specs, out_shape = [], [], []
    for g in gs:
        blk = (1, g.shape[1] // 2 // nb, g.shape[2])
        in_specs.append(pl.BlockSpec(blk, lambda i, p: (p[1], p[0] * nb + i, 0)))
    for g in gs:
        blk = (1, g.shape[1] // 2 // nb, g.shape[2])
        in_specs.append(pl.BlockSpec(blk, lambda i, p: (p[1], i, 0)))
    for g in gs:
        rb = g.shape[1] // 2 // nb
        in_specs.append(pl.BlockSpec((3, rb, g.shape[2]), lambda i, p: (0, i, 0)))
        out_specs.append(pl.BlockSpec((rb, g.shape[2]), lambda i, p: (p[0] * nb + i, 0)))
        out_shape.append(jax.ShapeDtypeStruct(g.shape[1:], F32))
    return pl.pallas_call(
        body, name="chip_sum",
        grid_spec=pltpu.PrefetchScalarGridSpec(num_scalar_prefetch=1, grid=(nb,), in_specs=in_specs, out_specs=out_specs),
        out_shape=out_shape,
        compiler_params=_params(("arbitrary",)),
    )(place, *gs, *rs, *ris)


def _pair_fill(gfs, sm4):
    n = len(gfs)
    hs = SMALL_ROWS // 2

    def body(*refs):
        g_refs, sm4_ref = refs[n + 1:2 * n + 1], refs[2 * n + 1]
        send, recv = refs[2 * n + 2:]
        x, y, c, me, sib, others = _place()
        copies, waits = [], []
        for a in range(n):
            h = gfs[a].shape[0] // 2
            mine, theirs = _half(g_refs[a], c, h), _half(g_refs[a], 1 - c, h)
            copies.append(pltpu.make_async_remote_copy(src_ref=mine, dst_ref=mine, send_sem=send.at[a],
                                                       recv_sem=recv.at[a], device_id=sib, device_id_type=MESH))
            waits.append(pltpu.make_async_remote_copy(src_ref=theirs, dst_ref=theirs, send_sem=send.at[a],
                                                      recv_sem=recv.at[a], device_id=sib, device_id_type=MESH))
        for j, (px, py) in enumerate(others):
            chip = 2 * px + py
            mine, theirs = _half(sm4_ref.at[chip], c, hs), _half(sm4_ref.at[chip], 1 - c, hs)
            copies.append(pltpu.make_async_remote_copy(src_ref=mine, dst_ref=mine, send_sem=send.at[n + j],
                                                       recv_sem=recv.at[n + j], device_id=sib, device_id_type=MESH))
            waits.append(pltpu.make_async_remote_copy(src_ref=theirs, dst_ref=theirs, send_sem=send.at[n + j],
                                                      recv_sem=recv.at[n + j], device_id=sib, device_id_type=MESH))
        for cp in copies:
            cp.start()
        for w in waits:
            w.wait_recv()
        for cp in copies:
            cp.wait_send()

    return pl.pallas_call(
        body, name="pair_fill", in_specs=[ANY] * (n + 1), out_specs=[ANY] * (n + 1),
        out_shape=[jax.ShapeDtypeStruct(g.shape, g.dtype) for g in gfs] + [jax.ShapeDtypeStruct(sm4.shape, sm4.dtype)],
        input_output_aliases={i: i for i in range(n + 1)},
        scratch_shapes=[pltpu.SemaphoreType.DMA((n + 3,)), pltpu.SemaphoreType.DMA((n + 3,))],
    )(*gfs, sm4)


def _adamw_math(w, g, m, v):
    m = ADAM_B1 * m + (1.0 - ADAM_B1) * g
    v = ADAM_B2 * v + (1.0 - ADAM_B2) * (g * g)
    m_hat = m / (1.0 - ADAM_B1 ** ADAM_STEP)
    v_hat = v / (1.0 - ADAM_B2 ** ADAM_STEP)
    return -ADAM_LR * (m_hat / (jnp.sqrt(v_hat) + ADAM_EPS) + ADAM_WD * w), m, v


def _adamw(items, steps, name):
    n = len(items)

    def body(*refs):
        for a in range(n):
            d, mo, vo = _adamw_math(*(r[...] for r in refs[4 * a:4 * a + 4]))
            for out, val in zip(refs[4 * n + 3 * a:4 * n + 3 * a + 3], (d, mo, vo)):
                out[...] = val

    spec = lambda w: pl.BlockSpec((w.shape[0] // steps, w.shape[1]), lambda i: (i, 0))
    flat = pl.pallas_call(
        body, name=name, grid=(steps,), in_specs=[spec(it[0]) for it in items for _ in range(4)],
        out_specs=[spec(it[0]) for it in items for _ in range(3)],
        out_shape=[jax.ShapeDtypeStruct(it[0].shape, F32) for it in items for _ in range(3)],
        compiler_params=_params(("arbitrary",)),
    )(*[a for it in items for a in it])
    return [flat[3 * a:3 * a + 3] for a in range(n)]


def _adamw_small(sm4, w, m, v):
    def body(sm4_ref, w_ref, m_ref, v_ref, g_ref, d_ref, mo_ref, vo_ref):
        g = ((sm4_ref[0] + sm4_ref[1]) + sm4_ref[2]) + sm4_ref[3]
        g_ref[...] = g
        d, mo, vo = _adamw_math(w_ref[...], g, m_ref[...], v_ref[...])
        d_ref[...] = d
        mo_ref[...] = mo
        vo_ref[...] = vo

    return pl.pallas_call(
        body, name="adamw_small", out_shape=[jax.ShapeDtypeStruct(w.shape, F32)] * 4,
        compiler_params=pltpu.CompilerParams(vmem_limit_bytes=VMEM_LIMIT),
    )(sm4, w, m, v)


SMALL_NAMES = ("attn_pre_norm", "mla_q_norm", "mla_kv_norm", "mla_w_ukv", "mla_out_norm", "hgrn_lb_logits",
               "hgrn_out_norm", "attn_post_norm", "ffn_pre_norm", "ffn_post_norm")
BIG_NAMES = ("w_in", "mla_w_uq", "w_out", "w_gate", "w_up", "w_down")
WEIGHT_NAMES = ("attn_pre_norm", "w_in", "mla_q_norm", "mla_w_uq", "mla_kv_norm", "mla_w_ukv", "mla_out_norm",
                "hgrn_lb_logits", "hgrn_out_norm", "w_out", "attn_post_norm", "ffn_pre_norm", "w_gate", "w_up", "w_down",
                "ffn_post_norm")


UQ_COMM_SHAPE = (192, 384)


def _pack_small(vals, extra=None):
    parts = [vals[n].reshape(-1) for n in SMALL_NAMES]
    if extra is not None:
        parts.append(extra.reshape(1))
    flat = jnp.concatenate(parts)
    return jnp.pad(flat, (0, SMALL_ROWS * D_MODEL - flat.shape[0])).reshape(SMALL_ROWS, D_MODEL)


def _unpack_small(buf, shapes):
    flat = buf.reshape(-1)
    out, off = {}, 0
    for n, size in zip(SMALL_NAMES, SMALL_SIZES):
        out[n] = flat[off:off + size].reshape(shapes[n])
        off += size
    return out


def kernel(x, positions, attn_pre_norm, w_in, mla_q_norm, mla_w_uq, mla_kv_norm, mla_w_ukv, mla_out_norm, hgrn_lb_logits, hgrn_out_norm, w_out, attn_post_norm, ffn_pre_norm, w_gate, w_up, w_down, ffn_post_norm, loss_target, m_attn_pre_norm, m_w_in, m_mla_q_norm, m_mla_w_uq, m_mla_kv_norm, m_mla_w_ukv, m_mla_out_norm, m_hgrn_lb_logits, m_hgrn_out_norm, m_w_out, m_attn_post_norm, m_ffn_pre_norm, m_w_gate, m_w_up, m_w_down, m_ffn_post_norm, v_attn_pre_norm, v_w_in, v_mla_q_norm, v_mla_w_uq, v_mla_kv_norm, v_mla_w_ukv, v_mla_out_norm, v_hgrn_lb_logits, v_hgrn_out_norm, v_w_out, v_attn_post_norm, v_ffn_pre_norm, v_w_gate, v_w_up, v_w_down, v_ffn_post_norm):
    args = locals()
    W = {n: args[n] for n in WEIGHT_NAMES}
    M = {n: args["m_" + n] for n in WEIGHT_NAMES}
    V = {n: args["v_" + n] for n in WEIGHT_NAMES}
    T = x.shape[1]
    cx, cy, cc = lax.axis_index("x"), lax.axis_index("y"), lax.axis_index("c")

    win_rows = D_IN // N_CHIPS
    shard2d = {"w_in": (win_rows, D_MODEL), "mla_w_uq": (Q_RANK // N_CHIPS, MLA_HEADS * MLA_QK),
               "w_out": (D_MODEL // N_CHIPS, D_MODEL), "w_gate": (FF_SHARD, D_MODEL), "w_up": (FF_SHARD, D_MODEL),
               "w_down": (FF_SHARD, D_MODEL)}
    transposed = ("w_in", "w_gate", "w_up")
    to2d = lambda n, a: a[0].T if n in transposed else a.reshape(shard2d[n])
    from2d = lambda n, t: t.T[None] if n in transposed else t.reshape(W[n].shape)
    me = 2 * cx + cy
    place = jnp.stack([cc, me]).astype(jnp.int32)
    local_b = [to2d(n, W[n]).astype(BF16) for n in BIG_NAMES]
    local_b[0] = jnp.pad(local_b[0], ((0, FF_SHARD - win_rows), (0, 0)))
    stacks = _gather_chips(local_b[:2], "gather_weights")
    win4, wuq4 = [lax.dynamic_update_slice(s, l[None], (me, 0, 0)) for s, l in zip(stacks, local_b)]
    win_t = win4[:, :win_rows].reshape(D_IN, D_MODEL)
    wuq_full = wuq4.reshape(Q_RANK, MLA_HEADS, MLA_QK)
    win_arr, wq_arr, wk_arr, wv_arr = _arrange_weights(win_t, wuq_full, mla_w_ukv[0].astype(BF16))
    small = {n: W[n][0] if n == "mla_w_ukv" else W[n].reshape(-1, W[n].shape[-1]) for n in SMALL_NAMES}

    loss_local, dx, grads, (ffn_rs, ffn_ris) = _local_step(x[0], positions.reshape(T, 1), loss_target[0], small, win_arr,
                                                           wq_arr, wk_arr, wv_arr, local_b[2:], place)

    gs = [grads["w_in"], grads["mla_w_uq"].reshape((N_CHIPS,) + UQ_COMM_SHAPE), grads["w_out"].reshape((N_CHIPS,) + shard2d["w_out"])]
    ffn_gs = [grads["w_gate"], grads["w_up"], grads["w_down"]]
    sm = _pack_small(grads, loss_local)
    *rs, ssib = _pair_swap(gs, sm)
    *ps, pair = _pair_sum(place, gs, rs, small=(sm, ssib))
    *ris, sm4 = _chip_swap(ps, pair)
    gfs = _chip_sum(place, gs + ffn_gs, rs + ffn_rs, ris + ffn_ris)
    *gfin, smf = _pair_fill(gfs, sm4)

    G, DW, NM, NV = {}, {}, {}, {}
    g2d = {n: gfin[k].reshape((-1,) + shard2d[n][1:]) for k, n in enumerate(BIG_NAMES)}
    for names_, steps in ((("w_in", "mla_w_uq"), 3), (("w_out", "w_gate", "w_up", "w_down"), 8)):
        res = _adamw([(to2d(n, W[n]), g2d[n], to2d(n, M[n]), to2d(n, V[n])) for n in names_], steps, "adamw_" + names_[0])
        for n, (d, mo, vo) in zip(names_, res):
            G[n] = from2d(n, g2d[n][:shard2d[n][0]])
            DW[n], NM[n], NV[n] = (from2d(n, t) for t in (d, mo, vo))
    gs_buf, ds, ms, vs = _adamw_small(smf, _pack_small(W), _pack_small(M), _pack_small(V))
    shapes = {n: W[n].shape for n in SMALL_NAMES}
    for dst, buf in ((G, gs_buf), (DW, ds), (NM, ms), (NV, vs)):
        dst.update(_unpack_small(buf, shapes))

    loss = gs_buf.reshape(-1)[sum(SMALL_SIZES)]
    return (loss, dx[None], *[G[n] for n in WEIGHT_NAMES], *[DW[n] for n in WEIGHT_NAMES],
            *[NM[n] for n in WEIGHT_NAMES], *[NV[n] for n in WEIGHT_NAMES])
```

```python
import jax
import jax.numpy as jnp
from jax import lax
from jax.experimental import pallas as pl
from jax.experimental.pallas import tpu as pltpu

F32 = jnp.float32
BF16 = jnp.bfloat16
MXU_DTYPE = BF16

D_MODEL = 1024
MLA_HEADS = 8
MLA_NOPE = 64
MLA_ROPE = 32
MLA_V = 64
MLA_QK = MLA_NOPE + MLA_ROPE
Q_RANK = 384
KV_RANK = 128
MLA_WIDTH = MLA_HEADS * MLA_V
HEAD_PAD = 128
HGRN_HEADS = 4
HGRN_DIM = 128
HGRN_WIDTH = HGRN_HEADS * HGRN_DIM
CHUNK = 64
SUB = 16
HGRN_CPI = 2
D_IN = Q_RANK + KV_RANK + MLA_ROPE + 4 * HGRN_WIDTH
D_IN_ARR = Q_RANK + KV_RANK + HEAD_PAD + 4 * HGRN_WIDTH
D_FF = 2816
N_CHIPS = 4
FF_SHARD = D_FF // N_CHIPS
EPS = 1e-6
ROPE_THETA = 10000.0
ATTN_SCALE = MLA_QK ** -0.5
ATTN_SCALE_LOG2 = ATTN_SCALE * 1.4426950408889634
NEG_BIG = -1e30

ADAM_LR = 0.001
ADAM_B1 = 0.9
ADAM_B2 = 0.999
ADAM_EPS = 1e-08
ADAM_WD = 0.01
ADAM_STEP = 10

VMEM_LIMIT = 56 * 1024 * 1024

SMALL_VIEWS = (("attn_pre_norm", 1, 1024), ("mla_q_norm", 1, 384), ("mla_kv_norm", 1, 128), ("mla_w_ukv", 128, 1024),
               ("mla_out_norm", 1, 512), ("hgrn_lb_logits", 2, 512), ("hgrn_out_norm", 1, 512),
               ("attn_post_norm", 1, 1024), ("ffn_pre_norm", 1, 1024), ("ffn_post_norm", 1, 1024), ("loss", 1, 1))
ROW_TILE = 8
SMALL_ROWS = sum(-(-rows // ROW_TILE) * ROW_TILE for _, rows, _ in SMALL_VIEWS)

MESH = pl.DeviceIdType.MESH
ANY = pl.BlockSpec(memory_space=pl.ANY)


def _dot(a, b, dims, exact):
    if exact:
        return lax.dot_general(a.astype(F32), b.astype(F32), (dims, ((), ())), precision=lax.Precision.HIGH,
                               preferred_element_type=F32)
    return lax.dot_general(a.astype(MXU_DTYPE), b.astype(MXU_DTYPE), (dims, ((), ())), preferred_element_type=F32)


def _mm(a, b, exact=False):
    return _dot(a, b, ((1,), (0,)), exact)


def _mm_nt(a, b, exact=False):
    return _dot(a, b, ((1,), (1,)), exact)


def _mm_tn(a, b, exact=False):
    return _dot(a, b, ((0,), (0,)), exact)


def _rms_fwd(x, w):
    r = lax.rsqrt(jnp.mean(x * x, axis=-1, keepdims=True) + EPS)
    xn = x * r
    return xn * w, xn, r


def _rms_bwd(dy, xn, r, w):
    dxn = dy * w
    dx = r * (dxn - xn * jnp.mean(dxn * xn, axis=-1, keepdims=True))
    dw = jnp.sum(dy * xn, axis=0, keepdims=True)
    return dx, dw


def _group_sums(v, gs):
    t, n = v.shape
    lane = lax.broadcasted_iota(jnp.int32, (t, 128), 1)
    out = []
    for p in range(n // 128):
        vb = v[:, 128 * p:128 * (p + 1)]
        if gs == 128:
            out.append(jnp.sum(vb, axis=-1, keepdims=True))
        else:
            out.append(jnp.sum(jnp.where(lane < 64, vb, 0.0), axis=-1, keepdims=True))
            out.append(jnp.sum(jnp.where(lane >= 64, vb, 0.0), axis=-1, keepdims=True))
    return out


def _group_bcast(sums, gs, t):
    lane = lax.broadcasted_iota(jnp.int32, (t, 128), 1)
    if gs == 128:
        return jnp.concatenate([jnp.broadcast_to(s, (t, 128)) for s in sums], axis=-1)
    return jnp.concatenate([jnp.where(lane < 64, sums[2 * p], sums[2 * p + 1]) for p in range(len(sums) // 2)],
                           axis=-1)


def _grms_fwd(x, w, gs):
    t = x.shape[0]
    r = lax.rsqrt(_group_bcast(_group_sums(x * x, gs), gs, t) * (1.0 / gs) + EPS)
    xn = x * r
    return xn * w, xn, r


def _grms_bwd(dy, xn, r, w, gs):
    t = dy.shape[0]
    dxn = dy * w
    dx = r * (dxn - xn * (_group_bcast(_group_sums(dxn * xn, gs), gs, t) * (1.0 / gs)))
    dw = jnp.sum(dy * xn, axis=0, keepdims=True)
    return dx, dw


def _rope_tables(c_tab, s_tab):
    lane = lax.broadcasted_iota(jnp.int32, c_tab.shape, 1)
    first = (lane >= MLA_NOPE) & (lane < MLA_NOPE + MLA_ROPE // 2)
    second = (lane >= MLA_NOPE + MLA_ROPE // 2) & (lane < MLA_QK)
    return c_tab, jnp.where(first, -s_tab, 0.0), jnp.where(second, s_tab, 0.0)


def _rope(v, c, sa, sb):
    return v * c + pltpu.roll(v, HEAD_PAD - MLA_ROPE // 2, 1) * sa + pltpu.roll(v, MLA_ROPE // 2, 1) * sb


def _rope_bwd(d, c, sa, sb):
    return d * c - pltpu.roll(d, HEAD_PAD - MLA_ROPE // 2, 1) * sa - pltpu.roll(d, MLA_ROPE // 2, 1) * sb


def _params(sem, vmem=VMEM_LIMIT):
    return pltpu.CompilerParams(dimension_semantics=sem, vmem_limit_bytes=vmem)


def _in_fwd(x, pos, invf, w_pre, win, qnw, wq, kvnw, wk, wv, tt=256):
    T = x.shape[0]

    def body(x_ref, pos_ref, invf_ref, wpre_ref, win_ref, qnw_ref, wq_ref, kvnw_ref, wk_ref, wv_ref,
             cq_ref, ckv_ref, xph_ref, q_ref, k_ref, v_ref, kt_ref, vt_ref, rc_ref, rs_ref):
        u, _, _ = _rms_fwd(x_ref[...], wpre_ref[...])
        lo = Q_RANK + KV_RANK + HEAD_PAD
        xp = _mm_nt(u, win_ref[:lo, :])
        xph_ref[...] = _mm_nt(u, win_ref[lo:, :])
        cq = xp[:, :Q_RANK]
        ckv = xp[:, Q_RANK:Q_RANK + KV_RANK]
        kr = xp[:, Q_RANK + KV_RANK:]
        cq_ref[...] = cq
        ckv_ref[...] = ckv
        ang = pos_ref[...].astype(F32) * invf_ref[...]
        c_tab = jnp.cos(ang)
        s_tab = jnp.sin(ang)
        rc_ref[...] = c_tab
        rs_ref[...] = s_tab
        c, sa, sb = _rope_tables(c_tab, s_tab)
        qn, _, _ = _rms_fwd(cq, qnw_ref[...])
        q = _mm(qn, wq_ref[...])
        kvn, _, _ = _rms_fwd(ckv, kvnw_ref[...])
        kn = _mm(kvn, wk_ref[...])
        v = _mm(kvn, wv_ref[...])
        v_ref[...] = v.astype(v_ref.dtype)
        vt_ref[...] = v.T.astype(vt_ref.dtype)
        krr = _rope(kr, c, sa, sb)
        for h in range(MLA_HEADS):
            sl = slice(HEAD_PAD * h, HEAD_PAD * (h + 1))
            q_ref[:, sl] = _rope(q[:, sl], c, sa, sb).astype(q_ref.dtype)
            kh = kn[:, sl] + krr
            k_ref[:, sl] = kh.astype(k_ref.dtype)
            kt_ref[sl, :] = kh.T.astype(kt_ref.dtype)

    row = lambda w: pl.BlockSpec((tt, w), lambda i: (i, 0))
    full = lambda a: pl.BlockSpec(a.shape, lambda i: (0,) * a.ndim)
    qk_w = MLA_HEADS * HEAD_PAD
    return pl.pallas_call(
        body, name="in_fwd", grid=(T // tt,),
        in_specs=[row(D_MODEL), row(1), full(invf), full(w_pre), full(win), full(qnw), full(wq), full(kvnw),
                  full(wk), full(wv)],
        out_specs=[row(Q_RANK), row(KV_RANK), row(4 * HGRN_WIDTH), row(qk_w), row(qk_w), row(MLA_WIDTH),
                   pl.BlockSpec((qk_w, tt), lambda i: (0, i)), pl.BlockSpec((MLA_WIDTH, tt), lambda i: (0, i)),
                   row(HEAD_PAD), row(HEAD_PAD)],
        out_shape=[jax.ShapeDtypeStruct((T, Q_RANK), F32), jax.ShapeDtypeStruct((T, KV_RANK), F32),
                   jax.ShapeDtypeStruct((T, 4 * HGRN_WIDTH), F32), jax.ShapeDtypeStruct((T, qk_w), MXU_DTYPE),
                   jax.ShapeDtypeStruct((T, qk_w), MXU_DTYPE), jax.ShapeDtypeStruct((T, MLA_WIDTH), MXU_DTYPE),
                   jax.ShapeDtypeStruct((qk_w, T), MXU_DTYPE), jax.ShapeDtypeStruct((MLA_WIDTH, T), MXU_DTYPE),
                   jax.ShapeDtypeStruct((T, HEAD_PAD), F32), jax.ShapeDtypeStruct((T, HEAD_PAD), F32)],
        compiler_params=_params(("arbitrary",)),
    )(x, pos, invf, w_pre, win, qnw, wq, kvnw, wk, wv)


def _attn_fwd_t(qb, kb, vt, gather=(), tq=256, hps=8):
    T = qb.shape[0]
    nq = T // tq
    ng = len(gather)
    steps = (MLA_HEADS // hps) * nq
    pass_on = steps - 3

    def body(q_ref, k_ref, vt_ref, *rest):
        o_ref, lse_ref = rest[ng:ng + 2]
        acc_scr = rest[2 * ng + 2]
        qi = pl.program_id(1)
        step_no = pl.program_id(0) * nq + qi
        if ng:
            gat = _Gather(rest[:ng], rest[ng + 2:2 * ng + 2], *rest[2 * ng + 3:])

            @pl.when(step_no == 0)
            def _():
                for cp in gat.sends():
                    cp.start()

            @pl.when(step_no == pass_on)
            def _():
                for arrival in gat.arrivals():
                    arrival.wait_recv()
                for cp in gat.forwards():
                    cp.start()

        heads = [slice(HEAD_PAD * a, HEAD_PAD * (a + 1)) for a in range(hps)]
        acc_scr[...] = jnp.zeros_like(acc_scr)

        def step(j, carry, masked):
            start = pl.multiple_of(j * tq, tq)
            scores = [_mm_nt(k_ref[pl.ds(start, tq), heads[a]], q_ref[:, heads[a]]) for a in range(hps)]
            new = []
            for a in range(hps):
                m, l = carry[a]
                s = scores[a] * ATTN_SCALE_LOG2
                if masked:
                    kk = lax.broadcasted_iota(jnp.int32, (tq, tq), 0)
                    qq = lax.broadcasted_iota(jnp.int32, (tq, tq), 1)
                    s = jnp.where(kk <= qq, s, NEG_BIG)
                m_new = jnp.maximum(m, jnp.max(s, axis=0, keepdims=True))
                alpha = jnp.exp2(m - m_new)
                p = jnp.exp2(s - m_new)
                l = l * alpha + jnp.sum(p, axis=0, keepdims=True)
                vtj = vt_ref[2 * MLA_V * (a // 2):2 * MLA_V * (a // 2 + 1), pl.ds(start, tq)]
                acc_scr[a] = acc_scr[a] * alpha + _mm(vtj, p)
                new.append((m_new, l))
            return tuple(new)

        init = tuple((jnp.full((1, tq), NEG_BIG, F32), jnp.zeros((1, tq), F32)) for _ in range(hps))
        carry = lax.fori_loop(0, qi, lambda j, c: step(j, c, False), init)
        carry = step(qi, carry, True)
        row = lax.broadcasted_iota(jnp.int32, (2 * MLA_V, tq), 0)
        for pr in range(hps // 2):
            (m0, l0), (m1, l1) = carry[2 * pr], carry[2 * pr + 1]
            ot = jnp.where(row < MLA_V, acc_scr[2 * pr] / l0, acc_scr[2 * pr + 1] / l1)
            o_ref[:, 2 * MLA_V * pr:2 * MLA_V * (pr + 1)] = ot.T
            lse_ref[pr, 0:1, :] = m0 + jnp.log2(l0)
            lse_ref[pr, 1:2, :] = m1 + jnp.log2(l1)

        if ng:
            @pl.when(step_no == steps - 1)
            def _():
                for arrival in gat.forward_arrivals():
                    arrival.wait_recv()
                for cp in gat.sends() + gat.forwards():
                    cp.wait_send()

    return pl.pallas_call(
        body, name="attn_fwd", grid=(MLA_HEADS // hps, nq),
        in_specs=[pl.BlockSpec((tq, hps * HEAD_PAD), lambda g, i: (i, g)),
                  pl.BlockSpec((T, hps * HEAD_PAD), lambda g, i: (0, g)),
                  pl.BlockSpec((hps * MLA_V, T), lambda g, i: (g, 0))] + [ANY] * ng,
        out_specs=[pl.BlockSpec((tq, hps * MLA_V), lambda g, i: (i, g)),
                   pl.BlockSpec((hps // 2, 2, tq), lambda g, i: (g, 0, i))] + [ANY] * ng,
        out_shape=[jax.ShapeDtypeStruct((T, MLA_WIDTH), F32), jax.ShapeDtypeStruct((MLA_HEADS // 2, 2, T), F32)]
        + _Gather.out_shapes(gather),
        scratch_shapes=[pltpu.VMEM((hps, 2 * MLA_V, tq), F32)] + (_Gather.semaphores(gather) if ng else []),
        compiler_params=_params(("arbitrary", "arbitrary")),
    )(qb, kb, vt, *gather)


def _attn_bwd_t(qb, kb, kt, vb, dob, lse, dvec, send=(), tq=256, hps=4):
    T = qb.shape[0]
    nq = T // tq
    ns = len(send)
    steps = (MLA_HEADS // hps) * nq

    def body(q_ref, k_ref, kt_ref, v_ref, do_ref, lse_ref, d_ref, *rest):
        dqt_ref, dk_ref, dv_ref = rest[ns:ns + 3]
        va_scr, dv_scr = rest[2 * ns + 3:2 * ns + 5]
        j = pl.program_id(1)
        step_no = pl.program_id(0) * nq + j
        if ns:
            @pl.when(step_no == 0)
            def _():
                for cp in _chip_swap_copies(rest[:ns], rest[ns + 3:2 * ns + 3], *rest[2 * ns + 5:]):
                    cp.start()

        @pl.when(j == 0)
        def _():
            dqt_ref[...] = jnp.zeros_like(dqt_ref)

        lane = lax.broadcasted_iota(jnp.int32, (tq, 2 * MLA_V), 1)
        heads = [slice(HEAD_PAD * a, HEAD_PAD * (a + 1)) for a in range(hps)]
        pairs = [slice(2 * MLA_V * p, 2 * MLA_V * (p + 1)) for p in range(hps // 2)]
        for pr in range(hps // 2):
            vpair = v_ref[:, pairs[pr]]
            va_scr[2 * pr] = jnp.where(lane < MLA_V, vpair, jnp.zeros_like(vpair))
            va_scr[2 * pr + 1] = jnp.where(lane >= MLA_V, vpair, jnp.zeros_like(vpair))
        dk_ref[...] = jnp.zeros_like(dk_ref)
        dv_scr[...] = jnp.zeros_like(dv_scr)

        def step(i, masked):
            start = pl.multiple_of(i * tq, tq)
            rows = pl.ds(start, tq)
            scores = [_mm_nt(k_ref[:, heads[a]], q_ref[rows, heads[a]]) for a in range(hps)]
            dps = [_mm_nt(va_scr[a], do_ref[rows, pairs[a // 2]]) for a in range(hps)]
            for a in range(hps):
                pr, r = a // 2, a % 2
                p = jnp.exp2(scores[a] * ATTN_SCALE_LOG2 - lse_ref[pr, r:r + 1, rows])
                if masked:
                    kk = lax.broadcasted_iota(jnp.int32, (tq, tq), 0)
                    qq = lax.broadcasted_iota(jnp.int32, (tq, tq), 1)
                    p = jnp.where(kk <= qq, p, 0.0)
                ds = p * (dps[a] - d_ref[pr, r:r + 1, rows]) * ATTN_SCALE
                dv_scr[a] += _mm(p, do_ref[rows, pairs[pr]])
                dk_ref[:, heads[a]] += _mm(ds, q_ref[rows, heads[a]])
                dqt_ref[heads[a], rows] += _mm(kt_ref[heads[a], :], ds)

        def loop_body(i, _):
            step(i, False)
            return 0

        step(j, True)
        lax.fori_loop(j + 1, nq, loop_body, 0)
        for pr in range(hps // 2):
            dv_ref[:, pairs[pr]] = jnp.where(lane < MLA_V, dv_scr[2 * pr], dv_scr[2 * pr + 1])

        if ns:
            @pl.when(step_no == steps - 1)
            def _():
                for cp in _chip_swap_copies(rest[:ns], rest[ns + 3:2 * ns + 3], *rest[2 * ns + 5:]):
                    cp.wait()

    stat = pl.BlockSpec((hps // 2, 2, T), lambda g, j: (g, 0, 0))
    return pl.pallas_call(
        body, name="attn_bwd", grid=(MLA_HEADS // hps, nq),
        in_specs=[pl.BlockSpec((T, hps * HEAD_PAD), lambda g, j: (0, g)),
                  pl.BlockSpec((tq, hps * HEAD_PAD), lambda g, j: (j, g)),
                  pl.BlockSpec((hps * HEAD_PAD, tq), lambda g, j: (g, j)),
                  pl.BlockSpec((tq, hps * MLA_V), lambda g, j: (j, g)),
                  pl.BlockSpec((T, hps * MLA_V), lambda g, j: (0, g)), stat, stat] + [ANY] * ns,
        out_specs=[pl.BlockSpec((hps * HEAD_PAD, T), lambda g, j: (g, 0)),
                   pl.BlockSpec((tq, hps * HEAD_PAD), lambda g, j: (j, g)),
                   pl.BlockSpec((tq, hps * MLA_V), lambda g, j: (j, g))] + [ANY] * ns,
        out_shape=[jax.ShapeDtypeStruct((MLA_HEADS * HEAD_PAD, T), F32),
                   jax.ShapeDtypeStruct((T, MLA_HEADS * HEAD_PAD), F32),
                   jax.ShapeDtypeStruct((T, MLA_WIDTH), F32)] + _chip_swap_shapes(send),
        scratch_shapes=[pltpu.VMEM((hps, tq, 2 * MLA_V), vb.dtype), pltpu.VMEM((hps, tq, 2 * MLA_V), F32)]
        + ([pltpu.SemaphoreType.DMA((3 * ns,)), pltpu.SemaphoreType.DMA((3 * ns,))] if ns else []),
        compiler_params=_params(("arbitrary", "arbitrary")),
    )(qb, kb, kt, vb, dob, lse, dvec, *send)


def _cumsum_rows(x):
    n = x.shape[0]
    row = lax.broadcasted_iota(jnp.int32, x.shape, 0)
    s = 1
    while s < n:
        x = x + jnp.where(row >= s, pltpu.roll(x, s, 0), 0.0)
        s *= 2
    return x


def _rev_cumsum_rows(x):
    n = x.shape[0]
    row = lax.broadcasted_iota(jnp.int32, x.shape, 0)
    s = 1
    while s < n:
        x = x + jnp.where(row < n - s, pltpu.roll(x, n - s, 0), 0.0)
        s *= 2
    return x


def _lb_from_logits(l):
    l0, l1 = l[0:1, :], l[1:2, :]
    m = jnp.maximum(l0, l1)
    e0, e1 = jnp.exp(l0 - m), jnp.exp(l1 - m)
    return e0 / (e0 + e1)


def _hgrn_gates(hq, hf, lb):
    sig_f = jax.nn.sigmoid(hf)
    f = lb + (1.0 - lb) * sig_f
    sig_q = jax.nn.sigmoid(hq)
    return sig_f, f, jnp.log(f), 1.0 - f, sig_q, hq * sig_q


def _hgrn_intra(q, kk, b, exact=False):
    row = lax.broadcasted_iota(jnp.int32, b.shape, 0)
    qs, ks, eqs, eks, a_rows = [], [], [], [], []
    for i in range(CHUNK // SUB):
        ref = b[SUB * i + SUB // 2:SUB * i + SUB // 2 + 1, :]
        eq = jnp.exp(b[SUB * i:SUB * (i + 1), :] - ref)
        ek = jnp.exp(jnp.where(row < SUB * (i + 1), ref - b, NEG_BIG))
        qi = q[SUB * i:SUB * (i + 1), :] * eq
        ki = kk * ek
        a_rows.append(_mm_nt(qi, ki, exact))
        qs.append(qi), ks.append(ki), eqs.append(eq), eks.append(ek)
    tt = lax.broadcasted_iota(jnp.int32, (CHUNK, CHUNK), 0)
    ss = lax.broadcasted_iota(jnp.int32, (CHUNK, CHUNK), 1)
    causal = ss <= tt
    a = jnp.where(causal, jnp.concatenate(a_rows, axis=0), 0.0)
    return a, causal, qs, ks, eqs, eks


def _hgrn_fwd(xph, lbl, tg=512):
    T = xph.shape[0]
    ng, ncg = T // tg, tg // CHUNK
    cols = [slice(HGRN_DIM * h, HGRN_DIM * (h + 1)) for h in range(HGRN_HEADS)]

    def body(lbl_ref, hq_ref, hf_ref, hi_ref, o_ref, st_ref, s_scr):
        @pl.when(pl.program_id(0) == 0)
        def _():
            s_scr[...] = jnp.zeros_like(s_scr)

        lb = _lb_from_logits(lbl_ref[...])

        def chunks(it, _):
            pre = []
            for k in range(HGRN_CPI):
                c = it * HGRN_CPI + k
                rows = pl.ds(pl.multiple_of(c * CHUNK, CHUNK), CHUNK)
                for cs in cols:
                    _, _, lf, kk, _, q = _hgrn_gates(hq_ref[rows, cs], hf_ref[rows, cs], lb[:, cs])
                    v = hi_ref[rows, cs]
                    b = _cumsum_rows(lf)
                    a = _hgrn_intra(q, kk, b)[0]
                    b_last = b[CHUNK - 1:CHUNK, :]
                    pre.append((c, rows, q * jnp.exp(b), a, v, jnp.exp(b_last), _mm_tn(v, kk * jnp.exp(b_last - b))))
            for i, (c, rows, qe, a, v, ebl, upd) in enumerate(pre):
                h = i % HGRN_HEADS
                st = s_scr[h]
                st_ref[h, c] = st
                o_ref[rows, cols[h]] = _mm_nt(qe, st) + _mm(a, v)
                s_scr[h] = st * ebl + upd
            return 0

        lax.fori_loop(0, ncg // HGRN_CPI, chunks, 0)

    col = lambda k: pl.BlockSpec((tg, HGRN_WIDTH), lambda g: (g, k))
    return pl.pallas_call(
        body, name="hgrn_fwd", grid=(ng,),
        in_specs=[pl.BlockSpec((2, HGRN_WIDTH), lambda g: (0, 0)), col(0), col(1), col(2)],
        out_specs=[col(0), pl.BlockSpec((HGRN_HEADS, ncg, HGRN_DIM, HGRN_DIM), lambda g: (0, g, 0, 0))],
        out_shape=[jax.ShapeDtypeStruct((T, HGRN_WIDTH), F32),
                   jax.ShapeDtypeStruct((HGRN_HEADS, T // CHUNK, HGRN_DIM, HGRN_DIM), F32)],
        scratch_shapes=[pltpu.VMEM((HGRN_HEADS, HGRN_DIM, HGRN_DIM), F32)],
        compiler_params=_params(("arbitrary",)),
    )(lbl, xph, xph, xph)


def _hgrn_bwd(xph, lbl, states, d_o, tg=512):
    T = xph.shape[0]
    ng, ncg = T // tg, tg // CHUNK
    cols = [slice(HGRN_DIM * h, HGRN_DIM * (h + 1)) for h in range(HGRN_HEADS)]
    nsub = CHUNK // SUB

    def body(lbl_ref, hq_ref, hf_ref, hi_ref, st_ref, do_ref, dhq_ref, dhf_ref, dhi_ref, dlg_ref, ds_scr, dlb_scr):
        g = pl.program_id(0)

        @pl.when(g == 0)
        def _():
            ds_scr[...] = jnp.zeros_like(ds_scr)
            dlb_scr[...] = jnp.zeros_like(dlb_scr)

        lb = _lb_from_logits(lbl_ref[...])

        def chunks(it, _):
            pre = []
            for k, h in ((k, h) for k in range(HGRN_CPI) for h in range(HGRN_HEADS)):
                cs = cols[h]
                c = ncg - 1 - (it * HGRN_CPI + k)
                rows = pl.ds(pl.multiple_of(c * CHUNK, CHUNK), CHUNK)
                hq = hq_ref[rows, cs]
                sig_f, f, lf, kk, sig_q, q = _hgrn_gates(hq, hf_ref[rows, cs], lb[:, cs])
                v = hi_ref[rows, cs]
                do = do_ref[rows, cs]
                b = _cumsum_rows(lf)
                eb = jnp.exp(b)
                a, causal, qs, ks, eqs, eks = _hgrn_intra(q, kk, b)
                b_last = b[CHUNK - 1:CHUNK, :]
                st = st_ref[h, c]
                pre.append(dict(h=h, cs=cs, rows=rows, hq=hq, sig_f=sig_f, f=f, kk=kk, sig_q=sig_q, q=q, v=v, eb=eb, qs=qs,
                                ks=ks, eqs=eqs,
                                eks=eks, ebl=jnp.exp(b_last), el=jnp.exp(b_last - b), st=st,
                                da=jnp.where(causal, _mm_nt(do, v, True), 0.0), dq=_mm(do, st, True) * eb,
                                dv=_mm_tn(a, do), dsu=_mm_tn(do, q * eb, True)))
            for w in pre:
                dq_rows = []
                dk = jnp.zeros_like(w["q"])
                for i in range(nsub):
                    dai = w["da"][SUB * i:SUB * (i + 1), :]
                    dq_rows.append(_mm(dai, w["ks"][i], True) * w["eqs"][i])
                    dk = dk + _mm_tn(dai, w["qs"][i], True) * w["eks"][i]
                w["dq"] = w["dq"] + jnp.concatenate(dq_rows, axis=0)
                w["dk"] = dk
            for w in pre:
                h, cs, rows = w["h"], w["cs"], w["rows"]
                kk, el, ebl, dst = w["kk"], w["el"], w["ebl"], ds_scr[h]
                dk_state = _mm(w["v"], dst, True) * el
                dk = w["dk"] + dk_state
                e_last = (ebl * jnp.sum(w["st"] * dst, axis=0, keepdims=True)
                          + jnp.sum(kk * dk_state, axis=0, keepdims=True))
                dlf = _rev_cumsum_rows(w["q"] * w["dq"] - kk * dk) + e_last
                ds_scr[h] = dst * ebl + w["dsu"]
                df = dlf / w["f"] - dk
                sig_f, sig_q = w["sig_f"], w["sig_q"]
                dhf_ref[rows, cs] = df * (1.0 - lb[:, cs]) * sig_f * (1.0 - sig_f)
                dlb_scr[:, cs] += jnp.sum(df * (1.0 - sig_f), axis=0, keepdims=True)
                dhq_ref[rows, cs] = w["dq"] * sig_q * (1.0 + w["hq"] * (1.0 - sig_q))
                dhi_ref[rows, cs] = w["dv"] + _mm_nt(kk * el, dst)
            return 0

        lax.fori_loop(0, ncg // HGRN_CPI, chunks, 0)

        @pl.when(g == ng - 1)
        def _():
            dl0 = dlb_scr[...] * lb * (1.0 - lb)
            dlg_ref[...] = jnp.concatenate([dl0, -dl0], axis=0)

    col = lambda k: pl.BlockSpec((tg, HGRN_WIDTH), lambda g: (ng - 1 - g, k))
    logits = pl.BlockSpec((2, HGRN_WIDTH), lambda g: (0, 0))
    big = jax.ShapeDtypeStruct((T, HGRN_WIDTH), F32)
    return pl.pallas_call(
        body, name="hgrn_bwd", grid=(ng,),
        in_specs=[logits, col(0), col(1), col(2),
                  pl.BlockSpec((HGRN_HEADS, ncg, HGRN_DIM, HGRN_DIM), lambda g: (0, ng - 1 - g, 0, 0)), col(0)],
        out_specs=[col(0), col(0), col(0), logits],
        out_shape=[big, big, big, jax.ShapeDtypeStruct((2, HGRN_WIDTH), F32)],
        scratch_shapes=[pltpu.VMEM((HGRN_HEADS, HGRN_DIM, HGRN_DIM), F32), pltpu.VMEM((1, HGRN_WIDTH), F32)],
        compiler_params=_params(("arbitrary",)),
    )(lbl, xph, xph, xph, states, d_o)


def _proj_fwd(x, o_raw, oh_raw, xph, wout, w_mla, w_hg, w_post, w_fpre, tt=512):
    T = x.shape[0]

    def body(x_ref, o_ref, oh_ref, hg_ref, wout_ref, wmla_ref, whg_ref, wpost_ref, wfpre_ref,
             h1_ref, y1_ref, z_ref, mix_ref):
        om, _, _ = _grms_fwd(o_ref[...], wmla_ref[...], MLA_V)
        hg = hg_ref[...]
        ohn, _, _ = _grms_fwd(oh_ref[...], whg_ref[...], HGRN_DIM)
        mix = jnp.concatenate([om, ohn * (hg * jax.nn.sigmoid(hg))], axis=-1)
        mix_ref[...] = mix.astype(mix_ref.dtype)
        y1 = _mm(mix, wout_ref[...])
        y1_ref[...] = y1
        h1 = x_ref[...] + _rms_fwd(y1, wpost_ref[...])[0]
        h1_ref[...] = h1
        z_ref[...] = _rms_fwd(h1, wfpre_ref[...])[0].astype(z_ref.dtype)

    row = lambda w: pl.BlockSpec((tt, w), lambda i: (i, 0))
    full = lambda a: pl.BlockSpec(a.shape, lambda i: (0,) * a.ndim)
    sds = jax.ShapeDtypeStruct
    return pl.pallas_call(
        body, name="proj_fwd", grid=(T // tt,),
        in_specs=[row(D_MODEL), row(MLA_WIDTH), row(HGRN_WIDTH), pl.BlockSpec((tt, HGRN_WIDTH), lambda i: (i, 3)),
                  full(wout), full(w_mla), full(w_hg), full(w_post), full(w_fpre)],
        out_specs=[row(D_MODEL)] * 4,
        out_shape=[sds((T, D_MODEL), F32), sds((T, D_MODEL), F32), sds((T, D_MODEL), MXU_DTYPE),
                   sds((T, D_MODEL), MXU_DTYPE)],
        compiler_params=_params(("arbitrary",)),
    )(x, o_raw, oh_raw, xph, wout, w_mla, w_hg, w_post, w_fpre)


def _ffn_fwd(zb, h1, tgt, w_fpost, wg, wu, wd, tt=256):
    T = zb.shape[0]
    nj = N_CHIPS

    def body(z_ref, h1_ref, tgt_ref, wfpost_ref, wg_ref, wu_ref, wd_ref, g_ref, up_ref, dy2_ref, dh2_ref, loss_ref, dwf_ref):
        @pl.when(pl.program_id(0) == 0)
        def _():
            loss_ref[...] = jnp.zeros_like(loss_ref)
            dwf_ref[...] = jnp.zeros_like(dwf_ref)

        z = z_ref[...]
        gs = [_mm_nt(z, wg_ref[j]) for j in range(nj)]
        ups = [_mm_nt(z, wu_ref[j]) for j in range(nj)]
        y2 = jnp.zeros((tt, D_MODEL), F32)
        for j in range(nj):
            g_ref[j] = gs[j]
            up_ref[j] = ups[j]
            y2 = y2 + _mm(gs[j] * jax.nn.sigmoid(gs[j]) * ups[j], wd_ref[j])
        w = wfpost_ref[...]
        y2s, y2n, r2 = _rms_fwd(y2, w)
        e = h1_ref[...] + y2s - tgt_ref[...]
        loss_ref[...] += jnp.sum(e * e, axis=0, keepdims=True)
        dh2 = e * (1.0 / D_MODEL)
        dh2_ref[...] = dh2
        dy2, dwf = _rms_bwd(dh2, y2n, r2, w)
        dy2_ref[...] = dy2.astype(dy2_ref.dtype)
        dwf_ref[...] += dwf

    row = pl.BlockSpec((tt, D_MODEL), lambda i: (i, 0))
    vec = pl.BlockSpec((1, D_MODEL), lambda i: (0, 0))
    resident = pl.BlockSpec((nj, FF_SHARD, D_MODEL), lambda i: (0, 0, 0), pipeline_mode=pl.Buffered(1))
    act = pl.BlockSpec((nj, tt, FF_SHARD), lambda i: (0, i, 0))
    sds = jax.ShapeDtypeStruct
    return pl.pallas_call(
        body, name="ffn_fwd", grid=(T // tt,),
        in_specs=[row, row, row, vec, resident, resident, resident],
        out_specs=[act, act, row, row, vec, vec],
        out_shape=[sds((nj, T, FF_SHARD), F32), sds((nj, T, FF_SHARD), F32), sds((T, D_MODEL), MXU_DTYPE),
                   sds((T, D_MODEL), F32), sds((1, D_MODEL), F32), sds((1, D_MODEL), F32)],
        compiler_params=_params(("arbitrary",)),
    )(zb, h1, tgt, w_fpost, wg, wu, wd)


def _ffn_bwd(zb, g, up, dy2b, wg, wu, wd, tt=512):
    T = zb.shape[0]
    nj = N_CHIPS

    def body(z_ref, g_ref, up_ref, dy2_ref, wg_ref, wu_ref, wd_ref, dwg_ref, dwu_ref, dwd_ref, dz_ref):
        @pl.when(pl.program_id(1) == 0)
        def _():
            dwg_ref[...] = jnp.zeros_like(dwg_ref)
            dwu_ref[...] = jnp.zeros_like(dwu_ref)
            dwd_ref[...] = jnp.zeros_like(dwd_ref)

        z, g_, up_, dy2 = z_ref[...], g_ref[0], up_ref[0], dy2_ref[...]
        sg = jax.nn.sigmoid(g_)
        act = g_ * sg
        dff = _mm_nt(dy2, wd_ref[0])
        dwd_ref[0] += _mm_tn(act * up_, dy2)
        dg = dff * up_ * sg * (1.0 + g_ * (1.0 - sg))
        dup = dff * act
        dwg_ref[0] += _mm_tn(dg, z)
        dwu_ref[0] += _mm_tn(dup, z)
        dz_ref[0] = _mm(dg, wg_ref[0]) + _mm(dup, wu_ref[0])

    row = pl.BlockSpec((tt, D_MODEL), lambda j, i: (i, 0))
    act = pl.BlockSpec((1, tt, FF_SHARD), lambda j, i: (j, i, 0))
    w_sh = pl.BlockSpec((1, FF_SHARD, D_MODEL), lambda j, i: (j, 0, 0))
    w_grad = jax.ShapeDtypeStruct((nj, FF_SHARD, D_MODEL), F32)
    return pl.pallas_call(
        body, name="ffn_bwd", grid=(nj, T // tt),
        in_specs=[row, act, act, row, w_sh, w_sh, w_sh],
        out_specs=[w_sh, w_sh, w_sh, pl.BlockSpec((1, tt, D_MODEL), lambda j, i: (j, i, 0))],
        out_shape=[w_grad, w_grad, w_grad, jax.ShapeDtypeStruct((nj, T, D_MODEL), F32)],
        compiler_params=_params(("arbitrary", "arbitrary")),
    )(zb, g, up, dy2b, wg, wu, wd)


def _mid_bwd(dzp, dh2, h1, y1, mixb, o_raw, oh_raw, xph, wout, w_fpre, w_post, w_mla, w_hg, swap=(), tt=256):
    T = dh2.shape[0]
    nsw = len(swap)
    n_in, n_out = 13, 10

    def body(*refs):
        (dzp_ref, dh2_ref, h1_ref, y1_ref, mix_ref, o_ref, oh_ref, hg_ref, wout_ref, wfpre_ref, wpost_ref,
         wmla_ref, whg_ref) = refs[:n_in]
        (dh1_ref, dwout_ref, do_ref, doh_ref, dhg_ref, dvec_ref, dwfpre_ref, dwpost_ref, dwmla_ref,
         dwhg_ref) = refs[n_in + nsw:n_in + nsw + n_out]
        swap_copies = lambda: _pair_swap_copies(refs[n_in:n_in + nsw], refs[n_in + nsw + n_out:n_in + 2 * nsw + n_out],
                                                *refs[n_in + 2 * nsw + n_out:])

        @pl.when(pl.program_id(0) == 0)
        def _():
            for r in (dwout_ref, dwfpre_ref, dwpost_ref, dwmla_ref, dwhg_ref):
                r[...] = jnp.zeros_like(r)
            for cp in (swap_copies() if nsw else ()):
                cp.start()

        dz = dzp_ref[0] + dzp_ref[1] + dzp_ref[2] + dzp_ref[3]
        wfpre = wfpre_ref[...]
        _, h1n, r = _rms_fwd(h1_ref[...], wfpre)
        dh1_z, dwfpre = _rms_bwd(dz, h1n, r, wfpre)
        dwfpre_ref[...] += dwfpre
        dh1 = dh2_ref[...] + dh1_z
        dh1_ref[...] = dh1
        wpost = wpost_ref[...]
        _, y1n, r1 = _rms_fwd(y1_ref[...], wpost)
        dy1, dwpost = _rms_bwd(dh1, y1n, r1, wpost)
        dwpost_ref[...] += dwpost
        dmix = _mm_nt(dy1, wout_ref[...])
        dwout_ref[...] += _mm_tn(mix_ref[...], dy1)
        wmla = wmla_ref[...]
        o = o_ref[...]
        _, on, ro = _grms_fwd(o, wmla, MLA_V)
        d_o, dwmla = _grms_bwd(dmix[:, :MLA_WIDTH], on, ro, wmla, MLA_V)
        dwmla_ref[...] += dwmla
        do_ref[...] = d_o.astype(do_ref.dtype)
        hh = lax.broadcasted_iota(jnp.int32, (MLA_HEADS, MLA_WIDTH), 0)
        ll = lax.broadcasted_iota(jnp.int32, (MLA_HEADS, MLA_WIDTH), 1)
        sel = jnp.where((ll >= hh * MLA_V) & (ll < (hh + 1) * MLA_V), 1.0, 0.0)
        dvec_ref[...] = _mm_nt(sel, d_o * o, True)
        whg = whg_ref[...]
        hg = hg_ref[...]
        sg = jax.nn.sigmoid(hg)
        _, ohn, rh = _grms_fwd(oh_ref[...], whg, HGRN_DIM)
        dmh = dmix[:, MLA_WIDTH:]
        dhg_ref[...] = dmh * ohn * whg * sg * (1.0 + hg * (1.0 - sg))
        d_oh, dwhg = _grms_bwd(dmh * (hg * sg), ohn, rh, whg, HGRN_DIM)
        dwhg_ref[...] += dwhg
        doh_ref[...] = d_oh

        if nsw:
            @pl.when(pl.program_id(0) == T // tt - 1)
            def _():
                for cp in swap_copies():
                    cp.wait()

    row = lambda w: pl.BlockSpec((tt, w), lambda i: (i, 0))
    full = lambda a: pl.BlockSpec(a.shape, lambda i: (0,) * a.ndim)
    vec = lambda w: pl.BlockSpec((1, w), lambda i: (0, 0))
    sds = jax.ShapeDtypeStruct
    return pl.pallas_call(
        body, name="mid_bwd", grid=(T // tt,),
        in_specs=[pl.BlockSpec((N_CHIPS, tt, D_MODEL), lambda i: (0, i, 0)), row(D_MODEL), row(D_MODEL), row(D_MODEL),
                  row(D_MODEL), row(MLA_WIDTH), row(HGRN_WIDTH), pl.BlockSpec((tt, HGRN_WIDTH), lambda i: (i, 3)),
                  full(wout), vec(D_MODEL), vec(D_MODEL), vec(MLA_WIDTH), vec(HGRN_WIDTH)] + [ANY] * nsw,
        out_specs=[row(D_MODEL), full(wout), row(MLA_WIDTH), row(HGRN_WIDTH), row(HGRN_WIDTH),
                   pl.BlockSpec((MLA_HEADS, tt), lambda i: (0, i)),
                   vec(D_MODEL), vec(D_MODEL), vec(MLA_WIDTH), vec(HGRN_WIDTH)] + [ANY] * nsw,
        out_shape=[sds((T, D_MODEL), F32), sds(wout.shape, F32), sds((T, MLA_WIDTH), MXU_DTYPE), sds((T, HGRN_WIDTH), F32),
                   sds((T, HGRN_WIDTH), F32), sds((MLA_HEADS, T), F32),
                   sds((1, D_MODEL), F32), sds((1, D_MODEL), F32), sds((1, MLA_WIDTH), F32), sds((1, HGRN_WIDTH), F32)]
        + _half_stack_shapes(swap),
        scratch_shapes=[pltpu.SemaphoreType.DMA((nsw,)), pltpu.SemaphoreType.DMA((nsw,))] if nsw else [],
        compiler_params=_params(("arbitrary",)),
    )(dzp, dh2, h1, y1, mixb, o_raw, oh_raw, xph, wout, w_fpre, w_post, w_mla, w_hg, *swap)


def _in_bwd(x, dh1, cq, ckv, dq, dk, dv, dhq, dhf, dhi, dhg, rc, rs, w_pre, win, qnw, wq, kvnw, wk, wv, tt=256):
    T = x.shape[0]

    def body(x_ref, dh1_ref, cq_ref, ckv_ref, dq_ref, dk_ref, dv_ref, dhq_ref, dhf_ref, dhi_ref, dhg_ref, rc_ref, rs_ref,
             wpre_ref, win_ref, qnw_ref, wq_ref, kvnw_ref, wk_ref, wv_ref,
             dx_ref, dwin_ref, dwq_ref, dwk_ref, dwv_ref, dwpre_ref, dqnw_ref, dkvnw_ref):
        @pl.when(pl.program_id(0) == 0)
        def _():
            for r in (dwin_ref, dwq_ref, dwk_ref, dwv_ref, dwpre_ref, dqnw_ref, dkvnw_ref):
                r[...] = jnp.zeros_like(r)

        def add_win_grad(r, first):
            for arr0, n, chip, row0 in _win_grad_segments():
                if first <= arr0 and arr0 + n <= first + r.shape[0]:
                    dwin_ref[chip, row0:row0 + n, :] += r[arr0 - first:arr0 - first + n]

        lo = Q_RANK + KV_RANK + HEAD_PAD
        wpre = wpre_ref[...]
        u, xn, rx = _rms_fwd(x_ref[...], wpre)
        dxp_h = jnp.concatenate([dhq_ref[...], dhf_ref[...], dhi_ref[...], dhg_ref[...]], axis=-1)
        add_win_grad(_mm_tn(dxp_h, u), lo)
        du = _mm(dxp_h, win_ref[lo:, :])
        c, sa, sb = _rope_tables(rc_ref[...], rs_ref[...])
        lane = lax.broadcasted_iota(jnp.int32, (tt, HEAD_PAD), 1)
        dk_all = dk_ref[...]
        dq_lin = []
        dkr = jnp.zeros((tt, HEAD_PAD), F32)
        for h in range(MLA_HEADS):
            sl = slice(HEAD_PAD * h, HEAD_PAD * (h + 1))
            dq_lin.append(_rope_bwd(dq_ref[sl, :].T, c, sa, sb))
            dkr = dkr + dk_all[:, sl]
        dq_lin = jnp.concatenate(dq_lin, axis=-1)
        dkr = jnp.where((lane >= MLA_NOPE) & (lane < MLA_QK), _rope_bwd(dkr, c, sa, sb), 0.0)
        qnw = qnw_ref[...]
        qn, cqn, rq = _rms_fwd(cq_ref[...], qnw)
        dwq_ref[...] += _mm_tn(qn, dq_lin)
        dcq, dqnw = _rms_bwd(_mm_nt(dq_lin, wq_ref[...]), cqn, rq, qnw)
        dqnw_ref[...] += dqnw
        kvnw = kvnw_ref[...]
        kvn, ckvn, rkv = _rms_fwd(ckv_ref[...], kvnw)
        dv_ = dv_ref[...]
        dwk_ref[...] += _mm_tn(kvn, dk_all)
        dwv_ref[...] += _mm_tn(kvn, dv_)
        dckv, dkvnw = _rms_bwd(_mm_nt(dk_all, wk_ref[...]) + _mm_nt(dv_, wv_ref[...]), ckvn, rkv, kvnw)
        dkvnw_ref[...] += dkvnw
        dxp_a = jnp.concatenate([dcq, dckv, dkr], axis=-1)
        add_win_grad(_mm_tn(dxp_a, u), 0)
        dx_u, dwpre = _rms_bwd(du + _mm(dxp_a, win_ref[:lo, :]), xn, rx, wpre)
        dwpre_ref[...] += dwpre
        dx_ref[...] = dh1_ref[...] + dx_u

    row = lambda w: pl.BlockSpec((tt, w), lambda i: (i, 0))
    full = lambda a: pl.BlockSpec(a.shape, lambda i: (0,) * a.ndim)
    sds = jax.ShapeDtypeStruct
    qk_w = MLA_HEADS * HEAD_PAD
    return pl.pallas_call(
        body, name="in_bwd", grid=(T // tt,),
        in_specs=[row(D_MODEL), row(D_MODEL), row(Q_RANK), row(KV_RANK), pl.BlockSpec((qk_w, tt), lambda i: (0, i)),
                  row(qk_w), row(MLA_WIDTH),
                  row(HGRN_WIDTH), row(HGRN_WIDTH), row(HGRN_WIDTH), row(HGRN_WIDTH), row(HEAD_PAD), row(HEAD_PAD),
                  full(w_pre), full(win), full(qnw), full(wq), full(kvnw), full(wk), full(wv)],
        out_specs=[row(D_MODEL), pl.BlockSpec(WIN_COMM_SHAPE, lambda i: (0, 0, 0)), full(wq), full(wk), full(wv),
                   full(w_pre), full(qnw), full(kvnw)],
        out_shape=[sds((T, D_MODEL), F32), sds(WIN_COMM_SHAPE, F32), sds(wq.shape, F32), sds(wk.shape, F32),
                   sds(wv.shape, F32), sds(w_pre.shape, F32), sds(qnw.shape, F32), sds(kvnw.shape, F32)],
        compiler_params=_params(("arbitrary",)),
    )(x, dh1, cq, ckv, dq, dk, dv, dhq, dhf, dhi, dhg, rc, rs, w_pre, win, qnw, wq, kvnw, wk, wv)


def _arrange_weights(win_t, wuq_full, wukv):
    dt = win_t.dtype
    z = lambda n: jnp.zeros((n, D_MODEL), dt)
    s2 = Q_RANK + KV_RANK
    win_arr = jnp.concatenate([win_t[:s2], z(MLA_NOPE), win_t[s2:s2 + MLA_ROPE], z(HEAD_PAD - MLA_QK),
                               win_t[s2 + MLA_ROPE:]], axis=0)
    wq_arr = jnp.pad(wuq_full, ((0, 0), (0, 0), (0, HEAD_PAD - MLA_QK))).reshape(Q_RANK, MLA_HEADS * HEAD_PAD)
    wk_arr = jnp.pad(wukv[:, :, :MLA_NOPE], ((0, 0), (0, 0), (0, HEAD_PAD - MLA_NOPE))).reshape(
        KV_RANK, MLA_HEADS * HEAD_PAD)
    wv_arr = wukv[:, :, MLA_NOPE:].reshape(KV_RANK, MLA_WIDTH)
    return win_arr, wq_arr, wk_arr, wv_arr


WIN_COMM_SHAPE = (N_CHIPS, FF_SHARD, D_MODEL)


def _win_grad_segments():
    s2 = Q_RANK + KV_RANK
    runs = [(0, s2, 0), (s2, s2 + MLA_ROPE, MLA_NOPE), (s2 + MLA_ROPE, D_IN, HEAD_PAD - MLA_ROPE)]
    per = D_IN // N_CHIPS
    segs = []
    for lo, hi, shift in runs:
        for k in range(N_CHIPS):
            a, b = max(lo, per * k), min(hi, per * (k + 1))
            if a < b:
                segs.append((a + shift, b - a, k, a - per * k))
    return segs


def _unarrange_grads(dwq_arr, dwk_arr, dwv_arr):
    dwuq = dwq_arr.reshape(Q_RANK, MLA_HEADS, HEAD_PAD)[:, :, :MLA_QK]
    dwukv = jnp.concatenate([dwk_arr.reshape(KV_RANK, MLA_HEADS, HEAD_PAD)[:, :, :MLA_NOPE],
                             dwv_arr.reshape(KV_RANK, MLA_HEADS, MLA_V)], axis=-1)
    return dwuq, dwukv


def _rope_inv_freq():
    inv = 1.0 / (ROPE_THETA ** (jnp.arange(0, MLA_ROPE, 2, dtype=F32) / MLA_ROPE))
    z = lambda n: jnp.zeros((n,), F32)
    return jnp.concatenate([z(MLA_NOPE), inv, inv, z(HEAD_PAD - MLA_QK)]).reshape(1, HEAD_PAD)


def _local_step(x, pos, tgt, small, win_arr, wq_arr, wk_arr, wv_arr, late, place=None):
    invf = _rope_inv_freq()
    cq, ckv, xph, qb, kb, vb, kt, vt, rc, rs = _in_fwd(x, pos, invf, small["attn_pre_norm"], win_arr, small["mla_q_norm"],
                                               wq_arr, small["mla_kv_norm"], wk_arr, wv_arr)
    if place is None:
        o_raw, lse = _attn_fwd_t(qb, kb, vt)
        wout, wg, wu, wd = late
    else:
        o_raw, lse, *stacks = _attn_fwd_t(qb, kb, vt, gather=late)
        wout, wg, wu, wd = [lax.dynamic_update_slice(s, l[None], (place[1], 0, 0)) for s, l in zip(stacks, late)]
        wout = wout.reshape(D_MODEL, D_MODEL)
    oh_raw, states = _hgrn_fwd(xph, small["hgrn_lb_logits"])
    h1, y1, zb, mixb = _proj_fwd(x, o_raw, oh_raw, xph, wout, small["mla_out_norm"], small["hgrn_out_norm"],
                                 small["attn_post_norm"], small["ffn_pre_norm"])
    g, up, dy2b, dh2, loss_acc, d_fpost = _ffn_fwd(zb, h1, tgt, small["ffn_post_norm"], wg, wu, wd)
    dwg, dwu, dwd, dzp = _ffn_bwd(zb, g, up, dy2b, wg, wu, wd)
    ffn_grads = [] if place is None else [dwg, dwu, dwd]
    dh1, dwout, d_o, d_oh, dhg, dvec, d_fpre, d_post, d_mla, d_hg, *ffn_rs = _mid_bwd(
        dzp, dh2, h1, y1, mixb, o_raw, oh_raw, xph, wout, small["ffn_pre_norm"], small["attn_post_norm"],
        small["mla_out_norm"], small["hgrn_out_norm"], swap=ffn_grads)
    ffn_ps = _pair_sum(place, ffn_grads, ffn_rs, name="pair_sum_ffn") if ffn_grads else []
    dq, dk, dv, *ffn_ris = _attn_bwd_t(qb, kb, kt, vb, d_o, lse, dvec.reshape(lse.shape), send=ffn_ps)
    dhq, dhf, dhi, d_lbl = _hgrn_bwd(xph, small["hgrn_lb_logits"], states, d_oh)
    dx, dwin4, dwq_arr, dwk_arr, dwv_arr, d_pre, d_qn, d_kvn = _in_bwd(
        x, dh1, cq, ckv, dq, dk, dv, dhq, dhf, dhi, dhg, rc, rs, small["attn_pre_norm"], win_arr,
        small["mla_q_norm"], wq_arr, small["mla_kv_norm"], wk_arr, wv_arr)
    dwuq, dwukv = _unarrange_grads(dwq_arr, dwk_arr, dwv_arr)
    loss = 0.5 * jnp.sum(loss_acc) * (1.0 / D_MODEL)
    grads = dict(attn_pre_norm=d_pre, w_in=dwin4, mla_q_norm=d_qn, mla_w_uq=dwuq, mla_kv_norm=d_kvn, mla_w_ukv=dwukv,
                 mla_out_norm=d_mla, hgrn_lb_logits=d_lbl, hgrn_out_norm=d_hg, w_out=dwout, attn_post_norm=d_post,
                 ffn_pre_norm=d_fpre, w_gate=dwg, w_up=dwu, w_down=dwd, ffn_post_norm=d_fpost)
    if place is None:
        return loss, dx, grads
    return loss, dx, grads, (ffn_rs, ffn_ris)


def _place():
    x, y, c = lax.axis_index("x"), lax.axis_index("y"), lax.axis_index("c")
    others = [(1 - x, y), (x, 1 - y), (1 - x, 1 - y)]
    return x, y, c, 2 * x + y, (x, y, 1 - c), others


def _half(ref, c, rows):
    return ref.at[pl.ds(pl.multiple_of(c * rows, 8), rows)]


def _rcopy(src, dst, send, recv, k, to):
    return pltpu.make_async_remote_copy(src_ref=src, dst_ref=dst, send_sem=send.at[k], recv_sem=recv.at[k],
                                        device_id=to, device_id_type=MESH)


class _Gather:
    def __init__(self, ins, outs, send, recv):
        self.ins, self.outs, self.send, self.recv = ins, outs, send, recv
        self.n = len(ins)
        self.halves = [r.shape[0] // 2 for r in ins]
        _, _, self.c, self.me, self.sib, self.others = _place()

    def _each(self):
        for j, (px, py) in enumerate(self.others):
            for a in range(self.n):
                yield j * self.n + a, a, 2 * px + py, (px, py, self.c)

    def sends(self):
        return [_rcopy(_half(self.ins[a], self.c, self.halves[a]), _half(self.outs[a].at[self.me], self.c, self.halves[a]),
                       self.send, self.recv, k, to) for k, a, _, to in self._each()]

    def arrivals(self):
        parts = [(k, _half(self.outs[a].at[chip], self.c, self.halves[a]), to) for k, a, chip, to in self._each()]
        return [_rcopy(p, p, self.send, self.recv, k, to) for k, p, to in parts]

    def forwards(self):
        parts = [(k, _half(self.outs[a].at[chip], self.c, self.halves[a])) for k, a, chip, _ in self._each()]
        return [_rcopy(p, p, self.send, self.recv, 3 * self.n + k, self.sib) for k, p in parts]

    def forward_arrivals(self):
        parts = [(k, _half(self.outs[a].at[chip], 1 - self.c, self.halves[a])) for k, a, chip, _ in self._each()]
        return [_rcopy(p, p, self.send, self.recv, 3 * self.n + k, self.sib) for k, p in parts]

    @staticmethod
    def out_shapes(arrs):
        return [jax.ShapeDtypeStruct((N_CHIPS,) + a.shape, a.dtype) for a in arrs]

    @staticmethod
    def semaphores(arrs):
        return [pltpu.SemaphoreType.DMA((6 * len(arrs),)), pltpu.SemaphoreType.DMA((6 * len(arrs),))]


def _gather_chips(arrs, name):
    n = len(arrs)

    def body(*refs):
        gat = _Gather(refs[:n], refs[n:2 * n], *refs[2 * n:])
        sends, forwards = gat.sends(), gat.forwards()
        for cp in sends:
            cp.start()
        for arrival, fw in zip(gat.arrivals(), forwards):
            arrival.wait_recv()
            fw.start()
        for arrival in gat.forward_arrivals():
            arrival.wait_recv()
        for cp in sends + forwards:
            cp.wait_send()

    return pl.pallas_call(body, name=name, in_specs=[ANY] * n, out_specs=[ANY] * n, out_shape=_Gather.out_shapes(arrs),
                          scratch_shapes=_Gather.semaphores(arrs))(*arrs)


GRAD_BLOCKS = 2


def _pair_swap_copies(g_refs, r_refs, send, recv):
    _, _, c, _, sib, _ = _place()
    copies = []
    for a, (g, r) in enumerate(zip(g_refs, r_refs)):
        h = g.shape[1] // 2
        copies.append(_rcopy(g.at[:, pl.ds(pl.multiple_of((1 - c) * h, 8), h)], r, send, recv, a, sib))
    return copies


def _half_stack_shapes(gs, dtype=None):
    return [jax.ShapeDtypeStruct((N_CHIPS, g.shape[1] // 2, g.shape[2]), dtype or g.dtype) for g in gs]


def _pair_swap(gs, sm):
    n = len(gs)

    def body(*refs):
        g_refs, sm_ref = refs[:n], refs[n]
        r_refs, ssib_ref = refs[n + 1:2 * n + 1], refs[2 * n + 1]
        send, recv = refs[2 * n + 2:]
        copies = _pair_swap_copies(g_refs, r_refs, send, recv)
        copies.append(_rcopy(sm_ref, ssib_ref, send, recv, n, _place()[4]))
        for cp in copies:
            cp.start()
        for cp in copies:
            cp.wait()

    return pl.pallas_call(
        body, name="pair_swap", in_specs=[ANY] * (n + 1), out_specs=[ANY] * (n + 1),
        out_shape=_half_stack_shapes(gs) + [jax.ShapeDtypeStruct(sm.shape, sm.dtype)],
        scratch_shapes=[pltpu.SemaphoreType.DMA((n + 1,)), pltpu.SemaphoreType.DMA((n + 1,))],
    )(*gs, sm)


def _pair_sum(place, gs, rs, small=None, name="pair_sum"):
    n = len(gs)
    nb = GRAD_BLOCKS

    def body(place_ref, *refs):
        g_refs, r_refs, p_refs = refs[:n], refs[n:2 * n], refs[-n - 1:-1] if small else refs[-n:]
        for a in range(n):
            p_refs[a][0] = (g_refs[a][0] + r_refs[a][0]).astype(p_refs[a].dtype)
        if small:
            @pl.when((pl.program_id(0) == 0) & (pl.program_id(1) == 0))
            def _():
                refs[-1][...] = refs[2 * n][...] + refs[2 * n + 1][...]

    in_specs, out_specs = [], []
    for g in gs:
        blk = (1, g.shape[1] // 2 // nb, g.shape[2])
        in_specs.append(pl.BlockSpec(blk, lambda i, k, p: (k, p[0] * nb + i, 0)))
    for g in gs:
        blk = (1, g.shape[1] // 2 // nb, g.shape[2])
        in_specs.append(pl.BlockSpec(blk, lambda i, k, p: (k, i, 0)))
        out_specs.append(pl.BlockSpec(blk, lambda i, k, p: (k, i, 0)))
    out_shape = _half_stack_shapes(gs, BF16)
    if small:
        sm_spec = pl.BlockSpec(small[0].shape, lambda i, k, p: (0, 0))
        in_specs += [sm_spec, sm_spec]
        out_specs.append(sm_spec)
        out_shape.append(jax.ShapeDtypeStruct(small[0].shape, F32))
    return pl.pallas_call(
        body, name=name,
        grid_spec=pltpu.PrefetchScalarGridSpec(num_scalar_prefetch=1, grid=(nb, N_CHIPS), in_specs=in_specs,
                                               out_specs=out_specs),
        out_shape=out_shape,
        compiler_params=_params(("arbitrary", "arbitrary")),
    )(place, *gs, *rs, *(small or ()))


def _chip_swap_copies(p_refs, ri_refs, send, recv):
    _, _, c, _, _, others = _place()
    n = len(p_refs)
    return [_rcopy(p_refs[a].at[2 * px + py], ri_refs[a].at[j], send, recv, j * n + a, (px, py, c))
            for j, (px, py) in enumerate(others) for a in range(n)]


def _chip_swap_shapes(ps):
    return [jax.ShapeDtypeStruct((3,) + p.shape[1:], p.dtype) for p in ps]


def _chip_swap(ps, pair):
    n = len(ps)
    hs = SMALL_ROWS // 2

    def body(*refs):
        p_refs, pair_ref = refs[:n], refs[n]
        ri_refs, sm4_ref = refs[n + 1:2 * n + 1], refs[2 * n + 1]
        send, recv, lsem = refs[2 * n + 2:]
        x, y, c, me, sib, others = _place()
        local = pltpu.make_async_copy(pair_ref, sm4_ref.at[me], lsem.at[0])
        local.start()
        copies = _chip_swap_copies(p_refs, ri_refs, send, recv)
        arrivals = list(copies)
        for j, (px, py) in enumerate(others):
            copies.append(_rcopy(_half(pair_ref, c, hs), _half(sm4_ref.at[me], c, hs), send, recv, 3 * n + j, (px, py, c)))
            part = _half(sm4_ref.at[2 * px + py], c, hs)
            arrivals.append(_rcopy(part, part, send, recv, 3 * n + j, (px, py, c)))
        for cp in copies:
            cp.start()
        for arrival in arrivals:
            arrival.wait_recv()
        for cp in copies:
            cp.wait_send()
        local.wait()

    k = 3 * (n + 1)
    return pl.pallas_call(
        body, name="chip_swap", in_specs=[ANY] * (n + 1), out_specs=[ANY] * (n + 1),
        out_shape=_chip_swap_shapes(ps) + [jax.ShapeDtypeStruct((N_CHIPS,) + pair.shape, pair.dtype)],
        scratch_shapes=[pltpu.SemaphoreType.DMA((k,)), pltpu.SemaphoreType.DMA((k,)), pltpu.SemaphoreType.DMA((1,))],
    )(*ps, pair)


def _chip_sum(place, gs, rs, ris):
    n = len(gs)
    nb = GRAD_BLOCKS

    def body(place_ref, *refs):
        g_refs, r_refs, ri_refs, o_refs = refs[:n], refs[n:2 * n], refs[2 * n:3 * n], refs[3 * n:]
        for a in range(n):
            ri = ri_refs[a]
            o_refs[a][...] = (g_refs[a][0] + r_refs[a][0]) + ri[0].astype(F32) + ri[1].astype(F32) + ri[2].astype(F32)

    in_specs, out_specs, out_shape = [], [], []
    for g in gs:
        blk = (1, g.shape[1] // 2 // nb, g.shape[2])
        in_specs.append(pl.BlockSpec(blk, lambda i, p: (p[1], p[0] * nb + i, 0)))
    for g in gs:
        blk = (1, g.shape[1] // 2 // nb, g.shape[2])
        in_specs.append(pl.BlockSpec(blk, lambda i, p: (p[1], i, 0)))
    for g in gs:
        rb = g.shape[1] // 2 // nb
        in_specs.append(pl.BlockSpec((3, rb, g.shape[2]), lambda i, p: (0, i, 0)))
        out_specs.append(pl.BlockSpec((rb, g.shape[2]), lambda i, p: (p[0] * nb + i, 0)))
        out_shape.append(jax.ShapeDtypeStruct(g.shape[1:], F32))
    return pl.pallas_call(
        body, name="chip_sum",
        grid_spec=pltpu.PrefetchScalarGridSpec(num_scalar_prefetch=1, grid=(nb,), in_specs=in_specs, out_specs=out_specs),
        out_shape=out_shape,
        compiler_params=_params(("arbitrary",)),
    )(place, *gs, *rs, *ris)


def _pair_fill(gfs, sm4):
    n = len(gfs)
    hs = SMALL_ROWS // 2

    def body(*refs):
        g_refs, sm4_ref = refs[n + 1:2 * n + 1], refs[2 * n + 1]
        send, recv = refs[2 * n + 2:]
        x, y, c, me, sib, others = _place()
        copies, waits = [], []
        for a in range(n):
            h = gfs[a].shape[0] // 2
            mine, theirs = _half(g_refs[a], c, h), _half(g_refs[a], 1 - c, h)
            copies.append(pltpu.make_async_remote_copy(src_ref=mine, dst_ref=mine, send_sem=send.at[a],
                                                       recv_sem=recv.at[a], device_id=sib, device_id_type=MESH))
            waits.append(pltpu.make_async_remote_copy(src_ref=theirs, dst_ref=theirs, send_sem=send.at[a],
                                                      recv_sem=recv.at[a], device_id=sib, device_id_type=MESH))
        for j, (px, py) in enumerate(others):
            chip = 2 * px + py
            mine, theirs = _half(sm4_ref.at[chip], c, hs), _half(sm4_ref.at[chip], 1 - c, hs)
            copies.append(pltpu.make_async_remote_copy(src_ref=mine, dst_ref=mine, send_sem=send.at[n + j],
                                                       recv_sem=recv.at[n + j], device_id=sib, device_id_type=MESH))
            waits.append(pltpu.make_async_remote_copy(src_ref=theirs, dst_ref=theirs, send_sem=send.at[n + j],
                                                      recv_sem=recv.at[n + j], device_id=sib, device_id_type=MESH))
        for cp in copies:
            cp.start()
        for w in waits:
            w.wait_recv()
        for cp in copies:
            cp.wait_send()

    return pl.pallas_call(
        body, name="pair_fill", in_specs=[ANY] * (n + 1), out_specs=[ANY] * (n + 1),
        out_shape=[jax.ShapeDtypeStruct(g.shape, g.dtype) for g in gfs] + [jax.ShapeDtypeStruct(sm4.shape, sm4.dtype)],
        input_output_aliases={i: i for i in range(n + 1)},
        scratch_shapes=[pltpu.SemaphoreType.DMA((n + 3,)), pltpu.SemaphoreType.DMA((n + 3,))],
    )(*gfs, sm4)


def _adamw_math(w, g, m, v):
    m = ADAM_B1 * m + (1.0 - ADAM_B1) * g
    v = ADAM_B2 * v + (1.0 - ADAM_B2) * (g * g)
    m_hat = m / (1.0 - ADAM_B1 ** ADAM_STEP)
    v_hat = v / (1.0 - ADAM_B2 ** ADAM_STEP)
    return -ADAM_LR * (m_hat / (jnp.sqrt(v_hat) + ADAM_EPS) + ADAM_WD * w), m, v


def _adamw(items, steps, name):
    n = len(items)

    def body(*refs):
        for a in range(n):
            d, mo, vo = _adamw_math(*(r[...] for r in refs[4 * a:4 * a + 4]))
            for out, val in zip(refs[4 * n + 3 * a:4 * n + 3 * a + 3], (d, mo, vo)):
                out[...] = val

    spec = lambda w: pl.BlockSpec((w.shape[0] // steps, w.shape[1]), lambda i: (i, 0))
    flat = pl.pallas_call(
        body, name=name, grid=(steps,), in_specs=[spec(it[0]) for it in items for _ in range(4)],
        out_specs=[spec(it[0]) for it in items for _ in range(3)],
        out_shape=[jax.ShapeDtypeStruct(it[0].shape, F32) for it in items for _ in range(3)],
        compiler_params=_params(("arbitrary",)),
    )(*[a for it in items for a in it])
    return [flat[3 * a:3 * a + 3] for a in range(n)]


def _adamw_small(sm4, wmv):
    views = SMALL_VIEWS[:-1]
    n = len(views)

    def body(sm4_ref, *refs):
        g_all = ((sm4_ref[0] + sm4_ref[1]) + sm4_ref[2]) + sm4_ref[3]
        row = 0
        for a, (_, rows, cols) in enumerate(views):
            g = g_all[row:row + rows, :cols]
            row += -(-rows // ROW_TILE) * ROW_TILE
            d, mo, vo = _adamw_math(refs[3 * a][...], g, refs[3 * a + 1][...], refs[3 * a + 2][...])
            for out, val in zip(refs[3 * n + 4 * a:3 * n + 4 * a + 4], (g, d, mo, vo)):
                out[...] = val
        refs[-1][...] = g_all[row:row + 1, :128]

    flat = pl.pallas_call(
        body, name="adamw_small",
        out_shape=[jax.ShapeDtypeStruct((rows, cols), F32) for _, rows, cols in views for _ in range(4)]
        + [jax.ShapeDtypeStruct((1, 128), F32)],
        compiler_params=pltpu.CompilerParams(vmem_limit_bytes=VMEM_LIMIT),
    )(sm4, *[a for t in wmv for a in t])
    return [flat[4 * a:4 * a + 4] for a in range(n)] + [flat[-1]]


SMALL_NAMES = ("attn_pre_norm", "mla_q_norm", "mla_kv_norm", "mla_w_ukv", "mla_out_norm", "hgrn_lb_logits",
               "hgrn_out_norm", "attn_post_norm", "ffn_pre_norm", "ffn_post_norm")
BIG_NAMES = ("w_in", "mla_w_uq", "w_out", "w_gate", "w_up", "w_down")
WEIGHT_NAMES = ("attn_pre_norm", "w_in", "mla_q_norm", "mla_w_uq", "mla_kv_norm", "mla_w_ukv", "mla_out_norm",
                "hgrn_lb_logits", "hgrn_out_norm", "w_out", "attn_post_norm", "ffn_pre_norm", "w_gate", "w_up", "w_down",
                "ffn_post_norm")


UQ_COMM_SHAPE = (192, 384)


def _pack_small(vals):
    parts = []
    for name, rows, cols in SMALL_VIEWS:
        pad_rows = -(-rows // ROW_TILE) * ROW_TILE - rows
        parts.append(jnp.pad(vals[name].reshape(rows, cols), ((0, pad_rows), (0, D_MODEL - cols))))
    return jnp.concatenate(parts, axis=0)


def kernel(x, positions, attn_pre_norm, w_in, mla_q_norm, mla_w_uq, mla_kv_norm, mla_w_ukv, mla_out_norm, hgrn_lb_logits, hgrn_out_norm, w_out, attn_post_norm, ffn_pre_norm, w_gate, w_up, w_down, ffn_post_norm, loss_target, m_attn_pre_norm, m_w_in, m_mla_q_norm, m_mla_w_uq, m_mla_kv_norm, m_mla_w_ukv, m_mla_out_norm, m_hgrn_lb_logits, m_hgrn_out_norm, m_w_out, m_attn_post_norm, m_ffn_pre_norm, m_w_gate, m_w_up, m_w_down, m_ffn_post_norm, v_attn_pre_norm, v_w_in, v_mla_q_norm, v_mla_w_uq, v_mla_kv_norm, v_mla_w_ukv, v_mla_out_norm, v_hgrn_lb_logits, v_hgrn_out_norm, v_w_out, v_attn_post_norm, v_ffn_pre_norm, v_w_gate, v_w_up, v_w_down, v_ffn_post_norm):
    args = locals()
    W = {n: args[n] for n in WEIGHT_NAMES}
    M = {n: args["m_" + n] for n in WEIGHT_NAMES}
    V = {n: args["v_" + n] for n in WEIGHT_NAMES}
    T = x.shape[1]
    cx, cy, cc = lax.axis_index("x"), lax.axis_index("y"), lax.axis_index("c")

    win_rows = D_IN // N_CHIPS
    shard2d = {"w_in": (win_rows, D_MODEL), "mla_w_uq": (Q_RANK // N_CHIPS, MLA_HEADS * MLA_QK),
               "w_out": (D_MODEL // N_CHIPS, D_MODEL), "w_gate": (FF_SHARD, D_MODEL), "w_up": (FF_SHARD, D_MODEL),
               "w_down": (FF_SHARD, D_MODEL)}
    transposed = ("w_in", "w_gate", "w_up")
    to2d = lambda n, a: a[0].T if n in transposed else a.reshape(shard2d[n])
    from2d = lambda n, t: t.T[None] if n in transposed else t.reshape(W[n].shape)
    me = 2 * cx + cy
    place = jnp.stack([cc, me]).astype(jnp.int32)
    local_b = [to2d(n, W[n]).astype(BF16) for n in BIG_NAMES]
    local_b[0] = jnp.pad(local_b[0], ((0, FF_SHARD - win_rows), (0, 0)))
    stacks = _gather_chips(local_b[:2], "gather_weights")
    win4, wuq4 = [lax.dynamic_update_slice(s, l[None], (me, 0, 0)) for s, l in zip(stacks, local_b)]
    win_t = win4[:, :win_rows].reshape(D_IN, D_MODEL)
    wuq_full = wuq4.reshape(Q_RANK, MLA_HEADS, MLA_QK)
    win_arr, wq_arr, wk_arr, wv_arr = _arrange_weights(win_t, wuq_full, mla_w_ukv[0].astype(BF16))
    small = {n: W[n][0] if n == "mla_w_ukv" else W[n].reshape(-1, W[n].shape[-1]) for n in SMALL_NAMES}

    loss_local, dx, grads, (ffn_rs, ffn_ris) = _local_step(x[0], positions.reshape(T, 1), loss_target[0], small, win_arr,
                                                           wq_arr, wk_arr, wv_arr, local_b[2:], place)

    gs = [grads["w_in"], grads["mla_w_uq"].reshape((N_CHIPS,) + UQ_COMM_SHAPE), grads["w_out"].reshape((N_CHIPS,) + shard2d["w_out"])]
    ffn_gs = [grads["w_gate"], grads["w_up"], grads["w_down"]]
    sm = _pack_small({**grads, "loss": loss_local})
    *rs, ssib = _pair_swap(gs, sm)
    *ps, pair = _pair_sum(place, gs, rs, small=(sm, ssib))
    *ris, sm4 = _chip_swap(ps, pair)
    gfs = _chip_sum(place, gs + ffn_gs, rs + ffn_rs, ris + ffn_ris)
    *gfin, smf = _pair_fill(gfs, sm4)

    G, DW, NM, NV = {}, {}, {}, {}
    g2d = {n: gfin[k].reshape((-1,) + shard2d[n][1:]) for k, n in enumerate(BIG_NAMES)}
    for names_, steps in ((("w_in", "mla_w_uq"), 3), (("w_out", "w_gate", "w_up", "w_down"), 8)):
        res = _adamw([(to2d(n, W[n]), g2d[n], to2d(n, M[n]), to2d(n, V[n])) for n in names_], steps, "adamw_" + names_[0])
        for n, (d, mo, vo) in zip(names_, res):
            G[n] = from2d(n, g2d[n][:shard2d[n][0]])
            DW[n], NM[n], NV[n] = (from2d(n, t) for t in (d, mo, vo))
    view2d = lambda n, a: a.reshape(next((r, c) for name, r, c in SMALL_VIEWS if name == n))
    *res, loss_row = _adamw_small(smf, [tuple(view2d(n, t[n]) for t in (W, M, V)) for n in SMALL_NAMES])
    for n, outs in zip(SMALL_NAMES, res):
        G[n], DW[n], NM[n], NV[n] = (t.reshape(W[n].shape) for t in outs)
    loss = loss_row[0, 0]
    return (loss, dx[None], *[G[n] for n in WEIGHT_NAMES], *[DW[n] for n in WEIGHT_NAMES],
            *[NM[n] for n in WEIGHT_NAMES], *[NV[n] for n in WEIGHT_NAMES])
```

```python
import jax
import jax.numpy as jnp
from jax import lax
from jax.experimental import pallas as pl
from jax.experimental.pallas import tpu as pltpu

F32 = jnp.float32
BF16 = jnp.bfloat16
MXU_DTYPE = BF16

D_MODEL = 1024
MLA_HEADS = 8
MLA_NOPE = 64
MLA_ROPE = 32
MLA_V = 64
MLA_QK = MLA_NOPE + MLA_ROPE
Q_RANK = 384
KV_RANK = 128
MLA_WIDTH = MLA_HEADS * MLA_V
HEAD_PAD = 128
HGRN_HEADS = 4
HGRN_DIM = 128
HGRN_WIDTH = HGRN_HEADS * HGRN_DIM
CHUNK = 64
SUB = 16
HGRN_CPI = 4
D_IN = Q_RANK + KV_RANK + MLA_ROPE + 4 * HGRN_WIDTH
D_IN_ARR = Q_RANK + KV_RANK + HEAD_PAD + 4 * HGRN_WIDTH
D_FF = 2816
N_CHIPS = 4
FF_SHARD = D_FF // N_CHIPS
EPS = 1e-6
ROPE_THETA = 10000.0
ATTN_SCALE = MLA_QK ** -0.5
ATTN_SCALE_LOG2 = ATTN_SCALE * 1.4426950408889634
NEG_BIG = -1e30

ADAM_LR = 0.001
ADAM_B1 = 0.9
ADAM_B2 = 0.999
ADAM_EPS = 1e-08
ADAM_WD = 0.01
ADAM_STEP = 10

VMEM_LIMIT = 56 * 1024 * 1024

SMALL_VIEWS = (("attn_pre_norm", 1, 1024), ("mla_q_norm", 1, 384), ("mla_kv_norm", 1, 128), ("mla_w_ukv", 128, 1024),
               ("mla_out_norm", 1, 512), ("hgrn_lb_logits", 2, 512), ("hgrn_out_norm", 1, 512),
               ("attn_post_norm", 1, 1024), ("ffn_pre_norm", 1, 1024), ("ffn_post_norm", 1, 1024), ("loss", 1, 1))
ROW_TILE = 8
SMALL_ROWS = sum(-(-rows // ROW_TILE) * ROW_TILE for _, rows, _ in SMALL_VIEWS)

MESH = pl.DeviceIdType.MESH
ANY = pl.BlockSpec(memory_space=pl.ANY)


def _dot(a, b, dims, exact):
    if exact:
        return lax.dot_general(a.astype(F32), b.astype(F32), (dims, ((), ())), precision=lax.Precision.HIGH,
                               preferred_element_type=F32)
    return lax.dot_general(a.astype(MXU_DTYPE), b.astype(MXU_DTYPE), (dims, ((), ())), preferred_element_type=F32)


def _mm(a, b, exact=False):
    return _dot(a, b, ((1,), (0,)), exact)


def _mm_nt(a, b, exact=False):
    return _dot(a, b, ((1,), (1,)), exact)


def _mm_tn(a, b, exact=False):
    return _dot(a, b, ((0,), (0,)), exact)


def _rms_fwd(x, w):
    r = lax.rsqrt(jnp.mean(x * x, axis=-1, keepdims=True) + EPS)
    xn = x * r
    return xn * w, xn, r


def _rms_bwd(dy, xn, r, w):
    dxn = dy * w
    dx = r * (dxn - xn * jnp.mean(dxn * xn, axis=-1, keepdims=True))
    dw = jnp.sum(dy * xn, axis=0, keepdims=True)
    return dx, dw


def _group_sums(v, gs):
    t, n = v.shape
    lane = lax.broadcasted_iota(jnp.int32, (t, 128), 1)
    out = []
    for p in range(n // 128):
        vb = v[:, 128 * p:128 * (p + 1)]
        if gs == 128:
            out.append(jnp.sum(vb, axis=-1, keepdims=True))
        else:
            out.append(jnp.sum(jnp.where(lane < 64, vb, 0.0), axis=-1, keepdims=True))
            out.append(jnp.sum(jnp.where(lane >= 64, vb, 0.0), axis=-1, keepdims=True))
    return out


def _group_bcast(sums, gs, t):
    lane = lax.broadcasted_iota(jnp.int32, (t, 128), 1)
    if gs == 128:
        return jnp.concatenate([jnp.broadcast_to(s, (t, 128)) for s in sums], axis=-1)
    return jnp.concatenate([jnp.where(lane < 64, sums[2 * p], sums[2 * p + 1]) for p in range(len(sums) // 2)],
                           axis=-1)


def _grms_fwd(x, w, gs):
    t = x.shape[0]
    r = lax.rsqrt(_group_bcast(_group_sums(x * x, gs), gs, t) * (1.0 / gs) + EPS)
    xn = x * r
    return xn * w, xn, r


def _grms_bwd(dy, xn, r, w, gs):
    t = dy.shape[0]
    dxn = dy * w
    dx = r * (dxn - xn * (_group_bcast(_group_sums(dxn * xn, gs), gs, t) * (1.0 / gs)))
    dw = jnp.sum(dy * xn, axis=0, keepdims=True)
    return dx, dw


def _rope_tables(c_tab, s_tab):
    lane = lax.broadcasted_iota(jnp.int32, c_tab.shape, 1)
    first = (lane >= MLA_NOPE) & (lane < MLA_NOPE + MLA_ROPE // 2)
    second = (lane >= MLA_NOPE + MLA_ROPE // 2) & (lane < MLA_QK)
    return c_tab, jnp.where(first, -s_tab, 0.0), jnp.where(second, s_tab, 0.0)


def _rope(v, c, sa, sb):
    return v * c + pltpu.roll(v, HEAD_PAD - MLA_ROPE // 2, 1) * sa + pltpu.roll(v, MLA_ROPE // 2, 1) * sb


def _rope_bwd(d, c, sa, sb):
    return d * c - pltpu.roll(d, HEAD_PAD - MLA_ROPE // 2, 1) * sa - pltpu.roll(d, MLA_ROPE // 2, 1) * sb


def _params(sem, vmem=VMEM_LIMIT):
    return pltpu.CompilerParams(dimension_semantics=sem, vmem_limit_bytes=vmem)


def _in_fwd(x, pos, invf, w_pre, win, qnw, wq, kvnw, wk, wv, tt=512):
    T = x.shape[0]

    def body(x_ref, pos_ref, invf_ref, wpre_ref, win_ref, qnw_ref, wq_ref, kvnw_ref, wk_ref, wv_ref,
             cq_ref, ckv_ref, xph_ref, q_ref, k_ref, v_ref, kt_ref, vt_ref, rc_ref, rs_ref):
        u, _, _ = _rms_fwd(x_ref[...], wpre_ref[...])
        lo = Q_RANK + KV_RANK + HEAD_PAD
        xp = _mm_nt(u, win_ref[:lo, :])
        xph_ref[...] = _mm_nt(u, win_ref[lo:, :])
        cq = xp[:, :Q_RANK]
        ckv = xp[:, Q_RANK:Q_RANK + KV_RANK]
        kr = xp[:, Q_RANK + KV_RANK:]
        cq_ref[...] = cq
        ckv_ref[...] = ckv
        ang = pos_ref[...].astype(F32) * invf_ref[...]
        c_tab = jnp.cos(ang)
        s_tab = jnp.sin(ang)
        rc_ref[...] = c_tab
        rs_ref[...] = s_tab
        c, sa, sb = _rope_tables(c_tab, s_tab)
        qn, _, _ = _rms_fwd(cq, qnw_ref[...])
        q = _mm(qn, wq_ref[...])
        kvn, _, _ = _rms_fwd(ckv, kvnw_ref[...])
        kn = _mm(kvn, wk_ref[...])
        v = _mm(kvn, wv_ref[...])
        v_ref[...] = v.astype(v_ref.dtype)
        vt_ref[...] = v.T.astype(vt_ref.dtype)
        krr = _rope(kr, c, sa, sb)
        for h in range(MLA_HEADS):
            sl = slice(HEAD_PAD * h, HEAD_PAD * (h + 1))
            q_ref[:, sl] = _rope(q[:, sl], c, sa, sb).astype(q_ref.dtype)
            kh = kn[:, sl] + krr
            k_ref[:, sl] = kh.astype(k_ref.dtype)
            kt_ref[sl, :] = kh.T.astype(kt_ref.dtype)

    row = lambda w: pl.BlockSpec((tt, w), lambda i: (i, 0))
    full = lambda a: pl.BlockSpec(a.shape, lambda i: (0,) * a.ndim)
    qk_w = MLA_HEADS * HEAD_PAD
    return pl.pallas_call(
        body, name="in_fwd", grid=(T // tt,),
        in_specs=[row(D_MODEL), row(1), full(invf), full(w_pre), full(win), full(qnw), full(wq), full(kvnw),
                  full(wk), full(wv)],
        out_specs=[row(Q_RANK), row(KV_RANK), row(4 * HGRN_WIDTH), row(qk_w), row(qk_w), row(MLA_WIDTH),
                   pl.BlockSpec((qk_w, tt), lambda i: (0, i)), pl.BlockSpec((MLA_WIDTH, tt), lambda i: (0, i)),
                   row(HEAD_PAD), row(HEAD_PAD)],
        out_shape=[jax.ShapeDtypeStruct((T, Q_RANK), F32), jax.ShapeDtypeStruct((T, KV_RANK), F32),
                   jax.ShapeDtypeStruct((T, 4 * HGRN_WIDTH), F32), jax.ShapeDtypeStruct((T, qk_w), MXU_DTYPE),
                   jax.ShapeDtypeStruct((T, qk_w), MXU_DTYPE), jax.ShapeDtypeStruct((T, MLA_WIDTH), MXU_DTYPE),
                   jax.ShapeDtypeStruct((qk_w, T), MXU_DTYPE), jax.ShapeDtypeStruct((MLA_WIDTH, T), MXU_DTYPE),
                   jax.ShapeDtypeStruct((T, HEAD_PAD), F32), jax.ShapeDtypeStruct((T, HEAD_PAD), F32)],
        compiler_params=_params(("arbitrary",)),
    )(x, pos, invf, w_pre, win, qnw, wq, kvnw, wk, wv)


def _attn_fwd_t(qb, kb, vt, gather=(), tq=256, hps=8):
    T = qb.shape[0]
    nq = T // tq
    ng = len(gather)
    steps = (MLA_HEADS // hps) * nq
    pass_on = steps - 3

    def body(q_ref, k_ref, vt_ref, *rest):
        o_ref, lse_ref = rest[ng:ng + 2]
        acc_scr = rest[2 * ng + 2]
        qi = pl.program_id(1)
        step_no = pl.program_id(0) * nq + qi
        if ng:
            gat = _Gather(rest[:ng], rest[ng + 2:2 * ng + 2], *rest[2 * ng + 3:])

            @pl.when(step_no == 0)
            def _():
                for cp in gat.sends():
                    cp.start()

            @pl.when(step_no == pass_on)
            def _():
                for arrival in gat.arrivals():
                    arrival.wait_recv()
                for cp in gat.forwards():
                    cp.start()

        heads = [slice(HEAD_PAD * a, HEAD_PAD * (a + 1)) for a in range(hps)]
        acc_scr[...] = jnp.zeros_like(acc_scr)

        def step(j, carry, masked):
            start = pl.multiple_of(j * tq, tq)
            scores = [_mm_nt(k_ref[pl.ds(start, tq), heads[a]], q_ref[:, heads[a]]) for a in range(hps)]
            new = []
            for a in range(hps):
                m, l = carry[a]
                s = scores[a] * ATTN_SCALE_LOG2
                if masked:
                    kk = lax.broadcasted_iota(jnp.int32, (tq, tq), 0)
                    qq = lax.broadcasted_iota(jnp.int32, (tq, tq), 1)
                    s = jnp.where(kk <= qq, s, NEG_BIG)
                m_new = jnp.maximum(m, jnp.max(s, axis=0, keepdims=True))
                alpha = jnp.exp2(m - m_new)
                p = jnp.exp2(s - m_new)
                l = l * alpha + jnp.sum(p, axis=0, keepdims=True)
                vtj = vt_ref[2 * MLA_V * (a // 2):2 * MLA_V * (a // 2 + 1), pl.ds(start, tq)]
                acc_scr[a] = acc_scr[a] * alpha + _mm(vtj, p)
                new.append((m_new, l))
            return tuple(new)

        init = tuple((jnp.full((1, tq), NEG_BIG, F32), jnp.zeros((1, tq), F32)) for _ in range(hps))
        carry = lax.fori_loop(0, qi, lambda j, c: step(j, c, False), init)
        carry = step(qi, carry, True)
        row = lax.broadcasted_iota(jnp.int32, (2 * MLA_V, tq), 0)
        for pr in range(hps // 2):
            (m0, l0), (m1, l1) = carry[2 * pr], carry[2 * pr + 1]
            ot = jnp.where(row < MLA_V, acc_scr[2 * pr] / l0, acc_scr[2 * pr + 1] / l1)
            o_ref[:, 2 * MLA_V * pr:2 * MLA_V * (pr + 1)] = ot.T
            lse_ref[pr, 0:1, :] = m0 + jnp.log2(l0)
            lse_ref[pr, 1:2, :] = m1 + jnp.log2(l1)

        if ng:
            @pl.when(step_no == steps - 1)
            def _():
                for arrival in gat.forward_arrivals():
                    arrival.wait_recv()
                for cp in gat.sends() + gat.forwards():
                    cp.wait_send()

    return pl.pallas_call(
        body, name="attn_fwd", grid=(MLA_HEADS // hps, nq),
        in_specs=[pl.BlockSpec((tq, hps * HEAD_PAD), lambda g, i: (i, g)),
                  pl.BlockSpec((T, hps * HEAD_PAD), lambda g, i: (0, g)),
                  pl.BlockSpec((hps * MLA_V, T), lambda g, i: (g, 0))] + [ANY] * ng,
        out_specs=[pl.BlockSpec((tq, hps * MLA_V), lambda g, i: (i, g)),
                   pl.BlockSpec((hps // 2, 2, tq), lambda g, i: (g, 0, i))] + [ANY] * ng,
        out_shape=[jax.ShapeDtypeStruct((T, MLA_WIDTH), F32), jax.ShapeDtypeStruct((MLA_HEADS // 2, 2, T), F32)]
        + _Gather.out_shapes(gather),
        scratch_shapes=[pltpu.VMEM((hps, 2 * MLA_V, tq), F32)] + (_Gather.semaphores(gather) if ng else []),
        compiler_params=_params(("arbitrary", "arbitrary")),
    )(qb, kb, vt, *gather)


def _attn_bwd_t(qb, kb, kt, vb, dob, lse, dvec, send=(), tq=256, hps=4):
    T = qb.shape[0]
    nq = T // tq
    ns = len(send)
    steps = (MLA_HEADS // hps) * nq

    def body(q_ref, k_ref, kt_ref, v_ref, do_ref, lse_ref, d_ref, *rest):
        dqt_ref, dk_ref, dv_ref = rest[ns:ns + 3]
        va_scr, dv_scr = rest[2 * ns + 3:2 * ns + 5]
        j = pl.program_id(1)
        step_no = pl.program_id(0) * nq + j
        if ns:
            @pl.when(step_no == 0)
            def _():
                for cp in _chip_swap_copies(rest[:ns], rest[ns + 3:2 * ns + 3], *rest[2 * ns + 5:]):
                    cp.start()

        @pl.when(j == 0)
        def _():
            dqt_ref[...] = jnp.zeros_like(dqt_ref)

        lane = lax.broadcasted_iota(jnp.int32, (tq, 2 * MLA_V), 1)
        heads = [slice(HEAD_PAD * a, HEAD_PAD * (a + 1)) for a in range(hps)]
        pairs = [slice(2 * MLA_V * p, 2 * MLA_V * (p + 1)) for p in range(hps // 2)]
        for pr in range(hps // 2):
            vpair = v_ref[:, pairs[pr]]
            va_scr[2 * pr] = jnp.where(lane < MLA_V, vpair, jnp.zeros_like(vpair))
            va_scr[2 * pr + 1] = jnp.where(lane >= MLA_V, vpair, jnp.zeros_like(vpair))
        dk_ref[...] = jnp.zeros_like(dk_ref)
        dv_scr[...] = jnp.zeros_like(dv_scr)

        def step(i, masked):
            start = pl.multiple_of(i * tq, tq)
            rows = pl.ds(start, tq)
            scores = [_mm_nt(k_ref[:, heads[a]], q_ref[rows, heads[a]]) for a in range(hps)]
            dps = [_mm_nt(va_scr[a], do_ref[rows, pairs[a // 2]]) for a in range(hps)]
            for a in range(hps):
                pr, r = a // 2, a % 2
                p = jnp.exp2(scores[a] * ATTN_SCALE_LOG2 - lse_ref[pr, r:r + 1, rows])
                if masked:
                    kk = lax.broadcasted_iota(jnp.int32, (tq, tq), 0)
                    qq = lax.broadcasted_iota(jnp.int32, (tq, tq), 1)
                    p = jnp.where(kk <= qq, p, 0.0)
                ds = p * (dps[a] - d_ref[pr, r:r + 1, rows]) * ATTN_SCALE
                dv_scr[a] += _mm(p, do_ref[rows, pairs[pr]])
                dk_ref[:, heads[a]] += _mm(ds, q_ref[rows, heads[a]])
                dqt_ref[heads[a], rows] += _mm(kt_ref[heads[a], :], ds)

        def loop_body(i, _):
            step(i, False)
            return 0

        step(j, True)
        lax.fori_loop(j + 1, nq, loop_body, 0)
        for pr in range(hps // 2):
            dv_ref[:, pairs[pr]] = jnp.where(lane < MLA_V, dv_scr[2 * pr], dv_scr[2 * pr + 1])

        if ns:
            @pl.when(step_no == steps - 1)
            def _():
                for cp in _chip_swap_copies(rest[:ns], rest[ns + 3:2 * ns + 3], *rest[2 * ns + 5:]):
                    cp.wait()

    stat = pl.BlockSpec((hps // 2, 2, T), lambda g, j: (g, 0, 0))
    return pl.pallas_call(
        body, name="attn_bwd", grid=(MLA_HEADS // hps, nq),
        in_specs=[pl.BlockSpec((T, hps * HEAD_PAD), lambda g, j: (0, g)),
                  pl.BlockSpec((tq, hps * HEAD_PAD), lambda g, j: (j, g)),
                  pl.BlockSpec((hps * HEAD_PAD, tq), lambda g, j: (g, j)),
                  pl.BlockSpec((tq, hps * MLA_V), lambda g, j: (j, g)),
                  pl.BlockSpec((T, hps * MLA_V), lambda g, j: (0, g)), stat, stat] + [ANY] * ns,
        out_specs=[pl.BlockSpec((hps * HEAD_PAD, T), lambda g, j: (g, 0)),
                   pl.BlockSpec((tq, hps * HEAD_PAD), lambda g, j: (j, g)),
                   pl.BlockSpec((tq, hps * MLA_V), lambda g, j: (j, g))] + [ANY] * ns,
        out_shape=[jax.ShapeDtypeStruct((MLA_HEADS * HEAD_PAD, T), F32),
                   jax.ShapeDtypeStruct((T, MLA_HEADS * HEAD_PAD), F32),
                   jax.ShapeDtypeStruct((T, MLA_WIDTH), F32)] + _chip_swap_shapes(send),
        scratch_shapes=[pltpu.VMEM((hps, tq, 2 * MLA_V), vb.dtype), pltpu.VMEM((hps, tq, 2 * MLA_V), F32)]
        + ([pltpu.SemaphoreType.DMA((3 * ns,)), pltpu.SemaphoreType.DMA((3 * ns,))] if ns else []),
        compiler_params=_params(("arbitrary", "arbitrary")),
    )(qb, kb, kt, vb, dob, lse, dvec, *send)


def _cumsum_rows(x):
    n = x.shape[0]
    row = lax.broadcasted_iota(jnp.int32, x.shape, 0)
    s = 1
    while s < n:
        x = x + jnp.where(row >= s, pltpu.roll(x, s, 0), 0.0)
        s *= 2
    return x


def _rev_cumsum_rows(x):
    n = x.shape[0]
    row = lax.broadcasted_iota(jnp.int32, x.shape, 0)
    s = 1
    while s < n:
        x = x + jnp.where(row < n - s, pltpu.roll(x, n - s, 0), 0.0)
        s *= 2
    return x


def _lb_from_logits(l):
    l0, l1 = l[0:1, :], l[1:2, :]
    m = jnp.maximum(l0, l1)
    e0, e1 = jnp.exp(l0 - m), jnp.exp(l1 - m)
    return e0 / (e0 + e1)


def _hgrn_gates(hq, hf, lb):
    sig_f = jax.nn.sigmoid(hf)
    f = lb + (1.0 - lb) * sig_f
    sig_q = jax.nn.sigmoid(hq)
    return sig_f, f, jnp.log(f), 1.0 - f, sig_q, hq * sig_q


def _hgrn_intra(q, kk, b, exact=False):
    row = lax.broadcasted_iota(jnp.int32, b.shape, 0)
    qs, ks, eqs, eks, a_rows = [], [], [], [], []
    for i in range(CHUNK // SUB):
        ref = b[SUB * i + SUB // 2:SUB * i + SUB // 2 + 1, :]
        eq = jnp.exp(b[SUB * i:SUB * (i + 1), :] - ref)
        ek = jnp.exp(jnp.where(row < SUB * (i + 1), ref - b, NEG_BIG))
        qi = q[SUB * i:SUB * (i + 1), :] * eq
        ki = kk * ek
        a_rows.append(_mm_nt(qi, ki, exact))
        qs.append(qi), ks.append(ki), eqs.append(eq), eks.append(ek)
    tt = lax.broadcasted_iota(jnp.int32, (CHUNK, CHUNK), 0)
    ss = lax.broadcasted_iota(jnp.int32, (CHUNK, CHUNK), 1)
    causal = ss <= tt
    a = jnp.where(causal, jnp.concatenate(a_rows, axis=0), 0.0)
    return a, causal, qs, ks, eqs, eks


def _hgrn_fwd(xph, lbl, tg=512):
    T = xph.shape[0]
    ng, ncg = T // tg, tg // CHUNK
    cols = [slice(HGRN_DIM * h, HGRN_DIM * (h + 1)) for h in range(HGRN_HEADS)]

    def body(lbl_ref, hq_ref, hf_ref, hi_ref, o_ref, st_ref, s_scr):
        @pl.when(pl.program_id(0) == 0)
        def _():
            s_scr[...] = jnp.zeros_like(s_scr)

        lb = _lb_from_logits(lbl_ref[...])

        def chunks(it, _):
            pre = []
            for k in range(HGRN_CPI):
                c = it * HGRN_CPI + k
                rows = pl.ds(pl.multiple_of(c * CHUNK, CHUNK), CHUNK)
                for cs in cols:
                    _, _, lf, kk, _, q = _hgrn_gates(hq_ref[rows, cs], hf_ref[rows, cs], lb[:, cs])
                    v = hi_ref[rows, cs]
                    b = _cumsum_rows(lf)
                    a = _hgrn_intra(q, kk, b)[0]
                    b_last = b[CHUNK - 1:CHUNK, :]
                    pre.append((c, rows, q * jnp.exp(b), a, v, jnp.exp(b_last), _mm_tn(v, kk * jnp.exp(b_last - b))))
            for i, (c, rows, qe, a, v, ebl, upd) in enumerate(pre):
                h = i % HGRN_HEADS
                st = s_scr[h]
                st_ref[h, c] = st
                o_ref[rows, cols[h]] = _mm_nt(qe, st) + _mm(a, v)
                s_scr[h] = st * ebl + upd
            return 0

        lax.fori_loop(0, ncg // HGRN_CPI, chunks, 0)

    col = lambda k: pl.BlockSpec((tg, HGRN_WIDTH), lambda g: (g, k))
    return pl.pallas_call(
        body, name="hgrn_fwd", grid=(ng,),
        in_specs=[pl.BlockSpec((2, HGRN_WIDTH), lambda g: (0, 0)), col(0), col(1), col(2)],
        out_specs=[col(0), pl.BlockSpec((HGRN_HEADS, ncg, HGRN_DIM, HGRN_DIM), lambda g: (0, g, 0, 0))],
        out_shape=[jax.ShapeDtypeStruct((T, HGRN_WIDTH), F32),
                   jax.ShapeDtypeStruct((HGRN_HEADS, T // CHUNK, HGRN_DIM, HGRN_DIM), F32)],
        scratch_shapes=[pltpu.VMEM((HGRN_HEADS, HGRN_DIM, HGRN_DIM), F32)],
        compiler_params=_params(("arbitrary",)),
    )(lbl, xph, xph, xph)


def _hgrn_bwd(xph, lbl, states, d_o, tg=512):
    T = xph.shape[0]
    ng, ncg = T // tg, tg // CHUNK
    cols = [slice(HGRN_DIM * h, HGRN_DIM * (h + 1)) for h in range(HGRN_HEADS)]
    nsub = CHUNK // SUB

    def body(lbl_ref, hq_ref, hf_ref, hi_ref, st_ref, do_ref, dhq_ref, dhf_ref, dhi_ref, dlg_ref, ds_scr, dlb_scr):
        g = pl.program_id(0)

        @pl.when(g == 0)
        def _():
            ds_scr[...] = jnp.zeros_like(ds_scr)
            dlb_scr[...] = jnp.zeros_like(dlb_scr)

        lb = _lb_from_logits(lbl_ref[...])

        def chunks(it, _):
            pre = []
            for k, h in ((k, h) for k in range(HGRN_CPI) for h in range(HGRN_HEADS)):
                cs = cols[h]
                c = ncg - 1 - (it * HGRN_CPI + k)
                rows = pl.ds(pl.multiple_of(c * CHUNK, CHUNK), CHUNK)
                hq = hq_ref[rows, cs]
                sig_f, f, lf, kk, sig_q, q = _hgrn_gates(hq, hf_ref[rows, cs], lb[:, cs])
                v = hi_ref[rows, cs]
                do = do_ref[rows, cs]
                b = _cumsum_rows(lf)
                eb = jnp.exp(b)
                a, causal, qs, ks, eqs, eks = _hgrn_intra(q, kk, b)
                b_last = b[CHUNK - 1:CHUNK, :]
                st = st_ref[h, c]
                pre.append(dict(h=h, cs=cs, rows=rows, hq=hq, sig_f=sig_f, f=f, kk=kk, sig_q=sig_q, q=q, v=v, eb=eb, qs=qs,
                                ks=ks, eqs=eqs,
                                eks=eks, ebl=jnp.exp(b_last), el=jnp.exp(b_last - b), st=st,
                                da=jnp.where(causal, _mm_nt(do, v, True), 0.0), dq=_mm(do, st, True) * eb,
                                dv=_mm_tn(a, do), dsu=_mm_tn(do, q * eb, True)))
            for w in pre:
                dq_rows = []
                dk = jnp.zeros_like(w["q"])
                for i in range(nsub):
                    dai = w["da"][SUB * i:SUB * (i + 1), :]
                    dq_rows.append(_mm(dai, w["ks"][i], True) * w["eqs"][i])
                    dk = dk + _mm_tn(dai, w["qs"][i], True) * w["eks"][i]
                w["dq"] = w["dq"] + jnp.concatenate(dq_rows, axis=0)
                w["dk"] = dk
            for w in pre:
                h, cs, rows = w["h"], w["cs"], w["rows"]
                kk, el, ebl, dst = w["kk"], w["el"], w["ebl"], ds_scr[h]
                dk_state = _mm(w["v"], dst, True) * el
                dk = w["dk"] + dk_state
                e_last = (ebl * jnp.sum(w["st"] * dst, axis=0, keepdims=True)
                          + jnp.sum(kk * dk_state, axis=0, keepdims=True))
                dlf = _rev_cumsum_rows(w["q"] * w["dq"] - kk * dk) + e_last
                ds_scr[h] = dst * ebl + w["dsu"]
                df = dlf / w["f"] - dk
                sig_f, sig_q = w["sig_f"], w["sig_q"]
                dhf_ref[rows, cs] = df * (1.0 - lb[:, cs]) * sig_f * (1.0 - sig_f)
                dlb_scr[:, cs] += jnp.sum(df * (1.0 - sig_f), axis=0, keepdims=True)
                dhq_ref[rows, cs] = w["dq"] * sig_q * (1.0 + w["hq"] * (1.0 - sig_q))
                dhi_ref[rows, cs] = w["dv"] + _mm_nt(kk * el, dst)
            return 0

        lax.fori_loop(0, ncg // HGRN_CPI, chunks, 0)

        @pl.when(g == ng - 1)
        def _():
            dl0 = dlb_scr[...] * lb * (1.0 - lb)
            dlg_ref[...] = jnp.concatenate([dl0, -dl0], axis=0)

    col = lambda k: pl.BlockSpec((tg, HGRN_WIDTH), lambda g: (ng - 1 - g, k))
    logits = pl.BlockSpec((2, HGRN_WIDTH), lambda g: (0, 0))
    big = jax.ShapeDtypeStruct((T, HGRN_WIDTH), F32)
    return pl.pallas_call(
        body, name="hgrn_bwd", grid=(ng,),
        in_specs=[logits, col(0), col(1), col(2),
                  pl.BlockSpec((HGRN_HEADS, ncg, HGRN_DIM, HGRN_DIM), lambda g: (0, ng - 1 - g, 0, 0)), col(0)],
        out_specs=[col(0), col(0), col(0), logits],
        out_shape=[big, big, big, jax.ShapeDtypeStruct((2, HGRN_WIDTH), F32)],
        scratch_shapes=[pltpu.VMEM((HGRN_HEADS, HGRN_DIM, HGRN_DIM), F32), pltpu.VMEM((1, HGRN_WIDTH), F32)],
        compiler_params=_params(("arbitrary",)),
    )(lbl, xph, xph, xph, states, d_o)


def _proj_fwd(x, o_raw, oh_raw, xph, wout, w_mla, w_hg, w_post, w_fpre, tt=512):
    T = x.shape[0]

    def body(x_ref, o_ref, oh_ref, hg_ref, wout_ref, wmla_ref, whg_ref, wpost_ref, wfpre_ref,
             h1_ref, z_ref, mix_ref):
        om, _, _ = _grms_fwd(o_ref[...], wmla_ref[...], MLA_V)
        hg = hg_ref[...]
        ohn, _, _ = _grms_fwd(oh_ref[...], whg_ref[...], HGRN_DIM)
        mix = jnp.concatenate([om, ohn * (hg * jax.nn.sigmoid(hg))], axis=-1)
        mix_ref[...] = mix.astype(mix_ref.dtype)
        h1 = x_ref[...] + _rms_fwd(_mm(mix, wout_ref[...]), wpost_ref[...])[0]
        h1_ref[...] = h1
        z_ref[...] = _rms_fwd(h1, wfpre_ref[...])[0].astype(z_ref.dtype)

    row = lambda w: pl.BlockSpec((tt, w), lambda i: (i, 0))
    full = lambda a: pl.BlockSpec(a.shape, lambda i: (0,) * a.ndim)
    sds = jax.ShapeDtypeStruct
    return pl.pallas_call(
        body, name="proj_fwd", grid=(T // tt,),
        in_specs=[row(D_MODEL), row(MLA_WIDTH), row(HGRN_WIDTH), pl.BlockSpec((tt, HGRN_WIDTH), lambda i: (i, 3)),
                  full(wout), full(w_mla), full(w_hg), full(w_post), full(w_fpre)],
        out_specs=[row(D_MODEL)] * 3,
        out_shape=[sds((T, D_MODEL), F32), sds((T, D_MODEL), MXU_DTYPE), sds((T, D_MODEL), MXU_DTYPE)],
        compiler_params=_params(("arbitrary",)),
    )(x, o_raw, oh_raw, xph, wout, w_mla, w_hg, w_post, w_fpre)


def _ffn_fwd(zb, h1, tgt, w_fpost, wg, wu, wd, tt=256):
    T = zb.shape[0]
    nj = N_CHIPS

    def body(z_ref, h1_ref, tgt_ref, wfpost_ref, wg_ref, wu_ref, wd_ref, g_ref, up_ref, dy2_ref, dh2_ref, loss_ref, dwf_ref):
        @pl.when(pl.program_id(0) == 0)
        def _():
            loss_ref[...] = jnp.zeros_like(loss_ref)
            dwf_ref[...] = jnp.zeros_like(dwf_ref)

        z = z_ref[...]
        gs = [_mm_nt(z, wg_ref[j]) for j in range(nj)]
        ups = [_mm_nt(z, wu_ref[j]) for j in range(nj)]
        y2 = jnp.zeros((tt, D_MODEL), F32)
        for j in range(nj):
            g_ref[j] = gs[j]
            up_ref[j] = ups[j]
            y2 = y2 + _mm(gs[j] * jax.nn.sigmoid(gs[j]) * ups[j], wd_ref[j])
        w = wfpost_ref[...]
        y2s, y2n, r2 = _rms_fwd(y2, w)
        e = h1_ref[...] + y2s - tgt_ref[...]
        loss_ref[...] += jnp.sum(e * e, axis=0, keepdims=True)
        dh2 = e * (1.0 / D_MODEL)
        dh2_ref[...] = dh2
        dy2, dwf = _rms_bwd(dh2, y2n, r2, w)
        dy2_ref[...] = dy2.astype(dy2_ref.dtype)
        dwf_ref[...] += dwf

    row = pl.BlockSpec((tt, D_MODEL), lambda i: (i, 0))
    vec = pl.BlockSpec((1, D_MODEL), lambda i: (0, 0))
    resident = pl.BlockSpec((nj, FF_SHARD, D_MODEL), lambda i: (0, 0, 0), pipeline_mode=pl.Buffered(1))
    act = pl.BlockSpec((nj, tt, FF_SHARD), lambda i: (0, i, 0))
    sds = jax.ShapeDtypeStruct
    return pl.pallas_call(
        body, name="ffn_fwd", grid=(T // tt,),
        in_specs=[row, row, row, vec, resident, resident, resident],
        out_specs=[act, act, row, row, vec, vec],
        out_shape=[sds((nj, T, FF_SHARD), F32), sds((nj, T, FF_SHARD), F32), sds((T, D_MODEL), MXU_DTYPE),
                   sds((T, D_MODEL), F32), sds((1, D_MODEL), F32), sds((1, D_MODEL), F32)],
        compiler_params=_params(("arbitrary",)),
    )(zb, h1, tgt, w_fpost, wg, wu, wd)


def _ffn_bwd(zb, g, up, dy2b, wg, wu, wd, tt=512):
    T = zb.shape[0]
    nj = N_CHIPS

    def body(z_ref, g_ref, up_ref, dy2_ref, wg_ref, wu_ref, wd_ref, dwg_ref, dwu_ref, dwd_ref, dz_ref):
        @pl.when(pl.program_id(1) == 0)
        def _():
            dwg_ref[...] = jnp.zeros_like(dwg_ref)
            dwu_ref[...] = jnp.zeros_like(dwu_ref)
            dwd_ref[...] = jnp.zeros_like(dwd_ref)

        z, g_, up_, dy2 = z_ref[...], g_ref[0], up_ref[0], dy2_ref[...]
        sg = jax.nn.sigmoid(g_)
        act = g_ * sg
        dff = _mm_nt(dy2, wd_ref[0])
        dwd_ref[0] += _mm_tn(act * up_, dy2)
        dg = dff * up_ * sg * (1.0 + g_ * (1.0 - sg))
        dup = dff * act
        dwg_ref[0] += _mm_tn(dg, z)
        dwu_ref[0] += _mm_tn(dup, z)
        dz_ref[0] = _mm(dg, wg_ref[0]) + _mm(dup, wu_ref[0])

    row = pl.BlockSpec((tt, D_MODEL), lambda j, i: (i, 0))
    act = pl.BlockSpec((1, tt, FF_SHARD), lambda j, i: (j, i, 0))
    w_sh = pl.BlockSpec((1, FF_SHARD, D_MODEL), lambda j, i: (j, 0, 0))
    w_grad = jax.ShapeDtypeStruct((nj, FF_SHARD, D_MODEL), F32)
    return pl.pallas_call(
        body, name="ffn_bwd", grid=(nj, T // tt),
        in_specs=[row, act, act, row, w_sh, w_sh, w_sh],
        out_specs=[w_sh, w_sh, w_sh, pl.BlockSpec((1, tt, D_MODEL), lambda j, i: (j, i, 0))],
        out_shape=[w_grad, w_grad, w_grad, jax.ShapeDtypeStruct((nj, T, D_MODEL), F32)],
        compiler_params=_params(("arbitrary", "arbitrary")),
    )(zb, g, up, dy2b, wg, wu, wd)


def _mid_bwd(dzp, dh2, h1, mixb, o_raw, oh_raw, xph, wout, w_fpre, w_post, w_mla, w_hg, swap=(), tt=256):
    T = dh2.shape[0]
    nsw = len(swap)
    n_in, n_out = 12, 10

    def body(*refs):
        (dzp_ref, dh2_ref, h1_ref, mix_ref, o_ref, oh_ref, hg_ref, wout_ref, wfpre_ref, wpost_ref,
         wmla_ref, whg_ref) = refs[:n_in]
        (dh1_ref, dwout_ref, do_ref, doh_ref, dhg_ref, dvec_ref, dwfpre_ref, dwpost_ref, dwmla_ref,
         dwhg_ref) = refs[n_in + nsw:n_in + nsw + n_out]
        swap_copies = lambda: _pair_swap_copies(refs[n_in:n_in + nsw], refs[n_in + nsw + n_out:n_in + 2 * nsw + n_out],
                                                *refs[n_in + 2 * nsw + n_out:])

        @pl.when(pl.program_id(0) == 0)
        def _():
            for r in (dwout_ref, dwfpre_ref, dwpost_ref, dwmla_ref, dwhg_ref):
                r[...] = jnp.zeros_like(r)
            for cp in (swap_copies() if nsw else ()):
                cp.start()

        y1 = _mm(mix_ref[...], wout_ref[...])
        dz = dzp_ref[0] + dzp_ref[1] + dzp_ref[2] + dzp_ref[3]
        wfpre = wfpre_ref[...]
        _, h1n, r = _rms_fwd(h1_ref[...], wfpre)
        dh1_z, dwfpre = _rms_bwd(dz, h1n, r, wfpre)
        dwfpre_ref[...] += dwfpre
        dh1 = dh2_ref[...] + dh1_z
        dh1_ref[...] = dh1
        wpost = wpost_ref[...]
        _, y1n, r1 = _rms_fwd(y1, wpost)
        dy1, dwpost = _rms_bwd(dh1, y1n, r1, wpost)
        dwpost_ref[...] += dwpost
        dmix = _mm_nt(dy1, wout_ref[...])
        dwout_ref[...] += _mm_tn(mix_ref[...], dy1)
        wmla = wmla_ref[...]
        o = o_ref[...]
        _, on, ro = _grms_fwd(o, wmla, MLA_V)
        d_o, dwmla = _grms_bwd(dmix[:, :MLA_WIDTH], on, ro, wmla, MLA_V)
        dwmla_ref[...] += dwmla
        do_ref[...] = d_o.astype(do_ref.dtype)
        hh = lax.broadcasted_iota(jnp.int32, (MLA_HEADS, MLA_WIDTH), 0)
        ll = lax.broadcasted_iota(jnp.int32, (MLA_HEADS, MLA_WIDTH), 1)
        sel = jnp.where((ll >= hh * MLA_V) & (ll < (hh + 1) * MLA_V), 1.0, 0.0)
        dvec_ref[...] = _mm_nt(sel, d_o * o, True)
        whg = whg_ref[...]
        hg = hg_ref[...]
        sg = jax.nn.sigmoid(hg)
        _, ohn, rh = _grms_fwd(oh_ref[...], whg, HGRN_DIM)
        dmh = dmix[:, MLA_WIDTH:]
        dhg_ref[...] = dmh * ohn * whg * sg * (1.0 + hg * (1.0 - sg))
        d_oh, dwhg = _grms_bwd(dmh * (hg * sg), ohn, rh, whg, HGRN_DIM)
        dwhg_ref[...] += dwhg
        doh_ref[...] = d_oh

        if nsw:
            @pl.when(pl.program_id(0) == T // tt - 1)
            def _():
                for cp in swap_copies():
                    cp.wait()

    row = lambda w: pl.BlockSpec((tt, w), lambda i: (i, 0))
    full = lambda a: pl.BlockSpec(a.shape, lambda i: (0,) * a.ndim)
    vec = lambda w: pl.BlockSpec((1, w), lambda i: (0, 0))
    sds = jax.ShapeDtypeStruct
    return pl.pallas_call(
        body, name="mid_bwd", grid=(T // tt,),
        in_specs=[pl.BlockSpec((N_CHIPS, tt, D_MODEL), lambda i: (0, i, 0)), row(D_MODEL), row(D_MODEL),
                  row(D_MODEL), row(MLA_WIDTH), row(HGRN_WIDTH), pl.BlockSpec((tt, HGRN_WIDTH), lambda i: (i, 3)),
                  full(wout), vec(D_MODEL), vec(D_MODEL), vec(MLA_WIDTH), vec(HGRN_WIDTH)] + [ANY] * nsw,
        out_specs=[row(D_MODEL), full(wout), row(MLA_WIDTH), row(HGRN_WIDTH), row(HGRN_WIDTH),
                   pl.BlockSpec((MLA_HEADS, tt), lambda i: (0, i)),
                   vec(D_MODEL), vec(D_MODEL), vec(MLA_WIDTH), vec(HGRN_WIDTH)] + [ANY] * nsw,
        out_shape=[sds((T, D_MODEL), F32), sds(wout.shape, F32), sds((T, MLA_WIDTH), MXU_DTYPE), sds((T, HGRN_WIDTH), F32),
                   sds((T, HGRN_WIDTH), F32), sds((MLA_HEADS, T), F32),
                   sds((1, D_MODEL), F32), sds((1, D_MODEL), F32), sds((1, MLA_WIDTH), F32), sds((1, HGRN_WIDTH), F32)]
        + _half_stack_shapes(swap),
        scratch_shapes=[pltpu.SemaphoreType.DMA((nsw,)), pltpu.SemaphoreType.DMA((nsw,))] if nsw else [],
        compiler_params=_params(("arbitrary",)),
    )(dzp, dh2, h1, mixb, o_raw, oh_raw, xph, wout, w_fpre, w_post, w_mla, w_hg, *swap)


def _in_bwd(x, dh1, cq, ckv, dq, dk, dv, dhq, dhf, dhi, dhg, rc, rs, w_pre, win, qnw, wq, kvnw, wk, wv, tt=256):
    T = x.shape[0]

    def body(x_ref, dh1_ref, cq_ref, ckv_ref, dq_ref, dk_ref, dv_ref, dhq_ref, dhf_ref, dhi_ref, dhg_ref, rc_ref, rs_ref,
             wpre_ref, win_ref, qnw_ref, wq_ref, kvnw_ref, wk_ref, wv_ref,
             dx_ref, dwin_ref, dwq_ref, dwk_ref, dwv_ref, dwpre_ref, dqnw_ref, dkvnw_ref):
        @pl.when(pl.program_id(0) == 0)
        def _():
            for r in (dwin_ref, dwq_ref, dwk_ref, dwv_ref, dwpre_ref, dqnw_ref, dkvnw_ref):
                r[...] = jnp.zeros_like(r)

        def add_win_grad(r, first):
            for arr0, n, chip, row0 in _win_grad_segments():
                if first <= arr0 and arr0 + n <= first + r.shape[0]:
                    dwin_ref[chip, row0:row0 + n, :] += r[arr0 - first:arr0 - first + n]

        lo = Q_RANK + KV_RANK + HEAD_PAD
        wpre = wpre_ref[...]
        u, xn, rx = _rms_fwd(x_ref[...], wpre)
        dxp_h = jnp.concatenate([dhq_ref[...], dhf_ref[...], dhi_ref[...], dhg_ref[...]], axis=-1)
        add_win_grad(_mm_tn(dxp_h, u), lo)
        du = _mm(dxp_h, win_ref[lo:, :])
        c, sa, sb = _rope_tables(rc_ref[...], rs_ref[...])
        lane = lax.broadcasted_iota(jnp.int32, (tt, HEAD_PAD), 1)
        dk_all = dk_ref[...]
        dq_lin = []
        dkr = jnp.zeros((tt, HEAD_PAD), F32)
        for h in range(MLA_HEADS):
            sl = slice(HEAD_PAD * h, HEAD_PAD * (h + 1))
            dq_lin.append(_rope_bwd(dq_ref[sl, :].T, c, sa, sb))
            dkr = dkr + dk_all[:, sl]
        dq_lin = jnp.concatenate(dq_lin, axis=-1)
        dkr = jnp.where((lane >= MLA_NOPE) & (lane < MLA_QK), _rope_bwd(dkr, c, sa, sb), 0.0)
        qnw = qnw_ref[...]
        qn, cqn, rq = _rms_fwd(cq_ref[...], qnw)
        dwq_ref[...] += _mm_tn(qn, dq_lin)
        dcq, dqnw = _rms_bwd(_mm_nt(dq_lin, wq_ref[...]), cqn, rq, qnw)
        dqnw_ref[...] += dqnw
        kvnw = kvnw_ref[...]
        kvn, ckvn, rkv = _rms_fwd(ckv_ref[...], kvnw)
        dv_ = dv_ref[...]
        dwk_ref[...] += _mm_tn(kvn, dk_all)
        dwv_ref[...] += _mm_tn(kvn, dv_)
        dckv, dkvnw = _rms_bwd(_mm_nt(dk_all, wk_ref[...]) + _mm_nt(dv_, wv_ref[...]), ckvn, rkv, kvnw)
        dkvnw_ref[...] += dkvnw
        dxp_a = jnp.concatenate([dcq, dckv, dkr], axis=-1)
        add_win_grad(_mm_tn(dxp_a, u), 0)
        dx_u, dwpre = _rms_bwd(du + _mm(dxp_a, win_ref[:lo, :]), xn, rx, wpre)
        dwpre_ref[...] += dwpre
        dx_ref[...] = dh1_ref[...] + dx_u

    row = lambda w: pl.BlockSpec((tt, w), lambda i: (i, 0))
    full = lambda a: pl.BlockSpec(a.shape, lambda i: (0,) * a.ndim)
    sds = jax.ShapeDtypeStruct
    qk_w = MLA_HEADS * HEAD_PAD
    return pl.pallas_call(
        body, name="in_bwd", grid=(T // tt,),
        in_specs=[row(D_MODEL), row(D_MODEL), row(Q_RANK), row(KV_RANK), pl.BlockSpec((qk_w, tt), lambda i: (0, i)),
                  row(qk_w), row(MLA_WIDTH),
                  row(HGRN_WIDTH), row(HGRN_WIDTH), row(HGRN_WIDTH), row(HGRN_WIDTH), row(HEAD_PAD), row(HEAD_PAD),
                  full(w_pre), full(win), full(qnw), full(wq), full(kvnw), full(wk), full(wv)],
        out_specs=[row(D_MODEL), pl.BlockSpec(WIN_COMM_SHAPE, lambda i: (0, 0, 0)), full(wq), full(wk), full(wv),
                   full(w_pre), full(qnw), full(kvnw)],
        out_shape=[sds((T, D_MODEL), F32), sds(WIN_COMM_SHAPE, F32), sds(wq.shape, F32), sds(wk.shape, F32),
                   sds(wv.shape, F32), sds(w_pre.shape, F32), sds(qnw.shape, F32), sds(kvnw.shape, F32)],
        compiler_params=_params(("arbitrary",)),
    )(x, dh1, cq, ckv, dq, dk, dv, dhq, dhf, dhi, dhg, rc, rs, w_pre, win, qnw, wq, kvnw, wk, wv)


def _arrange_weights(win_t, wuq_full, wukv):
    dt = win_t.dtype
    z = lambda n: jnp.zeros((n, D_MODEL), dt)
    s2 = Q_RANK + KV_RANK
    win_arr = jnp.concatenate([win_t[:s2], z(MLA_NOPE), win_t[s2:s2 + MLA_ROPE], z(HEAD_PAD - MLA_QK),
                               win_t[s2 + MLA_ROPE:]], axis=0)
    wq_arr = jnp.pad(wuq_full, ((0, 0), (0, 0), (0, HEAD_PAD - MLA_QK))).reshape(Q_RANK, MLA_HEADS * HEAD_PAD)
    wk_arr = jnp.pad(wukv[:, :, :MLA_NOPE], ((0, 0), (0, 0), (0, HEAD_PAD - MLA_NOPE))).reshape(
        KV_RANK, MLA_HEADS * HEAD_PAD)
    wv_arr = wukv[:, :, MLA_NOPE:].reshape(KV_RANK, MLA_WIDTH)
    return win_arr, wq_arr, wk_arr, wv_arr


WIN_COMM_SHAPE = (N_CHIPS, FF_SHARD, D_MODEL)


def _win_grad_segments():
    s2 = Q_RANK + KV_RANK
    runs = [(0, s2, 0), (s2, s2 + MLA_ROPE, MLA_NOPE), (s2 + MLA_ROPE, D_IN, HEAD_PAD - MLA_ROPE)]
    per = D_IN // N_CHIPS
    segs = []
    for lo, hi, shift in runs:
        for k in range(N_CHIPS):
            a, b = max(lo, per * k), min(hi, per * (k + 1))
            if a < b:
                segs.append((a + shift, b - a, k, a - per * k))
    return segs


def _unarrange_grads(dwq_arr, dwk_arr, dwv_arr):
    dwuq = dwq_arr.reshape(Q_RANK, MLA_HEADS, HEAD_PAD)[:, :, :MLA_QK]
    dwukv = jnp.concatenate([dwk_arr.reshape(KV_RANK, MLA_HEADS, HEAD_PAD)[:, :, :MLA_NOPE],
                             dwv_arr.reshape(KV_RANK, MLA_HEADS, MLA_V)], axis=-1)
    return dwuq, dwukv


def _rope_inv_freq():
    inv = 1.0 / (ROPE_THETA ** (jnp.arange(0, MLA_ROPE, 2, dtype=F32) / MLA_ROPE))
    z = lambda n: jnp.zeros((n,), F32)
    return jnp.concatenate([z(MLA_NOPE), inv, inv, z(HEAD_PAD - MLA_QK)]).reshape(1, HEAD_PAD)


def _local_step(x, pos, tgt, small, win_arr, wq_arr, wk_arr, wv_arr, late, place=None):
    invf = _rope_inv_freq()
    cq, ckv, xph, qb, kb, vb, kt, vt, rc, rs = _in_fwd(x, pos, invf, small["attn_pre_norm"], win_arr, small["mla_q_norm"],
                                               wq_arr, small["mla_kv_norm"], wk_arr, wv_arr)
    if place is None:
        o_raw, lse = _attn_fwd_t(qb, kb, vt)
        wout, wg, wu, wd = late
    else:
        o_raw, lse, *stacks = _attn_fwd_t(qb, kb, vt, gather=late)
        wout, wg, wu, wd = [lax.dynamic_update_slice(s, l[None], (place[1], 0, 0)) for s, l in zip(stacks, late)]
        wout = wout.reshape(D_MODEL, D_MODEL)
    oh_raw, states = _hgrn_fwd(xph, small["hgrn_lb_logits"])
    h1, zb, mixb = _proj_fwd(x, o_raw, oh_raw, xph, wout, small["mla_out_norm"], small["hgrn_out_norm"],
                                 small["attn_post_norm"], small["ffn_pre_norm"])
    g, up, dy2b, dh2, loss_acc, d_fpost = _ffn_fwd(zb, h1, tgt, small["ffn_post_norm"], wg, wu, wd)
    dwg, dwu, dwd, dzp = _ffn_bwd(zb, g, up, dy2b, wg, wu, wd)
    ffn_grads = [] if place is None else [dwg, dwu, dwd]
    dh1, dwout, d_o, d_oh, dhg, dvec, d_fpre, d_post, d_mla, d_hg, *ffn_rs = _mid_bwd(
        dzp, dh2, h1, mixb, o_raw, oh_raw, xph, wout, small["ffn_pre_norm"], small["attn_post_norm"],
        small["mla_out_norm"], small["hgrn_out_norm"], swap=ffn_grads)
    ffn_ps = _pair_sum(place, ffn_grads, ffn_rs, name="pair_sum_ffn") if ffn_grads else []
    dq, dk, dv, *ffn_ris = _attn_bwd_t(qb, kb, kt, vb, d_o, lse, dvec.reshape(lse.shape), send=ffn_ps)
    dhq, dhf, dhi, d_lbl = _hgrn_bwd(xph, small["hgrn_lb_logits"], states, d_oh)
    dx, dwin4, dwq_arr, dwk_arr, dwv_arr, d_pre, d_qn, d_kvn = _in_bwd(
        x, dh1, cq, ckv, dq, dk, dv, dhq, dhf, dhi, dhg, rc, rs, small["attn_pre_norm"], win_arr,
        small["mla_q_norm"], wq_arr, small["mla_kv_norm"], wk_arr, wv_arr)
    dwuq, dwukv = _unarrange_grads(dwq_arr, dwk_arr, dwv_arr)
    loss = 0.5 * jnp.sum(loss_acc) * (1.0 / D_MODEL)
    grads = dict(attn_pre_norm=d_pre, w_in=dwin4, mla_q_norm=d_qn, mla_w_uq=dwuq, mla_kv_norm=d_kvn, mla_w_ukv=dwukv,
                 mla_out_norm=d_mla, hgrn_lb_logits=d_lbl, hgrn_out_norm=d_hg, w_out=dwout, attn_post_norm=d_post,
                 ffn_pre_norm=d_fpre, w_gate=dwg, w_up=dwu, w_down=dwd, ffn_post_norm=d_fpost)
    if place is None:
        return loss, dx, grads
    return loss, dx, grads, (ffn_rs, ffn_ris)


def _place():
    x, y, c = lax.axis_index("x"), lax.axis_index("y"), lax.axis_index("c")
    others = [(1 - x, y), (x, 1 - y), (1 - x, 1 - y)]
    return x, y, c, 2 * x + y, (x, y, 1 - c), others


def _half(ref, c, rows):
    return ref.at[pl.ds(pl.multiple_of(c * rows, 8), rows)]


def _rcopy(src, dst, send, recv, k, to):
    return pltpu.make_async_remote_copy(src_ref=src, dst_ref=dst, send_sem=send.at[k], recv_sem=recv.at[k],
                                        device_id=to, device_id_type=MESH)


class _Gather:
    def __init__(self, ins, outs, send, recv):
        self.ins, self.outs, self.send, self.recv = ins, outs, send, recv
        self.n = len(ins)
        self.halves = [r.shape[0] // 2 for r in ins]
        _, _, self.c, self.me, self.sib, self.others = _place()

    def _each(self):
        for j, (px, py) in enumerate(self.others):
            for a in range(self.n):
                yield j * self.n + a, a, 2 * px + py, (px, py, self.c)

    def sends(self):
        return [_rcopy(_half(self.ins[a], self.c, self.halves[a]), _half(self.outs[a].at[self.me], self.c, self.halves[a]),
                       self.send, self.recv, k, to) for k, a, _, to in self._each()]

    def arrivals(self):
        parts = [(k, _half(self.outs[a].at[chip], self.c, self.halves[a]), to) for k, a, chip, to in self._each()]
        return [_rcopy(p, p, self.send, self.recv, k, to) for k, p, to in parts]

    def forwards(self):
        parts = [(k, _half(self.outs[a].at[chip], self.c, self.halves[a])) for k, a, chip, _ in self._each()]
        return [_rcopy(p, p, self.send, self.recv, 3 * self.n + k, self.sib) for k, p in parts]

    def forward_arrivals(self):
        parts = [(k, _half(self.outs[a].at[chip], 1 - self.c, self.halves[a])) for k, a, chip, _ in self._each()]
        return [_rcopy(p, p, self.send, self.recv, 3 * self.n + k, self.sib) for k, p in parts]

    @staticmethod
    def out_shapes(arrs):
        return [jax.ShapeDtypeStruct((N_CHIPS,) + a.shape, a.dtype) for a in arrs]

    @staticmethod
    def semaphores(arrs):
        return [pltpu.SemaphoreType.DMA((6 * len(arrs),)), pltpu.SemaphoreType.DMA((6 * len(arrs),))]


def _gather_chips(arrs, name):
    n = len(arrs)

    def body(*refs):
        gat = _Gather(refs[:n], refs[n:2 * n], *refs[2 * n:])
        sends, forwards = gat.sends(), gat.forwards()
        for cp in sends:
            cp.start()
        for arrival, fw in zip(gat.arrivals(), forwards):
            arrival.wait_recv()
            fw.start()
        for arrival in gat.forward_arrivals():
            arrival.wait_recv()
        for cp in sends + forwards:
            cp.wait_send()

    return pl.pallas_call(body, name=name, in_specs=[ANY] * n, out_specs=[ANY] * n, out_shape=_Gather.out_shapes(arrs),
                          scratch_shapes=_Gather.semaphores(arrs))(*arrs)


GRAD_BLOCKS = 2


def _pair_swap_copies(g_refs, r_refs, send, recv):
    _, _, c, _, sib, _ = _place()
    copies = []
    for a, (g, r) in enumerate(zip(g_refs, r_refs)):
        h = g.shape[1] // 2
        copies.append(_rcopy(g.at[:, pl.ds(pl.multiple_of((1 - c) * h, 8), h)], r, send, recv, a, sib))
    return copies


def _half_stack_shapes(gs, dtype=None):
    return [jax.ShapeDtypeStruct((N_CHIPS, g.shape[1] // 2, g.shape[2]), dtype or g.dtype) for g in gs]


def _pair_swap(gs, sm):
    n = len(gs)

    def body(*refs):
        g_refs, sm_ref = refs[:n], refs[n]
        r_refs, ssib_ref = refs[n + 1:2 * n + 1], refs[2 * n + 1]
        send, recv = refs[2 * n + 2:]
        copies = _pair_swap_copies(g_refs, r_refs, send, recv)
        copies.append(_rcopy(sm_ref, ssib_ref, send, recv, n, _place()[4]))
        for cp in copies:
            cp.start()
        for cp in copies:
            cp.wait()

    return pl.pallas_call(
        body, name="pair_swap", in_specs=[ANY] * (n + 1), out_specs=[ANY] * (n + 1),
        out_shape=_half_stack_shapes(gs) + [jax.ShapeDtypeStruct(sm.shape, sm.dtype)],
        scratch_shapes=[pltpu.SemaphoreType.DMA((n + 1,)), pltpu.SemaphoreType.DMA((n + 1,))],
    )(*gs, sm)


def _pair_sum(place, gs, rs, small=None, name="pair_sum"):
    n = len(gs)
    nb = GRAD_BLOCKS

    def body(place_ref, *refs):
        g_refs, r_refs, p_refs = refs[:n], refs[n:2 * n], refs[-n - 1:-1] if small else refs[-n:]
        for a in range(n):
            p_refs[a][0] = (g_refs[a][0] + r_refs[a][0]).astype(p_refs[a].dtype)
        if small:
            @pl.when((pl.program_id(0) == 0) & (pl.program_id(1) == 0))
            def _():
                refs[-1][...] = refs[2 * n][...] + refs[2 * n + 1][...]

    in_specs, out_specs = [], []
    for g in gs:
        blk = (1, g.shape[1] // 2 // nb, g.shape[2])
        in_specs.append(pl.BlockSpec(blk, lambda i, k, p: (k, p[0] * nb + i, 0)))
    for g in gs:
        blk = (1, g.shape[1] // 2 // nb, g.shape[2])
        in_specs.append(pl.BlockSpec(blk, lambda i, k, p: (k, i, 0)))
        out_specs.append(pl.BlockSpec(blk, lambda i, k, p: (k, i, 0)))
    out_shape = _half_stack_shapes(gs, BF16)
    if small:
        sm_spec = pl.BlockSpec(small[0].shape, lambda i, k, p: (0, 0))
        in_specs += [sm_spec, sm_spec]
        out_specs.append(sm_spec)
        out_shape.append(jax.ShapeDtypeStruct(small[0].shape, F32))
    return pl.pallas_call(
        body, name=name,
        grid_spec=pltpu.PrefetchScalarGridSpec(num_scalar_prefetch=1, grid=(nb, N_CHIPS), in_specs=in_specs,
                                               out_specs=out_specs),
        out_shape=out_shape,
        compiler_params=_params(("arbitrary", "arbitrary")),
    )(place, *gs, *rs, *(small or ()))


def _chip_swap_copies(p_refs, ri_refs, send, recv):
    _, _, c, _, _, others = _place()
    n = len(p_refs)
    return [_rcopy(p_refs[a].at[2 * px + py], ri_refs[a].at[j], send, recv, j * n + a, (px, py, c))
            for j, (px, py) in enumerate(others) for a in range(n)]


def _chip_swap_shapes(ps):
    return [jax.ShapeDtypeStruct((3,) + p.shape[1:], p.dtype) for p in ps]


def _chip_swap(ps, pair):
    n = len(ps)
    hs = SMALL_ROWS // 2

    def body(*refs):
        p_refs, pair_ref = refs[:n], refs[n]
        ri_refs, sm4_ref = refs[n + 1:2 * n + 1], refs[2 * n + 1]
        send, recv, lsem = refs[2 * n + 2:]
        x, y, c, me, sib, others = _place()
        local = pltpu.make_async_copy(pair_ref, sm4_ref.at[me], lsem.at[0])
        local.start()
        copies = _chip_swap_copies(p_refs, ri_refs, send, recv)
        arrivals = list(copies)
        for j, (px, py) in enumerate(others):
            copies.append(_rcopy(_half(pair_ref, c, hs), _half(sm4_ref.at[me], c, hs), send, recv, 3 * n + j, (px, py, c)))
            part = _half(sm4_ref.at[2 * px + py], c, hs)
            arrivals.append(_rcopy(part, part, send, recv, 3 * n + j, (px, py, c)))
        for cp in copies:
            cp.start()
        for arrival in arrivals:
            arrival.wait_recv()
        for cp in copies:
            cp.wait_send()
        local.wait()

    k = 3 * (n + 1)
    return pl.pallas_call(
        body, name="chip_swap", in_specs=[ANY] * (n + 1), out_specs=[ANY] * (n + 1),
        out_shape=_chip_swap_shapes(ps) + [jax.ShapeDtypeStruct((N_CHIPS,) + pair.shape, pair.dtype)],
        scratch_shapes=[pltpu.SemaphoreType.DMA((k,)), pltpu.SemaphoreType.DMA((k,)), pltpu.SemaphoreType.DMA((1,))],
    )(*ps, pair)


def _chip_sum(place, gs, rs, ris):
    n = len(gs)
    nb = GRAD_BLOCKS

    def body(place_ref, *refs):
        g_refs, r_refs, ri_refs, o_refs = refs[:n], refs[n:2 * n], refs[2 * n:3 * n], refs[3 * n:]
        for a in range(n):
            ri = ri_refs[a]
            o_refs[a][...] = (g_refs[a][0] + r_refs[a][0]) + ri[0].astype(F32) + ri[1].astype(F32) + ri[2].astype(F32)

    in_specs, out_specs, out_shape = [], [], []
    for g in gs:
        blk = (1, g.shape[1] // 2 // nb, g.shape[2])
        in_specs.append(pl.BlockSpec(blk, lambda i, p: (p[1], p[0] * nb + i, 0)))
    for g in gs:
        blk = (1, g.shape[1] // 2 // nb, g.shape[2])
        in_specs.append(pl.BlockSpec(blk, lambda i, p: (p[1], i, 0)))
    for g in gs:
        rb = g.shape[1] // 2 // nb
        in_specs.append(pl.BlockSpec((3, rb, g.shape[2]), lambda i, p: (0, i, 0)))
        out_specs.append(pl.BlockSpec((rb, g.shape[2]), lambda i, p: (p[0] * nb + i, 0)))
        out_shape.append(jax.ShapeDtypeStruct(g.shape[1:], F32))
    return pl.pallas_call(
        body, name="chip_sum",
        grid_spec=pltpu.PrefetchScalarGridSpec(num_scalar_prefetch=1, grid=(nb,), in_specs=in_specs, out_specs=out_specs),
        out_shape=out_shape,
        compiler_params=_params(("arbitrary",)),
    )(place, *gs, *rs, *ris)


def _pair_fill(gfs, sm4):
    n = len(gfs)
    hs = SMALL_ROWS // 2

    def body(*refs):
        g_refs, sm4_ref = refs[n + 1:2 * n + 1], refs[2 * n + 1]
        send, recv = refs[2 * n + 2:]
        x, y, c, me, sib, others = _place()
        copies, waits = [], []
        for a in range(n):
            h = gfs[a].shape[0] // 2
            mine, theirs = _half(g_refs[a], c, h), _half(g_refs[a], 1 - c, h)
            copies.append(pltpu.make_async_remote_copy(src_ref=mine, dst_ref=mine, send_sem=send.at[a],
                                                       recv_sem=recv.at[a], device_id=sib, device_id_type=MESH))
            waits.append(pltpu.make_async_remote_copy(src_ref=theirs, dst_ref=theirs, send_sem=send.at[a],
                                                      recv_sem=recv.at[a], device_id=sib, device_id_type=MESH))
        for j, (px, py) in enumerate(others):
            chip = 2 * px + py
            mine, theirs = _half(sm4_ref.at[chip], c, hs), _half(sm4_ref.at[chip], 1 - c, hs)
            copies.append(pltpu.make_async_remote_copy(src_ref=mine, dst_ref=mine, send_sem=send.at[n + j],
                                                       recv_sem=recv.at[n + j], device_id=sib, device_id_type=MESH))
            waits.append(pltpu.make_async_remote_copy(src_ref=theirs, dst_ref=theirs, send_sem=send.at[n + j],
                                                      recv_sem=recv.at[n + j], device_id=sib, device_id_type=MESH))
        for cp in copies:
            cp.start()
        for w in waits:
            w.wait_recv()
        for cp in copies:
            cp.wait_send()

    return pl.pallas_call(
        body, name="pair_fill", in_specs=[ANY] * (n + 1), out_specs=[ANY] * (n + 1),
        out_shape=[jax.ShapeDtypeStruct(g.shape, g.dtype) for g in gfs] + [jax.ShapeDtypeStruct(sm4.shape, sm4.dtype)],
        input_output_aliases={i: i for i in range(n + 1)},
        scratch_shapes=[pltpu.SemaphoreType.DMA((n + 3,)), pltpu.SemaphoreType.DMA((n + 3,))],
    )(*gfs, sm4)


def _adamw_math(w, g, m, v):
    m = ADAM_B1 * m + (1.0 - ADAM_B1) * g
    v = ADAM_B2 * v + (1.0 - ADAM_B2) * (g * g)
    m_hat = m / (1.0 - ADAM_B1 ** ADAM_STEP)
    v_hat = v / (1.0 - ADAM_B2 ** ADAM_STEP)
    return -ADAM_LR * (m_hat / (jnp.sqrt(v_hat) + ADAM_EPS) + ADAM_WD * w), m, v


def _adamw(items, steps, name):
    n = len(items)

    def body(*refs):
        for a in range(n):
            d, mo, vo = _adamw_math(*(r[...] for r in refs[4 * a:4 * a + 4]))
            for out, val in zip(refs[4 * n + 3 * a:4 * n + 3 * a + 3], (d, mo, vo)):
                out[...] = val

    spec = lambda w: pl.BlockSpec((w.shape[0] // steps, w.shape[1]), lambda i: (i, 0))
    flat = pl.pallas_call(
        body, name=name, grid=(steps,), in_specs=[spec(it[0]) for it in items for _ in range(4)],
        out_specs=[spec(it[0]) for it in items for _ in range(3)],
        out_shape=[jax.ShapeDtypeStruct(it[0].shape, F32) for it in items for _ in range(3)],
        compiler_params=_params(("arbitrary",)),
    )(*[a for it in items for a in it])
    return [flat[3 * a:3 * a + 3] for a in range(n)]


def _adamw_small(sm4, wmv):
    views = SMALL_VIEWS[:-1]
    n = len(views)

    def body(sm4_ref, *refs):
        g_all = ((sm4_ref[0] + sm4_ref[1]) + sm4_ref[2]) + sm4_ref[3]
        row = 0
        for a, (_, rows, cols) in enumerate(views):
            g = g_all[row:row + rows, :cols]
            row += -(-rows // ROW_TILE) * ROW_TILE
            d, mo, vo = _adamw_math(refs[3 * a][...], g, refs[3 * a + 1][...], refs[3 * a + 2][...])
            for out, val in zip(refs[3 * n + 4 * a:3 * n + 4 * a + 4], (g, d, mo, vo)):
                out[...] = val
        refs[-1][...] = g_all[row:row + 1, :128]

    flat = pl.pallas_call(
        body, name="adamw_small",
        out_shape=[jax.ShapeDtypeStruct((rows, cols), F32) for _, rows, cols in views for _ in range(4)]
        + [jax.ShapeDtypeStruct((1, 128), F32)],
        compiler_params=pltpu.CompilerParams(vmem_limit_bytes=VMEM_LIMIT),
    )(sm4, *[a for t in wmv for a in t])
    return [flat[4 * a:4 * a + 4] for a in range(n)] + [flat[-1]]


SMALL_NAMES = ("attn_pre_norm", "mla_q_norm", "mla_kv_norm", "mla_w_ukv", "mla_out_norm", "hgrn_lb_logits",
               "hgrn_out_norm", "attn_post_norm", "ffn_pre_norm", "ffn_post_norm")
BIG_NAMES = ("w_in", "mla_w_uq", "w_out", "w_gate", "w_up", "w_down")
WEIGHT_NAMES = ("attn_pre_norm", "w_in", "mla_q_norm", "mla_w_uq", "mla_kv_norm", "mla_w_ukv", "mla_out_norm",
                "hgrn_lb_logits", "hgrn_out_norm", "w_out", "attn_post_norm", "ffn_pre_norm", "w_gate", "w_up", "w_down",
                "ffn_post_norm")


UQ_COMM_SHAPE = (192, 384)


def _pack_small(vals):
    parts = []
    for name, rows, cols in SMALL_VIEWS:
        pad_rows = -(-rows // ROW_TILE) * ROW_TILE - rows
        parts.append(jnp.pad(vals[name].reshape(rows, cols), ((0, pad_rows), (0, D_MODEL - cols))))
    return jnp.concatenate(parts, axis=0)


def kernel(x, positions, attn_pre_norm, w_in, mla_q_norm, mla_w_uq, mla_kv_norm, mla_w_ukv, mla_out_norm, hgrn_lb_logits, hgrn_out_norm, w_out, attn_post_norm, ffn_pre_norm, w_gate, w_up, w_down, ffn_post_norm, loss_target, m_attn_pre_norm, m_w_in, m_mla_q_norm, m_mla_w_uq, m_mla_kv_norm, m_mla_w_ukv, m_mla_out_norm, m_hgrn_lb_logits, m_hgrn_out_norm, m_w_out, m_attn_post_norm, m_ffn_pre_norm, m_w_gate, m_w_up, m_w_down, m_ffn_post_norm, v_attn_pre_norm, v_w_in, v_mla_q_norm, v_mla_w_uq, v_mla_kv_norm, v_mla_w_ukv, v_mla_out_norm, v_hgrn_lb_logits, v_hgrn_out_norm, v_w_out, v_attn_post_norm, v_ffn_pre_norm, v_w_gate, v_w_up, v_w_down, v_ffn_post_norm):
    args = locals()
    W = {n: args[n] for n in WEIGHT_NAMES}
    M = {n: args["m_" + n] for n in WEIGHT_NAMES}
    V = {n: args["v_" + n] for n in WEIGHT_NAMES}
    T = x.shape[1]
    cx, cy, cc = lax.axis_index("x"), lax.axis_index("y"), lax.axis_index("c")

    win_rows = D_IN // N_CHIPS
    shard2d = {"w_in": (win_rows, D_MODEL), "mla_w_uq": (Q_RANK // N_CHIPS, MLA_HEADS * MLA_QK),
               "w_out": (D_MODEL // N_CHIPS, D_MODEL), "w_gate": (FF_SHARD, D_MODEL), "w_up": (FF_SHARD, D_MODEL),
               "w_down": (FF_SHARD, D_MODEL)}
    transposed = ("w_in", "w_gate", "w_up")
    to2d = lambda n, a: a[0].T if n in transposed else a.reshape(shard2d[n])
    from2d = lambda n, t: t.T[None] if n in transposed else t.reshape(W[n].shape)
    me = 2 * cx + cy
    place = jnp.stack([cc, me]).astype(jnp.int32)
    local_b = [to2d(n, W[n]).astype(BF16) for n in BIG_NAMES]
    local_b[0] = jnp.pad(local_b[0], ((0, FF_SHARD - win_rows), (0, 0)))
    stacks = _gather_chips(local_b[:2], "gather_weights")
    win4, wuq4 = [lax.dynamic_update_slice(s, l[None], (me, 0, 0)) for s, l in zip(stacks, local_b)]
    win_t = win4[:, :win_rows].reshape(D_IN, D_MODEL)
    wuq_full = wuq4.reshape(Q_RANK, MLA_HEADS, MLA_QK)
    win_arr, wq_arr, wk_arr, wv_arr = _arrange_weights(win_t, wuq_full, mla_w_ukv[0].astype(BF16))
    small = {n: W[n][0] if n == "mla_w_ukv" else W[n].reshape(-1, W[n].shape[-1]) for n in SMALL_NAMES}

    loss_local, dx, grads, (ffn_rs, ffn_ris) = _local_step(x[0], positions.reshape(T, 1), loss_target[0], small, win_arr,
                                                           wq_arr, wk_arr, wv_arr, local_b[2:], place)

    gs = [grads["w_in"], grads["mla_w_uq"].reshape((N_CHIPS,) + UQ_COMM_SHAPE), grads["w_out"].reshape((N_CHIPS,) + shard2d["w_out"])]
    ffn_gs = [grads["w_gate"], grads["w_up"], grads["w_down"]]
    sm = _pack_small({**grads, "loss": loss_local})
    *rs, ssib = _pair_swap(gs, sm)
    *ps, pair = _pair_sum(place, gs, rs, small=(sm, ssib))
    *ris, sm4 = _chip_swap(ps, pair)
    gfs = _chip_sum(place, gs + ffn_gs, rs + ffn_rs, ris + ffn_ris)
    *gfin, smf = _pair_fill(gfs, sm4)

    G, DW, NM, NV = {}, {}, {}, {}
    g2d = {n: gfin[k].reshape((-1,) + shard2d[n][1:]) for k, n in enumerate(BIG_NAMES)}
    for names_, steps in ((("w_in", "mla_w_uq"), 3), (("w_out", "w_gate", "w_up", "w_down"), 8)):
        res = _adamw([(to2d(n, W[n]), g2d[n], to2d(n, M[n]), to2d(n, V[n])) for n in names_], steps, "adamw_" + names_[0])
        for n, (d, mo, vo) in zip(names_, res):
            G[n] = from2d(n, g2d[n][:shard2d[n][0]])
            DW[n], NM[n], NV[n] = (from2d(n, t) for t in (d, mo, vo))
    view2d = lambda n, a: a.reshape(next((r, c) for name, r, c in SMALL_VIEWS if name == n))
    *res, loss_row = _adamw_small(smf, [tuple(view2d(n, t[n]) for t in (W, M, V)) for n in SMALL_NAMES])
    for n, outs in zip(SMALL_NAMES, res):
        G[n], DW[n], NM[n], NV[n] = (t.reshape(W[n].shape) for t in outs)
    loss = loss_row[0, 0]
    return (loss, dx[None], *[G[n] for n in WEIGHT_NAMES], *[DW[n] for n in WEIGHT_NAMES],
            *[NM[n] for n in WEIGHT_NAMES], *[NV[n] for n in WEIGHT_NAMES])
```

```python
import jax
import jax.numpy as jnp
from jax import lax
from jax.experimental import pallas as pl
from jax.experimental.pallas import tpu as pltpu

F32 = jnp.float32
BF16 = jnp.bfloat16
MXU_DTYPE = BF16

D_MODEL = 1024
MLA_HEADS = 8
MLA_NOPE = 64
MLA_ROPE = 32
MLA_V = 64
MLA_QK = MLA_NOPE + MLA_ROPE
Q_RANK = 384
KV_RANK = 128
MLA_WIDTH = MLA_HEADS * MLA_V
HEAD_PAD = 128
HGRN_HEADS = 4
HGRN_DIM = 128
HGRN_WIDTH = HGRN_HEADS * HGRN_DIM
CHUNK = 64
SUB = 16
HGRN_CPI = 4
D_IN = Q_RANK + KV_RANK + MLA_ROPE + 4 * HGRN_WIDTH
D_IN_ARR = Q_RANK + KV_RANK + HEAD_PAD + 4 * HGRN_WIDTH
D_FF = 2816
N_CHIPS = 4
FF_SHARD = D_FF // N_CHIPS
EPS = 1e-6
ROPE_THETA = 10000.0
ATTN_SCALE = MLA_QK ** -0.5
ATTN_SCALE_LOG2 = ATTN_SCALE * 1.4426950408889634
NEG_BIG = -1e30

ADAM_LR = 0.001
ADAM_B1 = 0.9
ADAM_B2 = 0.999
ADAM_EPS = 1e-08
ADAM_WD = 0.01
ADAM_STEP = 10

VMEM_LIMIT = 56 * 1024 * 1024

SMALL_VIEWS = (("attn_pre_norm", 1, 1024), ("mla_q_norm", 1, 384), ("mla_kv_norm", 1, 128), ("mla_w_ukv", 128, 1024),
               ("mla_out_norm", 1, 512), ("hgrn_lb_logits", 2, 512), ("hgrn_out_norm", 1, 512),
               ("attn_post_norm", 1, 1024), ("ffn_pre_norm", 1, 1024), ("ffn_post_norm", 1, 1024), ("loss", 1, 1))
ROW_TILE = 8
SMALL_ROWS = sum(-(-rows // ROW_TILE) * ROW_TILE for _, rows, _ in SMALL_VIEWS)

MESH = pl.DeviceIdType.MESH
ANY = pl.BlockSpec(memory_space=pl.ANY)


def _dot(a, b, dims, exact):
    if exact:
        return lax.dot_general(a.astype(F32), b.astype(F32), (dims, ((), ())), precision=lax.Precision.HIGH,
                               preferred_element_type=F32)
    return lax.dot_general(a.astype(MXU_DTYPE), b.astype(MXU_DTYPE), (dims, ((), ())), preferred_element_type=F32)


def _mm(a, b, exact=False):
    return _dot(a, b, ((1,), (0,)), exact)


def _mm_nt(a, b, exact=False):
    return _dot(a, b, ((1,), (1,)), exact)


def _mm_tn(a, b, exact=False):
    return _dot(a, b, ((0,), (0,)), exact)


def _rms_fwd(x, w):
    r = lax.rsqrt(jnp.mean(x * x, axis=-1, keepdims=True) + EPS)
    xn = x * r
    return xn * w, xn, r


def _rms_bwd(dy, xn, r, w):
    dxn = dy * w
    dx = r * (dxn - xn * jnp.mean(dxn * xn, axis=-1, keepdims=True))
    dw = jnp.sum(dy * xn, axis=0, keepdims=True)
    return dx, dw


def _group_sums(v, gs):
    t, n = v.shape
    lane = lax.broadcasted_iota(jnp.int32, (t, 128), 1)
    out = []
    for p in range(n // 128):
        vb = v[:, 128 * p:128 * (p + 1)]
        if gs == 128:
            out.append(jnp.sum(vb, axis=-1, keepdims=True))
        else:
            out.append(jnp.sum(jnp.where(lane < 64, vb, 0.0), axis=-1, keepdims=True))
            out.append(jnp.sum(jnp.where(lane >= 64, vb, 0.0), axis=-1, keepdims=True))
    return out


def _group_bcast(sums, gs, t):
    lane = lax.broadcasted_iota(jnp.int32, (t, 128), 1)
    if gs == 128:
        return jnp.concatenate([jnp.broadcast_to(s, (t, 128)) for s in sums], axis=-1)
    return jnp.concatenate([jnp.where(lane < 64, sums[2 * p], sums[2 * p + 1]) for p in range(len(sums) // 2)],
                           axis=-1)


def _grms_fwd(x, w, gs):
    t = x.shape[0]
    r = lax.rsqrt(_group_bcast(_group_sums(x * x, gs), gs, t) * (1.0 / gs) + EPS)
    xn = x * r
    return xn * w, xn, r


def _grms_bwd(dy, xn, r, w, gs):
    t = dy.shape[0]
    dxn = dy * w
    dx = r * (dxn - xn * (_group_bcast(_group_sums(dxn * xn, gs), gs, t) * (1.0 / gs)))
    dw = jnp.sum(dy * xn, axis=0, keepdims=True)
    return dx, dw


def _rope_tables(c_tab, s_tab):
    lane = lax.broadcasted_iota(jnp.int32, c_tab.shape, 1)
    first = (lane >= MLA_NOPE) & (lane < MLA_NOPE + MLA_ROPE // 2)
    second = (lane >= MLA_NOPE + MLA_ROPE // 2) & (lane < MLA_QK)
    return c_tab, jnp.where(first, -s_tab, 0.0), jnp.where(second, s_tab, 0.0)


def _rope(v, c, sa, sb):
    return v * c + pltpu.roll(v, HEAD_PAD - MLA_ROPE // 2, 1) * sa + pltpu.roll(v, MLA_ROPE // 2, 1) * sb


def _rope_bwd(d, c, sa, sb):
    return d * c - pltpu.roll(d, HEAD_PAD - MLA_ROPE // 2, 1) * sa - pltpu.roll(d, MLA_ROPE // 2, 1) * sb


def _params(sem, vmem=VMEM_LIMIT):
    return pltpu.CompilerParams(dimension_semantics=sem, vmem_limit_bytes=vmem)


def _in_fwd(x, pos, invf, w_pre, win, qnw, wq, kvnw, wk, wv, tt=512):
    T = x.shape[0]

    def body(x_ref, pos_ref, invf_ref, wpre_ref, win_ref, qnw_ref, wq_ref, kvnw_ref, wk_ref, wv_ref,
             cq_ref, ckv_ref, xph_ref, q_ref, k_ref, v_ref, kt_ref, vt_ref, rc_ref, rs_ref):
        u, _, _ = _rms_fwd(x_ref[...], wpre_ref[...])
        lo = Q_RANK + KV_RANK + HEAD_PAD
        xp = _mm_nt(u, win_ref[:lo, :])
        xph_ref[...] = _mm_nt(u, win_ref[lo:, :])
        cq = xp[:, :Q_RANK]
        ckv = xp[:, Q_RANK:Q_RANK + KV_RANK]
        kr = xp[:, Q_RANK + KV_RANK:]
        cq_ref[...] = cq
        ckv_ref[...] = ckv
        ang = pos_ref[...].astype(F32) * invf_ref[...]
        c_tab = jnp.cos(ang)
        s_tab = jnp.sin(ang)
        rc_ref[...] = c_tab
        rs_ref[...] = s_tab
        c, sa, sb = _rope_tables(c_tab, s_tab)
        qn, _, _ = _rms_fwd(cq, qnw_ref[...])
        q = _mm(qn, wq_ref[...])
        kvn, _, _ = _rms_fwd(ckv, kvnw_ref[...])
        kn = _mm(kvn, wk_ref[...])
        v = _mm(kvn, wv_ref[...])
        v_ref[...] = v.astype(v_ref.dtype)
        vt_ref[...] = v.T.astype(vt_ref.dtype)
        krr = _rope(kr, c, sa, sb)
        for h in range(MLA_HEADS):
            sl = slice(HEAD_PAD * h, HEAD_PAD * (h + 1))
            q_ref[:, sl] = _rope(q[:, sl], c, sa, sb).astype(q_ref.dtype)
            kh = kn[:, sl] + krr
            k_ref[:, sl] = kh.astype(k_ref.dtype)
            kt_ref[sl, :] = kh.T.astype(kt_ref.dtype)

    row = lambda w: pl.BlockSpec((tt, w), lambda i: (i, 0))
    full = lambda a: pl.BlockSpec(a.shape, lambda i: (0,) * a.ndim)
    qk_w = MLA_HEADS * HEAD_PAD
    return pl.pallas_call(
        body, name="in_fwd", grid=(T // tt,),
        in_specs=[row(D_MODEL), row(1), full(invf), full(w_pre), full(win), full(qnw), full(wq), full(kvnw),
                  full(wk), full(wv)],
        out_specs=[row(Q_RANK), row(KV_RANK), row(4 * HGRN_WIDTH), row(qk_w), row(qk_w), row(MLA_WIDTH),
                   pl.BlockSpec((qk_w, tt), lambda i: (0, i)), pl.BlockSpec((MLA_WIDTH, tt), lambda i: (0, i)),
                   row(HEAD_PAD), row(HEAD_PAD)],
        out_shape=[jax.ShapeDtypeStruct((T, Q_RANK), F32), jax.ShapeDtypeStruct((T, KV_RANK), F32),
                   jax.ShapeDtypeStruct((T, 4 * HGRN_WIDTH), F32), jax.ShapeDtypeStruct((T, qk_w), MXU_DTYPE),
                   jax.ShapeDtypeStruct((T, qk_w), MXU_DTYPE), jax.ShapeDtypeStruct((T, MLA_WIDTH), MXU_DTYPE),
                   jax.ShapeDtypeStruct((qk_w, T), MXU_DTYPE), jax.ShapeDtypeStruct((MLA_WIDTH, T), MXU_DTYPE),
                   jax.ShapeDtypeStruct((T, HEAD_PAD), F32), jax.ShapeDtypeStruct((T, HEAD_PAD), F32)],
        compiler_params=_params(("arbitrary",)),
    )(x, pos, invf, w_pre, win, qnw, wq, kvnw, wk, wv)


def _attn_fwd_t(qb, kb, vt, gather=(), tq=256, hps=8):
    T = qb.shape[0]
    nq = T // tq
    ng = len(gather)
    steps = (MLA_HEADS // hps) * nq
    pass_on = steps - 3

    def body(q_ref, k_ref, vt_ref, *rest):
        o_ref, lse_ref = rest[ng:ng + 2]
        acc_scr = rest[2 * ng + 2]
        qi = pl.program_id(1)
        step_no = pl.program_id(0) * nq + qi
        if ng:
            gat = _Gather(rest[:ng], rest[ng + 2:2 * ng + 2], *rest[2 * ng + 3:])

            @pl.when(step_no == 0)
            def _():
                for cp in gat.sends():
                    cp.start()

            @pl.when(step_no == pass_on)
            def _():
                for arrival in gat.arrivals():
                    arrival.wait_recv()
                for cp in gat.forwards():
                    cp.start()

        heads = [slice(HEAD_PAD * a, HEAD_PAD * (a + 1)) for a in range(hps)]
        acc_scr[...] = jnp.zeros_like(acc_scr)

        def step(j, carry, masked):
            start = pl.multiple_of(j * tq, tq)
            scores = [_mm_nt(k_ref[pl.ds(start, tq), heads[a]], q_ref[:, heads[a]]) for a in range(hps)]
            new = []
            for a in range(hps):
                m, l = carry[a]
                s = scores[a] * ATTN_SCALE_LOG2
                if masked:
                    kk = lax.broadcasted_iota(jnp.int32, (tq, tq), 0)
                    qq = lax.broadcasted_iota(jnp.int32, (tq, tq), 1)
                    s = jnp.where(kk <= qq, s, NEG_BIG)
                m_new = jnp.maximum(m, jnp.max(s, axis=0, keepdims=True))
                alpha = jnp.exp2(m - m_new)
                p = jnp.exp2(s - m_new)
                l = l * alpha + jnp.sum(p, axis=0, keepdims=True)
                vtj = vt_ref[2 * MLA_V * (a // 2):2 * MLA_V * (a // 2 + 1), pl.ds(start, tq)]
                acc_scr[a] = acc_scr[a] * alpha + _mm(vtj, p)
                new.append((m_new, l))
            return tuple(new)

        init = tuple((jnp.full((1, tq), NEG_BIG, F32), jnp.zeros((1, tq), F32)) for _ in range(hps))
        carry = lax.fori_loop(0, qi, lambda j, c: step(j, c, False), init)
        carry = step(qi, carry, True)
        row = lax.broadcasted_iota(jnp.int32, (2 * MLA_V, tq), 0)
        for pr in range(hps // 2):
            (m0, l0), (m1, l1) = carry[2 * pr], carry[2 * pr + 1]
            ot = jnp.where(row < MLA_V, acc_scr[2 * pr] / l0, acc_scr[2 * pr + 1] / l1)
            o_ref[:, 2 * MLA_V * pr:2 * MLA_V * (pr + 1)] = ot.T
            lse_ref[pr, 0:1, :] = m0 + jnp.log2(l0)
            lse_ref[pr, 1:2, :] = m1 + jnp.log2(l1)

        if ng:
            @pl.when(step_no == steps - 1)
            def _():
                for arrival in gat.forward_arrivals():
                    arrival.wait_recv()
                for cp in gat.sends() + gat.forwards():
                    cp.wait_send()

    return pl.pallas_call(
        body, name="attn_fwd", grid=(MLA_HEADS // hps, nq),
        in_specs=[pl.BlockSpec((tq, hps * HEAD_PAD), lambda g, i: (i, g)),
                  pl.BlockSpec((T, hps * HEAD_PAD), lambda g, i: (0, g)),
                  pl.BlockSpec((hps * MLA_V, T), lambda g, i: (g, 0))] + [ANY] * ng,
        out_specs=[pl.BlockSpec((tq, hps * MLA_V), lambda g, i: (i, g)),
                   pl.BlockSpec((hps // 2, 2, tq), lambda g, i: (g, 0, i))] + [ANY] * ng,
        out_shape=[jax.ShapeDtypeStruct((T, MLA_WIDTH), F32), jax.ShapeDtypeStruct((MLA_HEADS // 2, 2, T), F32)]
        + _Gather.out_shapes(gather),
        scratch_shapes=[pltpu.VMEM((hps, 2 * MLA_V, tq), F32)] + (_Gather.semaphores(gather) if ng else []),
        compiler_params=_params(("arbitrary", "arbitrary")),
    )(qb, kb, vt, *gather)


def _attn_bwd_t(qb, kb, kt, vb, dob, lse, dvec, send=(), tq=256, hps=4):
    T = qb.shape[0]
    nq = T // tq
    ns = len(send)
    steps = (MLA_HEADS // hps) * nq

    def body(q_ref, k_ref, kt_ref, v_ref, do_ref, lse_ref, d_ref, *rest):
        dqt_ref, dk_ref, dv_ref = rest[ns:ns + 3]
        va_scr, dv_scr = rest[2 * ns + 3:2 * ns + 5]
        j = pl.program_id(1)
        step_no = pl.program_id(0) * nq + j
        if ns:
            @pl.when(step_no == 0)
            def _():
                for cp in _chip_swap_copies(rest[:ns], rest[ns + 3:2 * ns + 3], *rest[2 * ns + 5:]):
                    cp.start()

        @pl.when(j == 0)
        def _():
            dqt_ref[...] = jnp.zeros_like(dqt_ref)

        lane = lax.broadcasted_iota(jnp.int32, (tq, 2 * MLA_V), 1)
        heads = [slice(HEAD_PAD * a, HEAD_PAD * (a + 1)) for a in range(hps)]
        pairs = [slice(2 * MLA_V * p, 2 * MLA_V * (p + 1)) for p in range(hps // 2)]
        for pr in range(hps // 2):
            vpair = v_ref[:, pairs[pr]]
            va_scr[2 * pr] = jnp.where(lane < MLA_V, vpair, jnp.zeros_like(vpair))
            va_scr[2 * pr + 1] = jnp.where(lane >= MLA_V, vpair, jnp.zeros_like(vpair))
        dk_ref[...] = jnp.zeros_like(dk_ref)
        dv_scr[...] = jnp.zeros_like(dv_scr)

        def step(i, masked):
            start = pl.multiple_of(i * tq, tq)
            rows = pl.ds(start, tq)
            scores = [_mm_nt(k_ref[:, heads[a]], q_ref[rows, heads[a]]) for a in range(hps)]
            dps = [_mm_nt(va_scr[a], do_ref[rows, pairs[a // 2]]) for a in range(hps)]
            for a in range(hps):
                pr, r = a // 2, a % 2
                p = jnp.exp2(scores[a] * ATTN_SCALE_LOG2 - lse_ref[pr, r:r + 1, rows])
                if masked:
                    kk = lax.broadcasted_iota(jnp.int32, (tq, tq), 0)
                    qq = lax.broadcasted_iota(jnp.int32, (tq, tq), 1)
                    p = jnp.where(kk <= qq, p, 0.0)
                ds = p * (dps[a] - d_ref[pr, r:r + 1, rows]) * ATTN_SCALE
                dv_scr[a] += _mm(p, do_ref[rows, pairs[pr]])
                dk_ref[:, heads[a]] += _mm(ds, q_ref[rows, heads[a]])
                dqt_ref[heads[a], rows] += _mm(kt_ref[heads[a], :], ds)

        def loop_body(i, _):
            step(i, False)
            return 0

        step(j, True)
        lax.fori_loop(j + 1, nq, loop_body, 0)
        for pr in range(hps // 2):
            dv_ref[:, pairs[pr]] = jnp.where(lane < MLA_V, dv_scr[2 * pr], dv_scr[2 * pr + 1])

        if ns:
            @pl.when(step_no == steps - 1)
            def _():
                for cp in _chip_swap_copies(rest[:ns], rest[ns + 3:2 * ns + 3], *rest[2 * ns + 5:]):
                    cp.wait()

    stat = pl.BlockSpec((hps // 2, 2, T), lambda g, j: (g, 0, 0))
    return pl.pallas_call(
        body, name="attn_bwd", grid=(MLA_HEADS // hps, nq),
        in_specs=[pl.BlockSpec((T, hps * HEAD_PAD), lambda g, j: (0, g)),
                  pl.BlockSpec((tq, hps * HEAD_PAD), lambda g, j: (j, g)),
                  pl.BlockSpec((hps * HEAD_PAD, tq), lambda g, j: (g, j)),
                  pl.BlockSpec((tq, hps * MLA_V), lambda g, j: (j, g)),
                  pl.BlockSpec((T, hps * MLA_V), lambda g, j: (0, g)), stat, stat] + [ANY] * ns,
        out_specs=[pl.BlockSpec((hps * HEAD_PAD, T), lambda g, j: (g, 0)),
                   pl.BlockSpec((tq, hps * HEAD_PAD), lambda g, j: (j, g)),
                   pl.BlockSpec((tq, hps * MLA_V), lambda g, j: (j, g))] + [ANY] * ns,
        out_shape=[jax.ShapeDtypeStruct((MLA_HEADS * HEAD_PAD, T), F32),
                   jax.ShapeDtypeStruct((T, MLA_HEADS * HEAD_PAD), F32),
                   jax.ShapeDtypeStruct((T, MLA_WIDTH), F32)] + _chip_swap_shapes(send),
        scratch_shapes=[pltpu.VMEM((hps, tq, 2 * MLA_V), vb.dtype), pltpu.VMEM((hps, tq, 2 * MLA_V), F32)]
        + ([pltpu.SemaphoreType.DMA((3 * ns,)), pltpu.SemaphoreType.DMA((3 * ns,))] if ns else []),
        compiler_params=_params(("arbitrary", "arbitrary")),
    )(qb, kb, kt, vb, dob, lse, dvec, *send)


def _cumsum_rows(x):
    n = x.shape[0]
    row = lax.broadcasted_iota(jnp.int32, x.shape, 0)
    s = 1
    while s < n:
        x = x + jnp.where(row >= s, pltpu.roll(x, s, 0), 0.0)
        s *= 2
    return x


def _rev_cumsum_rows(x):
    n = x.shape[0]
    row = lax.broadcasted_iota(jnp.int32, x.shape, 0)
    s = 1
    while s < n:
        x = x + jnp.where(row < n - s, pltpu.roll(x, n - s, 0), 0.0)
        s *= 2
    return x


def _lb_from_logits(l):
    l0, l1 = l[0:1, :], l[1:2, :]
    m = jnp.maximum(l0, l1)
    e0, e1 = jnp.exp(l0 - m), jnp.exp(l1 - m)
    return e0 / (e0 + e1)


def _hgrn_gates(hq, hf, lb):
    sig_f = jax.nn.sigmoid(hf)
    f = lb + (1.0 - lb) * sig_f
    sig_q = jax.nn.sigmoid(hq)
    return sig_f, f, jnp.log(f), 1.0 - f, sig_q, hq * sig_q


def _hgrn_intra(q, kk, b, exact=False):
    row = lax.broadcasted_iota(jnp.int32, b.shape, 0)
    qs, ks, eqs, eks, a_rows = [], [], [], [], []
    for i in range(CHUNK // SUB):
        ref = b[SUB * i + SUB // 2:SUB * i + SUB // 2 + 1, :]
        eq = jnp.exp(b[SUB * i:SUB * (i + 1), :] - ref)
        ek = jnp.exp(jnp.where(row < SUB * (i + 1), ref - b, NEG_BIG))
        qi = q[SUB * i:SUB * (i + 1), :] * eq
        ki = kk * ek
        a_rows.append(_mm_nt(qi, ki, exact))
        qs.append(qi), ks.append(ki), eqs.append(eq), eks.append(ek)
    tt = lax.broadcasted_iota(jnp.int32, (CHUNK, CHUNK), 0)
    ss = lax.broadcasted_iota(jnp.int32, (CHUNK, CHUNK), 1)
    causal = ss <= tt
    a = jnp.where(causal, jnp.concatenate(a_rows, axis=0), 0.0)
    return a, causal, qs, ks, eqs, eks


def _hgrn_fwd(xph, lbl, tg=512):
    T = xph.shape[0]
    ng, ncg = T // tg, tg // CHUNK
    cols = [slice(HGRN_DIM * h, HGRN_DIM * (h + 1)) for h in range(HGRN_HEADS)]

    def body(lbl_ref, hq_ref, hf_ref, hi_ref, o_ref, st_ref, s_scr):
        @pl.when(pl.program_id(0) == 0)
        def _():
            s_scr[...] = jnp.zeros_like(s_scr)

        lb = _lb_from_logits(lbl_ref[...])

        def chunks(it, _):
            pre = []
            for k in range(HGRN_CPI):
                c = it * HGRN_CPI + k
                rows = pl.ds(pl.multiple_of(c * CHUNK, CHUNK), CHUNK)
                for cs in cols:
                    _, _, lf, kk, _, q = _hgrn_gates(hq_ref[rows, cs], hf_ref[rows, cs], lb[:, cs])
                    v = hi_ref[rows, cs]
                    b = _cumsum_rows(lf)
                    a = _hgrn_intra(q, kk, b)[0]
                    b_last = b[CHUNK - 1:CHUNK, :]
                    pre.append((c, rows, q * jnp.exp(b), a, v, jnp.exp(b_last), _mm_tn(v, kk * jnp.exp(b_last - b))))
            for i, (c, rows, qe, a, v, ebl, upd) in enumerate(pre):
                h = i % HGRN_HEADS
                st = s_scr[h]
                st_ref[h, c] = st
                o_ref[rows, cols[h]] = _mm_nt(qe, st) + _mm(a, v)
                s_scr[h] = st * ebl + upd
            return 0

        lax.fori_loop(0, ncg // HGRN_CPI, chunks, 0)

    col = lambda k: pl.BlockSpec((tg, HGRN_WIDTH), lambda g: (g, k))
    return pl.pallas_call(
        body, name="hgrn_fwd", grid=(ng,),
        in_specs=[pl.BlockSpec((2, HGRN_WIDTH), lambda g: (0, 0)), col(0), col(1), col(2)],
        out_specs=[col(0), pl.BlockSpec((HGRN_HEADS, ncg, HGRN_DIM, HGRN_DIM), lambda g: (0, g, 0, 0))],
        out_shape=[jax.ShapeDtypeStruct((T, HGRN_WIDTH), F32),
                   jax.ShapeDtypeStruct((HGRN_HEADS, T // CHUNK, HGRN_DIM, HGRN_DIM), F32)],
        scratch_shapes=[pltpu.VMEM((HGRN_HEADS, HGRN_DIM, HGRN_DIM), F32)],
        compiler_params=_params(("arbitrary",)),
    )(lbl, xph, xph, xph)


def _hgrn_bwd(xph, lbl, states, d_o, tg=512):
    T = xph.shape[0]
    ng, ncg = T // tg, tg // CHUNK
    cols = [slice(HGRN_DIM * h, HGRN_DIM * (h + 1)) for h in range(HGRN_HEADS)]
    nsub = CHUNK // SUB

    def body(lbl_ref, hq_ref, hf_ref, hi_ref, st_ref, do_ref, dhq_ref, dhf_ref, dhi_ref, dlg_ref, ds_scr, dlb_scr):
        g = pl.program_id(0)

        @pl.when(g == 0)
        def _():
            ds_scr[...] = jnp.zeros_like(ds_scr)
            dlb_scr[...] = jnp.zeros_like(dlb_scr)

        lb = _lb_from_logits(lbl_ref[...])

        def chunks(it, _):
            pre = []
            for k, h in ((k, h) for k in range(HGRN_CPI) for h in range(HGRN_HEADS)):
                cs = cols[h]
                c = ncg - 1 - (it * HGRN_CPI + k)
                rows = pl.ds(pl.multiple_of(c * CHUNK, CHUNK), CHUNK)
                hq = hq_ref[rows, cs]
                sig_f, f, lf, kk, sig_q, q = _hgrn_gates(hq, hf_ref[rows, cs], lb[:, cs])
                v = hi_ref[rows, cs]
                do = do_ref[rows, cs]
                b = _cumsum_rows(lf)
                eb = jnp.exp(b)
                a, causal, qs, ks, eqs, eks = _hgrn_intra(q, kk, b)
                b_last = b[CHUNK - 1:CHUNK, :]
                st = st_ref[h, c]
                pre.append(dict(h=h, cs=cs, rows=rows, hq=hq, sig_f=sig_f, f=f, kk=kk, sig_q=sig_q, q=q, v=v, eb=eb, qs=qs,
                                ks=ks, eqs=eqs,
                                eks=eks, ebl=jnp.exp(b_last), el=jnp.exp(b_last - b), st=st,
                                da=jnp.where(causal, _mm_nt(do, v, True), 0.0), dq=_mm(do, st, True) * eb,
                                dv=_mm_tn(a, do), dsu=_mm_tn(do, q * eb, True)))
            for w in pre:
                dq_rows = []
                dk = jnp.zeros_like(w["q"])
                for i in range(nsub):
                    dai = w["da"][SUB * i:SUB * (i + 1), :]
                    dq_rows.append(_mm(dai, w["ks"][i], True) * w["eqs"][i])
                    dk = dk + _mm_tn(dai, w["qs"][i], True) * w["eks"][i]
                w["dq"] = w["dq"] + jnp.concatenate(dq_rows, axis=0)
                w["dk"] = dk
            for w in pre:
                h, cs, rows = w["h"], w["cs"], w["rows"]
                kk, el, ebl, dst = w["kk"], w["el"], w["ebl"], ds_scr[h]
                dk_state = _mm(w["v"], dst, True) * el
                dk = w["dk"] + dk_state
                e_last = (ebl * jnp.sum(w["st"] * dst, axis=0, keepdims=True)
                          + jnp.sum(kk * dk_state, axis=0, keepdims=True))
                dlf = _rev_cumsum_rows(w["q"] * w["dq"] - kk * dk) + e_last
                ds_scr[h] = dst * ebl + w["dsu"]
                df = dlf / w["f"] - dk
                sig_f, sig_q = w["sig_f"], w["sig_q"]
                dhf_ref[rows, cs] = df * (1.0 - lb[:, cs]) * sig_f * (1.0 - sig_f)
                dlb_scr[:, cs] += jnp.sum(df * (1.0 - sig_f), axis=0, keepdims=True)
                dhq_ref[rows, cs] = w["dq"] * sig_q * (1.0 + w["hq"] * (1.0 - sig_q))
                dhi_ref[rows, cs] = w["dv"] + _mm_nt(kk * el, dst)
            return 0

        lax.fori_loop(0, ncg // HGRN_CPI, chunks, 0)

        @pl.when(g == ng - 1)
        def _():
            dl0 = dlb_scr[...] * lb * (1.0 - lb)
            dlg_ref[...] = jnp.concatenate([dl0, -dl0], axis=0)

    col = lambda k: pl.BlockSpec((tg, HGRN_WIDTH), lambda g: (ng - 1 - g, k))
    logits = pl.BlockSpec((2, HGRN_WIDTH), lambda g: (0, 0))
    big = jax.ShapeDtypeStruct((T, HGRN_WIDTH), F32)
    return pl.pallas_call(
        body, name="hgrn_bwd", grid=(ng,),
        in_specs=[logits, col(0), col(1), col(2),
                  pl.BlockSpec((HGRN_HEADS, ncg, HGRN_DIM, HGRN_DIM), lambda g: (0, ng - 1 - g, 0, 0)), col(0)],
        out_specs=[col(0), col(0), col(0), logits],
        out_shape=[big, big, big, jax.ShapeDtypeStruct((2, HGRN_WIDTH), F32)],
        scratch_shapes=[pltpu.VMEM((HGRN_HEADS, HGRN_DIM, HGRN_DIM), F32), pltpu.VMEM((1, HGRN_WIDTH), F32)],
        compiler_params=_params(("arbitrary",)),
    )(lbl, xph, xph, xph, states, d_o)


def _proj_fwd(x, o_raw, oh_raw, xph, wout, w_mla, w_hg, w_post, w_fpre, tt=512):
    T = x.shape[0]

    def body(x_ref, o_ref, oh_ref, hg_ref, wout_ref, wmla_ref, whg_ref, wpost_ref, wfpre_ref,
             h1_ref, y1_ref, z_ref, mix_ref):
        om, _, _ = _grms_fwd(o_ref[...], wmla_ref[...], MLA_V)
        hg = hg_ref[...]
        ohn, _, _ = _grms_fwd(oh_ref[...], whg_ref[...], HGRN_DIM)
        mix = jnp.concatenate([om, ohn * (hg * jax.nn.sigmoid(hg))], axis=-1)
        mix_ref[...] = mix.astype(mix_ref.dtype)
        y1 = _mm(mix, wout_ref[...])
        y1_ref[...] = y1
        h1 = x_ref[...] + _rms_fwd(y1, wpost_ref[...])[0]
        h1_ref[...] = h1
        z_ref[...] = _rms_fwd(h1, wfpre_ref[...])[0].astype(z_ref.dtype)

    row = lambda w: pl.BlockSpec((tt, w), lambda i: (i, 0))
    full = lambda a: pl.BlockSpec(a.shape, lambda i: (0,) * a.ndim)
    sds = jax.ShapeDtypeStruct
    return pl.pallas_call(
        body, name="proj_fwd", grid=(T // tt,),
        in_specs=[row(D_MODEL), row(MLA_WIDTH), row(HGRN_WIDTH), pl.BlockSpec((tt, HGRN_WIDTH), lambda i: (i, 3)),
                  full(wout), full(w_mla), full(w_hg), full(w_post), full(w_fpre)],
        out_specs=[row(D_MODEL)] * 4,
        out_shape=[sds((T, D_MODEL), F32), sds((T, D_MODEL), F32), sds((T, D_MODEL), MXU_DTYPE),
                   sds((T, D_MODEL), MXU_DTYPE)],
        compiler_params=_params(("arbitrary",)),
    )(x, o_raw, oh_raw, xph, wout, w_mla, w_hg, w_post, w_fpre)


def _ffn_fwd(zb, h1, tgt, w_fpost, wg, wu, wd, tt=256):
    T = zb.shape[0]
    nj = N_CHIPS

    def body(z_ref, h1_ref, tgt_ref, wfpost_ref, wg_ref, wu_ref, wd_ref, g_ref, up_ref, dy2_ref, dh2_ref, loss_ref, dwf_ref):
        @pl.when(pl.program_id(0) == 0)
        def _():
            loss_ref[...] = jnp.zeros_like(loss_ref)
            dwf_ref[...] = jnp.zeros_like(dwf_ref)

        z = z_ref[...]
        gs = [_mm_nt(z, wg_ref[j]) for j in range(nj)]
        ups = [_mm_nt(z, wu_ref[j]) for j in range(nj)]
        y2 = jnp.zeros((tt, D_MODEL), F32)
        for j in range(nj):
            g_ref[j] = gs[j]
            up_ref[j] = ups[j]
            y2 = y2 + _mm(gs[j] * jax.nn.sigmoid(gs[j]) * ups[j], wd_ref[j])
        w = wfpost_ref[...]
        y2s, y2n, r2 = _rms_fwd(y2, w)
        e = h1_ref[...] + y2s - tgt_ref[...]
        loss_ref[...] += jnp.sum(e * e, axis=0, keepdims=True)
        dh2 = e * (1.0 / D_MODEL)
        dh2_ref[...] = dh2
        dy2, dwf = _rms_bwd(dh2, y2n, r2, w)
        dy2_ref[...] = dy2.astype(dy2_ref.dtype)
        dwf_ref[...] += dwf

    row = pl.BlockSpec((tt, D_MODEL), lambda i: (i, 0))
    vec = pl.BlockSpec((1, D_MODEL), lambda i: (0, 0))
    resident = pl.BlockSpec((nj, FF_SHARD, D_MODEL), lambda i: (0, 0, 0), pipeline_mode=pl.Buffered(1))
    act = pl.BlockSpec((nj, tt, FF_SHARD), lambda i: (0, i, 0))
    sds = jax.ShapeDtypeStruct
    return pl.pallas_call(
        body, name="ffn_fwd", grid=(T // tt,),
        in_specs=[row, row, row, vec, resident, resident, resident],
        out_specs=[act, act, row, row, vec, vec],
        out_shape=[sds((nj, T, FF_SHARD), F32), sds((nj, T, FF_SHARD), F32), sds((T, D_MODEL), MXU_DTYPE),
                   sds((T, D_MODEL), F32), sds((1, D_MODEL), F32), sds((1, D_MODEL), F32)],
        compiler_params=_params(("arbitrary",)),
    )(zb, h1, tgt, w_fpost, wg, wu, wd)


def _ffn_bwd(zb, g, up, dy2b, wg, wu, wd, tt=512):
    T = zb.shape[0]
    nj = N_CHIPS

    def body(z_ref, g_ref, up_ref, dy2_ref, wg_ref, wu_ref, wd_ref, dwg_ref, dwu_ref, dwd_ref, dz_ref):
        @pl.when(pl.program_id(1) == 0)
        def _():
            dwg_ref[...] = jnp.zeros_like(dwg_ref)
            dwu_ref[...] = jnp.zeros_like(dwu_ref)
            dwd_ref[...] = jnp.zeros_like(dwd_ref)

        z, g_, up_, dy2 = z_ref[...], g_ref[0], up_ref[0], dy2_ref[...]
        sg = jax.nn.sigmoid(g_)
        act = g_ * sg
        dff = _mm_nt(dy2, wd_ref[0])
        dwd_ref[0] += _mm_tn(act * up_, dy2)
        dg = dff * up_ * sg * (1.0 + g_ * (1.0 - sg))
        dup = dff * act
        dwg_ref[0] += _mm_tn(dg, z)
        dwu_ref[0] += _mm_tn(dup, z)
        dz_ref[0] = _mm(dg, wg_ref[0]) + _mm(dup, wu_ref[0])

    row = pl.BlockSpec((tt, D_MODEL), lambda j, i: (i, 0))
    act = pl.BlockSpec((1, tt, FF_SHARD), lambda j, i: (j, i, 0))
    w_sh = pl.BlockSpec((1, FF_SHARD, D_MODEL), lambda j, i: (j, 0, 0))
    w_grad = jax.ShapeDtypeStruct((nj, FF_SHARD, D_MODEL), F32)
    return pl.pallas_call(
        body, name="ffn_bwd", grid=(nj, T // tt),
        in_specs=[row, act, act, row, w_sh, w_sh, w_sh],
        out_specs=[w_sh, w_sh, w_sh, pl.BlockSpec((1, tt, D_MODEL), lambda j, i: (j, i, 0))],
        out_shape=[w_grad, w_grad, w_grad, jax.ShapeDtypeStruct((nj, T, D_MODEL), F32)],
        compiler_params=_params(("arbitrary", "arbitrary")),
    )(zb, g, up, dy2b, wg, wu, wd)


def _mid_bwd(dzp, dh2, h1, y1, mixb, o_raw, oh_raw, xph, wout, w_fpre, w_post, w_mla, w_hg, swap=(), tt=256):
    T = dh2.shape[0]
    nsw = len(swap)
    n_in, n_out = 13, 10

    def body(*refs):
        (dzp_ref, dh2_ref, h1_ref, y1_ref, mix_ref, o_ref, oh_ref, hg_ref, wout_ref, wfpre_ref, wpost_ref,
         wmla_ref, whg_ref) = refs[:n_in]
        (dh1_ref, dwout_ref, do_ref, doh_ref, dhg_ref, dvec_ref, dwfpre_ref, dwpost_ref, dwmla_ref,
         dwhg_ref) = refs[n_in + nsw:n_in + nsw + n_out]
        swap_copies = lambda: _pair_swap_copies(refs[n_in:n_in + nsw], refs[n_in + nsw + n_out:n_in + 2 * nsw + n_out],
                                                *refs[n_in + 2 * nsw + n_out:])

        @pl.when(pl.program_id(0) == 0)
        def _():
            for r in (dwout_ref, dwfpre_ref, dwpost_ref, dwmla_ref, dwhg_ref):
                r[...] = jnp.zeros_like(r)
            for cp in (swap_copies() if nsw else ()):
                cp.start()

        dz = dzp_ref[0] + dzp_ref[1] + dzp_ref[2] + dzp_ref[3]
        wfpre = wfpre_ref[...]
        _, h1n, r = _rms_fwd(h1_ref[...], wfpre)
        dh1_z, dwfpre = _rms_bwd(dz, h1n, r, wfpre)
        dwfpre_ref[...] += dwfpre
        dh1 = dh2_ref[...] + dh1_z
        dh1_ref[...] = dh1
        wpost = wpost_ref[...]
        _, y1n, r1 = _rms_fwd(y1_ref[...], wpost)
        dy1, dwpost = _rms_bwd(dh1, y1n, r1, wpost)
        dwpost_ref[...] += dwpost
        dmix = _mm_nt(dy1, wout_ref[...])
        dwout_ref[...] += _mm_tn(mix_ref[...], dy1)
        wmla = wmla_ref[...]
        o = o_ref[...]
        _, on, ro = _grms_fwd(o, wmla, MLA_V)
        d_o, dwmla = _grms_bwd(dmix[:, :MLA_WIDTH], on, ro, wmla, MLA_V)
        dwmla_ref[...] += dwmla
        do_ref[...] = d_o.astype(do_ref.dtype)
        hh = lax.broadcasted_iota(jnp.int32, (MLA_HEADS, MLA_WIDTH), 0)
        ll = lax.broadcasted_iota(jnp.int32, (MLA_HEADS, MLA_WIDTH), 1)
        sel = jnp.where((ll >= hh * MLA_V) & (ll < (hh + 1) * MLA_V), 1.0, 0.0)
        dvec_ref[...] = _mm_nt(sel, d_o * o, True)
        whg = whg_ref[...]
        hg = hg_ref[...]
        sg = jax.nn.sigmoid(hg)
        _, ohn, rh = _grms_fwd(oh_ref[...], whg, HGRN_DIM)
        dmh = dmix[:, MLA_WIDTH:]
        dhg_ref[...] = dmh * ohn * whg * sg * (1.0 + hg * (1.0 - sg))
        d_oh, dwhg = _grms_bwd(dmh * (hg * sg), ohn, rh, whg, HGRN_DIM)
        dwhg_ref[...] += dwhg
        doh_ref[...] = d_oh

        if nsw:
            @pl.when(pl.program_id(0) == T // tt - 1)
            def _():
                for cp in swap_copies():
                    cp.wait()

    row = lambda w: pl.BlockSpec((tt, w), lambda i: (i, 0))
    full = lambda a: pl.BlockSpec(a.shape, lambda i: (0,) * a.ndim)
    vec = lambda w: pl.BlockSpec((1, w), lambda i: (0, 0))
    sds = jax.ShapeDtypeStruct
    return pl.pallas_call(
        body, name="mid_bwd", grid=(T // tt,),
        in_specs=[pl.BlockSpec((N_CHIPS, tt, D_MODEL), lambda i: (0, i, 0)), row(D_MODEL), row(D_MODEL), row(D_MODEL),
                  row(D_MODEL), row(MLA_WIDTH), row(HGRN_WIDTH), pl.BlockSpec((tt, HGRN_WIDTH), lambda i: (i, 3)),
                  full(wout), vec(D_MODEL), vec(D_MODEL), vec(MLA_WIDTH), vec(HGRN_WIDTH)] + [ANY] * nsw,
        out_specs=[row(D_MODEL), full(wout), row(MLA_WIDTH), row(HGRN_WIDTH), row(HGRN_WIDTH),
                   pl.BlockSpec((MLA_HEADS, tt), lambda i: (0, i)),
                   vec(D_MODEL), vec(D_MODEL), vec(MLA_WIDTH), vec(HGRN_WIDTH)] + [ANY] * nsw,
        out_shape=[sds((T, D_MODEL), F32), sds(wout.shape, F32), sds((T, MLA_WIDTH), MXU_DTYPE), sds((T, HGRN_WIDTH), F32),
                   sds((T, HGRN_WIDTH), F32), sds((MLA_HEADS, T), F32),
                   sds((1, D_MODEL), F32), sds((1, D_MODEL), F32), sds((1, MLA_WIDTH), F32), sds((1, HGRN_WIDTH), F32)]
        + _half_stack_shapes(swap),
        scratch_shapes=[pltpu.SemaphoreType.DMA((nsw,)), pltpu.SemaphoreType.DMA((nsw,))] if nsw else [],
        compiler_params=_params(("arbitrary",)),
    )(dzp, dh2, h1, y1, mixb, o_raw, oh_raw, xph, wout, w_fpre, w_post, w_mla, w_hg, *swap)


def _in_bwd(x, dh1, cq, ckv, dq, dk, dv, dhq, dhf, dhi, dhg, rc, rs, w_pre, win, qnw, wq, kvnw, wk, wv, tt=256):
    T = x.shape[0]

    def body(x_ref, dh1_ref, cq_ref, ckv_ref, dq_ref, dk_ref, dv_ref, dhq_ref, dhf_ref, dhi_ref, dhg_ref, rc_ref, rs_ref,
             wpre_ref, win_ref, qnw_ref, wq_ref, kvnw_ref, wk_ref, wv_ref,
             dx_ref, dwin_ref, dwq_ref, dwk_ref, dwv_ref, dwpre_ref, dqnw_ref, dkvnw_ref):
        @pl.when(pl.program_id(0) == 0)
        def _():
            for r in (dwin_ref, dwq_ref, dwk_ref, dwv_ref, dwpre_ref, dqnw_ref, dkvnw_ref):
                r[...] = jnp.zeros_like(r)

        def add_win_grad(r, first):
            for arr0, n, chip, row0 in _win_grad_segments():
                if first <= arr0 and arr0 + n <= first + r.shape[0]:
                    dwin_ref[chip, row0:row0 + n, :] += r[arr0 - first:arr0 - first + n]

        lo = Q_RANK + KV_RANK + HEAD_PAD
        wpre = wpre_ref[...]
        u, xn, rx = _rms_fwd(x_ref[...], wpre)
        dxp_h = jnp.concatenate([dhq_ref[...], dhf_ref[...], dhi_ref[...], dhg_ref[...]], axis=-1)
        add_win_grad(_mm_tn(dxp_h, u), lo)
        du = _mm(dxp_h, win_ref[lo:, :])
        c, sa, sb = _rope_tables(rc_ref[...], rs_ref[...])
        lane = lax.broadcasted_iota(jnp.int32, (tt, HEAD_PAD), 1)
        dk_all = dk_ref[...]
        dq_lin = []
        dkr = jnp.zeros((tt, HEAD_PAD), F32)
        for h in range(MLA_HEADS):
            sl = slice(HEAD_PAD * h, HEAD_PAD * (h + 1))
            dq_lin.append(_rope_bwd(dq_ref[sl, :].T, c, sa, sb))
            dkr = dkr + dk_all[:, sl]
        dq_lin = jnp.concatenate(dq_lin, axis=-1)
        dkr = jnp.where((lane >= MLA_NOPE) & (lane < MLA_QK), _rope_bwd(dkr, c, sa, sb), 0.0)
        qnw = qnw_ref[...]
        qn, cqn, rq = _rms_fwd(cq_ref[...], qnw)
        dwq_ref[...] += _mm_tn(qn, dq_lin)
        dcq, dqnw = _rms_bwd(_mm_nt(dq_lin, wq_ref[...]), cqn, rq, qnw)
        dqnw_ref[...] += dqnw
        kvnw = kvnw_ref[...]
        kvn, ckvn, rkv = _rms_fwd(ckv_ref[...], kvnw)
        dv_ = dv_ref[...]
        dwk_ref[...] += _mm_tn(kvn, dk_all)
        dwv_ref[...] += _mm_tn(kvn, dv_)
        dckv, dkvnw = _rms_bwd(_mm_nt(dk_all, wk_ref[...]) + _mm_nt(dv_, wv_ref[...]), ckvn, rkv, kvnw)
        dkvnw_ref[...] += dkvnw
        dxp_a = jnp.concatenate([dcq, dckv, dkr], axis=-1)
        add_win_grad(_mm_tn(dxp_a, u), 0)
        dx_u, dwpre = _rms_bwd(du + _mm(dxp_a, win_ref[:lo, :]), xn, rx, wpre)
        dwpre_ref[...] += dwpre
        dx_ref[...] = dh1_ref[...] + dx_u

    row = lambda w: pl.BlockSpec((tt, w), lambda i: (i, 0))
    full = lambda a: pl.BlockSpec(a.shape, lambda i: (0,) * a.ndim)
    sds = jax.ShapeDtypeStruct
    qk_w = MLA_HEADS * HEAD_PAD
    return pl.pallas_call(
        body, name="in_bwd", grid=(T // tt,),
        in_specs=[row(D_MODEL), row(D_MODEL), row(Q_RANK), row(KV_RANK), pl.BlockSpec((qk_w, tt), lambda i: (0, i)),
                  row(qk_w), row(MLA_WIDTH),
                  row(HGRN_WIDTH), row(HGRN_WIDTH), row(HGRN_WIDTH), row(HGRN_WIDTH), row(HEAD_PAD), row(HEAD_PAD),
                  full(w_pre), full(win), full(qnw), full(wq), full(kvnw), full(wk), full(wv)],
        out_specs=[row(D_MODEL), pl.BlockSpec(WIN_COMM_SHAPE, lambda i: (0, 0, 0)), full(wq), full(wk), full(wv),
                   full(w_pre), full(qnw), full(kvnw)],
        out_shape=[sds((T, D_MODEL), F32), sds(WIN_COMM_SHAPE, F32), sds(wq.shape, F32), sds(wk.shape, F32),
                   sds(wv.shape, F32), sds(w_pre.shape, F32), sds(qnw.shape, F32), sds(kvnw.shape, F32)],
        compiler_params=_params(("arbitrary",)),
    )(x, dh1, cq, ckv, dq, dk, dv, dhq, dhf, dhi, dhg, rc, rs, w_pre, win, qnw, wq, kvnw, wk, wv)


def _arrange_weights(win_t, wuq_full, wukv):
    dt = win_t.dtype
    z = lambda n: jnp.zeros((n, D_MODEL), dt)
    s2 = Q_RANK + KV_RANK
    win_arr = jnp.concatenate([win_t[:s2], z(MLA_NOPE), win_t[s2:s2 + MLA_ROPE], z(HEAD_PAD - MLA_QK),
                               win_t[s2 + MLA_ROPE:]], axis=0)
    wq_arr = jnp.pad(wuq_full, ((0, 0), (0, 0), (0, HEAD_PAD - MLA_QK))).reshape(Q_RANK, MLA_HEADS * HEAD_PAD)
    wk_arr = jnp.pad(wukv[:, :, :MLA_NOPE], ((0, 0), (0, 0), (0, HEAD_PAD - MLA_NOPE))).reshape(
        KV_RANK, MLA_HEADS * HEAD_PAD)
    wv_arr = wukv[:, :, MLA_NOPE:].reshape(KV_RANK, MLA_WIDTH)
    return win_arr, wq_arr, wk_arr, wv_arr


WIN_COMM_SHAPE = (N_CHIPS, FF_SHARD, D_MODEL)


def _win_grad_segments():
    s2 = Q_RANK + KV_RANK
    runs = [(0, s2, 0), (s2, s2 + MLA_ROPE, MLA_NOPE), (s2 + MLA_ROPE, D_IN, HEAD_PAD - MLA_ROPE)]
    per = D_IN // N_CHIPS
    segs = []
    for lo, hi, shift in runs:
        for k in range(N_CHIPS):
            a, b = max(lo, per * k), min(hi, per * (k + 1))
            if a < b:
                segs.append((a + shift, b - a, k, a - per * k))
    return segs


def _unarrange_grads(dwq_arr, dwk_arr, dwv_arr):
    dwuq = dwq_arr.reshape(Q_RANK, MLA_HEADS, HEAD_PAD)[:, :, :MLA_QK]
    dwukv = jnp.concatenate([dwk_arr.reshape(KV_RANK, MLA_HEADS, HEAD_PAD)[:, :, :MLA_NOPE],
                             dwv_arr.reshape(KV_RANK, MLA_HEADS, MLA_V)], axis=-1)
    return dwuq, dwukv


def _rope_inv_freq():
    inv = 1.0 / (ROPE_THETA ** (jnp.arange(0, MLA_ROPE, 2, dtype=F32) / MLA_ROPE))
    z = lambda n: jnp.zeros((n,), F32)
    return jnp.concatenate([z(MLA_NOPE), inv, inv, z(HEAD_PAD - MLA_QK)]).reshape(1, HEAD_PAD)


def _local_step(x, pos, tgt, small, win_arr, wq_arr, wk_arr, wv_arr, late, place=None):
    invf = _rope_inv_freq()
    cq, ckv, xph, qb, kb, vb, kt, vt, rc, rs = _in_fwd(x, pos, invf, small["attn_pre_norm"], win_arr, small["mla_q_norm"],
                                               wq_arr, small["mla_kv_norm"], wk_arr, wv_arr)
    if place is None:
        o_raw, lse = _attn_fwd_t(qb, kb, vt)
        wout, wg, wu, wd = late
    else:
        o_raw, lse, *stacks = _attn_fwd_t(qb, kb, vt, gather=late)
        wout, wg, wu, wd = [lax.dynamic_update_slice(s, l[None], (place[1], 0, 0)) for s, l in zip(stacks, late)]
        wout = wout.reshape(D_MODEL, D_MODEL)
    oh_raw, states = _hgrn_fwd(xph, small["hgrn_lb_logits"])
    h1, y1, zb, mixb = _proj_fwd(x, o_raw, oh_raw, xph, wout, small["mla_out_norm"], small["hgrn_out_norm"],
                                 small["attn_post_norm"], small["ffn_pre_norm"])
    g, up, dy2b, dh2, loss_acc, d_fpost = _ffn_fwd(zb, h1, tgt, small["ffn_post_norm"], wg, wu, wd)
    dwg, dwu, dwd, dzp = _ffn_bwd(zb, g, up, dy2b, wg, wu, wd)
    ffn_grads = [] if place is None else [dwg, dwu, dwd]
    dh1, dwout, d_o, d_oh, dhg, dvec, d_fpre, d_post, d_mla, d_hg, *ffn_rs = _mid_bwd(
        dzp, dh2, h1, y1, mixb, o_raw, oh_raw, xph, wout, small["ffn_pre_norm"], small["attn_post_norm"],
        small["mla_out_norm"], small["hgrn_out_norm"], swap=ffn_grads)
    ffn_ps = _pair_sum(place, ffn_grads, ffn_rs, name="pair_sum_ffn") if ffn_grads else []
    dq, dk, dv, *ffn_ris = _attn_bwd_t(qb, kb, kt, vb, d_o, lse, dvec.reshape(lse.shape), send=ffn_ps)
    dhq, dhf, dhi, d_lbl = _hgrn_bwd(xph, small["hgrn_lb_logits"], states, d_oh)
    dx, dwin4, dwq_arr, dwk_arr, dwv_arr, d_pre, d_qn, d_kvn = _in_bwd(
        x, dh1, cq, ckv, dq, dk, dv, dhq, dhf, dhi, dhg, rc, rs, small["attn_pre_norm"], win_arr,
        small["mla_q_norm"], wq_arr, small["mla_kv_norm"], wk_arr, wv_arr)
    dwuq, dwukv = _unarrange_grads(dwq_arr, dwk_arr, dwv_arr)
    loss = 0.5 * jnp.sum(loss_acc) * (1.0 / D_MODEL)
    grads = dict(attn_pre_norm=d_pre, w_in=dwin4, mla_q_norm=d_qn, mla_w_uq=dwuq, mla_kv_norm=d_kvn, mla_w_ukv=dwukv,
                 mla_out_norm=d_mla, hgrn_lb_logits=d_lbl, hgrn_out_norm=d_hg, w_out=dwout, attn_post_norm=d_post,
                 ffn_pre_norm=d_fpre, w_gate=dwg, w_up=dwu, w_down=dwd, ffn_post_norm=d_fpost)
    if place is None:
        return loss, dx, grads
    return loss, dx, grads, (ffn_rs, ffn_ris)


def _place():
    x, y, c = lax.axis_index("x"), lax.axis_index("y"), lax.axis_index("c")
    others = [(1 - x, y), (x, 1 - y), (1 - x, 1 - y)]
    return x, y, c, 2 * x + y, (x, y, 1 - c), others


def _half(ref, c, rows):
    return ref.at[pl.ds(pl.multiple_of(c * rows, 8), rows)]


def _rcopy(src, dst, send, recv, k, to):
    return pltpu.make_async_remote_copy(src_ref=src, dst_ref=dst, send_sem=send.at[k], recv_sem=recv.at[k],
                                        device_id=to, device_id_type=MESH)


class _Gather:
    def __init__(self, ins, outs, send, recv):
        self.ins, self.outs, self.send, self.recv = ins, outs, send, recv
        self.n = len(ins)
        self.halves = [r.shape[0] // 2 for r in ins]
        _, _, self.c, self.me, self.sib, self.others = _place()

    def _each(self):
        for j, (px, py) in enumerate(self.others):
            for a in range(self.n):
                yield j * self.n + a, a, 2 * px + py, (px, py, self.c)

    def sends(self):
        return [_rcopy(_half(self.ins[a], self.c, self.halves[a]), _half(self.outs[a].at[self.me], self.c, self.halves[a]),
                       self.send, self.recv, k, to) for k, a, _, to in self._each()]

    def arrivals(self):
        parts = [(k, _half(self.outs[a].at[chip], self.c, self.halves[a]), to) for k, a, chip, to in self._each()]
        return [_rcopy(p, p, self.send, self.recv, k, to) for k, p, to in parts]

    def forwards(self):
        parts = [(k, _half(self.outs[a].at[chip], self.c, self.halves[a])) for k, a, chip, _ in self._each()]
        return [_rcopy(p, p, self.send, self.recv, 3 * self.n + k, self.sib) for k, p in parts]

    def forward_arrivals(self):
        parts = [(k, _half(self.outs[a].at[chip], 1 - self.c, self.halves[a])) for k, a, chip, _ in self._each()]
        return [_rcopy(p, p, self.send, self.recv, 3 * self.n + k, self.sib) for k, p in parts]

    @staticmethod
    def out_shapes(arrs):
        return [jax.ShapeDtypeStruct((N_CHIPS,) + a.shape, a.dtype) for a in arrs]

    @staticmethod
    def semaphores(arrs):
        return [pltpu.SemaphoreType.DMA((6 * len(arrs),)), pltpu.SemaphoreType.DMA((6 * len(arrs),))]


def _gather_chips(arrs, name):
    n = len(arrs)

    def body(*refs):
        gat = _Gather(refs[:n], refs[n:2 * n], *refs[2 * n:])
        sends, forwards = gat.sends(), gat.forwards()
        for cp in sends:
            cp.start()
        for arrival, fw in zip(gat.arrivals(), forwards):
            arrival.wait_recv()
            fw.start()
        for arrival in gat.forward_arrivals():
            arrival.wait_recv()
        for cp in sends + forwards:
            cp.wait_send()

    return pl.pallas_call(body, name=name, in_specs=[ANY] * n, out_specs=[ANY] * n, out_shape=_Gather.out_shapes(arrs),
                          scratch_shapes=_Gather.semaphores(arrs))(*arrs)


GRAD_BLOCKS = 2


def _pair_swap_copies(g_refs, r_refs, send, recv):
    _, _, c, _, sib, _ = _place()
    copies = []
    for a, (g, r) in enumerate(zip(g_refs, r_refs)):
        h = g.shape[1] // 2
        copies.append(_rcopy(g.at[:, pl.ds(pl.multiple_of((1 - c) * h, 8), h)], r, send, recv, a, sib))
    return copies


def _half_stack_shapes(gs, dtype=None):
    return [jax.ShapeDtypeStruct((N_CHIPS, g.shape[1] // 2, g.shape[2]), dtype or g.dtype) for g in gs]


def _pair_swap(gs, sm):
    n = len(gs)

    def body(*refs):
        g_refs, sm_ref = refs[:n], refs[n]
        r_refs, ssib_ref = refs[n + 1:2 * n + 1], refs[2 * n + 1]
        send, recv = refs[2 * n + 2:]
        copies = _pair_swap_copies(g_refs, r_refs, send, recv)
        copies.append(_rcopy(sm_ref, ssib_ref, send, recv, n, _place()[4]))
        for cp in copies:
            cp.start()
        for cp in copies:
            cp.wait()

    return pl.pallas_call(
        body, name="pair_swap", in_specs=[ANY] * (n + 1), out_specs=[ANY] * (n + 1),
        out_shape=_half_stack_shapes(gs) + [jax.ShapeDtypeStruct(sm.shape, sm.dtype)],
        scratch_shapes=[pltpu.SemaphoreType.DMA((n + 1,)), pltpu.SemaphoreType.DMA((n + 1,))],
    )(*gs, sm)


def _pair_sum(place, gs, rs, small=None, name="pair_sum"):
    n = len(gs)
    nb = GRAD_BLOCKS

    def body(place_ref, *refs):
        g_refs, r_refs, p_refs = refs[:n], refs[n:2 * n], refs[-n - 1:-1] if small else refs[-n:]
        for a in range(n):
            p_refs[a][0] = (g_refs[a][0] + r_refs[a][0]).astype(p_refs[a].dtype)
        if small:
            @pl.when((pl.program_id(0) == 0) & (pl.program_id(1) == 0))
            def _():
                refs[-1][...] = refs[2 * n][...] + refs[2 * n + 1][...]

    in_specs, out_specs = [], []
    for g in gs:
        blk = (1, g.shape[1] // 2 // nb, g.shape[2])
        in_specs.append(pl.BlockSpec(blk, lambda i, k, p: (k, p[0] * nb + i, 0)))
    for g in gs:
        blk = (1, g.shape[1] // 2 // nb, g.shape[2])
        in_specs.append(pl.BlockSpec(blk, lambda i, k, p: (k, i, 0)))
        out_specs.append(pl.BlockSpec(blk, lambda i, k, p: (k, i, 0)))
    out_shape = _half_stack_shapes(gs, BF16)
    if small:
        sm_spec = pl.BlockSpec(small[0].shape, lambda i, k, p: (0, 0))
        in_specs += [sm_spec, sm_spec]
        out_specs.append(sm_spec)
        out_shape.append(jax.ShapeDtypeStruct(small[0].shape, F32))
    return pl.pallas_call(
        body, name=name,
        grid_spec=pltpu.PrefetchScalarGridSpec(num_scalar_prefetch=1, grid=(nb, N_CHIPS), in_specs=in_specs,
                                               out_specs=out_specs),
        out_shape=out_shape,
        compiler_params=_params(("arbitrary", "arbitrary")),
    )(place, *gs, *rs, *(small or ()))


def _chip_swap_copies(p_refs, ri_refs, send, recv):
    _, _, c, _, _, others = _place()
    n = len(p_refs)
    return [_rcopy(p_refs[a].at[2 * px + py], ri_refs[a].at[j], send, recv, j * n + a, (px, py, c))
            for j, (px, py) in enumerate(others) for a in range(n)]


def _chip_swap_shapes(ps):
    return [jax.ShapeDtypeStruct((3,) + p.shape[1:], p.dtype) for p in ps]


def _chip_swap(ps, pair):
    n = len(ps)
    hs = SMALL_ROWS // 2

    def body(*refs):
        p_refs, pair_ref = refs[:n], refs[n]
        ri_refs, sm4_ref = refs[n + 1:2 * n + 1], refs[2 * n + 1]
        send, recv, lsem = refs[2 * n + 2:]
        x, y, c, me, sib, others = _place()
        local = pltpu.make_async_copy(pair_ref, sm4_ref.at[me], lsem.at[0])
        local.start()
        copies = _chip_swap_copies(p_refs, ri_refs, send, recv)
        arrivals = list(copies)
        for j, (px, py) in enumerate(others):
            copies.append(_rcopy(_half(pair_ref, c, hs), _half(sm4_ref.at[me], c, hs), send, recv, 3 * n + j, (px, py, c)))
            part = _half(sm4_ref.at[2 * px + py], c, hs)
            arrivals.append(_rcopy(part, part, send, recv, 3 * n + j, (px, py, c)))
        for cp in copies:
            cp.start()
        for arrival in arrivals:
            arrival.wait_recv()
        for cp in copies:
            cp.wait_send()
        local.wait()

    k = 3 * (n + 1)
    return pl.pallas_call(
        body, name="chip_swap", in_specs=[ANY] * (n + 1), out_specs=[ANY] * (n + 1),
        out_shape=_chip_swap_shapes(ps) + [jax.ShapeDtypeStruct((N_CHIPS,) + pair.shape, pair.dtype)],
        scratch_shapes=[pltpu.SemaphoreType.DMA((k,)), pltpu.SemaphoreType.DMA((k,)), pltpu.SemaphoreType.DMA((1,))],
    )(*ps, pair)


def _chip_sum(place, gs, rs, ris):
    n = len(gs)
    nb = GRAD_BLOCKS

    def body(place_ref, *refs):
        g_refs, r_refs, ri_refs, o_refs = refs[:n], refs[n:2 * n], refs[2 * n:3 * n], refs[3 * n:]
        for a in range(n):
            ri = ri_refs[a]
            o_refs[a][...] = (g_refs[a][0] + r_refs[a][0]) + ri[0].astype(F32) + ri[1].astype(F32) + ri[2].astype(F32)

    in_specs, out_specs, out_shape = [], [], []
    for g in gs:
        blk = (1, g.shape[1] // 2 // nb, g.shape[2])
        in_specs.append(pl.BlockSpec(blk, lambda i, p: (p[1], p[0] * nb + i, 0)))
    for g in gs:
        blk = (1, g.shape[1] // 2 // nb, g.shape[2])
        in_specs.append(pl.BlockSpec(blk, lambda i, p: (p[1], i, 0)))
    for g in gs:
        rb = g.shape[1] // 2 // nb
        in_specs.append(pl.BlockSpec((3, rb, g.shape[2]), lambda i, p: (0, i, 0)))
        out_specs.append(pl.BlockSpec((rb, g.shape[2]), lambda i, p: (p[0] * nb + i, 0)))
        out_shape.append(jax.ShapeDtypeStruct(g.shape[1:], F32))
    return pl.pallas_call(
        body, name="chip_sum",
        grid_spec=pltpu.PrefetchScalarGridSpec(num_scalar_prefetch=1, grid=(nb,), in_specs=in_specs, out_specs=out_specs),
        out_shape=out_shape,
        compiler_params=_params(("arbitrary",)),
    )(place, *gs, *rs, *ris)


def _pair_fill(gfs, sm4):
    n = len(gfs)
    hs = SMALL_ROWS // 2

    def body(*refs):
        g_refs, sm4_ref = refs[n + 1:2 * n + 1], refs[2 * n + 1]
        send, recv = refs[2 * n + 2:]
        x, y, c, me, sib, others = _place()
        copies, waits = [], []
        for a in range(n):
            h = gfs[a].shape[0] // 2
            mine, theirs = _half(g_refs[a], c, h), _half(g_refs[a], 1 - c, h)
            copies.append(pltpu.make_async_remote_copy(src_ref=mine, dst_ref=mine, send_sem=send.at[a],
                                                       recv_sem=recv.at[a], device_id=sib, device_id_type=MESH))
            waits.append(pltpu.make_async_remote_copy(src_ref=theirs, dst_ref=theirs, send_sem=send.at[a],
                                                      recv_sem=recv.at[a], device_id=sib, device_id_type=MESH))
        for j, (px, py) in enumerate(others):
            chip = 2 * px + py
            mine, theirs = _half(sm4_ref.at[chip], c, hs), _half(sm4_ref.at[chip], 1 - c, hs)
            copies.append(pltpu.make_async_remote_copy(src_ref=mine, dst_ref=mine, send_sem=send.at[n + j],
                                                       recv_sem=recv.at[n + j], device_id=sib, device_id_type=MESH))
            waits.append(pltpu.make_async_remote_copy(src_ref=theirs, dst_ref=theirs, send_sem=send.at[n + j],
                                                      recv_sem=recv.at[n + j], device_id=sib, device_id_type=MESH))
        for cp in copies:
            cp.start()
        for w in waits:
            w.wait_recv()
        for cp in copies:
            cp.wait_send()

    return pl.pallas_call(
        body, name="pair_fill", in_specs=[ANY] * (n + 1), out_specs=[ANY] * (n + 1),
        out_shape=[jax.ShapeDtypeStruct(g.shape, g.dtype) for g in gfs] + [jax.ShapeDtypeStruct(sm4.shape, sm4.dtype)],
        input_output_aliases={i: i for i in range(n + 1)},
        scratch_shapes=[pltpu.SemaphoreType.DMA((n + 3,)), pltpu.SemaphoreType.DMA((n + 3,))],
    )(*gfs, sm4)


def _adamw_math(w, g, m, v):
    m = ADAM_B1 * m + (1.0 - ADAM_B1) * g
    v = ADAM_B2 * v + (1.0 - ADAM_B2) * (g * g)
    m_hat = m / (1.0 - ADAM_B1 ** ADAM_STEP)
    v_hat = v / (1.0 - ADAM_B2 ** ADAM_STEP)
    return -ADAM_LR * (m_hat / (jnp.sqrt(v_hat) + ADAM_EPS) + ADAM_WD * w), m, v


def _adamw(items, steps, name):
    n = len(items)

    def body(*refs):
        for a in range(n):
            d, mo, vo = _adamw_math(*(r[...] for r in refs[4 * a:4 * a + 4]))
            for out, val in zip(refs[4 * n + 3 * a:4 * n + 3 * a + 3], (d, mo, vo)):
                out[...] = val

    spec = lambda w: pl.BlockSpec((w.shape[0] // steps, w.shape[1]), lambda i: (i, 0))
    flat = pl.pallas_call(
        body, name=name, grid=(steps,), in_specs=[spec(it[0]) for it in items for _ in range(4)],
        out_specs=[spec(it[0]) for it in items for _ in range(3)],
        out_shape=[jax.ShapeDtypeStruct(it[0].shape, F32) for it in items for _ in range(3)],
        compiler_params=_params(("arbitrary",)),
    )(*[a for it in items for a in it])
    return [flat[3 * a:3 * a + 3] for a in range(n)]


def _adamw_small(sm4, wmv):
    views = SMALL_VIEWS[:-1]
    n = len(views)

    def body(sm4_ref, *refs):
        g_all = ((sm4_ref[0] + sm4_ref[1]) + sm4_ref[2]) + sm4_ref[3]
        row = 0
        for a, (_, rows, cols) in enumerate(views):
            g = g_all[row:row + rows, :cols]
            row += -(-rows // ROW_TILE) * ROW_TILE
            d, mo, vo = _adamw_math(refs[3 * a][...], g, refs[3 * a + 1][...], refs[3 * a + 2][...])
            for out, val in zip(refs[3 * n + 4 * a:3 * n + 4 * a + 4], (g, d, mo, vo)):
                out[...] = val
        refs[-1][...] = g_all[row:row + 1, :128]

    flat = pl.pallas_call(
        body, name="adamw_small",
        out_shape=[jax.ShapeDtypeStruct((rows, cols), F32) for _, rows, cols in views for _ in range(4)]
        + [jax.ShapeDtypeStruct((1, 128), F32)],
        compiler_params=pltpu.CompilerParams(vmem_limit_bytes=VMEM_LIMIT),
    )(sm4, *[a for t in wmv for a in t])
    return [flat[4 * a:4 * a + 4] for a in range(n)] + [flat[-1]]


SMALL_NAMES = ("attn_pre_norm", "mla_q_norm", "mla_kv_norm", "mla_w_ukv", "mla_out_norm", "hgrn_lb_logits",
               "hgrn_out_norm", "attn_post_norm", "ffn_pre_norm", "ffn_post_norm")
BIG_NAMES = ("w_in", "mla_w_uq", "w_out", "w_gate", "w_up", "w_down")
WEIGHT_NAMES = ("attn_pre_norm", "w_in", "mla_q_norm", "mla_w_uq", "mla_kv_norm", "mla_w_ukv", "mla_out_norm",
                "hgrn_lb_logits", "hgrn_out_norm", "w_out", "attn_post_norm", "ffn_pre_norm", "w_gate", "w_up", "w_down",
                "ffn_post_norm")


UQ_COMM_SHAPE = (192, 384)


def _pack_small(vals):
    parts = []
    for name, rows, cols in SMALL_VIEWS:
        pad_rows = -(-rows // ROW_TILE) * ROW_TILE - rows
        parts.append(jnp.pad(vals[name].reshape(rows, cols), ((0, pad_rows), (0, D_MODEL - cols))))
    return jnp.concatenate(parts, axis=0)


def kernel(x, positions, attn_pre_norm, w_in, mla_q_norm, mla_w_uq, mla_kv_norm, mla_w_ukv, mla_out_norm, hgrn_lb_logits, hgrn_out_norm, w_out, attn_post_norm, ffn_pre_norm, w_gate, w_up, w_down, ffn_post_norm, loss_target, m_attn_pre_norm, m_w_in, m_mla_q_norm, m_mla_w_uq, m_mla_kv_norm, m_mla_w_ukv, m_mla_out_norm, m_hgrn_lb_logits, m_hgrn_out_norm, m_w_out, m_attn_post_norm, m_ffn_pre_norm, m_w_gate, m_w_up, m_w_down, m_ffn_post_norm, v_attn_pre_norm, v_w_in, v_mla_q_norm, v_mla_w_uq, v_mla_kv_norm, v_mla_w_ukv, v_mla_out_norm, v_hgrn_lb_logits, v_hgrn_out_norm, v_w_out, v_attn_post_norm, v_ffn_pre_norm, v_w_gate, v_w_up, v_w_down, v_ffn_post_norm):
    args = locals()
    W = {n: args[n] for n in WEIGHT_NAMES}
    M = {n: args["m_" + n] for n in WEIGHT_NAMES}
    V = {n: args["v_" + n] for n in WEIGHT_NAMES}
    T = x.shape[1]
    cx, cy, cc = lax.axis_index("x"), lax.axis_index("y"), lax.axis_index("c")

    win_rows = D_IN // N_CHIPS
    shard2d = {"w_in": (win_rows, D_MODEL), "mla_w_uq": (Q_RANK // N_CHIPS, MLA_HEADS * MLA_QK),
               "w_out": (D_MODEL // N_CHIPS, D_MODEL), "w_gate": (FF_SHARD, D_MODEL), "w_up": (FF_SHARD, D_MODEL),
               "w_down": (FF_SHARD, D_MODEL)}
    transposed = ("w_in", "w_gate", "w_up")
    to2d = lambda n, a: a[0].T if n in transposed else a.reshape(shard2d[n])
    from2d = lambda n, t: t.T[None] if n in transposed else t.reshape(W[n].shape)
    me = 2 * cx + cy
    place = jnp.stack([cc, me]).astype(jnp.int32)
    local_b = [to2d(n, W[n]).astype(BF16) for n in BIG_NAMES]
    local_b[0] = jnp.pad(local_b[0], ((0, FF_SHARD - win_rows), (0, 0)))
    stacks = _gather_chips(local_b[:2], "gather_weights")
    win4, wuq4 = [lax.dynamic_update_slice(s, l[None], (me, 0, 0)) for s, l in zip(stacks, local_b)]
    win_t = win4[:, :win_rows].reshape(D_IN, D_MODEL)
    wuq_full = wuq4.reshape(Q_RANK, MLA_HEADS, MLA_QK)
    win_arr, wq_arr, wk_arr, wv_arr = _arrange_weights(win_t, wuq_full, mla_w_ukv[0].astype(BF16))
    small = {n: W[n][0] if n == "mla_w_ukv" else W[n].reshape(-1, W[n].shape[-1]) for n in SMALL_NAMES}

    loss_local, dx, grads, (ffn_rs, ffn_ris) = _local_step(x[0], positions.reshape(T, 1), loss_target[0], small, win_arr,
                                                           wq_arr, wk_arr, wv_arr, local_b[2:], place)

    gs = [grads["w_in"], grads["mla_w_uq"].reshape((N_CHIPS,) + UQ_COMM_SHAPE), grads["w_out"].reshape((N_CHIPS,) + shard2d["w_out"])]
    ffn_gs = [grads["w_gate"], grads["w_up"], grads["w_down"]]
    sm = _pack_small({**grads, "loss": loss_local})
    *rs, ssib = _pair_swap(gs, sm)
    *ps, pair = _pair_sum(place, gs, rs, small=(sm, ssib))
    *ris, sm4 = _chip_swap(ps, pair)
    gfs = _chip_sum(place, gs + ffn_gs, rs + ffn_rs, ris + ffn_ris)
    *gfin, smf = _pair_fill(gfs, sm4)

    G, DW, NM, NV = {}, {}, {}, {}
    g2d = {n: gfin[k].reshape((-1,) + shard2d[n][1:]) for k, n in enumerate(BIG_NAMES)}
    for names_, steps in ((("w_in", "mla_w_uq"), 3), (("w_out", "w_gate", "w_up", "w_down"), 8)):
        res = _adamw([(to2d(n, W[n]), g2d[n], to2d(n, M[n]), to2d(n, V[n])) for n in names_], steps, "adamw_" + names_[0])
        for n, (d, mo, vo) in zip(names_, res):
            G[n] = from2d(n, g2d[n][:shard2d[n][0]])
            DW[n], NM[n], NV[n] = (from2d(n, t) for t in (d, mo, vo))
    view2d = lambda n, a: a.reshape(next((r, c) for name, r, c in SMALL_VIEWS if name == n))
    *res, loss_row = _adamw_small(smf, [tuple(view2d(n, t[n]) for t in (W, M, V)) for n in SMALL_NAMES])
    for n, outs in zip(SMALL_NAMES, res):
        G[n], DW[n], NM[n], NV[n] = (t.reshape(W[n].shape) for t in outs)
    loss = loss_row[0, 0]
    return (loss, dx[None], *[G[n] for n in WEIGHT_NAMES], *[DW[n] for n in WEIGHT_NAMES],
            *[NM[n] for n in WEIGHT_NAMES], *[NV[n] for n in WEIGHT_NAMES])
```

```python
import jax
import jax.numpy as jnp
from jax import lax
from jax.experimental import pallas as pl
from jax.experimental.pallas import tpu as pltpu

F32 = jnp.float32
BF16 = jnp.bfloat16
MXU_DTYPE = BF16

D_MODEL = 1024
MLA_HEADS = 8
MLA_NOPE = 64
MLA_ROPE = 32
MLA_V = 64
MLA_QK = MLA_NOPE + MLA_ROPE
Q_RANK = 384
KV_RANK = 128
MLA_WIDTH = MLA_HEADS * MLA_V
HEAD_PAD = 128
HGRN_HEADS = 4
HGRN_DIM = 128
HGRN_WIDTH = HGRN_HEADS * HGRN_DIM
CHUNK = 64
SUB = 16
HGRN_CPI = 4
D_IN = Q_RANK + KV_RANK + MLA_ROPE + 4 * HGRN_WIDTH
D_IN_ARR = Q_RANK + KV_RANK + HEAD_PAD + 4 * HGRN_WIDTH
D_FF = 2816
N_CHIPS = 4
FF_SHARD = D_FF // N_CHIPS
EPS = 1e-6
ROPE_THETA = 10000.0
ATTN_SCALE = MLA_QK ** -0.5
ATTN_SCALE_LOG2 = ATTN_SCALE * 1.4426950408889634
NEG_BIG = -1e30

ADAM_LR = 0.001
ADAM_B1 = 0.9
ADAM_B2 = 0.999
ADAM_EPS = 1e-08
ADAM_WD = 0.01
ADAM_STEP = 10

VMEM_LIMIT = 56 * 1024 * 1024

SMALL_VIEWS = (("attn_pre_norm", 1, 1024), ("mla_q_norm", 1, 384), ("mla_kv_norm", 1, 128), ("mla_w_ukv", 128, 1024),
               ("mla_out_norm", 1, 512), ("hgrn_lb_logits", 2, 512), ("hgrn_out_norm", 1, 512),
               ("attn_post_norm", 1, 1024), ("ffn_pre_norm", 1, 1024), ("ffn_post_norm", 1, 1024), ("loss", 1, 1))
ROW_TILE = 8
SMALL_ROWS = sum(-(-rows // ROW_TILE) * ROW_TILE for _, rows, _ in SMALL_VIEWS)

MESH = pl.DeviceIdType.MESH
ANY = pl.BlockSpec(memory_space=pl.ANY)


def _dot(a, b, dims, exact):
    if exact:
        return lax.dot_general(a.astype(F32), b.astype(F32), (dims, ((), ())), precision=lax.Precision.HIGH,
                               preferred_element_type=F32)
    return lax.dot_general(a.astype(MXU_DTYPE), b.astype(MXU_DTYPE), (dims, ((), ())), preferred_element_type=F32)


def _mm(a, b, exact=False):
    return _dot(a, b, ((1,), (0,)), exact)


def _mm_nt(a, b, exact=False):
    return _dot(a, b, ((1,), (1,)), exact)


def _mm_tn(a, b, exact=False):
    return _dot(a, b, ((0,), (0,)), exact)


def _rms_fwd(x, w):
    r = lax.rsqrt(jnp.mean(x * x, axis=-1, keepdims=True) + EPS)
    xn = x * r
    return xn * w, xn, r


def _rms_bwd(dy, xn, r, w):
    dxn = dy * w
    dx = r * (dxn - xn * jnp.mean(dxn * xn, axis=-1, keepdims=True))
    dw = jnp.sum(dy * xn, axis=0, keepdims=True)
    return dx, dw


def _group_sums(v, gs):
    t, n = v.shape
    lane = lax.broadcasted_iota(jnp.int32, (t, 128), 1)
    out = []
    for p in range(n // 128):
        vb = v[:, 128 * p:128 * (p + 1)]
        if gs == 128:
            out.append(jnp.sum(vb, axis=-1, keepdims=True))
        else:
            out.append(jnp.sum(jnp.where(lane < 64, vb, 0.0), axis=-1, keepdims=True))
            out.append(jnp.sum(jnp.where(lane >= 64, vb, 0.0), axis=-1, keepdims=True))
    return out


def _group_bcast(sums, gs, t):
    lane = lax.broadcasted_iota(jnp.int32, (t, 128), 1)
    if gs == 128:
        return jnp.concatenate([jnp.broadcast_to(s, (t, 128)) for s in sums], axis=-1)
    return jnp.concatenate([jnp.where(lane < 64, sums[2 * p], sums[2 * p + 1]) for p in range(len(sums) // 2)],
                           axis=-1)


def _grms_fwd(x, w, gs):
    t = x.shape[0]
    r = lax.rsqrt(_group_bcast(_group_sums(x * x, gs), gs, t) * (1.0 / gs) + EPS)
    xn = x * r
    return xn * w, xn, r


def _grms_bwd(dy, xn, r, w, gs):
    t = dy.shape[0]
    dxn = dy * w
    dx = r * (dxn - xn * (_group_bcast(_group_sums(dxn * xn, gs), gs, t) * (1.0 / gs)))
    dw = jnp.sum(dy * xn, axis=0, keepdims=True)
    return dx, dw


def _rope_tables(c_tab, s_tab):
    lane = lax.broadcasted_iota(jnp.int32, c_tab.shape, 1)
    first = (lane >= MLA_NOPE) & (lane < MLA_NOPE + MLA_ROPE // 2)
    second = (lane >= MLA_NOPE + MLA_ROPE // 2) & (lane < MLA_QK)
    return c_tab, jnp.where(first, -s_tab, 0.0), jnp.where(second, s_tab, 0.0)


def _rope(v, c, sa, sb):
    return v * c + pltpu.roll(v, HEAD_PAD - MLA_ROPE // 2, 1) * sa + pltpu.roll(v, MLA_ROPE // 2, 1) * sb


def _rope_bwd(d, c, sa, sb):
    return d * c - pltpu.roll(d, HEAD_PAD - MLA_ROPE // 2, 1) * sa - pltpu.roll(d, MLA_ROPE // 2, 1) * sb


def _params(sem, vmem=VMEM_LIMIT):
    return pltpu.CompilerParams(dimension_semantics=sem, vmem_limit_bytes=vmem)


def _in_fwd(x, pos, invf, w_pre, win, qnw, wq, kvnw, wk, wv, tt=512):
    T = x.shape[0]

    def body(x_ref, pos_ref, invf_ref, wpre_ref, win_ref, qnw_ref, wq_ref, kvnw_ref, wk_ref, wv_ref,
             cq_ref, ckv_ref, xph_ref, q_ref, k_ref, v_ref, kt_ref, vt_ref, rc_ref, rs_ref):
        u, _, _ = _rms_fwd(x_ref[...], wpre_ref[...])
        lo = Q_RANK + KV_RANK + HEAD_PAD
        xp = _mm_nt(u, win_ref[:lo, :])
        xph_ref[...] = _mm_nt(u, win_ref[lo:, :])
        cq = xp[:, :Q_RANK]
        ckv = xp[:, Q_RANK:Q_RANK + KV_RANK]
        kr = xp[:, Q_RANK + KV_RANK:]
        cq_ref[...] = cq
        ckv_ref[...] = ckv
        ang = pos_ref[...].astype(F32) * invf_ref[...]
        c_tab = jnp.cos(ang)
        s_tab = jnp.sin(ang)
        rc_ref[...] = c_tab
        rs_ref[...] = s_tab
        c, sa, sb = _rope_tables(c_tab, s_tab)
        qn, _, _ = _rms_fwd(cq, qnw_ref[...])
        q = _mm(qn, wq_ref[...])
        kvn, _, _ = _rms_fwd(ckv, kvnw_ref[...])
        kn = _mm(kvn, wk_ref[...])
        v = _mm(kvn, wv_ref[...])
        v_ref[...] = v.astype(v_ref.dtype)
        vt_ref[...] = v.T.astype(vt_ref.dtype)
        krr = _rope(kr, c, sa, sb)
        for h in range(MLA_HEADS):
            sl = slice(HEAD_PAD * h, HEAD_PAD * (h + 1))
            q_ref[:, sl] = _rope(q[:, sl], c, sa, sb).astype(q_ref.dtype)
            kh = kn[:, sl] + krr
            k_ref[:, sl] = kh.astype(k_ref.dtype)
            kt_ref[sl, :] = kh.T.astype(kt_ref.dtype)

    row = lambda w: pl.BlockSpec((tt, w), lambda i: (i, 0))
    full = lambda a: pl.BlockSpec(a.shape, lambda i: (0,) * a.ndim)
    qk_w = MLA_HEADS * HEAD_PAD
    return pl.pallas_call(
        body, name="in_fwd", grid=(T // tt,),
        in_specs=[row(D_MODEL), row(1), full(invf), full(w_pre), full(win), full(qnw), full(wq), full(kvnw),
                  full(wk), full(wv)],
        out_specs=[row(Q_RANK), row(KV_RANK), row(4 * HGRN_WIDTH), row(qk_w), row(qk_w), row(MLA_WIDTH),
                   pl.BlockSpec((qk_w, tt), lambda i: (0, i)), pl.BlockSpec((MLA_WIDTH, tt), lambda i: (0, i)),
                   row(HEAD_PAD), row(HEAD_PAD)],
        out_shape=[jax.ShapeDtypeStruct((T, Q_RANK), F32), jax.ShapeDtypeStruct((T, KV_RANK), F32),
                   jax.ShapeDtypeStruct((T, 4 * HGRN_WIDTH), F32), jax.ShapeDtypeStruct((T, qk_w), MXU_DTYPE),
                   jax.ShapeDtypeStruct((T, qk_w), MXU_DTYPE), jax.ShapeDtypeStruct((T, MLA_WIDTH), MXU_DTYPE),
                   jax.ShapeDtypeStruct((qk_w, T), MXU_DTYPE), jax.ShapeDtypeStruct((MLA_WIDTH, T), MXU_DTYPE),
                   jax.ShapeDtypeStruct((T, HEAD_PAD), F32), jax.ShapeDtypeStruct((T, HEAD_PAD), F32)],
        compiler_params=_params(("arbitrary",)),
    )(x, pos, invf, w_pre, win, qnw, wq, kvnw, wk, wv)


def _attn_fwd_t(qb, kb, vt, gather=(), tq=256, hps=8):
    T = qb.shape[0]
    nq = T // tq
    ng = len(gather)
    steps = (MLA_HEADS // hps) * nq
    pass_on = steps - 3

    def body(q_ref, k_ref, vt_ref, *rest):
        o_ref, lse_ref = rest[ng:ng + 2]
        acc_scr = rest[2 * ng + 2]
        qi = pl.program_id(1)
        step_no = pl.program_id(0) * nq + qi
        if ng:
            gat = _Gather(rest[:ng], rest[ng + 2:2 * ng + 2], *rest[2 * ng + 3:])

            @pl.when(step_no == 0)
            def _():
                for cp in gat.sends():
                    cp.start()

            @pl.when(step_no == pass_on)
            def _():
                for arrival in gat.arrivals():
                    arrival.wait_recv()
                for cp in gat.forwards():
                    cp.start()

        heads = [slice(HEAD_PAD * a, HEAD_PAD * (a + 1)) for a in range(hps)]
        acc_scr[...] = jnp.zeros_like(acc_scr)

        def step(j, carry, masked):
            start = pl.multiple_of(j * tq, tq)
            scores = [_mm_nt(k_ref[pl.ds(start, tq), heads[a]], q_ref[:, heads[a]]) for a in range(hps)]
            new = []
            for a in range(hps):
                m, l = carry[a]
                s = scores[a] * ATTN_SCALE_LOG2
                if masked:
                    kk = lax.broadcasted_iota(jnp.int32, (tq, tq), 0)
                    qq = lax.broadcasted_iota(jnp.int32, (tq, tq), 1)
                    s = jnp.where(kk <= qq, s, NEG_BIG)
                m_new = jnp.maximum(m, jnp.max(s, axis=0, keepdims=True))
                alpha = jnp.exp2(m - m_new)
                p = jnp.exp2(s - m_new)
                l = l * alpha + jnp.sum(p, axis=0, keepdims=True)
                vtj = vt_ref[2 * MLA_V * (a // 2):2 * MLA_V * (a // 2 + 1), pl.ds(start, tq)]
                acc_scr[a] = acc_scr[a] * alpha + _mm(vtj, p)
                new.append((m_new, l))
            return tuple(new)

        init = tuple((jnp.full((1, tq), NEG_BIG, F32), jnp.zeros((1, tq), F32)) for _ in range(hps))
        carry = lax.fori_loop(0, qi, lambda j, c: step(j, c, False), init)
        carry = step(qi, carry, True)
        row = lax.broadcasted_iota(jnp.int32, (2 * MLA_V, tq), 0)
        for pr in range(hps // 2):
            (m0, l0), (m1, l1) = carry[2 * pr], carry[2 * pr + 1]
            ot = jnp.where(row < MLA_V, acc_scr[2 * pr] / l0, acc_scr[2 * pr + 1] / l1)
            o_ref[:, 2 * MLA_V * pr:2 * MLA_V * (pr + 1)] = ot.T
            lse_ref[pr, 0:1, :] = m0 + jnp.log2(l0)
            lse_ref[pr, 1:2, :] = m1 + jnp.log2(l1)

        if ng:
            @pl.when(step_no == steps - 1)
            def _():
                for arrival in gat.forward_arrivals():
                    arrival.wait_recv()
                for cp in gat.sends() + gat.forwards():
                    cp.wait_send()

    return pl.pallas_call(
        body, name="attn_fwd", grid=(MLA_HEADS // hps, nq),
        in_specs=[pl.BlockSpec((tq, hps * HEAD_PAD), lambda g, i: (i, g)),
                  pl.BlockSpec((T, hps * HEAD_PAD), lambda g, i: (0, g)),
                  pl.BlockSpec((hps * MLA_V, T), lambda g, i: (g, 0))] + [ANY] * ng,
        out_specs=[pl.BlockSpec((tq, hps * MLA_V), lambda g, i: (i, g)),
                   pl.BlockSpec((hps // 2, 2, tq), lambda g, i: (g, 0, i))] + [ANY] * ng,
        out_shape=[jax.ShapeDtypeStruct((T, MLA_WIDTH), F32), jax.ShapeDtypeStruct((MLA_HEADS // 2, 2, T), F32)]
        + _Gather.out_shapes(gather),
        scratch_shapes=[pltpu.VMEM((hps, 2 * MLA_V, tq), F32)] + (_Gather.semaphores(gather) if ng else []),
        compiler_params=_params(("arbitrary", "arbitrary")),
    )(qb, kb, vt, *gather)


def _attn_bwd_t(qb, kb, kt, vb, dob, lse, dvec, send=(), tq=256, hps=4):
    T = qb.shape[0]
    nq = T // tq
    ns = len(send)
    steps = (MLA_HEADS // hps) * nq

    def body(q_ref, k_ref, kt_ref, v_ref, do_ref, lse_ref, d_ref, *rest):
        dqt_ref, dk_ref, dv_ref = rest[ns:ns + 3]
        va_scr, dv_scr = rest[2 * ns + 3:2 * ns + 5]
        j = pl.program_id(1)
        step_no = pl.program_id(0) * nq + j
        if ns:
            @pl.when(step_no == 0)
            def _():
                for cp in _chip_swap_copies(rest[:ns], rest[ns + 3:2 * ns + 3], *rest[2 * ns + 5:]):
                    cp.start()

        @pl.when(j == 0)
        def _():
            dqt_ref[...] = jnp.zeros_like(dqt_ref)

        lane = lax.broadcasted_iota(jnp.int32, (tq, 2 * MLA_V), 1)
        heads = [slice(HEAD_PAD * a, HEAD_PAD * (a + 1)) for a in range(hps)]
        pairs = [slice(2 * MLA_V * p, 2 * MLA_V * (p + 1)) for p in range(hps // 2)]
        for pr in range(hps // 2):
            vpair = v_ref[:, pairs[pr]]
            va_scr[2 * pr] = jnp.where(lane < MLA_V, vpair, jnp.zeros_like(vpair))
            va_scr[2 * pr + 1] = jnp.where(lane >= MLA_V, vpair, jnp.zeros_like(vpair))
        dk_ref[...] = jnp.zeros_like(dk_ref)
        dv_scr[...] = jnp.zeros_like(dv_scr)

        def step(i, masked):
            start = pl.multiple_of(i * tq, tq)
            rows = pl.ds(start, tq)
            scores = [_mm_nt(k_ref[:, heads[a]], q_ref[rows, heads[a]]) for a in range(hps)]
            dps = [_mm_nt(va_scr[a], do_ref[rows, pairs[a // 2]]) for a in range(hps)]
            for a in range(hps):
                pr, r = a // 2, a % 2
                p = jnp.exp2(scores[a] * ATTN_SCALE_LOG2 - lse_ref[pr, r:r + 1, rows])
                if masked:
                    kk = lax.broadcasted_iota(jnp.int32, (tq, tq), 0)
                    qq = lax.broadcasted_iota(jnp.int32, (tq, tq), 1)
                    p = jnp.where(kk <= qq, p, 0.0)
                ds = p * (dps[a] - d_ref[pr, r:r + 1, rows]) * ATTN_SCALE
                dv_scr[a] += _mm(p, do_ref[rows, pairs[pr]])
                dk_ref[:, heads[a]] += _mm(ds, q_ref[rows, heads[a]])
                dqt_ref[heads[a], rows] += _mm(kt_ref[heads[a], :], ds)

        def loop_body(i, _):
            step(i, False)
            return 0

        step(j, True)
        lax.fori_loop(j + 1, nq, loop_body, 0)
        for pr in range(hps // 2):
            dv_ref[:, pairs[pr]] = jnp.where(lane < MLA_V, dv_scr[2 * pr], dv_scr[2 * pr + 1])

        if ns:
            @pl.when(step_no == steps - 1)
            def _():
                for cp in _chip_swap_copies(rest[:ns], rest[ns + 3:2 * ns + 3], *rest[2 * ns + 5:]):
                    cp.wait()

    stat = pl.BlockSpec((hps // 2, 2, T), lambda g, j: (g, 0, 0))
    return pl.pallas_call(
        body, name="attn_bwd", grid=(MLA_HEADS // hps, nq),
        in_specs=[pl.BlockSpec((T, hps * HEAD_PAD), lambda g, j: (0, g)),
                  pl.BlockSpec((tq, hps * HEAD_PAD), lambda g, j: (j, g)),
                  pl.BlockSpec((hps * HEAD_PAD, tq), lambda g, j: (g, j)),
                  pl.BlockSpec((tq, hps * MLA_V), lambda g, j: (j, g)),
                  pl.BlockSpec((T, hps * MLA_V), lambda g, j: (0, g)), stat, stat] + [ANY] * ns,
        out_specs=[pl.BlockSpec((hps * HEAD_PAD, T), lambda g, j: (g, 0)),
                   pl.BlockSpec((tq, hps * HEAD_PAD), lambda g, j: (j, g)),
                   pl.BlockSpec((tq, hps * MLA_V), lambda g, j: (j, g))] + [ANY] * ns,
        out_shape=[jax.ShapeDtypeStruct((MLA_HEADS * HEAD_PAD, T), F32),
                   jax.ShapeDtypeStruct((T, MLA_HEADS * HEAD_PAD), F32),
                   jax.ShapeDtypeStruct((T, MLA_WIDTH), F32)] + _chip_swap_shapes(send),
        scratch_shapes=[pltpu.VMEM((hps, tq, 2 * MLA_V), vb.dtype), pltpu.VMEM((hps, tq, 2 * MLA_V), F32)]
        + ([pltpu.SemaphoreType.DMA((3 * ns,)), pltpu.SemaphoreType.DMA((3 * ns,))] if ns else []),
        compiler_params=_params(("arbitrary", "arbitrary")),
    )(qb, kb, kt, vb, dob, lse, dvec, *send)


def _cumsum_rows(x):
    n = x.shape[0]
    row = lax.broadcasted_iota(jnp.int32, x.shape, 0)
    s = 1
    while s < n:
        x = x + jnp.where(row >= s, pltpu.roll(x, s, 0), 0.0)
        s *= 2
    return x


def _rev_cumsum_rows(x):
    n = x.shape[0]
    row = lax.broadcasted_iota(jnp.int32, x.shape, 0)
    s = 1
    while s < n:
        x = x + jnp.where(row < n - s, pltpu.roll(x, n - s, 0), 0.0)
        s *= 2
    return x


def _lb_from_logits(l):
    l0, l1 = l[0:1, :], l[1:2, :]
    m = jnp.maximum(l0, l1)
    e0, e1 = jnp.exp(l0 - m), jnp.exp(l1 - m)
    return e0 / (e0 + e1)


def _hgrn_gates(hq, hf, lb):
    sig_f = jax.nn.sigmoid(hf)
    f = lb + (1.0 - lb) * sig_f
    sig_q = jax.nn.sigmoid(hq)
    return sig_f, f, jnp.log(f), 1.0 - f, sig_q, hq * sig_q


def _hgrn_intra(q, kk, b, exact=False):
    row = lax.broadcasted_iota(jnp.int32, b.shape, 0)
    qs, ks, eqs, eks, a_rows = [], [], [], [], []
    for i in range(CHUNK // SUB):
        ref = b[SUB * i + SUB // 2:SUB * i + SUB // 2 + 1, :]
        eq = jnp.exp(b[SUB * i:SUB * (i + 1), :] - ref)
        ek = jnp.exp(jnp.where(row < SUB * (i + 1), ref - b, NEG_BIG))
        qi = q[SUB * i:SUB * (i + 1), :] * eq
        ki = kk * ek
        a_rows.append(_mm_nt(qi, ki, exact))
        qs.append(qi), ks.append(ki), eqs.append(eq), eks.append(ek)
    tt = lax.broadcasted_iota(jnp.int32, (CHUNK, CHUNK), 0)
    ss = lax.broadcasted_iota(jnp.int32, (CHUNK, CHUNK), 1)
    causal = ss <= tt
    a = jnp.where(causal, jnp.concatenate(a_rows, axis=0), 0.0)
    return a, causal, qs, ks, eqs, eks


def _hgrn_fwd(xph, lbl, tg=512):
    T = xph.shape[0]
    ng, ncg = T // tg, tg // CHUNK
    cols = [slice(HGRN_DIM * h, HGRN_DIM * (h + 1)) for h in range(HGRN_HEADS)]

    def body(lbl_ref, hq_ref, hf_ref, hi_ref, o_ref, st_ref, s_scr):
        @pl.when(pl.program_id(0) == 0)
        def _():
            s_scr[...] = jnp.zeros_like(s_scr)

        lb = _lb_from_logits(lbl_ref[...])

        def chunks(it, _):
            pre = []
            for k in range(HGRN_CPI):
                c = it * HGRN_CPI + k
                rows = pl.ds(pl.multiple_of(c * CHUNK, CHUNK), CHUNK)
                for cs in cols:
                    _, _, lf, kk, _, q = _hgrn_gates(hq_ref[rows, cs], hf_ref[rows, cs], lb[:, cs])
                    v = hi_ref[rows, cs]
                    b = _cumsum_rows(lf)
                    a = _hgrn_intra(q, kk, b)[0]
                    b_last = b[CHUNK - 1:CHUNK, :]
                    pre.append((c, rows, q * jnp.exp(b), a, v, jnp.exp(b_last), _mm_tn(v, kk * jnp.exp(b_last - b))))
            for i, (c, rows, qe, a, v, ebl, upd) in enumerate(pre):
                h = i % HGRN_HEADS
                st = s_scr[h]
                st_ref[h, c] = st
                o_ref[rows, cols[h]] = _mm_nt(qe, st) + _mm(a, v)
                s_scr[h] = st * ebl + upd
            return 0

        lax.fori_loop(0, ncg // HGRN_CPI, chunks, 0)

    col = lambda k: pl.BlockSpec((tg, HGRN_WIDTH), lambda g: (g, k))
    return pl.pallas_call(
        body, name="hgrn_fwd", grid=(ng,),
        in_specs=[pl.BlockSpec((2, HGRN_WIDTH), lambda g: (0, 0)), col(0), col(1), col(2)],
        out_specs=[col(0), pl.BlockSpec((HGRN_HEADS, ncg, HGRN_DIM, HGRN_DIM), lambda g: (0, g, 0, 0))],
        out_shape=[jax.ShapeDtypeStruct((T, HGRN_WIDTH), F32),
                   jax.ShapeDtypeStruct((HGRN_HEADS, T // CHUNK, HGRN_DIM, HGRN_DIM), F32)],
        scratch_shapes=[pltpu.VMEM((HGRN_HEADS, HGRN_DIM, HGRN_DIM), F32)],
        compiler_params=_params(("arbitrary",)),
    )(lbl, xph, xph, xph)


def _hgrn_bwd(xph, lbl, states, d_o, fill=(), tg=512):
    T = xph.shape[0]
    ng, ncg = T // tg, tg // CHUNK
    cols = [slice(HGRN_DIM * h, HGRN_DIM * (h + 1)) for h in range(HGRN_HEADS)]
    nsub = CHUNK // SUB
    nf = len(fill)

    def body(lbl_ref, hq_ref, hf_ref, hi_ref, st_ref, do_ref, *rest):
        dhq_ref, dhf_ref, dhi_ref, dlg_ref = rest[nf:nf + 4]
        ds_scr, dlb_scr = rest[2 * nf + 4:2 * nf + 6]
        fill_copies = lambda: _pair_fill_copies(rest[nf + 4:2 * nf + 4], *rest[2 * nf + 6:])
        g = pl.program_id(0)

        @pl.when(g == 0)
        def _():
            ds_scr[...] = jnp.zeros_like(ds_scr)
            dlb_scr[...] = jnp.zeros_like(dlb_scr)
            for cp in (fill_copies()[0] if nf else ()):
                cp.start()

        lb = _lb_from_logits(lbl_ref[...])

        def chunks(it, _):
            pre = []
            for k, h in ((k, h) for k in range(HGRN_CPI) for h in range(HGRN_HEADS)):
                cs = cols[h]
                c = ncg - 1 - (it * HGRN_CPI + k)
                rows = pl.ds(pl.multiple_of(c * CHUNK, CHUNK), CHUNK)
                hq = hq_ref[rows, cs]
                sig_f, f, lf, kk, sig_q, q = _hgrn_gates(hq, hf_ref[rows, cs], lb[:, cs])
                v = hi_ref[rows, cs]
                do = do_ref[rows, cs]
                b = _cumsum_rows(lf)
                eb = jnp.exp(b)
                a, causal, qs, ks, eqs, eks = _hgrn_intra(q, kk, b)
                b_last = b[CHUNK - 1:CHUNK, :]
                st = st_ref[h, c]
                pre.append(dict(h=h, cs=cs, rows=rows, hq=hq, sig_f=sig_f, f=f, kk=kk, sig_q=sig_q, q=q, v=v, eb=eb, qs=qs,
                                ks=ks, eqs=eqs,
                                eks=eks, ebl=jnp.exp(b_last), el=jnp.exp(b_last - b), st=st,
                                da=jnp.where(causal, _mm_nt(do, v, True), 0.0), dq=_mm(do, st, True) * eb,
                                dv=_mm_tn(a, do), dsu=_mm_tn(do, q * eb, True)))
            for w in pre:
                dq_rows = []
                dk = jnp.zeros_like(w["q"])
                for i in range(nsub):
                    dai = w["da"][SUB * i:SUB * (i + 1), :]
                    dq_rows.append(_mm(dai, w["ks"][i], True) * w["eqs"][i])
                    dk = dk + _mm_tn(dai, w["qs"][i], True) * w["eks"][i]
                w["dq"] = w["dq"] + jnp.concatenate(dq_rows, axis=0)
                w["dk"] = dk
            for w in pre:
                h, cs, rows = w["h"], w["cs"], w["rows"]
                kk, el, ebl, dst = w["kk"], w["el"], w["ebl"], ds_scr[h]
                dk_state = _mm(w["v"], dst, True) * el
                dk = w["dk"] + dk_state
                e_last = (ebl * jnp.sum(w["st"] * dst, axis=0, keepdims=True)
                          + jnp.sum(kk * dk_state, axis=0, keepdims=True))
                dlf = _rev_cumsum_rows(w["q"] * w["dq"] - kk * dk) + e_last
                ds_scr[h] = dst * ebl + w["dsu"]
                df = dlf / w["f"] - dk
                sig_f, sig_q = w["sig_f"], w["sig_q"]
                dhf_ref[rows, cs] = df * (1.0 - lb[:, cs]) * sig_f * (1.0 - sig_f)
                dlb_scr[:, cs] += jnp.sum(df * (1.0 - sig_f), axis=0, keepdims=True)
                dhq_ref[rows, cs] = w["dq"] * sig_q * (1.0 + w["hq"] * (1.0 - sig_q))
                dhi_ref[rows, cs] = w["dv"] + _mm_nt(kk * el, dst)
            return 0

        lax.fori_loop(0, ncg // HGRN_CPI, chunks, 0)

        @pl.when(g == ng - 1)
        def _():
            dl0 = dlb_scr[...] * lb * (1.0 - lb)
            dlg_ref[...] = jnp.concatenate([dl0, -dl0], axis=0)
            if nf:
                copies, waits = fill_copies()
                for w in waits:
                    w.wait_recv()
                for cp in copies:
                    cp.wait_send()

    col = lambda k: pl.BlockSpec((tg, HGRN_WIDTH), lambda g: (ng - 1 - g, k))
    logits = pl.BlockSpec((2, HGRN_WIDTH), lambda g: (0, 0))
    big = jax.ShapeDtypeStruct((T, HGRN_WIDTH), F32)
    n_in, n_out = 6, 4
    return pl.pallas_call(
        body, name="hgrn_bwd", grid=(ng,),
        in_specs=[logits, col(0), col(1), col(2),
                  pl.BlockSpec((HGRN_HEADS, ncg, HGRN_DIM, HGRN_DIM), lambda g: (0, ng - 1 - g, 0, 0)), col(0)] + [ANY] * nf,
        out_specs=[col(0), col(0), col(0), logits] + [ANY] * nf,
        out_shape=[big, big, big, jax.ShapeDtypeStruct((2, HGRN_WIDTH), F32)]
        + [jax.ShapeDtypeStruct(f.shape, f.dtype) for f in fill],
        input_output_aliases={n_in + k: n_out + k for k in range(nf)},
        scratch_shapes=[pltpu.VMEM((HGRN_HEADS, HGRN_DIM, HGRN_DIM), F32), pltpu.VMEM((1, HGRN_WIDTH), F32)]
        + ([pltpu.SemaphoreType.DMA((nf,)), pltpu.SemaphoreType.DMA((nf,))] if nf else []),
        compiler_params=_params(("arbitrary",)),
    )(lbl, xph, xph, xph, states, d_o, *fill)


def _proj_fwd(x, o_raw, oh_raw, xph, wout, w_mla, w_hg, w_post, w_fpre, tt=512):
    T = x.shape[0]

    def body(x_ref, o_ref, oh_ref, hg_ref, wout_ref, wmla_ref, whg_ref, wpost_ref, wfpre_ref,
             h1_ref, y1_ref, z_ref, mix_ref):
        om, _, _ = _grms_fwd(o_ref[...], wmla_ref[...], MLA_V)
        hg = hg_ref[...]
        ohn, _, _ = _grms_fwd(oh_ref[...], whg_ref[...], HGRN_DIM)
        mix = jnp.concatenate([om, ohn * (hg * jax.nn.sigmoid(hg))], axis=-1)
        mix_ref[...] = mix.astype(mix_ref.dtype)
        y1 = _mm(mix, wout_ref[...])
        y1_ref[...] = y1
        h1 = x_ref[...] + _rms_fwd(y1, wpost_ref[...])[0]
        h1_ref[...] = h1
        z_ref[...] = _rms_fwd(h1, wfpre_ref[...])[0].astype(z_ref.dtype)

    row = lambda w: pl.BlockSpec((tt, w), lambda i: (i, 0))
    full = lambda a: pl.BlockSpec(a.shape, lambda i: (0,) * a.ndim)
    sds = jax.ShapeDtypeStruct
    return pl.pallas_call(
        body, name="proj_fwd", grid=(T // tt,),
        in_specs=[row(D_MODEL), row(MLA_WIDTH), row(HGRN_WIDTH), pl.BlockSpec((tt, HGRN_WIDTH), lambda i: (i, 3)),
                  full(wout), full(w_mla), full(w_hg), full(w_post), full(w_fpre)],
        out_specs=[row(D_MODEL)] * 4,
        out_shape=[sds((T, D_MODEL), F32), sds((T, D_MODEL), F32), sds((T, D_MODEL), MXU_DTYPE),
                   sds((T, D_MODEL), MXU_DTYPE)],
        compiler_params=_params(("arbitrary",)),
    )(x, o_raw, oh_raw, xph, wout, w_mla, w_hg, w_post, w_fpre)


def _ffn_fwd(zb, h1, tgt, w_fpost, wg, wu, wd, tt=256):
    T = zb.shape[0]
    nj = N_CHIPS

    def body(z_ref, h1_ref, tgt_ref, wfpost_ref, wg_ref, wu_ref, wd_ref, g_ref, up_ref, dy2_ref, dh2_ref, loss_ref, dwf_ref):
        @pl.when(pl.program_id(0) == 0)
        def _():
            loss_ref[...] = jnp.zeros_like(loss_ref)
            dwf_ref[...] = jnp.zeros_like(dwf_ref)

        z = z_ref[...]
        gs = [_mm_nt(z, wg_ref[j]) for j in range(nj)]
        ups = [_mm_nt(z, wu_ref[j]) for j in range(nj)]
        y2 = jnp.zeros((tt, D_MODEL), F32)
        for j in range(nj):
            g_ref[j] = gs[j]
            up_ref[j] = ups[j]
            y2 = y2 + _mm(gs[j] * jax.nn.sigmoid(gs[j]) * ups[j], wd_ref[j])
        w = wfpost_ref[...]
        y2s, y2n, r2 = _rms_fwd(y2, w)
        e = h1_ref[...] + y2s - tgt_ref[...]
        loss_ref[...] += jnp.sum(e * e, axis=0, keepdims=True)
        dh2 = e * (1.0 / D_MODEL)
        dh2_ref[...] = dh2
        dy2, dwf = _rms_bwd(dh2, y2n, r2, w)
        dy2_ref[...] = dy2.astype(dy2_ref.dtype)
        dwf_ref[...] += dwf

    row = pl.BlockSpec((tt, D_MODEL), lambda i: (i, 0))
    vec = pl.BlockSpec((1, D_MODEL), lambda i: (0, 0))
    resident = pl.BlockSpec((nj, FF_SHARD, D_MODEL), lambda i: (0, 0, 0), pipeline_mode=pl.Buffered(1))
    act = pl.BlockSpec((nj, tt, FF_SHARD), lambda i: (0, i, 0))
    sds = jax.ShapeDtypeStruct
    return pl.pallas_call(
        body, name="ffn_fwd", grid=(T // tt,),
        in_specs=[row, row, row, vec, resident, resident, resident],
        out_specs=[act, act, row, row, vec, vec],
        out_shape=[sds((nj, T, FF_SHARD), F32), sds((nj, T, FF_SHARD), F32), sds((T, D_MODEL), MXU_DTYPE),
                   sds((T, D_MODEL), F32), sds((1, D_MODEL), F32), sds((1, D_MODEL), F32)],
        compiler_params=_params(("arbitrary",)),
    )(zb, h1, tgt, w_fpost, wg, wu, wd)


def _ffn_bwd(zb, g, up, dy2b, wg, wu, wd, tt=512):
    T = zb.shape[0]
    nj = N_CHIPS

    def body(z_ref, g_ref, up_ref, dy2_ref, wg_ref, wu_ref, wd_ref, dwg_ref, dwu_ref, dwd_ref, dz_ref):
        @pl.when(pl.program_id(1) == 0)
        def _():
            dwg_ref[...] = jnp.zeros_like(dwg_ref)
            dwu_ref[...] = jnp.zeros_like(dwu_ref)
            dwd_ref[...] = jnp.zeros_like(dwd_ref)

        z, g_, up_, dy2 = z_ref[...], g_ref[0], up_ref[0], dy2_ref[...]
        sg = jax.nn.sigmoid(g_)
        act = g_ * sg
        dff = _mm_nt(dy2, wd_ref[0])
        dwd_ref[0] += _mm_tn(act * up_, dy2)
        dg = dff * up_ * sg * (1.0 + g_ * (1.0 - sg))
        dup = dff * act
        dwg_ref[0] += _mm_tn(dg, z)
        dwu_ref[0] += _mm_tn(dup, z)
        dz_ref[0] = _mm(dg, wg_ref[0]) + _mm(dup, wu_ref[0])

    row = pl.BlockSpec((tt, D_MODEL), lambda j, i: (i, 0))
    act = pl.BlockSpec((1, tt, FF_SHARD), lambda j, i: (j, i, 0))
    w_sh = pl.BlockSpec((1, FF_SHARD, D_MODEL), lambda j, i: (j, 0, 0))
    w_grad = jax.ShapeDtypeStruct((nj, FF_SHARD, D_MODEL), F32)
    return pl.pallas_call(
        body, name="ffn_bwd", grid=(nj, T // tt),
        in_specs=[row, act, act, row, w_sh, w_sh, w_sh],
        out_specs=[w_sh, w_sh, w_sh, pl.BlockSpec((1, tt, D_MODEL), lambda j, i: (j, i, 0))],
        out_shape=[w_grad, w_grad, w_grad, jax.ShapeDtypeStruct((nj, T, D_MODEL), F32)],
        compiler_params=_params(("arbitrary", "arbitrary")),
    )(zb, g, up, dy2b, wg, wu, wd)


def _mid_bwd(dzp, dh2, h1, y1, mixb, o_raw, oh_raw, xph, wout, w_fpre, w_post, w_mla, w_hg, swap=(), tt=256):
    T = dh2.shape[0]
    nsw = len(swap)
    n_in, n_out = 13, 10

    def body(*refs):
        (dzp_ref, dh2_ref, h1_ref, y1_ref, mix_ref, o_ref, oh_ref, hg_ref, wout_ref, wfpre_ref, wpost_ref,
         wmla_ref, whg_ref) = refs[:n_in]
        (dh1_ref, dwout_ref, do_ref, doh_ref, dhg_ref, dvec_ref, dwfpre_ref, dwpost_ref, dwmla_ref,
         dwhg_ref) = refs[n_in + nsw:n_in + nsw + n_out]
        swap_copies = lambda: _pair_swap_copies(refs[n_in:n_in + nsw], refs[n_in + nsw + n_out:n_in + 2 * nsw + n_out],
                                                *refs[n_in + 2 * nsw + n_out:])

        @pl.when(pl.program_id(0) == 0)
        def _():
            for r in (dwout_ref, dwfpre_ref, dwpost_ref, dwmla_ref, dwhg_ref):
                r[...] = jnp.zeros_like(r)
            for cp in (swap_copies() if nsw else ()):
                cp.start()

        dz = dzp_ref[0] + dzp_ref[1] + dzp_ref[2] + dzp_ref[3]
        wfpre = wfpre_ref[...]
        _, h1n, r = _rms_fwd(h1_ref[...], wfpre)
        dh1_z, dwfpre = _rms_bwd(dz, h1n, r, wfpre)
        dwfpre_ref[...] += dwfpre
        dh1 = dh2_ref[...] + dh1_z
        dh1_ref[...] = dh1
        wpost = wpost_ref[...]
        _, y1n, r1 = _rms_fwd(y1_ref[...], wpost)
        dy1, dwpost = _rms_bwd(dh1, y1n, r1, wpost)
        dwpost_ref[...] += dwpost
        dmix = _mm_nt(dy1, wout_ref[...])
        dwout_ref[...] += _mm_tn(mix_ref[...], dy1)
        wmla = wmla_ref[...]
        o = o_ref[...]
        _, on, ro = _grms_fwd(o, wmla, MLA_V)
        d_o, dwmla = _grms_bwd(dmix[:, :MLA_WIDTH], on, ro, wmla, MLA_V)
        dwmla_ref[...] += dwmla
        do_ref[...] = d_o.astype(do_ref.dtype)
        hh = lax.broadcasted_iota(jnp.int32, (MLA_HEADS, MLA_WIDTH), 0)
        ll = lax.broadcasted_iota(jnp.int32, (MLA_HEADS, MLA_WIDTH), 1)
        sel = jnp.where((ll >= hh * MLA_V) & (ll < (hh + 1) * MLA_V), 1.0, 0.0)
        dvec_ref[...] = _mm_nt(sel, d_o * o, True)
        whg = whg_ref[...]
        hg = hg_ref[...]
        sg = jax.nn.sigmoid(hg)
        _, ohn, rh = _grms_fwd(oh_ref[...], whg, HGRN_DIM)
        dmh = dmix[:, MLA_WIDTH:]
        dhg_ref[...] = dmh * ohn * whg * sg * (1.0 + hg * (1.0 - sg))
        d_oh, dwhg = _grms_bwd(dmh * (hg * sg), ohn, rh, whg, HGRN_DIM)
        dwhg_ref[...] += dwhg
        doh_ref[...] = d_oh

        if nsw:
            @pl.when(pl.program_id(0) == T // tt - 1)
            def _():
                for cp in swap_copies():
                    cp.wait()

    row = lambda w: pl.BlockSpec((tt, w), lambda i: (i, 0))
    full = lambda a: pl.BlockSpec(a.shape, lambda i: (0,) * a.ndim)
    vec = lambda w: pl.BlockSpec((1, w), lambda i: (0, 0))
    sds = jax.ShapeDtypeStruct
    return pl.pallas_call(
        body, name="mid_bwd", grid=(T // tt,),
        in_specs=[pl.BlockSpec((N_CHIPS, tt, D_MODEL), lambda i: (0, i, 0)), row(D_MODEL), row(D_MODEL), row(D_MODEL),
                  row(D_MODEL), row(MLA_WIDTH), row(HGRN_WIDTH), pl.BlockSpec((tt, HGRN_WIDTH), lambda i: (i, 3)),
                  full(wout), vec(D_MODEL), vec(D_MODEL), vec(MLA_WIDTH), vec(HGRN_WIDTH)] + [ANY] * nsw,
        out_specs=[row(D_MODEL), full(wout), row(MLA_WIDTH), row(HGRN_WIDTH), row(HGRN_WIDTH),
                   pl.BlockSpec((MLA_HEADS, tt), lambda i: (0, i)),
                   vec(D_MODEL), vec(D_MODEL), vec(MLA_WIDTH), vec(HGRN_WIDTH)] + [ANY] * nsw,
        out_shape=[sds((T, D_MODEL), F32), sds(wout.shape, F32), sds((T, MLA_WIDTH), MXU_DTYPE), sds((T, HGRN_WIDTH), F32),
                   sds((T, HGRN_WIDTH), F32), sds((MLA_HEADS, T), F32),
                   sds((1, D_MODEL), F32), sds((1, D_MODEL), F32), sds((1, MLA_WIDTH), F32), sds((1, HGRN_WIDTH), F32)]
        + _half_stack_shapes(swap),
        scratch_shapes=[pltpu.SemaphoreType.DMA((nsw,)), pltpu.SemaphoreType.DMA((nsw,))] if nsw else [],
        compiler_params=_params(("arbitrary",)),
    )(dzp, dh2, h1, y1, mixb, o_raw, oh_raw, xph, wout, w_fpre, w_post, w_mla, w_hg, *swap)


def _in_bwd(x, dh1, cq, ckv, dq, dk, dv, dhq, dhf, dhi, dhg, rc, rs, w_pre, win, qnw, wq, kvnw, wk, wv, tt=256):
    T = x.shape[0]

    def body(x_ref, dh1_ref, cq_ref, ckv_ref, dq_ref, dk_ref, dv_ref, dhq_ref, dhf_ref, dhi_ref, dhg_ref, rc_ref, rs_ref,
             wpre_ref, win_ref, qnw_ref, wq_ref, kvnw_ref, wk_ref, wv_ref,
             dx_ref, dwin_ref, dwq_ref, dwk_ref, dwv_ref, dwpre_ref, dqnw_ref, dkvnw_ref):
        @pl.when(pl.program_id(0) == 0)
        def _():
            for r in (dwin_ref, dwq_ref, dwk_ref, dwv_ref, dwpre_ref, dqnw_ref, dkvnw_ref):
                r[...] = jnp.zeros_like(r)

        def add_win_grad(r, first):
            for arr0, n, chip, row0 in _win_grad_segments():
                if first <= arr0 and arr0 + n <= first + r.shape[0]:
                    dwin_ref[chip, row0:row0 + n, :] += r[arr0 - first:arr0 - first + n]

        lo = Q_RANK + KV_RANK + HEAD_PAD
        wpre = wpre_ref[...]
        u, xn, rx = _rms_fwd(x_ref[...], wpre)
        dxp_h = jnp.concatenate([dhq_ref[...], dhf_ref[...], dhi_ref[...], dhg_ref[...]], axis=-1)
        add_win_grad(_mm_tn(dxp_h, u), lo)
        du = _mm(dxp_h, win_ref[lo:, :])
        c, sa, sb = _rope_tables(rc_ref[...], rs_ref[...])
        lane = lax.broadcasted_iota(jnp.int32, (tt, HEAD_PAD), 1)
        dk_all = dk_ref[...]
        dq_lin = []
        dkr = jnp.zeros((tt, HEAD_PAD), F32)
        for h in range(MLA_HEADS):
            sl = slice(HEAD_PAD * h, HEAD_PAD * (h + 1))
            dq_lin.append(_rope_bwd(dq_ref[sl, :].T, c, sa, sb))
            dkr = dkr + dk_all[:, sl]
        dq_lin = jnp.concatenate(dq_lin, axis=-1)
        dkr = jnp.where((lane >= MLA_NOPE) & (lane < MLA_QK), _rope_bwd(dkr, c, sa, sb), 0.0)
        qnw = qnw_ref[...]
        qn, cqn, rq = _rms_fwd(cq_ref[...], qnw)
        dwq_ref[...] += _mm_tn(qn, dq_lin)
        dcq, dqnw = _rms_bwd(_mm_nt(dq_lin, wq_ref[...]), cqn, rq, qnw)
        dqnw_ref[...] += dqnw
        kvnw = kvnw_ref[...]
        kvn, ckvn, rkv = _rms_fwd(ckv_ref[...], kvnw)
        dv_ = dv_ref[...]
        dwk_ref[...] += _mm_tn(kvn, dk_all)
        dwv_ref[...] += _mm_tn(kvn, dv_)
        dckv, dkvnw = _rms_bwd(_mm_nt(dk_all, wk_ref[...]) + _mm_nt(dv_, wv_ref[...]), ckvn, rkv, kvnw)
        dkvnw_ref[...] += dkvnw
        dxp_a = jnp.concatenate([dcq, dckv, dkr], axis=-1)
        add_win_grad(_mm_tn(dxp_a, u), 0)
        dx_u, dwpre = _rms_bwd(du + _mm(dxp_a, win_ref[:lo, :]), xn, rx, wpre)
        dwpre_ref[...] += dwpre
        dx_ref[...] = dh1_ref[...] + dx_u

    row = lambda w: pl.BlockSpec((tt, w), lambda i: (i, 0))
    full = lambda a: pl.BlockSpec(a.shape, lambda i: (0,) * a.ndim)
    sds = jax.ShapeDtypeStruct
    qk_w = MLA_HEADS * HEAD_PAD
    return pl.pallas_call(
        body, name="in_bwd", grid=(T // tt,),
        in_specs=[row(D_MODEL), row(D_MODEL), row(Q_RANK), row(KV_RANK), pl.BlockSpec((qk_w, tt), lambda i: (0, i)),
                  row(qk_w), row(MLA_WIDTH),
                  row(HGRN_WIDTH), row(HGRN_WIDTH), row(HGRN_WIDTH), row(HGRN_WIDTH), row(HEAD_PAD), row(HEAD_PAD),
                  full(w_pre), full(win), full(qnw), full(wq), full(kvnw), full(wk), full(wv)],
        out_specs=[row(D_MODEL), pl.BlockSpec(WIN_COMM_SHAPE, lambda i: (0, 0, 0)), full(wq), full(wk), full(wv),
                   full(w_pre), full(qnw), full(kvnw)],
        out_shape=[sds((T, D_MODEL), F32), sds(WIN_COMM_SHAPE, F32), sds(wq.shape, F32), sds(wk.shape, F32),
                   sds(wv.shape, F32), sds(w_pre.shape, F32), sds(qnw.shape, F32), sds(kvnw.shape, F32)],
        compiler_params=_params(("arbitrary",)),
    )(x, dh1, cq, ckv, dq, dk, dv, dhq, dhf, dhi, dhg, rc, rs, w_pre, win, qnw, wq, kvnw, wk, wv)


def _arrange_weights(win_t, wuq_full, wukv):
    dt = win_t.dtype
    z = lambda n: jnp.zeros((n, D_MODEL), dt)
    s2 = Q_RANK + KV_RANK
    win_arr = jnp.concatenate([win_t[:s2], z(MLA_NOPE), win_t[s2:s2 + MLA_ROPE], z(HEAD_PAD - MLA_QK),
                               win_t[s2 + MLA_ROPE:]], axis=0)
    wq_arr = jnp.pad(wuq_full, ((0, 0), (0, 0), (0, HEAD_PAD - MLA_QK))).reshape(Q_RANK, MLA_HEADS * HEAD_PAD)
    wk_arr = jnp.pad(wukv[:, :, :MLA_NOPE], ((0, 0), (0, 0), (0, HEAD_PAD - MLA_NOPE))).reshape(
        KV_RANK, MLA_HEADS * HEAD_PAD)
    wv_arr = wukv[:, :, MLA_NOPE:].reshape(KV_RANK, MLA_WIDTH)
    return win_arr, wq_arr, wk_arr, wv_arr


WIN_COMM_SHAPE = (N_CHIPS, FF_SHARD, D_MODEL)


def _win_grad_segments():
    s2 = Q_RANK + KV_RANK
    runs = [(0, s2, 0), (s2, s2 + MLA_ROPE, MLA_NOPE), (s2 + MLA_ROPE, D_IN, HEAD_PAD - MLA_ROPE)]
    per = D_IN // N_CHIPS
    segs = []
    for lo, hi, shift in runs:
        for k in range(N_CHIPS):
            a, b = max(lo, per * k), min(hi, per * (k + 1))
            if a < b:
                segs.append((a + shift, b - a, k, a - per * k))
    return segs


def _unarrange_grads(dwq_arr, dwk_arr, dwv_arr):
    dwuq = dwq_arr.reshape(Q_RANK, MLA_HEADS, HEAD_PAD)[:, :, :MLA_QK]
    dwukv = jnp.concatenate([dwk_arr.reshape(KV_RANK, MLA_HEADS, HEAD_PAD)[:, :, :MLA_NOPE],
                             dwv_arr.reshape(KV_RANK, MLA_HEADS, MLA_V)], axis=-1)
    return dwuq, dwukv


def _rope_inv_freq():
    inv = 1.0 / (ROPE_THETA ** (jnp.arange(0, MLA_ROPE, 2, dtype=F32) / MLA_ROPE))
    z = lambda n: jnp.zeros((n,), F32)
    return jnp.concatenate([z(MLA_NOPE), inv, inv, z(HEAD_PAD - MLA_QK)]).reshape(1, HEAD_PAD)


def _local_step(x, pos, tgt, small, win_arr, wq_arr, wk_arr, wv_arr, late, place=None):
    invf = _rope_inv_freq()
    cq, ckv, xph, qb, kb, vb, kt, vt, rc, rs = _in_fwd(x, pos, invf, small["attn_pre_norm"], win_arr, small["mla_q_norm"],
                                               wq_arr, small["mla_kv_norm"], wk_arr, wv_arr)
    if place is None:
        o_raw, lse = _attn_fwd_t(qb, kb, vt)
        wout, wg, wu, wd = late
    else:
        o_raw, lse, *stacks = _attn_fwd_t(qb, kb, vt, gather=late)
        wout, wg, wu, wd = [lax.dynamic_update_slice(s, l[None], (place[1], 0, 0)) for s, l in zip(stacks, late)]
        wout = wout.reshape(D_MODEL, D_MODEL)
    oh_raw, states = _hgrn_fwd(xph, small["hgrn_lb_logits"])
    h1, y1, zb, mixb = _proj_fwd(x, o_raw, oh_raw, xph, wout, small["mla_out_norm"], small["hgrn_out_norm"],
                                 small["attn_post_norm"], small["ffn_pre_norm"])
    g, up, dy2b, dh2, loss_acc, d_fpost = _ffn_fwd(zb, h1, tgt, small["ffn_post_norm"], wg, wu, wd)
    dwg, dwu, dwd, dzp = _ffn_bwd(zb, g, up, dy2b, wg, wu, wd)
    ffn_grads = [] if place is None else [dwg, dwu, dwd]
    dh1, dwout, d_o, d_oh, dhg, dvec, d_fpre, d_post, d_mla, d_hg, *ffn_rs = _mid_bwd(
        dzp, dh2, h1, y1, mixb, o_raw, oh_raw, xph, wout, small["ffn_pre_norm"], small["attn_post_norm"],
        small["mla_out_norm"], small["hgrn_out_norm"], swap=ffn_grads)
    ffn_ps = _pair_sum(place, ffn_grads, ffn_rs, name="pair_sum_ffn") if ffn_grads else []
    dq, dk, dv, *ffn_ris = _attn_bwd_t(qb, kb, kt, vb, d_o, lse, dvec.reshape(lse.shape), send=ffn_ps)
    ffn_sums = _chip_sum(place, ffn_grads, ffn_rs, ffn_ris, name="chip_sum_ffn") if ffn_grads else []
    dhq, dhf, dhi, d_lbl, *ffn_final = _hgrn_bwd(xph, small["hgrn_lb_logits"], states, d_oh, fill=ffn_sums)
    dx, dwin4, dwq_arr, dwk_arr, dwv_arr, d_pre, d_qn, d_kvn = _in_bwd(
        x, dh1, cq, ckv, dq, dk, dv, dhq, dhf, dhi, dhg, rc, rs, small["attn_pre_norm"], win_arr,
        small["mla_q_norm"], wq_arr, small["mla_kv_norm"], wk_arr, wv_arr)
    dwuq, dwukv = _unarrange_grads(dwq_arr, dwk_arr, dwv_arr)
    loss = 0.5 * jnp.sum(loss_acc) * (1.0 / D_MODEL)
    grads = dict(attn_pre_norm=d_pre, w_in=dwin4, mla_q_norm=d_qn, mla_w_uq=dwuq, mla_kv_norm=d_kvn, mla_w_ukv=dwukv,
                 mla_out_norm=d_mla, hgrn_lb_logits=d_lbl, hgrn_out_norm=d_hg, w_out=dwout, attn_post_norm=d_post,
                 ffn_pre_norm=d_fpre, w_gate=dwg, w_up=dwu, w_down=dwd, ffn_post_norm=d_fpost)
    if place is None:
        return loss, dx, grads
    return loss, dx, grads, ffn_final


def _place():
    x, y, c = lax.axis_index("x"), lax.axis_index("y"), lax.axis_index("c")
    others = [(1 - x, y), (x, 1 - y), (1 - x, 1 - y)]
    return x, y, c, 2 * x + y, (x, y, 1 - c), others


def _half(ref, c, rows):
    return ref.at[pl.ds(pl.multiple_of(c * rows, 8), rows)]


def _rcopy(src, dst, send, recv, k, to):
    return pltpu.make_async_remote_copy(src_ref=src, dst_ref=dst, send_sem=send.at[k], recv_sem=recv.at[k],
                                        device_id=to, device_id_type=MESH)


class _Gather:
    def __init__(self, ins, outs, send, recv):
        self.ins, self.outs, self.send, self.recv = ins, outs, send, recv
        self.n = len(ins)
        self.halves = [r.shape[0] // 2 for r in ins]
        _, _, self.c, self.me, self.sib, self.others = _place()

    def _each(self):
        for j, (px, py) in enumerate(self.others):
            for a in range(self.n):
                yield j * self.n + a, a, 2 * px + py, (px, py, self.c)

    def sends(self):
        return [_rcopy(_half(self.ins[a], self.c, self.halves[a]), _half(self.outs[a].at[self.me], self.c, self.halves[a]),
                       self.send, self.recv, k, to) for k, a, _, to in self._each()]

    def arrivals(self):
        parts = [(k, _half(self.outs[a].at[chip], self.c, self.halves[a]), to) for k, a, chip, to in self._each()]
        return [_rcopy(p, p, self.send, self.recv, k, to) for k, p, to in parts]

    def forwards(self):
        parts = [(k, _half(self.outs[a].at[chip], self.c, self.halves[a])) for k, a, chip, _ in self._each()]
        return [_rcopy(p, p, self.send, self.recv, 3 * self.n + k, self.sib) for k, p in parts]

    def forward_arrivals(self):
        parts = [(k, _half(self.outs[a].at[chip], 1 - self.c, self.halves[a])) for k, a, chip, _ in self._each()]
        return [_rcopy(p, p, self.send, self.recv, 3 * self.n + k, self.sib) for k, p in parts]

    @staticmethod
    def out_shapes(arrs):
        return [jax.ShapeDtypeStruct((N_CHIPS,) + a.shape, a.dtype) for a in arrs]

    @staticmethod
    def semaphores(arrs):
        return [pltpu.SemaphoreType.DMA((6 * len(arrs),)), pltpu.SemaphoreType.DMA((6 * len(arrs),))]


def _gather_chips(arrs, name):
    n = len(arrs)

    def body(*refs):
        gat = _Gather(refs[:n], refs[n:2 * n], *refs[2 * n:])
        sends, forwards = gat.sends(), gat.forwards()
        for cp in sends:
            cp.start()
        for arrival, fw in zip(gat.arrivals(), forwards):
            arrival.wait_recv()
            fw.start()
        for arrival in gat.forward_arrivals():
            arrival.wait_recv()
        for cp in sends + forwards:
            cp.wait_send()

    return pl.pallas_call(body, name=name, in_specs=[ANY] * n, out_specs=[ANY] * n, out_shape=_Gather.out_shapes(arrs),
                          scratch_shapes=_Gather.semaphores(arrs))(*arrs)


GRAD_BLOCKS = 2


def _pair_swap_copies(g_refs, r_refs, send, recv):
    _, _, c, _, sib, _ = _place()
    copies = []
    for a, (g, r) in enumerate(zip(g_refs, r_refs)):
        h = g.shape[1] // 2
        copies.append(_rcopy(g.at[:, pl.ds(pl.multiple_of((1 - c) * h, 8), h)], r, send, recv, a, sib))
    return copies


def _half_stack_shapes(gs, dtype=None):
    return [jax.ShapeDtypeStruct((N_CHIPS, g.shape[1] // 2, g.shape[2]), dtype or g.dtype) for g in gs]


def _pair_swap(gs, sm):
    n = len(gs)

    def body(*refs):
        g_refs, sm_ref = refs[:n], refs[n]
        r_refs, ssib_ref = refs[n + 1:2 * n + 1], refs[2 * n + 1]
        send, recv = refs[2 * n + 2:]
        copies = _pair_swap_copies(g_refs, r_refs, send, recv)
        copies.append(_rcopy(sm_ref, ssib_ref, send, recv, n, _place()[4]))
        for cp in copies:
            cp.start()
        for cp in copies:
            cp.wait()

    return pl.pallas_call(
        body, name="pair_swap", in_specs=[ANY] * (n + 1), out_specs=[ANY] * (n + 1),
        out_shape=_half_stack_shapes(gs) + [jax.ShapeDtypeStruct(sm.shape, sm.dtype)],
        scratch_shapes=[pltpu.SemaphoreType.DMA((n + 1,)), pltpu.SemaphoreType.DMA((n + 1,))],
    )(*gs, sm)


def _pair_sum(place, gs, rs, small=None, name="pair_sum"):
    n = len(gs)
    nb = GRAD_BLOCKS

    def body(place_ref, *refs):
        g_refs, r_refs, p_refs = refs[:n], refs[n:2 * n], refs[-n - 1:-1] if small else refs[-n:]
        for a in range(n):
            p_refs[a][0] = (g_refs[a][0] + r_refs[a][0]).astype(p_refs[a].dtype)
        if small:
            @pl.when((pl.program_id(0) == 0) & (pl.program_id(1) == 0))
            def _():
                refs[-1][...] = refs[2 * n][...] + refs[2 * n + 1][...]

    in_specs, out_specs = [], []
    for g in gs:
        blk = (1, g.shape[1] // 2 // nb, g.shape[2])
        in_specs.append(pl.BlockSpec(blk, lambda i, k, p: (k, p[0] * nb + i, 0)))
    for g in gs:
        blk = (1, g.shape[1] // 2 // nb, g.shape[2])
        in_specs.append(pl.BlockSpec(blk, lambda i, k, p: (k, i, 0)))
        out_specs.append(pl.BlockSpec(blk, lambda i, k, p: (k, i, 0)))
    out_shape = _half_stack_shapes(gs, BF16)
    if small:
        sm_spec = pl.BlockSpec(small[0].shape, lambda i, k, p: (0, 0))
        in_specs += [sm_spec, sm_spec]
        out_specs.append(sm_spec)
        out_shape.append(jax.ShapeDtypeStruct(small[0].shape, F32))
    return pl.pallas_call(
        body, name=name,
        grid_spec=pltpu.PrefetchScalarGridSpec(num_scalar_prefetch=1, grid=(nb, N_CHIPS), in_specs=in_specs,
                                               out_specs=out_specs),
        out_shape=out_shape,
        compiler_params=_params(("arbitrary", "arbitrary")),
    )(place, *gs, *rs, *(small or ()))


def _chip_swap_copies(p_refs, ri_refs, send, recv):
    _, _, c, _, _, others = _place()
    n = len(p_refs)
    return [_rcopy(p_refs[a].at[2 * px + py], ri_refs[a].at[j], send, recv, j * n + a, (px, py, c))
            for j, (px, py) in enumerate(others) for a in range(n)]


def _chip_swap_shapes(ps):
    return [jax.ShapeDtypeStruct((3,) + p.shape[1:], p.dtype) for p in ps]


def _chip_swap(ps, pair):
    n = len(ps)

    def body(*refs):
        start, finish = _chip_swap_plan(refs[:n], refs[n], refs[n + 1:2 * n + 1], refs[2 * n + 1], *refs[2 * n + 2:])
        start()
        finish()

    return pl.pallas_call(
        body, name="chip_swap", in_specs=[ANY] * (n + 1), out_specs=[ANY] * (n + 1),
        out_shape=_chip_swap_out_shapes(ps, pair), scratch_shapes=_chip_swap_semaphores(n),
    )(*ps, pair)


def _chip_swap_plan(p_refs, pair_ref, ri_refs, sm4_ref, send, recv, lsem):
    n = len(p_refs)
    hs = SMALL_ROWS // 2
    x, y, c, me, sib, others = _place()
    local = pltpu.make_async_copy(pair_ref, sm4_ref.at[me], lsem.at[0])
    copies = _chip_swap_copies(p_refs, ri_refs, send, recv)
    arrivals = list(copies)
    for j, (px, py) in enumerate(others):
        copies.append(_rcopy(_half(pair_ref, c, hs), _half(sm4_ref.at[me], c, hs), send, recv, 3 * n + j, (px, py, c)))
        part = _half(sm4_ref.at[2 * px + py], c, hs)
        arrivals.append(_rcopy(part, part, send, recv, 3 * n + j, (px, py, c)))

    def start():
        local.start()
        for cp in copies:
            cp.start()

    def finish():
        for arrival in arrivals:
            arrival.wait_recv()
        for cp in copies:
            cp.wait_send()
        local.wait()

    return start, finish


def _chip_swap_out_shapes(ps, pair):
    return _chip_swap_shapes(ps) + [jax.ShapeDtypeStruct((N_CHIPS,) + pair.shape, pair.dtype)]


def _chip_swap_semaphores(n):
    k = 3 * (n + 1)
    return [pltpu.SemaphoreType.DMA((k,)), pltpu.SemaphoreType.DMA((k,)), pltpu.SemaphoreType.DMA((1,))]


def _chip_sum(place, gs, rs, ris, name="chip_sum"):
    n = len(gs)
    nb = GRAD_BLOCKS

    def body(place_ref, *refs):
        g_refs, r_refs, ri_refs, o_refs = refs[:n], refs[n:2 * n], refs[2 * n:3 * n], refs[3 * n:]
        for a in range(n):
            ri = ri_refs[a]
            o_refs[a][...] = (g_refs[a][0] + r_refs[a][0]) + ri[0].astype(F32) + ri[1].astype(F32) + ri[2].astype(F32)

    in_specs, out_specs, out_shape = [], [], []
    for g in gs:
        blk = (1, g.shape[1] // 2 // nb, g.shape[2])
        in_specs.append(pl.BlockSpec(blk, lambda i, p: (p[1], p[0] * nb + i, 0)))
    for g in gs:
        blk = (1, g.shape[1] // 2 // nb, g.shape[2])
        in_specs.append(pl.BlockSpec(blk, lambda i, p: (p[1], i, 0)))
    for g in gs:
        rb = g.shape[1] // 2 // nb
        in_specs.append(pl.BlockSpec((3, rb, g.shape[2]), lambda i, p: (0, i, 0)))
        out_specs.append(pl.BlockSpec((rb, g.shape[2]), lambda i, p: (p[0] * nb + i, 0)))
        out_shape.append(jax.ShapeDtypeStruct(g.shape[1:], F32))
    return pl.pallas_call(
        body, name=name,
        grid_spec=pltpu.PrefetchScalarGridSpec(num_scalar_prefetch=1, grid=(nb,), in_specs=in_specs, out_specs=out_specs),
        out_shape=out_shape,
        compiler_params=_params(("arbitrary",)),
    )(place, *gs, *rs, *ris)


def _pair_fill_copies(g_refs, send, recv):
    _, _, c, _, sib, _ = _place()
    copies, waits = [], []
    for a, g in enumerate(g_refs):
        h = g.shape[0] // 2
        mine, theirs = _half(g, c, h), _half(g, 1 - c, h)
        copies.append(_rcopy(mine, mine, send, recv, a, sib))
        waits.append(_rcopy(theirs, theirs, send, recv, a, sib))
    return copies, waits


def _pair_fill(gfs, sm4):
    n = len(gfs)
    hs = SMALL_ROWS // 2

    def body(*refs):
        g_refs, sm4_ref = refs[n + 1:2 * n + 1], refs[2 * n + 1]
        send, recv = refs[2 * n + 2:]
        x, y, c, me, sib, others = _place()
        copies, waits = _pair_fill_copies(g_refs, send, recv)
        for j, (px, py) in enumerate(others):
            chip = 2 * px + py
            mine, theirs = _half(sm4_ref.at[chip], c, hs), _half(sm4_ref.at[chip], 1 - c, hs)
            copies.append(pltpu.make_async_remote_copy(src_ref=mine, dst_ref=mine, send_sem=send.at[n + j],
                                                       recv_sem=recv.at[n + j], device_id=sib, device_id_type=MESH))
            waits.append(pltpu.make_async_remote_copy(src_ref=theirs, dst_ref=theirs, send_sem=send.at[n + j],
                                                      recv_sem=recv.at[n + j], device_id=sib, device_id_type=MESH))
        for cp in copies:
            cp.start()
        for w in waits:
            w.wait_recv()
        for cp in copies:
            cp.wait_send()

    return pl.pallas_call(
        body, name="pair_fill", in_specs=[ANY] * (n + 1), out_specs=[ANY] * (n + 1),
        out_shape=[jax.ShapeDtypeStruct(g.shape, g.dtype) for g in gfs] + [jax.ShapeDtypeStruct(sm4.shape, sm4.dtype)],
        input_output_aliases={i: i for i in range(n + 1)},
        scratch_shapes=[pltpu.SemaphoreType.DMA((n + 3,)), pltpu.SemaphoreType.DMA((n + 3,))],
    )(*gfs, sm4)


def _adamw_math(w, g, m, v):
    m = ADAM_B1 * m + (1.0 - ADAM_B1) * g
    v = ADAM_B2 * v + (1.0 - ADAM_B2) * (g * g)
    m_hat = m / (1.0 - ADAM_B1 ** ADAM_STEP)
    v_hat = v / (1.0 - ADAM_B2 ** ADAM_STEP)
    return -ADAM_LR * (m_hat / (jnp.sqrt(v_hat) + ADAM_EPS) + ADAM_WD * w), m, v


def _adamw(items, steps, name, swap=None):
    n = len(items)
    ns = len(swap[0]) if swap else 0

    def body(*refs):
        ins, outs = refs[:4 * n], refs[4 * n + ns + 1:7 * n + ns + 1] if swap else refs[4 * n:7 * n]
        if swap:
            start, finish = _chip_swap_plan(refs[4 * n:4 * n + ns], refs[4 * n + ns], refs[7 * n + ns + 1:7 * n + 2 * ns + 1],
                                            refs[7 * n + 2 * ns + 1], *refs[7 * n + 2 * ns + 2:])
            pl.when(pl.program_id(0) == 0)(start)
        for a in range(n):
            d, mo, vo = _adamw_math(*(r[...] for r in ins[4 * a:4 * a + 4]))
            for out, val in zip(outs[3 * a:3 * a + 3], (d, mo, vo)):
                out[...] = val
        if swap:
            pl.when(pl.program_id(0) == steps - 1)(finish)

    spec = lambda w: pl.BlockSpec((w.shape[0] // steps, w.shape[1]), lambda i: (i, 0))
    extra = list(swap[0]) + [swap[1]] if swap else []
    flat = pl.pallas_call(
        body, name=name, grid=(steps,),
        in_specs=[spec(it[0]) for it in items for _ in range(4)] + [ANY] * len(extra),
        out_specs=[spec(it[0]) for it in items for _ in range(3)] + [ANY] * len(extra),
        out_shape=[jax.ShapeDtypeStruct(it[0].shape, F32) for it in items for _ in range(3)]
        + (_chip_swap_out_shapes(*swap) if swap else []),
        scratch_shapes=_chip_swap_semaphores(ns) if swap else [],
        compiler_params=_params(("arbitrary",)),
    )(*[a for it in items for a in it], *extra)
    res = [flat[3 * a:3 * a + 3] for a in range(n)]
    return (res, flat[3 * n:]) if swap else res


def _adamw_small(sm4, wmv):
    views = SMALL_VIEWS[:-1]
    n = len(views)

    def body(sm4_ref, *refs):
        g_all = ((sm4_ref[0] + sm4_ref[1]) + sm4_ref[2]) + sm4_ref[3]
        row = 0
        for a, (_, rows, cols) in enumerate(views):
            g = g_all[row:row + rows, :cols]
            row += -(-rows // ROW_TILE) * ROW_TILE
            d, mo, vo = _adamw_math(refs[3 * a][...], g, refs[3 * a + 1][...], refs[3 * a + 2][...])
            for out, val in zip(refs[3 * n + 4 * a:3 * n + 4 * a + 4], (g, d, mo, vo)):
                out[...] = val
        refs[-1][...] = g_all[row:row + 1, :128]

    flat = pl.pallas_call(
        body, name="adamw_small",
        out_shape=[jax.ShapeDtypeStruct((rows, cols), F32) for _, rows, cols in views for _ in range(4)]
        + [jax.ShapeDtypeStruct((1, 128), F32)],
        compiler_params=pltpu.CompilerParams(vmem_limit_bytes=VMEM_LIMIT),
    )(sm4, *[a for t in wmv for a in t])
    return [flat[4 * a:4 * a + 4] for a in range(n)] + [flat[-1]]


SMALL_NAMES = ("attn_pre_norm", "mla_q_norm", "mla_kv_norm", "mla_w_ukv", "mla_out_norm", "hgrn_lb_logits",
               "hgrn_out_norm", "attn_post_norm", "ffn_pre_norm", "ffn_post_norm")
BIG_NAMES = ("w_in", "mla_w_uq", "w_out", "w_gate", "w_up", "w_down")
WEIGHT_NAMES = ("attn_pre_norm", "w_in", "mla_q_norm", "mla_w_uq", "mla_kv_norm", "mla_w_ukv", "mla_out_norm",
                "hgrn_lb_logits", "hgrn_out_norm", "w_out", "attn_post_norm", "ffn_pre_norm", "w_gate", "w_up", "w_down",
                "ffn_post_norm")


UQ_COMM_SHAPE = (192, 384)


def _pack_small(vals):
    parts = []
    for name, rows, cols in SMALL_VIEWS:
        pad_rows = -(-rows // ROW_TILE) * ROW_TILE - rows
        parts.append(jnp.pad(vals[name].reshape(rows, cols), ((0, pad_rows), (0, D_MODEL - cols))))
    return jnp.concatenate(parts, axis=0)


def kernel(x, positions, attn_pre_norm, w_in, mla_q_norm, mla_w_uq, mla_kv_norm, mla_w_ukv, mla_out_norm, hgrn_lb_logits, hgrn_out_norm, w_out, attn_post_norm, ffn_pre_norm, w_gate, w_up, w_down, ffn_post_norm, loss_target, m_attn_pre_norm, m_w_in, m_mla_q_norm, m_mla_w_uq, m_mla_kv_norm, m_mla_w_ukv, m_mla_out_norm, m_hgrn_lb_logits, m_hgrn_out_norm, m_w_out, m_attn_post_norm, m_ffn_pre_norm, m_w_gate, m_w_up, m_w_down, m_ffn_post_norm, v_attn_pre_norm, v_w_in, v_mla_q_norm, v_mla_w_uq, v_mla_kv_norm, v_mla_w_ukv, v_mla_out_norm, v_hgrn_lb_logits, v_hgrn_out_norm, v_w_out, v_attn_post_norm, v_ffn_pre_norm, v_w_gate, v_w_up, v_w_down, v_ffn_post_norm):
    args = locals()
    W = {n: args[n] for n in WEIGHT_NAMES}
    M = {n: args["m_" + n] for n in WEIGHT_NAMES}
    V = {n: args["v_" + n] for n in WEIGHT_NAMES}
    T = x.shape[1]
    cx, cy, cc = lax.axis_index("x"), lax.axis_index("y"), lax.axis_index("c")

    win_rows = D_IN // N_CHIPS
    shard2d = {"w_in": (win_rows, D_MODEL), "mla_w_uq": (Q_RANK // N_CHIPS, MLA_HEADS * MLA_QK),
               "w_out": (D_MODEL // N_CHIPS, D_MODEL), "w_gate": (FF_SHARD, D_MODEL), "w_up": (FF_SHARD, D_MODEL),
               "w_down": (FF_SHARD, D_MODEL)}
    transposed = ("w_in", "w_gate", "w_up")
    to2d = lambda n, a: a[0].T if n in transposed else a.reshape(shard2d[n])
    from2d = lambda n, t: t.T[None] if n in transposed else t.reshape(W[n].shape)
    me = 2 * cx + cy
    place = jnp.stack([cc, me]).astype(jnp.int32)
    local_b = [to2d(n, W[n]).astype(BF16) for n in BIG_NAMES]
    local_b[0] = jnp.pad(local_b[0], ((0, FF_SHARD - win_rows), (0, 0)))
    stacks = _gather_chips(local_b[:2], "gather_weights")
    win4, wuq4 = [lax.dynamic_update_slice(s, l[None], (me, 0, 0)) for s, l in zip(stacks, local_b)]
    win_t = win4[:, :win_rows].reshape(D_IN, D_MODEL)
    wuq_full = wuq4.reshape(Q_RANK, MLA_HEADS, MLA_QK)
    win_arr, wq_arr, wk_arr, wv_arr = _arrange_weights(win_t, wuq_full, mla_w_ukv[0].astype(BF16))
    small = {n: W[n][0] if n == "mla_w_ukv" else W[n].reshape(-1, W[n].shape[-1]) for n in SMALL_NAMES}

    loss_local, dx, grads, ffn_final = _local_step(x[0], positions.reshape(T, 1), loss_target[0], small, win_arr,
                                                           wq_arr, wk_arr, wv_arr, local_b[2:], place)

    gs = [grads["w_in"], grads["mla_w_uq"].reshape((N_CHIPS,) + UQ_COMM_SHAPE), grads["w_out"].reshape((N_CHIPS,) + shard2d["w_out"])]
    sm = _pack_small({**grads, "loss": loss_local})
    *rs, ssib = _pair_swap(gs, sm)
    *ps, pair = _pair_sum(place, gs, rs, small=(sm, ssib))
    ffn_names, rest_names = BIG_NAMES[3:], BIG_NAMES[:3]
    g2d = dict(zip(ffn_names, ffn_final))
    adam_in = lambda names_: [(to2d(n, W[n]), g2d[n], to2d(n, M[n]), to2d(n, V[n])) for n in names_]
    updates = {}
    res, (*ris, sm4) = _adamw(adam_in(ffn_names), 8, "adamw_ffn", swap=(ps, pair))
    updates.update(zip(ffn_names, res))
    *gfin, smf = _pair_fill(_chip_sum(place, gs, rs, ris), sm4)

    g2d.update({n: gfin[k].reshape((-1,) + shard2d[n][1:]) for k, n in enumerate(rest_names)})
    updates.update(zip(rest_names[:2], _adamw(adam_in(rest_names[:2]), 3, "adamw_w_in")))
    updates.update(zip(rest_names[2:], _adamw(adam_in(rest_names[2:]), 8, "adamw_w_out")))
    G, DW, NM, NV = {}, {}, {}, {}
    for n, (d, mo, vo) in updates.items():
        G[n] = from2d(n, g2d[n][:shard2d[n][0]])
        DW[n], NM[n], NV[n] = (from2d(n, t) for t in (d, mo, vo))
    view2d = lambda n, a: a.reshape(next((r, c) for name, r, c in SMALL_VIEWS if name == n))
    *res, loss_row = _adamw_small(smf, [tuple(view2d(n, t[n]) for t in (W, M, V)) for n in SMALL_NAMES])
    for n, outs in zip(SMALL_NAMES, res):
        G[n], DW[n], NM[n], NV[n] = (t.reshape(W[n].shape) for t in outs)
    loss = loss_row[0, 0]
    return (loss, dx[None], *[G[n] for n in WEIGHT_NAMES], *[DW[n] for n in WEIGHT_NAMES],
            *[NM[n] for n in WEIGHT_NAMES], *[NV[n] for n in WEIGHT_NAMES])
```

```python
import jax
import jax.numpy as jnp
from jax import lax
from jax.experimental import pallas as pl
from jax.experimental.pallas import tpu as pltpu

F32 = jnp.float32
BF16 = jnp.bfloat16
MXU_DTYPE = BF16

D_MODEL = 1024
MLA_HEADS = 8
MLA_NOPE = 64
MLA_ROPE = 32
MLA_V = 64
MLA_QK = MLA_NOPE + MLA_ROPE
Q_RANK = 384
KV_RANK = 128
MLA_WIDTH = MLA_HEADS * MLA_V
HEAD_PAD = 128
HGRN_HEADS = 4
HGRN_DIM = 128
HGRN_WIDTH = HGRN_HEADS * HGRN_DIM
CHUNK = 64
SUB = 16
HGRN_CPI = 4
D_IN = Q_RANK + KV_RANK + MLA_ROPE + 4 * HGRN_WIDTH
D_IN_ARR = Q_RANK + KV_RANK + HEAD_PAD + 4 * HGRN_WIDTH
D_FF = 2816
N_CHIPS = 4
FF_SHARD = D_FF // N_CHIPS
EPS = 1e-6
ROPE_THETA = 10000.0
ATTN_SCALE = MLA_QK ** -0.5
ATTN_SCALE_LOG2 = ATTN_SCALE * 1.4426950408889634
NEG_BIG = -1e30

ADAM_LR = 0.001
ADAM_B1 = 0.9
ADAM_B2 = 0.999
ADAM_EPS = 1e-08
ADAM_WD = 0.01
ADAM_STEP = 10

VMEM_LIMIT = 56 * 1024 * 1024

SMALL_VIEWS = (("attn_pre_norm", 1, 1024), ("mla_q_norm", 1, 384), ("mla_kv_norm", 1, 128), ("mla_w_ukv", 128, 1024),
               ("mla_out_norm", 1, 512), ("hgrn_lb_logits", 2, 512), ("hgrn_out_norm", 1, 512),
               ("attn_post_norm", 1, 1024), ("ffn_pre_norm", 1, 1024), ("ffn_post_norm", 1, 1024), ("loss", 1, 1))
ROW_TILE = 8
SMALL_ROWS = sum(-(-rows // ROW_TILE) * ROW_TILE for _, rows, _ in SMALL_VIEWS)

MESH = pl.DeviceIdType.MESH
ANY = pl.BlockSpec(memory_space=pl.ANY)


def _dot(a, b, dims, exact):
    if exact:
        return lax.dot_general(a.astype(F32), b.astype(F32), (dims, ((), ())), precision=lax.Precision.HIGH,
                               preferred_element_type=F32)
    return lax.dot_general(a.astype(MXU_DTYPE), b.astype(MXU_DTYPE), (dims, ((), ())), preferred_element_type=F32)


def _mm(a, b, exact=False):
    return _dot(a, b, ((1,), (0,)), exact)


def _mm_nt(a, b, exact=False):
    return _dot(a, b, ((1,), (1,)), exact)


def _mm_tn(a, b, exact=False):
    return _dot(a, b, ((0,), (0,)), exact)


def _rms_fwd(x, w):
    r = lax.rsqrt(jnp.mean(x * x, axis=-1, keepdims=True) + EPS)
    xn = x * r
    return xn * w, xn, r


def _rms_bwd(dy, xn, r, w):
    dxn = dy * w
    dx = r * (dxn - xn * jnp.mean(dxn * xn, axis=-1, keepdims=True))
    dw = jnp.sum(dy * xn, axis=0, keepdims=True)
    return dx, dw


def _group_sums(v, gs):
    t, n = v.shape
    lane = lax.broadcasted_iota(jnp.int32, (t, 128), 1)
    out = []
    for p in range(n // 128):
        vb = v[:, 128 * p:128 * (p + 1)]
        if gs == 128:
            out.append(jnp.sum(vb, axis=-1, keepdims=True))
        else:
            out.append(jnp.sum(jnp.where(lane < 64, vb, 0.0), axis=-1, keepdims=True))
            out.append(jnp.sum(jnp.where(lane >= 64, vb, 0.0), axis=-1, keepdims=True))
    return out


def _group_bcast(sums, gs, t):
    lane = lax.broadcasted_iota(jnp.int32, (t, 128), 1)
    if gs == 128:
        return jnp.concatenate([jnp.broadcast_to(s, (t, 128)) for s in sums], axis=-1)
    return jnp.concatenate([jnp.where(lane < 64, sums[2 * p], sums[2 * p + 1]) for p in range(len(sums) // 2)],
                           axis=-1)


def _grms_fwd(x, w, gs):
    t = x.shape[0]
    r = lax.rsqrt(_group_bcast(_group_sums(x * x, gs), gs, t) * (1.0 / gs) + EPS)
    xn = x * r
    return xn * w, xn, r


def _grms_bwd(dy, xn, r, w, gs):
    t = dy.shape[0]
    dxn = dy * w
    dx = r * (dxn - xn * (_group_bcast(_group_sums(dxn * xn, gs), gs, t) * (1.0 / gs)))
    dw = jnp.sum(dy * xn, axis=0, keepdims=True)
    return dx, dw


def _rope_tables(c_tab, s_tab):
    lane = lax.broadcasted_iota(jnp.int32, c_tab.shape, 1)
    first = (lane >= MLA_NOPE) & (lane < MLA_NOPE + MLA_ROPE // 2)
    second = (lane >= MLA_NOPE + MLA_ROPE // 2) & (lane < MLA_QK)
    return c_tab, jnp.where(first, -s_tab, 0.0), jnp.where(second, s_tab, 0.0)


def _rope(v, c, sa, sb):
    return v * c + pltpu.roll(v, HEAD_PAD - MLA_ROPE // 2, 1) * sa + pltpu.roll(v, MLA_ROPE // 2, 1) * sb


def _rope_bwd(d, c, sa, sb):
    return d * c - pltpu.roll(d, HEAD_PAD - MLA_ROPE // 2, 1) * sa - pltpu.roll(d, MLA_ROPE // 2, 1) * sb


def _params(sem, vmem=VMEM_LIMIT):
    return pltpu.CompilerParams(dimension_semantics=sem, vmem_limit_bytes=vmem)


def _in_fwd(x, pos, invf, w_pre, win, qnw, wq, kvnw, wk, wv, tt=512):
    T = x.shape[0]

    def body(x_ref, pos_ref, invf_ref, wpre_ref, win_ref, qnw_ref, wq_ref, kvnw_ref, wk_ref, wv_ref,
             cq_ref, ckv_ref, xph_ref, q_ref, k_ref, v_ref, kt_ref, vt_ref, rc_ref, rs_ref):
        u, _, _ = _rms_fwd(x_ref[...], wpre_ref[...])
        lo = Q_RANK + KV_RANK + HEAD_PAD
        xp = _mm_nt(u, win_ref[:lo, :])
        xph_ref[...] = _mm_nt(u, win_ref[lo:, :])
        cq = xp[:, :Q_RANK]
        ckv = xp[:, Q_RANK:Q_RANK + KV_RANK]
        kr = xp[:, Q_RANK + KV_RANK:]
        cq_ref[...] = cq
        ckv_ref[...] = ckv
        ang = pos_ref[...].astype(F32) * invf_ref[...]
        c_tab = jnp.cos(ang)
        s_tab = jnp.sin(ang)
        rc_ref[...] = c_tab
        rs_ref[...] = s_tab
        c, sa, sb = _rope_tables(c_tab, s_tab)
        qn, _, _ = _rms_fwd(cq, qnw_ref[...])
        q = _mm(qn, wq_ref[...])
        kvn, _, _ = _rms_fwd(ckv, kvnw_ref[...])
        kn = _mm(kvn, wk_ref[...])
        v = _mm(kvn, wv_ref[...])
        v_ref[...] = v.astype(v_ref.dtype)
        vt_ref[...] = v.T.astype(vt_ref.dtype)
        krr = _rope(kr, c, sa, sb)
        for h in range(MLA_HEADS):
            sl = slice(HEAD_PAD * h, HEAD_PAD * (h + 1))
            q_ref[:, sl] = _rope(q[:, sl], c, sa, sb).astype(q_ref.dtype)
            kh = kn[:, sl] + krr
            k_ref[:, sl] = kh.astype(k_ref.dtype)
            kt_ref[sl, :] = kh.T.astype(kt_ref.dtype)

    row = lambda w: pl.BlockSpec((tt, w), lambda i: (i, 0))
    full = lambda a: pl.BlockSpec(a.shape, lambda i: (0,) * a.ndim)
    qk_w = MLA_HEADS * HEAD_PAD
    return pl.pallas_call(
        body, name="in_fwd", grid=(T // tt,),
        in_specs=[row(D_MODEL), row(1), full(invf), full(w_pre), full(win), full(qnw), full(wq), full(kvnw),
                  full(wk), full(wv)],
        out_specs=[row(Q_RANK), row(KV_RANK), row(4 * HGRN_WIDTH), row(qk_w), row(qk_w), row(MLA_WIDTH),
                   pl.BlockSpec((qk_w, tt), lambda i: (0, i)), pl.BlockSpec((MLA_WIDTH, tt), lambda i: (0, i)),
                   row(HEAD_PAD), row(HEAD_PAD)],
        out_shape=[jax.ShapeDtypeStruct((T, Q_RANK), F32), jax.ShapeDtypeStruct((T, KV_RANK), F32),
                   jax.ShapeDtypeStruct((T, 4 * HGRN_WIDTH), F32), jax.ShapeDtypeStruct((T, qk_w), MXU_DTYPE),
                   jax.ShapeDtypeStruct((T, qk_w), MXU_DTYPE), jax.ShapeDtypeStruct((T, MLA_WIDTH), MXU_DTYPE),
                   jax.ShapeDtypeStruct((qk_w, T), MXU_DTYPE), jax.ShapeDtypeStruct((MLA_WIDTH, T), MXU_DTYPE),
                   jax.ShapeDtypeStruct((T, HEAD_PAD), F32), jax.ShapeDtypeStruct((T, HEAD_PAD), F32)],
        compiler_params=_params(("arbitrary",)),
    )(x, pos, invf, w_pre, win, qnw, wq, kvnw, wk, wv)


def _attn_fwd_t(qb, kb, vt, gather=(), tq=256, hps=8):
    T = qb.shape[0]
    nq = T // tq
    ng = len(gather)
    steps = (MLA_HEADS // hps) * nq
    pass_on = steps - 3

    def body(q_ref, k_ref, vt_ref, *rest):
        o_ref, lse_ref = rest[ng:ng + 2]
        acc_scr = rest[2 * ng + 2]
        qi = pl.program_id(1)
        step_no = pl.program_id(0) * nq + qi
        if ng:
            gat = _Gather(rest[:ng], rest[ng + 2:2 * ng + 2], *rest[2 * ng + 3:])

            @pl.when(step_no == 0)
            def _():
                for cp in gat.sends():
                    cp.start()

            @pl.when(step_no == pass_on)
            def _():
                for arrival in gat.arrivals():
                    arrival.wait_recv()
                for cp in gat.forwards():
                    cp.start()

        heads = [slice(HEAD_PAD * a, HEAD_PAD * (a + 1)) for a in range(hps)]
        acc_scr[...] = jnp.zeros_like(acc_scr)

        def step(j, carry, masked):
            start = pl.multiple_of(j * tq, tq)
            scores = [_mm_nt(k_ref[pl.ds(start, tq), heads[a]], q_ref[:, heads[a]]) for a in range(hps)]
            new = []
            for a in range(hps):
                m, l = carry[a]
                s = scores[a] * ATTN_SCALE_LOG2
                if masked:
                    kk = lax.broadcasted_iota(jnp.int32, (tq, tq), 0)
                    qq = lax.broadcasted_iota(jnp.int32, (tq, tq), 1)
                    s = jnp.where(kk <= qq, s, NEG_BIG)
                m_new = jnp.maximum(m, jnp.max(s, axis=0, keepdims=True))
                alpha = jnp.exp2(m - m_new)
                p = jnp.exp2(s - m_new)
                l = l * alpha + jnp.sum(p, axis=0, keepdims=True)
                vtj = vt_ref[2 * MLA_V * (a // 2):2 * MLA_V * (a // 2 + 1), pl.ds(start, tq)]
                acc_scr[a] = acc_scr[a] * alpha + _mm(vtj, p)
                new.append((m_new, l))
            return tuple(new)

        init = tuple((jnp.full((1, tq), NEG_BIG, F32), jnp.zeros((1, tq), F32)) for _ in range(hps))
        carry = lax.fori_loop(0, qi, lambda j, c: step(j, c, False), init)
        carry = step(qi, carry, True)
        row = lax.broadcasted_iota(jnp.int32, (2 * MLA_V, tq), 0)
        for pr in range(hps // 2):
            (m0, l0), (m1, l1) = carry[2 * pr], carry[2 * pr + 1]
            ot = jnp.where(row < MLA_V, acc_scr[2 * pr] / l0, acc_scr[2 * pr + 1] / l1)
            o_ref[:, 2 * MLA_V * pr:2 * MLA_V * (pr + 1)] = ot.T
            lse_ref[pr, 0:1, :] = m0 + jnp.log2(l0)
            lse_ref[pr, 1:2, :] = m1 + jnp.log2(l1)

        if ng:
            @pl.when(step_no == steps - 1)
            def _():
                for arrival in gat.forward_arrivals():
                    arrival.wait_recv()
                for cp in gat.sends() + gat.forwards():
                    cp.wait_send()

    return pl.pallas_call(
        body, name="attn_fwd", grid=(MLA_HEADS // hps, nq),
        in_specs=[pl.BlockSpec((tq, hps * HEAD_PAD), lambda g, i: (i, g)),
                  pl.BlockSpec((T, hps * HEAD_PAD), lambda g, i: (0, g)),
                  pl.BlockSpec((hps * MLA_V, T), lambda g, i: (g, 0))] + [ANY] * ng,
        out_specs=[pl.BlockSpec((tq, hps * MLA_V), lambda g, i: (i, g)),
                   pl.BlockSpec((hps // 2, 2, tq), lambda g, i: (g, 0, i))] + [ANY] * ng,
        out_shape=[jax.ShapeDtypeStruct((T, MLA_WIDTH), F32), jax.ShapeDtypeStruct((MLA_HEADS // 2, 2, T), F32)]
        + _Gather.out_shapes(gather),
        scratch_shapes=[pltpu.VMEM((hps, 2 * MLA_V, tq), F32)] + (_Gather.semaphores(gather) if ng else []),
        compiler_params=_params(("arbitrary", "arbitrary")),
    )(qb, kb, vt, *gather)


def _attn_bwd_t(qb, kb, kt, vb, dob, lse, dvec, send=(), tq=256, hps=4):
    T = qb.shape[0]
    nq = T // tq
    ns = len(send)
    steps = (MLA_HEADS // hps) * nq

    def body(q_ref, k_ref, kt_ref, v_ref, do_ref, lse_ref, d_ref, *rest):
        dqt_ref, dk_ref, dv_ref = rest[ns:ns + 3]
        va_scr, dv_scr = rest[2 * ns + 3:2 * ns + 5]
        j = pl.program_id(1)
        step_no = pl.program_id(0) * nq + j
        if ns:
            @pl.when(step_no == 0)
            def _():
                for cp in _chip_swap_copies(rest[:ns], rest[ns + 3:2 * ns + 3], *rest[2 * ns + 5:]):
                    cp.start()

        @pl.when(j == 0)
        def _():
            dqt_ref[...] = jnp.zeros_like(dqt_ref)

        lane = lax.broadcasted_iota(jnp.int32, (tq, 2 * MLA_V), 1)
        heads = [slice(HEAD_PAD * a, HEAD_PAD * (a + 1)) for a in range(hps)]
        pairs = [slice(2 * MLA_V * p, 2 * MLA_V * (p + 1)) for p in range(hps // 2)]
        for pr in range(hps // 2):
            vpair = v_ref[:, pairs[pr]]
            va_scr[2 * pr] = jnp.where(lane < MLA_V, vpair, jnp.zeros_like(vpair))
            va_scr[2 * pr + 1] = jnp.where(lane >= MLA_V, vpair, jnp.zeros_like(vpair))
        dk_ref[...] = jnp.zeros_like(dk_ref)
        dv_scr[...] = jnp.zeros_like(dv_scr)

        def step(i, masked):
            start = pl.multiple_of(i * tq, tq)
            rows = pl.ds(start, tq)
            scores = [_mm_nt(k_ref[:, heads[a]], q_ref[rows, heads[a]]) for a in range(hps)]
            dps = [_mm_nt(va_scr[a], do_ref[rows, pairs[a // 2]]) for a in range(hps)]
            for a in range(hps):
                pr, r = a // 2, a % 2
                p = jnp.exp2(scores[a] * ATTN_SCALE_LOG2 - lse_ref[pr, r:r + 1, rows])
                if masked:
                    kk = lax.broadcasted_iota(jnp.int32, (tq, tq), 0)
                    qq = lax.broadcasted_iota(jnp.int32, (tq, tq), 1)
                    p = jnp.where(kk <= qq, p, 0.0)
                ds = p * (dps[a] - d_ref[pr, r:r + 1, rows]) * ATTN_SCALE
                dv_scr[a] += _mm(p, do_ref[rows, pairs[pr]])
                dk_ref[:, heads[a]] += _mm(ds, q_ref[rows, heads[a]])
                dqt_ref[heads[a], rows] += _mm(kt_ref[heads[a], :], ds)

        def loop_body(i, _):
            step(i, False)
            return 0

        step(j, True)
        lax.fori_loop(j + 1, nq, loop_body, 0)
        for pr in range(hps // 2):
            dv_ref[:, pairs[pr]] = jnp.where(lane < MLA_V, dv_scr[2 * pr], dv_scr[2 * pr + 1])

        if ns:
            @pl.when(step_no == steps - 1)
            def _():
                for cp in _chip_swap_copies(rest[:ns], rest[ns + 3:2 * ns + 3], *rest[2 * ns + 5:]):
                    cp.wait()

    stat = pl.BlockSpec((hps // 2, 2, T), lambda g, j: (g, 0, 0))
    return pl.pallas_call(
        body, name="attn_bwd", grid=(MLA_HEADS // hps, nq),
        in_specs=[pl.BlockSpec((T, hps * HEAD_PAD), lambda g, j: (0, g)),
                  pl.BlockSpec((tq, hps * HEAD_PAD), lambda g, j: (j, g)),
                  pl.BlockSpec((hps * HEAD_PAD, tq), lambda g, j: (g, j)),
                  pl.BlockSpec((tq, hps * MLA_V), lambda g, j: (j, g)),
                  pl.BlockSpec((T, hps * MLA_V), lambda g, j: (0, g)), stat, stat] + [ANY] * ns,
        out_specs=[pl.BlockSpec((hps * HEAD_PAD, T), lambda g, j: (g, 0)),
                   pl.BlockSpec((tq, hps * HEAD_PAD), lambda g, j: (j, g)),
                   pl.BlockSpec((tq, hps * MLA_V), lambda g, j: (j, g))] + [ANY] * ns,
        out_shape=[jax.ShapeDtypeStruct((MLA_HEADS * HEAD_PAD, T), F32),
                   jax.ShapeDtypeStruct((T, MLA_HEADS * HEAD_PAD), F32),
                   jax.ShapeDtypeStruct((T, MLA_WIDTH), F32)] + _chip_swap_shapes(send),
        scratch_shapes=[pltpu.VMEM((hps, tq, 2 * MLA_V), vb.dtype), pltpu.VMEM((hps, tq, 2 * MLA_V), F32)]
        + ([pltpu.SemaphoreType.DMA((3 * ns,)), pltpu.SemaphoreType.DMA((3 * ns,))] if ns else []),
        compiler_params=_params(("arbitrary", "arbitrary")),
    )(qb, kb, kt, vb, dob, lse, dvec, *send)


def _cumsum_rows(x):
    n = x.shape[0]
    row = lax.broadcasted_iota(jnp.int32, x.shape, 0)
    s = 1
    while s < n:
        x = x + jnp.where(row >= s, pltpu.roll(x, s, 0), 0.0)
        s *= 2
    return x


def _rev_cumsum_rows(x):
    n = x.shape[0]
    row = lax.broadcasted_iota(jnp.int32, x.shape, 0)
    s = 1
    while s < n:
        x = x + jnp.where(row < n - s, pltpu.roll(x, n - s, 0), 0.0)
        s *= 2
    return x


def _lb_from_logits(l):
    l0, l1 = l[0:1, :], l[1:2, :]
    m = jnp.maximum(l0, l1)
    e0, e1 = jnp.exp(l0 - m), jnp.exp(l1 - m)
    return e0 / (e0 + e1)


def _hgrn_gates(hq, hf, lb):
    sig_f = jax.nn.sigmoid(hf)
    f = lb + (1.0 - lb) * sig_f
    sig_q = jax.nn.sigmoid(hq)
    return sig_f, f, jnp.log(f), 1.0 - f, sig_q, hq * sig_q


def _hgrn_intra(q, kk, b, exact=False):
    row = lax.broadcasted_iota(jnp.int32, b.shape, 0)
    qs, ks, eqs, eks, a_rows = [], [], [], [], []
    for i in range(CHUNK // SUB):
        ref = b[SUB * i + SUB // 2:SUB * i + SUB // 2 + 1, :]
        eq = jnp.exp(b[SUB * i:SUB * (i + 1), :] - ref)
        ek = jnp.exp(jnp.where(row < SUB * (i + 1), ref - b, NEG_BIG))
        qi = q[SUB * i:SUB * (i + 1), :] * eq
        ki = kk * ek
        a_rows.append(_mm_nt(qi, ki, exact))
        qs.append(qi), ks.append(ki), eqs.append(eq), eks.append(ek)
    tt = lax.broadcasted_iota(jnp.int32, (CHUNK, CHUNK), 0)
    ss = lax.broadcasted_iota(jnp.int32, (CHUNK, CHUNK), 1)
    causal = ss <= tt
    a = jnp.where(causal, jnp.concatenate(a_rows, axis=0), 0.0)
    return a, causal, qs, ks, eqs, eks


def _hgrn_fwd(xph, lbl, tg=512):
    T = xph.shape[0]
    ng, ncg = T // tg, tg // CHUNK
    cols = [slice(HGRN_DIM * h, HGRN_DIM * (h + 1)) for h in range(HGRN_HEADS)]

    def body(lbl_ref, hq_ref, hf_ref, hi_ref, o_ref, st_ref, s_scr):
        @pl.when(pl.program_id(0) == 0)
        def _():
            s_scr[...] = jnp.zeros_like(s_scr)

        lb = _lb_from_logits(lbl_ref[...])

        def chunks(it, _):
            pre = []
            for k in range(HGRN_CPI):
                c = it * HGRN_CPI + k
                rows = pl.ds(pl.multiple_of(c * CHUNK, CHUNK), CHUNK)
                for cs in cols:
                    _, _, lf, kk, _, q = _hgrn_gates(hq_ref[rows, cs], hf_ref[rows, cs], lb[:, cs])
                    v = hi_ref[rows, cs]
                    b = _cumsum_rows(lf)
                    a = _hgrn_intra(q, kk, b)[0]
                    b_last = b[CHUNK - 1:CHUNK, :]
                    pre.append((c, rows, q * jnp.exp(b), a, v, jnp.exp(b_last), _mm_tn(v, kk * jnp.exp(b_last - b))))
            for i, (c, rows, qe, a, v, ebl, upd) in enumerate(pre):
                h = i % HGRN_HEADS
                st = s_scr[h]
                st_ref[h, c] = st
                o_ref[rows, cols[h]] = _mm_nt(qe, st) + _mm(a, v)
                s_scr[h] = st * ebl + upd
            return 0

        lax.fori_loop(0, ncg // HGRN_CPI, chunks, 0)

    col = lambda k: pl.BlockSpec((tg, HGRN_WIDTH), lambda g: (g, k))
    return pl.pallas_call(
        body, name="hgrn_fwd", grid=(ng,),
        in_specs=[pl.BlockSpec((2, HGRN_WIDTH), lambda g: (0, 0)), col(0), col(1), col(2)],
        out_specs=[col(0), pl.BlockSpec((HGRN_HEADS, ncg, HGRN_DIM, HGRN_DIM), lambda g: (0, g, 0, 0))],
        out_shape=[jax.ShapeDtypeStruct((T, HGRN_WIDTH), F32),
                   jax.ShapeDtypeStruct((HGRN_HEADS, T // CHUNK, HGRN_DIM, HGRN_DIM), F32)],
        scratch_shapes=[pltpu.VMEM((HGRN_HEADS, HGRN_DIM, HGRN_DIM), F32)],
        compiler_params=_params(("arbitrary",)),
    )(lbl, xph, xph, xph)


def _hgrn_bwd(xph, lbl, states, d_o, fill=(), tg=512):
    T = xph.shape[0]
    ng, ncg = T // tg, tg // CHUNK
    cols = [slice(HGRN_DIM * h, HGRN_DIM * (h + 1)) for h in range(HGRN_HEADS)]
    nsub = CHUNK // SUB
    nf = len(fill)

    def body(lbl_ref, hq_ref, hf_ref, hi_ref, st_ref, do_ref, *rest):
        dhq_ref, dhf_ref, dhi_ref, dlg_ref = rest[nf:nf + 4]
        ds_scr, dlb_scr = rest[2 * nf + 4:2 * nf + 6]
        fill_copies = lambda: _pair_fill_copies(rest[nf + 4:2 * nf + 4], *rest[2 * nf + 6:])
        g = pl.program_id(0)

        @pl.when(g == 0)
        def _():
            ds_scr[...] = jnp.zeros_like(ds_scr)
            dlb_scr[...] = jnp.zeros_like(dlb_scr)
            for cp in (fill_copies()[0] if nf else ()):
                cp.start()

        lb = _lb_from_logits(lbl_ref[...])

        def chunks(it, _):
            pre = []
            for k, h in ((k, h) for k in range(HGRN_CPI) for h in range(HGRN_HEADS)):
                cs = cols[h]
                c = ncg - 1 - (it * HGRN_CPI + k)
                rows = pl.ds(pl.multiple_of(c * CHUNK, CHUNK), CHUNK)
                hq = hq_ref[rows, cs]
                sig_f, f, lf, kk, sig_q, q = _hgrn_gates(hq, hf_ref[rows, cs], lb[:, cs])
                v = hi_ref[rows, cs]
                do = do_ref[rows, cs]
                b = _cumsum_rows(lf)
                eb = jnp.exp(b)
                a, causal, qs, ks, eqs, eks = _hgrn_intra(q, kk, b)
                b_last = b[CHUNK - 1:CHUNK, :]
                st = st_ref[h, c]
                pre.append(dict(h=h, cs=cs, rows=rows, hq=hq, sig_f=sig_f, f=f, kk=kk, sig_q=sig_q, q=q, v=v, eb=eb, qs=qs,
                                ks=ks, eqs=eqs,
                                eks=eks, ebl=jnp.exp(b_last), el=jnp.exp(b_last - b), st=st,
                                da=jnp.where(causal, _mm_nt(do, v, True), 0.0), dq=_mm(do, st, True) * eb,
                                dv=_mm_tn(a, do), dsu=_mm_tn(do, q * eb, True)))
            for w in pre:
                dq_rows = []
                dk = jnp.zeros_like(w["q"])
                for i in range(nsub):
                    dai = w["da"][SUB * i:SUB * (i + 1), :]
                    dq_rows.append(_mm(dai, w["ks"][i], True) * w["eqs"][i])
                    dk = dk + _mm_tn(dai, w["qs"][i], True) * w["eks"][i]
                w["dq"] = w["dq"] + jnp.concatenate(dq_rows, axis=0)
                w["dk"] = dk
            for w in pre:
                h, cs, rows = w["h"], w["cs"], w["rows"]
                kk, el, ebl, dst = w["kk"], w["el"], w["ebl"], ds_scr[h]
                dk_state = _mm(w["v"], dst, True) * el
                dk = w["dk"] + dk_state
                e_last = (ebl * jnp.sum(w["st"] * dst, axis=0, keepdims=True)
                          + jnp.sum(kk * dk_state, axis=0, keepdims=True))
                dlf = _rev_cumsum_rows(w["q"] * w["dq"] - kk * dk) + e_last
                ds_scr[h] = dst * ebl + w["dsu"]
                df = dlf / w["f"] - dk
                sig_f, sig_q = w["sig_f"], w["sig_q"]
                dhf_ref[rows, cs] = df * (1.0 - lb[:, cs]) * sig_f * (1.0 - sig_f)
                dlb_scr[:, cs] += jnp.sum(df * (1.0 - sig_f), axis=0, keepdims=True)
                dhq_ref[rows, cs] = w["dq"] * sig_q * (1.0 + w["hq"] * (1.0 - sig_q))
                dhi_ref[rows, cs] = w["dv"] + _mm_nt(kk * el, dst)
            return 0

        lax.fori_loop(0, ncg // HGRN_CPI, chunks, 0)

        @pl.when(g == ng - 1)
        def _():
            dl0 = dlb_scr[...] * lb * (1.0 - lb)
            dlg_ref[...] = jnp.concatenate([dl0, -dl0], axis=0)
            if nf:
                copies, waits = fill_copies()
                for w in waits:
                    w.wait_recv()
                for cp in copies:
                    cp.wait_send()

    col = lambda k: pl.BlockSpec((tg, HGRN_WIDTH), lambda g: (ng - 1 - g, k))
    logits = pl.BlockSpec((2, HGRN_WIDTH), lambda g: (0, 0))
    big = jax.ShapeDtypeStruct((T, HGRN_WIDTH), F32)
    n_in, n_out = 6, 4
    return pl.pallas_call(
        body, name="hgrn_bwd", grid=(ng,),
        in_specs=[logits, col(0), col(1), col(2),
                  pl.BlockSpec((HGRN_HEADS, ncg, HGRN_DIM, HGRN_DIM), lambda g: (0, ng - 1 - g, 0, 0)), col(0)] + [ANY] * nf,
        out_specs=[col(0), col(0), col(0), logits] + [ANY] * nf,
        out_shape=[big, big, big, jax.ShapeDtypeStruct((2, HGRN_WIDTH), F32)]
        + [jax.ShapeDtypeStruct(f.shape, f.dtype) for f in fill],
        input_output_aliases={n_in + k: n_out + k for k in range(nf)},
        scratch_shapes=[pltpu.VMEM((HGRN_HEADS, HGRN_DIM, HGRN_DIM), F32), pltpu.VMEM((1, HGRN_WIDTH), F32)]
        + ([pltpu.SemaphoreType.DMA((nf,)), pltpu.SemaphoreType.DMA((nf,))] if nf else []),
        compiler_params=_params(("arbitrary",)),
    )(lbl, xph, xph, xph, states, d_o, *fill)


def _proj_fwd(x, o_raw, oh_raw, xph, wout, w_mla, w_hg, w_post, w_fpre, tt=512):
    T = x.shape[0]

    def body(x_ref, o_ref, oh_ref, hg_ref, wout_ref, wmla_ref, whg_ref, wpost_ref, wfpre_ref,
             h1_ref, y1_ref, z_ref, mix_ref):
        om, _, _ = _grms_fwd(o_ref[...], wmla_ref[...], MLA_V)
        hg = hg_ref[...]
        ohn, _, _ = _grms_fwd(oh_ref[...], whg_ref[...], HGRN_DIM)
        mix = jnp.concatenate([om, ohn * (hg * jax.nn.sigmoid(hg))], axis=-1)
        mix_ref[...] = mix.astype(mix_ref.dtype)
        y1 = _mm(mix, wout_ref[...])
        y1_ref[...] = y1
        h1 = x_ref[...] + _rms_fwd(y1, wpost_ref[...])[0]
        h1_ref[...] = h1
        z_ref[...] = _rms_fwd(h1, wfpre_ref[...])[0].astype(z_ref.dtype)

    row = lambda w: pl.BlockSpec((tt, w), lambda i: (i, 0))
    full = lambda a: pl.BlockSpec(a.shape, lambda i: (0,) * a.ndim)
    sds = jax.ShapeDtypeStruct
    return pl.pallas_call(
        body, name="proj_fwd", grid=(T // tt,),
        in_specs=[row(D_MODEL), row(MLA_WIDTH), row(HGRN_WIDTH), pl.BlockSpec((tt, HGRN_WIDTH), lambda i: (i, 3)),
                  full(wout), full(w_mla), full(w_hg), full(w_post), full(w_fpre)],
        out_specs=[row(D_MODEL)] * 4,
        out_shape=[sds((T, D_MODEL), F32), sds((T, D_MODEL), F32), sds((T, D_MODEL), MXU_DTYPE),
                   sds((T, D_MODEL), MXU_DTYPE)],
        compiler_params=_params(("arbitrary",)),
    )(x, o_raw, oh_raw, xph, wout, w_mla, w_hg, w_post, w_fpre)


def _ffn_fwd(zb, h1, tgt, w_fpost, wg, wu, wd, tt=256):
    T = zb.shape[0]
    nj = N_CHIPS

    def body(z_ref, h1_ref, tgt_ref, wfpost_ref, wg_ref, wu_ref, wd_ref, g_ref, up_ref, dy2_ref, dh2_ref, loss_ref, dwf_ref):
        @pl.when(pl.program_id(0) == 0)
        def _():
            loss_ref[...] = jnp.zeros_like(loss_ref)
            dwf_ref[...] = jnp.zeros_like(dwf_ref)

        z = z_ref[...]
        gs = [_mm_nt(z, wg_ref[j]) for j in range(nj)]
        ups = [_mm_nt(z, wu_ref[j]) for j in range(nj)]
        y2 = jnp.zeros((tt, D_MODEL), F32)
        for j in range(nj):
            g_ref[j] = gs[j]
            up_ref[j] = ups[j]
            y2 = y2 + _mm(gs[j] * jax.nn.sigmoid(gs[j]) * ups[j], wd_ref[j])
        w = wfpost_ref[...]
        y2s, y2n, r2 = _rms_fwd(y2, w)
        e = h1_ref[...] + y2s - tgt_ref[...]
        loss_ref[...] += jnp.sum(e * e, axis=0, keepdims=True)
        dh2 = e * (1.0 / D_MODEL)
        dh2_ref[...] = dh2
        dy2, dwf = _rms_bwd(dh2, y2n, r2, w)
        dy2_ref[...] = dy2.astype(dy2_ref.dtype)
        dwf_ref[...] += dwf

    row = pl.BlockSpec((tt, D_MODEL), lambda i: (i, 0))
    vec = pl.BlockSpec((1, D_MODEL), lambda i: (0, 0))
    resident = pl.BlockSpec((nj, FF_SHARD, D_MODEL), lambda i: (0, 0, 0), pipeline_mode=pl.Buffered(1))
    act = pl.BlockSpec((nj, tt, FF_SHARD), lambda i: (0, i, 0))
    sds = jax.ShapeDtypeStruct
    return pl.pallas_call(
        body, name="ffn_fwd", grid=(T // tt,),
        in_specs=[row, row, row, vec, resident, resident, resident],
        out_specs=[act, act, row, row, vec, vec],
        out_shape=[sds((nj, T, FF_SHARD), F32), sds((nj, T, FF_SHARD), F32), sds((T, D_MODEL), MXU_DTYPE),
                   sds((T, D_MODEL), F32), sds((1, D_MODEL), F32), sds((1, D_MODEL), F32)],
        compiler_params=_params(("arbitrary",)),
    )(zb, h1, tgt, w_fpost, wg, wu, wd)


def _ffn_bwd(zb, g, up, dy2b, wg, wu, wd, tt=512):
    T = zb.shape[0]
    nj = N_CHIPS

    def body(z_ref, g_ref, up_ref, dy2_ref, wg_ref, wu_ref, wd_ref, dwg_ref, dwu_ref, dwd_ref, dz_ref):
        @pl.when(pl.program_id(1) == 0)
        def _():
            dwg_ref[...] = jnp.zeros_like(dwg_ref)
            dwu_ref[...] = jnp.zeros_like(dwu_ref)
            dwd_ref[...] = jnp.zeros_like(dwd_ref)

        z, g_, up_, dy2 = z_ref[...], g_ref[0], up_ref[0], dy2_ref[...]
        sg = jax.nn.sigmoid(g_)
        act = g_ * sg
        dff = _mm_nt(dy2, wd_ref[0])
        dwd_ref[0] += _mm_tn(act * up_, dy2)
        dg = dff * up_ * sg * (1.0 + g_ * (1.0 - sg))
        dup = dff * act
        dwg_ref[0] += _mm_tn(dg, z)
        dwu_ref[0] += _mm_tn(dup, z)
        dz_ref[0] = _mm(dg, wg_ref[0]) + _mm(dup, wu_ref[0])

    row = pl.BlockSpec((tt, D_MODEL), lambda j, i: (i, 0))
    act = pl.BlockSpec((1, tt, FF_SHARD), lambda j, i: (j, i, 0))
    w_sh = pl.BlockSpec((1, FF_SHARD, D_MODEL), lambda j, i: (j, 0, 0))
    w_grad = jax.ShapeDtypeStruct((nj, FF_SHARD, D_MODEL), F32)
    return pl.pallas_call(
        body, name="ffn_bwd", grid=(nj, T // tt),
        in_specs=[row, act, act, row, w_sh, w_sh, w_sh],
        out_specs=[w_sh, w_sh, w_sh, pl.BlockSpec((1, tt, D_MODEL), lambda j, i: (j, i, 0))],
        out_shape=[w_grad, w_grad, w_grad, jax.ShapeDtypeStruct((nj, T, D_MODEL), F32)],
        compiler_params=_params(("arbitrary", "arbitrary")),
    )(zb, g, up, dy2b, wg, wu, wd)


def _mid_bwd(dzp, dh2, h1, y1, mixb, o_raw, oh_raw, xph, wout, w_fpre, w_post, w_mla, w_hg, swap=(), tt=256):
    T = dh2.shape[0]
    nsw = len(swap)
    n_in, n_out = 13, 10

    def body(*refs):
        (dzp_ref, dh2_ref, h1_ref, y1_ref, mix_ref, o_ref, oh_ref, hg_ref, wout_ref, wfpre_ref, wpost_ref,
         wmla_ref, whg_ref) = refs[:n_in]
        (dh1_ref, dwout_ref, do_ref, doh_ref, dhg_ref, dvec_ref, dwfpre_ref, dwpost_ref, dwmla_ref,
         dwhg_ref) = refs[n_in + nsw:n_in + nsw + n_out]
        swap_copies = lambda: _pair_swap_copies(refs[n_in:n_in + nsw], refs[n_in + nsw + n_out:n_in + 2 * nsw + n_out],
                                                *refs[n_in + 2 * nsw + n_out:])

        @pl.when(pl.program_id(0) == 0)
        def _():
            for r in (dwout_ref, dwfpre_ref, dwpost_ref, dwmla_ref, dwhg_ref):
                r[...] = jnp.zeros_like(r)
            for cp in (swap_copies() if nsw else ()):
                cp.start()

        dz = dzp_ref[0] + dzp_ref[1] + dzp_ref[2] + dzp_ref[3]
        wfpre = wfpre_ref[...]
        _, h1n, r = _rms_fwd(h1_ref[...], wfpre)
        dh1_z, dwfpre = _rms_bwd(dz, h1n, r, wfpre)
        dwfpre_ref[...] += dwfpre
        dh1 = dh2_ref[...] + dh1_z
        dh1_ref[...] = dh1
        wpost = wpost_ref[...]
        _, y1n, r1 = _rms_fwd(y1_ref[...], wpost)
        dy1, dwpost = _rms_bwd(dh1, y1n, r1, wpost)
        dwpost_ref[...] += dwpost
        dmix = _mm_nt(dy1, wout_ref[...])
        dwout_ref[...] += _mm_tn(mix_ref[...], dy1)
        wmla = wmla_ref[...]
        o = o_ref[...]
        _, on, ro = _grms_fwd(o, wmla, MLA_V)
        d_o, dwmla = _grms_bwd(dmix[:, :MLA_WIDTH], on, ro, wmla, MLA_V)
        dwmla_ref[...] += dwmla
        do_ref[...] = d_o.astype(do_ref.dtype)
        hh = lax.broadcasted_iota(jnp.int32, (MLA_HEADS, MLA_WIDTH), 0)
        ll = lax.broadcasted_iota(jnp.int32, (MLA_HEADS, MLA_WIDTH), 1)
        sel = jnp.where((ll >= hh * MLA_V) & (ll < (hh + 1) * MLA_V), 1.0, 0.0)
        dvec_ref[...] = _mm_nt(sel, d_o * o, True)
        whg = whg_ref[...]
        hg = hg_ref[...]
        sg = jax.nn.sigmoid(hg)
        _, ohn, rh = _grms_fwd(oh_ref[...], whg, HGRN_DIM)
        dmh = dmix[:, MLA_WIDTH:]
        dhg_ref[...] = dmh * ohn * whg * sg * (1.0 + hg * (1.0 - sg))
        d_oh, dwhg = _grms_bwd(dmh * (hg * sg), ohn, rh, whg, HGRN_DIM)
        dwhg_ref[...] += dwhg
        doh_ref[...] = d_oh

        if nsw:
            @pl.when(pl.program_id(0) == T // tt - 1)
            def _():
                for cp in swap_copies():
                    cp.wait()

    row = lambda w: pl.BlockSpec((tt, w), lambda i: (i, 0))
    full = lambda a: pl.BlockSpec(a.shape, lambda i: (0,) * a.ndim)
    vec = lambda w: pl.BlockSpec((1, w), lambda i: (0, 0))
    sds = jax.ShapeDtypeStruct
    return pl.pallas_call(
        body, name="mid_bwd", grid=(T // tt,),
        in_specs=[pl.BlockSpec((N_CHIPS, tt, D_MODEL), lambda i: (0, i, 0)), row(D_MODEL), row(D_MODEL), row(D_MODEL),
                  row(D_MODEL), row(MLA_WIDTH), row(HGRN_WIDTH), pl.BlockSpec((tt, HGRN_WIDTH), lambda i: (i, 3)),
                  full(wout), vec(D_MODEL), vec(D_MODEL), vec(MLA_WIDTH), vec(HGRN_WIDTH)] + [ANY] * nsw,
        out_specs=[row(D_MODEL), full(wout), row(MLA_WIDTH), row(HGRN_WIDTH), row(HGRN_WIDTH),
                   pl.BlockSpec((MLA_HEADS, tt), lambda i: (0, i)),
                   vec(D_MODEL), vec(D_MODEL), vec(MLA_WIDTH), vec(HGRN_WIDTH)] + [ANY] * nsw,
        out_shape=[sds((T, D_MODEL), F32), sds(wout.shape, F32), sds((T, MLA_WIDTH), MXU_DTYPE), sds((T, HGRN_WIDTH), F32),
                   sds((T, HGRN_WIDTH), F32), sds((MLA_HEADS, T), F32),
                   sds((1, D_MODEL), F32), sds((1, D_MODEL), F32), sds((1, MLA_WIDTH), F32), sds((1, HGRN_WIDTH), F32)]
        + _half_stack_shapes(swap),
        scratch_shapes=[pltpu.SemaphoreType.DMA((nsw,)), pltpu.SemaphoreType.DMA((nsw,))] if nsw else [],
        compiler_params=_params(("arbitrary",)),
    )(dzp, dh2, h1, y1, mixb, o_raw, oh_raw, xph, wout, w_fpre, w_post, w_mla, w_hg, *swap)


def _in_bwd(x, dh1, cq, ckv, dq, dk, dv, dhq, dhf, dhi, dhg, rc, rs, w_pre, win, qnw, wq, kvnw, wk, wv, tt=256):
    T = x.shape[0]

    def body(x_ref, dh1_ref, cq_ref, ckv_ref, dq_ref, dk_ref, dv_ref, dhq_ref, dhf_ref, dhi_ref, dhg_ref, rc_ref, rs_ref,
             wpre_ref, win_ref, qnw_ref, wq_ref, kvnw_ref, wk_ref, wv_ref,
             dx_ref, dwin_ref, dwq_ref, dwk_ref, dwv_ref, dwpre_ref, dqnw_ref, dkvnw_ref):
        @pl.when(pl.program_id(0) == 0)
        def _():
            for r in (dwin_ref, dwq_ref, dwk_ref, dwv_ref, dwpre_ref, dqnw_ref, dkvnw_ref):
                r[...] = jnp.zeros_like(r)

        def add_win_grad(r, first):
            for arr0, n, chip, row0 in _win_grad_segments():
                if first <= arr0 and arr0 + n <= first + r.shape[0]:
                    dwin_ref[chip, row0:row0 + n, :] += r[arr0 - first:arr0 - first + n]

        lo = Q_RANK + KV_RANK + HEAD_PAD
        wpre = wpre_ref[...]
        u, xn, rx = _rms_fwd(x_ref[...], wpre)
        dxp_h = jnp.concatenate([dhq_ref[...], dhf_ref[...], dhi_ref[...], dhg_ref[...]], axis=-1)
        add_win_grad(_mm_tn(dxp_h, u), lo)
        du = _mm(dxp_h, win_ref[lo:, :])
        c, sa, sb = _rope_tables(rc_ref[...], rs_ref[...])
        lane = lax.broadcasted_iota(jnp.int32, (tt, HEAD_PAD), 1)
        dk_all = dk_ref[...]
        dq_lin = []
        dkr = jnp.zeros((tt, HEAD_PAD), F32)
        for h in range(MLA_HEADS):
            sl = slice(HEAD_PAD * h, HEAD_PAD * (h + 1))
            dq_lin.append(_rope_bwd(dq_ref[sl, :].T, c, sa, sb))
            dkr = dkr + dk_all[:, sl]
        dq_lin = jnp.concatenate(dq_lin, axis=-1)
        dkr = jnp.where((lane >= MLA_NOPE) & (lane < MLA_QK), _rope_bwd(dkr, c, sa, sb), 0.0)
        qnw = qnw_ref[...]
        qn, cqn, rq = _rms_fwd(cq_ref[...], qnw)
        dwq_ref[...] += _mm_tn(qn, dq_lin)
        dcq, dqnw = _rms_bwd(_mm_nt(dq_lin, wq_ref[...]), cqn, rq, qnw)
        dqnw_ref[...] += dqnw
        kvnw = kvnw_ref[...]
        kvn, ckvn, rkv = _rms_fwd(ckv_ref[...], kvnw)
        dv_ = dv_ref[...]
        dwk_ref[...] += _mm_tn(kvn, dk_all)
        dwv_ref[...] += _mm_tn(kvn, dv_)
        dckv, dkvnw = _rms_bwd(_mm_nt(dk_all, wk_ref[...]) + _mm_nt(dv_, wv_ref[...]), ckvn, rkv, kvnw)
        dkvnw_ref[...] += dkvnw
        dxp_a = jnp.concatenate([dcq, dckv, dkr], axis=-1)
        add_win_grad(_mm_tn(dxp_a, u), 0)
        dx_u, dwpre = _rms_bwd(du + _mm(dxp_a, win_ref[:lo, :]), xn, rx, wpre)
        dwpre_ref[...] += dwpre
        dx_ref[...] = dh1_ref[...] + dx_u

    row = lambda w: pl.BlockSpec((tt, w), lambda i: (i, 0))
    full = lambda a: pl.BlockSpec(a.shape, lambda i: (0,) * a.ndim)
    sds = jax.ShapeDtypeStruct
    qk_w = MLA_HEADS * HEAD_PAD
    return pl.pallas_call(
        body, name="in_bwd", grid=(T // tt,),
        in_specs=[row(D_MODEL), row(D_MODEL), row(Q_RANK), row(KV_RANK), pl.BlockSpec((qk_w, tt), lambda i: (0, i)),
                  row(qk_w), row(MLA_WIDTH),
                  row(HGRN_WIDTH), row(HGRN_WIDTH), row(HGRN_WIDTH), row(HGRN_WIDTH), row(HEAD_PAD), row(HEAD_PAD),
                  full(w_pre), full(win), full(qnw), full(wq), full(kvnw), full(wk), full(wv)],
        out_specs=[row(D_MODEL), pl.BlockSpec(WIN_COMM_SHAPE, lambda i: (0, 0, 0)), full(wq), full(wk), full(wv),
                   full(w_pre), full(qnw), full(kvnw)],
        out_shape=[sds((T, D_MODEL), F32), sds(WIN_COMM_SHAPE, F32), sds(wq.shape, F32), sds(wk.shape, F32),
                   sds(wv.shape, F32), sds(w_pre.shape, F32), sds(qnw.shape, F32), sds(kvnw.shape, F32)],
        compiler_params=_params(("arbitrary",)),
    )(x, dh1, cq, ckv, dq, dk, dv, dhq, dhf, dhi, dhg, rc, rs, w_pre, win, qnw, wq, kvnw, wk, wv)


def _arrange_weights(win_t, wuq_full, wukv):
    dt = win_t.dtype
    z = lambda n: jnp.zeros((n, D_MODEL), dt)
    s2 = Q_RANK + KV_RANK
    win_arr = jnp.concatenate([win_t[:s2], z(MLA_NOPE), win_t[s2:s2 + MLA_ROPE], z(HEAD_PAD - MLA_QK),
                               win_t[s2 + MLA_ROPE:]], axis=0)
    wq_arr = jnp.pad(wuq_full, ((0, 0), (0, 0), (0, HEAD_PAD - MLA_QK))).reshape(Q_RANK, MLA_HEADS * HEAD_PAD)
    wk_arr = jnp.pad(wukv[:, :, :MLA_NOPE], ((0, 0), (0, 0), (0, HEAD_PAD - MLA_NOPE))).reshape(
        KV_RANK, MLA_HEADS * HEAD_PAD)
    wv_arr = wukv[:, :, MLA_NOPE:].reshape(KV_RANK, MLA_WIDTH)
    return win_arr, wq_arr, wk_arr, wv_arr


WIN_COMM_SHAPE = (N_CHIPS, FF_SHARD, D_MODEL)


def _win_grad_segments():
    s2 = Q_RANK + KV_RANK
    runs = [(0, s2, 0), (s2, s2 + MLA_ROPE, MLA_NOPE), (s2 + MLA_ROPE, D_IN, HEAD_PAD - MLA_ROPE)]
    per = D_IN // N_CHIPS
    segs = []
    for lo, hi, shift in runs:
        for k in range(N_CHIPS):
            a, b = max(lo, per * k), min(hi, per * (k + 1))
            if a < b:
                segs.append((a + shift, b - a, k, a - per * k))
    return segs


def _unarrange_grads(dwq_arr, dwk_arr, dwv_arr):
    dwuq = dwq_arr.reshape(Q_RANK, MLA_HEADS, HEAD_PAD)[:, :, :MLA_QK]
    dwukv = jnp.concatenate([dwk_arr.reshape(KV_RANK, MLA_HEADS, HEAD_PAD)[:, :, :MLA_NOPE],
                             dwv_arr.reshape(KV_RANK, MLA_HEADS, MLA_V)], axis=-1)
    return dwuq, dwukv


def _rope_inv_freq():
    inv = 1.0 / (ROPE_THETA ** (jnp.arange(0, MLA_ROPE, 2, dtype=F32) / MLA_ROPE))
    z = lambda n: jnp.zeros((n,), F32)
    return jnp.concatenate([z(MLA_NOPE), inv, inv, z(HEAD_PAD - MLA_QK)]).reshape(1, HEAD_PAD)


def _local_step(x, pos, tgt, small, win_arr, wq_arr, wk_arr, wv_arr, late, place=None):
    invf = _rope_inv_freq()
    cq, ckv, xph, qb, kb, vb, kt, vt, rc, rs = _in_fwd(x, pos, invf, small["attn_pre_norm"], win_arr, small["mla_q_norm"],
                                               wq_arr, small["mla_kv_norm"], wk_arr, wv_arr)
    if place is None:
        o_raw, lse = _attn_fwd_t(qb, kb, vt)
        wout, wg, wu, wd = late
    else:
        o_raw, lse, *stacks = _attn_fwd_t(qb, kb, vt, gather=late)
        wout, wg, wu, wd = [lax.dynamic_update_slice(s, l[None], (place[1], 0, 0)) for s, l in zip(stacks, late)]
        wout = wout.reshape(D_MODEL, D_MODEL)
    oh_raw, states = _hgrn_fwd(xph, small["hgrn_lb_logits"])
    h1, y1, zb, mixb = _proj_fwd(x, o_raw, oh_raw, xph, wout, small["mla_out_norm"], small["hgrn_out_norm"],
                                 small["attn_post_norm"], small["ffn_pre_norm"])
    g, up, dy2b, dh2, loss_acc, d_fpost = _ffn_fwd(zb, h1, tgt, small["ffn_post_norm"], wg, wu, wd)
    dwg, dwu, dwd, dzp = _ffn_bwd(zb, g, up, dy2b, wg, wu, wd)
    ffn_grads = [] if place is None else [dwg, dwu, dwd]
    dh1, dwout, d_o, d_oh, dhg, dvec, d_fpre, d_post, d_mla, d_hg, *ffn_rs = _mid_bwd(
        dzp, dh2, h1, y1, mixb, o_raw, oh_raw, xph, wout, small["ffn_pre_norm"], small["attn_post_norm"],
        small["mla_out_norm"], small["hgrn_out_norm"], swap=ffn_grads)
    ffn_ps = _pair_sum(place, ffn_grads, ffn_rs, name="pair_sum_ffn") if ffn_grads else []
    dq, dk, dv, *ffn_ris = _attn_bwd_t(qb, kb, kt, vb, d_o, lse, dvec.reshape(lse.shape), send=ffn_ps)
    ffn_sums = _chip_sum(place, ffn_grads, ffn_rs, ffn_ris, name="chip_sum_ffn") if ffn_grads else []
    dhq, dhf, dhi, d_lbl, *ffn_final = _hgrn_bwd(xph, small["hgrn_lb_logits"], states, d_oh, fill=ffn_sums)
    dx, dwin4, dwq_arr, dwk_arr, dwv_arr, d_pre, d_qn, d_kvn = _in_bwd(
        x, dh1, cq, ckv, dq, dk, dv, dhq, dhf, dhi, dhg, rc, rs, small["attn_pre_norm"], win_arr,
        small["mla_q_norm"], wq_arr, small["mla_kv_norm"], wk_arr, wv_arr)
    dwuq, dwukv = _unarrange_grads(dwq_arr, dwk_arr, dwv_arr)
    loss = 0.5 * jnp.sum(loss_acc) * (1.0 / D_MODEL)
    grads = dict(attn_pre_norm=d_pre, w_in=dwin4, mla_q_norm=d_qn, mla_w_uq=dwuq, mla_kv_norm=d_kvn, mla_w_ukv=dwukv,
                 mla_out_norm=d_mla, hgrn_lb_logits=d_lbl, hgrn_out_norm=d_hg, w_out=dwout, attn_post_norm=d_post,
                 ffn_pre_norm=d_fpre, w_gate=dwg, w_up=dwu, w_down=dwd, ffn_post_norm=d_fpost)
    if place is None:
        return loss, dx, grads
    return loss, dx, grads, ffn_final


def _place():
    x, y, c = lax.axis_index("x"), lax.axis_index("y"), lax.axis_index("c")
    others = [(1 - x, y), (x, 1 - y), (1 - x, 1 - y)]
    return x, y, c, 2 * x + y, (x, y, 1 - c), others


def _half(ref, c, rows):
    return ref.at[pl.ds(pl.multiple_of(c * rows, 8), rows)]


def _rcopy(src, dst, send, recv, k, to):
    return pltpu.make_async_remote_copy(src_ref=src, dst_ref=dst, send_sem=send.at[k], recv_sem=recv.at[k],
                                        device_id=to, device_id_type=MESH)


class _Gather:
    def __init__(self, ins, outs, send, recv):
        self.ins, self.outs, self.send, self.recv = ins, outs, send, recv
        self.n = len(ins)
        self.halves = [r.shape[0] // 2 for r in ins]
        _, _, self.c, self.me, self.sib, self.others = _place()

    def _each(self):
        for j, (px, py) in enumerate(self.others):
            for a in range(self.n):
                yield j * self.n + a, a, 2 * px + py, (px, py, self.c)

    def sends(self):
        return [_rcopy(_half(self.ins[a], self.c, self.halves[a]), _half(self.outs[a].at[self.me], self.c, self.halves[a]),
                       self.send, self.recv, k, to) for k, a, _, to in self._each()]

    def arrivals(self):
        parts = [(k, _half(self.outs[a].at[chip], self.c, self.halves[a]), to) for k, a, chip, to in self._each()]
        return [_rcopy(p, p, self.send, self.recv, k, to) for k, p, to in parts]

    def forwards(self):
        parts = [(k, _half(self.outs[a].at[chip], self.c, self.halves[a])) for k, a, chip, _ in self._each()]
        return [_rcopy(p, p, self.send, self.recv, 3 * self.n + k, self.sib) for k, p in parts]

    def forward_arrivals(self):
        parts = [(k, _half(self.outs[a].at[chip], 1 - self.c, self.halves[a])) for k, a, chip, _ in self._each()]
        return [_rcopy(p, p, self.send, self.recv, 3 * self.n + k, self.sib) for k, p in parts]

    @staticmethod
    def out_shapes(arrs):
        return [jax.ShapeDtypeStruct((N_CHIPS,) + a.shape, a.dtype) for a in arrs]

    @staticmethod
    def semaphores(arrs):
        return [pltpu.SemaphoreType.DMA((6 * len(arrs),)), pltpu.SemaphoreType.DMA((6 * len(arrs),))]


def _gather_chips(arrs, name):
    n = len(arrs)

    def body(*refs):
        gat = _Gather(refs[:n], refs[n:2 * n], *refs[2 * n:])
        sends, forwards = gat.sends(), gat.forwards()
        for cp in sends:
            cp.start()
        for arrival, fw in zip(gat.arrivals(), forwards):
            arrival.wait_recv()
            fw.start()
        for arrival in gat.forward_arrivals():
            arrival.wait_recv()
        for cp in sends + forwards:
            cp.wait_send()

    return pl.pallas_call(body, name=name, in_specs=[ANY] * n, out_specs=[ANY] * n, out_shape=_Gather.out_shapes(arrs),
                          scratch_shapes=_Gather.semaphores(arrs))(*arrs)


GRAD_BLOCKS = 2


def _pair_swap_copies(g_refs, r_refs, send, recv):
    _, _, c, _, sib, _ = _place()
    copies = []
    for a, (g, r) in enumerate(zip(g_refs, r_refs)):
        h = g.shape[1] // 2
        copies.append(_rcopy(g.at[:, pl.ds(pl.multiple_of((1 - c) * h, 8), h)], r, send, recv, a, sib))
    return copies


def _half_stack_shapes(gs, dtype=None):
    return [jax.ShapeDtypeStruct((N_CHIPS, g.shape[1] // 2, g.shape[2]), dtype or g.dtype) for g in gs]


def _pair_swap(gs, sm):
    n = len(gs)

    def body(*refs):
        g_refs, sm_ref = refs[:n], refs[n]
        r_refs, ssib_ref = refs[n + 1:2 * n + 1], refs[2 * n + 1]
        send, recv = refs[2 * n + 2:]
        copies = _pair_swap_copies(g_refs, r_refs, send, recv)
        copies.append(_rcopy(sm_ref, ssib_ref, send, recv, n, _place()[4]))
        for cp in copies:
            cp.start()
        for cp in copies:
            cp.wait()

    return pl.pallas_call(
        body, name="pair_swap", in_specs=[ANY] * (n + 1), out_specs=[ANY] * (n + 1),
        out_shape=_half_stack_shapes(gs) + [jax.ShapeDtypeStruct(sm.shape, sm.dtype)],
        scratch_shapes=[pltpu.SemaphoreType.DMA((n + 1,)), pltpu.SemaphoreType.DMA((n + 1,))],
    )(*gs, sm)


def _pair_sum(place, gs, rs, small=None, name="pair_sum"):
    n = len(gs)
    nb = GRAD_BLOCKS

    def body(place_ref, *refs):
        g_refs, r_refs, p_refs = refs[:n], refs[n:2 * n], refs[-n - 1:-1] if small else refs[-n:]
        for a in range(n):
            p_refs[a][0] = (g_refs[a][0] + r_refs[a][0]).astype(p_refs[a].dtype)
        if small:
            @pl.when((pl.program_id(0) == 0) & (pl.program_id(1) == 0))
            def _():
                refs[-1][...] = refs[2 * n][...] + refs[2 * n + 1][...]

    in_specs, out_specs = [], []
    for g in gs:
        blk = (1, g.shape[1] // 2 // nb, g.shape[2])
        in_specs.append(pl.BlockSpec(blk, lambda i, k, p: (k, p[0] * nb + i, 0)))
    for g in gs:
        blk = (1, g.shape[1] // 2 // nb, g.shape[2])
        in_specs.append(pl.BlockSpec(blk, lambda i, k, p: (k, i, 0)))
        out_specs.append(pl.BlockSpec(blk, lambda i, k, p: (k, i, 0)))
    out_shape = _half_stack_shapes(gs, BF16)
    if small:
        sm_spec = pl.BlockSpec(small[0].shape, lambda i, k, p: (0, 0))
        in_specs += [sm_spec, sm_spec]
        out_specs.append(sm_spec)
        out_shape.append(jax.ShapeDtypeStruct(small[0].shape, F32))
    return pl.pallas_call(
        body, name=name,
        grid_spec=pltpu.PrefetchScalarGridSpec(num_scalar_prefetch=1, grid=(nb, N_CHIPS), in_specs=in_specs,
                                               out_specs=out_specs),
        out_shape=out_shape,
        compiler_params=_params(("arbitrary", "arbitrary")),
    )(place, *gs, *rs, *(small or ()))


def _chip_swap_copies(p_refs, ri_refs, send, recv):
    _, _, c, _, _, others = _place()
    n = len(p_refs)
    return [_rcopy(p_refs[a].at[2 * px + py], ri_refs[a].at[j], send, recv, j * n + a, (px, py, c))
            for j, (px, py) in enumerate(others) for a in range(n)]


def _chip_swap_shapes(ps):
    return [jax.ShapeDtypeStruct((3,) + p.shape[1:], p.dtype) for p in ps]


def _chip_swap(ps, pair):
    n = len(ps)

    def body(*refs):
        start, finish = _chip_swap_plan(refs[:n], refs[n], refs[n + 1:2 * n + 1], refs[2 * n + 1], *refs[2 * n + 2:])
        start()
        finish()

    return pl.pallas_call(
        body, name="chip_swap", in_specs=[ANY] * (n + 1), out_specs=[ANY] * (n + 1),
        out_shape=_chip_swap_out_shapes(ps, pair), scratch_shapes=_chip_swap_semaphores(n),
    )(*ps, pair)


def _chip_swap_plan(p_refs, pair_ref, ri_refs, sm4_ref, send, recv, lsem):
    n = len(p_refs)
    hs = SMALL_ROWS // 2
    x, y, c, me, sib, others = _place()
    local = pltpu.make_async_copy(pair_ref, sm4_ref.at[me], lsem.at[0])
    copies = _chip_swap_copies(p_refs, ri_refs, send, recv)
    arrivals = list(copies)
    for j, (px, py) in enumerate(others):
        copies.append(_rcopy(_half(pair_ref, c, hs), _half(sm4_ref.at[me], c, hs), send, recv, 3 * n + j, (px, py, c)))
        part = _half(sm4_ref.at[2 * px + py], c, hs)
        arrivals.append(_rcopy(part, part, send, recv, 3 * n + j, (px, py, c)))

    def start():
        local.start()
        for cp in copies:
            cp.start()

    def finish():
        for arrival in arrivals:
            arrival.wait_recv()
        for cp in copies:
            cp.wait_send()
        local.wait()

    return start, finish


def _chip_swap_out_shapes(ps, pair):
    return _chip_swap_shapes(ps) + [jax.ShapeDtypeStruct((N_CHIPS,) + pair.shape, pair.dtype)]


def _chip_swap_semaphores(n):
    k = 3 * (n + 1)
    return [pltpu.SemaphoreType.DMA((k,)), pltpu.SemaphoreType.DMA((k,)), pltpu.SemaphoreType.DMA((1,))]


def _chip_sum(place, gs, rs, ris, name="chip_sum"):
    n = len(gs)
    nb = GRAD_BLOCKS

    def body(place_ref, *refs):
        g_refs, r_refs, ri_refs, o_refs = refs[:n], refs[n:2 * n], refs[2 * n:3 * n], refs[3 * n:]
        for a in range(n):
            ri = ri_refs[a]
            o_refs[a][...] = (g_refs[a][0] + r_refs[a][0]) + ri[0].astype(F32) + ri[1].astype(F32) + ri[2].astype(F32)

    in_specs, out_specs, out_shape = [], [], []
    for g in gs:
        blk = (1, g.shape[1] // 2 // nb, g.shape[2])
        in_specs.append(pl.BlockSpec(blk, lambda i, p: (p[1], p[0] * nb + i, 0)))
    for g in gs:
        blk = (1, g.shape[1] // 2 // nb, g.shape[2])
        in_specs.append(pl.BlockSpec(blk, lambda i, p: (p[1], i, 0)))
    for g in gs:
        rb = g.shape[1] // 2 // nb
        in_specs.append(pl.BlockSpec((3, rb, g.shape[2]), lambda i, p: (0, i, 0)))
        out_specs.append(pl.BlockSpec((rb, g.shape[2]), lambda i, p: (p[0] * nb + i, 0)))
        out_shape.append(jax.ShapeDtypeStruct(g.shape[1:], F32))
    return pl.pallas_call(
        body, name=name,
        grid_spec=pltpu.PrefetchScalarGridSpec(num_scalar_prefetch=1, grid=(nb,), in_specs=in_specs, out_specs=out_specs),
        out_shape=out_shape,
        compiler_params=_params(("arbitrary",)),
    )(place, *gs, *rs, *ris)


def _pair_fill_copies(g_refs, send, recv):
    _, _, c, _, sib, _ = _place()
    copies, waits = [], []
    for a, g in enumerate(g_refs):
        h = g.shape[0] // 2
        mine, theirs = _half(g, c, h), _half(g, 1 - c, h)
        copies.append(_rcopy(mine, mine, send, recv, a, sib))
        waits.append(_rcopy(theirs, theirs, send, recv, a, sib))
    return copies, waits


def _pair_fill(gfs, sm4):
    n = len(gfs)
    hs = SMALL_ROWS // 2

    def body(*refs):
        g_refs, sm4_ref = refs[n + 1:2 * n + 1], refs[2 * n + 1]
        send, recv = refs[2 * n + 2:]
        x, y, c, me, sib, others = _place()
        copies, waits = _pair_fill_copies(g_refs, send, recv)
        for j, (px, py) in enumerate(others):
            chip = 2 * px + py
            mine, theirs = _half(sm4_ref.at[chip], c, hs), _half(sm4_ref.at[chip], 1 - c, hs)
            copies.append(pltpu.make_async_remote_copy(src_ref=mine, dst_ref=mine, send_sem=send.at[n + j],
                                                       recv_sem=recv.at[n + j], device_id=sib, device_id_type=MESH))
            waits.append(pltpu.make_async_remote_copy(src_ref=theirs, dst_ref=theirs, send_sem=send.at[n + j],
                                                      recv_sem=recv.at[n + j], device_id=sib, device_id_type=MESH))
        for cp in copies:
            cp.start()
        for w in waits:
            w.wait_recv()
        for cp in copies:
            cp.wait_send()

    return pl.pallas_call(
        body, name="pair_fill", in_specs=[ANY] * (n + 1), out_specs=[ANY] * (n + 1),
        out_shape=[jax.ShapeDtypeStruct(g.shape, g.dtype) for g in gfs] + [jax.ShapeDtypeStruct(sm4.shape, sm4.dtype)],
        input_output_aliases={i: i for i in range(n + 1)},
        scratch_shapes=[pltpu.SemaphoreType.DMA((n + 3,)), pltpu.SemaphoreType.DMA((n + 3,))],
    )(*gfs, sm4)


def _adamw_math(w, g, m, v):
    m = ADAM_B1 * m + (1.0 - ADAM_B1) * g
    v = ADAM_B2 * v + (1.0 - ADAM_B2) * (g * g)
    m_hat = m / (1.0 - ADAM_B1 ** ADAM_STEP)
    v_hat = v / (1.0 - ADAM_B2 ** ADAM_STEP)
    return -ADAM_LR * (m_hat / (jnp.sqrt(v_hat) + ADAM_EPS) + ADAM_WD * w), m, v


def _adamw(items, steps, name):
    n = len(items)

    def body(*refs):
        for a in range(n):
            g = refs[4 * a + 1][...]
            d, mo, vo = _adamw_math(refs[4 * a][...], g, refs[4 * a + 2][...], refs[4 * a + 3][...])
            for out, val in zip(refs[4 * n + 4 * a:4 * n + 4 * a + 4], (g, d, mo, vo)):
                out[...] = val

    spec = lambda w: pl.BlockSpec((w.shape[0] // steps, w.shape[1]), lambda i: (i, 0))
    flat = pl.pallas_call(
        body, name=name, grid=(steps,), in_specs=[spec(it[0]) for it in items for _ in range(4)],
        out_specs=[spec(it[0]) for it in items for _ in range(4)],
        out_shape=[jax.ShapeDtypeStruct(it[0].shape, F32) for it in items for _ in range(4)],
        compiler_params=_params(("arbitrary",)),
    )(*[a for it in items for a in it])
    return [flat[4 * a:4 * a + 4] for a in range(n)]


def _adamw_small(sm4, wmv):
    views = SMALL_VIEWS[:-1]
    n = len(views)

    def body(sm4_ref, *refs):
        g_all = ((sm4_ref[0] + sm4_ref[1]) + sm4_ref[2]) + sm4_ref[3]
        row = 0
        for a, (_, rows, cols) in enumerate(views):
            g = g_all[row:row + rows, :cols]
            row += -(-rows // ROW_TILE) * ROW_TILE
            d, mo, vo = _adamw_math(refs[3 * a][...], g, refs[3 * a + 1][...], refs[3 * a + 2][...])
            for out, val in zip(refs[3 * n + 4 * a:3 * n + 4 * a + 4], (g, d, mo, vo)):
                out[...] = val
        refs[-1][...] = g_all[row:row + 1, :128]

    flat = pl.pallas_call(
        body, name="adamw_small",
        out_shape=[jax.ShapeDtypeStruct((rows, cols), F32) for _, rows, cols in views for _ in range(4)]
        + [jax.ShapeDtypeStruct((1, 128), F32)],
        compiler_params=pltpu.CompilerParams(vmem_limit_bytes=VMEM_LIMIT),
    )(sm4, *[a for t in wmv for a in t])
    return [flat[4 * a:4 * a + 4] for a in range(n)] + [flat[-1]]


SMALL_NAMES = ("attn_pre_norm", "mla_q_norm", "mla_kv_norm", "mla_w_ukv", "mla_out_norm", "hgrn_lb_logits",
               "hgrn_out_norm", "attn_post_norm", "ffn_pre_norm", "ffn_post_norm")
BIG_NAMES = ("w_in", "mla_w_uq", "w_out", "w_gate", "w_up", "w_down")
WEIGHT_NAMES = ("attn_pre_norm", "w_in", "mla_q_norm", "mla_w_uq", "mla_kv_norm", "mla_w_ukv", "mla_out_norm",
                "hgrn_lb_logits", "hgrn_out_norm", "w_out", "attn_post_norm", "ffn_pre_norm", "w_gate", "w_up", "w_down",
                "ffn_post_norm")


UQ_COMM_SHAPE = (192, 384)


def _pack_small(vals):
    parts = []
    for name, rows, cols in SMALL_VIEWS:
        pad_rows = -(-rows // ROW_TILE) * ROW_TILE - rows
        parts.append(jnp.pad(vals[name].reshape(rows, cols), ((0, pad_rows), (0, D_MODEL - cols))))
    return jnp.concatenate(parts, axis=0)


def kernel(x, positions, attn_pre_norm, w_in, mla_q_norm, mla_w_uq, mla_kv_norm, mla_w_ukv, mla_out_norm, hgrn_lb_logits, hgrn_out_norm, w_out, attn_post_norm, ffn_pre_norm, w_gate, w_up, w_down, ffn_post_norm, loss_target, m_attn_pre_norm, m_w_in, m_mla_q_norm, m_mla_w_uq, m_mla_kv_norm, m_mla_w_ukv, m_mla_out_norm, m_hgrn_lb_logits, m_hgrn_out_norm, m_w_out, m_attn_post_norm, m_ffn_pre_norm, m_w_gate, m_w_up, m_w_down, m_ffn_post_norm, v_attn_pre_norm, v_w_in, v_mla_q_norm, v_mla_w_uq, v_mla_kv_norm, v_mla_w_ukv, v_mla_out_norm, v_hgrn_lb_logits, v_hgrn_out_norm, v_w_out, v_attn_post_norm, v_ffn_pre_norm, v_w_gate, v_w_up, v_w_down, v_ffn_post_norm):
    args = locals()
    W = {n: args[n] for n in WEIGHT_NAMES}
    M = {n: args["m_" + n] for n in WEIGHT_NAMES}
    V = {n: args["v_" + n] for n in WEIGHT_NAMES}
    T = x.shape[1]
    cx, cy, cc = lax.axis_index("x"), lax.axis_index("y"), lax.axis_index("c")

    win_rows = D_IN // N_CHIPS
    shard2d = {"w_in": (win_rows, D_MODEL), "mla_w_uq": (Q_RANK // N_CHIPS, MLA_HEADS * MLA_QK),
               "w_out": (D_MODEL // N_CHIPS, D_MODEL), "w_gate": (FF_SHARD, D_MODEL), "w_up": (FF_SHARD, D_MODEL),
               "w_down": (FF_SHARD, D_MODEL)}
    transposed = ("w_in", "w_gate", "w_up")
    to2d = lambda n, a: a[0].T if n in transposed else a.reshape(shard2d[n])
    from2d = lambda n, t: t.T[None] if n in transposed else t.reshape(W[n].shape)
    me = 2 * cx + cy
    place = jnp.stack([cc, me]).astype(jnp.int32)
    local_b = [to2d(n, W[n]).astype(BF16) for n in BIG_NAMES]
    local_b[0] = jnp.pad(local_b[0], ((0, FF_SHARD - win_rows), (0, 0)))
    stacks = _gather_chips(local_b[:2], "gather_weights")
    win4, wuq4 = [lax.dynamic_update_slice(s, l[None], (me, 0, 0)) for s, l in zip(stacks, local_b)]
    win_t = win4[:, :win_rows].reshape(D_IN, D_MODEL)
    wuq_full = wuq4.reshape(Q_RANK, MLA_HEADS, MLA_QK)
    win_arr, wq_arr, wk_arr, wv_arr = _arrange_weights(win_t, wuq_full, mla_w_ukv[0].astype(BF16))
    small = {n: W[n][0] if n == "mla_w_ukv" else W[n].reshape(-1, W[n].shape[-1]) for n in SMALL_NAMES}

    loss_local, dx, grads, ffn_final = _local_step(x[0], positions.reshape(T, 1), loss_target[0], small, win_arr,
                                                           wq_arr, wk_arr, wv_arr, local_b[2:], place)

    gs = [grads["w_in"], grads["mla_w_uq"].reshape((N_CHIPS,) + UQ_COMM_SHAPE), grads["w_out"].reshape((N_CHIPS,) + shard2d["w_out"])]
    sm = _pack_small({**grads, "loss": loss_local})
    *rs, ssib = _pair_swap(gs, sm)
    *ps, pair = _pair_sum(place, gs, rs, small=(sm, ssib))
    ffn_names, rest_names = BIG_NAMES[3:], BIG_NAMES[:3]
    g2d = dict(zip(ffn_names, ffn_final))
    adam_in = lambda names_: [(to2d(n, W[n]), g2d[n], to2d(n, M[n]), to2d(n, V[n])) for n in names_]
    updates = dict(zip(ffn_names, _adamw(adam_in(ffn_names), 8, "adamw_ffn")))
    *ris, sm4 = _chip_swap(ps, pair)
    *gfin, smf = _pair_fill(_chip_sum(place, gs, rs, ris), sm4)

    g2d.update({n: gfin[k].reshape((-1,) + shard2d[n][1:]) for k, n in enumerate(rest_names)})
    updates.update(zip(rest_names[:2], _adamw(adam_in(rest_names[:2]), 3, "adamw_w_in")))
    updates.update(zip(rest_names[2:], _adamw(adam_in(rest_names[2:]), 8, "adamw_w_out")))
    G, DW, NM, NV = {}, {}, {}, {}
    for n, outs in updates.items():
        G[n], DW[n], NM[n], NV[n] = (from2d(n, t) for t in outs)
    view2d = lambda n, a: a.reshape(next((r, c) for name, r, c in SMALL_VIEWS if name == n))
    *res, loss_row = _adamw_small(smf, [tuple(view2d(n, t[n]) for t in (W, M, V)) for n in SMALL_NAMES])
    for n, outs in zip(SMALL_NAMES, res):
        G[n], DW[n], NM[n], NV[n] = (t.reshape(W[n].shape) for t in outs)
    loss = loss_row[0, 0]
    return (loss, dx[None], *[G[n] for n in WEIGHT_NAMES], *[DW[n] for n in WEIGHT_NAMES],
            *[NM[n] for n in WEIGHT_NAMES], *[NV[n] for n in WEIGHT_NAMES])
```

```python
import jax
import jax.numpy as jnp
from jax import lax
from jax.experimental import pallas as pl
from jax.experimental.pallas import tpu as pltpu

F32 = jnp.float32
BF16 = jnp.bfloat16
MXU_DTYPE = BF16

D_MODEL = 1024
MLA_HEADS = 8
MLA_NOPE = 64
MLA_ROPE = 32
MLA_V = 64
MLA_QK = MLA_NOPE + MLA_ROPE
Q_RANK = 384
KV_RANK = 128
MLA_WIDTH = MLA_HEADS * MLA_V
HEAD_PAD = 128
HGRN_HEADS = 4
HGRN_DIM = 128
HGRN_WIDTH = HGRN_HEADS * HGRN_DIM
CHUNK = 64
SUB = 16
HGRN_CPI = 4
D_IN = Q_RANK + KV_RANK + MLA_ROPE + 4 * HGRN_WIDTH
D_IN_ARR = Q_RANK + KV_RANK + HEAD_PAD + 4 * HGRN_WIDTH
D_FF = 2816
N_CHIPS = 4
FF_SHARD = D_FF // N_CHIPS
EPS = 1e-6
ROPE_THETA = 10000.0
ATTN_SCALE = MLA_QK ** -0.5
ATTN_SCALE_LOG2 = ATTN_SCALE * 1.4426950408889634
NEG_BIG = -1e30

ADAM_LR = 0.001
ADAM_B1 = 0.9
ADAM_B2 = 0.999
ADAM_EPS = 1e-08
ADAM_WD = 0.01
ADAM_STEP = 10

VMEM_LIMIT = 56 * 1024 * 1024

SMALL_VIEWS = (("attn_pre_norm", 1, 1024), ("mla_q_norm", 1, 384), ("mla_kv_norm", 1, 128), ("mla_w_ukv", 128, 1024),
               ("mla_out_norm", 1, 512), ("hgrn_lb_logits", 2, 512), ("hgrn_out_norm", 1, 512),
               ("attn_post_norm", 1, 1024), ("ffn_pre_norm", 1, 1024), ("ffn_post_norm", 1, 1024), ("loss", 1, 1))
ROW_TILE = 8
SMALL_ROWS = sum(-(-rows // ROW_TILE) * ROW_TILE for _, rows, _ in SMALL_VIEWS)

MESH = pl.DeviceIdType.MESH
ANY = pl.BlockSpec(memory_space=pl.ANY)


def _dot(a, b, dims, exact):
    if exact:
        return lax.dot_general(a.astype(F32), b.astype(F32), (dims, ((), ())), precision=lax.Precision.HIGH,
                               preferred_element_type=F32)
    return lax.dot_general(a.astype(MXU_DTYPE), b.astype(MXU_DTYPE), (dims, ((), ())), preferred_element_type=F32)


def _mm(a, b, exact=False):
    return _dot(a, b, ((1,), (0,)), exact)


def _mm_nt(a, b, exact=False):
    return _dot(a, b, ((1,), (1,)), exact)


def _mm_tn(a, b, exact=False):
    return _dot(a, b, ((0,), (0,)), exact)


def _rms_fwd(x, w):
    r = lax.rsqrt(jnp.mean(x * x, axis=-1, keepdims=True) + EPS)
    xn = x * r
    return xn * w, xn, r


def _rms_bwd(dy, xn, r, w):
    dxn = dy * w
    dx = r * (dxn - xn * jnp.mean(dxn * xn, axis=-1, keepdims=True))
    dw = jnp.sum(dy * xn, axis=0, keepdims=True)
    return dx, dw


def _group_sums(v, gs):
    t, n = v.shape
    lane = lax.broadcasted_iota(jnp.int32, (t, 128), 1)
    out = []
    for p in range(n // 128):
        vb = v[:, 128 * p:128 * (p + 1)]
        if gs == 128:
            out.append(jnp.sum(vb, axis=-1, keepdims=True))
        else:
            out.append(jnp.sum(jnp.where(lane < 64, vb, 0.0), axis=-1, keepdims=True))
            out.append(jnp.sum(jnp.where(lane >= 64, vb, 0.0), axis=-1, keepdims=True))
    return out


def _group_bcast(sums, gs, t):
    lane = lax.broadcasted_iota(jnp.int32, (t, 128), 1)
    if gs == 128:
        return jnp.concatenate([jnp.broadcast_to(s, (t, 128)) for s in sums], axis=-1)
    return jnp.concatenate([jnp.where(lane < 64, sums[2 * p], sums[2 * p + 1]) for p in range(len(sums) // 2)],
                           axis=-1)


def _grms_fwd(x, w, gs):
    t = x.shape[0]
    r = lax.rsqrt(_group_bcast(_group_sums(x * x, gs), gs, t) * (1.0 / gs) + EPS)
    xn = x * r
    return xn * w, xn, r


def _grms_bwd(dy, xn, r, w, gs):
    t = dy.shape[0]
    dxn = dy * w
    dx = r * (dxn - xn * (_group_bcast(_group_sums(dxn * xn, gs), gs, t) * (1.0 / gs)))
    dw = jnp.sum(dy * xn, axis=0, keepdims=True)
    return dx, dw


def _rope_tables(c_tab, s_tab):
    lane = lax.broadcasted_iota(jnp.int32, c_tab.shape, 1)
    first = (lane >= MLA_NOPE) & (lane < MLA_NOPE + MLA_ROPE // 2)
    second = (lane >= MLA_NOPE + MLA_ROPE // 2) & (lane < MLA_QK)
    return c_tab, jnp.where(first, -s_tab, 0.0), jnp.where(second, s_tab, 0.0)


def _rope(v, c, sa, sb):
    return v * c + pltpu.roll(v, HEAD_PAD - MLA_ROPE // 2, 1) * sa + pltpu.roll(v, MLA_ROPE // 2, 1) * sb


def _rope_bwd(d, c, sa, sb):
    return d * c - pltpu.roll(d, HEAD_PAD - MLA_ROPE // 2, 1) * sa - pltpu.roll(d, MLA_ROPE // 2, 1) * sb


def _params(sem, vmem=VMEM_LIMIT):
    return pltpu.CompilerParams(dimension_semantics=sem, vmem_limit_bytes=vmem)


def _in_fwd(x, pos, invf, w_pre, win, qnw, wq, kvnw, wk, wv, tt=512):
    T = x.shape[0]

    def body(x_ref, pos_ref, invf_ref, wpre_ref, win_ref, qnw_ref, wq_ref, kvnw_ref, wk_ref, wv_ref,
             cq_ref, ckv_ref, xph_ref, q_ref, k_ref, v_ref, kt_ref, vt_ref, rc_ref, rs_ref):
        u, _, _ = _rms_fwd(x_ref[...], wpre_ref[...])
        lo = Q_RANK + KV_RANK + HEAD_PAD
        xp = _mm_nt(u, win_ref[:lo, :])
        xph_ref[...] = _mm_nt(u, win_ref[lo:, :])
        cq = xp[:, :Q_RANK]
        ckv = xp[:, Q_RANK:Q_RANK + KV_RANK]
        kr = xp[:, Q_RANK + KV_RANK:]
        cq_ref[...] = cq
        ckv_ref[...] = ckv
        ang = pos_ref[...].astype(F32) * invf_ref[...]
        c_tab = jnp.cos(ang)
        s_tab = jnp.sin(ang)
        rc_ref[...] = c_tab
        rs_ref[...] = s_tab
        c, sa, sb = _rope_tables(c_tab, s_tab)
        qn, _, _ = _rms_fwd(cq, qnw_ref[...])
        q = _mm(qn, wq_ref[...])
        kvn, _, _ = _rms_fwd(ckv, kvnw_ref[...])
        kn = _mm(kvn, wk_ref[...])
        v = _mm(kvn, wv_ref[...])
        v_ref[...] = v.astype(v_ref.dtype)
        vt_ref[...] = v.T.astype(vt_ref.dtype)
        krr = _rope(kr, c, sa, sb)
        for h in range(MLA_HEADS):
            sl = slice(HEAD_PAD * h, HEAD_PAD * (h + 1))
            q_ref[:, sl] = (_rope(q[:, sl], c, sa, sb) * ATTN_SCALE_LOG2).astype(q_ref.dtype)
            kh = kn[:, sl] + krr
            k_ref[:, sl] = kh.astype(k_ref.dtype)
            kt_ref[sl, :] = kh.T.astype(kt_ref.dtype)

    row = lambda w: pl.BlockSpec((tt, w), lambda i: (i, 0))
    full = lambda a: pl.BlockSpec(a.shape, lambda i: (0,) * a.ndim)
    qk_w = MLA_HEADS * HEAD_PAD
    return pl.pallas_call(
        body, name="in_fwd", grid=(T // tt,),
        in_specs=[row(D_MODEL), row(1), full(invf), full(w_pre), full(win), full(qnw), full(wq), full(kvnw),
                  full(wk), full(wv)],
        out_specs=[row(Q_RANK), row(KV_RANK), row(4 * HGRN_WIDTH), row(qk_w), row(qk_w), row(MLA_WIDTH),
                   pl.BlockSpec((qk_w, tt), lambda i: (0, i)), pl.BlockSpec((MLA_WIDTH, tt), lambda i: (0, i)),
                   row(HEAD_PAD), row(HEAD_PAD)],
        out_shape=[jax.ShapeDtypeStruct((T, Q_RANK), F32), jax.ShapeDtypeStruct((T, KV_RANK), F32),
                   jax.ShapeDtypeStruct((T, 4 * HGRN_WIDTH), F32), jax.ShapeDtypeStruct((T, qk_w), MXU_DTYPE),
                   jax.ShapeDtypeStruct((T, qk_w), MXU_DTYPE), jax.ShapeDtypeStruct((T, MLA_WIDTH), MXU_DTYPE),
                   jax.ShapeDtypeStruct((qk_w, T), MXU_DTYPE), jax.ShapeDtypeStruct((MLA_WIDTH, T), MXU_DTYPE),
                   jax.ShapeDtypeStruct((T, HEAD_PAD), F32), jax.ShapeDtypeStruct((T, HEAD_PAD), F32)],
        compiler_params=_params(("arbitrary",)),
    )(x, pos, invf, w_pre, win, qnw, wq, kvnw, wk, wv)


def _attn_fwd_t(qb, kb, vt, gather=(), tq=256, hps=8):
    T = qb.shape[0]
    nq = T // tq
    ng = len(gather)
    steps = (MLA_HEADS // hps) * nq
    pass_on = steps - 3

    def body(q_ref, k_ref, vt_ref, *rest):
        o_ref, lse_ref = rest[ng:ng + 2]
        acc_scr = rest[2 * ng + 2]
        qi = pl.program_id(1)
        step_no = pl.program_id(0) * nq + qi
        if ng:
            gat = _Gather(rest[:ng], rest[ng + 2:2 * ng + 2], *rest[2 * ng + 3:])

            @pl.when(step_no == 0)
            def _():
                for cp in gat.sends():
                    cp.start()

            @pl.when(step_no == pass_on)
            def _():
                for arrival in gat.arrivals():
                    arrival.wait_recv()
                for cp in gat.forwards():
                    cp.start()

        heads = [slice(HEAD_PAD * a, HEAD_PAD * (a + 1)) for a in range(hps)]
        acc_scr[...] = jnp.zeros_like(acc_scr)

        def step(j, carry, masked):
            start = pl.multiple_of(j * tq, tq)
            scores = [_mm_nt(k_ref[pl.ds(start, tq), heads[a]], q_ref[:, heads[a]]) for a in range(hps)]
            new = []
            for a in range(hps):
                m, l = carry[a]
                s = scores[a]
                if masked:
                    kk = lax.broadcasted_iota(jnp.int32, (tq, tq), 0)
                    qq = lax.broadcasted_iota(jnp.int32, (tq, tq), 1)
                    s = jnp.where(kk <= qq, s, NEG_BIG)
                m_new = jnp.maximum(m, jnp.max(s, axis=0, keepdims=True))
                alpha = jnp.exp2(m - m_new)
                p = jnp.exp2(s - m_new)
                l = l * alpha + jnp.sum(p, axis=0, keepdims=True)
                vtj = vt_ref[2 * MLA_V * (a // 2):2 * MLA_V * (a // 2 + 1), pl.ds(start, tq)]
                acc_scr[a] = acc_scr[a] * alpha + _mm(vtj, p)
                new.append((m_new, l))
            return tuple(new)

        init = tuple((jnp.full((1, tq), NEG_BIG, F32), jnp.zeros((1, tq), F32)) for _ in range(hps))
        carry = lax.fori_loop(0, qi, lambda j, c: step(j, c, False), init)
        carry = step(qi, carry, True)
        row = lax.broadcasted_iota(jnp.int32, (2 * MLA_V, tq), 0)
        for pr in range(hps // 2):
            (m0, l0), (m1, l1) = carry[2 * pr], carry[2 * pr + 1]
            ot = jnp.where(row < MLA_V, acc_scr[2 * pr] / l0, acc_scr[2 * pr + 1] / l1)
            o_ref[:, 2 * MLA_V * pr:2 * MLA_V * (pr + 1)] = ot.T
            lse_ref[pr, 0:1, :] = m0 + jnp.log2(l0)
            lse_ref[pr, 1:2, :] = m1 + jnp.log2(l1)

        if ng:
            @pl.when(step_no == steps - 1)
            def _():
                for arrival in gat.forward_arrivals():
                    arrival.wait_recv()
                for cp in gat.sends() + gat.forwards():
                    cp.wait_send()

    return pl.pallas_call(
        body, name="attn_fwd", grid=(MLA_HEADS // hps, nq),
        in_specs=[pl.BlockSpec((tq, hps * HEAD_PAD), lambda g, i: (i, g)),
                  pl.BlockSpec((T, hps * HEAD_PAD), lambda g, i: (0, g)),
                  pl.BlockSpec((hps * MLA_V, T), lambda g, i: (g, 0))] + [ANY] * ng,
        out_specs=[pl.BlockSpec((tq, hps * MLA_V), lambda g, i: (i, g)),
                   pl.BlockSpec((hps // 2, 2, tq), lambda g, i: (g, 0, i))] + [ANY] * ng,
        out_shape=[jax.ShapeDtypeStruct((T, MLA_WIDTH), F32), jax.ShapeDtypeStruct((MLA_HEADS // 2, 2, T), F32)]
        + _Gather.out_shapes(gather),
        scratch_shapes=[pltpu.VMEM((hps, 2 * MLA_V, tq), F32)] + (_Gather.semaphores(gather) if ng else []),
        compiler_params=_params(("arbitrary", "arbitrary")),
    )(qb, kb, vt, *gather)


def _attn_bwd_t(qb, kb, kt, vb, dob, lse, dvec, send=(), tq=256, hps=4):
    T = qb.shape[0]
    nq = T // tq
    ns = len(send)
    steps = (MLA_HEADS // hps) * nq

    def body(q_ref, k_ref, kt_ref, v_ref, do_ref, lse_ref, d_ref, *rest):
        dqt_ref, dk_ref, dv_ref = rest[ns:ns + 3]
        va_scr, dv_scr = rest[2 * ns + 3:2 * ns + 5]
        j = pl.program_id(1)
        step_no = pl.program_id(0) * nq + j
        if ns:
            @pl.when(step_no == 0)
            def _():
                for cp in _chip_swap_copies(rest[:ns], rest[ns + 3:2 * ns + 3], *rest[2 * ns + 5:]):
                    cp.start()

        @pl.when(j == 0)
        def _():
            dqt_ref[...] = jnp.zeros_like(dqt_ref)

        lane = lax.broadcasted_iota(jnp.int32, (tq, 2 * MLA_V), 1)
        heads = [slice(HEAD_PAD * a, HEAD_PAD * (a + 1)) for a in range(hps)]
        pairs = [slice(2 * MLA_V * p, 2 * MLA_V * (p + 1)) for p in range(hps // 2)]
        for pr in range(hps // 2):
            vpair = v_ref[:, pairs[pr]]
            va_scr[2 * pr] = jnp.where(lane < MLA_V, vpair, jnp.zeros_like(vpair))
            va_scr[2 * pr + 1] = jnp.where(lane >= MLA_V, vpair, jnp.zeros_like(vpair))
        dk_ref[...] = jnp.zeros_like(dk_ref)
        dv_scr[...] = jnp.zeros_like(dv_scr)

        def step(i, masked):
            start = pl.multiple_of(i * tq, tq)
            rows = pl.ds(start, tq)
            scores = [_mm_nt(k_ref[:, heads[a]], q_ref[rows, heads[a]]) for a in range(hps)]
            dps = [_mm_nt(va_scr[a], do_ref[rows, pairs[a // 2]]) for a in range(hps)]
            for a in range(hps):
                pr, r = a // 2, a % 2
                p = jnp.exp2(scores[a] - lse_ref[pr, r:r + 1, rows])
                if masked:
                    kk = lax.broadcasted_iota(jnp.int32, (tq, tq), 0)
                    qq = lax.broadcasted_iota(jnp.int32, (tq, tq), 1)
                    p = jnp.where(kk <= qq, p, 0.0)
                ds = p * (dps[a] - d_ref[pr, r:r + 1, rows])
                dv_scr[a] += _mm(p, do_ref[rows, pairs[pr]])
                dk_ref[:, heads[a]] += _mm(ds, q_ref[rows, heads[a]])
                dqt_ref[heads[a], rows] += _mm(kt_ref[heads[a], :], ds)

        def loop_body(i, _):
            step(i, False)
            return 0

        step(j, True)
        lax.fori_loop(j + 1, nq, loop_body, 0)
        for pr in range(hps // 2):
            dv_ref[:, pairs[pr]] = jnp.where(lane < MLA_V, dv_scr[2 * pr], dv_scr[2 * pr + 1])
        dk_ref[...] = dk_ref[...] * (ATTN_SCALE / ATTN_SCALE_LOG2)

        if ns:
            @pl.when(step_no == steps - 1)
            def _():
                for cp in _chip_swap_copies(rest[:ns], rest[ns + 3:2 * ns + 3], *rest[2 * ns + 5:]):
                    cp.wait()

    stat = pl.BlockSpec((hps // 2, 2, T), lambda g, j: (g, 0, 0))
    return pl.pallas_call(
        body, name="attn_bwd", grid=(MLA_HEADS // hps, nq),
        in_specs=[pl.BlockSpec((T, hps * HEAD_PAD), lambda g, j: (0, g)),
                  pl.BlockSpec((tq, hps * HEAD_PAD), lambda g, j: (j, g)),
                  pl.BlockSpec((hps * HEAD_PAD, tq), lambda g, j: (g, j)),
                  pl.BlockSpec((tq, hps * MLA_V), lambda g, j: (j, g)),
                  pl.BlockSpec((T, hps * MLA_V), lambda g, j: (0, g)), stat, stat] + [ANY] * ns,
        out_specs=[pl.BlockSpec((hps * HEAD_PAD, T), lambda g, j: (g, 0)),
                   pl.BlockSpec((tq, hps * HEAD_PAD), lambda g, j: (j, g)),
                   pl.BlockSpec((tq, hps * MLA_V), lambda g, j: (j, g))] + [ANY] * ns,
        out_shape=[jax.ShapeDtypeStruct((MLA_HEADS * HEAD_PAD, T), F32),
                   jax.ShapeDtypeStruct((T, MLA_HEADS * HEAD_PAD), F32),
                   jax.ShapeDtypeStruct((T, MLA_WIDTH), F32)] + _chip_swap_shapes(send),
        scratch_shapes=[pltpu.VMEM((hps, tq, 2 * MLA_V), vb.dtype), pltpu.VMEM((hps, tq, 2 * MLA_V), F32)]
        + ([pltpu.SemaphoreType.DMA((3 * ns,)), pltpu.SemaphoreType.DMA((3 * ns,))] if ns else []),
        compiler_params=_params(("arbitrary", "arbitrary")),
    )(qb, kb, kt, vb, dob, lse, dvec, *send)


def _cumsum_rows(x):
    n = x.shape[0]
    row = lax.broadcasted_iota(jnp.int32, x.shape, 0)
    s = 1
    while s < n:
        x = x + jnp.where(row >= s, pltpu.roll(x, s, 0), 0.0)
        s *= 2
    return x


def _rev_cumsum_rows(x):
    n = x.shape[0]
    row = lax.broadcasted_iota(jnp.int32, x.shape, 0)
    s = 1
    while s < n:
        x = x + jnp.where(row < n - s, pltpu.roll(x, n - s, 0), 0.0)
        s *= 2
    return x


def _lb_from_logits(l):
    l0, l1 = l[0:1, :], l[1:2, :]
    m = jnp.maximum(l0, l1)
    e0, e1 = jnp.exp(l0 - m), jnp.exp(l1 - m)
    return e0 / (e0 + e1)


def _hgrn_gates(hq, hf, lb):
    sig_f = jax.nn.sigmoid(hf)
    f = lb + (1.0 - lb) * sig_f
    sig_q = jax.nn.sigmoid(hq)
    return sig_f, f, jnp.log(f), 1.0 - f, sig_q, hq * sig_q


def _hgrn_intra(q, kk, b, exact=False):
    row = lax.broadcasted_iota(jnp.int32, b.shape, 0)
    qs, ks, eqs, eks, a_rows = [], [], [], [], []
    for i in range(CHUNK // SUB):
        ref = b[SUB * i + SUB // 2:SUB * i + SUB // 2 + 1, :]
        eq = jnp.exp(b[SUB * i:SUB * (i + 1), :] - ref)
        ek = jnp.exp(jnp.where(row < SUB * (i + 1), ref - b, NEG_BIG))
        qi = q[SUB * i:SUB * (i + 1), :] * eq
        ki = kk * ek
        a_rows.append(_mm_nt(qi, ki, exact))
        qs.append(qi), ks.append(ki), eqs.append(eq), eks.append(ek)
    tt = lax.broadcasted_iota(jnp.int32, (CHUNK, CHUNK), 0)
    ss = lax.broadcasted_iota(jnp.int32, (CHUNK, CHUNK), 1)
    causal = ss <= tt
    a = jnp.where(causal, jnp.concatenate(a_rows, axis=0), 0.0)
    return a, causal, qs, ks, eqs, eks


def _hgrn_fwd(xph, lbl, tg=512):
    T = xph.shape[0]
    ng, ncg = T // tg, tg // CHUNK
    cols = [slice(HGRN_DIM * h, HGRN_DIM * (h + 1)) for h in range(HGRN_HEADS)]

    def body(lbl_ref, hq_ref, hf_ref, hi_ref, o_ref, st_ref, s_scr):
        @pl.when(pl.program_id(0) == 0)
        def _():
            s_scr[...] = jnp.zeros_like(s_scr)

        lb = _lb_from_logits(lbl_ref[...])

        def chunks(it, _):
            pre = []
            for k in range(HGRN_CPI):
                c = it * HGRN_CPI + k
                rows = pl.ds(pl.multiple_of(c * CHUNK, CHUNK), CHUNK)
                for cs in cols:
                    _, _, lf, kk, _, q = _hgrn_gates(hq_ref[rows, cs], hf_ref[rows, cs], lb[:, cs])
                    v = hi_ref[rows, cs]
                    b = _cumsum_rows(lf)
                    a = _hgrn_intra(q, kk, b)[0]
                    b_last = b[CHUNK - 1:CHUNK, :]
                    pre.append((c, rows, q * jnp.exp(b), a, v, jnp.exp(b_last), _mm_tn(v, kk * jnp.exp(b_last - b))))
            for i, (c, rows, qe, a, v, ebl, upd) in enumerate(pre):
                h = i % HGRN_HEADS
                st = s_scr[h]
                st_ref[h, c] = st
                o_ref[rows, cols[h]] = _mm_nt(qe, st) + _mm(a, v)
                s_scr[h] = st * ebl + upd
            return 0

        lax.fori_loop(0, ncg // HGRN_CPI, chunks, 0)

    col = lambda k: pl.BlockSpec((tg, HGRN_WIDTH), lambda g: (g, k))
    return pl.pallas_call(
        body, name="hgrn_fwd", grid=(ng,),
        in_specs=[pl.BlockSpec((2, HGRN_WIDTH), lambda g: (0, 0)), col(0), col(1), col(2)],
        out_specs=[col(0), pl.BlockSpec((HGRN_HEADS, ncg, HGRN_DIM, HGRN_DIM), lambda g: (0, g, 0, 0))],
        out_shape=[jax.ShapeDtypeStruct((T, HGRN_WIDTH), F32),
                   jax.ShapeDtypeStruct((HGRN_HEADS, T // CHUNK, HGRN_DIM, HGRN_DIM), F32)],
        scratch_shapes=[pltpu.VMEM((HGRN_HEADS, HGRN_DIM, HGRN_DIM), F32)],
        compiler_params=_params(("arbitrary",)),
    )(lbl, xph, xph, xph)


def _hgrn_bwd(xph, lbl, states, d_o, fill=(), tg=512):
    T = xph.shape[0]
    ng, ncg = T // tg, tg // CHUNK
    cols = [slice(HGRN_DIM * h, HGRN_DIM * (h + 1)) for h in range(HGRN_HEADS)]
    nsub = CHUNK // SUB
    nf = len(fill)

    def body(lbl_ref, hq_ref, hf_ref, hi_ref, st_ref, do_ref, *rest):
        dhq_ref, dhf_ref, dhi_ref, dlg_ref = rest[nf:nf + 4]
        ds_scr, dlb_scr = rest[2 * nf + 4:2 * nf + 6]
        fill_copies = lambda: _pair_fill_copies(rest[nf + 4:2 * nf + 4], *rest[2 * nf + 6:])
        g = pl.program_id(0)

        @pl.when(g == 0)
        def _():
            ds_scr[...] = jnp.zeros_like(ds_scr)
            dlb_scr[...] = jnp.zeros_like(dlb_scr)
            for cp in (fill_copies()[0] if nf else ()):
                cp.start()

        lb = _lb_from_logits(lbl_ref[...])

        def chunks(it, _):
            pre = []
            for k, h in ((k, h) for k in range(HGRN_CPI) for h in range(HGRN_HEADS)):
                cs = cols[h]
                c = ncg - 1 - (it * HGRN_CPI + k)
                rows = pl.ds(pl.multiple_of(c * CHUNK, CHUNK), CHUNK)
                hq = hq_ref[rows, cs]
                sig_f, f, lf, kk, sig_q, q = _hgrn_gates(hq, hf_ref[rows, cs], lb[:, cs])
                v = hi_ref[rows, cs]
                do = do_ref[rows, cs]
                b = _cumsum_rows(lf)
                eb = jnp.exp(b)
                a, causal, qs, ks, eqs, eks = _hgrn_intra(q, kk, b)
                b_last = b[CHUNK - 1:CHUNK, :]
                st = st_ref[h, c]
                pre.append(dict(h=h, cs=cs, rows=rows, hq=hq, sig_f=sig_f, f=f, kk=kk, sig_q=sig_q, q=q, v=v, eb=eb, qs=qs,
                                ks=ks, eqs=eqs,
                                eks=eks, ebl=jnp.exp(b_last), el=jnp.exp(b_last - b), st=st,
                                da=jnp.where(causal, _mm_nt(do, v, True), 0.0), dq=_mm(do, st, True) * eb,
                                dv=_mm_tn(a, do), dsu=_mm_tn(do, q * eb, True)))
            for w in pre:
                dq_rows = []
                dk = jnp.zeros_like(w["q"])
                for i in range(nsub):
                    dai = w["da"][SUB * i:SUB * (i + 1), :]
                    dq_rows.append(_mm(dai, w["ks"][i], True) * w["eqs"][i])
                    dk = dk + _mm_tn(dai, w["qs"][i], True) * w["eks"][i]
                w["dq"] = w["dq"] + jnp.concatenate(dq_rows, axis=0)
                w["dk"] = dk
            for w in pre:
                h, cs, rows = w["h"], w["cs"], w["rows"]
                kk, el, ebl, dst = w["kk"], w["el"], w["ebl"], ds_scr[h]
                dk_state = _mm(w["v"], dst, True) * el
                dk = w["dk"] + dk_state
                e_last = (ebl * jnp.sum(w["st"] * dst, axis=0, keepdims=True)
                          + jnp.sum(kk * dk_state, axis=0, keepdims=True))
                dlf = _rev_cumsum_rows(w["q"] * w["dq"] - kk * dk) + e_last
                ds_scr[h] = dst * ebl + w["dsu"]
                df = dlf / w["f"] - dk
                sig_f, sig_q = w["sig_f"], w["sig_q"]
                dhf_ref[rows, cs] = df * (1.0 - lb[:, cs]) * sig_f * (1.0 - sig_f)
                dlb_scr[:, cs] += jnp.sum(df * (1.0 - sig_f), axis=0, keepdims=True)
                dhq_ref[rows, cs] = w["dq"] * sig_q * (1.0 + w["hq"] * (1.0 - sig_q))
                dhi_ref[rows, cs] = w["dv"] + _mm_nt(kk * el, dst)
            return 0

        lax.fori_loop(0, ncg // HGRN_CPI, chunks, 0)

        @pl.when(g == ng - 1)
        def _():
            dl0 = dlb_scr[...] * lb * (1.0 - lb)
            dlg_ref[...] = jnp.concatenate([dl0, -dl0], axis=0)
            if nf:
                copies, waits = fill_copies()
                for w in waits:
                    w.wait_recv()
                for cp in copies:
                    cp.wait_send()

    col = lambda k: pl.BlockSpec((tg, HGRN_WIDTH), lambda g: (ng - 1 - g, k))
    logits = pl.BlockSpec((2, HGRN_WIDTH), lambda g: (0, 0))
    big = jax.ShapeDtypeStruct((T, HGRN_WIDTH), F32)
    n_in, n_out = 6, 4
    return pl.pallas_call(
        body, name="hgrn_bwd", grid=(ng,),
        in_specs=[logits, col(0), col(1), col(2),
                  pl.BlockSpec((HGRN_HEADS, ncg, HGRN_DIM, HGRN_DIM), lambda g: (0, ng - 1 - g, 0, 0)), col(0)] + [ANY] * nf,
        out_specs=[col(0), col(0), col(0), logits] + [ANY] * nf,
        out_shape=[big, big, big, jax.ShapeDtypeStruct((2, HGRN_WIDTH), F32)]
        + [jax.ShapeDtypeStruct(f.shape, f.dtype) for f in fill],
        input_output_aliases={n_in + k: n_out + k for k in range(nf)},
        scratch_shapes=[pltpu.VMEM((HGRN_HEADS, HGRN_DIM, HGRN_DIM), F32), pltpu.VMEM((1, HGRN_WIDTH), F32)]
        + ([pltpu.SemaphoreType.DMA((nf,)), pltpu.SemaphoreType.DMA((nf,))] if nf else []),
        compiler_params=_params(("arbitrary",)),
    )(lbl, xph, xph, xph, states, d_o, *fill)


def _proj_fwd(x, o_raw, oh_raw, xph, wout, w_mla, w_hg, w_post, w_fpre, tt=512):
    T = x.shape[0]

    def body(x_ref, o_ref, oh_ref, hg_ref, wout_ref, wmla_ref, whg_ref, wpost_ref, wfpre_ref,
             h1_ref, y1_ref, z_ref, mix_ref):
        om, _, _ = _grms_fwd(o_ref[...], wmla_ref[...], MLA_V)
        hg = hg_ref[...]
        ohn, _, _ = _grms_fwd(oh_ref[...], whg_ref[...], HGRN_DIM)
        mix = jnp.concatenate([om, ohn * (hg * jax.nn.sigmoid(hg))], axis=-1)
        mix_ref[...] = mix.astype(mix_ref.dtype)
        y1 = _mm(mix, wout_ref[...])
        y1_ref[...] = y1
        h1 = x_ref[...] + _rms_fwd(y1, wpost_ref[...])[0]
        h1_ref[...] = h1
        z_ref[...] = _rms_fwd(h1, wfpre_ref[...])[0].astype(z_ref.dtype)

    row = lambda w: pl.BlockSpec((tt, w), lambda i: (i, 0))
    full = lambda a: pl.BlockSpec(a.shape, lambda i: (0,) * a.ndim)
    sds = jax.ShapeDtypeStruct
    return pl.pallas_call(
        body, name="proj_fwd", grid=(T // tt,),
        in_specs=[row(D_MODEL), row(MLA_WIDTH), row(HGRN_WIDTH), pl.BlockSpec((tt, HGRN_WIDTH), lambda i: (i, 3)),
                  full(wout), full(w_mla), full(w_hg), full(w_post), full(w_fpre)],
        out_specs=[row(D_MODEL)] * 4,
        out_shape=[sds((T, D_MODEL), F32), sds((T, D_MODEL), F32), sds((T, D_MODEL), MXU_DTYPE),
                   sds((T, D_MODEL), MXU_DTYPE)],
        compiler_params=_params(("arbitrary",)),
    )(x, o_raw, oh_raw, xph, wout, w_mla, w_hg, w_post, w_fpre)


def _ffn_fwd(zb, h1, tgt, w_fpost, wg, wu, wd, tt=256):
    T = zb.shape[0]
    nj = N_CHIPS

    def body(z_ref, h1_ref, tgt_ref, wfpost_ref, wg_ref, wu_ref, wd_ref, g_ref, up_ref, dy2_ref, dh2_ref, loss_ref, dwf_ref):
        @pl.when(pl.program_id(0) == 0)
        def _():
            loss_ref[...] = jnp.zeros_like(loss_ref)
            dwf_ref[...] = jnp.zeros_like(dwf_ref)

        z = z_ref[...]
        gs = [_mm_nt(z, wg_ref[j]) for j in range(nj)]
        ups = [_mm_nt(z, wu_ref[j]) for j in range(nj)]
        y2 = jnp.zeros((tt, D_MODEL), F32)
        for j in range(nj):
            g_ref[j] = gs[j]
            up_ref[j] = ups[j]
            y2 = y2 + _mm(gs[j] * jax.nn.sigmoid(gs[j]) * ups[j], wd_ref[j])
        w = wfpost_ref[...]
        y2s, y2n, r2 = _rms_fwd(y2, w)
        e = h1_ref[...] + y2s - tgt_ref[...]
        loss_ref[...] += jnp.sum(e * e, axis=0, keepdims=True)
        dh2 = e * (1.0 / D_MODEL)
        dh2_ref[...] = dh2
        dy2, dwf = _rms_bwd(dh2, y2n, r2, w)
        dy2_ref[...] = dy2.astype(dy2_ref.dtype)
        dwf_ref[...] += dwf

    row = pl.BlockSpec((tt, D_MODEL), lambda i: (i, 0))
    vec = pl.BlockSpec((1, D_MODEL), lambda i: (0, 0))
    resident = pl.BlockSpec((nj, FF_SHARD, D_MODEL), lambda i: (0, 0, 0), pipeline_mode=pl.Buffered(1))
    act = pl.BlockSpec((nj, tt, FF_SHARD), lambda i: (0, i, 0))
    sds = jax.ShapeDtypeStruct
    return pl.pallas_call(
        body, name="ffn_fwd", grid=(T // tt,),
        in_specs=[row, row, row, vec, resident, resident, resident],
        out_specs=[act, act, row, row, vec, vec],
        out_shape=[sds((nj, T, FF_SHARD), F32), sds((nj, T, FF_SHARD), F32), sds((T, D_MODEL), MXU_DTYPE),
                   sds((T, D_MODEL), F32), sds((1, D_MODEL), F32), sds((1, D_MODEL), F32)],
        compiler_params=_params(("arbitrary",)),
    )(zb, h1, tgt, w_fpost, wg, wu, wd)


def _ffn_bwd(zb, g, up, dy2b, wg, wu, wd, tt=512):
    T = zb.shape[0]
    nj = N_CHIPS

    def body(z_ref, g_ref, up_ref, dy2_ref, wg_ref, wu_ref, wd_ref, dwg_ref, dwu_ref, dwd_ref, dz_ref):
        @pl.when(pl.program_id(1) == 0)
        def _():
            dwg_ref[...] = jnp.zeros_like(dwg_ref)
            dwu_ref[...] = jnp.zeros_like(dwu_ref)
            dwd_ref[...] = jnp.zeros_like(dwd_ref)

        z, g_, up_, dy2 = z_ref[...], g_ref[0], up_ref[0], dy2_ref[...]
        sg = jax.nn.sigmoid(g_)
        act = g_ * sg
        dff = _mm_nt(dy2, wd_ref[0])
        dwd_ref[0] += _mm_tn(act * up_, dy2)
        dg = dff * up_ * sg * (1.0 + g_ * (1.0 - sg))
        dup = dff * act
        dwg_ref[0] += _mm_tn(dg, z)
        dwu_ref[0] += _mm_tn(dup, z)
        dz_ref[0] = _mm(dg, wg_ref[0]) + _mm(dup, wu_ref[0])

    row = pl.BlockSpec((tt, D_MODEL), lambda j, i: (i, 0))
    act = pl.BlockSpec((1, tt, FF_SHARD), lambda j, i: (j, i, 0))
    w_sh = pl.BlockSpec((1, FF_SHARD, D_MODEL), lambda j, i: (j, 0, 0))
    w_grad = jax.ShapeDtypeStruct((nj, FF_SHARD, D_MODEL), F32)
    return pl.pallas_call(
        body, name="ffn_bwd", grid=(nj, T // tt),
        in_specs=[row, act, act, row, w_sh, w_sh, w_sh],
        out_specs=[w_sh, w_sh, w_sh, pl.BlockSpec((1, tt, D_MODEL), lambda j, i: (j, i, 0))],
        out_shape=[w_grad, w_grad, w_grad, jax.ShapeDtypeStruct((nj, T, D_MODEL), F32)],
        compiler_params=_params(("arbitrary", "arbitrary")),
    )(zb, g, up, dy2b, wg, wu, wd)


def _mid_bwd(dzp, dh2, h1, y1, mixb, o_raw, oh_raw, xph, wout, w_fpre, w_post, w_mla, w_hg, swap=(), tt=256):
    T = dh2.shape[0]
    nsw = len(swap)
    n_in, n_out = 13, 10

    def body(*refs):
        (dzp_ref, dh2_ref, h1_ref, y1_ref, mix_ref, o_ref, oh_ref, hg_ref, wout_ref, wfpre_ref, wpost_ref,
         wmla_ref, whg_ref) = refs[:n_in]
        (dh1_ref, dwout_ref, do_ref, doh_ref, dhg_ref, dvec_ref, dwfpre_ref, dwpost_ref, dwmla_ref,
         dwhg_ref) = refs[n_in + nsw:n_in + nsw + n_out]
        swap_copies = lambda: _pair_swap_copies(refs[n_in:n_in + nsw], refs[n_in + nsw + n_out:n_in + 2 * nsw + n_out],
                                                *refs[n_in + 2 * nsw + n_out:])

        @pl.when(pl.program_id(0) == 0)
        def _():
            for r in (dwout_ref, dwfpre_ref, dwpost_ref, dwmla_ref, dwhg_ref):
                r[...] = jnp.zeros_like(r)
            for cp in (swap_copies() if nsw else ()):
                cp.start()

        dz = dzp_ref[0] + dzp_ref[1] + dzp_ref[2] + dzp_ref[3]
        wfpre = wfpre_ref[...]
        _, h1n, r = _rms_fwd(h1_ref[...], wfpre)
        dh1_z, dwfpre = _rms_bwd(dz, h1n, r, wfpre)
        dwfpre_ref[...] += dwfpre
        dh1 = dh2_ref[...] + dh1_z
        dh1_ref[...] = dh1
        wpost = wpost_ref[...]
        _, y1n, r1 = _rms_fwd(y1_ref[...], wpost)
        dy1, dwpost = _rms_bwd(dh1, y1n, r1, wpost)
        dwpost_ref[...] += dwpost
        dmix = _mm_nt(dy1, wout_ref[...])
        dwout_ref[...] += _mm_tn(mix_ref[...], dy1)
        wmla = wmla_ref[...]
        o = o_ref[...]
        _, on, ro = _grms_fwd(o, wmla, MLA_V)
        d_o, dwmla = _grms_bwd(dmix[:, :MLA_WIDTH], on, ro, wmla, MLA_V)
        dwmla_ref[...] += dwmla
        do_ref[...] = d_o.astype(do_ref.dtype)
        hh = lax.broadcasted_iota(jnp.int32, (MLA_HEADS, MLA_WIDTH), 0)
        ll = lax.broadcasted_iota(jnp.int32, (MLA_HEADS, MLA_WIDTH), 1)
        sel = jnp.where((ll >= hh * MLA_V) & (ll < (hh + 1) * MLA_V), 1.0, 0.0)
        dvec_ref[...] = _mm_nt(sel, d_o * o, True)
        whg = whg_ref[...]
        hg = hg_ref[...]
        sg = jax.nn.sigmoid(hg)
        _, ohn, rh = _grms_fwd(oh_ref[...], whg, HGRN_DIM)
        dmh = dmix[:, MLA_WIDTH:]
        dhg_ref[...] = dmh * ohn * whg * sg * (1.0 + hg * (1.0 - sg))
        d_oh, dwhg = _grms_bwd(dmh * (hg * sg), ohn, rh, whg, HGRN_DIM)
        dwhg_ref[...] += dwhg
        doh_ref[...] = d_oh

        if nsw:
            @pl.when(pl.program_id(0) == T // tt - 1)
            def _():
                for cp in swap_copies():
                    cp.wait()

    row = lambda w: pl.BlockSpec((tt, w), lambda i: (i, 0))
    full = lambda a: pl.BlockSpec(a.shape, lambda i: (0,) * a.ndim)
    vec = lambda w: pl.BlockSpec((1, w), lambda i: (0, 0))
    sds = jax.ShapeDtypeStruct
    return pl.pallas_call(
        body, name="mid_bwd", grid=(T // tt,),
        in_specs=[pl.BlockSpec((N_CHIPS, tt, D_MODEL), lambda i: (0, i, 0)), row(D_MODEL), row(D_MODEL), row(D_MODEL),
                  row(D_MODEL), row(MLA_WIDTH), row(HGRN_WIDTH), pl.BlockSpec((tt, HGRN_WIDTH), lambda i: (i, 3)),
                  full(wout), vec(D_MODEL), vec(D_MODEL), vec(MLA_WIDTH), vec(HGRN_WIDTH)] + [ANY] * nsw,
        out_specs=[row(D_MODEL), full(wout), row(MLA_WIDTH), row(HGRN_WIDTH), row(HGRN_WIDTH),
                   pl.BlockSpec((MLA_HEADS, tt), lambda i: (0, i)),
                   vec(D_MODEL), vec(D_MODEL), vec(MLA_WIDTH), vec(HGRN_WIDTH)] + [ANY] * nsw,
        out_shape=[sds((T, D_MODEL), F32), sds(wout.shape, F32), sds((T, MLA_WIDTH), MXU_DTYPE), sds((T, HGRN_WIDTH), F32),
                   sds((T, HGRN_WIDTH), F32), sds((MLA_HEADS, T), F32),
                   sds((1, D_MODEL), F32), sds((1, D_MODEL), F32), sds((1, MLA_WIDTH), F32), sds((1, HGRN_WIDTH), F32)]
        + _half_stack_shapes(swap),
        scratch_shapes=[pltpu.SemaphoreType.DMA((nsw,)), pltpu.SemaphoreType.DMA((nsw,))] if nsw else [],
        compiler_params=_params(("arbitrary",)),
    )(dzp, dh2, h1, y1, mixb, o_raw, oh_raw, xph, wout, w_fpre, w_post, w_mla, w_hg, *swap)


def _in_bwd(x, dh1, cq, ckv, dq, dk, dv, dhq, dhf, dhi, dhg, rc, rs, w_pre, win, qnw, wq, kvnw, wk, wv, tt=256):
    T = x.shape[0]

    def body(x_ref, dh1_ref, cq_ref, ckv_ref, dq_ref, dk_ref, dv_ref, dhq_ref, dhf_ref, dhi_ref, dhg_ref, rc_ref, rs_ref,
             wpre_ref, win_ref, qnw_ref, wq_ref, kvnw_ref, wk_ref, wv_ref,
             dx_ref, dwin_ref, dwq_ref, dwk_ref, dwv_ref, dwpre_ref, dqnw_ref, dkvnw_ref):
        @pl.when(pl.program_id(0) == 0)
        def _():
            for r in (dwin_ref, dwq_ref, dwk_ref, dwv_ref, dwpre_ref, dqnw_ref, dkvnw_ref):
                r[...] = jnp.zeros_like(r)

        def add_win_grad(r, first):
            for arr0, n, chip, row0 in _win_grad_segments():
                if first <= arr0 and arr0 + n <= first + r.shape[0]:
                    dwin_ref[chip, row0:row0 + n, :] += r[arr0 - first:arr0 - first + n]

        lo = Q_RANK + KV_RANK + HEAD_PAD
        dxp_h = jnp.concatenate([dhq_ref[...], dhf_ref[...], dhi_ref[...], dhg_ref[...]], axis=-1)
        du = _mm(dxp_h, win_ref[lo:, :])
        wpre = wpre_ref[...]
        u, xn, rx = _rms_fwd(x_ref[...], wpre)
        add_win_grad(_mm_tn(dxp_h, u), lo)
        c, sa, sb = _rope_tables(rc_ref[...], rs_ref[...])
        lane = lax.broadcasted_iota(jnp.int32, (tt, HEAD_PAD), 1)
        dk_all = dk_ref[...]
        dq_lin = []
        dkr = jnp.zeros((tt, HEAD_PAD), F32)
        for h in range(MLA_HEADS):
            sl = slice(HEAD_PAD * h, HEAD_PAD * (h + 1))
            dq_lin.append(_rope_bwd(dq_ref[sl, :].T * ATTN_SCALE, c, sa, sb))
            dkr = dkr + dk_all[:, sl]
        dq_lin = jnp.concatenate(dq_lin, axis=-1)
        dkr = jnp.where((lane >= MLA_NOPE) & (lane < MLA_QK), _rope_bwd(dkr, c, sa, sb), 0.0)
        qnw = qnw_ref[...]
        qn, cqn, rq = _rms_fwd(cq_ref[...], qnw)
        dwq_ref[...] += _mm_tn(qn, dq_lin)
        dcq, dqnw = _rms_bwd(_mm_nt(dq_lin, wq_ref[...]), cqn, rq, qnw)
        dqnw_ref[...] += dqnw
        kvnw = kvnw_ref[...]
        kvn, ckvn, rkv = _rms_fwd(ckv_ref[...], kvnw)
        dv_ = dv_ref[...]
        dwk_ref[...] += _mm_tn(kvn, dk_all)
        dwv_ref[...] += _mm_tn(kvn, dv_)
        dckv, dkvnw = _rms_bwd(_mm_nt(dk_all, wk_ref[...]) + _mm_nt(dv_, wv_ref[...]), ckvn, rkv, kvnw)
        dkvnw_ref[...] += dkvnw
        dxp_a = jnp.concatenate([dcq, dckv, dkr], axis=-1)
        add_win_grad(_mm_tn(dxp_a, u), 0)
        dx_u, dwpre = _rms_bwd(du + _mm(dxp_a, win_ref[:lo, :]), xn, rx, wpre)
        dwpre_ref[...] += dwpre
        dx_ref[...] = dh1_ref[...] + dx_u

    row = lambda w: pl.BlockSpec((tt, w), lambda i: (i, 0))
    full = lambda a: pl.BlockSpec(a.shape, lambda i: (0,) * a.ndim)
    sds = jax.ShapeDtypeStruct
    qk_w = MLA_HEADS * HEAD_PAD
    return pl.pallas_call(
        body, name="in_bwd", grid=(T // tt,),
        in_specs=[row(D_MODEL), row(D_MODEL), row(Q_RANK), row(KV_RANK), pl.BlockSpec((qk_w, tt), lambda i: (0, i)),
                  row(qk_w), row(MLA_WIDTH),
                  row(HGRN_WIDTH), row(HGRN_WIDTH), row(HGRN_WIDTH), row(HGRN_WIDTH), row(HEAD_PAD), row(HEAD_PAD),
                  full(w_pre), full(win), full(qnw), full(wq), full(kvnw), full(wk), full(wv)],
        out_specs=[row(D_MODEL), pl.BlockSpec(WIN_COMM_SHAPE, lambda i: (0, 0, 0)), full(wq), full(wk), full(wv),
                   full(w_pre), full(qnw), full(kvnw)],
        out_shape=[sds((T, D_MODEL), F32), sds(WIN_COMM_SHAPE, F32), sds(wq.shape, F32), sds(wk.shape, F32),
                   sds(wv.shape, F32), sds(w_pre.shape, F32), sds(qnw.shape, F32), sds(kvnw.shape, F32)],
        compiler_params=_params(("arbitrary",)),
    )(x, dh1, cq, ckv, dq, dk, dv, dhq, dhf, dhi, dhg, rc, rs, w_pre, win, qnw, wq, kvnw, wk, wv)


def _arrange_weights(win_t, wuq_full, wukv):
    dt = win_t.dtype
    z = lambda n: jnp.zeros((n, D_MODEL), dt)
    s2 = Q_RANK + KV_RANK
    win_arr = jnp.concatenate([win_t[:s2], z(MLA_NOPE), win_t[s2:s2 + MLA_ROPE], z(HEAD_PAD - MLA_QK),
                               win_t[s2 + MLA_ROPE:]], axis=0)
    wq_arr = jnp.pad(wuq_full, ((0, 0), (0, 0), (0, HEAD_PAD - MLA_QK))).reshape(Q_RANK, MLA_HEADS * HEAD_PAD)
    wk_arr = jnp.pad(wukv[:, :, :MLA_NOPE], ((0, 0), (0, 0), (0, HEAD_PAD - MLA_NOPE))).reshape(
        KV_RANK, MLA_HEADS * HEAD_PAD)
    wv_arr = wukv[:, :, MLA_NOPE:].reshape(KV_RANK, MLA_WIDTH)
    return win_arr, wq_arr, wk_arr, wv_arr


WIN_COMM_SHAPE = (N_CHIPS, FF_SHARD, D_MODEL)


def _win_grad_segments():
    s2 = Q_RANK + KV_RANK
    runs = [(0, s2, 0), (s2, s2 + MLA_ROPE, MLA_NOPE), (s2 + MLA_ROPE, D_IN, HEAD_PAD - MLA_ROPE)]
    per = D_IN // N_CHIPS
    segs = []
    for lo, hi, shift in runs:
        for k in range(N_CHIPS):
            a, b = max(lo, per * k), min(hi, per * (k + 1))
            if a < b:
                segs.append((a + shift, b - a, k, a - per * k))
    return segs


def _unarrange_grads(dwq_arr, dwk_arr, dwv_arr):
    dwuq = dwq_arr.reshape(Q_RANK, MLA_HEADS, HEAD_PAD)[:, :, :MLA_QK]
    dwukv = jnp.concatenate([dwk_arr.reshape(KV_RANK, MLA_HEADS, HEAD_PAD)[:, :, :MLA_NOPE],
                             dwv_arr.reshape(KV_RANK, MLA_HEADS, MLA_V)], axis=-1)
    return dwuq, dwukv


def _rope_inv_freq():
    inv = 1.0 / (ROPE_THETA ** (jnp.arange(0, MLA_ROPE, 2, dtype=F32) / MLA_ROPE))
    z = lambda n: jnp.zeros((n,), F32)
    return jnp.concatenate([z(MLA_NOPE), inv, inv, z(HEAD_PAD - MLA_QK)]).reshape(1, HEAD_PAD)


def _local_step(x, pos, tgt, small, win_arr, wq_arr, wk_arr, wv_arr, late, place=None):
    invf = _rope_inv_freq()
    cq, ckv, xph, qb, kb, vb, kt, vt, rc, rs = _in_fwd(x, pos, invf, small["attn_pre_norm"], win_arr, small["mla_q_norm"],
                                               wq_arr, small["mla_kv_norm"], wk_arr, wv_arr)
    if place is None:
        o_raw, lse = _attn_fwd_t(qb, kb, vt)
        wout, wg, wu, wd = late
    else:
        o_raw, lse, *stacks = _attn_fwd_t(qb, kb, vt, gather=late)
        wout, wg, wu, wd = [lax.dynamic_update_slice(s, l[None], (place[1], 0, 0)) for s, l in zip(stacks, late)]
        wout = wout.reshape(D_MODEL, D_MODEL)
    oh_raw, states = _hgrn_fwd(xph, small["hgrn_lb_logits"])
    h1, y1, zb, mixb = _proj_fwd(x, o_raw, oh_raw, xph, wout, small["mla_out_norm"], small["hgrn_out_norm"],
                                 small["attn_post_norm"], small["ffn_pre_norm"])
    g, up, dy2b, dh2, loss_acc, d_fpost = _ffn_fwd(zb, h1, tgt, small["ffn_post_norm"], wg, wu, wd)
    dwg, dwu, dwd, dzp = _ffn_bwd(zb, g, up, dy2b, wg, wu, wd)
    ffn_grads = [] if place is None else [dwg, dwu, dwd]
    dh1, dwout, d_o, d_oh, dhg, dvec, d_fpre, d_post, d_mla, d_hg, *ffn_rs = _mid_bwd(
        dzp, dh2, h1, y1, mixb, o_raw, oh_raw, xph, wout, small["ffn_pre_norm"], small["attn_post_norm"],
        small["mla_out_norm"], small["hgrn_out_norm"], swap=ffn_grads)
    ffn_ps = _pair_sum(place, ffn_grads, ffn_rs, name="pair_sum_ffn") if ffn_grads else []
    dq, dk, dv, *ffn_ris = _attn_bwd_t(qb, kb, kt, vb, d_o, lse, dvec.reshape(lse.shape), send=ffn_ps)
    ffn_sums = _chip_sum(place, ffn_grads, ffn_rs, ffn_ris, name="chip_sum_ffn") if ffn_grads else []
    dhq, dhf, dhi, d_lbl, *ffn_final = _hgrn_bwd(xph, small["hgrn_lb_logits"], states, d_oh, fill=ffn_sums)
    dx, dwin4, dwq_arr, dwk_arr, dwv_arr, d_pre, d_qn, d_kvn = _in_bwd(
        x, dh1, cq, ckv, dq, dk, dv, dhq, dhf, dhi, dhg, rc, rs, small["attn_pre_norm"], win_arr,
        small["mla_q_norm"], wq_arr, small["mla_kv_norm"], wk_arr, wv_arr)
    dwuq, dwukv = _unarrange_grads(dwq_arr, dwk_arr, dwv_arr)
    loss = 0.5 * jnp.sum(loss_acc) * (1.0 / D_MODEL)
    grads = dict(attn_pre_norm=d_pre, w_in=dwin4, mla_q_norm=d_qn, mla_w_uq=dwuq, mla_kv_norm=d_kvn, mla_w_ukv=dwukv,
                 mla_out_norm=d_mla, hgrn_lb_logits=d_lbl, hgrn_out_norm=d_hg, w_out=dwout, attn_post_norm=d_post,
                 ffn_pre_norm=d_fpre, w_gate=dwg, w_up=dwu, w_down=dwd, ffn_post_norm=d_fpost)
    if place is None:
        return loss, dx, grads
    return loss, dx, grads, ffn_final


def _place():
    x, y, c = lax.axis_index("x"), lax.axis_index("y"), lax.axis_index("c")
    others = [(1 - x, y), (x, 1 - y), (1 - x, 1 - y)]
    return x, y, c, 2 * x + y, (x, y, 1 - c), others


def _half(ref, c, rows):
    return ref.at[pl.ds(pl.multiple_of(c * rows, 8), rows)]


def _rcopy(src, dst, send, recv, k, to):
    return pltpu.make_async_remote_copy(src_ref=src, dst_ref=dst, send_sem=send.at[k], recv_sem=recv.at[k],
                                        device_id=to, device_id_type=MESH)


class _Gather:
    def __init__(self, ins, outs, send, recv):
        self.ins, self.outs, self.send, self.recv = ins, outs, send, recv
        self.n = len(ins)
        self.halves = [r.shape[0] // 2 for r in ins]
        _, _, self.c, self.me, self.sib, self.others = _place()

    def _each(self):
        for j, (px, py) in enumerate(self.others):
            for a in range(self.n):
                yield j * self.n + a, a, 2 * px + py, (px, py, self.c)

    def sends(self):
        return [_rcopy(_half(self.ins[a], self.c, self.halves[a]), _half(self.outs[a].at[self.me], self.c, self.halves[a]),
                       self.send, self.recv, k, to) for k, a, _, to in self._each()]

    def arrivals(self):
        parts = [(k, _half(self.outs[a].at[chip], self.c, self.halves[a]), to) for k, a, chip, to in self._each()]
        return [_rcopy(p, p, self.send, self.recv, k, to) for k, p, to in parts]

    def forwards(self):
        parts = [(k, _half(self.outs[a].at[chip], self.c, self.halves[a])) for k, a, chip, _ in self._each()]
        return [_rcopy(p, p, self.send, self.recv, 3 * self.n + k, self.sib) for k, p in parts]

    def forward_arrivals(self):
        parts = [(k, _half(self.outs[a].at[chip], 1 - self.c, self.halves[a])) for k, a, chip, _ in self._each()]
        return [_rcopy(p, p, self.send, self.recv, 3 * self.n + k, self.sib) for k, p in parts]

    @staticmethod
    def out_shapes(arrs):
        return [jax.ShapeDtypeStruct((N_CHIPS,) + a.shape, a.dtype) for a in arrs]

    @staticmethod
    def semaphores(arrs):
        return [pltpu.SemaphoreType.DMA((6 * len(arrs),)), pltpu.SemaphoreType.DMA((6 * len(arrs),))]


def _gather_chips(arrs, name):
    n = len(arrs)

    def body(*refs):
        gat = _Gather(refs[:n], refs[n:2 * n], *refs[2 * n:])
        sends, forwards = gat.sends(), gat.forwards()
        for cp in sends:
            cp.start()
        for arrival, fw in zip(gat.arrivals(), forwards):
            arrival.wait_recv()
            fw.start()
        for arrival in gat.forward_arrivals():
            arrival.wait_recv()
        for cp in sends + forwards:
            cp.wait_send()

    return pl.pallas_call(body, name=name, in_specs=[ANY] * n, out_specs=[ANY] * n, out_shape=_Gather.out_shapes(arrs),
                          scratch_shapes=_Gather.semaphores(arrs))(*arrs)


GRAD_BLOCKS = 2


def _pair_swap_copies(g_refs, r_refs, send, recv):
    _, _, c, _, sib, _ = _place()
    copies = []
    for a, (g, r) in enumerate(zip(g_refs, r_refs)):
        h = g.shape[1] // 2
        copies.append(_rcopy(g.at[:, pl.ds(pl.multiple_of((1 - c) * h, 8), h)], r, send, recv, a, sib))
    return copies


def _half_stack_shapes(gs, dtype=None):
    return [jax.ShapeDtypeStruct((N_CHIPS, g.shape[1] // 2, g.shape[2]), dtype or g.dtype) for g in gs]


def _pair_swap(gs, sm):
    n = len(gs)

    def body(*refs):
        g_refs, sm_ref = refs[:n], refs[n]
        r_refs, ssib_ref = refs[n + 1:2 * n + 1], refs[2 * n + 1]
        send, recv = refs[2 * n + 2:]
        copies = _pair_swap_copies(g_refs, r_refs, send, recv)
        copies.append(_rcopy(sm_ref, ssib_ref, send, recv, n, _place()[4]))
        for cp in copies:
            cp.start()
        for cp in copies:
            cp.wait()

    return pl.pallas_call(
        body, name="pair_swap", in_specs=[ANY] * (n + 1), out_specs=[ANY] * (n + 1),
        out_shape=_half_stack_shapes(gs) + [jax.ShapeDtypeStruct(sm.shape, sm.dtype)],
        scratch_shapes=[pltpu.SemaphoreType.DMA((n + 1,)), pltpu.SemaphoreType.DMA((n + 1,))],
    )(*gs, sm)


def _pair_sum(place, gs, rs, small=None, name="pair_sum"):
    n = len(gs)
    nb = GRAD_BLOCKS

    def body(place_ref, *refs):
        g_refs, r_refs, p_refs = refs[:n], refs[n:2 * n], refs[-n - 1:-1] if small else refs[-n:]
        for a in range(n):
            p_refs[a][0] = (g_refs[a][0] + r_refs[a][0]).astype(p_refs[a].dtype)
        if small:
            @pl.when((pl.program_id(0) == 0) & (pl.program_id(1) == 0))
            def _():
                refs[-1][...] = refs[2 * n][...] + refs[2 * n + 1][...]

    in_specs, out_specs = [], []
    for g in gs:
        blk = (1, g.shape[1] // 2 // nb, g.shape[2])
        in_specs.append(pl.BlockSpec(blk, lambda i, k, p: (k, p[0] * nb + i, 0)))
    for g in gs:
        blk = (1, g.shape[1] // 2 // nb, g.shape[2])
        in_specs.append(pl.BlockSpec(blk, lambda i, k, p: (k, i, 0)))
        out_specs.append(pl.BlockSpec(blk, lambda i, k, p: (k, i, 0)))
    out_shape = _half_stack_shapes(gs, BF16)
    if small:
        sm_spec = pl.BlockSpec(small[0].shape, lambda i, k, p: (0, 0))
        in_specs += [sm_spec, sm_spec]
        out_specs.append(sm_spec)
        out_shape.append(jax.ShapeDtypeStruct(small[0].shape, F32))
    return pl.pallas_call(
        body, name=name,
        grid_spec=pltpu.PrefetchScalarGridSpec(num_scalar_prefetch=1, grid=(nb, N_CHIPS), in_specs=in_specs,
                                               out_specs=out_specs),
        out_shape=out_shape,
        compiler_params=_params(("arbitrary", "arbitrary")),
    )(place, *gs, *rs, *(small or ()))


def _chip_swap_copies(p_refs, ri_refs, send, recv):
    _, _, c, _, _, others = _place()
    n = len(p_refs)
    return [_rcopy(p_refs[a].at[2 * px + py], ri_refs[a].at[j], send, recv, j * n + a, (px, py, c))
            for j, (px, py) in enumerate(others) for a in range(n)]


def _chip_swap_shapes(ps):
    return [jax.ShapeDtypeStruct((3,) + p.shape[1:], p.dtype) for p in ps]


def _chip_swap(ps, pair):
    n = len(ps)

    def body(*refs):
        start, finish = _chip_swap_plan(refs[:n], refs[n], refs[n + 1:2 * n + 1], refs[2 * n + 1], *refs[2 * n + 2:])
        start()
        finish()

    return pl.pallas_call(
        body, name="chip_swap", in_specs=[ANY] * (n + 1), out_specs=[ANY] * (n + 1),
        out_shape=_chip_swap_out_shapes(ps, pair), scratch_shapes=_chip_swap_semaphores(n),
    )(*ps, pair)


def _chip_swap_plan(p_refs, pair_ref, ri_refs, sm4_ref, send, recv, lsem):
    n = len(p_refs)
    hs = SMALL_ROWS // 2
    x, y, c, me, sib, others = _place()
    local = pltpu.make_async_copy(pair_ref, sm4_ref.at[me], lsem.at[0])
    copies = _chip_swap_copies(p_refs, ri_refs, send, recv)
    arrivals = list(copies)
    for j, (px, py) in enumerate(others):
        copies.append(_rcopy(_half(pair_ref, c, hs), _half(sm4_ref.at[me], c, hs), send, recv, 3 * n + j, (px, py, c)))
        part = _half(sm4_ref.at[2 * px + py], c, hs)
        arrivals.append(_rcopy(part, part, send, recv, 3 * n + j, (px, py, c)))

    def start():
        local.start()
        for cp in copies:
            cp.start()

    def finish():
        for arrival in arrivals:
            arrival.wait_recv()
        for cp in copies:
            cp.wait_send()
        local.wait()

    return start, finish


def _chip_swap_out_shapes(ps, pair):
    return _chip_swap_shapes(ps) + [jax.ShapeDtypeStruct((N_CHIPS,) + pair.shape, pair.dtype)]


def _chip_swap_semaphores(n):
    k = 3 * (n + 1)
    return [pltpu.SemaphoreType.DMA((k,)), pltpu.SemaphoreType.DMA((k,)), pltpu.SemaphoreType.DMA((1,))]


def _chip_sum(place, gs, rs, ris, name="chip_sum"):
    n = len(gs)
    nb = GRAD_BLOCKS

    def body(place_ref, *refs):
        g_refs, r_refs, ri_refs, o_refs = refs[:n], refs[n:2 * n], refs[2 * n:3 * n], refs[3 * n:]
        for a in range(n):
            ri = ri_refs[a]
            o_refs[a][...] = (g_refs[a][0] + r_refs[a][0]) + ri[0].astype(F32) + ri[1].astype(F32) + ri[2].astype(F32)

    in_specs, out_specs, out_shape = [], [], []
    for g in gs:
        blk = (1, g.shape[1] // 2 // nb, g.shape[2])
        in_specs.append(pl.BlockSpec(blk, lambda i, p: (p[1], p[0] * nb + i, 0)))
    for g in gs:
        blk = (1, g.shape[1] // 2 // nb, g.shape[2])
        in_specs.append(pl.BlockSpec(blk, lambda i, p: (p[1], i, 0)))
    for g in gs:
        rb = g.shape[1] // 2 // nb
        in_specs.append(pl.BlockSpec((3, rb, g.shape[2]), lambda i, p: (0, i, 0)))
        out_specs.append(pl.BlockSpec((rb, g.shape[2]), lambda i, p: (p[0] * nb + i, 0)))
        out_shape.append(jax.ShapeDtypeStruct(g.shape[1:], F32))
    return pl.pallas_call(
        body, name=name,
        grid_spec=pltpu.PrefetchScalarGridSpec(num_scalar_prefetch=1, grid=(nb,), in_specs=in_specs, out_specs=out_specs),
        out_shape=out_shape,
        compiler_params=_params(("arbitrary",)),
    )(place, *gs, *rs, *ris)


def _pair_fill_copies(g_refs, send, recv):
    _, _, c, _, sib, _ = _place()
    copies, waits = [], []
    for a, g in enumerate(g_refs):
        h = g.shape[0] // 2
        mine, theirs = _half(g, c, h), _half(g, 1 - c, h)
        copies.append(_rcopy(mine, mine, send, recv, a, sib))
        waits.append(_rcopy(theirs, theirs, send, recv, a, sib))
    return copies, waits


def _pair_fill(gfs, sm4):
    n = len(gfs)
    hs = SMALL_ROWS // 2

    def body(*refs):
        g_refs, sm4_ref = refs[n + 1:2 * n + 1], refs[2 * n + 1]
        send, recv = refs[2 * n + 2:]
        x, y, c, me, sib, others = _place()
        copies, waits = _pair_fill_copies(g_refs, send, recv)
        for j, (px, py) in enumerate(others):
            chip = 2 * px + py
            mine, theirs = _half(sm4_ref.at[chip], c, hs), _half(sm4_ref.at[chip], 1 - c, hs)
            copies.append(pltpu.make_async_remote_copy(src_ref=mine, dst_ref=mine, send_sem=send.at[n + j],
                                                       recv_sem=recv.at[n + j], device_id=sib, device_id_type=MESH))
            waits.append(pltpu.make_async_remote_copy(src_ref=theirs, dst_ref=theirs, send_sem=send.at[n + j],
                                                      recv_sem=recv.at[n + j], device_id=sib, device_id_type=MESH))
        for cp in copies:
            cp.start()
        for w in waits:
            w.wait_recv()
        for cp in copies:
            cp.wait_send()

    return pl.pallas_call(
        body, name="pair_fill", in_specs=[ANY] * (n + 1), out_specs=[ANY] * (n + 1),
        out_shape=[jax.ShapeDtypeStruct(g.shape, g.dtype) for g in gfs] + [jax.ShapeDtypeStruct(sm4.shape, sm4.dtype)],
        input_output_aliases={i: i for i in range(n + 1)},
        scratch_shapes=[pltpu.SemaphoreType.DMA((n + 3,)), pltpu.SemaphoreType.DMA((n + 3,))],
    )(*gfs, sm4)


def _adamw_math(w, g, m, v):
    m = ADAM_B1 * m + (1.0 - ADAM_B1) * g
    v = ADAM_B2 * v + (1.0 - ADAM_B2) * (g * g)
    m_hat = m / (1.0 - ADAM_B1 ** ADAM_STEP)
    v_hat = v / (1.0 - ADAM_B2 ** ADAM_STEP)
    return -ADAM_LR * (m_hat / (jnp.sqrt(v_hat) + ADAM_EPS) + ADAM_WD * w), m, v


def _adamw(items, steps, name):
    n = len(items)

    def body(*refs):
        for a in range(n):
            g = refs[4 * a + 1][...]
            d, mo, vo = _adamw_math(refs[4 * a][...], g, refs[4 * a + 2][...], refs[4 * a + 3][...])
            for out, val in zip(refs[4 * n + 4 * a:4 * n + 4 * a + 4], (g, d, mo, vo)):
                out[...] = val

    spec = lambda w: pl.BlockSpec((w.shape[0] // steps, w.shape[1]), lambda i: (i, 0))
    flat = pl.pallas_call(
        body, name=name, grid=(steps,), in_specs=[spec(it[0]) for it in items for _ in range(4)],
        out_specs=[spec(it[0]) for it in items for _ in range(4)],
        out_shape=[jax.ShapeDtypeStruct(it[0].shape, F32) for it in items for _ in range(4)],
        compiler_params=_params(("arbitrary",)),
    )(*[a for it in items for a in it])
    return [flat[4 * a:4 * a + 4] for a in range(n)]


def _adamw_small(sm4, wmv):
    views = SMALL_VIEWS[:-1]
    n = len(views)

    def body(sm4_ref, *refs):
        g_all = ((sm4_ref[0] + sm4_ref[1]) + sm4_ref[2]) + sm4_ref[3]
        row = 0
        for a, (_, rows, cols) in enumerate(views):
            g = g_all[row:row + rows, :cols]
            row += -(-rows // ROW_TILE) * ROW_TILE
            d, mo, vo = _adamw_math(refs[3 * a][...], g, refs[3 * a + 1][...], refs[3 * a + 2][...])
            for out, val in zip(refs[3 * n + 4 * a:3 * n + 4 * a + 4], (g, d, mo, vo)):
                out[...] = val
        refs[-1][...] = g_all[row:row + 1, :128]

    flat = pl.pallas_call(
        body, name="adamw_small",
        out_shape=[jax.ShapeDtypeStruct((rows, cols), F32) for _, rows, cols in views for _ in range(4)]
        + [jax.ShapeDtypeStruct((1, 128), F32)],
        compiler_params=pltpu.CompilerParams(vmem_limit_bytes=VMEM_LIMIT),
    )(sm4, *[a for t in wmv for a in t])
    return [flat[4 * a:4 * a + 4] for a in range(n)] + [flat[-1]]


SMALL_NAMES = ("attn_pre_norm", "mla_q_norm", "mla_kv_norm", "mla_w_ukv", "mla_out_norm", "hgrn_lb_logits",
               "hgrn_out_norm", "attn_post_norm", "ffn_pre_norm", "ffn_post_norm")
BIG_NAMES = ("w_in", "mla_w_uq", "w_out", "w_gate", "w_up", "w_down")
WEIGHT_NAMES = ("attn_pre_norm", "w_in", "mla_q_norm", "mla_w_uq", "mla_kv_norm", "mla_w_ukv", "mla_out_norm",
                "hgrn_lb_logits", "hgrn_out_norm", "w_out", "attn_post_norm", "ffn_pre_norm", "w_gate", "w_up", "w_down",
                "ffn_post_norm")


UQ_COMM_SHAPE = (192, 384)


def _pack_small(vals):
    parts = []
    for name, rows, cols in SMALL_VIEWS:
        pad_rows = -(-rows // ROW_TILE) * ROW_TILE - rows
        parts.append(jnp.pad(vals[name].reshape(rows, cols), ((0, pad_rows), (0, D_MODEL - cols))))
    return jnp.concatenate(parts, axis=0)


def kernel(x, positions, attn_pre_norm, w_in, mla_q_norm, mla_w_uq, mla_kv_norm, mla_w_ukv, mla_out_norm, hgrn_lb_logits, hgrn_out_norm, w_out, attn_post_norm, ffn_pre_norm, w_gate, w_up, w_down, ffn_post_norm, loss_target, m_attn_pre_norm, m_w_in, m_mla_q_norm, m_mla_w_uq, m_mla_kv_norm, m_mla_w_ukv, m_mla_out_norm, m_hgrn_lb_logits, m_hgrn_out_norm, m_w_out, m_attn_post_norm, m_ffn_pre_norm, m_w_gate, m_w_up, m_w_down, m_ffn_post_norm, v_attn_pre_norm, v_w_in, v_mla_q_norm, v_mla_w_uq, v_mla_kv_norm, v_mla_w_ukv, v_mla_out_norm, v_hgrn_lb_logits, v_hgrn_out_norm, v_w_out, v_attn_post_norm, v_ffn_pre_norm, v_w_gate, v_w_up, v_w_down, v_ffn_post_norm):
    args = locals()
    W = {n: args[n] for n in WEIGHT_NAMES}
    M = {n: args["m_" + n] for n in WEIGHT_NAMES}
    V = {n: args["v_" + n] for n in WEIGHT_NAMES}
    T = x.shape[1]
    cx, cy, cc = lax.axis_index("x"), lax.axis_index("y"), lax.axis_index("c")

    win_rows = D_IN // N_CHIPS
    shard2d = {"w_in": (win_rows, D_MODEL), "mla_w_uq": (Q_RANK // N_CHIPS, MLA_HEADS * MLA_QK),
               "w_out": (D_MODEL // N_CHIPS, D_MODEL), "w_gate": (FF_SHARD, D_MODEL), "w_up": (FF_SHARD, D_MODEL),
               "w_down": (FF_SHARD, D_MODEL)}
    transposed = ("w_in", "w_gate", "w_up")
    to2d = lambda n, a: a[0].T if n in transposed else a.reshape(shard2d[n])
    from2d = lambda n, t: t.T[None] if n in transposed else t.reshape(W[n].shape)
    me = 2 * cx + cy
    place = jnp.stack([cc, me]).astype(jnp.int32)
    local_b = [to2d(n, W[n]).astype(BF16) for n in BIG_NAMES]
    local_b[0] = jnp.pad(local_b[0], ((0, FF_SHARD - win_rows), (0, 0)))
    stacks = _gather_chips(local_b[:2], "gather_weights")
    win4, wuq4 = [lax.dynamic_update_slice(s, l[None], (me, 0, 0)) for s, l in zip(stacks, local_b)]
    win_t = win4[:, :win_rows].reshape(D_IN, D_MODEL)
    wuq_full = wuq4.reshape(Q_RANK, MLA_HEADS, MLA_QK)
    win_arr, wq_arr, wk_arr, wv_arr = _arrange_weights(win_t, wuq_full, mla_w_ukv[0].astype(BF16))
    small = {n: W[n][0] if n == "mla_w_ukv" else W[n].reshape(-1, W[n].shape[-1]) for n in SMALL_NAMES}

    loss_local, dx, grads, ffn_final = _local_step(x[0], positions.reshape(T, 1), loss_target[0], small, win_arr,
                                                           wq_arr, wk_arr, wv_arr, local_b[2:], place)

    gs = [grads["w_in"], grads["mla_w_uq"].reshape((N_CHIPS,) + UQ_COMM_SHAPE), grads["w_out"].reshape((N_CHIPS,) + shard2d["w_out"])]
    sm = _pack_small({**grads, "loss": loss_local})
    *rs, ssib = _pair_swap(gs, sm)
    *ps, pair = _pair_sum(place, gs, rs, small=(sm, ssib))
    ffn_names, rest_names = BIG_NAMES[3:], BIG_NAMES[:3]
    g2d = dict(zip(ffn_names, ffn_final))
    adam_in = lambda names_: [(to2d(n, W[n]), g2d[n], to2d(n, M[n]), to2d(n, V[n])) for n in names_]
    updates = dict(zip(ffn_names, _adamw(adam_in(ffn_names), 8, "adamw_ffn")))
    *ris, sm4 = _chip_swap(ps, pair)
    *gfin, smf = _pair_fill(_chip_sum(place, gs, rs, ris), sm4)

    g2d.update({n: gfin[k].reshape((-1,) + shard2d[n][1:]) for k, n in enumerate(rest_names)})
    updates.update(zip(rest_names[:2], _adamw(adam_in(rest_names[:2]), 3, "adamw_w_in")))
    updates.update(zip(rest_names[2:], _adamw(adam_in(rest_names[2:]), 8, "adamw_w_out")))
    G, DW, NM, NV = {}, {}, {}, {}
    for n, outs in updates.items():
        G[n], DW[n], NM[n], NV[n] = (from2d(n, t) for t in outs)
    view2d = lambda n, a: a.reshape(next((r, c) for name, r, c in SMALL_VIEWS if name == n))
    *res, loss_row = _adamw_small(smf, [tuple(view2d(n, t[n]) for t in (W, M, V)) for n in SMALL_NAMES])
    for n, outs in zip(SMALL_NAMES, res):
        G[n], DW[n], NM[n], NV[n] = (t.reshape(W[n].shape) for t in outs)
    loss = loss_row[0, 0]
    return (loss, dx[None], *[G[n] for n in WEIGHT_NAMES], *[DW[n] for n in WEIGHT_NAMES],
            *[NM[n] for n in WEIGHT_NAMES], *[NV[n] for n in WEIGHT_NAMES])
```

```python
import jax
import jax.numpy as jnp
from jax import lax
from jax.experimental import pallas as pl
from jax.experimental.pallas import tpu as pltpu

F32 = jnp.float32
BF16 = jnp.bfloat16
MXU_DTYPE = BF16

D_MODEL = 1024
MLA_HEADS = 8
MLA_NOPE = 64
MLA_ROPE = 32
MLA_V = 64
MLA_QK = MLA_NOPE + MLA_ROPE
Q_RANK = 384
KV_RANK = 128
MLA_WIDTH = MLA_HEADS * MLA_V
HEAD_PAD = 128
HGRN_HEADS = 4
HGRN_DIM = 128
HGRN_WIDTH = HGRN_HEADS * HGRN_DIM
CHUNK = 64
SUB = 16
HGRN_CPI = 4
D_IN = Q_RANK + KV_RANK + MLA_ROPE + 4 * HGRN_WIDTH
D_IN_ARR = Q_RANK + KV_RANK + HEAD_PAD + 4 * HGRN_WIDTH
D_FF = 2816
N_CHIPS = 4
FF_SHARD = D_FF // N_CHIPS
EPS = 1e-6
ROPE_THETA = 10000.0
ATTN_SCALE = MLA_QK ** -0.5
ATTN_SCALE_LOG2 = ATTN_SCALE * 1.4426950408889634
NEG_BIG = -1e30

ADAM_LR = 0.001
ADAM_B1 = 0.9
ADAM_B2 = 0.999
ADAM_EPS = 1e-08
ADAM_WD = 0.01
ADAM_STEP = 10

VMEM_LIMIT = 56 * 1024 * 1024

SMALL_VIEWS = (("attn_pre_norm", 1, 1024), ("mla_q_norm", 1, 384), ("mla_kv_norm", 1, 128), ("mla_w_ukv", 128, 1024),
               ("mla_out_norm", 1, 512), ("hgrn_lb_logits", 2, 512), ("hgrn_out_norm", 1, 512),
               ("attn_post_norm", 1, 1024), ("ffn_pre_norm", 1, 1024), ("ffn_post_norm", 1, 1024), ("loss", 1, 1))
ROW_TILE = 8
SMALL_ROWS = sum(-(-rows // ROW_TILE) * ROW_TILE for _, rows, _ in SMALL_VIEWS)

MESH = pl.DeviceIdType.MESH
ANY = pl.BlockSpec(memory_space=pl.ANY)


def _dot(a, b, dims, exact):
    if exact:
        return lax.dot_general(a.astype(F32), b.astype(F32), (dims, ((), ())), precision=lax.Precision.HIGH,
                               preferred_element_type=F32)
    return lax.dot_general(a.astype(MXU_DTYPE), b.astype(MXU_DTYPE), (dims, ((), ())), preferred_element_type=F32)


def _mm(a, b, exact=False):
    return _dot(a, b, ((1,), (0,)), exact)


def _mm_nt(a, b, exact=False):
    return _dot(a, b, ((1,), (1,)), exact)


def _mm_tn(a, b, exact=False):
    return _dot(a, b, ((0,), (0,)), exact)


def _rms_fwd(x, w):
    r = lax.rsqrt(jnp.mean(x * x, axis=-1, keepdims=True) + EPS)
    xn = x * r
    return xn * w, xn, r


def _rms_bwd(dy, xn, r, w):
    dxn = dy * w
    dx = r * (dxn - xn * jnp.mean(dxn * xn, axis=-1, keepdims=True))
    dw = jnp.sum(dy * xn, axis=0, keepdims=True)
    return dx, dw


def _group_sums(v, gs):
    t, n = v.shape
    lane = lax.broadcasted_iota(jnp.int32, (t, 128), 1)
    out = []
    for p in range(n // 128):
        vb = v[:, 128 * p:128 * (p + 1)]
        if gs == 128:
            out.append(jnp.sum(vb, axis=-1, keepdims=True))
        else:
            out.append(jnp.sum(jnp.where(lane < 64, vb, 0.0), axis=-1, keepdims=True))
            out.append(jnp.sum(jnp.where(lane >= 64, vb, 0.0), axis=-1, keepdims=True))
    return out


def _group_bcast(sums, gs, t):
    lane = lax.broadcasted_iota(jnp.int32, (t, 128), 1)
    if gs == 128:
        return jnp.concatenate([jnp.broadcast_to(s, (t, 128)) for s in sums], axis=-1)
    return jnp.concatenate([jnp.where(lane < 64, sums[2 * p], sums[2 * p + 1]) for p in range(len(sums) // 2)],
                           axis=-1)


def _grms_fwd(x, w, gs):
    t = x.shape[0]
    r = lax.rsqrt(_group_bcast(_group_sums(x * x, gs), gs, t) * (1.0 / gs) + EPS)
    xn = x * r
    return xn * w, xn, r


def _grms_bwd(dy, xn, r, w, gs):
    t = dy.shape[0]
    dxn = dy * w
    dx = r * (dxn - xn * (_group_bcast(_group_sums(dxn * xn, gs), gs, t) * (1.0 / gs)))
    dw = jnp.sum(dy * xn, axis=0, keepdims=True)
    return dx, dw


def _rope_tables(c_tab, s_tab):
    lane = lax.broadcasted_iota(jnp.int32, c_tab.shape, 1)
    first = (lane >= MLA_NOPE) & (lane < MLA_NOPE + MLA_ROPE // 2)
    second = (lane >= MLA_NOPE + MLA_ROPE // 2) & (lane < MLA_QK)
    return c_tab, jnp.where(first, -s_tab, 0.0), jnp.where(second, s_tab, 0.0)


def _rope(v, c, sa, sb):
    return v * c + pltpu.roll(v, HEAD_PAD - MLA_ROPE // 2, 1) * sa + pltpu.roll(v, MLA_ROPE // 2, 1) * sb


def _rope_bwd(d, c, sa, sb):
    return d * c - pltpu.roll(d, HEAD_PAD - MLA_ROPE // 2, 1) * sa - pltpu.roll(d, MLA_ROPE // 2, 1) * sb


def _params(sem, vmem=VMEM_LIMIT):
    return pltpu.CompilerParams(dimension_semantics=sem, vmem_limit_bytes=vmem)


def _in_fwd(x, pos, invf, w_pre, win, qnw, wq, kvnw, wk, wv, tt=512):
    T = x.shape[0]

    def body(x_ref, pos_ref, invf_ref, wpre_ref, win_ref, qnw_ref, wq_ref, kvnw_ref, wk_ref, wv_ref,
             cq_ref, ckv_ref, xph_ref, q_ref, k_ref, v_ref, kt_ref, vt_ref, rc_ref, rs_ref):
        u, _, _ = _rms_fwd(x_ref[...], wpre_ref[...])
        lo = Q_RANK + KV_RANK + HEAD_PAD
        xp = _mm_nt(u, win_ref[:lo, :])
        xph_ref[...] = _mm_nt(u, win_ref[lo:, :])
        cq = xp[:, :Q_RANK]
        ckv = xp[:, Q_RANK:Q_RANK + KV_RANK]
        kr = xp[:, Q_RANK + KV_RANK:]
        cq_ref[...] = cq
        ckv_ref[...] = ckv
        ang = pos_ref[...].astype(F32) * invf_ref[...]
        c_tab = jnp.cos(ang)
        s_tab = jnp.sin(ang)
        rc_ref[...] = c_tab
        rs_ref[...] = s_tab
        c, sa, sb = _rope_tables(c_tab, s_tab)
        qn, _, _ = _rms_fwd(cq, qnw_ref[...])
        q = _mm(qn, wq_ref[...])
        kvn, _, _ = _rms_fwd(ckv, kvnw_ref[...])
        kn = _mm(kvn, wk_ref[...])
        v = _mm(kvn, wv_ref[...])
        v_ref[...] = v.astype(v_ref.dtype)
        vt_ref[...] = v.T.astype(vt_ref.dtype)
        krr = _rope(kr, c, sa, sb)
        for h in range(MLA_HEADS):
            sl = slice(HEAD_PAD * h, HEAD_PAD * (h + 1))
            q_ref[:, sl] = (_rope(q[:, sl], c, sa, sb) * ATTN_SCALE_LOG2).astype(q_ref.dtype)
            kh = kn[:, sl] + krr
            k_ref[:, sl] = kh.astype(k_ref.dtype)
            kt_ref[sl, :] = kh.T.astype(kt_ref.dtype)

    row = lambda w: pl.BlockSpec((tt, w), lambda i: (i, 0))
    full = lambda a: pl.BlockSpec(a.shape, lambda i: (0,) * a.ndim)
    qk_w = MLA_HEADS * HEAD_PAD
    return pl.pallas_call(
        body, name="in_fwd", grid=(T // tt,),
        in_specs=[row(D_MODEL), row(1), full(invf), full(w_pre), full(win), full(qnw), full(wq), full(kvnw),
                  full(wk), full(wv)],
        out_specs=[row(Q_RANK), row(KV_RANK), row(4 * HGRN_WIDTH), row(qk_w), row(qk_w), row(MLA_WIDTH),
                   pl.BlockSpec((qk_w, tt), lambda i: (0, i)), pl.BlockSpec((MLA_WIDTH, tt), lambda i: (0, i)),
                   row(HEAD_PAD), row(HEAD_PAD)],
        out_shape=[jax.ShapeDtypeStruct((T, Q_RANK), F32), jax.ShapeDtypeStruct((T, KV_RANK), F32),
                   jax.ShapeDtypeStruct((T, 4 * HGRN_WIDTH), F32), jax.ShapeDtypeStruct((T, qk_w), MXU_DTYPE),
                   jax.ShapeDtypeStruct((T, qk_w), MXU_DTYPE), jax.ShapeDtypeStruct((T, MLA_WIDTH), MXU_DTYPE),
                   jax.ShapeDtypeStruct((qk_w, T), MXU_DTYPE), jax.ShapeDtypeStruct((MLA_WIDTH, T), MXU_DTYPE),
                   jax.ShapeDtypeStruct((T, HEAD_PAD), F32), jax.ShapeDtypeStruct((T, HEAD_PAD), F32)],
        compiler_params=_params(("arbitrary",)),
    )(x, pos, invf, w_pre, win, qnw, wq, kvnw, wk, wv)


def _attn_fwd_t(qb, kb, vt, gather=(), tq=256, hps=8):
    T = qb.shape[0]
    nq = T // tq
    ng = len(gather)
    steps = (MLA_HEADS // hps) * nq
    pass_on = steps - 3

    def body(q_ref, k_ref, vt_ref, *rest):
        o_ref, lse_ref = rest[ng:ng + 2]
        acc_scr = rest[2 * ng + 2]
        qi = pl.program_id(1)
        step_no = pl.program_id(0) * nq + qi
        if ng:
            gat = _Gather(rest[:ng], rest[ng + 2:2 * ng + 2], *rest[2 * ng + 3:])

            @pl.when(step_no == 0)
            def _():
                for cp in gat.sends():
                    cp.start()

            @pl.when(step_no == pass_on)
            def _():
                for arrival in gat.arrivals():
                    arrival.wait_recv()
                for cp in gat.forwards():
                    cp.start()

        heads = [slice(HEAD_PAD * a, HEAD_PAD * (a + 1)) for a in range(hps)]
        acc_scr[...] = jnp.zeros_like(acc_scr)

        def step(j, carry, masked):
            start = pl.multiple_of(j * tq, tq)
            scores = [_mm_nt(k_ref[pl.ds(start, tq), heads[a]], q_ref[:, heads[a]]) for a in range(hps)]
            new = []
            for a in range(hps):
                m, l = carry[a]
                s = scores[a]
                if masked:
                    kk = lax.broadcasted_iota(jnp.int32, (tq, tq), 0)
                    qq = lax.broadcasted_iota(jnp.int32, (tq, tq), 1)
                    s = jnp.where(kk <= qq, s, NEG_BIG)
                m_new = jnp.maximum(m, jnp.max(s, axis=0, keepdims=True))
                alpha = jnp.exp2(m - m_new)
                p = jnp.exp2(s - m_new)
                l = l * alpha + jnp.sum(p, axis=0, keepdims=True)
                vtj = vt_ref[2 * MLA_V * (a // 2):2 * MLA_V * (a // 2 + 1), pl.ds(start, tq)]
                acc_scr[a] = acc_scr[a] * alpha + _mm(vtj, p)
                new.append((m_new, l))
            return tuple(new)

        init = tuple((jnp.full((1, tq), NEG_BIG, F32), jnp.zeros((1, tq), F32)) for _ in range(hps))
        carry = lax.fori_loop(0, qi, lambda j, c: step(j, c, False), init)
        carry = step(qi, carry, True)
        row = lax.broadcasted_iota(jnp.int32, (2 * MLA_V, tq), 0)
        for pr in range(hps // 2):
            (m0, l0), (m1, l1) = carry[2 * pr], carry[2 * pr + 1]
            ot = jnp.where(row < MLA_V, acc_scr[2 * pr] / l0, acc_scr[2 * pr + 1] / l1)
            o_ref[:, 2 * MLA_V * pr:2 * MLA_V * (pr + 1)] = ot.T
            lse_ref[pr, 0:1, :] = m0 + jnp.log2(l0)
            lse_ref[pr, 1:2, :] = m1 + jnp.log2(l1)

        if ng:
            @pl.when(step_no == steps - 1)
            def _():
                for arrival in gat.forward_arrivals():
                    arrival.wait_recv()
                for cp in gat.sends() + gat.forwards():
                    cp.wait_send()

    return pl.pallas_call(
        body, name="attn_fwd", grid=(MLA_HEADS // hps, nq),
        in_specs=[pl.BlockSpec((tq, hps * HEAD_PAD), lambda g, i: (i, g)),
                  pl.BlockSpec((T, hps * HEAD_PAD), lambda g, i: (0, g)),
                  pl.BlockSpec((hps * MLA_V, T), lambda g, i: (g, 0))] + [ANY] * ng,
        out_specs=[pl.BlockSpec((tq, hps * MLA_V), lambda g, i: (i, g)),
                   pl.BlockSpec((hps // 2, 2, tq), lambda g, i: (g, 0, i))] + [ANY] * ng,
        out_shape=[jax.ShapeDtypeStruct((T, MLA_WIDTH), F32), jax.ShapeDtypeStruct((MLA_HEADS // 2, 2, T), F32)]
        + _Gather.out_shapes(gather),
        scratch_shapes=[pltpu.VMEM((hps, 2 * MLA_V, tq), F32)] + (_Gather.semaphores(gather) if ng else []),
        compiler_params=_params(("arbitrary", "arbitrary")),
    )(qb, kb, vt, *gather)


def _attn_bwd_t(qb, kb, kt, vb, dob, lse, dvec, send=(), tq=256, hps=4):
    T = qb.shape[0]
    nq = T // tq
    ns = len(send)
    steps = (MLA_HEADS // hps) * nq

    def body(q_ref, k_ref, kt_ref, v_ref, do_ref, lse_ref, d_ref, *rest):
        dqt_ref, dk_ref, dv_ref = rest[ns:ns + 3]
        va_scr, dv_scr = rest[2 * ns + 3:2 * ns + 5]
        j = pl.program_id(1)
        step_no = pl.program_id(0) * nq + j
        if ns:
            @pl.when(step_no == 0)
            def _():
                for cp in _chip_swap_copies(rest[:ns], rest[ns + 3:2 * ns + 3], *rest[2 * ns + 5:]):
                    cp.start()

        @pl.when(j == 0)
        def _():
            dqt_ref[...] = jnp.zeros_like(dqt_ref)

        lane = lax.broadcasted_iota(jnp.int32, (tq, 2 * MLA_V), 1)
        heads = [slice(HEAD_PAD * a, HEAD_PAD * (a + 1)) for a in range(hps)]
        pairs = [slice(2 * MLA_V * p, 2 * MLA_V * (p + 1)) for p in range(hps // 2)]
        for pr in range(hps // 2):
            vpair = v_ref[:, pairs[pr]]
            va_scr[2 * pr] = jnp.where(lane < MLA_V, vpair, jnp.zeros_like(vpair))
            va_scr[2 * pr + 1] = jnp.where(lane >= MLA_V, vpair, jnp.zeros_like(vpair))
        dk_ref[...] = jnp.zeros_like(dk_ref)
        dv_scr[...] = jnp.zeros_like(dv_scr)

        def step(i, masked):
            start = pl.multiple_of(i * tq, tq)
            rows = pl.ds(start, tq)
            scores = [_mm_nt(k_ref[:, heads[a]], q_ref[rows, heads[a]]) for a in range(hps)]
            dps = [_mm_nt(va_scr[a], do_ref[rows, pairs[a // 2]]) for a in range(hps)]
            for a in range(hps):
                pr, r = a // 2, a % 2
                p = jnp.exp2(scores[a] - lse_ref[pr, r:r + 1, rows])
                if masked:
                    kk = lax.broadcasted_iota(jnp.int32, (tq, tq), 0)
                    qq = lax.broadcasted_iota(jnp.int32, (tq, tq), 1)
                    p = jnp.where(kk <= qq, p, 0.0)
                ds = p * (dps[a] - d_ref[pr, r:r + 1, rows])
                dv_scr[a] += _mm(p, do_ref[rows, pairs[pr]])
                dk_ref[:, heads[a]] += _mm(ds, q_ref[rows, heads[a]])
                dqt_ref[heads[a], rows] += _mm(kt_ref[heads[a], :], ds)

        def loop_body(i, _):
            step(i, False)
            return 0

        step(j, True)
        lax.fori_loop(j + 1, nq, loop_body, 0)
        for pr in range(hps // 2):
            dv_ref[:, pairs[pr]] = jnp.where(lane < MLA_V, dv_scr[2 * pr], dv_scr[2 * pr + 1])
        dk_ref[...] = dk_ref[...] * (ATTN_SCALE / ATTN_SCALE_LOG2)

        if ns:
            @pl.when(step_no == steps - 1)
            def _():
                for cp in _chip_swap_copies(rest[:ns], rest[ns + 3:2 * ns + 3], *rest[2 * ns + 5:]):
                    cp.wait()

    stat = pl.BlockSpec((hps // 2, 2, T), lambda g, j: (g, 0, 0))
    return pl.pallas_call(
        body, name="attn_bwd", grid=(MLA_HEADS // hps, nq),
        in_specs=[pl.BlockSpec((T, hps * HEAD_PAD), lambda g, j: (0, g)),
                  pl.BlockSpec((tq, hps * HEAD_PAD), lambda g, j: (j, g)),
                  pl.BlockSpec((hps * HEAD_PAD, tq), lambda g, j: (g, j)),
                  pl.BlockSpec((tq, hps * MLA_V), lambda g, j: (j, g)),
                  pl.BlockSpec((T, hps * MLA_V), lambda g, j: (0, g)), stat, stat] + [ANY] * ns,
        out_specs=[pl.BlockSpec((hps * HEAD_PAD, T), lambda g, j: (g, 0)),
                   pl.BlockSpec((tq, hps * HEAD_PAD), lambda g, j: (j, g)),
                   pl.BlockSpec((tq, hps * MLA_V), lambda g, j: (j, g))] + [ANY] * ns,
        out_shape=[jax.ShapeDtypeStruct((MLA_HEADS * HEAD_PAD, T), F32),
                   jax.ShapeDtypeStruct((T, MLA_HEADS * HEAD_PAD), F32),
                   jax.ShapeDtypeStruct((T, MLA_WIDTH), F32)] + _chip_swap_shapes(send),
        scratch_shapes=[pltpu.VMEM((hps, tq, 2 * MLA_V), vb.dtype), pltpu.VMEM((hps, tq, 2 * MLA_V), F32)]
        + ([pltpu.SemaphoreType.DMA((3 * ns,)), pltpu.SemaphoreType.DMA((3 * ns,))] if ns else []),
        compiler_params=_params(("arbitrary", "arbitrary")),
    )(qb, kb, kt, vb, dob, lse, dvec, *send)


def _cumsum_rows(x):
    n = x.shape[0]
    row = lax.broadcasted_iota(jnp.int32, x.shape, 0)
    s = 1
    while s < n:
        x = x + jnp.where(row >= s, pltpu.roll(x, s, 0), 0.0)
        s *= 2
    return x


def _rev_cumsum_rows(x):
    n = x.shape[0]
    row = lax.broadcasted_iota(jnp.int32, x.shape, 0)
    s = 1
    while s < n:
        x = x + jnp.where(row < n - s, pltpu.roll(x, n - s, 0), 0.0)
        s *= 2
    return x


def _lb_from_logits(l):
    l0, l1 = l[0:1, :], l[1:2, :]
    m = jnp.maximum(l0, l1)
    e0, e1 = jnp.exp(l0 - m), jnp.exp(l1 - m)
    return e0 / (e0 + e1)


def _hgrn_gates(hq, hf, lb):
    sig_f = jax.nn.sigmoid(hf)
    f = lb + (1.0 - lb) * sig_f
    sig_q = jax.nn.sigmoid(hq)
    return sig_f, f, jnp.log(f), 1.0 - f, sig_q, hq * sig_q


def _hgrn_intra(q, kk, b, exact=False):
    row = lax.broadcasted_iota(jnp.int32, b.shape, 0)
    qs, ks, eqs, eks, a_rows = [], [], [], [], []
    for i in range(CHUNK // SUB):
        ref = b[SUB * i + SUB // 2:SUB * i + SUB // 2 + 1, :]
        eq = jnp.exp(b[SUB * i:SUB * (i + 1), :] - ref)
        ek = jnp.exp(jnp.where(row < SUB * (i + 1), ref - b, NEG_BIG))
        qi = q[SUB * i:SUB * (i + 1), :] * eq
        ki = kk * ek
        a_rows.append(_mm_nt(qi, ki, exact))
        qs.append(qi), ks.append(ki), eqs.append(eq), eks.append(ek)
    tt = lax.broadcasted_iota(jnp.int32, (CHUNK, CHUNK), 0)
    ss = lax.broadcasted_iota(jnp.int32, (CHUNK, CHUNK), 1)
    causal = ss <= tt
    a = jnp.where(causal, jnp.concatenate(a_rows, axis=0), 0.0)
    return a, causal, qs, ks, eqs, eks


def _hgrn_fwd(xph, lbl, tg=512):
    T = xph.shape[0]
    ng, ncg = T // tg, tg // CHUNK
    cols = [slice(HGRN_DIM * h, HGRN_DIM * (h + 1)) for h in range(HGRN_HEADS)]

    def body(lbl_ref, hq_ref, hf_ref, hi_ref, o_ref, st_ref, s_scr):
        @pl.when(pl.program_id(0) == 0)
        def _():
            s_scr[...] = jnp.zeros_like(s_scr)

        lb = _lb_from_logits(lbl_ref[...])

        def chunks(it, _):
            pre = []
            for k in range(HGRN_CPI):
                c = it * HGRN_CPI + k
                rows = pl.ds(pl.multiple_of(c * CHUNK, CHUNK), CHUNK)
                for cs in cols:
                    _, _, lf, kk, _, q = _hgrn_gates(hq_ref[rows, cs], hf_ref[rows, cs], lb[:, cs])
                    v = hi_ref[rows, cs]
                    b = _cumsum_rows(lf)
                    a = _hgrn_intra(q, kk, b)[0]
                    b_last = b[CHUNK - 1:CHUNK, :]
                    pre.append((c, rows, q * jnp.exp(b), a, v, jnp.exp(b_last), _mm_tn(v, kk * jnp.exp(b_last - b))))
            for i, (c, rows, qe, a, v, ebl, upd) in enumerate(pre):
                h = i % HGRN_HEADS
                st = s_scr[h]
                st_ref[h, c] = st
                o_ref[rows, cols[h]] = _mm_nt(qe, st) + _mm(a, v)
                s_scr[h] = st * ebl + upd
            return 0

        lax.fori_loop(0, ncg // HGRN_CPI, chunks, 0)

    col = lambda k: pl.BlockSpec((tg, HGRN_WIDTH), lambda g: (g, k))
    return pl.pallas_call(
        body, name="hgrn_fwd", grid=(ng,),
        in_specs=[pl.BlockSpec((2, HGRN_WIDTH), lambda g: (0, 0)), col(0), col(1), col(2)],
        out_specs=[col(0), pl.BlockSpec((HGRN_HEADS, ncg, HGRN_DIM, HGRN_DIM), lambda g: (0, g, 0, 0))],
        out_shape=[jax.ShapeDtypeStruct((T, HGRN_WIDTH), F32),
                   jax.ShapeDtypeStruct((HGRN_HEADS, T // CHUNK, HGRN_DIM, HGRN_DIM), F32)],
        scratch_shapes=[pltpu.VMEM((HGRN_HEADS, HGRN_DIM, HGRN_DIM), F32)],
        compiler_params=_params(("arbitrary",)),
    )(lbl, xph, xph, xph)


def _hgrn_bwd(xph, lbl, states, d_o, fill=(), tg=512):
    T = xph.shape[0]
    ng, ncg = T // tg, tg // CHUNK
    cols = [slice(HGRN_DIM * h, HGRN_DIM * (h + 1)) for h in range(HGRN_HEADS)]
    nsub = CHUNK // SUB
    nf = len(fill)

    def body(lbl_ref, hq_ref, hf_ref, hi_ref, st_ref, do_ref, *rest):
        dhq_ref, dhf_ref, dhi_ref, dlg_ref = rest[nf:nf + 4]
        ds_scr, dlb_scr = rest[2 * nf + 4:2 * nf + 6]
        fill_copies = lambda: _pair_fill_copies(rest[nf + 4:2 * nf + 4], *rest[2 * nf + 6:])
        g = pl.program_id(0)

        @pl.when(g == 0)
        def _():
            ds_scr[...] = jnp.zeros_like(ds_scr)
            dlb_scr[...] = jnp.zeros_like(dlb_scr)
            for cp in (fill_copies()[0] if nf else ()):
                cp.start()

        lb = _lb_from_logits(lbl_ref[...])

        def chunks(it, _):
            pre = []
            for k, h in ((k, h) for k in range(HGRN_CPI) for h in range(HGRN_HEADS)):
                cs = cols[h]
                c = ncg - 1 - (it * HGRN_CPI + k)
                rows = pl.ds(pl.multiple_of(c * CHUNK, CHUNK), CHUNK)
                hq = hq_ref[rows, cs]
                sig_f, f, lf, kk, sig_q, q = _hgrn_gates(hq, hf_ref[rows, cs], lb[:, cs])
                v = hi_ref[rows, cs]
                do = do_ref[rows, cs]
                b = _cumsum_rows(lf)
                eb = jnp.exp(b)
                a, causal, qs, ks, eqs, eks = _hgrn_intra(q, kk, b)
                b_last = b[CHUNK - 1:CHUNK, :]
                st = st_ref[h, c]
                pre.append(dict(h=h, cs=cs, rows=rows, hq=hq, sig_f=sig_f, f=f, kk=kk, sig_q=sig_q, q=q, v=v, eb=eb, qs=qs,
                                ks=ks, eqs=eqs,
                                eks=eks, ebl=jnp.exp(b_last), el=jnp.exp(b_last - b), st=st,
                                da=jnp.where(causal, _mm_nt(do, v, True), 0.0), dq=_mm(do, st, True) * eb,
                                dv=_mm_tn(a, do), dsu=_mm_tn(do, q * eb, True)))
            for w in pre:
                dq_rows = []
                dk = jnp.zeros_like(w["q"])
                for i in range(nsub):
                    dai = w["da"][SUB * i:SUB * (i + 1), :]
                    dq_rows.append(_mm(dai, w["ks"][i], True) * w["eqs"][i])
                    dk = dk + _mm_tn(dai, w["qs"][i], True) * w["eks"][i]
                w["dq"] = w["dq"] + jnp.concatenate(dq_rows, axis=0)
                w["dk"] = dk
            for w in pre:
                h, cs, rows = w["h"], w["cs"], w["rows"]
                kk, el, ebl, dst = w["kk"], w["el"], w["ebl"], ds_scr[h]
                dk_state = _mm(w["v"], dst, True) * el
                dk = w["dk"] + dk_state
                e_last = (ebl * jnp.sum(w["st"] * dst, axis=0, keepdims=True)
                          + jnp.sum(kk * dk_state, axis=0, keepdims=True))
                dlf = _rev_cumsum_rows(w["q"] * w["dq"] - kk * dk) + e_last
                ds_scr[h] = dst * ebl + w["dsu"]
                df = dlf / w["f"] - dk
                sig_f, sig_q = w["sig_f"], w["sig_q"]
                dhf_ref[rows, cs] = df * (1.0 - lb[:, cs]) * sig_f * (1.0 - sig_f)
                dlb_scr[:, cs] += jnp.sum(df * (1.0 - sig_f), axis=0, keepdims=True)
                dhq_ref[rows, cs] = w["dq"] * sig_q * (1.0 + w["hq"] * (1.0 - sig_q))
                dhi_ref[rows, cs] = w["dv"] + _mm_nt(kk * el, dst)
            return 0

        lax.fori_loop(0, ncg // HGRN_CPI, chunks, 0)

        @pl.when(g == ng - 1)
        def _():
            dl0 = dlb_scr[...] * lb * (1.0 - lb)
            dlg_ref[...] = jnp.concatenate([dl0, -dl0], axis=0)
            if nf:
                copies, waits = fill_copies()
                for w in waits:
                    w.wait_recv()
                for cp in copies:
                    cp.wait_send()

    col = lambda k: pl.BlockSpec((tg, HGRN_WIDTH), lambda g: (ng - 1 - g, k))
    logits = pl.BlockSpec((2, HGRN_WIDTH), lambda g: (0, 0))
    big = jax.ShapeDtypeStruct((T, HGRN_WIDTH), F32)
    n_in, n_out = 6, 4
    return pl.pallas_call(
        body, name="hgrn_bwd", grid=(ng,),
        in_specs=[logits, col(0), col(1), col(2),
                  pl.BlockSpec((HGRN_HEADS, ncg, HGRN_DIM, HGRN_DIM), lambda g: (0, ng - 1 - g, 0, 0)), col(0)] + [ANY] * nf,
        out_specs=[col(0), col(0), col(0), logits] + [ANY] * nf,
        out_shape=[big, big, big, jax.ShapeDtypeStruct((2, HGRN_WIDTH), F32)]
        + [jax.ShapeDtypeStruct(f.shape, f.dtype) for f in fill],
        input_output_aliases={n_in + k: n_out + k for k in range(nf)},
        scratch_shapes=[pltpu.VMEM((HGRN_HEADS, HGRN_DIM, HGRN_DIM), F32), pltpu.VMEM((1, HGRN_WIDTH), F32)]
        + ([pltpu.SemaphoreType.DMA((nf,)), pltpu.SemaphoreType.DMA((nf,))] if nf else []),
        compiler_params=_params(("arbitrary",)),
    )(lbl, xph, xph, xph, states, d_o, *fill)


def _proj_fwd(x, o_raw, oh_raw, xph, wout, w_mla, w_hg, w_post, w_fpre, tt=512):
    T = x.shape[0]

    def body(x_ref, o_ref, oh_ref, hg_ref, wout_ref, wmla_ref, whg_ref, wpost_ref, wfpre_ref,
             h1_ref, y1_ref, z_ref, mix_ref):
        om, _, _ = _grms_fwd(o_ref[...], wmla_ref[...], MLA_V)
        hg = hg_ref[...]
        ohn, _, _ = _grms_fwd(oh_ref[...], whg_ref[...], HGRN_DIM)
        mix = jnp.concatenate([om, ohn * (hg * jax.nn.sigmoid(hg))], axis=-1)
        mix_ref[...] = mix.astype(mix_ref.dtype)
        y1 = _mm(mix, wout_ref[...])
        y1_ref[...] = y1
        h1 = x_ref[...] + _rms_fwd(y1, wpost_ref[...])[0]
        h1_ref[...] = h1
        z_ref[...] = _rms_fwd(h1, wfpre_ref[...])[0].astype(z_ref.dtype)

    row = lambda w: pl.BlockSpec((tt, w), lambda i: (i, 0))
    full = lambda a: pl.BlockSpec(a.shape, lambda i: (0,) * a.ndim)
    sds = jax.ShapeDtypeStruct
    return pl.pallas_call(
        body, name="proj_fwd", grid=(T // tt,),
        in_specs=[row(D_MODEL), row(MLA_WIDTH), row(HGRN_WIDTH), pl.BlockSpec((tt, HGRN_WIDTH), lambda i: (i, 3)),
                  full(wout), full(w_mla), full(w_hg), full(w_post), full(w_fpre)],
        out_specs=[row(D_MODEL)] * 4,
        out_shape=[sds((T, D_MODEL), F32), sds((T, D_MODEL), F32), sds((T, D_MODEL), MXU_DTYPE),
                   sds((T, D_MODEL), MXU_DTYPE)],
        compiler_params=_params(("arbitrary",)),
    )(x, o_raw, oh_raw, xph, wout, w_mla, w_hg, w_post, w_fpre)


def _ffn_fwd(zb, h1, tgt, w_fpost, wg, wu, wd, tt=256):
    T = zb.shape[0]
    nj = N_CHIPS

    def body(z_ref, h1_ref, tgt_ref, wfpost_ref, wg_ref, wu_ref, wd_ref, g_ref, up_ref, dy2_ref, dh2_ref, loss_ref, dwf_ref):
        @pl.when(pl.program_id(0) == 0)
        def _():
            loss_ref[...] = jnp.zeros_like(loss_ref)
            dwf_ref[...] = jnp.zeros_like(dwf_ref)

        z = z_ref[...]
        gs = [_mm_nt(z, wg_ref[j]) for j in range(nj)]
        ups = [_mm_nt(z, wu_ref[j]) for j in range(nj)]
        y2 = jnp.zeros((tt, D_MODEL), F32)
        for j in range(nj):
            g_ref[j] = gs[j]
            up_ref[j] = ups[j]
            y2 = y2 + _mm(gs[j] * jax.nn.sigmoid(gs[j]) * ups[j], wd_ref[j])
        w = wfpost_ref[...]
        y2s, y2n, r2 = _rms_fwd(y2, w)
        e = h1_ref[...] + y2s - tgt_ref[...]
        loss_ref[...] += jnp.sum(e * e, axis=0, keepdims=True)
        dh2 = e * (1.0 / D_MODEL)
        dh2_ref[...] = dh2
        dy2, dwf = _rms_bwd(dh2, y2n, r2, w)
        dy2_ref[...] = dy2.astype(dy2_ref.dtype)
        dwf_ref[...] += dwf

    row = pl.BlockSpec((tt, D_MODEL), lambda i: (i, 0))
    vec = pl.BlockSpec((1, D_MODEL), lambda i: (0, 0))
    resident = pl.BlockSpec((nj, FF_SHARD, D_MODEL), lambda i: (0, 0, 0), pipeline_mode=pl.Buffered(1))
    act = pl.BlockSpec((nj, tt, FF_SHARD), lambda i: (0, i, 0))
    sds = jax.ShapeDtypeStruct
    return pl.pallas_call(
        body, name="ffn_fwd", grid=(T // tt,),
        in_specs=[row, row, row, vec, resident, resident, resident],
        out_specs=[act, act, row, row, vec, vec],
        out_shape=[sds((nj, T, FF_SHARD), F32), sds((nj, T, FF_SHARD), F32), sds((T, D_MODEL), MXU_DTYPE),
                   sds((T, D_MODEL), F32), sds((1, D_MODEL), F32), sds((1, D_MODEL), F32)],
        compiler_params=_params(("arbitrary",)),
    )(zb, h1, tgt, w_fpost, wg, wu, wd)


FFN_BWD_SHARDS = 2


def _ffn_bwd(zb, g, up, dy2b, wg, wu, wd, tt=512):
    T = zb.shape[0]
    spp = FFN_BWD_SHARDS
    npass = N_CHIPS // spp

    def body(z_ref, g_ref, up_ref, dy2_ref, wg_ref, wu_ref, wd_ref, dwg_ref, dwu_ref, dwd_ref, dz_ref):
        @pl.when(pl.program_id(1) == 0)
        def _():
            dwg_ref[...] = jnp.zeros_like(dwg_ref)
            dwu_ref[...] = jnp.zeros_like(dwu_ref)
            dwd_ref[...] = jnp.zeros_like(dwd_ref)

        z, dy2 = z_ref[...], dy2_ref[...]
        dffs = [_mm_nt(dy2, wd_ref[k]) for k in range(spp)]
        dz = jnp.zeros((tt, D_MODEL), F32)
        for k in range(spp):
            g_, up_ = g_ref[k], up_ref[k]
            sg = jax.nn.sigmoid(g_)
            act = g_ * sg
            dwd_ref[k] += _mm_tn(act * up_, dy2)
            dg = dffs[k] * up_ * sg * (1.0 + g_ * (1.0 - sg))
            dup = dffs[k] * act
            dwg_ref[k] += _mm_tn(dg, z)
            dwu_ref[k] += _mm_tn(dup, z)
            dz = dz + _mm(dg, wg_ref[k]) + _mm(dup, wu_ref[k])
        dz_ref[0] = dz

    row = pl.BlockSpec((tt, D_MODEL), lambda p, i: (i, 0))
    act = pl.BlockSpec((spp, tt, FF_SHARD), lambda p, i: (p, i, 0))
    w_sh = pl.BlockSpec((spp, FF_SHARD, D_MODEL), lambda p, i: (p, 0, 0), pipeline_mode=pl.Buffered(1))
    w_grad = jax.ShapeDtypeStruct((N_CHIPS, FF_SHARD, D_MODEL), F32)
    return pl.pallas_call(
        body, name="ffn_bwd", grid=(npass, T // tt),
        in_specs=[row, act, act, row, w_sh, w_sh, w_sh],
        out_specs=[w_sh, w_sh, w_sh, pl.BlockSpec((1, tt, D_MODEL), lambda p, i: (p, i, 0))],
        out_shape=[w_grad, w_grad, w_grad, jax.ShapeDtypeStruct((npass, T, D_MODEL), F32)],
        compiler_params=_params(("arbitrary", "arbitrary")),
    )(zb, g, up, dy2b, wg, wu, wd)


def _mid_bwd(dzp, dh2, h1, y1, mixb, o_raw, oh_raw, xph, wout, w_fpre, w_post, w_mla, w_hg, swap=(), tt=256):
    T = dh2.shape[0]
    nsw = len(swap)
    n_in, n_out = 13, 10

    def body(*refs):
        (dzp_ref, dh2_ref, h1_ref, y1_ref, mix_ref, o_ref, oh_ref, hg_ref, wout_ref, wfpre_ref, wpost_ref,
         wmla_ref, whg_ref) = refs[:n_in]
        (dh1_ref, dwout_ref, do_ref, doh_ref, dhg_ref, dvec_ref, dwfpre_ref, dwpost_ref, dwmla_ref,
         dwhg_ref) = refs[n_in + nsw:n_in + nsw + n_out]
        swap_copies = lambda: _pair_swap_copies(refs[n_in:n_in + nsw], refs[n_in + nsw + n_out:n_in + 2 * nsw + n_out],
                                                *refs[n_in + 2 * nsw + n_out:])

        @pl.when(pl.program_id(0) == 0)
        def _():
            for r in (dwout_ref, dwfpre_ref, dwpost_ref, dwmla_ref, dwhg_ref):
                r[...] = jnp.zeros_like(r)
            for cp in (swap_copies() if nsw else ()):
                cp.start()

        dz = dzp_ref[0] + dzp_ref[1]
        wfpre = wfpre_ref[...]
        _, h1n, r = _rms_fwd(h1_ref[...], wfpre)
        dh1_z, dwfpre = _rms_bwd(dz, h1n, r, wfpre)
        dwfpre_ref[...] += dwfpre
        dh1 = dh2_ref[...] + dh1_z
        dh1_ref[...] = dh1
        wpost = wpost_ref[...]
        _, y1n, r1 = _rms_fwd(y1_ref[...], wpost)
        dy1, dwpost = _rms_bwd(dh1, y1n, r1, wpost)
        dwpost_ref[...] += dwpost
        dmix = _mm_nt(dy1, wout_ref[...])
        dwout_ref[...] += _mm_tn(mix_ref[...], dy1)
        wmla = wmla_ref[...]
        o = o_ref[...]
        _, on, ro = _grms_fwd(o, wmla, MLA_V)
        d_o, dwmla = _grms_bwd(dmix[:, :MLA_WIDTH], on, ro, wmla, MLA_V)
        dwmla_ref[...] += dwmla
        do_ref[...] = d_o.astype(do_ref.dtype)
        hh = lax.broadcasted_iota(jnp.int32, (MLA_HEADS, MLA_WIDTH), 0)
        ll = lax.broadcasted_iota(jnp.int32, (MLA_HEADS, MLA_WIDTH), 1)
        sel = jnp.where((ll >= hh * MLA_V) & (ll < (hh + 1) * MLA_V), 1.0, 0.0)
        dvec_ref[...] = _mm_nt(sel, d_o * o, True)
        whg = whg_ref[...]
        hg = hg_ref[...]
        sg = jax.nn.sigmoid(hg)
        _, ohn, rh = _grms_fwd(oh_ref[...], whg, HGRN_DIM)
        dmh = dmix[:, MLA_WIDTH:]
        dhg_ref[...] = dmh * ohn * whg * sg * (1.0 + hg * (1.0 - sg))
        d_oh, dwhg = _grms_bwd(dmh * (hg * sg), ohn, rh, whg, HGRN_DIM)
        dwhg_ref[...] += dwhg
        doh_ref[...] = d_oh

        if nsw:
            @pl.when(pl.program_id(0) == T // tt - 1)
            def _():
                for cp in swap_copies():
                    cp.wait()

    row = lambda w: pl.BlockSpec((tt, w), lambda i: (i, 0))
    full = lambda a: pl.BlockSpec(a.shape, lambda i: (0,) * a.ndim)
    vec = lambda w: pl.BlockSpec((1, w), lambda i: (0, 0))
    sds = jax.ShapeDtypeStruct
    return pl.pallas_call(
        body, name="mid_bwd", grid=(T // tt,),
        in_specs=[pl.BlockSpec((N_CHIPS // FFN_BWD_SHARDS, tt, D_MODEL), lambda i: (0, i, 0)), row(D_MODEL), row(D_MODEL), row(D_MODEL),
                  row(D_MODEL), row(MLA_WIDTH), row(HGRN_WIDTH), pl.BlockSpec((tt, HGRN_WIDTH), lambda i: (i, 3)),
                  full(wout), vec(D_MODEL), vec(D_MODEL), vec(MLA_WIDTH), vec(HGRN_WIDTH)] + [ANY] * nsw,
        out_specs=[row(D_MODEL), full(wout), row(MLA_WIDTH), row(HGRN_WIDTH), row(HGRN_WIDTH),
                   pl.BlockSpec((MLA_HEADS, tt), lambda i: (0, i)),
                   vec(D_MODEL), vec(D_MODEL), vec(MLA_WIDTH), vec(HGRN_WIDTH)] + [ANY] * nsw,
        out_shape=[sds((T, D_MODEL), F32), sds(wout.shape, F32), sds((T, MLA_WIDTH), MXU_DTYPE), sds((T, HGRN_WIDTH), F32),
                   sds((T, HGRN_WIDTH), F32), sds((MLA_HEADS, T), F32),
                   sds((1, D_MODEL), F32), sds((1, D_MODEL), F32), sds((1, MLA_WIDTH), F32), sds((1, HGRN_WIDTH), F32)]
        + _half_stack_shapes(swap),
        scratch_shapes=[pltpu.SemaphoreType.DMA((nsw,)), pltpu.SemaphoreType.DMA((nsw,))] if nsw else [],
        compiler_params=_params(("arbitrary",)),
    )(dzp, dh2, h1, y1, mixb, o_raw, oh_raw, xph, wout, w_fpre, w_post, w_mla, w_hg, *swap)


def _in_bwd(x, dh1, cq, ckv, dq, dk, dv, dhq, dhf, dhi, dhg, rc, rs, w_pre, win, qnw, wq, kvnw, wk, wv, tt=256):
    T = x.shape[0]

    def body(x_ref, dh1_ref, cq_ref, ckv_ref, dq_ref, dk_ref, dv_ref, dhq_ref, dhf_ref, dhi_ref, dhg_ref, rc_ref, rs_ref,
             wpre_ref, win_ref, qnw_ref, wq_ref, kvnw_ref, wk_ref, wv_ref,
             dx_ref, dwin_ref, dwq_ref, dwk_ref, dwv_ref, dwpre_ref, dqnw_ref, dkvnw_ref):
        @pl.when(pl.program_id(0) == 0)
        def _():
            for r in (dwin_ref, dwq_ref, dwk_ref, dwv_ref, dwpre_ref, dqnw_ref, dkvnw_ref):
                r[...] = jnp.zeros_like(r)

        def add_win_grad(r, first):
            for arr0, n, chip, row0 in _win_grad_segments():
                if first <= arr0 and arr0 + n <= first + r.shape[0]:
                    dwin_ref[chip, row0:row0 + n, :] += r[arr0 - first:arr0 - first + n]

        lo = Q_RANK + KV_RANK + HEAD_PAD
        dxp_h = jnp.concatenate([dhq_ref[...], dhf_ref[...], dhi_ref[...], dhg_ref[...]], axis=-1)
        du = _mm(dxp_h, win_ref[lo:, :])
        wpre = wpre_ref[...]
        u, xn, rx = _rms_fwd(x_ref[...], wpre)
        add_win_grad(_mm_tn(dxp_h, u), lo)
        c, sa, sb = _rope_tables(rc_ref[...], rs_ref[...])
        lane = lax.broadcasted_iota(jnp.int32, (tt, HEAD_PAD), 1)
        dk_all = dk_ref[...]
        dq_lin = []
        dkr = jnp.zeros((tt, HEAD_PAD), F32)
        for h in range(MLA_HEADS):
            sl = slice(HEAD_PAD * h, HEAD_PAD * (h + 1))
            dq_lin.append(_rope_bwd(dq_ref[sl, :].T * ATTN_SCALE, c, sa, sb))
            dkr = dkr + dk_all[:, sl]
        dq_lin = jnp.concatenate(dq_lin, axis=-1)
        dkr = jnp.where((lane >= MLA_NOPE) & (lane < MLA_QK), _rope_bwd(dkr, c, sa, sb), 0.0)
        qnw = qnw_ref[...]
        qn, cqn, rq = _rms_fwd(cq_ref[...], qnw)
        dwq_ref[...] += _mm_tn(qn, dq_lin)
        dcq, dqnw = _rms_bwd(_mm_nt(dq_lin, wq_ref[...]), cqn, rq, qnw)
        dqnw_ref[...] += dqnw
        kvnw = kvnw_ref[...]
        kvn, ckvn, rkv = _rms_fwd(ckv_ref[...], kvnw)
        dv_ = dv_ref[...]
        dwk_ref[...] += _mm_tn(kvn, dk_all)
        dwv_ref[...] += _mm_tn(kvn, dv_)
        dckv, dkvnw = _rms_bwd(_mm_nt(dk_all, wk_ref[...]) + _mm_nt(dv_, wv_ref[...]), ckvn, rkv, kvnw)
        dkvnw_ref[...] += dkvnw
        dxp_a = jnp.concatenate([dcq, dckv, dkr], axis=-1)
        add_win_grad(_mm_tn(dxp_a, u), 0)
        dx_u, dwpre = _rms_bwd(du + _mm(dxp_a, win_ref[:lo, :]), xn, rx, wpre)
        dwpre_ref[...] += dwpre
        dx_ref[...] = dh1_ref[...] + dx_u

    row = lambda w: pl.BlockSpec((tt, w), lambda i: (i, 0))
    full = lambda a: pl.BlockSpec(a.shape, lambda i: (0,) * a.ndim)
    sds = jax.ShapeDtypeStruct
    qk_w = MLA_HEADS * HEAD_PAD
    return pl.pallas_call(
        body, name="in_bwd", grid=(T // tt,),
        in_specs=[row(D_MODEL), row(D_MODEL), row(Q_RANK), row(KV_RANK), pl.BlockSpec((qk_w, tt), lambda i: (0, i)),
                  row(qk_w), row(MLA_WIDTH),
                  row(HGRN_WIDTH), row(HGRN_WIDTH), row(HGRN_WIDTH), row(HGRN_WIDTH), row(HEAD_PAD), row(HEAD_PAD),
                  full(w_pre), full(win), full(qnw), full(wq), full(kvnw), full(wk), full(wv)],
        out_specs=[row(D_MODEL), pl.BlockSpec(WIN_COMM_SHAPE, lambda i: (0, 0, 0)), full(wq), full(wk), full(wv),
                   full(w_pre), full(qnw), full(kvnw)],
        out_shape=[sds((T, D_MODEL), F32), sds(WIN_COMM_SHAPE, F32), sds(wq.shape, F32), sds(wk.shape, F32),
                   sds(wv.shape, F32), sds(w_pre.shape, F32), sds(qnw.shape, F32), sds(kvnw.shape, F32)],
        compiler_params=_params(("arbitrary",)),
    )(x, dh1, cq, ckv, dq, dk, dv, dhq, dhf, dhi, dhg, rc, rs, w_pre, win, qnw, wq, kvnw, wk, wv)


def _arrange_weights(win_t, wuq_full, wukv):
    dt = win_t.dtype
    z = lambda n: jnp.zeros((n, D_MODEL), dt)
    s2 = Q_RANK + KV_RANK
    win_arr = jnp.concatenate([win_t[:s2], z(MLA_NOPE), win_t[s2:s2 + MLA_ROPE], z(HEAD_PAD - MLA_QK),
                               win_t[s2 + MLA_ROPE:]], axis=0)
    wq_arr = jnp.pad(wuq_full, ((0, 0), (0, 0), (0, HEAD_PAD - MLA_QK))).reshape(Q_RANK, MLA_HEADS * HEAD_PAD)
    wk_arr = jnp.pad(wukv[:, :, :MLA_NOPE], ((0, 0), (0, 0), (0, HEAD_PAD - MLA_NOPE))).reshape(
        KV_RANK, MLA_HEADS * HEAD_PAD)
    wv_arr = wukv[:, :, MLA_NOPE:].reshape(KV_RANK, MLA_WIDTH)
    return win_arr, wq_arr, wk_arr, wv_arr


WIN_COMM_SHAPE = (N_CHIPS, FF_SHARD, D_MODEL)


def _win_grad_segments():
    s2 = Q_RANK + KV_RANK
    runs = [(0, s2, 0), (s2, s2 + MLA_ROPE, MLA_NOPE), (s2 + MLA_ROPE, D_IN, HEAD_PAD - MLA_ROPE)]
    per = D_IN // N_CHIPS
    segs = []
    for lo, hi, shift in runs:
        for k in range(N_CHIPS):
            a, b = max(lo, per * k), min(hi, per * (k + 1))
            if a < b:
                segs.append((a + shift, b - a, k, a - per * k))
    return segs


def _unarrange_grads(dwq_arr, dwk_arr, dwv_arr):
    dwuq = dwq_arr.reshape(Q_RANK, MLA_HEADS, HEAD_PAD)[:, :, :MLA_QK]
    dwukv = jnp.concatenate([dwk_arr.reshape(KV_RANK, MLA_HEADS, HEAD_PAD)[:, :, :MLA_NOPE],
                             dwv_arr.reshape(KV_RANK, MLA_HEADS, MLA_V)], axis=-1)
    return dwuq, dwukv


def _rope_inv_freq():
    inv = 1.0 / (ROPE_THETA ** (jnp.arange(0, MLA_ROPE, 2, dtype=F32) / MLA_ROPE))
    z = lambda n: jnp.zeros((n,), F32)
    return jnp.concatenate([z(MLA_NOPE), inv, inv, z(HEAD_PAD - MLA_QK)]).reshape(1, HEAD_PAD)


def _local_step(x, pos, tgt, small, win_arr, wq_arr, wk_arr, wv_arr, late, place=None):
    invf = _rope_inv_freq()
    cq, ckv, xph, qb, kb, vb, kt, vt, rc, rs = _in_fwd(x, pos, invf, small["attn_pre_norm"], win_arr, small["mla_q_norm"],
                                               wq_arr, small["mla_kv_norm"], wk_arr, wv_arr)
    if place is None:
        o_raw, lse = _attn_fwd_t(qb, kb, vt)
        wout, wg, wu, wd = late
    else:
        o_raw, lse, *stacks = _attn_fwd_t(qb, kb, vt, gather=late)
        wout, wg, wu, wd = [lax.dynamic_update_slice(s, l[None], (place[1], 0, 0)) for s, l in zip(stacks, late)]
        wout = wout.reshape(D_MODEL, D_MODEL)
    oh_raw, states = _hgrn_fwd(xph, small["hgrn_lb_logits"])
    h1, y1, zb, mixb = _proj_fwd(x, o_raw, oh_raw, xph, wout, small["mla_out_norm"], small["hgrn_out_norm"],
                                 small["attn_post_norm"], small["ffn_pre_norm"])
    g, up, dy2b, dh2, loss_acc, d_fpost = _ffn_fwd(zb, h1, tgt, small["ffn_post_norm"], wg, wu, wd)
    dwg, dwu, dwd, dzp = _ffn_bwd(zb, g, up, dy2b, wg, wu, wd)
    ffn_grads = [] if place is None else [dwg, dwu, dwd]
    dh1, dwout, d_o, d_oh, dhg, dvec, d_fpre, d_post, d_mla, d_hg, *ffn_rs = _mid_bwd(
        dzp, dh2, h1, y1, mixb, o_raw, oh_raw, xph, wout, small["ffn_pre_norm"], small["attn_post_norm"],
        small["mla_out_norm"], small["hgrn_out_norm"], swap=ffn_grads)
    ffn_ps = _pair_sum(place, ffn_grads, ffn_rs, name="pair_sum_ffn") if ffn_grads else []
    dq, dk, dv, *ffn_ris = _attn_bwd_t(qb, kb, kt, vb, d_o, lse, dvec.reshape(lse.shape), send=ffn_ps)
    ffn_sums = _chip_sum(place, ffn_grads, ffn_rs, ffn_ris, name="chip_sum_ffn") if ffn_grads else []
    dhq, dhf, dhi, d_lbl, *ffn_final = _hgrn_bwd(xph, small["hgrn_lb_logits"], states, d_oh, fill=ffn_sums)
    dx, dwin4, dwq_arr, dwk_arr, dwv_arr, d_pre, d_qn, d_kvn = _in_bwd(
        x, dh1, cq, ckv, dq, dk, dv, dhq, dhf, dhi, dhg, rc, rs, small["attn_pre_norm"], win_arr,
        small["mla_q_norm"], wq_arr, small["mla_kv_norm"], wk_arr, wv_arr)
    dwuq, dwukv = _unarrange_grads(dwq_arr, dwk_arr, dwv_arr)
    loss = 0.5 * jnp.sum(loss_acc) * (1.0 / D_MODEL)
    grads = dict(attn_pre_norm=d_pre, w_in=dwin4, mla_q_norm=d_qn, mla_w_uq=dwuq, mla_kv_norm=d_kvn, mla_w_ukv=dwukv,
                 mla_out_norm=d_mla, hgrn_lb_logits=d_lbl, hgrn_out_norm=d_hg, w_out=dwout, attn_post_norm=d_post,
                 ffn_pre_norm=d_fpre, w_gate=dwg, w_up=dwu, w_down=dwd, ffn_post_norm=d_fpost)
    if place is None:
        return loss, dx, grads
    return loss, dx, grads, ffn_final


def _place():
    x, y, c = lax.axis_index("x"), lax.axis_index("y"), lax.axis_index("c")
    others = [(1 - x, y), (x, 1 - y), (1 - x, 1 - y)]
    return x, y, c, 2 * x + y, (x, y, 1 - c), others


def _half(ref, c, rows):
    return ref.at[pl.ds(pl.multiple_of(c * rows, 8), rows)]


def _rcopy(src, dst, send, recv, k, to):
    return pltpu.make_async_remote_copy(src_ref=src, dst_ref=dst, send_sem=send.at[k], recv_sem=recv.at[k],
                                        device_id=to, device_id_type=MESH)


class _Gather:
    def __init__(self, ins, outs, send, recv):
        self.ins, self.outs, self.send, self.recv = ins, outs, send, recv
        self.n = len(ins)
        self.halves = [r.shape[0] // 2 for r in ins]
        _, _, self.c, self.me, self.sib, self.others = _place()

    def _each(self):
        for j, (px, py) in enumerate(self.others):
            for a in range(self.n):
                yield j * self.n + a, a, 2 * px + py, (px, py, self.c)

    def sends(self):
        return [_rcopy(_half(self.ins[a], self.c, self.halves[a]), _half(self.outs[a].at[self.me], self.c, self.halves[a]),
                       self.send, self.recv, k, to) for k, a, _, to in self._each()]

    def arrivals(self):
        parts = [(k, _half(self.outs[a].at[chip], self.c, self.halves[a]), to) for k, a, chip, to in self._each()]
        return [_rcopy(p, p, self.send, self.recv, k, to) for k, p, to in parts]

    def forwards(self):
        parts = [(k, _half(self.outs[a].at[chip], self.c, self.halves[a])) for k, a, chip, _ in self._each()]
        return [_rcopy(p, p, self.send, self.recv, 3 * self.n + k, self.sib) for k, p in parts]

    def forward_arrivals(self):
        parts = [(k, _half(self.outs[a].at[chip], 1 - self.c, self.halves[a])) for k, a, chip, _ in self._each()]
        return [_rcopy(p, p, self.send, self.recv, 3 * self.n + k, self.sib) for k, p in parts]

    @staticmethod
    def out_shapes(arrs):
        return [jax.ShapeDtypeStruct((N_CHIPS,) + a.shape, a.dtype) for a in arrs]

    @staticmethod
    def semaphores(arrs):
        return [pltpu.SemaphoreType.DMA((6 * len(arrs),)), pltpu.SemaphoreType.DMA((6 * len(arrs),))]


def _gather_chips(arrs, name):
    n = len(arrs)

    def body(*refs):
        gat = _Gather(refs[:n], refs[n:2 * n], *refs[2 * n:])
        sends, forwards = gat.sends(), gat.forwards()
        for cp in sends:
            cp.start()
        for arrival, fw in zip(gat.arrivals(), forwards):
            arrival.wait_recv()
            fw.start()
        for arrival in gat.forward_arrivals():
            arrival.wait_recv()
        for cp in sends + forwards:
            cp.wait_send()

    return pl.pallas_call(body, name=name, in_specs=[ANY] * n, out_specs=[ANY] * n, out_shape=_Gather.out_shapes(arrs),
                          scratch_shapes=_Gather.semaphores(arrs))(*arrs)


GRAD_BLOCKS = 2


def _pair_swap_copies(g_refs, r_refs, send, recv):
    _, _, c, _, sib, _ = _place()
    copies = []
    for a, (g, r) in enumerate(zip(g_refs, r_refs)):
        h = g.shape[1] // 2
        copies.append(_rcopy(g.at[:, pl.ds(pl.multiple_of((1 - c) * h, 8), h)], r, send, recv, a, sib))
    return copies


def _half_stack_shapes(gs, dtype=None):
    return [jax.ShapeDtypeStruct((N_CHIPS, g.shape[1] // 2, g.shape[2]), dtype or g.dtype) for g in gs]


def _pair_swap(gs, sm):
    n = len(gs)

    def body(*refs):
        g_refs, sm_ref = refs[:n], refs[n]
        r_refs, ssib_ref = refs[n + 1:2 * n + 1], refs[2 * n + 1]
        send, recv = refs[2 * n + 2:]
        copies = _pair_swap_copies(g_refs, r_refs, send, recv)
        copies.append(_rcopy(sm_ref, ssib_ref, send, recv, n, _place()[4]))
        for cp in copies:
            cp.start()
        for cp in copies:
            cp.wait()

    return pl.pallas_call(
        body, name="pair_swap", in_specs=[ANY] * (n + 1), out_specs=[ANY] * (n + 1),
        out_shape=_half_stack_shapes(gs) + [jax.ShapeDtypeStruct(sm.shape, sm.dtype)],
        scratch_shapes=[pltpu.SemaphoreType.DMA((n + 1,)), pltpu.SemaphoreType.DMA((n + 1,))],
    )(*gs, sm)


def _pair_sum(place, gs, rs, small=None, name="pair_sum"):
    n = len(gs)
    nb = GRAD_BLOCKS

    def body(place_ref, *refs):
        g_refs, r_refs, p_refs = refs[:n], refs[n:2 * n], refs[-n - 1:-1] if small else refs[-n:]
        for a in range(n):
            p_refs[a][0] = (g_refs[a][0] + r_refs[a][0]).astype(p_refs[a].dtype)
        if small:
            @pl.when((pl.program_id(0) == 0) & (pl.program_id(1) == 0))
            def _():
                refs[-1][...] = refs[2 * n][...] + refs[2 * n + 1][...]

    in_specs, out_specs = [], []
    for g in gs:
        blk = (1, g.shape[1] // 2 // nb, g.shape[2])
        in_specs.append(pl.BlockSpec(blk, lambda i, k, p: (k, p[0] * nb + i, 0)))
    for g in gs:
        blk = (1, g.shape[1] // 2 // nb, g.shape[2])
        in_specs.append(pl.BlockSpec(blk, lambda i, k, p: (k, i, 0)))
        out_specs.append(pl.BlockSpec(blk, lambda i, k, p: (k, i, 0)))
    out_shape = _half_stack_shapes(gs, BF16)
    if small:
        sm_spec = pl.BlockSpec(small[0].shape, lambda i, k, p: (0, 0))
        in_specs += [sm_spec, sm_spec]
        out_specs.append(sm_spec)
        out_shape.append(jax.ShapeDtypeStruct(small[0].shape, F32))
    return pl.pallas_call(
        body, name=name,
        grid_spec=pltpu.PrefetchScalarGridSpec(num_scalar_prefetch=1, grid=(nb, N_CHIPS), in_specs=in_specs,
                                               out_specs=out_specs),
        out_shape=out_shape,
        compiler_params=_params(("arbitrary", "arbitrary")),
    )(place, *gs, *rs, *(small or ()))


def _chip_swap_copies(p_refs, ri_refs, send, recv):
    _, _, c, _, _, others = _place()
    n = len(p_refs)
    return [_rcopy(p_refs[a].at[2 * px + py], ri_refs[a].at[j], send, recv, j * n + a, (px, py, c))
            for j, (px, py) in enumerate(others) for a in range(n)]


def _chip_swap_shapes(ps):
    return [jax.ShapeDtypeStruct((3,) + p.shape[1:], p.dtype) for p in ps]


def _chip_swap(ps, pair):
    n = len(ps)

    def body(*refs):
        start, finish = _chip_swap_plan(refs[:n], refs[n], refs[n + 1:2 * n + 1], refs[2 * n + 1], *refs[2 * n + 2:])
        start()
        finish()

    return pl.pallas_call(
        body, name="chip_swap", in_specs=[ANY] * (n + 1), out_specs=[ANY] * (n + 1),
        out_shape=_chip_swap_out_shapes(ps, pair), scratch_shapes=_chip_swap_semaphores(n),
    )(*ps, pair)


def _chip_swap_plan(p_refs, pair_ref, ri_refs, sm4_ref, send, recv, lsem):
    n = len(p_refs)
    hs = SMALL_ROWS // 2
    x, y, c, me, sib, others = _place()
    local = pltpu.make_async_copy(pair_ref, sm4_ref.at[me], lsem.at[0])
    copies = _chip_swap_copies(p_refs, ri_refs, send, recv)
    arrivals = list(copies)
    for j, (px, py) in enumerate(others):
        copies.append(_rcopy(_half(pair_ref, c, hs), _half(sm4_ref.at[me], c, hs), send, recv, 3 * n + j, (px, py, c)))
        part = _half(sm4_ref.at[2 * px + py], c, hs)
        arrivals.append(_rcopy(part, part, send, recv, 3 * n + j, (px, py, c)))

    def start():
        local.start()
        for cp in copies:
            cp.start()

    def finish():
        for arrival in arrivals:
            arrival.wait_recv()
        for cp in copies:
            cp.wait_send()
        local.wait()

    return start, finish


def _chip_swap_out_shapes(ps, pair):
    return _chip_swap_shapes(ps) + [jax.ShapeDtypeStruct((N_CHIPS,) + pair.shape, pair.dtype)]


def _chip_swap_semaphores(n):
    k = 3 * (n + 1)
    return [pltpu.SemaphoreType.DMA((k,)), pltpu.SemaphoreType.DMA((k,)), pltpu.SemaphoreType.DMA((1,))]


def _chip_sum(place, gs, rs, ris, name="chip_sum"):
    n = len(gs)
    nb = GRAD_BLOCKS

    def body(place_ref, *refs):
        g_refs, r_refs, ri_refs, o_refs = refs[:n], refs[n:2 * n], refs[2 * n:3 * n], refs[3 * n:]
        for a in range(n):
            ri = ri_refs[a]
            o_refs[a][...] = (g_refs[a][0] + r_refs[a][0]) + ri[0].astype(F32) + ri[1].astype(F32) + ri[2].astype(F32)

    in_specs, out_specs, out_shape = [], [], []
    for g in gs:
        blk = (1, g.shape[1] // 2 // nb, g.shape[2])
        in_specs.append(pl.BlockSpec(blk, lambda i, p: (p[1], p[0] * nb + i, 0)))
    for g in gs:
        blk = (1, g.shape[1] // 2 // nb, g.shape[2])
        in_specs.append(pl.BlockSpec(blk, lambda i, p: (p[1], i, 0)))
    for g in gs:
        rb = g.shape[1] // 2 // nb
        in_specs.append(pl.BlockSpec((3, rb, g.shape[2]), lambda i, p: (0, i, 0)))
        out_specs.append(pl.BlockSpec((rb, g.shape[2]), lambda i, p: (p[0] * nb + i, 0)))
        out_shape.append(jax.ShapeDtypeStruct(g.shape[1:], F32))
    return pl.pallas_call(
        body, name=name,
        grid_spec=pltpu.PrefetchScalarGridSpec(num_scalar_prefetch=1, grid=(nb,), in_specs=in_specs, out_specs=out_specs),
        out_shape=out_shape,
        compiler_params=_params(("arbitrary",)),
    )(place, *gs, *rs, *ris)


def _pair_fill_copies(g_refs, send, recv):
    _, _, c, _, sib, _ = _place()
    copies, waits = [], []
    for a, g in enumerate(g_refs):
        h = g.shape[0] // 2
        mine, theirs = _half(g, c, h), _half(g, 1 - c, h)
        copies.append(_rcopy(mine, mine, send, recv, a, sib))
        waits.append(_rcopy(theirs, theirs, send, recv, a, sib))
    return copies, waits


def _pair_fill(gfs, sm4):
    n = len(gfs)
    hs = SMALL_ROWS // 2

    def body(*refs):
        g_refs, sm4_ref = refs[n + 1:2 * n + 1], refs[2 * n + 1]
        send, recv = refs[2 * n + 2:]
        x, y, c, me, sib, others = _place()
        copies, waits = _pair_fill_copies(g_refs, send, recv)
        for j, (px, py) in enumerate(others):
            chip = 2 * px + py
            mine, theirs = _half(sm4_ref.at[chip], c, hs), _half(sm4_ref.at[chip], 1 - c, hs)
            copies.append(pltpu.make_async_remote_copy(src_ref=mine, dst_ref=mine, send_sem=send.at[n + j],
                                                       recv_sem=recv.at[n + j], device_id=sib, device_id_type=MESH))
            waits.append(pltpu.make_async_remote_copy(src_ref=theirs, dst_ref=theirs, send_sem=send.at[n + j],
                                                      recv_sem=recv.at[n + j], device_id=sib, device_id_type=MESH))
        for cp in copies:
            cp.start()
        for w in waits:
            w.wait_recv()
        for cp in copies:
            cp.wait_send()

    return pl.pallas_call(
        body, name="pair_fill", in_specs=[ANY] * (n + 1), out_specs=[ANY] * (n + 1),
        out_shape=[jax.ShapeDtypeStruct(g.shape, g.dtype) for g in gfs] + [jax.ShapeDtypeStruct(sm4.shape, sm4.dtype)],
        input_output_aliases={i: i for i in range(n + 1)},
        scratch_shapes=[pltpu.SemaphoreType.DMA((n + 3,)), pltpu.SemaphoreType.DMA((n + 3,))],
    )(*gfs, sm4)


def _adamw_math(w, g, m, v):
    m = ADAM_B1 * m + (1.0 - ADAM_B1) * g
    v = ADAM_B2 * v + (1.0 - ADAM_B2) * (g * g)
    m_hat = m / (1.0 - ADAM_B1 ** ADAM_STEP)
    v_hat = v / (1.0 - ADAM_B2 ** ADAM_STEP)
    return -ADAM_LR * (m_hat / (jnp.sqrt(v_hat) + ADAM_EPS) + ADAM_WD * w), m, v


def _adamw(items, steps, name):
    n = len(items)

    def body(*refs):
        for a in range(n):
            g = refs[4 * a + 1][...]
            d, mo, vo = _adamw_math(refs[4 * a][...], g, refs[4 * a + 2][...], refs[4 * a + 3][...])
            for out, val in zip(refs[4 * n + 4 * a:4 * n + 4 * a + 4], (g, d, mo, vo)):
                out[...] = val

    spec = lambda w: pl.BlockSpec((w.shape[0] // steps, w.shape[1]), lambda i: (i, 0))
    flat = pl.pallas_call(
        body, name=name, grid=(steps,), in_specs=[spec(it[0]) for it in items for _ in range(4)],
        out_specs=[spec(it[0]) for it in items for _ in range(4)],
        out_shape=[jax.ShapeDtypeStruct(it[0].shape, F32) for it in items for _ in range(4)],
        compiler_params=_params(("arbitrary",)),
    )(*[a for it in items for a in it])
    return [flat[4 * a:4 * a + 4] for a in range(n)]


def _adamw_small(sm4, wmv):
    views = SMALL_VIEWS[:-1]
    n = len(views)

    def body(sm4_ref, *refs):
        g_all = ((sm4_ref[0] + sm4_ref[1]) + sm4_ref[2]) + sm4_ref[3]
        row = 0
        for a, (_, rows, cols) in enumerate(views):
            g = g_all[row:row + rows, :cols]
            row += -(-rows // ROW_TILE) * ROW_TILE
            d, mo, vo = _adamw_math(refs[3 * a][...], g, refs[3 * a + 1][...], refs[3 * a + 2][...])
            for out, val in zip(refs[3 * n + 4 * a:3 * n + 4 * a + 4], (g, d, mo, vo)):
                out[...] = val
        refs[-1][...] = g_all[row:row + 1, :128]

    flat = pl.pallas_call(
        body, name="adamw_small",
        out_shape=[jax.ShapeDtypeStruct((rows, cols), F32) for _, rows, cols in views for _ in range(4)]
        + [jax.ShapeDtypeStruct((1, 128), F32)],
        compiler_params=pltpu.CompilerParams(vmem_limit_bytes=VMEM_LIMIT),
    )(sm4, *[a for t in wmv for a in t])
    return [flat[4 * a:4 * a + 4] for a in range(n)] + [flat[-1]]


SMALL_NAMES = ("attn_pre_norm", "mla_q_norm", "mla_kv_norm", "mla_w_ukv", "mla_out_norm", "hgrn_lb_logits",
               "hgrn_out_norm", "attn_post_norm", "ffn_pre_norm", "ffn_post_norm")
BIG_NAMES = ("w_in", "mla_w_uq", "w_out", "w_gate", "w_up", "w_down")
WEIGHT_NAMES = ("attn_pre_norm", "w_in", "mla_q_norm", "mla_w_uq", "mla_kv_norm", "mla_w_ukv", "mla_out_norm",
                "hgrn_lb_logits", "hgrn_out_norm", "w_out", "attn_post_norm", "ffn_pre_norm", "w_gate", "w_up", "w_down",
                "ffn_post_norm")


UQ_COMM_SHAPE = (192, 384)


def _pack_small(vals):
    parts = []
    for name, rows, cols in SMALL_VIEWS:
        pad_rows = -(-rows // ROW_TILE) * ROW_TILE - rows
        parts.append(jnp.pad(vals[name].reshape(rows, cols), ((0, pad_rows), (0, D_MODEL - cols))))
    return jnp.concatenate(parts, axis=0)


def kernel(x, positions, attn_pre_norm, w_in, mla_q_norm, mla_w_uq, mla_kv_norm, mla_w_ukv, mla_out_norm, hgrn_lb_logits, hgrn_out_norm, w_out, attn_post_norm, ffn_pre_norm, w_gate, w_up, w_down, ffn_post_norm, loss_target, m_attn_pre_norm, m_w_in, m_mla_q_norm, m_mla_w_uq, m_mla_kv_norm, m_mla_w_ukv, m_mla_out_norm, m_hgrn_lb_logits, m_hgrn_out_norm, m_w_out, m_attn_post_norm, m_ffn_pre_norm, m_w_gate, m_w_up, m_w_down, m_ffn_post_norm, v_attn_pre_norm, v_w_in, v_mla_q_norm, v_mla_w_uq, v_mla_kv_norm, v_mla_w_ukv, v_mla_out_norm, v_hgrn_lb_logits, v_hgrn_out_norm, v_w_out, v_attn_post_norm, v_ffn_pre_norm, v_w_gate, v_w_up, v_w_down, v_ffn_post_norm):
    args = locals()
    W = {n: args[n] for n in WEIGHT_NAMES}
    M = {n: args["m_" + n] for n in WEIGHT_NAMES}
    V = {n: args["v_" + n] for n in WEIGHT_NAMES}
    T = x.shape[1]
    cx, cy, cc = lax.axis_index("x"), lax.axis_index("y"), lax.axis_index("c")

    win_rows = D_IN // N_CHIPS
    shard2d = {"w_in": (win_rows, D_MODEL), "mla_w_uq": (Q_RANK // N_CHIPS, MLA_HEADS * MLA_QK),
               "w_out": (D_MODEL // N_CHIPS, D_MODEL), "w_gate": (FF_SHARD, D_MODEL), "w_up": (FF_SHARD, D_MODEL),
               "w_down": (FF_SHARD, D_MODEL)}
    transposed = ("w_in", "w_gate", "w_up")
    to2d = lambda n, a: a[0].T if n in transposed else a.reshape(shard2d[n])
    from2d = lambda n, t: t.T[None] if n in transposed else t.reshape(W[n].shape)
    me = 2 * cx + cy
    place = jnp.stack([cc, me]).astype(jnp.int32)
    local_b = [to2d(n, W[n]).astype(BF16) for n in BIG_NAMES]
    local_b[0] = jnp.pad(local_b[0], ((0, FF_SHARD - win_rows), (0, 0)))
    stacks = _gather_chips(local_b[:2], "gather_weights")
    win4, wuq4 = [lax.dynamic_update_slice(s, l[None], (me, 0, 0)) for s, l in zip(stacks, local_b)]
    win_t = win4[:, :win_rows].reshape(D_IN, D_MODEL)
    wuq_full = wuq4.reshape(Q_RANK, MLA_HEADS, MLA_QK)
    win_arr, wq_arr, wk_arr, wv_arr = _arrange_weights(win_t, wuq_full, mla_w_ukv[0].astype(BF16))
    small = {n: W[n][0] if n == "mla_w_ukv" else W[n].reshape(-1, W[n].shape[-1]) for n in SMALL_NAMES}

    loss_local, dx, grads, ffn_final = _local_step(x[0], positions.reshape(T, 1), loss_target[0], small, win_arr,
                                                           wq_arr, wk_arr, wv_arr, local_b[2:], place)

    gs = [grads["w_in"], grads["mla_w_uq"].reshape((N_CHIPS,) + UQ_COMM_SHAPE), grads["w_out"].reshape((N_CHIPS,) + shard2d["w_out"])]
    sm = _pack_small({**grads, "loss": loss_local})
    *rs, ssib = _pair_swap(gs, sm)
    *ps, pair = _pair_sum(place, gs, rs, small=(sm, ssib))
    ffn_names, rest_names = BIG_NAMES[3:], BIG_NAMES[:3]
    g2d = dict(zip(ffn_names, ffn_final))
    adam_in = lambda names_: [(to2d(n, W[n]), g2d[n], to2d(n, M[n]), to2d(n, V[n])) for n in names_]
    updates = dict(zip(ffn_names, _adamw(adam_in(ffn_names), 8, "adamw_ffn")))
    *ris, sm4 = _chip_swap(ps, pair)
    *gfin, smf = _pair_fill(_chip_sum(place, gs, rs, ris), sm4)

    g2d.update({n: gfin[k].reshape((-1,) + shard2d[n][1:]) for k, n in enumerate(rest_names)})
    updates.update(zip(rest_names[:2], _adamw(adam_in(rest_names[:2]), 3, "adamw_w_in")))
    updates.update(zip(rest_names[2:], _adamw(adam_in(rest_names[2:]), 8, "adamw_w_out")))
    G, DW, NM, NV = {}, {}, {}, {}
    for n, outs in updates.items():
        G[n], DW[n], NM[n], NV[n] = (from2d(n, t) for t in outs)
    view2d = lambda n, a: a.reshape(next((r, c) for name, r, c in SMALL_VIEWS if name == n))
    *res, loss_row = _adamw_small(smf, [tuple(view2d(n, t[n]) for t in (W, M, V)) for n in SMALL_NAMES])
    for n, outs in zip(SMALL_NAMES, res):
        G[n], DW[n], NM[n], NV[n] = (t.reshape(W[n].shape) for t in outs)
    loss = loss_row[0, 0]
    return (loss, dx[None], *[G[n] for n in WEIGHT_NAMES], *[DW[n] for n in WEIGHT_NAMES],
            *[NM[n] for n in WEIGHT_NAMES], *[NV[n] for n in WEIGHT_NAMES])
```

```python
import jax
import jax.numpy as jnp
from jax import lax
from jax.experimental import pallas as pl
from jax.experimental.pallas import tpu as pltpu

F32 = jnp.float32
BF16 = jnp.bfloat16
MXU_DTYPE = BF16

D_MODEL = 1024
MLA_HEADS = 8
MLA_NOPE = 64
MLA_ROPE = 32
MLA_V = 64
MLA_QK = MLA_NOPE + MLA_ROPE
Q_RANK = 384
KV_RANK = 128
MLA_WIDTH = MLA_HEADS * MLA_V
HEAD_PAD = 128
HGRN_HEADS = 4
HGRN_DIM = 128
HGRN_WIDTH = HGRN_HEADS * HGRN_DIM
CHUNK = 64
SUB = 16
HGRN_CPI = 4
D_IN = Q_RANK + KV_RANK + MLA_ROPE + 4 * HGRN_WIDTH
D_IN_ARR = Q_RANK + KV_RANK + HEAD_PAD + 4 * HGRN_WIDTH
D_FF = 2816
N_CHIPS = 4
FF_SHARD = D_FF // N_CHIPS
EPS = 1e-6
ROPE_THETA = 10000.0
ATTN_SCALE = MLA_QK ** -0.5
ATTN_SCALE_LOG2 = ATTN_SCALE * 1.4426950408889634
NEG_BIG = -1e30

ADAM_LR = 0.001
ADAM_B1 = 0.9
ADAM_B2 = 0.999
ADAM_EPS = 1e-08
ADAM_WD = 0.01
ADAM_STEP = 10

VMEM_LIMIT = 56 * 1024 * 1024

SMALL_VIEWS = (("attn_pre_norm", 1, 1024), ("mla_q_norm", 1, 384), ("mla_kv_norm", 1, 128), ("mla_w_ukv", 128, 1024),
               ("mla_out_norm", 1, 512), ("hgrn_lb_logits", 2, 512), ("hgrn_out_norm", 1, 512),
               ("attn_post_norm", 1, 1024), ("ffn_pre_norm", 1, 1024), ("ffn_post_norm", 1, 1024), ("loss", 1, 1))
ROW_TILE = 8
SMALL_ROWS = sum(-(-rows // ROW_TILE) * ROW_TILE for _, rows, _ in SMALL_VIEWS)

MESH = pl.DeviceIdType.MESH
ANY = pl.BlockSpec(memory_space=pl.ANY)


def _dot(a, b, dims, exact):
    if exact:
        return lax.dot_general(a.astype(F32), b.astype(F32), (dims, ((), ())), precision=lax.Precision.HIGH,
                               preferred_element_type=F32)
    return lax.dot_general(a.astype(MXU_DTYPE), b.astype(MXU_DTYPE), (dims, ((), ())), preferred_element_type=F32)


def _mm(a, b, exact=False):
    return _dot(a, b, ((1,), (0,)), exact)


def _mm_nt(a, b, exact=False):
    return _dot(a, b, ((1,), (1,)), exact)


def _mm_tn(a, b, exact=False):
    return _dot(a, b, ((0,), (0,)), exact)


def _rms_fwd(x, w):
    r = lax.rsqrt(jnp.mean(x * x, axis=-1, keepdims=True) + EPS)
    xn = x * r
    return xn * w, xn, r


def _rms_bwd(dy, xn, r, w):
    dxn = dy * w
    dx = r * (dxn - xn * jnp.mean(dxn * xn, axis=-1, keepdims=True))
    dw = jnp.sum(dy * xn, axis=0, keepdims=True)
    return dx, dw


def _group_sums(v, gs):
    t, n = v.shape
    lane = lax.broadcasted_iota(jnp.int32, (t, 128), 1)
    out = []
    for p in range(n // 128):
        vb = v[:, 128 * p:128 * (p + 1)]
        if gs == 128:
            out.append(jnp.sum(vb, axis=-1, keepdims=True))
        else:
            out.append(jnp.sum(jnp.where(lane < 64, vb, 0.0), axis=-1, keepdims=True))
            out.append(jnp.sum(jnp.where(lane >= 64, vb, 0.0), axis=-1, keepdims=True))
    return out


def _group_bcast(sums, gs, t):
    lane = lax.broadcasted_iota(jnp.int32, (t, 128), 1)
    if gs == 128:
        return jnp.concatenate([jnp.broadcast_to(s, (t, 128)) for s in sums], axis=-1)
    return jnp.concatenate([jnp.where(lane < 64, sums[2 * p], sums[2 * p + 1]) for p in range(len(sums) // 2)],
                           axis=-1)


def _grms_fwd(x, w, gs):
    t = x.shape[0]
    r = lax.rsqrt(_group_bcast(_group_sums(x * x, gs), gs, t) * (1.0 / gs) + EPS)
    xn = x * r
    return xn * w, xn, r


def _grms_bwd(dy, xn, r, w, gs):
    t = dy.shape[0]
    dxn = dy * w
    dx = r * (dxn - xn * (_group_bcast(_group_sums(dxn * xn, gs), gs, t) * (1.0 / gs)))
    dw = jnp.sum(dy * xn, axis=0, keepdims=True)
    return dx, dw


def _rope_tables(c_tab, s_tab):
    lane = lax.broadcasted_iota(jnp.int32, c_tab.shape, 1)
    first = (lane >= MLA_NOPE) & (lane < MLA_NOPE + MLA_ROPE // 2)
    second = (lane >= MLA_NOPE + MLA_ROPE // 2) & (lane < MLA_QK)
    return c_tab, jnp.where(first, -s_tab, 0.0), jnp.where(second, s_tab, 0.0)


def _rope(v, c, sa, sb):
    return v * c + pltpu.roll(v, HEAD_PAD - MLA_ROPE // 2, 1) * sa + pltpu.roll(v, MLA_ROPE // 2, 1) * sb


def _rope_bwd(d, c, sa, sb):
    return d * c - pltpu.roll(d, HEAD_PAD - MLA_ROPE // 2, 1) * sa - pltpu.roll(d, MLA_ROPE // 2, 1) * sb


def _params(sem, vmem=VMEM_LIMIT):
    return pltpu.CompilerParams(dimension_semantics=sem, vmem_limit_bytes=vmem)


def _in_fwd(x, pos, invf, w_pre, win, qnw, wq, kvnw, wk, wv, tt=512):
    T = x.shape[0]

    def body(x_ref, pos_ref, invf_ref, wpre_ref, win_ref, qnw_ref, wq_ref, kvnw_ref, wk_ref, wv_ref,
             cq_ref, ckv_ref, xph_ref, q_ref, k_ref, v_ref, kt_ref, vt_ref, rc_ref, rs_ref):
        u, _, _ = _rms_fwd(x_ref[...], wpre_ref[...])
        lo = Q_RANK + KV_RANK + HEAD_PAD
        xp = _mm_nt(u, win_ref[:lo, :])
        xph_ref[...] = _mm_nt(u, win_ref[lo:, :])
        cq = xp[:, :Q_RANK]
        ckv = xp[:, Q_RANK:Q_RANK + KV_RANK]
        kr = xp[:, Q_RANK + KV_RANK:]
        cq_ref[...] = cq
        ckv_ref[...] = ckv
        ang = pos_ref[...].astype(F32) * invf_ref[...]
        c_tab = jnp.cos(ang)
        s_tab = jnp.sin(ang)
        rc_ref[...] = c_tab
        rs_ref[...] = s_tab
        c, sa, sb = _rope_tables(c_tab, s_tab)
        qn, _, _ = _rms_fwd(cq, qnw_ref[...])
        q = _mm(qn, wq_ref[...])
        kvn, _, _ = _rms_fwd(ckv, kvnw_ref[...])
        kn = _mm(kvn, wk_ref[...])
        v = _mm(kvn, wv_ref[...])
        v_ref[...] = v.astype(v_ref.dtype)
        vt_ref[...] = v.T.astype(vt_ref.dtype)
        krr = _rope(kr, c, sa, sb)
        for h in range(MLA_HEADS):
            sl = slice(HEAD_PAD * h, HEAD_PAD * (h + 1))
            q_ref[:, sl] = (_rope(q[:, sl], c, sa, sb) * ATTN_SCALE_LOG2).astype(q_ref.dtype)
            kh = kn[:, sl] + krr
            k_ref[:, sl] = kh.astype(k_ref.dtype)
            kt_ref[sl, :] = kh.T.astype(kt_ref.dtype)

    row = lambda w: pl.BlockSpec((tt, w), lambda i: (i, 0))
    full = lambda a: pl.BlockSpec(a.shape, lambda i: (0,) * a.ndim)
    qk_w = MLA_HEADS * HEAD_PAD
    return pl.pallas_call(
        body, name="in_fwd", grid=(T // tt,),
        in_specs=[row(D_MODEL), row(1), full(invf), full(w_pre), full(win), full(qnw), full(wq), full(kvnw),
                  full(wk), full(wv)],
        out_specs=[row(Q_RANK), row(KV_RANK), row(4 * HGRN_WIDTH), row(qk_w), row(qk_w), row(MLA_WIDTH),
                   pl.BlockSpec((qk_w, tt), lambda i: (0, i)), pl.BlockSpec((MLA_WIDTH, tt), lambda i: (0, i)),
                   row(HEAD_PAD), row(HEAD_PAD)],
        out_shape=[jax.ShapeDtypeStruct((T, Q_RANK), F32), jax.ShapeDtypeStruct((T, KV_RANK), F32),
                   jax.ShapeDtypeStruct((T, 4 * HGRN_WIDTH), F32), jax.ShapeDtypeStruct((T, qk_w), MXU_DTYPE),
                   jax.ShapeDtypeStruct((T, qk_w), MXU_DTYPE), jax.ShapeDtypeStruct((T, MLA_WIDTH), MXU_DTYPE),
                   jax.ShapeDtypeStruct((qk_w, T), MXU_DTYPE), jax.ShapeDtypeStruct((MLA_WIDTH, T), MXU_DTYPE),
                   jax.ShapeDtypeStruct((T, HEAD_PAD), F32), jax.ShapeDtypeStruct((T, HEAD_PAD), F32)],
        compiler_params=_params(("arbitrary",)),
    )(x, pos, invf, w_pre, win, qnw, wq, kvnw, wk, wv)


def _attn_fwd_t(qb, kb, vt, gather=(), tq=256, hps=8):
    T = qb.shape[0]
    nq = T // tq
    ng = len(gather)
    steps = (MLA_HEADS // hps) * nq
    pass_on = steps - 3

    def body(q_ref, k_ref, vt_ref, *rest):
        o_ref, lse_ref = rest[ng:ng + 2]
        acc_scr = rest[2 * ng + 2]
        qi = pl.program_id(1)
        step_no = pl.program_id(0) * nq + qi
        if ng:
            gat = _Gather(rest[:ng], rest[ng + 2:2 * ng + 2], *rest[2 * ng + 3:])

            @pl.when(step_no == 0)
            def _():
                for cp in gat.sends():
                    cp.start()

            @pl.when(step_no == pass_on)
            def _():
                for arrival in gat.arrivals():
                    arrival.wait_recv()
                for cp in gat.forwards():
                    cp.start()

        heads = [slice(HEAD_PAD * a, HEAD_PAD * (a + 1)) for a in range(hps)]
        acc_scr[...] = jnp.zeros_like(acc_scr)

        def step(j, carry, masked):
            start = pl.multiple_of(j * tq, tq)
            scores = [_mm_nt(k_ref[pl.ds(start, tq), heads[a]], q_ref[:, heads[a]]) for a in range(hps)]
            new = []
            for a in range(hps):
                m, l = carry[a]
                s = scores[a]
                if masked:
                    kk = lax.broadcasted_iota(jnp.int32, (tq, tq), 0)
                    qq = lax.broadcasted_iota(jnp.int32, (tq, tq), 1)
                    s = jnp.where(kk <= qq, s, NEG_BIG)
                m_new = jnp.maximum(m, jnp.max(s, axis=0, keepdims=True))
                alpha = jnp.exp2(m - m_new)
                p = jnp.exp2(s - m_new)
                l = l * alpha + jnp.sum(p, axis=0, keepdims=True)
                vtj = vt_ref[2 * MLA_V * (a // 2):2 * MLA_V * (a // 2 + 1), pl.ds(start, tq)]
                acc_scr[a] = acc_scr[a] * alpha + _mm(vtj, p)
                new.append((m_new, l))
            return tuple(new)

        init = tuple((jnp.full((1, tq), NEG_BIG, F32), jnp.zeros((1, tq), F32)) for _ in range(hps))
        carry = lax.fori_loop(0, qi, lambda j, c: step(j, c, False), init)
        carry = step(qi, carry, True)
        row = lax.broadcasted_iota(jnp.int32, (2 * MLA_V, tq), 0)
        for pr in range(hps // 2):
            (m0, l0), (m1, l1) = carry[2 * pr], carry[2 * pr + 1]
            ot = jnp.where(row < MLA_V, acc_scr[2 * pr] / l0, acc_scr[2 * pr + 1] / l1)
            o_ref[:, 2 * MLA_V * pr:2 * MLA_V * (pr + 1)] = ot.T
            lse_ref[pr, 0:1, :] = m0 + jnp.log2(l0)
            lse_ref[pr, 1:2, :] = m1 + jnp.log2(l1)

        if ng:
            @pl.when(step_no == steps - 1)
            def _():
                for arrival in gat.forward_arrivals():
                    arrival.wait_recv()
                for cp in gat.sends() + gat.forwards():
                    cp.wait_send()

    return pl.pallas_call(
        body, name="attn_fwd", grid=(MLA_HEADS // hps, nq),
        in_specs=[pl.BlockSpec((tq, hps * HEAD_PAD), lambda g, i: (i, g)),
                  pl.BlockSpec((T, hps * HEAD_PAD), lambda g, i: (0, g)),
                  pl.BlockSpec((hps * MLA_V, T), lambda g, i: (g, 0))] + [ANY] * ng,
        out_specs=[pl.BlockSpec((tq, hps * MLA_V), lambda g, i: (i, g)),
                   pl.BlockSpec((hps // 2, 2, tq), lambda g, i: (g, 0, i))] + [ANY] * ng,
        out_shape=[jax.ShapeDtypeStruct((T, MLA_WIDTH), F32), jax.ShapeDtypeStruct((MLA_HEADS // 2, 2, T), F32)]
        + _Gather.out_shapes(gather),
        scratch_shapes=[pltpu.VMEM((hps, 2 * MLA_V, tq), F32)] + (_Gather.semaphores(gather) if ng else []),
        compiler_params=_params(("arbitrary", "arbitrary")),
    )(qb, kb, vt, *gather)


def _attn_bwd_t(qb, kb, kt, vb, dob, lse, dvec, send=(), tq=512, hps=4):
    T = qb.shape[0]
    nq = T // tq
    ns = len(send)
    steps = (MLA_HEADS // hps) * nq

    def body(q_ref, k_ref, kt_ref, v_ref, do_ref, lse_ref, d_ref, *rest):
        dqt_ref, dk_ref, dv_ref = rest[ns:ns + 3]
        va_scr, dv_scr = rest[2 * ns + 3:2 * ns + 5]
        j = pl.program_id(1)
        step_no = pl.program_id(0) * nq + j
        if ns:
            @pl.when(step_no == 0)
            def _():
                for cp in _chip_swap_copies(rest[:ns], rest[ns + 3:2 * ns + 3], *rest[2 * ns + 5:]):
                    cp.start()

        @pl.when(j == 0)
        def _():
            dqt_ref[...] = jnp.zeros_like(dqt_ref)

        lane = lax.broadcasted_iota(jnp.int32, (tq, 2 * MLA_V), 1)
        heads = [slice(HEAD_PAD * a, HEAD_PAD * (a + 1)) for a in range(hps)]
        pairs = [slice(2 * MLA_V * p, 2 * MLA_V * (p + 1)) for p in range(hps // 2)]
        for pr in range(hps // 2):
            vpair = v_ref[:, pairs[pr]]
            va_scr[2 * pr] = jnp.where(lane < MLA_V, vpair, jnp.zeros_like(vpair))
            va_scr[2 * pr + 1] = jnp.where(lane >= MLA_V, vpair, jnp.zeros_like(vpair))
        dk_ref[...] = jnp.zeros_like(dk_ref)
        dv_scr[...] = jnp.zeros_like(dv_scr)

        def step(i, masked):
            start = pl.multiple_of(i * tq, tq)
            rows = pl.ds(start, tq)
            scores = [_mm_nt(k_ref[:, heads[a]], q_ref[rows, heads[a]]) for a in range(hps)]
            dps = [_mm_nt(va_scr[a], do_ref[rows, pairs[a // 2]]) for a in range(hps)]
            for a in range(hps):
                pr, r = a // 2, a % 2
                p = jnp.exp2(scores[a] - lse_ref[pr, r:r + 1, rows])
                if masked:
                    kk = lax.broadcasted_iota(jnp.int32, (tq, tq), 0)
                    qq = lax.broadcasted_iota(jnp.int32, (tq, tq), 1)
                    p = jnp.where(kk <= qq, p, 0.0)
                ds = p * (dps[a] - d_ref[pr, r:r + 1, rows])
                dv_scr[a] += _mm(p, do_ref[rows, pairs[pr]])
                dk_ref[:, heads[a]] += _mm(ds, q_ref[rows, heads[a]])
                dqt_ref[heads[a], rows] += _mm(kt_ref[heads[a], :], ds)

        def loop_body(i, _):
            step(i, False)
            return 0

        step(j, True)
        lax.fori_loop(j + 1, nq, loop_body, 0)
        for pr in range(hps // 2):
            dv_ref[:, pairs[pr]] = jnp.where(lane < MLA_V, dv_scr[2 * pr], dv_scr[2 * pr + 1])
        dk_ref[...] = dk_ref[...] * (ATTN_SCALE / ATTN_SCALE_LOG2)

        if ns:
            @pl.when(step_no == steps - 1)
            def _():
                for cp in _chip_swap_copies(rest[:ns], rest[ns + 3:2 * ns + 3], *rest[2 * ns + 5:]):
                    cp.wait()

    stat = pl.BlockSpec((hps // 2, 2, T), lambda g, j: (g, 0, 0))
    return pl.pallas_call(
        body, name="attn_bwd", grid=(MLA_HEADS // hps, nq),
        in_specs=[pl.BlockSpec((T, hps * HEAD_PAD), lambda g, j: (0, g)),
                  pl.BlockSpec((tq, hps * HEAD_PAD), lambda g, j: (j, g)),
                  pl.BlockSpec((hps * HEAD_PAD, tq), lambda g, j: (g, j)),
                  pl.BlockSpec((tq, hps * MLA_V), lambda g, j: (j, g)),
                  pl.BlockSpec((T, hps * MLA_V), lambda g, j: (0, g)), stat, stat] + [ANY] * ns,
        out_specs=[pl.BlockSpec((hps * HEAD_PAD, T), lambda g, j: (g, 0)),
                   pl.BlockSpec((tq, hps * HEAD_PAD), lambda g, j: (j, g)),
                   pl.BlockSpec((tq, hps * MLA_V), lambda g, j: (j, g))] + [ANY] * ns,
        out_shape=[jax.ShapeDtypeStruct((MLA_HEADS * HEAD_PAD, T), F32),
                   jax.ShapeDtypeStruct((T, MLA_HEADS * HEAD_PAD), F32),
                   jax.ShapeDtypeStruct((T, MLA_WIDTH), F32)] + _chip_swap_shapes(send),
        scratch_shapes=[pltpu.VMEM((hps, tq, 2 * MLA_V), vb.dtype), pltpu.VMEM((hps, tq, 2 * MLA_V), F32)]
        + ([pltpu.SemaphoreType.DMA((3 * ns,)), pltpu.SemaphoreType.DMA((3 * ns,))] if ns else []),
        compiler_params=_params(("arbitrary", "arbitrary")),
    )(qb, kb, kt, vb, dob, lse, dvec, *send)


def _cumsum_rows(x):
    n = x.shape[0]
    row = lax.broadcasted_iota(jnp.int32, x.shape, 0)
    s = 1
    while s < n:
        x = x + jnp.where(row >= s, pltpu.roll(x, s, 0), 0.0)
        s *= 2
    return x


def _rev_cumsum_rows(x):
    n = x.shape[0]
    row = lax.broadcasted_iota(jnp.int32, x.shape, 0)
    s = 1
    while s < n:
        x = x + jnp.where(row < n - s, pltpu.roll(x, n - s, 0), 0.0)
        s *= 2
    return x


def _lb_from_logits(l):
    l0, l1 = l[0:1, :], l[1:2, :]
    m = jnp.maximum(l0, l1)
    e0, e1 = jnp.exp(l0 - m), jnp.exp(l1 - m)
    return e0 / (e0 + e1)


def _hgrn_gates(hq, hf, lb):
    sig_f = jax.nn.sigmoid(hf)
    f = lb + (1.0 - lb) * sig_f
    sig_q = jax.nn.sigmoid(hq)
    return sig_f, f, jnp.log(f), 1.0 - f, sig_q, hq * sig_q


def _hgrn_intra(q, kk, b, exact=False):
    row = lax.broadcasted_iota(jnp.int32, b.shape, 0)
    qs, ks, eqs, eks, a_rows = [], [], [], [], []
    for i in range(CHUNK // SUB):
        ref = b[SUB * i + SUB // 2:SUB * i + SUB // 2 + 1, :]
        eq = jnp.exp(b[SUB * i:SUB * (i + 1), :] - ref)
        ek = jnp.exp(jnp.where(row < SUB * (i + 1), ref - b, NEG_BIG))
        qi = q[SUB * i:SUB * (i + 1), :] * eq
        ki = kk * ek
        a_rows.append(_mm_nt(qi, ki, exact))
        qs.append(qi), ks.append(ki), eqs.append(eq), eks.append(ek)
    tt = lax.broadcasted_iota(jnp.int32, (CHUNK, CHUNK), 0)
    ss = lax.broadcasted_iota(jnp.int32, (CHUNK, CHUNK), 1)
    causal = ss <= tt
    a = jnp.where(causal, jnp.concatenate(a_rows, axis=0), 0.0)
    return a, causal, qs, ks, eqs, eks


def _hgrn_fwd(xph, lbl, tg=512):
    T = xph.shape[0]
    ng, ncg = T // tg, tg // CHUNK
    cols = [slice(HGRN_DIM * h, HGRN_DIM * (h + 1)) for h in range(HGRN_HEADS)]

    def body(lbl_ref, hq_ref, hf_ref, hi_ref, o_ref, st_ref, s_scr):
        @pl.when(pl.program_id(0) == 0)
        def _():
            s_scr[...] = jnp.zeros_like(s_scr)

        lb = _lb_from_logits(lbl_ref[...])

        def chunks(it, _):
            pre = []
            for k in range(HGRN_CPI):
                c = it * HGRN_CPI + k
                rows = pl.ds(pl.multiple_of(c * CHUNK, CHUNK), CHUNK)
                for cs in cols:
                    _, _, lf, kk, _, q = _hgrn_gates(hq_ref[rows, cs], hf_ref[rows, cs], lb[:, cs])
                    v = hi_ref[rows, cs]
                    b = _cumsum_rows(lf)
                    a = _hgrn_intra(q, kk, b)[0]
                    b_last = b[CHUNK - 1:CHUNK, :]
                    pre.append((c, rows, q * jnp.exp(b), a, v, jnp.exp(b_last), _mm_tn(v, kk * jnp.exp(b_last - b))))
            for i, (c, rows, qe, a, v, ebl, upd) in enumerate(pre):
                h = i % HGRN_HEADS
                st = s_scr[h]
                st_ref[h, c] = st
                o_ref[rows, cols[h]] = _mm_nt(qe, st) + _mm(a, v)
                s_scr[h] = st * ebl + upd
            return 0

        lax.fori_loop(0, ncg // HGRN_CPI, chunks, 0)

    col = lambda k: pl.BlockSpec((tg, HGRN_WIDTH), lambda g: (g, k))
    return pl.pallas_call(
        body, name="hgrn_fwd", grid=(ng,),
        in_specs=[pl.BlockSpec((2, HGRN_WIDTH), lambda g: (0, 0)), col(0), col(1), col(2)],
        out_specs=[col(0), pl.BlockSpec((HGRN_HEADS, ncg, HGRN_DIM, HGRN_DIM), lambda g: (0, g, 0, 0))],
        out_shape=[jax.ShapeDtypeStruct((T, HGRN_WIDTH), F32),
                   jax.ShapeDtypeStruct((HGRN_HEADS, T // CHUNK, HGRN_DIM, HGRN_DIM), F32)],
        scratch_shapes=[pltpu.VMEM((HGRN_HEADS, HGRN_DIM, HGRN_DIM), F32)],
        compiler_params=_params(("arbitrary",)),
    )(lbl, xph, xph, xph)


def _hgrn_bwd(xph, lbl, states, d_o, fill=(), tg=512):
    T = xph.shape[0]
    ng, ncg = T // tg, tg // CHUNK
    cols = [slice(HGRN_DIM * h, HGRN_DIM * (h + 1)) for h in range(HGRN_HEADS)]
    nsub = CHUNK // SUB
    nf = len(fill)

    def body(lbl_ref, hq_ref, hf_ref, hi_ref, st_ref, do_ref, *rest):
        dhq_ref, dhf_ref, dhi_ref, dlg_ref = rest[nf:nf + 4]
        ds_scr, dlb_scr = rest[2 * nf + 4:2 * nf + 6]
        fill_copies = lambda: _pair_fill_copies(rest[nf + 4:2 * nf + 4], *rest[2 * nf + 6:])
        g = pl.program_id(0)

        @pl.when(g == 0)
        def _():
            ds_scr[...] = jnp.zeros_like(ds_scr)
            dlb_scr[...] = jnp.zeros_like(dlb_scr)
            for cp in (fill_copies()[0] if nf else ()):
                cp.start()

        lb = _lb_from_logits(lbl_ref[...])

        def chunks(it, _):
            pre = []
            for k, h in ((k, h) for k in range(HGRN_CPI) for h in range(HGRN_HEADS)):
                cs = cols[h]
                c = ncg - 1 - (it * HGRN_CPI + k)
                rows = pl.ds(pl.multiple_of(c * CHUNK, CHUNK), CHUNK)
                hq = hq_ref[rows, cs]
                sig_f, f, lf, kk, sig_q, q = _hgrn_gates(hq, hf_ref[rows, cs], lb[:, cs])
                v = hi_ref[rows, cs]
                do = do_ref[rows, cs]
                b = _cumsum_rows(lf)
                eb = jnp.exp(b)
                a, causal, qs, ks, eqs, eks = _hgrn_intra(q, kk, b)
                b_last = b[CHUNK - 1:CHUNK, :]
                st = st_ref[h, c]
                pre.append(dict(h=h, cs=cs, rows=rows, hq=hq, sig_f=sig_f, f=f, kk=kk, sig_q=sig_q, q=q, v=v, eb=eb, qs=qs,
                                ks=ks, eqs=eqs,
                                eks=eks, ebl=jnp.exp(b_last), el=jnp.exp(b_last - b), st=st,
                                da=jnp.where(causal, _mm_nt(do, v, True), 0.0), dq=_mm(do, st, True) * eb,
                                dv=_mm_tn(a, do), dsu=_mm_tn(do, q * eb, True)))
            for w in pre:
                dq_rows = []
                dk = jnp.zeros_like(w["q"])
                for i in range(nsub):
                    dai = w["da"][SUB * i:SUB * (i + 1), :]
                    dq_rows.append(_mm(dai, w["ks"][i], True) * w["eqs"][i])
                    dk = dk + _mm_tn(dai, w["qs"][i], True) * w["eks"][i]
                w["dq"] = w["dq"] + jnp.concatenate(dq_rows, axis=0)
                w["dk"] = dk
            for w in pre:
                h, cs, rows = w["h"], w["cs"], w["rows"]
                kk, el, ebl, dst = w["kk"], w["el"], w["ebl"], ds_scr[h]
                dk_state = _mm(w["v"], dst, True) * el
                dk = w["dk"] + dk_state
                e_last = (ebl * jnp.sum(w["st"] * dst, axis=0, keepdims=True)
                          + jnp.sum(kk * dk_state, axis=0, keepdims=True))
                dlf = _rev_cumsum_rows(w["q"] * w["dq"] - kk * dk) + e_last
                ds_scr[h] = dst * ebl + w["dsu"]
                df = dlf / w["f"] - dk
                sig_f, sig_q = w["sig_f"], w["sig_q"]
                dhf_ref[rows, cs] = df * (1.0 - lb[:, cs]) * sig_f * (1.0 - sig_f)
                dlb_scr[:, cs] += jnp.sum(df * (1.0 - sig_f), axis=0, keepdims=True)
                dhq_ref[rows, cs] = w["dq"] * sig_q * (1.0 + w["hq"] * (1.0 - sig_q))
                dhi_ref[rows, cs] = w["dv"] + _mm_nt(kk * el, dst)
            return 0

        lax.fori_loop(0, ncg // HGRN_CPI, chunks, 0)

        @pl.when(g == ng - 1)
        def _():
            dl0 = dlb_scr[...] * lb * (1.0 - lb)
            dlg_ref[...] = jnp.concatenate([dl0, -dl0], axis=0)
            if nf:
                copies, waits = fill_copies()
                for w in waits:
                    w.wait_recv()
                for cp in copies:
                    cp.wait_send()

    col = lambda k: pl.BlockSpec((tg, HGRN_WIDTH), lambda g: (ng - 1 - g, k))
    logits = pl.BlockSpec((2, HGRN_WIDTH), lambda g: (0, 0))
    big = jax.ShapeDtypeStruct((T, HGRN_WIDTH), F32)
    n_in, n_out = 6, 4
    return pl.pallas_call(
        body, name="hgrn_bwd", grid=(ng,),
        in_specs=[logits, col(0), col(1), col(2),
                  pl.BlockSpec((HGRN_HEADS, ncg, HGRN_DIM, HGRN_DIM), lambda g: (0, ng - 1 - g, 0, 0)), col(0)] + [ANY] * nf,
        out_specs=[col(0), col(0), col(0), logits] + [ANY] * nf,
        out_shape=[big, big, big, jax.ShapeDtypeStruct((2, HGRN_WIDTH), F32)]
        + [jax.ShapeDtypeStruct(f.shape, f.dtype) for f in fill],
        input_output_aliases={n_in + k: n_out + k for k in range(nf)},
        scratch_shapes=[pltpu.VMEM((HGRN_HEADS, HGRN_DIM, HGRN_DIM), F32), pltpu.VMEM((1, HGRN_WIDTH), F32)]
        + ([pltpu.SemaphoreType.DMA((nf,)), pltpu.SemaphoreType.DMA((nf,))] if nf else []),
        compiler_params=_params(("arbitrary",)),
    )(lbl, xph, xph, xph, states, d_o, *fill)


def _proj_fwd(x, o_raw, oh_raw, xph, wout, w_mla, w_hg, w_post, w_fpre, tt=512):
    T = x.shape[0]

    def body(x_ref, o_ref, oh_ref, hg_ref, wout_ref, wmla_ref, whg_ref, wpost_ref, wfpre_ref,
             h1_ref, y1_ref, z_ref, mix_ref):
        om, _, _ = _grms_fwd(o_ref[...], wmla_ref[...], MLA_V)
        hg = hg_ref[...]
        ohn, _, _ = _grms_fwd(oh_ref[...], whg_ref[...], HGRN_DIM)
        mix = jnp.concatenate([om, ohn * (hg * jax.nn.sigmoid(hg))], axis=-1)
        mix_ref[...] = mix.astype(mix_ref.dtype)
        y1 = _mm(mix, wout_ref[...])
        y1_ref[...] = y1
        h1 = x_ref[...] + _rms_fwd(y1, wpost_ref[...])[0]
        h1_ref[...] = h1
        z_ref[...] = _rms_fwd(h1, wfpre_ref[...])[0].astype(z_ref.dtype)

    row = lambda w: pl.BlockSpec((tt, w), lambda i: (i, 0))
    full = lambda a: pl.BlockSpec(a.shape, lambda i: (0,) * a.ndim)
    sds = jax.ShapeDtypeStruct
    return pl.pallas_call(
        body, name="proj_fwd", grid=(T // tt,),
        in_specs=[row(D_MODEL), row(MLA_WIDTH), row(HGRN_WIDTH), pl.BlockSpec((tt, HGRN_WIDTH), lambda i: (i, 3)),
                  full(wout), full(w_mla), full(w_hg), full(w_post), full(w_fpre)],
        out_specs=[row(D_MODEL)] * 4,
        out_shape=[sds((T, D_MODEL), F32), sds((T, D_MODEL), F32), sds((T, D_MODEL), MXU_DTYPE),
                   sds((T, D_MODEL), MXU_DTYPE)],
        compiler_params=_params(("arbitrary",)),
    )(x, o_raw, oh_raw, xph, wout, w_mla, w_hg, w_post, w_fpre)


def _ffn_fwd(zb, h1, tgt, w_fpost, wg, wu, wd, tt=256):
    T = zb.shape[0]
    nj = N_CHIPS

    def body(z_ref, h1_ref, tgt_ref, wfpost_ref, wg_ref, wu_ref, wd_ref, g_ref, up_ref, dy2_ref, dh2_ref, loss_ref, dwf_ref):
        @pl.when(pl.program_id(0) == 0)
        def _():
            loss_ref[...] = jnp.zeros_like(loss_ref)
            dwf_ref[...] = jnp.zeros_like(dwf_ref)

        z = z_ref[...]
        gs = [_mm_nt(z, wg_ref[j]) for j in range(nj)]
        ups = [_mm_nt(z, wu_ref[j]) for j in range(nj)]
        y2 = jnp.zeros((tt, D_MODEL), F32)
        for j in range(nj):
            g_ref[j] = gs[j]
            up_ref[j] = ups[j]
            y2 = y2 + _mm(gs[j] * jax.nn.sigmoid(gs[j]) * ups[j], wd_ref[j])
        w = wfpost_ref[...]
        y2s, y2n, r2 = _rms_fwd(y2, w)
        e = h1_ref[...] + y2s - tgt_ref[...]
        loss_ref[...] += jnp.sum(e * e, axis=0, keepdims=True)
        dh2 = e * (1.0 / D_MODEL)
        dh2_ref[...] = dh2
        dy2, dwf = _rms_bwd(dh2, y2n, r2, w)
        dy2_ref[...] = dy2.astype(dy2_ref.dtype)
        dwf_ref[...] += dwf

    row = pl.BlockSpec((tt, D_MODEL), lambda i: (i, 0))
    vec = pl.BlockSpec((1, D_MODEL), lambda i: (0, 0))
    resident = pl.BlockSpec((nj, FF_SHARD, D_MODEL), lambda i: (0, 0, 0), pipeline_mode=pl.Buffered(1))
    act = pl.BlockSpec((nj, tt, FF_SHARD), lambda i: (0, i, 0))
    sds = jax.ShapeDtypeStruct
    return pl.pallas_call(
        body, name="ffn_fwd", grid=(T // tt,),
        in_specs=[row, row, row, vec, resident, resident, resident],
        out_specs=[act, act, row, row, vec, vec],
        out_shape=[sds((nj, T, FF_SHARD), F32), sds((nj, T, FF_SHARD), F32), sds((T, D_MODEL), MXU_DTYPE),
                   sds((T, D_MODEL), F32), sds((1, D_MODEL), F32), sds((1, D_MODEL), F32)],
        compiler_params=_params(("arbitrary",)),
    )(zb, h1, tgt, w_fpost, wg, wu, wd)


def _ffn_bwd(zb, g, up, dy2b, wg, wu, wd, tt=512):
    T = zb.shape[0]
    nj = N_CHIPS

    def body(z_ref, g_ref, up_ref, dy2_ref, wg_ref, wu_ref, wd_ref, dwg_ref, dwu_ref, dwd_ref, dz_ref):
        @pl.when(pl.program_id(1) == 0)
        def _():
            dwg_ref[...] = jnp.zeros_like(dwg_ref)
            dwu_ref[...] = jnp.zeros_like(dwu_ref)
            dwd_ref[...] = jnp.zeros_like(dwd_ref)

        z, g_, up_, dy2 = z_ref[...], g_ref[0], up_ref[0], dy2_ref[...]
        sg = jax.nn.sigmoid(g_)
        act = g_ * sg
        dff = _mm_nt(dy2, wd_ref[0])
        dwd_ref[0] += _mm_tn(act * up_, dy2)
        dg = dff * up_ * sg * (1.0 + g_ * (1.0 - sg))
        dup = dff * act
        dwg_ref[0] += _mm_tn(dg, z)
        dwu_ref[0] += _mm_tn(dup, z)
        dz_ref[0] = _mm(dg, wg_ref[0]) + _mm(dup, wu_ref[0])

    row = pl.BlockSpec((tt, D_MODEL), lambda j, i: (i, 0))
    act = pl.BlockSpec((1, tt, FF_SHARD), lambda j, i: (j, i, 0))
    w_sh = pl.BlockSpec((1, FF_SHARD, D_MODEL), lambda j, i: (j, 0, 0))
    w_grad = jax.ShapeDtypeStruct((nj, FF_SHARD, D_MODEL), F32)
    return pl.pallas_call(
        body, name="ffn_bwd", grid=(nj, T // tt),
        in_specs=[row, act, act, row, w_sh, w_sh, w_sh],
        out_specs=[w_sh, w_sh, w_sh, pl.BlockSpec((1, tt, D_MODEL), lambda j, i: (j, i, 0))],
        out_shape=[w_grad, w_grad, w_grad, jax.ShapeDtypeStruct((nj, T, D_MODEL), F32)],
        compiler_params=_params(("arbitrary", "arbitrary")),
    )(zb, g, up, dy2b, wg, wu, wd)


def _mid_bwd(dzp, dh2, h1, y1, mixb, o_raw, oh_raw, xph, wout, w_fpre, w_post, w_mla, w_hg, swap=(), tt=256):
    T = dh2.shape[0]
    nsw = len(swap)
    n_in, n_out = 13, 10

    def body(*refs):
        (dzp_ref, dh2_ref, h1_ref, y1_ref, mix_ref, o_ref, oh_ref, hg_ref, wout_ref, wfpre_ref, wpost_ref,
         wmla_ref, whg_ref) = refs[:n_in]
        (dh1_ref, dwout_ref, do_ref, doh_ref, dhg_ref, dvec_ref, dwfpre_ref, dwpost_ref, dwmla_ref,
         dwhg_ref) = refs[n_in + nsw:n_in + nsw + n_out]
        swap_copies = lambda: _pair_swap_copies(refs[n_in:n_in + nsw], refs[n_in + nsw + n_out:n_in + 2 * nsw + n_out],
                                                *refs[n_in + 2 * nsw + n_out:])

        @pl.when(pl.program_id(0) == 0)
        def _():
            for r in (dwout_ref, dwfpre_ref, dwpost_ref, dwmla_ref, dwhg_ref):
                r[...] = jnp.zeros_like(r)
            for cp in (swap_copies() if nsw else ()):
                cp.start()

        dz = dzp_ref[0] + dzp_ref[1] + dzp_ref[2] + dzp_ref[3]
        wfpre = wfpre_ref[...]
        _, h1n, r = _rms_fwd(h1_ref[...], wfpre)
        dh1_z, dwfpre = _rms_bwd(dz, h1n, r, wfpre)
        dwfpre_ref[...] += dwfpre
        dh1 = dh2_ref[...] + dh1_z
        dh1_ref[...] = dh1
        wpost = wpost_ref[...]
        _, y1n, r1 = _rms_fwd(y1_ref[...], wpost)
        dy1, dwpost = _rms_bwd(dh1, y1n, r1, wpost)
        dwpost_ref[...] += dwpost
        dmix = _mm_nt(dy1, wout_ref[...])
        dwout_ref[...] += _mm_tn(mix_ref[...], dy1)
        wmla = wmla_ref[...]
        o = o_ref[...]
        _, on, ro = _grms_fwd(o, wmla, MLA_V)
        d_o, dwmla = _grms_bwd(dmix[:, :MLA_WIDTH], on, ro, wmla, MLA_V)
        dwmla_ref[...] += dwmla
        do_ref[...] = d_o.astype(do_ref.dtype)
        hh = lax.broadcasted_iota(jnp.int32, (MLA_HEADS, MLA_WIDTH), 0)
        ll = lax.broadcasted_iota(jnp.int32, (MLA_HEADS, MLA_WIDTH), 1)
        sel = jnp.where((ll >= hh * MLA_V) & (ll < (hh + 1) * MLA_V), 1.0, 0.0)
        dvec_ref[...] = _mm_nt(sel, d_o * o, True)
        whg = whg_ref[...]
        hg = hg_ref[...]
        sg = jax.nn.sigmoid(hg)
        _, ohn, rh = _grms_fwd(oh_ref[...], whg, HGRN_DIM)
        dmh = dmix[:, MLA_WIDTH:]
        dhg_ref[...] = dmh * ohn * whg * sg * (1.0 + hg * (1.0 - sg))
        d_oh, dwhg = _grms_bwd(dmh * (hg * sg), ohn, rh, whg, HGRN_DIM)
        dwhg_ref[...] += dwhg
        doh_ref[...] = d_oh

        if nsw:
            @pl.when(pl.program_id(0) == T // tt - 1)
            def _():
                for cp in swap_copies():
                    cp.wait()

    row = lambda w: pl.BlockSpec((tt, w), lambda i: (i, 0))
    full = lambda a: pl.BlockSpec(a.shape, lambda i: (0,) * a.ndim)
    vec = lambda w: pl.BlockSpec((1, w), lambda i: (0, 0))
    sds = jax.ShapeDtypeStruct
    return pl.pallas_call(
        body, name="mid_bwd", grid=(T // tt,),
        in_specs=[pl.BlockSpec((N_CHIPS, tt, D_MODEL), lambda i: (0, i, 0)), row(D_MODEL), row(D_MODEL), row(D_MODEL),
                  row(D_MODEL), row(MLA_WIDTH), row(HGRN_WIDTH), pl.BlockSpec((tt, HGRN_WIDTH), lambda i: (i, 3)),
                  full(wout), vec(D_MODEL), vec(D_MODEL), vec(MLA_WIDTH), vec(HGRN_WIDTH)] + [ANY] * nsw,
        out_specs=[row(D_MODEL), full(wout), row(MLA_WIDTH), row(HGRN_WIDTH), row(HGRN_WIDTH),
                   pl.BlockSpec((MLA_HEADS, tt), lambda i: (0, i)),
                   vec(D_MODEL), vec(D_MODEL), vec(MLA_WIDTH), vec(HGRN_WIDTH)] + [ANY] * nsw,
        out_shape=[sds((T, D_MODEL), F32), sds(wout.shape, F32), sds((T, MLA_WIDTH), MXU_DTYPE), sds((T, HGRN_WIDTH), F32),
                   sds((T, HGRN_WIDTH), F32), sds((MLA_HEADS, T), F32),
                   sds((1, D_MODEL), F32), sds((1, D_MODEL), F32), sds((1, MLA_WIDTH), F32), sds((1, HGRN_WIDTH), F32)]
        + _half_stack_shapes(swap),
        scratch_shapes=[pltpu.SemaphoreType.DMA((nsw,)), pltpu.SemaphoreType.DMA((nsw,))] if nsw else [],
        compiler_params=_params(("arbitrary",)),
    )(dzp, dh2, h1, y1, mixb, o_raw, oh_raw, xph, wout, w_fpre, w_post, w_mla, w_hg, *swap)


def _in_bwd(x, dh1, cq, ckv, dq, dk, dv, dhq, dhf, dhi, dhg, rc, rs, w_pre, win, qnw, wq, kvnw, wk, wv, tt=256):
    T = x.shape[0]

    def body(x_ref, dh1_ref, cq_ref, ckv_ref, dq_ref, dk_ref, dv_ref, dhq_ref, dhf_ref, dhi_ref, dhg_ref, rc_ref, rs_ref,
             wpre_ref, win_ref, qnw_ref, wq_ref, kvnw_ref, wk_ref, wv_ref,
             dx_ref, dwin_ref, dwq_ref, dwk_ref, dwv_ref, dwpre_ref, dqnw_ref, dkvnw_ref):
        @pl.when(pl.program_id(0) == 0)
        def _():
            for r in (dwin_ref, dwq_ref, dwk_ref, dwv_ref, dwpre_ref, dqnw_ref, dkvnw_ref):
                r[...] = jnp.zeros_like(r)

        def add_win_grad(r, first):
            for arr0, n, chip, row0 in _win_grad_segments():
                if first <= arr0 and arr0 + n <= first + r.shape[0]:
                    dwin_ref[chip, row0:row0 + n, :] += r[arr0 - first:arr0 - first + n]

        lo = Q_RANK + KV_RANK + HEAD_PAD
        dxp_h = jnp.concatenate([dhq_ref[...], dhf_ref[...], dhi_ref[...], dhg_ref[...]], axis=-1)
        du = _mm(dxp_h, win_ref[lo:, :])
        wpre = wpre_ref[...]
        u, xn, rx = _rms_fwd(x_ref[...], wpre)
        add_win_grad(_mm_tn(dxp_h, u), lo)
        c, sa, sb = _rope_tables(rc_ref[...], rs_ref[...])
        lane = lax.broadcasted_iota(jnp.int32, (tt, HEAD_PAD), 1)
        dk_all = dk_ref[...]
        dq_lin = []
        dkr = jnp.zeros((tt, HEAD_PAD), F32)
        for h in range(MLA_HEADS):
            sl = slice(HEAD_PAD * h, HEAD_PAD * (h + 1))
            dq_lin.append(_rope_bwd(dq_ref[sl, :].T * ATTN_SCALE, c, sa, sb))
            dkr = dkr + dk_all[:, sl]
        dq_lin = jnp.concatenate(dq_lin, axis=-1)
        dkr = jnp.where((lane >= MLA_NOPE) & (lane < MLA_QK), _rope_bwd(dkr, c, sa, sb), 0.0)
        qnw = qnw_ref[...]
        qn, cqn, rq = _rms_fwd(cq_ref[...], qnw)
        dwq_ref[...] += _mm_tn(qn, dq_lin)
        dcq, dqnw = _rms_bwd(_mm_nt(dq_lin, wq_ref[...]), cqn, rq, qnw)
        dqnw_ref[...] += dqnw
        kvnw = kvnw_ref[...]
        kvn, ckvn, rkv = _rms_fwd(ckv_ref[...], kvnw)
        dv_ = dv_ref[...]
        dwk_ref[...] += _mm_tn(kvn, dk_all)
        dwv_ref[...] += _mm_tn(kvn, dv_)
        dckv, dkvnw = _rms_bwd(_mm_nt(dk_all, wk_ref[...]) + _mm_nt(dv_, wv_ref[...]), ckvn, rkv, kvnw)
        dkvnw_ref[...] += dkvnw
        dxp_a = jnp.concatenate([dcq, dckv, dkr], axis=-1)
        add_win_grad(_mm_tn(dxp_a, u), 0)
        dx_u, dwpre = _rms_bwd(du + _mm(dxp_a, win_ref[:lo, :]), xn, rx, wpre)
        dwpre_ref[...] += dwpre
        dx_ref[...] = dh1_ref[...] + dx_u

    row = lambda w: pl.BlockSpec((tt, w), lambda i: (i, 0))
    full = lambda a: pl.BlockSpec(a.shape, lambda i: (0,) * a.ndim)
    sds = jax.ShapeDtypeStruct
    qk_w = MLA_HEADS * HEAD_PAD
    return pl.pallas_call(
        body, name="in_bwd", grid=(T // tt,),
        in_specs=[row(D_MODEL), row(D_MODEL), row(Q_RANK), row(KV_RANK), pl.BlockSpec((qk_w, tt), lambda i: (0, i)),
                  row(qk_w), row(MLA_WIDTH),
                  row(HGRN_WIDTH), row(HGRN_WIDTH), row(HGRN_WIDTH), row(HGRN_WIDTH), row(HEAD_PAD), row(HEAD_PAD),
                  full(w_pre), full(win), full(qnw), full(wq), full(kvnw), full(wk), full(wv)],
        out_specs=[row(D_MODEL), pl.BlockSpec(WIN_COMM_SHAPE, lambda i: (0, 0, 0)), full(wq), full(wk), full(wv),
                   full(w_pre), full(qnw), full(kvnw)],
        out_shape=[sds((T, D_MODEL), F32), sds(WIN_COMM_SHAPE, F32), sds(wq.shape, F32), sds(wk.shape, F32),
                   sds(wv.shape, F32), sds(w_pre.shape, F32), sds(qnw.shape, F32), sds(kvnw.shape, F32)],
        compiler_params=_params(("arbitrary",)),
    )(x, dh1, cq, ckv, dq, dk, dv, dhq, dhf, dhi, dhg, rc, rs, w_pre, win, qnw, wq, kvnw, wk, wv)


def _arrange_weights(win_t, wuq_full, wukv):
    dt = win_t.dtype
    z = lambda n: jnp.zeros((n, D_MODEL), dt)
    s2 = Q_RANK + KV_RANK
    win_arr = jnp.concatenate([win_t[:s2], z(MLA_NOPE), win_t[s2:s2 + MLA_ROPE], z(HEAD_PAD - MLA_QK),
                               win_t[s2 + MLA_ROPE:]], axis=0)
    wq_arr = jnp.pad(wuq_full, ((0, 0), (0, 0), (0, HEAD_PAD - MLA_QK))).reshape(Q_RANK, MLA_HEADS * HEAD_PAD)
    wk_arr = jnp.pad(wukv[:, :, :MLA_NOPE], ((0, 0), (0, 0), (0, HEAD_PAD - MLA_NOPE))).reshape(
        KV_RANK, MLA_HEADS * HEAD_PAD)
    wv_arr = wukv[:, :, MLA_NOPE:].reshape(KV_RANK, MLA_WIDTH)
    return win_arr, wq_arr, wk_arr, wv_arr


WIN_COMM_SHAPE = (N_CHIPS, FF_SHARD, D_MODEL)


def _win_grad_segments():
    s2 = Q_RANK + KV_RANK
    runs = [(0, s2, 0), (s2, s2 + MLA_ROPE, MLA_NOPE), (s2 + MLA_ROPE, D_IN, HEAD_PAD - MLA_ROPE)]
    per = D_IN // N_CHIPS
    segs = []
    for lo, hi, shift in runs:
        for k in range(N_CHIPS):
            a, b = max(lo, per * k), min(hi, per * (k + 1))
            if a < b:
                segs.append((a + shift, b - a, k, a - per * k))
    return segs


def _unarrange_grads(dwq_arr, dwk_arr, dwv_arr):
    dwuq = dwq_arr.reshape(Q_RANK, MLA_HEADS, HEAD_PAD)[:, :, :MLA_QK]
    dwukv = jnp.concatenate([dwk_arr.reshape(KV_RANK, MLA_HEADS, HEAD_PAD)[:, :, :MLA_NOPE],
                             dwv_arr.reshape(KV_RANK, MLA_HEADS, MLA_V)], axis=-1)
    return dwuq, dwukv


def _rope_inv_freq():
    inv = 1.0 / (ROPE_THETA ** (jnp.arange(0, MLA_ROPE, 2, dtype=F32) / MLA_ROPE))
    z = lambda n: jnp.zeros((n,), F32)
    return jnp.concatenate([z(MLA_NOPE), inv, inv, z(HEAD_PAD - MLA_QK)]).reshape(1, HEAD_PAD)


def _local_step(x, pos, tgt, small, win_arr, wq_arr, wk_arr, wv_arr, late, place=None):
    invf = _rope_inv_freq()
    cq, ckv, xph, qb, kb, vb, kt, vt, rc, rs = _in_fwd(x, pos, invf, small["attn_pre_norm"], win_arr, small["mla_q_norm"],
                                               wq_arr, small["mla_kv_norm"], wk_arr, wv_arr)
    if place is None:
        o_raw, lse = _attn_fwd_t(qb, kb, vt)
        wout, wg, wu, wd = late
    else:
        o_raw, lse, *stacks = _attn_fwd_t(qb, kb, vt, gather=late)
        wout, wg, wu, wd = [lax.dynamic_update_slice(s, l[None], (place[1], 0, 0)) for s, l in zip(stacks, late)]
        wout = wout.reshape(D_MODEL, D_MODEL)
    oh_raw, states = _hgrn_fwd(xph, small["hgrn_lb_logits"])
    h1, y1, zb, mixb = _proj_fwd(x, o_raw, oh_raw, xph, wout, small["mla_out_norm"], small["hgrn_out_norm"],
                                 small["attn_post_norm"], small["ffn_pre_norm"])
    g, up, dy2b, dh2, loss_acc, d_fpost = _ffn_fwd(zb, h1, tgt, small["ffn_post_norm"], wg, wu, wd)
    dwg, dwu, dwd, dzp = _ffn_bwd(zb, g, up, dy2b, wg, wu, wd)
    ffn_grads = [] if place is None else [dwg, dwu, dwd]
    dh1, dwout, d_o, d_oh, dhg, dvec, d_fpre, d_post, d_mla, d_hg, *ffn_rs = _mid_bwd(
        dzp, dh2, h1, y1, mixb, o_raw, oh_raw, xph, wout, small["ffn_pre_norm"], small["attn_post_norm"],
        small["mla_out_norm"], small["hgrn_out_norm"], swap=ffn_grads)
    ffn_ps = _pair_sum(place, ffn_grads, ffn_rs, name="pair_sum_ffn") if ffn_grads else []
    dq, dk, dv, *ffn_ris = _attn_bwd_t(qb, kb, kt, vb, d_o, lse, dvec.reshape(lse.shape), send=ffn_ps)
    ffn_sums = _chip_sum(place, ffn_grads, ffn_rs, ffn_ris, name="chip_sum_ffn") if ffn_grads else []
    dhq, dhf, dhi, d_lbl, *ffn_final = _hgrn_bwd(xph, small["hgrn_lb_logits"], states, d_oh, fill=ffn_sums)
    dx, dwin4, dwq_arr, dwk_arr, dwv_arr, d_pre, d_qn, d_kvn = _in_bwd(
        x, dh1, cq, ckv, dq, dk, dv, dhq, dhf, dhi, dhg, rc, rs, small["attn_pre_norm"], win_arr,
        small["mla_q_norm"], wq_arr, small["mla_kv_norm"], wk_arr, wv_arr)
    dwuq, dwukv = _unarrange_grads(dwq_arr, dwk_arr, dwv_arr)
    loss = 0.5 * jnp.sum(loss_acc) * (1.0 / D_MODEL)
    grads = dict(attn_pre_norm=d_pre, w_in=dwin4, mla_q_norm=d_qn, mla_w_uq=dwuq, mla_kv_norm=d_kvn, mla_w_ukv=dwukv,
                 mla_out_norm=d_mla, hgrn_lb_logits=d_lbl, hgrn_out_norm=d_hg, w_out=dwout, attn_post_norm=d_post,
                 ffn_pre_norm=d_fpre, w_gate=dwg, w_up=dwu, w_down=dwd, ffn_post_norm=d_fpost)
    if place is None:
        return loss, dx, grads
    return loss, dx, grads, ffn_final


def _place():
    x, y, c = lax.axis_index("x"), lax.axis_index("y"), lax.axis_index("c")
    others = [(1 - x, y), (x, 1 - y), (1 - x, 1 - y)]
    return x, y, c, 2 * x + y, (x, y, 1 - c), others


def _half(ref, c, rows):
    return ref.at[pl.ds(pl.multiple_of(c * rows, 8), rows)]


def _rcopy(src, dst, send, recv, k, to):
    return pltpu.make_async_remote_copy(src_ref=src, dst_ref=dst, send_sem=send.at[k], recv_sem=recv.at[k],
                                        device_id=to, device_id_type=MESH)


class _Gather:
    def __init__(self, ins, outs, send, recv):
        self.ins, self.outs, self.send, self.recv = ins, outs, send, recv
        self.n = len(ins)
        self.halves = [r.shape[0] // 2 for r in ins]
        _, _, self.c, self.me, self.sib, self.others = _place()

    def _each(self):
        for j, (px, py) in enumerate(self.others):
            for a in range(self.n):
                yield j * self.n + a, a, 2 * px + py, (px, py, self.c)

    def sends(self):
        return [_rcopy(_half(self.ins[a], self.c, self.halves[a]), _half(self.outs[a].at[self.me], self.c, self.halves[a]),
                       self.send, self.recv, k, to) for k, a, _, to in self._each()]

    def arrivals(self):
        parts = [(k, _half(self.outs[a].at[chip], self.c, self.halves[a]), to) for k, a, chip, to in self._each()]
        return [_rcopy(p, p, self.send, self.recv, k, to) for k, p, to in parts]

    def forwards(self):
        parts = [(k, _half(self.outs[a].at[chip], self.c, self.halves[a])) for k, a, chip, _ in self._each()]
        return [_rcopy(p, p, self.send, self.recv, 3 * self.n + k, self.sib) for k, p in parts]

    def forward_arrivals(self):
        parts = [(k, _half(self.outs[a].at[chip], 1 - self.c, self.halves[a])) for k, a, chip, _ in self._each()]
        return [_rcopy(p, p, self.send, self.recv, 3 * self.n + k, self.sib) for k, p in parts]

    @staticmethod
    def out_shapes(arrs):
        return [jax.ShapeDtypeStruct((N_CHIPS,) + a.shape, a.dtype) for a in arrs]

    @staticmethod
    def semaphores(arrs):
        return [pltpu.SemaphoreType.DMA((6 * len(arrs),)), pltpu.SemaphoreType.DMA((6 * len(arrs),))]


def _gather_chips(arrs, name):
    n = len(arrs)

    def body(*refs):
        gat = _Gather(refs[:n], refs[n:2 * n], *refs[2 * n:])
        sends, forwards = gat.sends(), gat.forwards()
        for cp in sends:
            cp.start()
        for arrival, fw in zip(gat.arrivals(), forwards):
            arrival.wait_recv()
            fw.start()
        for arrival in gat.forward_arrivals():
            arrival.wait_recv()
        for cp in sends + forwards:
            cp.wait_send()

    return pl.pallas_call(body, name=name, in_specs=[ANY] * n, out_specs=[ANY] * n, out_shape=_Gather.out_shapes(arrs),
                          scratch_shapes=_Gather.semaphores(arrs))(*arrs)


GRAD_BLOCKS = 2


def _pair_swap_copies(g_refs, r_refs, send, recv):
    _, _, c, _, sib, _ = _place()
    copies = []
    for a, (g, r) in enumerate(zip(g_refs, r_refs)):
        h = g.shape[1] // 2
        copies.append(_rcopy(g.at[:, pl.ds(pl.multiple_of((1 - c) * h, 8), h)], r, send, recv, a, sib))
    return copies


def _half_stack_shapes(gs, dtype=None):
    return [jax.ShapeDtypeStruct((N_CHIPS, g.shape[1] // 2, g.shape[2]), dtype or g.dtype) for g in gs]


def _pair_swap(gs, sm):
    n = len(gs)

    def body(*refs):
        g_refs, sm_ref = refs[:n], refs[n]
        r_refs, ssib_ref = refs[n + 1:2 * n + 1], refs[2 * n + 1]
        send, recv = refs[2 * n + 2:]
        copies = _pair_swap_copies(g_refs, r_refs, send, recv)
        copies.append(_rcopy(sm_ref, ssib_ref, send, recv, n, _place()[4]))
        for cp in copies:
            cp.start()
        for cp in copies:
            cp.wait()

    return pl.pallas_call(
        body, name="pair_swap", in_specs=[ANY] * (n + 1), out_specs=[ANY] * (n + 1),
        out_shape=_half_stack_shapes(gs) + [jax.ShapeDtypeStruct(sm.shape, sm.dtype)],
        scratch_shapes=[pltpu.SemaphoreType.DMA((n + 1,)), pltpu.SemaphoreType.DMA((n + 1,))],
    )(*gs, sm)


def _pair_sum(place, gs, rs, small=None, name="pair_sum"):
    n = len(gs)
    nb = GRAD_BLOCKS

    def body(place_ref, *refs):
        g_refs, r_refs, p_refs = refs[:n], refs[n:2 * n], refs[-n - 1:-1] if small else refs[-n:]
        for a in range(n):
            p_refs[a][0] = (g_refs[a][0] + r_refs[a][0]).astype(p_refs[a].dtype)
        if small:
            @pl.when((pl.program_id(0) == 0) & (pl.program_id(1) == 0))
            def _():
                refs[-1][...] = refs[2 * n][...] + refs[2 * n + 1][...]

    in_specs, out_specs = [], []
    for g in gs:
        blk = (1, g.shape[1] // 2 // nb, g.shape[2])
        in_specs.append(pl.BlockSpec(blk, lambda i, k, p: (k, p[0] * nb + i, 0)))
    for g in gs:
        blk = (1, g.shape[1] // 2 // nb, g.shape[2])
        in_specs.append(pl.BlockSpec(blk, lambda i, k, p: (k, i, 0)))
        out_specs.append(pl.BlockSpec(blk, lambda i, k, p: (k, i, 0)))
    out_shape = _half_stack_shapes(gs, BF16)
    if small:
        sm_spec = pl.BlockSpec(small[0].shape, lambda i, k, p: (0, 0))
        in_specs += [sm_spec, sm_spec]
        out_specs.append(sm_spec)
        out_shape.append(jax.ShapeDtypeStruct(small[0].shape, F32))
    return pl.pallas_call(
        body, name=name,
        grid_spec=pltpu.PrefetchScalarGridSpec(num_scalar_prefetch=1, grid=(nb, N_CHIPS), in_specs=in_specs,
                                               out_specs=out_specs),
        out_shape=out_shape,
        compiler_params=_params(("arbitrary", "arbitrary")),
    )(place, *gs, *rs, *(small or ()))


def _chip_swap_copies(p_refs, ri_refs, send, recv):
    _, _, c, _, _, others = _place()
    n = len(p_refs)
    return [_rcopy(p_refs[a].at[2 * px + py], ri_refs[a].at[j], send, recv, j * n + a, (px, py, c))
            for j, (px, py) in enumerate(others) for a in range(n)]


def _chip_swap_shapes(ps):
    return [jax.ShapeDtypeStruct((3,) + p.shape[1:], p.dtype) for p in ps]


def _chip_swap(ps, pair):
    n = len(ps)

    def body(*refs):
        start, finish = _chip_swap_plan(refs[:n], refs[n], refs[n + 1:2 * n + 1], refs[2 * n + 1], *refs[2 * n + 2:])
        start()
        finish()

    return pl.pallas_call(
        body, name="chip_swap", in_specs=[ANY] * (n + 1), out_specs=[ANY] * (n + 1),
        out_shape=_chip_swap_out_shapes(ps, pair), scratch_shapes=_chip_swap_semaphores(n),
    )(*ps, pair)


def _chip_swap_plan(p_refs, pair_ref, ri_refs, sm4_ref, send, recv, lsem):
    n = len(p_refs)
    hs = SMALL_ROWS // 2
    x, y, c, me, sib, others = _place()
    local = pltpu.make_async_copy(pair_ref, sm4_ref.at[me], lsem.at[0])
    copies = _chip_swap_copies(p_refs, ri_refs, send, recv)
    arrivals = list(copies)
    for j, (px, py) in enumerate(others):
        copies.append(_rcopy(_half(pair_ref, c, hs), _half(sm4_ref.at[me], c, hs), send, recv, 3 * n + j, (px, py, c)))
        part = _half(sm4_ref.at[2 * px + py], c, hs)
        arrivals.append(_rcopy(part, part, send, recv, 3 * n + j, (px, py, c)))

    def start():
        local.start()
        for cp in copies:
            cp.start()

    def finish():
        for arrival in arrivals:
            arrival.wait_recv()
        for cp in copies:
            cp.wait_send()
        local.wait()

    return start, finish


def _chip_swap_out_shapes(ps, pair):
    return _chip_swap_shapes(ps) + [jax.ShapeDtypeStruct((N_CHIPS,) + pair.shape, pair.dtype)]


def _chip_swap_semaphores(n):
    k = 3 * (n + 1)
    return [pltpu.SemaphoreType.DMA((k,)), pltpu.SemaphoreType.DMA((k,)), pltpu.SemaphoreType.DMA((1,))]


def _chip_sum(place, gs, rs, ris, name="chip_sum"):
    n = len(gs)
    nb = GRAD_BLOCKS

    def body(place_ref, *refs):
        g_refs, r_refs, ri_refs, o_refs = refs[:n], refs[n:2 * n], refs[2 * n:3 * n], refs[3 * n:]
        for a in range(n):
            ri = ri_refs[a]
            o_refs[a][...] = (g_refs[a][0] + r_refs[a][0]) + ri[0].astype(F32) + ri[1].astype(F32) + ri[2].astype(F32)

    in_specs, out_specs, out_shape = [], [], []
    for g in gs:
        blk = (1, g.shape[1] // 2 // nb, g.shape[2])
        in_specs.append(pl.BlockSpec(blk, lambda i, p: (p[1], p[0] * nb + i, 0)))
    for g in gs:
        blk = (1, g.shape[1] // 2 // nb, g.shape[2])
        in_specs.append(pl.BlockSpec(blk, lambda i, p: (p[1], i, 0)))
    for g in gs:
        rb = g.shape[1] // 2 // nb
        in_specs.append(pl.BlockSpec((3, rb, g.shape[2]), lambda i, p: (0, i, 0)))
        out_specs.append(pl.BlockSpec((rb, g.shape[2]), lambda i, p: (p[0] * nb + i, 0)))
        out_shape.append(jax.ShapeDtypeStruct(g.shape[1:], F32))
    return pl.pallas_call(
        body, name=name,
        grid_spec=pltpu.PrefetchScalarGridSpec(num_scalar_prefetch=1, grid=(nb,), in_specs=in_specs, out_specs=out_specs),
        out_shape=out_shape,
        compiler_params=_params(("arbitrary",)),
    )(place, *gs, *rs, *ris)


def _pair_fill_copies(g_refs, send, recv):
    _, _, c, _, sib, _ = _place()
    copies, waits = [], []
    for a, g in enumerate(g_refs):
        h = g.shape[0] // 2
        mine, theirs = _half(g, c, h), _half(g, 1 - c, h)
        copies.append(_rcopy(mine, mine, send, recv, a, sib))
        waits.append(_rcopy(theirs, theirs, send, recv, a, sib))
    return copies, waits


def _pair_fill(gfs, sm4):
    n = len(gfs)
    hs = SMALL_ROWS // 2

    def body(*refs):
        g_refs, sm4_ref = refs[n + 1:2 * n + 1], refs[2 * n + 1]
        send, recv = refs[2 * n + 2:]
        x, y, c, me, sib, others = _place()
        copies, waits = _pair_fill_copies(g_refs, send, recv)
        for j, (px, py) in enumerate(others):
            chip = 2 * px + py
            mine, theirs = _half(sm4_ref.at[chip], c, hs), _half(sm4_ref.at[chip], 1 - c, hs)
            copies.append(pltpu.make_async_remote_copy(src_ref=mine, dst_ref=mine, send_sem=send.at[n + j],
                                                       recv_sem=recv.at[n + j], device_id=sib, device_id_type=MESH))
            waits.append(pltpu.make_async_remote_copy(src_ref=theirs, dst_ref=theirs, send_sem=send.at[n + j],
                                                      recv_sem=recv.at[n + j], device_id=sib, device_id_type=MESH))
        for cp in copies:
            cp.start()
        for w in waits:
            w.wait_recv()
        for cp in copies:
            cp.wait_send()

    return pl.pallas_call(
        body, name="pair_fill", in_specs=[ANY] * (n + 1), out_specs=[ANY] * (n + 1),
        out_shape=[jax.ShapeDtypeStruct(g.shape, g.dtype) for g in gfs] + [jax.ShapeDtypeStruct(sm4.shape, sm4.dtype)],
        input_output_aliases={i: i for i in range(n + 1)},
        scratch_shapes=[pltpu.SemaphoreType.DMA((n + 3,)), pltpu.SemaphoreType.DMA((n + 3,))],
    )(*gfs, sm4)


def _adamw_math(w, g, m, v):
    m = ADAM_B1 * m + (1.0 - ADAM_B1) * g
    v = ADAM_B2 * v + (1.0 - ADAM_B2) * (g * g)
    m_hat = m / (1.0 - ADAM_B1 ** ADAM_STEP)
    v_hat = v / (1.0 - ADAM_B2 ** ADAM_STEP)
    return -ADAM_LR * (m_hat / (jnp.sqrt(v_hat) + ADAM_EPS) + ADAM_WD * w), m, v


def _adamw(items, steps, name):
    n = len(items)

    def body(*refs):
        for a in range(n):
            g = refs[4 * a + 1][...]
            d, mo, vo = _adamw_math(refs[4 * a][...], g, refs[4 * a + 2][...], refs[4 * a + 3][...])
            for out, val in zip(refs[4 * n + 4 * a:4 * n + 4 * a + 4], (g, d, mo, vo)):
                out[...] = val

    spec = lambda w: pl.BlockSpec((w.shape[0] // steps, w.shape[1]), lambda i: (i, 0))
    flat = pl.pallas_call(
        body, name=name, grid=(steps,), in_specs=[spec(it[0]) for it in items for _ in range(4)],
        out_specs=[spec(it[0]) for it in items for _ in range(4)],
        out_shape=[jax.ShapeDtypeStruct(it[0].shape, F32) for it in items for _ in range(4)],
        compiler_params=_params(("arbitrary",)),
    )(*[a for it in items for a in it])
    return [flat[4 * a:4 * a + 4] for a in range(n)]


def _adamw_small(sm4, wmv):
    views = SMALL_VIEWS[:-1]
    n = len(views)

    def body(sm4_ref, *refs):
        g_all = ((sm4_ref[0] + sm4_ref[1]) + sm4_ref[2]) + sm4_ref[3]
        row = 0
        for a, (_, rows, cols) in enumerate(views):
            g = g_all[row:row + rows, :cols]
            row += -(-rows // ROW_TILE) * ROW_TILE
            d, mo, vo = _adamw_math(refs[3 * a][...], g, refs[3 * a + 1][...], refs[3 * a + 2][...])
            for out, val in zip(refs[3 * n + 4 * a:3 * n + 4 * a + 4], (g, d, mo, vo)):
                out[...] = val
        refs[-1][...] = g_all[row:row + 1, :128]

    flat = pl.pallas_call(
        body, name="adamw_small",
        out_shape=[jax.ShapeDtypeStruct((rows, cols), F32) for _, rows, cols in views for _ in range(4)]
        + [jax.ShapeDtypeStruct((1, 128), F32)],
        compiler_params=pltpu.CompilerParams(vmem_limit_bytes=VMEM_LIMIT),
    )(sm4, *[a for t in wmv for a in t])
    return [flat[4 * a:4 * a + 4] for a in range(n)] + [flat[-1]]


SMALL_NAMES = ("attn_pre_norm", "mla_q_norm", "mla_kv_norm", "mla_w_ukv", "mla_out_norm", "hgrn_lb_logits",
               "hgrn_out_norm", "attn_post_norm", "ffn_pre_norm", "ffn_post_norm")
BIG_NAMES = ("w_in", "mla_w_uq", "w_out", "w_gate", "w_up", "w_down")
WEIGHT_NAMES = ("attn_pre_norm", "w_in", "mla_q_norm", "mla_w_uq", "mla_kv_norm", "mla_w_ukv", "mla_out_norm",
                "hgrn_lb_logits", "hgrn_out_norm", "w_out", "attn_post_norm", "ffn_pre_norm", "w_gate", "w_up", "w_down",
                "ffn_post_norm")


UQ_COMM_SHAPE = (192, 384)


def _pack_small(vals):
    parts = []
    for name, rows, cols in SMALL_VIEWS:
        pad_rows = -(-rows // ROW_TILE) * ROW_TILE - rows
        parts.append(jnp.pad(vals[name].reshape(rows, cols), ((0, pad_rows), (0, D_MODEL - cols))))
    return jnp.concatenate(parts, axis=0)


def kernel(x, positions, attn_pre_norm, w_in, mla_q_norm, mla_w_uq, mla_kv_norm, mla_w_ukv, mla_out_norm, hgrn_lb_logits, hgrn_out_norm, w_out, attn_post_norm, ffn_pre_norm, w_gate, w_up, w_down, ffn_post_norm, loss_target, m_attn_pre_norm, m_w_in, m_mla_q_norm, m_mla_w_uq, m_mla_kv_norm, m_mla_w_ukv, m_mla_out_norm, m_hgrn_lb_logits, m_hgrn_out_norm, m_w_out, m_attn_post_norm, m_ffn_pre_norm, m_w_gate, m_w_up, m_w_down, m_ffn_post_norm, v_attn_pre_norm, v_w_in, v_mla_q_norm, v_mla_w_uq, v_mla_kv_norm, v_mla_w_ukv, v_mla_out_norm, v_hgrn_lb_logits, v_hgrn_out_norm, v_w_out, v_attn_post_norm, v_ffn_pre_norm, v_w_gate, v_w_up, v_w_down, v_ffn_post_norm):
    args = locals()
    W = {n: args[n] for n in WEIGHT_NAMES}
    M = {n: args["m_" + n] for n in WEIGHT_NAMES}
    V = {n: args["v_" + n] for n in WEIGHT_NAMES}
    T = x.shape[1]
    cx, cy, cc = lax.axis_index("x"), lax.axis_index("y"), lax.axis_index("c")

    win_rows = D_IN // N_CHIPS
    shard2d = {"w_in": (win_rows, D_MODEL), "mla_w_uq": (Q_RANK // N_CHIPS, MLA_HEADS * MLA_QK),
               "w_out": (D_MODEL // N_CHIPS, D_MODEL), "w_gate": (FF_SHARD, D_MODEL), "w_up": (FF_SHARD, D_MODEL),
               "w_down": (FF_SHARD, D_MODEL)}
    transposed = ("w_in", "w_gate", "w_up")
    to2d = lambda n, a: a[0].T if n in transposed else a.reshape(shard2d[n])
    from2d = lambda n, t: t.T[None] if n in transposed else t.reshape(W[n].shape)
    me = 2 * cx + cy
    place = jnp.stack([cc, me]).astype(jnp.int32)
    local_b = [to2d(n, W[n]).astype(BF16) for n in BIG_NAMES]
    local_b[0] = jnp.pad(local_b[0], ((0, FF_SHARD - win_rows), (0, 0)))
    stacks = _gather_chips(local_b[:2], "gather_weights")
    win4, wuq4 = [lax.dynamic_update_slice(s, l[None], (me, 0, 0)) for s, l in zip(stacks, local_b)]
    win_t = win4[:, :win_rows].reshape(D_IN, D_MODEL)
    wuq_full = wuq4.reshape(Q_RANK, MLA_HEADS, MLA_QK)
    win_arr, wq_arr, wk_arr, wv_arr = _arrange_weights(win_t, wuq_full, mla_w_ukv[0].astype(BF16))
    small = {n: W[n][0] if n == "mla_w_ukv" else W[n].reshape(-1, W[n].shape[-1]) for n in SMALL_NAMES}

    loss_local, dx, grads, ffn_final = _local_step(x[0], positions.reshape(T, 1), loss_target[0], small, win_arr,
                                                           wq_arr, wk_arr, wv_arr, local_b[2:], place)

    gs = [grads["w_in"], grads["mla_w_uq"].reshape((N_CHIPS,) + UQ_COMM_SHAPE), grads["w_out"].reshape((N_CHIPS,) + shard2d["w_out"])]
    sm = _pack_small({**grads, "loss": loss_local})
    *rs, ssib = _pair_swap(gs, sm)
    *ps, pair = _pair_sum(place, gs, rs, small=(sm, ssib))
    ffn_names, rest_names = BIG_NAMES[3:], BIG_NAMES[:3]
    g2d = dict(zip(ffn_names, ffn_final))
    adam_in = lambda names_: [(to2d(n, W[n]), g2d[n], to2d(n, M[n]), to2d(n, V[n])) for n in names_]
    updates = dict(zip(ffn_names, _adamw(adam_in(ffn_names), 8, "adamw_ffn")))
    *ris, sm4 = _chip_swap(ps, pair)
    *gfin, smf = _pair_fill(_chip_sum(place, gs, rs, ris), sm4)

    g2d.update({n: gfin[k].reshape((-1,) + shard2d[n][1:]) for k, n in enumerate(rest_names)})
    updates.update(zip(rest_names[:2], _adamw(adam_in(rest_names[:2]), 3, "adamw_w_in")))
    updates.update(zip(rest_names[2:], _adamw(adam_in(rest_names[2:]), 8, "adamw_w_out")))
    G, DW, NM, NV = {}, {}, {}, {}
    for n, outs in updates.items():
        G[n], DW[n], NM[n], NV[n] = (from2d(n, t) for t in outs)
    view2d = lambda n, a: a.reshape(next((r, c) for name, r, c in SMALL_VIEWS if name == n))
    *res, loss_row = _adamw_small(smf, [tuple(view2d(n, t[n]) for t in (W, M, V)) for n in SMALL_NAMES])
    for n, outs in zip(SMALL_NAMES, res):
        G[n], DW[n], NM[n], NV[n] = (t.reshape(W[n].shape) for t in outs)
    loss = loss_row[0, 0]
    return (loss, dx[None], *[G[n] for n in WEIGHT_NAMES], *[DW[n] for n in WEIGHT_NAMES],
            *[NM[n] for n in WEIGHT_NAMES], *[NV[n] for n in WEIGHT_NAMES])
```

```python
import jax
import jax.numpy as jnp
from jax import lax
from jax.experimental import pallas as pl
from jax.experimental.pallas import tpu as pltpu

F32 = jnp.float32
BF16 = jnp.bfloat16
MXU_DTYPE = BF16

D_MODEL = 1024
MLA_HEADS = 8
MLA_NOPE = 64
MLA_ROPE = 32
MLA_V = 64
MLA_QK = MLA_NOPE + MLA_ROPE
Q_RANK = 384
KV_RANK = 128
MLA_WIDTH = MLA_HEADS * MLA_V
HEAD_PAD = 128
HGRN_HEADS = 4
HGRN_DIM = 128
HGRN_WIDTH = HGRN_HEADS * HGRN_DIM
CHUNK = 64
SUB = 16
HGRN_CPI = 4
D_IN = Q_RANK + KV_RANK + MLA_ROPE + 4 * HGRN_WIDTH
D_IN_ARR = Q_RANK + KV_RANK + HEAD_PAD + 4 * HGRN_WIDTH
D_FF = 2816
N_CHIPS = 4
FF_SHARD = D_FF // N_CHIPS
EPS = 1e-6
ROPE_THETA = 10000.0
ATTN_SCALE = MLA_QK ** -0.5
ATTN_SCALE_LOG2 = ATTN_SCALE * 1.4426950408889634
NEG_BIG = -1e30

ADAM_LR = 0.001
ADAM_B1 = 0.9
ADAM_B2 = 0.999
ADAM_EPS = 1e-08
ADAM_WD = 0.01
ADAM_STEP = 10

VMEM_LIMIT = 56 * 1024 * 1024

SMALL_VIEWS = (("attn_pre_norm", 1, 1024), ("mla_q_norm", 1, 384), ("mla_kv_norm", 1, 128), ("mla_w_ukv", 128, 1024),
               ("mla_out_norm", 1, 512), ("hgrn_lb_logits", 2, 512), ("hgrn_out_norm", 1, 512),
               ("attn_post_norm", 1, 1024), ("ffn_pre_norm", 1, 1024), ("ffn_post_norm", 1, 1024), ("loss", 1, 1))
ROW_TILE = 8
SMALL_ROWS = sum(-(-rows // ROW_TILE) * ROW_TILE for _, rows, _ in SMALL_VIEWS)

MESH = pl.DeviceIdType.MESH
ANY = pl.BlockSpec(memory_space=pl.ANY)


def _dot(a, b, dims, exact):
    if exact:
        return lax.dot_general(a.astype(F32), b.astype(F32), (dims, ((), ())), precision=lax.Precision.HIGH,
                               preferred_element_type=F32)
    return lax.dot_general(a.astype(MXU_DTYPE), b.astype(MXU_DTYPE), (dims, ((), ())), preferred_element_type=F32)


def _mm(a, b, exact=False):
    return _dot(a, b, ((1,), (0,)), exact)


def _mm_nt(a, b, exact=False):
    return _dot(a, b, ((1,), (1,)), exact)


def _mm_tn(a, b, exact=False):
    return _dot(a, b, ((0,), (0,)), exact)


def _rms_fwd(x, w):
    r = lax.rsqrt(jnp.mean(x * x, axis=-1, keepdims=True) + EPS)
    xn = x * r
    return xn * w, xn, r


def _rms_bwd(dy, xn, r, w):
    dxn = dy * w
    dx = r * (dxn - xn * jnp.mean(dxn * xn, axis=-1, keepdims=True))
    dw = jnp.sum(dy * xn, axis=0, keepdims=True)
    return dx, dw


def _group_sums(v, gs):
    t, n = v.shape
    lane = lax.broadcasted_iota(jnp.int32, (t, 128), 1)
    out = []
    for p in range(n // 128):
        vb = v[:, 128 * p:128 * (p + 1)]
        if gs == 128:
            out.append(jnp.sum(vb, axis=-1, keepdims=True))
        else:
            out.append(jnp.sum(jnp.where(lane < 64, vb, 0.0), axis=-1, keepdims=True))
            out.append(jnp.sum(jnp.where(lane >= 64, vb, 0.0), axis=-1, keepdims=True))
    return out


def _group_bcast(sums, gs, t):
    lane = lax.broadcasted_iota(jnp.int32, (t, 128), 1)
    if gs == 128:
        return jnp.concatenate([jnp.broadcast_to(s, (t, 128)) for s in sums], axis=-1)
    return jnp.concatenate([jnp.where(lane < 64, sums[2 * p], sums[2 * p + 1]) for p in range(len(sums) // 2)],
                           axis=-1)


def _grms_fwd(x, w, gs):
    t = x.shape[0]
    r = lax.rsqrt(_group_bcast(_group_sums(x * x, gs), gs, t) * (1.0 / gs) + EPS)
    xn = x * r
    return xn * w, xn, r


def _grms_bwd(dy, xn, r, w, gs):
    t = dy.shape[0]
    dxn = dy * w
    dx = r * (dxn - xn * (_group_bcast(_group_sums(dxn * xn, gs), gs, t) * (1.0 / gs)))
    dw = jnp.sum(dy * xn, axis=0, keepdims=True)
    return dx, dw


def _rope_tables(c_tab, s_tab):
    lane = lax.broadcasted_iota(jnp.int32, c_tab.shape, 1)
    first = (lane >= MLA_NOPE) & (lane < MLA_NOPE + MLA_ROPE // 2)
    second = (lane >= MLA_NOPE + MLA_ROPE // 2) & (lane < MLA_QK)
    return c_tab, jnp.where(first, -s_tab, 0.0), jnp.where(second, s_tab, 0.0)


def _rope(v, c, sa, sb):
    return v * c + pltpu.roll(v, HEAD_PAD - MLA_ROPE // 2, 1) * sa + pltpu.roll(v, MLA_ROPE // 2, 1) * sb


def _rope_bwd(d, c, sa, sb):
    return d * c - pltpu.roll(d, HEAD_PAD - MLA_ROPE // 2, 1) * sa - pltpu.roll(d, MLA_ROPE // 2, 1) * sb


def _params(sem, vmem=VMEM_LIMIT):
    return pltpu.CompilerParams(dimension_semantics=sem, vmem_limit_bytes=vmem)


def _in_fwd(x, pos, invf, w_pre, win, qnw, wq, kvnw, wk, wv, tt=512):
    T = x.shape[0]

    def body(x_ref, pos_ref, invf_ref, wpre_ref, win_ref, qnw_ref, wq_ref, kvnw_ref, wk_ref, wv_ref,
             cq_ref, ckv_ref, xph_ref, q_ref, k_ref, v_ref, kt_ref, vt_ref, rc_ref, rs_ref):
        u, _, _ = _rms_fwd(x_ref[...], wpre_ref[...])
        lo = Q_RANK + KV_RANK + HEAD_PAD
        xp = _mm_nt(u, win_ref[:lo, :])
        xph_ref[...] = _mm_nt(u, win_ref[lo:, :])
        cq = xp[:, :Q_RANK]
        ckv = xp[:, Q_RANK:Q_RANK + KV_RANK]
        kr = xp[:, Q_RANK + KV_RANK:]
        cq_ref[...] = cq
        ckv_ref[...] = ckv
        ang = pos_ref[...].astype(F32) * invf_ref[...]
        c_tab = jnp.cos(ang)
        s_tab = jnp.sin(ang)
        rc_ref[...] = c_tab
        rs_ref[...] = s_tab
        c, sa, sb = _rope_tables(c_tab, s_tab)
        qn, _, _ = _rms_fwd(cq, qnw_ref[...])
        q = _mm(qn, wq_ref[...])
        kvn, _, _ = _rms_fwd(ckv, kvnw_ref[...])
        kn = _mm(kvn, wk_ref[...])
        v = _mm(kvn, wv_ref[...])
        v_ref[...] = v.astype(v_ref.dtype)
        vt_ref[...] = v.T.astype(vt_ref.dtype)
        krr = _rope(kr, c, sa, sb)
        for h in range(MLA_HEADS):
            sl = slice(HEAD_PAD * h, HEAD_PAD * (h + 1))
            q_ref[:, sl] = (_rope(q[:, sl], c, sa, sb) * ATTN_SCALE_LOG2).astype(q_ref.dtype)
            kh = kn[:, sl] + krr
            k_ref[:, sl] = kh.astype(k_ref.dtype)
            kt_ref[sl, :] = kh.T.astype(kt_ref.dtype)

    row = lambda w: pl.BlockSpec((tt, w), lambda i: (i, 0))
    full = lambda a: pl.BlockSpec(a.shape, lambda i: (0,) * a.ndim)
    qk_w = MLA_HEADS * HEAD_PAD
    return pl.pallas_call(
        body, name="in_fwd", grid=(T // tt,),
        in_specs=[row(D_MODEL), row(1), full(invf), full(w_pre), full(win), full(qnw), full(wq), full(kvnw),
                  full(wk), full(wv)],
        out_specs=[row(Q_RANK), row(KV_RANK), row(4 * HGRN_WIDTH), row(qk_w), row(qk_w), row(MLA_WIDTH),
                   pl.BlockSpec((qk_w, tt), lambda i: (0, i)), pl.BlockSpec((MLA_WIDTH, tt), lambda i: (0, i)),
                   row(HEAD_PAD), row(HEAD_PAD)],
        out_shape=[jax.ShapeDtypeStruct((T, Q_RANK), F32), jax.ShapeDtypeStruct((T, KV_RANK), F32),
                   jax.ShapeDtypeStruct((T, 4 * HGRN_WIDTH), F32), jax.ShapeDtypeStruct((T, qk_w), MXU_DTYPE),
                   jax.ShapeDtypeStruct((T, qk_w), MXU_DTYPE), jax.ShapeDtypeStruct((T, MLA_WIDTH), MXU_DTYPE),
                   jax.ShapeDtypeStruct((qk_w, T), MXU_DTYPE), jax.ShapeDtypeStruct((MLA_WIDTH, T), MXU_DTYPE),
                   jax.ShapeDtypeStruct((T, HEAD_PAD), F32), jax.ShapeDtypeStruct((T, HEAD_PAD), F32)],
        compiler_params=_params(("arbitrary",)),
    )(x, pos, invf, w_pre, win, qnw, wq, kvnw, wk, wv)


def _attn_fwd_t(qb, kb, vt, gather=(), tq=512, hps=8):
    T = qb.shape[0]
    nq = T // tq
    ng = len(gather)
    steps = (MLA_HEADS // hps) * nq
    pass_on = steps - 3

    def body(q_ref, k_ref, vt_ref, *rest):
        o_ref, lse_ref = rest[ng:ng + 2]
        acc_scr = rest[2 * ng + 2]
        qi = pl.program_id(1)
        step_no = pl.program_id(0) * nq + qi
        if ng:
            gat = _Gather(rest[:ng], rest[ng + 2:2 * ng + 2], *rest[2 * ng + 3:])

            @pl.when(step_no == 0)
            def _():
                for cp in gat.sends():
                    cp.start()

            @pl.when(step_no == pass_on)
            def _():
                for arrival in gat.arrivals():
                    arrival.wait_recv()
                for cp in gat.forwards():
                    cp.start()

        heads = [slice(HEAD_PAD * a, HEAD_PAD * (a + 1)) for a in range(hps)]
        acc_scr[...] = jnp.zeros_like(acc_scr)

        def step(j, carry, masked):
            start = pl.multiple_of(j * tq, tq)
            scores = [_mm_nt(k_ref[pl.ds(start, tq), heads[a]], q_ref[:, heads[a]]) for a in range(hps)]
            new = []
            for a in range(hps):
                m, l = carry[a]
                s = scores[a]
                if masked:
                    kk = lax.broadcasted_iota(jnp.int32, (tq, tq), 0)
                    qq = lax.broadcasted_iota(jnp.int32, (tq, tq), 1)
                    s = jnp.where(kk <= qq, s, NEG_BIG)
                m_new = jnp.maximum(m, jnp.max(s, axis=0, keepdims=True))
                alpha = jnp.exp2(m - m_new)
                p = jnp.exp2(s - m_new)
                l = l * alpha + jnp.sum(p, axis=0, keepdims=True)
                vtj = vt_ref[2 * MLA_V * (a // 2):2 * MLA_V * (a // 2 + 1), pl.ds(start, tq)]
                acc_scr[a] = acc_scr[a] * alpha + _mm(vtj, p)
                new.append((m_new, l))
            return tuple(new)

        init = tuple((jnp.full((1, tq), NEG_BIG, F32), jnp.zeros((1, tq), F32)) for _ in range(hps))
        carry = lax.fori_loop(0, qi, lambda j, c: step(j, c, False), init)
        carry = step(qi, carry, True)
        row = lax.broadcasted_iota(jnp.int32, (2 * MLA_V, tq), 0)
        for pr in range(hps // 2):
            (m0, l0), (m1, l1) = carry[2 * pr], carry[2 * pr + 1]
            ot = jnp.where(row < MLA_V, acc_scr[2 * pr] / l0, acc_scr[2 * pr + 1] / l1)
            o_ref[:, 2 * MLA_V * pr:2 * MLA_V * (pr + 1)] = ot.T
            lse_ref[pr, 0:1, :] = m0 + jnp.log2(l0)
            lse_ref[pr, 1:2, :] = m1 + jnp.log2(l1)

        if ng:
            @pl.when(step_no == steps - 1)
            def _():
                for arrival in gat.forward_arrivals():
                    arrival.wait_recv()
                for cp in gat.sends() + gat.forwards():
                    cp.wait_send()

    return pl.pallas_call(
        body, name="attn_fwd", grid=(MLA_HEADS // hps, nq),
        in_specs=[pl.BlockSpec((tq, hps * HEAD_PAD), lambda g, i: (i, g)),
                  pl.BlockSpec((T, hps * HEAD_PAD), lambda g, i: (0, g)),
                  pl.BlockSpec((hps * MLA_V, T), lambda g, i: (g, 0))] + [ANY] * ng,
        out_specs=[pl.BlockSpec((tq, hps * MLA_V), lambda g, i: (i, g)),
                   pl.BlockSpec((hps // 2, 2, tq), lambda g, i: (g, 0, i))] + [ANY] * ng,
        out_shape=[jax.ShapeDtypeStruct((T, MLA_WIDTH), F32), jax.ShapeDtypeStruct((MLA_HEADS // 2, 2, T), F32)]
        + _Gather.out_shapes(gather),
        scratch_shapes=[pltpu.VMEM((hps, 2 * MLA_V, tq), F32)] + (_Gather.semaphores(gather) if ng else []),
        compiler_params=_params(("arbitrary", "arbitrary")),
    )(qb, kb, vt, *gather)


def _attn_bwd_t(qb, kb, kt, vb, dob, lse, dvec, send=(), tq=512, hps=4):
    T = qb.shape[0]
    nq = T // tq
    ns = len(send)
    steps = (MLA_HEADS // hps) * nq

    def body(q_ref, k_ref, kt_ref, v_ref, do_ref, lse_ref, d_ref, *rest):
        dqt_ref, dk_ref, dv_ref = rest[ns:ns + 3]
        va_scr, dv_scr = rest[2 * ns + 3:2 * ns + 5]
        j = pl.program_id(1)
        step_no = pl.program_id(0) * nq + j
        if ns:
            @pl.when(step_no == 0)
            def _():
                for cp in _chip_swap_copies(rest[:ns], rest[ns + 3:2 * ns + 3], *rest[2 * ns + 5:]):
                    cp.start()

        @pl.when(j == 0)
        def _():
            dqt_ref[...] = jnp.zeros_like(dqt_ref)

        lane = lax.broadcasted_iota(jnp.int32, (tq, 2 * MLA_V), 1)
        heads = [slice(HEAD_PAD * a, HEAD_PAD * (a + 1)) for a in range(hps)]
        pairs = [slice(2 * MLA_V * p, 2 * MLA_V * (p + 1)) for p in range(hps // 2)]
        for pr in range(hps // 2):
            vpair = v_ref[:, pairs[pr]]
            va_scr[2 * pr] = jnp.where(lane < MLA_V, vpair, jnp.zeros_like(vpair))
            va_scr[2 * pr + 1] = jnp.where(lane >= MLA_V, vpair, jnp.zeros_like(vpair))
        dk_ref[...] = jnp.zeros_like(dk_ref)
        dv_scr[...] = jnp.zeros_like(dv_scr)

        def step(i, masked):
            start = pl.multiple_of(i * tq, tq)
            rows = pl.ds(start, tq)
            scores = [_mm_nt(k_ref[:, heads[a]], q_ref[rows, heads[a]]) for a in range(hps)]
            dps = [_mm_nt(va_scr[a], do_ref[rows, pairs[a // 2]]) for a in range(hps)]
            for a in range(hps):
                pr, r = a // 2, a % 2
                p = jnp.exp2(scores[a] - lse_ref[pr, r:r + 1, rows])
                if masked:
                    kk = lax.broadcasted_iota(jnp.int32, (tq, tq), 0)
                    qq = lax.broadcasted_iota(jnp.int32, (tq, tq), 1)
                    p = jnp.where(kk <= qq, p, 0.0)
                ds = p * (dps[a] - d_ref[pr, r:r + 1, rows])
                dv_scr[a] += _mm(p, do_ref[rows, pairs[pr]])
                dk_ref[:, heads[a]] += _mm(ds, q_ref[rows, heads[a]])
                dqt_ref[heads[a], rows] += _mm(kt_ref[heads[a], :], ds)

        def loop_body(i, _):
            step(i, False)
            return 0

        step(j, True)
        lax.fori_loop(j + 1, nq, loop_body, 0)
        for pr in range(hps // 2):
            dv_ref[:, pairs[pr]] = jnp.where(lane < MLA_V, dv_scr[2 * pr], dv_scr[2 * pr + 1])
        dk_ref[...] = dk_ref[...] * (ATTN_SCALE / ATTN_SCALE_LOG2)

        if ns:
            @pl.when(step_no == steps - 1)
            def _():
                for cp in _chip_swap_copies(rest[:ns], rest[ns + 3:2 * ns + 3], *rest[2 * ns + 5:]):
                    cp.wait()

    stat = pl.BlockSpec((hps // 2, 2, T), lambda g, j: (g, 0, 0))
    return pl.pallas_call(
        body, name="attn_bwd", grid=(MLA_HEADS // hps, nq),
        in_specs=[pl.BlockSpec((T, hps * HEAD_PAD), lambda g, j: (0, g)),
                  pl.BlockSpec((tq, hps * HEAD_PAD), lambda g, j: (j, g)),
                  pl.BlockSpec((hps * HEAD_PAD, tq), lambda g, j: (g, j)),
                  pl.BlockSpec((tq, hps * MLA_V), lambda g, j: (j, g)),
                  pl.BlockSpec((T, hps * MLA_V), lambda g, j: (0, g)), stat, stat] + [ANY] * ns,
        out_specs=[pl.BlockSpec((hps * HEAD_PAD, T), lambda g, j: (g, 0)),
                   pl.BlockSpec((tq, hps * HEAD_PAD), lambda g, j: (j, g)),
                   pl.BlockSpec((tq, hps * MLA_V), lambda g, j: (j, g))] + [ANY] * ns,
        out_shape=[jax.ShapeDtypeStruct((MLA_HEADS * HEAD_PAD, T), F32),
                   jax.ShapeDtypeStruct((T, MLA_HEADS * HEAD_PAD), F32),
                   jax.ShapeDtypeStruct((T, MLA_WIDTH), F32)] + _chip_swap_shapes(send),
        scratch_shapes=[pltpu.VMEM((hps, tq, 2 * MLA_V), vb.dtype), pltpu.VMEM((hps, tq, 2 * MLA_V), F32)]
        + ([pltpu.SemaphoreType.DMA((3 * ns,)), pltpu.SemaphoreType.DMA((3 * ns,))] if ns else []),
        compiler_params=_params(("arbitrary", "arbitrary")),
    )(qb, kb, kt, vb, dob, lse, dvec, *send)


def _cumsum_rows(x):
    n = x.shape[0]
    row = lax.broadcasted_iota(jnp.int32, x.shape, 0)
    s = 1
    while s < n:
        x = x + jnp.where(row >= s, pltpu.roll(x, s, 0), 0.0)
        s *= 2
    return x


def _rev_cumsum_rows(x):
    n = x.shape[0]
    row = lax.broadcasted_iota(jnp.int32, x.shape, 0)
    s = 1
    while s < n:
        x = x + jnp.where(row < n - s, pltpu.roll(x, n - s, 0), 0.0)
        s *= 2
    return x


def _lb_from_logits(l):
    l0, l1 = l[0:1, :], l[1:2, :]
    m = jnp.maximum(l0, l1)
    e0, e1 = jnp.exp(l0 - m), jnp.exp(l1 - m)
    return e0 / (e0 + e1)


def _hgrn_gates(hq, hf, lb):
    sig_f = jax.nn.sigmoid(hf)
    f = lb + (1.0 - lb) * sig_f
    sig_q = jax.nn.sigmoid(hq)
    return sig_f, f, jnp.log(f), 1.0 - f, sig_q, hq * sig_q


def _hgrn_intra(q, kk, b, exact=False):
    row = lax.broadcasted_iota(jnp.int32, b.shape, 0)
    qs, ks, eqs, eks, a_rows = [], [], [], [], []
    for i in range(CHUNK // SUB):
        ref = b[SUB * i + SUB // 2:SUB * i + SUB // 2 + 1, :]
        eq = jnp.exp(b[SUB * i:SUB * (i + 1), :] - ref)
        ek = jnp.exp(jnp.where(row < SUB * (i + 1), ref - b, NEG_BIG))
        qi = q[SUB * i:SUB * (i + 1), :] * eq
        ki = kk * ek
        a_rows.append(_mm_nt(qi, ki, exact))
        qs.append(qi), ks.append(ki), eqs.append(eq), eks.append(ek)
    tt = lax.broadcasted_iota(jnp.int32, (CHUNK, CHUNK), 0)
    ss = lax.broadcasted_iota(jnp.int32, (CHUNK, CHUNK), 1)
    causal = ss <= tt
    a = jnp.where(causal, jnp.concatenate(a_rows, axis=0), 0.0)
    return a, causal, qs, ks, eqs, eks


def _hgrn_fwd(xph, lbl, tg=512):
    T = xph.shape[0]
    ng, ncg = T // tg, tg // CHUNK
    cols = [slice(HGRN_DIM * h, HGRN_DIM * (h + 1)) for h in range(HGRN_HEADS)]

    def body(lbl_ref, hq_ref, hf_ref, hi_ref, o_ref, st_ref, s_scr):
        @pl.when(pl.program_id(0) == 0)
        def _():
            s_scr[...] = jnp.zeros_like(s_scr)

        lb = _lb_from_logits(lbl_ref[...])

        def chunks(it, _):
            pre = []
            for k in range(HGRN_CPI):
                c = it * HGRN_CPI + k
                rows = pl.ds(pl.multiple_of(c * CHUNK, CHUNK), CHUNK)
                for cs in cols:
                    _, _, lf, kk, _, q = _hgrn_gates(hq_ref[rows, cs], hf_ref[rows, cs], lb[:, cs])
                    v = hi_ref[rows, cs]
                    b = _cumsum_rows(lf)
                    a = _hgrn_intra(q, kk, b)[0]
                    b_last = b[CHUNK - 1:CHUNK, :]
                    pre.append((c, rows, q * jnp.exp(b), a, v, jnp.exp(b_last), _mm_tn(v, kk * jnp.exp(b_last - b))))
            for i, (c, rows, qe, a, v, ebl, upd) in enumerate(pre):
                h = i % HGRN_HEADS
                st = s_scr[h]
                st_ref[h, c] = st
                o_ref[rows, cols[h]] = _mm_nt(qe, st) + _mm(a, v)
                s_scr[h] = st * ebl + upd
            return 0

        lax.fori_loop(0, ncg // HGRN_CPI, chunks, 0)

    col = lambda k: pl.BlockSpec((tg, HGRN_WIDTH), lambda g: (g, k))
    return pl.pallas_call(
        body, name="hgrn_fwd", grid=(ng,),
        in_specs=[pl.BlockSpec((2, HGRN_WIDTH), lambda g: (0, 0)), col(0), col(1), col(2)],
        out_specs=[col(0), pl.BlockSpec((HGRN_HEADS, ncg, HGRN_DIM, HGRN_DIM), lambda g: (0, g, 0, 0))],
        out_shape=[jax.ShapeDtypeStruct((T, HGRN_WIDTH), F32),
                   jax.ShapeDtypeStruct((HGRN_HEADS, T // CHUNK, HGRN_DIM, HGRN_DIM), F32)],
        scratch_shapes=[pltpu.VMEM((HGRN_HEADS, HGRN_DIM, HGRN_DIM), F32)],
        compiler_params=_params(("arbitrary",)),
    )(lbl, xph, xph, xph)


def _hgrn_bwd(xph, lbl, states, d_o, fill=(), tg=512):
    T = xph.shape[0]
    ng, ncg = T // tg, tg // CHUNK
    cols = [slice(HGRN_DIM * h, HGRN_DIM * (h + 1)) for h in range(HGRN_HEADS)]
    nsub = CHUNK // SUB
    nf = len(fill)

    def body(lbl_ref, hq_ref, hf_ref, hi_ref, st_ref, do_ref, *rest):
        dhq_ref, dhf_ref, dhi_ref, dlg_ref = rest[nf:nf + 4]
        ds_scr, dlb_scr = rest[2 * nf + 4:2 * nf + 6]
        fill_copies = lambda: _pair_fill_copies(rest[nf + 4:2 * nf + 4], *rest[2 * nf + 6:])
        g = pl.program_id(0)

        @pl.when(g == 0)
        def _():
            ds_scr[...] = jnp.zeros_like(ds_scr)
            dlb_scr[...] = jnp.zeros_like(dlb_scr)
            for cp in (fill_copies()[0] if nf else ()):
                cp.start()

        lb = _lb_from_logits(lbl_ref[...])

        def chunks(it, _):
            pre = []
            for k, h in ((k, h) for k in range(HGRN_CPI) for h in range(HGRN_HEADS)):
                cs = cols[h]
                c = ncg - 1 - (it * HGRN_CPI + k)
                rows = pl.ds(pl.multiple_of(c * CHUNK, CHUNK), CHUNK)
                hq = hq_ref[rows, cs]
                sig_f, f, lf, kk, sig_q, q = _hgrn_gates(hq, hf_ref[rows, cs], lb[:, cs])
                v = hi_ref[rows, cs]
                do = do_ref[rows, cs]
                b = _cumsum_rows(lf)
                eb = jnp.exp(b)
                a, causal, qs, ks, eqs, eks = _hgrn_intra(q, kk, b)
                b_last = b[CHUNK - 1:CHUNK, :]
                st = st_ref[h, c]
                pre.append(dict(h=h, cs=cs, rows=rows, hq=hq, sig_f=sig_f, f=f, kk=kk, sig_q=sig_q, q=q, v=v, eb=eb, qs=qs,
                                ks=ks, eqs=eqs,
                                eks=eks, ebl=jnp.exp(b_last), el=jnp.exp(b_last - b), st=st,
                                da=jnp.where(causal, _mm_nt(do, v, True), 0.0), dq=_mm(do, st, True) * eb,
                                dv=_mm_tn(a, do), dsu=_mm_tn(do, q * eb, True)))
            for w in pre:
                dq_rows = []
                dk = jnp.zeros_like(w["q"])
                for i in range(nsub):
                    dai = w["da"][SUB * i:SUB * (i + 1), :]
                    dq_rows.append(_mm(dai, w["ks"][i], True) * w["eqs"][i])
                    dk = dk + _mm_tn(dai, w["qs"][i], True) * w["eks"][i]
                w["dq"] = w["dq"] + jnp.concatenate(dq_rows, axis=0)
                w["dk"] = dk
            for w in pre:
                h, cs, rows = w["h"], w["cs"], w["rows"]
                kk, el, ebl, dst = w["kk"], w["el"], w["ebl"], ds_scr[h]
                dk_state = _mm(w["v"], dst, True) * el
                dk = w["dk"] + dk_state
                e_last = (ebl * jnp.sum(w["st"] * dst, axis=0, keepdims=True)
                          + jnp.sum(kk * dk_state, axis=0, keepdims=True))
                dlf = _rev_cumsum_rows(w["q"] * w["dq"] - kk * dk) + e_last
                ds_scr[h] = dst * ebl + w["dsu"]
                df = dlf / w["f"] - dk
                sig_f, sig_q = w["sig_f"], w["sig_q"]
                dhf_ref[rows, cs] = df * (1.0 - lb[:, cs]) * sig_f * (1.0 - sig_f)
                dlb_scr[:, cs] += jnp.sum(df * (1.0 - sig_f), axis=0, keepdims=True)
                dhq_ref[rows, cs] = w["dq"] * sig_q * (1.0 + w["hq"] * (1.0 - sig_q))
                dhi_ref[rows, cs] = w["dv"] + _mm_nt(kk * el, dst)
            return 0

        lax.fori_loop(0, ncg // HGRN_CPI, chunks, 0)

        @pl.when(g == ng - 1)
        def _():
            dl0 = dlb_scr[...] * lb * (1.0 - lb)
            dlg_ref[...] = jnp.concatenate([dl0, -dl0], axis=0)
            if nf:
                copies, waits = fill_copies()
                for w in waits:
                    w.wait_recv()
                for cp in copies:
                    cp.wait_send()

    col = lambda k: pl.BlockSpec((tg, HGRN_WIDTH), lambda g: (ng - 1 - g, k))
    logits = pl.BlockSpec((2, HGRN_WIDTH), lambda g: (0, 0))
    big = jax.ShapeDtypeStruct((T, HGRN_WIDTH), F32)
    n_in, n_out = 6, 4
    return pl.pallas_call(
        body, name="hgrn_bwd", grid=(ng,),
        in_specs=[logits, col(0), col(1), col(2),
                  pl.BlockSpec((HGRN_HEADS, ncg, HGRN_DIM, HGRN_DIM), lambda g: (0, ng - 1 - g, 0, 0)), col(0)] + [ANY] * nf,
        out_specs=[col(0), col(0), col(0), logits] + [ANY] * nf,
        out_shape=[big, big, big, jax.ShapeDtypeStruct((2, HGRN_WIDTH), F32)]
        + [jax.ShapeDtypeStruct(f.shape, f.dtype) for f in fill],
        input_output_aliases={n_in + k: n_out + k for k in range(nf)},
        scratch_shapes=[pltpu.VMEM((HGRN_HEADS, HGRN_DIM, HGRN_DIM), F32), pltpu.VMEM((1, HGRN_WIDTH), F32)]
        + ([pltpu.SemaphoreType.DMA((nf,)), pltpu.SemaphoreType.DMA((nf,))] if nf else []),
        compiler_params=_params(("arbitrary",)),
    )(lbl, xph, xph, xph, states, d_o, *fill)


def _proj_fwd(x, o_raw, oh_raw, xph, wout, w_mla, w_hg, w_post, w_fpre, tt=512):
    T = x.shape[0]

    def body(x_ref, o_ref, oh_ref, hg_ref, wout_ref, wmla_ref, whg_ref, wpost_ref, wfpre_ref,
             h1_ref, y1_ref, z_ref, mix_ref):
        om, _, _ = _grms_fwd(o_ref[...], wmla_ref[...], MLA_V)
        hg = hg_ref[...]
        ohn, _, _ = _grms_fwd(oh_ref[...], whg_ref[...], HGRN_DIM)
        mix = jnp.concatenate([om, ohn * (hg * jax.nn.sigmoid(hg))], axis=-1)
        mix_ref[...] = mix.astype(mix_ref.dtype)
        y1 = _mm(mix, wout_ref[...])
        y1_ref[...] = y1
        h1 = x_ref[...] + _rms_fwd(y1, wpost_ref[...])[0]
        h1_ref[...] = h1
        z_ref[...] = _rms_fwd(h1, wfpre_ref[...])[0].astype(z_ref.dtype)

    row = lambda w: pl.BlockSpec((tt, w), lambda i: (i, 0))
    full = lambda a: pl.BlockSpec(a.shape, lambda i: (0,) * a.ndim)
    sds = jax.ShapeDtypeStruct
    return pl.pallas_call(
        body, name="proj_fwd", grid=(T // tt,),
        in_specs=[row(D_MODEL), row(MLA_WIDTH), row(HGRN_WIDTH), pl.BlockSpec((tt, HGRN_WIDTH), lambda i: (i, 3)),
                  full(wout), full(w_mla), full(w_hg), full(w_post), full(w_fpre)],
        out_specs=[row(D_MODEL)] * 4,
        out_shape=[sds((T, D_MODEL), F32), sds((T, D_MODEL), F32), sds((T, D_MODEL), MXU_DTYPE),
                   sds((T, D_MODEL), MXU_DTYPE)],
        compiler_params=_params(("arbitrary",)),
    )(x, o_raw, oh_raw, xph, wout, w_mla, w_hg, w_post, w_fpre)


def _ffn_fwd(zb, h1, tgt, w_fpost, wg, wu, wd, tt=256):
    T = zb.shape[0]
    nj = N_CHIPS

    def body(z_ref, h1_ref, tgt_ref, wfpost_ref, wg_ref, wu_ref, wd_ref, g_ref, up_ref, dy2_ref, dh2_ref, loss_ref, dwf_ref):
        @pl.when(pl.program_id(0) == 0)
        def _():
            loss_ref[...] = jnp.zeros_like(loss_ref)
            dwf_ref[...] = jnp.zeros_like(dwf_ref)

        z = z_ref[...]
        gs = [_mm_nt(z, wg_ref[j]) for j in range(nj)]
        ups = [_mm_nt(z, wu_ref[j]) for j in range(nj)]
        y2 = jnp.zeros((tt, D_MODEL), F32)
        for j in range(nj):
            g_ref[j] = gs[j]
            up_ref[j] = ups[j]
            y2 = y2 + _mm(gs[j] * jax.nn.sigmoid(gs[j]) * ups[j], wd_ref[j])
        w = wfpost_ref[...]
        y2s, y2n, r2 = _rms_fwd(y2, w)
        e = h1_ref[...] + y2s - tgt_ref[...]
        loss_ref[...] += jnp.sum(e * e, axis=0, keepdims=True)
        dh2 = e * (1.0 / D_MODEL)
        dh2_ref[...] = dh2
        dy2, dwf = _rms_bwd(dh2, y2n, r2, w)
        dy2_ref[...] = dy2.astype(dy2_ref.dtype)
        dwf_ref[...] += dwf

    row = pl.BlockSpec((tt, D_MODEL), lambda i: (i, 0))
    vec = pl.BlockSpec((1, D_MODEL), lambda i: (0, 0))
    resident = pl.BlockSpec((nj, FF_SHARD, D_MODEL), lambda i: (0, 0, 0), pipeline_mode=pl.Buffered(1))
    act = pl.BlockSpec((nj, tt, FF_SHARD), lambda i: (0, i, 0))
    sds = jax.ShapeDtypeStruct
    return pl.pallas_call(
        body, name="ffn_fwd", grid=(T // tt,),
        in_specs=[row, row, row, vec, resident, resident, resident],
        out_specs=[act, act, row, row, vec, vec],
        out_shape=[sds((nj, T, FF_SHARD), F32), sds((nj, T, FF_SHARD), F32), sds((T, D_MODEL), MXU_DTYPE),
                   sds((T, D_MODEL), F32), sds((1, D_MODEL), F32), sds((1, D_MODEL), F32)],
        compiler_params=_params(("arbitrary",)),
    )(zb, h1, tgt, w_fpost, wg, wu, wd)


def _ffn_bwd(zb, g, up, dy2b, wg, wu, wd, tt=512):
    T = zb.shape[0]
    nj = N_CHIPS

    def body(z_ref, g_ref, up_ref, dy2_ref, wg_ref, wu_ref, wd_ref, dwg_ref, dwu_ref, dwd_ref, dz_ref):
        @pl.when(pl.program_id(1) == 0)
        def _():
            dwg_ref[...] = jnp.zeros_like(dwg_ref)
            dwu_ref[...] = jnp.zeros_like(dwu_ref)
            dwd_ref[...] = jnp.zeros_like(dwd_ref)

        z, g_, up_, dy2 = z_ref[...], g_ref[0], up_ref[0], dy2_ref[...]
        sg = jax.nn.sigmoid(g_)
        act = g_ * sg
        dff = _mm_nt(dy2, wd_ref[0])
        dwd_ref[0] += _mm_tn(act * up_, dy2)
        dg = dff * up_ * sg * (1.0 + g_ * (1.0 - sg))
        dup = dff * act
        dwg_ref[0] += _mm_tn(dg, z)
        dwu_ref[0] += _mm_tn(dup, z)
        dz_ref[0] = _mm(dg, wg_ref[0]) + _mm(dup, wu_ref[0])

    row = pl.BlockSpec((tt, D_MODEL), lambda j, i: (i, 0))
    act = pl.BlockSpec((1, tt, FF_SHARD), lambda j, i: (j, i, 0))
    w_sh = pl.BlockSpec((1, FF_SHARD, D_MODEL), lambda j, i: (j, 0, 0))
    w_grad = jax.ShapeDtypeStruct((nj, FF_SHARD, D_MODEL), F32)
    return pl.pallas_call(
        body, name="ffn_bwd", grid=(nj, T // tt),
        in_specs=[row, act, act, row, w_sh, w_sh, w_sh],
        out_specs=[w_sh, w_sh, w_sh, pl.BlockSpec((1, tt, D_MODEL), lambda j, i: (j, i, 0))],
        out_shape=[w_grad, w_grad, w_grad, jax.ShapeDtypeStruct((nj, T, D_MODEL), F32)],
        compiler_params=_params(("arbitrary", "arbitrary")),
    )(zb, g, up, dy2b, wg, wu, wd)


def _mid_bwd(dzp, dh2, h1, y1, mixb, o_raw, oh_raw, xph, wout, w_fpre, w_post, w_mla, w_hg, swap=(), tt=256):
    T = dh2.shape[0]
    nsw = len(swap)
    n_in, n_out = 13, 10

    def body(*refs):
        (dzp_ref, dh2_ref, h1_ref, y1_ref, mix_ref, o_ref, oh_ref, hg_ref, wout_ref, wfpre_ref, wpost_ref,
         wmla_ref, whg_ref) = refs[:n_in]
        (dh1_ref, dwout_ref, do_ref, doh_ref, dhg_ref, dvec_ref, dwfpre_ref, dwpost_ref, dwmla_ref,
         dwhg_ref) = refs[n_in + nsw:n_in + nsw + n_out]
        swap_copies = lambda: _pair_swap_copies(refs[n_in:n_in + nsw], refs[n_in + nsw + n_out:n_in + 2 * nsw + n_out],
                                                *refs[n_in + 2 * nsw + n_out:])

        @pl.when(pl.program_id(0) == 0)
        def _():
            for r in (dwout_ref, dwfpre_ref, dwpost_ref, dwmla_ref, dwhg_ref):
                r[...] = jnp.zeros_like(r)
            for cp in (swap_copies() if nsw else ()):
                cp.start()

        dz = dzp_ref[0] + dzp_ref[1] + dzp_ref[2] + dzp_ref[3]
        wfpre = wfpre_ref[...]
        _, h1n, r = _rms_fwd(h1_ref[...], wfpre)
        dh1_z, dwfpre = _rms_bwd(dz, h1n, r, wfpre)
        dwfpre_ref[...] += dwfpre
        dh1 = dh2_ref[...] + dh1_z
        dh1_ref[...] = dh1
        wpost = wpost_ref[...]
        _, y1n, r1 = _rms_fwd(y1_ref[...], wpost)
        dy1, dwpost = _rms_bwd(dh1, y1n, r1, wpost)
        dwpost_ref[...] += dwpost
        dmix = _mm_nt(dy1, wout_ref[...])
        dwout_ref[...] += _mm_tn(mix_ref[...], dy1)
        wmla = wmla_ref[...]
        o = o_ref[...]
        _, on, ro = _grms_fwd(o, wmla, MLA_V)
        d_o, dwmla = _grms_bwd(dmix[:, :MLA_WIDTH], on, ro, wmla, MLA_V)
        dwmla_ref[...] += dwmla
        do_ref[...] = d_o.astype(do_ref.dtype)
        hh = lax.broadcasted_iota(jnp.int32, (MLA_HEADS, MLA_WIDTH), 0)
        ll = lax.broadcasted_iota(jnp.int32, (MLA_HEADS, MLA_WIDTH), 1)
        sel = jnp.where((ll >= hh * MLA_V) & (ll < (hh + 1) * MLA_V), 1.0, 0.0)
        dvec_ref[...] = _mm_nt(sel, d_o * o, True)
        whg = whg_ref[...]
        hg = hg_ref[...]
        sg = jax.nn.sigmoid(hg)
        _, ohn, rh = _grms_fwd(oh_ref[...], whg, HGRN_DIM)
        dmh = dmix[:, MLA_WIDTH:]
        dhg_ref[...] = dmh * ohn * whg * sg * (1.0 + hg * (1.0 - sg))
        d_oh, dwhg = _grms_bwd(dmh * (hg * sg), ohn, rh, whg, HGRN_DIM)
        dwhg_ref[...] += dwhg
        doh_ref[...] = d_oh

        if nsw:
            @pl.when(pl.program_id(0) == T // tt - 1)
            def _():
                for cp in swap_copies():
                    cp.wait()

    row = lambda w: pl.BlockSpec((tt, w), lambda i: (i, 0))
    full = lambda a: pl.BlockSpec(a.shape, lambda i: (0,) * a.ndim)
    vec = lambda w: pl.BlockSpec((1, w), lambda i: (0, 0))
    sds = jax.ShapeDtypeStruct
    return pl.pallas_call(
        body, name="mid_bwd", grid=(T // tt,),
        in_specs=[pl.BlockSpec((N_CHIPS, tt, D_MODEL), lambda i: (0, i, 0)), row(D_MODEL), row(D_MODEL), row(D_MODEL),
                  row(D_MODEL), row(MLA_WIDTH), row(HGRN_WIDTH), pl.BlockSpec((tt, HGRN_WIDTH), lambda i: (i, 3)),
                  full(wout), vec(D_MODEL), vec(D_MODEL), vec(MLA_WIDTH), vec(HGRN_WIDTH)] + [ANY] * nsw,
        out_specs=[row(D_MODEL), full(wout), row(MLA_WIDTH), row(HGRN_WIDTH), row(HGRN_WIDTH),
                   pl.BlockSpec((MLA_HEADS, tt), lambda i: (0, i)),
                   vec(D_MODEL), vec(D_MODEL), vec(MLA_WIDTH), vec(HGRN_WIDTH)] + [ANY] * nsw,
        out_shape=[sds((T, D_MODEL), F32), sds(wout.shape, F32), sds((T, MLA_WIDTH), MXU_DTYPE), sds((T, HGRN_WIDTH), F32),
                   sds((T, HGRN_WIDTH), F32), sds((MLA_HEADS, T), F32),
                   sds((1, D_MODEL), F32), sds((1, D_MODEL), F32), sds((1, MLA_WIDTH), F32), sds((1, HGRN_WIDTH), F32)]
        + _half_stack_shapes(swap),
        scratch_shapes=[pltpu.SemaphoreType.DMA((nsw,)), pltpu.SemaphoreType.DMA((nsw,))] if nsw else [],
        compiler_params=_params(("arbitrary",)),
    )(dzp, dh2, h1, y1, mixb, o_raw, oh_raw, xph, wout, w_fpre, w_post, w_mla, w_hg, *swap)


def _in_bwd(x, dh1, cq, ckv, dq, dk, dv, dhq, dhf, dhi, dhg, rc, rs, w_pre, win, qnw, wq, kvnw, wk, wv, tt=256):
    T = x.shape[0]

    def body(x_ref, dh1_ref, cq_ref, ckv_ref, dq_ref, dk_ref, dv_ref, dhq_ref, dhf_ref, dhi_ref, dhg_ref, rc_ref, rs_ref,
             wpre_ref, win_ref, qnw_ref, wq_ref, kvnw_ref, wk_ref, wv_ref,
             dx_ref, dwin_ref, dwq_ref, dwk_ref, dwv_ref, dwpre_ref, dqnw_ref, dkvnw_ref):
        @pl.when(pl.program_id(0) == 0)
        def _():
            for r in (dwin_ref, dwq_ref, dwk_ref, dwv_ref, dwpre_ref, dqnw_ref, dkvnw_ref):
                r[...] = jnp.zeros_like(r)

        def add_win_grad(r, first):
            for arr0, n, chip, row0 in _win_grad_segments():
                if first <= arr0 and arr0 + n <= first + r.shape[0]:
                    dwin_ref[chip, row0:row0 + n, :] += r[arr0 - first:arr0 - first + n]

        lo = Q_RANK + KV_RANK + HEAD_PAD
        dxp_h = jnp.concatenate([dhq_ref[...], dhf_ref[...], dhi_ref[...], dhg_ref[...]], axis=-1)
        du = _mm(dxp_h, win_ref[lo:, :])
        wpre = wpre_ref[...]
        u, xn, rx = _rms_fwd(x_ref[...], wpre)
        add_win_grad(_mm_tn(dxp_h, u), lo)
        c, sa, sb = _rope_tables(rc_ref[...], rs_ref[...])
        lane = lax.broadcasted_iota(jnp.int32, (tt, HEAD_PAD), 1)
        dk_all = dk_ref[...]
        dq_lin = []
        dkr = jnp.zeros((tt, HEAD_PAD), F32)
        for h in range(MLA_HEADS):
            sl = slice(HEAD_PAD * h, HEAD_PAD * (h + 1))
            dq_lin.append(_rope_bwd(dq_ref[sl, :].T * ATTN_SCALE, c, sa, sb))
            dkr = dkr + dk_all[:, sl]
        dq_lin = jnp.concatenate(dq_lin, axis=-1)
        dkr = jnp.where((lane >= MLA_NOPE) & (lane < MLA_QK), _rope_bwd(dkr, c, sa, sb), 0.0)
        qnw = qnw_ref[...]
        qn, cqn, rq = _rms_fwd(cq_ref[...], qnw)
        dwq_ref[...] += _mm_tn(qn, dq_lin)
        dcq, dqnw = _rms_bwd(_mm_nt(dq_lin, wq_ref[...]), cqn, rq, qnw)
        dqnw_ref[...] += dqnw
        kvnw = kvnw_ref[...]
        kvn, ckvn, rkv = _rms_fwd(ckv_ref[...], kvnw)
        dv_ = dv_ref[...]
        dwk_ref[...] += _mm_tn(kvn, dk_all)
        dwv_ref[...] += _mm_tn(kvn, dv_)
        dckv, dkvnw = _rms_bwd(_mm_nt(dk_all, wk_ref[...]) + _mm_nt(dv_, wv_ref[...]), ckvn, rkv, kvnw)
        dkvnw_ref[...] += dkvnw
        dxp_a = jnp.concatenate([dcq, dckv, dkr], axis=-1)
        add_win_grad(_mm_tn(dxp_a, u), 0)
        dx_u, dwpre = _rms_bwd(du + _mm(dxp_a, win_ref[:lo, :]), xn, rx, wpre)
        dwpre_ref[...] += dwpre
        dx_ref[...] = dh1_ref[...] + dx_u

    row = lambda w: pl.BlockSpec((tt, w), lambda i: (i, 0))
    full = lambda a: pl.BlockSpec(a.shape, lambda i: (0,) * a.ndim)
    sds = jax.ShapeDtypeStruct
    qk_w = MLA_HEADS * HEAD_PAD
    return pl.pallas_call(
        body, name="in_bwd", grid=(T // tt,),
        in_specs=[row(D_MODEL), row(D_MODEL), row(Q_RANK), row(KV_RANK), pl.BlockSpec((qk_w, tt), lambda i: (0, i)),
                  row(qk_w), row(MLA_WIDTH),
                  row(HGRN_WIDTH), row(HGRN_WIDTH), row(HGRN_WIDTH), row(HGRN_WIDTH), row(HEAD_PAD), row(HEAD_PAD),
                  full(w_pre), full(win), full(qnw), full(wq), full(kvnw), full(wk), full(wv)],
        out_specs=[row(D_MODEL), pl.BlockSpec(WIN_COMM_SHAPE, lambda i: (0, 0, 0)), full(wq), full(wk), full(wv),
                   full(w_pre), full(qnw), full(kvnw)],
        out_shape=[sds((T, D_MODEL), F32), sds(WIN_COMM_SHAPE, F32), sds(wq.shape, F32), sds(wk.shape, F32),
                   sds(wv.shape, F32), sds(w_pre.shape, F32), sds(qnw.shape, F32), sds(kvnw.shape, F32)],
        compiler_params=_params(("arbitrary",)),
    )(x, dh1, cq, ckv, dq, dk, dv, dhq, dhf, dhi, dhg, rc, rs, w_pre, win, qnw, wq, kvnw, wk, wv)


def _arrange_weights(win_t, wuq_full, wukv):
    dt = win_t.dtype
    z = lambda n: jnp.zeros((n, D_MODEL), dt)
    s2 = Q_RANK + KV_RANK
    win_arr = jnp.concatenate([win_t[:s2], z(MLA_NOPE), win_t[s2:s2 + MLA_ROPE], z(HEAD_PAD - MLA_QK),
                               win_t[s2 + MLA_ROPE:]], axis=0)
    wq_arr = jnp.pad(wuq_full, ((0, 0), (0, 0), (0, HEAD_PAD - MLA_QK))).reshape(Q_RANK, MLA_HEADS * HEAD_PAD)
    wk_arr = jnp.pad(wukv[:, :, :MLA_NOPE], ((0, 0), (0, 0), (0, HEAD_PAD - MLA_NOPE))).reshape(
        KV_RANK, MLA_HEADS * HEAD_PAD)
    wv_arr = wukv[:, :, MLA_NOPE:].reshape(KV_RANK, MLA_WIDTH)
    return win_arr, wq_arr, wk_arr, wv_arr


WIN_COMM_SHAPE = (N_CHIPS, FF_SHARD, D_MODEL)


def _win_grad_segments():
    s2 = Q_RANK + KV_RANK
    runs = [(0, s2, 0), (s2, s2 + MLA_ROPE, MLA_NOPE), (s2 + MLA_ROPE, D_IN, HEAD_PAD - MLA_ROPE)]
    per = D_IN // N_CHIPS
    segs = []
    for lo, hi, shift in runs:
        for k in range(N_CHIPS):
            a, b = max(lo, per * k), min(hi, per * (k + 1))
            if a < b:
                segs.append((a + shift, b - a, k, a - per * k))
    return segs


def _unarrange_grads(dwq_arr, dwk_arr, dwv_arr):
    dwuq = dwq_arr.reshape(Q_RANK, MLA_HEADS, HEAD_PAD)[:, :, :MLA_QK]
    dwukv = jnp.concatenate([dwk_arr.reshape(KV_RANK, MLA_HEADS, HEAD_PAD)[:, :, :MLA_NOPE],
                             dwv_arr.reshape(KV_RANK, MLA_HEADS, MLA_V)], axis=-1)
    return dwuq, dwukv


def _rope_inv_freq():
    inv = 1.0 / (ROPE_THETA ** (jnp.arange(0, MLA_ROPE, 2, dtype=F32) / MLA_ROPE))
    z = lambda n: jnp.zeros((n,), F32)
    return jnp.concatenate([z(MLA_NOPE), inv, inv, z(HEAD_PAD - MLA_QK)]).reshape(1, HEAD_PAD)


def _local_step(x, pos, tgt, small, win_arr, wq_arr, wk_arr, wv_arr, late, place=None):
    invf = _rope_inv_freq()
    cq, ckv, xph, qb, kb, vb, kt, vt, rc, rs = _in_fwd(x, pos, invf, small["attn_pre_norm"], win_arr, small["mla_q_norm"],
                                               wq_arr, small["mla_kv_norm"], wk_arr, wv_arr)
    if place is None:
        o_raw, lse = _attn_fwd_t(qb, kb, vt)
        wout, wg, wu, wd = late
    else:
        o_raw, lse, *stacks = _attn_fwd_t(qb, kb, vt, gather=late)
        wout, wg, wu, wd = [lax.dynamic_update_slice(s, l[None], (place[1], 0, 0)) for s, l in zip(stacks, late)]
        wout = wout.reshape(D_MODEL, D_MODEL)
    oh_raw, states = _hgrn_fwd(xph, small["hgrn_lb_logits"])
    h1, y1, zb, mixb = _proj_fwd(x, o_raw, oh_raw, xph, wout, small["mla_out_norm"], small["hgrn_out_norm"],
                                 small["attn_post_norm"], small["ffn_pre_norm"])
    g, up, dy2b, dh2, loss_acc, d_fpost = _ffn_fwd(zb, h1, tgt, small["ffn_post_norm"], wg, wu, wd)
    dwg, dwu, dwd, dzp = _ffn_bwd(zb, g, up, dy2b, wg, wu, wd)
    ffn_grads = [] if place is None else [dwg, dwu, dwd]
    dh1, dwout, d_o, d_oh, dhg, dvec, d_fpre, d_post, d_mla, d_hg, *ffn_rs = _mid_bwd(
        dzp, dh2, h1, y1, mixb, o_raw, oh_raw, xph, wout, small["ffn_pre_norm"], small["attn_post_norm"],
        small["mla_out_norm"], small["hgrn_out_norm"], swap=ffn_grads)
    ffn_ps = _pair_sum(place, ffn_grads, ffn_rs, name="pair_sum_ffn") if ffn_grads else []
    dq, dk, dv, *ffn_ris = _attn_bwd_t(qb, kb, kt, vb, d_o, lse, dvec.reshape(lse.shape), send=ffn_ps)
    ffn_sums = _chip_sum(place, ffn_grads, ffn_rs, ffn_ris, name="chip_sum_ffn") if ffn_grads else []
    dhq, dhf, dhi, d_lbl, *ffn_final = _hgrn_bwd(xph, small["hgrn_lb_logits"], states, d_oh, fill=ffn_sums)
    dx, dwin4, dwq_arr, dwk_arr, dwv_arr, d_pre, d_qn, d_kvn = _in_bwd(
        x, dh1, cq, ckv, dq, dk, dv, dhq, dhf, dhi, dhg, rc, rs, small["attn_pre_norm"], win_arr,
        small["mla_q_norm"], wq_arr, small["mla_kv_norm"], wk_arr, wv_arr)
    dwuq, dwukv = _unarrange_grads(dwq_arr, dwk_arr, dwv_arr)
    loss = 0.5 * jnp.sum(loss_acc) * (1.0 / D_MODEL)
    grads = dict(attn_pre_norm=d_pre, w_in=dwin4, mla_q_norm=d_qn, mla_w_uq=dwuq, mla_kv_norm=d_kvn, mla_w_ukv=dwukv,
                 mla_out_norm=d_mla, hgrn_lb_logits=d_lbl, hgrn_out_norm=d_hg, w_out=dwout, attn_post_norm=d_post,
                 ffn_pre_norm=d_fpre, w_gate=dwg, w_up=dwu, w_down=dwd, ffn_post_norm=d_fpost)
    if place is None:
        return loss, dx, grads
    return loss, dx, grads, ffn_final


def _place():
    x, y, c = lax.axis_index("x"), lax.axis_index("y"), lax.axis_index("c")
    others = [(1 - x, y), (x, 1 - y), (1 - x, 1 - y)]
    return x, y, c, 2 * x + y, (x, y, 1 - c), others


def _half(ref, c, rows):
    return ref.at[pl.ds(pl.multiple_of(c * rows, 8), rows)]


def _rcopy(src, dst, send, recv, k, to):
    return pltpu.make_async_remote_copy(src_ref=src, dst_ref=dst, send_sem=send.at[k], recv_sem=recv.at[k],
                                        device_id=to, device_id_type=MESH)


class _Gather:
    def __init__(self, ins, outs, send, recv):
        self.ins, self.outs, self.send, self.recv = ins, outs, send, recv
        self.n = len(ins)
        self.halves = [r.shape[0] // 2 for r in ins]
        _, _, self.c, self.me, self.sib, self.others = _place()

    def _each(self):
        for j, (px, py) in enumerate(self.others):
            for a in range(self.n):
                yield j * self.n + a, a, 2 * px + py, (px, py, self.c)

    def sends(self):
        return [_rcopy(_half(self.ins[a], self.c, self.halves[a]), _half(self.outs[a].at[self.me], self.c, self.halves[a]),
                       self.send, self.recv, k, to) for k, a, _, to in self._each()]

    def arrivals(self):
        parts = [(k, _half(self.outs[a].at[chip], self.c, self.halves[a]), to) for k, a, chip, to in self._each()]
        return [_rcopy(p, p, self.send, self.recv, k, to) for k, p, to in parts]

    def forwards(self):
        parts = [(k, _half(self.outs[a].at[chip], self.c, self.halves[a])) for k, a, chip, _ in self._each()]
        return [_rcopy(p, p, self.send, self.recv, 3 * self.n + k, self.sib) for k, p in parts]

    def forward_arrivals(self):
        parts = [(k, _half(self.outs[a].at[chip], 1 - self.c, self.halves[a])) for k, a, chip, _ in self._each()]
        return [_rcopy(p, p, self.send, self.recv, 3 * self.n + k, self.sib) for k, p in parts]

    @staticmethod
    def out_shapes(arrs):
        return [jax.ShapeDtypeStruct((N_CHIPS,) + a.shape, a.dtype) for a in arrs]

    @staticmethod
    def semaphores(arrs):
        return [pltpu.SemaphoreType.DMA((6 * len(arrs),)), pltpu.SemaphoreType.DMA((6 * len(arrs),))]


def _gather_chips(arrs, name):
    n = len(arrs)

    def body(*refs):
        gat = _Gather(refs[:n], refs[n:2 * n], *refs[2 * n:])
        sends, forwards = gat.sends(), gat.forwards()
        for cp in sends:
            cp.start()
        for arrival, fw in zip(gat.arrivals(), forwards):
            arrival.wait_recv()
            fw.start()
        for arrival in gat.forward_arrivals():
            arrival.wait_recv()
        for cp in sends + forwards:
            cp.wait_send()

    return pl.pallas_call(body, name=name, in_specs=[ANY] * n, out_specs=[ANY] * n, out_shape=_Gather.out_shapes(arrs),
                          scratch_shapes=_Gather.semaphores(arrs))(*arrs)


GRAD_BLOCKS = 2


def _pair_swap_copies(g_refs, r_refs, send, recv):
    _, _, c, _, sib, _ = _place()
    copies = []
    for a, (g, r) in enumerate(zip(g_refs, r_refs)):
        h = g.shape[1] // 2
        copies.append(_rcopy(g.at[:, pl.ds(pl.multiple_of((1 - c) * h, 8), h)], r, send, recv, a, sib))
    return copies


def _half_stack_shapes(gs, dtype=None):
    return [jax.ShapeDtypeStruct((N_CHIPS, g.shape[1] // 2, g.shape[2]), dtype or g.dtype) for g in gs]


def _pair_swap(gs, sm):
    n = len(gs)

    def body(*refs):
        g_refs, sm_ref = refs[:n], refs[n]
        r_refs, ssib_ref = refs[n + 1:2 * n + 1], refs[2 * n + 1]
        send, recv = refs[2 * n + 2:]
        copies = _pair_swap_copies(g_refs, r_refs, send, recv)
        copies.append(_rcopy(sm_ref, ssib_ref, send, recv, n, _place()[4]))
        for cp in copies:
            cp.start()
        for cp in copies:
            cp.wait()

    return pl.pallas_call(
        body, name="pair_swap", in_specs=[ANY] * (n + 1), out_specs=[ANY] * (n + 1),
        out_shape=_half_stack_shapes(gs) + [jax.ShapeDtypeStruct(sm.shape, sm.dtype)],
        scratch_shapes=[pltpu.SemaphoreType.DMA((n + 1,)), pltpu.SemaphoreType.DMA((n + 1,))],
    )(*gs, sm)


def _pair_sum(place, gs, rs, small=None, name="pair_sum"):
    n = len(gs)
    nb = GRAD_BLOCKS

    def body(place_ref, *refs):
        g_refs, r_refs, p_refs = refs[:n], refs[n:2 * n], refs[-n - 1:-1] if small else refs[-n:]
        for a in range(n):
            p_refs[a][0] = (g_refs[a][0] + r_refs[a][0]).astype(p_refs[a].dtype)
        if small:
            @pl.when((pl.program_id(0) == 0) & (pl.program_id(1) == 0))
            def _():
                refs[-1][...] = refs[2 * n][...] + refs[2 * n + 1][...]

    in_specs, out_specs = [], []
    for g in gs:
        blk = (1, g.shape[1] // 2 // nb, g.shape[2])
        in_specs.append(pl.BlockSpec(blk, lambda i, k, p: (k, p[0] * nb + i, 0)))
    for g in gs:
        blk = (1, g.shape[1] // 2 // nb, g.shape[2])
        in_specs.append(pl.BlockSpec(blk, lambda i, k, p: (k, i, 0)))
        out_specs.append(pl.BlockSpec(blk, lambda i, k, p: (k, i, 0)))
    out_shape = _half_stack_shapes(gs, BF16)
    if small:
        sm_spec = pl.BlockSpec(small[0].shape, lambda i, k, p: (0, 0))
        in_specs += [sm_spec, sm_spec]
        out_specs.append(sm_spec)
        out_shape.append(jax.ShapeDtypeStruct(small[0].shape, F32))
    return pl.pallas_call(
        body, name=name,
        grid_spec=pltpu.PrefetchScalarGridSpec(num_scalar_prefetch=1, grid=(nb, N_CHIPS), in_specs=in_specs,
                                               out_specs=out_specs),
        out_shape=out_shape,
        compiler_params=_params(("arbitrary", "arbitrary")),
    )(place, *gs, *rs, *(small or ()))


def _chip_swap_copies(p_refs, ri_refs, send, recv):
    _, _, c, _, _, others = _place()
    n = len(p_refs)
    return [_rcopy(p_refs[a].at[2 * px + py], ri_refs[a].at[j], send, recv, j * n + a, (px, py, c))
            for j, (px, py) in enumerate(others) for a in range(n)]


def _chip_swap_shapes(ps):
    return [jax.ShapeDtypeStruct((3,) + p.shape[1:], p.dtype) for p in ps]


def _chip_swap(ps, pair):
    n = len(ps)

    def body(*refs):
        start, finish = _chip_swap_plan(refs[:n], refs[n], refs[n + 1:2 * n + 1], refs[2 * n + 1], *refs[2 * n + 2:])
        start()
        finish()

    return pl.pallas_call(
        body, name="chip_swap", in_specs=[ANY] * (n + 1), out_specs=[ANY] * (n + 1),
        out_shape=_chip_swap_out_shapes(ps, pair), scratch_shapes=_chip_swap_semaphores(n),
    )(*ps, pair)


def _chip_swap_plan(p_refs, pair_ref, ri_refs, sm4_ref, send, recv, lsem):
    n = len(p_refs)
    hs = SMALL_ROWS // 2
    x, y, c, me, sib, others = _place()
    local = pltpu.make_async_copy(pair_ref, sm4_ref.at[me], lsem.at[0])
    copies = _chip_swap_copies(p_refs, ri_refs, send, recv)
    arrivals = list(copies)
    for j, (px, py) in enumerate(others):
        copies.append(_rcopy(_half(pair_ref, c, hs), _half(sm4_ref.at[me], c, hs), send, recv, 3 * n + j, (px, py, c)))
        part = _half(sm4_ref.at[2 * px + py], c, hs)
        arrivals.append(_rcopy(part, part, send, recv, 3 * n + j, (px, py, c)))

    def start():
        local.start()
        for cp in copies:
            cp.start()

    def finish():
        for arrival in arrivals:
            arrival.wait_recv()
        for cp in copies:
            cp.wait_send()
        local.wait()

    return start, finish


def _chip_swap_out_shapes(ps, pair):
    return _chip_swap_shapes(ps) + [jax.ShapeDtypeStruct((N_CHIPS,) + pair.shape, pair.dtype)]


def _chip_swap_semaphores(n):
    k = 3 * (n + 1)
    return [pltpu.SemaphoreType.DMA((k,)), pltpu.SemaphoreType.DMA((k,)), pltpu.SemaphoreType.DMA((1,))]


def _chip_sum(place, gs, rs, ris, name="chip_sum"):
    n = len(gs)
    nb = GRAD_BLOCKS

    def body(place_ref, *refs):
        g_refs, r_refs, ri_refs, o_refs = refs[:n], refs[n:2 * n], refs[2 * n:3 * n], refs[3 * n:]
        for a in range(n):
            ri = ri_refs[a]
            o_refs[a][...] = (g_refs[a][0] + r_refs[a][0]) + ri[0].astype(F32) + ri[1].astype(F32) + ri[2].astype(F32)

    in_specs, out_specs, out_shape = [], [], []
    for g in gs:
        blk = (1, g.shape[1] // 2 // nb, g.shape[2])
        in_specs.append(pl.BlockSpec(blk, lambda i, p: (p[1], p[0] * nb + i, 0)))
    for g in gs:
        blk = (1, g.shape[1] // 2 // nb, g.shape[2])
        in_specs.append(pl.BlockSpec(blk, lambda i, p: (p[1], i, 0)))
    for g in gs:
        rb = g.shape[1] // 2 // nb
        in_specs.append(pl.BlockSpec((3, rb, g.shape[2]), lambda i, p: (0, i, 0)))
        out_specs.append(pl.BlockSpec((rb, g.shape[2]), lambda i, p: (p[0] * nb + i, 0)))
        out_shape.append(jax.ShapeDtypeStruct(g.shape[1:], F32))
    return pl.pallas_call(
        body, name=name,
        grid_spec=pltpu.PrefetchScalarGridSpec(num_scalar_prefetch=1, grid=(nb,), in_specs=in_specs, out_specs=out_specs),
        out_shape=out_shape,
        compiler_params=_params(("arbitrary",)),
    )(place, *gs, *rs, *ris)


def _pair_fill_copies(g_refs, send, recv):
    _, _, c, _, sib, _ = _place()
    copies, waits = [], []
    for a, g in enumerate(g_refs):
        h = g.shape[0] // 2
        mine, theirs = _half(g, c, h), _half(g, 1 - c, h)
        copies.append(_rcopy(mine, mine, send, recv, a, sib))
        waits.append(_rcopy(theirs, theirs, send, recv, a, sib))
    return copies, waits


def _pair_fill(gfs, sm4):
    n = len(gfs)
    hs = SMALL_ROWS // 2

    def body(*refs):
        g_refs, sm4_ref = refs[n + 1:2 * n + 1], refs[2 * n + 1]
        send, recv = refs[2 * n + 2:]
        x, y, c, me, sib, others = _place()
        copies, waits = _pair_fill_copies(g_refs, send, recv)
        for j, (px, py) in enumerate(others):
            chip = 2 * px + py
            mine, theirs = _half(sm4_ref.at[chip], c, hs), _half(sm4_ref.at[chip], 1 - c, hs)
            copies.append(pltpu.make_async_remote_copy(src_ref=mine, dst_ref=mine, send_sem=send.at[n + j],
                                                       recv_sem=recv.at[n + j], device_id=sib, device_id_type=MESH))
            waits.append(pltpu.make_async_remote_copy(src_ref=theirs, dst_ref=theirs, send_sem=send.at[n + j],
                                                      recv_sem=recv.at[n + j], device_id=sib, device_id_type=MESH))
        for cp in copies:
            cp.start()
        for w in waits:
            w.wait_recv()
        for cp in copies:
            cp.wait_send()

    return pl.pallas_call(
        body, name="pair_fill", in_specs=[ANY] * (n + 1), out_specs=[ANY] * (n + 1),
        out_shape=[jax.ShapeDtypeStruct(g.shape, g.dtype) for g in gfs] + [jax.ShapeDtypeStruct(sm4.shape, sm4.dtype)],
        input_output_aliases={i: i for i in range(n + 1)},
        scratch_shapes=[pltpu.SemaphoreType.DMA((n + 3,)), pltpu.SemaphoreType.DMA((n + 3,))],
    )(*gfs, sm4)


def _adamw_math(w, g, m, v):
    m = ADAM_B1 * m + (1.0 - ADAM_B1) * g
    v = ADAM_B2 * v + (1.0 - ADAM_B2) * (g * g)
    m_hat = m / (1.0 - ADAM_B1 ** ADAM_STEP)
    v_hat = v / (1.0 - ADAM_B2 ** ADAM_STEP)
    return -ADAM_LR * (m_hat / (jnp.sqrt(v_hat) + ADAM_EPS) + ADAM_WD * w), m, v


def _adamw(items, steps, name):
    n = len(items)

    def body(*refs):
        for a in range(n):
            g = refs[4 * a + 1][...]
            d, mo, vo = _adamw_math(refs[4 * a][...], g, refs[4 * a + 2][...], refs[4 * a + 3][...])
            for out, val in zip(refs[4 * n + 4 * a:4 * n + 4 * a + 4], (g, d, mo, vo)):
                out[...] = val

    spec = lambda w: pl.BlockSpec((w.shape[0] // steps, w.shape[1]), lambda i: (i, 0))
    flat = pl.pallas_call(
        body, name=name, grid=(steps,), in_specs=[spec(it[0]) for it in items for _ in range(4)],
        out_specs=[spec(it[0]) for it in items for _ in range(4)],
        out_shape=[jax.ShapeDtypeStruct(it[0].shape, F32) for it in items for _ in range(4)],
        compiler_params=_params(("arbitrary",)),
    )(*[a for it in items for a in it])
    return [flat[4 * a:4 * a + 4] for a in range(n)]


def _adamw_small(sm4, wmv):
    views = SMALL_VIEWS[:-1]
    n = len(views)

    def body(sm4_ref, *refs):
        g_all = ((sm4_ref[0] + sm4_ref[1]) + sm4_ref[2]) + sm4_ref[3]
        row = 0
        for a, (_, rows, cols) in enumerate(views):
            g = g_all[row:row + rows, :cols]
            row += -(-rows // ROW_TILE) * ROW_TILE
            d, mo, vo = _adamw_math(refs[3 * a][...], g, refs[3 * a + 1][...], refs[3 * a + 2][...])
            for out, val in zip(refs[3 * n + 4 * a:3 * n + 4 * a + 4], (g, d, mo, vo)):
                out[...] = val
        refs[-1][...] = g_all[row:row + 1, :128]

    flat = pl.pallas_call(
        body, name="adamw_small",
        out_shape=[jax.ShapeDtypeStruct((rows, cols), F32) for _, rows, cols in views for _ in range(4)]
        + [jax.ShapeDtypeStruct((1, 128), F32)],
        compiler_params=pltpu.CompilerParams(vmem_limit_bytes=VMEM_LIMIT),
    )(sm4, *[a for t in wmv for a in t])
    return [flat[4 * a:4 * a + 4] for a in range(n)] + [flat[-1]]


SMALL_NAMES = ("attn_pre_norm", "mla_q_norm", "mla_kv_norm", "mla_w_ukv", "mla_out_norm", "hgrn_lb_logits",
               "hgrn_out_norm", "attn_post_norm", "ffn_pre_norm", "ffn_post_norm")
BIG_NAMES = ("w_in", "mla_w_uq", "w_out", "w_gate", "w_up", "w_down")
WEIGHT_NAMES = ("attn_pre_norm", "w_in", "mla_q_norm", "mla_w_uq", "mla_kv_norm", "mla_w_ukv", "mla_out_norm",
                "hgrn_lb_logits", "hgrn_out_norm", "w_out", "attn_post_norm", "ffn_pre_norm", "w_gate", "w_up", "w_down",
                "ffn_post_norm")


UQ_COMM_SHAPE = (192, 384)


def _pack_small(vals):
    parts = []
    for name, rows, cols in SMALL_VIEWS:
        pad_rows = -(-rows // ROW_TILE) * ROW_TILE - rows
        parts.append(jnp.pad(vals[name].reshape(rows, cols), ((0, pad_rows), (0, D_MODEL - cols))))
    return jnp.concatenate(parts, axis=0)


def kernel(x, positions, attn_pre_norm, w_in, mla_q_norm, mla_w_uq, mla_kv_norm, mla_w_ukv, mla_out_norm, hgrn_lb_logits, hgrn_out_norm, w_out, attn_post_norm, ffn_pre_norm, w_gate, w_up, w_down, ffn_post_norm, loss_target, m_attn_pre_norm, m_w_in, m_mla_q_norm, m_mla_w_uq, m_mla_kv_norm, m_mla_w_ukv, m_mla_out_norm, m_hgrn_lb_logits, m_hgrn_out_norm, m_w_out, m_attn_post_norm, m_ffn_pre_norm, m_w_gate, m_w_up, m_w_down, m_ffn_post_norm, v_attn_pre_norm, v_w_in, v_mla_q_norm, v_mla_w_uq, v_mla_kv_norm, v_mla_w_ukv, v_mla_out_norm, v_hgrn_lb_logits, v_hgrn_out_norm, v_w_out, v_attn_post_norm, v_ffn_pre_norm, v_w_gate, v_w_up, v_w_down, v_ffn_post_norm):
    args = locals()
    W = {n: args[n] for n in WEIGHT_NAMES}
    M = {n: args["m_" + n] for n in WEIGHT_NAMES}
    V = {n: args["v_" + n] for n in WEIGHT_NAMES}
    T = x.shape[1]
    cx, cy, cc = lax.axis_index("x"), lax.axis_index("y"), lax.axis_index("c")

    win_rows = D_IN // N_CHIPS
    shard2d = {"w_in": (win_rows, D_MODEL), "mla_w_uq": (Q_RANK // N_CHIPS, MLA_HEADS * MLA_QK),
               "w_out": (D_MODEL // N_CHIPS, D_MODEL), "w_gate": (FF_SHARD, D_MODEL), "w_up": (FF_SHARD, D_MODEL),
               "w_down": (FF_SHARD, D_MODEL)}
    transposed = ("w_in", "w_gate", "w_up")
    to2d = lambda n, a: a[0].T if n in transposed else a.reshape(shard2d[n])
    from2d = lambda n, t: t.T[None] if n in transposed else t.reshape(W[n].shape)
    me = 2 * cx + cy
    place = jnp.stack([cc, me]).astype(jnp.int32)
    local_b = [to2d(n, W[n]).astype(BF16) for n in BIG_NAMES]
    local_b[0] = jnp.pad(local_b[0], ((0, FF_SHARD - win_rows), (0, 0)))
    stacks = _gather_chips(local_b[:2], "gather_weights")
    win4, wuq4 = [lax.dynamic_update_slice(s, l[None], (me, 0, 0)) for s, l in zip(stacks, local_b)]
    win_t = win4[:, :win_rows].reshape(D_IN, D_MODEL)
    wuq_full = wuq4.reshape(Q_RANK, MLA_HEADS, MLA_QK)
    win_arr, wq_arr, wk_arr, wv_arr = _arrange_weights(win_t, wuq_full, mla_w_ukv[0].astype(BF16))
    small = {n: W[n][0] if n == "mla_w_ukv" else W[n].reshape(-1, W[n].shape[-1]) for n in SMALL_NAMES}

    loss_local, dx, grads, ffn_final = _local_step(x[0], positions.reshape(T, 1), loss_target[0], small, win_arr,
                                                           wq_arr, wk_arr, wv_arr, local_b[2:], place)

    gs = [grads["w_in"], grads["mla_w_uq"].reshape((N_CHIPS,) + UQ_COMM_SHAPE), grads["w_out"].reshape((N_CHIPS,) + shard2d["w_out"])]
    sm = _pack_small({**grads, "loss": loss_local})
    *rs, ssib = _pair_swap(gs, sm)
    *ps, pair = _pair_sum(place, gs, rs, small=(sm, ssib))
    ffn_names, rest_names = BIG_NAMES[3:], BIG_NAMES[:3]
    g2d = dict(zip(ffn_names, ffn_final))
    adam_in = lambda names_: [(to2d(n, W[n]), g2d[n], to2d(n, M[n]), to2d(n, V[n])) for n in names_]
    updates = dict(zip(ffn_names, _adamw(adam_in(ffn_names), 8, "adamw_ffn")))
    *ris, sm4 = _chip_swap(ps, pair)
    *gfin, smf = _pair_fill(_chip_sum(place, gs, rs, ris), sm4)

    g2d.update({n: gfin[k].reshape((-1,) + shard2d[n][1:]) for k, n in enumerate(rest_names)})
    updates.update(zip(rest_names[:2], _adamw(adam_in(rest_names[:2]), 3, "adamw_w_in")))
    updates.update(zip(rest_names[2:], _adamw(adam_in(rest_names[2:]), 8, "adamw_w_out")))
    G, DW, NM, NV = {}, {}, {}, {}
    for n, outs in updates.items():
        G[n], DW[n], NM[n], NV[n] = (from2d(n, t) for t in outs)
    view2d = lambda n, a: a.reshape(next((r, c) for name, r, c in SMALL_VIEWS if name == n))
    *res, loss_row = _adamw_small(smf, [tuple(view2d(n, t[n]) for t in (W, M, V)) for n in SMALL_NAMES])
    for n, outs in zip(SMALL_NAMES, res):
        G[n], DW[n], NM[n], NV[n] = (t.reshape(W[n].shape) for t in outs)
    loss = loss_row[0, 0]
    return (loss, dx[None], *[G[n] for n in WEIGHT_NAMES], *[DW[n] for n in WEIGHT_NAMES],
            *[NM[n] for n in WEIGHT_NAMES], *[NV[n] for n in WEIGHT_NAMES])
```

```python
import jax
import jax.numpy as jnp
from jax import lax
from jax.experimental import pallas as pl
from jax.experimental.pallas import tpu as pltpu

F32 = jnp.float32
BF16 = jnp.bfloat16
MXU_DTYPE = BF16

D_MODEL = 1024
MLA_HEADS = 8
MLA_NOPE = 64
MLA_ROPE = 32
MLA_V = 64
MLA_QK = MLA_NOPE + MLA_ROPE
Q_RANK = 384
KV_RANK = 128
MLA_WIDTH = MLA_HEADS * MLA_V
HEAD_PAD = 128
HGRN_HEADS = 4
HGRN_DIM = 128
HGRN_WIDTH = HGRN_HEADS * HGRN_DIM
CHUNK = 64
SUB = 16
HGRN_CPI = 4
D_IN = Q_RANK + KV_RANK + MLA_ROPE + 4 * HGRN_WIDTH
D_IN_ARR = Q_RANK + KV_RANK + HEAD_PAD + 4 * HGRN_WIDTH
D_FF = 2816
N_CHIPS = 4
FF_SHARD = D_FF // N_CHIPS
EPS = 1e-6
ROPE_THETA = 10000.0
ATTN_SCALE = MLA_QK ** -0.5
ATTN_SCALE_LOG2 = ATTN_SCALE * 1.4426950408889634
NEG_BIG = -1e30

ADAM_LR = 0.001
ADAM_B1 = 0.9
ADAM_B2 = 0.999
ADAM_EPS = 1e-08
ADAM_WD = 0.01
ADAM_STEP = 10

VMEM_LIMIT = 56 * 1024 * 1024

SMALL_VIEWS = (("attn_pre_norm", 1, 1024), ("mla_q_norm", 1, 384), ("mla_kv_norm", 1, 128), ("mla_w_ukv", 128, 1024),
               ("mla_out_norm", 1, 512), ("hgrn_lb_logits", 2, 512), ("hgrn_out_norm", 1, 512),
               ("attn_post_norm", 1, 1024), ("ffn_pre_norm", 1, 1024), ("ffn_post_norm", 1, 1024), ("loss", 1, 1))
ROW_TILE = 8
SMALL_ROWS = sum(-(-rows // ROW_TILE) * ROW_TILE for _, rows, _ in SMALL_VIEWS)

MESH = pl.DeviceIdType.MESH
ANY = pl.BlockSpec(memory_space=pl.ANY)


def _dot(a, b, dims, exact):
    if exact:
        return lax.dot_general(a.astype(F32), b.astype(F32), (dims, ((), ())), precision=lax.Precision.HIGH,
                               preferred_element_type=F32)
    return lax.dot_general(a.astype(MXU_DTYPE), b.astype(MXU_DTYPE), (dims, ((), ())), preferred_element_type=F32)


def _mm(a, b, exact=False):
    return _dot(a, b, ((1,), (0,)), exact)


def _mm_nt(a, b, exact=False):
    return _dot(a, b, ((1,), (1,)), exact)


def _mm_tn(a, b, exact=False):
    return _dot(a, b, ((0,), (0,)), exact)


def _rms_fwd(x, w):
    r = lax.rsqrt(jnp.mean(x * x, axis=-1, keepdims=True) + EPS)
    xn = x * r
    return xn * w, xn, r


def _rms_bwd(dy, xn, r, w):
    dxn = dy * w
    dx = r * (dxn - xn * jnp.mean(dxn * xn, axis=-1, keepdims=True))
    dw = jnp.sum(dy * xn, axis=0, keepdims=True)
    return dx, dw


def _group_sums(v, gs):
    t, n = v.shape
    lane = lax.broadcasted_iota(jnp.int32, (t, 128), 1)
    out = []
    for p in range(n // 128):
        vb = v[:, 128 * p:128 * (p + 1)]
        if gs == 128:
            out.append(jnp.sum(vb, axis=-1, keepdims=True))
        else:
            out.append(jnp.sum(jnp.where(lane < 64, vb, 0.0), axis=-1, keepdims=True))
            out.append(jnp.sum(jnp.where(lane >= 64, vb, 0.0), axis=-1, keepdims=True))
    return out


def _group_bcast(sums, gs, t):
    lane = lax.broadcasted_iota(jnp.int32, (t, 128), 1)
    if gs == 128:
        return jnp.concatenate([jnp.broadcast_to(s, (t, 128)) for s in sums], axis=-1)
    return jnp.concatenate([jnp.where(lane < 64, sums[2 * p], sums[2 * p + 1]) for p in range(len(sums) // 2)],
                           axis=-1)


def _grms_fwd(x, w, gs):
    t = x.shape[0]
    r = lax.rsqrt(_group_bcast(_group_sums(x * x, gs), gs, t) * (1.0 / gs) + EPS)
    xn = x * r
    return xn * w, xn, r


def _grms_bwd(dy, xn, r, w, gs):
    t = dy.shape[0]
    dxn = dy * w
    dx = r * (dxn - xn * (_group_bcast(_group_sums(dxn * xn, gs), gs, t) * (1.0 / gs)))
    dw = jnp.sum(dy * xn, axis=0, keepdims=True)
    return dx, dw


def _rope_tables(c_tab, s_tab):
    lane = lax.broadcasted_iota(jnp.int32, c_tab.shape, 1)
    first = (lane >= MLA_NOPE) & (lane < MLA_NOPE + MLA_ROPE // 2)
    second = (lane >= MLA_NOPE + MLA_ROPE // 2) & (lane < MLA_QK)
    return c_tab, jnp.where(first, -s_tab, 0.0), jnp.where(second, s_tab, 0.0)


def _rope(v, c, sa, sb):
    return v * c + pltpu.roll(v, HEAD_PAD - MLA_ROPE // 2, 1) * sa + pltpu.roll(v, MLA_ROPE // 2, 1) * sb


def _rope_bwd(d, c, sa, sb):
    return d * c - pltpu.roll(d, HEAD_PAD - MLA_ROPE // 2, 1) * sa - pltpu.roll(d, MLA_ROPE // 2, 1) * sb


def _params(sem, vmem=VMEM_LIMIT):
    return pltpu.CompilerParams(dimension_semantics=sem, vmem_limit_bytes=vmem)


def _in_fwd(x, pos, invf, w_pre, win, qnw, wq, kvnw, wk, wv, tt=512):
    T = x.shape[0]

    def body(x_ref, pos_ref, invf_ref, wpre_ref, win_ref, qnw_ref, wq_ref, kvnw_ref, wk_ref, wv_ref,
             cq_ref, ckv_ref, xph_ref, q_ref, k_ref, v_ref, kt_ref, vt_ref, rc_ref, rs_ref):
        u, _, _ = _rms_fwd(x_ref[...], wpre_ref[...])
        lo = Q_RANK + KV_RANK + HEAD_PAD
        xp = _mm_nt(u, win_ref[:lo, :])
        xph_ref[...] = _mm_nt(u, win_ref[lo:, :])
        cq = xp[:, :Q_RANK]
        ckv = xp[:, Q_RANK:Q_RANK + KV_RANK]
        kr = xp[:, Q_RANK + KV_RANK:]
        cq_ref[...] = cq
        ckv_ref[...] = ckv
        ang = pos_ref[...].astype(F32) * invf_ref[...]
        c_tab = jnp.cos(ang)
        s_tab = jnp.sin(ang)
        rc_ref[...] = c_tab
        rs_ref[...] = s_tab
        c, sa, sb = _rope_tables(c_tab, s_tab)
        qn, _, _ = _rms_fwd(cq, qnw_ref[...])
        q = _mm(qn, wq_ref[...])
        kvn, _, _ = _rms_fwd(ckv, kvnw_ref[...])
        kn = _mm(kvn, wk_ref[...])
        v = _mm(kvn, wv_ref[...])
        v_ref[...] = v.astype(v_ref.dtype)
        vt_ref[...] = v.T.astype(vt_ref.dtype)
        krr = _rope(kr, c, sa, sb)
        for h in range(MLA_HEADS):
            sl = slice(HEAD_PAD * h, HEAD_PAD * (h + 1))
            q_ref[:, sl] = (_rope(q[:, sl], c, sa, sb) * ATTN_SCALE_LOG2).astype(q_ref.dtype)
            kh = kn[:, sl] + krr
            k_ref[:, sl] = kh.astype(k_ref.dtype)
            kt_ref[sl, :] = kh.T.astype(kt_ref.dtype)

    row = lambda w: pl.BlockSpec((tt, w), lambda i: (i, 0))
    full = lambda a: pl.BlockSpec(a.shape, lambda i: (0,) * a.ndim)
    qk_w = MLA_HEADS * HEAD_PAD
    return pl.pallas_call(
        body, name="in_fwd", grid=(T // tt,),
        in_specs=[row(D_MODEL), row(1), full(invf), full(w_pre), full(win), full(qnw), full(wq), full(kvnw),
                  full(wk), full(wv)],
        out_specs=[row(Q_RANK), row(KV_RANK), row(4 * HGRN_WIDTH), row(qk_w), row(qk_w), row(MLA_WIDTH),
                   pl.BlockSpec((qk_w, tt), lambda i: (0, i)), pl.BlockSpec((MLA_WIDTH, tt), lambda i: (0, i)),
                   row(HEAD_PAD), row(HEAD_PAD)],
        out_shape=[jax.ShapeDtypeStruct((T, Q_RANK), F32), jax.ShapeDtypeStruct((T, KV_RANK), F32),
                   jax.ShapeDtypeStruct((T, 4 * HGRN_WIDTH), F32), jax.ShapeDtypeStruct((T, qk_w), MXU_DTYPE),
                   jax.ShapeDtypeStruct((T, qk_w), MXU_DTYPE), jax.ShapeDtypeStruct((T, MLA_WIDTH), MXU_DTYPE),
                   jax.ShapeDtypeStruct((qk_w, T), MXU_DTYPE), jax.ShapeDtypeStruct((MLA_WIDTH, T), MXU_DTYPE),
                   jax.ShapeDtypeStruct((T, HEAD_PAD), F32), jax.ShapeDtypeStruct((T, HEAD_PAD), F32)],
        compiler_params=_params(("arbitrary",)),
    )(x, pos, invf, w_pre, win, qnw, wq, kvnw, wk, wv)


def _attn_fwd_t(qb, kb, vt, gather=(), tq=256, hps=8):
    T = qb.shape[0]
    nq = T // tq
    ng = len(gather)
    steps = (MLA_HEADS // hps) * nq
    pass_on = steps - 3

    def body(q_ref, k_ref, vt_ref, *rest):
        o_ref, lse_ref = rest[ng:ng + 2]
        acc_scr = rest[2 * ng + 2]
        qi = pl.program_id(1)
        step_no = pl.program_id(0) * nq + qi
        if ng:
            gat = _Gather(rest[:ng], rest[ng + 2:2 * ng + 2], *rest[2 * ng + 3:])

            @pl.when(step_no == 0)
            def _():
                for cp in gat.sends():
                    cp.start()

            @pl.when(step_no == pass_on)
            def _():
                for arrival in gat.arrivals():
                    arrival.wait_recv()
                for cp in gat.forwards():
                    cp.start()

        heads = [slice(HEAD_PAD * a, HEAD_PAD * (a + 1)) for a in range(hps)]
        acc_scr[...] = jnp.zeros_like(acc_scr)

        def step(j, carry, masked):
            start = pl.multiple_of(j * tq, tq)
            scores = [_mm_nt(k_ref[pl.ds(start, tq), heads[a]], q_ref[:, heads[a]]) for a in range(hps)]
            new, probs, alphas = [], [], []
            for a in range(hps):
                m, l = carry[a]
                s = scores[a]
                if masked:
                    kk = lax.broadcasted_iota(jnp.int32, (tq, tq), 0)
                    qq = lax.broadcasted_iota(jnp.int32, (tq, tq), 1)
                    s = jnp.where(kk <= qq, s, NEG_BIG)
                m_new = jnp.maximum(m, jnp.max(s, axis=0, keepdims=True))
                alpha = jnp.exp2(m - m_new)
                p = jnp.exp2(s - m_new)
                l = l * alpha + jnp.sum(p, axis=0, keepdims=True)
                new.append((m_new, l))
                probs.append(p.astype(MXU_DTYPE))
                alphas.append(alpha)
                if a % 2:
                    pr = a // 2
                    vtj = vt_ref[2 * MLA_V * pr:2 * MLA_V * (pr + 1), pl.ds(start, tq)]
                    none = jnp.zeros((MLA_V, tq), vtj.dtype)
                    pv = (_mm(jnp.concatenate([vtj[:MLA_V], none], axis=0), probs[a - 1])
                          + _mm(jnp.concatenate([none, vtj[MLA_V:]], axis=0), probs[a]))
                    acc_scr[pr] = acc_scr[pr] * jnp.where(row < MLA_V, alphas[a - 1], alphas[a]) + pv
            return tuple(new)

        row = lax.broadcasted_iota(jnp.int32, (2 * MLA_V, tq), 0)
        init = tuple((jnp.full((1, tq), NEG_BIG, F32), jnp.zeros((1, tq), F32)) for _ in range(hps))
        carry = lax.fori_loop(0, qi, lambda j, c: step(j, c, False), init)
        carry = step(qi, carry, True)
        for pr in range(hps // 2):
            (m0, l0), (m1, l1) = carry[2 * pr], carry[2 * pr + 1]
            ot = acc_scr[pr] / jnp.where(row < MLA_V, l0, l1)
            o_ref[:, 2 * MLA_V * pr:2 * MLA_V * (pr + 1)] = ot.T
            lse_ref[pr, 0:1, :] = m0 + jnp.log2(l0)
            lse_ref[pr, 1:2, :] = m1 + jnp.log2(l1)

        if ng:
            @pl.when(step_no == steps - 1)
            def _():
                for arrival in gat.forward_arrivals():
                    arrival.wait_recv()
                for cp in gat.sends() + gat.forwards():
                    cp.wait_send()

    return pl.pallas_call(
        body, name="attn_fwd", grid=(MLA_HEADS // hps, nq),
        in_specs=[pl.BlockSpec((tq, hps * HEAD_PAD), lambda g, i: (i, g)),
                  pl.BlockSpec((T, hps * HEAD_PAD), lambda g, i: (0, g)),
                  pl.BlockSpec((hps * MLA_V, T), lambda g, i: (g, 0))] + [ANY] * ng,
        out_specs=[pl.BlockSpec((tq, hps * MLA_V), lambda g, i: (i, g)),
                   pl.BlockSpec((hps // 2, 2, tq), lambda g, i: (g, 0, i))] + [ANY] * ng,
        out_shape=[jax.ShapeDtypeStruct((T, MLA_WIDTH), F32), jax.ShapeDtypeStruct((MLA_HEADS // 2, 2, T), F32)]
        + _Gather.out_shapes(gather),
        scratch_shapes=[pltpu.VMEM((hps // 2, 2 * MLA_V, tq), F32)] + (_Gather.semaphores(gather) if ng else []),
        compiler_params=_params(("arbitrary", "arbitrary")),
    )(qb, kb, vt, *gather)


def _attn_bwd_t(qb, kb, kt, vb, dob, lse, dvec, send=(), tq=512, hps=4):
    T = qb.shape[0]
    nq = T // tq
    ns = len(send)
    steps = (MLA_HEADS // hps) * nq

    def body(q_ref, k_ref, kt_ref, v_ref, do_ref, lse_ref, d_ref, *rest):
        dqt_ref, dk_ref, dv_ref = rest[ns:ns + 3]
        va_scr, dv_scr = rest[2 * ns + 3:2 * ns + 5]
        j = pl.program_id(1)
        step_no = pl.program_id(0) * nq + j
        if ns:
            @pl.when(step_no == 0)
            def _():
                for cp in _chip_swap_copies(rest[:ns], rest[ns + 3:2 * ns + 3], *rest[2 * ns + 5:]):
                    cp.start()

        @pl.when(j == 0)
        def _():
            dqt_ref[...] = jnp.zeros_like(dqt_ref)

        lane = lax.broadcasted_iota(jnp.int32, (tq, 2 * MLA_V), 1)
        heads = [slice(HEAD_PAD * a, HEAD_PAD * (a + 1)) for a in range(hps)]
        pairs = [slice(2 * MLA_V * p, 2 * MLA_V * (p + 1)) for p in range(hps // 2)]
        for pr in range(hps // 2):
            vpair = v_ref[:, pairs[pr]]
            va_scr[2 * pr] = jnp.where(lane < MLA_V, vpair, jnp.zeros_like(vpair))
            va_scr[2 * pr + 1] = jnp.where(lane >= MLA_V, vpair, jnp.zeros_like(vpair))
        dk_ref[...] = jnp.zeros_like(dk_ref)
        dv_scr[...] = jnp.zeros_like(dv_scr)

        def step(i, masked):
            start = pl.multiple_of(i * tq, tq)
            rows = pl.ds(start, tq)
            scores = [_mm_nt(k_ref[:, heads[a]], q_ref[rows, heads[a]]) for a in range(hps)]
            dps = [_mm_nt(va_scr[a], do_ref[rows, pairs[a // 2]]) for a in range(hps)]
            for a in range(hps):
                pr, r = a // 2, a % 2
                p = jnp.exp2(scores[a] - lse_ref[pr, r:r + 1, rows])
                if masked:
                    kk = lax.broadcasted_iota(jnp.int32, (tq, tq), 0)
                    qq = lax.broadcasted_iota(jnp.int32, (tq, tq), 1)
                    p = jnp.where(kk <= qq, p, 0.0)
                ds = p * (dps[a] - d_ref[pr, r:r + 1, rows])
                dv_scr[a] += _mm(p, do_ref[rows, pairs[pr]])
                dk_ref[:, heads[a]] += _mm(ds, q_ref[rows, heads[a]])
                dqt_ref[heads[a], rows] += _mm(kt_ref[heads[a], :], ds)

        def loop_body(i, _):
            step(i, False)
            return 0

        step(j, True)
        lax.fori_loop(j + 1, nq, loop_body, 0)
        for pr in range(hps // 2):
            dv_ref[:, pairs[pr]] = jnp.where(lane < MLA_V, dv_scr[2 * pr], dv_scr[2 * pr + 1])
        dk_ref[...] = dk_ref[...] * (ATTN_SCALE / ATTN_SCALE_LOG2)

        if ns:
            @pl.when(step_no == steps - 1)
            def _():
                for cp in _chip_swap_copies(rest[:ns], rest[ns + 3:2 * ns + 3], *rest[2 * ns + 5:]):
                    cp.wait()

    stat = pl.BlockSpec((hps // 2, 2, T), lambda g, j: (g, 0, 0))
    return pl.pallas_call(
        body, name="attn_bwd", grid=(MLA_HEADS // hps, nq),
        in_specs=[pl.BlockSpec((T, hps * HEAD_PAD), lambda g, j: (0, g)),
                  pl.BlockSpec((tq, hps * HEAD_PAD), lambda g, j: (j, g)),
                  pl.BlockSpec((hps * HEAD_PAD, tq), lambda g, j: (g, j)),
                  pl.BlockSpec((tq, hps * MLA_V), lambda g, j: (j, g)),
                  pl.BlockSpec((T, hps * MLA_V), lambda g, j: (0, g)), stat, stat] + [ANY] * ns,
        out_specs=[pl.BlockSpec((hps * HEAD_PAD, T), lambda g, j: (g, 0)),
                   pl.BlockSpec((tq, hps * HEAD_PAD), lambda g, j: (j, g)),
                   pl.BlockSpec((tq, hps * MLA_V), lambda g, j: (j, g))] + [ANY] * ns,
        out_shape=[jax.ShapeDtypeStruct((MLA_HEADS * HEAD_PAD, T), F32),
                   jax.ShapeDtypeStruct((T, MLA_HEADS * HEAD_PAD), F32),
                   jax.ShapeDtypeStruct((T, MLA_WIDTH), F32)] + _chip_swap_shapes(send),
        scratch_shapes=[pltpu.VMEM((hps, tq, 2 * MLA_V), vb.dtype), pltpu.VMEM((hps, tq, 2 * MLA_V), F32)]
        + ([pltpu.SemaphoreType.DMA((3 * ns,)), pltpu.SemaphoreType.DMA((3 * ns,))] if ns else []),
        compiler_params=_params(("arbitrary", "arbitrary")),
    )(qb, kb, kt, vb, dob, lse, dvec, *send)


def _cumsum_rows(x):
    n = x.shape[0]
    row = lax.broadcasted_iota(jnp.int32, x.shape, 0)
    s = 1
    while s < n:
        x = x + jnp.where(row >= s, pltpu.roll(x, s, 0), 0.0)
        s *= 2
    return x


def _rev_cumsum_rows(x):
    n = x.shape[0]
    row = lax.broadcasted_iota(jnp.int32, x.shape, 0)
    s = 1
    while s < n:
        x = x + jnp.where(row < n - s, pltpu.roll(x, n - s, 0), 0.0)
        s *= 2
    return x


def _lb_from_logits(l):
    l0, l1 = l[0:1, :], l[1:2, :]
    m = jnp.maximum(l0, l1)
    e0, e1 = jnp.exp(l0 - m), jnp.exp(l1 - m)
    return e0 / (e0 + e1)


def _hgrn_gates(hq, hf, lb):
    sig_f = jax.nn.sigmoid(hf)
    f = lb + (1.0 - lb) * sig_f
    sig_q = jax.nn.sigmoid(hq)
    return sig_f, f, jnp.log(f), 1.0 - f, sig_q, hq * sig_q


def _hgrn_intra(q, kk, b, exact=False):
    row = lax.broadcasted_iota(jnp.int32, b.shape, 0)
    qs, ks, eqs, eks, a_rows = [], [], [], [], []
    for i in range(CHUNK // SUB):
        ref = b[SUB * i + SUB // 2:SUB * i + SUB // 2 + 1, :]
        eq = jnp.exp(b[SUB * i:SUB * (i + 1), :] - ref)
        ek = jnp.exp(jnp.where(row < SUB * (i + 1), ref - b, NEG_BIG))
        qi = q[SUB * i:SUB * (i + 1), :] * eq
        ki = kk * ek
        a_rows.append(_mm_nt(qi, ki, exact))
        qs.append(qi), ks.append(ki), eqs.append(eq), eks.append(ek)
    tt = lax.broadcasted_iota(jnp.int32, (CHUNK, CHUNK), 0)
    ss = lax.broadcasted_iota(jnp.int32, (CHUNK, CHUNK), 1)
    causal = ss <= tt
    a = jnp.where(causal, jnp.concatenate(a_rows, axis=0), 0.0)
    return a, causal, qs, ks, eqs, eks


def _hgrn_fwd(xph, lbl, tg=512):
    T = xph.shape[0]
    ng, ncg = T // tg, tg // CHUNK
    cols = [slice(HGRN_DIM * h, HGRN_DIM * (h + 1)) for h in range(HGRN_HEADS)]

    def body(lbl_ref, hq_ref, hf_ref, hi_ref, o_ref, st_ref, s_scr):
        @pl.when(pl.program_id(0) == 0)
        def _():
            s_scr[...] = jnp.zeros_like(s_scr)

        lb = _lb_from_logits(lbl_ref[...])

        def chunks(it, _):
            pre = []
            for k in range(HGRN_CPI):
                c = it * HGRN_CPI + k
                rows = pl.ds(pl.multiple_of(c * CHUNK, CHUNK), CHUNK)
                for cs in cols:
                    _, _, lf, kk, _, q = _hgrn_gates(hq_ref[rows, cs], hf_ref[rows, cs], lb[:, cs])
                    v = hi_ref[rows, cs]
                    b = _cumsum_rows(lf)
                    a = _hgrn_intra(q, kk, b)[0]
                    b_last = b[CHUNK - 1:CHUNK, :]
                    pre.append((c, rows, q * jnp.exp(b), a, v, jnp.exp(b_last), _mm_tn(v, kk * jnp.exp(b_last - b))))
            for i, (c, rows, qe, a, v, ebl, upd) in enumerate(pre):
                h = i % HGRN_HEADS
                st = s_scr[h]
                st_ref[h, c] = st
                o_ref[rows, cols[h]] = _mm_nt(qe, st) + _mm(a, v)
                s_scr[h] = st * ebl + upd
            return 0

        lax.fori_loop(0, ncg // HGRN_CPI, chunks, 0)

    col = lambda k: pl.BlockSpec((tg, HGRN_WIDTH), lambda g: (g, k))
    return pl.pallas_call(
        body, name="hgrn_fwd", grid=(ng,),
        in_specs=[pl.BlockSpec((2, HGRN_WIDTH), lambda g: (0, 0)), col(0), col(1), col(2)],
        out_specs=[col(0), pl.BlockSpec((HGRN_HEADS, ncg, HGRN_DIM, HGRN_DIM), lambda g: (0, g, 0, 0))],
        out_shape=[jax.ShapeDtypeStruct((T, HGRN_WIDTH), F32),
                   jax.ShapeDtypeStruct((HGRN_HEADS, T // CHUNK, HGRN_DIM, HGRN_DIM), F32)],
        scratch_shapes=[pltpu.VMEM((HGRN_HEADS, HGRN_DIM, HGRN_DIM), F32)],
        compiler_params=_params(("arbitrary",)),
    )(lbl, xph, xph, xph)


def _hgrn_bwd(xph, lbl, states, d_o, fill=(), tg=512):
    T = xph.shape[0]
    ng, ncg = T // tg, tg // CHUNK
    cols = [slice(HGRN_DIM * h, HGRN_DIM * (h + 1)) for h in range(HGRN_HEADS)]
    nsub = CHUNK // SUB
    nf = len(fill)

    def body(lbl_ref, hq_ref, hf_ref, hi_ref, st_ref, do_ref, *rest):
        dhq_ref, dhf_ref, dhi_ref, dlg_ref = rest[nf:nf + 4]
        ds_scr, dlb_scr = rest[2 * nf + 4:2 * nf + 6]
        fill_copies = lambda: _pair_fill_copies(rest[nf + 4:2 * nf + 4], *rest[2 * nf + 6:])
        g = pl.program_id(0)

        @pl.when(g == 0)
        def _():
            ds_scr[...] = jnp.zeros_like(ds_scr)
            dlb_scr[...] = jnp.zeros_like(dlb_scr)
            for cp in (fill_copies()[0] if nf else ()):
                cp.start()

        lb = _lb_from_logits(lbl_ref[...])

        def chunks(it, _):
            pre = []
            for k, h in ((k, h) for k in range(HGRN_CPI) for h in range(HGRN_HEADS)):
                cs = cols[h]
                c = ncg - 1 - (it * HGRN_CPI + k)
                rows = pl.ds(pl.multiple_of(c * CHUNK, CHUNK), CHUNK)
                hq = hq_ref[rows, cs]
                sig_f, f, lf, kk, sig_q, q = _hgrn_gates(hq, hf_ref[rows, cs], lb[:, cs])
                v = hi_ref[rows, cs]
                do = do_ref[rows, cs]
                b = _cumsum_rows(lf)
                eb = jnp.exp(b)
                a, causal, qs, ks, eqs, eks = _hgrn_intra(q, kk, b)
                b_last = b[CHUNK - 1:CHUNK, :]
                st = st_ref[h, c]
                pre.append(dict(h=h, cs=cs, rows=rows, hq=hq, sig_f=sig_f, f=f, kk=kk, sig_q=sig_q, q=q, v=v, eb=eb, qs=qs,
                                ks=ks, eqs=eqs,
                                eks=eks, ebl=jnp.exp(b_last), el=jnp.exp(b_last - b), st=st,
                                da=jnp.where(causal, _mm_nt(do, v, True), 0.0), dq=_mm(do, st, True) * eb,
                                dv=_mm_tn(a, do), dsu=_mm_tn(do, q * eb, True)))
            for w in pre:
                dq_rows = []
                dk = jnp.zeros_like(w["q"])
                for i in range(nsub):
                    dai = w["da"][SUB * i:SUB * (i + 1), :]
                    dq_rows.append(_mm(dai, w["ks"][i], True) * w["eqs"][i])
                    dk = dk + _mm_tn(dai, w["qs"][i], True) * w["eks"][i]
                w["dq"] = w["dq"] + jnp.concatenate(dq_rows, axis=0)
                w["dk"] = dk
            for w in pre:
                h, cs, rows = w["h"], w["cs"], w["rows"]
                kk, el, ebl, dst = w["kk"], w["el"], w["ebl"], ds_scr[h]
                dk_state = _mm(w["v"], dst, True) * el
                dk = w["dk"] + dk_state
                e_last = (ebl * jnp.sum(w["st"] * dst, axis=0, keepdims=True)
                          + jnp.sum(kk * dk_state, axis=0, keepdims=True))
                dlf = _rev_cumsum_rows(w["q"] * w["dq"] - kk * dk) + e_last
                ds_scr[h] = dst * ebl + w["dsu"]
                df = dlf / w["f"] - dk
                sig_f, sig_q = w["sig_f"], w["sig_q"]
                dhf_ref[rows, cs] = df * (1.0 - lb[:, cs]) * sig_f * (1.0 - sig_f)
                dlb_scr[:, cs] += jnp.sum(df * (1.0 - sig_f), axis=0, keepdims=True)
                dhq_ref[rows, cs] = w["dq"] * sig_q * (1.0 + w["hq"] * (1.0 - sig_q))
                dhi_ref[rows, cs] = w["dv"] + _mm_nt(kk * el, dst)
            return 0

        lax.fori_loop(0, ncg // HGRN_CPI, chunks, 0)

        @pl.when(g == ng - 1)
        def _():
            dl0 = dlb_scr[...] * lb * (1.0 - lb)
            dlg_ref[...] = jnp.concatenate([dl0, -dl0], axis=0)
            if nf:
                copies, waits = fill_copies()
                for w in waits:
                    w.wait_recv()
                for cp in copies:
                    cp.wait_send()

    col = lambda k: pl.BlockSpec((tg, HGRN_WIDTH), lambda g: (ng - 1 - g, k))
    logits = pl.BlockSpec((2, HGRN_WIDTH), lambda g: (0, 0))
    big = jax.ShapeDtypeStruct((T, HGRN_WIDTH), F32)
    n_in, n_out = 6, 4
    return pl.pallas_call(
        body, name="hgrn_bwd", grid=(ng,),
        in_specs=[logits, col(0), col(1), col(2),
                  pl.BlockSpec((HGRN_HEADS, ncg, HGRN_DIM, HGRN_DIM), lambda g: (0, ng - 1 - g, 0, 0)), col(0)] + [ANY] * nf,
        out_specs=[col(0), col(0), col(0), logits] + [ANY] * nf,
        out_shape=[big, big, big, jax.ShapeDtypeStruct((2, HGRN_WIDTH), F32)]
        + [jax.ShapeDtypeStruct(f.shape, f.dtype) for f in fill],
        input_output_aliases={n_in + k: n_out + k for k in range(nf)},
        scratch_shapes=[pltpu.VMEM((HGRN_HEADS, HGRN_DIM, HGRN_DIM), F32), pltpu.VMEM((1, HGRN_WIDTH), F32)]
        + ([pltpu.SemaphoreType.DMA((nf,)), pltpu.SemaphoreType.DMA((nf,))] if nf else []),
        compiler_params=_params(("arbitrary",)),
    )(lbl, xph, xph, xph, states, d_o, *fill)


def _proj_fwd(x, o_raw, oh_raw, xph, wout, w_mla, w_hg, w_post, w_fpre, tt=512):
    T = x.shape[0]

    def body(x_ref, o_ref, oh_ref, hg_ref, wout_ref, wmla_ref, whg_ref, wpost_ref, wfpre_ref,
             h1_ref, y1_ref, z_ref, mix_ref):
        om, _, _ = _grms_fwd(o_ref[...], wmla_ref[...], MLA_V)
        hg = hg_ref[...]
        ohn, _, _ = _grms_fwd(oh_ref[...], whg_ref[...], HGRN_DIM)
        mix = jnp.concatenate([om, ohn * (hg * jax.nn.sigmoid(hg))], axis=-1)
        mix_ref[...] = mix.astype(mix_ref.dtype)
        y1 = _mm(mix, wout_ref[...])
        y1_ref[...] = y1
        h1 = x_ref[...] + _rms_fwd(y1, wpost_ref[...])[0]
        h1_ref[...] = h1
        z_ref[...] = _rms_fwd(h1, wfpre_ref[...])[0].astype(z_ref.dtype)

    row = lambda w: pl.BlockSpec((tt, w), lambda i: (i, 0))
    full = lambda a: pl.BlockSpec(a.shape, lambda i: (0,) * a.ndim)
    sds = jax.ShapeDtypeStruct
    return pl.pallas_call(
        body, name="proj_fwd", grid=(T // tt,),
        in_specs=[row(D_MODEL), row(MLA_WIDTH), row(HGRN_WIDTH), pl.BlockSpec((tt, HGRN_WIDTH), lambda i: (i, 3)),
                  full(wout), full(w_mla), full(w_hg), full(w_post), full(w_fpre)],
        out_specs=[row(D_MODEL)] * 4,
        out_shape=[sds((T, D_MODEL), F32), sds((T, D_MODEL), F32), sds((T, D_MODEL), MXU_DTYPE),
                   sds((T, D_MODEL), MXU_DTYPE)],
        compiler_params=_params(("arbitrary",)),
    )(x, o_raw, oh_raw, xph, wout, w_mla, w_hg, w_post, w_fpre)


def _ffn_fwd(zb, h1, tgt, w_fpost, wg, wu, wd, tt=256):
    T = zb.shape[0]
    nj = N_CHIPS

    def body(z_ref, h1_ref, tgt_ref, wfpost_ref, wg_ref, wu_ref, wd_ref, g_ref, up_ref, dy2_ref, dh2_ref, loss_ref, dwf_ref):
        @pl.when(pl.program_id(0) == 0)
        def _():
            loss_ref[...] = jnp.zeros_like(loss_ref)
            dwf_ref[...] = jnp.zeros_like(dwf_ref)

        z = z_ref[...]
        gs = [_mm_nt(z, wg_ref[j]) for j in range(nj)]
        ups = [_mm_nt(z, wu_ref[j]) for j in range(nj)]
        y2 = jnp.zeros((tt, D_MODEL), F32)
        for j in range(nj):
            g_ref[j] = gs[j]
            up_ref[j] = ups[j]
            y2 = y2 + _mm(gs[j] * jax.nn.sigmoid(gs[j]) * ups[j], wd_ref[j])
        w = wfpost_ref[...]
        y2s, y2n, r2 = _rms_fwd(y2, w)
        e = h1_ref[...] + y2s - tgt_ref[...]
        loss_ref[...] += jnp.sum(e * e, axis=0, keepdims=True)
        dh2 = e * (1.0 / D_MODEL)
        dh2_ref[...] = dh2
        dy2, dwf = _rms_bwd(dh2, y2n, r2, w)
        dy2_ref[...] = dy2.astype(dy2_ref.dtype)
        dwf_ref[...] += dwf

    row = pl.BlockSpec((tt, D_MODEL), lambda i: (i, 0))
    vec = pl.BlockSpec((1, D_MODEL), lambda i: (0, 0))
    resident = pl.BlockSpec((nj, FF_SHARD, D_MODEL), lambda i: (0, 0, 0), pipeline_mode=pl.Buffered(1))
    act = pl.BlockSpec((nj, tt, FF_SHARD), lambda i: (0, i, 0))
    sds = jax.ShapeDtypeStruct
    return pl.pallas_call(
        body, name="ffn_fwd", grid=(T // tt,),
        in_specs=[row, row, row, vec, resident, resident, resident],
        out_specs=[act, act, row, row, vec, vec],
        out_shape=[sds((nj, T, FF_SHARD), F32), sds((nj, T, FF_SHARD), F32), sds((T, D_MODEL), MXU_DTYPE),
                   sds((T, D_MODEL), F32), sds((1, D_MODEL), F32), sds((1, D_MODEL), F32)],
        compiler_params=_params(("arbitrary",)),
    )(zb, h1, tgt, w_fpost, wg, wu, wd)


def _ffn_bwd(zb, g, up, dy2b, wg, wu, wd, tt=512):
    T = zb.shape[0]
    nj = N_CHIPS

    def body(z_ref, g_ref, up_ref, dy2_ref, wg_ref, wu_ref, wd_ref, dwg_ref, dwu_ref, dwd_ref, dz_ref):
        @pl.when(pl.program_id(1) == 0)
        def _():
            dwg_ref[...] = jnp.zeros_like(dwg_ref)
            dwu_ref[...] = jnp.zeros_like(dwu_ref)
            dwd_ref[...] = jnp.zeros_like(dwd_ref)

        z, g_, up_, dy2 = z_ref[...], g_ref[0], up_ref[0], dy2_ref[...]
        sg = jax.nn.sigmoid(g_)
        act = g_ * sg
        dff = _mm_nt(dy2, wd_ref[0])
        dwd_ref[0] += _mm_tn(act * up_, dy2)
        dg = dff * up_ * sg * (1.0 + g_ * (1.0 - sg))
        dup = dff * act
        dwg_ref[0] += _mm_tn(dg, z)
        dwu_ref[0] += _mm_tn(dup, z)
        dz_ref[0] = _mm(dg, wg_ref[0]) + _mm(dup, wu_ref[0])

    row = pl.BlockSpec((tt, D_MODEL), lambda j, i: (i, 0))
    act = pl.BlockSpec((1, tt, FF_SHARD), lambda j, i: (j, i, 0))
    w_sh = pl.BlockSpec((1, FF_SHARD, D_MODEL), lambda j, i: (j, 0, 0))
    w_grad = jax.ShapeDtypeStruct((nj, FF_SHARD, D_MODEL), F32)
    return pl.pallas_call(
        body, name="ffn_bwd", grid=(nj, T // tt),
        in_specs=[row, act, act, row, w_sh, w_sh, w_sh],
        out_specs=[w_sh, w_sh, w_sh, pl.BlockSpec((1, tt, D_MODEL), lambda j, i: (j, i, 0))],
        out_shape=[w_grad, w_grad, w_grad, jax.ShapeDtypeStruct((nj, T, D_MODEL), F32)],
        compiler_params=_params(("arbitrary", "arbitrary")),
    )(zb, g, up, dy2b, wg, wu, wd)


def _mid_bwd(dzp, dh2, h1, y1, mixb, o_raw, oh_raw, xph, wout, w_fpre, w_post, w_mla, w_hg, swap=(), tt=256):
    T = dh2.shape[0]
    nsw = len(swap)
    n_in, n_out = 13, 10

    def body(*refs):
        (dzp_ref, dh2_ref, h1_ref, y1_ref, mix_ref, o_ref, oh_ref, hg_ref, wout_ref, wfpre_ref, wpost_ref,
         wmla_ref, whg_ref) = refs[:n_in]
        (dh1_ref, dwout_ref, do_ref, doh_ref, dhg_ref, dvec_ref, dwfpre_ref, dwpost_ref, dwmla_ref,
         dwhg_ref) = refs[n_in + nsw:n_in + nsw + n_out]
        swap_copies = lambda: _pair_swap_copies(refs[n_in:n_in + nsw], refs[n_in + nsw + n_out:n_in + 2 * nsw + n_out],
                                                *refs[n_in + 2 * nsw + n_out:])

        @pl.when(pl.program_id(0) == 0)
        def _():
            for r in (dwout_ref, dwfpre_ref, dwpost_ref, dwmla_ref, dwhg_ref):
                r[...] = jnp.zeros_like(r)
            for cp in (swap_copies() if nsw else ()):
                cp.start()

        dz = dzp_ref[0] + dzp_ref[1] + dzp_ref[2] + dzp_ref[3]
        wfpre = wfpre_ref[...]
        _, h1n, r = _rms_fwd(h1_ref[...], wfpre)
        dh1_z, dwfpre = _rms_bwd(dz, h1n, r, wfpre)
        dwfpre_ref[...] += dwfpre
        dh1 = dh2_ref[...] + dh1_z
        dh1_ref[...] = dh1
        wpost = wpost_ref[...]
        _, y1n, r1 = _rms_fwd(y1_ref[...], wpost)
        dy1, dwpost = _rms_bwd(dh1, y1n, r1, wpost)
        dwpost_ref[...] += dwpost
        dmix = _mm_nt(dy1, wout_ref[...])
        dwout_ref[...] += _mm_tn(mix_ref[...], dy1)
        wmla = wmla_ref[...]
        o = o_ref[...]
        _, on, ro = _grms_fwd(o, wmla, MLA_V)
        d_o, dwmla = _grms_bwd(dmix[:, :MLA_WIDTH], on, ro, wmla, MLA_V)
        dwmla_ref[...] += dwmla
        do_ref[...] = d_o.astype(do_ref.dtype)
        hh = lax.broadcasted_iota(jnp.int32, (MLA_HEADS, MLA_WIDTH), 0)
        ll = lax.broadcasted_iota(jnp.int32, (MLA_HEADS, MLA_WIDTH), 1)
        sel = jnp.where((ll >= hh * MLA_V) & (ll < (hh + 1) * MLA_V), 1.0, 0.0)
        dvec_ref[...] = _mm_nt(sel, d_o * o, True)
        whg = whg_ref[...]
        hg = hg_ref[...]
        sg = jax.nn.sigmoid(hg)
        _, ohn, rh = _grms_fwd(oh_ref[...], whg, HGRN_DIM)
        dmh = dmix[:, MLA_WIDTH:]
        dhg_ref[...] = dmh * ohn * whg * sg * (1.0 + hg * (1.0 - sg))
        d_oh, dwhg = _grms_bwd(dmh * (hg * sg), ohn, rh, whg, HGRN_DIM)
        dwhg_ref[...] += dwhg
        doh_ref[...] = d_oh

        if nsw:
            @pl.when(pl.program_id(0) == T // tt - 1)
            def _():
                for cp in swap_copies():
                    cp.wait()

    row = lambda w: pl.BlockSpec((tt, w), lambda i: (i, 0))
    full = lambda a: pl.BlockSpec(a.shape, lambda i: (0,) * a.ndim)
    vec = lambda w: pl.BlockSpec((1, w), lambda i: (0, 0))
    sds = jax.ShapeDtypeStruct
    return pl.pallas_call(
        body, name="mid_bwd", grid=(T // tt,),
        in_specs=[pl.BlockSpec((N_CHIPS, tt, D_MODEL), lambda i: (0, i, 0)), row(D_MODEL), row(D_MODEL), row(D_MODEL),
                  row(D_MODEL), row(MLA_WIDTH), row(HGRN_WIDTH), pl.BlockSpec((tt, HGRN_WIDTH), lambda i: (i, 3)),
                  full(wout), vec(D_MODEL), vec(D_MODEL), vec(MLA_WIDTH), vec(HGRN_WIDTH)] + [ANY] * nsw,
        out_specs=[row(D_MODEL), full(wout), row(MLA_WIDTH), row(HGRN_WIDTH), row(HGRN_WIDTH),
                   pl.BlockSpec((MLA_HEADS, tt), lambda i: (0, i)),
                   vec(D_MODEL), vec(D_MODEL), vec(MLA_WIDTH), vec(HGRN_WIDTH)] + [ANY] * nsw,
        out_shape=[sds((T, D_MODEL), F32), sds(wout.shape, F32), sds((T, MLA_WIDTH), MXU_DTYPE), sds((T, HGRN_WIDTH), F32),
                   sds((T, HGRN_WIDTH), F32), sds((MLA_HEADS, T), F32),
                   sds((1, D_MODEL), F32), sds((1, D_MODEL), F32), sds((1, MLA_WIDTH), F32), sds((1, HGRN_WIDTH), F32)]
        + _half_stack_shapes(swap),
        scratch_shapes=[pltpu.SemaphoreType.DMA((nsw,)), pltpu.SemaphoreType.DMA((nsw,))] if nsw else [],
        compiler_params=_params(("arbitrary",)),
    )(dzp, dh2, h1, y1, mixb, o_raw, oh_raw, xph, wout, w_fpre, w_post, w_mla, w_hg, *swap)


def _in_bwd(x, dh1, cq, ckv, dq, dk, dv, dhq, dhf, dhi, dhg, rc, rs, w_pre, win, qnw, wq, kvnw, wk, wv, tt=256):
    T = x.shape[0]

    def body(x_ref, dh1_ref, cq_ref, ckv_ref, dq_ref, dk_ref, dv_ref, dhq_ref, dhf_ref, dhi_ref, dhg_ref, rc_ref, rs_ref,
             wpre_ref, win_ref, qnw_ref, wq_ref, kvnw_ref, wk_ref, wv_ref,
             dx_ref, dwin_ref, dwq_ref, dwk_ref, dwv_ref, dwpre_ref, dqnw_ref, dkvnw_ref):
        @pl.when(pl.program_id(0) == 0)
        def _():
            for r in (dwin_ref, dwq_ref, dwk_ref, dwv_ref, dwpre_ref, dqnw_ref, dkvnw_ref):
                r[...] = jnp.zeros_like(r)

        def add_win_grad(r, first):
            for arr0, n, chip, row0 in _win_grad_segments():
                if first <= arr0 and arr0 + n <= first + r.shape[0]:
                    dwin_ref[chip, row0:row0 + n, :] += r[arr0 - first:arr0 - first + n]

        lo = Q_RANK + KV_RANK + HEAD_PAD
        dxp_h = jnp.concatenate([dhq_ref[...], dhf_ref[...], dhi_ref[...], dhg_ref[...]], axis=-1)
        du = _mm(dxp_h, win_ref[lo:, :])
        wpre = wpre_ref[...]
        u, xn, rx = _rms_fwd(x_ref[...], wpre)
        add_win_grad(_mm_tn(dxp_h, u), lo)
        c, sa, sb = _rope_tables(rc_ref[...], rs_ref[...])
        lane = lax.broadcasted_iota(jnp.int32, (tt, HEAD_PAD), 1)
        dk_all = dk_ref[...]
        dq_lin = []
        dkr = jnp.zeros((tt, HEAD_PAD), F32)
        for h in range(MLA_HEADS):
            sl = slice(HEAD_PAD * h, HEAD_PAD * (h + 1))
            dq_lin.append(_rope_bwd(dq_ref[sl, :].T * ATTN_SCALE, c, sa, sb))
            dkr = dkr + dk_all[:, sl]
        dq_lin = jnp.concatenate(dq_lin, axis=-1)
        dkr = jnp.where((lane >= MLA_NOPE) & (lane < MLA_QK), _rope_bwd(dkr, c, sa, sb), 0.0)
        qnw = qnw_ref[...]
        qn, cqn, rq = _rms_fwd(cq_ref[...], qnw)
        dwq_ref[...] += _mm_tn(qn, dq_lin)
        dcq, dqnw = _rms_bwd(_mm_nt(dq_lin, wq_ref[...]), cqn, rq, qnw)
        dqnw_ref[...] += dqnw
        kvnw = kvnw_ref[...]
        kvn, ckvn, rkv = _rms_fwd(ckv_ref[...], kvnw)
        dv_ = dv_ref[...]
        dwk_ref[...] += _mm_tn(kvn, dk_all)
        dwv_ref[...] += _mm_tn(kvn, dv_)
        dckv, dkvnw = _rms_bwd(_mm_nt(dk_all, wk_ref[...]) + _mm_nt(dv_, wv_ref[...]), ckvn, rkv, kvnw)
        dkvnw_ref[...] += dkvnw
        dxp_a = jnp.concatenate([dcq, dckv, dkr], axis=-1)
        add_win_grad(_mm_tn(dxp_a, u), 0)
        dx_u, dwpre = _rms_bwd(du + _mm(dxp_a, win_ref[:lo, :]), xn, rx, wpre)
        dwpre_ref[...] += dwpre
        dx_ref[...] = dh1_ref[...] + dx_u

    row = lambda w: pl.BlockSpec((tt, w), lambda i: (i, 0))
    full = lambda a: pl.BlockSpec(a.shape, lambda i: (0,) * a.ndim)
    sds = jax.ShapeDtypeStruct
    qk_w = MLA_HEADS * HEAD_PAD
    return pl.pallas_call(
        body, name="in_bwd", grid=(T // tt,),
        in_specs=[row(D_MODEL), row(D_MODEL), row(Q_RANK), row(KV_RANK), pl.BlockSpec((qk_w, tt), lambda i: (0, i)),
                  row(qk_w), row(MLA_WIDTH),
                  row(HGRN_WIDTH), row(HGRN_WIDTH), row(HGRN_WIDTH), row(HGRN_WIDTH), row(HEAD_PAD), row(HEAD_PAD),
                  full(w_pre), full(win), full(qnw), full(wq), full(kvnw), full(wk), full(wv)],
        out_specs=[row(D_MODEL), pl.BlockSpec(WIN_COMM_SHAPE, lambda i: (0, 0, 0)), full(wq), full(wk), full(wv),
                   full(w_pre), full(qnw), full(kvnw)],
        out_shape=[sds((T, D_MODEL), F32), sds(WIN_COMM_SHAPE, F32), sds(wq.shape, F32), sds(wk.shape, F32),
                   sds(wv.shape, F32), sds(w_pre.shape, F32), sds(qnw.shape, F32), sds(kvnw.shape, F32)],
        compiler_params=_params(("arbitrary",)),
    )(x, dh1, cq, ckv, dq, dk, dv, dhq, dhf, dhi, dhg, rc, rs, w_pre, win, qnw, wq, kvnw, wk, wv)


def _arrange_weights(win_t, wuq_full, wukv):
    dt = win_t.dtype
    z = lambda n: jnp.zeros((n, D_MODEL), dt)
    s2 = Q_RANK + KV_RANK
    win_arr = jnp.concatenate([win_t[:s2], z(MLA_NOPE), win_t[s2:s2 + MLA_ROPE], z(HEAD_PAD - MLA_QK),
                               win_t[s2 + MLA_ROPE:]], axis=0)
    wq_arr = jnp.pad(wuq_full, ((0, 0), (0, 0), (0, HEAD_PAD - MLA_QK))).reshape(Q_RANK, MLA_HEADS * HEAD_PAD)
    wk_arr = jnp.pad(wukv[:, :, :MLA_NOPE], ((0, 0), (0, 0), (0, HEAD_PAD - MLA_NOPE))).reshape(
        KV_RANK, MLA_HEADS * HEAD_PAD)
    wv_arr = wukv[:, :, MLA_NOPE:].reshape(KV_RANK, MLA_WIDTH)
    return win_arr, wq_arr, wk_arr, wv_arr


WIN_COMM_SHAPE = (N_CHIPS, FF_SHARD, D_MODEL)


def _win_grad_segments():
    s2 = Q_RANK + KV_RANK
    runs = [(0, s2, 0), (s2, s2 + MLA_ROPE, MLA_NOPE), (s2 + MLA_ROPE, D_IN, HEAD_PAD - MLA_ROPE)]
    per = D_IN // N_CHIPS
    segs = []
    for lo, hi, shift in runs:
        for k in range(N_CHIPS):
            a, b = max(lo, per * k), min(hi, per * (k + 1))
            if a < b:
                segs.append((a + shift, b - a, k, a - per * k))
    return segs


def _unarrange_grads(dwq_arr, dwk_arr, dwv_arr):
    dwuq = dwq_arr.reshape(Q_RANK, MLA_HEADS, HEAD_PAD)[:, :, :MLA_QK]
    dwukv = jnp.concatenate([dwk_arr.reshape(KV_RANK, MLA_HEADS, HEAD_PAD)[:, :, :MLA_NOPE],
                             dwv_arr.reshape(KV_RANK, MLA_HEADS, MLA_V)], axis=-1)
    return dwuq, dwukv


def _rope_inv_freq():
    inv = 1.0 / (ROPE_THETA ** (jnp.arange(0, MLA_ROPE, 2, dtype=F32) / MLA_ROPE))
    z = lambda n: jnp.zeros((n,), F32)
    return jnp.concatenate([z(MLA_NOPE), inv, inv, z(HEAD_PAD - MLA_QK)]).reshape(1, HEAD_PAD)


def _local_step(x, pos, tgt, small, win_arr, wq_arr, wk_arr, wv_arr, late, place=None):
    invf = _rope_inv_freq()
    cq, ckv, xph, qb, kb, vb, kt, vt, rc, rs = _in_fwd(x, pos, invf, small["attn_pre_norm"], win_arr, small["mla_q_norm"],
                                               wq_arr, small["mla_kv_norm"], wk_arr, wv_arr)
    if place is None:
        o_raw, lse = _attn_fwd_t(qb, kb, vt)
        wout, wg, wu, wd = late
    else:
        o_raw, lse, *stacks = _attn_fwd_t(qb, kb, vt, gather=late)
        wout, wg, wu, wd = [lax.dynamic_update_slice(s, l[None], (place[1], 0, 0)) for s, l in zip(stacks, late)]
        wout = wout.reshape(D_MODEL, D_MODEL)
    oh_raw, states = _hgrn_fwd(xph, small["hgrn_lb_logits"])
    h1, y1, zb, mixb = _proj_fwd(x, o_raw, oh_raw, xph, wout, small["mla_out_norm"], small["hgrn_out_norm"],
                                 small["attn_post_norm"], small["ffn_pre_norm"])
    g, up, dy2b, dh2, loss_acc, d_fpost = _ffn_fwd(zb, h1, tgt, small["ffn_post_norm"], wg, wu, wd)
    dwg, dwu, dwd, dzp = _ffn_bwd(zb, g, up, dy2b, wg, wu, wd)
    ffn_grads = [] if place is None else [dwg, dwu, dwd]
    dh1, dwout, d_o, d_oh, dhg, dvec, d_fpre, d_post, d_mla, d_hg, *ffn_rs = _mid_bwd(
        dzp, dh2, h1, y1, mixb, o_raw, oh_raw, xph, wout, small["ffn_pre_norm"], small["attn_post_norm"],
        small["mla_out_norm"], small["hgrn_out_norm"], swap=ffn_grads)
    ffn_ps = _pair_sum(place, ffn_grads, ffn_rs, name="pair_sum_ffn") if ffn_grads else []
    dq, dk, dv, *ffn_ris = _attn_bwd_t(qb, kb, kt, vb, d_o, lse, dvec.reshape(lse.shape), send=ffn_ps)
    ffn_sums = _chip_sum(place, ffn_grads, ffn_rs, ffn_ris, name="chip_sum_ffn") if ffn_grads else []
    dhq, dhf, dhi, d_lbl, *ffn_final = _hgrn_bwd(xph, small["hgrn_lb_logits"], states, d_oh, fill=ffn_sums)
    dx, dwin4, dwq_arr, dwk_arr, dwv_arr, d_pre, d_qn, d_kvn = _in_bwd(
        x, dh1, cq, ckv, dq, dk, dv, dhq, dhf, dhi, dhg, rc, rs, small["attn_pre_norm"], win_arr,
        small["mla_q_norm"], wq_arr, small["mla_kv_norm"], wk_arr, wv_arr)
    dwuq, dwukv = _unarrange_grads(dwq_arr, dwk_arr, dwv_arr)
    loss = 0.5 * jnp.sum(loss_acc) * (1.0 / D_MODEL)
    grads = dict(attn_pre_norm=d_pre, w_in=dwin4, mla_q_norm=d_qn, mla_w_uq=dwuq, mla_kv_norm=d_kvn, mla_w_ukv=dwukv,
                 mla_out_norm=d_mla, hgrn_lb_logits=d_lbl, hgrn_out_norm=d_hg, w_out=dwout, attn_post_norm=d_post,
                 ffn_pre_norm=d_fpre, w_gate=dwg, w_up=dwu, w_down=dwd, ffn_post_norm=d_fpost)
    if place is None:
        return loss, dx, grads
    return loss, dx, grads, ffn_final


def _place():
    x, y, c = lax.axis_index("x"), lax.axis_index("y"), lax.axis_index("c")
    others = [(1 - x, y), (x, 1 - y), (1 - x, 1 - y)]
    return x, y, c, 2 * x + y, (x, y, 1 - c), others


def _half(ref, c, rows):
    return ref.at[pl.ds(pl.multiple_of(c * rows, 8), rows)]


def _rcopy(src, dst, send, recv, k, to):
    return pltpu.make_async_remote_copy(src_ref=src, dst_ref=dst, send_sem=send.at[k], recv_sem=recv.at[k],
                                        device_id=to, device_id_type=MESH)


class _Gather:
    def __init__(self, ins, outs, send, recv):
        self.ins, self.outs, self.send, self.recv = ins, outs, send, recv
        self.n = len(ins)
        self.halves = [r.shape[0] // 2 for r in ins]
        _, _, self.c, self.me, self.sib, self.others = _place()

    def _each(self):
        for j, (px, py) in enumerate(self.others):
            for a in range(self.n):
                yield j * self.n + a, a, 2 * px + py, (px, py, self.c)

    def sends(self):
        return [_rcopy(_half(self.ins[a], self.c, self.halves[a]), _half(self.outs[a].at[self.me], self.c, self.halves[a]),
                       self.send, self.recv, k, to) for k, a, _, to in self._each()]

    def arrivals(self):
        parts = [(k, _half(self.outs[a].at[chip], self.c, self.halves[a]), to) for k, a, chip, to in self._each()]
        return [_rcopy(p, p, self.send, self.recv, k, to) for k, p, to in parts]

    def forwards(self):
        parts = [(k, _half(self.outs[a].at[chip], self.c, self.halves[a])) for k, a, chip, _ in self._each()]
        return [_rcopy(p, p, self.send, self.recv, 3 * self.n + k, self.sib) for k, p in parts]

    def forward_arrivals(self):
        parts = [(k, _half(self.outs[a].at[chip], 1 - self.c, self.halves[a])) for k, a, chip, _ in self._each()]
        return [_rcopy(p, p, self.send, self.recv, 3 * self.n + k, self.sib) for k, p in parts]

    @staticmethod
    def out_shapes(arrs):
        return [jax.ShapeDtypeStruct((N_CHIPS,) + a.shape, a.dtype) for a in arrs]

    @staticmethod
    def semaphores(arrs):
        return [pltpu.SemaphoreType.DMA((6 * len(arrs),)), pltpu.SemaphoreType.DMA((6 * len(arrs),))]


def _gather_chips(arrs, name):
    n = len(arrs)

    def body(*refs):
        gat = _Gather(refs[:n], refs[n:2 * n], *refs[2 * n:])
        sends, forwards = gat.sends(), gat.forwards()
        for cp in sends:
            cp.start()
        for arrival, fw in zip(gat.arrivals(), forwards):
            arrival.wait_recv()
            fw.start()
        for arrival in gat.forward_arrivals():
            arrival.wait_recv()
        for cp in sends + forwards:
            cp.wait_send()

    return pl.pallas_call(body, name=name, in_specs=[ANY] * n, out_specs=[ANY] * n, out_shape=_Gather.out_shapes(arrs),
                          scratch_shapes=_Gather.semaphores(arrs))(*arrs)


GRAD_BLOCKS = 2


def _pair_swap_copies(g_refs, r_refs, send, recv):
    _, _, c, _, sib, _ = _place()
    copies = []
    for a, (g, r) in enumerate(zip(g_refs, r_refs)):
        h = g.shape[1] // 2
        copies.append(_rcopy(g.at[:, pl.ds(pl.multiple_of((1 - c) * h, 8), h)], r, send, recv, a, sib))
    return copies


def _half_stack_shapes(gs, dtype=None):
    return [jax.ShapeDtypeStruct((N_CHIPS, g.shape[1] // 2, g.shape[2]), dtype or g.dtype) for g in gs]


def _pair_swap(gs, sm):
    n = len(gs)

    def body(*refs):
        g_refs, sm_ref = refs[:n], refs[n]
        r_refs, ssib_ref = refs[n + 1:2 * n + 1], refs[2 * n + 1]
        send, recv = refs[2 * n + 2:]
        copies = _pair_swap_copies(g_refs, r_refs, send, recv)
        copies.append(_rcopy(sm_ref, ssib_ref, send, recv, n, _place()[4]))
        for cp in copies:
            cp.start()
        for cp in copies:
            cp.wait()

    return pl.pallas_call(
        body, name="pair_swap", in_specs=[ANY] * (n + 1), out_specs=[ANY] * (n + 1),
        out_shape=_half_stack_shapes(gs) + [jax.ShapeDtypeStruct(sm.shape, sm.dtype)],
        scratch_shapes=[pltpu.SemaphoreType.DMA((n + 1,)), pltpu.SemaphoreType.DMA((n + 1,))],
    )(*gs, sm)


def _pair_sum(place, gs, rs, small=None, name="pair_sum"):
    n = len(gs)
    nb = GRAD_BLOCKS

    def body(place_ref, *refs):
        g_refs, r_refs, p_refs = refs[:n], refs[n:2 * n], refs[-n - 1:-1] if small else refs[-n:]
        for a in range(n):
            p_refs[a][0] = (g_refs[a][0] + r_refs[a][0]).astype(p_refs[a].dtype)
        if small:
            @pl.when((pl.program_id(0) == 0) & (pl.program_id(1) == 0))
            def _():
                refs[-1][...] = refs[2 * n][...] + refs[2 * n + 1][...]

    in_specs, out_specs = [], []
    for g in gs:
        blk = (1, g.shape[1] // 2 // nb, g.shape[2])
        in_specs.append(pl.BlockSpec(blk, lambda i, k, p: (k, p[0] * nb + i, 0)))
    for g in gs:
        blk = (1, g.shape[1] // 2 // nb, g.shape[2])
        in_specs.append(pl.BlockSpec(blk, lambda i, k, p: (k, i, 0)))
        out_specs.append(pl.BlockSpec(blk, lambda i, k, p: (k, i, 0)))
    out_shape = _half_stack_shapes(gs, BF16)
    if small:
        sm_spec = pl.BlockSpec(small[0].shape, lambda i, k, p: (0, 0))
        in_specs += [sm_spec, sm_spec]
        out_specs.append(sm_spec)
        out_shape.append(jax.ShapeDtypeStruct(small[0].shape, F32))
    return pl.pallas_call(
        body, name=name,
        grid_spec=pltpu.PrefetchScalarGridSpec(num_scalar_prefetch=1, grid=(nb, N_CHIPS), in_specs=in_specs,
                                               out_specs=out_specs),
        out_shape=out_shape,
        compiler_params=_params(("arbitrary", "arbitrary")),
    )(place, *gs, *rs, *(small or ()))


def _chip_swap_copies(p_refs, ri_refs, send, recv):
    _, _, c, _, _, others = _place()
    n = len(p_refs)
    return [_rcopy(p_refs[a].at[2 * px + py], ri_refs[a].at[j], send, recv, j * n + a, (px, py, c))
            for j, (px, py) in enumerate(others) for a in range(n)]


def _chip_swap_shapes(ps):
    return [jax.ShapeDtypeStruct((3,) + p.shape[1:], p.dtype) for p in ps]


def _chip_swap(ps, pair):
    n = len(ps)

    def body(*refs):
        start, finish = _chip_swap_plan(refs[:n], refs[n], refs[n + 1:2 * n + 1], refs[2 * n + 1], *refs[2 * n + 2:])
        start()
        finish()

    return pl.pallas_call(
        body, name="chip_swap", in_specs=[ANY] * (n + 1), out_specs=[ANY] * (n + 1),
        out_shape=_chip_swap_out_shapes(ps, pair), scratch_shapes=_chip_swap_semaphores(n),
    )(*ps, pair)


def _chip_swap_plan(p_refs, pair_ref, ri_refs, sm4_ref, send, recv, lsem):
    n = len(p_refs)
    hs = SMALL_ROWS // 2
    x, y, c, me, sib, others = _place()
    local = pltpu.make_async_copy(pair_ref, sm4_ref.at[me], lsem.at[0])
    copies = _chip_swap_copies(p_refs, ri_refs, send, recv)
    arrivals = list(copies)
    for j, (px, py) in enumerate(others):
        copies.append(_rcopy(_half(pair_ref, c, hs), _half(sm4_ref.at[me], c, hs), send, recv, 3 * n + j, (px, py, c)))
        part = _half(sm4_ref.at[2 * px + py], c, hs)
        arrivals.append(_rcopy(part, part, send, recv, 3 * n + j, (px, py, c)))

    def start():
        local.start()
        for cp in copies:
            cp.start()

    def finish():
        for arrival in arrivals:
            arrival.wait_recv()
        for cp in copies:
            cp.wait_send()
        local.wait()

    return start, finish


def _chip_swap_out_shapes(ps, pair):
    return _chip_swap_shapes(ps) + [jax.ShapeDtypeStruct((N_CHIPS,) + pair.shape, pair.dtype)]


def _chip_swap_semaphores(n):
    k = 3 * (n + 1)
    return [pltpu.SemaphoreType.DMA((k,)), pltpu.SemaphoreType.DMA((k,)), pltpu.SemaphoreType.DMA((1,))]


def _chip_sum(place, gs, rs, ris, name="chip_sum"):
    n = len(gs)
    nb = GRAD_BLOCKS

    def body(place_ref, *refs):
        g_refs, r_refs, ri_refs, o_refs = refs[:n], refs[n:2 * n], refs[2 * n:3 * n], refs[3 * n:]
        for a in range(n):
            ri = ri_refs[a]
            o_refs[a][...] = (g_refs[a][0] + r_refs[a][0]) + ri[0].astype(F32) + ri[1].astype(F32) + ri[2].astype(F32)

    in_specs, out_specs, out_shape = [], [], []
    for g in gs:
        blk = (1, g.shape[1] // 2 // nb, g.shape[2])
        in_specs.append(pl.BlockSpec(blk, lambda i, p: (p[1], p[0] * nb + i, 0)))
    for g in gs:
        blk = (1, g.shape[1] // 2 // nb, g.shape[2])
        in_specs.append(pl.BlockSpec(blk, lambda i, p: (p[1], i, 0)))
    for g in gs:
        rb = g.shape[1] // 2 // nb
        in_specs.append(pl.BlockSpec((3, rb, g.shape[2]), lambda i, p: (0, i, 0)))
        out_specs.append(pl.BlockSpec((rb, g.shape[2]), lambda i, p: (p[0] * nb + i, 0)))
        out_shape.append(jax.ShapeDtypeStruct(g.shape[1:], F32))
    return pl.pallas_call(
        body, name=name,
        grid_spec=pltpu.PrefetchScalarGridSpec(num_scalar_prefetch=1, grid=(nb,), in_specs=in_specs, out_specs=out_specs),
        out_shape=out_shape,
        compiler_params=_params(("arbitrary",)),
    )(place, *gs, *rs, *ris)


def _pair_fill_copies(g_refs, send, recv):
    _, _, c, _, sib, _ = _place()
    copies, waits = [], []
    for a, g in enumerate(g_refs):
        h = g.shape[0] // 2
        mine, theirs = _half(g, c, h), _half(g, 1 - c, h)
        copies.append(_rcopy(mine, mine, send, recv, a, sib))
        waits.append(_rcopy(theirs, theirs, send, recv, a, sib))
    return copies, waits


def _pair_fill(gfs, sm4):
    n = len(gfs)
    hs = SMALL_ROWS // 2

    def body(*refs):
        g_refs, sm4_ref = refs[n + 1:2 * n + 1], refs[2 * n + 1]
        send, recv = refs[2 * n + 2:]
        x, y, c, me, sib, others = _place()
        copies, waits = _pair_fill_copies(g_refs, send, recv)
        for j, (px, py) in enumerate(others):
            chip = 2 * px + py
            mine, theirs = _half(sm4_ref.at[chip], c, hs), _half(sm4_ref.at[chip], 1 - c, hs)
            copies.append(pltpu.make_async_remote_copy(src_ref=mine, dst_ref=mine, send_sem=send.at[n + j],
                                                       recv_sem=recv.at[n + j], device_id=sib, device_id_type=MESH))
            waits.append(pltpu.make_async_remote_copy(src_ref=theirs, dst_ref=theirs, send_sem=send.at[n + j],
                                                      recv_sem=recv.at[n + j], device_id=sib, device_id_type=MESH))
        for cp in copies:
            cp.start()
        for w in waits:
            w.wait_recv()
        for cp in copies:
            cp.wait_send()

    return pl.pallas_call(
        body, name="pair_fill", in_specs=[ANY] * (n + 1), out_specs=[ANY] * (n + 1),
        out_shape=[jax.ShapeDtypeStruct(g.shape, g.dtype) for g in gfs] + [jax.ShapeDtypeStruct(sm4.shape, sm4.dtype)],
        input_output_aliases={i: i for i in range(n + 1)},
        scratch_shapes=[pltpu.SemaphoreType.DMA((n + 3,)), pltpu.SemaphoreType.DMA((n + 3,))],
    )(*gfs, sm4)


def _adamw_math(w, g, m, v):
    m = ADAM_B1 * m + (1.0 - ADAM_B1) * g
    v = ADAM_B2 * v + (1.0 - ADAM_B2) * (g * g)
    m_hat = m / (1.0 - ADAM_B1 ** ADAM_STEP)
    v_hat = v / (1.0 - ADAM_B2 ** ADAM_STEP)
    return -ADAM_LR * (m_hat / (jnp.sqrt(v_hat) + ADAM_EPS) + ADAM_WD * w), m, v


def _adamw(items, steps, name):
    n = len(items)

    def body(*refs):
        for a in range(n):
            g = refs[4 * a + 1][...]
            d, mo, vo = _adamw_math(refs[4 * a][...], g, refs[4 * a + 2][...], refs[4 * a + 3][...])
            for out, val in zip(refs[4 * n + 4 * a:4 * n + 4 * a + 4], (g, d, mo, vo)):
                out[...] = val

    spec = lambda w: pl.BlockSpec((w.shape[0] // steps, w.shape[1]), lambda i: (i, 0))
    flat = pl.pallas_call(
        body, name=name, grid=(steps,), in_specs=[spec(it[0]) for it in items for _ in range(4)],
        out_specs=[spec(it[0]) for it in items for _ in range(4)],
        out_shape=[jax.ShapeDtypeStruct(it[0].shape, F32) for it in items for _ in range(4)],
        compiler_params=_params(("arbitrary",)),
    )(*[a for it in items for a in it])
    return [flat[4 * a:4 * a + 4] for a in range(n)]


def _adamw_small(sm4, wmv):
    views = SMALL_VIEWS[:-1]
    n = len(views)

    def body(sm4_ref, *refs):
        g_all = ((sm4_ref[0] + sm4_ref[1]) + sm4_ref[2]) + sm4_ref[3]
        row = 0
        for a, (_, rows, cols) in enumerate(views):
            g = g_all[row:row + rows, :cols]
            row += -(-rows // ROW_TILE) * ROW_TILE
            d, mo, vo = _adamw_math(refs[3 * a][...], g, refs[3 * a + 1][...], refs[3 * a + 2][...])
            for out, val in zip(refs[3 * n + 4 * a:3 * n + 4 * a + 4], (g, d, mo, vo)):
                out[...] = val
        refs[-1][...] = g_all[row:row + 1, :128]

    flat = pl.pallas_call(
        body, name="adamw_small",
        out_shape=[jax.ShapeDtypeStruct((rows, cols), F32) for _, rows, cols in views for _ in range(4)]
        + [jax.ShapeDtypeStruct((1, 128), F32)],
        compiler_params=pltpu.CompilerParams(vmem_limit_bytes=VMEM_LIMIT),
    )(sm4, *[a for t in wmv for a in t])
    return [flat[4 * a:4 * a + 4] for a in range(n)] + [flat[-1]]


SMALL_NAMES = ("attn_pre_norm", "mla_q_norm", "mla_kv_norm", "mla_w_ukv", "mla_out_norm", "hgrn_lb_logits",
               "hgrn_out_norm", "attn_post_norm", "ffn_pre_norm", "ffn_post_norm")
BIG_NAMES = ("w_in", "mla_w_uq", "w_out", "w_gate", "w_up", "w_down")
WEIGHT_NAMES = ("attn_pre_norm", "w_in", "mla_q_norm", "mla_w_uq", "mla_kv_norm", "mla_w_ukv", "mla_out_norm",
                "hgrn_lb_logits", "hgrn_out_norm", "w_out", "attn_post_norm", "ffn_pre_norm", "w_gate", "w_up", "w_down",
                "ffn_post_norm")


UQ_COMM_SHAPE = (192, 384)


def _pack_small(vals):
    parts = []
    for name, rows, cols in SMALL_VIEWS:
        pad_rows = -(-rows // ROW_TILE) * ROW_TILE - rows
        parts.append(jnp.pad(vals[name].reshape(rows, cols), ((0, pad_rows), (0, D_MODEL - cols))))
    return jnp.concatenate(parts, axis=0)


def kernel(x, positions, attn_pre_norm, w_in, mla_q_norm, mla_w_uq, mla_kv_norm, mla_w_ukv, mla_out_norm, hgrn_lb_logits, hgrn_out_norm, w_out, attn_post_norm, ffn_pre_norm, w_gate, w_up, w_down, ffn_post_norm, loss_target, m_attn_pre_norm, m_w_in, m_mla_q_norm, m_mla_w_uq, m_mla_kv_norm, m_mla_w_ukv, m_mla_out_norm, m_hgrn_lb_logits, m_hgrn_out_norm, m_w_out, m_attn_post_norm, m_ffn_pre_norm, m_w_gate, m_w_up, m_w_down, m_ffn_post_norm, v_attn_pre_norm, v_w_in, v_mla_q_norm, v_mla_w_uq, v_mla_kv_norm, v_mla_w_ukv, v_mla_out_norm, v_hgrn_lb_logits, v_hgrn_out_norm, v_w_out, v_attn_post_norm, v_ffn_pre_norm, v_w_gate, v_w_up, v_w_down, v_ffn_post_norm):
    args = locals()
    W = {n: args[n] for n in WEIGHT_NAMES}
    M = {n: args["m_" + n] for n in WEIGHT_NAMES}
    V = {n: args["v_" + n] for n in WEIGHT_NAMES}
    T = x.shape[1]
    cx, cy, cc = lax.axis_index("x"), lax.axis_index("y"), lax.axis_index("c")

    win_rows = D_IN // N_CHIPS
    shard2d = {"w_in": (win_rows, D_MODEL), "mla_w_uq": (Q_RANK // N_CHIPS, MLA_HEADS * MLA_QK),
               "w_out": (D_MODEL // N_CHIPS, D_MODEL), "w_gate": (FF_SHARD, D_MODEL), "w_up": (FF_SHARD, D_MODEL),
               "w_down": (FF_SHARD, D_MODEL)}
    transposed = ("w_in", "w_gate", "w_up")
    to2d = lambda n, a: a[0].T if n in transposed else a.reshape(shard2d[n])
    from2d = lambda n, t: t.T[None] if n in transposed else t.reshape(W[n].shape)
    me = 2 * cx + cy
    place = jnp.stack([cc, me]).astype(jnp.int32)
    local_b = [to2d(n, W[n]).astype(BF16) for n in BIG_NAMES]
    local_b[0] = jnp.pad(local_b[0], ((0, FF_SHARD - win_rows), (0, 0)))
    stacks = _gather_chips(local_b[:2], "gather_weights")
    win4, wuq4 = [lax.dynamic_update_slice(s, l[None], (me, 0, 0)) for s, l in zip(stacks, local_b)]
    win_t = win4[:, :win_rows].reshape(D_IN, D_MODEL)
    wuq_full = wuq4.reshape(Q_RANK, MLA_HEADS, MLA_QK)
    win_arr, wq_arr, wk_arr, wv_arr = _arrange_weights(win_t, wuq_full, mla_w_ukv[0].astype(BF16))
    small = {n: W[n][0] if n == "mla_w_ukv" else W[n].reshape(-1, W[n].shape[-1]) for n in SMALL_NAMES}

    loss_local, dx, grads, ffn_final = _local_step(x[0], positions.reshape(T, 1), loss_target[0], small, win_arr,
                                                           wq_arr, wk_arr, wv_arr, local_b[2:], place)

    gs = [grads["w_in"], grads["mla_w_uq"].reshape((N_CHIPS,) + UQ_COMM_SHAPE), grads["w_out"].reshape((N_CHIPS,) + shard2d["w_out"])]
    sm = _pack_small({**grads, "loss": loss_local})
    *rs, ssib = _pair_swap(gs, sm)
    *ps, pair = _pair_sum(place, gs, rs, small=(sm, ssib))
    ffn_names, rest_names = BIG_NAMES[3:], BIG_NAMES[:3]
    g2d = dict(zip(ffn_names, ffn_final))
    adam_in = lambda names_: [(to2d(n, W[n]), g2d[n], to2d(n, M[n]), to2d(n, V[n])) for n in names_]
    updates = dict(zip(ffn_names, _adamw(adam_in(ffn_names), 8, "adamw_ffn")))
    *ris, sm4 = _chip_swap(ps, pair)
    *gfin, smf = _pair_fill(_chip_sum(place, gs, rs, ris), sm4)

    g2d.update({n: gfin[k].reshape((-1,) + shard2d[n][1:]) for k, n in enumerate(rest_names)})
    updates.update(zip(rest_names[:2], _adamw(adam_in(rest_names[:2]), 3, "adamw_w_in")))
    updates.update(zip(rest_names[2:], _adamw(adam_in(rest_names[2:]), 8, "adamw_w_out")))
    G, DW, NM, NV = {}, {}, {}, {}
    for n, outs in updates.items():
        G[n], DW[n], NM[n], NV[n] = (from2d(n, t) for t in outs)
    view2d = lambda n, a: a.reshape(next((r, c) for name, r, c in SMALL_VIEWS if name == n))
    *res, loss_row = _adamw_small(smf, [tuple(view2d(n, t[n]) for t in (W, M, V)) for n in SMALL_NAMES])
    for n, outs in zip(SMALL_NAMES, res):
        G[n], DW[n], NM[n], NV[n] = (t.reshape(W[n].shape) for t in outs)
    loss = loss_row[0, 0]
    return (loss, dx[None], *[G[n] for n in WEIGHT_NAMES], *[DW[n] for n in WEIGHT_NAMES],
            *[NM[n] for n in WEIGHT_NAMES], *[NV[n] for n in WEIGHT_NAMES])
```

```python
import jax
import jax.numpy as jnp
from jax import lax
from jax.experimental import pallas as pl
from jax.experimental.pallas import tpu as pltpu

F32 = jnp.float32
BF16 = jnp.bfloat16
MXU_DTYPE = BF16

D_MODEL = 1024
MLA_HEADS = 8
MLA_NOPE = 64
MLA_ROPE = 32
MLA_V = 64
MLA_QK = MLA_NOPE + MLA_ROPE
Q_RANK = 384
KV_RANK = 128
MLA_WIDTH = MLA_HEADS * MLA_V
HEAD_PAD = 128
HGRN_HEADS = 4
HGRN_DIM = 128
HGRN_WIDTH = HGRN_HEADS * HGRN_DIM
CHUNK = 64
SUB = 16
HGRN_CPI = 4
D_IN = Q_RANK + KV_RANK + MLA_ROPE + 4 * HGRN_WIDTH
D_IN_ARR = Q_RANK + KV_RANK + HEAD_PAD + 4 * HGRN_WIDTH
D_FF = 2816
N_CHIPS = 4
FF_SHARD = D_FF // N_CHIPS
EPS = 1e-6
ROPE_THETA = 10000.0
ATTN_SCALE = MLA_QK ** -0.5
ATTN_SCALE_LOG2 = ATTN_SCALE * 1.4426950408889634
NEG_BIG = -1e30

ADAM_LR = 0.001
ADAM_B1 = 0.9
ADAM_B2 = 0.999
ADAM_EPS = 1e-08
ADAM_WD = 0.01
ADAM_STEP = 10

VMEM_LIMIT = 56 * 1024 * 1024

SMALL_VIEWS = (("attn_pre_norm", 1, 1024), ("mla_q_norm", 1, 384), ("mla_kv_norm", 1, 128), ("mla_w_ukv", 128, 1024),
               ("mla_out_norm", 1, 512), ("hgrn_lb_logits", 2, 512), ("hgrn_out_norm", 1, 512),
               ("attn_post_norm", 1, 1024), ("ffn_pre_norm", 1, 1024), ("ffn_post_norm", 1, 1024), ("loss", 1, 1))
ROW_TILE = 8
SMALL_ROWS = sum(-(-rows // ROW_TILE) * ROW_TILE for _, rows, _ in SMALL_VIEWS)

MESH = pl.DeviceIdType.MESH
ANY = pl.BlockSpec(memory_space=pl.ANY)


def _dot(a, b, dims, exact):
    if exact:
        return lax.dot_general(a.astype(F32), b.astype(F32), (dims, ((), ())), precision=lax.Precision.HIGH,
                               preferred_element_type=F32)
    return lax.dot_general(a.astype(MXU_DTYPE), b.astype(MXU_DTYPE), (dims, ((), ())), preferred_element_type=F32)


def _mm(a, b, exact=False):
    return _dot(a, b, ((1,), (0,)), exact)


def _mm_nt(a, b, exact=False):
    return _dot(a, b, ((1,), (1,)), exact)


def _mm_tn(a, b, exact=False):
    return _dot(a, b, ((0,), (0,)), exact)


def _rms_fwd(x, w):
    r = lax.rsqrt(jnp.mean(x * x, axis=-1, keepdims=True) + EPS)
    xn = x * r
    return xn * w, xn, r


def _rms_bwd(dy, xn, r, w):
    dxn = dy * w
    dx = r * (dxn - xn * jnp.mean(dxn * xn, axis=-1, keepdims=True))
    dw = jnp.sum(dy * xn, axis=0, keepdims=True)
    return dx, dw


def _group_sums(v, gs):
    t, n = v.shape
    lane = lax.broadcasted_iota(jnp.int32, (t, 128), 1)
    out = []
    for p in range(n // 128):
        vb = v[:, 128 * p:128 * (p + 1)]
        if gs == 128:
            out.append(jnp.sum(vb, axis=-1, keepdims=True))
        else:
            out.append(jnp.sum(jnp.where(lane < 64, vb, 0.0), axis=-1, keepdims=True))
            out.append(jnp.sum(jnp.where(lane >= 64, vb, 0.0), axis=-1, keepdims=True))
    return out


def _group_bcast(sums, gs, t):
    lane = lax.broadcasted_iota(jnp.int32, (t, 128), 1)
    if gs == 128:
        return jnp.concatenate([jnp.broadcast_to(s, (t, 128)) for s in sums], axis=-1)
    return jnp.concatenate([jnp.where(lane < 64, sums[2 * p], sums[2 * p + 1]) for p in range(len(sums) // 2)],
                           axis=-1)


def _grms_fwd(x, w, gs):
    t = x.shape[0]
    r = lax.rsqrt(_group_bcast(_group_sums(x * x, gs), gs, t) * (1.0 / gs) + EPS)
    xn = x * r
    return xn * w, xn, r


def _grms_bwd(dy, xn, r, w, gs):
    t = dy.shape[0]
    dxn = dy * w
    dx = r * (dxn - xn * (_group_bcast(_group_sums(dxn * xn, gs), gs, t) * (1.0 / gs)))
    dw = jnp.sum(dy * xn, axis=0, keepdims=True)
    return dx, dw


def _rope_tables(c_tab, s_tab):
    lane = lax.broadcasted_iota(jnp.int32, c_tab.shape, 1)
    first = (lane >= MLA_NOPE) & (lane < MLA_NOPE + MLA_ROPE // 2)
    second = (lane >= MLA_NOPE + MLA_ROPE // 2) & (lane < MLA_QK)
    return c_tab, jnp.where(first, -s_tab, 0.0), jnp.where(second, s_tab, 0.0)


def _rope(v, c, sa, sb):
    return v * c + pltpu.roll(v, HEAD_PAD - MLA_ROPE // 2, 1) * sa + pltpu.roll(v, MLA_ROPE // 2, 1) * sb


def _rope_bwd(d, c, sa, sb):
    return d * c - pltpu.roll(d, HEAD_PAD - MLA_ROPE // 2, 1) * sa - pltpu.roll(d, MLA_ROPE // 2, 1) * sb


def _params(sem, vmem=VMEM_LIMIT):
    return pltpu.CompilerParams(dimension_semantics=sem, vmem_limit_bytes=vmem)


def _in_fwd(x, pos, invf, w_pre, win, qnw, wq, kvnw, wk, wv, tt=512):
    T = x.shape[0]

    def body(x_ref, pos_ref, invf_ref, wpre_ref, win_ref, qnw_ref, wq_ref, kvnw_ref, wk_ref, wv_ref,
             cq_ref, ckv_ref, xph_ref, q_ref, k_ref, v_ref, kt_ref, vt_ref, rc_ref, rs_ref):
        u, _, _ = _rms_fwd(x_ref[...], wpre_ref[...])
        lo = Q_RANK + KV_RANK + HEAD_PAD
        xp = _mm_nt(u, win_ref[:lo, :])
        xph_ref[...] = _mm_nt(u, win_ref[lo:, :])
        cq = xp[:, :Q_RANK]
        ckv = xp[:, Q_RANK:Q_RANK + KV_RANK]
        kr = xp[:, Q_RANK + KV_RANK:]
        cq_ref[...] = cq
        ckv_ref[...] = ckv
        ang = pos_ref[...].astype(F32) * invf_ref[...]
        c_tab = jnp.cos(ang)
        s_tab = jnp.sin(ang)
        rc_ref[...] = c_tab
        rs_ref[...] = s_tab
        c, sa, sb = _rope_tables(c_tab, s_tab)
        qn, _, _ = _rms_fwd(cq, qnw_ref[...])
        q = _mm(qn, wq_ref[...])
        kvn, _, _ = _rms_fwd(ckv, kvnw_ref[...])
        kn = _mm(kvn, wk_ref[...])
        v = _mm(kvn, wv_ref[...])
        v_ref[...] = v.astype(v_ref.dtype)
        vt_ref[...] = v.T.astype(vt_ref.dtype)
        krr = _rope(kr, c, sa, sb)
        for h in range(MLA_HEADS):
            sl = slice(HEAD_PAD * h, HEAD_PAD * (h + 1))
            q_ref[:, sl] = (_rope(q[:, sl], c, sa, sb) * ATTN_SCALE_LOG2).astype(q_ref.dtype)
            kh = kn[:, sl] + krr
            k_ref[:, sl] = kh.astype(k_ref.dtype)
            kt_ref[sl, :] = kh.T.astype(kt_ref.dtype)

    row = lambda w: pl.BlockSpec((tt, w), lambda i: (i, 0))
    full = lambda a: pl.BlockSpec(a.shape, lambda i: (0,) * a.ndim)
    qk_w = MLA_HEADS * HEAD_PAD
    return pl.pallas_call(
        body, name="in_fwd", grid=(T // tt,),
        in_specs=[row(D_MODEL), row(1), full(invf), full(w_pre), full(win), full(qnw), full(wq), full(kvnw),
                  full(wk), full(wv)],
        out_specs=[row(Q_RANK), row(KV_RANK), row(4 * HGRN_WIDTH), row(qk_w), row(qk_w), row(MLA_WIDTH),
                   pl.BlockSpec((qk_w, tt), lambda i: (0, i)), pl.BlockSpec((MLA_WIDTH, tt), lambda i: (0, i)),
                   row(HEAD_PAD), row(HEAD_PAD)],
        out_shape=[jax.ShapeDtypeStruct((T, Q_RANK), F32), jax.ShapeDtypeStruct((T, KV_RANK), F32),
                   jax.ShapeDtypeStruct((T, 4 * HGRN_WIDTH), F32), jax.ShapeDtypeStruct((T, qk_w), MXU_DTYPE),
                   jax.ShapeDtypeStruct((T, qk_w), MXU_DTYPE), jax.ShapeDtypeStruct((T, MLA_WIDTH), MXU_DTYPE),
                   jax.ShapeDtypeStruct((qk_w, T), MXU_DTYPE), jax.ShapeDtypeStruct((MLA_WIDTH, T), MXU_DTYPE),
                   jax.ShapeDtypeStruct((T, HEAD_PAD), F32), jax.ShapeDtypeStruct((T, HEAD_PAD), F32)],
        compiler_params=_params(("arbitrary",)),
    )(x, pos, invf, w_pre, win, qnw, wq, kvnw, wk, wv)


def _attn_fwd_t(qb, kb, vt, gather=(), tq=256, hps=8):
    T = qb.shape[0]
    nq = T // tq
    ng = len(gather)
    steps = (MLA_HEADS // hps) * nq
    pass_on = steps - 3

    def body(q_ref, k_ref, vt_ref, *rest):
        o_ref, lse_ref = rest[ng:ng + 2]
        acc_scr = rest[2 * ng + 2]
        qi = pl.program_id(1)
        step_no = pl.program_id(0) * nq + qi
        if ng:
            gat = _Gather(rest[:ng], rest[ng + 2:2 * ng + 2], *rest[2 * ng + 3:])

            @pl.when(step_no == 0)
            def _():
                for cp in gat.sends():
                    cp.start()

            @pl.when(step_no == pass_on)
            def _():
                for arrival in gat.arrivals():
                    arrival.wait_recv()
                for cp in gat.forwards():
                    cp.start()

        heads = [slice(HEAD_PAD * a, HEAD_PAD * (a + 1)) for a in range(hps)]
        acc_scr[...] = jnp.zeros_like(acc_scr)

        def step(j, carry, masked):
            start = pl.multiple_of(j * tq, tq)
            scores = [_mm_nt(k_ref[pl.ds(start, tq), heads[a]], q_ref[:, heads[a]]) for a in range(hps)]
            new, probs, alphas = [], [], []
            for a in range(hps):
                m, l = carry[a]
                s = scores[a]
                if masked:
                    kk = lax.broadcasted_iota(jnp.int32, (tq, tq), 0)
                    qq = lax.broadcasted_iota(jnp.int32, (tq, tq), 1)
                    s = jnp.where(kk <= qq, s, NEG_BIG)
                m_new = jnp.maximum(m, jnp.max(s, axis=0, keepdims=True))
                alpha = jnp.exp2(m - m_new)
                p = jnp.exp2(s - m_new)
                l = l * alpha + jnp.sum(p, axis=0, keepdims=True)
                new.append((m_new, l))
                probs.append(p.astype(MXU_DTYPE))
                alphas.append(alpha)
                if a % 2:
                    pr = a // 2
                    vtj = vt_ref[2 * MLA_V * pr:2 * MLA_V * (pr + 1), pl.ds(start, tq)]
                    none = jnp.zeros((MLA_V, tq), vtj.dtype)
                    pv = (_mm(jnp.concatenate([vtj[:MLA_V], none], axis=0), probs[a - 1])
                          + _mm(jnp.concatenate([none, vtj[MLA_V:]], axis=0), probs[a]))
                    acc_scr[pr] = acc_scr[pr] * jnp.where(row < MLA_V, alphas[a - 1], alphas[a]) + pv
            return tuple(new)

        row = lax.broadcasted_iota(jnp.int32, (2 * MLA_V, tq), 0)
        init = tuple((jnp.full((1, tq), NEG_BIG, F32), jnp.zeros((1, tq), F32)) for _ in range(hps))
        carry = lax.fori_loop(0, qi, lambda j, c: step(j, c, False), init)
        carry = step(qi, carry, True)
        for pr in range(hps // 2):
            (m0, l0), (m1, l1) = carry[2 * pr], carry[2 * pr + 1]
            ot = acc_scr[pr] / jnp.where(row < MLA_V, l0, l1)
            o_ref[:, 2 * MLA_V * pr:2 * MLA_V * (pr + 1)] = ot.T
            lse_ref[pr, 0:1, :] = m0 + jnp.log2(l0)
            lse_ref[pr, 1:2, :] = m1 + jnp.log2(l1)

        if ng:
            @pl.when(step_no == steps - 1)
            def _():
                for arrival in gat.forward_arrivals():
                    arrival.wait_recv()
                for cp in gat.sends() + gat.forwards():
                    cp.wait_send()

    return pl.pallas_call(
        body, name="attn_fwd", grid=(MLA_HEADS // hps, nq),
        in_specs=[pl.BlockSpec((tq, hps * HEAD_PAD), lambda g, i: (i, g)),
                  pl.BlockSpec((T, hps * HEAD_PAD), lambda g, i: (0, g)),
                  pl.BlockSpec((hps * MLA_V, T), lambda g, i: (g, 0))] + [ANY] * ng,
        out_specs=[pl.BlockSpec((tq, hps * MLA_V), lambda g, i: (i, g)),
                   pl.BlockSpec((hps // 2, 2, tq), lambda g, i: (g, 0, i))] + [ANY] * ng,
        out_shape=[jax.ShapeDtypeStruct((T, MLA_WIDTH), F32), jax.ShapeDtypeStruct((MLA_HEADS // 2, 2, T), F32)]
        + _Gather.out_shapes(gather),
        scratch_shapes=[pltpu.VMEM((hps // 2, 2 * MLA_V, tq), F32)] + (_Gather.semaphores(gather) if ng else []),
        compiler_params=_params(("arbitrary", "arbitrary")),
    )(qb, kb, vt, *gather)


def _attn_bwd_t(qb, kb, kt, vb, dob, lse, dvec, send=(), tq=512, hps=4):
    T = qb.shape[0]
    nq = T // tq
    ns = len(send)
    steps = (MLA_HEADS // hps) * nq

    def body(q_ref, k_ref, kt_ref, v_ref, do_ref, lse_ref, d_ref, *rest):
        dqt_ref, dk_ref, dv_ref = rest[ns:ns + 3]
        va_scr, dv_scr = rest[2 * ns + 3:2 * ns + 5]
        j = pl.program_id(1)
        step_no = pl.program_id(0) * nq + j
        if ns:
            @pl.when(step_no == 0)
            def _():
                for cp in _chip_swap_copies(rest[:ns], rest[ns + 3:2 * ns + 3], *rest[2 * ns + 5:]):
                    cp.start()

        @pl.when(j == 0)
        def _():
            dqt_ref[...] = jnp.zeros_like(dqt_ref)

        lane = lax.broadcasted_iota(jnp.int32, (tq, 2 * MLA_V), 1)
        heads = [slice(HEAD_PAD * a, HEAD_PAD * (a + 1)) for a in range(hps)]
        pairs = [slice(2 * MLA_V * p, 2 * MLA_V * (p + 1)) for p in range(hps // 2)]
        for pr in range(hps // 2):
            vpair = v_ref[:, pairs[pr]]
            va_scr[2 * pr] = jnp.where(lane < MLA_V, vpair, jnp.zeros_like(vpair))
            va_scr[2 * pr + 1] = jnp.where(lane >= MLA_V, vpair, jnp.zeros_like(vpair))
        dk_ref[...] = jnp.zeros_like(dk_ref)
        dv_scr[...] = jnp.zeros_like(dv_scr)

        def step(i, masked):
            start = pl.multiple_of(i * tq, tq)
            rows = pl.ds(start, tq)
            scores = [_mm_nt(k_ref[:, heads[a]], q_ref[rows, heads[a]]) for a in range(hps)]
            dps = [_mm_nt(va_scr[a], do_ref[rows, pairs[a // 2]]) for a in range(hps)]
            for a in range(hps):
                pr, r = a // 2, a % 2
                p = jnp.exp2(scores[a] - lse_ref[pr, r:r + 1, rows])
                if masked:
                    kk = lax.broadcasted_iota(jnp.int32, (tq, tq), 0)
                    qq = lax.broadcasted_iota(jnp.int32, (tq, tq), 1)
                    p = jnp.where(kk <= qq, p, 0.0)
                ds = p * (dps[a] - d_ref[pr, r:r + 1, rows])
                dv_scr[a] += _mm(p, do_ref[rows, pairs[pr]])
                dk_ref[:, heads[a]] += _mm(ds, q_ref[rows, heads[a]])
                dqt_ref[heads[a], rows] += _mm(kt_ref[heads[a], :], ds)

        def loop_body(i, _):
            step(i, False)
            return 0

        step(j, True)
        lax.fori_loop(j + 1, nq, loop_body, 0)
        for pr in range(hps // 2):
            dv_ref[:, pairs[pr]] = jnp.where(lane < MLA_V, dv_scr[2 * pr], dv_scr[2 * pr + 1])
        dk_ref[...] = dk_ref[...] * (ATTN_SCALE / ATTN_SCALE_LOG2)

        if ns:
            @pl.when(step_no == steps - 1)
            def _():
                for cp in _chip_swap_copies(rest[:ns], rest[ns + 3:2 * ns + 3], *rest[2 * ns + 5:]):
                    cp.wait()

    stat = pl.BlockSpec((hps // 2, 2, T), lambda g, j: (g, 0, 0))
    return pl.pallas_call(
        body, name="attn_bwd", grid=(MLA_HEADS // hps, nq),
        in_specs=[pl.BlockSpec((T, hps * HEAD_PAD), lambda g, j: (0, g)),
                  pl.BlockSpec((tq, hps * HEAD_PAD), lambda g, j: (j, g)),
                  pl.BlockSpec((hps * HEAD_PAD, tq), lambda g, j: (g, j)),
                  pl.BlockSpec((tq, hps * MLA_V), lambda g, j: (j, g)),
                  pl.BlockSpec((T, hps * MLA_V), lambda g, j: (0, g)), stat, stat] + [ANY] * ns,
        out_specs=[pl.BlockSpec((hps * HEAD_PAD, T), lambda g, j: (g, 0)),
                   pl.BlockSpec((tq, hps * HEAD_PAD), lambda g, j: (j, g)),
                   pl.BlockSpec((tq, hps * MLA_V), lambda g, j: (j, g))] + [ANY] * ns,
        out_shape=[jax.ShapeDtypeStruct((MLA_HEADS * HEAD_PAD, T), F32),
                   jax.ShapeDtypeStruct((T, MLA_HEADS * HEAD_PAD), F32),
                   jax.ShapeDtypeStruct((T, MLA_WIDTH), F32)] + _chip_swap_shapes(send),
        scratch_shapes=[pltpu.VMEM((hps, tq, 2 * MLA_V), vb.dtype), pltpu.VMEM((hps, tq, 2 * MLA_V), F32)]
        + ([pltpu.SemaphoreType.DMA((3 * ns,)), pltpu.SemaphoreType.DMA((3 * ns,))] if ns else []),
        compiler_params=_params(("arbitrary", "arbitrary")),
    )(qb, kb, kt, vb, dob, lse, dvec, *send)


def _cumsum_rows(x):
    n = x.shape[0]
    row = lax.broadcasted_iota(jnp.int32, x.shape, 0)
    s = 1
    while s < n:
        x = x + jnp.where(row >= s, pltpu.roll(x, s, 0), 0.0)
        s *= 2
    return x


def _rev_cumsum_rows(x):
    n = x.shape[0]
    row = lax.broadcasted_iota(jnp.int32, x.shape, 0)
    s = 1
    while s < n:
        x = x + jnp.where(row < n - s, pltpu.roll(x, n - s, 0), 0.0)
        s *= 2
    return x


def _lb_from_logits(l):
    l0, l1 = l[0:1, :], l[1:2, :]
    m = jnp.maximum(l0, l1)
    e0, e1 = jnp.exp(l0 - m), jnp.exp(l1 - m)
    return e0 / (e0 + e1)


def _hgrn_gates(hq, hf, lb):
    sig_f = jax.nn.sigmoid(hf)
    f = lb + (1.0 - lb) * sig_f
    sig_q = jax.nn.sigmoid(hq)
    return sig_f, f, jnp.log(f), 1.0 - f, sig_q, hq * sig_q


def _hgrn_intra(q, kk, b, exact=False):
    row = lax.broadcasted_iota(jnp.int32, b.shape, 0)
    qs, ks, eqs, eks, a_rows = [], [], [], [], []
    for i in range(CHUNK // SUB):
        ref = b[SUB * i + SUB // 2:SUB * i + SUB // 2 + 1, :]
        eq = jnp.exp(b[SUB * i:SUB * (i + 1), :] - ref)
        ek = jnp.exp(jnp.where(row < SUB * (i + 1), ref - b, NEG_BIG))
        qi = q[SUB * i:SUB * (i + 1), :] * eq
        ki = kk * ek
        a_rows.append(_mm_nt(qi, ki, exact))
        qs.append(qi), ks.append(ki), eqs.append(eq), eks.append(ek)
    tt = lax.broadcasted_iota(jnp.int32, (CHUNK, CHUNK), 0)
    ss = lax.broadcasted_iota(jnp.int32, (CHUNK, CHUNK), 1)
    causal = ss <= tt
    a = jnp.where(causal, jnp.concatenate(a_rows, axis=0), 0.0)
    return a, causal, qs, ks, eqs, eks


def _hgrn_fwd(xph, lbl, tg=512):
    T = xph.shape[0]
    ng, ncg = T // tg, tg // CHUNK
    cols = [slice(HGRN_DIM * h, HGRN_DIM * (h + 1)) for h in range(HGRN_HEADS)]

    def body(lbl_ref, hq_ref, hf_ref, hi_ref, o_ref, st_ref, s_scr):
        @pl.when(pl.program_id(0) == 0)
        def _():
            s_scr[...] = jnp.zeros_like(s_scr)

        lb = _lb_from_logits(lbl_ref[...])

        def chunks(it, _):
            pre = []
            for k in range(HGRN_CPI):
                c = it * HGRN_CPI + k
                rows = pl.ds(pl.multiple_of(c * CHUNK, CHUNK), CHUNK)
                for cs in cols:
                    _, _, lf, kk, _, q = _hgrn_gates(hq_ref[rows, cs], hf_ref[rows, cs], lb[:, cs])
                    v = hi_ref[rows, cs]
                    b = _cumsum_rows(lf)
                    a = _hgrn_intra(q, kk, b)[0]
                    b_last = b[CHUNK - 1:CHUNK, :]
                    pre.append((c, rows, q * jnp.exp(b), a, v, jnp.exp(b_last), _mm_tn(v, kk * jnp.exp(b_last - b))))
            for i, (c, rows, qe, a, v, ebl, upd) in enumerate(pre):
                h = i % HGRN_HEADS
                st = s_scr[h]
                st_ref[h, c] = st
                o_ref[rows, cols[h]] = _mm_nt(qe, st) + _mm(a, v)
                s_scr[h] = st * ebl + upd
            return 0

        lax.fori_loop(0, ncg // HGRN_CPI, chunks, 0)

    col = lambda k: pl.BlockSpec((tg, HGRN_WIDTH), lambda g: (g, k))
    return pl.pallas_call(
        body, name="hgrn_fwd", grid=(ng,),
        in_specs=[pl.BlockSpec((2, HGRN_WIDTH), lambda g: (0, 0)), col(0), col(1), col(2)],
        out_specs=[col(0), pl.BlockSpec((HGRN_HEADS, ncg, HGRN_DIM, HGRN_DIM), lambda g: (0, g, 0, 0))],
        out_shape=[jax.ShapeDtypeStruct((T, HGRN_WIDTH), F32),
                   jax.ShapeDtypeStruct((HGRN_HEADS, T // CHUNK, HGRN_DIM, HGRN_DIM), F32)],
        scratch_shapes=[pltpu.VMEM((HGRN_HEADS, HGRN_DIM, HGRN_DIM), F32)],
        compiler_params=_params(("arbitrary",)),
    )(lbl, xph, xph, xph)


def _hgrn_bwd(xph, lbl, states, d_o, fill=(), tg=512):
    T = xph.shape[0]
    ng, ncg = T // tg, tg // CHUNK
    cols = [slice(HGRN_DIM * h, HGRN_DIM * (h + 1)) for h in range(HGRN_HEADS)]
    nsub = CHUNK // SUB
    nf = len(fill)

    def body(lbl_ref, hq_ref, hf_ref, hi_ref, st_ref, do_ref, *rest):
        dhq_ref, dhf_ref, dhi_ref, dlg_ref = rest[nf:nf + 4]
        ds_scr, dlb_scr = rest[2 * nf + 4:2 * nf + 6]
        fill_copies = lambda: _pair_fill_copies(rest[nf + 4:2 * nf + 4], *rest[2 * nf + 6:])
        g = pl.program_id(0)

        @pl.when(g == 0)
        def _():
            ds_scr[...] = jnp.zeros_like(ds_scr)
            dlb_scr[...] = jnp.zeros_like(dlb_scr)
            for cp in (fill_copies()[0] if nf else ()):
                cp.start()

        lb = _lb_from_logits(lbl_ref[...])

        def chunks(it, _):
            pre = []
            for k, h in ((k, h) for k in range(HGRN_CPI) for h in range(HGRN_HEADS)):
                cs = cols[h]
                c = ncg - 1 - (it * HGRN_CPI + k)
                rows = pl.ds(pl.multiple_of(c * CHUNK, CHUNK), CHUNK)
                hq = hq_ref[rows, cs]
                sig_f, f, lf, kk, sig_q, q = _hgrn_gates(hq, hf_ref[rows, cs], lb[:, cs])
                v = hi_ref[rows, cs]
                do = do_ref[rows, cs]
                b = _cumsum_rows(lf)
                eb = jnp.exp(b)
                a, causal, qs, ks, eqs, eks = _hgrn_intra(q, kk, b)
                b_last = b[CHUNK - 1:CHUNK, :]
                st = st_ref[h, c]
                pre.append(dict(h=h, cs=cs, rows=rows, hq=hq, sig_f=sig_f, f=f, kk=kk, sig_q=sig_q, q=q, v=v, eb=eb, qs=qs,
                                ks=ks, eqs=eqs,
                                eks=eks, ebl=jnp.exp(b_last), el=jnp.exp(b_last - b), st=st,
                                da=jnp.where(causal, _mm_nt(do, v, True), 0.0), dq=_mm(do, st, True) * eb,
                                dv=_mm_tn(a, do), dsu=_mm_tn(do, q * eb, True)))
            for w in pre:
                dq_rows = []
                dk = jnp.zeros_like(w["q"])
                for i in range(nsub):
                    dai = w["da"][SUB * i:SUB * (i + 1), :]
                    dq_rows.append(_mm(dai, w["ks"][i], True) * w["eqs"][i])
                    dk = dk + _mm_tn(dai, w["qs"][i], True) * w["eks"][i]
                w["dq"] = w["dq"] + jnp.concatenate(dq_rows, axis=0)
                w["dk"] = dk
            for w in pre:
                h, cs, rows = w["h"], w["cs"], w["rows"]
                kk, el, ebl, dst = w["kk"], w["el"], w["ebl"], ds_scr[h]
                dk_state = _mm(w["v"], dst, True) * el
                dk = w["dk"] + dk_state
                e_last = (ebl * jnp.sum(w["st"] * dst, axis=0, keepdims=True)
                          + jnp.sum(kk * dk_state, axis=0, keepdims=True))
                dlf = _rev_cumsum_rows(w["q"] * w["dq"] - kk * dk) + e_last
                ds_scr[h] = dst * ebl + w["dsu"]
                df = dlf / w["f"] - dk
                sig_f, sig_q = w["sig_f"], w["sig_q"]
                dhf_ref[rows, cs] = df * (1.0 - lb[:, cs]) * sig_f * (1.0 - sig_f)
                dlb_scr[:, cs] += jnp.sum(df * (1.0 - sig_f), axis=0, keepdims=True)
                dhq_ref[rows, cs] = w["dq"] * sig_q * (1.0 + w["hq"] * (1.0 - sig_q))
                dhi_ref[rows, cs] = w["dv"] + _mm_nt(kk * el, dst)
            return 0

        lax.fori_loop(0, ncg // HGRN_CPI, chunks, 0)

        @pl.when(g == ng - 1)
        def _():
            dl0 = dlb_scr[...] * lb * (1.0 - lb)
            dlg_ref[...] = jnp.concatenate([dl0, -dl0], axis=0)
            if nf:
                copies, waits = fill_copies()
                for w in waits:
                    w.wait_recv()
                for cp in copies:
                    cp.wait_send()

    col = lambda k: pl.BlockSpec((tg, HGRN_WIDTH), lambda g: (ng - 1 - g, k))
    logits = pl.BlockSpec((2, HGRN_WIDTH), lambda g: (0, 0))
    big = jax.ShapeDtypeStruct((T, HGRN_WIDTH), F32)
    n_in, n_out = 6, 4
    return pl.pallas_call(
        body, name="hgrn_bwd", grid=(ng,),
        in_specs=[logits, col(0), col(1), col(2),
                  pl.BlockSpec((HGRN_HEADS, ncg, HGRN_DIM, HGRN_DIM), lambda g: (0, ng - 1 - g, 0, 0)), col(0)] + [ANY] * nf,
        out_specs=[col(0), col(0), col(0), logits] + [ANY] * nf,
        out_shape=[big, big, big, jax.ShapeDtypeStruct((2, HGRN_WIDTH), F32)]
        + [jax.ShapeDtypeStruct(f.shape, f.dtype) for f in fill],
        input_output_aliases={n_in + k: n_out + k for k in range(nf)},
        scratch_shapes=[pltpu.VMEM((HGRN_HEADS, HGRN_DIM, HGRN_DIM), F32), pltpu.VMEM((1, HGRN_WIDTH), F32)]
        + ([pltpu.SemaphoreType.DMA((nf,)), pltpu.SemaphoreType.DMA((nf,))] if nf else []),
        compiler_params=_params(("arbitrary",)),
    )(lbl, xph, xph, xph, states, d_o, *fill)


def _proj_fwd(x, o_raw, oh_raw, xph, wout, w_mla, w_hg, w_post, w_fpre, tt=512):
    T = x.shape[0]

    def body(x_ref, o_ref, oh_ref, hg_ref, wout_ref, wmla_ref, whg_ref, wpost_ref, wfpre_ref,
             h1_ref, y1_ref, z_ref, mix_ref):
        om, _, _ = _grms_fwd(o_ref[...], wmla_ref[...], MLA_V)
        hg = hg_ref[...]
        ohn, _, _ = _grms_fwd(oh_ref[...], whg_ref[...], HGRN_DIM)
        mix = jnp.concatenate([om, ohn * (hg * jax.nn.sigmoid(hg))], axis=-1)
        mix_ref[...] = mix.astype(mix_ref.dtype)
        y1 = _mm(mix, wout_ref[...])
        y1_ref[...] = y1
        h1 = x_ref[...] + _rms_fwd(y1, wpost_ref[...])[0]
        h1_ref[...] = h1
        z_ref[...] = _rms_fwd(h1, wfpre_ref[...])[0].astype(z_ref.dtype)

    row = lambda w: pl.BlockSpec((tt, w), lambda i: (i, 0))
    full = lambda a: pl.BlockSpec(a.shape, lambda i: (0,) * a.ndim)
    sds = jax.ShapeDtypeStruct
    return pl.pallas_call(
        body, name="proj_fwd", grid=(T // tt,),
        in_specs=[row(D_MODEL), row(MLA_WIDTH), row(HGRN_WIDTH), pl.BlockSpec((tt, HGRN_WIDTH), lambda i: (i, 3)),
                  full(wout), full(w_mla), full(w_hg), full(w_post), full(w_fpre)],
        out_specs=[row(D_MODEL)] * 4,
        out_shape=[sds((T, D_MODEL), F32), sds((T, D_MODEL), F32), sds((T, D_MODEL), MXU_DTYPE),
                   sds((T, D_MODEL), MXU_DTYPE)],
        compiler_params=_params(("arbitrary",)),
    )(x, o_raw, oh_raw, xph, wout, w_mla, w_hg, w_post, w_fpre)


def _ffn_fwd(zb, h1, tgt, w_fpost, wg, wu, wd, tt=256):
    T = zb.shape[0]
    nj = N_CHIPS

    def body(z_ref, h1_ref, tgt_ref, wfpost_ref, wg_ref, wu_ref, wd_ref, g_ref, up_ref, dy2_ref, dh2_ref, loss_ref, dwf_ref):
        @pl.when(pl.program_id(0) == 0)
        def _():
            loss_ref[...] = jnp.zeros_like(loss_ref)
            dwf_ref[...] = jnp.zeros_like(dwf_ref)

        z = z_ref[...]
        gs = [_mm_nt(z, wg_ref[j]) for j in range(nj)]
        ups = [_mm_nt(z, wu_ref[j]) for j in range(nj)]
        y2 = jnp.zeros((tt, D_MODEL), F32)
        for j in range(nj):
            g_ref[j] = gs[j]
            up_ref[j] = ups[j]
            y2 = y2 + _mm(gs[j] * jax.nn.sigmoid(gs[j]) * ups[j], wd_ref[j])
        w = wfpost_ref[...]
        y2s, y2n, r2 = _rms_fwd(y2, w)
        e = h1_ref[...] + y2s - tgt_ref[...]
        loss_ref[...] += jnp.sum(e * e, axis=0, keepdims=True)
        dh2 = e * (1.0 / D_MODEL)
        dh2_ref[...] = dh2
        dy2, dwf = _rms_bwd(dh2, y2n, r2, w)
        dy2_ref[...] = dy2.astype(dy2_ref.dtype)
        dwf_ref[...] += dwf

    row = pl.BlockSpec((tt, D_MODEL), lambda i: (i, 0))
    vec = pl.BlockSpec((1, D_MODEL), lambda i: (0, 0))
    resident = pl.BlockSpec((nj, FF_SHARD, D_MODEL), lambda i: (0, 0, 0), pipeline_mode=pl.Buffered(1))
    act = pl.BlockSpec((nj, tt, FF_SHARD), lambda i: (0, i, 0))
    sds = jax.ShapeDtypeStruct
    return pl.pallas_call(
        body, name="ffn_fwd", grid=(T // tt,),
        in_specs=[row, row, row, vec, resident, resident, resident],
        out_specs=[act, act, row, row, vec, vec],
        out_shape=[sds((nj, T, FF_SHARD), F32), sds((nj, T, FF_SHARD), F32), sds((T, D_MODEL), MXU_DTYPE),
                   sds((T, D_MODEL), F32), sds((1, D_MODEL), F32), sds((1, D_MODEL), F32)],
        compiler_params=_params(("arbitrary",)),
    )(zb, h1, tgt, w_fpost, wg, wu, wd)


def _ffn_bwd(zb, g, up, dy2b, wg, wu, wd, tt=512):
    T = zb.shape[0]
    nj = N_CHIPS

    def body(z_ref, g_ref, up_ref, dy2_ref, wg_ref, wu_ref, wd_ref, dwg_ref, dwu_ref, dwd_ref, dz_ref):
        @pl.when(pl.program_id(1) == 0)
        def _():
            dwg_ref[...] = jnp.zeros_like(dwg_ref)
            dwu_ref[...] = jnp.zeros_like(dwu_ref)
            dwd_ref[...] = jnp.zeros_like(dwd_ref)

        z, g_, up_, dy2 = z_ref[...], g_ref[0], up_ref[0], dy2_ref[...]
        sg = jax.nn.sigmoid(g_)
        act = g_ * sg
        dff = _mm_nt(dy2, wd_ref[0])
        dwd_ref[0] += _mm_tn(act * up_, dy2)
        dg = dff * up_ * sg * (1.0 + g_ * (1.0 - sg))
        dup = dff * act
        dwg_ref[0] += _mm_tn(dg, z)
        dwu_ref[0] += _mm_tn(dup, z)
        dz_ref[0] = _mm(dg, wg_ref[0]) + _mm(dup, wu_ref[0])

    row = pl.BlockSpec((tt, D_MODEL), lambda j, i: (i, 0))
    act = pl.BlockSpec((1, tt, FF_SHARD), lambda j, i: (j, i, 0))
    w_sh = pl.BlockSpec((1, FF_SHARD, D_MODEL), lambda j, i: (j, 0, 0))
    w_grad = jax.ShapeDtypeStruct((nj, FF_SHARD, D_MODEL), F32)
    return pl.pallas_call(
        body, name="ffn_bwd", grid=(nj, T // tt),
        in_specs=[row, act, act, row, w_sh, w_sh, w_sh],
        out_specs=[w_sh, w_sh, w_sh, pl.BlockSpec((1, tt, D_MODEL), lambda j, i: (j, i, 0))],
        out_shape=[w_grad, w_grad, w_grad, jax.ShapeDtypeStruct((nj, T, D_MODEL), F32)],
        compiler_params=_params(("arbitrary", "arbitrary")),
    )(zb, g, up, dy2b, wg, wu, wd)


def _mid_bwd(dzp, dh2, h1, y1, mixb, o_raw, oh_raw, xph, wout, w_fpre, w_post, w_mla, w_hg, swap=(), tt=256):
    T = dh2.shape[0]
    nsw = len(swap)
    n_in, n_out = 13, 10

    def body(*refs):
        (dzp_ref, dh2_ref, h1_ref, y1_ref, mix_ref, o_ref, oh_ref, hg_ref, wout_ref, wfpre_ref, wpost_ref,
         wmla_ref, whg_ref) = refs[:n_in]
        (dh1_ref, dwout_ref, do_ref, doh_ref, dhg_ref, dvec_ref, dwfpre_ref, dwpost_ref, dwmla_ref,
         dwhg_ref) = refs[n_in + nsw:n_in + nsw + n_out]
        swap_copies = lambda: _pair_swap_copies(refs[n_in:n_in + nsw], refs[n_in + nsw + n_out:n_in + 2 * nsw + n_out],
                                                *refs[n_in + 2 * nsw + n_out:])

        @pl.when(pl.program_id(0) == 0)
        def _():
            for r in (dwout_ref, dwfpre_ref, dwpost_ref, dwmla_ref, dwhg_ref):
                r[...] = jnp.zeros_like(r)
            for cp in (swap_copies() if nsw else ()):
                cp.start()

        dz = dzp_ref[0] + dzp_ref[1] + dzp_ref[2] + dzp_ref[3]
        wfpre = wfpre_ref[...]
        _, h1n, r = _rms_fwd(h1_ref[...], wfpre)
        dh1_z, dwfpre = _rms_bwd(dz, h1n, r, wfpre)
        dwfpre_ref[...] += dwfpre
        dh1 = dh2_ref[...] + dh1_z
        dh1_ref[...] = dh1
        wpost = wpost_ref[...]
        _, y1n, r1 = _rms_fwd(y1_ref[...], wpost)
        dy1, dwpost = _rms_bwd(dh1, y1n, r1, wpost)
        dwpost_ref[...] += dwpost
        dmix = _mm_nt(dy1, wout_ref[...])
        dwout_ref[...] += _mm_tn(mix_ref[...], dy1)
        wmla = wmla_ref[...]
        o = o_ref[...]
        _, on, ro = _grms_fwd(o, wmla, MLA_V)
        d_o, dwmla = _grms_bwd(dmix[:, :MLA_WIDTH], on, ro, wmla, MLA_V)
        dwmla_ref[...] += dwmla
        do_ref[...] = d_o.astype(do_ref.dtype)
        hh = lax.broadcasted_iota(jnp.int32, (MLA_HEADS, MLA_WIDTH), 0)
        ll = lax.broadcasted_iota(jnp.int32, (MLA_HEADS, MLA_WIDTH), 1)
        sel = jnp.where((ll >= hh * MLA_V) & (ll < (hh + 1) * MLA_V), 1.0, 0.0)
        dvec_ref[...] = _mm_nt(sel, d_o * o, True)
        whg = whg_ref[...]
        hg = hg_ref[...]
        sg = jax.nn.sigmoid(hg)
        _, ohn, rh = _grms_fwd(oh_ref[...], whg, HGRN_DIM)
        dmh = dmix[:, MLA_WIDTH:]
        dhg_ref[...] = dmh * ohn * whg * sg * (1.0 + hg * (1.0 - sg))
        d_oh, dwhg = _grms_bwd(dmh * (hg * sg), ohn, rh, whg, HGRN_DIM)
        dwhg_ref[...] += dwhg
        doh_ref[...] = d_oh

        if nsw:
            @pl.when(pl.program_id(0) == T // tt - 1)
            def _():
                for cp in swap_copies():
                    cp.wait()

    row = lambda w: pl.BlockSpec((tt, w), lambda i: (i, 0))
    full = lambda a: pl.BlockSpec(a.shape, lambda i: (0,) * a.ndim)
    vec = lambda w: pl.BlockSpec((1, w), lambda i: (0, 0))
    sds = jax.ShapeDtypeStruct
    return pl.pallas_call(
        body, name="mid_bwd", grid=(T // tt,),
        in_specs=[pl.BlockSpec((N_CHIPS, tt, D_MODEL), lambda i: (0, i, 0)), row(D_MODEL), row(D_MODEL), row(D_MODEL),
                  row(D_MODEL), row(MLA_WIDTH), row(HGRN_WIDTH), pl.BlockSpec((tt, HGRN_WIDTH), lambda i: (i, 3)),
                  full(wout), vec(D_MODEL), vec(D_MODEL), vec(MLA_WIDTH), vec(HGRN_WIDTH)] + [ANY] * nsw,
        out_specs=[row(D_MODEL), full(wout), row(MLA_WIDTH), row(HGRN_WIDTH), row(HGRN_WIDTH),
                   pl.BlockSpec((MLA_HEADS, tt), lambda i: (0, i)),
                   vec(D_MODEL), vec(D_MODEL), vec(MLA_WIDTH), vec(HGRN_WIDTH)] + [ANY] * nsw,
        out_shape=[sds((T, D_MODEL), F32), sds(wout.shape, F32), sds((T, MLA_WIDTH), MXU_DTYPE), sds((T, HGRN_WIDTH), F32),
                   sds((T, HGRN_WIDTH), F32), sds((MLA_HEADS, T), F32),
                   sds((1, D_MODEL), F32), sds((1, D_MODEL), F32), sds((1, MLA_WIDTH), F32), sds((1, HGRN_WIDTH), F32)]
        + _half_stack_shapes(swap),
        scratch_shapes=[pltpu.SemaphoreType.DMA((nsw,)), pltpu.SemaphoreType.DMA((nsw,))] if nsw else [],
        compiler_params=_params(("arbitrary",)),
    )(dzp, dh2, h1, y1, mixb, o_raw, oh_raw, xph, wout, w_fpre, w_post, w_mla, w_hg, *swap)


def _in_bwd(x, dh1, cq, ckv, dq, dk, dv, dhq, dhf, dhi, dhg, rc, rs, w_pre, win, qnw, wq, kvnw, wk, wv, tt=256):
    T = x.shape[0]

    def body(x_ref, dh1_ref, cq_ref, ckv_ref, dq_ref, dk_ref, dv_ref, dhq_ref, dhf_ref, dhi_ref, dhg_ref, rc_ref, rs_ref,
             wpre_ref, win_ref, qnw_ref, wq_ref, kvnw_ref, wk_ref, wv_ref,
             dx_ref, dwin_ref, dwq_ref, dwk_ref, dwv_ref, dwpre_ref, dqnw_ref, dkvnw_ref):
        @pl.when(pl.program_id(0) == 0)
        def _():
            for r in (dwin_ref, dwq_ref, dwk_ref, dwv_ref, dwpre_ref, dqnw_ref, dkvnw_ref):
                r[...] = jnp.zeros_like(r)

        def add_win_grad(r, first):
            for arr0, n, chip, row0 in _win_grad_segments():
                if first <= arr0 and arr0 + n <= first + r.shape[0]:
                    dwin_ref[chip, row0:row0 + n, :] += r[arr0 - first:arr0 - first + n]

        lo = Q_RANK + KV_RANK + HEAD_PAD
        dxp_h = jnp.concatenate([dhq_ref[...], dhf_ref[...], dhi_ref[...], dhg_ref[...]], axis=-1)
        du = _mm(dxp_h, win_ref[lo:, :])
        wpre = wpre_ref[...]
        u, xn, rx = _rms_fwd(x_ref[...], wpre)
        add_win_grad(_mm_tn(dxp_h, u), lo)
        c, sa, sb = _rope_tables(rc_ref[...], rs_ref[...])
        lane = lax.broadcasted_iota(jnp.int32, (tt, HEAD_PAD), 1)
        dk_all = dk_ref[...]
        dq_lin = []
        dkr = jnp.zeros((tt, HEAD_PAD), F32)
        for h in range(MLA_HEADS):
            sl = slice(HEAD_PAD * h, HEAD_PAD * (h + 1))
            dq_lin.append(_rope_bwd(dq_ref[sl, :].T * ATTN_SCALE, c, sa, sb))
            dkr = dkr + dk_all[:, sl]
        dq_lin = jnp.concatenate(dq_lin, axis=-1)
        dkr = jnp.where((lane >= MLA_NOPE) & (lane < MLA_QK), _rope_bwd(dkr, c, sa, sb), 0.0)
        qnw = qnw_ref[...]
        qn, cqn, rq = _rms_fwd(cq_ref[...], qnw)
        dwq_ref[...] += _mm_tn(qn, dq_lin)
        dcq, dqnw = _rms_bwd(_mm_nt(dq_lin, wq_ref[...]), cqn, rq, qnw)
        dqnw_ref[...] += dqnw
        kvnw = kvnw_ref[...]
        kvn, ckvn, rkv = _rms_fwd(ckv_ref[...], kvnw)
        dv_ = dv_ref[...]
        dwk_ref[...] += _mm_tn(kvn, dk_all)
        dwv_ref[...] += _mm_tn(kvn, dv_)
        dckv, dkvnw = _rms_bwd(_mm_nt(dk_all, wk_ref[...]) + _mm_nt(dv_, wv_ref[...]), ckvn, rkv, kvnw)
        dkvnw_ref[...] += dkvnw
        dxp_a = jnp.concatenate([dcq, dckv, dkr], axis=-1)
        add_win_grad(_mm_tn(dxp_a, u), 0)
        dx_u, dwpre = _rms_bwd(du + _mm(dxp_a, win_ref[:lo, :]), xn, rx, wpre)
        dwpre_ref[...] += dwpre
        dx_ref[...] = dh1_ref[...] + dx_u

    row = lambda w: pl.BlockSpec((tt, w), lambda i: (i, 0))
    full = lambda a: pl.BlockSpec(a.shape, lambda i: (0,) * a.ndim)
    sds = jax.ShapeDtypeStruct
    qk_w = MLA_HEADS * HEAD_PAD
    return pl.pallas_call(
        body, name="in_bwd", grid=(T // tt,),
        in_specs=[row(D_MODEL), row(D_MODEL), row(Q_RANK), row(KV_RANK), pl.BlockSpec((qk_w, tt), lambda i: (0, i)),
                  row(qk_w), row(MLA_WIDTH),
                  row(HGRN_WIDTH), row(HGRN_WIDTH), row(HGRN_WIDTH), row(HGRN_WIDTH), row(HEAD_PAD), row(HEAD_PAD),
                  full(w_pre), full(win), full(qnw), full(wq), full(kvnw), full(wk), full(wv)],
        out_specs=[row(D_MODEL), pl.BlockSpec(WIN_COMM_SHAPE, lambda i: (0, 0, 0)), full(wq), full(wk), full(wv),
                   full(w_pre), full(qnw), full(kvnw)],
        out_shape=[sds((T, D_MODEL), F32), sds(WIN_COMM_SHAPE, F32), sds(wq.shape, F32), sds(wk.shape, F32),
                   sds(wv.shape, F32), sds(w_pre.shape, F32), sds(qnw.shape, F32), sds(kvnw.shape, F32)],
        compiler_params=_params(("arbitrary",)),
    )(x, dh1, cq, ckv, dq, dk, dv, dhq, dhf, dhi, dhg, rc, rs, w_pre, win, qnw, wq, kvnw, wk, wv)


def _arrange_weights(win_t, wuq_full, wukv):
    dt = win_t.dtype
    z = lambda n: jnp.zeros((n, D_MODEL), dt)
    s2 = Q_RANK + KV_RANK
    win_arr = jnp.concatenate([win_t[:s2], z(MLA_NOPE), win_t[s2:s2 + MLA_ROPE], z(HEAD_PAD - MLA_QK),
                               win_t[s2 + MLA_ROPE:]], axis=0)
    wq_arr = jnp.pad(wuq_full, ((0, 0), (0, 0), (0, HEAD_PAD - MLA_QK))).reshape(Q_RANK, MLA_HEADS * HEAD_PAD)
    wk_arr = jnp.pad(wukv[:, :, :MLA_NOPE], ((0, 0), (0, 0), (0, HEAD_PAD - MLA_NOPE))).reshape(
        KV_RANK, MLA_HEADS * HEAD_PAD)
    wv_arr = wukv[:, :, MLA_NOPE:].reshape(KV_RANK, MLA_WIDTH)
    return win_arr, wq_arr, wk_arr, wv_arr


WIN_COMM_SHAPE = (N_CHIPS, FF_SHARD, D_MODEL)


def _win_grad_segments():
    s2 = Q_RANK + KV_RANK
    runs = [(0, s2, 0), (s2, s2 + MLA_ROPE, MLA_NOPE), (s2 + MLA_ROPE, D_IN, HEAD_PAD - MLA_ROPE)]
    per = D_IN // N_CHIPS
    segs = []
    for lo, hi, shift in runs:
        for k in range(N_CHIPS):
            a, b = max(lo, per * k), min(hi, per * (k + 1))
            if a < b:
                segs.append((a + shift, b - a, k, a - per * k))
    return segs


def _unarrange_grads(dwq_arr, dwk_arr, dwv_arr):
    dwuq = dwq_arr.reshape(Q_RANK, MLA_HEADS, HEAD_PAD)[:, :, :MLA_QK]
    dwukv = jnp.concatenate([dwk_arr.reshape(KV_RANK, MLA_HEADS, HEAD_PAD)[:, :, :MLA_NOPE],
                             dwv_arr.reshape(KV_RANK, MLA_HEADS, MLA_V)], axis=-1)
    return dwuq, dwukv


def _rope_inv_freq():
    inv = 1.0 / (ROPE_THETA ** (jnp.arange(0, MLA_ROPE, 2, dtype=F32) / MLA_ROPE))
    z = lambda n: jnp.zeros((n,), F32)
    return jnp.concatenate([z(MLA_NOPE), inv, inv, z(HEAD_PAD - MLA_QK)]).reshape(1, HEAD_PAD)


def _local_step(x, pos, tgt, small, win_arr, wq_arr, wk_arr, wv_arr, late, place=None):
    invf = _rope_inv_freq()
    cq, ckv, xph, qb, kb, vb, kt, vt, rc, rs = _in_fwd(x, pos, invf, small["attn_pre_norm"], win_arr, small["mla_q_norm"],
                                               wq_arr, small["mla_kv_norm"], wk_arr, wv_arr)
    if place is None:
        o_raw, lse = _attn_fwd_t(qb, kb, vt)
        wout, wg, wu, wd = late
    else:
        o_raw, lse, *stacks = _attn_fwd_t(qb, kb, vt, gather=late)
        wout, wg, wu, wd = [lax.dynamic_update_slice(s, l[None], (place[1], 0, 0)) for s, l in zip(stacks, late)]
        wout = wout.reshape(D_MODEL, D_MODEL)
    oh_raw, states = _hgrn_fwd(xph, small["hgrn_lb_logits"])
    h1, y1, zb, mixb = _proj_fwd(x, o_raw, oh_raw, xph, wout, small["mla_out_norm"], small["hgrn_out_norm"],
                                 small["attn_post_norm"], small["ffn_pre_norm"])
    g, up, dy2b, dh2, loss_acc, d_fpost = _ffn_fwd(zb, h1, tgt, small["ffn_post_norm"], wg, wu, wd)
    dwg, dwu, dwd, dzp = _ffn_bwd(zb, g, up, dy2b, wg, wu, wd)
    ffn_grads = [] if place is None else [dwg, dwu, dwd]
    dh1, dwout, d_o, d_oh, dhg, dvec, d_fpre, d_post, d_mla, d_hg, *ffn_rs = _mid_bwd(
        dzp, dh2, h1, y1, mixb, o_raw, oh_raw, xph, wout, small["ffn_pre_norm"], small["attn_post_norm"],
        small["mla_out_norm"], small["hgrn_out_norm"], swap=ffn_grads)
    if ffn_grads:
        ffn_grads = ffn_grads + [dwout.reshape(N_CHIPS, D_MODEL // N_CHIPS, D_MODEL)]
        ffn_rs += _pair_swap(ffn_grads[3:], (), "pair_swap_w_out")
    ffn_ps = _pair_sum(place, ffn_grads, ffn_rs, name="pair_sum_ffn") if ffn_grads else []
    dq, dk, dv, *ffn_ris = _attn_bwd_t(qb, kb, kt, vb, d_o, lse, dvec.reshape(lse.shape), send=ffn_ps)
    ffn_sums = _chip_sum(place, ffn_grads, ffn_rs, ffn_ris, name="chip_sum_ffn") if ffn_grads else []
    dhq, dhf, dhi, d_lbl, *ffn_final = _hgrn_bwd(xph, small["hgrn_lb_logits"], states, d_oh, fill=ffn_sums)
    dx, dwin4, dwq_arr, dwk_arr, dwv_arr, d_pre, d_qn, d_kvn = _in_bwd(
        x, dh1, cq, ckv, dq, dk, dv, dhq, dhf, dhi, dhg, rc, rs, small["attn_pre_norm"], win_arr,
        small["mla_q_norm"], wq_arr, small["mla_kv_norm"], wk_arr, wv_arr)
    dwuq, dwukv = _unarrange_grads(dwq_arr, dwk_arr, dwv_arr)
    loss = 0.5 * jnp.sum(loss_acc) * (1.0 / D_MODEL)
    grads = dict(attn_pre_norm=d_pre, w_in=dwin4, mla_q_norm=d_qn, mla_w_uq=dwuq, mla_kv_norm=d_kvn, mla_w_ukv=dwukv,
                 mla_out_norm=d_mla, hgrn_lb_logits=d_lbl, hgrn_out_norm=d_hg, w_out=dwout, attn_post_norm=d_post,
                 ffn_pre_norm=d_fpre, w_gate=dwg, w_up=dwu, w_down=dwd, ffn_post_norm=d_fpost)
    if place is None:
        return loss, dx, grads
    return loss, dx, grads, ffn_final


def _place():
    x, y, c = lax.axis_index("x"), lax.axis_index("y"), lax.axis_index("c")
    others = [(1 - x, y), (x, 1 - y), (1 - x, 1 - y)]
    return x, y, c, 2 * x + y, (x, y, 1 - c), others


def _half(ref, c, rows):
    return ref.at[pl.ds(pl.multiple_of(c * rows, 8), rows)]


def _rcopy(src, dst, send, recv, k, to):
    return pltpu.make_async_remote_copy(src_ref=src, dst_ref=dst, send_sem=send.at[k], recv_sem=recv.at[k],
                                        device_id=to, device_id_type=MESH)


class _Gather:
    def __init__(self, ins, outs, send, recv):
        self.ins, self.outs, self.send, self.recv = ins, outs, send, recv
        self.n = len(ins)
        self.halves = [r.shape[0] // 2 for r in ins]
        _, _, self.c, self.me, self.sib, self.others = _place()

    def _each(self):
        for j, (px, py) in enumerate(self.others):
            for a in range(self.n):
                yield j * self.n + a, a, 2 * px + py, (px, py, self.c)

    def sends(self):
        return [_rcopy(_half(self.ins[a], self.c, self.halves[a]), _half(self.outs[a].at[self.me], self.c, self.halves[a]),
                       self.send, self.recv, k, to) for k, a, _, to in self._each()]

    def arrivals(self):
        parts = [(k, _half(self.outs[a].at[chip], self.c, self.halves[a]), to) for k, a, chip, to in self._each()]
        return [_rcopy(p, p, self.send, self.recv, k, to) for k, p, to in parts]

    def forwards(self):
        parts = [(k, _half(self.outs[a].at[chip], self.c, self.halves[a])) for k, a, chip, _ in self._each()]
        return [_rcopy(p, p, self.send, self.recv, 3 * self.n + k, self.sib) for k, p in parts]

    def forward_arrivals(self):
        parts = [(k, _half(self.outs[a].at[chip], 1 - self.c, self.halves[a])) for k, a, chip, _ in self._each()]
        return [_rcopy(p, p, self.send, self.recv, 3 * self.n + k, self.sib) for k, p in parts]

    @staticmethod
    def out_shapes(arrs):
        return [jax.ShapeDtypeStruct((N_CHIPS,) + a.shape, a.dtype) for a in arrs]

    @staticmethod
    def semaphores(arrs):
        return [pltpu.SemaphoreType.DMA((6 * len(arrs),)), pltpu.SemaphoreType.DMA((6 * len(arrs),))]


def _gather_chips(arrs, name):
    n = len(arrs)

    def body(*refs):
        gat = _Gather(refs[:n], refs[n:2 * n], *refs[2 * n:])
        sends, forwards = gat.sends(), gat.forwards()
        for cp in sends:
            cp.start()
        for arrival, fw in zip(gat.arrivals(), forwards):
            arrival.wait_recv()
            fw.start()
        for arrival in gat.forward_arrivals():
            arrival.wait_recv()
        for cp in sends + forwards:
            cp.wait_send()

    return pl.pallas_call(body, name=name, in_specs=[ANY] * n, out_specs=[ANY] * n, out_shape=_Gather.out_shapes(arrs),
                          scratch_shapes=_Gather.semaphores(arrs))(*arrs)


GRAD_BLOCKS = 2


def _pair_swap_copies(g_refs, r_refs, send, recv):
    _, _, c, _, sib, _ = _place()
    copies = []
    for a, (g, r) in enumerate(zip(g_refs, r_refs)):
        h = g.shape[1] // 2
        copies.append(_rcopy(g.at[:, pl.ds(pl.multiple_of((1 - c) * h, 8), h)], r, send, recv, a, sib))
    return copies


def _half_stack_shapes(gs, dtype=None):
    return [jax.ShapeDtypeStruct((N_CHIPS, g.shape[1] // 2, g.shape[2]), dtype or g.dtype) for g in gs]


def _pair_swap(gs, wholes, name):
    n, nw = len(gs), len(wholes)

    def body(*refs):
        ins, outs, (send, recv) = refs[:n + nw], refs[n + nw:2 * (n + nw)], refs[2 * (n + nw):]
        copies = _pair_swap_copies(ins[:n], outs[:n], send, recv)
        copies += [_rcopy(ins[n + k], outs[n + k], send, recv, n + k, _place()[4]) for k in range(nw)]
        for cp in copies:
            cp.start()
        for cp in copies:
            cp.wait()

    return pl.pallas_call(
        body, name=name, in_specs=[ANY] * (n + nw), out_specs=[ANY] * (n + nw),
        out_shape=_half_stack_shapes(gs) + [jax.ShapeDtypeStruct(w.shape, w.dtype) for w in wholes],
        scratch_shapes=[pltpu.SemaphoreType.DMA((n + nw,)), pltpu.SemaphoreType.DMA((n + nw,))],
    )(*gs, *wholes)


def _pair_sum(place, gs, rs, small=None, name="pair_sum"):
    n = len(gs)
    nb = GRAD_BLOCKS

    def body(place_ref, *refs):
        g_refs, r_refs, p_refs = refs[:n], refs[n:2 * n], refs[-n - 1:-1] if small else refs[-n:]
        for a in range(n):
            p_refs[a][0] = (g_refs[a][0] + r_refs[a][0]).astype(p_refs[a].dtype)
        if small:
            @pl.when((pl.program_id(0) == 0) & (pl.program_id(1) == 0))
            def _():
                refs[-1][...] = refs[2 * n][...] + refs[2 * n + 1][...]

    in_specs, out_specs = [], []
    for g in gs:
        blk = (1, g.shape[1] // 2 // nb, g.shape[2])
        in_specs.append(pl.BlockSpec(blk, lambda i, k, p: (k, p[0] * nb + i, 0)))
    for g in gs:
        blk = (1, g.shape[1] // 2 // nb, g.shape[2])
        in_specs.append(pl.BlockSpec(blk, lambda i, k, p: (k, i, 0)))
        out_specs.append(pl.BlockSpec(blk, lambda i, k, p: (k, i, 0)))
    out_shape = _half_stack_shapes(gs, BF16)
    if small:
        sm_spec = pl.BlockSpec(small[0].shape, lambda i, k, p: (0, 0))
        in_specs += [sm_spec, sm_spec]
        out_specs.append(sm_spec)
        out_shape.append(jax.ShapeDtypeStruct(small[0].shape, F32))
    return pl.pallas_call(
        body, name=name,
        grid_spec=pltpu.PrefetchScalarGridSpec(num_scalar_prefetch=1, grid=(nb, N_CHIPS), in_specs=in_specs,
                                               out_specs=out_specs),
        out_shape=out_shape,
        compiler_params=_params(("arbitrary", "arbitrary")),
    )(place, *gs, *rs, *(small or ()))


def _chip_swap_copies(p_refs, ri_refs, send, recv):
    _, _, c, _, _, others = _place()
    n = len(p_refs)
    return [_rcopy(p_refs[a].at[2 * px + py], ri_refs[a].at[j], send, recv, j * n + a, (px, py, c))
            for j, (px, py) in enumerate(others) for a in range(n)]


def _chip_swap_shapes(ps):
    return [jax.ShapeDtypeStruct((3,) + p.shape[1:], p.dtype) for p in ps]


def _chip_swap(ps, pair):
    n = len(ps)

    def body(*refs):
        start, finish = _chip_swap_plan(refs[:n], refs[n], refs[n + 1:2 * n + 1], refs[2 * n + 1], *refs[2 * n + 2:])
        start()
        finish()

    return pl.pallas_call(
        body, name="chip_swap", in_specs=[ANY] * (n + 1), out_specs=[ANY] * (n + 1),
        out_shape=_chip_swap_out_shapes(ps, pair), scratch_shapes=_chip_swap_semaphores(n),
    )(*ps, pair)


def _chip_swap_plan(p_refs, pair_ref, ri_refs, sm4_ref, send, recv, lsem):
    n = len(p_refs)
    hs = SMALL_ROWS // 2
    x, y, c, me, sib, others = _place()
    local = pltpu.make_async_copy(pair_ref, sm4_ref.at[me], lsem.at[0])
    copies = _chip_swap_copies(p_refs, ri_refs, send, recv)
    arrivals = list(copies)
    for j, (px, py) in enumerate(others):
        copies.append(_rcopy(_half(pair_ref, c, hs), _half(sm4_ref.at[me], c, hs), send, recv, 3 * n + j, (px, py, c)))
        part = _half(sm4_ref.at[2 * px + py], c, hs)
        arrivals.append(_rcopy(part, part, send, recv, 3 * n + j, (px, py, c)))

    def start():
        local.start()
        for cp in copies:
            cp.start()

    def finish():
        for arrival in arrivals:
            arrival.wait_recv()
        for cp in copies:
            cp.wait_send()
        local.wait()

    return start, finish


def _chip_swap_out_shapes(ps, pair):
    return _chip_swap_shapes(ps) + [jax.ShapeDtypeStruct((N_CHIPS,) + pair.shape, pair.dtype)]


def _chip_swap_semaphores(n):
    k = 3 * (n + 1)
    return [pltpu.SemaphoreType.DMA((k,)), pltpu.SemaphoreType.DMA((k,)), pltpu.SemaphoreType.DMA((1,))]


def _chip_sum(place, gs, rs, ris, name="chip_sum"):
    n = len(gs)
    nb = GRAD_BLOCKS

    def body(place_ref, *refs):
        g_refs, r_refs, ri_refs, o_refs = refs[:n], refs[n:2 * n], refs[2 * n:3 * n], refs[3 * n:]
        for a in range(n):
            ri = ri_refs[a]
            o_refs[a][...] = (g_refs[a][0] + r_refs[a][0]) + ri[0].astype(F32) + ri[1].astype(F32) + ri[2].astype(F32)

    in_specs, out_specs, out_shape = [], [], []
    for g in gs:
        blk = (1, g.shape[1] // 2 // nb, g.shape[2])
        in_specs.append(pl.BlockSpec(blk, lambda i, p: (p[1], p[0] * nb + i, 0)))
    for g in gs:
        blk = (1, g.shape[1] // 2 // nb, g.shape[2])
        in_specs.append(pl.BlockSpec(blk, lambda i, p: (p[1], i, 0)))
    for g in gs:
        rb = g.shape[1] // 2 // nb
        in_specs.append(pl.BlockSpec((3, rb, g.shape[2]), lambda i, p: (0, i, 0)))
        out_specs.append(pl.BlockSpec((rb, g.shape[2]), lambda i, p: (p[0] * nb + i, 0)))
        out_shape.append(jax.ShapeDtypeStruct(g.shape[1:], F32))
    return pl.pallas_call(
        body, name=name,
        grid_spec=pltpu.PrefetchScalarGridSpec(num_scalar_prefetch=1, grid=(nb,), in_specs=in_specs, out_specs=out_specs),
        out_shape=out_shape,
        compiler_params=_params(("arbitrary",)),
    )(place, *gs, *rs, *ris)


def _pair_fill_copies(g_refs, send, recv):
    _, _, c, _, sib, _ = _place()
    copies, waits = [], []
    for a, g in enumerate(g_refs):
        h = g.shape[0] // 2
        mine, theirs = _half(g, c, h), _half(g, 1 - c, h)
        copies.append(_rcopy(mine, mine, send, recv, a, sib))
        waits.append(_rcopy(theirs, theirs, send, recv, a, sib))
    return copies, waits


def _pair_fill(gfs, sm4):
    n = len(gfs)
    hs = SMALL_ROWS // 2

    def body(*refs):
        g_refs, sm4_ref = refs[n + 1:2 * n + 1], refs[2 * n + 1]
        send, recv = refs[2 * n + 2:]
        x, y, c, me, sib, others = _place()
        copies, waits = _pair_fill_copies(g_refs, send, recv)
        for j, (px, py) in enumerate(others):
            chip = 2 * px + py
            mine, theirs = _half(sm4_ref.at[chip], c, hs), _half(sm4_ref.at[chip], 1 - c, hs)
            copies.append(pltpu.make_async_remote_copy(src_ref=mine, dst_ref=mine, send_sem=send.at[n + j],
                                                       recv_sem=recv.at[n + j], device_id=sib, device_id_type=MESH))
            waits.append(pltpu.make_async_remote_copy(src_ref=theirs, dst_ref=theirs, send_sem=send.at[n + j],
                                                      recv_sem=recv.at[n + j], device_id=sib, device_id_type=MESH))
        for cp in copies:
            cp.start()
        for w in waits:
            w.wait_recv()
        for cp in copies:
            cp.wait_send()

    return pl.pallas_call(
        body, name="pair_fill", in_specs=[ANY] * (n + 1), out_specs=[ANY] * (n + 1),
        out_shape=[jax.ShapeDtypeStruct(g.shape, g.dtype) for g in gfs] + [jax.ShapeDtypeStruct(sm4.shape, sm4.dtype)],
        input_output_aliases={i: i for i in range(n + 1)},
        scratch_shapes=[pltpu.SemaphoreType.DMA((n + 3,)), pltpu.SemaphoreType.DMA((n + 3,))],
    )(*gfs, sm4)


def _adamw_math(w, g, m, v):
    m = ADAM_B1 * m + (1.0 - ADAM_B1) * g
    v = ADAM_B2 * v + (1.0 - ADAM_B2) * (g * g)
    m_hat = m / (1.0 - ADAM_B1 ** ADAM_STEP)
    v_hat = v / (1.0 - ADAM_B2 ** ADAM_STEP)
    return -ADAM_LR * (m_hat / (jnp.sqrt(v_hat) + ADAM_EPS) + ADAM_WD * w), m, v


def _adamw(items, steps, name):
    n = len(items)

    def body(*refs):
        for a in range(n):
            g = refs[4 * a + 1][...]
            d, mo, vo = _adamw_math(refs[4 * a][...], g, refs[4 * a + 2][...], refs[4 * a + 3][...])
            for out, val in zip(refs[4 * n + 4 * a:4 * n + 4 * a + 4], (g, d, mo, vo)):
                out[...] = val

    spec = lambda w: pl.BlockSpec((w.shape[0] // steps, w.shape[1]), lambda i: (i, 0))
    flat = pl.pallas_call(
        body, name=name, grid=(steps,), in_specs=[spec(it[0]) for it in items for _ in range(4)],
        out_specs=[spec(it[0]) for it in items for _ in range(4)],
        out_shape=[jax.ShapeDtypeStruct(it[0].shape, F32) for it in items for _ in range(4)],
        compiler_params=_params(("arbitrary",)),
    )(*[a for it in items for a in it])
    return [flat[4 * a:4 * a + 4] for a in range(n)]


def _adamw_small(sm4, wmv):
    views = SMALL_VIEWS[:-1]
    n = len(views)

    def body(sm4_ref, *refs):
        g_all = ((sm4_ref[0] + sm4_ref[1]) + sm4_ref[2]) + sm4_ref[3]
        row = 0
        for a, (_, rows, cols) in enumerate(views):
            g = g_all[row:row + rows, :cols]
            row += -(-rows // ROW_TILE) * ROW_TILE
            d, mo, vo = _adamw_math(refs[3 * a][...], g, refs[3 * a + 1][...], refs[3 * a + 2][...])
            for out, val in zip(refs[3 * n + 4 * a:3 * n + 4 * a + 4], (g, d, mo, vo)):
                out[...] = val
        refs[-1][...] = g_all[row:row + 1, :128]

    flat = pl.pallas_call(
        body, name="adamw_small",
        out_shape=[jax.ShapeDtypeStruct((rows, cols), F32) for _, rows, cols in views for _ in range(4)]
        + [jax.ShapeDtypeStruct((1, 128), F32)],
        compiler_params=pltpu.CompilerParams(vmem_limit_bytes=VMEM_LIMIT),
    )(sm4, *[a for t in wmv for a in t])
    return [flat[4 * a:4 * a + 4] for a in range(n)] + [flat[-1]]


SMALL_NAMES = ("attn_pre_norm", "mla_q_norm", "mla_kv_norm", "mla_w_ukv", "mla_out_norm", "hgrn_lb_logits",
               "hgrn_out_norm", "attn_post_norm", "ffn_pre_norm", "ffn_post_norm")
BIG_NAMES = ("w_in", "mla_w_uq", "w_out", "w_gate", "w_up", "w_down")
WEIGHT_NAMES = ("attn_pre_norm", "w_in", "mla_q_norm", "mla_w_uq", "mla_kv_norm", "mla_w_ukv", "mla_out_norm",
                "hgrn_lb_logits", "hgrn_out_norm", "w_out", "attn_post_norm", "ffn_pre_norm", "w_gate", "w_up", "w_down",
                "ffn_post_norm")


UQ_COMM_SHAPE = (192, 384)


def _pack_small(vals):
    parts = []
    for name, rows, cols in SMALL_VIEWS:
        pad_rows = -(-rows // ROW_TILE) * ROW_TILE - rows
        parts.append(jnp.pad(vals[name].reshape(rows, cols), ((0, pad_rows), (0, D_MODEL - cols))))
    return jnp.concatenate(parts, axis=0)


def kernel(x, positions, attn_pre_norm, w_in, mla_q_norm, mla_w_uq, mla_kv_norm, mla_w_ukv, mla_out_norm, hgrn_lb_logits, hgrn_out_norm, w_out, attn_post_norm, ffn_pre_norm, w_gate, w_up, w_down, ffn_post_norm, loss_target, m_attn_pre_norm, m_w_in, m_mla_q_norm, m_mla_w_uq, m_mla_kv_norm, m_mla_w_ukv, m_mla_out_norm, m_hgrn_lb_logits, m_hgrn_out_norm, m_w_out, m_attn_post_norm, m_ffn_pre_norm, m_w_gate, m_w_up, m_w_down, m_ffn_post_norm, v_attn_pre_norm, v_w_in, v_mla_q_norm, v_mla_w_uq, v_mla_kv_norm, v_mla_w_ukv, v_mla_out_norm, v_hgrn_lb_logits, v_hgrn_out_norm, v_w_out, v_attn_post_norm, v_ffn_pre_norm, v_w_gate, v_w_up, v_w_down, v_ffn_post_norm):
    args = locals()
    W = {n: args[n] for n in WEIGHT_NAMES}
    M = {n: args["m_" + n] for n in WEIGHT_NAMES}
    V = {n: args["v_" + n] for n in WEIGHT_NAMES}
    T = x.shape[1]
    cx, cy, cc = lax.axis_index("x"), lax.axis_index("y"), lax.axis_index("c")

    win_rows = D_IN // N_CHIPS
    shard2d = {"w_in": (win_rows, D_MODEL), "mla_w_uq": (Q_RANK // N_CHIPS, MLA_HEADS * MLA_QK),
               "w_out": (D_MODEL // N_CHIPS, D_MODEL), "w_gate": (FF_SHARD, D_MODEL), "w_up": (FF_SHARD, D_MODEL),
               "w_down": (FF_SHARD, D_MODEL)}
    transposed = ("w_in", "w_gate", "w_up")
    to2d = lambda n, a: a[0].T if n in transposed else a.reshape(shard2d[n])
    from2d = lambda n, t: t.T[None] if n in transposed else t.reshape(W[n].shape)
    me = 2 * cx + cy
    place = jnp.stack([cc, me]).astype(jnp.int32)
    local_b = [to2d(n, W[n]).astype(BF16) for n in BIG_NAMES]
    local_b[0] = jnp.pad(local_b[0], ((0, FF_SHARD - win_rows), (0, 0)))
    stacks = _gather_chips(local_b[:2], "gather_weights")
    win4, wuq4 = [lax.dynamic_update_slice(s, l[None], (me, 0, 0)) for s, l in zip(stacks, local_b)]
    win_t = win4[:, :win_rows].reshape(D_IN, D_MODEL)
    wuq_full = wuq4.reshape(Q_RANK, MLA_HEADS, MLA_QK)
    win_arr, wq_arr, wk_arr, wv_arr = _arrange_weights(win_t, wuq_full, mla_w_ukv[0].astype(BF16))
    small = {n: W[n][0] if n == "mla_w_ukv" else W[n].reshape(-1, W[n].shape[-1]) for n in SMALL_NAMES}

    loss_local, dx, grads, ffn_final = _local_step(x[0], positions.reshape(T, 1), loss_target[0], small, win_arr,
                                                           wq_arr, wk_arr, wv_arr, local_b[2:], place)

    gs = [grads["w_in"], grads["mla_w_uq"].reshape((N_CHIPS,) + UQ_COMM_SHAPE)]
    sm = _pack_small({**grads, "loss": loss_local})
    *rs, ssib = _pair_swap(gs, (sm,), "pair_swap")
    *ps, pair = _pair_sum(place, gs, rs, small=(sm, ssib))
    ffn_names, rest_names = BIG_NAMES[3:], BIG_NAMES[:2]
    g2d = dict(zip(ffn_names + BIG_NAMES[2:3], ffn_final))
    adam_in = lambda names_: [(to2d(n, W[n]), g2d[n], to2d(n, M[n]), to2d(n, V[n])) for n in names_]
    updates = dict(zip(ffn_names, _adamw(adam_in(ffn_names), 8, "adamw_ffn")))
    updates.update(zip(BIG_NAMES[2:3], _adamw(adam_in(BIG_NAMES[2:3]), 8, "adamw_w_out")))
    *ris, sm4 = _chip_swap(ps, pair)
    *gfin, smf = _pair_fill(_chip_sum(place, gs, rs, ris), sm4)

    g2d.update({n: gfin[k].reshape((-1,) + shard2d[n][1:]) for k, n in enumerate(rest_names)})
    updates.update(zip(rest_names, _adamw(adam_in(rest_names), 3, "adamw_w_in")))
    G, DW, NM, NV = {}, {}, {}, {}
    for n, outs in updates.items():
        G[n], DW[n], NM[n], NV[n] = (from2d(n, t) for t in outs)
    view2d = lambda n, a: a.reshape(next((r, c) for name, r, c in SMALL_VIEWS if name == n))
    *res, loss_row = _adamw_small(smf, [tuple(view2d(n, t[n]) for t in (W, M, V)) for n in SMALL_NAMES])
    for n, outs in zip(SMALL_NAMES, res):
        G[n], DW[n], NM[n], NV[n] = (t.reshape(W[n].shape) for t in outs)
    loss = loss_row[0, 0]
    return (loss, dx[None], *[G[n] for n in WEIGHT_NAMES], *[DW[n] for n in WEIGHT_NAMES],
            *[NM[n] for n in WEIGHT_NAMES], *[NV[n] for n in WEIGHT_NAMES])
```

```python
import jax
import jax.numpy as jnp
from jax import lax
from jax.experimental import pallas as pl
from jax.experimental.pallas import tpu as pltpu

F32 = jnp.float32
BF16 = jnp.bfloat16
MXU_DTYPE = BF16

D_MODEL = 1024
MLA_HEADS = 8
MLA_NOPE = 64
MLA_ROPE = 32
MLA_V = 64
MLA_QK = MLA_NOPE + MLA_ROPE
Q_RANK = 384
KV_RANK = 128
MLA_WIDTH = MLA_HEADS * MLA_V
HEAD_PAD = 128
HGRN_HEADS = 4
HGRN_DIM = 128
HGRN_WIDTH = HGRN_HEADS * HGRN_DIM
CHUNK = 64
SUB = 16
HGRN_CPI = 4
D_IN = Q_RANK + KV_RANK + MLA_ROPE + 4 * HGRN_WIDTH
D_IN_ARR = Q_RANK + KV_RANK + HEAD_PAD + 4 * HGRN_WIDTH
D_FF = 2816
N_CHIPS = 4
FF_SHARD = D_FF // N_CHIPS
EPS = 1e-6
ROPE_THETA = 10000.0
ATTN_SCALE = MLA_QK ** -0.5
ATTN_SCALE_LOG2 = ATTN_SCALE * 1.4426950408889634
NEG_BIG = -1e30

ADAM_LR = 0.001
ADAM_B1 = 0.9
ADAM_B2 = 0.999
ADAM_EPS = 1e-08
ADAM_WD = 0.01
ADAM_STEP = 10

VMEM_LIMIT = 56 * 1024 * 1024

SMALL_VIEWS = (("attn_pre_norm", 1, 1024), ("mla_q_norm", 1, 384), ("mla_kv_norm", 1, 128), ("mla_w_ukv", 128, 1024),
               ("mla_out_norm", 1, 512), ("hgrn_lb_logits", 2, 512), ("hgrn_out_norm", 1, 512),
               ("attn_post_norm", 1, 1024), ("ffn_pre_norm", 1, 1024), ("ffn_post_norm", 1, 1024), ("loss", 1, 1))
ROW_TILE = 8


def _small_layout():
    offsets, row = {}, 0
    for whole in (True, False):
        for name, rows, _ in SMALL_VIEWS:
            if (rows % ROW_TILE == 0) == whole:
                offsets[name] = row
                row += rows
    return offsets, -(-row // (2 * ROW_TILE)) * 2 * ROW_TILE


SMALL_OFFSETS, SMALL_ROWS = _small_layout()

MESH = pl.DeviceIdType.MESH
ANY = pl.BlockSpec(memory_space=pl.ANY)


def _dot(a, b, dims, exact):
    if exact:
        return lax.dot_general(a.astype(F32), b.astype(F32), (dims, ((), ())), precision=lax.Precision.HIGH,
                               preferred_element_type=F32)
    return lax.dot_general(a.astype(MXU_DTYPE), b.astype(MXU_DTYPE), (dims, ((), ())), preferred_element_type=F32)


def _mm(a, b, exact=False):
    return _dot(a, b, ((1,), (0,)), exact)


def _mm_nt(a, b, exact=False):
    return _dot(a, b, ((1,), (1,)), exact)


def _mm_tn(a, b, exact=False):
    return _dot(a, b, ((0,), (0,)), exact)


def _rms_fwd(x, w):
    r = lax.rsqrt(jnp.mean(x * x, axis=-1, keepdims=True) + EPS)
    xn = x * r
    return xn * w, xn, r


def _rms_bwd(dy, xn, r, w):
    dxn = dy * w
    dx = r * (dxn - xn * jnp.mean(dxn * xn, axis=-1, keepdims=True))
    dw = jnp.sum(dy * xn, axis=0, keepdims=True)
    return dx, dw


def _group_sums(v, gs):
    t, n = v.shape
    lane = lax.broadcasted_iota(jnp.int32, (t, 128), 1)
    out = []
    for p in range(n // 128):
        vb = v[:, 128 * p:128 * (p + 1)]
        if gs == 128:
            out.append(jnp.sum(vb, axis=-1, keepdims=True))
        else:
            out.append(jnp.sum(jnp.where(lane < 64, vb, 0.0), axis=-1, keepdims=True))
            out.append(jnp.sum(jnp.where(lane >= 64, vb, 0.0), axis=-1, keepdims=True))
    return out


def _group_bcast(sums, gs, t):
    lane = lax.broadcasted_iota(jnp.int32, (t, 128), 1)
    if gs == 128:
        return jnp.concatenate([jnp.broadcast_to(s, (t, 128)) for s in sums], axis=-1)
    return jnp.concatenate([jnp.where(lane < 64, sums[2 * p], sums[2 * p + 1]) for p in range(len(sums) // 2)],
                           axis=-1)


def _grms_fwd(x, w, gs):
    t = x.shape[0]
    r = lax.rsqrt(_group_bcast(_group_sums(x * x, gs), gs, t) * (1.0 / gs) + EPS)
    xn = x * r
    return xn * w, xn, r


def _grms_bwd(dy, xn, r, w, gs):
    t = dy.shape[0]
    dxn = dy * w
    dx = r * (dxn - xn * (_group_bcast(_group_sums(dxn * xn, gs), gs, t) * (1.0 / gs)))
    dw = jnp.sum(dy * xn, axis=0, keepdims=True)
    return dx, dw


def _rope_tables(c_tab, s_tab):
    lane = lax.broadcasted_iota(jnp.int32, c_tab.shape, 1)
    first = (lane >= MLA_NOPE) & (lane < MLA_NOPE + MLA_ROPE // 2)
    second = (lane >= MLA_NOPE + MLA_ROPE // 2) & (lane < MLA_QK)
    return c_tab, jnp.where(first, -s_tab, 0.0), jnp.where(second, s_tab, 0.0)


def _rope(v, c, sa, sb):
    return v * c + pltpu.roll(v, HEAD_PAD - MLA_ROPE // 2, 1) * sa + pltpu.roll(v, MLA_ROPE // 2, 1) * sb


def _rope_bwd(d, c, sa, sb):
    return d * c - pltpu.roll(d, HEAD_PAD - MLA_ROPE // 2, 1) * sa - pltpu.roll(d, MLA_ROPE // 2, 1) * sb


def _params(sem, vmem=VMEM_LIMIT):
    return pltpu.CompilerParams(dimension_semantics=sem, vmem_limit_bytes=vmem)


def _in_fwd(x, pos, invf, w_pre, win, qnw, wq, kvnw, wk, wv, tt=512):
    T = x.shape[0]

    def body(x_ref, pos_ref, invf_ref, wpre_ref, win_ref, qnw_ref, wq_ref, kvnw_ref, wk_ref, wv_ref,
             cq_ref, ckv_ref, xph_ref, q_ref, k_ref, v_ref, kt_ref, vt_ref, rc_ref, rs_ref):
        u, _, _ = _rms_fwd(x_ref[...], wpre_ref[...])
        lo = Q_RANK + KV_RANK + HEAD_PAD
        xp = _mm_nt(u, win_ref[:lo, :])
        xph_ref[...] = _mm_nt(u, win_ref[lo:, :])
        cq = xp[:, :Q_RANK]
        ckv = xp[:, Q_RANK:Q_RANK + KV_RANK]
        kr = xp[:, Q_RANK + KV_RANK:]
        cq_ref[...] = cq
        ckv_ref[...] = ckv
        ang = pos_ref[...].astype(F32) * invf_ref[...]
        c_tab = jnp.cos(ang)
        s_tab = jnp.sin(ang)
        rc_ref[...] = c_tab
        rs_ref[...] = s_tab
        c, sa, sb = _rope_tables(c_tab, s_tab)
        qn, _, _ = _rms_fwd(cq, qnw_ref[...])
        q = _mm(qn, wq_ref[...])
        kvn, _, _ = _rms_fwd(ckv, kvnw_ref[...])
        kn = _mm(kvn, wk_ref[...])
        v = _mm(kvn, wv_ref[...])
        v_ref[...] = v.astype(v_ref.dtype)
        vt_ref[...] = v.T.astype(vt_ref.dtype)
        krr = _rope(kr, c, sa, sb)
        for h in range(MLA_HEADS):
            sl = slice(HEAD_PAD * h, HEAD_PAD * (h + 1))
            q_ref[:, sl] = (_rope(q[:, sl], c, sa, sb) * ATTN_SCALE_LOG2).astype(q_ref.dtype)
            kh = kn[:, sl] + krr
            k_ref[:, sl] = kh.astype(k_ref.dtype)
            kt_ref[sl, :] = kh.T.astype(kt_ref.dtype)

    row = lambda w: pl.BlockSpec((tt, w), lambda i: (i, 0))
    full = lambda a: pl.BlockSpec(a.shape, lambda i: (0,) * a.ndim)
    qk_w = MLA_HEADS * HEAD_PAD
    return pl.pallas_call(
        body, name="in_fwd", grid=(T // tt,),
        in_specs=[row(D_MODEL), row(1), full(invf), full(w_pre), full(win), full(qnw), full(wq), full(kvnw),
                  full(wk), full(wv)],
        out_specs=[row(Q_RANK), row(KV_RANK), row(4 * HGRN_WIDTH), row(qk_w), row(qk_w), row(MLA_WIDTH),
                   pl.BlockSpec((qk_w, tt), lambda i: (0, i)), pl.BlockSpec((MLA_WIDTH, tt), lambda i: (0, i)),
                   row(HEAD_PAD), row(HEAD_PAD)],
        out_shape=[jax.ShapeDtypeStruct((T, Q_RANK), F32), jax.ShapeDtypeStruct((T, KV_RANK), F32),
                   jax.ShapeDtypeStruct((T, 4 * HGRN_WIDTH), F32), jax.ShapeDtypeStruct((T, qk_w), MXU_DTYPE),
                   jax.ShapeDtypeStruct((T, qk_w), MXU_DTYPE), jax.ShapeDtypeStruct((T, MLA_WIDTH), MXU_DTYPE),
                   jax.ShapeDtypeStruct((qk_w, T), MXU_DTYPE), jax.ShapeDtypeStruct((MLA_WIDTH, T), MXU_DTYPE),
                   jax.ShapeDtypeStruct((T, HEAD_PAD), F32), jax.ShapeDtypeStruct((T, HEAD_PAD), F32)],
        compiler_params=_params(("arbitrary",)),
    )(x, pos, invf, w_pre, win, qnw, wq, kvnw, wk, wv)


def _attn_fwd_t(qb, kb, vt, gather=(), tq=256, hps=8):
    T = qb.shape[0]
    nq = T // tq
    ng = len(gather)
    steps = (MLA_HEADS // hps) * nq
    pass_on = steps - 3

    def body(q_ref, k_ref, vt_ref, *rest):
        o_ref, lse_ref = rest[ng:ng + 2]
        acc_scr = rest[2 * ng + 2]
        qi = pl.program_id(1)
        step_no = pl.program_id(0) * nq + qi
        if ng:
            gat = _Gather(rest[:ng], rest[ng + 2:2 * ng + 2], *rest[2 * ng + 3:])

            @pl.when(step_no == 0)
            def _():
                for cp in gat.sends():
                    cp.start()

            @pl.when(step_no == pass_on)
            def _():
                for arrival in gat.arrivals():
                    arrival.wait_recv()
                for cp in gat.forwards():
                    cp.start()

        heads = [slice(HEAD_PAD * a, HEAD_PAD * (a + 1)) for a in range(hps)]
        acc_scr[...] = jnp.zeros_like(acc_scr)

        def step(j, carry, masked):
            start = pl.multiple_of(j * tq, tq)
            scores = [_mm_nt(k_ref[pl.ds(start, tq), heads[a]], q_ref[:, heads[a]]) for a in range(hps)]
            new, probs, alphas = [], [], []
            for a in range(hps):
                m, l = carry[a]
                s = scores[a]
                if masked:
                    kk = lax.broadcasted_iota(jnp.int32, (tq, tq), 0)
                    qq = lax.broadcasted_iota(jnp.int32, (tq, tq), 1)
                    s = jnp.where(kk <= qq, s, NEG_BIG)
                m_new = jnp.maximum(m, jnp.max(s, axis=0, keepdims=True))
                alpha = jnp.exp2(m - m_new)
                p = jnp.exp2(s - m_new)
                l = l * alpha + jnp.sum(p, axis=0, keepdims=True)
                new.append((m_new, l))
                probs.append(p.astype(MXU_DTYPE))
                alphas.append(alpha)
                if a % 2:
                    pr = a // 2
                    vtj = vt_ref[2 * MLA_V * pr:2 * MLA_V * (pr + 1), pl.ds(start, tq)]
                    none = jnp.zeros((MLA_V, tq), vtj.dtype)
                    pv = (_mm(jnp.concatenate([vtj[:MLA_V], none], axis=0), probs[a - 1])
                          + _mm(jnp.concatenate([none, vtj[MLA_V:]], axis=0), probs[a]))
                    acc_scr[pr] = acc_scr[pr] * jnp.where(row < MLA_V, alphas[a - 1], alphas[a]) + pv
            return tuple(new)

        row = lax.broadcasted_iota(jnp.int32, (2 * MLA_V, tq), 0)
        init = tuple((jnp.full((1, tq), NEG_BIG, F32), jnp.zeros((1, tq), F32)) for _ in range(hps))
        carry = lax.fori_loop(0, qi, lambda j, c: step(j, c, False), init)
        carry = step(qi, carry, True)
        for pr in range(hps // 2):
            (m0, l0), (m1, l1) = carry[2 * pr], carry[2 * pr + 1]
            ot = acc_scr[pr] / jnp.where(row < MLA_V, l0, l1)
            o_ref[:, 2 * MLA_V * pr:2 * MLA_V * (pr + 1)] = ot.T
            lse_ref[pr, 0:1, :] = m0 + jnp.log2(l0)
            lse_ref[pr, 1:2, :] = m1 + jnp.log2(l1)

        if ng:
            @pl.when(step_no == steps - 1)
            def _():
                for arrival in gat.forward_arrivals():
                    arrival.wait_recv()
                for cp in gat.sends() + gat.forwards():
                    cp.wait_send()

    return pl.pallas_call(
        body, name="attn_fwd", grid=(MLA_HEADS // hps, nq),
        in_specs=[pl.BlockSpec((tq, hps * HEAD_PAD), lambda g, i: (i, g)),
                  pl.BlockSpec((T, hps * HEAD_PAD), lambda g, i: (0, g)),
                  pl.BlockSpec((hps * MLA_V, T), lambda g, i: (g, 0))] + [ANY] * ng,
        out_specs=[pl.BlockSpec((tq, hps * MLA_V), lambda g, i: (i, g)),
                   pl.BlockSpec((hps // 2, 2, tq), lambda g, i: (g, 0, i))] + [ANY] * ng,
        out_shape=[jax.ShapeDtypeStruct((T, MLA_WIDTH), F32), jax.ShapeDtypeStruct((MLA_HEADS // 2, 2, T), F32)]
        + _Gather.out_shapes(gather),
        scratch_shapes=[pltpu.VMEM((hps // 2, 2 * MLA_V, tq), F32)] + (_Gather.semaphores(gather) if ng else []),
        compiler_params=_params(("arbitrary", "arbitrary")),
    )(qb, kb, vt, *gather)


def _attn_bwd_t(qb, kb, kt, vb, dob, lse, dvec, send=(), tq=512, hps=4):
    T = qb.shape[0]
    nq = T // tq
    ns = len(send)
    steps = (MLA_HEADS // hps) * nq

    def body(q_ref, k_ref, kt_ref, v_ref, do_ref, lse_ref, d_ref, *rest):
        dqt_ref, dk_ref, dv_ref = rest[ns:ns + 3]
        va_scr, dv_scr = rest[2 * ns + 3:2 * ns + 5]
        j = pl.program_id(1)
        step_no = pl.program_id(0) * nq + j
        if ns:
            @pl.when(step_no == 0)
            def _():
                for cp in _chip_swap_copies(rest[:ns], rest[ns + 3:2 * ns + 3], *rest[2 * ns + 5:]):
                    cp.start()

        @pl.when(j == 0)
        def _():
            dqt_ref[...] = jnp.zeros_like(dqt_ref)

        lane = lax.broadcasted_iota(jnp.int32, (tq, 2 * MLA_V), 1)
        heads = [slice(HEAD_PAD * a, HEAD_PAD * (a + 1)) for a in range(hps)]
        pairs = [slice(2 * MLA_V * p, 2 * MLA_V * (p + 1)) for p in range(hps // 2)]
        for pr in range(hps // 2):
            vpair = v_ref[:, pairs[pr]]
            va_scr[2 * pr] = jnp.where(lane < MLA_V, vpair, jnp.zeros_like(vpair))
            va_scr[2 * pr + 1] = jnp.where(lane >= MLA_V, vpair, jnp.zeros_like(vpair))
        dk_ref[...] = jnp.zeros_like(dk_ref)
        dv_scr[...] = jnp.zeros_like(dv_scr)

        def step(i, masked):
            start = pl.multiple_of(i * tq, tq)
            rows = pl.ds(start, tq)
            scores = [_mm_nt(k_ref[:, heads[a]], q_ref[rows, heads[a]]) for a in range(hps)]
            dps = [_mm_nt(va_scr[a], do_ref[rows, pairs[a // 2]]) for a in range(hps)]
            for a in range(hps):
                pr, r = a // 2, a % 2
                p = jnp.exp2(scores[a] - lse_ref[pr, r:r + 1, rows])
                if masked:
                    kk = lax.broadcasted_iota(jnp.int32, (tq, tq), 0)
                    qq = lax.broadcasted_iota(jnp.int32, (tq, tq), 1)
                    p = jnp.where(kk <= qq, p, 0.0)
                ds = p * (dps[a] - d_ref[pr, r:r + 1, rows])
                dv_scr[a] += _mm(p, do_ref[rows, pairs[pr]])
                dk_ref[:, heads[a]] += _mm(ds, q_ref[rows, heads[a]])
                dqt_ref[heads[a], rows] += _mm(kt_ref[heads[a], :], ds)

        def loop_body(i, _):
            step(i, False)
            return 0

        step(j, True)
        lax.fori_loop(j + 1, nq, loop_body, 0)
        for pr in range(hps // 2):
            dv_ref[:, pairs[pr]] = jnp.where(lane < MLA_V, dv_scr[2 * pr], dv_scr[2 * pr + 1])
        dk_ref[...] = dk_ref[...] * (ATTN_SCALE / ATTN_SCALE_LOG2)

        if ns:
            @pl.when(step_no == steps - 1)
            def _():
                for cp in _chip_swap_copies(rest[:ns], rest[ns + 3:2 * ns + 3], *rest[2 * ns + 5:]):
                    cp.wait()

    stat = pl.BlockSpec((hps // 2, 2, T), lambda g, j: (g, 0, 0))
    return pl.pallas_call(
        body, name="attn_bwd", grid=(MLA_HEADS // hps, nq),
        in_specs=[pl.BlockSpec((T, hps * HEAD_PAD), lambda g, j: (0, g)),
                  pl.BlockSpec((tq, hps * HEAD_PAD), lambda g, j: (j, g)),
                  pl.BlockSpec((hps * HEAD_PAD, tq), lambda g, j: (g, j)),
                  pl.BlockSpec((tq, hps * MLA_V), lambda g, j: (j, g)),
                  pl.BlockSpec((T, hps * MLA_V), lambda g, j: (0, g)), stat, stat] + [ANY] * ns,
        out_specs=[pl.BlockSpec((hps * HEAD_PAD, T), lambda g, j: (g, 0)),
                   pl.BlockSpec((tq, hps * HEAD_PAD), lambda g, j: (j, g)),
                   pl.BlockSpec((tq, hps * MLA_V), lambda g, j: (j, g))] + [ANY] * ns,
        out_shape=[jax.ShapeDtypeStruct((MLA_HEADS * HEAD_PAD, T), F32),
                   jax.ShapeDtypeStruct((T, MLA_HEADS * HEAD_PAD), F32),
                   jax.ShapeDtypeStruct((T, MLA_WIDTH), F32)] + _chip_swap_shapes(send),
        scratch_shapes=[pltpu.VMEM((hps, tq, 2 * MLA_V), vb.dtype), pltpu.VMEM((hps, tq, 2 * MLA_V), F32)]
        + ([pltpu.SemaphoreType.DMA((3 * ns,)), pltpu.SemaphoreType.DMA((3 * ns,))] if ns else []),
        compiler_params=_params(("arbitrary", "arbitrary")),
    )(qb, kb, kt, vb, dob, lse, dvec, *send)


def _cumsum_rows(x):
    n = x.shape[0]
    row = lax.broadcasted_iota(jnp.int32, x.shape, 0)
    s = 1
    while s < n:
        x = x + jnp.where(row >= s, pltpu.roll(x, s, 0), 0.0)
        s *= 2
    return x


def _rev_cumsum_rows(x):
    n = x.shape[0]
    row = lax.broadcasted_iota(jnp.int32, x.shape, 0)
    s = 1
    while s < n:
        x = x + jnp.where(row < n - s, pltpu.roll(x, n - s, 0), 0.0)
        s *= 2
    return x


def _lb_from_logits(l):
    l0, l1 = l[0:1, :], l[1:2, :]
    m = jnp.maximum(l0, l1)
    e0, e1 = jnp.exp(l0 - m), jnp.exp(l1 - m)
    return e0 / (e0 + e1)


def _hgrn_gates(hq, hf, lb):
    sig_f = jax.nn.sigmoid(hf)
    f = lb + (1.0 - lb) * sig_f
    sig_q = jax.nn.sigmoid(hq)
    return sig_f, f, jnp.log(f), 1.0 - f, sig_q, hq * sig_q


def _hgrn_intra(q, kk, b, exact=False):
    row = lax.broadcasted_iota(jnp.int32, b.shape, 0)
    qs, ks, eqs, eks, a_rows = [], [], [], [], []
    for i in range(CHUNK // SUB):
        ref = b[SUB * i + SUB // 2:SUB * i + SUB // 2 + 1, :]
        eq = jnp.exp(b[SUB * i:SUB * (i + 1), :] - ref)
        ek = jnp.exp(jnp.where(row < SUB * (i + 1), ref - b, NEG_BIG))
        qi = q[SUB * i:SUB * (i + 1), :] * eq
        ki = kk * ek
        a_rows.append(_mm_nt(qi, ki, exact))
        qs.append(qi), ks.append(ki), eqs.append(eq), eks.append(ek)
    tt = lax.broadcasted_iota(jnp.int32, (CHUNK, CHUNK), 0)
    ss = lax.broadcasted_iota(jnp.int32, (CHUNK, CHUNK), 1)
    causal = ss <= tt
    a = jnp.where(causal, jnp.concatenate(a_rows, axis=0), 0.0)
    return a, causal, qs, ks, eqs, eks


def _hgrn_fwd(xph, lbl, tg=512):
    T = xph.shape[0]
    ng, ncg = T // tg, tg // CHUNK
    cols = [slice(HGRN_DIM * h, HGRN_DIM * (h + 1)) for h in range(HGRN_HEADS)]

    def body(lbl_ref, hq_ref, hf_ref, hi_ref, o_ref, st_ref, s_scr):
        @pl.when(pl.program_id(0) == 0)
        def _():
            s_scr[...] = jnp.zeros_like(s_scr)

        lb = _lb_from_logits(lbl_ref[...])

        def chunks(it, _):
            pre = []
            for k in range(HGRN_CPI):
                c = it * HGRN_CPI + k
                rows = pl.ds(pl.multiple_of(c * CHUNK, CHUNK), CHUNK)
                for cs in cols:
                    _, _, lf, kk, _, q = _hgrn_gates(hq_ref[rows, cs], hf_ref[rows, cs], lb[:, cs])
                    v = hi_ref[rows, cs]
                    b = _cumsum_rows(lf)
                    a = _hgrn_intra(q, kk, b)[0]
                    b_last = b[CHUNK - 1:CHUNK, :]
                    pre.append((c, rows, q * jnp.exp(b), a, v, jnp.exp(b_last), _mm_tn(v, kk * jnp.exp(b_last - b))))
            for i, (c, rows, qe, a, v, ebl, upd) in enumerate(pre):
                h = i % HGRN_HEADS
                st = s_scr[h]
                st_ref[h, c] = st
                o_ref[rows, cols[h]] = _mm_nt(qe, st) + _mm(a, v)
                s_scr[h] = st * ebl + upd
            return 0

        lax.fori_loop(0, ncg // HGRN_CPI, chunks, 0)

    col = lambda k: pl.BlockSpec((tg, HGRN_WIDTH), lambda g: (g, k))
    return pl.pallas_call(
        body, name="hgrn_fwd", grid=(ng,),
        in_specs=[pl.BlockSpec((2, HGRN_WIDTH), lambda g: (0, 0)), col(0), col(1), col(2)],
        out_specs=[col(0), pl.BlockSpec((HGRN_HEADS, ncg, HGRN_DIM, HGRN_DIM), lambda g: (0, g, 0, 0))],
        out_shape=[jax.ShapeDtypeStruct((T, HGRN_WIDTH), F32),
                   jax.ShapeDtypeStruct((HGRN_HEADS, T // CHUNK, HGRN_DIM, HGRN_DIM), F32)],
        scratch_shapes=[pltpu.VMEM((HGRN_HEADS, HGRN_DIM, HGRN_DIM), F32)],
        compiler_params=_params(("arbitrary",)),
    )(lbl, xph, xph, xph)


def _hgrn_bwd(xph, lbl, states, d_o, fill=(), tg=512):
    T = xph.shape[0]
    ng, ncg = T // tg, tg // CHUNK
    cols = [slice(HGRN_DIM * h, HGRN_DIM * (h + 1)) for h in range(HGRN_HEADS)]
    nsub = CHUNK // SUB
    nf = len(fill)

    def body(lbl_ref, hq_ref, hf_ref, hi_ref, st_ref, do_ref, *rest):
        dhq_ref, dhf_ref, dhi_ref, dlg_ref = rest[nf:nf + 4]
        ds_scr, dlb_scr = rest[2 * nf + 4:2 * nf + 6]
        fill_copies = lambda: _pair_fill_copies(rest[nf + 4:2 * nf + 4], *rest[2 * nf + 6:])
        g = pl.program_id(0)

        @pl.when(g == 0)
        def _():
            ds_scr[...] = jnp.zeros_like(ds_scr)
            dlb_scr[...] = jnp.zeros_like(dlb_scr)
            for cp in (fill_copies()[0] if nf else ()):
                cp.start()

        lb = _lb_from_logits(lbl_ref[...])

        def chunks(it, _):
            pre = []
            for k, h in ((k, h) for k in range(HGRN_CPI) for h in range(HGRN_HEADS)):
                cs = cols[h]
                c = ncg - 1 - (it * HGRN_CPI + k)
                rows = pl.ds(pl.multiple_of(c * CHUNK, CHUNK), CHUNK)
                hq = hq_ref[rows, cs]
                sig_f, f, lf, kk, sig_q, q = _hgrn_gates(hq, hf_ref[rows, cs], lb[:, cs])
                v = hi_ref[rows, cs]
                do = do_ref[rows, cs]
                b = _cumsum_rows(lf)
                eb = jnp.exp(b)
                a, causal, qs, ks, eqs, eks = _hgrn_intra(q, kk, b)
                b_last = b[CHUNK - 1:CHUNK, :]
                st = st_ref[h, c]
                pre.append(dict(h=h, cs=cs, rows=rows, hq=hq, sig_f=sig_f, f=f, kk=kk, sig_q=sig_q, q=q, v=v, eb=eb, qs=qs,
                                ks=ks, eqs=eqs,
                                eks=eks, ebl=jnp.exp(b_last), el=jnp.exp(b_last - b), st=st,
                                da=jnp.where(causal, _mm_nt(do, v, True), 0.0), dq=_mm(do, st, True) * eb,
                                dv=_mm_tn(a, do), dsu=_mm_tn(do, q * eb, True)))
            for w in pre:
                dq_rows = []
                dk = jnp.zeros_like(w["q"])
                for i in range(nsub):
                    dai = w["da"][SUB * i:SUB * (i + 1), :]
                    dq_rows.append(_mm(dai, w["ks"][i], True) * w["eqs"][i])
                    dk = dk + _mm_tn(dai, w["qs"][i], True) * w["eks"][i]
                w["dq"] = w["dq"] + jnp.concatenate(dq_rows, axis=0)
                w["dk"] = dk
            for w in pre:
                h, cs, rows = w["h"], w["cs"], w["rows"]
                kk, el, ebl, dst = w["kk"], w["el"], w["ebl"], ds_scr[h]
                dk_state = _mm(w["v"], dst, True) * el
                dk = w["dk"] + dk_state
                e_last = (ebl * jnp.sum(w["st"] * dst, axis=0, keepdims=True)
                          + jnp.sum(kk * dk_state, axis=0, keepdims=True))
                dlf = _rev_cumsum_rows(w["q"] * w["dq"] - kk * dk) + e_last
                ds_scr[h] = dst * ebl + w["dsu"]
                df = dlf / w["f"] - dk
                sig_f, sig_q = w["sig_f"], w["sig_q"]
                dhf_ref[rows, cs] = df * (1.0 - lb[:, cs]) * sig_f * (1.0 - sig_f)
                dlb_scr[:, cs] += jnp.sum(df * (1.0 - sig_f), axis=0, keepdims=True)
                dhq_ref[rows, cs] = w["dq"] * sig_q * (1.0 + w["hq"] * (1.0 - sig_q))
                dhi_ref[rows, cs] = w["dv"] + _mm_nt(kk * el, dst)
            return 0

        lax.fori_loop(0, ncg // HGRN_CPI, chunks, 0)

        @pl.when(g == ng - 1)
        def _():
            dl0 = dlb_scr[...] * lb * (1.0 - lb)
            dlg_ref[...] = jnp.concatenate([dl0, -dl0], axis=0)
            if nf:
                copies, waits = fill_copies()
                for w in waits:
                    w.wait_recv()
                for cp in copies:
                    cp.wait_send()

    col = lambda k: pl.BlockSpec((tg, HGRN_WIDTH), lambda g: (ng - 1 - g, k))
    logits = pl.BlockSpec((2, HGRN_WIDTH), lambda g: (0, 0))
    big = jax.ShapeDtypeStruct((T, HGRN_WIDTH), F32)
    n_in, n_out = 6, 4
    return pl.pallas_call(
        body, name="hgrn_bwd", grid=(ng,),
        in_specs=[logits, col(0), col(1), col(2),
                  pl.BlockSpec((HGRN_HEADS, ncg, HGRN_DIM, HGRN_DIM), lambda g: (0, ng - 1 - g, 0, 0)), col(0)] + [ANY] * nf,
        out_specs=[col(0), col(0), col(0), logits] + [ANY] * nf,
        out_shape=[big, big, big, jax.ShapeDtypeStruct((2, HGRN_WIDTH), F32)]
        + [jax.ShapeDtypeStruct(f.shape, f.dtype) for f in fill],
        input_output_aliases={n_in + k: n_out + k for k in range(nf)},
        scratch_shapes=[pltpu.VMEM((HGRN_HEADS, HGRN_DIM, HGRN_DIM), F32), pltpu.VMEM((1, HGRN_WIDTH), F32)]
        + ([pltpu.SemaphoreType.DMA((nf,)), pltpu.SemaphoreType.DMA((nf,))] if nf else []),
        compiler_params=_params(("arbitrary",)),
    )(lbl, xph, xph, xph, states, d_o, *fill)


def _proj_fwd(x, o_raw, oh_raw, xph, wout, w_mla, w_hg, w_post, w_fpre, tt=512):
    T = x.shape[0]

    def body(x_ref, o_ref, oh_ref, hg_ref, wout_ref, wmla_ref, whg_ref, wpost_ref, wfpre_ref,
             h1_ref, y1_ref, z_ref, mix_ref):
        om, _, _ = _grms_fwd(o_ref[...], wmla_ref[...], MLA_V)
        hg = hg_ref[...]
        ohn, _, _ = _grms_fwd(oh_ref[...], whg_ref[...], HGRN_DIM)
        mix = jnp.concatenate([om, ohn * (hg * jax.nn.sigmoid(hg))], axis=-1)
        mix_ref[...] = mix.astype(mix_ref.dtype)
        y1 = _mm(mix, wout_ref[...])
        y1_ref[...] = y1
        h1 = x_ref[...] + _rms_fwd(y1, wpost_ref[...])[0]
        h1_ref[...] = h1
        z_ref[...] = _rms_fwd(h1, wfpre_ref[...])[0].astype(z_ref.dtype)

    row = lambda w: pl.BlockSpec((tt, w), lambda i: (i, 0))
    full = lambda a: pl.BlockSpec(a.shape, lambda i: (0,) * a.ndim)
    sds = jax.ShapeDtypeStruct
    return pl.pallas_call(
        body, name="proj_fwd", grid=(T // tt,),
        in_specs=[row(D_MODEL), row(MLA_WIDTH), row(HGRN_WIDTH), pl.BlockSpec((tt, HGRN_WIDTH), lambda i: (i, 3)),
                  full(wout), full(w_mla), full(w_hg), full(w_post), full(w_fpre)],
        out_specs=[row(D_MODEL)] * 4,
        out_shape=[sds((T, D_MODEL), F32), sds((T, D_MODEL), F32), sds((T, D_MODEL), MXU_DTYPE),
                   sds((T, D_MODEL), MXU_DTYPE)],
        compiler_params=_params(("arbitrary",)),
    )(x, o_raw, oh_raw, xph, wout, w_mla, w_hg, w_post, w_fpre)


def _ffn_fwd(zb, h1, tgt, w_fpost, wg, wu, wd, tt=256):
    T = zb.shape[0]
    nj = N_CHIPS

    def body(z_ref, h1_ref, tgt_ref, wfpost_ref, wg_ref, wu_ref, wd_ref, g_ref, up_ref, dy2_ref, dh2_ref, loss_ref, dwf_ref):
        @pl.when(pl.program_id(0) == 0)
        def _():
            loss_ref[...] = jnp.zeros_like(loss_ref)
            dwf_ref[...] = jnp.zeros_like(dwf_ref)

        z = z_ref[...]
        gs = [_mm_nt(z, wg_ref[j]) for j in range(nj)]
        ups = [_mm_nt(z, wu_ref[j]) for j in range(nj)]
        y2 = jnp.zeros((tt, D_MODEL), F32)
        for j in range(nj):
            g_ref[j] = gs[j]
            up_ref[j] = ups[j]
            y2 = y2 + _mm(gs[j] * jax.nn.sigmoid(gs[j]) * ups[j], wd_ref[j])
        w = wfpost_ref[...]
        y2s, y2n, r2 = _rms_fwd(y2, w)
        e = h1_ref[...] + y2s - tgt_ref[...]
        loss_ref[...] += jnp.sum(e * e, axis=0, keepdims=True)
        dh2 = e * (1.0 / D_MODEL)
        dh2_ref[...] = dh2
        dy2, dwf = _rms_bwd(dh2, y2n, r2, w)
        dy2_ref[...] = dy2.astype(dy2_ref.dtype)
        dwf_ref[...] += dwf

    row = pl.BlockSpec((tt, D_MODEL), lambda i: (i, 0))
    vec = pl.BlockSpec((1, D_MODEL), lambda i: (0, 0))
    resident = pl.BlockSpec((nj, FF_SHARD, D_MODEL), lambda i: (0, 0, 0), pipeline_mode=pl.Buffered(1))
    act = pl.BlockSpec((nj, tt, FF_SHARD), lambda i: (0, i, 0))
    sds = jax.ShapeDtypeStruct
    return pl.pallas_call(
        body, name="ffn_fwd", grid=(T // tt,),
        in_specs=[row, row, row, vec, resident, resident, resident],
        out_specs=[act, act, row, row, vec, vec],
        out_shape=[sds((nj, T, FF_SHARD), F32), sds((nj, T, FF_SHARD), F32), sds((T, D_MODEL), MXU_DTYPE),
                   sds((T, D_MODEL), F32), sds((1, D_MODEL), F32), sds((1, D_MODEL), F32)],
        compiler_params=_params(("arbitrary",)),
    )(zb, h1, tgt, w_fpost, wg, wu, wd)


def _ffn_bwd(zb, g, up, dy2b, wg, wu, wd, tt=512):
    T = zb.shape[0]
    nj = N_CHIPS

    def body(z_ref, g_ref, up_ref, dy2_ref, wg_ref, wu_ref, wd_ref, dwg_ref, dwu_ref, dwd_ref, dz_ref):
        @pl.when(pl.program_id(1) == 0)
        def _():
            dwg_ref[...] = jnp.zeros_like(dwg_ref)
            dwu_ref[...] = jnp.zeros_like(dwu_ref)
            dwd_ref[...] = jnp.zeros_like(dwd_ref)

        z, g_, up_, dy2 = z_ref[...], g_ref[0], up_ref[0], dy2_ref[...]
        sg = jax.nn.sigmoid(g_)
        act = g_ * sg
        dff = _mm_nt(dy2, wd_ref[0])
        dwd_ref[0] += _mm_tn(act * up_, dy2)
        dg = dff * up_ * sg * (1.0 + g_ * (1.0 - sg))
        dup = dff * act
        dwg_ref[0] += _mm_tn(dg, z)
        dwu_ref[0] += _mm_tn(dup, z)
        dz_ref[0] = _mm(dg, wg_ref[0]) + _mm(dup, wu_ref[0])

    row = pl.BlockSpec((tt, D_MODEL), lambda j, i: (i, 0))
    act = pl.BlockSpec((1, tt, FF_SHARD), lambda j, i: (j, i, 0))
    w_sh = pl.BlockSpec((1, FF_SHARD, D_MODEL), lambda j, i: (j, 0, 0))
    w_grad = jax.ShapeDtypeStruct((nj, FF_SHARD, D_MODEL), F32)
    return pl.pallas_call(
        body, name="ffn_bwd", grid=(nj, T // tt),
        in_specs=[row, act, act, row, w_sh, w_sh, w_sh],
        out_specs=[w_sh, w_sh, w_sh, pl.BlockSpec((1, tt, D_MODEL), lambda j, i: (j, i, 0))],
        out_shape=[w_grad, w_grad, w_grad, jax.ShapeDtypeStruct((nj, T, D_MODEL), F32)],
        compiler_params=_params(("arbitrary", "arbitrary")),
    )(zb, g, up, dy2b, wg, wu, wd)


def _mid_bwd(dzp, dh2, h1, y1, mixb, o_raw, oh_raw, xph, wout, w_fpre, w_post, w_mla, w_hg, swap=(), tt=256):
    T = dh2.shape[0]
    nsw = len(swap)
    n_in, n_out = 13, 10

    def body(*refs):
        (dzp_ref, dh2_ref, h1_ref, y1_ref, mix_ref, o_ref, oh_ref, hg_ref, wout_ref, wfpre_ref, wpost_ref,
         wmla_ref, whg_ref) = refs[:n_in]
        (dh1_ref, dwout_ref, do_ref, doh_ref, dhg_ref, dvec_ref, dwfpre_ref, dwpost_ref, dwmla_ref,
         dwhg_ref) = refs[n_in + nsw:n_in + nsw + n_out]
        swap_copies = lambda: _pair_swap_copies(refs[n_in:n_in + nsw], refs[n_in + nsw + n_out:n_in + 2 * nsw + n_out],
                                                *refs[n_in + 2 * nsw + n_out:])

        @pl.when(pl.program_id(0) == 0)
        def _():
            for r in (dwout_ref, dwfpre_ref, dwpost_ref, dwmla_ref, dwhg_ref):
                r[...] = jnp.zeros_like(r)
            for cp in (swap_copies() if nsw else ()):
                cp.start()

        dz = dzp_ref[0] + dzp_ref[1] + dzp_ref[2] + dzp_ref[3]
        wfpre = wfpre_ref[...]
        _, h1n, r = _rms_fwd(h1_ref[...], wfpre)
        dh1_z, dwfpre = _rms_bwd(dz, h1n, r, wfpre)
        dwfpre_ref[...] += dwfpre
        dh1 = dh2_ref[...] + dh1_z
        dh1_ref[...] = dh1
        wpost = wpost_ref[...]
        _, y1n, r1 = _rms_fwd(y1_ref[...], wpost)
        dy1, dwpost = _rms_bwd(dh1, y1n, r1, wpost)
        dwpost_ref[...] += dwpost
        dmix = _mm_nt(dy1, wout_ref[...])
        dwout_ref[...] += _mm_tn(mix_ref[...], dy1)
        wmla = wmla_ref[...]
        o = o_ref[...]
        _, on, ro = _grms_fwd(o, wmla, MLA_V)
        d_o, dwmla = _grms_bwd(dmix[:, :MLA_WIDTH], on, ro, wmla, MLA_V)
        dwmla_ref[...] += dwmla
        do_ref[...] = d_o.astype(do_ref.dtype)
        hh = lax.broadcasted_iota(jnp.int32, (MLA_HEADS, MLA_WIDTH), 0)
        ll = lax.broadcasted_iota(jnp.int32, (MLA_HEADS, MLA_WIDTH), 1)
        sel = jnp.where((ll >= hh * MLA_V) & (ll < (hh + 1) * MLA_V), 1.0, 0.0)
        dvec_ref[...] = _mm_nt(sel, d_o * o, True)
        whg = whg_ref[...]
        hg = hg_ref[...]
        sg = jax.nn.sigmoid(hg)
        _, ohn, rh = _grms_fwd(oh_ref[...], whg, HGRN_DIM)
        dmh = dmix[:, MLA_WIDTH:]
        dhg_ref[...] = dmh * ohn * whg * sg * (1.0 + hg * (1.0 - sg))
        d_oh, dwhg = _grms_bwd(dmh * (hg * sg), ohn, rh, whg, HGRN_DIM)
        dwhg_ref[...] += dwhg
        doh_ref[...] = d_oh

        if nsw:
            @pl.when(pl.program_id(0) == T // tt - 1)
            def _():
                for cp in swap_copies():
                    cp.wait()

    row = lambda w: pl.BlockSpec((tt, w), lambda i: (i, 0))
    full = lambda a: pl.BlockSpec(a.shape, lambda i: (0,) * a.ndim)
    vec = lambda w: pl.BlockSpec((1, w), lambda i: (0, 0))
    sds = jax.ShapeDtypeStruct
    return pl.pallas_call(
        body, name="mid_bwd", grid=(T // tt,),
        in_specs=[pl.BlockSpec((N_CHIPS, tt, D_MODEL), lambda i: (0, i, 0)), row(D_MODEL), row(D_MODEL), row(D_MODEL),
                  row(D_MODEL), row(MLA_WIDTH), row(HGRN_WIDTH), pl.BlockSpec((tt, HGRN_WIDTH), lambda i: (i, 3)),
                  full(wout), vec(D_MODEL), vec(D_MODEL), vec(MLA_WIDTH), vec(HGRN_WIDTH)] + [ANY] * nsw,
        out_specs=[row(D_MODEL), full(wout), row(MLA_WIDTH), row(HGRN_WIDTH), row(HGRN_WIDTH),
                   pl.BlockSpec((MLA_HEADS, tt), lambda i: (0, i)),
                   vec(D_MODEL), vec(D_MODEL), vec(MLA_WIDTH), vec(HGRN_WIDTH)] + [ANY] * nsw,
        out_shape=[sds((T, D_MODEL), F32), sds(wout.shape, F32), sds((T, MLA_WIDTH), MXU_DTYPE), sds((T, HGRN_WIDTH), F32),
                   sds((T, HGRN_WIDTH), F32), sds((MLA_HEADS, T), F32),
                   sds((1, D_MODEL), F32), sds((1, D_MODEL), F32), sds((1, MLA_WIDTH), F32), sds((1, HGRN_WIDTH), F32)]
        + _half_stack_shapes(swap),
        scratch_shapes=[pltpu.SemaphoreType.DMA((nsw,)), pltpu.SemaphoreType.DMA((nsw,))] if nsw else [],
        compiler_params=_params(("arbitrary",)),
    )(dzp, dh2, h1, y1, mixb, o_raw, oh_raw, xph, wout, w_fpre, w_post, w_mla, w_hg, *swap)


def _in_bwd(x, dh1, cq, ckv, dq, dk, dv, dhq, dhf, dhi, dhg, rc, rs, w_pre, win, qnw, wq, kvnw, wk, wv, tt=256):
    T = x.shape[0]

    def body(x_ref, dh1_ref, cq_ref, ckv_ref, dq_ref, dk_ref, dv_ref, dhq_ref, dhf_ref, dhi_ref, dhg_ref, rc_ref, rs_ref,
             wpre_ref, win_ref, qnw_ref, wq_ref, kvnw_ref, wk_ref, wv_ref,
             dx_ref, dwin_ref, dwq_ref, dwk_ref, dwv_ref, dwpre_ref, dqnw_ref, dkvnw_ref):
        @pl.when(pl.program_id(0) == 0)
        def _():
            for r in (dwin_ref, dwq_ref, dwk_ref, dwv_ref, dwpre_ref, dqnw_ref, dkvnw_ref):
                r[...] = jnp.zeros_like(r)

        def add_win_grad(r, first):
            for arr0, n, chip, row0 in _win_grad_segments():
                if first <= arr0 and arr0 + n <= first + r.shape[0]:
                    dwin_ref[chip, row0:row0 + n, :] += r[arr0 - first:arr0 - first + n]

        lo = Q_RANK + KV_RANK + HEAD_PAD
        dxp_h = jnp.concatenate([dhq_ref[...], dhf_ref[...], dhi_ref[...], dhg_ref[...]], axis=-1)
        du = _mm(dxp_h, win_ref[lo:, :])
        wpre = wpre_ref[...]
        u, xn, rx = _rms_fwd(x_ref[...], wpre)
        add_win_grad(_mm_tn(dxp_h, u), lo)
        c, sa, sb = _rope_tables(rc_ref[...], rs_ref[...])
        lane = lax.broadcasted_iota(jnp.int32, (tt, HEAD_PAD), 1)
        dk_all = dk_ref[...]
        dq_lin = []
        dkr = jnp.zeros((tt, HEAD_PAD), F32)
        for h in range(MLA_HEADS):
            sl = slice(HEAD_PAD * h, HEAD_PAD * (h + 1))
            dq_lin.append(_rope_bwd(dq_ref[sl, :].T * ATTN_SCALE, c, sa, sb))
            dkr = dkr + dk_all[:, sl]
        dq_lin = jnp.concatenate(dq_lin, axis=-1)
        dkr = jnp.where((lane >= MLA_NOPE) & (lane < MLA_QK), _rope_bwd(dkr, c, sa, sb), 0.0)
        qnw = qnw_ref[...]
        qn, cqn, rq = _rms_fwd(cq_ref[...], qnw)
        dwq_ref[...] += _mm_tn(qn, dq_lin)
        dcq, dqnw = _rms_bwd(_mm_nt(dq_lin, wq_ref[...]), cqn, rq, qnw)
        dqnw_ref[...] += dqnw
        kvnw = kvnw_ref[...]
        kvn, ckvn, rkv = _rms_fwd(ckv_ref[...], kvnw)
        dv_ = dv_ref[...]
        dwk_ref[...] += _mm_tn(kvn, dk_all)
        dwv_ref[...] += _mm_tn(kvn, dv_)
        dckv, dkvnw = _rms_bwd(_mm_nt(dk_all, wk_ref[...]) + _mm_nt(dv_, wv_ref[...]), ckvn, rkv, kvnw)
        dkvnw_ref[...] += dkvnw
        dxp_a = jnp.concatenate([dcq, dckv, dkr], axis=-1)
        add_win_grad(_mm_tn(dxp_a, u), 0)
        dx_u, dwpre = _rms_bwd(du + _mm(dxp_a, win_ref[:lo, :]), xn, rx, wpre)
        dwpre_ref[...] += dwpre
        dx_ref[...] = dh1_ref[...] + dx_u

    row = lambda w: pl.BlockSpec((tt, w), lambda i: (i, 0))
    full = lambda a: pl.BlockSpec(a.shape, lambda i: (0,) * a.ndim)
    sds = jax.ShapeDtypeStruct
    qk_w = MLA_HEADS * HEAD_PAD
    return pl.pallas_call(
        body, name="in_bwd", grid=(T // tt,),
        in_specs=[row(D_MODEL), row(D_MODEL), row(Q_RANK), row(KV_RANK), pl.BlockSpec((qk_w, tt), lambda i: (0, i)),
                  row(qk_w), row(MLA_WIDTH),
                  row(HGRN_WIDTH), row(HGRN_WIDTH), row(HGRN_WIDTH), row(HGRN_WIDTH), row(HEAD_PAD), row(HEAD_PAD),
                  full(w_pre), full(win), full(qnw), full(wq), full(kvnw), full(wk), full(wv)],
        out_specs=[row(D_MODEL), pl.BlockSpec(WIN_COMM_SHAPE, lambda i: (0, 0, 0)), full(wq), full(wk), full(wv),
                   full(w_pre), full(qnw), full(kvnw)],
        out_shape=[sds((T, D_MODEL), F32), sds(WIN_COMM_SHAPE, F32), sds(wq.shape, F32), sds(wk.shape, F32),
                   sds(wv.shape, F32), sds(w_pre.shape, F32), sds(qnw.shape, F32), sds(kvnw.shape, F32)],
        compiler_params=_params(("arbitrary",)),
    )(x, dh1, cq, ckv, dq, dk, dv, dhq, dhf, dhi, dhg, rc, rs, w_pre, win, qnw, wq, kvnw, wk, wv)


def _arrange_weights(win_t, wuq_full, wukv):
    dt = win_t.dtype
    z = lambda n: jnp.zeros((n, D_MODEL), dt)
    s2 = Q_RANK + KV_RANK
    win_arr = jnp.concatenate([win_t[:s2], z(MLA_NOPE), win_t[s2:s2 + MLA_ROPE], z(HEAD_PAD - MLA_QK),
                               win_t[s2 + MLA_ROPE:]], axis=0)
    wq_arr = jnp.pad(wuq_full, ((0, 0), (0, 0), (0, HEAD_PAD - MLA_QK))).reshape(Q_RANK, MLA_HEADS * HEAD_PAD)
    wk_arr = jnp.pad(wukv[:, :, :MLA_NOPE], ((0, 0), (0, 0), (0, HEAD_PAD - MLA_NOPE))).reshape(
        KV_RANK, MLA_HEADS * HEAD_PAD)
    wv_arr = wukv[:, :, MLA_NOPE:].reshape(KV_RANK, MLA_WIDTH)
    return win_arr, wq_arr, wk_arr, wv_arr


WIN_COMM_SHAPE = (N_CHIPS, FF_SHARD, D_MODEL)


def _win_grad_segments():
    s2 = Q_RANK + KV_RANK
    runs = [(0, s2, 0), (s2, s2 + MLA_ROPE, MLA_NOPE), (s2 + MLA_ROPE, D_IN, HEAD_PAD - MLA_ROPE)]
    per = D_IN // N_CHIPS
    segs = []
    for lo, hi, shift in runs:
        for k in range(N_CHIPS):
            a, b = max(lo, per * k), min(hi, per * (k + 1))
            if a < b:
                segs.append((a + shift, b - a, k, a - per * k))
    return segs


def _unarrange_grads(dwq_arr, dwk_arr, dwv_arr):
    dwuq = dwq_arr.reshape(Q_RANK, MLA_HEADS, HEAD_PAD)[:, :, :MLA_QK]
    dwukv = jnp.concatenate([dwk_arr.reshape(KV_RANK, MLA_HEADS, HEAD_PAD)[:, :, :MLA_NOPE],
                             dwv_arr.reshape(KV_RANK, MLA_HEADS, MLA_V)], axis=-1)
    return dwuq, dwukv


def _rope_inv_freq():
    inv = 1.0 / (ROPE_THETA ** (jnp.arange(0, MLA_ROPE, 2, dtype=F32) / MLA_ROPE))
    z = lambda n: jnp.zeros((n,), F32)
    return jnp.concatenate([z(MLA_NOPE), inv, inv, z(HEAD_PAD - MLA_QK)]).reshape(1, HEAD_PAD)


def _local_step(x, pos, tgt, small, win_arr, wq_arr, wk_arr, wv_arr, late, place=None):
    invf = _rope_inv_freq()
    cq, ckv, xph, qb, kb, vb, kt, vt, rc, rs = _in_fwd(x, pos, invf, small["attn_pre_norm"], win_arr, small["mla_q_norm"],
                                               wq_arr, small["mla_kv_norm"], wk_arr, wv_arr)
    if place is None:
        o_raw, lse = _attn_fwd_t(qb, kb, vt)
        wout, wg, wu, wd = late
    else:
        o_raw, lse, *stacks = _attn_fwd_t(qb, kb, vt, gather=late)
        wout, wg, wu, wd = [lax.dynamic_update_slice(s, l[None], (place[1], 0, 0)) for s, l in zip(stacks, late)]
        wout = wout.reshape(D_MODEL, D_MODEL)
    oh_raw, states = _hgrn_fwd(xph, small["hgrn_lb_logits"])
    h1, y1, zb, mixb = _proj_fwd(x, o_raw, oh_raw, xph, wout, small["mla_out_norm"], small["hgrn_out_norm"],
                                 small["attn_post_norm"], small["ffn_pre_norm"])
    g, up, dy2b, dh2, loss_acc, d_fpost = _ffn_fwd(zb, h1, tgt, small["ffn_post_norm"], wg, wu, wd)
    dwg, dwu, dwd, dzp = _ffn_bwd(zb, g, up, dy2b, wg, wu, wd)
    ffn_grads = [] if place is None else [dwg, dwu, dwd]
    dh1, dwout, d_o, d_oh, dhg, dvec, d_fpre, d_post, d_mla, d_hg, *ffn_rs = _mid_bwd(
        dzp, dh2, h1, y1, mixb, o_raw, oh_raw, xph, wout, small["ffn_pre_norm"], small["attn_post_norm"],
        small["mla_out_norm"], small["hgrn_out_norm"], swap=ffn_grads)
    if ffn_grads:
        ffn_grads = ffn_grads + [dwout.reshape(N_CHIPS, D_MODEL // N_CHIPS, D_MODEL)]
        ffn_rs += _pair_swap(ffn_grads[3:], (), "pair_swap_w_out")
    ffn_ps = _pair_sum(place, ffn_grads, ffn_rs, name="pair_sum_ffn") if ffn_grads else []
    dq, dk, dv, *ffn_ris = _attn_bwd_t(qb, kb, kt, vb, d_o, lse, dvec.reshape(lse.shape), send=ffn_ps)
    ffn_sums = _chip_sum(place, ffn_grads, ffn_rs, ffn_ris, name="chip_sum_ffn") if ffn_grads else []
    dhq, dhf, dhi, d_lbl, *ffn_final = _hgrn_bwd(xph, small["hgrn_lb_logits"], states, d_oh, fill=ffn_sums)
    dx, dwin4, dwq_arr, dwk_arr, dwv_arr, d_pre, d_qn, d_kvn = _in_bwd(
        x, dh1, cq, ckv, dq, dk, dv, dhq, dhf, dhi, dhg, rc, rs, small["attn_pre_norm"], win_arr,
        small["mla_q_norm"], wq_arr, small["mla_kv_norm"], wk_arr, wv_arr)
    dwuq, dwukv = _unarrange_grads(dwq_arr, dwk_arr, dwv_arr)
    loss = 0.5 * jnp.sum(loss_acc) * (1.0 / D_MODEL)
    grads = dict(attn_pre_norm=d_pre, w_in=dwin4, mla_q_norm=d_qn, mla_w_uq=dwuq, mla_kv_norm=d_kvn, mla_w_ukv=dwukv,
                 mla_out_norm=d_mla, hgrn_lb_logits=d_lbl, hgrn_out_norm=d_hg, w_out=dwout, attn_post_norm=d_post,
                 ffn_pre_norm=d_fpre, w_gate=dwg, w_up=dwu, w_down=dwd, ffn_post_norm=d_fpost)
    if place is None:
        return loss, dx, grads
    return loss, dx, grads, ffn_final


def _place():
    x, y, c = lax.axis_index("x"), lax.axis_index("y"), lax.axis_index("c")
    others = [(1 - x, y), (x, 1 - y), (1 - x, 1 - y)]
    return x, y, c, 2 * x + y, (x, y, 1 - c), others


def _half(ref, c, rows):
    return ref.at[pl.ds(pl.multiple_of(c * rows, 8), rows)]


def _rcopy(src, dst, send, recv, k, to):
    return pltpu.make_async_remote_copy(src_ref=src, dst_ref=dst, send_sem=send.at[k], recv_sem=recv.at[k],
                                        device_id=to, device_id_type=MESH)


class _Gather:
    def __init__(self, ins, outs, send, recv):
        self.ins, self.outs, self.send, self.recv = ins, outs, send, recv
        self.n = len(ins)
        self.halves = [r.shape[0] // 2 for r in ins]
        _, _, self.c, self.me, self.sib, self.others = _place()

    def _each(self):
        for j, (px, py) in enumerate(self.others):
            for a in range(self.n):
                yield j * self.n + a, a, 2 * px + py, (px, py, self.c)

    def sends(self):
        return [_rcopy(_half(self.ins[a], self.c, self.halves[a]), _half(self.outs[a].at[self.me], self.c, self.halves[a]),
                       self.send, self.recv, k, to) for k, a, _, to in self._each()]

    def arrivals(self):
        parts = [(k, _half(self.outs[a].at[chip], self.c, self.halves[a]), to) for k, a, chip, to in self._each()]
        return [_rcopy(p, p, self.send, self.recv, k, to) for k, p, to in parts]

    def forwards(self):
        parts = [(k, _half(self.outs[a].at[chip], self.c, self.halves[a])) for k, a, chip, _ in self._each()]
        return [_rcopy(p, p, self.send, self.recv, 3 * self.n + k, self.sib) for k, p in parts]

    def forward_arrivals(self):
        parts = [(k, _half(self.outs[a].at[chip], 1 - self.c, self.halves[a])) for k, a, chip, _ in self._each()]
        return [_rcopy(p, p, self.send, self.recv, 3 * self.n + k, self.sib) for k, p in parts]

    @staticmethod
    def out_shapes(arrs):
        return [jax.ShapeDtypeStruct((N_CHIPS,) + a.shape, a.dtype) for a in arrs]

    @staticmethod
    def semaphores(arrs):
        return [pltpu.SemaphoreType.DMA((6 * len(arrs),)), pltpu.SemaphoreType.DMA((6 * len(arrs),))]


def _gather_chips(arrs, name):
    n = len(arrs)

    def body(*refs):
        gat = _Gather(refs[:n], refs[n:2 * n], *refs[2 * n:])
        sends, forwards = gat.sends(), gat.forwards()
        for cp in sends:
            cp.start()
        for arrival, fw in zip(gat.arrivals(), forwards):
            arrival.wait_recv()
            fw.start()
        for arrival in gat.forward_arrivals():
            arrival.wait_recv()
        for cp in sends + forwards:
            cp.wait_send()

    return pl.pallas_call(body, name=name, in_specs=[ANY] * n, out_specs=[ANY] * n, out_shape=_Gather.out_shapes(arrs),
                          scratch_shapes=_Gather.semaphores(arrs))(*arrs)


GRAD_BLOCKS = 2


def _pair_swap_copies(g_refs, r_refs, send, recv):
    _, _, c, _, sib, _ = _place()
    copies = []
    for a, (g, r) in enumerate(zip(g_refs, r_refs)):
        h = g.shape[1] // 2
        copies.append(_rcopy(g.at[:, pl.ds(pl.multiple_of((1 - c) * h, 8), h)], r, send, recv, a, sib))
    return copies


def _half_stack_shapes(gs, dtype=None):
    return [jax.ShapeDtypeStruct((N_CHIPS, g.shape[1] // 2, g.shape[2]), dtype or g.dtype) for g in gs]


def _pair_swap(gs, wholes, name):
    n, nw = len(gs), len(wholes)

    def body(*refs):
        ins, outs, (send, recv) = refs[:n + nw], refs[n + nw:2 * (n + nw)], refs[2 * (n + nw):]
        copies = _pair_swap_copies(ins[:n], outs[:n], send, recv)
        copies += [_rcopy(ins[n + k], outs[n + k], send, recv, n + k, _place()[4]) for k in range(nw)]
        for cp in copies:
            cp.start()
        for cp in copies:
            cp.wait()

    return pl.pallas_call(
        body, name=name, in_specs=[ANY] * (n + nw), out_specs=[ANY] * (n + nw),
        out_shape=_half_stack_shapes(gs) + [jax.ShapeDtypeStruct(w.shape, w.dtype) for w in wholes],
        scratch_shapes=[pltpu.SemaphoreType.DMA((n + nw,)), pltpu.SemaphoreType.DMA((n + nw,))],
    )(*gs, *wholes)


def _pair_sum(place, gs, rs, small=None, name="pair_sum"):
    n = len(gs)
    nb = GRAD_BLOCKS

    def body(place_ref, *refs):
        g_refs, r_refs, p_refs = refs[:n], refs[n:2 * n], refs[-n - 1:-1] if small else refs[-n:]
        for a in range(n):
            p_refs[a][0] = (g_refs[a][0] + r_refs[a][0]).astype(p_refs[a].dtype)
        if small:
            @pl.when((pl.program_id(0) == 0) & (pl.program_id(1) == 0))
            def _():
                refs[-1][...] = refs[2 * n][...] + refs[2 * n + 1][...]

    in_specs, out_specs = [], []
    for g in gs:
        blk = (1, g.shape[1] // 2 // nb, g.shape[2])
        in_specs.append(pl.BlockSpec(blk, lambda i, k, p: (k, p[0] * nb + i, 0)))
    for g in gs:
        blk = (1, g.shape[1] // 2 // nb, g.shape[2])
        in_specs.append(pl.BlockSpec(blk, lambda i, k, p: (k, i, 0)))
        out_specs.append(pl.BlockSpec(blk, lambda i, k, p: (k, i, 0)))
    out_shape = _half_stack_shapes(gs, BF16)
    if small:
        sm_spec = pl.BlockSpec(small[0].shape, lambda i, k, p: (0, 0))
        in_specs += [sm_spec, sm_spec]
        out_specs.append(sm_spec)
        out_shape.append(jax.ShapeDtypeStruct(small[0].shape, F32))
    return pl.pallas_call(
        body, name=name,
        grid_spec=pltpu.PrefetchScalarGridSpec(num_scalar_prefetch=1, grid=(nb, N_CHIPS), in_specs=in_specs,
                                               out_specs=out_specs),
        out_shape=out_shape,
        compiler_params=_params(("arbitrary", "arbitrary")),
    )(place, *gs, *rs, *(small or ()))


def _chip_swap_copies(p_refs, ri_refs, send, recv):
    _, _, c, _, _, others = _place()
    n = len(p_refs)
    return [_rcopy(p_refs[a].at[2 * px + py], ri_refs[a].at[j], send, recv, j * n + a, (px, py, c))
            for j, (px, py) in enumerate(others) for a in range(n)]


def _chip_swap_shapes(ps):
    return [jax.ShapeDtypeStruct((3,) + p.shape[1:], p.dtype) for p in ps]


def _chip_swap(ps, pair):
    n = len(ps)

    def body(*refs):
        start, finish = _chip_swap_plan(refs[:n], refs[n], refs[n + 1:2 * n + 1], refs[2 * n + 1], *refs[2 * n + 2:])
        start()
        finish()

    return pl.pallas_call(
        body, name="chip_swap", in_specs=[ANY] * (n + 1), out_specs=[ANY] * (n + 1),
        out_shape=_chip_swap_out_shapes(ps, pair), scratch_shapes=_chip_swap_semaphores(n),
    )(*ps, pair)


def _chip_swap_plan(p_refs, pair_ref, ri_refs, sm4_ref, send, recv, lsem):
    n = len(p_refs)
    hs = SMALL_ROWS // 2
    x, y, c, me, sib, others = _place()
    local = pltpu.make_async_copy(pair_ref, sm4_ref.at[me], lsem.at[0])
    copies = _chip_swap_copies(p_refs, ri_refs, send, recv)
    arrivals = list(copies)
    for j, (px, py) in enumerate(others):
        copies.append(_rcopy(_half(pair_ref, c, hs), _half(sm4_ref.at[me], c, hs), send, recv, 3 * n + j, (px, py, c)))
        part = _half(sm4_ref.at[2 * px + py], c, hs)
        arrivals.append(_rcopy(part, part, send, recv, 3 * n + j, (px, py, c)))

    def start():
        local.start()
        for cp in copies:
            cp.start()

    def finish():
        for arrival in arrivals:
            arrival.wait_recv()
        for cp in copies:
            cp.wait_send()
        local.wait()

    return start, finish


def _chip_swap_out_shapes(ps, pair):
    return _chip_swap_shapes(ps) + [jax.ShapeDtypeStruct((N_CHIPS,) + pair.shape, pair.dtype)]


def _chip_swap_semaphores(n):
    k = 3 * (n + 1)
    return [pltpu.SemaphoreType.DMA((k,)), pltpu.SemaphoreType.DMA((k,)), pltpu.SemaphoreType.DMA((1,))]


def _chip_sum(place, gs, rs, ris, name="chip_sum"):
    n = len(gs)
    nb = GRAD_BLOCKS

    def body(place_ref, *refs):
        g_refs, r_refs, ri_refs, o_refs = refs[:n], refs[n:2 * n], refs[2 * n:3 * n], refs[3 * n:]
        for a in range(n):
            ri = ri_refs[a]
            o_refs[a][...] = (g_refs[a][0] + r_refs[a][0]) + ri[0].astype(F32) + ri[1].astype(F32) + ri[2].astype(F32)

    in_specs, out_specs, out_shape = [], [], []
    for g in gs:
        blk = (1, g.shape[1] // 2 // nb, g.shape[2])
        in_specs.append(pl.BlockSpec(blk, lambda i, p: (p[1], p[0] * nb + i, 0)))
    for g in gs:
        blk = (1, g.shape[1] // 2 // nb, g.shape[2])
        in_specs.append(pl.BlockSpec(blk, lambda i, p: (p[1], i, 0)))
    for g in gs:
        rb = g.shape[1] // 2 // nb
        in_specs.append(pl.BlockSpec((3, rb, g.shape[2]), lambda i, p: (0, i, 0)))
        out_specs.append(pl.BlockSpec((rb, g.shape[2]), lambda i, p: (p[0] * nb + i, 0)))
        out_shape.append(jax.ShapeDtypeStruct(g.shape[1:], F32))
    return pl.pallas_call(
        body, name=name,
        grid_spec=pltpu.PrefetchScalarGridSpec(num_scalar_prefetch=1, grid=(nb,), in_specs=in_specs, out_specs=out_specs),
        out_shape=out_shape,
        compiler_params=_params(("arbitrary",)),
    )(place, *gs, *rs, *ris)


def _pair_fill_copies(g_refs, send, recv):
    _, _, c, _, sib, _ = _place()
    copies, waits = [], []
    for a, g in enumerate(g_refs):
        h = g.shape[0] // 2
        mine, theirs = _half(g, c, h), _half(g, 1 - c, h)
        copies.append(_rcopy(mine, mine, send, recv, a, sib))
        waits.append(_rcopy(theirs, theirs, send, recv, a, sib))
    return copies, waits


def _pair_fill(gfs, sm4):
    n = len(gfs)
    hs = SMALL_ROWS // 2

    def body(*refs):
        g_refs, sm4_ref = refs[n + 1:2 * n + 1], refs[2 * n + 1]
        send, recv = refs[2 * n + 2:]
        x, y, c, me, sib, others = _place()
        copies, waits = _pair_fill_copies(g_refs, send, recv)
        for j, (px, py) in enumerate(others):
            chip = 2 * px + py
            mine, theirs = _half(sm4_ref.at[chip], c, hs), _half(sm4_ref.at[chip], 1 - c, hs)
            copies.append(pltpu.make_async_remote_copy(src_ref=mine, dst_ref=mine, send_sem=send.at[n + j],
                                                       recv_sem=recv.at[n + j], device_id=sib, device_id_type=MESH))
            waits.append(pltpu.make_async_remote_copy(src_ref=theirs, dst_ref=theirs, send_sem=send.at[n + j],
                                                      recv_sem=recv.at[n + j], device_id=sib, device_id_type=MESH))
        for cp in copies:
            cp.start()
        for w in waits:
            w.wait_recv()
        for cp in copies:
            cp.wait_send()

    return pl.pallas_call(
        body, name="pair_fill", in_specs=[ANY] * (n + 1), out_specs=[ANY] * (n + 1),
        out_shape=[jax.ShapeDtypeStruct(g.shape, g.dtype) for g in gfs] + [jax.ShapeDtypeStruct(sm4.shape, sm4.dtype)],
        input_output_aliases={i: i for i in range(n + 1)},
        scratch_shapes=[pltpu.SemaphoreType.DMA((n + 3,)), pltpu.SemaphoreType.DMA((n + 3,))],
    )(*gfs, sm4)


def _adamw_math(w, g, m, v):
    m = ADAM_B1 * m + (1.0 - ADAM_B1) * g
    v = ADAM_B2 * v + (1.0 - ADAM_B2) * (g * g)
    m_hat = m / (1.0 - ADAM_B1 ** ADAM_STEP)
    v_hat = v / (1.0 - ADAM_B2 ** ADAM_STEP)
    return -ADAM_LR * (m_hat / (jnp.sqrt(v_hat) + ADAM_EPS) + ADAM_WD * w), m, v


def _adamw(items, steps, name):
    n = len(items)

    def body(*refs):
        for a in range(n):
            g = refs[4 * a + 1][...]
            d, mo, vo = _adamw_math(refs[4 * a][...], g, refs[4 * a + 2][...], refs[4 * a + 3][...])
            for out, val in zip(refs[4 * n + 4 * a:4 * n + 4 * a + 4], (g, d, mo, vo)):
                out[...] = val

    spec = lambda w: pl.BlockSpec((w.shape[0] // steps, w.shape[1]), lambda i: (i, 0))
    flat = pl.pallas_call(
        body, name=name, grid=(steps,), in_specs=[spec(it[0]) for it in items for _ in range(4)],
        out_specs=[spec(it[0]) for it in items for _ in range(4)],
        out_shape=[jax.ShapeDtypeStruct(it[0].shape, F32) for it in items for _ in range(4)],
        compiler_params=_params(("arbitrary",)),
    )(*[a for it in items for a in it])
    return [flat[4 * a:4 * a + 4] for a in range(n)]


def _adamw_small(sm4, wmv):
    views = SMALL_VIEWS[:-1]
    n = len(views)

    def body(sm4_ref, *refs):
        g_all = ((sm4_ref[0] + sm4_ref[1]) + sm4_ref[2]) + sm4_ref[3]
        for a, (name, rows, cols) in enumerate(views):
            row = SMALL_OFFSETS[name]
            g = g_all[row:row + rows, :cols]
            d, mo, vo = _adamw_math(refs[3 * a][...], g, refs[3 * a + 1][...], refs[3 * a + 2][...])
            for out, val in zip(refs[3 * n + 4 * a:3 * n + 4 * a + 4], (g, d, mo, vo)):
                out[...] = val
        row = SMALL_OFFSETS["loss"]
        refs[-1][...] = g_all[row:row + 1, :128]

    flat = pl.pallas_call(
        body, name="adamw_small",
        out_shape=[jax.ShapeDtypeStruct((rows, cols), F32) for _, rows, cols in views for _ in range(4)]
        + [jax.ShapeDtypeStruct((1, 128), F32)],
        compiler_params=pltpu.CompilerParams(vmem_limit_bytes=VMEM_LIMIT),
    )(sm4, *[a for t in wmv for a in t])
    return [flat[4 * a:4 * a + 4] for a in range(n)] + [flat[-1]]


SMALL_NAMES = ("attn_pre_norm", "mla_q_norm", "mla_kv_norm", "mla_w_ukv", "mla_out_norm", "hgrn_lb_logits",
               "hgrn_out_norm", "attn_post_norm", "ffn_pre_norm", "ffn_post_norm")
BIG_NAMES = ("w_in", "mla_w_uq", "w_out", "w_gate", "w_up", "w_down")
WEIGHT_NAMES = ("attn_pre_norm", "w_in", "mla_q_norm", "mla_w_uq", "mla_kv_norm", "mla_w_ukv", "mla_out_norm",
                "hgrn_lb_logits", "hgrn_out_norm", "w_out", "attn_post_norm", "ffn_pre_norm", "w_gate", "w_up", "w_down",
                "ffn_post_norm")


UQ_COMM_SHAPE = (192, 384)


def _pack_small(vals):
    parts, row = [], 0
    for name, rows, cols in sorted(SMALL_VIEWS, key=lambda view: SMALL_OFFSETS[view[0]]):
        assert SMALL_OFFSETS[name] == row
        parts.append(jnp.pad(vals[name].reshape(rows, cols), ((0, 0), (0, D_MODEL - cols))))
        row += rows
    parts.append(jnp.zeros((SMALL_ROWS - row, D_MODEL), F32))
    return jnp.concatenate(parts, axis=0)


def kernel(x, positions, attn_pre_norm, w_in, mla_q_norm, mla_w_uq, mla_kv_norm, mla_w_ukv, mla_out_norm, hgrn_lb_logits, hgrn_out_norm, w_out, attn_post_norm, ffn_pre_norm, w_gate, w_up, w_down, ffn_post_norm, loss_target, m_attn_pre_norm, m_w_in, m_mla_q_norm, m_mla_w_uq, m_mla_kv_norm, m_mla_w_ukv, m_mla_out_norm, m_hgrn_lb_logits, m_hgrn_out_norm, m_w_out, m_attn_post_norm, m_ffn_pre_norm, m_w_gate, m_w_up, m_w_down, m_ffn_post_norm, v_attn_pre_norm, v_w_in, v_mla_q_norm, v_mla_w_uq, v_mla_kv_norm, v_mla_w_ukv, v_mla_out_norm, v_hgrn_lb_logits, v_hgrn_out_norm, v_w_out, v_attn_post_norm, v_ffn_pre_norm, v_w_gate, v_w_up, v_w_down, v_ffn_post_norm):
    args = locals()
    W = {n: args[n] for n in WEIGHT_NAMES}
    M = {n: args["m_" + n] for n in WEIGHT_NAMES}
    V = {n: args["v_" + n] for n in WEIGHT_NAMES}
    T = x.shape[1]
    cx, cy, cc = lax.axis_index("x"), lax.axis_index("y"), lax.axis_index("c")

    win_rows = D_IN // N_CHIPS
    shard2d = {"w_in": (win_rows, D_MODEL), "mla_w_uq": (Q_RANK // N_CHIPS, MLA_HEADS * MLA_QK),
               "w_out": (D_MODEL // N_CHIPS, D_MODEL), "w_gate": (FF_SHARD, D_MODEL), "w_up": (FF_SHARD, D_MODEL),
               "w_down": (FF_SHARD, D_MODEL)}
    transposed = ("w_in", "w_gate", "w_up")
    to2d = lambda n, a: a[0].T if n in transposed else a.reshape(shard2d[n])
    from2d = lambda n, t: t.T[None] if n in transposed else t.reshape(W[n].shape)
    me = 2 * cx + cy
    place = jnp.stack([cc, me]).astype(jnp.int32)
    local_b = [to2d(n, W[n]).astype(BF16) for n in BIG_NAMES]
    local_b[0] = jnp.pad(local_b[0], ((0, FF_SHARD - win_rows), (0, 0)))
    stacks = _gather_chips(local_b[:2], "gather_weights")
    win4, wuq4 = [lax.dynamic_update_slice(s, l[None], (me, 0, 0)) for s, l in zip(stacks, local_b)]
    win_t = win4[:, :win_rows].reshape(D_IN, D_MODEL)
    wuq_full = wuq4.reshape(Q_RANK, MLA_HEADS, MLA_QK)
    win_arr, wq_arr, wk_arr, wv_arr = _arrange_weights(win_t, wuq_full, mla_w_ukv[0].astype(BF16))
    small = {n: W[n][0] if n == "mla_w_ukv" else W[n].reshape(-1, W[n].shape[-1]) for n in SMALL_NAMES}

    loss_local, dx, grads, ffn_final = _local_step(x[0], positions.reshape(T, 1), loss_target[0], small, win_arr,
                                                           wq_arr, wk_arr, wv_arr, local_b[2:], place)

    gs = [grads["w_in"], grads["mla_w_uq"].reshape((N_CHIPS,) + UQ_COMM_SHAPE)]
    sm = _pack_small({**grads, "loss": loss_local})
    *rs, ssib = _pair_swap(gs, (sm,), "pair_swap")
    *ps, pair = _pair_sum(place, gs, rs, small=(sm, ssib))
    ffn_names, rest_names = BIG_NAMES[3:], BIG_NAMES[:2]
    g2d = dict(zip(ffn_names + BIG_NAMES[2:3], ffn_final))
    adam_in = lambda names_: [(to2d(n, W[n]), g2d[n], to2d(n, M[n]), to2d(n, V[n])) for n in names_]
    updates = dict(zip(ffn_names, _adamw(adam_in(ffn_names), 8, "adamw_ffn")))
    updates.update(zip(BIG_NAMES[2:3], _adamw(adam_in(BIG_NAMES[2:3]), 8, "adamw_w_out")))
    *ris, sm4 = _chip_swap(ps, pair)
    *gfin, smf = _pair_fill(_chip_sum(place, gs, rs, ris), sm4)

    g2d.update({n: gfin[k].reshape((-1,) + shard2d[n][1:]) for k, n in enumerate(rest_names)})
    updates.update(zip(rest_names, _adamw(adam_in(rest_names), 3, "adamw_w_in")))
    G, DW, NM, NV = {}, {}, {}, {}
    for n, outs in updates.items():
        G[n], DW[n], NM[n], NV[n] = (from2d(n, t) for t in outs)
    view2d = lambda n, a: a.reshape(next((r, c) for name, r, c in SMALL_VIEWS if name == n))
    *res, loss_row = _adamw_small(smf, [tuple(view2d(n, t[n]) for t in (W, M, V)) for n in SMALL_NAMES])
    for n, outs in zip(SMALL_NAMES, res):
        G[n], DW[n], NM[n], NV[n] = (t.reshape(W[n].shape) for t in outs)
    loss = loss_row[0, 0]
    return (loss, dx[None], *[G[n] for n in WEIGHT_NAMES], *[DW[n] for n in WEIGHT_NAMES],
            *[NM[n] for n in WEIGHT_NAMES], *[NV[n] for n in WEIGHT_NAMES])
```

```python
import jax
import jax.numpy as jnp
from jax import lax
from jax.experimental import pallas as pl
from jax.experimental.pallas import tpu as pltpu

F32 = jnp.float32
BF16 = jnp.bfloat16
MXU_DTYPE = BF16

D_MODEL = 1024
MLA_HEADS = 8
MLA_NOPE = 64
MLA_ROPE = 32
MLA_V = 64
MLA_QK = MLA_NOPE + MLA_ROPE
Q_RANK = 384
KV_RANK = 128
MLA_WIDTH = MLA_HEADS * MLA_V
HEAD_PAD = 128
HGRN_HEADS = 4
HGRN_DIM = 128
HGRN_WIDTH = HGRN_HEADS * HGRN_DIM
CHUNK = 64
SUB = 16
HGRN_CPI = 4
D_IN = Q_RANK + KV_RANK + MLA_ROPE + 4 * HGRN_WIDTH
D_IN_ARR = Q_RANK + KV_RANK + HEAD_PAD + 4 * HGRN_WIDTH
D_FF = 2816
N_CHIPS = 4
FF_SHARD = D_FF // N_CHIPS
EPS = 1e-6
ROPE_THETA = 10000.0
ATTN_SCALE = MLA_QK ** -0.5
ATTN_SCALE_LOG2 = ATTN_SCALE * 1.4426950408889634
NEG_BIG = -1e30

ADAM_LR = 0.001
ADAM_B1 = 0.9
ADAM_B2 = 0.999
ADAM_EPS = 1e-08
ADAM_WD = 0.01
ADAM_STEP = 10

VMEM_LIMIT = 56 * 1024 * 1024

SMALL_VIEWS = (("attn_pre_norm", 1, 1024), ("mla_q_norm", 1, 384), ("mla_kv_norm", 1, 128), ("mla_w_ukv", 128, 1024),
               ("mla_out_norm", 1, 512), ("hgrn_lb_logits", 2, 512), ("hgrn_out_norm", 1, 512),
               ("attn_post_norm", 1, 1024), ("ffn_pre_norm", 1, 1024), ("ffn_post_norm", 1, 1024), ("loss", 1, 1))
ROW_TILE = 8


def _small_layout():
    offsets, row = {}, 0
    for whole in (True, False):
        for name, rows, _ in SMALL_VIEWS:
            if (rows % ROW_TILE == 0) == whole:
                offsets[name] = row
                row += rows
    return offsets, -(-row // (2 * ROW_TILE)) * 2 * ROW_TILE


SMALL_OFFSETS, SMALL_ROWS = _small_layout()

MESH = pl.DeviceIdType.MESH
ANY = pl.BlockSpec(memory_space=pl.ANY)


def _dot(a, b, dims, exact):
    if exact:
        return lax.dot_general(a.astype(F32), b.astype(F32), (dims, ((), ())), precision=lax.Precision.HIGH,
                               preferred_element_type=F32)
    return lax.dot_general(a.astype(MXU_DTYPE), b.astype(MXU_DTYPE), (dims, ((), ())), preferred_element_type=F32)


def _mm(a, b, exact=False):
    return _dot(a, b, ((1,), (0,)), exact)


def _mm_nt(a, b, exact=False):
    return _dot(a, b, ((1,), (1,)), exact)


def _mm_tn(a, b, exact=False):
    return _dot(a, b, ((0,), (0,)), exact)


def _rms_fwd(x, w):
    r = lax.rsqrt(jnp.mean(x * x, axis=-1, keepdims=True) + EPS)
    xn = x * r
    return xn * w, xn, r


def _rms_bwd(dy, xn, r, w):
    dxn = dy * w
    dx = r * (dxn - xn * jnp.mean(dxn * xn, axis=-1, keepdims=True))
    dw = jnp.sum(dy * xn, axis=0, keepdims=True)
    return dx, dw


def _group_sums(v, gs):
    t, n = v.shape
    lane = lax.broadcasted_iota(jnp.int32, (t, 128), 1)
    out = []
    for p in range(n // 128):
        vb = v[:, 128 * p:128 * (p + 1)]
        if gs == 128:
            out.append(jnp.sum(vb, axis=-1, keepdims=True))
        else:
            out.append(jnp.sum(jnp.where(lane < 64, vb, 0.0), axis=-1, keepdims=True))
            out.append(jnp.sum(jnp.where(lane >= 64, vb, 0.0), axis=-1, keepdims=True))
    return out


def _group_bcast(sums, gs, t):
    lane = lax.broadcasted_iota(jnp.int32, (t, 128), 1)
    if gs == 128:
        return jnp.concatenate([jnp.broadcast_to(s, (t, 128)) for s in sums], axis=-1)
    return jnp.concatenate([jnp.where(lane < 64, sums[2 * p], sums[2 * p + 1]) for p in range(len(sums) // 2)],
                           axis=-1)


def _grms_fwd(x, w, gs):
    t = x.shape[0]
    r = lax.rsqrt(_group_bcast(_group_sums(x * x, gs), gs, t) * (1.0 / gs) + EPS)
    xn = x * r
    return xn * w, xn, r


def _grms_bwd(dy, xn, r, w, gs):
    t = dy.shape[0]
    dxn = dy * w
    dx = r * (dxn - xn * (_group_bcast(_group_sums(dxn * xn, gs), gs, t) * (1.0 / gs)))
    dw = jnp.sum(dy * xn, axis=0, keepdims=True)
    return dx, dw


def _rope_tables(c_tab, s_tab):
    lane = lax.broadcasted_iota(jnp.int32, c_tab.shape, 1)
    first = (lane >= MLA_NOPE) & (lane < MLA_NOPE + MLA_ROPE // 2)
    second = (lane >= MLA_NOPE + MLA_ROPE // 2) & (lane < MLA_QK)
    return c_tab, jnp.where(first, -s_tab, 0.0), jnp.where(second, s_tab, 0.0)


def _rope(v, c, sa, sb):
    return v * c + pltpu.roll(v, HEAD_PAD - MLA_ROPE // 2, 1) * sa + pltpu.roll(v, MLA_ROPE // 2, 1) * sb


def _rope_bwd(d, c, sa, sb):
    return d * c - pltpu.roll(d, HEAD_PAD - MLA_ROPE // 2, 1) * sa - pltpu.roll(d, MLA_ROPE // 2, 1) * sb


def _params(sem, vmem=VMEM_LIMIT):
    return pltpu.CompilerParams(dimension_semantics=sem, vmem_limit_bytes=vmem)


def _in_fwd(x, pos, invf, w_pre, win, qnw, wq, kvnw, wk, wv, tt=512):
    T = x.shape[0]

    def body(x_ref, pos_ref, invf_ref, wpre_ref, win_ref, qnw_ref, wq_ref, kvnw_ref, wk_ref, wv_ref,
             cq_ref, ckv_ref, xph_ref, q_ref, k_ref, v_ref, kt_ref, vt_ref, rc_ref, rs_ref):
        u, _, _ = _rms_fwd(x_ref[...], wpre_ref[...])
        lo = Q_RANK + KV_RANK + HEAD_PAD
        xp = _mm_nt(u, win_ref[:lo, :])
        xph_ref[...] = _mm_nt(u, win_ref[lo:, :])
        cq = xp[:, :Q_RANK]
        ckv = xp[:, Q_RANK:Q_RANK + KV_RANK]
        kr = xp[:, Q_RANK + KV_RANK:]
        cq_ref[...] = cq
        ckv_ref[...] = ckv
        ang = pos_ref[...].astype(F32) * invf_ref[...]
        c_tab = jnp.cos(ang)
        s_tab = jnp.sin(ang)
        rc_ref[...] = c_tab
        rs_ref[...] = s_tab
        c, sa, sb = _rope_tables(c_tab, s_tab)
        qn, _, _ = _rms_fwd(cq, qnw_ref[...])
        q = _mm(qn, wq_ref[...])
        kvn, _, _ = _rms_fwd(ckv, kvnw_ref[...])
        kn = _mm(kvn, wk_ref[...])
        v = _mm(kvn, wv_ref[...])
        v_ref[...] = v.astype(v_ref.dtype)
        vt_ref[...] = v.T.astype(vt_ref.dtype)
        krr = _rope(kr, c, sa, sb)
        for h in range(MLA_HEADS):
            sl = slice(HEAD_PAD * h, HEAD_PAD * (h + 1))
            q_ref[:, sl] = (_rope(q[:, sl], c, sa, sb) * ATTN_SCALE_LOG2).astype(q_ref.dtype)
            kh = kn[:, sl] + krr
            k_ref[:, sl] = kh.astype(k_ref.dtype)
            kt_ref[sl, :] = kh.T.astype(kt_ref.dtype)

    row = lambda w: pl.BlockSpec((tt, w), lambda i: (i, 0))
    full = lambda a: pl.BlockSpec(a.shape, lambda i: (0,) * a.ndim)
    qk_w = MLA_HEADS * HEAD_PAD
    return pl.pallas_call(
        body, name="in_fwd", grid=(T // tt,),
        in_specs=[row(D_MODEL), row(1), full(invf), full(w_pre), full(win), full(qnw), full(wq), full(kvnw),
                  full(wk), full(wv)],
        out_specs=[row(Q_RANK), row(KV_RANK), row(4 * HGRN_WIDTH), row(qk_w), row(qk_w), row(MLA_WIDTH),
                   pl.BlockSpec((qk_w, tt), lambda i: (0, i)), pl.BlockSpec((MLA_WIDTH, tt), lambda i: (0, i)),
                   row(HEAD_PAD), row(HEAD_PAD)],
        out_shape=[jax.ShapeDtypeStruct((T, Q_RANK), F32), jax.ShapeDtypeStruct((T, KV_RANK), F32),
                   jax.ShapeDtypeStruct((T, 4 * HGRN_WIDTH), F32), jax.ShapeDtypeStruct((T, qk_w), MXU_DTYPE),
                   jax.ShapeDtypeStruct((T, qk_w), MXU_DTYPE), jax.ShapeDtypeStruct((T, MLA_WIDTH), MXU_DTYPE),
                   jax.ShapeDtypeStruct((qk_w, T), MXU_DTYPE), jax.ShapeDtypeStruct((MLA_WIDTH, T), MXU_DTYPE),
                   jax.ShapeDtypeStruct((T, HEAD_PAD), F32), jax.ShapeDtypeStruct((T, HEAD_PAD), F32)],
        compiler_params=_params(("arbitrary",)),
    )(x, pos, invf, w_pre, win, qnw, wq, kvnw, wk, wv)


def _attn_fwd_t(qb, kb, vt, gather=(), tq=256, hps=8):
    T = qb.shape[0]
    nq = T // tq
    ng = len(gather)
    steps = (MLA_HEADS // hps) * nq
    pass_on = steps - 3

    def body(q_ref, k_ref, vt_ref, *rest):
        o_ref, lse_ref = rest[ng:ng + 2]
        acc_scr = rest[2 * ng + 2]
        qi = pl.program_id(1)
        step_no = pl.program_id(0) * nq + qi
        if ng:
            gat = _Gather(rest[:ng], rest[ng + 2:2 * ng + 2], *rest[2 * ng + 3:])

            @pl.when(step_no == 0)
            def _():
                for cp in gat.sends():
                    cp.start()

            @pl.when(step_no == pass_on)
            def _():
                for arrival in gat.arrivals():
                    arrival.wait_recv()
                for cp in gat.forwards():
                    cp.start()

        heads = [slice(HEAD_PAD * a, HEAD_PAD * (a + 1)) for a in range(hps)]
        acc_scr[...] = jnp.zeros_like(acc_scr)

        def step(j, carry, masked):
            start = pl.multiple_of(j * tq, tq)
            scores = [_mm_nt(k_ref[pl.ds(start, tq), heads[a]], q_ref[:, heads[a]]) for a in range(hps)]
            new, probs, alphas = [], [], []
            for a in range(hps):
                m, l = carry[a]
                s = scores[a]
                if masked:
                    kk = lax.broadcasted_iota(jnp.int32, (tq, tq), 0)
                    qq = lax.broadcasted_iota(jnp.int32, (tq, tq), 1)
                    s = jnp.where(kk <= qq, s, NEG_BIG)
                m_new = jnp.maximum(m, jnp.max(s, axis=0, keepdims=True))
                alpha = jnp.exp2(m - m_new)
                p = jnp.exp2(s - m_new)
                l = l * alpha + jnp.sum(p, axis=0, keepdims=True)
                new.append((m_new, l))
                probs.append(p.astype(MXU_DTYPE))
                alphas.append(alpha)
                if a % 2:
                    pr = a // 2
                    vtj = vt_ref[2 * MLA_V * pr:2 * MLA_V * (pr + 1), pl.ds(start, tq)]
                    none = jnp.zeros((MLA_V, tq), vtj.dtype)
                    pv = (_mm(jnp.concatenate([vtj[:MLA_V], none], axis=0), probs[a - 1])
                          + _mm(jnp.concatenate([none, vtj[MLA_V:]], axis=0), probs[a]))
                    acc_scr[pr] = acc_scr[pr] * jnp.where(row < MLA_V, alphas[a - 1], alphas[a]) + pv
            return tuple(new)

        row = lax.broadcasted_iota(jnp.int32, (2 * MLA_V, tq), 0)
        init = tuple((jnp.full((1, tq), NEG_BIG, F32), jnp.zeros((1, tq), F32)) for _ in range(hps))
        carry = lax.fori_loop(0, qi, lambda j, c: step(j, c, False), init)
        carry = step(qi, carry, True)
        for pr in range(hps // 2):
            (m0, l0), (m1, l1) = carry[2 * pr], carry[2 * pr + 1]
            ot = acc_scr[pr] / jnp.where(row < MLA_V, l0, l1)
            o_ref[:, 2 * MLA_V * pr:2 * MLA_V * (pr + 1)] = ot.T
            lse_ref[pr, 0:1, :] = m0 + jnp.log2(l0)
            lse_ref[pr, 1:2, :] = m1 + jnp.log2(l1)

        if ng:
            @pl.when(step_no == steps - 1)
            def _():
                for arrival in gat.forward_arrivals():
                    arrival.wait_recv()
                for cp in gat.sends() + gat.forwards():
                    cp.wait_send()

    return pl.pallas_call(
        body, name="attn_fwd", grid=(MLA_HEADS // hps, nq),
        in_specs=[pl.BlockSpec((tq, hps * HEAD_PAD), lambda g, i: (i, g)),
                  pl.BlockSpec((T, hps * HEAD_PAD), lambda g, i: (0, g)),
                  pl.BlockSpec((hps * MLA_V, T), lambda g, i: (g, 0))] + [ANY] * ng,
        out_specs=[pl.BlockSpec((tq, hps * MLA_V), lambda g, i: (i, g)),
                   pl.BlockSpec((hps // 2, 2, tq), lambda g, i: (g, 0, i))] + [ANY] * ng,
        out_shape=[jax.ShapeDtypeStruct((T, MLA_WIDTH), F32), jax.ShapeDtypeStruct((MLA_HEADS // 2, 2, T), F32)]
        + _Gather.out_shapes(gather),
        scratch_shapes=[pltpu.VMEM((hps // 2, 2 * MLA_V, tq), F32)] + (_Gather.semaphores(gather) if ng else []),
        compiler_params=_params(("arbitrary", "arbitrary")),
    )(qb, kb, vt, *gather)


def _attn_bwd_t(qb, kb, kt, vb, dob, lse, dvec, send=(), tq=512, hps=4):
    T = qb.shape[0]
    nq = T // tq
    ns = len(send)
    steps = (MLA_HEADS // hps) * nq

    def body(q_ref, k_ref, kt_ref, v_ref, do_ref, lse_ref, d_ref, *rest):
        dqt_ref, dk_ref, dv_ref = rest[ns:ns + 3]
        va_scr, dv_scr = rest[2 * ns + 3:2 * ns + 5]
        j = pl.program_id(1)
        step_no = pl.program_id(0) * nq + j
        if ns:
            @pl.when(step_no == 0)
            def _():
                for cp in _chip_swap_copies(rest[:ns], rest[ns + 3:2 * ns + 3], *rest[2 * ns + 5:]):
                    cp.start()

        @pl.when(j == 0)
        def _():
            dqt_ref[...] = jnp.zeros_like(dqt_ref)

        lane = lax.broadcasted_iota(jnp.int32, (tq, 2 * MLA_V), 1)
        heads = [slice(HEAD_PAD * a, HEAD_PAD * (a + 1)) for a in range(hps)]
        pairs = [slice(2 * MLA_V * p, 2 * MLA_V * (p + 1)) for p in range(hps // 2)]
        for pr in range(hps // 2):
            vpair = v_ref[:, pairs[pr]]
            va_scr[2 * pr] = jnp.where(lane < MLA_V, vpair, jnp.zeros_like(vpair))
            va_scr[2 * pr + 1] = jnp.where(lane >= MLA_V, vpair, jnp.zeros_like(vpair))
        dk_ref[...] = jnp.zeros_like(dk_ref)
        dv_scr[...] = jnp.zeros_like(dv_scr)

        def step(i, masked):
            start = pl.multiple_of(i * tq, tq)
            rows = pl.ds(start, tq)
            scores = [_mm_nt(k_ref[:, heads[a]], q_ref[rows, heads[a]]) for a in range(hps)]
            dps = [_mm_nt(va_scr[a], do_ref[rows, pairs[a // 2]]) for a in range(hps)]
            for a in range(hps):
                pr, r = a // 2, a % 2
                p = jnp.exp2(scores[a] - lse_ref[pr, r:r + 1, rows])
                if masked:
                    kk = lax.broadcasted_iota(jnp.int32, (tq, tq), 0)
                    qq = lax.broadcasted_iota(jnp.int32, (tq, tq), 1)
                    p = jnp.where(kk <= qq, p, 0.0)
                ds = p * (dps[a] - d_ref[pr, r:r + 1, rows])
                dv_scr[a] += _mm(p, do_ref[rows, pairs[pr]])
                dk_ref[:, heads[a]] += _mm(ds, q_ref[rows, heads[a]])
                dqt_ref[heads[a], rows] += _mm(kt_ref[heads[a], :], ds)

        def loop_body(i, _):
            step(i, False)
            return 0

        step(j, True)
        lax.fori_loop(j + 1, nq, loop_body, 0)
        for pr in range(hps // 2):
            dv_ref[:, pairs[pr]] = jnp.where(lane < MLA_V, dv_scr[2 * pr], dv_scr[2 * pr + 1])
        dk_ref[...] = dk_ref[...] * (ATTN_SCALE / ATTN_SCALE_LOG2)

        if ns:
            @pl.when(step_no == steps - 1)
            def _():
                for cp in _chip_swap_copies(rest[:ns], rest[ns + 3:2 * ns + 3], *rest[2 * ns + 5:]):
                    cp.wait()

    stat = pl.BlockSpec((hps // 2, 2, T), lambda g, j: (g, 0, 0))
    return pl.pallas_call(
        body, name="attn_bwd", grid=(MLA_HEADS // hps, nq),
        in_specs=[pl.BlockSpec((T, hps * HEAD_PAD), lambda g, j: (0, g)),
                  pl.BlockSpec((tq, hps * HEAD_PAD), lambda g, j: (j, g)),
                  pl.BlockSpec((hps * HEAD_PAD, tq), lambda g, j: (g, j)),
                  pl.BlockSpec((tq, hps * MLA_V), lambda g, j: (j, g)),
                  pl.BlockSpec((T, hps * MLA_V), lambda g, j: (0, g)), stat, stat] + [ANY] * ns,
        out_specs=[pl.BlockSpec((hps * HEAD_PAD, T), lambda g, j: (g, 0)),
                   pl.BlockSpec((tq, hps * HEAD_PAD), lambda g, j: (j, g)),
                   pl.BlockSpec((tq, hps * MLA_V), lambda g, j: (j, g))] + [ANY] * ns,
        out_shape=[jax.ShapeDtypeStruct((MLA_HEADS * HEAD_PAD, T), F32),
                   jax.ShapeDtypeStruct((T, MLA_HEADS * HEAD_PAD), F32),
                   jax.ShapeDtypeStruct((T, MLA_WIDTH), F32)] + _chip_swap_shapes(send),
        scratch_shapes=[pltpu.VMEM((hps, tq, 2 * MLA_V), vb.dtype), pltpu.VMEM((hps, tq, 2 * MLA_V), F32)]
        + ([pltpu.SemaphoreType.DMA((3 * ns,)), pltpu.SemaphoreType.DMA((3 * ns,))] if ns else []),
        compiler_params=_params(("arbitrary", "arbitrary")),
    )(qb, kb, kt, vb, dob, lse, dvec, *send)


def _cumsum_rows(x):
    n = x.shape[0]
    row = lax.broadcasted_iota(jnp.int32, x.shape, 0)
    s = 1
    while s < n:
        x = x + jnp.where(row >= s, pltpu.roll(x, s, 0), 0.0)
        s *= 2
    return x


def _rev_cumsum_rows(x):
    n = x.shape[0]
    row = lax.broadcasted_iota(jnp.int32, x.shape, 0)
    s = 1
    while s < n:
        x = x + jnp.where(row < n - s, pltpu.roll(x, n - s, 0), 0.0)
        s *= 2
    return x


def _lb_from_logits(l):
    l0, l1 = l[0:1, :], l[1:2, :]
    m = jnp.maximum(l0, l1)
    e0, e1 = jnp.exp(l0 - m), jnp.exp(l1 - m)
    return e0 / (e0 + e1)


def _hgrn_gates(hq, hf, lb):
    sig_f = jax.nn.sigmoid(hf)
    f = lb + (1.0 - lb) * sig_f
    sig_q = jax.nn.sigmoid(hq)
    return sig_f, f, jnp.log(f), 1.0 - f, sig_q, hq * sig_q


def _hgrn_intra(q, kk, b, exact=False):
    row = lax.broadcasted_iota(jnp.int32, b.shape, 0)
    qs, ks, eqs, eks, a_rows = [], [], [], [], []
    for i in range(CHUNK // SUB):
        ref = b[SUB * i + SUB // 2:SUB * i + SUB // 2 + 1, :]
        eq = jnp.exp(b[SUB * i:SUB * (i + 1), :] - ref)
        ek = jnp.exp(jnp.where(row < SUB * (i + 1), ref - b, NEG_BIG))
        qi = q[SUB * i:SUB * (i + 1), :] * eq
        ki = kk * ek
        a_rows.append(_mm_nt(qi, ki, exact))
        qs.append(qi), ks.append(ki), eqs.append(eq), eks.append(ek)
    tt = lax.broadcasted_iota(jnp.int32, (CHUNK, CHUNK), 0)
    ss = lax.broadcasted_iota(jnp.int32, (CHUNK, CHUNK), 1)
    causal = ss <= tt
    a = jnp.where(causal, jnp.concatenate(a_rows, axis=0), 0.0)
    return a, causal, qs, ks, eqs, eks


def _hgrn_fwd(xph, lbl, tg=512):
    T = xph.shape[0]
    ng, ncg = T // tg, tg // CHUNK
    cols = [slice(HGRN_DIM * h, HGRN_DIM * (h + 1)) for h in range(HGRN_HEADS)]

    def body(lbl_ref, hq_ref, hf_ref, hi_ref, o_ref, st_ref, s_scr):
        @pl.when(pl.program_id(0) == 0)
        def _():
            s_scr[...] = jnp.zeros_like(s_scr)

        lb = _lb_from_logits(lbl_ref[...])

        def chunks(it, _):
            pre = []
            for k in range(HGRN_CPI):
                c = it * HGRN_CPI + k
                rows = pl.ds(pl.multiple_of(c * CHUNK, CHUNK), CHUNK)
                for cs in cols:
                    _, _, lf, kk, _, q = _hgrn_gates(hq_ref[rows, cs], hf_ref[rows, cs], lb[:, cs])
                    v = hi_ref[rows, cs]
                    b = _cumsum_rows(lf)
                    a = _hgrn_intra(q, kk, b)[0]
                    b_last = b[CHUNK - 1:CHUNK, :]
                    pre.append((c, rows, q * jnp.exp(b), a, v, jnp.exp(b_last), _mm_tn(v, kk * jnp.exp(b_last - b))))
            for i, (c, rows, qe, a, v, ebl, upd) in enumerate(pre):
                h = i % HGRN_HEADS
                st = s_scr[h]
                st_ref[h, c] = st
                o_ref[rows, cols[h]] = _mm_nt(qe, st) + _mm(a, v)
                s_scr[h] = st * ebl + upd
            return 0

        lax.fori_loop(0, ncg // HGRN_CPI, chunks, 0)

    col = lambda k: pl.BlockSpec((tg, HGRN_WIDTH), lambda g: (g, k))
    return pl.pallas_call(
        body, name="hgrn_fwd", grid=(ng,),
        in_specs=[pl.BlockSpec((2, HGRN_WIDTH), lambda g: (0, 0)), col(0), col(1), col(2)],
        out_specs=[col(0), pl.BlockSpec((HGRN_HEADS, ncg, HGRN_DIM, HGRN_DIM), lambda g: (0, g, 0, 0))],
        out_shape=[jax.ShapeDtypeStruct((T, HGRN_WIDTH), F32),
                   jax.ShapeDtypeStruct((HGRN_HEADS, T // CHUNK, HGRN_DIM, HGRN_DIM), F32)],
        scratch_shapes=[pltpu.VMEM((HGRN_HEADS, HGRN_DIM, HGRN_DIM), F32)],
        compiler_params=_params(("arbitrary",)),
    )(lbl, xph, xph, xph)


def _hgrn_bwd(xph, lbl, states, d_o, fill=(), tg=512):
    T = xph.shape[0]
    ng, ncg = T // tg, tg // CHUNK
    cols = [slice(HGRN_DIM * h, HGRN_DIM * (h + 1)) for h in range(HGRN_HEADS)]
    nsub = CHUNK // SUB
    nf = len(fill)

    def body(lbl_ref, hq_ref, hf_ref, hi_ref, st_ref, do_ref, *rest):
        dhq_ref, dhf_ref, dhi_ref, dlg_ref = rest[nf:nf + 4]
        ds_scr, dlb_scr = rest[2 * nf + 4:2 * nf + 6]
        fill_copies = lambda: _pair_fill_copies(rest[nf + 4:2 * nf + 4], *rest[2 * nf + 6:])
        g = pl.program_id(0)

        @pl.when(g == 0)
        def _():
            ds_scr[...] = jnp.zeros_like(ds_scr)
            dlb_scr[...] = jnp.zeros_like(dlb_scr)
            for cp in (fill_copies()[0] if nf else ()):
                cp.start()

        lb = _lb_from_logits(lbl_ref[...])

        def chunks(it, _):
            pre = []
            for k, h in ((k, h) for k in range(HGRN_CPI) for h in range(HGRN_HEADS)):
                cs = cols[h]
                c = ncg - 1 - (it * HGRN_CPI + k)
                rows = pl.ds(pl.multiple_of(c * CHUNK, CHUNK), CHUNK)
                hq = hq_ref[rows, cs]
                sig_f, f, lf, kk, sig_q, q = _hgrn_gates(hq, hf_ref[rows, cs], lb[:, cs])
                v = hi_ref[rows, cs]
                do = do_ref[rows, cs]
                b = _cumsum_rows(lf)
                eb = jnp.exp(b)
                a, causal, qs, ks, eqs, eks = _hgrn_intra(q, kk, b)
                b_last = b[CHUNK - 1:CHUNK, :]
                st = st_ref[h, c]
                pre.append(dict(h=h, cs=cs, rows=rows, hq=hq, sig_f=sig_f, f=f, kk=kk, sig_q=sig_q, q=q, v=v, eb=eb, qs=qs,
                                ks=ks, eqs=eqs,
                                eks=eks, ebl=jnp.exp(b_last), el=jnp.exp(b_last - b), st=st,
                                da=jnp.where(causal, _mm_nt(do, v, True), 0.0), dq=_mm(do, st, True) * eb,
                                dv=_mm_tn(a, do), dsu=_mm_tn(do, q * eb, True)))
            for w in pre:
                dq_rows = []
                dk = jnp.zeros_like(w["q"])
                for i in range(nsub):
                    dai = w["da"][SUB * i:SUB * (i + 1), :]
                    dq_rows.append(_mm(dai, w["ks"][i], True) * w["eqs"][i])
                    dk = dk + _mm_tn(dai, w["qs"][i], True) * w["eks"][i]
                w["dq"] = w["dq"] + jnp.concatenate(dq_rows, axis=0)
                w["dk"] = dk
            for w in pre:
                h, cs, rows = w["h"], w["cs"], w["rows"]
                kk, el, ebl, dst = w["kk"], w["el"], w["ebl"], ds_scr[h]
                dk_state = _mm(w["v"], dst, True) * el
                dk = w["dk"] + dk_state
                e_last = (ebl * jnp.sum(w["st"] * dst, axis=0, keepdims=True)
                          + jnp.sum(kk * dk_state, axis=0, keepdims=True))
                dlf = _rev_cumsum_rows(w["q"] * w["dq"] - kk * dk) + e_last
                ds_scr[h] = dst * ebl + w["dsu"]
                df = dlf / w["f"] - dk
                sig_f, sig_q = w["sig_f"], w["sig_q"]
                dhf_ref[rows, cs] = df * (1.0 - lb[:, cs]) * sig_f * (1.0 - sig_f)
                dlb_scr[:, cs] += jnp.sum(df * (1.0 - sig_f), axis=0, keepdims=True)
                dhq_ref[rows, cs] = w["dq"] * sig_q * (1.0 + w["hq"] * (1.0 - sig_q))
                dhi_ref[rows, cs] = w["dv"] + _mm_nt(kk * el, dst)
            return 0

        lax.fori_loop(0, ncg // HGRN_CPI, chunks, 0)

        @pl.when(g == ng - 1)
        def _():
            dl0 = dlb_scr[...] * lb * (1.0 - lb)
            dlg_ref[...] = jnp.concatenate([dl0, -dl0], axis=0)
            if nf:
                copies, waits = fill_copies()
                for w in waits:
                    w.wait_recv()
                for cp in copies:
                    cp.wait_send()

    col = lambda k: pl.BlockSpec((tg, HGRN_WIDTH), lambda g: (ng - 1 - g, k))
    logits = pl.BlockSpec((2, HGRN_WIDTH), lambda g: (0, 0))
    big = jax.ShapeDtypeStruct((T, HGRN_WIDTH), F32)
    n_in, n_out = 6, 4
    return pl.pallas_call(
        body, name="hgrn_bwd", grid=(ng,),
        in_specs=[logits, col(0), col(1), col(2),
                  pl.BlockSpec((HGRN_HEADS, ncg, HGRN_DIM, HGRN_DIM), lambda g: (0, ng - 1 - g, 0, 0)), col(0)] + [ANY] * nf,
        out_specs=[col(0), col(0), col(0), logits] + [ANY] * nf,
        out_shape=[big, big, big, jax.ShapeDtypeStruct((2, HGRN_WIDTH), F32)]
        + [jax.ShapeDtypeStruct(f.shape, f.dtype) for f in fill],
        input_output_aliases={n_in + k: n_out + k for k in range(nf)},
        scratch_shapes=[pltpu.VMEM((HGRN_HEADS, HGRN_DIM, HGRN_DIM), F32), pltpu.VMEM((1, HGRN_WIDTH), F32)]
        + ([pltpu.SemaphoreType.DMA((nf,)), pltpu.SemaphoreType.DMA((nf,))] if nf else []),
        compiler_params=_params(("arbitrary",)),
    )(lbl, xph, xph, xph, states, d_o, *fill)


def _proj_fwd(x, o_raw, oh_raw, xph, wout, w_mla, w_hg, w_post, w_fpre, tt=512):
    T = x.shape[0]

    def body(x_ref, o_ref, oh_ref, hg_ref, wout_ref, wmla_ref, whg_ref, wpost_ref, wfpre_ref,
             h1_ref, y1_ref, z_ref, mix_ref):
        om, _, _ = _grms_fwd(o_ref[...], wmla_ref[...], MLA_V)
        hg = hg_ref[...]
        ohn, _, _ = _grms_fwd(oh_ref[...], whg_ref[...], HGRN_DIM)
        mix = jnp.concatenate([om, ohn * (hg * jax.nn.sigmoid(hg))], axis=-1)
        mix_ref[...] = mix.astype(mix_ref.dtype)
        y1 = _mm(mix, wout_ref[...])
        y1_ref[...] = y1
        h1 = x_ref[...] + _rms_fwd(y1, wpost_ref[...])[0]
        h1_ref[...] = h1
        z_ref[...] = _rms_fwd(h1, wfpre_ref[...])[0].astype(z_ref.dtype)

    row = lambda w: pl.BlockSpec((tt, w), lambda i: (i, 0))
    full = lambda a: pl.BlockSpec(a.shape, lambda i: (0,) * a.ndim)
    sds = jax.ShapeDtypeStruct
    return pl.pallas_call(
        body, name="proj_fwd", grid=(T // tt,),
        in_specs=[row(D_MODEL), row(MLA_WIDTH), row(HGRN_WIDTH), pl.BlockSpec((tt, HGRN_WIDTH), lambda i: (i, 3)),
                  full(wout), full(w_mla), full(w_hg), full(w_post), full(w_fpre)],
        out_specs=[row(D_MODEL)] * 4,
        out_shape=[sds((T, D_MODEL), F32), sds((T, D_MODEL), F32), sds((T, D_MODEL), MXU_DTYPE),
                   sds((T, D_MODEL), MXU_DTYPE)],
        compiler_params=_params(("arbitrary",)),
    )(x, o_raw, oh_raw, xph, wout, w_mla, w_hg, w_post, w_fpre)


def _ffn_fwd(zb, h1, tgt, w_fpost, wg, wu, wd, tt=256):
    T = zb.shape[0]
    nj = N_CHIPS

    def body(z_ref, h1_ref, tgt_ref, wfpost_ref, wg_ref, wu_ref, wd_ref, g_ref, up_ref, dy2_ref, dh2_ref, loss_ref, dwf_ref):
        @pl.when(pl.program_id(0) == 0)
        def _():
            loss_ref[...] = jnp.zeros_like(loss_ref)
            dwf_ref[...] = jnp.zeros_like(dwf_ref)

        z = z_ref[...]
        gs = [_mm_nt(z, wg_ref[j]) for j in range(nj)]
        ups = [_mm_nt(z, wu_ref[j]) for j in range(nj)]
        y2 = jnp.zeros((tt, D_MODEL), F32)
        for j in range(nj):
            g_ref[j] = gs[j]
            up_ref[j] = ups[j]
            y2 = y2 + _mm(gs[j] * jax.nn.sigmoid(gs[j]) * ups[j], wd_ref[j])
        w = wfpost_ref[...]
        y2s, y2n, r2 = _rms_fwd(y2, w)
        e = h1_ref[...] + y2s - tgt_ref[...]
        loss_ref[...] += jnp.sum(e * e, axis=0, keepdims=True)
        dh2 = e * (1.0 / D_MODEL)
        dh2_ref[...] = dh2
        dy2, dwf = _rms_bwd(dh2, y2n, r2, w)
        dy2_ref[...] = dy2.astype(dy2_ref.dtype)
        dwf_ref[...] += dwf

    row = pl.BlockSpec((tt, D_MODEL), lambda i: (i, 0))
    vec = pl.BlockSpec((1, D_MODEL), lambda i: (0, 0))
    resident = pl.BlockSpec((nj, FF_SHARD, D_MODEL), lambda i: (0, 0, 0), pipeline_mode=pl.Buffered(1))
    act = pl.BlockSpec((nj, tt, FF_SHARD), lambda i: (0, i, 0))
    sds = jax.ShapeDtypeStruct
    return pl.pallas_call(
        body, name="ffn_fwd", grid=(T // tt,),
        in_specs=[row, row, row, vec, resident, resident, resident],
        out_specs=[act, act, row, row, vec, vec],
        out_shape=[sds((nj, T, FF_SHARD), F32), sds((nj, T, FF_SHARD), F32), sds((T, D_MODEL), MXU_DTYPE),
                   sds((T, D_MODEL), F32), sds((1, D_MODEL), F32), sds((1, D_MODEL), F32)],
        compiler_params=_params(("arbitrary",)),
    )(zb, h1, tgt, w_fpost, wg, wu, wd)


def _ffn_bwd(zb, g, up, dy2b, wg, wu, wd, tt=512):
    T = zb.shape[0]
    nj = N_CHIPS

    def body(z_ref, g_ref, up_ref, dy2_ref, wg_ref, wu_ref, wd_ref, dwg_ref, dwu_ref, dwd_ref, dz_ref):
        @pl.when(pl.program_id(1) == 0)
        def _():
            dwg_ref[...] = jnp.zeros_like(dwg_ref)
            dwu_ref[...] = jnp.zeros_like(dwu_ref)
            dwd_ref[...] = jnp.zeros_like(dwd_ref)

        z, g_, up_, dy2 = z_ref[...], g_ref[0], up_ref[0], dy2_ref[...]
        sg = jax.nn.sigmoid(g_)
        act = g_ * sg
        dff = _mm_nt(dy2, wd_ref[0])
        dwd_ref[0] += _mm_tn(act * up_, dy2)
        dg = dff * up_ * sg * (1.0 + g_ * (1.0 - sg))
        dup = dff * act
        dwg_ref[0] += _mm_tn(dg, z)
        dwu_ref[0] += _mm_tn(dup, z)
        dz_ref[0] = _mm(dg, wg_ref[0]) + _mm(dup, wu_ref[0])

    row = pl.BlockSpec((tt, D_MODEL), lambda j, i: (i, 0))
    act = pl.BlockSpec((1, tt, FF_SHARD), lambda j, i: (j, i, 0))
    w_sh = pl.BlockSpec((1, FF_SHARD, D_MODEL), lambda j, i: (j, 0, 0))
    w_grad = jax.ShapeDtypeStruct((nj, FF_SHARD, D_MODEL), F32)
    return pl.pallas_call(
        body, name="ffn_bwd", grid=(nj, T // tt),
        in_specs=[row, act, act, row, w_sh, w_sh, w_sh],
        out_specs=[w_sh, w_sh, w_sh, pl.BlockSpec((1, tt, D_MODEL), lambda j, i: (j, i, 0))],
        out_shape=[w_grad, w_grad, w_grad, jax.ShapeDtypeStruct((nj, T, D_MODEL), F32)],
        compiler_params=_params(("arbitrary", "arbitrary")),
    )(zb, g, up, dy2b, wg, wu, wd)


def _mid_bwd(dzp, dh2, h1, y1, mixb, o_raw, oh_raw, xph, wout, w_fpre, w_post, w_mla, w_hg, swap=(), tt=256):
    T = dh2.shape[0]
    nsw = len(swap)
    n_in, n_out = 13, 10

    def body(*refs):
        (dzp_ref, dh2_ref, h1_ref, y1_ref, mix_ref, o_ref, oh_ref, hg_ref, wout_ref, wfpre_ref, wpost_ref,
         wmla_ref, whg_ref) = refs[:n_in]
        (dh1_ref, dwout_ref, do_ref, doh_ref, dhg_ref, dvec_ref, dwfpre_ref, dwpost_ref, dwmla_ref,
         dwhg_ref) = refs[n_in + nsw:n_in + nsw + n_out]
        swap_copies = lambda: _pair_swap_copies(refs[n_in:n_in + nsw], refs[n_in + nsw + n_out:n_in + 2 * nsw + n_out],
                                                *refs[n_in + 2 * nsw + n_out:])

        @pl.when(pl.program_id(0) == 0)
        def _():
            for r in (dwout_ref, dwfpre_ref, dwpost_ref, dwmla_ref, dwhg_ref):
                r[...] = jnp.zeros_like(r)
            for cp in (swap_copies() if nsw else ()):
                cp.start()

        dz = dzp_ref[0] + dzp_ref[1] + dzp_ref[2] + dzp_ref[3]
        wfpre = wfpre_ref[...]
        _, h1n, r = _rms_fwd(h1_ref[...], wfpre)
        dh1_z, dwfpre = _rms_bwd(dz, h1n, r, wfpre)
        dwfpre_ref[...] += dwfpre
        dh1 = dh2_ref[...] + dh1_z
        dh1_ref[...] = dh1
        wpost = wpost_ref[...]
        _, y1n, r1 = _rms_fwd(y1_ref[...], wpost)
        dy1, dwpost = _rms_bwd(dh1, y1n, r1, wpost)
        dwpost_ref[...] += dwpost
        dmix = _mm_nt(dy1, wout_ref[...])
        dwout_ref[...] += _mm_tn(mix_ref[...], dy1)
        wmla = wmla_ref[...]
        o = o_ref[...]
        _, on, ro = _grms_fwd(o, wmla, MLA_V)
        d_o, dwmla = _grms_bwd(dmix[:, :MLA_WIDTH], on, ro, wmla, MLA_V)
        dwmla_ref[...] += dwmla
        do_ref[...] = d_o.astype(do_ref.dtype)
        hh = lax.broadcasted_iota(jnp.int32, (MLA_HEADS, MLA_WIDTH), 0)
        ll = lax.broadcasted_iota(jnp.int32, (MLA_HEADS, MLA_WIDTH), 1)
        sel = jnp.where((ll >= hh * MLA_V) & (ll < (hh + 1) * MLA_V), 1.0, 0.0)
        dvec_ref[...] = _mm_nt(sel, d_o * o, True)
        whg = whg_ref[...]
        hg = hg_ref[...]
        sg = jax.nn.sigmoid(hg)
        _, ohn, rh = _grms_fwd(oh_ref[...], whg, HGRN_DIM)
        dmh = dmix[:, MLA_WIDTH:]
        dhg_ref[...] = dmh * ohn * whg * sg * (1.0 + hg * (1.0 - sg))
        d_oh, dwhg = _grms_bwd(dmh * (hg * sg), ohn, rh, whg, HGRN_DIM)
        dwhg_ref[...] += dwhg
        doh_ref[...] = d_oh

        if nsw:
            @pl.when(pl.program_id(0) == T // tt - 1)
            def _():
                for cp in swap_copies():
                    cp.wait()

    row = lambda w: pl.BlockSpec((tt, w), lambda i: (i, 0))
    full = lambda a: pl.BlockSpec(a.shape, lambda i: (0,) * a.ndim)
    vec = lambda w: pl.BlockSpec((1, w), lambda i: (0, 0))
    sds = jax.ShapeDtypeStruct
    return pl.pallas_call(
        body, name="mid_bwd", grid=(T // tt,),
        in_specs=[pl.BlockSpec((N_CHIPS, tt, D_MODEL), lambda i: (0, i, 0)), row(D_MODEL), row(D_MODEL), row(D_MODEL),
                  row(D_MODEL), row(MLA_WIDTH), row(HGRN_WIDTH), pl.BlockSpec((tt, HGRN_WIDTH), lambda i: (i, 3)),
                  full(wout), vec(D_MODEL), vec(D_MODEL), vec(MLA_WIDTH), vec(HGRN_WIDTH)] + [ANY] * nsw,
        out_specs=[row(D_MODEL), full(wout), row(MLA_WIDTH), row(HGRN_WIDTH), row(HGRN_WIDTH),
                   pl.BlockSpec((MLA_HEADS, tt), lambda i: (0, i)),
                   vec(D_MODEL), vec(D_MODEL), vec(MLA_WIDTH), vec(HGRN_WIDTH)] + [ANY] * nsw,
        out_shape=[sds((T, D_MODEL), F32), sds(wout.shape, F32), sds((T, MLA_WIDTH), MXU_DTYPE), sds((T, HGRN_WIDTH), F32),
                   sds((T, HGRN_WIDTH), F32), sds((MLA_HEADS, T), F32),
                   sds((1, D_MODEL), F32), sds((1, D_MODEL), F32), sds((1, MLA_WIDTH), F32), sds((1, HGRN_WIDTH), F32)]
        + _half_stack_shapes(swap),
        scratch_shapes=[pltpu.SemaphoreType.DMA((nsw,)), pltpu.SemaphoreType.DMA((nsw,))] if nsw else [],
        compiler_params=_params(("arbitrary",)),
    )(dzp, dh2, h1, y1, mixb, o_raw, oh_raw, xph, wout, w_fpre, w_post, w_mla, w_hg, *swap)


def _in_bwd(x, dh1, cq, ckv, dq, dk, dv, dhq, dhf, dhi, dhg, rc, rs, w_pre, win, qnw, wq, kvnw, wk, wv, tt=256):
    T = x.shape[0]

    def body(x_ref, dh1_ref, cq_ref, ckv_ref, dq_ref, dk_ref, dv_ref, dhq_ref, dhf_ref, dhi_ref, dhg_ref, rc_ref, rs_ref,
             wpre_ref, win_ref, qnw_ref, wq_ref, kvnw_ref, wk_ref, wv_ref,
             dx_ref, dwin_ref, dwq_ref, dwk_ref, dwv_ref, dwpre_ref, dqnw_ref, dkvnw_ref):
        @pl.when(pl.program_id(0) == 0)
        def _():
            for r in (dwin_ref, dwq_ref, dwk_ref, dwv_ref, dwpre_ref, dqnw_ref, dkvnw_ref):
                r[...] = jnp.zeros_like(r)

        def add_win_grad(r, first):
            for arr0, n, chip, row0 in _win_grad_segments():
                if first <= arr0 and arr0 + n <= first + r.shape[0]:
                    dwin_ref[chip, row0:row0 + n, :] += r[arr0 - first:arr0 - first + n]

        lo = Q_RANK + KV_RANK + HEAD_PAD
        dxp_h = jnp.concatenate([dhq_ref[...], dhf_ref[...], dhi_ref[...], dhg_ref[...]], axis=-1)
        du = _mm(dxp_h, win_ref[lo:, :])
        wpre = wpre_ref[...]
        u, xn, rx = _rms_fwd(x_ref[...], wpre)
        add_win_grad(_mm_tn(dxp_h, u), lo)
        c, sa, sb = _rope_tables(rc_ref[...], rs_ref[...])
        lane = lax.broadcasted_iota(jnp.int32, (tt, HEAD_PAD), 1)
        dk_all = dk_ref[...]
        dq_lin = []
        dkr = jnp.zeros((tt, HEAD_PAD), F32)
        for h in range(MLA_HEADS):
            sl = slice(HEAD_PAD * h, HEAD_PAD * (h + 1))
            dq_lin.append(_rope_bwd(dq_ref[sl, :].T * ATTN_SCALE, c, sa, sb))
            dkr = dkr + dk_all[:, sl]
        dq_lin = jnp.concatenate(dq_lin, axis=-1)
        dkr = jnp.where((lane >= MLA_NOPE) & (lane < MLA_QK), _rope_bwd(dkr, c, sa, sb), 0.0)
        qnw = qnw_ref[...]
        qn, cqn, rq = _rms_fwd(cq_ref[...], qnw)
        dwq_ref[...] += _mm_tn(qn, dq_lin)
        dcq, dqnw = _rms_bwd(_mm_nt(dq_lin, wq_ref[...]), cqn, rq, qnw)
        dqnw_ref[...] += dqnw
        kvnw = kvnw_ref[...]
        kvn, ckvn, rkv = _rms_fwd(ckv_ref[...], kvnw)
        dv_ = dv_ref[...]
        dwk_ref[...] += _mm_tn(kvn, dk_all)
        dwv_ref[...] += _mm_tn(kvn, dv_)
        dckv, dkvnw = _rms_bwd(_mm_nt(dk_all, wk_ref[...]) + _mm_nt(dv_, wv_ref[...]), ckvn, rkv, kvnw)
        dkvnw_ref[...] += dkvnw
        dxp_a = jnp.concatenate([dcq, dckv, dkr], axis=-1)
        add_win_grad(_mm_tn(dxp_a, u), 0)
        dx_u, dwpre = _rms_bwd(du + _mm(dxp_a, win_ref[:lo, :]), xn, rx, wpre)
        dwpre_ref[...] += dwpre
        dx_ref[...] = dh1_ref[...] + dx_u

    row = lambda w: pl.BlockSpec((tt, w), lambda i: (i, 0))
    full = lambda a: pl.BlockSpec(a.shape, lambda i: (0,) * a.ndim)
    sds = jax.ShapeDtypeStruct
    qk_w = MLA_HEADS * HEAD_PAD
    return pl.pallas_call(
        body, name="in_bwd", grid=(T // tt,),
        in_specs=[row(D_MODEL), row(D_MODEL), row(Q_RANK), row(KV_RANK), pl.BlockSpec((qk_w, tt), lambda i: (0, i)),
                  row(qk_w), row(MLA_WIDTH),
                  row(HGRN_WIDTH), row(HGRN_WIDTH), row(HGRN_WIDTH), row(HGRN_WIDTH), row(HEAD_PAD), row(HEAD_PAD),
                  full(w_pre), full(win), full(qnw), full(wq), full(kvnw), full(wk), full(wv)],
        out_specs=[row(D_MODEL), pl.BlockSpec(WIN_COMM_SHAPE, lambda i: (0, 0, 0)), full(wq), full(wk), full(wv),
                   full(w_pre), full(qnw), full(kvnw)],
        out_shape=[sds((T, D_MODEL), F32), sds(WIN_COMM_SHAPE, F32), sds(wq.shape, F32), sds(wk.shape, F32),
                   sds(wv.shape, F32), sds(w_pre.shape, F32), sds(qnw.shape, F32), sds(kvnw.shape, F32)],
        compiler_params=_params(("arbitrary",)),
    )(x, dh1, cq, ckv, dq, dk, dv, dhq, dhf, dhi, dhg, rc, rs, w_pre, win, qnw, wq, kvnw, wk, wv)


def _arrange_weights(win_t, wuq_full, wukv):
    dt = win_t.dtype
    z = lambda n: jnp.zeros((n, D_MODEL), dt)
    s2 = Q_RANK + KV_RANK
    win_arr = jnp.concatenate([win_t[:s2], z(MLA_NOPE), win_t[s2:s2 + MLA_ROPE], z(HEAD_PAD - MLA_QK),
                               win_t[s2 + MLA_ROPE:]], axis=0)
    wq_arr = jnp.pad(wuq_full, ((0, 0), (0, 0), (0, HEAD_PAD - MLA_QK))).reshape(Q_RANK, MLA_HEADS * HEAD_PAD)
    wk_arr = jnp.pad(wukv[:, :, :MLA_NOPE], ((0, 0), (0, 0), (0, HEAD_PAD - MLA_NOPE))).reshape(
        KV_RANK, MLA_HEADS * HEAD_PAD)
    wv_arr = wukv[:, :, MLA_NOPE:].reshape(KV_RANK, MLA_WIDTH)
    return win_arr, wq_arr, wk_arr, wv_arr


WIN_COMM_SHAPE = (N_CHIPS, -(-D_IN // N_CHIPS // 32) * 32, D_MODEL)


def _win_grad_segments():
    s2 = Q_RANK + KV_RANK
    runs = [(0, s2, 0), (s2, s2 + MLA_ROPE, MLA_NOPE), (s2 + MLA_ROPE, D_IN, HEAD_PAD - MLA_ROPE)]
    per = D_IN // N_CHIPS
    segs = []
    for lo, hi, shift in runs:
        for k in range(N_CHIPS):
            a, b = max(lo, per * k), min(hi, per * (k + 1))
            if a < b:
                segs.append((a + shift, b - a, k, a - per * k))
    return segs


def _unarrange_grads(dwq_arr, dwk_arr, dwv_arr):
    dwuq = dwq_arr.reshape(Q_RANK, MLA_HEADS, HEAD_PAD)[:, :, :MLA_QK]
    dwukv = jnp.concatenate([dwk_arr.reshape(KV_RANK, MLA_HEADS, HEAD_PAD)[:, :, :MLA_NOPE],
                             dwv_arr.reshape(KV_RANK, MLA_HEADS, MLA_V)], axis=-1)
    return dwuq, dwukv


def _rope_inv_freq():
    inv = 1.0 / (ROPE_THETA ** (jnp.arange(0, MLA_ROPE, 2, dtype=F32) / MLA_ROPE))
    z = lambda n: jnp.zeros((n,), F32)
    return jnp.concatenate([z(MLA_NOPE), inv, inv, z(HEAD_PAD - MLA_QK)]).reshape(1, HEAD_PAD)


def _local_step(x, pos, tgt, small, win_arr, wq_arr, wk_arr, wv_arr, late, place=None):
    invf = _rope_inv_freq()
    cq, ckv, xph, qb, kb, vb, kt, vt, rc, rs = _in_fwd(x, pos, invf, small["attn_pre_norm"], win_arr, small["mla_q_norm"],
                                               wq_arr, small["mla_kv_norm"], wk_arr, wv_arr)
    if place is None:
        o_raw, lse = _attn_fwd_t(qb, kb, vt)
        wout, wg, wu, wd = late
    else:
        o_raw, lse, *stacks = _attn_fwd_t(qb, kb, vt, gather=late)
        wout, wg, wu, wd = [lax.dynamic_update_slice(s, l[None], (place[1], 0, 0)) for s, l in zip(stacks, late)]
        wout = wout.reshape(D_MODEL, D_MODEL)
    oh_raw, states = _hgrn_fwd(xph, small["hgrn_lb_logits"])
    h1, y1, zb, mixb = _proj_fwd(x, o_raw, oh_raw, xph, wout, small["mla_out_norm"], small["hgrn_out_norm"],
                                 small["attn_post_norm"], small["ffn_pre_norm"])
    g, up, dy2b, dh2, loss_acc, d_fpost = _ffn_fwd(zb, h1, tgt, small["ffn_post_norm"], wg, wu, wd)
    dwg, dwu, dwd, dzp = _ffn_bwd(zb, g, up, dy2b, wg, wu, wd)
    ffn_grads = [] if place is None else [dwg, dwu, dwd]
    dh1, dwout, d_o, d_oh, dhg, dvec, d_fpre, d_post, d_mla, d_hg, *ffn_rs = _mid_bwd(
        dzp, dh2, h1, y1, mixb, o_raw, oh_raw, xph, wout, small["ffn_pre_norm"], small["attn_post_norm"],
        small["mla_out_norm"], small["hgrn_out_norm"], swap=ffn_grads)
    if ffn_grads:
        ffn_grads = ffn_grads + [dwout.reshape(N_CHIPS, D_MODEL // N_CHIPS, D_MODEL)]
        ffn_rs += _pair_swap(ffn_grads[3:], (), "pair_swap_w_out")
    ffn_ps = _pair_sum(place, ffn_grads, ffn_rs, name="pair_sum_ffn") if ffn_grads else []
    dq, dk, dv, *ffn_ris = _attn_bwd_t(qb, kb, kt, vb, d_o, lse, dvec.reshape(lse.shape), send=ffn_ps)
    ffn_sums = _chip_sum(place, ffn_grads, ffn_rs, ffn_ris, name="chip_sum_ffn") if ffn_grads else []
    dhq, dhf, dhi, d_lbl, *ffn_final = _hgrn_bwd(xph, small["hgrn_lb_logits"], states, d_oh, fill=ffn_sums)
    dx, dwin4, dwq_arr, dwk_arr, dwv_arr, d_pre, d_qn, d_kvn = _in_bwd(
        x, dh1, cq, ckv, dq, dk, dv, dhq, dhf, dhi, dhg, rc, rs, small["attn_pre_norm"], win_arr,
        small["mla_q_norm"], wq_arr, small["mla_kv_norm"], wk_arr, wv_arr)
    dwuq, dwukv = _unarrange_grads(dwq_arr, dwk_arr, dwv_arr)
    loss = 0.5 * jnp.sum(loss_acc) * (1.0 / D_MODEL)
    grads = dict(attn_pre_norm=d_pre, w_in=dwin4, mla_q_norm=d_qn, mla_w_uq=dwuq, mla_kv_norm=d_kvn, mla_w_ukv=dwukv,
                 mla_out_norm=d_mla, hgrn_lb_logits=d_lbl, hgrn_out_norm=d_hg, w_out=dwout, attn_post_norm=d_post,
                 ffn_pre_norm=d_fpre, w_gate=dwg, w_up=dwu, w_down=dwd, ffn_post_norm=d_fpost)
    if place is None:
        return loss, dx, grads
    return loss, dx, grads, ffn_final


def _place():
    x, y, c = lax.axis_index("x"), lax.axis_index("y"), lax.axis_index("c")
    others = [(1 - x, y), (x, 1 - y), (1 - x, 1 - y)]
    return x, y, c, 2 * x + y, (x, y, 1 - c), others


def _half(ref, c, rows):
    return ref.at[pl.ds(pl.multiple_of(c * rows, 8), rows)]


def _rcopy(src, dst, send, recv, k, to):
    return pltpu.make_async_remote_copy(src_ref=src, dst_ref=dst, send_sem=send.at[k], recv_sem=recv.at[k],
                                        device_id=to, device_id_type=MESH)


class _Gather:
    def __init__(self, ins, outs, send, recv):
        self.ins, self.outs, self.send, self.recv = ins, outs, send, recv
        self.n = len(ins)
        self.halves = [r.shape[0] // 2 for r in ins]
        _, _, self.c, self.me, self.sib, self.others = _place()

    def _each(self):
        for j, (px, py) in enumerate(self.others):
            for a in range(self.n):
                yield j * self.n + a, a, 2 * px + py, (px, py, self.c)

    def sends(self):
        return [_rcopy(_half(self.ins[a], self.c, self.halves[a]), _half(self.outs[a].at[self.me], self.c, self.halves[a]),
                       self.send, self.recv, k, to) for k, a, _, to in self._each()]

    def arrivals(self):
        parts = [(k, _half(self.outs[a].at[chip], self.c, self.halves[a]), to) for k, a, chip, to in self._each()]
        return [_rcopy(p, p, self.send, self.recv, k, to) for k, p, to in parts]

    def forwards(self):
        parts = [(k, _half(self.outs[a].at[chip], self.c, self.halves[a])) for k, a, chip, _ in self._each()]
        return [_rcopy(p, p, self.send, self.recv, 3 * self.n + k, self.sib) for k, p in parts]

    def forward_arrivals(self):
        parts = [(k, _half(self.outs[a].at[chip], 1 - self.c, self.halves[a])) for k, a, chip, _ in self._each()]
        return [_rcopy(p, p, self.send, self.recv, 3 * self.n + k, self.sib) for k, p in parts]

    @staticmethod
    def out_shapes(arrs):
        return [jax.ShapeDtypeStruct((N_CHIPS,) + a.shape, a.dtype) for a in arrs]

    @staticmethod
    def semaphores(arrs):
        return [pltpu.SemaphoreType.DMA((6 * len(arrs),)), pltpu.SemaphoreType.DMA((6 * len(arrs),))]


def _gather_chips(arrs, name):
    n = len(arrs)

    def body(*refs):
        gat = _Gather(refs[:n], refs[n:2 * n], *refs[2 * n:])
        sends, forwards = gat.sends(), gat.forwards()
        for cp in sends:
            cp.start()
        for arrival, fw in zip(gat.arrivals(), forwards):
            arrival.wait_recv()
            fw.start()
        for arrival in gat.forward_arrivals():
            arrival.wait_recv()
        for cp in sends + forwards:
            cp.wait_send()

    return pl.pallas_call(body, name=name, in_specs=[ANY] * n, out_specs=[ANY] * n, out_shape=_Gather.out_shapes(arrs),
                          scratch_shapes=_Gather.semaphores(arrs))(*arrs)


def _grad_blocks(gs):
    return max(n for n in (1, 2, 3, 4) if all(g.shape[1] // 2 % (16 * n) == 0 for g in gs))


def _pair_swap_copies(g_refs, r_refs, send, recv):
    _, _, c, _, sib, _ = _place()
    copies = []
    for a, (g, r) in enumerate(zip(g_refs, r_refs)):
        h = g.shape[1] // 2
        copies.append(_rcopy(g.at[:, pl.ds(pl.multiple_of((1 - c) * h, 8), h)], r, send, recv, a, sib))
    return copies


def _half_stack_shapes(gs, dtype=None):
    return [jax.ShapeDtypeStruct((N_CHIPS, g.shape[1] // 2, g.shape[2]), dtype or g.dtype) for g in gs]


def _pair_swap(gs, wholes, name):
    n, nw = len(gs), len(wholes)

    def body(*refs):
        ins, outs, (send, recv) = refs[:n + nw], refs[n + nw:2 * (n + nw)], refs[2 * (n + nw):]
        copies = _pair_swap_copies(ins[:n], outs[:n], send, recv)
        copies += [_rcopy(ins[n + k], outs[n + k], send, recv, n + k, _place()[4]) for k in range(nw)]
        for cp in copies:
            cp.start()
        for cp in copies:
            cp.wait()

    return pl.pallas_call(
        body, name=name, in_specs=[ANY] * (n + nw), out_specs=[ANY] * (n + nw),
        out_shape=_half_stack_shapes(gs) + [jax.ShapeDtypeStruct(w.shape, w.dtype) for w in wholes],
        scratch_shapes=[pltpu.SemaphoreType.DMA((n + nw,)), pltpu.SemaphoreType.DMA((n + nw,))],
    )(*gs, *wholes)


def _pair_sum(place, gs, rs, small=None, name="pair_sum"):
    n = len(gs)
    nb = _grad_blocks(gs)

    def body(place_ref, *refs):
        g_refs, r_refs, p_refs = refs[:n], refs[n:2 * n], refs[-n - 1:-1] if small else refs[-n:]
        for a in range(n):
            p_refs[a][0] = (g_refs[a][0] + r_refs[a][0]).astype(p_refs[a].dtype)
        if small:
            @pl.when((pl.program_id(0) == 0) & (pl.program_id(1) == 0))
            def _():
                refs[-1][...] = refs[2 * n][...] + refs[2 * n + 1][...]

    in_specs, out_specs = [], []
    for g in gs:
        blk = (1, g.shape[1] // 2 // nb, g.shape[2])
        in_specs.append(pl.BlockSpec(blk, lambda i, k, p: (k, p[0] * nb + i, 0)))
    for g in gs:
        blk = (1, g.shape[1] // 2 // nb, g.shape[2])
        in_specs.append(pl.BlockSpec(blk, lambda i, k, p: (k, i, 0)))
        out_specs.append(pl.BlockSpec(blk, lambda i, k, p: (k, i, 0)))
    out_shape = _half_stack_shapes(gs, BF16)
    if small:
        sm_spec = pl.BlockSpec(small[0].shape, lambda i, k, p: (0, 0))
        in_specs += [sm_spec, sm_spec]
        out_specs.append(sm_spec)
        out_shape.append(jax.ShapeDtypeStruct(small[0].shape, F32))
    return pl.pallas_call(
        body, name=name,
        grid_spec=pltpu.PrefetchScalarGridSpec(num_scalar_prefetch=1, grid=(nb, N_CHIPS), in_specs=in_specs,
                                               out_specs=out_specs),
        out_shape=out_shape,
        compiler_params=_params(("arbitrary", "arbitrary")),
    )(place, *gs, *rs, *(small or ()))


def _chip_swap_copies(p_refs, ri_refs, send, recv):
    _, _, c, _, _, others = _place()
    n = len(p_refs)
    return [_rcopy(p_refs[a].at[2 * px + py], ri_refs[a].at[j], send, recv, j * n + a, (px, py, c))
            for j, (px, py) in enumerate(others) for a in range(n)]


def _chip_swap_shapes(ps):
    return [jax.ShapeDtypeStruct((3,) + p.shape[1:], p.dtype) for p in ps]


def _chip_swap(ps, pair):
    n = len(ps)

    def body(*refs):
        start, finish = _chip_swap_plan(refs[:n], refs[n], refs[n + 1:2 * n + 1], refs[2 * n + 1], *refs[2 * n + 2:])
        start()
        finish()

    return pl.pallas_call(
        body, name="chip_swap", in_specs=[ANY] * (n + 1), out_specs=[ANY] * (n + 1),
        out_shape=_chip_swap_out_shapes(ps, pair), scratch_shapes=_chip_swap_semaphores(n),
    )(*ps, pair)


def _chip_swap_plan(p_refs, pair_ref, ri_refs, sm4_ref, send, recv, lsem):
    n = len(p_refs)
    hs = SMALL_ROWS // 2
    x, y, c, me, sib, others = _place()
    local = pltpu.make_async_copy(pair_ref, sm4_ref.at[me], lsem.at[0])
    copies = _chip_swap_copies(p_refs, ri_refs, send, recv)
    arrivals = list(copies)
    for j, (px, py) in enumerate(others):
        copies.append(_rcopy(_half(pair_ref, c, hs), _half(sm4_ref.at[me], c, hs), send, recv, 3 * n + j, (px, py, c)))
        part = _half(sm4_ref.at[2 * px + py], c, hs)
        arrivals.append(_rcopy(part, part, send, recv, 3 * n + j, (px, py, c)))

    def start():
        local.start()
        for cp in copies:
            cp.start()

    def finish():
        for arrival in arrivals:
            arrival.wait_recv()
        for cp in copies:
            cp.wait_send()
        local.wait()

    return start, finish


def _chip_swap_out_shapes(ps, pair):
    return _chip_swap_shapes(ps) + [jax.ShapeDtypeStruct((N_CHIPS,) + pair.shape, pair.dtype)]


def _chip_swap_semaphores(n):
    k = 3 * (n + 1)
    return [pltpu.SemaphoreType.DMA((k,)), pltpu.SemaphoreType.DMA((k,)), pltpu.SemaphoreType.DMA((1,))]


def _chip_sum(place, gs, rs, ris, name="chip_sum"):
    n = len(gs)
    nb = _grad_blocks(gs)

    def body(place_ref, *refs):
        g_refs, r_refs, ri_refs, o_refs = refs[:n], refs[n:2 * n], refs[2 * n:3 * n], refs[3 * n:]
        for a in range(n):
            ri = ri_refs[a]
            o_refs[a][...] = (g_refs[a][0] + r_refs[a][0]) + ri[0].astype(F32) + ri[1].astype(F32) + ri[2].astype(F32)

    in_specs, out_specs, out_shape = [], [], []
    for g in gs:
        blk = (1, g.shape[1] // 2 // nb, g.shape[2])
        in_specs.append(pl.BlockSpec(blk, lambda i, p: (p[1], p[0] * nb + i, 0)))
    for g in gs:
        blk = (1, g.shape[1] // 2 // nb, g.shape[2])
        in_specs.append(pl.BlockSpec(blk, lambda i, p: (p[1], i, 0)))
    for g in gs:
        rb = g.shape[1] // 2 // nb
        in_specs.append(pl.BlockSpec((3, rb, g.shape[2]), lambda i, p: (0, i, 0)))
        out_specs.append(pl.BlockSpec((rb, g.shape[2]), lambda i, p: (p[0] * nb + i, 0)))
        out_shape.append(jax.ShapeDtypeStruct(g.shape[1:], F32))
    return pl.pallas_call(
        body, name=name,
        grid_spec=pltpu.PrefetchScalarGridSpec(num_scalar_prefetch=1, grid=(nb,), in_specs=in_specs, out_specs=out_specs),
        out_shape=out_shape,
        compiler_params=_params(("arbitrary",)),
    )(place, *gs, *rs, *ris)


def _pair_fill_copies(g_refs, send, recv):
    _, _, c, _, sib, _ = _place()
    copies, waits = [], []
    for a, g in enumerate(g_refs):
        h = g.shape[0] // 2
        mine, theirs = _half(g, c, h), _half(g, 1 - c, h)
        copies.append(_rcopy(mine, mine, send, recv, a, sib))
        waits.append(_rcopy(theirs, theirs, send, recv, a, sib))
    return copies, waits


def _pair_fill(gfs, sm4):
    n = len(gfs)
    hs = SMALL_ROWS // 2

    def body(*refs):
        g_refs, sm4_ref = refs[n + 1:2 * n + 1], refs[2 * n + 1]
        send, recv = refs[2 * n + 2:]
        x, y, c, me, sib, others = _place()
        copies, waits = _pair_fill_copies(g_refs, send, recv)
        for j, (px, py) in enumerate(others):
            chip = 2 * px + py
            mine, theirs = _half(sm4_ref.at[chip], c, hs), _half(sm4_ref.at[chip], 1 - c, hs)
            copies.append(pltpu.make_async_remote_copy(src_ref=mine, dst_ref=mine, send_sem=send.at[n + j],
                                                       recv_sem=recv.at[n + j], device_id=sib, device_id_type=MESH))
            waits.append(pltpu.make_async_remote_copy(src_ref=theirs, dst_ref=theirs, send_sem=send.at[n + j],
                                                      recv_sem=recv.at[n + j], device_id=sib, device_id_type=MESH))
        for cp in copies:
            cp.start()
        for w in waits:
            w.wait_recv()
        for cp in copies:
            cp.wait_send()

    return pl.pallas_call(
        body, name="pair_fill", in_specs=[ANY] * (n + 1), out_specs=[ANY] * (n + 1),
        out_shape=[jax.ShapeDtypeStruct(g.shape, g.dtype) for g in gfs] + [jax.ShapeDtypeStruct(sm4.shape, sm4.dtype)],
        input_output_aliases={i: i for i in range(n + 1)},
        scratch_shapes=[pltpu.SemaphoreType.DMA((n + 3,)), pltpu.SemaphoreType.DMA((n + 3,))],
    )(*gfs, sm4)


def _adamw_math(w, g, m, v):
    m = ADAM_B1 * m + (1.0 - ADAM_B1) * g
    v = ADAM_B2 * v + (1.0 - ADAM_B2) * (g * g)
    m_hat = m / (1.0 - ADAM_B1 ** ADAM_STEP)
    v_hat = v / (1.0 - ADAM_B2 ** ADAM_STEP)
    return -ADAM_LR * (m_hat / (jnp.sqrt(v_hat) + ADAM_EPS) + ADAM_WD * w), m, v


def _adamw(items, steps, name):
    n = len(items)

    def body(*refs):
        for a in range(n):
            g = refs[4 * a + 1][...]
            d, mo, vo = _adamw_math(refs[4 * a][...], g, refs[4 * a + 2][...], refs[4 * a + 3][...])
            for out, val in zip(refs[4 * n + 4 * a:4 * n + 4 * a + 4], (g, d, mo, vo)):
                out[...] = val

    spec = lambda w: pl.BlockSpec((w.shape[0] // steps, w.shape[1]), lambda i: (i, 0))
    flat = pl.pallas_call(
        body, name=name, grid=(steps,), in_specs=[spec(it[0]) for it in items for _ in range(4)],
        out_specs=[spec(it[0]) for it in items for _ in range(4)],
        out_shape=[jax.ShapeDtypeStruct(it[0].shape, F32) for it in items for _ in range(4)],
        compiler_params=_params(("arbitrary",)),
    )(*[a for it in items for a in it])
    return [flat[4 * a:4 * a + 4] for a in range(n)]


def _adamw_small(sm4, wmv):
    views = SMALL_VIEWS[:-1]
    n = len(views)

    def body(sm4_ref, *refs):
        g_all = ((sm4_ref[0] + sm4_ref[1]) + sm4_ref[2]) + sm4_ref[3]
        for a, (name, rows, cols) in enumerate(views):
            row = SMALL_OFFSETS[name]
            g = g_all[row:row + rows, :cols]
            d, mo, vo = _adamw_math(refs[3 * a][...], g, refs[3 * a + 1][...], refs[3 * a + 2][...])
            for out, val in zip(refs[3 * n + 4 * a:3 * n + 4 * a + 4], (g, d, mo, vo)):
                out[...] = val
        row = SMALL_OFFSETS["loss"]
        refs[-1][...] = g_all[row:row + 1, :128]

    flat = pl.pallas_call(
        body, name="adamw_small",
        out_shape=[jax.ShapeDtypeStruct((rows, cols), F32) for _, rows, cols in views for _ in range(4)]
        + [jax.ShapeDtypeStruct((1, 128), F32)],
        compiler_params=pltpu.CompilerParams(vmem_limit_bytes=VMEM_LIMIT),
    )(sm4, *[a for t in wmv for a in t])
    return [flat[4 * a:4 * a + 4] for a in range(n)] + [flat[-1]]


SMALL_NAMES = ("attn_pre_norm", "mla_q_norm", "mla_kv_norm", "mla_w_ukv", "mla_out_norm", "hgrn_lb_logits",
               "hgrn_out_norm", "attn_post_norm", "ffn_pre_norm", "ffn_post_norm")
BIG_NAMES = ("w_in", "mla_w_uq", "w_out", "w_gate", "w_up", "w_down")
WEIGHT_NAMES = ("attn_pre_norm", "w_in", "mla_q_norm", "mla_w_uq", "mla_kv_norm", "mla_w_ukv", "mla_out_norm",
                "hgrn_lb_logits", "hgrn_out_norm", "w_out", "attn_post_norm", "ffn_pre_norm", "w_gate", "w_up", "w_down",
                "ffn_post_norm")


UQ_COMM_SHAPE = (192, 384)


def _pack_small(vals):
    parts, row = [], 0
    for name, rows, cols in sorted(SMALL_VIEWS, key=lambda view: SMALL_OFFSETS[view[0]]):
        assert SMALL_OFFSETS[name] == row
        parts.append(jnp.pad(vals[name].reshape(rows, cols), ((0, 0), (0, D_MODEL - cols))))
        row += rows
    parts.append(jnp.zeros((SMALL_ROWS - row, D_MODEL), F32))
    return jnp.concatenate(parts, axis=0)


def kernel(x, positions, attn_pre_norm, w_in, mla_q_norm, mla_w_uq, mla_kv_norm, mla_w_ukv, mla_out_norm, hgrn_lb_logits, hgrn_out_norm, w_out, attn_post_norm, ffn_pre_norm, w_gate, w_up, w_down, ffn_post_norm, loss_target, m_attn_pre_norm, m_w_in, m_mla_q_norm, m_mla_w_uq, m_mla_kv_norm, m_mla_w_ukv, m_mla_out_norm, m_hgrn_lb_logits, m_hgrn_out_norm, m_w_out, m_attn_post_norm, m_ffn_pre_norm, m_w_gate, m_w_up, m_w_down, m_ffn_post_norm, v_attn_pre_norm, v_w_in, v_mla_q_norm, v_mla_w_uq, v_mla_kv_norm, v_mla_w_ukv, v_mla_out_norm, v_hgrn_lb_logits, v_hgrn_out_norm, v_w_out, v_attn_post_norm, v_ffn_pre_norm, v_w_gate, v_w_up, v_w_down, v_ffn_post_norm):
    args = locals()
    W = {n: args[n] for n in WEIGHT_NAMES}
    M = {n: args["m_" + n] for n in WEIGHT_NAMES}
    V = {n: args["v_" + n] for n in WEIGHT_NAMES}
    T = x.shape[1]
    cx, cy, cc = lax.axis_index("x"), lax.axis_index("y"), lax.axis_index("c")

    win_rows = D_IN // N_CHIPS
    shard2d = {"w_in": (win_rows, D_MODEL), "mla_w_uq": (Q_RANK // N_CHIPS, MLA_HEADS * MLA_QK),
               "w_out": (D_MODEL // N_CHIPS, D_MODEL), "w_gate": (FF_SHARD, D_MODEL), "w_up": (FF_SHARD, D_MODEL),
               "w_down": (FF_SHARD, D_MODEL)}
    transposed = ("w_in", "w_gate", "w_up")
    to2d = lambda n, a: a[0].T if n in transposed else a.reshape(shard2d[n])
    from2d = lambda n, t: t.T[None] if n in transposed else t.reshape(W[n].shape)
    me = 2 * cx + cy
    place = jnp.stack([cc, me]).astype(jnp.int32)
    local_b = [to2d(n, W[n]).astype(BF16) for n in BIG_NAMES]
    local_b[0] = jnp.pad(local_b[0], ((0, WIN_COMM_SHAPE[1] - win_rows), (0, 0)))
    stacks = _gather_chips(local_b[:2], "gather_weights")
    win4, wuq4 = [lax.dynamic_update_slice(s, l[None], (me, 0, 0)) for s, l in zip(stacks, local_b)]
    win_t = win4[:, :win_rows].reshape(D_IN, D_MODEL)
    wuq_full = wuq4.reshape(Q_RANK, MLA_HEADS, MLA_QK)
    win_arr, wq_arr, wk_arr, wv_arr = _arrange_weights(win_t, wuq_full, mla_w_ukv[0].astype(BF16))
    small = {n: W[n][0] if n == "mla_w_ukv" else W[n].reshape(-1, W[n].shape[-1]) for n in SMALL_NAMES}

    loss_local, dx, grads, ffn_final = _local_step(x[0], positions.reshape(T, 1), loss_target[0], small, win_arr,
                                                           wq_arr, wk_arr, wv_arr, local_b[2:], place)

    gs = [grads["w_in"], grads["mla_w_uq"].reshape((N_CHIPS,) + UQ_COMM_SHAPE)]
    sm = _pack_small({**grads, "loss": loss_local})
    *rs, ssib = _pair_swap(gs, (sm,), "pair_swap")
    *ps, pair = _pair_sum(place, gs, rs, small=(sm, ssib))
    ffn_names, rest_names = BIG_NAMES[3:], BIG_NAMES[:2]
    g2d = dict(zip(ffn_names + BIG_NAMES[2:3], ffn_final))
    adam_in = lambda names_: [(to2d(n, W[n]), g2d[n], to2d(n, M[n]), to2d(n, V[n])) for n in names_]
    updates = dict(zip(ffn_names, _adamw(adam_in(ffn_names), 8, "adamw_ffn")))
    updates.update(zip(BIG_NAMES[2:3], _adamw(adam_in(BIG_NAMES[2:3]), 8, "adamw_w_out")))
    *ris, sm4 = _chip_swap(ps, pair)
    *gfin, smf = _pair_fill(_chip_sum(place, gs, rs, ris), sm4)

    g2d.update({n: gfin[k].reshape((-1,) + shard2d[n][1:]) for k, n in enumerate(rest_names)})
    updates.update(zip(rest_names, _adamw(adam_in(rest_names), 3, "adamw_w_in")))
    G, DW, NM, NV = {}, {}, {}, {}
    for n, outs in updates.items():
        G[n], DW[n], NM[n], NV[n] = (from2d(n, t) for t in outs)
    view2d = lambda n, a: a.reshape(next((r, c) for name, r, c in SMALL_VIEWS if name == n))
    *res, loss_row = _adamw_small(smf, [tuple(view2d(n, t[n]) for t in (W, M, V)) for n in SMALL_NAMES])
    for n, outs in zip(SMALL_NAMES, res):
        G[n], DW[n], NM[n], NV[n] = (t.reshape(W[n].shape) for t in outs)
    loss = loss_row[0, 0]
    return (loss, dx[None], *[G[n] for n in WEIGHT_NAMES], *[DW[n] for n in WEIGHT_NAMES],
            *[NM[n] for n in WEIGHT_NAMES], *[NV[n] for n in WEIGHT_NAMES])
```

```python
import jax
import jax.numpy as jnp
from jax import lax
from jax.experimental import pallas as pl
from jax.experimental.pallas import tpu as pltpu

F32 = jnp.float32
BF16 = jnp.bfloat16
MXU_DTYPE = BF16

D_MODEL = 1024
MLA_HEADS = 8
MLA_NOPE = 64
MLA_ROPE = 32
MLA_V = 64
MLA_QK = MLA_NOPE + MLA_ROPE
Q_RANK = 384
KV_RANK = 128
MLA_WIDTH = MLA_HEADS * MLA_V
HEAD_PAD = 128
HGRN_HEADS = 4
HGRN_DIM = 128
HGRN_WIDTH = HGRN_HEADS * HGRN_DIM
CHUNK = 64
SUB = 16
HGRN_CPI = 4
D_IN = Q_RANK + KV_RANK + MLA_ROPE + 4 * HGRN_WIDTH
D_IN_ARR = Q_RANK + KV_RANK + HEAD_PAD + 4 * HGRN_WIDTH
D_FF = 2816
N_CHIPS = 4
FF_SHARD = D_FF // N_CHIPS
EPS = 1e-6
ROPE_THETA = 10000.0
ATTN_SCALE = MLA_QK ** -0.5
ATTN_SCALE_LOG2 = ATTN_SCALE * 1.4426950408889634
NEG_BIG = -1e30

ADAM_LR = 0.001
ADAM_B1 = 0.9
ADAM_B2 = 0.999
ADAM_EPS = 1e-08
ADAM_WD = 0.01
ADAM_STEP = 10

VMEM_LIMIT = 56 * 1024 * 1024
FFN_BWD_VMEM = 61 * 1024 * 1024

SMALL_VIEWS = (("attn_pre_norm", 1, 1024), ("mla_q_norm", 1, 384), ("mla_kv_norm", 1, 128), ("mla_w_ukv", 128, 1024),
               ("mla_out_norm", 1, 512), ("hgrn_lb_logits", 2, 512), ("hgrn_out_norm", 1, 512),
               ("attn_post_norm", 1, 1024), ("ffn_pre_norm", 1, 1024), ("ffn_post_norm", 1, 1024), ("loss", 1, 1))
ROW_TILE = 8


def _small_layout():
    offsets, row = {}, 0
    for whole in (True, False):
        for name, rows, _ in SMALL_VIEWS:
            if (rows % ROW_TILE == 0) == whole:
                offsets[name] = row
                row += rows
    return offsets, -(-row // (2 * ROW_TILE)) * 2 * ROW_TILE


SMALL_OFFSETS, SMALL_ROWS = _small_layout()

MESH = pl.DeviceIdType.MESH
ANY = pl.BlockSpec(memory_space=pl.ANY)


def _dot(a, b, dims, exact):
    if exact:
        return lax.dot_general(a.astype(F32), b.astype(F32), (dims, ((), ())), precision=lax.Precision.HIGH,
                               preferred_element_type=F32)
    return lax.dot_general(a.astype(MXU_DTYPE), b.astype(MXU_DTYPE), (dims, ((), ())), preferred_element_type=F32)


def _mm(a, b, exact=False):
    return _dot(a, b, ((1,), (0,)), exact)


def _mm_nt(a, b, exact=False):
    return _dot(a, b, ((1,), (1,)), exact)


def _mm_tn(a, b, exact=False):
    return _dot(a, b, ((0,), (0,)), exact)


def _rms_fwd(x, w):
    r = lax.rsqrt(jnp.mean(x * x, axis=-1, keepdims=True) + EPS)
    xn = x * r
    return xn * w, xn, r


def _rms_bwd(dy, xn, r, w):
    dxn = dy * w
    dx = r * (dxn - xn * jnp.mean(dxn * xn, axis=-1, keepdims=True))
    dw = jnp.sum(dy * xn, axis=0, keepdims=True)
    return dx, dw


def _group_sums(v, gs):
    t, n = v.shape
    lane = lax.broadcasted_iota(jnp.int32, (t, 128), 1)
    out = []
    for p in range(n // 128):
        vb = v[:, 128 * p:128 * (p + 1)]
        if gs == 128:
            out.append(jnp.sum(vb, axis=-1, keepdims=True))
        else:
            out.append(jnp.sum(jnp.where(lane < 64, vb, 0.0), axis=-1, keepdims=True))
            out.append(jnp.sum(jnp.where(lane >= 64, vb, 0.0), axis=-1, keepdims=True))
    return out


def _group_bcast(sums, gs, t):
    lane = lax.broadcasted_iota(jnp.int32, (t, 128), 1)
    if gs == 128:
        return jnp.concatenate([jnp.broadcast_to(s, (t, 128)) for s in sums], axis=-1)
    return jnp.concatenate([jnp.where(lane < 64, sums[2 * p], sums[2 * p + 1]) for p in range(len(sums) // 2)],
                           axis=-1)


def _grms_fwd(x, w, gs):
    t = x.shape[0]
    r = lax.rsqrt(_group_bcast(_group_sums(x * x, gs), gs, t) * (1.0 / gs) + EPS)
    xn = x * r
    return xn * w, xn, r


def _grms_bwd(dy, xn, r, w, gs):
    t = dy.shape[0]
    dxn = dy * w
    dx = r * (dxn - xn * (_group_bcast(_group_sums(dxn * xn, gs), gs, t) * (1.0 / gs)))
    dw = jnp.sum(dy * xn, axis=0, keepdims=True)
    return dx, dw


def _rope_tables(c_tab, s_tab):
    lane = lax.broadcasted_iota(jnp.int32, c_tab.shape, 1)
    first = (lane >= MLA_NOPE) & (lane < MLA_NOPE + MLA_ROPE // 2)
    second = (lane >= MLA_NOPE + MLA_ROPE // 2) & (lane < MLA_QK)
    return c_tab, jnp.where(first, -s_tab, 0.0), jnp.where(second, s_tab, 0.0)


def _rope(v, c, sa, sb):
    return v * c + pltpu.roll(v, HEAD_PAD - MLA_ROPE // 2, 1) * sa + pltpu.roll(v, MLA_ROPE // 2, 1) * sb


def _rope_bwd(d, c, sa, sb):
    return d * c - pltpu.roll(d, HEAD_PAD - MLA_ROPE // 2, 1) * sa - pltpu.roll(d, MLA_ROPE // 2, 1) * sb


def _params(sem, vmem=VMEM_LIMIT):
    return pltpu.CompilerParams(dimension_semantics=sem, vmem_limit_bytes=vmem)


def _in_fwd(x, pos, invf, w_pre, win, qnw, wq, kvnw, wk, wv, tt=512):
    T = x.shape[0]

    def body(x_ref, pos_ref, invf_ref, wpre_ref, win_ref, qnw_ref, wq_ref, kvnw_ref, wk_ref, wv_ref,
             cq_ref, ckv_ref, xph_ref, q_ref, k_ref, v_ref, kt_ref, vt_ref, rc_ref, rs_ref):
        u, _, _ = _rms_fwd(x_ref[...], wpre_ref[...])
        lo = Q_RANK + KV_RANK + HEAD_PAD
        xp = _mm_nt(u, win_ref[:lo, :])
        xph_ref[...] = _mm_nt(u, win_ref[lo:, :])
        cq = xp[:, :Q_RANK]
        ckv = xp[:, Q_RANK:Q_RANK + KV_RANK]
        kr = xp[:, Q_RANK + KV_RANK:]
        cq_ref[...] = cq
        ckv_ref[...] = ckv
        ang = pos_ref[...].astype(F32) * invf_ref[...]
        c_tab = jnp.cos(ang)
        s_tab = jnp.sin(ang)
        rc_ref[...] = c_tab
        rs_ref[...] = s_tab
        c, sa, sb = _rope_tables(c_tab, s_tab)
        qn, _, _ = _rms_fwd(cq, qnw_ref[...])
        q = _mm(qn, wq_ref[...])
        kvn, _, _ = _rms_fwd(ckv, kvnw_ref[...])
        kn = _mm(kvn, wk_ref[...])
        v = _mm(kvn, wv_ref[...])
        v_ref[...] = v.astype(v_ref.dtype)
        vt_ref[...] = v.T.astype(vt_ref.dtype)
        krr = _rope(kr, c, sa, sb)
        for h in range(MLA_HEADS):
            sl = slice(HEAD_PAD * h, HEAD_PAD * (h + 1))
            q_ref[:, sl] = (_rope(q[:, sl], c, sa, sb) * ATTN_SCALE_LOG2).astype(q_ref.dtype)
            kh = kn[:, sl] + krr
            k_ref[:, sl] = kh.astype(k_ref.dtype)
            kt_ref[sl, :] = kh.T.astype(kt_ref.dtype)

    row = lambda w: pl.BlockSpec((tt, w), lambda i: (i, 0))
    full = lambda a: pl.BlockSpec(a.shape, lambda i: (0,) * a.ndim)
    qk_w = MLA_HEADS * HEAD_PAD
    return pl.pallas_call(
        body, name="in_fwd", grid=(T // tt,),
        in_specs=[row(D_MODEL), row(1), full(invf), full(w_pre), full(win), full(qnw), full(wq), full(kvnw),
                  full(wk), full(wv)],
        out_specs=[row(Q_RANK), row(KV_RANK), row(4 * HGRN_WIDTH), row(qk_w), row(qk_w), row(MLA_WIDTH),
                   pl.BlockSpec((qk_w, tt), lambda i: (0, i)), pl.BlockSpec((MLA_WIDTH, tt), lambda i: (0, i)),
                   row(HEAD_PAD), row(HEAD_PAD)],
        out_shape=[jax.ShapeDtypeStruct((T, Q_RANK), F32), jax.ShapeDtypeStruct((T, KV_RANK), F32),
                   jax.ShapeDtypeStruct((T, 4 * HGRN_WIDTH), F32), jax.ShapeDtypeStruct((T, qk_w), MXU_DTYPE),
                   jax.ShapeDtypeStruct((T, qk_w), MXU_DTYPE), jax.ShapeDtypeStruct((T, MLA_WIDTH), MXU_DTYPE),
                   jax.ShapeDtypeStruct((qk_w, T), MXU_DTYPE), jax.ShapeDtypeStruct((MLA_WIDTH, T), MXU_DTYPE),
                   jax.ShapeDtypeStruct((T, HEAD_PAD), F32), jax.ShapeDtypeStruct((T, HEAD_PAD), F32)],
        compiler_params=_params(("arbitrary",)),
    )(x, pos, invf, w_pre, win, qnw, wq, kvnw, wk, wv)


def _attn_fwd_t(qb, kb, vt, gather=(), tq=256, hps=8):
    T = qb.shape[0]
    nq = T // tq
    ng = len(gather)
    steps = (MLA_HEADS // hps) * nq
    pass_on = steps - 3

    def body(q_ref, k_ref, vt_ref, *rest):
        o_ref, lse_ref = rest[ng:ng + 2]
        acc_scr = rest[2 * ng + 2]
        qi = pl.program_id(1)
        step_no = pl.program_id(0) * nq + qi
        if ng:
            gat = _Gather(rest[:ng], rest[ng + 2:2 * ng + 2], *rest[2 * ng + 3:])

            @pl.when(step_no == 0)
            def _():
                for cp in gat.sends():
                    cp.start()

            @pl.when(step_no == pass_on)
            def _():
                for arrival in gat.arrivals():
                    arrival.wait_recv()
                for cp in gat.forwards():
                    cp.start()

        heads = [slice(HEAD_PAD * a, HEAD_PAD * (a + 1)) for a in range(hps)]
        acc_scr[...] = jnp.zeros_like(acc_scr)

        def step(j, carry, masked):
            start = pl.multiple_of(j * tq, tq)
            scores = [_mm_nt(k_ref[pl.ds(start, tq), heads[a]], q_ref[:, heads[a]]) for a in range(hps)]
            new, probs, alphas = [], [], []
            for a in range(hps):
                m, l = carry[a]
                s = scores[a]
                if masked:
                    kk = lax.broadcasted_iota(jnp.int32, (tq, tq), 0)
                    qq = lax.broadcasted_iota(jnp.int32, (tq, tq), 1)
                    s = jnp.where(kk <= qq, s, NEG_BIG)
                m_new = jnp.maximum(m, jnp.max(s, axis=0, keepdims=True))
                alpha = jnp.exp2(m - m_new)
                p = jnp.exp2(s - m_new)
                l = l * alpha + jnp.sum(p, axis=0, keepdims=True)
                new.append((m_new, l))
                probs.append(p.astype(MXU_DTYPE))
                alphas.append(alpha)
                if a % 2:
                    pr = a // 2
                    vtj = vt_ref[2 * MLA_V * pr:2 * MLA_V * (pr + 1), pl.ds(start, tq)]
                    none = jnp.zeros((MLA_V, tq), vtj.dtype)
                    pv = (_mm(jnp.concatenate([vtj[:MLA_V], none], axis=0), probs[a - 1])
                          + _mm(jnp.concatenate([none, vtj[MLA_V:]], axis=0), probs[a]))
                    acc_scr[pr] = acc_scr[pr] * jnp.where(row < MLA_V, alphas[a - 1], alphas[a]) + pv
            return tuple(new)

        row = lax.broadcasted_iota(jnp.int32, (2 * MLA_V, tq), 0)
        init = tuple((jnp.full((1, tq), NEG_BIG, F32), jnp.zeros((1, tq), F32)) for _ in range(hps))
        carry = lax.fori_loop(0, qi, lambda j, c: step(j, c, False), init)
        carry = step(qi, carry, True)
        for pr in range(hps // 2):
            (m0, l0), (m1, l1) = carry[2 * pr], carry[2 * pr + 1]
            ot = acc_scr[pr] / jnp.where(row < MLA_V, l0, l1)
            o_ref[:, 2 * MLA_V * pr:2 * MLA_V * (pr + 1)] = ot.T
            lse_ref[pr, 0:1, :] = m0 + jnp.log2(l0)
            lse_ref[pr, 1:2, :] = m1 + jnp.log2(l1)

        if ng:
            @pl.when(step_no == steps - 1)
            def _():
                for arrival in gat.forward_arrivals():
                    arrival.wait_recv()
                for cp in gat.sends() + gat.forwards():
                    cp.wait_send()

    return pl.pallas_call(
        body, name="attn_fwd", grid=(MLA_HEADS // hps, nq),
        in_specs=[pl.BlockSpec((tq, hps * HEAD_PAD), lambda g, i: (i, g)),
                  pl.BlockSpec((T, hps * HEAD_PAD), lambda g, i: (0, g)),
                  pl.BlockSpec((hps * MLA_V, T), lambda g, i: (g, 0))] + [ANY] * ng,
        out_specs=[pl.BlockSpec((tq, hps * MLA_V), lambda g, i: (i, g)),
                   pl.BlockSpec((hps // 2, 2, tq), lambda g, i: (g, 0, i))] + [ANY] * ng,
        out_shape=[jax.ShapeDtypeStruct((T, MLA_WIDTH), F32), jax.ShapeDtypeStruct((MLA_HEADS // 2, 2, T), F32)]
        + _Gather.out_shapes(gather),
        scratch_shapes=[pltpu.VMEM((hps // 2, 2 * MLA_V, tq), F32)] + (_Gather.semaphores(gather) if ng else []),
        compiler_params=_params(("arbitrary", "arbitrary")),
    )(qb, kb, vt, *gather)


def _attn_bwd_t(qb, kb, kt, vb, dob, lse, dvec, send=(), tq=512, hps=4):
    T = qb.shape[0]
    nq = T // tq
    ns = len(send)
    steps = (MLA_HEADS // hps) * nq

    def body(q_ref, k_ref, kt_ref, v_ref, do_ref, lse_ref, d_ref, *rest):
        dqt_ref, dk_ref, dv_ref = rest[ns:ns + 3]
        va_scr, dv_scr = rest[2 * ns + 3:2 * ns + 5]
        j = pl.program_id(1)
        step_no = pl.program_id(0) * nq + j
        if ns:
            @pl.when(step_no == 0)
            def _():
                for cp in _chip_swap_copies(rest[:ns], rest[ns + 3:2 * ns + 3], *rest[2 * ns + 5:]):
                    cp.start()

        @pl.when(j == 0)
        def _():
            dqt_ref[...] = jnp.zeros_like(dqt_ref)

        lane = lax.broadcasted_iota(jnp.int32, (tq, 2 * MLA_V), 1)
        heads = [slice(HEAD_PAD * a, HEAD_PAD * (a + 1)) for a in range(hps)]
        pairs = [slice(2 * MLA_V * p, 2 * MLA_V * (p + 1)) for p in range(hps // 2)]
        for pr in range(hps // 2):
            vpair = v_ref[:, pairs[pr]]
            va_scr[2 * pr] = jnp.where(lane < MLA_V, vpair, jnp.zeros_like(vpair))
            va_scr[2 * pr + 1] = jnp.where(lane >= MLA_V, vpair, jnp.zeros_like(vpair))
        dk_ref[...] = jnp.zeros_like(dk_ref)
        dv_scr[...] = jnp.zeros_like(dv_scr)

        def step(i, masked):
            start = pl.multiple_of(i * tq, tq)
            rows = pl.ds(start, tq)
            scores = [_mm_nt(k_ref[:, heads[a]], q_ref[rows, heads[a]]) for a in range(hps)]
            dps = [_mm_nt(va_scr[a], do_ref[rows, pairs[a // 2]]) for a in range(hps)]
            for a in range(hps):
                pr, r = a // 2, a % 2
                p = jnp.exp2(scores[a] - lse_ref[pr, r:r + 1, rows])
                if masked:
                    kk = lax.broadcasted_iota(jnp.int32, (tq, tq), 0)
                    qq = lax.broadcasted_iota(jnp.int32, (tq, tq), 1)
                    p = jnp.where(kk <= qq, p, 0.0)
                ds = p * (dps[a] - d_ref[pr, r:r + 1, rows])
                dv_scr[a] += _mm(p, do_ref[rows, pairs[pr]])
                dk_ref[:, heads[a]] += _mm(ds, q_ref[rows, heads[a]])
                dqt_ref[heads[a], rows] += _mm(kt_ref[heads[a], :], ds)

        def loop_body(i, _):
            step(i, False)
            return 0

        step(j, True)
        lax.fori_loop(j + 1, nq, loop_body, 0)
        for pr in range(hps // 2):
            dv_ref[:, pairs[pr]] = jnp.where(lane < MLA_V, dv_scr[2 * pr], dv_scr[2 * pr + 1])
        dk_ref[...] = dk_ref[...] * (ATTN_SCALE / ATTN_SCALE_LOG2)

        if ns:
            @pl.when(step_no == steps - 1)
            def _():
                for cp in _chip_swap_copies(rest[:ns], rest[ns + 3:2 * ns + 3], *rest[2 * ns + 5:]):
                    cp.wait()

    stat = pl.BlockSpec((hps // 2, 2, T), lambda g, j: (g, 0, 0))
    return pl.pallas_call(
        body, name="attn_bwd", grid=(MLA_HEADS // hps, nq),
        in_specs=[pl.BlockSpec((T, hps * HEAD_PAD), lambda g, j: (0, g)),
                  pl.BlockSpec((tq, hps * HEAD_PAD), lambda g, j: (j, g)),
                  pl.BlockSpec((hps * HEAD_PAD, tq), lambda g, j: (g, j)),
                  pl.BlockSpec((tq, hps * MLA_V), lambda g, j: (j, g)),
                  pl.BlockSpec((T, hps * MLA_V), lambda g, j: (0, g)), stat, stat] + [ANY] * ns,
        out_specs=[pl.BlockSpec((hps * HEAD_PAD, T), lambda g, j: (g, 0)),
                   pl.BlockSpec((tq, hps * HEAD_PAD), lambda g, j: (j, g)),
                   pl.BlockSpec((tq, hps * MLA_V), lambda g, j: (j, g))] + [ANY] * ns,
        out_shape=[jax.ShapeDtypeStruct((MLA_HEADS * HEAD_PAD, T), F32),
                   jax.ShapeDtypeStruct((T, MLA_HEADS * HEAD_PAD), F32),
                   jax.ShapeDtypeStruct((T, MLA_WIDTH), F32)] + _chip_swap_shapes(send),
        scratch_shapes=[pltpu.VMEM((hps, tq, 2 * MLA_V), vb.dtype), pltpu.VMEM((hps, tq, 2 * MLA_V), F32)]
        + ([pltpu.SemaphoreType.DMA((3 * ns,)), pltpu.SemaphoreType.DMA((3 * ns,))] if ns else []),
        compiler_params=_params(("arbitrary", "arbitrary")),
    )(qb, kb, kt, vb, dob, lse, dvec, *send)


def _cumsum_rows(x):
    n = x.shape[0]
    row = lax.broadcasted_iota(jnp.int32, x.shape, 0)
    s = 1
    while s < n:
        x = x + jnp.where(row >= s, pltpu.roll(x, s, 0), 0.0)
        s *= 2
    return x


def _rev_cumsum_rows(x):
    n = x.shape[0]
    row = lax.broadcasted_iota(jnp.int32, x.shape, 0)
    s = 1
    while s < n:
        x = x + jnp.where(row < n - s, pltpu.roll(x, n - s, 0), 0.0)
        s *= 2
    return x


def _lb_from_logits(l):
    l0, l1 = l[0:1, :], l[1:2, :]
    m = jnp.maximum(l0, l1)
    e0, e1 = jnp.exp(l0 - m), jnp.exp(l1 - m)
    return e0 / (e0 + e1)


def _hgrn_gates(hq, hf, lb):
    sig_f = jax.nn.sigmoid(hf)
    f = lb + (1.0 - lb) * sig_f
    sig_q = jax.nn.sigmoid(hq)
    return sig_f, f, jnp.log(f), 1.0 - f, sig_q, hq * sig_q


def _hgrn_intra(q, kk, b, exact=False):
    row = lax.broadcasted_iota(jnp.int32, b.shape, 0)
    qs, ks, eqs, eks, a_rows = [], [], [], [], []
    for i in range(CHUNK // SUB):
        ref = b[SUB * i + SUB // 2:SUB * i + SUB // 2 + 1, :]
        eq = jnp.exp(b[SUB * i:SUB * (i + 1), :] - ref)
        ek = jnp.exp(jnp.where(row < SUB * (i + 1), ref - b, NEG_BIG))
        qi = q[SUB * i:SUB * (i + 1), :] * eq
        ki = kk * ek
        a_rows.append(_mm_nt(qi, ki, exact))
        qs.append(qi), ks.append(ki), eqs.append(eq), eks.append(ek)
    tt = lax.broadcasted_iota(jnp.int32, (CHUNK, CHUNK), 0)
    ss = lax.broadcasted_iota(jnp.int32, (CHUNK, CHUNK), 1)
    causal = ss <= tt
    a = jnp.where(causal, jnp.concatenate(a_rows, axis=0), 0.0)
    return a, causal, qs, ks, eqs, eks


def _hgrn_fwd(xph, lbl, tg=512):
    T = xph.shape[0]
    ng, ncg = T // tg, tg // CHUNK
    cols = [slice(HGRN_DIM * h, HGRN_DIM * (h + 1)) for h in range(HGRN_HEADS)]

    def body(lbl_ref, hq_ref, hf_ref, hi_ref, o_ref, st_ref, s_scr):
        @pl.when(pl.program_id(0) == 0)
        def _():
            s_scr[...] = jnp.zeros_like(s_scr)

        lb = _lb_from_logits(lbl_ref[...])

        def chunks(it, _):
            pre = []
            for k in range(HGRN_CPI):
                c = it * HGRN_CPI + k
                rows = pl.ds(pl.multiple_of(c * CHUNK, CHUNK), CHUNK)
                for cs in cols:
                    _, _, lf, kk, _, q = _hgrn_gates(hq_ref[rows, cs], hf_ref[rows, cs], lb[:, cs])
                    v = hi_ref[rows, cs]
                    b = _cumsum_rows(lf)
                    a = _hgrn_intra(q, kk, b)[0]
                    b_last = b[CHUNK - 1:CHUNK, :]
                    pre.append((c, rows, q * jnp.exp(b), a, v, jnp.exp(b_last), _mm_tn(v, kk * jnp.exp(b_last - b))))
            for i, (c, rows, qe, a, v, ebl, upd) in enumerate(pre):
                h = i % HGRN_HEADS
                st = s_scr[h]
                st_ref[h, c] = st
                o_ref[rows, cols[h]] = _mm_nt(qe, st) + _mm(a, v)
                s_scr[h] = st * ebl + upd
            return 0

        lax.fori_loop(0, ncg // HGRN_CPI, chunks, 0)

    col = lambda k: pl.BlockSpec((tg, HGRN_WIDTH), lambda g: (g, k))
    return pl.pallas_call(
        body, name="hgrn_fwd", grid=(ng,),
        in_specs=[pl.BlockSpec((2, HGRN_WIDTH), lambda g: (0, 0)), col(0), col(1), col(2)],
        out_specs=[col(0), pl.BlockSpec((HGRN_HEADS, ncg, HGRN_DIM, HGRN_DIM), lambda g: (0, g, 0, 0))],
        out_shape=[jax.ShapeDtypeStruct((T, HGRN_WIDTH), F32),
                   jax.ShapeDtypeStruct((HGRN_HEADS, T // CHUNK, HGRN_DIM, HGRN_DIM), F32)],
        scratch_shapes=[pltpu.VMEM((HGRN_HEADS, HGRN_DIM, HGRN_DIM), F32)],
        compiler_params=_params(("arbitrary",)),
    )(lbl, xph, xph, xph)


def _hgrn_bwd(xph, lbl, states, d_o, fill=(), tg=512):
    T = xph.shape[0]
    ng, ncg = T // tg, tg // CHUNK
    cols = [slice(HGRN_DIM * h, HGRN_DIM * (h + 1)) for h in range(HGRN_HEADS)]
    nsub = CHUNK // SUB
    nf = len(fill)

    def body(lbl_ref, hq_ref, hf_ref, hi_ref, st_ref, do_ref, *rest):
        dhq_ref, dhf_ref, dhi_ref, dlg_ref = rest[nf:nf + 4]
        ds_scr, dlb_scr = rest[2 * nf + 4:2 * nf + 6]
        fill_copies = lambda: _pair_fill_copies(rest[nf + 4:2 * nf + 4], *rest[2 * nf + 6:])
        g = pl.program_id(0)

        @pl.when(g == 0)
        def _():
            ds_scr[...] = jnp.zeros_like(ds_scr)
            dlb_scr[...] = jnp.zeros_like(dlb_scr)
            for cp in (fill_copies()[0] if nf else ()):
                cp.start()

        lb = _lb_from_logits(lbl_ref[...])

        def chunks(it, _):
            pre = []
            for k, h in ((k, h) for k in range(HGRN_CPI) for h in range(HGRN_HEADS)):
                cs = cols[h]
                c = ncg - 1 - (it * HGRN_CPI + k)
                rows = pl.ds(pl.multiple_of(c * CHUNK, CHUNK), CHUNK)
                hq = hq_ref[rows, cs]
                sig_f, f, lf, kk, sig_q, q = _hgrn_gates(hq, hf_ref[rows, cs], lb[:, cs])
                v = hi_ref[rows, cs]
                do = do_ref[rows, cs]
                b = _cumsum_rows(lf)
                eb = jnp.exp(b)
                a, causal, qs, ks, eqs, eks = _hgrn_intra(q, kk, b)
                b_last = b[CHUNK - 1:CHUNK, :]
                st = st_ref[h, c]
                pre.append(dict(h=h, cs=cs, rows=rows, hq=hq, sig_f=sig_f, f=f, kk=kk, sig_q=sig_q, q=q, v=v, eb=eb, qs=qs,
                                ks=ks, eqs=eqs,
                                eks=eks, ebl=jnp.exp(b_last), el=jnp.exp(b_last - b), st=st,
                                da=jnp.where(causal, _mm_nt(do, v, True), 0.0), dq=_mm(do, st, True) * eb,
                                dv=_mm_tn(a, do), dsu=_mm_tn(do, q * eb, True)))
            for w in pre:
                dq_rows = []
                dk = jnp.zeros_like(w["q"])
                for i in range(nsub):
                    dai = w["da"][SUB * i:SUB * (i + 1), :]
                    dq_rows.append(_mm(dai, w["ks"][i], True) * w["eqs"][i])
                    dk = dk + _mm_tn(dai, w["qs"][i], True) * w["eks"][i]
                w["dq"] = w["dq"] + jnp.concatenate(dq_rows, axis=0)
                w["dk"] = dk
            for w in pre:
                h, cs, rows = w["h"], w["cs"], w["rows"]
                kk, el, ebl, dst = w["kk"], w["el"], w["ebl"], ds_scr[h]
                dk_state = _mm(w["v"], dst, True) * el
                dk = w["dk"] + dk_state
                e_last = (ebl * jnp.sum(w["st"] * dst, axis=0, keepdims=True)
                          + jnp.sum(kk * dk_state, axis=0, keepdims=True))
                dlf = _rev_cumsum_rows(w["q"] * w["dq"] - kk * dk) + e_last
                ds_scr[h] = dst * ebl + w["dsu"]
                df = dlf / w["f"] - dk
                sig_f, sig_q = w["sig_f"], w["sig_q"]
                dhf_ref[rows, cs] = df * (1.0 - lb[:, cs]) * sig_f * (1.0 - sig_f)
                dlb_scr[:, cs] += jnp.sum(df * (1.0 - sig_f), axis=0, keepdims=True)
                dhq_ref[rows, cs] = w["dq"] * sig_q * (1.0 + w["hq"] * (1.0 - sig_q))
                dhi_ref[rows, cs] = w["dv"] + _mm_nt(kk * el, dst)
            return 0

        lax.fori_loop(0, ncg // HGRN_CPI, chunks, 0)

        @pl.when(g == ng - 1)
        def _():
            dl0 = dlb_scr[...] * lb * (1.0 - lb)
            dlg_ref[...] = jnp.concatenate([dl0, -dl0], axis=0)
            if nf:
                copies, waits = fill_copies()
                for w in waits:
                    w.wait_recv()
                for cp in copies:
                    cp.wait_send()

    col = lambda k: pl.BlockSpec((tg, HGRN_WIDTH), lambda g: (ng - 1 - g, k))
    logits = pl.BlockSpec((2, HGRN_WIDTH), lambda g: (0, 0))
    big = jax.ShapeDtypeStruct((T, HGRN_WIDTH), F32)
    n_in, n_out = 6, 4
    return pl.pallas_call(
        body, name="hgrn_bwd", grid=(ng,),
        in_specs=[logits, col(0), col(1), col(2),
                  pl.BlockSpec((HGRN_HEADS, ncg, HGRN_DIM, HGRN_DIM), lambda g: (0, ng - 1 - g, 0, 0)), col(0)] + [ANY] * nf,
        out_specs=[col(0), col(0), col(0), logits] + [ANY] * nf,
        out_shape=[big, big, big, jax.ShapeDtypeStruct((2, HGRN_WIDTH), F32)]
        + [jax.ShapeDtypeStruct(f.shape, f.dtype) for f in fill],
        input_output_aliases={n_in + k: n_out + k for k in range(nf)},
        scratch_shapes=[pltpu.VMEM((HGRN_HEADS, HGRN_DIM, HGRN_DIM), F32), pltpu.VMEM((1, HGRN_WIDTH), F32)]
        + ([pltpu.SemaphoreType.DMA((nf,)), pltpu.SemaphoreType.DMA((nf,))] if nf else []),
        compiler_params=_params(("arbitrary",)),
    )(lbl, xph, xph, xph, states, d_o, *fill)


def _proj_fwd(x, o_raw, oh_raw, xph, wout, w_mla, w_hg, w_post, w_fpre, tt=512):
    T = x.shape[0]

    def body(x_ref, o_ref, oh_ref, hg_ref, wout_ref, wmla_ref, whg_ref, wpost_ref, wfpre_ref,
             h1_ref, y1_ref, z_ref, mix_ref):
        om, _, _ = _grms_fwd(o_ref[...], wmla_ref[...], MLA_V)
        hg = hg_ref[...]
        ohn, _, _ = _grms_fwd(oh_ref[...], whg_ref[...], HGRN_DIM)
        mix = jnp.concatenate([om, ohn * (hg * jax.nn.sigmoid(hg))], axis=-1)
        mix_ref[...] = mix.astype(mix_ref.dtype)
        y1 = _mm(mix, wout_ref[...])
        y1_ref[...] = y1
        h1 = x_ref[...] + _rms_fwd(y1, wpost_ref[...])[0]
        h1_ref[...] = h1
        z_ref[...] = _rms_fwd(h1, wfpre_ref[...])[0].astype(z_ref.dtype)

    row = lambda w: pl.BlockSpec((tt, w), lambda i: (i, 0))
    full = lambda a: pl.BlockSpec(a.shape, lambda i: (0,) * a.ndim)
    sds = jax.ShapeDtypeStruct
    return pl.pallas_call(
        body, name="proj_fwd", grid=(T // tt,),
        in_specs=[row(D_MODEL), row(MLA_WIDTH), row(HGRN_WIDTH), pl.BlockSpec((tt, HGRN_WIDTH), lambda i: (i, 3)),
                  full(wout), full(w_mla), full(w_hg), full(w_post), full(w_fpre)],
        out_specs=[row(D_MODEL)] * 4,
        out_shape=[sds((T, D_MODEL), F32), sds((T, D_MODEL), F32), sds((T, D_MODEL), MXU_DTYPE),
                   sds((T, D_MODEL), MXU_DTYPE)],
        compiler_params=_params(("arbitrary",)),
    )(x, o_raw, oh_raw, xph, wout, w_mla, w_hg, w_post, w_fpre)


def _ffn_fwd(zb, h1, tgt, w_fpost, wg, wu, wd, tt=256):
    T = zb.shape[0]
    nj = N_CHIPS

    def body(z_ref, h1_ref, tgt_ref, wfpost_ref, wg_ref, wu_ref, wd_ref, g_ref, up_ref, dy2_ref, dh2_ref, loss_ref, dwf_ref):
        @pl.when(pl.program_id(0) == 0)
        def _():
            loss_ref[...] = jnp.zeros_like(loss_ref)
            dwf_ref[...] = jnp.zeros_like(dwf_ref)

        z = z_ref[...]
        gs = [_mm_nt(z, wg_ref[j]) for j in range(nj)]
        ups = [_mm_nt(z, wu_ref[j]) for j in range(nj)]
        y2 = jnp.zeros((tt, D_MODEL), F32)
        for j in range(nj):
            g_ref[j] = gs[j]
            up_ref[j] = ups[j]
            y2 = y2 + _mm(gs[j] * jax.nn.sigmoid(gs[j]) * ups[j], wd_ref[j])
        w = wfpost_ref[...]
        y2s, y2n, r2 = _rms_fwd(y2, w)
        e = h1_ref[...] + y2s - tgt_ref[...]
        loss_ref[...] += jnp.sum(e * e, axis=0, keepdims=True)
        dh2 = e * (1.0 / D_MODEL)
        dh2_ref[...] = dh2
        dy2, dwf = _rms_bwd(dh2, y2n, r2, w)
        dy2_ref[...] = dy2.astype(dy2_ref.dtype)
        dwf_ref[...] += dwf

    row = pl.BlockSpec((tt, D_MODEL), lambda i: (i, 0))
    vec = pl.BlockSpec((1, D_MODEL), lambda i: (0, 0))
    resident = pl.BlockSpec((nj, FF_SHARD, D_MODEL), lambda i: (0, 0, 0), pipeline_mode=pl.Buffered(1))
    act = pl.BlockSpec((nj, tt, FF_SHARD), lambda i: (0, i, 0))
    sds = jax.ShapeDtypeStruct
    return pl.pallas_call(
        body, name="ffn_fwd", grid=(T // tt,),
        in_specs=[row, row, row, vec, resident, resident, resident],
        out_specs=[act, act, row, row, vec, vec],
        out_shape=[sds((nj, T, FF_SHARD), F32), sds((nj, T, FF_SHARD), F32), sds((T, D_MODEL), MXU_DTYPE),
                   sds((T, D_MODEL), F32), sds((1, D_MODEL), F32), sds((1, D_MODEL), F32)],
        compiler_params=_params(("arbitrary",)),
    )(zb, h1, tgt, w_fpost, wg, wu, wd)


def _ffn_bwd(zb, g, up, dy2b, wg, wu, wd, tt=512):
    T = zb.shape[0]
    nj = N_CHIPS

    def body(z_ref, g_ref, up_ref, dy2_ref, wg_ref, wu_ref, wd_ref, dwg_ref, dwu_ref, dwd_ref, dz_ref):
        j, i = pl.program_id(0), pl.program_id(1)
        rows = pl.ds(pl.multiple_of(i * tt, tt), tt)

        @pl.when(i == 0)
        def _():
            dwg_ref[...] = jnp.zeros_like(dwg_ref)
            dwu_ref[...] = jnp.zeros_like(dwu_ref)
            dwd_ref[...] = jnp.zeros_like(dwd_ref)

        z, g_, up_, dy2 = z_ref[...], g_ref[0], up_ref[0], dy2_ref[...]
        sg = jax.nn.sigmoid(g_)
        act = g_ * sg
        dff = _mm_nt(dy2, wd_ref[0])
        dwd_ref[0] += _mm_tn(act * up_, dy2)
        dg = dff * up_ * sg * (1.0 + g_ * (1.0 - sg))
        dup = dff * act
        dwg_ref[0] += _mm_tn(dg, z)
        dwu_ref[0] += _mm_tn(dup, z)
        dz = _mm(dg, wg_ref[0]) + _mm(dup, wu_ref[0])

        @pl.when(j == 0)
        def _():
            dz_ref[rows, :] = dz

        @pl.when(j > 0)
        def _():
            dz_ref[rows, :] += dz

    row = pl.BlockSpec((tt, D_MODEL), lambda j, i: (i, 0))
    act = pl.BlockSpec((1, tt, FF_SHARD), lambda j, i: (j, i, 0))
    w_sh = pl.BlockSpec((1, FF_SHARD, D_MODEL), lambda j, i: (j, 0, 0))
    w_grad = jax.ShapeDtypeStruct((nj, FF_SHARD, D_MODEL), F32)
    return pl.pallas_call(
        body, name="ffn_bwd", grid=(nj, T // tt),
        in_specs=[row, act, act, row, w_sh, w_sh, w_sh],
        out_specs=[w_sh, w_sh, w_sh, pl.BlockSpec((T, D_MODEL), lambda j, i: (0, 0), pipeline_mode=pl.Buffered(1))],
        out_shape=[w_grad, w_grad, w_grad, jax.ShapeDtypeStruct((T, D_MODEL), F32)],
        compiler_params=_params(("arbitrary", "arbitrary"), vmem=FFN_BWD_VMEM),
    )(zb, g, up, dy2b, wg, wu, wd)


def _mid_bwd(dz, dh2, h1, y1, mixb, o_raw, oh_raw, xph, wout, w_fpre, w_post, w_mla, w_hg, swap=(), tt=256):
    T = dh2.shape[0]
    nsw = len(swap)
    n_in, n_out = 13, 10

    def body(*refs):
        (dz_ref, dh2_ref, h1_ref, y1_ref, mix_ref, o_ref, oh_ref, hg_ref, wout_ref, wfpre_ref, wpost_ref,
         wmla_ref, whg_ref) = refs[:n_in]
        (dh1_ref, dwout_ref, do_ref, doh_ref, dhg_ref, dvec_ref, dwfpre_ref, dwpost_ref, dwmla_ref,
         dwhg_ref) = refs[n_in + nsw:n_in + nsw + n_out]
        swap_copies = lambda: _pair_swap_copies(refs[n_in:n_in + nsw], refs[n_in + nsw + n_out:n_in + 2 * nsw + n_out],
                                                *refs[n_in + 2 * nsw + n_out:])

        @pl.when(pl.program_id(0) == 0)
        def _():
            for r in (dwout_ref, dwfpre_ref, dwpost_ref, dwmla_ref, dwhg_ref):
                r[...] = jnp.zeros_like(r)
            for cp in (swap_copies() if nsw else ()):
                cp.start()

        dz = dz_ref[...]
        wfpre = wfpre_ref[...]
        _, h1n, r = _rms_fwd(h1_ref[...], wfpre)
        dh1_z, dwfpre = _rms_bwd(dz, h1n, r, wfpre)
        dwfpre_ref[...] += dwfpre
        dh1 = dh2_ref[...] + dh1_z
        dh1_ref[...] = dh1
        wpost = wpost_ref[...]
        _, y1n, r1 = _rms_fwd(y1_ref[...], wpost)
        dy1, dwpost = _rms_bwd(dh1, y1n, r1, wpost)
        dwpost_ref[...] += dwpost
        dmix = _mm_nt(dy1, wout_ref[...])
        dwout_ref[...] += _mm_tn(mix_ref[...], dy1)
        wmla = wmla_ref[...]
        o = o_ref[...]
        _, on, ro = _grms_fwd(o, wmla, MLA_V)
        d_o, dwmla = _grms_bwd(dmix[:, :MLA_WIDTH], on, ro, wmla, MLA_V)
        dwmla_ref[...] += dwmla
        do_ref[...] = d_o.astype(do_ref.dtype)
        hh = lax.broadcasted_iota(jnp.int32, (MLA_HEADS, MLA_WIDTH), 0)
        ll = lax.broadcasted_iota(jnp.int32, (MLA_HEADS, MLA_WIDTH), 1)
        sel = jnp.where((ll >= hh * MLA_V) & (ll < (hh + 1) * MLA_V), 1.0, 0.0)
        dvec_ref[...] = _mm_nt(sel, d_o * o, True)
        whg = whg_ref[...]
        hg = hg_ref[...]
        sg = jax.nn.sigmoid(hg)
        _, ohn, rh = _grms_fwd(oh_ref[...], whg, HGRN_DIM)
        dmh = dmix[:, MLA_WIDTH:]
        dhg_ref[...] = dmh * ohn * whg * sg * (1.0 + hg * (1.0 - sg))
        d_oh, dwhg = _grms_bwd(dmh * (hg * sg), ohn, rh, whg, HGRN_DIM)
        dwhg_ref[...] += dwhg
        doh_ref[...] = d_oh

        if nsw:
            @pl.when(pl.program_id(0) == T // tt - 1)
            def _():
                for cp in swap_copies():
                    cp.wait()

    row = lambda w: pl.BlockSpec((tt, w), lambda i: (i, 0))
    full = lambda a: pl.BlockSpec(a.shape, lambda i: (0,) * a.ndim)
    vec = lambda w: pl.BlockSpec((1, w), lambda i: (0, 0))
    sds = jax.ShapeDtypeStruct
    return pl.pallas_call(
        body, name="mid_bwd", grid=(T // tt,),
        in_specs=[row(D_MODEL), row(D_MODEL), row(D_MODEL), row(D_MODEL),
                  row(D_MODEL), row(MLA_WIDTH), row(HGRN_WIDTH), pl.BlockSpec((tt, HGRN_WIDTH), lambda i: (i, 3)),
                  full(wout), vec(D_MODEL), vec(D_MODEL), vec(MLA_WIDTH), vec(HGRN_WIDTH)] + [ANY] * nsw,
        out_specs=[row(D_MODEL), full(wout), row(MLA_WIDTH), row(HGRN_WIDTH), row(HGRN_WIDTH),
                   pl.BlockSpec((MLA_HEADS, tt), lambda i: (0, i)),
                   vec(D_MODEL), vec(D_MODEL), vec(MLA_WIDTH), vec(HGRN_WIDTH)] + [ANY] * nsw,
        out_shape=[sds((T, D_MODEL), F32), sds(wout.shape, F32), sds((T, MLA_WIDTH), MXU_DTYPE), sds((T, HGRN_WIDTH), F32),
                   sds((T, HGRN_WIDTH), F32), sds((MLA_HEADS, T), F32),
                   sds((1, D_MODEL), F32), sds((1, D_MODEL), F32), sds((1, MLA_WIDTH), F32), sds((1, HGRN_WIDTH), F32)]
        + _half_stack_shapes(swap),
        scratch_shapes=[pltpu.SemaphoreType.DMA((nsw,)), pltpu.SemaphoreType.DMA((nsw,))] if nsw else [],
        compiler_params=_params(("arbitrary",)),
    )(dz, dh2, h1, y1, mixb, o_raw, oh_raw, xph, wout, w_fpre, w_post, w_mla, w_hg, *swap)


def _in_bwd(x, dh1, cq, ckv, dq, dk, dv, dhq, dhf, dhi, dhg, rc, rs, w_pre, win, qnw, wq, kvnw, wk, wv, tt=256):
    T = x.shape[0]

    def body(x_ref, dh1_ref, cq_ref, ckv_ref, dq_ref, dk_ref, dv_ref, dhq_ref, dhf_ref, dhi_ref, dhg_ref, rc_ref, rs_ref,
             wpre_ref, win_ref, qnw_ref, wq_ref, kvnw_ref, wk_ref, wv_ref,
             dx_ref, dwin_ref, dwq_ref, dwk_ref, dwv_ref, dwpre_ref, dqnw_ref, dkvnw_ref):
        @pl.when(pl.program_id(0) == 0)
        def _():
            for r in (dwin_ref, dwq_ref, dwk_ref, dwv_ref, dwpre_ref, dqnw_ref, dkvnw_ref):
                r[...] = jnp.zeros_like(r)

        def add_win_grad(r, first):
            for arr0, n, chip, row0 in _win_grad_segments():
                if first <= arr0 and arr0 + n <= first + r.shape[0]:
                    dwin_ref[chip, row0:row0 + n, :] += r[arr0 - first:arr0 - first + n]

        lo = Q_RANK + KV_RANK + HEAD_PAD
        dxp_h = jnp.concatenate([dhq_ref[...], dhf_ref[...], dhi_ref[...], dhg_ref[...]], axis=-1)
        du = _mm(dxp_h, win_ref[lo:, :])
        wpre = wpre_ref[...]
        u, xn, rx = _rms_fwd(x_ref[...], wpre)
        add_win_grad(_mm_tn(dxp_h, u), lo)
        c, sa, sb = _rope_tables(rc_ref[...], rs_ref[...])
        lane = lax.broadcasted_iota(jnp.int32, (tt, HEAD_PAD), 1)
        dk_all = dk_ref[...]
        dq_lin = []
        dkr = jnp.zeros((tt, HEAD_PAD), F32)
        for h in range(MLA_HEADS):
            sl = slice(HEAD_PAD * h, HEAD_PAD * (h + 1))
            dq_lin.append(_rope_bwd(dq_ref[sl, :].T * ATTN_SCALE, c, sa, sb))
            dkr = dkr + dk_all[:, sl]
        dq_lin = jnp.concatenate(dq_lin, axis=-1)
        dkr = jnp.where((lane >= MLA_NOPE) & (lane < MLA_QK), _rope_bwd(dkr, c, sa, sb), 0.0)
        qnw = qnw_ref[...]
        qn, cqn, rq = _rms_fwd(cq_ref[...], qnw)
        dwq_ref[...] += _mm_tn(qn, dq_lin)
        dcq, dqnw = _rms_bwd(_mm_nt(dq_lin, wq_ref[...]), cqn, rq, qnw)
        dqnw_ref[...] += dqnw
        kvnw = kvnw_ref[...]
        kvn, ckvn, rkv = _rms_fwd(ckv_ref[...], kvnw)
        dv_ = dv_ref[...]
        dwk_ref[...] += _mm_tn(kvn, dk_all)
        dwv_ref[...] += _mm_tn(kvn, dv_)
        dckv, dkvnw = _rms_bwd(_mm_nt(dk_all, wk_ref[...]) + _mm_nt(dv_, wv_ref[...]), ckvn, rkv, kvnw)
        dkvnw_ref[...] += dkvnw
        dxp_a = jnp.concatenate([dcq, dckv, dkr], axis=-1)
        add_win_grad(_mm_tn(dxp_a, u), 0)
        dx_u, dwpre = _rms_bwd(du + _mm(dxp_a, win_ref[:lo, :]), xn, rx, wpre)
        dwpre_ref[...] += dwpre
        dx_ref[...] = dh1_ref[...] + dx_u

    row = lambda w: pl.BlockSpec((tt, w), lambda i: (i, 0))
    full = lambda a: pl.BlockSpec(a.shape, lambda i: (0,) * a.ndim)
    sds = jax.ShapeDtypeStruct
    qk_w = MLA_HEADS * HEAD_PAD
    return pl.pallas_call(
        body, name="in_bwd", grid=(T // tt,),
        in_specs=[row(D_MODEL), row(D_MODEL), row(Q_RANK), row(KV_RANK), pl.BlockSpec((qk_w, tt), lambda i: (0, i)),
                  row(qk_w), row(MLA_WIDTH),
                  row(HGRN_WIDTH), row(HGRN_WIDTH), row(HGRN_WIDTH), row(HGRN_WIDTH), row(HEAD_PAD), row(HEAD_PAD),
                  full(w_pre), full(win), full(qnw), full(wq), full(kvnw), full(wk), full(wv)],
        out_specs=[row(D_MODEL), pl.BlockSpec(WIN_COMM_SHAPE, lambda i: (0, 0, 0)), full(wq), full(wk), full(wv),
                   full(w_pre), full(qnw), full(kvnw)],
        out_shape=[sds((T, D_MODEL), F32), sds(WIN_COMM_SHAPE, F32), sds(wq.shape, F32), sds(wk.shape, F32),
                   sds(wv.shape, F32), sds(w_pre.shape, F32), sds(qnw.shape, F32), sds(kvnw.shape, F32)],
        compiler_params=_params(("arbitrary",)),
    )(x, dh1, cq, ckv, dq, dk, dv, dhq, dhf, dhi, dhg, rc, rs, w_pre, win, qnw, wq, kvnw, wk, wv)


def _arrange_weights(win_t, wuq_full, wukv):
    dt = win_t.dtype
    z = lambda n: jnp.zeros((n, D_MODEL), dt)
    s2 = Q_RANK + KV_RANK
    win_arr = jnp.concatenate([win_t[:s2], z(MLA_NOPE), win_t[s2:s2 + MLA_ROPE], z(HEAD_PAD - MLA_QK),
                               win_t[s2 + MLA_ROPE:]], axis=0)
    wq_arr = jnp.pad(wuq_full, ((0, 0), (0, 0), (0, HEAD_PAD - MLA_QK))).reshape(Q_RANK, MLA_HEADS * HEAD_PAD)
    wk_arr = jnp.pad(wukv[:, :, :MLA_NOPE], ((0, 0), (0, 0), (0, HEAD_PAD - MLA_NOPE))).reshape(
        KV_RANK, MLA_HEADS * HEAD_PAD)
    wv_arr = wukv[:, :, MLA_NOPE:].reshape(KV_RANK, MLA_WIDTH)
    return win_arr, wq_arr, wk_arr, wv_arr


WIN_COMM_SHAPE = (N_CHIPS, -(-D_IN // N_CHIPS // 32) * 32, D_MODEL)


def _win_grad_segments():
    s2 = Q_RANK + KV_RANK
    runs = [(0, s2, 0), (s2, s2 + MLA_ROPE, MLA_NOPE), (s2 + MLA_ROPE, D_IN, HEAD_PAD - MLA_ROPE)]
    per = D_IN // N_CHIPS
    segs = []
    for lo, hi, shift in runs:
        for k in range(N_CHIPS):
            a, b = max(lo, per * k), min(hi, per * (k + 1))
            if a < b:
                segs.append((a + shift, b - a, k, a - per * k))
    return segs


def _unarrange_grads(dwq_arr, dwk_arr, dwv_arr):
    dwuq = dwq_arr.reshape(Q_RANK, MLA_HEADS, HEAD_PAD)[:, :, :MLA_QK]
    dwukv = jnp.concatenate([dwk_arr.reshape(KV_RANK, MLA_HEADS, HEAD_PAD)[:, :, :MLA_NOPE],
                             dwv_arr.reshape(KV_RANK, MLA_HEADS, MLA_V)], axis=-1)
    return dwuq, dwukv


def _rope_inv_freq():
    inv = 1.0 / (ROPE_THETA ** (jnp.arange(0, MLA_ROPE, 2, dtype=F32) / MLA_ROPE))
    z = lambda n: jnp.zeros((n,), F32)
    return jnp.concatenate([z(MLA_NOPE), inv, inv, z(HEAD_PAD - MLA_QK)]).reshape(1, HEAD_PAD)


def _local_step(x, pos, tgt, small, win_arr, wq_arr, wk_arr, wv_arr, late, place=None):
    invf = _rope_inv_freq()
    cq, ckv, xph, qb, kb, vb, kt, vt, rc, rs = _in_fwd(x, pos, invf, small["attn_pre_norm"], win_arr, small["mla_q_norm"],
                                               wq_arr, small["mla_kv_norm"], wk_arr, wv_arr)
    if place is None:
        o_raw, lse = _attn_fwd_t(qb, kb, vt)
        wout, wg, wu, wd = late
    else:
        o_raw, lse, *stacks = _attn_fwd_t(qb, kb, vt, gather=late)
        wout, wg, wu, wd = [lax.dynamic_update_slice(s, l[None], (place[1], 0, 0)) for s, l in zip(stacks, late)]
        wout = wout.reshape(D_MODEL, D_MODEL)
    oh_raw, states = _hgrn_fwd(xph, small["hgrn_lb_logits"])
    h1, y1, zb, mixb = _proj_fwd(x, o_raw, oh_raw, xph, wout, small["mla_out_norm"], small["hgrn_out_norm"],
                                 small["attn_post_norm"], small["ffn_pre_norm"])
    g, up, dy2b, dh2, loss_acc, d_fpost = _ffn_fwd(zb, h1, tgt, small["ffn_post_norm"], wg, wu, wd)
    dwg, dwu, dwd, dz = _ffn_bwd(zb, g, up, dy2b, wg, wu, wd)
    ffn_grads = [] if place is None else [dwg, dwu, dwd]
    dh1, dwout, d_o, d_oh, dhg, dvec, d_fpre, d_post, d_mla, d_hg, *ffn_rs = _mid_bwd(
        dz, dh2, h1, y1, mixb, o_raw, oh_raw, xph, wout, small["ffn_pre_norm"], small["attn_post_norm"],
        small["mla_out_norm"], small["hgrn_out_norm"], swap=ffn_grads)
    if ffn_grads:
        ffn_grads = ffn_grads + [dwout.reshape(N_CHIPS, D_MODEL // N_CHIPS, D_MODEL)]
        ffn_rs += _pair_swap(ffn_grads[3:], (), "pair_swap_w_out")
    ffn_ps = _pair_sum(place, ffn_grads, ffn_rs, name="pair_sum_ffn") if ffn_grads else []
    dq, dk, dv, *ffn_ris = _attn_bwd_t(qb, kb, kt, vb, d_o, lse, dvec.reshape(lse.shape), send=ffn_ps)
    ffn_sums = _chip_sum(place, ffn_grads, ffn_rs, ffn_ris, name="chip_sum_ffn") if ffn_grads else []
    dhq, dhf, dhi, d_lbl, *ffn_final = _hgrn_bwd(xph, small["hgrn_lb_logits"], states, d_oh, fill=ffn_sums)
    dx, dwin4, dwq_arr, dwk_arr, dwv_arr, d_pre, d_qn, d_kvn = _in_bwd(
        x, dh1, cq, ckv, dq, dk, dv, dhq, dhf, dhi, dhg, rc, rs, small["attn_pre_norm"], win_arr,
        small["mla_q_norm"], wq_arr, small["mla_kv_norm"], wk_arr, wv_arr)
    dwuq, dwukv = _unarrange_grads(dwq_arr, dwk_arr, dwv_arr)
    loss = 0.5 * jnp.sum(loss_acc) * (1.0 / D_MODEL)
    grads = dict(attn_pre_norm=d_pre, w_in=dwin4, mla_q_norm=d_qn, mla_w_uq=dwuq, mla_kv_norm=d_kvn, mla_w_ukv=dwukv,
                 mla_out_norm=d_mla, hgrn_lb_logits=d_lbl, hgrn_out_norm=d_hg, w_out=dwout, attn_post_norm=d_post,
                 ffn_pre_norm=d_fpre, w_gate=dwg, w_up=dwu, w_down=dwd, ffn_post_norm=d_fpost)
    if place is None:
        return loss, dx, grads
    return loss, dx, grads, ffn_final


def _place():
    x, y, c = lax.axis_index("x"), lax.axis_index("y"), lax.axis_index("c")
    others = [(1 - x, y), (x, 1 - y), (1 - x, 1 - y)]
    return x, y, c, 2 * x + y, (x, y, 1 - c), others


def _half(ref, c, rows):
    return ref.at[pl.ds(pl.multiple_of(c * rows, 8), rows)]


def _rcopy(src, dst, send, recv, k, to):
    return pltpu.make_async_remote_copy(src_ref=src, dst_ref=dst, send_sem=send.at[k], recv_sem=recv.at[k],
                                        device_id=to, device_id_type=MESH)


class _Gather:
    def __init__(self, ins, outs, send, recv):
        self.ins, self.outs, self.send, self.recv = ins, outs, send, recv
        self.n = len(ins)
        self.halves = [r.shape[0] // 2 for r in ins]
        _, _, self.c, self.me, self.sib, self.others = _place()

    def _each(self):
        for j, (px, py) in enumerate(self.others):
            for a in range(self.n):
                yield j * self.n + a, a, 2 * px + py, (px, py, self.c)

    def sends(self):
        return [_rcopy(_half(self.ins[a], self.c, self.halves[a]), _half(self.outs[a].at[self.me], self.c, self.halves[a]),
                       self.send, self.recv, k, to) for k, a, _, to in self._each()]

    def arrivals(self):
        parts = [(k, _half(self.outs[a].at[chip], self.c, self.halves[a]), to) for k, a, chip, to in self._each()]
        return [_rcopy(p, p, self.send, self.recv, k, to) for k, p, to in parts]

    def forwards(self):
        parts = [(k, _half(self.outs[a].at[chip], self.c, self.halves[a])) for k, a, chip, _ in self._each()]
        return [_rcopy(p, p, self.send, self.recv, 3 * self.n + k, self.sib) for k, p in parts]

    def forward_arrivals(self):
        parts = [(k, _half(self.outs[a].at[chip], 1 - self.c, self.halves[a])) for k, a, chip, _ in self._each()]
        return [_rcopy(p, p, self.send, self.recv, 3 * self.n + k, self.sib) for k, p in parts]

    @staticmethod
    def out_shapes(arrs):
        return [jax.ShapeDtypeStruct((N_CHIPS,) + a.shape, a.dtype) for a in arrs]

    @staticmethod
    def semaphores(arrs):
        return [pltpu.SemaphoreType.DMA((6 * len(arrs),)), pltpu.SemaphoreType.DMA((6 * len(arrs),))]


def _gather_chips(arrs, name):
    n = len(arrs)

    def body(*refs):
        gat = _Gather(refs[:n], refs[n:2 * n], *refs[2 * n:])
        sends, forwards = gat.sends(), gat.forwards()
        for cp in sends:
            cp.start()
        for arrival, fw in zip(gat.arrivals(), forwards):
            arrival.wait_recv()
            fw.start()
        for arrival in gat.forward_arrivals():
            arrival.wait_recv()
        for cp in sends + forwards:
            cp.wait_send()

    return pl.pallas_call(body, name=name, in_specs=[ANY] * n, out_specs=[ANY] * n, out_shape=_Gather.out_shapes(arrs),
                          scratch_shapes=_Gather.semaphores(arrs))(*arrs)


def _grad_blocks(gs):
    return max(n for n in (1, 2, 3, 4) if all(g.shape[1] // 2 % (16 * n) == 0 for g in gs))


def _pair_swap_copies(g_refs, r_refs, send, recv):
    _, _, c, _, sib, _ = _place()
    copies = []
    for a, (g, r) in enumerate(zip(g_refs, r_refs)):
        h = g.shape[1] // 2
        copies.append(_rcopy(g.at[:, pl.ds(pl.multiple_of((1 - c) * h, 8), h)], r, send, recv, a, sib))
    return copies


def _half_stack_shapes(gs, dtype=None):
    return [jax.ShapeDtypeStruct((N_CHIPS, g.shape[1] // 2, g.shape[2]), dtype or g.dtype) for g in gs]


def _pair_swap(gs, wholes, name):
    n, nw = len(gs), len(wholes)

    def body(*refs):
        ins, outs, (send, recv) = refs[:n + nw], refs[n + nw:2 * (n + nw)], refs[2 * (n + nw):]
        copies = _pair_swap_copies(ins[:n], outs[:n], send, recv)
        copies += [_rcopy(ins[n + k], outs[n + k], send, recv, n + k, _place()[4]) for k in range(nw)]
        for cp in copies:
            cp.start()
        for cp in copies:
            cp.wait()

    return pl.pallas_call(
        body, name=name, in_specs=[ANY] * (n + nw), out_specs=[ANY] * (n + nw),
        out_shape=_half_stack_shapes(gs) + [jax.ShapeDtypeStruct(w.shape, w.dtype) for w in wholes],
        scratch_shapes=[pltpu.SemaphoreType.DMA((n + nw,)), pltpu.SemaphoreType.DMA((n + nw,))],
    )(*gs, *wholes)


def _pair_sum(place, gs, rs, small=None, name="pair_sum"):
    n = len(gs)
    nb = _grad_blocks(gs)

    def body(place_ref, *refs):
        g_refs, r_refs, p_refs = refs[:n], refs[n:2 * n], refs[-n - 1:-1] if small else refs[-n:]
        for a in range(n):
            p_refs[a][0] = (g_refs[a][0] + r_refs[a][0]).astype(p_refs[a].dtype)
        if small:
            @pl.when((pl.program_id(0) == 0) & (pl.program_id(1) == 0))
            def _():
                refs[-1][...] = refs[2 * n][...] + refs[2 * n + 1][...]

    in_specs, out_specs = [], []
    for g in gs:
        blk = (1, g.shape[1] // 2 // nb, g.shape[2])
        in_specs.append(pl.BlockSpec(blk, lambda i, k, p: (k, p[0] * nb + i, 0)))
    for g in gs:
        blk = (1, g.shape[1] // 2 // nb, g.shape[2])
        in_specs.append(pl.BlockSpec(blk, lambda i, k, p: (k, i, 0)))
        out_specs.append(pl.BlockSpec(blk, lambda i, k, p: (k, i, 0)))
    out_shape = _half_stack_shapes(gs, BF16)
    if small:
        sm_spec = pl.BlockSpec(small[0].shape, lambda i, k, p: (0, 0))
        in_specs += [sm_spec, sm_spec]
        out_specs.append(sm_spec)
        out_shape.append(jax.ShapeDtypeStruct(small[0].shape, F32))
    return pl.pallas_call(
        body, name=name,
        grid_spec=pltpu.PrefetchScalarGridSpec(num_scalar_prefetch=1, grid=(nb, N_CHIPS), in_specs=in_specs,
                                               out_specs=out_specs),
        out_shape=out_shape,
        compiler_params=_params(("arbitrary", "arbitrary")),
    )(place, *gs, *rs, *(small or ()))


def _chip_swap_copies(p_refs, ri_refs, send, recv):
    _, _, c, _, _, others = _place()
    n = len(p_refs)
    return [_rcopy(p_refs[a].at[2 * px + py], ri_refs[a].at[j], send, recv, j * n + a, (px, py, c))
            for j, (px, py) in enumerate(others) for a in range(n)]


def _chip_swap_shapes(ps):
    return [jax.ShapeDtypeStruct((3,) + p.shape[1:], p.dtype) for p in ps]


def _chip_swap(ps, pair):
    n = len(ps)

    def body(*refs):
        start, finish = _chip_swap_plan(refs[:n], refs[n], refs[n + 1:2 * n + 1], refs[2 * n + 1], *refs[2 * n + 2:])
        start()
        finish()

    return pl.pallas_call(
        body, name="chip_swap", in_specs=[ANY] * (n + 1), out_specs=[ANY] * (n + 1),
        out_shape=_chip_swap_out_shapes(ps, pair), scratch_shapes=_chip_swap_semaphores(n),
    )(*ps, pair)


def _chip_swap_plan(p_refs, pair_ref, ri_refs, sm4_ref, send, recv, lsem):
    n = len(p_refs)
    hs = SMALL_ROWS // 2
    x, y, c, me, sib, others = _place()
    local = pltpu.make_async_copy(pair_ref, sm4_ref.at[me], lsem.at[0])
    copies = _chip_swap_copies(p_refs, ri_refs, send, recv)
    arrivals = list(copies)
    for j, (px, py) in enumerate(others):
        copies.append(_rcopy(_half(pair_ref, c, hs), _half(sm4_ref.at[me], c, hs), send, recv, 3 * n + j, (px, py, c)))
        part = _half(sm4_ref.at[2 * px + py], c, hs)
        arrivals.append(_rcopy(part, part, send, recv, 3 * n + j, (px, py, c)))

    def start():
        local.start()
        for cp in copies:
            cp.start()

    def finish():
        for arrival in arrivals:
            arrival.wait_recv()
        for cp in copies:
            cp.wait_send()
        local.wait()

    return start, finish


def _chip_swap_out_shapes(ps, pair):
    return _chip_swap_shapes(ps) + [jax.ShapeDtypeStruct((N_CHIPS,) + pair.shape, pair.dtype)]


def _chip_swap_semaphores(n):
    k = 3 * (n + 1)
    return [pltpu.SemaphoreType.DMA((k,)), pltpu.SemaphoreType.DMA((k,)), pltpu.SemaphoreType.DMA((1,))]


def _chip_sum(place, gs, rs, ris, name="chip_sum"):
    n = len(gs)
    nb = _grad_blocks(gs)

    def body(place_ref, *refs):
        g_refs, r_refs, ri_refs, o_refs = refs[:n], refs[n:2 * n], refs[2 * n:3 * n], refs[3 * n:]
        for a in range(n):
            ri = ri_refs[a]
            o_refs[a][...] = (g_refs[a][0] + r_refs[a][0]) + ri[0].astype(F32) + ri[1].astype(F32) + ri[2].astype(F32)

    in_specs, out_specs, out_shape = [], [], []
    for g in gs:
        blk = (1, g.shape[1] // 2 // nb, g.shape[2])
        in_specs.append(pl.BlockSpec(blk, lambda i, p: (p[1], p[0] * nb + i, 0)))
    for g in gs:
        blk = (1, g.shape[1] // 2 // nb, g.shape[2])
        in_specs.append(pl.BlockSpec(blk, lambda i, p: (p[1], i, 0)))
    for g in gs:
        rb = g.shape[1] // 2 // nb
        in_specs.append(pl.BlockSpec((3, rb, g.shape[2]), lambda i, p: (0, i, 0)))
        out_specs.append(pl.BlockSpec((rb, g.shape[2]), lambda i, p: (p[0] * nb + i, 0)))
        out_shape.append(jax.ShapeDtypeStruct(g.shape[1:], F32))
    return pl.pallas_call(
        body, name=name,
        grid_spec=pltpu.PrefetchScalarGridSpec(num_scalar_prefetch=1, grid=(nb,), in_specs=in_specs, out_specs=out_specs),
        out_shape=out_shape,
        compiler_params=_params(("arbitrary",)),
    )(place, *gs, *rs, *ris)


def _pair_fill_copies(g_refs, send, recv):
    _, _, c, _, sib, _ = _place()
    copies, waits = [], []
    for a, g in enumerate(g_refs):
        h = g.shape[0] // 2
        mine, theirs = _half(g, c, h), _half(g, 1 - c, h)
        copies.append(_rcopy(mine, mine, send, recv, a, sib))
        waits.append(_rcopy(theirs, theirs, send, recv, a, sib))
    return copies, waits


def _pair_fill(gfs, sm4):
    n = len(gfs)
    hs = SMALL_ROWS // 2

    def body(*refs):
        g_refs, sm4_ref = refs[n + 1:2 * n + 1], refs[2 * n + 1]
        send, recv = refs[2 * n + 2:]
        x, y, c, me, sib, others = _place()
        copies, waits = _pair_fill_copies(g_refs, send, recv)
        for j, (px, py) in enumerate(others):
            chip = 2 * px + py
            mine, theirs = _half(sm4_ref.at[chip], c, hs), _half(sm4_ref.at[chip], 1 - c, hs)
            copies.append(pltpu.make_async_remote_copy(src_ref=mine, dst_ref=mine, send_sem=send.at[n + j],
                                                       recv_sem=recv.at[n + j], device_id=sib, device_id_type=MESH))
            waits.append(pltpu.make_async_remote_copy(src_ref=theirs, dst_ref=theirs, send_sem=send.at[n + j],
                                                      recv_sem=recv.at[n + j], device_id=sib, device_id_type=MESH))
        for cp in copies:
            cp.start()
        for w in waits:
            w.wait_recv()
        for cp in copies:
            cp.wait_send()

    return pl.pallas_call(
        body, name="pair_fill", in_specs=[ANY] * (n + 1), out_specs=[ANY] * (n + 1),
        out_shape=[jax.ShapeDtypeStruct(g.shape, g.dtype) for g in gfs] + [jax.ShapeDtypeStruct(sm4.shape, sm4.dtype)],
        input_output_aliases={i: i for i in range(n + 1)},
        scratch_shapes=[pltpu.SemaphoreType.DMA((n + 3,)), pltpu.SemaphoreType.DMA((n + 3,))],
    )(*gfs, sm4)


def _adamw_math(w, g, m, v):
    m = ADAM_B1 * m + (1.0 - ADAM_B1) * g
    v = ADAM_B2 * v + (1.0 - ADAM_B2) * (g * g)
    m_hat = m / (1.0 - ADAM_B1 ** ADAM_STEP)
    v_hat = v / (1.0 - ADAM_B2 ** ADAM_STEP)
    return -ADAM_LR * (m_hat / (jnp.sqrt(v_hat) + ADAM_EPS) + ADAM_WD * w), m, v


def _adamw(items, steps, name):
    n = len(items)

    def body(*refs):
        for a in range(n):
            g = refs[4 * a + 1][...]
            d, mo, vo = _adamw_math(refs[4 * a][...], g, refs[4 * a + 2][...], refs[4 * a + 3][...])
            for out, val in zip(refs[4 * n + 4 * a:4 * n + 4 * a + 4], (g, d, mo, vo)):
                out[...] = val

    spec = lambda w: pl.BlockSpec((w.shape[0] // steps, w.shape[1]), lambda i: (i, 0))
    flat = pl.pallas_call(
        body, name=name, grid=(steps,), in_specs=[spec(it[0]) for it in items for _ in range(4)],
        out_specs=[spec(it[0]) for it in items for _ in range(4)],
        out_shape=[jax.ShapeDtypeStruct(it[0].shape, F32) for it in items for _ in range(4)],
        compiler_params=_params(("arbitrary",)),
    )(*[a for it in items for a in it])
    return [flat[4 * a:4 * a + 4] for a in range(n)]


def _adamw_small(sm4, wmv):
    views = SMALL_VIEWS[:-1]
    n = len(views)

    def body(sm4_ref, *refs):
        g_all = ((sm4_ref[0] + sm4_ref[1]) + sm4_ref[2]) + sm4_ref[3]
        for a, (name, rows, cols) in enumerate(views):
            row = SMALL_OFFSETS[name]
            g = g_all[row:row + rows, :cols]
            d, mo, vo = _adamw_math(refs[3 * a][...], g, refs[3 * a + 1][...], refs[3 * a + 2][...])
            for out, val in zip(refs[3 * n + 4 * a:3 * n + 4 * a + 4], (g, d, mo, vo)):
                out[...] = val
        row = SMALL_OFFSETS["loss"]
        refs[-1][...] = g_all[row:row + 1, :128]

    flat = pl.pallas_call(
        body, name="adamw_small",
        out_shape=[jax.ShapeDtypeStruct((rows, cols), F32) for _, rows, cols in views for _ in range(4)]
        + [jax.ShapeDtypeStruct((1, 128), F32)],
        compiler_params=pltpu.CompilerParams(vmem_limit_bytes=VMEM_LIMIT),
    )(sm4, *[a for t in wmv for a in t])
    return [flat[4 * a:4 * a + 4] for a in range(n)] + [flat[-1]]


SMALL_NAMES = ("attn_pre_norm", "mla_q_norm", "mla_kv_norm", "mla_w_ukv", "mla_out_norm", "hgrn_lb_logits",
               "hgrn_out_norm", "attn_post_norm", "ffn_pre_norm", "ffn_post_norm")
BIG_NAMES = ("w_in", "mla_w_uq", "w_out", "w_gate", "w_up", "w_down")
WEIGHT_NAMES = ("attn_pre_norm", "w_in", "mla_q_norm", "mla_w_uq", "mla_kv_norm", "mla_w_ukv", "mla_out_norm",
                "hgrn_lb_logits", "hgrn_out_norm", "w_out", "attn_post_norm", "ffn_pre_norm", "w_gate", "w_up", "w_down",
                "ffn_post_norm")


UQ_COMM_SHAPE = (192, 384)


def _pack_small(vals):
    parts, row = [], 0
    for name, rows, cols in sorted(SMALL_VIEWS, key=lambda view: SMALL_OFFSETS[view[0]]):
        assert SMALL_OFFSETS[name] == row
        parts.append(jnp.pad(vals[name].reshape(rows, cols), ((0, 0), (0, D_MODEL - cols))))
        row += rows
    parts.append(jnp.zeros((SMALL_ROWS - row, D_MODEL), F32))
    return jnp.concatenate(parts, axis=0)


def kernel(x, positions, attn_pre_norm, w_in, mla_q_norm, mla_w_uq, mla_kv_norm, mla_w_ukv, mla_out_norm, hgrn_lb_logits, hgrn_out_norm, w_out, attn_post_norm, ffn_pre_norm, w_gate, w_up, w_down, ffn_post_norm, loss_target, m_attn_pre_norm, m_w_in, m_mla_q_norm, m_mla_w_uq, m_mla_kv_norm, m_mla_w_ukv, m_mla_out_norm, m_hgrn_lb_logits, m_hgrn_out_norm, m_w_out, m_attn_post_norm, m_ffn_pre_norm, m_w_gate, m_w_up, m_w_down, m_ffn_post_norm, v_attn_pre_norm, v_w_in, v_mla_q_norm, v_mla_w_uq, v_mla_kv_norm, v_mla_w_ukv, v_mla_out_norm, v_hgrn_lb_logits, v_hgrn_out_norm, v_w_out, v_attn_post_norm, v_ffn_pre_norm, v_w_gate, v_w_up, v_w_down, v_ffn_post_norm):
    args = locals()
    W = {n: args[n] for n in WEIGHT_NAMES}
    M = {n: args["m_" + n] for n in WEIGHT_NAMES}
    V = {n: args["v_" + n] for n in WEIGHT_NAMES}
    T = x.shape[1]
    cx, cy, cc = lax.axis_index("x"), lax.axis_index("y"), lax.axis_index("c")

    win_rows = D_IN // N_CHIPS
    shard2d = {"w_in": (win_rows, D_MODEL), "mla_w_uq": (Q_RANK // N_CHIPS, MLA_HEADS * MLA_QK),
               "w_out": (D_MODEL // N_CHIPS, D_MODEL), "w_gate": (FF_SHARD, D_MODEL), "w_up": (FF_SHARD, D_MODEL),
               "w_down": (FF_SHARD, D_MODEL)}
    transposed = ("w_in", "w_gate", "w_up")
    to2d = lambda n, a: a[0].T if n in transposed else a.reshape(shard2d[n])
    from2d = lambda n, t: t.T[None] if n in transposed else t.reshape(W[n].shape)
    me = 2 * cx + cy
    place = jnp.stack([cc, me]).astype(jnp.int32)
    local_b = [to2d(n, W[n]).astype(BF16) for n in BIG_NAMES]
    local_b[0] = jnp.pad(local_b[0], ((0, WIN_COMM_SHAPE[1] - win_rows), (0, 0)))
    stacks = _gather_chips(local_b[:2], "gather_weights")
    win4, wuq4 = [lax.dynamic_update_slice(s, l[None], (me, 0, 0)) for s, l in zip(stacks, local_b)]
    win_t = win4[:, :win_rows].reshape(D_IN, D_MODEL)
    wuq_full = wuq4.reshape(Q_RANK, MLA_HEADS, MLA_QK)
    win_arr, wq_arr, wk_arr, wv_arr = _arrange_weights(win_t, wuq_full, mla_w_ukv[0].astype(BF16))
    small = {n: W[n][0] if n == "mla_w_ukv" else W[n].reshape(-1, W[n].shape[-1]) for n in SMALL_NAMES}

    loss_local, dx, grads, ffn_final = _local_step(x[0], positions.reshape(T, 1), loss_target[0], small, win_arr,
                                                           wq_arr, wk_arr, wv_arr, local_b[2:], place)

    gs = [grads["w_in"], grads["mla_w_uq"].reshape((N_CHIPS,) + UQ_COMM_SHAPE)]
    sm = _pack_small({**grads, "loss": loss_local})
    *rs, ssib = _pair_swap(gs, (sm,), "pair_swap")
    *ps, pair = _pair_sum(place, gs, rs, small=(sm, ssib))
    ffn_names, rest_names = BIG_NAMES[3:], BIG_NAMES[:2]
    g2d = dict(zip(ffn_names + BIG_NAMES[2:3], ffn_final))
    adam_in = lambda names_: [(to2d(n, W[n]), g2d[n], to2d(n, M[n]), to2d(n, V[n])) for n in names_]
    updates = dict(zip(ffn_names, _adamw(adam_in(ffn_names), 8, "adamw_ffn")))
    updates.update(zip(BIG_NAMES[2:3], _adamw(adam_in(BIG_NAMES[2:3]), 8, "adamw_w_out")))
    *ris, sm4 = _chip_swap(ps, pair)
    *gfin, smf = _pair_fill(_chip_sum(place, gs, rs, ris), sm4)

    g2d.update({n: gfin[k].reshape((-1,) + shard2d[n][1:]) for k, n in enumerate(rest_names)})
    updates.update(zip(rest_names, _adamw(adam_in(rest_names), 3, "adamw_w_in")))
    G, DW, NM, NV = {}, {}, {}, {}
    for n, outs in updates.items():
        G[n], DW[n], NM[n], NV[n] = (from2d(n, t) for t in outs)
    view2d = lambda n, a: a.reshape(next((r, c) for name, r, c in SMALL_VIEWS if name == n))
    *res, loss_row = _adamw_small(smf, [tuple(view2d(n, t[n]) for t in (W, M, V)) for n in SMALL_NAMES])
    for n, outs in zip(SMALL_NAMES, res):
        G[n], DW[n], NM[n], NV[n] = (t.reshape(W[n].shape) for t in outs)
    loss = loss_row[0, 0]
    return (loss, dx[None], *[G[n] for n in WEIGHT_NAMES], *[DW[n] for n in WEIGHT_NAMES],
            *[NM[n] for n in WEIGHT_NAMES], *[NV[n] for n in WEIGHT_NAMES])
```

```python
import jax
import jax.numpy as jnp
from jax import lax
from jax.experimental import pallas as pl
from jax.experimental.pallas import tpu as pltpu

F32 = jnp.float32
BF16 = jnp.bfloat16
MXU_DTYPE = BF16

D_MODEL = 1024
MLA_HEADS = 8
MLA_NOPE = 64
MLA_ROPE = 32
MLA_V = 64
MLA_QK = MLA_NOPE + MLA_ROPE
Q_RANK = 384
KV_RANK = 128
MLA_WIDTH = MLA_HEADS * MLA_V
HEAD_PAD = 128
HGRN_HEADS = 4
HGRN_DIM = 128
HGRN_WIDTH = HGRN_HEADS * HGRN_DIM
CHUNK = 64
SUB = 16
HGRN_CPI = 4
D_IN = Q_RANK + KV_RANK + MLA_ROPE + 4 * HGRN_WIDTH
D_IN_ARR = Q_RANK + KV_RANK + HEAD_PAD + 4 * HGRN_WIDTH
D_FF = 2816
N_CHIPS = 4
FF_SHARD = D_FF // N_CHIPS
EPS = 1e-6
ROPE_THETA = 10000.0
ATTN_SCALE = MLA_QK ** -0.5
ATTN_SCALE_LOG2 = ATTN_SCALE * 1.4426950408889634
NEG_BIG = -1e30

ADAM_LR = 0.001
ADAM_B1 = 0.9
ADAM_B2 = 0.999
ADAM_EPS = 1e-08
ADAM_WD = 0.01
ADAM_STEP = 10

VMEM_LIMIT = 56 * 1024 * 1024
FFN_BWD_VMEM = 62 * 1024 * 1024

SMALL_VIEWS = (("attn_pre_norm", 1, 1024), ("mla_q_norm", 1, 384), ("mla_kv_norm", 1, 128), ("mla_w_ukv", 128, 1024),
               ("mla_out_norm", 1, 512), ("hgrn_lb_logits", 2, 512), ("hgrn_out_norm", 1, 512),
               ("attn_post_norm", 1, 1024), ("ffn_pre_norm", 1, 1024), ("ffn_post_norm", 1, 1024), ("loss", 1, 1))
ROW_TILE = 8


def _small_layout():
    offsets, row = {}, 0
    for whole in (True, False):
        for name, rows, _ in SMALL_VIEWS:
            if (rows % ROW_TILE == 0) == whole:
                offsets[name] = row
                row += rows
    return offsets, -(-row // (2 * ROW_TILE)) * 2 * ROW_TILE


SMALL_OFFSETS, SMALL_ROWS = _small_layout()

MESH = pl.DeviceIdType.MESH
ANY = pl.BlockSpec(memory_space=pl.ANY)


def _dot(a, b, dims, exact):
    if exact:
        return lax.dot_general(a.astype(F32), b.astype(F32), (dims, ((), ())), precision=lax.Precision.HIGH,
                               preferred_element_type=F32)
    return lax.dot_general(a.astype(MXU_DTYPE), b.astype(MXU_DTYPE), (dims, ((), ())), preferred_element_type=F32)


def _mm(a, b, exact=False):
    return _dot(a, b, ((1,), (0,)), exact)


def _mm_nt(a, b, exact=False):
    return _dot(a, b, ((1,), (1,)), exact)


def _mm_tn(a, b, exact=False):
    return _dot(a, b, ((0,), (0,)), exact)


def _rms_fwd(x, w):
    r = lax.rsqrt(jnp.mean(x * x, axis=-1, keepdims=True) + EPS)
    xn = x * r
    return xn * w, xn, r


def _rms_bwd(dy, xn, r, w):
    dxn = dy * w
    dx = r * (dxn - xn * jnp.mean(dxn * xn, axis=-1, keepdims=True))
    dw = jnp.sum(dy * xn, axis=0, keepdims=True)
    return dx, dw


def _group_sums(v, gs):
    t, n = v.shape
    lane = lax.broadcasted_iota(jnp.int32, (t, 128), 1)
    out = []
    for p in range(n // 128):
        vb = v[:, 128 * p:128 * (p + 1)]
        if gs == 128:
            out.append(jnp.sum(vb, axis=-1, keepdims=True))
        else:
            out.append(jnp.sum(jnp.where(lane < 64, vb, 0.0), axis=-1, keepdims=True))
            out.append(jnp.sum(jnp.where(lane >= 64, vb, 0.0), axis=-1, keepdims=True))
    return out


def _group_bcast(sums, gs, t):
    lane = lax.broadcasted_iota(jnp.int32, (t, 128), 1)
    if gs == 128:
        return jnp.concatenate([jnp.broadcast_to(s, (t, 128)) for s in sums], axis=-1)
    return jnp.concatenate([jnp.where(lane < 64, sums[2 * p], sums[2 * p + 1]) for p in range(len(sums) // 2)],
                           axis=-1)


def _grms_fwd(x, w, gs):
    t = x.shape[0]
    r = lax.rsqrt(_group_bcast(_group_sums(x * x, gs), gs, t) * (1.0 / gs) + EPS)
    xn = x * r
    return xn * w, xn, r


def _grms_bwd(dy, xn, r, w, gs):
    t = dy.shape[0]
    dxn = dy * w
    dx = r * (dxn - xn * (_group_bcast(_group_sums(dxn * xn, gs), gs, t) * (1.0 / gs)))
    dw = jnp.sum(dy * xn, axis=0, keepdims=True)
    return dx, dw


def _rope_tables(c_tab, s_tab):
    lane = lax.broadcasted_iota(jnp.int32, c_tab.shape, 1)
    first = (lane >= MLA_NOPE) & (lane < MLA_NOPE + MLA_ROPE // 2)
    second = (lane >= MLA_NOPE + MLA_ROPE // 2) & (lane < MLA_QK)
    return c_tab, jnp.where(first, -s_tab, 0.0), jnp.where(second, s_tab, 0.0)


def _rope(v, c, sa, sb):
    return v * c + pltpu.roll(v, HEAD_PAD - MLA_ROPE // 2, 1) * sa + pltpu.roll(v, MLA_ROPE // 2, 1) * sb


def _rope_bwd(d, c, sa, sb):
    return d * c - pltpu.roll(d, HEAD_PAD - MLA_ROPE // 2, 1) * sa - pltpu.roll(d, MLA_ROPE // 2, 1) * sb


def _params(sem, vmem=VMEM_LIMIT):
    return pltpu.CompilerParams(dimension_semantics=sem, vmem_limit_bytes=vmem)


def _in_fwd(x, pos, invf, w_pre, win, qnw, wq, kvnw, wk, wv, tt=512):
    T = x.shape[0]

    def body(x_ref, pos_ref, invf_ref, wpre_ref, win_ref, qnw_ref, wq_ref, kvnw_ref, wk_ref, wv_ref,
             cq_ref, ckv_ref, xph_ref, q_ref, k_ref, v_ref, kt_ref, vt_ref, rc_ref, rs_ref):
        u, _, _ = _rms_fwd(x_ref[...], wpre_ref[...])
        lo = Q_RANK + KV_RANK + HEAD_PAD
        xp = _mm_nt(u, win_ref[:lo, :])
        xph_ref[...] = _mm_nt(u, win_ref[lo:, :])
        cq = xp[:, :Q_RANK]
        ckv = xp[:, Q_RANK:Q_RANK + KV_RANK]
        kr = xp[:, Q_RANK + KV_RANK:]
        cq_ref[...] = cq
        ckv_ref[...] = ckv
        ang = pos_ref[...].astype(F32) * invf_ref[...]
        c_tab = jnp.cos(ang)
        s_tab = jnp.sin(ang)
        rc_ref[...] = c_tab
        rs_ref[...] = s_tab
        c, sa, sb = _rope_tables(c_tab, s_tab)
        qn, _, _ = _rms_fwd(cq, qnw_ref[...])
        q = _mm(qn, wq_ref[...])
        kvn, _, _ = _rms_fwd(ckv, kvnw_ref[...])
        kn = _mm(kvn, wk_ref[...])
        v = _mm(kvn, wv_ref[...])
        v_ref[...] = v.astype(v_ref.dtype)
        vt_ref[...] = v.T.astype(vt_ref.dtype)
        krr = _rope(kr, c, sa, sb)
        for h in range(MLA_HEADS):
            sl = slice(HEAD_PAD * h, HEAD_PAD * (h + 1))
            q_ref[:, sl] = (_rope(q[:, sl], c, sa, sb) * ATTN_SCALE_LOG2).astype(q_ref.dtype)
            kh = kn[:, sl] + krr
            k_ref[:, sl] = kh.astype(k_ref.dtype)
            kt_ref[sl, :] = kh.T.astype(kt_ref.dtype)

    row = lambda w: pl.BlockSpec((tt, w), lambda i: (i, 0))
    full = lambda a: pl.BlockSpec(a.shape, lambda i: (0,) * a.ndim)
    qk_w = MLA_HEADS * HEAD_PAD
    return pl.pallas_call(
        body, name="in_fwd", grid=(T // tt,),
        in_specs=[row(D_MODEL), row(1), full(invf), full(w_pre), full(win), full(qnw), full(wq), full(kvnw),
                  full(wk), full(wv)],
        out_specs=[row(Q_RANK), row(KV_RANK), row(4 * HGRN_WIDTH), row(qk_w), row(qk_w), row(MLA_WIDTH),
                   pl.BlockSpec((qk_w, tt), lambda i: (0, i)), pl.BlockSpec((MLA_WIDTH, tt), lambda i: (0, i)),
                   row(HEAD_PAD), row(HEAD_PAD)],
        out_shape=[jax.ShapeDtypeStruct((T, Q_RANK), F32), jax.ShapeDtypeStruct((T, KV_RANK), F32),
                   jax.ShapeDtypeStruct((T, 4 * HGRN_WIDTH), F32), jax.ShapeDtypeStruct((T, qk_w), MXU_DTYPE),
                   jax.ShapeDtypeStruct((T, qk_w), MXU_DTYPE), jax.ShapeDtypeStruct((T, MLA_WIDTH), MXU_DTYPE),
                   jax.ShapeDtypeStruct((qk_w, T), MXU_DTYPE), jax.ShapeDtypeStruct((MLA_WIDTH, T), MXU_DTYPE),
                   jax.ShapeDtypeStruct((T, HEAD_PAD), F32), jax.ShapeDtypeStruct((T, HEAD_PAD), F32)],
        compiler_params=_params(("arbitrary",)),
    )(x, pos, invf, w_pre, win, qnw, wq, kvnw, wk, wv)


def _attn_fwd_t(qb, kb, vt, gather=(), tq=256, hps=8):
    T = qb.shape[0]
    nq = T // tq
    ng = len(gather)
    steps = (MLA_HEADS // hps) * nq
    pass_on = steps - 3

    def body(q_ref, k_ref, vt_ref, *rest):
        o_ref, lse_ref = rest[ng:ng + 2]
        acc_scr = rest[2 * ng + 2]
        qi = pl.program_id(1)
        step_no = pl.program_id(0) * nq + qi
        if ng:
            gat = _Gather(rest[:ng], rest[ng + 2:2 * ng + 2], *rest[2 * ng + 3:])

            @pl.when(step_no == 0)
            def _():
                for cp in gat.sends():
                    cp.start()

            @pl.when(step_no == pass_on)
            def _():
                for arrival in gat.arrivals():
                    arrival.wait_recv()
                for cp in gat.forwards():
                    cp.start()

        heads = [slice(HEAD_PAD * a, HEAD_PAD * (a + 1)) for a in range(hps)]
        acc_scr[...] = jnp.zeros_like(acc_scr)

        def step(j, carry, masked):
            start = pl.multiple_of(j * tq, tq)
            scores = [_mm_nt(k_ref[pl.ds(start, tq), heads[a]], q_ref[:, heads[a]]) for a in range(hps)]
            new, probs, alphas = [], [], []
            for a in range(hps):
                m, l = carry[a]
                s = scores[a]
                if masked:
                    kk = lax.broadcasted_iota(jnp.int32, (tq, tq), 0)
                    qq = lax.broadcasted_iota(jnp.int32, (tq, tq), 1)
                    s = jnp.where(kk <= qq, s, NEG_BIG)
                m_new = jnp.maximum(m, jnp.max(s, axis=0, keepdims=True))
                alpha = jnp.exp2(m - m_new)
                p = jnp.exp2(s - m_new)
                l = l * alpha + jnp.sum(p, axis=0, keepdims=True)
                new.append((m_new, l))
                probs.append(p.astype(MXU_DTYPE))
                alphas.append(alpha)
                if a % 2:
                    pr = a // 2
                    vtj = vt_ref[2 * MLA_V * pr:2 * MLA_V * (pr + 1), pl.ds(start, tq)]
                    none = jnp.zeros((MLA_V, tq), vtj.dtype)
                    pv = (_mm(jnp.concatenate([vtj[:MLA_V], none], axis=0), probs[a - 1])
                          + _mm(jnp.concatenate([none, vtj[MLA_V:]], axis=0), probs[a]))
                    acc_scr[pr] = acc_scr[pr] * jnp.where(row < MLA_V, alphas[a - 1], alphas[a]) + pv
            return tuple(new)

        row = lax.broadcasted_iota(jnp.int32, (2 * MLA_V, tq), 0)
        init = tuple((jnp.full((1, tq), NEG_BIG, F32), jnp.zeros((1, tq), F32)) for _ in range(hps))
        carry = lax.fori_loop(0, qi, lambda j, c: step(j, c, False), init)
        carry = step(qi, carry, True)
        for pr in range(hps // 2):
            (m0, l0), (m1, l1) = carry[2 * pr], carry[2 * pr + 1]
            ot = acc_scr[pr] / jnp.where(row < MLA_V, l0, l1)
            o_ref[:, 2 * MLA_V * pr:2 * MLA_V * (pr + 1)] = ot.T
            lse_ref[pr, 0:1, :] = m0 + jnp.log2(l0)
            lse_ref[pr, 1:2, :] = m1 + jnp.log2(l1)

        if ng:
            @pl.when(step_no == steps - 1)
            def _():
                for arrival in gat.forward_arrivals():
                    arrival.wait_recv()
                for cp in gat.sends() + gat.forwards():
                    cp.wait_send()

    return pl.pallas_call(
        body, name="attn_fwd", grid=(MLA_HEADS // hps, nq),
        in_specs=[pl.BlockSpec((tq, hps * HEAD_PAD), lambda g, i: (i, g)),
                  pl.BlockSpec((T, hps * HEAD_PAD), lambda g, i: (0, g)),
                  pl.BlockSpec((hps * MLA_V, T), lambda g, i: (g, 0))] + [ANY] * ng,
        out_specs=[pl.BlockSpec((tq, hps * MLA_V), lambda g, i: (i, g)),
                   pl.BlockSpec((hps // 2, 2, tq), lambda g, i: (g, 0, i))] + [ANY] * ng,
        out_shape=[jax.ShapeDtypeStruct((T, MLA_WIDTH), F32), jax.ShapeDtypeStruct((MLA_HEADS // 2, 2, T), F32)]
        + _Gather.out_shapes(gather),
        scratch_shapes=[pltpu.VMEM((hps // 2, 2 * MLA_V, tq), F32)] + (_Gather.semaphores(gather) if ng else []),
        compiler_params=_params(("arbitrary", "arbitrary")),
    )(qb, kb, vt, *gather)


def _attn_bwd_t(qb, kb, kt, vb, dob, lse, dvec, send=(), tq=512, hps=4):
    T = qb.shape[0]
    nq = T // tq
    ns = len(send)
    steps = (MLA_HEADS // hps) * nq

    def body(q_ref, k_ref, kt_ref, v_ref, do_ref, lse_ref, d_ref, *rest):
        dqt_ref, dk_ref, dv_ref = rest[ns:ns + 3]
        va_scr, dv_scr = rest[2 * ns + 3:2 * ns + 5]
        j = pl.program_id(1)
        step_no = pl.program_id(0) * nq + j
        if ns:
            @pl.when(step_no == 0)
            def _():
                for cp in _chip_swap_copies(rest[:ns], rest[ns + 3:2 * ns + 3], *rest[2 * ns + 5:]):
                    cp.start()

        @pl.when(j == 0)
        def _():
            dqt_ref[...] = jnp.zeros_like(dqt_ref)

        lane = lax.broadcasted_iota(jnp.int32, (tq, 2 * MLA_V), 1)
        heads = [slice(HEAD_PAD * a, HEAD_PAD * (a + 1)) for a in range(hps)]
        pairs = [slice(2 * MLA_V * p, 2 * MLA_V * (p + 1)) for p in range(hps // 2)]
        for pr in range(hps // 2):
            vpair = v_ref[:, pairs[pr]]
            va_scr[2 * pr] = jnp.where(lane < MLA_V, vpair, jnp.zeros_like(vpair))
            va_scr[2 * pr + 1] = jnp.where(lane >= MLA_V, vpair, jnp.zeros_like(vpair))
        dk_ref[...] = jnp.zeros_like(dk_ref)
        dv_scr[...] = jnp.zeros_like(dv_scr)

        def step(i, masked):
            start = pl.multiple_of(i * tq, tq)
            rows = pl.ds(start, tq)
            scores = [_mm_nt(k_ref[:, heads[a]], q_ref[rows, heads[a]]) for a in range(hps)]
            dps = [_mm_nt(va_scr[a], do_ref[rows, pairs[a // 2]]) for a in range(hps)]
            for a in range(hps):
                pr, r = a // 2, a % 2
                p = jnp.exp2(scores[a] - lse_ref[pr, r:r + 1, rows])
                if masked:
                    kk = lax.broadcasted_iota(jnp.int32, (tq, tq), 0)
                    qq = lax.broadcasted_iota(jnp.int32, (tq, tq), 1)
                    p = jnp.where(kk <= qq, p, 0.0)
                ds = p * (dps[a] - d_ref[pr, r:r + 1, rows])
                dv_scr[a] += _mm(p, do_ref[rows, pairs[pr]])
                dk_ref[:, heads[a]] += _mm(ds, q_ref[rows, heads[a]])
                dqt_ref[heads[a], rows] += _mm(kt_ref[heads[a], :], ds)

        def loop_body(i, _):
            step(i, False)
            return 0

        step(j, True)
        lax.fori_loop(j + 1, nq, loop_body, 0)
        for pr in range(hps // 2):
            dv_ref[:, pairs[pr]] = jnp.where(lane < MLA_V, dv_scr[2 * pr], dv_scr[2 * pr + 1])
        dk_ref[...] = dk_ref[...] * (ATTN_SCALE / ATTN_SCALE_LOG2)

        if ns:
            @pl.when(step_no == steps - 1)
            def _():
                for cp in _chip_swap_copies(rest[:ns], rest[ns + 3:2 * ns + 3], *rest[2 * ns + 5:]):
                    cp.wait()

    stat = pl.BlockSpec((hps // 2, 2, T), lambda g, j: (g, 0, 0))
    return pl.pallas_call(
        body, name="attn_bwd", grid=(MLA_HEADS // hps, nq),
        in_specs=[pl.BlockSpec((T, hps * HEAD_PAD), lambda g, j: (0, g)),
                  pl.BlockSpec((tq, hps * HEAD_PAD), lambda g, j: (j, g)),
                  pl.BlockSpec((hps * HEAD_PAD, tq), lambda g, j: (g, j)),
                  pl.BlockSpec((tq, hps * MLA_V), lambda g, j: (j, g)),
                  pl.BlockSpec((T, hps * MLA_V), lambda g, j: (0, g)), stat, stat] + [ANY] * ns,
        out_specs=[pl.BlockSpec((hps * HEAD_PAD, T), lambda g, j: (g, 0)),
                   pl.BlockSpec((tq, hps * HEAD_PAD), lambda g, j: (j, g)),
                   pl.BlockSpec((tq, hps * MLA_V), lambda g, j: (j, g))] + [ANY] * ns,
        out_shape=[jax.ShapeDtypeStruct((MLA_HEADS * HEAD_PAD, T), F32),
                   jax.ShapeDtypeStruct((T, MLA_HEADS * HEAD_PAD), F32),
                   jax.ShapeDtypeStruct((T, MLA_WIDTH), F32)] + _chip_swap_shapes(send),
        scratch_shapes=[pltpu.VMEM((hps, tq, 2 * MLA_V), vb.dtype), pltpu.VMEM((hps, tq, 2 * MLA_V), F32)]
        + ([pltpu.SemaphoreType.DMA((3 * ns,)), pltpu.SemaphoreType.DMA((3 * ns,))] if ns else []),
        compiler_params=_params(("arbitrary", "arbitrary")),
    )(qb, kb, kt, vb, dob, lse, dvec, *send)


def _cumsum_rows(x):
    n = x.shape[0]
    row = lax.broadcasted_iota(jnp.int32, x.shape, 0)
    s = 1
    while s < n:
        x = x + jnp.where(row >= s, pltpu.roll(x, s, 0), 0.0)
        s *= 2
    return x


def _rev_cumsum_rows(x):
    n = x.shape[0]
    row = lax.broadcasted_iota(jnp.int32, x.shape, 0)
    s = 1
    while s < n:
        x = x + jnp.where(row < n - s, pltpu.roll(x, n - s, 0), 0.0)
        s *= 2
    return x


def _lb_from_logits(l):
    l0, l1 = l[0:1, :], l[1:2, :]
    m = jnp.maximum(l0, l1)
    e0, e1 = jnp.exp(l0 - m), jnp.exp(l1 - m)
    return e0 / (e0 + e1)


def _hgrn_gates(hq, hf, lb):
    sig_f = jax.nn.sigmoid(hf)
    f = lb + (1.0 - lb) * sig_f
    sig_q = jax.nn.sigmoid(hq)
    return sig_f, f, jnp.log(f), 1.0 - f, sig_q, hq * sig_q


def _hgrn_intra(q, kk, b, exact=False):
    row = lax.broadcasted_iota(jnp.int32, b.shape, 0)
    qs, ks, eqs, eks, a_rows = [], [], [], [], []
    for i in range(CHUNK // SUB):
        ref = b[SUB * i + SUB // 2:SUB * i + SUB // 2 + 1, :]
        eq = jnp.exp(b[SUB * i:SUB * (i + 1), :] - ref)
        ek = jnp.exp(jnp.where(row < SUB * (i + 1), ref - b, NEG_BIG))
        qi = q[SUB * i:SUB * (i + 1), :] * eq
        ki = kk * ek
        a_rows.append(_mm_nt(qi, ki, exact))
        qs.append(qi), ks.append(ki), eqs.append(eq), eks.append(ek)
    tt = lax.broadcasted_iota(jnp.int32, (CHUNK, CHUNK), 0)
    ss = lax.broadcasted_iota(jnp.int32, (CHUNK, CHUNK), 1)
    causal = ss <= tt
    a = jnp.where(causal, jnp.concatenate(a_rows, axis=0), 0.0)
    return a, causal, qs, ks, eqs, eks


def _hgrn_fwd(xph, lbl, tg=512):
    T = xph.shape[0]
    ng, ncg = T // tg, tg // CHUNK
    cols = [slice(HGRN_DIM * h, HGRN_DIM * (h + 1)) for h in range(HGRN_HEADS)]

    def body(lbl_ref, hq_ref, hf_ref, hi_ref, o_ref, st_ref, s_scr):
        @pl.when(pl.program_id(0) == 0)
        def _():
            s_scr[...] = jnp.zeros_like(s_scr)

        lb = _lb_from_logits(lbl_ref[...])

        def chunks(it, _):
            pre = []
            for k in range(HGRN_CPI):
                c = it * HGRN_CPI + k
                rows = pl.ds(pl.multiple_of(c * CHUNK, CHUNK), CHUNK)
                for cs in cols:
                    _, _, lf, kk, _, q = _hgrn_gates(hq_ref[rows, cs], hf_ref[rows, cs], lb[:, cs])
                    v = hi_ref[rows, cs]
                    b = _cumsum_rows(lf)
                    a = _hgrn_intra(q, kk, b)[0]
                    b_last = b[CHUNK - 1:CHUNK, :]
                    pre.append((c, rows, q * jnp.exp(b), a, v, jnp.exp(b_last), _mm_tn(v, kk * jnp.exp(b_last - b))))
            for i, (c, rows, qe, a, v, ebl, upd) in enumerate(pre):
                h = i % HGRN_HEADS
                st = s_scr[h]
                st_ref[h, c] = st
                o_ref[rows, cols[h]] = _mm_nt(qe, st) + _mm(a, v)
                s_scr[h] = st * ebl + upd
            return 0

        lax.fori_loop(0, ncg // HGRN_CPI, chunks, 0)

    col = lambda k: pl.BlockSpec((tg, HGRN_WIDTH), lambda g: (g, k))
    return pl.pallas_call(
        body, name="hgrn_fwd", grid=(ng,),
        in_specs=[pl.BlockSpec((2, HGRN_WIDTH), lambda g: (0, 0)), col(0), col(1), col(2)],
        out_specs=[col(0), pl.BlockSpec((HGRN_HEADS, ncg, HGRN_DIM, HGRN_DIM), lambda g: (0, g, 0, 0))],
        out_shape=[jax.ShapeDtypeStruct((T, HGRN_WIDTH), F32),
                   jax.ShapeDtypeStruct((HGRN_HEADS, T // CHUNK, HGRN_DIM, HGRN_DIM), F32)],
        scratch_shapes=[pltpu.VMEM((HGRN_HEADS, HGRN_DIM, HGRN_DIM), F32)],
        compiler_params=_params(("arbitrary",)),
    )(lbl, xph, xph, xph)


def _hgrn_bwd(xph, lbl, states, d_o, fill=(), tg=512):
    T = xph.shape[0]
    ng, ncg = T // tg, tg // CHUNK
    cols = [slice(HGRN_DIM * h, HGRN_DIM * (h + 1)) for h in range(HGRN_HEADS)]
    nsub = CHUNK // SUB
    nf = len(fill)

    def body(lbl_ref, hq_ref, hf_ref, hi_ref, st_ref, do_ref, *rest):
        dhq_ref, dhf_ref, dhi_ref, dlg_ref = rest[nf:nf + 4]
        ds_scr, dlb_scr = rest[2 * nf + 4:2 * nf + 6]
        fill_copies = lambda: _pair_fill_copies(rest[nf + 4:2 * nf + 4], *rest[2 * nf + 6:])
        g = pl.program_id(0)

        @pl.when(g == 0)
        def _():
            ds_scr[...] = jnp.zeros_like(ds_scr)
            dlb_scr[...] = jnp.zeros_like(dlb_scr)
            for cp in (fill_copies()[0] if nf else ()):
                cp.start()

        lb = _lb_from_logits(lbl_ref[...])

        def chunks(it, _):
            pre = []
            for k, h in ((k, h) for k in range(HGRN_CPI) for h in range(HGRN_HEADS)):
                cs = cols[h]
                c = ncg - 1 - (it * HGRN_CPI + k)
                rows = pl.ds(pl.multiple_of(c * CHUNK, CHUNK), CHUNK)
                hq = hq_ref[rows, cs]
                sig_f, f, lf, kk, sig_q, q = _hgrn_gates(hq, hf_ref[rows, cs], lb[:, cs])
                v = hi_ref[rows, cs]
                do = do_ref[rows, cs]
                b = _cumsum_rows(lf)
                eb = jnp.exp(b)
                a, causal, qs, ks, eqs, eks = _hgrn_intra(q, kk, b)
                b_last = b[CHUNK - 1:CHUNK, :]
                st = st_ref[h, c]
                pre.append(dict(h=h, cs=cs, rows=rows, hq=hq, sig_f=sig_f, f=f, kk=kk, sig_q=sig_q, q=q, v=v, eb=eb, qs=qs,
                                ks=ks, eqs=eqs,
                                eks=eks, ebl=jnp.exp(b_last), el=jnp.exp(b_last - b), st=st,
                                da=jnp.where(causal, _mm_nt(do, v, True), 0.0), dq=_mm(do, st, True) * eb,
                                dv=_mm_tn(a, do), dsu=_mm_tn(do, q * eb, True)))
            for w in pre:
                dq_rows = []
                dk = jnp.zeros_like(w["q"])
                for i in range(nsub):
                    dai = w["da"][SUB * i:SUB * (i + 1), :]
                    dq_rows.append(_mm(dai, w["ks"][i], True) * w["eqs"][i])
                    dk = dk + _mm_tn(dai, w["qs"][i], True) * w["eks"][i]
                w["dq"] = w["dq"] + jnp.concatenate(dq_rows, axis=0)
                w["dk"] = dk
            for w in pre:
                h, cs, rows = w["h"], w["cs"], w["rows"]
                kk, el, ebl, dst = w["kk"], w["el"], w["ebl"], ds_scr[h]
                dk_state = _mm(w["v"], dst, True) * el
                dk = w["dk"] + dk_state
                e_last = (ebl * jnp.sum(w["st"] * dst, axis=0, keepdims=True)
                          + jnp.sum(kk * dk_state, axis=0, keepdims=True))
                dlf = _rev_cumsum_rows(w["q"] * w["dq"] - kk * dk) + e_last
                ds_scr[h] = dst * ebl + w["dsu"]
                df = dlf / w["f"] - dk
                sig_f, sig_q = w["sig_f"], w["sig_q"]
                dhf_ref[rows, cs] = df * (1.0 - lb[:, cs]) * sig_f * (1.0 - sig_f)
                dlb_scr[:, cs] += jnp.sum(df * (1.0 - sig_f), axis=0, keepdims=True)
                dhq_ref[rows, cs] = w["dq"] * sig_q * (1.0 + w["hq"] * (1.0 - sig_q))
                dhi_ref[rows, cs] = w["dv"] + _mm_nt(kk * el, dst)
            return 0

        lax.fori_loop(0, ncg // HGRN_CPI, chunks, 0)

        @pl.when(g == ng - 1)
        def _():
            dl0 = dlb_scr[...] * lb * (1.0 - lb)
            dlg_ref[...] = jnp.concatenate([dl0, -dl0], axis=0)
            if nf:
                copies, waits = fill_copies()
                for w in waits:
                    w.wait_recv()
                for cp in copies:
                    cp.wait_send()

    col = lambda k: pl.BlockSpec((tg, HGRN_WIDTH), lambda g: (ng - 1 - g, k))
    logits = pl.BlockSpec((2, HGRN_WIDTH), lambda g: (0, 0))
    big = jax.ShapeDtypeStruct((T, HGRN_WIDTH), F32)
    n_in, n_out = 6, 4
    return pl.pallas_call(
        body, name="hgrn_bwd", grid=(ng,),
        in_specs=[logits, col(0), col(1), col(2),
                  pl.BlockSpec((HGRN_HEADS, ncg, HGRN_DIM, HGRN_DIM), lambda g: (0, ng - 1 - g, 0, 0)), col(0)] + [ANY] * nf,
        out_specs=[col(0), col(0), col(0), logits] + [ANY] * nf,
        out_shape=[big, big, big, jax.ShapeDtypeStruct((2, HGRN_WIDTH), F32)]
        + [jax.ShapeDtypeStruct(f.shape, f.dtype) for f in fill],
        input_output_aliases={n_in + k: n_out + k for k in range(nf)},
        scratch_shapes=[pltpu.VMEM((HGRN_HEADS, HGRN_DIM, HGRN_DIM), F32), pltpu.VMEM((1, HGRN_WIDTH), F32)]
        + ([pltpu.SemaphoreType.DMA((nf,)), pltpu.SemaphoreType.DMA((nf,))] if nf else []),
        compiler_params=_params(("arbitrary",)),
    )(lbl, xph, xph, xph, states, d_o, *fill)


def _proj_fwd(x, o_raw, oh_raw, xph, wout, w_mla, w_hg, w_post, w_fpre, tt=512):
    T = x.shape[0]

    def body(x_ref, o_ref, oh_ref, hg_ref, wout_ref, wmla_ref, whg_ref, wpost_ref, wfpre_ref,
             h1_ref, y1_ref, z_ref, mix_ref):
        om, _, _ = _grms_fwd(o_ref[...], wmla_ref[...], MLA_V)
        hg = hg_ref[...]
        ohn, _, _ = _grms_fwd(oh_ref[...], whg_ref[...], HGRN_DIM)
        mix = jnp.concatenate([om, ohn * (hg * jax.nn.sigmoid(hg))], axis=-1)
        mix_ref[...] = mix.astype(mix_ref.dtype)
        y1 = _mm(mix, wout_ref[...])
        y1_ref[...] = y1
        h1 = x_ref[...] + _rms_fwd(y1, wpost_ref[...])[0]
        h1_ref[...] = h1
        z_ref[...] = _rms_fwd(h1, wfpre_ref[...])[0].astype(z_ref.dtype)

    row = lambda w: pl.BlockSpec((tt, w), lambda i: (i, 0))
    full = lambda a: pl.BlockSpec(a.shape, lambda i: (0,) * a.ndim)
    sds = jax.ShapeDtypeStruct
    return pl.pallas_call(
        body, name="proj_fwd", grid=(T // tt,),
        in_specs=[row(D_MODEL), row(MLA_WIDTH), row(HGRN_WIDTH), pl.BlockSpec((tt, HGRN_WIDTH), lambda i: (i, 3)),
                  full(wout), full(w_mla), full(w_hg), full(w_post), full(w_fpre)],
        out_specs=[row(D_MODEL)] * 4,
        out_shape=[sds((T, D_MODEL), F32), sds((T, D_MODEL), F32), sds((T, D_MODEL), MXU_DTYPE),
                   sds((T, D_MODEL), MXU_DTYPE)],
        compiler_params=_params(("arbitrary",)),
    )(x, o_raw, oh_raw, xph, wout, w_mla, w_hg, w_post, w_fpre)


def _ffn_fwd(zb, h1, tgt, w_fpost, wg, wu, wd, tt=256):
    T = zb.shape[0]
    nj = N_CHIPS

    def body(z_ref, h1_ref, tgt_ref, wfpost_ref, wg_ref, wu_ref, wd_ref, g_ref, up_ref, dy2_ref, dh2_ref, loss_ref, dwf_ref):
        @pl.when(pl.program_id(0) == 0)
        def _():
            loss_ref[...] = jnp.zeros_like(loss_ref)
            dwf_ref[...] = jnp.zeros_like(dwf_ref)

        z = z_ref[...]
        gs = [_mm_nt(z, wg_ref[j]) for j in range(nj)]
        ups = [_mm_nt(z, wu_ref[j]) for j in range(nj)]
        y2 = jnp.zeros((tt, D_MODEL), F32)
        for j in range(nj):
            g_ref[j] = gs[j]
            up_ref[j] = ups[j]
            y2 = y2 + _mm(gs[j] * jax.nn.sigmoid(gs[j]) * ups[j], wd_ref[j])
        w = wfpost_ref[...]
        y2s, y2n, r2 = _rms_fwd(y2, w)
        e = h1_ref[...] + y2s - tgt_ref[...]
        loss_ref[...] += jnp.sum(e * e, axis=0, keepdims=True)
        dh2 = e * (1.0 / D_MODEL)
        dh2_ref[...] = dh2
        dy2, dwf = _rms_bwd(dh2, y2n, r2, w)
        dy2_ref[...] = dy2.astype(dy2_ref.dtype)
        dwf_ref[...] += dwf

    row = pl.BlockSpec((tt, D_MODEL), lambda i: (i, 0))
    vec = pl.BlockSpec((1, D_MODEL), lambda i: (0, 0))
    resident = pl.BlockSpec((nj, FF_SHARD, D_MODEL), lambda i: (0, 0, 0), pipeline_mode=pl.Buffered(1))
    act = pl.BlockSpec((nj, tt, FF_SHARD), lambda i: (0, i, 0))
    sds = jax.ShapeDtypeStruct
    return pl.pallas_call(
        body, name="ffn_fwd", grid=(T // tt,),
        in_specs=[row, row, row, vec, resident, resident, resident],
        out_specs=[act, act, row, row, vec, vec],
        out_shape=[sds((nj, T, FF_SHARD), F32), sds((nj, T, FF_SHARD), F32), sds((T, D_MODEL), MXU_DTYPE),
                   sds((T, D_MODEL), F32), sds((1, D_MODEL), F32), sds((1, D_MODEL), F32)],
        compiler_params=_params(("arbitrary",)),
    )(zb, h1, tgt, w_fpost, wg, wu, wd)


def _ffn_bwd(zb, g, up, dy2b, wg, wu, wd, tt=512):
    T = zb.shape[0]
    nj = N_CHIPS

    def body(z_ref, g_ref, up_ref, dy2_ref, wg_ref, wu_ref, wd_ref, dwg_ref, dwu_ref, dwd_ref, dz_ref, acc_ref):
        j, i = pl.program_id(0), pl.program_id(1)
        rows = pl.ds(pl.multiple_of(i * tt, tt), tt)

        @pl.when(i == 0)
        def _():
            dwg_ref[...] = jnp.zeros_like(dwg_ref)
            dwu_ref[...] = jnp.zeros_like(dwu_ref)
            dwd_ref[...] = jnp.zeros_like(dwd_ref)

        z, g_, up_, dy2 = z_ref[...], g_ref[0], up_ref[0], dy2_ref[...]
        sg = jax.nn.sigmoid(g_)
        act = g_ * sg
        dff = _mm_nt(dy2, wd_ref[0])
        dwd_ref[0] += _mm_tn(act * up_, dy2)
        dg = dff * up_ * sg * (1.0 + g_ * (1.0 - sg))
        dup = dff * act
        dwg_ref[0] += _mm_tn(dg, z)
        dwu_ref[0] += _mm_tn(dup, z)
        dz = _mm(dg, wg_ref[0]) + _mm(dup, wu_ref[0])

        @pl.when(j == 0)
        def _():
            acc_ref[rows, :] = dz

        @pl.when((j > 0) & (j < nj - 1))
        def _():
            acc_ref[rows, :] += dz

        @pl.when(j == nj - 1)
        def _():
            dz_ref[...] = acc_ref[rows, :] + dz

    row = pl.BlockSpec((tt, D_MODEL), lambda j, i: (i, 0))
    act = pl.BlockSpec((1, tt, FF_SHARD), lambda j, i: (j, i, 0))
    w_sh = pl.BlockSpec((1, FF_SHARD, D_MODEL), lambda j, i: (j, 0, 0))
    w_grad = jax.ShapeDtypeStruct((nj, FF_SHARD, D_MODEL), F32)
    return pl.pallas_call(
        body, name="ffn_bwd", grid=(nj, T // tt),
        in_specs=[row, act, act, row, w_sh, w_sh, w_sh],
        out_specs=[w_sh, w_sh, w_sh, pl.BlockSpec((tt, D_MODEL), lambda j, i: (jnp.where(j == nj - 1, i, 0), 0))],
        out_shape=[w_grad, w_grad, w_grad, jax.ShapeDtypeStruct((T, D_MODEL), F32)],
        scratch_shapes=[pltpu.VMEM((T, D_MODEL), F32)],
        compiler_params=_params(("arbitrary", "arbitrary"), vmem=FFN_BWD_VMEM),
    )(zb, g, up, dy2b, wg, wu, wd)


def _mid_bwd(dz, dh2, h1, y1, mixb, o_raw, oh_raw, xph, wout, w_fpre, w_post, w_mla, w_hg, swap=(), tt=256):
    T = dh2.shape[0]
    nsw = len(swap)
    n_in, n_out = 13, 10

    def body(*refs):
        (dz_ref, dh2_ref, h1_ref, y1_ref, mix_ref, o_ref, oh_ref, hg_ref, wout_ref, wfpre_ref, wpost_ref,
         wmla_ref, whg_ref) = refs[:n_in]
        (dh1_ref, dwout_ref, do_ref, doh_ref, dhg_ref, dvec_ref, dwfpre_ref, dwpost_ref, dwmla_ref,
         dwhg_ref) = refs[n_in + nsw:n_in + nsw + n_out]
        swap_copies = lambda: _pair_swap_copies(refs[n_in:n_in + nsw], refs[n_in + nsw + n_out:n_in + 2 * nsw + n_out],
                                                *refs[n_in + 2 * nsw + n_out:])

        @pl.when(pl.program_id(0) == 0)
        def _():
            for r in (dwout_ref, dwfpre_ref, dwpost_ref, dwmla_ref, dwhg_ref):
                r[...] = jnp.zeros_like(r)
            for cp in (swap_copies() if nsw else ()):
                cp.start()

        dz = dz_ref[...]
        wfpre = wfpre_ref[...]
        _, h1n, r = _rms_fwd(h1_ref[...], wfpre)
        dh1_z, dwfpre = _rms_bwd(dz, h1n, r, wfpre)
        dwfpre_ref[...] += dwfpre
        dh1 = dh2_ref[...] + dh1_z
        dh1_ref[...] = dh1
        wpost = wpost_ref[...]
        _, y1n, r1 = _rms_fwd(y1_ref[...], wpost)
        dy1, dwpost = _rms_bwd(dh1, y1n, r1, wpost)
        dwpost_ref[...] += dwpost
        dmix = _mm_nt(dy1, wout_ref[...])
        dwout_ref[...] += _mm_tn(mix_ref[...], dy1)
        wmla = wmla_ref[...]
        o = o_ref[...]
        _, on, ro = _grms_fwd(o, wmla, MLA_V)
        d_o, dwmla = _grms_bwd(dmix[:, :MLA_WIDTH], on, ro, wmla, MLA_V)
        dwmla_ref[...] += dwmla
        do_ref[...] = d_o.astype(do_ref.dtype)
        hh = lax.broadcasted_iota(jnp.int32, (MLA_HEADS, MLA_WIDTH), 0)
        ll = lax.broadcasted_iota(jnp.int32, (MLA_HEADS, MLA_WIDTH), 1)
        sel = jnp.where((ll >= hh * MLA_V) & (ll < (hh + 1) * MLA_V), 1.0, 0.0)
        dvec_ref[...] = _mm_nt(sel, d_o * o, True)
        whg = whg_ref[...]
        hg = hg_ref[...]
        sg = jax.nn.sigmoid(hg)
        _, ohn, rh = _grms_fwd(oh_ref[...], whg, HGRN_DIM)
        dmh = dmix[:, MLA_WIDTH:]
        dhg_ref[...] = dmh * ohn * whg * sg * (1.0 + hg * (1.0 - sg))
        d_oh, dwhg = _grms_bwd(dmh * (hg * sg), ohn, rh, whg, HGRN_DIM)
        dwhg_ref[...] += dwhg
        doh_ref[...] = d_oh

        if nsw:
            @pl.when(pl.program_id(0) == T // tt - 1)
            def _():
                for cp in swap_copies():
                    cp.wait()

    row = lambda w: pl.BlockSpec((tt, w), lambda i: (i, 0))
    full = lambda a: pl.BlockSpec(a.shape, lambda i: (0,) * a.ndim)
    vec = lambda w: pl.BlockSpec((1, w), lambda i: (0, 0))
    sds = jax.ShapeDtypeStruct
    return pl.pallas_call(
        body, name="mid_bwd", grid=(T // tt,),
        in_specs=[row(D_MODEL), row(D_MODEL), row(D_MODEL), row(D_MODEL),
                  row(D_MODEL), row(MLA_WIDTH), row(HGRN_WIDTH), pl.BlockSpec((tt, HGRN_WIDTH), lambda i: (i, 3)),
                  full(wout), vec(D_MODEL), vec(D_MODEL), vec(MLA_WIDTH), vec(HGRN_WIDTH)] + [ANY] * nsw,
        out_specs=[row(D_MODEL), full(wout), row(MLA_WIDTH), row(HGRN_WIDTH), row(HGRN_WIDTH),
                   pl.BlockSpec((MLA_HEADS, tt), lambda i: (0, i)),
                   vec(D_MODEL), vec(D_MODEL), vec(MLA_WIDTH), vec(HGRN_WIDTH)] + [ANY] * nsw,
        out_shape=[sds((T, D_MODEL), F32), sds(wout.shape, F32), sds((T, MLA_WIDTH), MXU_DTYPE), sds((T, HGRN_WIDTH), F32),
                   sds((T, HGRN_WIDTH), F32), sds((MLA_HEADS, T), F32),
                   sds((1, D_MODEL), F32), sds((1, D_MODEL), F32), sds((1, MLA_WIDTH), F32), sds((1, HGRN_WIDTH), F32)]
        + _half_stack_shapes(swap),
        scratch_shapes=[pltpu.SemaphoreType.DMA((nsw,)), pltpu.SemaphoreType.DMA((nsw,))] if nsw else [],
        compiler_params=_params(("arbitrary",)),
    )(dz, dh2, h1, y1, mixb, o_raw, oh_raw, xph, wout, w_fpre, w_post, w_mla, w_hg, *swap)


def _in_bwd(x, dh1, cq, ckv, dq, dk, dv, dhq, dhf, dhi, dhg, rc, rs, w_pre, win, qnw, wq, kvnw, wk, wv, tt=256):
    T = x.shape[0]

    def body(x_ref, dh1_ref, cq_ref, ckv_ref, dq_ref, dk_ref, dv_ref, dhq_ref, dhf_ref, dhi_ref, dhg_ref, rc_ref, rs_ref,
             wpre_ref, win_ref, qnw_ref, wq_ref, kvnw_ref, wk_ref, wv_ref,
             dx_ref, dwin_ref, dwq_ref, dwk_ref, dwv_ref, dwpre_ref, dqnw_ref, dkvnw_ref):
        @pl.when(pl.program_id(0) == 0)
        def _():
            for r in (dwin_ref, dwq_ref, dwk_ref, dwv_ref, dwpre_ref, dqnw_ref, dkvnw_ref):
                r[...] = jnp.zeros_like(r)

        def add_win_grad(r, first):
            for arr0, n, chip, row0 in _win_grad_segments():
                if first <= arr0 and arr0 + n <= first + r.shape[0]:
                    dwin_ref[chip, row0:row0 + n, :] += r[arr0 - first:arr0 - first + n]

        lo = Q_RANK + KV_RANK + HEAD_PAD
        dxp_h = jnp.concatenate([dhq_ref[...], dhf_ref[...], dhi_ref[...], dhg_ref[...]], axis=-1)
        du = _mm(dxp_h, win_ref[lo:, :])
        wpre = wpre_ref[...]
        u, xn, rx = _rms_fwd(x_ref[...], wpre)
        add_win_grad(_mm_tn(dxp_h, u), lo)
        c, sa, sb = _rope_tables(rc_ref[...], rs_ref[...])
        lane = lax.broadcasted_iota(jnp.int32, (tt, HEAD_PAD), 1)
        dk_all = dk_ref[...]
        dq_lin = []
        dkr = jnp.zeros((tt, HEAD_PAD), F32)
        for h in range(MLA_HEADS):
            sl = slice(HEAD_PAD * h, HEAD_PAD * (h + 1))
            dq_lin.append(_rope_bwd(dq_ref[sl, :].T * ATTN_SCALE, c, sa, sb))
            dkr = dkr + dk_all[:, sl]
        dq_lin = jnp.concatenate(dq_lin, axis=-1)
        dkr = jnp.where((lane >= MLA_NOPE) & (lane < MLA_QK), _rope_bwd(dkr, c, sa, sb), 0.0)
        qnw = qnw_ref[...]
        qn, cqn, rq = _rms_fwd(cq_ref[...], qnw)
        dwq_ref[...] += _mm_tn(qn, dq_lin)
        dcq, dqnw = _rms_bwd(_mm_nt(dq_lin, wq_ref[...]), cqn, rq, qnw)
        dqnw_ref[...] += dqnw
        kvnw = kvnw_ref[...]
        kvn, ckvn, rkv = _rms_fwd(ckv_ref[...], kvnw)
        dv_ = dv_ref[...]
        dwk_ref[...] += _mm_tn(kvn, dk_all)
        dwv_ref[...] += _mm_tn(kvn, dv_)
        dckv, dkvnw = _rms_bwd(_mm_nt(dk_all, wk_ref[...]) + _mm_nt(dv_, wv_ref[...]), ckvn, rkv, kvnw)
        dkvnw_ref[...] += dkvnw
        dxp_a = jnp.concatenate([dcq, dckv, dkr], axis=-1)
        add_win_grad(_mm_tn(dxp_a, u), 0)
        dx_u, dwpre = _rms_bwd(du + _mm(dxp_a, win_ref[:lo, :]), xn, rx, wpre)
        dwpre_ref[...] += dwpre
        dx_ref[...] = dh1_ref[...] + dx_u

    row = lambda w: pl.BlockSpec((tt, w), lambda i: (i, 0))
    full = lambda a: pl.BlockSpec(a.shape, lambda i: (0,) * a.ndim)
    sds = jax.ShapeDtypeStruct
    qk_w = MLA_HEADS * HEAD_PAD
    return pl.pallas_call(
        body, name="in_bwd", grid=(T // tt,),
        in_specs=[row(D_MODEL), row(D_MODEL), row(Q_RANK), row(KV_RANK), pl.BlockSpec((qk_w, tt), lambda i: (0, i)),
                  row(qk_w), row(MLA_WIDTH),
                  row(HGRN_WIDTH), row(HGRN_WIDTH), row(HGRN_WIDTH), row(HGRN_WIDTH), row(HEAD_PAD), row(HEAD_PAD),
                  full(w_pre), full(win), full(qnw), full(wq), full(kvnw), full(wk), full(wv)],
        out_specs=[row(D_MODEL), pl.BlockSpec(WIN_COMM_SHAPE, lambda i: (0, 0, 0)), full(wq), full(wk), full(wv),
                   full(w_pre), full(qnw), full(kvnw)],
        out_shape=[sds((T, D_MODEL), F32), sds(WIN_COMM_SHAPE, F32), sds(wq.shape, F32), sds(wk.shape, F32),
                   sds(wv.shape, F32), sds(w_pre.shape, F32), sds(qnw.shape, F32), sds(kvnw.shape, F32)],
        compiler_params=_params(("arbitrary",)),
    )(x, dh1, cq, ckv, dq, dk, dv, dhq, dhf, dhi, dhg, rc, rs, w_pre, win, qnw, wq, kvnw, wk, wv)


def _arrange_weights(win_t, wuq_full, wukv):
    dt = win_t.dtype
    z = lambda n: jnp.zeros((n, D_MODEL), dt)
    s2 = Q_RANK + KV_RANK
    win_arr = jnp.concatenate([win_t[:s2], z(MLA_NOPE), win_t[s2:s2 + MLA_ROPE], z(HEAD_PAD - MLA_QK),
                               win_t[s2 + MLA_ROPE:]], axis=0)
    wq_arr = jnp.pad(wuq_full, ((0, 0), (0, 0), (0, HEAD_PAD - MLA_QK))).reshape(Q_RANK, MLA_HEADS * HEAD_PAD)
    wk_arr = jnp.pad(wukv[:, :, :MLA_NOPE], ((0, 0), (0, 0), (0, HEAD_PAD - MLA_NOPE))).reshape(
        KV_RANK, MLA_HEADS * HEAD_PAD)
    wv_arr = wukv[:, :, MLA_NOPE:].reshape(KV_RANK, MLA_WIDTH)
    return win_arr, wq_arr, wk_arr, wv_arr


WIN_COMM_SHAPE = (N_CHIPS, -(-D_IN // N_CHIPS // 32) * 32, D_MODEL)


def _win_grad_segments():
    s2 = Q_RANK + KV_RANK
    runs = [(0, s2, 0), (s2, s2 + MLA_ROPE, MLA_NOPE), (s2 + MLA_ROPE, D_IN, HEAD_PAD - MLA_ROPE)]
    per = D_IN // N_CHIPS
    segs = []
    for lo, hi, shift in runs:
        for k in range(N_CHIPS):
            a, b = max(lo, per * k), min(hi, per * (k + 1))
            if a < b:
                segs.append((a + shift, b - a, k, a - per * k))
    return segs


def _unarrange_grads(dwq_arr, dwk_arr, dwv_arr):
    dwuq = dwq_arr.reshape(Q_RANK, MLA_HEADS, HEAD_PAD)[:, :, :MLA_QK]
    dwukv = jnp.concatenate([dwk_arr.reshape(KV_RANK, MLA_HEADS, HEAD_PAD)[:, :, :MLA_NOPE],
                             dwv_arr.reshape(KV_RANK, MLA_HEADS, MLA_V)], axis=-1)
    return dwuq, dwukv


def _rope_inv_freq():
    inv = 1.0 / (ROPE_THETA ** (jnp.arange(0, MLA_ROPE, 2, dtype=F32) / MLA_ROPE))
    z = lambda n: jnp.zeros((n,), F32)
    return jnp.concatenate([z(MLA_NOPE), inv, inv, z(HEAD_PAD - MLA_QK)]).reshape(1, HEAD_PAD)


def _local_step(x, pos, tgt, small, win_arr, wq_arr, wk_arr, wv_arr, late, place=None):
    invf = _rope_inv_freq()
    cq, ckv, xph, qb, kb, vb, kt, vt, rc, rs = _in_fwd(x, pos, invf, small["attn_pre_norm"], win_arr, small["mla_q_norm"],
                                               wq_arr, small["mla_kv_norm"], wk_arr, wv_arr)
    if place is None:
        o_raw, lse = _attn_fwd_t(qb, kb, vt)
        wout, wg, wu, wd = late
    else:
        o_raw, lse, *stacks = _attn_fwd_t(qb, kb, vt, gather=late)
        wout, wg, wu, wd = [lax.dynamic_update_slice(s, l[None], (place[1], 0, 0)) for s, l in zip(stacks, late)]
        wout = wout.reshape(D_MODEL, D_MODEL)
    oh_raw, states = _hgrn_fwd(xph, small["hgrn_lb_logits"])
    h1, y1, zb, mixb = _proj_fwd(x, o_raw, oh_raw, xph, wout, small["mla_out_norm"], small["hgrn_out_norm"],
                                 small["attn_post_norm"], small["ffn_pre_norm"])
    g, up, dy2b, dh2, loss_acc, d_fpost = _ffn_fwd(zb, h1, tgt, small["ffn_post_norm"], wg, wu, wd)
    dwg, dwu, dwd, dz = _ffn_bwd(zb, g, up, dy2b, wg, wu, wd)
    ffn_grads = [] if place is None else [dwg, dwu, dwd]
    dh1, dwout, d_o, d_oh, dhg, dvec, d_fpre, d_post, d_mla, d_hg, *ffn_rs = _mid_bwd(
        dz, dh2, h1, y1, mixb, o_raw, oh_raw, xph, wout, small["ffn_pre_norm"], small["attn_post_norm"],
        small["mla_out_norm"], small["hgrn_out_norm"], swap=ffn_grads)
    if ffn_grads:
        ffn_grads = ffn_grads + [dwout.reshape(N_CHIPS, D_MODEL // N_CHIPS, D_MODEL)]
        ffn_rs += _pair_swap(ffn_grads[3:], (), "pair_swap_w_out")
    ffn_ps = _pair_sum(place, ffn_grads, ffn_rs, name="pair_sum_ffn") if ffn_grads else []
    dq, dk, dv, *ffn_ris = _attn_bwd_t(qb, kb, kt, vb, d_o, lse, dvec.reshape(lse.shape), send=ffn_ps)
    ffn_sums = _chip_sum(place, ffn_grads, ffn_rs, ffn_ris, name="chip_sum_ffn") if ffn_grads else []
    dhq, dhf, dhi, d_lbl, *ffn_final = _hgrn_bwd(xph, small["hgrn_lb_logits"], states, d_oh, fill=ffn_sums)
    dx, dwin4, dwq_arr, dwk_arr, dwv_arr, d_pre, d_qn, d_kvn = _in_bwd(
        x, dh1, cq, ckv, dq, dk, dv, dhq, dhf, dhi, dhg, rc, rs, small["attn_pre_norm"], win_arr,
        small["mla_q_norm"], wq_arr, small["mla_kv_norm"], wk_arr, wv_arr)
    dwuq, dwukv = _unarrange_grads(dwq_arr, dwk_arr, dwv_arr)
    loss = 0.5 * jnp.sum(loss_acc) * (1.0 / D_MODEL)
    grads = dict(attn_pre_norm=d_pre, w_in=dwin4, mla_q_norm=d_qn, mla_w_uq=dwuq, mla_kv_norm=d_kvn, mla_w_ukv=dwukv,
                 mla_out_norm=d_mla, hgrn_lb_logits=d_lbl, hgrn_out_norm=d_hg, w_out=dwout, attn_post_norm=d_post,
                 ffn_pre_norm=d_fpre, w_gate=dwg, w_up=dwu, w_down=dwd, ffn_post_norm=d_fpost)
    if place is None:
        return loss, dx, grads
    return loss, dx, grads, ffn_final


def _place():
    x, y, c = lax.axis_index("x"), lax.axis_index("y"), lax.axis_index("c")
    others = [(1 - x, y), (x, 1 - y), (1 - x, 1 - y)]
    return x, y, c, 2 * x + y, (x, y, 1 - c), others


def _half(ref, c, rows):
    return ref.at[pl.ds(pl.multiple_of(c * rows, 8), rows)]


def _rcopy(src, dst, send, recv, k, to):
    return pltpu.make_async_remote_copy(src_ref=src, dst_ref=dst, send_sem=send.at[k], recv_sem=recv.at[k],
                                        device_id=to, device_id_type=MESH)


class _Gather:
    def __init__(self, ins, outs, send, recv):
        self.ins, self.outs, self.send, self.recv = ins, outs, send, recv
        self.n = len(ins)
        self.halves = [r.shape[0] // 2 for r in ins]
        _, _, self.c, self.me, self.sib, self.others = _place()

    def _each(self):
        for j, (px, py) in enumerate(self.others):
            for a in range(self.n):
                yield j * self.n + a, a, 2 * px + py, (px, py, self.c)

    def sends(self):
        return [_rcopy(_half(self.ins[a], self.c, self.halves[a]), _half(self.outs[a].at[self.me], self.c, self.halves[a]),
                       self.send, self.recv, k, to) for k, a, _, to in self._each()]

    def arrivals(self):
        parts = [(k, _half(self.outs[a].at[chip], self.c, self.halves[a]), to) for k, a, chip, to in self._each()]
        return [_rcopy(p, p, self.send, self.recv, k, to) for k, p, to in parts]

    def forwards(self):
        parts = [(k, _half(self.outs[a].at[chip], self.c, self.halves[a])) for k, a, chip, _ in self._each()]
        return [_rcopy(p, p, self.send, self.recv, 3 * self.n + k, self.sib) for k, p in parts]

    def forward_arrivals(self):
        parts = [(k, _half(self.outs[a].at[chip], 1 - self.c, self.halves[a])) for k, a, chip, _ in self._each()]
        return [_rcopy(p, p, self.send, self.recv, 3 * self.n + k, self.sib) for k, p in parts]

    @staticmethod
    def out_shapes(arrs):
        return [jax.ShapeDtypeStruct((N_CHIPS,) + a.shape, a.dtype) for a in arrs]

    @staticmethod
    def semaphores(arrs):
        return [pltpu.SemaphoreType.DMA((6 * len(arrs),)), pltpu.SemaphoreType.DMA((6 * len(arrs),))]


def _gather_chips(arrs, name):
    n = len(arrs)

    def body(*refs):
        gat = _Gather(refs[:n], refs[n:2 * n], *refs[2 * n:])
        sends, forwards = gat.sends(), gat.forwards()
        for cp in sends:
            cp.start()
        for arrival, fw in zip(gat.arrivals(), forwards):
            arrival.wait_recv()
            fw.start()
        for arrival in gat.forward_arrivals():
            arrival.wait_recv()
        for cp in sends + forwards:
            cp.wait_send()

    return pl.pallas_call(body, name=name, in_specs=[ANY] * n, out_specs=[ANY] * n, out_shape=_Gather.out_shapes(arrs),
                          scratch_shapes=_Gather.semaphores(arrs))(*arrs)


def _grad_blocks(gs):
    return max(n for n in (1, 2, 3, 4) if all(g.shape[1] // 2 % (16 * n) == 0 for g in gs))


def _pair_swap_copies(g_refs, r_refs, send, recv):
    _, _, c, _, sib, _ = _place()
    copies = []
    for a, (g, r) in enumerate(zip(g_refs, r_refs)):
        h = g.shape[1] // 2
        copies.append(_rcopy(g.at[:, pl.ds(pl.multiple_of((1 - c) * h, 8), h)], r, send, recv, a, sib))
    return copies


def _half_stack_shapes(gs, dtype=None):
    return [jax.ShapeDtypeStruct((N_CHIPS, g.shape[1] // 2, g.shape[2]), dtype or g.dtype) for g in gs]


def _pair_swap(gs, wholes, name):
    n, nw = len(gs), len(wholes)

    def body(*refs):
        ins, outs, (send, recv) = refs[:n + nw], refs[n + nw:2 * (n + nw)], refs[2 * (n + nw):]
        copies = _pair_swap_copies(ins[:n], outs[:n], send, recv)
        copies += [_rcopy(ins[n + k], outs[n + k], send, recv, n + k, _place()[4]) for k in range(nw)]
        for cp in copies:
            cp.start()
        for cp in copies:
            cp.wait()

    return pl.pallas_call(
        body, name=name, in_specs=[ANY] * (n + nw), out_specs=[ANY] * (n + nw),
        out_shape=_half_stack_shapes(gs) + [jax.ShapeDtypeStruct(w.shape, w.dtype) for w in wholes],
        scratch_shapes=[pltpu.SemaphoreType.DMA((n + nw,)), pltpu.SemaphoreType.DMA((n + nw,))],
    )(*gs, *wholes)


def _pair_sum(place, gs, rs, small=None, name="pair_sum"):
    n = len(gs)
    nb = _grad_blocks(gs)

    def body(place_ref, *refs):
        g_refs, r_refs, p_refs = refs[:n], refs[n:2 * n], refs[-n - 1:-1] if small else refs[-n:]
        for a in range(n):
            p_refs[a][0] = (g_refs[a][0] + r_refs[a][0]).astype(p_refs[a].dtype)
        if small:
            @pl.when((pl.program_id(0) == 0) & (pl.program_id(1) == 0))
            def _():
                refs[-1][...] = refs[2 * n][...] + refs[2 * n + 1][...]

    in_specs, out_specs = [], []
    for g in gs:
        blk = (1, g.shape[1] // 2 // nb, g.shape[2])
        in_specs.append(pl.BlockSpec(blk, lambda i, k, p: (k, p[0] * nb + i, 0)))
    for g in gs:
        blk = (1, g.shape[1] // 2 // nb, g.shape[2])
        in_specs.append(pl.BlockSpec(blk, lambda i, k, p: (k, i, 0)))
        out_specs.append(pl.BlockSpec(blk, lambda i, k, p: (k, i, 0)))
    out_shape = _half_stack_shapes(gs, BF16)
    if small:
        sm_spec = pl.BlockSpec(small[0].shape, lambda i, k, p: (0, 0))
        in_specs += [sm_spec, sm_spec]
        out_specs.append(sm_spec)
        out_shape.append(jax.ShapeDtypeStruct(small[0].shape, F32))
    return pl.pallas_call(
        body, name=name,
        grid_spec=pltpu.PrefetchScalarGridSpec(num_scalar_prefetch=1, grid=(nb, N_CHIPS), in_specs=in_specs,
                                               out_specs=out_specs),
        out_shape=out_shape,
        compiler_params=_params(("arbitrary", "arbitrary")),
    )(place, *gs, *rs, *(small or ()))


def _chip_swap_copies(p_refs, ri_refs, send, recv):
    _, _, c, _, _, others = _place()
    n = len(p_refs)
    return [_rcopy(p_refs[a].at[2 * px + py], ri_refs[a].at[j], send, recv, j * n + a, (px, py, c))
            for j, (px, py) in enumerate(others) for a in range(n)]


def _chip_swap_shapes(ps):
    return [jax.ShapeDtypeStruct((3,) + p.shape[1:], p.dtype) for p in ps]


def _chip_swap(ps, pair):
    n = len(ps)

    def body(*refs):
        start, finish = _chip_swap_plan(refs[:n], refs[n], refs[n + 1:2 * n + 1], refs[2 * n + 1], *refs[2 * n + 2:])
        start()
        finish()

    return pl.pallas_call(
        body, name="chip_swap", in_specs=[ANY] * (n + 1), out_specs=[ANY] * (n + 1),
        out_shape=_chip_swap_out_shapes(ps, pair), scratch_shapes=_chip_swap_semaphores(n),
    )(*ps, pair)


def _chip_swap_plan(p_refs, pair_ref, ri_refs, sm4_ref, send, recv, lsem):
    n = len(p_refs)
    hs = SMALL_ROWS // 2
    x, y, c, me, sib, others = _place()
    local = pltpu.make_async_copy(pair_ref, sm4_ref.at[me], lsem.at[0])
    copies = _chip_swap_copies(p_refs, ri_refs, send, recv)
    arrivals = list(copies)
    for j, (px, py) in enumerate(others):
        copies.append(_rcopy(_half(pair_ref, c, hs), _half(sm4_ref.at[me], c, hs), send, recv, 3 * n + j, (px, py, c)))
        part = _half(sm4_ref.at[2 * px + py], c, hs)
        arrivals.append(_rcopy(part, part, send, recv, 3 * n + j, (px, py, c)))

    def start():
        local.start()
        for cp in copies:
            cp.start()

    def finish():
        for arrival in arrivals:
            arrival.wait_recv()
        for cp in copies:
            cp.wait_send()
        local.wait()

    return start, finish


def _chip_swap_out_shapes(ps, pair):
    return _chip_swap_shapes(ps) + [jax.ShapeDtypeStruct((N_CHIPS,) + pair.shape, pair.dtype)]


def _chip_swap_semaphores(n):
    k = 3 * (n + 1)
    return [pltpu.SemaphoreType.DMA((k,)), pltpu.SemaphoreType.DMA((k,)), pltpu.SemaphoreType.DMA((1,))]


def _chip_sum(place, gs, rs, ris, name="chip_sum"):
    n = len(gs)
    nb = _grad_blocks(gs)

    def body(place_ref, *refs):
        g_refs, r_refs, ri_refs, o_refs = refs[:n], refs[n:2 * n], refs[2 * n:3 * n], refs[3 * n:]
        for a in range(n):
            ri = ri_refs[a]
            o_refs[a][...] = (g_refs[a][0] + r_refs[a][0]) + ri[0].astype(F32) + ri[1].astype(F32) + ri[2].astype(F32)

    in_specs, out_specs, out_shape = [], [], []
    for g in gs:
        blk = (1, g.shape[1] // 2 // nb, g.shape[2])
        in_specs.append(pl.BlockSpec(blk, lambda i, p: (p[1], p[0] * nb + i, 0)))
    for g in gs:
        blk = (1, g.shape[1] // 2 // nb, g.shape[2])
        in_specs.append(pl.BlockSpec(blk, lambda i, p: (p[1], i, 0)))
    for g in gs:
        rb = g.shape[1] // 2 // nb
        in_specs.append(pl.BlockSpec((3, rb, g.shape[2]), lambda i, p: (0, i, 0)))
        out_specs.append(pl.BlockSpec((rb, g.shape[2]), lambda i, p: (p[0] * nb + i, 0)))
        out_shape.append(jax.ShapeDtypeStruct(g.shape[1:], F32))
    return pl.pallas_call(
        body, name=name,
        grid_spec=pltpu.PrefetchScalarGridSpec(num_scalar_prefetch=1, grid=(nb,), in_specs=in_specs, out_specs=out_specs),
        out_shape=out_shape,
        compiler_params=_params(("arbitrary",)),
    )(place, *gs, *rs, *ris)


def _pair_fill_copies(g_refs, send, recv):
    _, _, c, _, sib, _ = _place()
    copies, waits = [], []
    for a, g in enumerate(g_refs):
        h = g.shape[0] // 2
        mine, theirs = _half(g, c, h), _half(g, 1 - c, h)
        copies.append(_rcopy(mine, mine, send, recv, a, sib))
        waits.append(_rcopy(theirs, theirs, send, recv, a, sib))
    return copies, waits


def _pair_fill(gfs, sm4):
    n = len(gfs)
    hs = SMALL_ROWS // 2

    def body(*refs):
        g_refs, sm4_ref = refs[n + 1:2 * n + 1], refs[2 * n + 1]
        send, recv = refs[2 * n + 2:]
        x, y, c, me, sib, others = _place()
        copies, waits = _pair_fill_copies(g_refs, send, recv)
        for j, (px, py) in enumerate(others):
            chip = 2 * px + py
            mine, theirs = _half(sm4_ref.at[chip], c, hs), _half(sm4_ref.at[chip], 1 - c, hs)
            copies.append(pltpu.make_async_remote_copy(src_ref=mine, dst_ref=mine, send_sem=send.at[n + j],
                                                       recv_sem=recv.at[n + j], device_id=sib, device_id_type=MESH))
            waits.append(pltpu.make_async_remote_copy(src_ref=theirs, dst_ref=theirs, send_sem=send.at[n + j],
                                                      recv_sem=recv.at[n + j], device_id=sib, device_id_type=MESH))
        for cp in copies:
            cp.start()
        for w in waits:
            w.wait_recv()
        for cp in copies:
            cp.wait_send()

    return pl.pallas_call(
        body, name="pair_fill", in_specs=[ANY] * (n + 1), out_specs=[ANY] * (n + 1),
        out_shape=[jax.ShapeDtypeStruct(g.shape, g.dtype) for g in gfs] + [jax.ShapeDtypeStruct(sm4.shape, sm4.dtype)],
        input_output_aliases={i: i for i in range(n + 1)},
        scratch_shapes=[pltpu.SemaphoreType.DMA((n + 3,)), pltpu.SemaphoreType.DMA((n + 3,))],
    )(*gfs, sm4)


def _adamw_math(w, g, m, v):
    m = ADAM_B1 * m + (1.0 - ADAM_B1) * g
    v = ADAM_B2 * v + (1.0 - ADAM_B2) * (g * g)
    m_hat = m / (1.0 - ADAM_B1 ** ADAM_STEP)
    v_hat = v / (1.0 - ADAM_B2 ** ADAM_STEP)
    return -ADAM_LR * (m_hat / (jnp.sqrt(v_hat) + ADAM_EPS) + ADAM_WD * w), m, v


def _adamw(items, steps, name):
    n = len(items)

    def body(*refs):
        for a in range(n):
            g = refs[4 * a + 1][...]
            d, mo, vo = _adamw_math(refs[4 * a][...], g, refs[4 * a + 2][...], refs[4 * a + 3][...])
            for out, val in zip(refs[4 * n + 4 * a:4 * n + 4 * a + 4], (g, d, mo, vo)):
                out[...] = val

    spec = lambda w: pl.BlockSpec((w.shape[0] // steps, w.shape[1]), lambda i: (i, 0))
    flat = pl.pallas_call(
        body, name=name, grid=(steps,), in_specs=[spec(it[0]) for it in items for _ in range(4)],
        out_specs=[spec(it[0]) for it in items for _ in range(4)],
        out_shape=[jax.ShapeDtypeStruct(it[0].shape, F32) for it in items for _ in range(4)],
        compiler_params=_params(("arbitrary",)),
    )(*[a for it in items for a in it])
    return [flat[4 * a:4 * a + 4] for a in range(n)]


def _adamw_small(sm4, wmv):
    views = SMALL_VIEWS[:-1]
    n = len(views)

    def body(sm4_ref, *refs):
        g_all = ((sm4_ref[0] + sm4_ref[1]) + sm4_ref[2]) + sm4_ref[3]
        for a, (name, rows, cols) in enumerate(views):
            row = SMALL_OFFSETS[name]
            g = g_all[row:row + rows, :cols]
            d, mo, vo = _adamw_math(refs[3 * a][...], g, refs[3 * a + 1][...], refs[3 * a + 2][...])
            for out, val in zip(refs[3 * n + 4 * a:3 * n + 4 * a + 4], (g, d, mo, vo)):
                out[...] = val
        row = SMALL_OFFSETS["loss"]
        refs[-1][...] = g_all[row:row + 1, :128]

    flat = pl.pallas_call(
        body, name="adamw_small",
        out_shape=[jax.ShapeDtypeStruct((rows, cols), F32) for _, rows, cols in views for _ in range(4)]
        + [jax.ShapeDtypeStruct((1, 128), F32)],
        compiler_params=pltpu.CompilerParams(vmem_limit_bytes=VMEM_LIMIT),
    )(sm4, *[a for t in wmv for a in t])
    return [flat[4 * a:4 * a + 4] for a in range(n)] + [flat[-1]]


SMALL_NAMES = ("attn_pre_norm", "mla_q_norm", "mla_kv_norm", "mla_w_ukv", "mla_out_norm", "hgrn_lb_logits",
               "hgrn_out_norm", "attn_post_norm", "ffn_pre_norm", "ffn_post_norm")
BIG_NAMES = ("w_in", "mla_w_uq", "w_out", "w_gate", "w_up", "w_down")
WEIGHT_NAMES = ("attn_pre_norm", "w_in", "mla_q_norm", "mla_w_uq", "mla_kv_norm", "mla_w_ukv", "mla_out_norm",
                "hgrn_lb_logits", "hgrn_out_norm", "w_out", "attn_post_norm", "ffn_pre_norm", "w_gate", "w_up", "w_down",
                "ffn_post_norm")


UQ_COMM_SHAPE = (192, 384)


def _pack_small(vals):
    parts, row = [], 0
    for name, rows, cols in sorted(SMALL_VIEWS, key=lambda view: SMALL_OFFSETS[view[0]]):
        assert SMALL_OFFSETS[name] == row
        parts.append(jnp.pad(vals[name].reshape(rows, cols), ((0, 0), (0, D_MODEL - cols))))
        row += rows
    parts.append(jnp.zeros((SMALL_ROWS - row, D_MODEL), F32))
    return jnp.concatenate(parts, axis=0)


def kernel(x, positions, attn_pre_norm, w_in, mla_q_norm, mla_w_uq, mla_kv_norm, mla_w_ukv, mla_out_norm, hgrn_lb_logits, hgrn_out_norm, w_out, attn_post_norm, ffn_pre_norm, w_gate, w_up, w_down, ffn_post_norm, loss_target, m_attn_pre_norm, m_w_in, m_mla_q_norm, m_mla_w_uq, m_mla_kv_norm, m_mla_w_ukv, m_mla_out_norm, m_hgrn_lb_logits, m_hgrn_out_norm, m_w_out, m_attn_post_norm, m_ffn_pre_norm, m_w_gate, m_w_up, m_w_down, m_ffn_post_norm, v_attn_pre_norm, v_w_in, v_mla_q_norm, v_mla_w_uq, v_mla_kv_norm, v_mla_w_ukv, v_mla_out_norm, v_hgrn_lb_logits, v_hgrn_out_norm, v_w_out, v_attn_post_norm, v_ffn_pre_norm, v_w_gate, v_w_up, v_w_down, v_ffn_post_norm):
    args = locals()
    W = {n: args[n] for n in WEIGHT_NAMES}
    M = {n: args["m_" + n] for n in WEIGHT_NAMES}
    V = {n: args["v_" + n] for n in WEIGHT_NAMES}
    T = x.shape[1]
    cx, cy, cc = lax.axis_index("x"), lax.axis_index("y"), lax.axis_index("c")

    win_rows = D_IN // N_CHIPS
    shard2d = {"w_in": (win_rows, D_MODEL), "mla_w_uq": (Q_RANK // N_CHIPS, MLA_HEADS * MLA_QK),
               "w_out": (D_MODEL // N_CHIPS, D_MODEL), "w_gate": (FF_SHARD, D_MODEL), "w_up": (FF_SHARD, D_MODEL),
               "w_down": (FF_SHARD, D_MODEL)}
    transposed = ("w_in", "w_gate", "w_up")
    to2d = lambda n, a: a[0].T if n in transposed else a.reshape(shard2d[n])
    from2d = lambda n, t: t.T[None] if n in transposed else t.reshape(W[n].shape)
    me = 2 * cx + cy
    place = jnp.stack([cc, me]).astype(jnp.int32)
    local_b = [to2d(n, W[n]).astype(BF16) for n in BIG_NAMES]
    local_b[0] = jnp.pad(local_b[0], ((0, WIN_COMM_SHAPE[1] - win_rows), (0, 0)))
    stacks = _gather_chips(local_b[:2], "gather_weights")
    win4, wuq4 = [lax.dynamic_update_slice(s, l[None], (me, 0, 0)) for s, l in zip(stacks, local_b)]
    win_t = win4[:, :win_rows].reshape(D_IN, D_MODEL)
    wuq_full = wuq4.reshape(Q_RANK, MLA_HEADS, MLA_QK)
    win_arr, wq_arr, wk_arr, wv_arr = _arrange_weights(win_t, wuq_full, mla_w_ukv[0].astype(BF16))
    small = {n: W[n][0] if n == "mla_w_ukv" else W[n].reshape(-1, W[n].shape[-1]) for n in SMALL_NAMES}

    loss_local, dx, grads, ffn_final = _local_step(x[0], positions.reshape(T, 1), loss_target[0], small, win_arr,
                                                           wq_arr, wk_arr, wv_arr, local_b[2:], place)

    gs = [grads["w_in"], grads["mla_w_uq"].reshape((N_CHIPS,) + UQ_COMM_SHAPE)]
    sm = _pack_small({**grads, "loss": loss_local})
    *rs, ssib = _pair_swap(gs, (sm,), "pair_swap")
    *ps, pair = _pair_sum(place, gs, rs, small=(sm, ssib))
    ffn_names, rest_names = BIG_NAMES[3:], BIG_NAMES[:2]
    g2d = dict(zip(ffn_names + BIG_NAMES[2:3], ffn_final))
    adam_in = lambda names_: [(to2d(n, W[n]), g2d[n], to2d(n, M[n]), to2d(n, V[n])) for n in names_]
    updates = dict(zip(ffn_names, _adamw(adam_in(ffn_names), 8, "adamw_ffn")))
    updates.update(zip(BIG_NAMES[2:3], _adamw(adam_in(BIG_NAMES[2:3]), 8, "adamw_w_out")))
    *ris, sm4 = _chip_swap(ps, pair)
    *gfin, smf = _pair_fill(_chip_sum(place, gs, rs, ris), sm4)

    g2d.update({n: gfin[k].reshape((-1,) + shard2d[n][1:]) for k, n in enumerate(rest_names)})
    updates.update(zip(rest_names, _adamw(adam_in(rest_names), 3, "adamw_w_in")))
    G, DW, NM, NV = {}, {}, {}, {}
    for n, outs in updates.items():
        G[n], DW[n], NM[n], NV[n] = (from2d(n, t) for t in outs)
    view2d = lambda n, a: a.reshape(next((r, c) for name, r, c in SMALL_VIEWS if name == n))
    *res, loss_row = _adamw_small(smf, [tuple(view2d(n, t[n]) for t in (W, M, V)) for n in SMALL_NAMES])
    for n, outs in zip(SMALL_NAMES, res):
        G[n], DW[n], NM[n], NV[n] = (t.reshape(W[n].shape) for t in outs)
    loss = loss_row[0, 0]
    return (loss, dx[None], *[G[n] for n in WEIGHT_NAMES], *[DW[n] for n in WEIGHT_NAMES],
            *[NM[n] for n in WEIGHT_NAMES], *[NV[n] for n in WEIGHT_NAMES])
```

```python
import jax
import jax.numpy as jnp
from jax import lax
from jax.experimental import pallas as pl
from jax.experimental.pallas import tpu as pltpu

F32 = jnp.float32
BF16 = jnp.bfloat16
MXU_DTYPE = BF16

D_MODEL = 1024
MLA_HEADS = 8
MLA_NOPE = 64
MLA_ROPE = 32
MLA_V = 64
MLA_QK = MLA_NOPE + MLA_ROPE
Q_RANK = 384
KV_RANK = 128
MLA_WIDTH = MLA_HEADS * MLA_V
HEAD_PAD = 128
HGRN_HEADS = 4
HGRN_DIM = 128
HGRN_WIDTH = HGRN_HEADS * HGRN_DIM
CHUNK = 64
SUB = 16
HGRN_CPI = 4
D_IN = Q_RANK + KV_RANK + MLA_ROPE + 4 * HGRN_WIDTH
D_IN_ARR = Q_RANK + KV_RANK + HEAD_PAD + 4 * HGRN_WIDTH
D_FF = 2816
N_CHIPS = 4
FF_SHARD = D_FF // N_CHIPS
EPS = 1e-6
ROPE_THETA = 10000.0
ATTN_SCALE = MLA_QK ** -0.5
ATTN_SCALE_LOG2 = ATTN_SCALE * 1.4426950408889634
NEG_BIG = -1e30

ADAM_LR = 0.001
ADAM_B1 = 0.9
ADAM_B2 = 0.999
ADAM_EPS = 1e-08
ADAM_WD = 0.01
ADAM_STEP = 10

VMEM_LIMIT = 56 * 1024 * 1024
FFN_BWD_VMEM = 62 * 1024 * 1024

SMALL_VIEWS = (("attn_pre_norm", 1, 1024), ("mla_q_norm", 1, 384), ("mla_kv_norm", 1, 128), ("mla_w_ukv", 128, 1024),
               ("mla_out_norm", 1, 512), ("hgrn_lb_logits", 2, 512), ("hgrn_out_norm", 1, 512),
               ("attn_post_norm", 1, 1024), ("ffn_pre_norm", 1, 1024), ("ffn_post_norm", 1, 1024), ("loss", 1, 1))
ROW_TILE = 8


def _small_layout():
    offsets, row = {}, 0
    for whole in (True, False):
        for name, rows, _ in SMALL_VIEWS:
            if (rows % ROW_TILE == 0) == whole:
                offsets[name] = row
                row += rows
    return offsets, -(-row // (2 * ROW_TILE)) * 2 * ROW_TILE


SMALL_OFFSETS, SMALL_ROWS = _small_layout()

MESH = pl.DeviceIdType.MESH
ANY = pl.BlockSpec(memory_space=pl.ANY)


def _dot(a, b, dims, exact):
    if exact:
        return lax.dot_general(a.astype(F32), b.astype(F32), (dims, ((), ())), precision=lax.Precision.HIGH,
                               preferred_element_type=F32)
    return lax.dot_general(a.astype(MXU_DTYPE), b.astype(MXU_DTYPE), (dims, ((), ())), preferred_element_type=F32)


def _mm(a, b, exact=False):
    return _dot(a, b, ((1,), (0,)), exact)


def _mm_nt(a, b, exact=False):
    return _dot(a, b, ((1,), (1,)), exact)


def _mm_tn(a, b, exact=False):
    return _dot(a, b, ((0,), (0,)), exact)


def _rms_fwd(x, w):
    r = lax.rsqrt(jnp.mean(x * x, axis=-1, keepdims=True) + EPS)
    xn = x * r
    return xn * w, xn, r


def _rms_bwd(dy, xn, r, w):
    dxn = dy * w
    dx = r * (dxn - xn * jnp.mean(dxn * xn, axis=-1, keepdims=True))
    dw = jnp.sum(dy * xn, axis=0, keepdims=True)
    return dx, dw


def _group_sums(v, gs):
    t, n = v.shape
    lane = lax.broadcasted_iota(jnp.int32, (t, 128), 1)
    out = []
    for p in range(n // 128):
        vb = v[:, 128 * p:128 * (p + 1)]
        if gs == 128:
            out.append(jnp.sum(vb, axis=-1, keepdims=True))
        else:
            out.append(jnp.sum(jnp.where(lane < 64, vb, 0.0), axis=-1, keepdims=True))
            out.append(jnp.sum(jnp.where(lane >= 64, vb, 0.0), axis=-1, keepdims=True))
    return out


def _group_bcast(sums, gs, t):
    lane = lax.broadcasted_iota(jnp.int32, (t, 128), 1)
    if gs == 128:
        return jnp.concatenate([jnp.broadcast_to(s, (t, 128)) for s in sums], axis=-1)
    return jnp.concatenate([jnp.where(lane < 64, sums[2 * p], sums[2 * p + 1]) for p in range(len(sums) // 2)],
                           axis=-1)


def _grms_fwd(x, w, gs):
    t = x.shape[0]
    r = lax.rsqrt(_group_bcast(_group_sums(x * x, gs), gs, t) * (1.0 / gs) + EPS)
    xn = x * r
    return xn * w, xn, r


def _grms_bwd(dy, xn, r, w, gs):
    t = dy.shape[0]
    dxn = dy * w
    dx = r * (dxn - xn * (_group_bcast(_group_sums(dxn * xn, gs), gs, t) * (1.0 / gs)))
    dw = jnp.sum(dy * xn, axis=0, keepdims=True)
    return dx, dw


def _rope_tables(c_tab, s_tab):
    lane = lax.broadcasted_iota(jnp.int32, c_tab.shape, 1)
    first = (lane >= MLA_NOPE) & (lane < MLA_NOPE + MLA_ROPE // 2)
    second = (lane >= MLA_NOPE + MLA_ROPE // 2) & (lane < MLA_QK)
    return c_tab, jnp.where(first, -s_tab, 0.0), jnp.where(second, s_tab, 0.0)


def _rope(v, c, sa, sb):
    return v * c + pltpu.roll(v, HEAD_PAD - MLA_ROPE // 2, 1) * sa + pltpu.roll(v, MLA_ROPE // 2, 1) * sb


def _rope_bwd(d, c, sa, sb):
    return d * c - pltpu.roll(d, HEAD_PAD - MLA_ROPE // 2, 1) * sa - pltpu.roll(d, MLA_ROPE // 2, 1) * sb


def _params(sem, vmem=VMEM_LIMIT):
    return pltpu.CompilerParams(dimension_semantics=sem, vmem_limit_bytes=vmem)


def _in_fwd(x, pos, invf, w_pre, win, qnw, wq, kvnw, wk, wv, tt=512):
    T = x.shape[0]

    def body(x_ref, pos_ref, invf_ref, wpre_ref, win_ref, qnw_ref, wq_ref, kvnw_ref, wk_ref, wv_ref,
             cq_ref, ckv_ref, xph_ref, q_ref, k_ref, v_ref, kt_ref, vt_ref, rc_ref, rs_ref):
        u, _, _ = _rms_fwd(x_ref[...], wpre_ref[...])
        lo = Q_RANK + KV_RANK + HEAD_PAD
        xp = _mm_nt(u, win_ref[:lo, :])
        xph_ref[...] = _mm_nt(u, win_ref[lo:, :])
        cq = xp[:, :Q_RANK]
        ckv = xp[:, Q_RANK:Q_RANK + KV_RANK]
        kr = xp[:, Q_RANK + KV_RANK:]
        cq_ref[...] = cq
        ckv_ref[...] = ckv
        ang = pos_ref[...].astype(F32) * invf_ref[...]
        c_tab = jnp.cos(ang)
        s_tab = jnp.sin(ang)
        rc_ref[...] = c_tab
        rs_ref[...] = s_tab
        c, sa, sb = _rope_tables(c_tab, s_tab)
        qn, _, _ = _rms_fwd(cq, qnw_ref[...])
        q = _mm(qn, wq_ref[...])
        kvn, _, _ = _rms_fwd(ckv, kvnw_ref[...])
        kn = _mm(kvn, wk_ref[...])
        v = _mm(kvn, wv_ref[...])
        v_ref[...] = v.astype(v_ref.dtype)
        vt_ref[...] = v.T.astype(vt_ref.dtype)
        krr = _rope(kr, c, sa, sb)
        for h in range(MLA_HEADS):
            sl = slice(HEAD_PAD * h, HEAD_PAD * (h + 1))
            q_ref[:, sl] = (_rope(q[:, sl], c, sa, sb) * ATTN_SCALE_LOG2).astype(q_ref.dtype)
            kh = kn[:, sl] + krr
            k_ref[:, sl] = kh.astype(k_ref.dtype)
            kt_ref[sl, :] = kh.T.astype(kt_ref.dtype)

    row = lambda w: pl.BlockSpec((tt, w), lambda i: (i, 0))
    full = lambda a: pl.BlockSpec(a.shape, lambda i: (0,) * a.ndim)
    qk_w = MLA_HEADS * HEAD_PAD
    return pl.pallas_call(
        body, name="in_fwd", grid=(T // tt,),
        in_specs=[row(D_MODEL), row(1), full(invf), full(w_pre), full(win), full(qnw), full(wq), full(kvnw),
                  full(wk), full(wv)],
        out_specs=[row(Q_RANK), row(KV_RANK), row(4 * HGRN_WIDTH), row(qk_w), row(qk_w), row(MLA_WIDTH),
                   pl.BlockSpec((qk_w, tt), lambda i: (0, i)), pl.BlockSpec((MLA_WIDTH, tt), lambda i: (0, i)),
                   row(HEAD_PAD), row(HEAD_PAD)],
        out_shape=[jax.ShapeDtypeStruct((T, Q_RANK), F32), jax.ShapeDtypeStruct((T, KV_RANK), F32),
                   jax.ShapeDtypeStruct((T, 4 * HGRN_WIDTH), F32), jax.ShapeDtypeStruct((T, qk_w), MXU_DTYPE),
                   jax.ShapeDtypeStruct((T, qk_w), MXU_DTYPE), jax.ShapeDtypeStruct((T, MLA_WIDTH), MXU_DTYPE),
                   jax.ShapeDtypeStruct((qk_w, T), MXU_DTYPE), jax.ShapeDtypeStruct((MLA_WIDTH, T), MXU_DTYPE),
                   jax.ShapeDtypeStruct((T, HEAD_PAD), F32), jax.ShapeDtypeStruct((T, HEAD_PAD), F32)],
        compiler_params=_params(("arbitrary",)),
    )(x, pos, invf, w_pre, win, qnw, wq, kvnw, wk, wv)


def _attn_fwd_t(qb, kb, vt, gather=(), tq=256, hps=8):
    T = qb.shape[0]
    nq = T // tq
    ng = len(gather)
    steps = (MLA_HEADS // hps) * nq
    pass_on = steps - 3

    def body(q_ref, k_ref, vt_ref, *rest):
        o_ref, lse_ref = rest[ng:ng + 2]
        acc_scr = rest[2 * ng + 2]
        qi = pl.program_id(1)
        step_no = pl.program_id(0) * nq + qi
        if ng:
            gat = _Gather(rest[:ng], rest[ng + 2:2 * ng + 2], *rest[2 * ng + 3:])

            @pl.when(step_no == 0)
            def _():
                for cp in gat.sends():
                    cp.start()

            @pl.when(step_no == pass_on)
            def _():
                for arrival in gat.arrivals():
                    arrival.wait_recv()
                for cp in gat.forwards():
                    cp.start()

        heads = [slice(HEAD_PAD * a, HEAD_PAD * (a + 1)) for a in range(hps)]
        acc_scr[...] = jnp.zeros_like(acc_scr)

        def step(j, carry, masked):
            start = pl.multiple_of(j * tq, tq)
            scores = [_mm_nt(k_ref[pl.ds(start, tq), heads[a]], q_ref[:, heads[a]]) for a in range(hps)]
            new, probs, alphas = [], [], []
            for a in range(hps):
                m, l = carry[a]
                s = scores[a]
                if masked:
                    kk = lax.broadcasted_iota(jnp.int32, (tq, tq), 0)
                    qq = lax.broadcasted_iota(jnp.int32, (tq, tq), 1)
                    s = jnp.where(kk <= qq, s, NEG_BIG)
                m_new = jnp.maximum(m, jnp.max(s, axis=0, keepdims=True))
                alpha = jnp.exp2(m - m_new)
                p = jnp.exp2(s - m_new)
                l = l * alpha + jnp.sum(p, axis=0, keepdims=True)
                new.append((m_new, l))
                probs.append(p.astype(MXU_DTYPE))
                alphas.append(alpha)
                if a % 2:
                    pr = a // 2
                    vtj = vt_ref[2 * MLA_V * pr:2 * MLA_V * (pr + 1), pl.ds(start, tq)]
                    none = jnp.zeros((MLA_V, tq), vtj.dtype)
                    pv = (_mm(jnp.concatenate([vtj[:MLA_V], none], axis=0), probs[a - 1])
                          + _mm(jnp.concatenate([none, vtj[MLA_V:]], axis=0), probs[a]))
                    acc_scr[pr] = acc_scr[pr] * jnp.where(row < MLA_V, alphas[a - 1], alphas[a]) + pv
            return tuple(new)

        row = lax.broadcasted_iota(jnp.int32, (2 * MLA_V, tq), 0)
        init = tuple((jnp.full((1, tq), NEG_BIG, F32), jnp.zeros((1, tq), F32)) for _ in range(hps))
        carry = lax.fori_loop(0, qi, lambda j, c: step(j, c, False), init)
        carry = step(qi, carry, True)
        for pr in range(hps // 2):
            (m0, l0), (m1, l1) = carry[2 * pr], carry[2 * pr + 1]
            ot = acc_scr[pr] / jnp.where(row < MLA_V, l0, l1)
            o_ref[:, 2 * MLA_V * pr:2 * MLA_V * (pr + 1)] = ot.T
            lse_ref[pr, 0:1, :] = m0 + jnp.log2(l0)
            lse_ref[pr, 1:2, :] = m1 + jnp.log2(l1)

        if ng:
            @pl.when(step_no == steps - 1)
            def _():
                for arrival in gat.forward_arrivals():
                    arrival.wait_recv()
                for cp in gat.sends() + gat.forwards():
                    cp.wait_send()

    return pl.pallas_call(
        body, name="attn_fwd", grid=(MLA_HEADS // hps, nq),
        in_specs=[pl.BlockSpec((tq, hps * HEAD_PAD), lambda g, i: (i, g)),
                  pl.BlockSpec((T, hps * HEAD_PAD), lambda g, i: (0, g)),
                  pl.BlockSpec((hps * MLA_V, T), lambda g, i: (g, 0))] + [ANY] * ng,
        out_specs=[pl.BlockSpec((tq, hps * MLA_V), lambda g, i: (i, g)),
                   pl.BlockSpec((hps // 2, 2, tq), lambda g, i: (g, 0, i))] + [ANY] * ng,
        out_shape=[jax.ShapeDtypeStruct((T, MLA_WIDTH), F32), jax.ShapeDtypeStruct((MLA_HEADS // 2, 2, T), F32)]
        + _Gather.out_shapes(gather),
        scratch_shapes=[pltpu.VMEM((hps // 2, 2 * MLA_V, tq), F32)] + (_Gather.semaphores(gather) if ng else []),
        compiler_params=_params(("arbitrary", "arbitrary")),
    )(qb, kb, vt, *gather)


def _attn_bwd_t(qb, kb, kt, vb, dob, lse, dvec, send=(), tq=512, hps=4):
    T = qb.shape[0]
    nq = T // tq
    ns = len(send)
    steps = (MLA_HEADS // hps) * nq

    def body(q_ref, k_ref, kt_ref, v_ref, do_ref, lse_ref, d_ref, *rest):
        dqt_ref, dk_ref, dv_ref = rest[ns:ns + 3]
        va_scr, dv_scr = rest[2 * ns + 3:2 * ns + 5]
        j = pl.program_id(1)
        step_no = pl.program_id(0) * nq + j
        if ns:
            @pl.when(step_no == 0)
            def _():
                for cp in _chip_swap_copies(rest[:ns], rest[ns + 3:2 * ns + 3], *rest[2 * ns + 5:]):
                    cp.start()

        @pl.when(j == 0)
        def _():
            dqt_ref[...] = jnp.zeros_like(dqt_ref)

        lane = lax.broadcasted_iota(jnp.int32, (tq, 2 * MLA_V), 1)
        heads = [slice(HEAD_PAD * a, HEAD_PAD * (a + 1)) for a in range(hps)]
        pairs = [slice(2 * MLA_V * p, 2 * MLA_V * (p + 1)) for p in range(hps // 2)]
        for pr in range(hps // 2):
            vpair = v_ref[:, pairs[pr]]
            va_scr[2 * pr] = jnp.where(lane < MLA_V, vpair, jnp.zeros_like(vpair))
            va_scr[2 * pr + 1] = jnp.where(lane >= MLA_V, vpair, jnp.zeros_like(vpair))
        dk_ref[...] = jnp.zeros_like(dk_ref)
        dv_scr[...] = jnp.zeros_like(dv_scr)

        def step(i, masked):
            start = pl.multiple_of(i * tq, tq)
            rows = pl.ds(start, tq)
            scores = [_mm_nt(k_ref[:, heads[a]], q_ref[rows, heads[a]]) for a in range(hps)]
            dps = [_mm_nt(va_scr[a], do_ref[rows, pairs[a // 2]]) for a in range(hps)]
            for a in range(hps):
                pr, r = a // 2, a % 2
                p = jnp.exp2(scores[a] - lse_ref[pr, r:r + 1, rows])
                if masked:
                    kk = lax.broadcasted_iota(jnp.int32, (tq, tq), 0)
                    qq = lax.broadcasted_iota(jnp.int32, (tq, tq), 1)
                    p = jnp.where(kk <= qq, p, 0.0)
                ds = p * (dps[a] - d_ref[pr, r:r + 1, rows])
                dv_scr[a] += _mm(p, do_ref[rows, pairs[pr]])
                dk_ref[:, heads[a]] += _mm(ds, q_ref[rows, heads[a]])
                dqt_ref[heads[a], rows] += _mm(kt_ref[heads[a], :], ds)

        def loop_body(i, _):
            step(i, False)
            return 0

        step(j, True)
        lax.fori_loop(j + 1, nq, loop_body, 0)
        for pr in range(hps // 2):
            dv_ref[:, pairs[pr]] = jnp.where(lane < MLA_V, dv_scr[2 * pr], dv_scr[2 * pr + 1])
        dk_ref[...] = dk_ref[...] * (ATTN_SCALE / ATTN_SCALE_LOG2)

        if ns:
            @pl.when(step_no == steps - 1)
            def _():
                for cp in _chip_swap_copies(rest[:ns], rest[ns + 3:2 * ns + 3], *rest[2 * ns + 5:]):
                    cp.wait()

    stat = pl.BlockSpec((hps // 2, 2, T), lambda g, j: (g, 0, 0))
    return pl.pallas_call(
        body, name="attn_bwd", grid=(MLA_HEADS // hps, nq),
        in_specs=[pl.BlockSpec((T, hps * HEAD_PAD), lambda g, j: (0, g)),
                  pl.BlockSpec((tq, hps * HEAD_PAD), lambda g, j: (j, g)),
                  pl.BlockSpec((hps * HEAD_PAD, tq), lambda g, j: (g, j)),
                  pl.BlockSpec((tq, hps * MLA_V), lambda g, j: (j, g)),
                  pl.BlockSpec((T, hps * MLA_V), lambda g, j: (0, g)), stat, stat] + [ANY] * ns,
        out_specs=[pl.BlockSpec((hps * HEAD_PAD, T), lambda g, j: (g, 0)),
                   pl.BlockSpec((tq, hps * HEAD_PAD), lambda g, j: (j, g)),
                   pl.BlockSpec((tq, hps * MLA_V), lambda g, j: (j, g))] + [ANY] * ns,
        out_shape=[jax.ShapeDtypeStruct((MLA_HEADS * HEAD_PAD, T), F32),
                   jax.ShapeDtypeStruct((T, MLA_HEADS * HEAD_PAD), F32),
                   jax.ShapeDtypeStruct((T, MLA_WIDTH), F32)] + _chip_swap_shapes(send),
        scratch_shapes=[pltpu.VMEM((hps, tq, 2 * MLA_V), vb.dtype), pltpu.VMEM((hps, tq, 2 * MLA_V), F32)]
        + ([pltpu.SemaphoreType.DMA((3 * ns,)), pltpu.SemaphoreType.DMA((3 * ns,))] if ns else []),
        compiler_params=_params(("arbitrary", "arbitrary")),
    )(qb, kb, kt, vb, dob, lse, dvec, *send)


def _cumsum_rows(x):
    n = x.shape[0]
    row = lax.broadcasted_iota(jnp.int32, x.shape, 0)
    s = 1
    while s < n:
        x = x + jnp.where(row >= s, pltpu.roll(x, s, 0), 0.0)
        s *= 2
    return x


def _rev_cumsum_rows(x):
    n = x.shape[0]
    row = lax.broadcasted_iota(jnp.int32, x.shape, 0)
    s = 1
    while s < n:
        x = x + jnp.where(row < n - s, pltpu.roll(x, n - s, 0), 0.0)
        s *= 2
    return x


def _lb_from_logits(l):
    l0, l1 = l[0:1, :], l[1:2, :]
    m = jnp.maximum(l0, l1)
    e0, e1 = jnp.exp(l0 - m), jnp.exp(l1 - m)
    return e0 / (e0 + e1)


def _hgrn_gates(hq, hf, lb):
    sig_f = jax.nn.sigmoid(hf)
    f = lb + (1.0 - lb) * sig_f
    sig_q = jax.nn.sigmoid(hq)
    return sig_f, f, jnp.log(f), 1.0 - f, sig_q, hq * sig_q


def _hgrn_intra(q, kk, b, exact=False):
    row = lax.broadcasted_iota(jnp.int32, b.shape, 0)
    qs, ks, eqs, eks, a_rows = [], [], [], [], []
    for i in range(CHUNK // SUB):
        ref = b[SUB * i + SUB // 2:SUB * i + SUB // 2 + 1, :]
        eq = jnp.exp(b[SUB * i:SUB * (i + 1), :] - ref)
        ek = jnp.exp(jnp.where(row < SUB * (i + 1), ref - b, NEG_BIG))
        qi = q[SUB * i:SUB * (i + 1), :] * eq
        ki = kk * ek
        a_rows.append(_mm_nt(qi, ki, exact))
        qs.append(qi), ks.append(ki), eqs.append(eq), eks.append(ek)
    tt = lax.broadcasted_iota(jnp.int32, (CHUNK, CHUNK), 0)
    ss = lax.broadcasted_iota(jnp.int32, (CHUNK, CHUNK), 1)
    causal = ss <= tt
    a = jnp.where(causal, jnp.concatenate(a_rows, axis=0), 0.0)
    return a, causal, qs, ks, eqs, eks


def _hgrn_fwd(xph, lbl, tg=512):
    T = xph.shape[0]
    ng, ncg = T // tg, tg // CHUNK
    cols = [slice(HGRN_DIM * h, HGRN_DIM * (h + 1)) for h in range(HGRN_HEADS)]

    def body(lbl_ref, hq_ref, hf_ref, hi_ref, o_ref, st_ref, s_scr):
        @pl.when(pl.program_id(0) == 0)
        def _():
            s_scr[...] = jnp.zeros_like(s_scr)

        lb = _lb_from_logits(lbl_ref[...])

        def chunks(it, _):
            pre = []
            for k in range(HGRN_CPI):
                c = it * HGRN_CPI + k
                rows = pl.ds(pl.multiple_of(c * CHUNK, CHUNK), CHUNK)
                for cs in cols:
                    _, _, lf, kk, _, q = _hgrn_gates(hq_ref[rows, cs], hf_ref[rows, cs], lb[:, cs])
                    v = hi_ref[rows, cs]
                    b = _cumsum_rows(lf)
                    a = _hgrn_intra(q, kk, b)[0]
                    b_last = b[CHUNK - 1:CHUNK, :]
                    pre.append((c, rows, q * jnp.exp(b), a, v, jnp.exp(b_last), _mm_tn(v, kk * jnp.exp(b_last - b))))
            for i, (c, rows, qe, a, v, ebl, upd) in enumerate(pre):
                h = i % HGRN_HEADS
                st = s_scr[h]
                st_ref[h, c] = st
                o_ref[rows, cols[h]] = _mm_nt(qe, st) + _mm(a, v)
                s_scr[h] = st * ebl + upd
            return 0

        lax.fori_loop(0, ncg // HGRN_CPI, chunks, 0)

    col = lambda k: pl.BlockSpec((tg, HGRN_WIDTH), lambda g: (g, k))
    return pl.pallas_call(
        body, name="hgrn_fwd", grid=(ng,),
        in_specs=[pl.BlockSpec((2, HGRN_WIDTH), lambda g: (0, 0)), col(0), col(1), col(2)],
        out_specs=[col(0), pl.BlockSpec((HGRN_HEADS, ncg, HGRN_DIM, HGRN_DIM), lambda g: (0, g, 0, 0))],
        out_shape=[jax.ShapeDtypeStruct((T, HGRN_WIDTH), F32),
                   jax.ShapeDtypeStruct((HGRN_HEADS, T // CHUNK, HGRN_DIM, HGRN_DIM), F32)],
        scratch_shapes=[pltpu.VMEM((HGRN_HEADS, HGRN_DIM, HGRN_DIM), F32)],
        compiler_params=_params(("arbitrary",)),
    )(lbl, xph, xph, xph)


def _hgrn_bwd(xph, lbl, states, d_o, fill=(), tg=512):
    T = xph.shape[0]
    ng, ncg = T // tg, tg // CHUNK
    cols = [slice(HGRN_DIM * h, HGRN_DIM * (h + 1)) for h in range(HGRN_HEADS)]
    nsub = CHUNK // SUB
    nf = len(fill)

    def body(lbl_ref, hq_ref, hf_ref, hi_ref, st_ref, do_ref, *rest):
        dhq_ref, dhf_ref, dhi_ref, dlg_ref = rest[nf:nf + 4]
        ds_scr, dlb_scr = rest[2 * nf + 4:2 * nf + 6]
        fill_copies = lambda: _pair_fill_copies(rest[nf + 4:2 * nf + 4], *rest[2 * nf + 6:])
        g = pl.program_id(0)

        @pl.when(g == 0)
        def _():
            ds_scr[...] = jnp.zeros_like(ds_scr)
            dlb_scr[...] = jnp.zeros_like(dlb_scr)
            for cp in (fill_copies()[0] if nf else ()):
                cp.start()

        lb = _lb_from_logits(lbl_ref[...])

        def chunks(it, _):
            pre = []
            for k, h in ((k, h) for k in range(HGRN_CPI) for h in range(HGRN_HEADS)):
                cs = cols[h]
                c = ncg - 1 - (it * HGRN_CPI + k)
                rows = pl.ds(pl.multiple_of(c * CHUNK, CHUNK), CHUNK)
                hq = hq_ref[rows, cs]
                sig_f, f, lf, kk, sig_q, q = _hgrn_gates(hq, hf_ref[rows, cs], lb[:, cs])
                v = hi_ref[rows, cs]
                do = do_ref[rows, cs]
                b = _cumsum_rows(lf)
                eb = jnp.exp(b)
                a, causal, qs, ks, eqs, eks = _hgrn_intra(q, kk, b)
                b_last = b[CHUNK - 1:CHUNK, :]
                st = st_ref[h, c]
                pre.append(dict(h=h, cs=cs, rows=rows, hq=hq, sig_f=sig_f, f=f, kk=kk, sig_q=sig_q, q=q, v=v, eb=eb, qs=qs,
                                ks=ks, eqs=eqs,
                                eks=eks, ebl=jnp.exp(b_last), el=jnp.exp(b_last - b), st=st,
                                da=jnp.where(causal, _mm_nt(do, v, True), 0.0), dq=_mm(do, st, True) * eb,
                                dv=_mm_tn(a, do), dsu=_mm_tn(do, q * eb, True)))
            for w in pre:
                dq_rows = []
                dk = jnp.zeros_like(w["q"])
                for i in range(nsub):
                    dai = w["da"][SUB * i:SUB * (i + 1), :]
                    dq_rows.append(_mm(dai, w["ks"][i], True) * w["eqs"][i])
                    dk = dk + _mm_tn(dai, w["qs"][i], True) * w["eks"][i]
                w["dq"] = w["dq"] + jnp.concatenate(dq_rows, axis=0)
                w["dk"] = dk
            for w in pre:
                h, cs, rows = w["h"], w["cs"], w["rows"]
                kk, el, ebl, dst = w["kk"], w["el"], w["ebl"], ds_scr[h]
                dk_state = _mm(w["v"], dst, True) * el
                dk = w["dk"] + dk_state
                e_last = (ebl * jnp.sum(w["st"] * dst, axis=0, keepdims=True)
                          + jnp.sum(kk * dk_state, axis=0, keepdims=True))
                dlf = _rev_cumsum_rows(w["q"] * w["dq"] - kk * dk) + e_last
                ds_scr[h] = dst * ebl + w["dsu"]
                df = dlf / w["f"] - dk
                sig_f, sig_q = w["sig_f"], w["sig_q"]
                dhf_ref[rows, cs] = df * (1.0 - lb[:, cs]) * sig_f * (1.0 - sig_f)
                dlb_scr[:, cs] += jnp.sum(df * (1.0 - sig_f), axis=0, keepdims=True)
                dhq_ref[rows, cs] = w["dq"] * sig_q * (1.0 + w["hq"] * (1.0 - sig_q))
                dhi_ref[rows, cs] = w["dv"] + _mm_nt(kk * el, dst)
            return 0

        lax.fori_loop(0, ncg // HGRN_CPI, chunks, 0)

        @pl.when(g == ng - 1)
        def _():
            dl0 = dlb_scr[...] * lb * (1.0 - lb)
            dlg_ref[...] = jnp.concatenate([dl0, -dl0], axis=0)
            if nf:
                copies, waits = fill_copies()
                for w in waits:
                    w.wait_recv()
                for cp in copies:
                    cp.wait_send()

    col = lambda k: pl.BlockSpec((tg, HGRN_WIDTH), lambda g: (ng - 1 - g, k))
    logits = pl.BlockSpec((2, HGRN_WIDTH), lambda g: (0, 0))
    big = jax.ShapeDtypeStruct((T, HGRN_WIDTH), F32)
    n_in, n_out = 6, 4
    return pl.pallas_call(
        body, name="hgrn_bwd", grid=(ng,),
        in_specs=[logits, col(0), col(1), col(2),
                  pl.BlockSpec((HGRN_HEADS, ncg, HGRN_DIM, HGRN_DIM), lambda g: (0, ng - 1 - g, 0, 0)), col(0)] + [ANY] * nf,
        out_specs=[col(0), col(0), col(0), logits] + [ANY] * nf,
        out_shape=[big, big, big, jax.ShapeDtypeStruct((2, HGRN_WIDTH), F32)]
        + [jax.ShapeDtypeStruct(f.shape, f.dtype) for f in fill],
        input_output_aliases={n_in + k: n_out + k for k in range(nf)},
        scratch_shapes=[pltpu.VMEM((HGRN_HEADS, HGRN_DIM, HGRN_DIM), F32), pltpu.VMEM((1, HGRN_WIDTH), F32)]
        + ([pltpu.SemaphoreType.DMA((nf,)), pltpu.SemaphoreType.DMA((nf,))] if nf else []),
        compiler_params=_params(("arbitrary",)),
    )(lbl, xph, xph, xph, states, d_o, *fill)


def _proj_fwd(x, o_raw, oh_raw, xph, wout, w_mla, w_hg, w_post, w_fpre, tt=512):
    T = x.shape[0]

    def body(x_ref, o_ref, oh_ref, hg_ref, wout_ref, wmla_ref, whg_ref, wpost_ref, wfpre_ref,
             h1_ref, y1_ref, z_ref, mix_ref):
        om, _, _ = _grms_fwd(o_ref[...], wmla_ref[...], MLA_V)
        hg = hg_ref[...]
        ohn, _, _ = _grms_fwd(oh_ref[...], whg_ref[...], HGRN_DIM)
        mix = jnp.concatenate([om, ohn * (hg * jax.nn.sigmoid(hg))], axis=-1)
        mix_ref[...] = mix.astype(mix_ref.dtype)
        y1 = _mm(mix, wout_ref[...])
        y1_ref[...] = y1
        h1 = x_ref[...] + _rms_fwd(y1, wpost_ref[...])[0]
        h1_ref[...] = h1
        z_ref[...] = _rms_fwd(h1, wfpre_ref[...])[0].astype(z_ref.dtype)

    row = lambda w: pl.BlockSpec((tt, w), lambda i: (i, 0))
    full = lambda a: pl.BlockSpec(a.shape, lambda i: (0,) * a.ndim)
    sds = jax.ShapeDtypeStruct
    return pl.pallas_call(
        body, name="proj_fwd", grid=(T // tt,),
        in_specs=[row(D_MODEL), row(MLA_WIDTH), row(HGRN_WIDTH), pl.BlockSpec((tt, HGRN_WIDTH), lambda i: (i, 3)),
                  full(wout), full(w_mla), full(w_hg), full(w_post), full(w_fpre)],
        out_specs=[row(D_MODEL)] * 4,
        out_shape=[sds((T, D_MODEL), F32), sds((T, D_MODEL), F32), sds((T, D_MODEL), MXU_DTYPE),
                   sds((T, D_MODEL), MXU_DTYPE)],
        compiler_params=_params(("arbitrary",)),
    )(x, o_raw, oh_raw, xph, wout, w_mla, w_hg, w_post, w_fpre)


def _ffn_fwd(zb, h1, tgt, w_fpost, wg, wu, wd, tt=256):
    T = zb.shape[0]
    nj = N_CHIPS

    def body(z_ref, h1_ref, tgt_ref, wfpost_ref, wg_ref, wu_ref, wd_ref, g_ref, up_ref, dy2_ref, dh2_ref, loss_ref, dwf_ref):
        @pl.when(pl.program_id(0) == 0)
        def _():
            loss_ref[...] = jnp.zeros_like(loss_ref)
            dwf_ref[...] = jnp.zeros_like(dwf_ref)

        z = z_ref[...]
        gs = [_mm_nt(z, wg_ref[j]) for j in range(nj)]
        ups = [_mm_nt(z, wu_ref[j]) for j in range(nj)]
        y2 = jnp.zeros((tt, D_MODEL), F32)
        for j in range(nj):
            g_ref[j] = gs[j]
            up_ref[j] = ups[j]
            y2 = y2 + _mm(gs[j] * jax.nn.sigmoid(gs[j]) * ups[j], wd_ref[j])
        w = wfpost_ref[...]
        y2s, y2n, r2 = _rms_fwd(y2, w)
        e = h1_ref[...] + y2s - tgt_ref[...]
        loss_ref[...] += jnp.sum(e * e, axis=0, keepdims=True)
        dh2 = e * (1.0 / D_MODEL)
        dh2_ref[...] = dh2
        dy2, dwf = _rms_bwd(dh2, y2n, r2, w)
        dy2_ref[...] = dy2.astype(dy2_ref.dtype)
        dwf_ref[...] += dwf

    row = pl.BlockSpec((tt, D_MODEL), lambda i: (i, 0))
    vec = pl.BlockSpec((1, D_MODEL), lambda i: (0, 0))
    resident = pl.BlockSpec((nj, FF_SHARD, D_MODEL), lambda i: (0, 0, 0), pipeline_mode=pl.Buffered(1))
    act = pl.BlockSpec((nj, tt, FF_SHARD), lambda i: (0, i, 0))
    sds = jax.ShapeDtypeStruct
    return pl.pallas_call(
        body, name="ffn_fwd", grid=(T // tt,),
        in_specs=[row, row, row, vec, resident, resident, resident],
        out_specs=[act, act, row, row, vec, vec],
        out_shape=[sds((nj, T, FF_SHARD), F32), sds((nj, T, FF_SHARD), F32), sds((T, D_MODEL), MXU_DTYPE),
                   sds((T, D_MODEL), F32), sds((1, D_MODEL), F32), sds((1, D_MODEL), F32)],
        compiler_params=_params(("arbitrary",)),
    )(zb, h1, tgt, w_fpost, wg, wu, wd)


def _ffn_bwd(zb, g, up, dy2b, wg, wu, wd, tt=512):
    T = zb.shape[0]
    nj = N_CHIPS

    def body(z_ref, g_ref, up_ref, dy2_ref, wg_ref, wu_ref, wd_ref, dwg_ref, dwu_ref, dwd_ref, dz_ref, acc_ref):
        j, i = pl.program_id(0), pl.program_id(1)
        rows = pl.ds(pl.multiple_of(i * tt, tt), tt)

        @pl.when(i == 0)
        def _():
            dwg_ref[...] = jnp.zeros_like(dwg_ref)
            dwu_ref[...] = jnp.zeros_like(dwu_ref)
            dwd_ref[...] = jnp.zeros_like(dwd_ref)

        z, g_, up_, dy2 = z_ref[...], g_ref[0], up_ref[0], dy2_ref[...]
        sg = jax.nn.sigmoid(g_)
        act = g_ * sg
        dff = _mm_nt(dy2, wd_ref[0])
        dwd_ref[0] += _mm_tn(act * up_, dy2)
        dg = dff * up_ * sg * (1.0 + g_ * (1.0 - sg))
        dup = dff * act
        dwg_ref[0] += _mm_tn(dg, z)
        dwu_ref[0] += _mm_tn(dup, z)
        dz = _mm(dg, wg_ref[0]) + _mm(dup, wu_ref[0])

        @pl.when(j == 0)
        def _():
            acc_ref[rows, :] = dz

        @pl.when((j > 0) & (j < nj - 1))
        def _():
            acc_ref[rows, :] += dz

        @pl.when(j == nj - 1)
        def _():
            dz_ref[...] = acc_ref[rows, :] + dz

    row = pl.BlockSpec((tt, D_MODEL), lambda j, i: (i, 0))
    act = pl.BlockSpec((1, tt, FF_SHARD), lambda j, i: (j, i, 0))
    w_sh = pl.BlockSpec((1, FF_SHARD, D_MODEL), lambda j, i: (j, 0, 0))
    w_grad = jax.ShapeDtypeStruct((nj, FF_SHARD, D_MODEL), F32)
    return pl.pallas_call(
        body, name="ffn_bwd", grid=(nj, T // tt),
        in_specs=[row, act, act, row, w_sh, w_sh, w_sh],
        out_specs=[w_sh, w_sh, w_sh, pl.BlockSpec((tt, D_MODEL), lambda j, i: (jnp.where(j == nj - 1, i, 0), 0))],
        out_shape=[w_grad, w_grad, w_grad, jax.ShapeDtypeStruct((T, D_MODEL), F32)],
        scratch_shapes=[pltpu.VMEM((T, D_MODEL), F32)],
        compiler_params=_params(("arbitrary", "arbitrary"), vmem=FFN_BWD_VMEM),
    )(zb, g, up, dy2b, wg, wu, wd)


def _mid_bwd(dz, dh2, h1, y1, mixb, o_raw, oh_raw, xph, wout, w_fpre, w_post, w_mla, w_hg, swap=(), tt=512):
    T = dh2.shape[0]
    nsw = len(swap)
    n_in, n_out = 13, 10

    def body(*refs):
        (dz_ref, dh2_ref, h1_ref, y1_ref, mix_ref, o_ref, oh_ref, hg_ref, wout_ref, wfpre_ref, wpost_ref,
         wmla_ref, whg_ref) = refs[:n_in]
        (dh1_ref, dwout_ref, do_ref, doh_ref, dhg_ref, dvec_ref, dwfpre_ref, dwpost_ref, dwmla_ref,
         dwhg_ref) = refs[n_in + nsw:n_in + nsw + n_out]
        swap_copies = lambda: _pair_swap_copies(refs[n_in:n_in + nsw], refs[n_in + nsw + n_out:n_in + 2 * nsw + n_out],
                                                *refs[n_in + 2 * nsw + n_out:])

        @pl.when(pl.program_id(0) == 0)
        def _():
            for r in (dwout_ref, dwfpre_ref, dwpost_ref, dwmla_ref, dwhg_ref):
                r[...] = jnp.zeros_like(r)
            for cp in (swap_copies() if nsw else ()):
                cp.start()

        dz = dz_ref[...]
        wfpre = wfpre_ref[...]
        _, h1n, r = _rms_fwd(h1_ref[...], wfpre)
        dh1_z, dwfpre = _rms_bwd(dz, h1n, r, wfpre)
        dwfpre_ref[...] += dwfpre
        dh1 = dh2_ref[...] + dh1_z
        dh1_ref[...] = dh1
        wpost = wpost_ref[...]
        _, y1n, r1 = _rms_fwd(y1_ref[...], wpost)
        dy1, dwpost = _rms_bwd(dh1, y1n, r1, wpost)
        dwpost_ref[...] += dwpost
        dmix = _mm_nt(dy1, wout_ref[...])
        dwout_ref[...] += _mm_tn(mix_ref[...], dy1)
        wmla = wmla_ref[...]
        o = o_ref[...]
        _, on, ro = _grms_fwd(o, wmla, MLA_V)
        d_o, dwmla = _grms_bwd(dmix[:, :MLA_WIDTH], on, ro, wmla, MLA_V)
        dwmla_ref[...] += dwmla
        do_ref[...] = d_o.astype(do_ref.dtype)
        hh = lax.broadcasted_iota(jnp.int32, (MLA_HEADS, MLA_WIDTH), 0)
        ll = lax.broadcasted_iota(jnp.int32, (MLA_HEADS, MLA_WIDTH), 1)
        sel = jnp.where((ll >= hh * MLA_V) & (ll < (hh + 1) * MLA_V), 1.0, 0.0)
        dvec_ref[...] = _mm_nt(sel, d_o * o, True)
        whg = whg_ref[...]
        hg = hg_ref[...]
        sg = jax.nn.sigmoid(hg)
        _, ohn, rh = _grms_fwd(oh_ref[...], whg, HGRN_DIM)
        dmh = dmix[:, MLA_WIDTH:]
        dhg_ref[...] = dmh * ohn * whg * sg * (1.0 + hg * (1.0 - sg))
        d_oh, dwhg = _grms_bwd(dmh * (hg * sg), ohn, rh, whg, HGRN_DIM)
        dwhg_ref[...] += dwhg
        doh_ref[...] = d_oh

        if nsw:
            @pl.when(pl.program_id(0) == T // tt - 1)
            def _():
                for cp in swap_copies():
                    cp.wait()

    row = lambda w: pl.BlockSpec((tt, w), lambda i: (i, 0))
    full = lambda a: pl.BlockSpec(a.shape, lambda i: (0,) * a.ndim)
    vec = lambda w: pl.BlockSpec((1, w), lambda i: (0, 0))
    sds = jax.ShapeDtypeStruct
    return pl.pallas_call(
        body, name="mid_bwd", grid=(T // tt,),
        in_specs=[row(D_MODEL), row(D_MODEL), row(D_MODEL), row(D_MODEL),
                  row(D_MODEL), row(MLA_WIDTH), row(HGRN_WIDTH), pl.BlockSpec((tt, HGRN_WIDTH), lambda i: (i, 3)),
                  full(wout), vec(D_MODEL), vec(D_MODEL), vec(MLA_WIDTH), vec(HGRN_WIDTH)] + [ANY] * nsw,
        out_specs=[row(D_MODEL), full(wout), row(MLA_WIDTH), row(HGRN_WIDTH), row(HGRN_WIDTH),
                   pl.BlockSpec((MLA_HEADS, tt), lambda i: (0, i)),
                   vec(D_MODEL), vec(D_MODEL), vec(MLA_WIDTH), vec(HGRN_WIDTH)] + [ANY] * nsw,
        out_shape=[sds((T, D_MODEL), F32), sds(wout.shape, F32), sds((T, MLA_WIDTH), MXU_DTYPE), sds((T, HGRN_WIDTH), F32),
                   sds((T, HGRN_WIDTH), F32), sds((MLA_HEADS, T), F32),
                   sds((1, D_MODEL), F32), sds((1, D_MODEL), F32), sds((1, MLA_WIDTH), F32), sds((1, HGRN_WIDTH), F32)]
        + _half_stack_shapes(swap),
        scratch_shapes=[pltpu.SemaphoreType.DMA((nsw,)), pltpu.SemaphoreType.DMA((nsw,))] if nsw else [],
        compiler_params=_params(("arbitrary",)),
    )(dz, dh2, h1, y1, mixb, o_raw, oh_raw, xph, wout, w_fpre, w_post, w_mla, w_hg, *swap)


def _in_bwd(x, dh1, cq, ckv, dq, dk, dv, dhq, dhf, dhi, dhg, rc, rs, w_pre, win, qnw, wq, kvnw, wk, wv, tt=256):
    T = x.shape[0]

    def body(x_ref, dh1_ref, cq_ref, ckv_ref, dq_ref, dk_ref, dv_ref, dhq_ref, dhf_ref, dhi_ref, dhg_ref, rc_ref, rs_ref,
             wpre_ref, win_ref, qnw_ref, wq_ref, kvnw_ref, wk_ref, wv_ref,
             dx_ref, dwin_ref, dwq_ref, dwk_ref, dwv_ref, dwpre_ref, dqnw_ref, dkvnw_ref):
        @pl.when(pl.program_id(0) == 0)
        def _():
            for r in (dwin_ref, dwq_ref, dwk_ref, dwv_ref, dwpre_ref, dqnw_ref, dkvnw_ref):
                r[...] = jnp.zeros_like(r)

        def add_win_grad(r, first):
            for arr0, n, chip, row0 in _win_grad_segments():
                if first <= arr0 and arr0 + n <= first + r.shape[0]:
                    dwin_ref[chip, row0:row0 + n, :] += r[arr0 - first:arr0 - first + n]

        lo = Q_RANK + KV_RANK + HEAD_PAD
        dxp_h = jnp.concatenate([dhq_ref[...], dhf_ref[...], dhi_ref[...], dhg_ref[...]], axis=-1)
        du = _mm(dxp_h, win_ref[lo:, :])
        wpre = wpre_ref[...]
        u, xn, rx = _rms_fwd(x_ref[...], wpre)
        add_win_grad(_mm_tn(dxp_h, u), lo)
        c, sa, sb = _rope_tables(rc_ref[...], rs_ref[...])
        lane = lax.broadcasted_iota(jnp.int32, (tt, HEAD_PAD), 1)
        dk_all = dk_ref[...]
        dq_lin = []
        dkr = jnp.zeros((tt, HEAD_PAD), F32)
        for h in range(MLA_HEADS):
            sl = slice(HEAD_PAD * h, HEAD_PAD * (h + 1))
            dq_lin.append(_rope_bwd(dq_ref[sl, :].T * ATTN_SCALE, c, sa, sb))
            dkr = dkr + dk_all[:, sl]
        dq_lin = jnp.concatenate(dq_lin, axis=-1)
        dkr = jnp.where((lane >= MLA_NOPE) & (lane < MLA_QK), _rope_bwd(dkr, c, sa, sb), 0.0)
        qnw = qnw_ref[...]
        qn, cqn, rq = _rms_fwd(cq_ref[...], qnw)
        dwq_ref[...] += _mm_tn(qn, dq_lin)
        dcq, dqnw = _rms_bwd(_mm_nt(dq_lin, wq_ref[...]), cqn, rq, qnw)
        dqnw_ref[...] += dqnw
        kvnw = kvnw_ref[...]
        kvn, ckvn, rkv = _rms_fwd(ckv_ref[...], kvnw)
        dv_ = dv_ref[...]
        dwk_ref[...] += _mm_tn(kvn, dk_all)
        dwv_ref[...] += _mm_tn(kvn, dv_)
        dckv, dkvnw = _rms_bwd(_mm_nt(dk_all, wk_ref[...]) + _mm_nt(dv_, wv_ref[...]), ckvn, rkv, kvnw)
        dkvnw_ref[...] += dkvnw
        dxp_a = jnp.concatenate([dcq, dckv, dkr], axis=-1)
        add_win_grad(_mm_tn(dxp_a, u), 0)
        dx_u, dwpre = _rms_bwd(du + _mm(dxp_a, win_ref[:lo, :]), xn, rx, wpre)
        dwpre_ref[...] += dwpre
        dx_ref[...] = dh1_ref[...] + dx_u

    row = lambda w: pl.BlockSpec((tt, w), lambda i: (i, 0))
    full = lambda a: pl.BlockSpec(a.shape, lambda i: (0,) * a.ndim)
    sds = jax.ShapeDtypeStruct
    qk_w = MLA_HEADS * HEAD_PAD
    return pl.pallas_call(
        body, name="in_bwd", grid=(T // tt,),
        in_specs=[row(D_MODEL), row(D_MODEL), row(Q_RANK), row(KV_RANK), pl.BlockSpec((qk_w, tt), lambda i: (0, i)),
                  row(qk_w), row(MLA_WIDTH),
                  row(HGRN_WIDTH), row(HGRN_WIDTH), row(HGRN_WIDTH), row(HGRN_WIDTH), row(HEAD_PAD), row(HEAD_PAD),
                  full(w_pre), full(win), full(qnw), full(wq), full(kvnw), full(wk), full(wv)],
        out_specs=[row(D_MODEL), pl.BlockSpec(WIN_COMM_SHAPE, lambda i: (0, 0, 0)), full(wq), full(wk), full(wv),
                   full(w_pre), full(qnw), full(kvnw)],
        out_shape=[sds((T, D_MODEL), F32), sds(WIN_COMM_SHAPE, F32), sds(wq.shape, F32), sds(wk.shape, F32),
                   sds(wv.shape, F32), sds(w_pre.shape, F32), sds(qnw.shape, F32), sds(kvnw.shape, F32)],
        compiler_params=_params(("arbitrary",)),
    )(x, dh1, cq, ckv, dq, dk, dv, dhq, dhf, dhi, dhg, rc, rs, w_pre, win, qnw, wq, kvnw, wk, wv)


def _arrange_weights(win_t, wuq_full, wukv):
    dt = win_t.dtype
    z = lambda n: jnp.zeros((n, D_MODEL), dt)
    s2 = Q_RANK + KV_RANK
    win_arr = jnp.concatenate([win_t[:s2], z(MLA_NOPE), win_t[s2:s2 + MLA_ROPE], z(HEAD_PAD - MLA_QK),
                               win_t[s2 + MLA_ROPE:]], axis=0)
    wq_arr = jnp.pad(wuq_full, ((0, 0), (0, 0), (0, HEAD_PAD - MLA_QK))).reshape(Q_RANK, MLA_HEADS * HEAD_PAD)
    wk_arr = jnp.pad(wukv[:, :, :MLA_NOPE], ((0, 0), (0, 0), (0, HEAD_PAD - MLA_NOPE))).reshape(
        KV_RANK, MLA_HEADS * HEAD_PAD)
    wv_arr = wukv[:, :, MLA_NOPE:].reshape(KV_RANK, MLA_WIDTH)
    return win_arr, wq_arr, wk_arr, wv_arr


WIN_COMM_SHAPE = (N_CHIPS, -(-D_IN // N_CHIPS // 32) * 32, D_MODEL)


def _win_grad_segments():
    s2 = Q_RANK + KV_RANK
    runs = [(0, s2, 0), (s2, s2 + MLA_ROPE, MLA_NOPE), (s2 + MLA_ROPE, D_IN, HEAD_PAD - MLA_ROPE)]
    per = D_IN // N_CHIPS
    segs = []
    for lo, hi, shift in runs:
        for k in range(N_CHIPS):
            a, b = max(lo, per * k), min(hi, per * (k + 1))
            if a < b:
                segs.append((a + shift, b - a, k, a - per * k))
    return segs


def _unarrange_grads(dwq_arr, dwk_arr, dwv_arr):
    dwuq = dwq_arr.reshape(Q_RANK, MLA_HEADS, HEAD_PAD)[:, :, :MLA_QK]
    dwukv = jnp.concatenate([dwk_arr.reshape(KV_RANK, MLA_HEADS, HEAD_PAD)[:, :, :MLA_NOPE],
                             dwv_arr.reshape(KV_RANK, MLA_HEADS, MLA_V)], axis=-1)
    return dwuq, dwukv


def _rope_inv_freq():
    inv = 1.0 / (ROPE_THETA ** (jnp.arange(0, MLA_ROPE, 2, dtype=F32) / MLA_ROPE))
    z = lambda n: jnp.zeros((n,), F32)
    return jnp.concatenate([z(MLA_NOPE), inv, inv, z(HEAD_PAD - MLA_QK)]).reshape(1, HEAD_PAD)


def _local_step(x, pos, tgt, small, win_arr, wq_arr, wk_arr, wv_arr, late, place=None):
    invf = _rope_inv_freq()
    cq, ckv, xph, qb, kb, vb, kt, vt, rc, rs = _in_fwd(x, pos, invf, small["attn_pre_norm"], win_arr, small["mla_q_norm"],
                                               wq_arr, small["mla_kv_norm"], wk_arr, wv_arr)
    if place is None:
        o_raw, lse = _attn_fwd_t(qb, kb, vt)
        wout, wg, wu, wd = late
    else:
        o_raw, lse, *stacks = _attn_fwd_t(qb, kb, vt, gather=late)
        wout, wg, wu, wd = [lax.dynamic_update_slice(s, l[None], (place[1], 0, 0)) for s, l in zip(stacks, late)]
        wout = wout.reshape(D_MODEL, D_MODEL)
    oh_raw, states = _hgrn_fwd(xph, small["hgrn_lb_logits"])
    h1, y1, zb, mixb = _proj_fwd(x, o_raw, oh_raw, xph, wout, small["mla_out_norm"], small["hgrn_out_norm"],
                                 small["attn_post_norm"], small["ffn_pre_norm"])
    g, up, dy2b, dh2, loss_acc, d_fpost = _ffn_fwd(zb, h1, tgt, small["ffn_post_norm"], wg, wu, wd)
    dwg, dwu, dwd, dz = _ffn_bwd(zb, g, up, dy2b, wg, wu, wd)
    ffn_grads = [] if place is None else [dwg, dwu, dwd]
    dh1, dwout, d_o, d_oh, dhg, dvec, d_fpre, d_post, d_mla, d_hg, *ffn_rs = _mid_bwd(
        dz, dh2, h1, y1, mixb, o_raw, oh_raw, xph, wout, small["ffn_pre_norm"], small["attn_post_norm"],
        small["mla_out_norm"], small["hgrn_out_norm"], swap=ffn_grads)
    if ffn_grads:
        ffn_grads = ffn_grads + [dwout.reshape(N_CHIPS, D_MODEL // N_CHIPS, D_MODEL)]
        ffn_rs += _pair_swap(ffn_grads[3:], (), "pair_swap_w_out")
    ffn_ps = _pair_sum(place, ffn_grads, ffn_rs, name="pair_sum_ffn") if ffn_grads else []
    dq, dk, dv, *ffn_ris = _attn_bwd_t(qb, kb, kt, vb, d_o, lse, dvec.reshape(lse.shape), send=ffn_ps)
    ffn_sums = _chip_sum(place, ffn_grads, ffn_rs, ffn_ris, name="chip_sum_ffn") if ffn_grads else []
    dhq, dhf, dhi, d_lbl, *ffn_final = _hgrn_bwd(xph, small["hgrn_lb_logits"], states, d_oh, fill=ffn_sums)
    dx, dwin4, dwq_arr, dwk_arr, dwv_arr, d_pre, d_qn, d_kvn = _in_bwd(
        x, dh1, cq, ckv, dq, dk, dv, dhq, dhf, dhi, dhg, rc, rs, small["attn_pre_norm"], win_arr,
        small["mla_q_norm"], wq_arr, small["mla_kv_norm"], wk_arr, wv_arr)
    dwuq, dwukv = _unarrange_grads(dwq_arr, dwk_arr, dwv_arr)
    loss = 0.5 * jnp.sum(loss_acc) * (1.0 / D_MODEL)
    grads = dict(attn_pre_norm=d_pre, w_in=dwin4, mla_q_norm=d_qn, mla_w_uq=dwuq, mla_kv_norm=d_kvn, mla_w_ukv=dwukv,
                 mla_out_norm=d_mla, hgrn_lb_logits=d_lbl, hgrn_out_norm=d_hg, w_out=dwout, attn_post_norm=d_post,
                 ffn_pre_norm=d_fpre, w_gate=dwg, w_up=dwu, w_down=dwd, ffn_post_norm=d_fpost)
    if place is None:
        return loss, dx, grads
    return loss, dx, grads, ffn_final


def _place():
    x, y, c = lax.axis_index("x"), lax.axis_index("y"), lax.axis_index("c")
    others = [(1 - x, y), (x, 1 - y), (1 - x, 1 - y)]
    return x, y, c, 2 * x + y, (x, y, 1 - c), others


def _half(ref, c, rows):
    return ref.at[pl.ds(pl.multiple_of(c * rows, 8), rows)]


def _rcopy(src, dst, send, recv, k, to):
    return pltpu.make_async_remote_copy(src_ref=src, dst_ref=dst, send_sem=send.at[k], recv_sem=recv.at[k],
                                        device_id=to, device_id_type=MESH)


class _Gather:
    def __init__(self, ins, outs, send, recv):
        self.ins, self.outs, self.send, self.recv = ins, outs, send, recv
        self.n = len(ins)
        self.halves = [r.shape[0] // 2 for r in ins]
        _, _, self.c, self.me, self.sib, self.others = _place()

    def _each(self):
        for j, (px, py) in enumerate(self.others):
            for a in range(self.n):
                yield j * self.n + a, a, 2 * px + py, (px, py, self.c)

    def sends(self):
        return [_rcopy(_half(self.ins[a], self.c, self.halves[a]), _half(self.outs[a].at[self.me], self.c, self.halves[a]),
                       self.send, self.recv, k, to) for k, a, _, to in self._each()]

    def arrivals(self):
        parts = [(k, _half(self.outs[a].at[chip], self.c, self.halves[a]), to) for k, a, chip, to in self._each()]
        return [_rcopy(p, p, self.send, self.recv, k, to) for k, p, to in parts]

    def forwards(self):
        parts = [(k, _half(self.outs[a].at[chip], self.c, self.halves[a])) for k, a, chip, _ in self._each()]
        return [_rcopy(p, p, self.send, self.recv, 3 * self.n + k, self.sib) for k, p in parts]

    def forward_arrivals(self):
        parts = [(k, _half(self.outs[a].at[chip], 1 - self.c, self.halves[a])) for k, a, chip, _ in self._each()]
        return [_rcopy(p, p, self.send, self.recv, 3 * self.n + k, self.sib) for k, p in parts]

    @staticmethod
    def out_shapes(arrs):
        return [jax.ShapeDtypeStruct((N_CHIPS,) + a.shape, a.dtype) for a in arrs]

    @staticmethod
    def semaphores(arrs):
        return [pltpu.SemaphoreType.DMA((6 * len(arrs),)), pltpu.SemaphoreType.DMA((6 * len(arrs),))]


def _gather_chips(arrs, name):
    n = len(arrs)

    def body(*refs):
        gat = _Gather(refs[:n], refs[n:2 * n], *refs[2 * n:])
        sends, forwards = gat.sends(), gat.forwards()
        for cp in sends:
            cp.start()
        for arrival, fw in zip(gat.arrivals(), forwards):
            arrival.wait_recv()
            fw.start()
        for arrival in gat.forward_arrivals():
            arrival.wait_recv()
        for cp in sends + forwards:
            cp.wait_send()

    return pl.pallas_call(body, name=name, in_specs=[ANY] * n, out_specs=[ANY] * n, out_shape=_Gather.out_shapes(arrs),
                          scratch_shapes=_Gather.semaphores(arrs))(*arrs)


def _grad_blocks(gs):
    return max(n for n in (1, 2, 3, 4) if all(g.shape[1] // 2 % (16 * n) == 0 for g in gs))


def _pair_swap_copies(g_refs, r_refs, send, recv):
    _, _, c, _, sib, _ = _place()
    copies = []
    for a, (g, r) in enumerate(zip(g_refs, r_refs)):
        h = g.shape[1] // 2
        copies.append(_rcopy(g.at[:, pl.ds(pl.multiple_of((1 - c) * h, 8), h)], r, send, recv, a, sib))
    return copies


def _half_stack_shapes(gs, dtype=None):
    return [jax.ShapeDtypeStruct((N_CHIPS, g.shape[1] // 2, g.shape[2]), dtype or g.dtype) for g in gs]


def _pair_swap(gs, wholes, name):
    n, nw = len(gs), len(wholes)

    def body(*refs):
        ins, outs, (send, recv) = refs[:n + nw], refs[n + nw:2 * (n + nw)], refs[2 * (n + nw):]
        copies = _pair_swap_copies(ins[:n], outs[:n], send, recv)
        copies += [_rcopy(ins[n + k], outs[n + k], send, recv, n + k, _place()[4]) for k in range(nw)]
        for cp in copies:
            cp.start()
        for cp in copies:
            cp.wait()

    return pl.pallas_call(
        body, name=name, in_specs=[ANY] * (n + nw), out_specs=[ANY] * (n + nw),
        out_shape=_half_stack_shapes(gs) + [jax.ShapeDtypeStruct(w.shape, w.dtype) for w in wholes],
        scratch_shapes=[pltpu.SemaphoreType.DMA((n + nw,)), pltpu.SemaphoreType.DMA((n + nw,))],
    )(*gs, *wholes)


def _pair_sum(place, gs, rs, small=None, name="pair_sum"):
    n = len(gs)
    nb = _grad_blocks(gs)

    def body(place_ref, *refs):
        g_refs, r_refs, p_refs = refs[:n], refs[n:2 * n], refs[-n - 1:-1] if small else refs[-n:]
        for a in range(n):
            p_refs[a][0] = (g_refs[a][0] + r_refs[a][0]).astype(p_refs[a].dtype)
        if small:
            @pl.when((pl.program_id(0) == 0) & (pl.program_id(1) == 0))
            def _():
                refs[-1][...] = refs[2 * n][...] + refs[2 * n + 1][...]

    in_specs, out_specs = [], []
    for g in gs:
        blk = (1, g.shape[1] // 2 // nb, g.shape[2])
        in_specs.append(pl.BlockSpec(blk, lambda i, k, p: (k, p[0] * nb + i, 0)))
    for g in gs:
        blk = (1, g.shape[1] // 2 // nb, g.shape[2])
        in_specs.append(pl.BlockSpec(blk, lambda i, k, p: (k, i, 0)))
        out_specs.append(pl.BlockSpec(blk, lambda i, k, p: (k, i, 0)))
    out_shape = _half_stack_shapes(gs, BF16)
    if small:
        sm_spec = pl.BlockSpec(small[0].shape, lambda i, k, p: (0, 0))
        in_specs += [sm_spec, sm_spec]
        out_specs.append(sm_spec)
        out_shape.append(jax.ShapeDtypeStruct(small[0].shape, F32))
    return pl.pallas_call(
        body, name=name,
        grid_spec=pltpu.PrefetchScalarGridSpec(num_scalar_prefetch=1, grid=(nb, N_CHIPS), in_specs=in_specs,
                                               out_specs=out_specs),
        out_shape=out_shape,
        compiler_params=_params(("arbitrary", "arbitrary")),
    )(place, *gs, *rs, *(small or ()))


def _chip_swap_copies(p_refs, ri_refs, send, recv):
    _, _, c, _, _, others = _place()
    n = len(p_refs)
    return [_rcopy(p_refs[a].at[2 * px + py], ri_refs[a].at[j], send, recv, j * n + a, (px, py, c))
            for j, (px, py) in enumerate(others) for a in range(n)]


def _chip_swap_shapes(ps):
    return [jax.ShapeDtypeStruct((3,) + p.shape[1:], p.dtype) for p in ps]


def _chip_swap(ps, pair):
    n = len(ps)

    def body(*refs):
        start, finish = _chip_swap_plan(refs[:n], refs[n], refs[n + 1:2 * n + 1], refs[2 * n + 1], *refs[2 * n + 2:])
        start()
        finish()

    return pl.pallas_call(
        body, name="chip_swap", in_specs=[ANY] * (n + 1), out_specs=[ANY] * (n + 1),
        out_shape=_chip_swap_out_shapes(ps, pair), scratch_shapes=_chip_swap_semaphores(n),
    )(*ps, pair)


def _chip_swap_plan(p_refs, pair_ref, ri_refs, sm4_ref, send, recv, lsem):
    n = len(p_refs)
    hs = SMALL_ROWS // 2
    x, y, c, me, sib, others = _place()
    local = pltpu.make_async_copy(pair_ref, sm4_ref.at[me], lsem.at[0])
    copies = _chip_swap_copies(p_refs, ri_refs, send, recv)
    arrivals = list(copies)
    for j, (px, py) in enumerate(others):
        copies.append(_rcopy(_half(pair_ref, c, hs), _half(sm4_ref.at[me], c, hs), send, recv, 3 * n + j, (px, py, c)))
        part = _half(sm4_ref.at[2 * px + py], c, hs)
        arrivals.append(_rcopy(part, part, send, recv, 3 * n + j, (px, py, c)))

    def start():
        local.start()
        for cp in copies:
            cp.start()

    def finish():
        for arrival in arrivals:
            arrival.wait_recv()
        for cp in copies:
            cp.wait_send()
        local.wait()

    return start, finish


def _chip_swap_out_shapes(ps, pair):
    return _chip_swap_shapes(ps) + [jax.ShapeDtypeStruct((N_CHIPS,) + pair.shape, pair.dtype)]


def _chip_swap_semaphores(n):
    k = 3 * (n + 1)
    return [pltpu.SemaphoreType.DMA((k,)), pltpu.SemaphoreType.DMA((k,)), pltpu.SemaphoreType.DMA((1,))]


def _chip_sum(place, gs, rs, ris, name="chip_sum"):
    n = len(gs)
    nb = _grad_blocks(gs)

    def body(place_ref, *refs):
        g_refs, r_refs, ri_refs, o_refs = refs[:n], refs[n:2 * n], refs[2 * n:3 * n], refs[3 * n:]
        for a in range(n):
            ri = ri_refs[a]
            o_refs[a][...] = (g_refs[a][0] + r_refs[a][0]) + ri[0].astype(F32) + ri[1].astype(F32) + ri[2].astype(F32)

    in_specs, out_specs, out_shape = [], [], []
    for g in gs:
        blk = (1, g.shape[1] // 2 // nb, g.shape[2])
        in_specs.append(pl.BlockSpec(blk, lambda i, p: (p[1], p[0] * nb + i, 0)))
    for g in gs:
        blk = (1, g.shape[1] // 2 // nb, g.shape[2])
        in_specs.append(pl.BlockSpec(blk, lambda i, p: (p[1], i, 0)))
    for g in gs:
        rb = g.shape[1] // 2 // nb
        in_specs.append(pl.BlockSpec((3, rb, g.shape[2]), lambda i, p: (0, i, 0)))
        out_specs.append(pl.BlockSpec((rb, g.shape[2]), lambda i, p: (p[0] * nb + i, 0)))
        out_shape.append(jax.ShapeDtypeStruct(g.shape[1:], F32))
    return pl.pallas_call(
        body, name=name,
        grid_spec=pltpu.PrefetchScalarGridSpec(num_scalar_prefetch=1, grid=(nb,), in_specs=in_specs, out_specs=out_specs),
        out_shape=out_shape,
        compiler_params=_params(("arbitrary",)),
    )(place, *gs, *rs, *ris)


def _pair_fill_copies(g_refs, send, recv):
    _, _, c, _, sib, _ = _place()
    copies, waits = [], []
    for a, g in enumerate(g_refs):
        h = g.shape[0] // 2
        mine, theirs = _half(g, c, h), _half(g, 1 - c, h)
        copies.append(_rcopy(mine, mine, send, recv, a, sib))
        waits.append(_rcopy(theirs, theirs, send, recv, a, sib))
    return copies, waits


def _pair_fill(gfs, sm4):
    n = len(gfs)
    hs = SMALL_ROWS // 2

    def body(*refs):
        g_refs, sm4_ref = refs[n + 1:2 * n + 1], refs[2 * n + 1]
        send, recv = refs[2 * n + 2:]
        x, y, c, me, sib, others = _place()
        copies, waits = _pair_fill_copies(g_refs, send, recv)
        for j, (px, py) in enumerate(others):
            chip = 2 * px + py
            mine, theirs = _half(sm4_ref.at[chip], c, hs), _half(sm4_ref.at[chip], 1 - c, hs)
            copies.append(pltpu.make_async_remote_copy(src_ref=mine, dst_ref=mine, send_sem=send.at[n + j],
                                                       recv_sem=recv.at[n + j], device_id=sib, device_id_type=MESH))
            waits.append(pltpu.make_async_remote_copy(src_ref=theirs, dst_ref=theirs, send_sem=send.at[n + j],
                                                      recv_sem=recv.at[n + j], device_id=sib, device_id_type=MESH))
        for cp in copies:
            cp.start()
        for w in waits:
            w.wait_recv()
        for cp in copies:
            cp.wait_send()

    return pl.pallas_call(
        body, name="pair_fill", in_specs=[ANY] * (n + 1), out_specs=[ANY] * (n + 1),
        out_shape=[jax.ShapeDtypeStruct(g.shape, g.dtype) for g in gfs] + [jax.ShapeDtypeStruct(sm4.shape, sm4.dtype)],
        input_output_aliases={i: i for i in range(n + 1)},
        scratch_shapes=[pltpu.SemaphoreType.DMA((n + 3,)), pltpu.SemaphoreType.DMA((n + 3,))],
    )(*gfs, sm4)


def _adamw_math(w, g, m, v):
    m = ADAM_B1 * m + (1.0 - ADAM_B1) * g
    v = ADAM_B2 * v + (1.0 - ADAM_B2) * (g * g)
    m_hat = m / (1.0 - ADAM_B1 ** ADAM_STEP)
    v_hat = v / (1.0 - ADAM_B2 ** ADAM_STEP)
    return -ADAM_LR * (m_hat / (jnp.sqrt(v_hat) + ADAM_EPS) + ADAM_WD * w), m, v


def _adamw(items, steps, name):
    n = len(items)

    def body(*refs):
        for a in range(n):
            g = refs[4 * a + 1][...]
            d, mo, vo = _adamw_math(refs[4 * a][...], g, refs[4 * a + 2][...], refs[4 * a + 3][...])
            for out, val in zip(refs[4 * n + 4 * a:4 * n + 4 * a + 4], (g, d, mo, vo)):
                out[...] = val

    spec = lambda w: pl.BlockSpec((w.shape[0] // steps, w.shape[1]), lambda i: (i, 0))
    flat = pl.pallas_call(
        body, name=name, grid=(steps,), in_specs=[spec(it[0]) for it in items for _ in range(4)],
        out_specs=[spec(it[0]) for it in items for _ in range(4)],
        out_shape=[jax.ShapeDtypeStruct(it[0].shape, F32) for it in items for _ in range(4)],
        compiler_params=_params(("arbitrary",)),
    )(*[a for it in items for a in it])
    return [flat[4 * a:4 * a + 4] for a in range(n)]


def _adamw_small(sm4, wmv):
    views = SMALL_VIEWS[:-1]
    n = len(views)

    def body(sm4_ref, *refs):
        g_all = ((sm4_ref[0] + sm4_ref[1]) + sm4_ref[2]) + sm4_ref[3]
        for a, (name, rows, cols) in enumerate(views):
            row = SMALL_OFFSETS[name]
            g = g_all[row:row + rows, :cols]
            d, mo, vo = _adamw_math(refs[3 * a][...], g, refs[3 * a + 1][...], refs[3 * a + 2][...])
            for out, val in zip(refs[3 * n + 4 * a:3 * n + 4 * a + 4], (g, d, mo, vo)):
                out[...] = val
        row = SMALL_OFFSETS["loss"]
        refs[-1][...] = g_all[row:row + 1, :128]

    flat = pl.pallas_call(
        body, name="adamw_small",
        out_shape=[jax.ShapeDtypeStruct((rows, cols), F32) for _, rows, cols in views for _ in range(4)]
        + [jax.ShapeDtypeStruct((1, 128), F32)],
        compiler_params=pltpu.CompilerParams(vmem_limit_bytes=VMEM_LIMIT),
    )(sm4, *[a for t in wmv for a in t])
    return [flat[4 * a:4 * a + 4] for a in range(n)] + [flat[-1]]


SMALL_NAMES = ("attn_pre_norm", "mla_q_norm", "mla_kv_norm", "mla_w_ukv", "mla_out_norm", "hgrn_lb_logits",
               "hgrn_out_norm", "attn_post_norm", "ffn_pre_norm", "ffn_post_norm")
BIG_NAMES = ("w_in", "mla_w_uq", "w_out", "w_gate", "w_up", "w_down")
WEIGHT_NAMES = ("attn_pre_norm", "w_in", "mla_q_norm", "mla_w_uq", "mla_kv_norm", "mla_w_ukv", "mla_out_norm",
                "hgrn_lb_logits", "hgrn_out_norm", "w_out", "attn_post_norm", "ffn_pre_norm", "w_gate", "w_up", "w_down",
                "ffn_post_norm")


UQ_COMM_SHAPE = (192, 384)


def _pack_small(vals):
    parts, row = [], 0
    for name, rows, cols in sorted(SMALL_VIEWS, key=lambda view: SMALL_OFFSETS[view[0]]):
        assert SMALL_OFFSETS[name] == row
        parts.append(jnp.pad(vals[name].reshape(rows, cols), ((0, 0), (0, D_MODEL - cols))))
        row += rows
    parts.append(jnp.zeros((SMALL_ROWS - row, D_MODEL), F32))
    return jnp.concatenate(parts, axis=0)


def kernel(x, positions, attn_pre_norm, w_in, mla_q_norm, mla_w_uq, mla_kv_norm, mla_w_ukv, mla_out_norm, hgrn_lb_logits, hgrn_out_norm, w_out, attn_post_norm, ffn_pre_norm, w_gate, w_up, w_down, ffn_post_norm, loss_target, m_attn_pre_norm, m_w_in, m_mla_q_norm, m_mla_w_uq, m_mla_kv_norm, m_mla_w_ukv, m_mla_out_norm, m_hgrn_lb_logits, m_hgrn_out_norm, m_w_out, m_attn_post_norm, m_ffn_pre_norm, m_w_gate, m_w_up, m_w_down, m_ffn_post_norm, v_attn_pre_norm, v_w_in, v_mla_q_norm, v_mla_w_uq, v_mla_kv_norm, v_mla_w_ukv, v_mla_out_norm, v_hgrn_lb_logits, v_hgrn_out_norm, v_w_out, v_attn_post_norm, v_ffn_pre_norm, v_w_gate, v_w_up, v_w_down, v_ffn_post_norm):
    args = locals()
    W = {n: args[n] for n in WEIGHT_NAMES}
    M = {n: args["m_" + n] for n in WEIGHT_NAMES}
    V = {n: args["v_" + n] for n in WEIGHT_NAMES}
    T = x.shape[1]
    cx, cy, cc = lax.axis_index("x"), lax.axis_index("y"), lax.axis_index("c")

    win_rows = D_IN // N_CHIPS
    shard2d = {"w_in": (win_rows, D_MODEL), "mla_w_uq": (Q_RANK // N_CHIPS, MLA_HEADS * MLA_QK),
               "w_out": (D_MODEL // N_CHIPS, D_MODEL), "w_gate": (FF_SHARD, D_MODEL), "w_up": (FF_SHARD, D_MODEL),
               "w_down": (FF_SHARD, D_MODEL)}
    transposed = ("w_in", "w_gate", "w_up")
    to2d = lambda n, a: a[0].T if n in transposed else a.reshape(shard2d[n])
    from2d = lambda n, t: t.T[None] if n in transposed else t.reshape(W[n].shape)
    me = 2 * cx + cy
    place = jnp.stack([cc, me]).astype(jnp.int32)
    local_b = [to2d(n, W[n]).astype(BF16) for n in BIG_NAMES]
    local_b[0] = jnp.pad(local_b[0], ((0, WIN_COMM_SHAPE[1] - win_rows), (0, 0)))
    stacks = _gather_chips(local_b[:2], "gather_weights")
    win4, wuq4 = [lax.dynamic_update_slice(s, l[None], (me, 0, 0)) for s, l in zip(stacks, local_b)]
    win_t = win4[:, :win_rows].reshape(D_IN, D_MODEL)
    wuq_full = wuq4.reshape(Q_RANK, MLA_HEADS, MLA_QK)
    win_arr, wq_arr, wk_arr, wv_arr = _arrange_weights(win_t, wuq_full, mla_w_ukv[0].astype(BF16))
    small = {n: W[n][0] if n == "mla_w_ukv" else W[n].reshape(-1, W[n].shape[-1]) for n in SMALL_NAMES}

    loss_local, dx, grads, ffn_final = _local_step(x[0], positions.reshape(T, 1), loss_target[0], small, win_arr,
                                                           wq_arr, wk_arr, wv_arr, local_b[2:], place)

    gs = [grads["w_in"], grads["mla_w_uq"].reshape((N_CHIPS,) + UQ_COMM_SHAPE)]
    sm = _pack_small({**grads, "loss": loss_local})
    *rs, ssib = _pair_swap(gs, (sm,), "pair_swap")
    *ps, pair = _pair_sum(place, gs, rs, small=(sm, ssib))
    ffn_names, rest_names = BIG_NAMES[3:], BIG_NAMES[:2]
    g2d = dict(zip(ffn_names + BIG_NAMES[2:3], ffn_final))
    adam_in = lambda names_: [(to2d(n, W[n]), g2d[n], to2d(n, M[n]), to2d(n, V[n])) for n in names_]
    updates = dict(zip(ffn_names, _adamw(adam_in(ffn_names), 8, "adamw_ffn")))
    updates.update(zip(BIG_NAMES[2:3], _adamw(adam_in(BIG_NAMES[2:3]), 8, "adamw_w_out")))
    *ris, sm4 = _chip_swap(ps, pair)
    *gfin, smf = _pair_fill(_chip_sum(place, gs, rs, ris), sm4)

    g2d.update({n: gfin[k].reshape((-1,) + shard2d[n][1:]) for k, n in enumerate(rest_names)})
    updates.update(zip(rest_names, _adamw(adam_in(rest_names), 3, "adamw_w_in")))
    G, DW, NM, NV = {}, {}, {}, {}
    for n, outs in updates.items():
        G[n], DW[n], NM[n], NV[n] = (from2d(n, t) for t in outs)
    view2d = lambda n, a: a.reshape(next((r, c) for name, r, c in SMALL_VIEWS if name == n))
    *res, loss_row = _adamw_small(smf, [tuple(view2d(n, t[n]) for t in (W, M, V)) for n in SMALL_NAMES])
    for n, outs in zip(SMALL_NAMES, res):
        G[n], DW[n], NM[n], NV[n] = (t.reshape(W[n].shape) for t in outs)
    loss = loss_row[0, 0]
    return (loss, dx[None], *[G[n] for n in WEIGHT_NAMES], *[DW[n] for n in WEIGHT_NAMES],
            *[NM[n] for n in WEIGHT_NAMES], *[NV[n] for n in WEIGHT_NAMES])
```

```python
import jax
import jax.numpy as jnp
from jax import lax
from jax.experimental import pallas as pl
from jax.experimental.pallas import tpu as pltpu

F32 = jnp.float32
BF16 = jnp.bfloat16
MXU_DTYPE = BF16

D_MODEL = 1024
MLA_HEADS = 8
MLA_NOPE = 64
MLA_ROPE = 32
MLA_V = 64
MLA_QK = MLA_NOPE + MLA_ROPE
Q_RANK = 384
KV_RANK = 128
MLA_WIDTH = MLA_HEADS * MLA_V
HEAD_PAD = 128
HGRN_HEADS = 4
HGRN_DIM = 128
HGRN_WIDTH = HGRN_HEADS * HGRN_DIM
CHUNK = 64
SUB = 16
HGRN_CPI = 4
D_IN = Q_RANK + KV_RANK + MLA_ROPE + 4 * HGRN_WIDTH
D_IN_ARR = Q_RANK + KV_RANK + HEAD_PAD + 4 * HGRN_WIDTH
D_FF = 2816
N_CHIPS = 4
FF_SHARD = D_FF // N_CHIPS
EPS = 1e-6
ROPE_THETA = 10000.0
ATTN_SCALE = MLA_QK ** -0.5
ATTN_SCALE_LOG2 = ATTN_SCALE * 1.4426950408889634
NEG_BIG = -1e30

ADAM_LR = 0.001
ADAM_B1 = 0.9
ADAM_B2 = 0.999
ADAM_EPS = 1e-08
ADAM_WD = 0.01
ADAM_STEP = 10

VMEM_LIMIT = 56 * 1024 * 1024
FFN_BWD_VMEM = 62 * 1024 * 1024

SMALL_VIEWS = (("attn_pre_norm", 1, 1024), ("mla_q_norm", 1, 384), ("mla_kv_norm", 1, 128), ("mla_w_ukv", 128, 1024),
               ("mla_out_norm", 1, 512), ("hgrn_lb_logits", 2, 512), ("hgrn_out_norm", 1, 512),
               ("attn_post_norm", 1, 1024), ("ffn_pre_norm", 1, 1024), ("ffn_post_norm", 1, 1024), ("loss", 1, 1))
ROW_TILE = 8


def _small_layout():
    offsets, row = {}, 0
    for whole in (True, False):
        for name, rows, _ in SMALL_VIEWS:
            if (rows % ROW_TILE == 0) == whole:
                offsets[name] = row
                row += rows
    return offsets, -(-row // (2 * ROW_TILE)) * 2 * ROW_TILE


SMALL_OFFSETS, SMALL_ROWS = _small_layout()

MESH = pl.DeviceIdType.MESH
ANY = pl.BlockSpec(memory_space=pl.ANY)


def _dot(a, b, dims, exact):
    if exact:
        return lax.dot_general(a.astype(F32), b.astype(F32), (dims, ((), ())), precision=lax.Precision.HIGH,
                               preferred_element_type=F32)
    return lax.dot_general(a.astype(MXU_DTYPE), b.astype(MXU_DTYPE), (dims, ((), ())), preferred_element_type=F32)


def _mm(a, b, exact=False):
    return _dot(a, b, ((1,), (0,)), exact)


def _mm_nt(a, b, exact=False):
    return _dot(a, b, ((1,), (1,)), exact)


def _mm_tn(a, b, exact=False):
    return _dot(a, b, ((0,), (0,)), exact)


def _rms_fwd(x, w):
    r = lax.rsqrt(jnp.mean(x * x, axis=-1, keepdims=True) + EPS)
    xn = x * r
    return xn * w, xn, r


def _rms_bwd(dy, xn, r, w):
    dxn = dy * w
    dx = r * (dxn - xn * jnp.mean(dxn * xn, axis=-1, keepdims=True))
    dw = jnp.sum(dy * xn, axis=0, keepdims=True)
    return dx, dw


def _group_sums(v, gs):
    t, n = v.shape
    lane = lax.broadcasted_iota(jnp.int32, (t, 128), 1)
    out = []
    for p in range(n // 128):
        vb = v[:, 128 * p:128 * (p + 1)]
        if gs == 128:
            out.append(jnp.sum(vb, axis=-1, keepdims=True))
        else:
            out.append(jnp.sum(jnp.where(lane < 64, vb, 0.0), axis=-1, keepdims=True))
            out.append(jnp.sum(jnp.where(lane >= 64, vb, 0.0), axis=-1, keepdims=True))
    return out


def _group_bcast(sums, gs, t):
    lane = lax.broadcasted_iota(jnp.int32, (t, 128), 1)
    if gs == 128:
        return jnp.concatenate([jnp.broadcast_to(s, (t, 128)) for s in sums], axis=-1)
    return jnp.concatenate([jnp.where(lane < 64, sums[2 * p], sums[2 * p + 1]) for p in range(len(sums) // 2)],
                           axis=-1)


def _grms_fwd(x, w, gs):
    t = x.shape[0]
    r = lax.rsqrt(_group_bcast(_group_sums(x * x, gs), gs, t) * (1.0 / gs) + EPS)
    xn = x * r
    return xn * w, xn, r


def _grms_bwd(dy, xn, r, w, gs):
    t = dy.shape[0]
    dxn = dy * w
    dx = r * (dxn - xn * (_group_bcast(_group_sums(dxn * xn, gs), gs, t) * (1.0 / gs)))
    dw = jnp.sum(dy * xn, axis=0, keepdims=True)
    return dx, dw


def _rope_tables(c_tab, s_tab):
    lane = lax.broadcasted_iota(jnp.int32, c_tab.shape, 1)
    first = (lane >= MLA_NOPE) & (lane < MLA_NOPE + MLA_ROPE // 2)
    second = (lane >= MLA_NOPE + MLA_ROPE // 2) & (lane < MLA_QK)
    return c_tab, jnp.where(first, -s_tab, 0.0), jnp.where(second, s_tab, 0.0)


def _rope(v, c, sa, sb):
    return v * c + pltpu.roll(v, HEAD_PAD - MLA_ROPE // 2, 1) * sa + pltpu.roll(v, MLA_ROPE // 2, 1) * sb


def _rope_bwd(d, c, sa, sb):
    return d * c - pltpu.roll(d, HEAD_PAD - MLA_ROPE // 2, 1) * sa - pltpu.roll(d, MLA_ROPE // 2, 1) * sb


def _params(sem, vmem=VMEM_LIMIT):
    return pltpu.CompilerParams(dimension_semantics=sem, vmem_limit_bytes=vmem)


def _in_fwd(x, pos, invf, w_pre, win, qnw, wq, kvnw, wk, wv, tt=512):
    T = x.shape[0]

    def body(x_ref, pos_ref, invf_ref, wpre_ref, win_ref, qnw_ref, wq_ref, kvnw_ref, wk_ref, wv_ref,
             cq_ref, ckv_ref, xph_ref, q_ref, k_ref, v_ref, kt_ref, vt_ref, rc_ref, rs_ref):
        u, _, _ = _rms_fwd(x_ref[...], wpre_ref[...])
        lo = Q_RANK + KV_RANK + HEAD_PAD
        xp = _mm_nt(u, win_ref[:lo, :])
        xph_ref[...] = _mm_nt(u, win_ref[lo:, :])
        cq = xp[:, :Q_RANK]
        ckv = xp[:, Q_RANK:Q_RANK + KV_RANK]
        kr = xp[:, Q_RANK + KV_RANK:]
        cq_ref[...] = cq
        ckv_ref[...] = ckv
        ang = pos_ref[...].astype(F32) * invf_ref[...]
        c_tab = jnp.cos(ang)
        s_tab = jnp.sin(ang)
        rc_ref[...] = c_tab
        rs_ref[...] = s_tab
        c, sa, sb = _rope_tables(c_tab, s_tab)
        qn, _, _ = _rms_fwd(cq, qnw_ref[...])
        q = _mm(qn, wq_ref[...])
        kvn, _, _ = _rms_fwd(ckv, kvnw_ref[...])
        kn = _mm(kvn, wk_ref[...])
        v = _mm(kvn, wv_ref[...])
        v_ref[...] = v.astype(v_ref.dtype)
        vt_ref[...] = v.T.astype(vt_ref.dtype)
        krr = _rope(kr, c, sa, sb)
        for h in range(MLA_HEADS):
            sl = slice(HEAD_PAD * h, HEAD_PAD * (h + 1))
            q_ref[:, sl] = (_rope(q[:, sl], c, sa, sb) * ATTN_SCALE_LOG2).astype(q_ref.dtype)
            kh = kn[:, sl] + krr
            k_ref[:, sl] = kh.astype(k_ref.dtype)
            kt_ref[sl, :] = kh.T.astype(kt_ref.dtype)

    row = lambda w: pl.BlockSpec((tt, w), lambda i: (i, 0))
    full = lambda a: pl.BlockSpec(a.shape, lambda i: (0,) * a.ndim)
    qk_w = MLA_HEADS * HEAD_PAD
    return pl.pallas_call(
        body, name="in_fwd", grid=(T // tt,),
        in_specs=[row(D_MODEL), row(1), full(invf), full(w_pre), full(win), full(qnw), full(wq), full(kvnw),
                  full(wk), full(wv)],
        out_specs=[row(Q_RANK), row(KV_RANK), row(4 * HGRN_WIDTH), row(qk_w), row(qk_w), row(MLA_WIDTH),
                   pl.BlockSpec((qk_w, tt), lambda i: (0, i)), pl.BlockSpec((MLA_WIDTH, tt), lambda i: (0, i)),
                   row(HEAD_PAD), row(HEAD_PAD)],
        out_shape=[jax.ShapeDtypeStruct((T, Q_RANK), F32), jax.ShapeDtypeStruct((T, KV_RANK), F32),
                   jax.ShapeDtypeStruct((T, 4 * HGRN_WIDTH), F32), jax.ShapeDtypeStruct((T, qk_w), MXU_DTYPE),
                   jax.ShapeDtypeStruct((T, qk_w), MXU_DTYPE), jax.ShapeDtypeStruct((T, MLA_WIDTH), MXU_DTYPE),
                   jax.ShapeDtypeStruct((qk_w, T), MXU_DTYPE), jax.ShapeDtypeStruct((MLA_WIDTH, T), MXU_DTYPE),
                   jax.ShapeDtypeStruct((T, HEAD_PAD), F32), jax.ShapeDtypeStruct((T, HEAD_PAD), F32)],
        compiler_params=_params(("arbitrary",)),
    )(x, pos, invf, w_pre, win, qnw, wq, kvnw, wk, wv)


def _attn_fwd_t(qb, kb, vt, gather=(), tq=256, hps=8):
    T = qb.shape[0]
    nq = T // tq
    ng = len(gather)
    steps = (MLA_HEADS // hps) * nq
    pass_on = steps - 3

    def body(q_ref, k_ref, vt_ref, *rest):
        o_ref, lse_ref = rest[ng:ng + 2]
        acc_scr = rest[2 * ng + 2]
        qi = pl.program_id(1)
        step_no = pl.program_id(0) * nq + qi
        if ng:
            gat = _Gather(rest[:ng], rest[ng + 2:2 * ng + 2], *rest[2 * ng + 3:])

            @pl.when(step_no == 0)
            def _():
                for cp in gat.sends():
                    cp.start()

            @pl.when(step_no == pass_on)
            def _():
                for arrival in gat.arrivals():
                    arrival.wait_recv()
                for cp in gat.forwards():
                    cp.start()

        heads = [slice(HEAD_PAD * a, HEAD_PAD * (a + 1)) for a in range(hps)]
        acc_scr[...] = jnp.zeros_like(acc_scr)

        def step(j, carry, masked):
            start = pl.multiple_of(j * tq, tq)
            scores = [_mm_nt(k_ref[pl.ds(start, tq), heads[a]], q_ref[:, heads[a]]) for a in range(hps)]
            new, probs, alphas = [], [], []
            for a in range(hps):
                m, l = carry[a]
                s = scores[a]
                if masked:
                    kk = lax.broadcasted_iota(jnp.int32, (tq, tq), 0)
                    qq = lax.broadcasted_iota(jnp.int32, (tq, tq), 1)
                    s = jnp.where(kk <= qq, s, NEG_BIG)
                m_new = jnp.maximum(m, jnp.max(s, axis=0, keepdims=True))
                alpha = jnp.exp2(m - m_new)
                p = jnp.exp2(s - m_new)
                l = l * alpha + jnp.sum(p, axis=0, keepdims=True)
                new.append((m_new, l))
                probs.append(p.astype(MXU_DTYPE))
                alphas.append(alpha)
                if a % 2:
                    pr = a // 2
                    vtj = vt_ref[2 * MLA_V * pr:2 * MLA_V * (pr + 1), pl.ds(start, tq)]
                    none = jnp.zeros((MLA_V, tq), vtj.dtype)
                    pv = (_mm(jnp.concatenate([vtj[:MLA_V], none], axis=0), probs[a - 1])
                          + _mm(jnp.concatenate([none, vtj[MLA_V:]], axis=0), probs[a]))
                    acc_scr[pr] = acc_scr[pr] * jnp.where(row < MLA_V, alphas[a - 1], alphas[a]) + pv
            return tuple(new)

        row = lax.broadcasted_iota(jnp.int32, (2 * MLA_V, tq), 0)
        init = tuple((jnp.full((1, tq), NEG_BIG, F32), jnp.zeros((1, tq), F32)) for _ in range(hps))
        carry = lax.fori_loop(0, qi, lambda j, c: step(j, c, False), init)
        carry = step(qi, carry, True)
        for pr in range(hps // 2):
            (m0, l0), (m1, l1) = carry[2 * pr], carry[2 * pr + 1]
            ot = acc_scr[pr] / jnp.where(row < MLA_V, l0, l1)
            o_ref[:, 2 * MLA_V * pr:2 * MLA_V * (pr + 1)] = ot.T
            lse_ref[pr, 0:1, :] = m0 + jnp.log2(l0)
            lse_ref[pr, 1:2, :] = m1 + jnp.log2(l1)

        if ng:
            @pl.when(step_no == steps - 1)
            def _():
                for arrival in gat.forward_arrivals():
                    arrival.wait_recv()
                for cp in gat.sends() + gat.forwards():
                    cp.wait_send()

    return pl.pallas_call(
        body, name="attn_fwd", grid=(MLA_HEADS // hps, nq),
        in_specs=[pl.BlockSpec((tq, hps * HEAD_PAD), lambda g, i: (i, g)),
                  pl.BlockSpec((T, hps * HEAD_PAD), lambda g, i: (0, g)),
                  pl.BlockSpec((hps * MLA_V, T), lambda g, i: (g, 0))] + [ANY] * ng,
        out_specs=[pl.BlockSpec((tq, hps * MLA_V), lambda g, i: (i, g)),
                   pl.BlockSpec((hps // 2, 2, tq), lambda g, i: (g, 0, i))] + [ANY] * ng,
        out_shape=[jax.ShapeDtypeStruct((T, MLA_WIDTH), F32), jax.ShapeDtypeStruct((MLA_HEADS // 2, 2, T), F32)]
        + _Gather.out_shapes(gather),
        scratch_shapes=[pltpu.VMEM((hps // 2, 2 * MLA_V, tq), F32)] + (_Gather.semaphores(gather) if ng else []),
        compiler_params=_params(("arbitrary", "arbitrary")),
    )(qb, kb, vt, *gather)


def _attn_bwd_t(qb, kb, kt, vb, dob, lse, dvec, send=(), tq=512, hps=4):
    T = qb.shape[0]
    nq = T // tq
    ns = len(send)
    steps = (MLA_HEADS // hps) * nq

    def body(q_ref, k_ref, kt_ref, v_ref, do_ref, lse_ref, d_ref, *rest):
        dqt_ref, dk_ref, dv_ref = rest[ns:ns + 3]
        va_scr, dv_scr = rest[2 * ns + 3:2 * ns + 5]
        j = pl.program_id(1)
        step_no = pl.program_id(0) * nq + j
        if ns:
            @pl.when(step_no == 0)
            def _():
                for cp in _chip_swap_copies(rest[:ns], rest[ns + 3:2 * ns + 3], *rest[2 * ns + 5:]):
                    cp.start()

        @pl.when(j == 0)
        def _():
            dqt_ref[...] = jnp.zeros_like(dqt_ref)

        lane = lax.broadcasted_iota(jnp.int32, (tq, 2 * MLA_V), 1)
        heads = [slice(HEAD_PAD * a, HEAD_PAD * (a + 1)) for a in range(hps)]
        pairs = [slice(2 * MLA_V * p, 2 * MLA_V * (p + 1)) for p in range(hps // 2)]
        for pr in range(hps // 2):
            vpair = v_ref[:, pairs[pr]]
            va_scr[2 * pr] = jnp.where(lane < MLA_V, vpair, jnp.zeros_like(vpair))
            va_scr[2 * pr + 1] = jnp.where(lane >= MLA_V, vpair, jnp.zeros_like(vpair))
        dk_ref[...] = jnp.zeros_like(dk_ref)
        dv_scr[...] = jnp.zeros_like(dv_scr)

        def step(i, masked):
            start = pl.multiple_of(i * tq, tq)
            rows = pl.ds(start, tq)
            scores = [_mm_nt(k_ref[:, heads[a]], q_ref[rows, heads[a]]) for a in range(hps)]
            dps = [_mm_nt(va_scr[a], do_ref[rows, pairs[a // 2]]) for a in range(hps)]
            for a in range(hps):
                pr, r = a // 2, a % 2
                p = jnp.exp2(scores[a] - lse_ref[pr, r:r + 1, rows])
                if masked:
                    kk = lax.broadcasted_iota(jnp.int32, (tq, tq), 0)
                    qq = lax.broadcasted_iota(jnp.int32, (tq, tq), 1)
                    p = jnp.where(kk <= qq, p, 0.0)
                ds = p * (dps[a] - d_ref[pr, r:r + 1, rows])
                dv_scr[a] += _mm(p, do_ref[rows, pairs[pr]])
                dk_ref[:, heads[a]] += _mm(ds, q_ref[rows, heads[a]])
                dqt_ref[heads[a], rows] += _mm(kt_ref[heads[a], :], ds)

        def loop_body(i, _):
            step(i, False)
            return 0

        step(j, True)
        lax.fori_loop(j + 1, nq, loop_body, 0)
        for pr in range(hps // 2):
            dv_ref[:, pairs[pr]] = jnp.where(lane < MLA_V, dv_scr[2 * pr], dv_scr[2 * pr + 1])
        dk_ref[...] = dk_ref[...] * (ATTN_SCALE / ATTN_SCALE_LOG2)

        if ns:
            @pl.when(step_no == steps - 1)
            def _():
                for cp in _chip_swap_copies(rest[:ns], rest[ns + 3:2 * ns + 3], *rest[2 * ns + 5:]):
                    cp.wait()

    stat = pl.BlockSpec((hps // 2, 2, T), lambda g, j: (g, 0, 0))
    return pl.pallas_call(
        body, name="attn_bwd", grid=(MLA_HEADS // hps, nq),
        in_specs=[pl.BlockSpec((T, hps * HEAD_PAD), lambda g, j: (0, g)),
                  pl.BlockSpec((tq, hps * HEAD_PAD), lambda g, j: (j, g)),
                  pl.BlockSpec((hps * HEAD_PAD, tq), lambda g, j: (g, j)),
                  pl.BlockSpec((tq, hps * MLA_V), lambda g, j: (j, g)),
                  pl.BlockSpec((T, hps * MLA_V), lambda g, j: (0, g)), stat, stat] + [ANY] * ns,
        out_specs=[pl.BlockSpec((hps * HEAD_PAD, T), lambda g, j: (g, 0)),
                   pl.BlockSpec((tq, hps * HEAD_PAD), lambda g, j: (j, g)),
                   pl.BlockSpec((tq, hps * MLA_V), lambda g, j: (j, g))] + [ANY] * ns,
        out_shape=[jax.ShapeDtypeStruct((MLA_HEADS * HEAD_PAD, T), F32),
                   jax.ShapeDtypeStruct((T, MLA_HEADS * HEAD_PAD), F32),
                   jax.ShapeDtypeStruct((T, MLA_WIDTH), F32)] + _chip_swap_shapes(send),
        scratch_shapes=[pltpu.VMEM((hps, tq, 2 * MLA_V), vb.dtype), pltpu.VMEM((hps, tq, 2 * MLA_V), F32)]
        + ([pltpu.SemaphoreType.DMA((3 * ns,)), pltpu.SemaphoreType.DMA((3 * ns,))] if ns else []),
        compiler_params=_params(("arbitrary", "arbitrary")),
    )(qb, kb, kt, vb, dob, lse, dvec, *send)


def _cumsum_rows(x):
    n = x.shape[0]
    row = lax.broadcasted_iota(jnp.int32, x.shape, 0)
    s = 1
    while s < n:
        x = x + jnp.where(row >= s, pltpu.roll(x, s, 0), 0.0)
        s *= 2
    return x


def _rev_cumsum_rows(x):
    n = x.shape[0]
    row = lax.broadcasted_iota(jnp.int32, x.shape, 0)
    s = 1
    while s < n:
        x = x + jnp.where(row < n - s, pltpu.roll(x, n - s, 0), 0.0)
        s *= 2
    return x


def _lb_from_logits(l):
    l0, l1 = l[0:1, :], l[1:2, :]
    m = jnp.maximum(l0, l1)
    e0, e1 = jnp.exp(l0 - m), jnp.exp(l1 - m)
    return e0 / (e0 + e1)


def _hgrn_gates(hq, hf, lb):
    sig_f = jax.nn.sigmoid(hf)
    f = lb + (1.0 - lb) * sig_f
    sig_q = jax.nn.sigmoid(hq)
    return sig_f, f, jnp.log(f), 1.0 - f, sig_q, hq * sig_q


def _hgrn_intra(q, kk, b, exact=False):
    row = lax.broadcasted_iota(jnp.int32, b.shape, 0)
    qs, ks, eqs, eks, a_rows = [], [], [], [], []
    for i in range(CHUNK // SUB):
        ref = b[SUB * i + SUB // 2:SUB * i + SUB // 2 + 1, :]
        eq = jnp.exp(b[SUB * i:SUB * (i + 1), :] - ref)
        ek = jnp.exp(jnp.where(row < SUB * (i + 1), ref - b, NEG_BIG))
        qi = q[SUB * i:SUB * (i + 1), :] * eq
        ki = kk * ek
        a_rows.append(_mm_nt(qi, ki, exact))
        qs.append(qi), ks.append(ki), eqs.append(eq), eks.append(ek)
    tt = lax.broadcasted_iota(jnp.int32, (CHUNK, CHUNK), 0)
    ss = lax.broadcasted_iota(jnp.int32, (CHUNK, CHUNK), 1)
    causal = ss <= tt
    a = jnp.where(causal, jnp.concatenate(a_rows, axis=0), 0.0)
    return a, causal, qs, ks, eqs, eks


def _hgrn_fwd(xph, lbl, tg=512):
    T = xph.shape[0]
    ng, ncg = T // tg, tg // CHUNK
    cols = [slice(HGRN_DIM * h, HGRN_DIM * (h + 1)) for h in range(HGRN_HEADS)]

    def body(lbl_ref, hq_ref, hf_ref, hi_ref, o_ref, st_ref, s_scr):
        @pl.when(pl.program_id(0) == 0)
        def _():
            s_scr[...] = jnp.zeros_like(s_scr)

        lb = _lb_from_logits(lbl_ref[...])

        def chunks(it, _):
            pre = []
            for k in range(HGRN_CPI):
                c = it * HGRN_CPI + k
                rows = pl.ds(pl.multiple_of(c * CHUNK, CHUNK), CHUNK)
                for cs in cols:
                    _, _, lf, kk, _, q = _hgrn_gates(hq_ref[rows, cs], hf_ref[rows, cs], lb[:, cs])
                    v = hi_ref[rows, cs]
                    b = _cumsum_rows(lf)
                    a = _hgrn_intra(q, kk, b)[0]
                    b_last = b[CHUNK - 1:CHUNK, :]
                    pre.append((c, rows, q * jnp.exp(b), a, v, jnp.exp(b_last), _mm_tn(v, kk * jnp.exp(b_last - b))))
            for i, (c, rows, qe, a, v, ebl, upd) in enumerate(pre):
                h = i % HGRN_HEADS
                st = s_scr[h]
                st_ref[h, c] = st
                o_ref[rows, cols[h]] = _mm_nt(qe, st) + _mm(a, v)
                s_scr[h] = st * ebl + upd
            return 0

        lax.fori_loop(0, ncg // HGRN_CPI, chunks, 0)

    col = lambda k: pl.BlockSpec((tg, HGRN_WIDTH), lambda g: (g, k))
    return pl.pallas_call(
        body, name="hgrn_fwd", grid=(ng,),
        in_specs=[pl.BlockSpec((2, HGRN_WIDTH), lambda g: (0, 0)), col(0), col(1), col(2)],
        out_specs=[col(0), pl.BlockSpec((HGRN_HEADS, ncg, HGRN_DIM, HGRN_DIM), lambda g: (0, g, 0, 0))],
        out_shape=[jax.ShapeDtypeStruct((T, HGRN_WIDTH), F32),
                   jax.ShapeDtypeStruct((HGRN_HEADS, T // CHUNK, HGRN_DIM, HGRN_DIM), F32)],
        scratch_shapes=[pltpu.VMEM((HGRN_HEADS, HGRN_DIM, HGRN_DIM), F32)],
        compiler_params=_params(("arbitrary",)),
    )(lbl, xph, xph, xph)


def _hgrn_bwd(xph, lbl, states, d_o, fill=(), tg=512):
    T = xph.shape[0]
    ng, ncg = T // tg, tg // CHUNK
    cols = [slice(HGRN_DIM * h, HGRN_DIM * (h + 1)) for h in range(HGRN_HEADS)]
    nsub = CHUNK // SUB
    nf = len(fill)

    def body(lbl_ref, hq_ref, hf_ref, hi_ref, st_ref, do_ref, *rest):
        dhq_ref, dhf_ref, dhi_ref, dlg_ref = rest[nf:nf + 4]
        ds_scr, dlb_scr = rest[2 * nf + 4:2 * nf + 6]
        fill_copies = lambda: _pair_fill_copies(rest[nf + 4:2 * nf + 4], *rest[2 * nf + 6:])
        g = pl.program_id(0)

        @pl.when(g == 0)
        def _():
            ds_scr[...] = jnp.zeros_like(ds_scr)
            dlb_scr[...] = jnp.zeros_like(dlb_scr)
            for cp in (fill_copies()[0] if nf else ()):
                cp.start()

        lb = _lb_from_logits(lbl_ref[...])

        def chunks(it, _):
            pre = []
            for k, h in ((k, h) for k in range(HGRN_CPI) for h in range(HGRN_HEADS)):
                cs = cols[h]
                c = ncg - 1 - (it * HGRN_CPI + k)
                rows = pl.ds(pl.multiple_of(c * CHUNK, CHUNK), CHUNK)
                hq = hq_ref[rows, cs]
                sig_f, f, lf, kk, sig_q, q = _hgrn_gates(hq, hf_ref[rows, cs], lb[:, cs])
                v = hi_ref[rows, cs]
                do = do_ref[rows, cs]
                b = _cumsum_rows(lf)
                eb = jnp.exp(b)
                a, causal, qs, ks, eqs, eks = _hgrn_intra(q, kk, b)
                b_last = b[CHUNK - 1:CHUNK, :]
                st = st_ref[h, c]
                pre.append(dict(h=h, cs=cs, rows=rows, hq=hq, sig_f=sig_f, f=f, kk=kk, sig_q=sig_q, q=q, v=v, eb=eb, qs=qs,
                                ks=ks, eqs=eqs,
                                eks=eks, ebl=jnp.exp(b_last), el=jnp.exp(b_last - b), st=st,
                                da=jnp.where(causal, _mm_nt(do, v, True), 0.0), dq=_mm(do, st, True) * eb,
                                dv=_mm_tn(a, do), dsu=_mm_tn(do, q * eb, True)))
            for w in pre:
                dq_rows = []
                dk = jnp.zeros_like(w["q"])
                for i in range(nsub):
                    dai = w["da"][SUB * i:SUB * (i + 1), :]
                    dq_rows.append(_mm(dai, w["ks"][i], True) * w["eqs"][i])
                    dk = dk + _mm_tn(dai, w["qs"][i], True) * w["eks"][i]
                w["dq"] = w["dq"] + jnp.concatenate(dq_rows, axis=0)
                w["dk"] = dk
            for w in pre:
                h, cs, rows = w["h"], w["cs"], w["rows"]
                kk, el, ebl, dst = w["kk"], w["el"], w["ebl"], ds_scr[h]
                dk_state = _mm(w["v"], dst, True) * el
                dk = w["dk"] + dk_state
                e_last = (ebl * jnp.sum(w["st"] * dst, axis=0, keepdims=True)
                          + jnp.sum(kk * dk_state, axis=0, keepdims=True))
                dlf = _rev_cumsum_rows(w["q"] * w["dq"] - kk * dk) + e_last
                ds_scr[h] = dst * ebl + w["dsu"]
                df = dlf / w["f"] - dk
                sig_f, sig_q = w["sig_f"], w["sig_q"]
                dhf_ref[rows, cs] = df * (1.0 - lb[:, cs]) * sig_f * (1.0 - sig_f)
                dlb_scr[:, cs] += jnp.sum(df * (1.0 - sig_f), axis=0, keepdims=True)
                dhq_ref[rows, cs] = w["dq"] * sig_q * (1.0 + w["hq"] * (1.0 - sig_q))
                dhi_ref[rows, cs] = w["dv"] + _mm_nt(kk * el, dst)
            return 0

        lax.fori_loop(0, ncg // HGRN_CPI, chunks, 0)

        @pl.when(g == ng - 1)
        def _():
            dl0 = dlb_scr[...] * lb * (1.0 - lb)
            dlg_ref[...] = jnp.concatenate([dl0, -dl0], axis=0)
            if nf:
                copies, waits = fill_copies()
                for w in waits:
                    w.wait_recv()
                for cp in copies:
                    cp.wait_send()

    col = lambda k: pl.BlockSpec((tg, HGRN_WIDTH), lambda g: (ng - 1 - g, k))
    logits = pl.BlockSpec((2, HGRN_WIDTH), lambda g: (0, 0))
    big = jax.ShapeDtypeStruct((T, HGRN_WIDTH), F32)
    n_in, n_out = 6, 4
    return pl.pallas_call(
        body, name="hgrn_bwd", grid=(ng,),
        in_specs=[logits, col(0), col(1), col(2),
                  pl.BlockSpec((HGRN_HEADS, ncg, HGRN_DIM, HGRN_DIM), lambda g: (0, ng - 1 - g, 0, 0)), col(0)] + [ANY] * nf,
        out_specs=[col(0), col(0), col(0), logits] + [ANY] * nf,
        out_shape=[big, big, big, jax.ShapeDtypeStruct((2, HGRN_WIDTH), F32)]
        + [jax.ShapeDtypeStruct(f.shape, f.dtype) for f in fill],
        input_output_aliases={n_in + k: n_out + k for k in range(nf)},
        scratch_shapes=[pltpu.VMEM((HGRN_HEADS, HGRN_DIM, HGRN_DIM), F32), pltpu.VMEM((1, HGRN_WIDTH), F32)]
        + ([pltpu.SemaphoreType.DMA((nf,)), pltpu.SemaphoreType.DMA((nf,))] if nf else []),
        compiler_params=_params(("arbitrary",)),
    )(lbl, xph, xph, xph, states, d_o, *fill)


def _proj_fwd(x, o_raw, oh_raw, xph, wout, w_mla, w_hg, w_post, w_fpre, tt=512):
    T = x.shape[0]

    def body(x_ref, o_ref, oh_ref, hg_ref, wout_ref, wmla_ref, whg_ref, wpost_ref, wfpre_ref,
             h1_ref, y1_ref, z_ref, mix_ref):
        om, _, _ = _grms_fwd(o_ref[...], wmla_ref[...], MLA_V)
        hg = hg_ref[...]
        ohn, _, _ = _grms_fwd(oh_ref[...], whg_ref[...], HGRN_DIM)
        mix = jnp.concatenate([om, ohn * (hg * jax.nn.sigmoid(hg))], axis=-1)
        mix_ref[...] = mix.astype(mix_ref.dtype)
        y1 = _mm(mix, wout_ref[...])
        y1_ref[...] = y1
        h1 = x_ref[...] + _rms_fwd(y1, wpost_ref[...])[0]
        h1_ref[...] = h1
        z_ref[...] = _rms_fwd(h1, wfpre_ref[...])[0].astype(z_ref.dtype)

    row = lambda w: pl.BlockSpec((tt, w), lambda i: (i, 0))
    full = lambda a: pl.BlockSpec(a.shape, lambda i: (0,) * a.ndim)
    sds = jax.ShapeDtypeStruct
    return pl.pallas_call(
        body, name="proj_fwd", grid=(T // tt,),
        in_specs=[row(D_MODEL), row(MLA_WIDTH), row(HGRN_WIDTH), pl.BlockSpec((tt, HGRN_WIDTH), lambda i: (i, 3)),
                  full(wout), full(w_mla), full(w_hg), full(w_post), full(w_fpre)],
        out_specs=[row(D_MODEL)] * 4,
        out_shape=[sds((T, D_MODEL), F32), sds((T, D_MODEL), F32), sds((T, D_MODEL), MXU_DTYPE),
                   sds((T, D_MODEL), MXU_DTYPE)],
        compiler_params=_params(("arbitrary",)),
    )(x, o_raw, oh_raw, xph, wout, w_mla, w_hg, w_post, w_fpre)


def _ffn_fwd(zb, h1, tgt, w_fpost, wg, wu, wd, tt=256):
    T = zb.shape[0]
    nj = N_CHIPS

    def body(z_ref, h1_ref, tgt_ref, wfpost_ref, wg_ref, wu_ref, wd_ref, g_ref, up_ref, dy2_ref, dh2_ref, loss_ref, dwf_ref):
        @pl.when(pl.program_id(0) == 0)
        def _():
            loss_ref[...] = jnp.zeros_like(loss_ref)
            dwf_ref[...] = jnp.zeros_like(dwf_ref)

        z = z_ref[...]
        gs = [_mm_nt(z, wg_ref[j]) for j in range(nj)]
        ups = [_mm_nt(z, wu_ref[j]) for j in range(nj)]
        y2 = jnp.zeros((tt, D_MODEL), F32)
        for j in range(nj):
            g_ref[j] = gs[j]
            up_ref[j] = ups[j]
            y2 = y2 + _mm(gs[j] * jax.nn.sigmoid(gs[j]) * ups[j], wd_ref[j])
        w = wfpost_ref[...]
        y2s, y2n, r2 = _rms_fwd(y2, w)
        e = h1_ref[...] + y2s - tgt_ref[...]
        loss_ref[...] += jnp.sum(e * e, axis=0, keepdims=True)
        dh2 = e * (1.0 / D_MODEL)
        dh2_ref[...] = dh2
        dy2, dwf = _rms_bwd(dh2, y2n, r2, w)
        dy2_ref[...] = dy2.astype(dy2_ref.dtype)
        dwf_ref[...] += dwf

    row = pl.BlockSpec((tt, D_MODEL), lambda i: (i, 0))
    vec = pl.BlockSpec((1, D_MODEL), lambda i: (0, 0))
    resident = pl.BlockSpec((nj, FF_SHARD, D_MODEL), lambda i: (0, 0, 0), pipeline_mode=pl.Buffered(1))
    act = pl.BlockSpec((nj, tt, FF_SHARD), lambda i: (0, i, 0))
    sds = jax.ShapeDtypeStruct
    return pl.pallas_call(
        body, name="ffn_fwd", grid=(T // tt,),
        in_specs=[row, row, row, vec, resident, resident, resident],
        out_specs=[act, act, row, row, vec, vec],
        out_shape=[sds((nj, T, FF_SHARD), F32), sds((nj, T, FF_SHARD), F32), sds((T, D_MODEL), MXU_DTYPE),
                   sds((T, D_MODEL), F32), sds((1, D_MODEL), F32), sds((1, D_MODEL), F32)],
        compiler_params=_params(("arbitrary",)),
    )(zb, h1, tgt, w_fpost, wg, wu, wd)


def _ffn_bwd(zb, g, up, dy2b, wg, wu, wd, tt=512):
    T = zb.shape[0]
    nj = N_CHIPS

    def body(z_ref, g_ref, up_ref, dy2_ref, wg_ref, wu_ref, wd_ref, dwg_ref, dwu_ref, dwd_ref, dz_ref, acc_ref):
        j, i = pl.program_id(0), pl.program_id(1)
        rows = pl.ds(pl.multiple_of(i * tt, tt), tt)

        @pl.when(i == 0)
        def _():
            dwg_ref[...] = jnp.zeros_like(dwg_ref)
            dwu_ref[...] = jnp.zeros_like(dwu_ref)
            dwd_ref[...] = jnp.zeros_like(dwd_ref)

        z, g_, up_, dy2 = z_ref[...], g_ref[0], up_ref[0], dy2_ref[...]
        sg = jax.nn.sigmoid(g_)
        act = g_ * sg
        dff = _mm_nt(dy2, wd_ref[0])
        dwd_ref[0] += _mm_tn(act * up_, dy2)
        dg = dff * up_ * sg * (1.0 + g_ * (1.0 - sg))
        dup = dff * act
        dwg_ref[0] += _mm_tn(dg, z)
        dwu_ref[0] += _mm_tn(dup, z)
        dz = _mm(dg, wg_ref[0]) + _mm(dup, wu_ref[0])

        @pl.when(j == 0)
        def _():
            acc_ref[rows, :] = dz

        @pl.when((j > 0) & (j < nj - 1))
        def _():
            acc_ref[rows, :] += dz

        @pl.when(j == nj - 1)
        def _():
            dz_ref[...] = acc_ref[rows, :] + dz

    row = pl.BlockSpec((tt, D_MODEL), lambda j, i: (i, 0))
    act = pl.BlockSpec((1, tt, FF_SHARD), lambda j, i: (j, i, 0))
    w_sh = pl.BlockSpec((1, FF_SHARD, D_MODEL), lambda j, i: (j, 0, 0))
    w_grad = jax.ShapeDtypeStruct((nj, FF_SHARD, D_MODEL), F32)
    return pl.pallas_call(
        body, name="ffn_bwd", grid=(nj, T // tt),
        in_specs=[row, act, act, row, w_sh, w_sh, w_sh],
        out_specs=[w_sh, w_sh, w_sh, pl.BlockSpec((tt, D_MODEL), lambda j, i: (jnp.where(j == nj - 1, i, 0), 0))],
        out_shape=[w_grad, w_grad, w_grad, jax.ShapeDtypeStruct((T, D_MODEL), F32)],
        scratch_shapes=[pltpu.VMEM((T, D_MODEL), F32)],
        compiler_params=_params(("arbitrary", "arbitrary"), vmem=FFN_BWD_VMEM),
    )(zb, g, up, dy2b, wg, wu, wd)


def _mid_bwd(dz, dh2, h1, y1, mixb, o_raw, oh_raw, xph, wout, w_fpre, w_post, w_mla, w_hg, swap=(), tt=512):
    T = dh2.shape[0]
    nsw = len(swap)
    n_in, n_out = 13, 10

    def body(*refs):
        (dz_ref, dh2_ref, h1_ref, y1_ref, mix_ref, o_ref, oh_ref, hg_ref, wout_ref, wfpre_ref, wpost_ref,
         wmla_ref, whg_ref) = refs[:n_in]
        (dh1_ref, dwout_ref, do_ref, doh_ref, dhg_ref, dvec_ref, dwfpre_ref, dwpost_ref, dwmla_ref,
         dwhg_ref) = refs[n_in + nsw:n_in + nsw + n_out]
        sems = refs[n_in + 2 * nsw + n_out + 1:]
        swap_copies = lambda: _pair_swap_copies(refs[n_in:n_in + nsw], refs[n_in + nsw + n_out:n_in + 2 * nsw + n_out],
                                                *sems)

        def wout_copies():
            _, _, c, _, sib, _ = _place()
            rw_ref, h = refs[n_in + 2 * nsw + n_out], D_MODEL // N_CHIPS // 2
            return [_rcopy(dwout_ref.at[pl.ds(pl.multiple_of((2 * k + 1 - c) * h, 8), h)], rw_ref.at[k], *sems, nsw + k, sib)
                    for k in range(N_CHIPS)]

        @pl.when(pl.program_id(0) == 0)
        def _():
            for r in (dwout_ref, dwfpre_ref, dwpost_ref, dwmla_ref, dwhg_ref):
                r[...] = jnp.zeros_like(r)
            for cp in (swap_copies() if nsw else ()):
                cp.start()

        dz = dz_ref[...]
        wfpre = wfpre_ref[...]
        _, h1n, r = _rms_fwd(h1_ref[...], wfpre)
        dh1_z, dwfpre = _rms_bwd(dz, h1n, r, wfpre)
        dwfpre_ref[...] += dwfpre
        dh1 = dh2_ref[...] + dh1_z
        dh1_ref[...] = dh1
        wpost = wpost_ref[...]
        _, y1n, r1 = _rms_fwd(y1_ref[...], wpost)
        dy1, dwpost = _rms_bwd(dh1, y1n, r1, wpost)
        dwpost_ref[...] += dwpost
        dmix = _mm_nt(dy1, wout_ref[...])
        dwout_ref[...] += _mm_tn(mix_ref[...], dy1)
        wmla = wmla_ref[...]
        o = o_ref[...]
        _, on, ro = _grms_fwd(o, wmla, MLA_V)
        d_o, dwmla = _grms_bwd(dmix[:, :MLA_WIDTH], on, ro, wmla, MLA_V)
        dwmla_ref[...] += dwmla
        do_ref[...] = d_o.astype(do_ref.dtype)
        hh = lax.broadcasted_iota(jnp.int32, (MLA_HEADS, MLA_WIDTH), 0)
        ll = lax.broadcasted_iota(jnp.int32, (MLA_HEADS, MLA_WIDTH), 1)
        sel = jnp.where((ll >= hh * MLA_V) & (ll < (hh + 1) * MLA_V), 1.0, 0.0)
        dvec_ref[...] = _mm_nt(sel, d_o * o, True)
        whg = whg_ref[...]
        hg = hg_ref[...]
        sg = jax.nn.sigmoid(hg)
        _, ohn, rh = _grms_fwd(oh_ref[...], whg, HGRN_DIM)
        dmh = dmix[:, MLA_WIDTH:]
        dhg_ref[...] = dmh * ohn * whg * sg * (1.0 + hg * (1.0 - sg))
        d_oh, dwhg = _grms_bwd(dmh * (hg * sg), ohn, rh, whg, HGRN_DIM)
        dwhg_ref[...] += dwhg
        doh_ref[...] = d_oh

        if nsw:
            @pl.when(pl.program_id(0) == T // tt - 1)
            def _():
                for cp in wout_copies():
                    cp.start()
                for cp in swap_copies() + wout_copies():
                    cp.wait()

    row = lambda w: pl.BlockSpec((tt, w), lambda i: (i, 0))
    full = lambda a: pl.BlockSpec(a.shape, lambda i: (0,) * a.ndim)
    vec = lambda w: pl.BlockSpec((1, w), lambda i: (0, 0))
    sds = jax.ShapeDtypeStruct
    return pl.pallas_call(
        body, name="mid_bwd", grid=(T // tt,),
        in_specs=[row(D_MODEL), row(D_MODEL), row(D_MODEL), row(D_MODEL),
                  row(D_MODEL), row(MLA_WIDTH), row(HGRN_WIDTH), pl.BlockSpec((tt, HGRN_WIDTH), lambda i: (i, 3)),
                  full(wout), vec(D_MODEL), vec(D_MODEL), vec(MLA_WIDTH), vec(HGRN_WIDTH)] + [ANY] * nsw,
        out_specs=[row(D_MODEL), full(wout), row(MLA_WIDTH), row(HGRN_WIDTH), row(HGRN_WIDTH),
                   pl.BlockSpec((MLA_HEADS, tt), lambda i: (0, i)),
                   vec(D_MODEL), vec(D_MODEL), vec(MLA_WIDTH), vec(HGRN_WIDTH)] + [ANY] * (nsw + 1 if nsw else 0),
        out_shape=[sds((T, D_MODEL), F32), sds(wout.shape, F32), sds((T, MLA_WIDTH), MXU_DTYPE), sds((T, HGRN_WIDTH), F32),
                   sds((T, HGRN_WIDTH), F32), sds((MLA_HEADS, T), F32),
                   sds((1, D_MODEL), F32), sds((1, D_MODEL), F32), sds((1, MLA_WIDTH), F32), sds((1, HGRN_WIDTH), F32)]
        + (_half_stack_shapes(list(swap) + [sds((N_CHIPS, D_MODEL // N_CHIPS, D_MODEL), F32)]) if nsw else []),
        scratch_shapes=[pltpu.SemaphoreType.DMA((nsw + N_CHIPS,)), pltpu.SemaphoreType.DMA((nsw + N_CHIPS,))] if nsw else [],
        compiler_params=_params(("arbitrary",)),
    )(dz, dh2, h1, y1, mixb, o_raw, oh_raw, xph, wout, w_fpre, w_post, w_mla, w_hg, *swap)


def _in_bwd(x, dh1, cq, ckv, dq, dk, dv, dhq, dhf, dhi, dhg, rc, rs, w_pre, win, qnw, wq, kvnw, wk, wv, tt=256):
    T = x.shape[0]

    def body(x_ref, dh1_ref, cq_ref, ckv_ref, dq_ref, dk_ref, dv_ref, dhq_ref, dhf_ref, dhi_ref, dhg_ref, rc_ref, rs_ref,
             wpre_ref, win_ref, qnw_ref, wq_ref, kvnw_ref, wk_ref, wv_ref,
             dx_ref, dwin_ref, dwq_ref, dwk_ref, dwv_ref, dwpre_ref, dqnw_ref, dkvnw_ref):
        @pl.when(pl.program_id(0) == 0)
        def _():
            for r in (dwin_ref, dwq_ref, dwk_ref, dwv_ref, dwpre_ref, dqnw_ref, dkvnw_ref):
                r[...] = jnp.zeros_like(r)

        def add_win_grad(r, first):
            for arr0, n, chip, row0 in _win_grad_segments():
                if first <= arr0 and arr0 + n <= first + r.shape[0]:
                    dwin_ref[chip, row0:row0 + n, :] += r[arr0 - first:arr0 - first + n]

        lo = Q_RANK + KV_RANK + HEAD_PAD
        dxp_h = jnp.concatenate([dhq_ref[...], dhf_ref[...], dhi_ref[...], dhg_ref[...]], axis=-1)
        du = _mm(dxp_h, win_ref[lo:, :])
        wpre = wpre_ref[...]
        u, xn, rx = _rms_fwd(x_ref[...], wpre)
        add_win_grad(_mm_tn(dxp_h, u), lo)
        c, sa, sb = _rope_tables(rc_ref[...], rs_ref[...])
        lane = lax.broadcasted_iota(jnp.int32, (tt, HEAD_PAD), 1)
        dk_all = dk_ref[...]
        dq_lin = []
        dkr = jnp.zeros((tt, HEAD_PAD), F32)
        for h in range(MLA_HEADS):
            sl = slice(HEAD_PAD * h, HEAD_PAD * (h + 1))
            dq_lin.append(_rope_bwd(dq_ref[sl, :].T * ATTN_SCALE, c, sa, sb))
            dkr = dkr + dk_all[:, sl]
        dq_lin = jnp.concatenate(dq_lin, axis=-1)
        dkr = jnp.where((lane >= MLA_NOPE) & (lane < MLA_QK), _rope_bwd(dkr, c, sa, sb), 0.0)
        qnw = qnw_ref[...]
        qn, cqn, rq = _rms_fwd(cq_ref[...], qnw)
        dwq_ref[...] += _mm_tn(qn, dq_lin)
        dcq, dqnw = _rms_bwd(_mm_nt(dq_lin, wq_ref[...]), cqn, rq, qnw)
        dqnw_ref[...] += dqnw
        kvnw = kvnw_ref[...]
        kvn, ckvn, rkv = _rms_fwd(ckv_ref[...], kvnw)
        dv_ = dv_ref[...]
        dwk_ref[...] += _mm_tn(kvn, dk_all)
        dwv_ref[...] += _mm_tn(kvn, dv_)
        dckv, dkvnw = _rms_bwd(_mm_nt(dk_all, wk_ref[...]) + _mm_nt(dv_, wv_ref[...]), ckvn, rkv, kvnw)
        dkvnw_ref[...] += dkvnw
        dxp_a = jnp.concatenate([dcq, dckv, dkr], axis=-1)
        add_win_grad(_mm_tn(dxp_a, u), 0)
        dx_u, dwpre = _rms_bwd(du + _mm(dxp_a, win_ref[:lo, :]), xn, rx, wpre)
        dwpre_ref[...] += dwpre
        dx_ref[...] = dh1_ref[...] + dx_u

    row = lambda w: pl.BlockSpec((tt, w), lambda i: (i, 0))
    full = lambda a: pl.BlockSpec(a.shape, lambda i: (0,) * a.ndim)
    sds = jax.ShapeDtypeStruct
    qk_w = MLA_HEADS * HEAD_PAD
    return pl.pallas_call(
        body, name="in_bwd", grid=(T // tt,),
        in_specs=[row(D_MODEL), row(D_MODEL), row(Q_RANK), row(KV_RANK), pl.BlockSpec((qk_w, tt), lambda i: (0, i)),
                  row(qk_w), row(MLA_WIDTH),
                  row(HGRN_WIDTH), row(HGRN_WIDTH), row(HGRN_WIDTH), row(HGRN_WIDTH), row(HEAD_PAD), row(HEAD_PAD),
                  full(w_pre), full(win), full(qnw), full(wq), full(kvnw), full(wk), full(wv)],
        out_specs=[row(D_MODEL), pl.BlockSpec(WIN_COMM_SHAPE, lambda i: (0, 0, 0)), full(wq), full(wk), full(wv),
                   full(w_pre), full(qnw), full(kvnw)],
        out_shape=[sds((T, D_MODEL), F32), sds(WIN_COMM_SHAPE, F32), sds(wq.shape, F32), sds(wk.shape, F32),
                   sds(wv.shape, F32), sds(w_pre.shape, F32), sds(qnw.shape, F32), sds(kvnw.shape, F32)],
        compiler_params=_params(("arbitrary",)),
    )(x, dh1, cq, ckv, dq, dk, dv, dhq, dhf, dhi, dhg, rc, rs, w_pre, win, qnw, wq, kvnw, wk, wv)


def _arrange_weights(win_t, wuq_full, wukv):
    dt = win_t.dtype
    z = lambda n: jnp.zeros((n, D_MODEL), dt)
    s2 = Q_RANK + KV_RANK
    win_arr = jnp.concatenate([win_t[:s2], z(MLA_NOPE), win_t[s2:s2 + MLA_ROPE], z(HEAD_PAD - MLA_QK),
                               win_t[s2 + MLA_ROPE:]], axis=0)
    wq_arr = jnp.pad(wuq_full, ((0, 0), (0, 0), (0, HEAD_PAD - MLA_QK))).reshape(Q_RANK, MLA_HEADS * HEAD_PAD)
    wk_arr = jnp.pad(wukv[:, :, :MLA_NOPE], ((0, 0), (0, 0), (0, HEAD_PAD - MLA_NOPE))).reshape(
        KV_RANK, MLA_HEADS * HEAD_PAD)
    wv_arr = wukv[:, :, MLA_NOPE:].reshape(KV_RANK, MLA_WIDTH)
    return win_arr, wq_arr, wk_arr, wv_arr


WIN_COMM_SHAPE = (N_CHIPS, -(-D_IN // N_CHIPS // 32) * 32, D_MODEL)


def _win_grad_segments():
    s2 = Q_RANK + KV_RANK
    runs = [(0, s2, 0), (s2, s2 + MLA_ROPE, MLA_NOPE), (s2 + MLA_ROPE, D_IN, HEAD_PAD - MLA_ROPE)]
    per = D_IN // N_CHIPS
    segs = []
    for lo, hi, shift in runs:
        for k in range(N_CHIPS):
            a, b = max(lo, per * k), min(hi, per * (k + 1))
            if a < b:
                segs.append((a + shift, b - a, k, a - per * k))
    return segs


def _unarrange_grads(dwq_arr, dwk_arr, dwv_arr):
    dwuq = dwq_arr.reshape(Q_RANK, MLA_HEADS, HEAD_PAD)[:, :, :MLA_QK]
    dwukv = jnp.concatenate([dwk_arr.reshape(KV_RANK, MLA_HEADS, HEAD_PAD)[:, :, :MLA_NOPE],
                             dwv_arr.reshape(KV_RANK, MLA_HEADS, MLA_V)], axis=-1)
    return dwuq, dwukv


def _rope_inv_freq():
    inv = 1.0 / (ROPE_THETA ** (jnp.arange(0, MLA_ROPE, 2, dtype=F32) / MLA_ROPE))
    z = lambda n: jnp.zeros((n,), F32)
    return jnp.concatenate([z(MLA_NOPE), inv, inv, z(HEAD_PAD - MLA_QK)]).reshape(1, HEAD_PAD)


def _local_step(x, pos, tgt, small, win_arr, wq_arr, wk_arr, wv_arr, late, place=None):
    invf = _rope_inv_freq()
    cq, ckv, xph, qb, kb, vb, kt, vt, rc, rs = _in_fwd(x, pos, invf, small["attn_pre_norm"], win_arr, small["mla_q_norm"],
                                               wq_arr, small["mla_kv_norm"], wk_arr, wv_arr)
    if place is None:
        o_raw, lse = _attn_fwd_t(qb, kb, vt)
        wout, wg, wu, wd = late
    else:
        o_raw, lse, *stacks = _attn_fwd_t(qb, kb, vt, gather=late)
        wout, wg, wu, wd = [lax.dynamic_update_slice(s, l[None], (place[1], 0, 0)) for s, l in zip(stacks, late)]
        wout = wout.reshape(D_MODEL, D_MODEL)
    oh_raw, states = _hgrn_fwd(xph, small["hgrn_lb_logits"])
    h1, y1, zb, mixb = _proj_fwd(x, o_raw, oh_raw, xph, wout, small["mla_out_norm"], small["hgrn_out_norm"],
                                 small["attn_post_norm"], small["ffn_pre_norm"])
    g, up, dy2b, dh2, loss_acc, d_fpost = _ffn_fwd(zb, h1, tgt, small["ffn_post_norm"], wg, wu, wd)
    dwg, dwu, dwd, dz = _ffn_bwd(zb, g, up, dy2b, wg, wu, wd)
    ffn_grads = [] if place is None else [dwg, dwu, dwd]
    dh1, dwout, d_o, d_oh, dhg, dvec, d_fpre, d_post, d_mla, d_hg, *ffn_rs = _mid_bwd(
        dz, dh2, h1, y1, mixb, o_raw, oh_raw, xph, wout, small["ffn_pre_norm"], small["attn_post_norm"],
        small["mla_out_norm"], small["hgrn_out_norm"], swap=ffn_grads)
    if ffn_grads:
        ffn_grads = ffn_grads + [dwout.reshape(N_CHIPS, D_MODEL // N_CHIPS, D_MODEL)]
    ffn_ps = _pair_sum(place, ffn_grads, ffn_rs, name="pair_sum_ffn") if ffn_grads else []
    dq, dk, dv, *ffn_ris = _attn_bwd_t(qb, kb, kt, vb, d_o, lse, dvec.reshape(lse.shape), send=ffn_ps)
    ffn_sums = _chip_sum(place, ffn_grads, ffn_rs, ffn_ris, name="chip_sum_ffn") if ffn_grads else []
    dhq, dhf, dhi, d_lbl, *ffn_final = _hgrn_bwd(xph, small["hgrn_lb_logits"], states, d_oh, fill=ffn_sums)
    dx, dwin4, dwq_arr, dwk_arr, dwv_arr, d_pre, d_qn, d_kvn = _in_bwd(
        x, dh1, cq, ckv, dq, dk, dv, dhq, dhf, dhi, dhg, rc, rs, small["attn_pre_norm"], win_arr,
        small["mla_q_norm"], wq_arr, small["mla_kv_norm"], wk_arr, wv_arr)
    dwuq, dwukv = _unarrange_grads(dwq_arr, dwk_arr, dwv_arr)
    loss = 0.5 * jnp.sum(loss_acc) * (1.0 / D_MODEL)
    grads = dict(attn_pre_norm=d_pre, w_in=dwin4, mla_q_norm=d_qn, mla_w_uq=dwuq, mla_kv_norm=d_kvn, mla_w_ukv=dwukv,
                 mla_out_norm=d_mla, hgrn_lb_logits=d_lbl, hgrn_out_norm=d_hg, w_out=dwout, attn_post_norm=d_post,
                 ffn_pre_norm=d_fpre, w_gate=dwg, w_up=dwu, w_down=dwd, ffn_post_norm=d_fpost)
    if place is None:
        return loss, dx, grads
    return loss, dx, grads, ffn_final


def _place():
    x, y, c = lax.axis_index("x"), lax.axis_index("y"), lax.axis_index("c")
    others = [(1 - x, y), (x, 1 - y), (1 - x, 1 - y)]
    return x, y, c, 2 * x + y, (x, y, 1 - c), others


def _half(ref, c, rows):
    return ref.at[pl.ds(pl.multiple_of(c * rows, 8), rows)]


def _rcopy(src, dst, send, recv, k, to):
    return pltpu.make_async_remote_copy(src_ref=src, dst_ref=dst, send_sem=send.at[k], recv_sem=recv.at[k],
                                        device_id=to, device_id_type=MESH)


class _Gather:
    def __init__(self, ins, outs, send, recv):
        self.ins, self.outs, self.send, self.recv = ins, outs, send, recv
        self.n = len(ins)
        self.halves = [r.shape[0] // 2 for r in ins]
        _, _, self.c, self.me, self.sib, self.others = _place()

    def _each(self):
        for j, (px, py) in enumerate(self.others):
            for a in range(self.n):
                yield j * self.n + a, a, 2 * px + py, (px, py, self.c)

    def sends(self):
        return [_rcopy(_half(self.ins[a], self.c, self.halves[a]), _half(self.outs[a].at[self.me], self.c, self.halves[a]),
                       self.send, self.recv, k, to) for k, a, _, to in self._each()]

    def arrivals(self):
        parts = [(k, _half(self.outs[a].at[chip], self.c, self.halves[a]), to) for k, a, chip, to in self._each()]
        return [_rcopy(p, p, self.send, self.recv, k, to) for k, p, to in parts]

    def forwards(self):
        parts = [(k, _half(self.outs[a].at[chip], self.c, self.halves[a])) for k, a, chip, _ in self._each()]
        return [_rcopy(p, p, self.send, self.recv, 3 * self.n + k, self.sib) for k, p in parts]

    def forward_arrivals(self):
        parts = [(k, _half(self.outs[a].at[chip], 1 - self.c, self.halves[a])) for k, a, chip, _ in self._each()]
        return [_rcopy(p, p, self.send, self.recv, 3 * self.n + k, self.sib) for k, p in parts]

    @staticmethod
    def out_shapes(arrs):
        return [jax.ShapeDtypeStruct((N_CHIPS,) + a.shape, a.dtype) for a in arrs]

    @staticmethod
    def semaphores(arrs):
        return [pltpu.SemaphoreType.DMA((6 * len(arrs),)), pltpu.SemaphoreType.DMA((6 * len(arrs),))]


def _gather_chips(arrs, name):
    n = len(arrs)

    def body(*refs):
        gat = _Gather(refs[:n], refs[n:2 * n], *refs[2 * n:])
        sends, forwards = gat.sends(), gat.forwards()
        for cp in sends:
            cp.start()
        for arrival, fw in zip(gat.arrivals(), forwards):
            arrival.wait_recv()
            fw.start()
        for arrival in gat.forward_arrivals():
            arrival.wait_recv()
        for cp in sends + forwards:
            cp.wait_send()

    return pl.pallas_call(body, name=name, in_specs=[ANY] * n, out_specs=[ANY] * n, out_shape=_Gather.out_shapes(arrs),
                          scratch_shapes=_Gather.semaphores(arrs))(*arrs)


def _grad_blocks(gs):
    return max(n for n in (1, 2, 3, 4) if all(g.shape[1] // 2 % (16 * n) == 0 for g in gs))


def _pair_swap_copies(g_refs, r_refs, send, recv):
    _, _, c, _, sib, _ = _place()
    copies = []
    for a, (g, r) in enumerate(zip(g_refs, r_refs)):
        h = g.shape[1] // 2
        copies.append(_rcopy(g.at[:, pl.ds(pl.multiple_of((1 - c) * h, 8), h)], r, send, recv, a, sib))
    return copies


def _half_stack_shapes(gs, dtype=None):
    return [jax.ShapeDtypeStruct((N_CHIPS, g.shape[1] // 2, g.shape[2]), dtype or g.dtype) for g in gs]


def _pair_swap(gs, wholes):
    n, nw = len(gs), len(wholes)

    def body(*refs):
        ins, outs, (send, recv) = refs[:n + nw], refs[n + nw:2 * (n + nw)], refs[2 * (n + nw):]
        copies = _pair_swap_copies(ins[:n], outs[:n], send, recv)
        copies += [_rcopy(ins[n + k], outs[n + k], send, recv, n + k, _place()[4]) for k in range(nw)]
        for cp in copies:
            cp.start()
        for cp in copies:
            cp.wait()

    return pl.pallas_call(
        body, name="pair_swap", in_specs=[ANY] * (n + nw), out_specs=[ANY] * (n + nw),
        out_shape=_half_stack_shapes(gs) + [jax.ShapeDtypeStruct(w.shape, w.dtype) for w in wholes],
        scratch_shapes=[pltpu.SemaphoreType.DMA((n + nw,)), pltpu.SemaphoreType.DMA((n + nw,))],
    )(*gs, *wholes)


def _pair_sum(place, gs, rs, small=None, name="pair_sum"):
    n = len(gs)
    nb = _grad_blocks(gs)

    def body(place_ref, *refs):
        g_refs, r_refs, p_refs = refs[:n], refs[n:2 * n], refs[-n - 1:-1] if small else refs[-n:]
        for a in range(n):
            p_refs[a][0] = (g_refs[a][0] + r_refs[a][0]).astype(p_refs[a].dtype)
        if small:
            @pl.when((pl.program_id(0) == 0) & (pl.program_id(1) == 0))
            def _():
                refs[-1][...] = refs[2 * n][...] + refs[2 * n + 1][...]

    in_specs, out_specs = [], []
    for g in gs:
        blk = (1, g.shape[1] // 2 // nb, g.shape[2])
        in_specs.append(pl.BlockSpec(blk, lambda i, k, p: (k, p[0] * nb + i, 0)))
    for g in gs:
        blk = (1, g.shape[1] // 2 // nb, g.shape[2])
        in_specs.append(pl.BlockSpec(blk, lambda i, k, p: (k, i, 0)))
        out_specs.append(pl.BlockSpec(blk, lambda i, k, p: (k, i, 0)))
    out_shape = _half_stack_shapes(gs, BF16)
    if small:
        sm_spec = pl.BlockSpec(small[0].shape, lambda i, k, p: (0, 0))
        in_specs += [sm_spec, sm_spec]
        out_specs.append(sm_spec)
        out_shape.append(jax.ShapeDtypeStruct(small[0].shape, F32))
    return pl.pallas_call(
        body, name=name,
        grid_spec=pltpu.PrefetchScalarGridSpec(num_scalar_prefetch=1, grid=(nb, N_CHIPS), in_specs=in_specs,
                                               out_specs=out_specs),
        out_shape=out_shape,
        compiler_params=_params(("arbitrary", "arbitrary")),
    )(place, *gs, *rs, *(small or ()))


def _chip_swap_copies(p_refs, ri_refs, send, recv):
    _, _, c, _, _, others = _place()
    n = len(p_refs)
    return [_rcopy(p_refs[a].at[2 * px + py], ri_refs[a].at[j], send, recv, j * n + a, (px, py, c))
            for j, (px, py) in enumerate(others) for a in range(n)]


def _chip_swap_shapes(ps):
    return [jax.ShapeDtypeStruct((3,) + p.shape[1:], p.dtype) for p in ps]


def _chip_swap(ps, pair):
    n = len(ps)

    def body(*refs):
        start, finish = _chip_swap_plan(refs[:n], refs[n], refs[n + 1:2 * n + 1], refs[2 * n + 1], *refs[2 * n + 2:])
        start()
        finish()

    return pl.pallas_call(
        body, name="chip_swap", in_specs=[ANY] * (n + 1), out_specs=[ANY] * (n + 1),
        out_shape=_chip_swap_out_shapes(ps, pair), scratch_shapes=_chip_swap_semaphores(n),
    )(*ps, pair)


def _chip_swap_plan(p_refs, pair_ref, ri_refs, sm4_ref, send, recv, lsem):
    n = len(p_refs)
    hs = SMALL_ROWS // 2
    x, y, c, me, sib, others = _place()
    local = pltpu.make_async_copy(pair_ref, sm4_ref.at[me], lsem.at[0])
    copies = _chip_swap_copies(p_refs, ri_refs, send, recv)
    arrivals = list(copies)
    for j, (px, py) in enumerate(others):
        copies.append(_rcopy(_half(pair_ref, c, hs), _half(sm4_ref.at[me], c, hs), send, recv, 3 * n + j, (px, py, c)))
        part = _half(sm4_ref.at[2 * px + py], c, hs)
        arrivals.append(_rcopy(part, part, send, recv, 3 * n + j, (px, py, c)))

    def start():
        local.start()
        for cp in copies:
            cp.start()

    def finish():
        for arrival in arrivals:
            arrival.wait_recv()
        for cp in copies:
            cp.wait_send()
        local.wait()

    return start, finish


def _chip_swap_out_shapes(ps, pair):
    return _chip_swap_shapes(ps) + [jax.ShapeDtypeStruct((N_CHIPS,) + pair.shape, pair.dtype)]


def _chip_swap_semaphores(n):
    k = 3 * (n + 1)
    return [pltpu.SemaphoreType.DMA((k,)), pltpu.SemaphoreType.DMA((k,)), pltpu.SemaphoreType.DMA((1,))]


def _chip_sum(place, gs, rs, ris, name="chip_sum"):
    n = len(gs)
    nb = _grad_blocks(gs)

    def body(place_ref, *refs):
        g_refs, r_refs, ri_refs, o_refs = refs[:n], refs[n:2 * n], refs[2 * n:3 * n], refs[3 * n:]
        for a in range(n):
            ri = ri_refs[a]
            o_refs[a][...] = (g_refs[a][0] + r_refs[a][0]) + ri[0].astype(F32) + ri[1].astype(F32) + ri[2].astype(F32)

    in_specs, out_specs, out_shape = [], [], []
    for g in gs:
        blk = (1, g.shape[1] // 2 // nb, g.shape[2])
        in_specs.append(pl.BlockSpec(blk, lambda i, p: (p[1], p[0] * nb + i, 0)))
    for g in gs:
        blk = (1, g.shape[1] // 2 // nb, g.shape[2])
        in_specs.append(pl.BlockSpec(blk, lambda i, p: (p[1], i, 0)))
    for g in gs:
        rb = g.shape[1] // 2 // nb
        in_specs.append(pl.BlockSpec((3, rb, g.shape[2]), lambda i, p: (0, i, 0)))
        out_specs.append(pl.BlockSpec((rb, g.shape[2]), lambda i, p: (p[0] * nb + i, 0)))
        out_shape.append(jax.ShapeDtypeStruct(g.shape[1:], F32))
    return pl.pallas_call(
        body, name=name,
        grid_spec=pltpu.PrefetchScalarGridSpec(num_scalar_prefetch=1, grid=(nb,), in_specs=in_specs, out_specs=out_specs),
        out_shape=out_shape,
        compiler_params=_params(("arbitrary",)),
    )(place, *gs, *rs, *ris)


def _pair_fill_copies(g_refs, send, recv):
    _, _, c, _, sib, _ = _place()
    copies, waits = [], []
    for a, g in enumerate(g_refs):
        h = g.shape[0] // 2
        mine, theirs = _half(g, c, h), _half(g, 1 - c, h)
        copies.append(_rcopy(mine, mine, send, recv, a, sib))
        waits.append(_rcopy(theirs, theirs, send, recv, a, sib))
    return copies, waits


def _pair_fill(gfs, sm4):
    n = len(gfs)
    hs = SMALL_ROWS // 2

    def body(*refs):
        g_refs, sm4_ref = refs[n + 1:2 * n + 1], refs[2 * n + 1]
        send, recv = refs[2 * n + 2:]
        x, y, c, me, sib, others = _place()
        copies, waits = _pair_fill_copies(g_refs, send, recv)
        for j, (px, py) in enumerate(others):
            chip = 2 * px + py
            mine, theirs = _half(sm4_ref.at[chip], c, hs), _half(sm4_ref.at[chip], 1 - c, hs)
            copies.append(pltpu.make_async_remote_copy(src_ref=mine, dst_ref=mine, send_sem=send.at[n + j],
                                                       recv_sem=recv.at[n + j], device_id=sib, device_id_type=MESH))
            waits.append(pltpu.make_async_remote_copy(src_ref=theirs, dst_ref=theirs, send_sem=send.at[n + j],
                                                      recv_sem=recv.at[n + j], device_id=sib, device_id_type=MESH))
        for cp in copies:
            cp.start()
        for w in waits:
            w.wait_recv()
        for cp in copies:
            cp.wait_send()

    return pl.pallas_call(
        body, name="pair_fill", in_specs=[ANY] * (n + 1), out_specs=[ANY] * (n + 1),
        out_shape=[jax.ShapeDtypeStruct(g.shape, g.dtype) for g in gfs] + [jax.ShapeDtypeStruct(sm4.shape, sm4.dtype)],
        input_output_aliases={i: i for i in range(n + 1)},
        scratch_shapes=[pltpu.SemaphoreType.DMA((n + 3,)), pltpu.SemaphoreType.DMA((n + 3,))],
    )(*gfs, sm4)


def _adamw_math(w, g, m, v):
    m = ADAM_B1 * m + (1.0 - ADAM_B1) * g
    v = ADAM_B2 * v + (1.0 - ADAM_B2) * (g * g)
    m_hat = m / (1.0 - ADAM_B1 ** ADAM_STEP)
    v_hat = v / (1.0 - ADAM_B2 ** ADAM_STEP)
    return -ADAM_LR * (m_hat / (jnp.sqrt(v_hat) + ADAM_EPS) + ADAM_WD * w), m, v


def _adamw(items, steps, name):
    n = len(items)

    def body(*refs):
        for a in range(n):
            g = refs[4 * a + 1][...]
            d, mo, vo = _adamw_math(refs[4 * a][...], g, refs[4 * a + 2][...], refs[4 * a + 3][...])
            for out, val in zip(refs[4 * n + 4 * a:4 * n + 4 * a + 4], (g, d, mo, vo)):
                out[...] = val

    spec = lambda w: pl.BlockSpec((w.shape[0] // steps, w.shape[1]), lambda i: (i, 0))
    flat = pl.pallas_call(
        body, name=name, grid=(steps,), in_specs=[spec(it[0]) for it in items for _ in range(4)],
        out_specs=[spec(it[0]) for it in items for _ in range(4)],
        out_shape=[jax.ShapeDtypeStruct(it[0].shape, F32) for it in items for _ in range(4)],
        compiler_params=_params(("arbitrary",)),
    )(*[a for it in items for a in it])
    return [flat[4 * a:4 * a + 4] for a in range(n)]


def _adamw_small(sm4, wmv):
    views = SMALL_VIEWS[:-1]
    n = len(views)

    def body(sm4_ref, *refs):
        g_all = ((sm4_ref[0] + sm4_ref[1]) + sm4_ref[2]) + sm4_ref[3]
        for a, (name, rows, cols) in enumerate(views):
            row = SMALL_OFFSETS[name]
            g = g_all[row:row + rows, :cols]
            d, mo, vo = _adamw_math(refs[3 * a][...], g, refs[3 * a + 1][...], refs[3 * a + 2][...])
            for out, val in zip(refs[3 * n + 4 * a:3 * n + 4 * a + 4], (g, d, mo, vo)):
                out[...] = val
        row = SMALL_OFFSETS["loss"]
        refs[-1][...] = g_all[row:row + 1, :128]

    flat = pl.pallas_call(
        body, name="adamw_small",
        out_shape=[jax.ShapeDtypeStruct((rows, cols), F32) for _, rows, cols in views for _ in range(4)]
        + [jax.ShapeDtypeStruct((1, 128), F32)],
        compiler_params=pltpu.CompilerParams(vmem_limit_bytes=VMEM_LIMIT),
    )(sm4, *[a for t in wmv for a in t])
    return [flat[4 * a:4 * a + 4] for a in range(n)] + [flat[-1]]


SMALL_NAMES = ("attn_pre_norm", "mla_q_norm", "mla_kv_norm", "mla_w_ukv", "mla_out_norm", "hgrn_lb_logits",
               "hgrn_out_norm", "attn_post_norm", "ffn_pre_norm", "ffn_post_norm")
BIG_NAMES = ("w_in", "mla_w_uq", "w_out", "w_gate", "w_up", "w_down")
WEIGHT_NAMES = ("attn_pre_norm", "w_in", "mla_q_norm", "mla_w_uq", "mla_kv_norm", "mla_w_ukv", "mla_out_norm",
                "hgrn_lb_logits", "hgrn_out_norm", "w_out", "attn_post_norm", "ffn_pre_norm", "w_gate", "w_up", "w_down",
                "ffn_post_norm")


UQ_COMM_SHAPE = (192, 384)


def _pack_small(vals):
    parts, row = [], 0
    for name, rows, cols in sorted(SMALL_VIEWS, key=lambda view: SMALL_OFFSETS[view[0]]):
        assert SMALL_OFFSETS[name] == row
        parts.append(jnp.pad(vals[name].reshape(rows, cols), ((0, 0), (0, D_MODEL - cols))))
        row += rows
    parts.append(jnp.zeros((SMALL_ROWS - row, D_MODEL), F32))
    return jnp.concatenate(parts, axis=0)


def kernel(x, positions, attn_pre_norm, w_in, mla_q_norm, mla_w_uq, mla_kv_norm, mla_w_ukv, mla_out_norm, hgrn_lb_logits, hgrn_out_norm, w_out, attn_post_norm, ffn_pre_norm, w_gate, w_up, w_down, ffn_post_norm, loss_target, m_attn_pre_norm, m_w_in, m_mla_q_norm, m_mla_w_uq, m_mla_kv_norm, m_mla_w_ukv, m_mla_out_norm, m_hgrn_lb_logits, m_hgrn_out_norm, m_w_out, m_attn_post_norm, m_ffn_pre_norm, m_w_gate, m_w_up, m_w_down, m_ffn_post_norm, v_attn_pre_norm, v_w_in, v_mla_q_norm, v_mla_w_uq, v_mla_kv_norm, v_mla_w_ukv, v_mla_out_norm, v_hgrn_lb_logits, v_hgrn_out_norm, v_w_out, v_attn_post_norm, v_ffn_pre_norm, v_w_gate, v_w_up, v_w_down, v_ffn_post_norm):
    args = locals()
    W = {n: args[n] for n in WEIGHT_NAMES}
    M = {n: args["m_" + n] for n in WEIGHT_NAMES}
    V = {n: args["v_" + n] for n in WEIGHT_NAMES}
    T = x.shape[1]
    cx, cy, cc = lax.axis_index("x"), lax.axis_index("y"), lax.axis_index("c")

    win_rows = D_IN // N_CHIPS
    shard2d = {"w_in": (win_rows, D_MODEL), "mla_w_uq": (Q_RANK // N_CHIPS, MLA_HEADS * MLA_QK),
               "w_out": (D_MODEL // N_CHIPS, D_MODEL), "w_gate": (FF_SHARD, D_MODEL), "w_up": (FF_SHARD, D_MODEL),
               "w_down": (FF_SHARD, D_MODEL)}
    transposed = ("w_in", "w_gate", "w_up")
    to2d = lambda n, a: a[0].T if n in transposed else a.reshape(shard2d[n])
    from2d = lambda n, t: t.T[None] if n in transposed else t.reshape(W[n].shape)
    me = 2 * cx + cy
    place = jnp.stack([cc, me]).astype(jnp.int32)
    local_b = [to2d(n, W[n]).astype(BF16) for n in BIG_NAMES]
    local_b[0] = jnp.pad(local_b[0], ((0, WIN_COMM_SHAPE[1] - win_rows), (0, 0)))
    stacks = _gather_chips(local_b[:2], "gather_weights")
    win4, wuq4 = [lax.dynamic_update_slice(s, l[None], (me, 0, 0)) for s, l in zip(stacks, local_b)]
    win_t = win4[:, :win_rows].reshape(D_IN, D_MODEL)
    wuq_full = wuq4.reshape(Q_RANK, MLA_HEADS, MLA_QK)
    win_arr, wq_arr, wk_arr, wv_arr = _arrange_weights(win_t, wuq_full, mla_w_ukv[0].astype(BF16))
    small = {n: W[n][0] if n == "mla_w_ukv" else W[n].reshape(-1, W[n].shape[-1]) for n in SMALL_NAMES}

    loss_local, dx, grads, ffn_final = _local_step(x[0], positions.reshape(T, 1), loss_target[0], small, win_arr,
                                                           wq_arr, wk_arr, wv_arr, local_b[2:], place)

    gs = [grads["w_in"], grads["mla_w_uq"].reshape((N_CHIPS,) + UQ_COMM_SHAPE)]
    sm = _pack_small({**grads, "loss": loss_local})
    *rs, ssib = _pair_swap(gs, (sm,))
    *ps, pair = _pair_sum(place, gs, rs, small=(sm, ssib))
    ffn_names, rest_names = BIG_NAMES[3:], BIG_NAMES[:2]
    g2d = dict(zip(ffn_names + BIG_NAMES[2:3], ffn_final))
    adam_in = lambda names_: [(to2d(n, W[n]), g2d[n], to2d(n, M[n]), to2d(n, V[n])) for n in names_]
    updates = dict(zip(ffn_names, _adamw(adam_in(ffn_names), 8, "adamw_ffn")))
    updates.update(zip(BIG_NAMES[2:3], _adamw(adam_in(BIG_NAMES[2:3]), 8, "adamw_w_out")))
    *ris, sm4 = _chip_swap(ps, pair)
    *gfin, smf = _pair_fill(_chip_sum(place, gs, rs, ris), sm4)

    g2d.update({n: gfin[k].reshape((-1,) + shard2d[n][1:]) for k, n in enumerate(rest_names)})
    updates.update(zip(rest_names, _adamw(adam_in(rest_names), 3, "adamw_w_in")))
    G, DW, NM, NV = {}, {}, {}, {}
    for n, outs in updates.items():
        G[n], DW[n], NM[n], NV[n] = (from2d(n, t) for t in outs)
    view2d = lambda n, a: a.reshape(next((r, c) for name, r, c in SMALL_VIEWS if name == n))
    *res, loss_row = _adamw_small(smf, [tuple(view2d(n, t[n]) for t in (W, M, V)) for n in SMALL_NAMES])
    for n, outs in zip(SMALL_NAMES, res):
        G[n], DW[n], NM[n], NV[n] = (t.reshape(W[n].shape) for t in outs)
    loss = loss_row[0, 0]
    return (loss, dx[None], *[G[n] for n in WEIGHT_NAMES], *[DW[n] for n in WEIGHT_NAMES],
            *[NM[n] for n in WEIGHT_NAMES], *[NV[n] for n in WEIGHT_NAMES])
```

```python
import jax
import jax.numpy as jnp
from jax import lax
from jax.experimental import pallas as pl
from jax.experimental.pallas import tpu as pltpu

F32 = jnp.float32
BF16 = jnp.bfloat16
MXU_DTYPE = BF16

D_MODEL = 1024
MLA_HEADS = 8
MLA_NOPE = 64
MLA_ROPE = 32
MLA_V = 64
MLA_QK = MLA_NOPE + MLA_ROPE
Q_RANK = 384
KV_RANK = 128
MLA_WIDTH = MLA_HEADS * MLA_V
HEAD_PAD = 128
HGRN_HEADS = 4
HGRN_DIM = 128
HGRN_WIDTH = HGRN_HEADS * HGRN_DIM
CHUNK = 64
SUB = 16
HGRN_CPI = 4
D_IN = Q_RANK + KV_RANK + MLA_ROPE + 4 * HGRN_WIDTH
D_IN_ARR = Q_RANK + KV_RANK + HEAD_PAD + 4 * HGRN_WIDTH
D_FF = 2816
N_CHIPS = 4
FF_SHARD = D_FF // N_CHIPS
EPS = 1e-6
ROPE_THETA = 10000.0
ATTN_SCALE = MLA_QK ** -0.5
ATTN_SCALE_LOG2 = ATTN_SCALE * 1.4426950408889634
NEG_BIG = -1e30

ADAM_LR = 0.001
ADAM_B1 = 0.9
ADAM_B2 = 0.999
ADAM_EPS = 1e-08
ADAM_WD = 0.01
ADAM_STEP = 10

VMEM_LIMIT = 56 * 1024 * 1024
FFN_BWD_VMEM = 62 * 1024 * 1024

SMALL_VIEWS = (("attn_pre_norm", 1, 1024), ("mla_q_norm", 1, 384), ("mla_kv_norm", 1, 128), ("mla_w_ukv", 128, 1024),
               ("mla_out_norm", 1, 512), ("hgrn_lb_logits", 2, 512), ("hgrn_out_norm", 1, 512),
               ("attn_post_norm", 1, 1024), ("ffn_pre_norm", 1, 1024), ("ffn_post_norm", 1, 1024), ("loss", 1, 1))
ROW_TILE = 8


def _small_layout():
    offsets, row = {}, 0
    for whole in (True, False):
        for name, rows, _ in SMALL_VIEWS:
            if (rows % ROW_TILE == 0) == whole:
                offsets[name] = row
                row += rows
    return offsets, -(-row // (2 * ROW_TILE)) * 2 * ROW_TILE


SMALL_OFFSETS, SMALL_ROWS = _small_layout()

MESH = pl.DeviceIdType.MESH
ANY = pl.BlockSpec(memory_space=pl.ANY)


def _dot(a, b, dims, exact):
    if exact:
        return lax.dot_general(a.astype(F32), b.astype(F32), (dims, ((), ())), precision=lax.Precision.HIGH,
                               preferred_element_type=F32)
    return lax.dot_general(a.astype(MXU_DTYPE), b.astype(MXU_DTYPE), (dims, ((), ())), preferred_element_type=F32)


def _mm(a, b, exact=False):
    return _dot(a, b, ((1,), (0,)), exact)


def _mm_nt(a, b, exact=False):
    return _dot(a, b, ((1,), (1,)), exact)


def _mm_tn(a, b, exact=False):
    return _dot(a, b, ((0,), (0,)), exact)


def _rms_fwd(x, w):
    r = lax.rsqrt(jnp.mean(x * x, axis=-1, keepdims=True) + EPS)
    xn = x * r
    return xn * w, xn, r


def _rms_bwd(dy, xn, r, w):
    dxn = dy * w
    dx = r * (dxn - xn * jnp.mean(dxn * xn, axis=-1, keepdims=True))
    dw = jnp.sum(dy * xn, axis=0, keepdims=True)
    return dx, dw


def _group_sums(v, gs):
    t, n = v.shape
    lane = lax.broadcasted_iota(jnp.int32, (t, 128), 1)
    out = []
    for p in range(n // 128):
        vb = v[:, 128 * p:128 * (p + 1)]
        if gs == 128:
            out.append(jnp.sum(vb, axis=-1, keepdims=True))
        else:
            out.append(jnp.sum(jnp.where(lane < 64, vb, 0.0), axis=-1, keepdims=True))
            out.append(jnp.sum(jnp.where(lane >= 64, vb, 0.0), axis=-1, keepdims=True))
    return out


def _group_bcast(sums, gs, t):
    lane = lax.broadcasted_iota(jnp.int32, (t, 128), 1)
    if gs == 128:
        return jnp.concatenate([jnp.broadcast_to(s, (t, 128)) for s in sums], axis=-1)
    return jnp.concatenate([jnp.where(lane < 64, sums[2 * p], sums[2 * p + 1]) for p in range(len(sums) // 2)],
                           axis=-1)


def _grms_fwd(x, w, gs):
    t = x.shape[0]
    r = lax.rsqrt(_group_bcast(_group_sums(x * x, gs), gs, t) * (1.0 / gs) + EPS)
    xn = x * r
    return xn * w, xn, r


def _grms_bwd(dy, xn, r, w, gs):
    t = dy.shape[0]
    dxn = dy * w
    dx = r * (dxn - xn * (_group_bcast(_group_sums(dxn * xn, gs), gs, t) * (1.0 / gs)))
    dw = jnp.sum(dy * xn, axis=0, keepdims=True)
    return dx, dw


def _rope_tables(c_tab, s_tab):
    lane = lax.broadcasted_iota(jnp.int32, c_tab.shape, 1)
    first = (lane >= MLA_NOPE) & (lane < MLA_NOPE + MLA_ROPE // 2)
    second = (lane >= MLA_NOPE + MLA_ROPE // 2) & (lane < MLA_QK)
    return c_tab, jnp.where(first, -s_tab, 0.0), jnp.where(second, s_tab, 0.0)


def _rope(v, c, sa, sb):
    return v * c + pltpu.roll(v, HEAD_PAD - MLA_ROPE // 2, 1) * sa + pltpu.roll(v, MLA_ROPE // 2, 1) * sb


def _rope_bwd(d, c, sa, sb):
    return d * c - pltpu.roll(d, HEAD_PAD - MLA_ROPE // 2, 1) * sa - pltpu.roll(d, MLA_ROPE // 2, 1) * sb


def _params(sem, vmem=VMEM_LIMIT):
    return pltpu.CompilerParams(dimension_semantics=sem, vmem_limit_bytes=vmem)


def _in_fwd(x, pos, invf, w_pre, win, qnw, wq, kvnw, wk, wv, tt=512):
    T = x.shape[0]

    def body(x_ref, pos_ref, invf_ref, wpre_ref, win_ref, qnw_ref, wq_ref, kvnw_ref, wk_ref, wv_ref,
             cq_ref, ckv_ref, xph_ref, q_ref, k_ref, v_ref, kt_ref, vt_ref, rc_ref, rs_ref):
        u, _, _ = _rms_fwd(x_ref[...], wpre_ref[...])
        lo = Q_RANK + KV_RANK + HEAD_PAD
        xp = _mm_nt(u, win_ref[:lo, :])
        xph_ref[...] = _mm_nt(u, win_ref[lo:, :])
        cq = xp[:, :Q_RANK]
        ckv = xp[:, Q_RANK:Q_RANK + KV_RANK]
        kr = xp[:, Q_RANK + KV_RANK:]
        cq_ref[...] = cq
        ckv_ref[...] = ckv
        ang = pos_ref[...].astype(F32) * invf_ref[...]
        c_tab = jnp.cos(ang)
        s_tab = jnp.sin(ang)
        rc_ref[...] = c_tab
        rs_ref[...] = s_tab
        c, sa, sb = _rope_tables(c_tab, s_tab)
        qn, _, _ = _rms_fwd(cq, qnw_ref[...])
        q = _mm(qn, wq_ref[...])
        kvn, _, _ = _rms_fwd(ckv, kvnw_ref[...])
        kn = _mm(kvn, wk_ref[...])
        v = _mm(kvn, wv_ref[...])
        v_ref[...] = v.astype(v_ref.dtype)
        vt_ref[...] = v.T.astype(vt_ref.dtype)
        krr = _rope(kr, c, sa, sb)
        for h in range(MLA_HEADS):
            sl = slice(HEAD_PAD * h, HEAD_PAD * (h + 1))
            q_ref[:, sl] = (_rope(q[:, sl], c, sa, sb) * ATTN_SCALE_LOG2).astype(q_ref.dtype)
            kh = kn[:, sl] + krr
            k_ref[:, sl] = kh.astype(k_ref.dtype)
            kt_ref[sl, :] = kh.T.astype(kt_ref.dtype)

    row = lambda w: pl.BlockSpec((tt, w), lambda i: (i, 0))
    full = lambda a: pl.BlockSpec(a.shape, lambda i: (0,) * a.ndim)
    qk_w = MLA_HEADS * HEAD_PAD
    return pl.pallas_call(
        body, name="in_fwd", grid=(T // tt,),
        in_specs=[row(D_MODEL), row(1), full(invf), full(w_pre), full(win), full(qnw), full(wq), full(kvnw),
                  full(wk), full(wv)],
        out_specs=[row(Q_RANK), row(KV_RANK), row(4 * HGRN_WIDTH), row(qk_w), row(qk_w), row(MLA_WIDTH),
                   pl.BlockSpec((qk_w, tt), lambda i: (0, i)), pl.BlockSpec((MLA_WIDTH, tt), lambda i: (0, i)),
                   row(HEAD_PAD), row(HEAD_PAD)],
        out_shape=[jax.ShapeDtypeStruct((T, Q_RANK), F32), jax.ShapeDtypeStruct((T, KV_RANK), F32),
                   jax.ShapeDtypeStruct((T, 4 * HGRN_WIDTH), F32), jax.ShapeDtypeStruct((T, qk_w), MXU_DTYPE),
                   jax.ShapeDtypeStruct((T, qk_w), MXU_DTYPE), jax.ShapeDtypeStruct((T, MLA_WIDTH), MXU_DTYPE),
                   jax.ShapeDtypeStruct((qk_w, T), MXU_DTYPE), jax.ShapeDtypeStruct((MLA_WIDTH, T), MXU_DTYPE),
                   jax.ShapeDtypeStruct((T, HEAD_PAD), F32), jax.ShapeDtypeStruct((T, HEAD_PAD), F32)],
        compiler_params=_params(("arbitrary",)),
    )(x, pos, invf, w_pre, win, qnw, wq, kvnw, wk, wv)


def _attn_fwd_t(qb, kb, vt, gather=(), tq=256, hps=8):
    T = qb.shape[0]
    nq = T // tq
    ng = len(gather)
    steps = (MLA_HEADS // hps) * nq
    pass_on = steps - 3

    def body(q_ref, k_ref, vt_ref, *rest):
        o_ref, lse_ref = rest[ng:ng + 2]
        acc_scr = rest[2 * ng + 2]
        qi = pl.program_id(1)
        step_no = pl.program_id(0) * nq + qi
        if ng:
            gat = _Gather(rest[:ng], rest[ng + 2:2 * ng + 2], *rest[2 * ng + 3:])

            @pl.when(step_no == 0)
            def _():
                for cp in gat.sends():
                    cp.start()

            @pl.when(step_no == pass_on)
            def _():
                for arrival in gat.arrivals():
                    arrival.wait_recv()
                for cp in gat.forwards():
                    cp.start()

        heads = [slice(HEAD_PAD * a, HEAD_PAD * (a + 1)) for a in range(hps)]
        acc_scr[...] = jnp.zeros_like(acc_scr)

        def step(j, carry, masked):
            start = pl.multiple_of(j * tq, tq)
            scores = [_mm_nt(k_ref[pl.ds(start, tq), heads[a]], q_ref[:, heads[a]]) for a in range(hps)]
            new, probs, alphas = [], [], []
            for a in range(hps):
                m, l = carry[a]
                s = scores[a]
                if masked:
                    kk = lax.broadcasted_iota(jnp.int32, (tq, tq), 0)
                    qq = lax.broadcasted_iota(jnp.int32, (tq, tq), 1)
                    s = jnp.where(kk <= qq, s, NEG_BIG)
                m_new = jnp.maximum(m, jnp.max(s, axis=0, keepdims=True))
                alpha = jnp.exp2(m - m_new)
                p = jnp.exp2(s - m_new)
                l = l * alpha + jnp.sum(p, axis=0, keepdims=True)
                new.append((m_new, l))
                probs.append(p.astype(MXU_DTYPE))
                alphas.append(alpha)
                if a % 2:
                    pr = a // 2
                    vtj = vt_ref[2 * MLA_V * pr:2 * MLA_V * (pr + 1), pl.ds(start, tq)]
                    none = jnp.zeros((MLA_V, tq), vtj.dtype)
                    pv = (_mm(jnp.concatenate([vtj[:MLA_V], none], axis=0), probs[a - 1])
                          + _mm(jnp.concatenate([none, vtj[MLA_V:]], axis=0), probs[a]))
                    acc_scr[pr] = acc_scr[pr] * jnp.where(row < MLA_V, alphas[a - 1], alphas[a]) + pv
            return tuple(new)

        row = lax.broadcasted_iota(jnp.int32, (2 * MLA_V, tq), 0)
        init = tuple((jnp.full((1, tq), NEG_BIG, F32), jnp.zeros((1, tq), F32)) for _ in range(hps))
        carry = lax.fori_loop(0, qi, lambda j, c: step(j, c, False), init)
        carry = step(qi, carry, True)
        for pr in range(hps // 2):
            (m0, l0), (m1, l1) = carry[2 * pr], carry[2 * pr + 1]
            ot = acc_scr[pr] / jnp.where(row < MLA_V, l0, l1)
            o_ref[:, 2 * MLA_V * pr:2 * MLA_V * (pr + 1)] = ot.T
            lse_ref[pr, 0:1, :] = m0 + jnp.log2(l0)
            lse_ref[pr, 1:2, :] = m1 + jnp.log2(l1)

        if ng:
            @pl.when(step_no == steps - 1)
            def _():
                for arrival in gat.forward_arrivals():
                    arrival.wait_recv()
                for cp in gat.sends() + gat.forwards():
                    cp.wait_send()

    return pl.pallas_call(
        body, name="attn_fwd", grid=(MLA_HEADS // hps, nq),
        in_specs=[pl.BlockSpec((tq, hps * HEAD_PAD), lambda g, i: (i, g)),
                  pl.BlockSpec((T, hps * HEAD_PAD), lambda g, i: (0, g)),
                  pl.BlockSpec((hps * MLA_V, T), lambda g, i: (g, 0))] + [ANY] * ng,
        out_specs=[pl.BlockSpec((tq, hps * MLA_V), lambda g, i: (i, g)),
                   pl.BlockSpec((hps // 2, 2, tq), lambda g, i: (g, 0, i))] + [ANY] * ng,
        out_shape=[jax.ShapeDtypeStruct((T, MLA_WIDTH), F32), jax.ShapeDtypeStruct((MLA_HEADS // 2, 2, T), F32)]
        + _Gather.out_shapes(gather),
        scratch_shapes=[pltpu.VMEM((hps // 2, 2 * MLA_V, tq), F32)] + (_Gather.semaphores(gather) if ng else []),
        compiler_params=_params(("arbitrary", "arbitrary")),
    )(qb, kb, vt, *gather)


def _attn_bwd_t(qb, kb, kt, vb, dob, lse, dvec, send=(), tq=512, hps=4):
    T = qb.shape[0]
    nq = T // tq
    ns = len(send)
    steps = (MLA_HEADS // hps) * nq

    def body(q_ref, k_ref, kt_ref, v_ref, do_ref, lse_ref, d_ref, *rest):
        dqt_ref, dk_ref, dv_ref = rest[ns:ns + 3]
        va_scr, dv_scr = rest[2 * ns + 3:2 * ns + 5]
        j = pl.program_id(1)
        step_no = pl.program_id(0) * nq + j
        if ns:
            @pl.when(step_no == 0)
            def _():
                for cp in _chip_swap_copies(rest[:ns], rest[ns + 3:2 * ns + 3], *rest[2 * ns + 5:]):
                    cp.start()

        @pl.when(j == 0)
        def _():
            dqt_ref[...] = jnp.zeros_like(dqt_ref)

        lane = lax.broadcasted_iota(jnp.int32, (tq, 2 * MLA_V), 1)
        heads = [slice(HEAD_PAD * a, HEAD_PAD * (a + 1)) for a in range(hps)]
        pairs = [slice(2 * MLA_V * p, 2 * MLA_V * (p + 1)) for p in range(hps // 2)]
        for pr in range(hps // 2):
            vpair = v_ref[:, pairs[pr]]
            va_scr[2 * pr] = jnp.where(lane < MLA_V, vpair, jnp.zeros_like(vpair))
            va_scr[2 * pr + 1] = jnp.where(lane >= MLA_V, vpair, jnp.zeros_like(vpair))
        dk_ref[...] = jnp.zeros_like(dk_ref)
        dv_scr[...] = jnp.zeros_like(dv_scr)

        def step(i, masked):
            start = pl.multiple_of(i * tq, tq)
            rows = pl.ds(start, tq)
            scores = [_mm_nt(k_ref[:, heads[a]], q_ref[rows, heads[a]]) for a in range(hps)]
            dps = [_mm_nt(va_scr[a], do_ref[rows, pairs[a // 2]]) for a in range(hps)]
            for a in range(hps):
                pr, r = a // 2, a % 2
                p = jnp.exp2(scores[a] - lse_ref[pr, r:r + 1, rows])
                if masked:
                    kk = lax.broadcasted_iota(jnp.int32, (tq, tq), 0)
                    qq = lax.broadcasted_iota(jnp.int32, (tq, tq), 1)
                    p = jnp.where(kk <= qq, p, 0.0)
                ds = p * (dps[a] - d_ref[pr, r:r + 1, rows])
                dv_scr[a] += _mm(p, do_ref[rows, pairs[pr]])
                dk_ref[:, heads[a]] += _mm(ds, q_ref[rows, heads[a]])
                dqt_ref[heads[a], rows] += _mm(kt_ref[heads[a], :], ds)

        def loop_body(i, _):
            step(i, False)
            return 0

        step(j, True)
        lax.fori_loop(j + 1, nq, loop_body, 0)
        for pr in range(hps // 2):
            dv_ref[:, pairs[pr]] = jnp.where(lane < MLA_V, dv_scr[2 * pr], dv_scr[2 * pr + 1])
        dk_ref[...] = dk_ref[...] * (ATTN_SCALE / ATTN_SCALE_LOG2)

        if ns:
            @pl.when(step_no == steps - 1)
            def _():
                for cp in _chip_swap_copies(rest[:ns], rest[ns + 3:2 * ns + 3], *rest[2 * ns + 5:]):
                    cp.wait()

    stat = pl.BlockSpec((hps // 2, 2, T), lambda g, j: (g, 0, 0))
    return pl.pallas_call(
        body, name="attn_bwd", grid=(MLA_HEADS // hps, nq),
        in_specs=[pl.BlockSpec((T, hps * HEAD_PAD), lambda g, j: (0, g)),
                  pl.BlockSpec((tq, hps * HEAD_PAD), lambda g, j: (j, g)),
                  pl.BlockSpec((hps * HEAD_PAD, tq), lambda g, j: (g, j)),
                  pl.BlockSpec((tq, hps * MLA_V), lambda g, j: (j, g)),
                  pl.BlockSpec((T, hps * MLA_V), lambda g, j: (0, g)), stat, stat] + [ANY] * ns,
        out_specs=[pl.BlockSpec((hps * HEAD_PAD, T), lambda g, j: (g, 0)),
                   pl.BlockSpec((tq, hps * HEAD_PAD), lambda g, j: (j, g)),
                   pl.BlockSpec((tq, hps * MLA_V), lambda g, j: (j, g))] + [ANY] * ns,
        out_shape=[jax.ShapeDtypeStruct((MLA_HEADS * HEAD_PAD, T), F32),
                   jax.ShapeDtypeStruct((T, MLA_HEADS * HEAD_PAD), F32),
                   jax.ShapeDtypeStruct((T, MLA_WIDTH), F32)] + _chip_swap_shapes(send),
        scratch_shapes=[pltpu.VMEM((hps, tq, 2 * MLA_V), vb.dtype), pltpu.VMEM((hps, tq, 2 * MLA_V), F32)]
        + ([pltpu.SemaphoreType.DMA((3 * ns,)), pltpu.SemaphoreType.DMA((3 * ns,))] if ns else []),
        compiler_params=_params(("arbitrary", "arbitrary")),
    )(qb, kb, kt, vb, dob, lse, dvec, *send)


def _cumsum_rows(x):
    n = x.shape[0]
    row = lax.broadcasted_iota(jnp.int32, x.shape, 0)
    s = 1
    while s < n:
        x = x + jnp.where(row >= s, pltpu.roll(x, s, 0), 0.0)
        s *= 2
    return x


def _rev_cumsum_rows(x):
    n = x.shape[0]
    row = lax.broadcasted_iota(jnp.int32, x.shape, 0)
    s = 1
    while s < n:
        x = x + jnp.where(row < n - s, pltpu.roll(x, n - s, 0), 0.0)
        s *= 2
    return x


def _lb_from_logits(l):
    l0, l1 = l[0:1, :], l[1:2, :]
    m = jnp.maximum(l0, l1)
    e0, e1 = jnp.exp(l0 - m), jnp.exp(l1 - m)
    return e0 / (e0 + e1)


def _hgrn_gates(hq, hf, lb):
    sig_f = jax.nn.sigmoid(hf)
    f = lb + (1.0 - lb) * sig_f
    sig_q = jax.nn.sigmoid(hq)
    return sig_f, f, jnp.log(f), 1.0 - f, sig_q, hq * sig_q


def _hgrn_intra(q, kk, b, exact=False):
    row = lax.broadcasted_iota(jnp.int32, b.shape, 0)
    qs, ks, eqs, eks, a_rows = [], [], [], [], []
    for i in range(CHUNK // SUB):
        ref = b[SUB * i + SUB // 2:SUB * i + SUB // 2 + 1, :]
        eq = jnp.exp(b[SUB * i:SUB * (i + 1), :] - ref)
        ek = jnp.exp(jnp.where(row < SUB * (i + 1), ref - b, NEG_BIG))
        qi = q[SUB * i:SUB * (i + 1), :] * eq
        ki = kk * ek
        a_rows.append(_mm_nt(qi, ki, exact))
        qs.append(qi), ks.append(ki), eqs.append(eq), eks.append(ek)
    tt = lax.broadcasted_iota(jnp.int32, (CHUNK, CHUNK), 0)
    ss = lax.broadcasted_iota(jnp.int32, (CHUNK, CHUNK), 1)
    causal = ss <= tt
    a = jnp.where(causal, jnp.concatenate(a_rows, axis=0), 0.0)
    return a, causal, qs, ks, eqs, eks


def _hgrn_fwd(xph, lbl, tg=512):
    T = xph.shape[0]
    ng, ncg = T // tg, tg // CHUNK
    cols = [slice(HGRN_DIM * h, HGRN_DIM * (h + 1)) for h in range(HGRN_HEADS)]

    def body(lbl_ref, hq_ref, hf_ref, hi_ref, o_ref, st_ref, s_scr):
        @pl.when(pl.program_id(0) == 0)
        def _():
            s_scr[...] = jnp.zeros_like(s_scr)

        lb = _lb_from_logits(lbl_ref[...])

        def chunks(it, _):
            pre = []
            for k in range(HGRN_CPI):
                c = it * HGRN_CPI + k
                rows = pl.ds(pl.multiple_of(c * CHUNK, CHUNK), CHUNK)
                for cs in cols:
                    _, _, lf, kk, _, q = _hgrn_gates(hq_ref[rows, cs], hf_ref[rows, cs], lb[:, cs])
                    v = hi_ref[rows, cs]
                    b = _cumsum_rows(lf)
                    a = _hgrn_intra(q, kk, b)[0]
                    b_last = b[CHUNK - 1:CHUNK, :]
                    pre.append((c, rows, q * jnp.exp(b), a, v, jnp.exp(b_last), _mm_tn(v, kk * jnp.exp(b_last - b))))
            for i, (c, rows, qe, a, v, ebl, upd) in enumerate(pre):
                h = i % HGRN_HEADS
                st = s_scr[h]
                st_ref[h, c] = st
                o_ref[rows, cols[h]] = _mm_nt(qe, st) + _mm(a, v)
                s_scr[h] = st * ebl + upd
            return 0

        lax.fori_loop(0, ncg // HGRN_CPI, chunks, 0)

    col = lambda k: pl.BlockSpec((tg, HGRN_WIDTH), lambda g: (g, k))
    return pl.pallas_call(
        body, name="hgrn_fwd", grid=(ng,),
        in_specs=[pl.BlockSpec((2, HGRN_WIDTH), lambda g: (0, 0)), col(0), col(1), col(2)],
        out_specs=[col(0), pl.BlockSpec((HGRN_HEADS, ncg, HGRN_DIM, HGRN_DIM), lambda g: (0, g, 0, 0))],
        out_shape=[jax.ShapeDtypeStruct((T, HGRN_WIDTH), F32),
                   jax.ShapeDtypeStruct((HGRN_HEADS, T // CHUNK, HGRN_DIM, HGRN_DIM), F32)],
        scratch_shapes=[pltpu.VMEM((HGRN_HEADS, HGRN_DIM, HGRN_DIM), F32)],
        compiler_params=_params(("arbitrary",)),
    )(lbl, xph, xph, xph)


def _hgrn_bwd(xph, lbl, states, d_o, fill=(), tg=512):
    T = xph.shape[0]
    ng, ncg = T // tg, tg // CHUNK
    cols = [slice(HGRN_DIM * h, HGRN_DIM * (h + 1)) for h in range(HGRN_HEADS)]
    nsub = CHUNK // SUB
    nf = len(fill)

    def body(lbl_ref, hq_ref, hf_ref, hi_ref, st_ref, do_ref, *rest):
        dhq_ref, dhf_ref, dhi_ref, dlg_ref = rest[nf:nf + 4]
        ds_scr, dlb_scr = rest[2 * nf + 4:2 * nf + 6]
        fill_copies = lambda: _pair_fill_copies(rest[nf + 4:2 * nf + 4], *rest[2 * nf + 6:])
        g = pl.program_id(0)

        @pl.when(g == 0)
        def _():
            ds_scr[...] = jnp.zeros_like(ds_scr)
            dlb_scr[...] = jnp.zeros_like(dlb_scr)
            for cp in (fill_copies()[0] if nf else ()):
                cp.start()

        lb = _lb_from_logits(lbl_ref[...])

        def chunks(it, _):
            pre = []
            for k, h in ((k, h) for k in range(HGRN_CPI) for h in range(HGRN_HEADS)):
                cs = cols[h]
                c = ncg - 1 - (it * HGRN_CPI + k)
                rows = pl.ds(pl.multiple_of(c * CHUNK, CHUNK), CHUNK)
                hq = hq_ref[rows, cs]
                sig_f, f, lf, kk, sig_q, q = _hgrn_gates(hq, hf_ref[rows, cs], lb[:, cs])
                v = hi_ref[rows, cs]
                do = do_ref[rows, cs]
                b = _cumsum_rows(lf)
                eb = jnp.exp(b)
                a, causal, qs, ks, eqs, eks = _hgrn_intra(q, kk, b)
                b_last = b[CHUNK - 1:CHUNK, :]
                st = st_ref[h, c]
                pre.append(dict(h=h, cs=cs, rows=rows, hq=hq, sig_f=sig_f, f=f, kk=kk, sig_q=sig_q, q=q, v=v, eb=eb, qs=qs,
                                ks=ks, eqs=eqs,
                                eks=eks, ebl=jnp.exp(b_last), el=jnp.exp(b_last - b), st=st,
                                da=jnp.where(causal, _mm_nt(do, v, True), 0.0), dq=_mm(do, st, True) * eb,
                                dv=_mm_tn(a, do), dsu=_mm_tn(do, q * eb, True)))
            for w in pre:
                dq_rows = []
                dk = jnp.zeros_like(w["q"])
                for i in range(nsub):
                    dai = w["da"][SUB * i:SUB * (i + 1), :]
                    dq_rows.append(_mm(dai, w["ks"][i], True) * w["eqs"][i])
                    dk = dk + _mm_tn(dai, w["qs"][i], True) * w["eks"][i]
                w["dq"] = w["dq"] + jnp.concatenate(dq_rows, axis=0)
                w["dk"] = dk
            for w in pre:
                h, cs, rows = w["h"], w["cs"], w["rows"]
                kk, el, ebl, dst = w["kk"], w["el"], w["ebl"], ds_scr[h]
                dk_state = _mm(w["v"], dst, True) * el
                dk = w["dk"] + dk_state
                e_last = (ebl * jnp.sum(w["st"] * dst, axis=0, keepdims=True)
                          + jnp.sum(kk * dk_state, axis=0, keepdims=True))
                dlf = _rev_cumsum_rows(w["q"] * w["dq"] - kk * dk) + e_last
                ds_scr[h] = dst * ebl + w["dsu"]
                df = dlf / w["f"] - dk
                sig_f, sig_q = w["sig_f"], w["sig_q"]
                dhf_ref[rows, cs] = df * (1.0 - lb[:, cs]) * sig_f * (1.0 - sig_f)
                dlb_scr[:, cs] += jnp.sum(df * (1.0 - sig_f), axis=0, keepdims=True)
                dhq_ref[rows, cs] = w["dq"] * sig_q * (1.0 + w["hq"] * (1.0 - sig_q))
                dhi_ref[rows, cs] = w["dv"] + _mm_nt(kk * el, dst)
            return 0

        lax.fori_loop(0, ncg // HGRN_CPI, chunks, 0)

        @pl.when(g == ng - 1)
        def _():
            dl0 = dlb_scr[...] * lb * (1.0 - lb)
            dlg_ref[...] = jnp.concatenate([dl0, -dl0], axis=0)
            if nf:
                copies, waits = fill_copies()
                for w in waits:
                    w.wait_recv()
                for cp in copies:
                    cp.wait_send()

    col = lambda k: pl.BlockSpec((tg, HGRN_WIDTH), lambda g: (ng - 1 - g, k))
    logits = pl.BlockSpec((2, HGRN_WIDTH), lambda g: (0, 0))
    big = jax.ShapeDtypeStruct((T, HGRN_WIDTH), F32)
    n_in, n_out = 6, 4
    return pl.pallas_call(
        body, name="hgrn_bwd", grid=(ng,),
        in_specs=[logits, col(0), col(1), col(2),
                  pl.BlockSpec((HGRN_HEADS, ncg, HGRN_DIM, HGRN_DIM), lambda g: (0, ng - 1 - g, 0, 0)), col(0)] + [ANY] * nf,
        out_specs=[col(0), col(0), col(0), logits] + [ANY] * nf,
        out_shape=[big, big, big, jax.ShapeDtypeStruct((2, HGRN_WIDTH), F32)]
        + [jax.ShapeDtypeStruct(f.shape, f.dtype) for f in fill],
        input_output_aliases={n_in + k: n_out + k for k in range(nf)},
        scratch_shapes=[pltpu.VMEM((HGRN_HEADS, HGRN_DIM, HGRN_DIM), F32), pltpu.VMEM((1, HGRN_WIDTH), F32)]
        + ([pltpu.SemaphoreType.DMA((nf,)), pltpu.SemaphoreType.DMA((nf,))] if nf else []),
        compiler_params=_params(("arbitrary",)),
    )(lbl, xph, xph, xph, states, d_o, *fill)


def _proj_fwd(x, o_raw, oh_raw, xph, wout, w_mla, w_hg, w_post, w_fpre, tt=512):
    T = x.shape[0]

    def body(x_ref, o_ref, oh_ref, hg_ref, wout_ref, wmla_ref, whg_ref, wpost_ref, wfpre_ref,
             h1_ref, y1_ref, z_ref, mix_ref):
        om, _, _ = _grms_fwd(o_ref[...], wmla_ref[...], MLA_V)
        hg = hg_ref[...]
        ohn, _, _ = _grms_fwd(oh_ref[...], whg_ref[...], HGRN_DIM)
        mix = jnp.concatenate([om, ohn * (hg * jax.nn.sigmoid(hg))], axis=-1)
        mix_ref[...] = mix.astype(mix_ref.dtype)
        y1 = _mm(mix, wout_ref[...])
        y1_ref[...] = y1
        h1 = x_ref[...] + _rms_fwd(y1, wpost_ref[...])[0]
        h1_ref[...] = h1
        z_ref[...] = _rms_fwd(h1, wfpre_ref[...])[0].astype(z_ref.dtype)

    row = lambda w: pl.BlockSpec((tt, w), lambda i: (i, 0))
    full = lambda a: pl.BlockSpec(a.shape, lambda i: (0,) * a.ndim)
    sds = jax.ShapeDtypeStruct
    return pl.pallas_call(
        body, name="proj_fwd", grid=(T // tt,),
        in_specs=[row(D_MODEL), row(MLA_WIDTH), row(HGRN_WIDTH), pl.BlockSpec((tt, HGRN_WIDTH), lambda i: (i, 3)),
                  full(wout), full(w_mla), full(w_hg), full(w_post), full(w_fpre)],
        out_specs=[row(D_MODEL)] * 4,
        out_shape=[sds((T, D_MODEL), F32), sds((T, D_MODEL), F32), sds((T, D_MODEL), MXU_DTYPE),
                   sds((T, D_MODEL), MXU_DTYPE)],
        compiler_params=_params(("arbitrary",)),
    )(x, o_raw, oh_raw, xph, wout, w_mla, w_hg, w_post, w_fpre)


def _ffn_fwd(zb, h1, tgt, w_fpost, wg, wu, wd, tt=256):
    T = zb.shape[0]
    nj = N_CHIPS

    def body(z_ref, h1_ref, tgt_ref, wfpost_ref, wg_ref, wu_ref, wd_ref, g_ref, up_ref, dy2_ref, dh2_ref, loss_ref, dwf_ref):
        @pl.when(pl.program_id(0) == 0)
        def _():
            loss_ref[...] = jnp.zeros_like(loss_ref)
            dwf_ref[...] = jnp.zeros_like(dwf_ref)

        z = z_ref[...]
        gs = [_mm_nt(z, wg_ref[j]) for j in range(nj)]
        ups = [_mm_nt(z, wu_ref[j]) for j in range(nj)]
        y2 = jnp.zeros((tt, D_MODEL), F32)
        for j in range(nj):
            g_ref[j] = gs[j]
            up_ref[j] = ups[j]
            y2 = y2 + _mm(gs[j] * jax.nn.sigmoid(gs[j]) * ups[j], wd_ref[j])
        w = wfpost_ref[...]
        y2s, y2n, r2 = _rms_fwd(y2, w)
        e = h1_ref[...] + y2s - tgt_ref[...]
        loss_ref[...] += jnp.sum(e * e, axis=0, keepdims=True)
        dh2 = e * (1.0 / D_MODEL)
        dh2_ref[...] = dh2
        dy2, dwf = _rms_bwd(dh2, y2n, r2, w)
        dy2_ref[...] = dy2.astype(dy2_ref.dtype)
        dwf_ref[...] += dwf

    row = pl.BlockSpec((tt, D_MODEL), lambda i: (i, 0))
    vec = pl.BlockSpec((1, D_MODEL), lambda i: (0, 0))
    resident = pl.BlockSpec((nj, FF_SHARD, D_MODEL), lambda i: (0, 0, 0), pipeline_mode=pl.Buffered(1))
    act = pl.BlockSpec((nj, tt, FF_SHARD), lambda i: (0, i, 0))
    sds = jax.ShapeDtypeStruct
    return pl.pallas_call(
        body, name="ffn_fwd", grid=(T // tt,),
        in_specs=[row, row, row, vec, resident, resident, resident],
        out_specs=[act, act, row, row, vec, vec],
        out_shape=[sds((nj, T, FF_SHARD), F32), sds((nj, T, FF_SHARD), F32), sds((T, D_MODEL), MXU_DTYPE),
                   sds((T, D_MODEL), F32), sds((1, D_MODEL), F32), sds((1, D_MODEL), F32)],
        compiler_params=_params(("arbitrary",)),
    )(zb, h1, tgt, w_fpost, wg, wu, wd)


def _ffn_bwd(zb, g, up, dy2b, wg, wu, wd, tt=512):
    T = zb.shape[0]
    nj = N_CHIPS

    def body(z_ref, g_ref, up_ref, dy2_ref, wg_ref, wu_ref, wd_ref, dwg_ref, dwu_ref, dwd_ref, dz_ref, acc_ref):
        j, i = pl.program_id(0), pl.program_id(1)
        rows = pl.ds(pl.multiple_of(i * tt, tt), tt)

        @pl.when(i == 0)
        def _():
            dwg_ref[...] = jnp.zeros_like(dwg_ref)
            dwu_ref[...] = jnp.zeros_like(dwu_ref)
            dwd_ref[...] = jnp.zeros_like(dwd_ref)

        z, g_, up_, dy2 = z_ref[...], g_ref[0], up_ref[0], dy2_ref[...]
        sg = jax.nn.sigmoid(g_)
        act = g_ * sg
        dff = _mm_nt(dy2, wd_ref[0])
        dwd_ref[0] += _mm_tn(act * up_, dy2)
        dg = dff * up_ * sg * (1.0 + g_ * (1.0 - sg))
        dup = dff * act
        dwg_ref[0] += _mm_tn(dg, z)
        dwu_ref[0] += _mm_tn(dup, z)
        dz = _mm(dg, wg_ref[0]) + _mm(dup, wu_ref[0])

        @pl.when(j == 0)
        def _():
            acc_ref[rows, :] = dz

        @pl.when((j > 0) & (j < nj - 1))
        def _():
            acc_ref[rows, :] += dz

        @pl.when(j == nj - 1)
        def _():
            dz_ref[...] = acc_ref[rows, :] + dz

    row = pl.BlockSpec((tt, D_MODEL), lambda j, i: (i, 0))
    act = pl.BlockSpec((1, tt, FF_SHARD), lambda j, i: (j, i, 0))
    w_sh = pl.BlockSpec((1, FF_SHARD, D_MODEL), lambda j, i: (j, 0, 0))
    w_grad = jax.ShapeDtypeStruct((nj, FF_SHARD, D_MODEL), F32)
    return pl.pallas_call(
        body, name="ffn_bwd", grid=(nj, T // tt),
        in_specs=[row, act, act, row, w_sh, w_sh, w_sh],
        out_specs=[w_sh, w_sh, w_sh, pl.BlockSpec((tt, D_MODEL), lambda j, i: (jnp.where(j == nj - 1, i, 0), 0))],
        out_shape=[w_grad, w_grad, w_grad, jax.ShapeDtypeStruct((T, D_MODEL), F32)],
        scratch_shapes=[pltpu.VMEM((T, D_MODEL), F32)],
        compiler_params=_params(("arbitrary", "arbitrary"), vmem=FFN_BWD_VMEM),
    )(zb, g, up, dy2b, wg, wu, wd)


def _mid_bwd(dz, dh2, h1, y1, mixb, o_raw, oh_raw, xph, wout, w_fpre, w_post, w_mla, w_hg, swap=(), tt=512):
    T = dh2.shape[0]
    nsw = len(swap)
    n_in, n_out = 13, 10

    def body(*refs):
        (dz_ref, dh2_ref, h1_ref, y1_ref, mix_ref, o_ref, oh_ref, hg_ref, wout_ref, wfpre_ref, wpost_ref,
         wmla_ref, whg_ref) = refs[:n_in]
        (dh1_ref, dwout_ref, do_ref, doh_ref, dhg_ref, dvec_ref, dwfpre_ref, dwpost_ref, dwmla_ref,
         dwhg_ref) = refs[n_in + nsw:n_in + nsw + n_out]
        sems = refs[n_in + 2 * nsw + n_out + 1:]
        swap_copies = lambda: _pair_swap_copies(refs[n_in:n_in + nsw], refs[n_in + nsw + n_out:n_in + 2 * nsw + n_out],
                                                *sems)

        def wout_copies():
            _, _, c, _, sib, _ = _place()
            rw_ref, h = refs[n_in + 2 * nsw + n_out], D_MODEL // N_CHIPS // 2
            return [_rcopy(dwout_ref.at[pl.ds(pl.multiple_of((2 * k + 1 - c) * h, 8), h)], rw_ref.at[k], *sems, nsw + k, sib)
                    for k in range(N_CHIPS)]

        @pl.when(pl.program_id(0) == 0)
        def _():
            for r in (dwout_ref, dwfpre_ref, dwpost_ref, dwmla_ref, dwhg_ref):
                r[...] = jnp.zeros_like(r)
            for cp in (swap_copies() if nsw else ()):
                cp.start()

        dz = dz_ref[...]
        wfpre = wfpre_ref[...]
        _, h1n, r = _rms_fwd(h1_ref[...], wfpre)
        dh1_z, dwfpre = _rms_bwd(dz, h1n, r, wfpre)
        dwfpre_ref[...] += dwfpre
        dh1 = dh2_ref[...] + dh1_z
        dh1_ref[...] = dh1
        wpost = wpost_ref[...]
        _, y1n, r1 = _rms_fwd(y1_ref[...], wpost)
        dy1, dwpost = _rms_bwd(dh1, y1n, r1, wpost)
        dwpost_ref[...] += dwpost
        dmix = _mm_nt(dy1, wout_ref[...])
        dwout_ref[...] += _mm_tn(mix_ref[...], dy1)
        wmla = wmla_ref[...]
        o = o_ref[...]
        _, on, ro = _grms_fwd(o, wmla, MLA_V)
        d_o, dwmla = _grms_bwd(dmix[:, :MLA_WIDTH], on, ro, wmla, MLA_V)
        dwmla_ref[...] += dwmla
        do_ref[...] = d_o.astype(do_ref.dtype)
        hh = lax.broadcasted_iota(jnp.int32, (MLA_HEADS, MLA_WIDTH), 0)
        ll = lax.broadcasted_iota(jnp.int32, (MLA_HEADS, MLA_WIDTH), 1)
        sel = jnp.where((ll >= hh * MLA_V) & (ll < (hh + 1) * MLA_V), 1.0, 0.0)
        dvec_ref[...] = _mm_nt(sel, d_o * o, True)
        whg = whg_ref[...]
        hg = hg_ref[...]
        sg = jax.nn.sigmoid(hg)
        _, ohn, rh = _grms_fwd(oh_ref[...], whg, HGRN_DIM)
        dmh = dmix[:, MLA_WIDTH:]
        dhg_ref[...] = dmh * ohn * whg * sg * (1.0 + hg * (1.0 - sg))
        d_oh, dwhg = _grms_bwd(dmh * (hg * sg), ohn, rh, whg, HGRN_DIM)
        dwhg_ref[...] += dwhg
        doh_ref[...] = d_oh

        if nsw:
            @pl.when(pl.program_id(0) == T // tt - 1)
            def _():
                for cp in wout_copies():
                    cp.start()
                for cp in swap_copies() + wout_copies():
                    cp.wait()

    row = lambda w: pl.BlockSpec((tt, w), lambda i: (i, 0))
    full = lambda a: pl.BlockSpec(a.shape, lambda i: (0,) * a.ndim)
    vec = lambda w: pl.BlockSpec((1, w), lambda i: (0, 0))
    sds = jax.ShapeDtypeStruct
    return pl.pallas_call(
        body, name="mid_bwd", grid=(T // tt,),
        in_specs=[row(D_MODEL), row(D_MODEL), row(D_MODEL), row(D_MODEL),
                  row(D_MODEL), row(MLA_WIDTH), row(HGRN_WIDTH), pl.BlockSpec((tt, HGRN_WIDTH), lambda i: (i, 3)),
                  full(wout), vec(D_MODEL), vec(D_MODEL), vec(MLA_WIDTH), vec(HGRN_WIDTH)] + [ANY] * nsw,
        out_specs=[row(D_MODEL), full(wout), row(MLA_WIDTH), row(HGRN_WIDTH), row(HGRN_WIDTH),
                   pl.BlockSpec((MLA_HEADS, tt), lambda i: (0, i)),
                   vec(D_MODEL), vec(D_MODEL), vec(MLA_WIDTH), vec(HGRN_WIDTH)] + [ANY] * (nsw + 1 if nsw else 0),
        out_shape=[sds((T, D_MODEL), F32), sds(wout.shape, F32), sds((T, MLA_WIDTH), MXU_DTYPE), sds((T, HGRN_WIDTH), F32),
                   sds((T, HGRN_WIDTH), F32), sds((MLA_HEADS, T), F32),
                   sds((1, D_MODEL), F32), sds((1, D_MODEL), F32), sds((1, MLA_WIDTH), F32), sds((1, HGRN_WIDTH), F32)]
        + (_half_stack_shapes(list(swap) + [sds((N_CHIPS, D_MODEL // N_CHIPS, D_MODEL), F32)]) if nsw else []),
        scratch_shapes=[pltpu.SemaphoreType.DMA((nsw + N_CHIPS,)), pltpu.SemaphoreType.DMA((nsw + N_CHIPS,))] if nsw else [],
        compiler_params=_params(("arbitrary",)),
    )(dz, dh2, h1, y1, mixb, o_raw, oh_raw, xph, wout, w_fpre, w_post, w_mla, w_hg, *swap)


def _in_bwd(x, dh1, cq, ckv, dq, dk, dv, dhq, dhf, dhi, dhg, rc, rs, w_pre, win, qnw, wq, kvnw, wk, wv, tt=256):
    T = x.shape[0]

    def body(x_ref, dh1_ref, cq_ref, ckv_ref, dq_ref, dk_ref, dv_ref, dhq_ref, dhf_ref, dhi_ref, dhg_ref, rc_ref, rs_ref,
             wpre_ref, win_ref, qnw_ref, wq_ref, kvnw_ref, wk_ref, wv_ref,
             dx_ref, dwin_ref, dwq_ref, dwk_ref, dwv_ref, dwpre_ref, dqnw_ref, dkvnw_ref):
        @pl.when(pl.program_id(0) == 0)
        def _():
            for r in (dwin_ref, dwq_ref, dwk_ref, dwv_ref, dwpre_ref, dqnw_ref, dkvnw_ref):
                r[...] = jnp.zeros_like(r)

        def add_win_grad(r, first):
            for arr0, n, chip, row0 in _win_grad_segments():
                if first <= arr0 and arr0 + n <= first + r.shape[0]:
                    dwin_ref[chip, row0:row0 + n, :] += r[arr0 - first:arr0 - first + n]

        lo = Q_RANK + KV_RANK + HEAD_PAD
        dxp_h = jnp.concatenate([dhq_ref[...], dhf_ref[...], dhi_ref[...], dhg_ref[...]], axis=-1)
        du = _mm(dxp_h, win_ref[lo:, :])
        wpre = wpre_ref[...]
        u, xn, rx = _rms_fwd(x_ref[...], wpre)
        add_win_grad(_mm_tn(dxp_h, u), lo)
        c, sa, sb = _rope_tables(rc_ref[...], rs_ref[...])
        lane = lax.broadcasted_iota(jnp.int32, (tt, HEAD_PAD), 1)
        dk_all = dk_ref[...]
        dq_lin = []
        dkr = jnp.zeros((tt, HEAD_PAD), F32)
        for h in range(MLA_HEADS):
            sl = slice(HEAD_PAD * h, HEAD_PAD * (h + 1))
            dq_lin.append(_rope_bwd(dq_ref[sl, :].T * ATTN_SCALE, c, sa, sb))
            dkr = dkr + dk_all[:, sl]
        dq_lin = jnp.concatenate(dq_lin, axis=-1)
        dkr = jnp.where((lane >= MLA_NOPE) & (lane < MLA_QK), _rope_bwd(dkr, c, sa, sb), 0.0)
        qnw = qnw_ref[...]
        qn, cqn, rq = _rms_fwd(cq_ref[...], qnw)
        dwq_ref[...] += _mm_tn(qn, dq_lin)
        dcq, dqnw = _rms_bwd(_mm_nt(dq_lin, wq_ref[...]), cqn, rq, qnw)
        dqnw_ref[...] += dqnw
        kvnw = kvnw_ref[...]
        kvn, ckvn, rkv = _rms_fwd(ckv_ref[...], kvnw)
        dv_ = dv_ref[...]
        dwk_ref[...] += _mm_tn(kvn, dk_all)
        dwv_ref[...] += _mm_tn(kvn, dv_)
        dckv, dkvnw = _rms_bwd(_mm_nt(dk_all, wk_ref[...]) + _mm_nt(dv_, wv_ref[...]), ckvn, rkv, kvnw)
        dkvnw_ref[...] += dkvnw
        dxp_a = jnp.concatenate([dcq, dckv, dkr], axis=-1)
        add_win_grad(_mm_tn(dxp_a, u), 0)
        dx_u, dwpre = _rms_bwd(du + _mm(dxp_a, win_ref[:lo, :]), xn, rx, wpre)
        dwpre_ref[...] += dwpre
        dx_ref[...] = dh1_ref[...] + dx_u

    row = lambda w: pl.BlockSpec((tt, w), lambda i: (i, 0))
    full = lambda a: pl.BlockSpec(a.shape, lambda i: (0,) * a.ndim)
    sds = jax.ShapeDtypeStruct
    qk_w = MLA_HEADS * HEAD_PAD
    return pl.pallas_call(
        body, name="in_bwd", grid=(T // tt,),
        in_specs=[row(D_MODEL), row(D_MODEL), row(Q_RANK), row(KV_RANK), pl.BlockSpec((qk_w, tt), lambda i: (0, i)),
                  row(qk_w), row(MLA_WIDTH),
                  row(HGRN_WIDTH), row(HGRN_WIDTH), row(HGRN_WIDTH), row(HGRN_WIDTH), row(HEAD_PAD), row(HEAD_PAD),
                  full(w_pre), full(win), full(qnw), full(wq), full(kvnw), full(wk), full(wv)],
        out_specs=[row(D_MODEL), pl.BlockSpec(WIN_COMM_SHAPE, lambda i: (0, 0, 0)), full(wq), full(wk), full(wv),
                   full(w_pre), full(qnw), full(kvnw)],
        out_shape=[sds((T, D_MODEL), F32), sds(WIN_COMM_SHAPE, F32), sds(wq.shape, F32), sds(wk.shape, F32),
                   sds(wv.shape, F32), sds(w_pre.shape, F32), sds(qnw.shape, F32), sds(kvnw.shape, F32)],
        compiler_params=_params(("arbitrary",)),
    )(x, dh1, cq, ckv, dq, dk, dv, dhq, dhf, dhi, dhg, rc, rs, w_pre, win, qnw, wq, kvnw, wk, wv)


def _arrange_weights(win_t, wuq_full, wukv):
    dt = win_t.dtype
    z = lambda n: jnp.zeros((n, D_MODEL), dt)
    s2 = Q_RANK + KV_RANK
    win_arr = jnp.concatenate([win_t[:s2], z(MLA_NOPE), win_t[s2:s2 + MLA_ROPE], z(HEAD_PAD - MLA_QK),
                               win_t[s2 + MLA_ROPE:]], axis=0)
    wq_arr = jnp.pad(wuq_full, ((0, 0), (0, 0), (0, HEAD_PAD - MLA_QK))).reshape(Q_RANK, MLA_HEADS * HEAD_PAD)
    wk_arr = jnp.pad(wukv[:, :, :MLA_NOPE], ((0, 0), (0, 0), (0, HEAD_PAD - MLA_NOPE))).reshape(
        KV_RANK, MLA_HEADS * HEAD_PAD)
    wv_arr = wukv[:, :, MLA_NOPE:].reshape(KV_RANK, MLA_WIDTH)
    return win_arr, wq_arr, wk_arr, wv_arr


WIN_COMM_SHAPE = (N_CHIPS, -(-D_IN // N_CHIPS // 32) * 32, D_MODEL)


def _win_grad_segments():
    s2 = Q_RANK + KV_RANK
    runs = [(0, s2, 0), (s2, s2 + MLA_ROPE, MLA_NOPE), (s2 + MLA_ROPE, D_IN, HEAD_PAD - MLA_ROPE)]
    per = D_IN // N_CHIPS
    segs = []
    for lo, hi, shift in runs:
        for k in range(N_CHIPS):
            a, b = max(lo, per * k), min(hi, per * (k + 1))
            if a < b:
                segs.append((a + shift, b - a, k, a - per * k))
    return segs


def _unarrange_grads(dwq_arr, dwk_arr, dwv_arr):
    dwuq = dwq_arr.reshape(Q_RANK, MLA_HEADS, HEAD_PAD)[:, :, :MLA_QK]
    dwukv = jnp.concatenate([dwk_arr.reshape(KV_RANK, MLA_HEADS, HEAD_PAD)[:, :, :MLA_NOPE],
                             dwv_arr.reshape(KV_RANK, MLA_HEADS, MLA_V)], axis=-1)
    return dwuq, dwukv


def _rope_inv_freq():
    inv = 1.0 / (ROPE_THETA ** (jnp.arange(0, MLA_ROPE, 2, dtype=F32) / MLA_ROPE))
    z = lambda n: jnp.zeros((n,), F32)
    return jnp.concatenate([z(MLA_NOPE), inv, inv, z(HEAD_PAD - MLA_QK)]).reshape(1, HEAD_PAD)


def _local_step(x, pos, tgt, small, win_arr, wq_arr, wk_arr, wv_arr, late, place=None):
    invf = _rope_inv_freq()
    cq, ckv, xph, qb, kb, vb, kt, vt, rc, rs = _in_fwd(x, pos, invf, small["attn_pre_norm"], win_arr, small["mla_q_norm"],
                                               wq_arr, small["mla_kv_norm"], wk_arr, wv_arr)
    if place is None:
        o_raw, lse = _attn_fwd_t(qb, kb, vt)
        wout, wg, wu, wd = late
    else:
        o_raw, lse, *stacks = _attn_fwd_t(qb, kb, vt, gather=late)
        wout, wg, wu, wd = [lax.dynamic_update_slice(s, l[None], (place[1], 0, 0)) for s, l in zip(stacks, late)]
        wout = wout.reshape(D_MODEL, D_MODEL)
    oh_raw, states = _hgrn_fwd(xph, small["hgrn_lb_logits"])
    h1, y1, zb, mixb = _proj_fwd(x, o_raw, oh_raw, xph, wout, small["mla_out_norm"], small["hgrn_out_norm"],
                                 small["attn_post_norm"], small["ffn_pre_norm"])
    g, up, dy2b, dh2, loss_acc, d_fpost = _ffn_fwd(zb, h1, tgt, small["ffn_post_norm"], wg, wu, wd)
    dwg, dwu, dwd, dz = _ffn_bwd(zb, g, up, dy2b, wg, wu, wd)
    ffn_grads = [] if place is None else [dwg, dwu, dwd]
    dh1, dwout, d_o, d_oh, dhg, dvec, d_fpre, d_post, d_mla, d_hg, *ffn_rs = _mid_bwd(
        dz, dh2, h1, y1, mixb, o_raw, oh_raw, xph, wout, small["ffn_pre_norm"], small["attn_post_norm"],
        small["mla_out_norm"], small["hgrn_out_norm"], swap=ffn_grads)
    if ffn_grads:
        ffn_grads = ffn_grads + [dwout.reshape(N_CHIPS, D_MODEL // N_CHIPS, D_MODEL)]
    ffn_ps = _pair_sum(place, ffn_grads, ffn_rs, name="pair_sum_ffn") if ffn_grads else []
    dq, dk, dv, *ffn_ris = _attn_bwd_t(qb, kb, kt, vb, d_o, lse, dvec.reshape(lse.shape), send=ffn_ps)
    ffn_sums = _chip_sum(place, ffn_grads, ffn_rs, ffn_ris, name="chip_sum_ffn") if ffn_grads else []
    dhq, dhf, dhi, d_lbl, *ffn_final = _hgrn_bwd(xph, small["hgrn_lb_logits"], states, d_oh, fill=ffn_sums)
    dx, dwin4, dwq_arr, dwk_arr, dwv_arr, d_pre, d_qn, d_kvn = _in_bwd(
        x, dh1, cq, ckv, dq, dk, dv, dhq, dhf, dhi, dhg, rc, rs, small["attn_pre_norm"], win_arr,
        small["mla_q_norm"], wq_arr, small["mla_kv_norm"], wk_arr, wv_arr)
    dwuq, dwukv = _unarrange_grads(dwq_arr, dwk_arr, dwv_arr)
    loss = 0.5 * jnp.sum(loss_acc) * (1.0 / D_MODEL)
    grads = dict(attn_pre_norm=d_pre, w_in=dwin4, mla_q_norm=d_qn, mla_w_uq=dwuq, mla_kv_norm=d_kvn, mla_w_ukv=dwukv,
                 mla_out_norm=d_mla, hgrn_lb_logits=d_lbl, hgrn_out_norm=d_hg, w_out=dwout, attn_post_norm=d_post,
                 ffn_pre_norm=d_fpre, w_gate=dwg, w_up=dwu, w_down=dwd, ffn_post_norm=d_fpost)
    if place is None:
        return loss, dx, grads
    return loss, dx, grads, ffn_final


def _place():
    x, y, c = lax.axis_index("x"), lax.axis_index("y"), lax.axis_index("c")
    others = [(1 - x, y), (x, 1 - y), (1 - x, 1 - y)]
    return x, y, c, 2 * x + y, (x, y, 1 - c), others


def _half(ref, c, rows):
    return ref.at[pl.ds(pl.multiple_of(c * rows, 8), rows)]


def _rcopy(src, dst, send, recv, k, to):
    return pltpu.make_async_remote_copy(src_ref=src, dst_ref=dst, send_sem=send.at[k], recv_sem=recv.at[k],
                                        device_id=to, device_id_type=MESH)


class _Gather:
    def __init__(self, ins, outs, send, recv):
        self.ins, self.outs, self.send, self.recv = ins, outs, send, recv
        self.n = len(ins)
        self.halves = [r.shape[0] // 2 for r in ins]
        _, _, self.c, self.me, self.sib, self.others = _place()

    def _each(self):
        for j, (px, py) in enumerate(self.others):
            for a in range(self.n):
                yield j * self.n + a, a, 2 * px + py, (px, py, self.c)

    def sends(self):
        return [_rcopy(_half(self.ins[a], self.c, self.halves[a]), _half(self.outs[a].at[self.me], self.c, self.halves[a]),
                       self.send, self.recv, k, to) for k, a, _, to in self._each()]

    def arrivals(self):
        parts = [(k, _half(self.outs[a].at[chip], self.c, self.halves[a]), to) for k, a, chip, to in self._each()]
        return [_rcopy(p, p, self.send, self.recv, k, to) for k, p, to in parts]

    def forwards(self):
        parts = [(k, _half(self.outs[a].at[chip], self.c, self.halves[a])) for k, a, chip, _ in self._each()]
        return [_rcopy(p, p, self.send, self.recv, 3 * self.n + k, self.sib) for k, p in parts]

    def forward_arrivals(self):
        parts = [(k, _half(self.outs[a].at[chip], 1 - self.c, self.halves[a])) for k, a, chip, _ in self._each()]
        return [_rcopy(p, p, self.send, self.recv, 3 * self.n + k, self.sib) for k, p in parts]

    @staticmethod
    def out_shapes(arrs):
        return [jax.ShapeDtypeStruct((N_CHIPS,) + a.shape, a.dtype) for a in arrs]

    @staticmethod
    def semaphores(arrs):
        return [pltpu.SemaphoreType.DMA((6 * len(arrs),)), pltpu.SemaphoreType.DMA((6 * len(arrs),))]


def _gather_chips(arrs, name):
    n = len(arrs)

    def body(*refs):
        gat = _Gather(refs[:n], refs[n:2 * n], *refs[2 * n:])
        sends, forwards = gat.sends(), gat.forwards()
        for cp in sends:
            cp.start()
        for arrival, fw in zip(gat.arrivals(), forwards):
            arrival.wait_recv()
            fw.start()
        for arrival in gat.forward_arrivals():
            arrival.wait_recv()
        for cp in sends + forwards:
            cp.wait_send()

    return pl.pallas_call(body, name=name, in_specs=[ANY] * n, out_specs=[ANY] * n, out_shape=_Gather.out_shapes(arrs),
                          scratch_shapes=_Gather.semaphores(arrs))(*arrs)


def _grad_blocks(gs):
    return 2 if all(g.shape[1] // 2 % 32 == 0 for g in gs) else 1


def _pair_swap_copies(g_refs, r_refs, send, recv):
    _, _, c, _, sib, _ = _place()
    copies = []
    for a, (g, r) in enumerate(zip(g_refs, r_refs)):
        h = g.shape[1] // 2
        copies.append(_rcopy(g.at[:, pl.ds(pl.multiple_of((1 - c) * h, 8), h)], r, send, recv, a, sib))
    return copies


def _half_stack_shapes(gs, dtype=None):
    return [jax.ShapeDtypeStruct((N_CHIPS, g.shape[1] // 2, g.shape[2]), dtype or g.dtype) for g in gs]


def _pair_swap(gs, wholes):
    n, nw = len(gs), len(wholes)

    def body(*refs):
        ins, outs, (send, recv) = refs[:n + nw], refs[n + nw:2 * (n + nw)], refs[2 * (n + nw):]
        copies = _pair_swap_copies(ins[:n], outs[:n], send, recv)
        copies += [_rcopy(ins[n + k], outs[n + k], send, recv, n + k, _place()[4]) for k in range(nw)]
        for cp in copies:
            cp.start()
        for cp in copies:
            cp.wait()

    return pl.pallas_call(
        body, name="pair_swap", in_specs=[ANY] * (n + nw), out_specs=[ANY] * (n + nw),
        out_shape=_half_stack_shapes(gs) + [jax.ShapeDtypeStruct(w.shape, w.dtype) for w in wholes],
        scratch_shapes=[pltpu.SemaphoreType.DMA((n + nw,)), pltpu.SemaphoreType.DMA((n + nw,))],
    )(*gs, *wholes)


def _pair_sum(place, gs, rs, small=None, name="pair_sum"):
    n = len(gs)
    nb = _grad_blocks(gs)

    def body(place_ref, *refs):
        g_refs, r_refs, p_refs = refs[:n], refs[n:2 * n], refs[-n - 1:-1] if small else refs[-n:]
        for a in range(n):
            p_refs[a][0] = (g_refs[a][0] + r_refs[a][0]).astype(p_refs[a].dtype)
        if small:
            @pl.when((pl.program_id(0) == 0) & (pl.program_id(1) == 0))
            def _():
                refs[-1][...] = refs[2 * n][...] + refs[2 * n + 1][...]

    in_specs, out_specs = [], []
    for g in gs:
        blk = (1, g.shape[1] // 2 // nb, g.shape[2])
        in_specs.append(pl.BlockSpec(blk, lambda i, k, p: (k, p[0] * nb + i, 0)))
    for g in gs:
        blk = (1, g.shape[1] // 2 // nb, g.shape[2])
        in_specs.append(pl.BlockSpec(blk, lambda i, k, p: (k, i, 0)))
        out_specs.append(pl.BlockSpec(blk, lambda i, k, p: (k, i, 0)))
    out_shape = _half_stack_shapes(gs, BF16)
    if small:
        sm_spec = pl.BlockSpec(small[0].shape, lambda i, k, p: (0, 0))
        in_specs += [sm_spec, sm_spec]
        out_specs.append(sm_spec)
        out_shape.append(jax.ShapeDtypeStruct(small[0].shape, F32))
    return pl.pallas_call(
        body, name=name,
        grid_spec=pltpu.PrefetchScalarGridSpec(num_scalar_prefetch=1, grid=(nb, N_CHIPS), in_specs=in_specs,
                                               out_specs=out_specs),
        out_shape=out_shape,
        compiler_params=_params(("arbitrary", "arbitrary")),
    )(place, *gs, *rs, *(small or ()))


def _chip_swap_copies(p_refs, ri_refs, send, recv):
    _, _, c, _, _, others = _place()
    n = len(p_refs)
    return [_rcopy(p_refs[a].at[2 * px + py], ri_refs[a].at[j], send, recv, j * n + a, (px, py, c))
            for j, (px, py) in enumerate(others) for a in range(n)]


def _chip_swap_shapes(ps):
    return [jax.ShapeDtypeStruct((3,) + p.shape[1:], p.dtype) for p in ps]


def _chip_swap(ps, pair):
    n = len(ps)

    def body(*refs):
        start, finish = _chip_swap_plan(refs[:n], refs[n], refs[n + 1:2 * n + 1], refs[2 * n + 1], *refs[2 * n + 2:])
        start()
        finish()

    return pl.pallas_call(
        body, name="chip_swap", in_specs=[ANY] * (n + 1), out_specs=[ANY] * (n + 1),
        out_shape=_chip_swap_out_shapes(ps, pair), scratch_shapes=_chip_swap_semaphores(n),
    )(*ps, pair)


def _chip_swap_plan(p_refs, pair_ref, ri_refs, sm4_ref, send, recv, lsem):
    n = len(p_refs)
    hs = SMALL_ROWS // 2
    x, y, c, me, sib, others = _place()
    local = pltpu.make_async_copy(pair_ref, sm4_ref.at[me], lsem.at[0])
    copies = _chip_swap_copies(p_refs, ri_refs, send, recv)
    arrivals = list(copies)
    for j, (px, py) in enumerate(others):
        copies.append(_rcopy(_half(pair_ref, c, hs), _half(sm4_ref.at[me], c, hs), send, recv, 3 * n + j, (px, py, c)))
        part = _half(sm4_ref.at[2 * px + py], c, hs)
        arrivals.append(_rcopy(part, part, send, recv, 3 * n + j, (px, py, c)))

    def start():
        local.start()
        for cp in copies:
            cp.start()

    def finish():
        for arrival in arrivals:
            arrival.wait_recv()
        for cp in copies:
            cp.wait_send()
        local.wait()

    return start, finish


def _chip_swap_out_shapes(ps, pair):
    return _chip_swap_shapes(ps) + [jax.ShapeDtypeStruct((N_CHIPS,) + pair.shape, pair.dtype)]


def _chip_swap_semaphores(n):
    k = 3 * (n + 1)
    return [pltpu.SemaphoreType.DMA((k,)), pltpu.SemaphoreType.DMA((k,)), pltpu.SemaphoreType.DMA((1,))]


def _chip_sum(place, gs, rs, ris, name="chip_sum"):
    n = len(gs)
    nb = 1

    def body(place_ref, *refs):
        g_refs, r_refs, ri_refs, o_refs = refs[:n], refs[n:2 * n], refs[2 * n:3 * n], refs[3 * n:]
        for a in range(n):
            ri = ri_refs[a]
            o_refs[a][...] = (g_refs[a][0] + r_refs[a][0]) + ri[0].astype(F32) + ri[1].astype(F32) + ri[2].astype(F32)

    in_specs, out_specs, out_shape = [], [], []
    for g in gs:
        blk = (1, g.shape[1] // 2 // nb, g.shape[2])
        in_specs.append(pl.BlockSpec(blk, lambda i, p: (p[1], p[0] * nb + i, 0)))
    for g in gs:
        blk = (1, g.shape[1] // 2 // nb, g.shape[2])
        in_specs.append(pl.BlockSpec(blk, lambda i, p: (p[1], i, 0)))
    for g in gs:
        rb = g.shape[1] // 2 // nb
        in_specs.append(pl.BlockSpec((3, rb, g.shape[2]), lambda i, p: (0, i, 0)))
        out_specs.append(pl.BlockSpec((rb, g.shape[2]), lambda i, p: (p[0] * nb + i, 0)))
        out_shape.append(jax.ShapeDtypeStruct(g.shape[1:], F32))
    return pl.pallas_call(
        body, name=name,
        grid_spec=pltpu.PrefetchScalarGridSpec(num_scalar_prefetch=1, grid=(nb,), in_specs=in_specs, out_specs=out_specs),
        out_shape=out_shape,
        compiler_params=_params(("arbitrary",)),
    )(place, *gs, *rs, *ris)


def _pair_fill_copies(g_refs, send, recv):
    _, _, c, _, sib, _ = _place()
    copies, waits = [], []
    for a, g in enumerate(g_refs):
        h = g.shape[0] // 2
        mine, theirs = _half(g, c, h), _half(g, 1 - c, h)
        copies.append(_rcopy(mine, mine, send, recv, a, sib))
        waits.append(_rcopy(theirs, theirs, send, recv, a, sib))
    return copies, waits


def _pair_fill(gfs, sm4):
    n = len(gfs)
    hs = SMALL_ROWS // 2

    def body(*refs):
        g_refs, sm4_ref = refs[n + 1:2 * n + 1], refs[2 * n + 1]
        send, recv = refs[2 * n + 2:]
        x, y, c, me, sib, others = _place()
        copies, waits = _pair_fill_copies(g_refs, send, recv)
        for j, (px, py) in enumerate(others):
            chip = 2 * px + py
            mine, theirs = _half(sm4_ref.at[chip], c, hs), _half(sm4_ref.at[chip], 1 - c, hs)
            copies.append(pltpu.make_async_remote_copy(src_ref=mine, dst_ref=mine, send_sem=send.at[n + j],
                                                       recv_sem=recv.at[n + j], device_id=sib, device_id_type=MESH))
            waits.append(pltpu.make_async_remote_copy(src_ref=theirs, dst_ref=theirs, send_sem=send.at[n + j],
                                                      recv_sem=recv.at[n + j], device_id=sib, device_id_type=MESH))
        for cp in copies:
            cp.start()
        for w in waits:
            w.wait_recv()
        for cp in copies:
            cp.wait_send()

    return pl.pallas_call(
        body, name="pair_fill", in_specs=[ANY] * (n + 1), out_specs=[ANY] * (n + 1),
        out_shape=[jax.ShapeDtypeStruct(g.shape, g.dtype) for g in gfs] + [jax.ShapeDtypeStruct(sm4.shape, sm4.dtype)],
        input_output_aliases={i: i for i in range(n + 1)},
        scratch_shapes=[pltpu.SemaphoreType.DMA((n + 3,)), pltpu.SemaphoreType.DMA((n + 3,))],
    )(*gfs, sm4)


def _adamw_math(w, g, m, v):
    m = ADAM_B1 * m + (1.0 - ADAM_B1) * g
    v = ADAM_B2 * v + (1.0 - ADAM_B2) * (g * g)
    m_hat = m / (1.0 - ADAM_B1 ** ADAM_STEP)
    v_hat = v / (1.0 - ADAM_B2 ** ADAM_STEP)
    return -ADAM_LR * (m_hat / (jnp.sqrt(v_hat) + ADAM_EPS) + ADAM_WD * w), m, v


def _adamw(items, steps, name):
    n = len(items)

    def body(*refs):
        for a in range(n):
            g = refs[4 * a + 1][...]
            d, mo, vo = _adamw_math(refs[4 * a][...], g, refs[4 * a + 2][...], refs[4 * a + 3][...])
            for out, val in zip(refs[4 * n + 4 * a:4 * n + 4 * a + 4], (g, d, mo, vo)):
                out[...] = val

    spec = lambda w: pl.BlockSpec((w.shape[0] // steps, w.shape[1]), lambda i: (i, 0))
    flat = pl.pallas_call(
        body, name=name, grid=(steps,), in_specs=[spec(it[0]) for it in items for _ in range(4)],
        out_specs=[spec(it[0]) for it in items for _ in range(4)],
        out_shape=[jax.ShapeDtypeStruct(it[0].shape, F32) for it in items for _ in range(4)],
        compiler_params=_params(("arbitrary",)),
    )(*[a for it in items for a in it])
    return [flat[4 * a:4 * a + 4] for a in range(n)]


def _adamw_small(sm4, wmv):
    views = SMALL_VIEWS[:-1]
    n = len(views)

    def body(sm4_ref, *refs):
        g_all = ((sm4_ref[0] + sm4_ref[1]) + sm4_ref[2]) + sm4_ref[3]
        for a, (name, rows, cols) in enumerate(views):
            row = SMALL_OFFSETS[name]
            g = g_all[row:row + rows, :cols]
            d, mo, vo = _adamw_math(refs[3 * a][...], g, refs[3 * a + 1][...], refs[3 * a + 2][...])
            for out, val in zip(refs[3 * n + 4 * a:3 * n + 4 * a + 4], (g, d, mo, vo)):
                out[...] = val
        row = SMALL_OFFSETS["loss"]
        refs[-1][...] = g_all[row:row + 1, :128]

    flat = pl.pallas_call(
        body, name="adamw_small",
        out_shape=[jax.ShapeDtypeStruct((rows, cols), F32) for _, rows, cols in views for _ in range(4)]
        + [jax.ShapeDtypeStruct((1, 128), F32)],
        compiler_params=pltpu.CompilerParams(vmem_limit_bytes=VMEM_LIMIT),
    )(sm4, *[a for t in wmv for a in t])
    return [flat[4 * a:4 * a + 4] for a in range(n)] + [flat[-1]]


SMALL_NAMES = ("attn_pre_norm", "mla_q_norm", "mla_kv_norm", "mla_w_ukv", "mla_out_norm", "hgrn_lb_logits",
               "hgrn_out_norm", "attn_post_norm", "ffn_pre_norm", "ffn_post_norm")
BIG_NAMES = ("w_in", "mla_w_uq", "w_out", "w_gate", "w_up", "w_down")
WEIGHT_NAMES = ("attn_pre_norm", "w_in", "mla_q_norm", "mla_w_uq", "mla_kv_norm", "mla_w_ukv", "mla_out_norm",
                "hgrn_lb_logits", "hgrn_out_norm", "w_out", "attn_post_norm", "ffn_pre_norm", "w_gate", "w_up", "w_down",
                "ffn_post_norm")


UQ_COMM_SHAPE = (192, 384)


def _pack_small(vals):
    parts, row = [], 0
    for name, rows, cols in sorted(SMALL_VIEWS, key=lambda view: SMALL_OFFSETS[view[0]]):
        assert SMALL_OFFSETS[name] == row
        parts.append(jnp.pad(vals[name].reshape(rows, cols), ((0, 0), (0, D_MODEL - cols))))
        row += rows
    parts.append(jnp.zeros((SMALL_ROWS - row, D_MODEL), F32))
    return jnp.concatenate(parts, axis=0)


def kernel(x, positions, attn_pre_norm, w_in, mla_q_norm, mla_w_uq, mla_kv_norm, mla_w_ukv, mla_out_norm, hgrn_lb_logits, hgrn_out_norm, w_out, attn_post_norm, ffn_pre_norm, w_gate, w_up, w_down, ffn_post_norm, loss_target, m_attn_pre_norm, m_w_in, m_mla_q_norm, m_mla_w_uq, m_mla_kv_norm, m_mla_w_ukv, m_mla_out_norm, m_hgrn_lb_logits, m_hgrn_out_norm, m_w_out, m_attn_post_norm, m_ffn_pre_norm, m_w_gate, m_w_up, m_w_down, m_ffn_post_norm, v_attn_pre_norm, v_w_in, v_mla_q_norm, v_mla_w_uq, v_mla_kv_norm, v_mla_w_ukv, v_mla_out_norm, v_hgrn_lb_logits, v_hgrn_out_norm, v_w_out, v_attn_post_norm, v_ffn_pre_norm, v_w_gate, v_w_up, v_w_down, v_ffn_post_norm):
    args = locals()
    W = {n: args[n] for n in WEIGHT_NAMES}
    M = {n: args["m_" + n] for n in WEIGHT_NAMES}
    V = {n: args["v_" + n] for n in WEIGHT_NAMES}
    T = x.shape[1]
    cx, cy, cc = lax.axis_index("x"), lax.axis_index("y"), lax.axis_index("c")

    win_rows = D_IN // N_CHIPS
    shard2d = {"w_in": (win_rows, D_MODEL), "mla_w_uq": (Q_RANK // N_CHIPS, MLA_HEADS * MLA_QK),
               "w_out": (D_MODEL // N_CHIPS, D_MODEL), "w_gate": (FF_SHARD, D_MODEL), "w_up": (FF_SHARD, D_MODEL),
               "w_down": (FF_SHARD, D_MODEL)}
    transposed = ("w_in", "w_gate", "w_up")
    to2d = lambda n, a: a[0].T if n in transposed else a.reshape(shard2d[n])
    from2d = lambda n, t: t.T[None] if n in transposed else t.reshape(W[n].shape)
    me = 2 * cx + cy
    place = jnp.stack([cc, me]).astype(jnp.int32)
    local_b = [to2d(n, W[n]).astype(BF16) for n in BIG_NAMES]
    local_b[0] = jnp.pad(local_b[0], ((0, WIN_COMM_SHAPE[1] - win_rows), (0, 0)))
    stacks = _gather_chips(local_b[:2], "gather_weights")
    win4, wuq4 = [lax.dynamic_update_slice(s, l[None], (me, 0, 0)) for s, l in zip(stacks, local_b)]
    win_t = win4[:, :win_rows].reshape(D_IN, D_MODEL)
    wuq_full = wuq4.reshape(Q_RANK, MLA_HEADS, MLA_QK)
    win_arr, wq_arr, wk_arr, wv_arr = _arrange_weights(win_t, wuq_full, mla_w_ukv[0].astype(BF16))
    small = {n: W[n][0] if n == "mla_w_ukv" else W[n].reshape(-1, W[n].shape[-1]) for n in SMALL_NAMES}

    loss_local, dx, grads, ffn_final = _local_step(x[0], positions.reshape(T, 1), loss_target[0], small, win_arr,
                                                           wq_arr, wk_arr, wv_arr, local_b[2:], place)

    gs = [grads["w_in"], grads["mla_w_uq"].reshape((N_CHIPS,) + UQ_COMM_SHAPE)]
    sm = _pack_small({**grads, "loss": loss_local})
    *rs, ssib = _pair_swap(gs, (sm,))
    *ps, pair = _pair_sum(place, gs, rs, small=(sm, ssib))
    ffn_names, rest_names = BIG_NAMES[3:], BIG_NAMES[:2]
    g2d = dict(zip(ffn_names + BIG_NAMES[2:3], ffn_final))
    adam_in = lambda names_: [(to2d(n, W[n]), g2d[n], to2d(n, M[n]), to2d(n, V[n])) for n in names_]
    updates = dict(zip(ffn_names, _adamw(adam_in(ffn_names), 8, "adamw_ffn")))
    updates.update(zip(BIG_NAMES[2:3], _adamw(adam_in(BIG_NAMES[2:3]), 8, "adamw_w_out")))
    *ris, sm4 = _chip_swap(ps, pair)
    *gfin, smf = _pair_fill(_chip_sum(place, gs, rs, ris), sm4)

    g2d.update({n: gfin[k].reshape((-1,) + shard2d[n][1:]) for k, n in enumerate(rest_names)})
    updates.update(zip(rest_names, _adamw(adam_in(rest_names), 3, "adamw_w_in")))
    G, DW, NM, NV = {}, {}, {}, {}
    for n, outs in updates.items():
        G[n], DW[n], NM[n], NV[n] = (from2d(n, t) for t in outs)
    view2d = lambda n, a: a.reshape(next((r, c) for name, r, c in SMALL_VIEWS if name == n))
    *res, loss_row = _adamw_small(smf, [tuple(view2d(n, t[n]) for t in (W, M, V)) for n in SMALL_NAMES])
    for n, outs in zip(SMALL_NAMES, res):
        G[n], DW[n], NM[n], NV[n] = (t.reshape(W[n].shape) for t in outs)
    loss = loss_row[0, 0]
    return (loss, dx[None], *[G[n] for n in WEIGHT_NAMES], *[DW[n] for n in WEIGHT_NAMES],
            *[NM[n] for n in WEIGHT_NAMES], *[NV[n] for n in WEIGHT_NAMES])
```

```python
import jax
import jax.numpy as jnp
from jax import lax
from jax.experimental import pallas as pl
from jax.experimental.pallas import tpu as pltpu

F32 = jnp.float32
BF16 = jnp.bfloat16
MXU_DTYPE = BF16

D_MODEL = 1024
MLA_HEADS = 8
MLA_NOPE = 64
MLA_ROPE = 32
MLA_V = 64
MLA_QK = MLA_NOPE + MLA_ROPE
Q_RANK = 384
KV_RANK = 128
MLA_WIDTH = MLA_HEADS * MLA_V
HEAD_PAD = 128
HGRN_HEADS = 4
HGRN_DIM = 128
HGRN_WIDTH = HGRN_HEADS * HGRN_DIM
CHUNK = 64
SUB = 16
HGRN_CPI = 4
D_IN = Q_RANK + KV_RANK + MLA_ROPE + 4 * HGRN_WIDTH
D_IN_ARR = Q_RANK + KV_RANK + HEAD_PAD + 4 * HGRN_WIDTH
D_FF = 2816
N_CHIPS = 4
FF_SHARD = D_FF // N_CHIPS
EPS = 1e-6
ROPE_THETA = 10000.0
ATTN_SCALE = MLA_QK ** -0.5
ATTN_SCALE_LOG2 = ATTN_SCALE * 1.4426950408889634
NEG_BIG = -1e30

ADAM_LR = 0.001
ADAM_B1 = 0.9
ADAM_B2 = 0.999
ADAM_EPS = 1e-08
ADAM_WD = 0.01
ADAM_STEP = 10

VMEM_LIMIT = 56 * 1024 * 1024
FFN_BWD_VMEM = 62 * 1024 * 1024

SMALL_VIEWS = (("attn_pre_norm", 1, 1024), ("mla_q_norm", 1, 384), ("mla_kv_norm", 1, 128), ("mla_w_ukv", 128, 1024),
               ("mla_out_norm", 1, 512), ("hgrn_lb_logits", 2, 512), ("hgrn_out_norm", 1, 512),
               ("attn_post_norm", 1, 1024), ("ffn_pre_norm", 1, 1024), ("ffn_post_norm", 1, 1024), ("loss", 1, 1))
ROW_TILE = 8


def _small_layout():
    offsets, row = {}, 0
    for whole in (True, False):
        for name, rows, _ in SMALL_VIEWS:
            if (rows % ROW_TILE == 0) == whole:
                offsets[name] = row
                row += rows
    return offsets, -(-row // (2 * ROW_TILE)) * 2 * ROW_TILE


SMALL_OFFSETS, SMALL_ROWS = _small_layout()

MESH = pl.DeviceIdType.MESH
ANY = pl.BlockSpec(memory_space=pl.ANY)


def _dot(a, b, dims, exact):
    if exact:
        return lax.dot_general(a.astype(F32), b.astype(F32), (dims, ((), ())), precision=lax.Precision.HIGH,
                               preferred_element_type=F32)
    return lax.dot_general(a.astype(MXU_DTYPE), b.astype(MXU_DTYPE), (dims, ((), ())), preferred_element_type=F32)


def _mm(a, b, exact=False):
    return _dot(a, b, ((1,), (0,)), exact)


def _mm_nt(a, b, exact=False):
    return _dot(a, b, ((1,), (1,)), exact)


def _mm_tn(a, b, exact=False):
    return _dot(a, b, ((0,), (0,)), exact)


def _rms_fwd(x, w):
    r = lax.rsqrt(jnp.mean(x * x, axis=-1, keepdims=True) + EPS)
    xn = x * r
    return xn * w, xn, r


def _rms_bwd(dy, xn, r, w):
    dxn = dy * w
    dx = r * (dxn - xn * jnp.mean(dxn * xn, axis=-1, keepdims=True))
    dw = jnp.sum(dy * xn, axis=0, keepdims=True)
    return dx, dw


def _group_sums(v, gs):
    t, n = v.shape
    lane = lax.broadcasted_iota(jnp.int32, (t, 128), 1)
    out = []
    for p in range(n // 128):
        vb = v[:, 128 * p:128 * (p + 1)]
        if gs == 128:
            out.append(jnp.sum(vb, axis=-1, keepdims=True))
        else:
            out.append(jnp.sum(jnp.where(lane < 64, vb, 0.0), axis=-1, keepdims=True))
            out.append(jnp.sum(jnp.where(lane >= 64, vb, 0.0), axis=-1, keepdims=True))
    return out


def _group_bcast(sums, gs, t):
    lane = lax.broadcasted_iota(jnp.int32, (t, 128), 1)
    if gs == 128:
        return jnp.concatenate([jnp.broadcast_to(s, (t, 128)) for s in sums], axis=-1)
    return jnp.concatenate([jnp.where(lane < 64, sums[2 * p], sums[2 * p + 1]) for p in range(len(sums) // 2)],
                           axis=-1)


def _grms_fwd(x, w, gs):
    t = x.shape[0]
    r = lax.rsqrt(_group_bcast(_group_sums(x * x, gs), gs, t) * (1.0 / gs) + EPS)
    xn = x * r
    return xn * w, xn, r


def _grms_bwd(dy, xn, r, w, gs):
    t = dy.shape[0]
    dxn = dy * w
    dx = r * (dxn - xn * (_group_bcast(_group_sums(dxn * xn, gs), gs, t) * (1.0 / gs)))
    dw = jnp.sum(dy * xn, axis=0, keepdims=True)
    return dx, dw


def _rope_tables(c_tab, s_tab):
    lane = lax.broadcasted_iota(jnp.int32, c_tab.shape, 1)
    first = (lane >= MLA_NOPE) & (lane < MLA_NOPE + MLA_ROPE // 2)
    second = (lane >= MLA_NOPE + MLA_ROPE // 2) & (lane < MLA_QK)
    return c_tab, jnp.where(first, -s_tab, 0.0), jnp.where(second, s_tab, 0.0)


def _rope(v, c, sa, sb):
    return v * c + pltpu.roll(v, HEAD_PAD - MLA_ROPE // 2, 1) * sa + pltpu.roll(v, MLA_ROPE // 2, 1) * sb


def _rope_bwd(d, c, sa, sb):
    return d * c - pltpu.roll(d, HEAD_PAD - MLA_ROPE // 2, 1) * sa - pltpu.roll(d, MLA_ROPE // 2, 1) * sb


def _params(sem, vmem=VMEM_LIMIT):
    return pltpu.CompilerParams(dimension_semantics=sem, vmem_limit_bytes=vmem)


def _in_fwd(x, pos, invf, w_pre, win, qnw, wq, kvnw, wk, wv, tt=512):
    T = x.shape[0]

    def body(x_ref, pos_ref, invf_ref, wpre_ref, win_ref, qnw_ref, wq_ref, kvnw_ref, wk_ref, wv_ref,
             cq_ref, ckv_ref, xph_ref, q_ref, k_ref, v_ref, kt_ref, vt_ref, rc_ref, rs_ref):
        u, _, _ = _rms_fwd(x_ref[...], wpre_ref[...])
        lo = Q_RANK + KV_RANK + HEAD_PAD
        xp = _mm_nt(u, win_ref[:lo, :])
        xph_ref[...] = _mm_nt(u, win_ref[lo:, :])
        cq = xp[:, :Q_RANK]
        ckv = xp[:, Q_RANK:Q_RANK + KV_RANK]
        kr = xp[:, Q_RANK + KV_RANK:]
        cq_ref[...] = cq
        ckv_ref[...] = ckv
        ang = pos_ref[...].astype(F32) * invf_ref[...]
        c_tab = jnp.cos(ang)
        s_tab = jnp.sin(ang)
        rc_ref[...] = c_tab
        rs_ref[...] = s_tab
        c, sa, sb = _rope_tables(c_tab, s_tab)
        qn, _, _ = _rms_fwd(cq, qnw_ref[...])
        q = _mm(qn, wq_ref[...])
        kvn, _, _ = _rms_fwd(ckv, kvnw_ref[...])
        kn = _mm(kvn, wk_ref[...])
        v = _mm(kvn, wv_ref[...])
        v_ref[...] = v.astype(v_ref.dtype)
        vt_ref[...] = v.T.astype(vt_ref.dtype)
        krr = _rope(kr, c, sa, sb)
        for h in range(MLA_HEADS):
            sl = slice(HEAD_PAD * h, HEAD_PAD * (h + 1))
            q_ref[:, sl] = (_rope(q[:, sl], c, sa, sb) * ATTN_SCALE_LOG2).astype(q_ref.dtype)
            kh = kn[:, sl] + krr
            k_ref[:, sl] = kh.astype(k_ref.dtype)
            kt_ref[sl, :] = kh.T.astype(kt_ref.dtype)

    row = lambda w: pl.BlockSpec((tt, w), lambda i: (i, 0))
    full = lambda a: pl.BlockSpec(a.shape, lambda i: (0,) * a.ndim)
    qk_w = MLA_HEADS * HEAD_PAD
    return pl.pallas_call(
        body, name="in_fwd", grid=(T // tt,),
        in_specs=[row(D_MODEL), row(1), full(invf), full(w_pre), full(win), full(qnw), full(wq), full(kvnw),
                  full(wk), full(wv)],
        out_specs=[row(Q_RANK), row(KV_RANK), row(4 * HGRN_WIDTH), row(qk_w), row(qk_w), row(MLA_WIDTH),
                   pl.BlockSpec((qk_w, tt), lambda i: (0, i)), pl.BlockSpec((MLA_WIDTH, tt), lambda i: (0, i)),
                   row(HEAD_PAD), row(HEAD_PAD)],
        out_shape=[jax.ShapeDtypeStruct((T, Q_RANK), F32), jax.ShapeDtypeStruct((T, KV_RANK), F32),
                   jax.ShapeDtypeStruct((T, 4 * HGRN_WIDTH), F32), jax.ShapeDtypeStruct((T, qk_w), MXU_DTYPE),
                   jax.ShapeDtypeStruct((T, qk_w), MXU_DTYPE), jax.ShapeDtypeStruct((T, MLA_WIDTH), MXU_DTYPE),
                   jax.ShapeDtypeStruct((qk_w, T), MXU_DTYPE), jax.ShapeDtypeStruct((MLA_WIDTH, T), MXU_DTYPE),
                   jax.ShapeDtypeStruct((T, HEAD_PAD), F32), jax.ShapeDtypeStruct((T, HEAD_PAD), F32)],
        compiler_params=_params(("arbitrary",)),
    )(x, pos, invf, w_pre, win, qnw, wq, kvnw, wk, wv)


def _attn_fwd_t(qb, kb, vt, gather=(), tq=256, hps=8):
    T = qb.shape[0]
    nq = T // tq
    ng = len(gather)
    steps = (MLA_HEADS // hps) * nq
    pass_on = steps - 3

    def body(q_ref, k_ref, vt_ref, *rest):
        o_ref, lse_ref = rest[ng:ng + 2]
        acc_scr = rest[2 * ng + 2]
        qi = pl.program_id(1)
        step_no = pl.program_id(0) * nq + qi
        if ng:
            gat = _Gather(rest[:ng], rest[ng + 2:2 * ng + 2], *rest[2 * ng + 3:])

            @pl.when(step_no == 0)
            def _():
                for cp in gat.sends():
                    cp.start()

            @pl.when(step_no == pass_on)
            def _():
                for arrival in gat.arrivals():
                    arrival.wait_recv()
                for cp in gat.forwards():
                    cp.start()

        heads = [slice(HEAD_PAD * a, HEAD_PAD * (a + 1)) for a in range(hps)]
        acc_scr[...] = jnp.zeros_like(acc_scr)

        def step(j, carry, masked):
            start = pl.multiple_of(j * tq, tq)
            scores = [_mm_nt(k_ref[pl.ds(start, tq), heads[a]], q_ref[:, heads[a]]) for a in range(hps)]
            new, probs, alphas = [], [], []
            for a in range(hps):
                m, l = carry[a]
                s = scores[a]
                if masked:
                    kk = lax.broadcasted_iota(jnp.int32, (tq, tq), 0)
                    qq = lax.broadcasted_iota(jnp.int32, (tq, tq), 1)
                    s = jnp.where(kk <= qq, s, NEG_BIG)
                m_new = jnp.maximum(m, jnp.max(s, axis=0, keepdims=True))
                alpha = jnp.exp2(m - m_new)
                p = jnp.exp2(s - m_new)
                l = l * alpha + jnp.sum(p, axis=0, keepdims=True)
                new.append((m_new, l))
                probs.append(p.astype(MXU_DTYPE))
                alphas.append(alpha)
                if a % 2:
                    pr = a // 2
                    vtj = vt_ref[2 * MLA_V * pr:2 * MLA_V * (pr + 1), pl.ds(start, tq)]
                    none = jnp.zeros((MLA_V, tq), vtj.dtype)
                    pv = (_mm(jnp.concatenate([vtj[:MLA_V], none], axis=0), probs[a - 1])
                          + _mm(jnp.concatenate([none, vtj[MLA_V:]], axis=0), probs[a]))
                    acc_scr[pr] = acc_scr[pr] * jnp.where(row < MLA_V, alphas[a - 1], alphas[a]) + pv
            return tuple(new)

        row = lax.broadcasted_iota(jnp.int32, (2 * MLA_V, tq), 0)
        init = tuple((jnp.full((1, tq), NEG_BIG, F32), jnp.zeros((1, tq), F32)) for _ in range(hps))
        carry = lax.fori_loop(0, qi, lambda j, c: step(j, c, False), init)
        carry = step(qi, carry, True)
        for pr in range(hps // 2):
            (m0, l0), (m1, l1) = carry[2 * pr], carry[2 * pr + 1]
            ot = acc_scr[pr] / jnp.where(row < MLA_V, l0, l1)
            o_ref[:, 2 * MLA_V * pr:2 * MLA_V * (pr + 1)] = ot.T
            lse_ref[pr, 0:1, :] = m0 + jnp.log2(l0)
            lse_ref[pr, 1:2, :] = m1 + jnp.log2(l1)

        if ng:
            @pl.when(step_no == steps - 1)
            def _():
                for arrival in gat.forward_arrivals():
                    arrival.wait_recv()
                for cp in gat.sends() + gat.forwards():
                    cp.wait_send()

    return pl.pallas_call(
        body, name="attn_fwd", grid=(MLA_HEADS // hps, nq),
        in_specs=[pl.BlockSpec((tq, hps * HEAD_PAD), lambda g, i: (i, g)),
                  pl.BlockSpec((T, hps * HEAD_PAD), lambda g, i: (0, g)),
                  pl.BlockSpec((hps * MLA_V, T), lambda g, i: (g, 0))] + [ANY] * ng,
        out_specs=[pl.BlockSpec((tq, hps * MLA_V), lambda g, i: (i, g)),
                   pl.BlockSpec((hps // 2, 2, tq), lambda g, i: (g, 0, i))] + [ANY] * ng,
        out_shape=[jax.ShapeDtypeStruct((T, MLA_WIDTH), F32), jax.ShapeDtypeStruct((MLA_HEADS // 2, 2, T), F32)]
        + _Gather.out_shapes(gather),
        scratch_shapes=[pltpu.VMEM((hps // 2, 2 * MLA_V, tq), F32)] + (_Gather.semaphores(gather) if ng else []),
        compiler_params=_params(("arbitrary", "arbitrary")),
    )(qb, kb, vt, *gather)


def _attn_bwd_t(qb, kb, kt, vb, dob, lse, dvec, send=(), tq=512, hps=4):
    T = qb.shape[0]
    nq = T // tq
    ns = len(send)
    steps = (MLA_HEADS // hps) * nq

    def body(q_ref, k_ref, kt_ref, v_ref, do_ref, lse_ref, d_ref, *rest):
        dqt_ref, dk_ref, dv_ref = rest[ns:ns + 3]
        va_scr, dv_scr = rest[2 * ns + 3:2 * ns + 5]
        j = pl.program_id(1)
        step_no = pl.program_id(0) * nq + j
        if ns:
            @pl.when(step_no == 0)
            def _():
                for cp in _chip_swap_copies(rest[:ns], rest[ns + 3:2 * ns + 3], *rest[2 * ns + 5:]):
                    cp.start()

        @pl.when(j == 0)
        def _():
            dqt_ref[...] = jnp.zeros_like(dqt_ref)

        lane = lax.broadcasted_iota(jnp.int32, (tq, 2 * MLA_V), 1)
        heads = [slice(HEAD_PAD * a, HEAD_PAD * (a + 1)) for a in range(hps)]
        pairs = [slice(2 * MLA_V * p, 2 * MLA_V * (p + 1)) for p in range(hps // 2)]
        for pr in range(hps // 2):
            vpair = v_ref[:, pairs[pr]]
            va_scr[2 * pr] = jnp.where(lane < MLA_V, vpair, jnp.zeros_like(vpair))
            va_scr[2 * pr + 1] = jnp.where(lane >= MLA_V, vpair, jnp.zeros_like(vpair))
        dk_ref[...] = jnp.zeros_like(dk_ref)
        dv_scr[...] = jnp.zeros_like(dv_scr)

        def step(i, masked):
            start = pl.multiple_of(i * tq, tq)
            rows = pl.ds(start, tq)
            scores = [_mm_nt(k_ref[:, heads[a]], q_ref[rows, heads[a]]) for a in range(hps)]
            dps = [_mm_nt(va_scr[a], do_ref[rows, pairs[a // 2]]) for a in range(hps)]
            for a in range(hps):
                pr, r = a // 2, a % 2
                p = jnp.exp2(scores[a] - lse_ref[pr, r:r + 1, rows])
                if masked:
                    kk = lax.broadcasted_iota(jnp.int32, (tq, tq), 0)
                    qq = lax.broadcasted_iota(jnp.int32, (tq, tq), 1)
                    p = jnp.where(kk <= qq, p, 0.0)
                ds = p * (dps[a] - d_ref[pr, r:r + 1, rows])
                dv_scr[a] += _mm(p, do_ref[rows, pairs[pr]])
                dk_ref[:, heads[a]] += _mm(ds, q_ref[rows, heads[a]])
                dqt_ref[heads[a], rows] += _mm(kt_ref[heads[a], :], ds)

        def loop_body(i, _):
            step(i, False)
            return 0

        step(j, True)
        lax.fori_loop(j + 1, nq, loop_body, 0)
        for pr in range(hps // 2):
            dv_ref[:, pairs[pr]] = jnp.where(lane < MLA_V, dv_scr[2 * pr], dv_scr[2 * pr + 1])
        dk_ref[...] = dk_ref[...] * (ATTN_SCALE / ATTN_SCALE_LOG2)

        if ns:
            @pl.when(step_no == steps - 1)
            def _():
                for cp in _chip_swap_copies(rest[:ns], rest[ns + 3:2 * ns + 3], *rest[2 * ns + 5:]):
                    cp.wait()

    stat = pl.BlockSpec((hps // 2, 2, T), lambda g, j: (g, 0, 0))
    return pl.pallas_call(
        body, name="attn_bwd", grid=(MLA_HEADS // hps, nq),
        in_specs=[pl.BlockSpec((T, hps * HEAD_PAD), lambda g, j: (0, g)),
                  pl.BlockSpec((tq, hps * HEAD_PAD), lambda g, j: (j, g)),
                  pl.BlockSpec((hps * HEAD_PAD, tq), lambda g, j: (g, j)),
                  pl.BlockSpec((tq, hps * MLA_V), lambda g, j: (j, g)),
                  pl.BlockSpec((T, hps * MLA_V), lambda g, j: (0, g)), stat, stat] + [ANY] * ns,
        out_specs=[pl.BlockSpec((hps * HEAD_PAD, T), lambda g, j: (g, 0)),
                   pl.BlockSpec((tq, hps * HEAD_PAD), lambda g, j: (j, g)),
                   pl.BlockSpec((tq, hps * MLA_V), lambda g, j: (j, g))] + [ANY] * ns,
        out_shape=[jax.ShapeDtypeStruct((MLA_HEADS * HEAD_PAD, T), F32),
                   jax.ShapeDtypeStruct((T, MLA_HEADS * HEAD_PAD), F32),
                   jax.ShapeDtypeStruct((T, MLA_WIDTH), F32)] + _chip_swap_shapes(send),
        scratch_shapes=[pltpu.VMEM((hps, tq, 2 * MLA_V), vb.dtype), pltpu.VMEM((hps, tq, 2 * MLA_V), F32)]
        + ([pltpu.SemaphoreType.DMA((3 * ns,)), pltpu.SemaphoreType.DMA((3 * ns,))] if ns else []),
        compiler_params=_params(("arbitrary", "arbitrary")),
    )(qb, kb, kt, vb, dob, lse, dvec, *send)


def _cumsum_rows(x):
    n = x.shape[0]
    row = lax.broadcasted_iota(jnp.int32, x.shape, 0)
    s = 1
    while s < n:
        x = x + jnp.where(row >= s, pltpu.roll(x, s, 0), 0.0)
        s *= 2
    return x


def _rev_cumsum_rows(x):
    n = x.shape[0]
    row = lax.broadcasted_iota(jnp.int32, x.shape, 0)
    s = 1
    while s < n:
        x = x + jnp.where(row < n - s, pltpu.roll(x, n - s, 0), 0.0)
        s *= 2
    return x


def _lb_from_logits(l):
    l0, l1 = l[0:1, :], l[1:2, :]
    m = jnp.maximum(l0, l1)
    e0, e1 = jnp.exp(l0 - m), jnp.exp(l1 - m)
    return e0 / (e0 + e1)


def _hgrn_gates(hq, hf, lb):
    sig_f = jax.nn.sigmoid(hf)
    f = lb + (1.0 - lb) * sig_f
    sig_q = jax.nn.sigmoid(hq)
    return sig_f, f, jnp.log(f), 1.0 - f, sig_q, hq * sig_q


def _hgrn_intra(q, kk, b, exact=False):
    row = lax.broadcasted_iota(jnp.int32, b.shape, 0)
    qs, ks, eqs, eks, a_rows = [], [], [], [], []
    for i in range(CHUNK // SUB):
        ref = b[SUB * i + SUB // 2:SUB * i + SUB // 2 + 1, :]
        eq = jnp.exp(b[SUB * i:SUB * (i + 1), :] - ref)
        ek = jnp.exp(jnp.where(row < SUB * (i + 1), ref - b, NEG_BIG))
        qi = q[SUB * i:SUB * (i + 1), :] * eq
        ki = kk * ek
        a_rows.append(_mm_nt(qi, ki, exact))
        qs.append(qi), ks.append(ki), eqs.append(eq), eks.append(ek)
    tt = lax.broadcasted_iota(jnp.int32, (CHUNK, CHUNK), 0)
    ss = lax.broadcasted_iota(jnp.int32, (CHUNK, CHUNK), 1)
    causal = ss <= tt
    a = jnp.where(causal, jnp.concatenate(a_rows, axis=0), 0.0)
    return a, causal, qs, ks, eqs, eks


def _hgrn_fwd(xph, lbl, tg=512):
    T = xph.shape[0]
    ng, ncg = T // tg, tg // CHUNK
    cols = [slice(HGRN_DIM * h, HGRN_DIM * (h + 1)) for h in range(HGRN_HEADS)]

    def body(lbl_ref, hq_ref, hf_ref, hi_ref, o_ref, st_ref, s_scr):
        @pl.when(pl.program_id(0) == 0)
        def _():
            s_scr[...] = jnp.zeros_like(s_scr)

        lb = _lb_from_logits(lbl_ref[...])

        def chunks(it, _):
            pre = []
            for k in range(HGRN_CPI):
                c = it * HGRN_CPI + k
                rows = pl.ds(pl.multiple_of(c * CHUNK, CHUNK), CHUNK)
                for cs in cols:
                    _, _, lf, kk, _, q = _hgrn_gates(hq_ref[rows, cs], hf_ref[rows, cs], lb[:, cs])
                    v = hi_ref[rows, cs]
                    b = _cumsum_rows(lf)
                    a = _hgrn_intra(q, kk, b)[0]
                    b_last = b[CHUNK - 1:CHUNK, :]
                    pre.append((c, rows, q * jnp.exp(b), a, v, jnp.exp(b_last), _mm_tn(v, kk * jnp.exp(b_last - b))))
            for i, (c, rows, qe, a, v, ebl, upd) in enumerate(pre):
                h = i % HGRN_HEADS
                st = s_scr[h]
                st_ref[h, c] = st
                o_ref[rows, cols[h]] = _mm_nt(qe, st) + _mm(a, v)
                s_scr[h] = st * ebl + upd
            return 0

        lax.fori_loop(0, ncg // HGRN_CPI, chunks, 0)

    col = lambda k: pl.BlockSpec((tg, HGRN_WIDTH), lambda g: (g, k))
    return pl.pallas_call(
        body, name="hgrn_fwd", grid=(ng,),
        in_specs=[pl.BlockSpec((2, HGRN_WIDTH), lambda g: (0, 0)), col(0), col(1), col(2)],
        out_specs=[col(0), pl.BlockSpec((HGRN_HEADS, ncg, HGRN_DIM, HGRN_DIM), lambda g: (0, g, 0, 0))],
        out_shape=[jax.ShapeDtypeStruct((T, HGRN_WIDTH), F32),
                   jax.ShapeDtypeStruct((HGRN_HEADS, T // CHUNK, HGRN_DIM, HGRN_DIM), F32)],
        scratch_shapes=[pltpu.VMEM((HGRN_HEADS, HGRN_DIM, HGRN_DIM), F32)],
        compiler_params=_params(("arbitrary",)),
    )(lbl, xph, xph, xph)


def _hgrn_bwd(xph, lbl, states, d_o, fill=(), tg=512):
    T = xph.shape[0]
    ng, ncg = T // tg, tg // CHUNK
    cols = [slice(HGRN_DIM * h, HGRN_DIM * (h + 1)) for h in range(HGRN_HEADS)]
    nsub = CHUNK // SUB
    nf = len(fill)

    def body(lbl_ref, hq_ref, hf_ref, hi_ref, st_ref, do_ref, *rest):
        dhq_ref, dhf_ref, dhi_ref, dlg_ref = rest[nf:nf + 4]
        ds_scr, dlb_scr = rest[2 * nf + 4:2 * nf + 6]
        fill_copies = lambda: _pair_fill_copies(rest[nf + 4:2 * nf + 4], *rest[2 * nf + 6:])
        g = pl.program_id(0)

        @pl.when(g == 0)
        def _():
            ds_scr[...] = jnp.zeros_like(ds_scr)
            dlb_scr[...] = jnp.zeros_like(dlb_scr)
            for cp in (fill_copies()[0] if nf else ()):
                cp.start()

        lb = _lb_from_logits(lbl_ref[...])

        def chunks(it, _):
            pre = []
            for k, h in ((k, h) for k in range(HGRN_CPI) for h in range(HGRN_HEADS)):
                cs = cols[h]
                c = ncg - 1 - (it * HGRN_CPI + k)
                rows = pl.ds(pl.multiple_of(c * CHUNK, CHUNK), CHUNK)
                hq = hq_ref[rows, cs]
                sig_f, f, lf, kk, sig_q, q = _hgrn_gates(hq, hf_ref[rows, cs], lb[:, cs])
                v = hi_ref[rows, cs]
                do = do_ref[rows, cs]
                b = _cumsum_rows(lf)
                eb = jnp.exp(b)
                a, causal, qs, ks, eqs, eks = _hgrn_intra(q, kk, b)
                b_last = b[CHUNK - 1:CHUNK, :]
                st = st_ref[h, c]
                pre.append(dict(h=h, cs=cs, rows=rows, hq=hq, sig_f=sig_f, f=f, kk=kk, sig_q=sig_q, q=q, v=v, eb=eb, qs=qs,
                                ks=ks, eqs=eqs,
                                eks=eks, ebl=jnp.exp(b_last), el=jnp.exp(b_last - b), st=st,
                                da=jnp.where(causal, _mm_nt(do, v, True), 0.0), dq=_mm(do, st, True) * eb,
                                dv=_mm_tn(a, do), dsu=_mm_tn(do, q * eb, True)))
            for w in pre:
                dq_rows = []
                dk = jnp.zeros_like(w["q"])
                for i in range(nsub):
                    dai = w["da"][SUB * i:SUB * (i + 1), :]
                    dq_rows.append(_mm(dai, w["ks"][i], True) * w["eqs"][i])
                    dk = dk + _mm_tn(dai, w["qs"][i], True) * w["eks"][i]
                w["dq"] = w["dq"] + jnp.concatenate(dq_rows, axis=0)
                w["dk"] = dk
            for w in pre:
                h, cs, rows = w["h"], w["cs"], w["rows"]
                kk, el, ebl, dst = w["kk"], w["el"], w["ebl"], ds_scr[h]
                dk_state = _mm(w["v"], dst, True) * el
                dk = w["dk"] + dk_state
                e_last = (ebl * jnp.sum(w["st"] * dst, axis=0, keepdims=True)
                          + jnp.sum(kk * dk_state, axis=0, keepdims=True))
                dlf = _rev_cumsum_rows(w["q"] * w["dq"] - kk * dk) + e_last
                ds_scr[h] = dst * ebl + w["dsu"]
                df = dlf / w["f"] - dk
                sig_f, sig_q = w["sig_f"], w["sig_q"]
                dhf_ref[rows, cs] = df * (1.0 - lb[:, cs]) * sig_f * (1.0 - sig_f)
                dlb_scr[:, cs] += jnp.sum(df * (1.0 - sig_f), axis=0, keepdims=True)
                dhq_ref[rows, cs] = w["dq"] * sig_q * (1.0 + w["hq"] * (1.0 - sig_q))
                dhi_ref[rows, cs] = w["dv"] + _mm_nt(kk * el, dst)
            return 0

        lax.fori_loop(0, ncg // HGRN_CPI, chunks, 0)

        @pl.when(g == ng - 1)
        def _():
            dl0 = dlb_scr[...] * lb * (1.0 - lb)
            dlg_ref[...] = jnp.concatenate([dl0, -dl0], axis=0)
            if nf:
                copies, waits = fill_copies()
                for w in waits:
                    w.wait_recv()
                for cp in copies:
                    cp.wait_send()

    col = lambda k: pl.BlockSpec((tg, HGRN_WIDTH), lambda g: (ng - 1 - g, k))
    logits = pl.BlockSpec((2, HGRN_WIDTH), lambda g: (0, 0))
    big = jax.ShapeDtypeStruct((T, HGRN_WIDTH), F32)
    n_in, n_out = 6, 4
    return pl.pallas_call(
        body, name="hgrn_bwd", grid=(ng,),
        in_specs=[logits, col(0), col(1), col(2),
                  pl.BlockSpec((HGRN_HEADS, ncg, HGRN_DIM, HGRN_DIM), lambda g: (0, ng - 1 - g, 0, 0)), col(0)] + [ANY] * nf,
        out_specs=[col(0), col(0), col(0), logits] + [ANY] * nf,
        out_shape=[big, big, big, jax.ShapeDtypeStruct((2, HGRN_WIDTH), F32)]
        + [jax.ShapeDtypeStruct(f.shape, f.dtype) for f in fill],
        input_output_aliases={n_in + k: n_out + k for k in range(nf)},
        scratch_shapes=[pltpu.VMEM((HGRN_HEADS, HGRN_DIM, HGRN_DIM), F32), pltpu.VMEM((1, HGRN_WIDTH), F32)]
        + ([pltpu.SemaphoreType.DMA((nf,)), pltpu.SemaphoreType.DMA((nf,))] if nf else []),
        compiler_params=_params(("arbitrary",)),
    )(lbl, xph, xph, xph, states, d_o, *fill)


def _proj_fwd(x, o_raw, oh_raw, xph, wout, w_mla, w_hg, w_post, w_fpre, tt=512):
    T = x.shape[0]

    def body(x_ref, o_ref, oh_ref, hg_ref, wout_ref, wmla_ref, whg_ref, wpost_ref, wfpre_ref,
             h1_ref, y1_ref, z_ref, mix_ref):
        om, _, _ = _grms_fwd(o_ref[...], wmla_ref[...], MLA_V)
        hg = hg_ref[...]
        ohn, _, _ = _grms_fwd(oh_ref[...], whg_ref[...], HGRN_DIM)
        mix = jnp.concatenate([om, ohn * (hg * jax.nn.sigmoid(hg))], axis=-1)
        mix_ref[...] = mix.astype(mix_ref.dtype)
        y1 = _mm(mix, wout_ref[...])
        y1_ref[...] = y1
        h1 = x_ref[...] + _rms_fwd(y1, wpost_ref[...])[0]
        h1_ref[...] = h1
        z_ref[...] = _rms_fwd(h1, wfpre_ref[...])[0].astype(z_ref.dtype)

    row = lambda w: pl.BlockSpec((tt, w), lambda i: (i, 0))
    full = lambda a: pl.BlockSpec(a.shape, lambda i: (0,) * a.ndim)
    sds = jax.ShapeDtypeStruct
    return pl.pallas_call(
        body, name="proj_fwd", grid=(T // tt,),
        in_specs=[row(D_MODEL), row(MLA_WIDTH), row(HGRN_WIDTH), pl.BlockSpec((tt, HGRN_WIDTH), lambda i: (i, 3)),
                  full(wout), full(w_mla), full(w_hg), full(w_post), full(w_fpre)],
        out_specs=[row(D_MODEL)] * 4,
        out_shape=[sds((T, D_MODEL), F32), sds((T, D_MODEL), F32), sds((T, D_MODEL), MXU_DTYPE),
                   sds((T, D_MODEL), MXU_DTYPE)],
        compiler_params=_params(("arbitrary",)),
    )(x, o_raw, oh_raw, xph, wout, w_mla, w_hg, w_post, w_fpre)


def _ffn_fwd(zb, h1, tgt, w_fpost, wg, wu, wd, tt=256):
    T = zb.shape[0]
    nj = N_CHIPS

    def body(z_ref, h1_ref, tgt_ref, wfpost_ref, wg_ref, wu_ref, wd_ref, g_ref, up_ref, dy2_ref, dh2_ref, loss_ref, dwf_ref):
        @pl.when(pl.program_id(0) == 0)
        def _():
            loss_ref[...] = jnp.zeros_like(loss_ref)
            dwf_ref[...] = jnp.zeros_like(dwf_ref)

        z = z_ref[...]
        gs = [_mm_nt(z, wg_ref[j]) for j in range(nj)]
        ups = [_mm_nt(z, wu_ref[j]) for j in range(nj)]
        y2 = jnp.zeros((tt, D_MODEL), F32)
        for j in range(nj):
            g_ref[j] = gs[j]
            up_ref[j] = ups[j]
            y2 = y2 + _mm(gs[j] * jax.nn.sigmoid(gs[j]) * ups[j], wd_ref[j])
        w = wfpost_ref[...]
        y2s, y2n, r2 = _rms_fwd(y2, w)
        e = h1_ref[...] + y2s - tgt_ref[...]
        loss_ref[...] += jnp.sum(e * e, axis=0, keepdims=True)
        dh2 = e * (1.0 / D_MODEL)
        dh2_ref[...] = dh2
        dy2, dwf = _rms_bwd(dh2, y2n, r2, w)
        dy2_ref[...] = dy2.astype(dy2_ref.dtype)
        dwf_ref[...] += dwf

    row = pl.BlockSpec((tt, D_MODEL), lambda i: (i, 0))
    vec = pl.BlockSpec((1, D_MODEL), lambda i: (0, 0))
    resident = pl.BlockSpec((nj, FF_SHARD, D_MODEL), lambda i: (0, 0, 0), pipeline_mode=pl.Buffered(1))
    act = pl.BlockSpec((nj, tt, FF_SHARD), lambda i: (0, i, 0))
    sds = jax.ShapeDtypeStruct
    return pl.pallas_call(
        body, name="ffn_fwd", grid=(T // tt,),
        in_specs=[row, row, row, vec, resident, resident, resident],
        out_specs=[act, act, row, row, vec, vec],
        out_shape=[sds((nj, T, FF_SHARD), F32), sds((nj, T, FF_SHARD), F32), sds((T, D_MODEL), MXU_DTYPE),
                   sds((T, D_MODEL), F32), sds((1, D_MODEL), F32), sds((1, D_MODEL), F32)],
        compiler_params=_params(("arbitrary",)),
    )(zb, h1, tgt, w_fpost, wg, wu, wd)


def _ffn_bwd(zb, g, up, dy2b, wg, wu, wd, tt=512):
    T = zb.shape[0]
    nj = N_CHIPS

    def body(z_ref, g_ref, up_ref, dy2_ref, wg_ref, wu_ref, wd_ref, dwg_ref, dwu_ref, dwd_ref, dz_ref, acc_ref):
        j, i = pl.program_id(0), pl.program_id(1)
        rows = pl.ds(pl.multiple_of(i * tt, tt), tt)

        @pl.when(i == 0)
        def _():
            dwg_ref[...] = jnp.zeros_like(dwg_ref)
            dwu_ref[...] = jnp.zeros_like(dwu_ref)
            dwd_ref[...] = jnp.zeros_like(dwd_ref)

        z, g_, up_, dy2 = z_ref[...], g_ref[0], up_ref[0], dy2_ref[...]
        sg = jax.nn.sigmoid(g_)
        act = g_ * sg
        dff = _mm_nt(dy2, wd_ref[0])
        dwd_ref[0] += _mm_tn(act * up_, dy2)
        dg = dff * up_ * sg * (1.0 + g_ * (1.0 - sg))
        dup = dff * act
        dwg_ref[0] += _mm_tn(dg, z)
        dwu_ref[0] += _mm_tn(dup, z)
        dz = _mm(dg, wg_ref[0]) + _mm(dup, wu_ref[0])

        @pl.when(j == 0)
        def _():
            acc_ref[rows, :] = dz

        @pl.when((j > 0) & (j < nj - 1))
        def _():
            acc_ref[rows, :] += dz

        @pl.when(j == nj - 1)
        def _():
            dz_ref[...] = acc_ref[rows, :] + dz

    row = pl.BlockSpec((tt, D_MODEL), lambda j, i: (i, 0))
    act = pl.BlockSpec((1, tt, FF_SHARD), lambda j, i: (j, i, 0))
    w_sh = pl.BlockSpec((1, FF_SHARD, D_MODEL), lambda j, i: (j, 0, 0))
    w_grad = jax.ShapeDtypeStruct((nj, FF_SHARD, D_MODEL), F32)
    return pl.pallas_call(
        body, name="ffn_bwd", grid=(nj, T // tt),
        in_specs=[row, act, act, row, w_sh, w_sh, w_sh],
        out_specs=[w_sh, w_sh, w_sh, pl.BlockSpec((tt, D_MODEL), lambda j, i: (jnp.where(j == nj - 1, i, 0), 0))],
        out_shape=[w_grad, w_grad, w_grad, jax.ShapeDtypeStruct((T, D_MODEL), F32)],
        scratch_shapes=[pltpu.VMEM((T, D_MODEL), F32)],
        compiler_params=_params(("arbitrary", "arbitrary"), vmem=FFN_BWD_VMEM),
    )(zb, g, up, dy2b, wg, wu, wd)


def _mid_bwd(dz, dh2, h1, y1, mixb, o_raw, oh_raw, xph, wout, w_fpre, w_post, w_mla, w_hg, swap=(), tt=512):
    T = dh2.shape[0]
    nsw = len(swap)
    n_in, n_out = 13, 10

    def body(*refs):
        (dz_ref, dh2_ref, h1_ref, y1_ref, mix_ref, o_ref, oh_ref, hg_ref, wout_ref, wfpre_ref, wpost_ref,
         wmla_ref, whg_ref) = refs[:n_in]
        (dh1_ref, dwout_ref, do_ref, doh_ref, dhg_ref, dvec_ref, dwfpre_ref, dwpost_ref, dwmla_ref,
         dwhg_ref) = refs[n_in + nsw:n_in + nsw + n_out]
        sems = refs[n_in + 2 * nsw + n_out + 1:]
        swap_copies = lambda: _pair_swap_copies(refs[n_in:n_in + nsw], refs[n_in + nsw + n_out:n_in + 2 * nsw + n_out],
                                                *sems)

        def wout_copies():
            _, _, c, _, sib, _ = _place()
            rw_ref, h = refs[n_in + 2 * nsw + n_out], D_MODEL // N_CHIPS // 2
            return [_rcopy(dwout_ref.at[pl.ds(pl.multiple_of((2 * k + 1 - c) * h, 8), h)], rw_ref.at[k], *sems, nsw + k, sib)
                    for k in range(N_CHIPS)]

        @pl.when(pl.program_id(0) == 0)
        def _():
            for r in (dwout_ref, dwfpre_ref, dwpost_ref, dwmla_ref, dwhg_ref):
                r[...] = jnp.zeros_like(r)
            for cp in (swap_copies() if nsw else ()):
                cp.start()

        dz = dz_ref[...]
        wfpre = wfpre_ref[...]
        _, h1n, r = _rms_fwd(h1_ref[...], wfpre)
        dh1_z, dwfpre = _rms_bwd(dz, h1n, r, wfpre)
        dwfpre_ref[...] += dwfpre
        dh1 = dh2_ref[...] + dh1_z
        dh1_ref[...] = dh1
        wpost = wpost_ref[...]
        _, y1n, r1 = _rms_fwd(y1_ref[...], wpost)
        dy1, dwpost = _rms_bwd(dh1, y1n, r1, wpost)
        dwpost_ref[...] += dwpost
        dmix = _mm_nt(dy1, wout_ref[...])
        dwout_ref[...] += _mm_tn(mix_ref[...], dy1)
        wmla = wmla_ref[...]
        o = o_ref[...]
        _, on, ro = _grms_fwd(o, wmla, MLA_V)
        d_o, dwmla = _grms_bwd(dmix[:, :MLA_WIDTH], on, ro, wmla, MLA_V)
        dwmla_ref[...] += dwmla
        do_ref[...] = d_o.astype(do_ref.dtype)
        hh = lax.broadcasted_iota(jnp.int32, (MLA_HEADS, MLA_WIDTH), 0)
        ll = lax.broadcasted_iota(jnp.int32, (MLA_HEADS, MLA_WIDTH), 1)
        sel = jnp.where((ll >= hh * MLA_V) & (ll < (hh + 1) * MLA_V), 1.0, 0.0)
        dvec_ref[...] = _mm_nt(sel, d_o * o, True)
        whg = whg_ref[...]
        hg = hg_ref[...]
        sg = jax.nn.sigmoid(hg)
        _, ohn, rh = _grms_fwd(oh_ref[...], whg, HGRN_DIM)
        dmh = dmix[:, MLA_WIDTH:]
        dhg_ref[...] = dmh * ohn * whg * sg * (1.0 + hg * (1.0 - sg))
        d_oh, dwhg = _grms_bwd(dmh * (hg * sg), ohn, rh, whg, HGRN_DIM)
        dwhg_ref[...] += dwhg
        doh_ref[...] = d_oh

        if nsw:
            @pl.when(pl.program_id(0) == T // tt - 1)
            def _():
                for cp in wout_copies():
                    cp.start()
                for cp in swap_copies() + wout_copies():
                    cp.wait()

    row = lambda w: pl.BlockSpec((tt, w), lambda i: (i, 0))
    full = lambda a: pl.BlockSpec(a.shape, lambda i: (0,) * a.ndim)
    vec = lambda w: pl.BlockSpec((1, w), lambda i: (0, 0))
    sds = jax.ShapeDtypeStruct
    return pl.pallas_call(
        body, name="mid_bwd", grid=(T // tt,),
        in_specs=[row(D_MODEL), row(D_MODEL), row(D_MODEL), row(D_MODEL),
                  row(D_MODEL), row(MLA_WIDTH), row(HGRN_WIDTH), pl.BlockSpec((tt, HGRN_WIDTH), lambda i: (i, 3)),
                  full(wout), vec(D_MODEL), vec(D_MODEL), vec(MLA_WIDTH), vec(HGRN_WIDTH)] + [ANY] * nsw,
        out_specs=[row(D_MODEL), full(wout), row(MLA_WIDTH), row(HGRN_WIDTH), row(HGRN_WIDTH),
                   pl.BlockSpec((MLA_HEADS, tt), lambda i: (0, i)),
                   vec(D_MODEL), vec(D_MODEL), vec(MLA_WIDTH), vec(HGRN_WIDTH)] + [ANY] * (nsw + 1 if nsw else 0),
        out_shape=[sds((T, D_MODEL), F32), sds(wout.shape, F32), sds((T, MLA_WIDTH), MXU_DTYPE), sds((T, HGRN_WIDTH), F32),
                   sds((T, HGRN_WIDTH), F32), sds((MLA_HEADS, T), F32),
                   sds((1, D_MODEL), F32), sds((1, D_MODEL), F32), sds((1, MLA_WIDTH), F32), sds((1, HGRN_WIDTH), F32)]
        + (_half_stack_shapes(list(swap) + [sds((N_CHIPS, D_MODEL // N_CHIPS, D_MODEL), F32)]) if nsw else []),
        scratch_shapes=[pltpu.SemaphoreType.DMA((nsw + N_CHIPS,)), pltpu.SemaphoreType.DMA((nsw + N_CHIPS,))] if nsw else [],
        compiler_params=_params(("arbitrary",)),
    )(dz, dh2, h1, y1, mixb, o_raw, oh_raw, xph, wout, w_fpre, w_post, w_mla, w_hg, *swap)


def _in_bwd(x, dh1, cq, ckv, dq, dk, dv, dhq, dhf, dhi, dhg, rc, rs, w_pre, win, qnw, wq, kvnw, wk, wv, tt=256):
    T = x.shape[0]

    def body(x_ref, dh1_ref, cq_ref, ckv_ref, dq_ref, dk_ref, dv_ref, dhq_ref, dhf_ref, dhi_ref, dhg_ref, rc_ref, rs_ref,
             wpre_ref, win_ref, qnw_ref, wq_ref, kvnw_ref, wk_ref, wv_ref,
             dx_ref, dwin_ref, dwq_ref, dwk_ref, dwv_ref, dwpre_ref, dqnw_ref, dkvnw_ref):
        @pl.when(pl.program_id(0) == 0)
        def _():
            for r in (dwin_ref, dwq_ref, dwk_ref, dwv_ref, dwpre_ref, dqnw_ref, dkvnw_ref):
                r[...] = jnp.zeros_like(r)

        def add_win_grad(r, first):
            for arr0, n, chip, row0 in _win_grad_segments():
                if first <= arr0 and arr0 + n <= first + r.shape[0]:
                    dwin_ref[chip, row0:row0 + n, :] += r[arr0 - first:arr0 - first + n]

        lo = Q_RANK + KV_RANK + HEAD_PAD
        dxp_h = jnp.concatenate([dhq_ref[...], dhf_ref[...], dhi_ref[...], dhg_ref[...]], axis=-1)
        du = _mm(dxp_h, win_ref[lo:, :])
        wpre = wpre_ref[...]
        u, xn, rx = _rms_fwd(x_ref[...], wpre)
        add_win_grad(_mm_tn(dxp_h, u), lo)
        c, sa, sb = _rope_tables(rc_ref[...], rs_ref[...])
        lane = lax.broadcasted_iota(jnp.int32, (tt, HEAD_PAD), 1)
        dk_all = dk_ref[...]
        dq_lin = []
        dkr = jnp.zeros((tt, HEAD_PAD), F32)
        for h in range(MLA_HEADS):
            sl = slice(HEAD_PAD * h, HEAD_PAD * (h + 1))
            dq_lin.append(_rope_bwd(dq_ref[sl, :].T * ATTN_SCALE, c, sa, sb))
            dkr = dkr + dk_all[:, sl]
        dq_lin = jnp.concatenate(dq_lin, axis=-1)
        dkr = jnp.where((lane >= MLA_NOPE) & (lane < MLA_QK), _rope_bwd(dkr, c, sa, sb), 0.0)
        qnw = qnw_ref[...]
        qn, cqn, rq = _rms_fwd(cq_ref[...], qnw)
        dwq_ref[...] += _mm_tn(qn, dq_lin)
        dcq, dqnw = _rms_bwd(_mm_nt(dq_lin, wq_ref[...]), cqn, rq, qnw)
        dqnw_ref[...] += dqnw
        kvnw = kvnw_ref[...]
        kvn, ckvn, rkv = _rms_fwd(ckv_ref[...], kvnw)
        dv_ = dv_ref[...]
        dwk_ref[...] += _mm_tn(kvn, dk_all)
        dwv_ref[...] += _mm_tn(kvn, dv_)
        dckv, dkvnw = _rms_bwd(_mm_nt(dk_all, wk_ref[...]) + _mm_nt(dv_, wv_ref[...]), ckvn, rkv, kvnw)
        dkvnw_ref[...] += dkvnw
        dxp_a = jnp.concatenate([dcq, dckv, dkr], axis=-1)
        add_win_grad(_mm_tn(dxp_a, u), 0)
        dx_u, dwpre = _rms_bwd(du + _mm(dxp_a, win_ref[:lo, :]), xn, rx, wpre)
        dwpre_ref[...] += dwpre
        dx_ref[...] = dh1_ref[...] + dx_u

    row = lambda w: pl.BlockSpec((tt, w), lambda i: (i, 0))
    full = lambda a: pl.BlockSpec(a.shape, lambda i: (0,) * a.ndim)
    sds = jax.ShapeDtypeStruct
    qk_w = MLA_HEADS * HEAD_PAD
    return pl.pallas_call(
        body, name="in_bwd", grid=(T // tt,),
        in_specs=[row(D_MODEL), row(D_MODEL), row(Q_RANK), row(KV_RANK), pl.BlockSpec((qk_w, tt), lambda i: (0, i)),
                  row(qk_w), row(MLA_WIDTH),
                  row(HGRN_WIDTH), row(HGRN_WIDTH), row(HGRN_WIDTH), row(HGRN_WIDTH), row(HEAD_PAD), row(HEAD_PAD),
                  full(w_pre), full(win), full(qnw), full(wq), full(kvnw), full(wk), full(wv)],
        out_specs=[row(D_MODEL), pl.BlockSpec(WIN_COMM_SHAPE, lambda i: (0, 0, 0)), full(wq), full(wk), full(wv),
                   full(w_pre), full(qnw), full(kvnw)],
        out_shape=[sds((T, D_MODEL), F32), sds(WIN_COMM_SHAPE, F32), sds(wq.shape, F32), sds(wk.shape, F32),
                   sds(wv.shape, F32), sds(w_pre.shape, F32), sds(qnw.shape, F32), sds(kvnw.shape, F32)],
        compiler_params=_params(("arbitrary",)),
    )(x, dh1, cq, ckv, dq, dk, dv, dhq, dhf, dhi, dhg, rc, rs, w_pre, win, qnw, wq, kvnw, wk, wv)


def _arrange_weights(win_t, wuq_full, wukv):
    dt = win_t.dtype
    z = lambda n: jnp.zeros((n, D_MODEL), dt)
    s2 = Q_RANK + KV_RANK
    win_arr = jnp.concatenate([win_t[:s2], z(MLA_NOPE), win_t[s2:s2 + MLA_ROPE], z(HEAD_PAD - MLA_QK),
                               win_t[s2 + MLA_ROPE:]], axis=0)
    wq_arr = jnp.pad(wuq_full, ((0, 0), (0, 0), (0, HEAD_PAD - MLA_QK))).reshape(Q_RANK, MLA_HEADS * HEAD_PAD)
    wk_arr = jnp.pad(wukv[:, :, :MLA_NOPE], ((0, 0), (0, 0), (0, HEAD_PAD - MLA_NOPE))).reshape(
        KV_RANK, MLA_HEADS * HEAD_PAD)
    wv_arr = wukv[:, :, MLA_NOPE:].reshape(KV_RANK, MLA_WIDTH)
    return win_arr, wq_arr, wk_arr, wv_arr


WIN_COMM_SHAPE = (N_CHIPS, -(-D_IN // N_CHIPS // 32) * 32, D_MODEL)


def _win_grad_segments():
    s2 = Q_RANK + KV_RANK
    runs = [(0, s2, 0), (s2, s2 + MLA_ROPE, MLA_NOPE), (s2 + MLA_ROPE, D_IN, HEAD_PAD - MLA_ROPE)]
    per = D_IN // N_CHIPS
    segs = []
    for lo, hi, shift in runs:
        for k in range(N_CHIPS):
            a, b = max(lo, per * k), min(hi, per * (k + 1))
            if a < b:
                segs.append((a + shift, b - a, k, a - per * k))
    return segs


def _unarrange_grads(dwq_arr, dwk_arr, dwv_arr):
    dwuq = dwq_arr.reshape(Q_RANK, MLA_HEADS, HEAD_PAD)[:, :, :MLA_QK]
    dwukv = jnp.concatenate([dwk_arr.reshape(KV_RANK, MLA_HEADS, HEAD_PAD)[:, :, :MLA_NOPE],
                             dwv_arr.reshape(KV_RANK, MLA_HEADS, MLA_V)], axis=-1)
    return dwuq, dwukv


def _rope_inv_freq():
    inv = 1.0 / (ROPE_THETA ** (jnp.arange(0, MLA_ROPE, 2, dtype=F32) / MLA_ROPE))
    z = lambda n: jnp.zeros((n,), F32)
    return jnp.concatenate([z(MLA_NOPE), inv, inv, z(HEAD_PAD - MLA_QK)]).reshape(1, HEAD_PAD)


def _local_step(x, pos, tgt, small, win_arr, wq_arr, wk_arr, wv_arr, late, place=None):
    invf = _rope_inv_freq()
    cq, ckv, xph, qb, kb, vb, kt, vt, rc, rs = _in_fwd(x, pos, invf, small["attn_pre_norm"], win_arr, small["mla_q_norm"],
                                               wq_arr, small["mla_kv_norm"], wk_arr, wv_arr)
    if place is None:
        o_raw, lse = _attn_fwd_t(qb, kb, vt)
        wout, wg, wu, wd = late
    else:
        o_raw, lse, *stacks = _attn_fwd_t(qb, kb, vt, gather=late)
        wout, wg, wu, wd = [lax.dynamic_update_slice(s, l[None], (place[1], 0, 0)) for s, l in zip(stacks, late)]
        wout = wout.reshape(D_MODEL, D_MODEL)
    oh_raw, states = _hgrn_fwd(xph, small["hgrn_lb_logits"])
    h1, y1, zb, mixb = _proj_fwd(x, o_raw, oh_raw, xph, wout, small["mla_out_norm"], small["hgrn_out_norm"],
                                 small["attn_post_norm"], small["ffn_pre_norm"])
    g, up, dy2b, dh2, loss_acc, d_fpost = _ffn_fwd(zb, h1, tgt, small["ffn_post_norm"], wg, wu, wd)
    dwg, dwu, dwd, dz = _ffn_bwd(zb, g, up, dy2b, wg, wu, wd)
    ffn_grads = [] if place is None else [dwg, dwu, dwd]
    dh1, dwout, d_o, d_oh, dhg, dvec, d_fpre, d_post, d_mla, d_hg, *ffn_rs = _mid_bwd(
        dz, dh2, h1, y1, mixb, o_raw, oh_raw, xph, wout, small["ffn_pre_norm"], small["attn_post_norm"],
        small["mla_out_norm"], small["hgrn_out_norm"], swap=ffn_grads)
    if ffn_grads:
        ffn_grads = ffn_grads + [dwout.reshape(N_CHIPS, D_MODEL // N_CHIPS, D_MODEL)]
    ffn_ps = _pair_sum(place, ffn_grads, ffn_rs, name="pair_sum_ffn") if ffn_grads else []
    dq, dk, dv, *ffn_ris = _attn_bwd_t(qb, kb, kt, vb, d_o, lse, dvec.reshape(lse.shape), send=ffn_ps)
    ffn_sums = _chip_sum(place, ffn_grads, ffn_rs, ffn_ris, name="chip_sum_ffn") if ffn_grads else []
    dhq, dhf, dhi, d_lbl, *ffn_final = _hgrn_bwd(xph, small["hgrn_lb_logits"], states, d_oh, fill=ffn_sums)
    dx, dwin4, dwq_arr, dwk_arr, dwv_arr, d_pre, d_qn, d_kvn = _in_bwd(
        x, dh1, cq, ckv, dq, dk, dv, dhq, dhf, dhi, dhg, rc, rs, small["attn_pre_norm"], win_arr,
        small["mla_q_norm"], wq_arr, small["mla_kv_norm"], wk_arr, wv_arr)
    dwuq, dwukv = _unarrange_grads(dwq_arr, dwk_arr, dwv_arr)
    loss = 0.5 * jnp.sum(loss_acc) * (1.0 / D_MODEL)
    grads = dict(attn_pre_norm=d_pre, w_in=dwin4, mla_q_norm=d_qn, mla_w_uq=dwuq, mla_kv_norm=d_kvn, mla_w_ukv=dwukv,
                 mla_out_norm=d_mla, hgrn_lb_logits=d_lbl, hgrn_out_norm=d_hg, w_out=dwout, attn_post_norm=d_post,
                 ffn_pre_norm=d_fpre, w_gate=dwg, w_up=dwu, w_down=dwd, ffn_post_norm=d_fpost)
    if place is None:
        return loss, dx, grads
    return loss, dx, grads, ffn_final


def _place():
    x, y, c = lax.axis_index("x"), lax.axis_index("y"), lax.axis_index("c")
    others = [(1 - x, y), (x, 1 - y), (1 - x, 1 - y)]
    return x, y, c, 2 * x + y, (x, y, 1 - c), others


def _half(ref, c, rows):
    return ref.at[pl.ds(pl.multiple_of(c * rows, 8), rows)]


def _rcopy(src, dst, send, recv, k, to):
    return pltpu.make_async_remote_copy(src_ref=src, dst_ref=dst, send_sem=send.at[k], recv_sem=recv.at[k],
                                        device_id=to, device_id_type=MESH)


class _Gather:
    def __init__(self, ins, outs, send, recv):
        self.ins, self.outs, self.send, self.recv = ins, outs, send, recv
        self.n = len(ins)
        self.halves = [r.shape[0] // 2 for r in ins]
        _, _, self.c, self.me, self.sib, self.others = _place()

    def _each(self):
        for j, (px, py) in enumerate(self.others):
            for a in range(self.n):
                yield j * self.n + a, a, 2 * px + py, (px, py, self.c)

    def sends(self):
        return [_rcopy(_half(self.ins[a], self.c, self.halves[a]), _half(self.outs[a].at[self.me], self.c, self.halves[a]),
                       self.send, self.recv, k, to) for k, a, _, to in self._each()]

    def arrivals(self):
        parts = [(k, _half(self.outs[a].at[chip], self.c, self.halves[a]), to) for k, a, chip, to in self._each()]
        return [_rcopy(p, p, self.send, self.recv, k, to) for k, p, to in parts]

    def forwards(self):
        parts = [(k, _half(self.outs[a].at[chip], self.c, self.halves[a])) for k, a, chip, _ in self._each()]
        return [_rcopy(p, p, self.send, self.recv, 3 * self.n + k, self.sib) for k, p in parts]

    def forward_arrivals(self):
        parts = [(k, _half(self.outs[a].at[chip], 1 - self.c, self.halves[a])) for k, a, chip, _ in self._each()]
        return [_rcopy(p, p, self.send, self.recv, 3 * self.n + k, self.sib) for k, p in parts]

    @staticmethod
    def out_shapes(arrs):
        return [jax.ShapeDtypeStruct((N_CHIPS,) + a.shape, a.dtype) for a in arrs]

    @staticmethod
    def semaphores(arrs):
        return [pltpu.SemaphoreType.DMA((6 * len(arrs),)), pltpu.SemaphoreType.DMA((6 * len(arrs),))]


def _gather_chips(arrs, name):
    n = len(arrs)

    def body(*refs):
        gat = _Gather(refs[:n], refs[n:2 * n], *refs[2 * n:])
        sends, forwards = gat.sends(), gat.forwards()
        for cp in sends:
            cp.start()
        for arrival, fw in zip(gat.arrivals(), forwards):
            arrival.wait_recv()
            fw.start()
        for arrival in gat.forward_arrivals():
            arrival.wait_recv()
        for cp in sends + forwards:
            cp.wait_send()

    return pl.pallas_call(body, name=name, in_specs=[ANY] * n, out_specs=[ANY] * n, out_shape=_Gather.out_shapes(arrs),
                          scratch_shapes=_Gather.semaphores(arrs))(*arrs)


def _grad_blocks(gs):
    return 2 if all(g.shape[1] // 2 % 32 == 0 for g in gs) else 1


def _pair_swap_copies(g_refs, r_refs, send, recv):
    _, _, c, _, sib, _ = _place()
    copies = []
    for a, (g, r) in enumerate(zip(g_refs, r_refs)):
        h = g.shape[1] // 2
        copies.append(_rcopy(g.at[:, pl.ds(pl.multiple_of((1 - c) * h, 8), h)], r, send, recv, a, sib))
    return copies


def _half_stack_shapes(gs, dtype=None):
    return [jax.ShapeDtypeStruct((N_CHIPS, g.shape[1] // 2, g.shape[2]), dtype or g.dtype) for g in gs]


def _pair_swap(gs, wholes):
    n, nw = len(gs), len(wholes)

    def body(*refs):
        ins, outs, (send, recv) = refs[:n + nw], refs[n + nw:2 * (n + nw)], refs[2 * (n + nw):]
        copies = _pair_swap_copies(ins[:n], outs[:n], send, recv)
        copies += [_rcopy(ins[n + k], outs[n + k], send, recv, n + k, _place()[4]) for k in range(nw)]
        for cp in copies:
            cp.start()
        for cp in copies:
            cp.wait()

    return pl.pallas_call(
        body, name="pair_swap", in_specs=[ANY] * (n + nw), out_specs=[ANY] * (n + nw),
        out_shape=_half_stack_shapes(gs) + [jax.ShapeDtypeStruct(w.shape, w.dtype) for w in wholes],
        scratch_shapes=[pltpu.SemaphoreType.DMA((n + nw,)), pltpu.SemaphoreType.DMA((n + nw,))],
    )(*gs, *wholes)


def _pair_sum(place, gs, rs, small=None, name="pair_sum"):
    n = len(gs)
    nb = _grad_blocks(gs)

    def body(place_ref, *refs):
        g_refs, r_refs, p_refs = refs[:n], refs[n:2 * n], refs[-n - 1:-1] if small else refs[-n:]
        for a in range(n):
            p_refs[a][0] = (g_refs[a][0] + r_refs[a][0]).astype(p_refs[a].dtype)
        if small:
            @pl.when((pl.program_id(0) == 0) & (pl.program_id(1) == 0))
            def _():
                refs[-1][...] = refs[2 * n][...] + refs[2 * n + 1][...]

    in_specs, out_specs = [], []
    for g in gs:
        blk = (1, g.shape[1] // 2 // nb, g.shape[2])
        in_specs.append(pl.BlockSpec(blk, lambda i, k, p: (k, p[0] * nb + i, 0)))
    for g in gs:
        blk = (1, g.shape[1] // 2 // nb, g.shape[2])
        in_specs.append(pl.BlockSpec(blk, lambda i, k, p: (k, i, 0)))
        out_specs.append(pl.BlockSpec(blk, lambda i, k, p: (k, i, 0)))
    out_shape = _half_stack_shapes(gs, BF16)
    if small:
        sm_spec = pl.BlockSpec(small[0].shape, lambda i, k, p: (0, 0))
        in_specs += [sm_spec, sm_spec]
        out_specs.append(sm_spec)
        out_shape.append(jax.ShapeDtypeStruct(small[0].shape, F32))
    return pl.pallas_call(
        body, name=name,
        grid_spec=pltpu.PrefetchScalarGridSpec(num_scalar_prefetch=1, grid=(nb, N_CHIPS), in_specs=in_specs,
                                               out_specs=out_specs),
        out_shape=out_shape,
        compiler_params=_params(("arbitrary", "arbitrary")),
    )(place, *gs, *rs, *(small or ()))


def _chip_swap_copies(p_refs, ri_refs, send, recv):
    _, _, c, _, _, others = _place()
    n = len(p_refs)
    return [_rcopy(p_refs[a].at[2 * px + py], ri_refs[a].at[j], send, recv, j * n + a, (px, py, c))
            for j, (px, py) in enumerate(others) for a in range(n)]


def _chip_swap_shapes(ps):
    return [jax.ShapeDtypeStruct((3,) + p.shape[1:], p.dtype) for p in ps]


def _chip_swap(ps, pair):
    n = len(ps)

    def body(*refs):
        start, finish = _chip_swap_plan(refs[:n], refs[n], refs[n + 1:2 * n + 1], refs[2 * n + 1], *refs[2 * n + 2:])
        start()
        finish()

    return pl.pallas_call(
        body, name="chip_swap", in_specs=[ANY] * (n + 1), out_specs=[ANY] * (n + 1),
        out_shape=_chip_swap_out_shapes(ps, pair), scratch_shapes=_chip_swap_semaphores(n),
    )(*ps, pair)


def _chip_swap_plan(p_refs, pair_ref, ri_refs, sm4_ref, send, recv, lsem):
    n = len(p_refs)
    hs = SMALL_ROWS // 2
    x, y, c, me, sib, others = _place()
    local = pltpu.make_async_copy(pair_ref, sm4_ref.at[me], lsem.at[0])
    copies = _chip_swap_copies(p_refs, ri_refs, send, recv)
    arrivals = list(copies)
    for j, (px, py) in enumerate(others):
        copies.append(_rcopy(_half(pair_ref, c, hs), _half(sm4_ref.at[me], c, hs), send, recv, 3 * n + j, (px, py, c)))
        part = _half(sm4_ref.at[2 * px + py], c, hs)
        arrivals.append(_rcopy(part, part, send, recv, 3 * n + j, (px, py, c)))

    def start():
        local.start()
        for cp in copies:
            cp.start()

    def finish():
        for arrival in arrivals:
            arrival.wait_recv()
        for cp in copies:
            cp.wait_send()
        local.wait()

    return start, finish


def _chip_swap_out_shapes(ps, pair):
    return _chip_swap_shapes(ps) + [jax.ShapeDtypeStruct((N_CHIPS,) + pair.shape, pair.dtype)]


def _chip_swap_semaphores(n):
    k = 3 * (n + 1)
    return [pltpu.SemaphoreType.DMA((k,)), pltpu.SemaphoreType.DMA((k,)), pltpu.SemaphoreType.DMA((1,))]


def _chip_sum(place, gs, rs, ris, name="chip_sum"):
    n = len(gs)
    nb = 1

    def body(place_ref, *refs):
        g_refs, r_refs, ri_refs, o_refs = refs[:n], refs[n:2 * n], refs[2 * n:3 * n], refs[3 * n:]
        for a in range(n):
            ri = ri_refs[a]
            o_refs[a][...] = (g_refs[a][0] + r_refs[a][0]) + ri[0].astype(F32) + ri[1].astype(F32) + ri[2].astype(F32)

    in_specs, out_specs, out_shape = [], [], []
    for g in gs:
        blk = (1, g.shape[1] // 2 // nb, g.shape[2])
        in_specs.append(pl.BlockSpec(blk, lambda i, p: (p[1], p[0] * nb + i, 0)))
    for g in gs:
        blk = (1, g.shape[1] // 2 // nb, g.shape[2])
        in_specs.append(pl.BlockSpec(blk, lambda i, p: (p[1], i, 0)))
    for g in gs:
        rb = g.shape[1] // 2 // nb
        in_specs.append(pl.BlockSpec((3, rb, g.shape[2]), lambda i, p: (0, i, 0)))
        out_specs.append(pl.BlockSpec((rb, g.shape[2]), lambda i, p: (p[0] * nb + i, 0)))
        out_shape.append(jax.ShapeDtypeStruct(g.shape[1:], F32))
    return pl.pallas_call(
        body, name=name,
        grid_spec=pltpu.PrefetchScalarGridSpec(num_scalar_prefetch=1, grid=(nb,), in_specs=in_specs, out_specs=out_specs),
        out_shape=out_shape,
        compiler_params=_params(("arbitrary",)),
    )(place, *gs, *rs, *ris)


def _pair_fill_copies(g_refs, send, recv):
    _, _, c, _, sib, _ = _place()
    copies, waits = [], []
    for a, g in enumerate(g_refs):
        h = g.shape[0] // 2
        mine, theirs = _half(g, c, h), _half(g, 1 - c, h)
        copies.append(_rcopy(mine, mine, send, recv, a, sib))
        waits.append(_rcopy(theirs, theirs, send, recv, a, sib))
    return copies, waits


def _pair_fill(gfs, sm4):
    n = len(gfs)
    hs = SMALL_ROWS // 2

    def body(*refs):
        g_refs, sm4_ref = refs[n + 1:2 * n + 1], refs[2 * n + 1]
        send, recv = refs[2 * n + 2:]
        x, y, c, me, sib, others = _place()
        copies, waits = _pair_fill_copies(g_refs, send, recv)
        for j, (px, py) in enumerate(others):
            chip = 2 * px + py
            mine, theirs = _half(sm4_ref.at[chip], c, hs), _half(sm4_ref.at[chip], 1 - c, hs)
            copies.append(pltpu.make_async_remote_copy(src_ref=mine, dst_ref=mine, send_sem=send.at[n + j],
                                                       recv_sem=recv.at[n + j], device_id=sib, device_id_type=MESH))
            waits.append(pltpu.make_async_remote_copy(src_ref=theirs, dst_ref=theirs, send_sem=send.at[n + j],
                                                      recv_sem=recv.at[n + j], device_id=sib, device_id_type=MESH))
        for cp in copies:
            cp.start()
        for w in waits:
            w.wait_recv()
        for cp in copies:
            cp.wait_send()

    return pl.pallas_call(
        body, name="pair_fill", in_specs=[ANY] * (n + 1), out_specs=[ANY] * (n + 1),
        out_shape=[jax.ShapeDtypeStruct(g.shape, g.dtype) for g in gfs] + [jax.ShapeDtypeStruct(sm4.shape, sm4.dtype)],
        input_output_aliases={i: i for i in range(n + 1)},
        scratch_shapes=[pltpu.SemaphoreType.DMA((n + 3,)), pltpu.SemaphoreType.DMA((n + 3,))],
    )(*gfs, sm4)


def _adamw_math(w, g, m, v):
    m = ADAM_B1 * m + (1.0 - ADAM_B1) * g
    v = ADAM_B2 * v + (1.0 - ADAM_B2) * (g * g)
    m_hat = m / (1.0 - ADAM_B1 ** ADAM_STEP)
    v_hat = v / (1.0 - ADAM_B2 ** ADAM_STEP)
    return -ADAM_LR * (m_hat / (jnp.sqrt(v_hat) + ADAM_EPS) + ADAM_WD * w), m, v


def _adamw(items, steps, name):
    n = len(items)

    def body(*refs):
        for a in range(n):
            g = refs[4 * a + 1][...]
            d, mo, vo = _adamw_math(refs[4 * a][...], g, refs[4 * a + 2][...], refs[4 * a + 3][...])
            for out, val in zip(refs[4 * n + 4 * a:4 * n + 4 * a + 4], (g, d, mo, vo)):
                out[...] = val

    spec = lambda w: pl.BlockSpec((w.shape[0] // steps, w.shape[1]), lambda i: (i, 0))
    flat = pl.pallas_call(
        body, name=name, grid=(steps,), in_specs=[spec(it[0]) for it in items for _ in range(4)],
        out_specs=[spec(it[0]) for it in items for _ in range(4)],
        out_shape=[jax.ShapeDtypeStruct(it[0].shape, F32) for it in items for _ in range(4)],
        compiler_params=_params(("arbitrary",)),
    )(*[a for it in items for a in it])
    return [flat[4 * a:4 * a + 4] for a in range(n)]


def _adamw_small(sm4, wmv):
    views = SMALL_VIEWS[:-1]
    n = len(views)

    def body(sm4_ref, *refs):
        g_all = ((sm4_ref[0] + sm4_ref[1]) + sm4_ref[2]) + sm4_ref[3]
        for a, (name, rows, cols) in enumerate(views):
            row = SMALL_OFFSETS[name]
            g = g_all[row:row + rows, :cols]
            d, mo, vo = _adamw_math(refs[3 * a][...], g, refs[3 * a + 1][...], refs[3 * a + 2][...])
            for out, val in zip(refs[3 * n + 4 * a:3 * n + 4 * a + 4], (g, d, mo, vo)):
                out[...] = val
        row = SMALL_OFFSETS["loss"]
        refs[-1][...] = g_all[row:row + 1, :128]

    flat = pl.pallas_call(
        body, name="adamw_small",
        out_shape=[jax.ShapeDtypeStruct((rows, cols), F32) for _, rows, cols in views for _ in range(4)]
        + [jax.ShapeDtypeStruct((1, 128), F32)],
        compiler_params=pltpu.CompilerParams(vmem_limit_bytes=VMEM_LIMIT),
    )(sm4, *[a for t in wmv for a in t])
    return [flat[4 * a:4 * a + 4] for a in range(n)] + [flat[-1]]


SMALL_NAMES = ("attn_pre_norm", "mla_q_norm", "mla_kv_norm", "mla_w_ukv", "mla_out_norm", "hgrn_lb_logits",
               "hgrn_out_norm", "attn_post_norm", "ffn_pre_norm", "ffn_post_norm")
BIG_NAMES = ("w_in", "mla_w_uq", "w_out", "w_gate", "w_up", "w_down")
WEIGHT_NAMES = ("attn_pre_norm", "w_in", "mla_q_norm", "mla_w_uq", "mla_kv_norm", "mla_w_ukv", "mla_out_norm",
                "hgrn_lb_logits", "hgrn_out_norm", "w_out", "attn_post_norm", "ffn_pre_norm", "w_gate", "w_up", "w_down",
                "ffn_post_norm")


UQ_COMM_SHAPE = (192, 384)


def _pack_small(vals):
    parts, row = [], 0
    for name, rows, cols in sorted(SMALL_VIEWS, key=lambda view: SMALL_OFFSETS[view[0]]):
        assert SMALL_OFFSETS[name] == row
        parts.append(jnp.pad(vals[name].reshape(rows, cols), ((0, 0), (0, D_MODEL - cols))))
        row += rows
    parts.append(jnp.zeros((SMALL_ROWS - row, D_MODEL), F32))
    return jnp.concatenate(parts, axis=0)


def kernel(x, positions, attn_pre_norm, w_in, mla_q_norm, mla_w_uq, mla_kv_norm, mla_w_ukv, mla_out_norm, hgrn_lb_logits, hgrn_out_norm, w_out, attn_post_norm, ffn_pre_norm, w_gate, w_up, w_down, ffn_post_norm, loss_target, m_attn_pre_norm, m_w_in, m_mla_q_norm, m_mla_w_uq, m_mla_kv_norm, m_mla_w_ukv, m_mla_out_norm, m_hgrn_lb_logits, m_hgrn_out_norm, m_w_out, m_attn_post_norm, m_ffn_pre_norm, m_w_gate, m_w_up, m_w_down, m_ffn_post_norm, v_attn_pre_norm, v_w_in, v_mla_q_norm, v_mla_w_uq, v_mla_kv_norm, v_mla_w_ukv, v_mla_out_norm, v_hgrn_lb_logits, v_hgrn_out_norm, v_w_out, v_attn_post_norm, v_ffn_pre_norm, v_w_gate, v_w_up, v_w_down, v_ffn_post_norm):
    args = locals()
    W = {n: args[n] for n in WEIGHT_NAMES}
    M = {n: args["m_" + n] for n in WEIGHT_NAMES}
    V = {n: args["v_" + n] for n in WEIGHT_NAMES}
    T = x.shape[1]
    cx, cy, cc = lax.axis_index("x"), lax.axis_index("y"), lax.axis_index("c")

    win_rows = D_IN // N_CHIPS
    shard2d = {"w_in": (win_rows, D_MODEL), "mla_w_uq": (Q_RANK // N_CHIPS, MLA_HEADS * MLA_QK),
               "w_out": (D_MODEL // N_CHIPS, D_MODEL), "w_gate": (FF_SHARD, D_MODEL), "w_up": (FF_SHARD, D_MODEL),
               "w_down": (FF_SHARD, D_MODEL)}
    transposed = ("w_in", "w_gate", "w_up")
    to2d = lambda n, a: a[0].T if n in transposed else a.reshape(shard2d[n])
    from2d = lambda n, t: t.T[None] if n in transposed else t.reshape(W[n].shape)
    me = 2 * cx + cy
    place = jnp.stack([cc, me]).astype(jnp.int32)
    local_b = [to2d(n, W[n]).astype(BF16) for n in BIG_NAMES]
    local_b[0] = jnp.pad(local_b[0], ((0, WIN_COMM_SHAPE[1] - win_rows), (0, 0)))
    stacks = _gather_chips(local_b[:2], "gather_weights")
    win4, wuq4 = [lax.dynamic_update_slice(s, l[None], (me, 0, 0)) for s, l in zip(stacks, local_b)]
    win_t = win4[:, :win_rows].reshape(D_IN, D_MODEL)
    wuq_full = wuq4.reshape(Q_RANK, MLA_HEADS, MLA_QK)
    win_arr, wq_arr, wk_arr, wv_arr = _arrange_weights(win_t, wuq_full, mla_w_ukv[0].astype(BF16))
    small = {n: W[n][0] if n == "mla_w_ukv" else W[n].reshape(-1, W[n].shape[-1]) for n in SMALL_NAMES}

    loss_local, dx, grads, ffn_final = _local_step(x[0], positions.reshape(T, 1), loss_target[0], small, win_arr,
                                                           wq_arr, wk_arr, wv_arr, local_b[2:], place)

    gs = [grads["w_in"], grads["mla_w_uq"].reshape((N_CHIPS,) + UQ_COMM_SHAPE)]
    sm = _pack_small({**grads, "loss": loss_local})
    *rs, ssib = _pair_swap(gs, (sm,))
    *ps, pair = _pair_sum(place, gs, rs, small=(sm, ssib))
    ffn_names, rest_names = BIG_NAMES[3:], BIG_NAMES[:2]
    g2d = dict(zip(ffn_names + BIG_NAMES[2:3], ffn_final))
    adam_in = lambda names_: [(to2d(n, W[n]), g2d[n], to2d(n, M[n]), to2d(n, V[n])) for n in names_]
    early_names = ffn_names + BIG_NAMES[2:3]
    updates = dict(zip(early_names, _adamw(adam_in(early_names), 4, "adamw_ffn")))
    *ris, sm4 = _chip_swap(ps, pair)
    *gfin, smf = _pair_fill(_chip_sum(place, gs, rs, ris), sm4)

    g2d.update({n: gfin[k].reshape((-1,) + shard2d[n][1:]) for k, n in enumerate(rest_names)})
    updates.update(zip(rest_names, _adamw(adam_in(rest_names), 1, "adamw_w_in")))
    G, DW, NM, NV = {}, {}, {}, {}
    for n, outs in updates.items():
        G[n], DW[n], NM[n], NV[n] = (from2d(n, t) for t in outs)
    view2d = lambda n, a: a.reshape(next((r, c) for name, r, c in SMALL_VIEWS if name == n))
    *res, loss_row = _adamw_small(smf, [tuple(view2d(n, t[n]) for t in (W, M, V)) for n in SMALL_NAMES])
    for n, outs in zip(SMALL_NAMES, res):
        G[n], DW[n], NM[n], NV[n] = (t.reshape(W[n].shape) for t in outs)
    loss = loss_row[0, 0]
    return (loss, dx[None], *[G[n] for n in WEIGHT_NAMES], *[DW[n] for n in WEIGHT_NAMES],
            *[NM[n] for n in WEIGHT_NAMES], *[NV[n] for n in WEIGHT_NAMES])
```

```python
import jax
import jax.numpy as jnp
from jax import lax
from jax.experimental import pallas as pl
from jax.experimental.pallas import tpu as pltpu

F32 = jnp.float32
BF16 = jnp.bfloat16
MXU_DTYPE = BF16

D_MODEL = 1024
MLA_HEADS = 8
MLA_NOPE = 64
MLA_ROPE = 32
MLA_V = 64
MLA_QK = MLA_NOPE + MLA_ROPE
Q_RANK = 384
KV_RANK = 128
MLA_WIDTH = MLA_HEADS * MLA_V
HEAD_PAD = 128
HGRN_HEADS = 4
HGRN_DIM = 128
HGRN_WIDTH = HGRN_HEADS * HGRN_DIM
CHUNK = 64
SUB = 16
HGRN_CPI = 4
D_IN = Q_RANK + KV_RANK + MLA_ROPE + 4 * HGRN_WIDTH
D_IN_ARR = Q_RANK + KV_RANK + HEAD_PAD + 4 * HGRN_WIDTH
D_FF = 2816
N_CHIPS = 4
FF_SHARD = D_FF // N_CHIPS
EPS = 1e-6
ROPE_THETA = 10000.0
ATTN_SCALE = MLA_QK ** -0.5
ATTN_SCALE_LOG2 = ATTN_SCALE * 1.4426950408889634
NEG_BIG = -1e30

ADAM_LR = 0.001
ADAM_B1 = 0.9
ADAM_B2 = 0.999
ADAM_EPS = 1e-08
ADAM_WD = 0.01
ADAM_STEP = 10

VMEM_LIMIT = 56 * 1024 * 1024
FFN_BWD_VMEM = 62 * 1024 * 1024

SMALL_VIEWS = (("attn_pre_norm", 1, 1024), ("mla_q_norm", 1, 384), ("mla_kv_norm", 1, 128), ("mla_w_ukv", 128, 1024),
               ("mla_out_norm", 1, 512), ("hgrn_lb_logits", 2, 512), ("hgrn_out_norm", 1, 512),
               ("attn_post_norm", 1, 1024), ("ffn_pre_norm", 1, 1024), ("ffn_post_norm", 1, 1024), ("loss", 1, 1))
ROW_TILE = 8


def _small_layout():
    offsets, row = {}, 0
    for whole in (True, False):
        for name, rows, _ in SMALL_VIEWS:
            if (rows % ROW_TILE == 0) == whole:
                offsets[name] = row
                row += rows
    return offsets, -(-row // (2 * ROW_TILE)) * 2 * ROW_TILE


SMALL_OFFSETS, SMALL_ROWS = _small_layout()

MESH = pl.DeviceIdType.MESH
ANY = pl.BlockSpec(memory_space=pl.ANY)


def _dot(a, b, dims, exact):
    if exact:
        return lax.dot_general(a.astype(F32), b.astype(F32), (dims, ((), ())), precision=lax.Precision.HIGH,
                               preferred_element_type=F32)
    return lax.dot_general(a.astype(MXU_DTYPE), b.astype(MXU_DTYPE), (dims, ((), ())), preferred_element_type=F32)


def _mm(a, b, exact=False):
    return _dot(a, b, ((1,), (0,)), exact)


def _mm_nt(a, b, exact=False):
    return _dot(a, b, ((1,), (1,)), exact)


def _mm_tn(a, b, exact=False):
    return _dot(a, b, ((0,), (0,)), exact)


def _rms_fwd(x, w):
    r = lax.rsqrt(jnp.mean(x * x, axis=-1, keepdims=True) + EPS)
    xn = x * r
    return xn * w, xn, r


def _rms_bwd(dy, xn, r, w):
    dxn = dy * w
    dx = r * (dxn - xn * jnp.mean(dxn * xn, axis=-1, keepdims=True))
    dw = jnp.sum(dy * xn, axis=0, keepdims=True)
    return dx, dw


def _group_sums(v, gs):
    t, n = v.shape
    lane = lax.broadcasted_iota(jnp.int32, (t, 128), 1)
    out = []
    for p in range(n // 128):
        vb = v[:, 128 * p:128 * (p + 1)]
        if gs == 128:
            out.append(jnp.sum(vb, axis=-1, keepdims=True))
        else:
            out.append(jnp.sum(jnp.where(lane < 64, vb, 0.0), axis=-1, keepdims=True))
            out.append(jnp.sum(jnp.where(lane >= 64, vb, 0.0), axis=-1, keepdims=True))
    return out


def _group_bcast(sums, gs, t):
    lane = lax.broadcasted_iota(jnp.int32, (t, 128), 1)
    if gs == 128:
        return jnp.concatenate([jnp.broadcast_to(s, (t, 128)) for s in sums], axis=-1)
    return jnp.concatenate([jnp.where(lane < 64, sums[2 * p], sums[2 * p + 1]) for p in range(len(sums) // 2)],
                           axis=-1)


def _grms_fwd(x, w, gs):
    t = x.shape[0]
    r = lax.rsqrt(_group_bcast(_group_sums(x * x, gs), gs, t) * (1.0 / gs) + EPS)
    xn = x * r
    return xn * w, xn, r


def _grms_bwd(dy, xn, r, w, gs):
    t = dy.shape[0]
    dxn = dy * w
    dx = r * (dxn - xn * (_group_bcast(_group_sums(dxn * xn, gs), gs, t) * (1.0 / gs)))
    dw = jnp.sum(dy * xn, axis=0, keepdims=True)
    return dx, dw


def _rope_tables(c_tab, s_tab):
    lane = lax.broadcasted_iota(jnp.int32, c_tab.shape, 1)
    first = (lane >= MLA_NOPE) & (lane < MLA_NOPE + MLA_ROPE // 2)
    second = (lane >= MLA_NOPE + MLA_ROPE // 2) & (lane < MLA_QK)
    return c_tab, jnp.where(first, -s_tab, 0.0), jnp.where(second, s_tab, 0.0)


def _rope(v, c, sa, sb):
    return v * c + pltpu.roll(v, HEAD_PAD - MLA_ROPE // 2, 1) * sa + pltpu.roll(v, MLA_ROPE // 2, 1) * sb


def _rope_bwd(d, c, sa, sb):
    return d * c - pltpu.roll(d, HEAD_PAD - MLA_ROPE // 2, 1) * sa - pltpu.roll(d, MLA_ROPE // 2, 1) * sb


def _params(sem, vmem=VMEM_LIMIT):
    return pltpu.CompilerParams(dimension_semantics=sem, vmem_limit_bytes=vmem)


def _in_fwd(x, pos, invf, w_pre, win, qnw, wq, kvnw, wk, wv, tt=512):
    T = x.shape[0]

    def body(x_ref, pos_ref, invf_ref, wpre_ref, win_ref, qnw_ref, wq_ref, kvnw_ref, wk_ref, wv_ref,
             cq_ref, ckv_ref, xph_ref, q_ref, k_ref, v_ref, kt_ref, vt_ref, rc_ref, rs_ref):
        u, _, _ = _rms_fwd(x_ref[...], wpre_ref[...])
        lo = Q_RANK + KV_RANK + HEAD_PAD
        xp = _mm_nt(u, win_ref[:lo, :])
        xph_ref[...] = _mm_nt(u, win_ref[lo:, :])
        cq = xp[:, :Q_RANK]
        ckv = xp[:, Q_RANK:Q_RANK + KV_RANK]
        kr = xp[:, Q_RANK + KV_RANK:]
        cq_ref[...] = cq
        ckv_ref[...] = ckv
        ang = pos_ref[...].astype(F32) * invf_ref[...]
        c_tab = jnp.cos(ang)
        s_tab = jnp.sin(ang)
        rc_ref[...] = c_tab
        rs_ref[...] = s_tab
        c, sa, sb = _rope_tables(c_tab, s_tab)
        qn, _, _ = _rms_fwd(cq, qnw_ref[...])
        q = _mm(qn, wq_ref[...])
        kvn, _, _ = _rms_fwd(ckv, kvnw_ref[...])
        kn = _mm(kvn, wk_ref[...])
        v = _mm(kvn, wv_ref[...])
        v_ref[...] = v.astype(v_ref.dtype)
        vt_ref[...] = v.T.astype(vt_ref.dtype)
        krr = _rope(kr, c, sa, sb)
        for h in range(MLA_HEADS):
            sl = slice(HEAD_PAD * h, HEAD_PAD * (h + 1))
            q_ref[:, sl] = (_rope(q[:, sl], c, sa, sb) * ATTN_SCALE_LOG2).astype(q_ref.dtype)
            kh = kn[:, sl] + krr
            k_ref[:, sl] = kh.astype(k_ref.dtype)
            kt_ref[sl, :] = kh.T.astype(kt_ref.dtype)

    row = lambda w: pl.BlockSpec((tt, w), lambda i: (i, 0))
    full = lambda a: pl.BlockSpec(a.shape, lambda i: (0,) * a.ndim)
    qk_w = MLA_HEADS * HEAD_PAD
    return pl.pallas_call(
        body, name="in_fwd", grid=(T // tt,),
        in_specs=[row(D_MODEL), row(1), full(invf), full(w_pre), full(win), full(qnw), full(wq), full(kvnw),
                  full(wk), full(wv)],
        out_specs=[row(Q_RANK), row(KV_RANK), row(4 * HGRN_WIDTH), row(qk_w), row(qk_w), row(MLA_WIDTH),
                   pl.BlockSpec((qk_w, tt), lambda i: (0, i)), pl.BlockSpec((MLA_WIDTH, tt), lambda i: (0, i)),
                   row(HEAD_PAD), row(HEAD_PAD)],
        out_shape=[jax.ShapeDtypeStruct((T, Q_RANK), F32), jax.ShapeDtypeStruct((T, KV_RANK), F32),
                   jax.ShapeDtypeStruct((T, 4 * HGRN_WIDTH), F32), jax.ShapeDtypeStruct((T, qk_w), MXU_DTYPE),
                   jax.ShapeDtypeStruct((T, qk_w), MXU_DTYPE), jax.ShapeDtypeStruct((T, MLA_WIDTH), MXU_DTYPE),
                   jax.ShapeDtypeStruct((qk_w, T), MXU_DTYPE), jax.ShapeDtypeStruct((MLA_WIDTH, T), MXU_DTYPE),
                   jax.ShapeDtypeStruct((T, HEAD_PAD), F32), jax.ShapeDtypeStruct((T, HEAD_PAD), F32)],
        compiler_params=_params(("arbitrary",)),
    )(x, pos, invf, w_pre, win, qnw, wq, kvnw, wk, wv)


def _attn_fwd_t(qb, kb, vt, gather=(), tq=256, hps=8):
    T = qb.shape[0]
    nq = T // tq
    ng = len(gather)
    steps = (MLA_HEADS // hps) * nq
    pass_on = steps - 3

    def body(q_ref, k_ref, vt_ref, *rest):
        o_ref, lse_ref = rest[ng:ng + 2]
        acc_scr = rest[2 * ng + 2]
        qi = pl.program_id(1)
        step_no = pl.program_id(0) * nq + qi
        if ng:
            gat = _Gather(rest[:ng], rest[ng + 2:2 * ng + 2], *rest[2 * ng + 3:])

            @pl.when(step_no == 0)
            def _():
                for cp in gat.sends():
                    cp.start()

            @pl.when(step_no == pass_on)
            def _():
                for arrival in gat.arrivals():
                    arrival.wait_recv()
                for cp in gat.forwards():
                    cp.start()

        heads = [slice(HEAD_PAD * a, HEAD_PAD * (a + 1)) for a in range(hps)]
        acc_scr[...] = jnp.zeros_like(acc_scr)

        def step(j, carry, masked):
            start = pl.multiple_of(j * tq, tq)
            scores = [_mm_nt(k_ref[pl.ds(start, tq), heads[a]], q_ref[:, heads[a]]) for a in range(hps)]
            new, probs, alphas = [], [], []
            for a in range(hps):
                m, l = carry[a]
                s = scores[a]
                if masked:
                    kk = lax.broadcasted_iota(jnp.int32, (tq, tq), 0)
                    qq = lax.broadcasted_iota(jnp.int32, (tq, tq), 1)
                    s = jnp.where(kk <= qq, s, NEG_BIG)
                m_new = jnp.maximum(m, jnp.max(s, axis=0, keepdims=True))
                alpha = jnp.exp2(m - m_new)
                p = jnp.exp2(s - m_new)
                l = l * alpha + jnp.sum(p, axis=0, keepdims=True)
                new.append((m_new, l))
                probs.append(p.astype(MXU_DTYPE))
                alphas.append(alpha)
                if a % 2:
                    pr = a // 2
                    vtj = vt_ref[2 * MLA_V * pr:2 * MLA_V * (pr + 1), pl.ds(start, tq)]
                    none = jnp.zeros((MLA_V, tq), vtj.dtype)
                    pv = (_mm(jnp.concatenate([vtj[:MLA_V], none], axis=0), probs[a - 1])
                          + _mm(jnp.concatenate([none, vtj[MLA_V:]], axis=0), probs[a]))
                    acc_scr[pr] = acc_scr[pr] * jnp.where(row < MLA_V, alphas[a - 1], alphas[a]) + pv
            return tuple(new)

        row = lax.broadcasted_iota(jnp.int32, (2 * MLA_V, tq), 0)
        init = tuple((jnp.full((1, tq), NEG_BIG, F32), jnp.zeros((1, tq), F32)) for _ in range(hps))
        carry = lax.fori_loop(0, qi, lambda j, c: step(j, c, False), init)
        carry = step(qi, carry, True)
        for pr in range(hps // 2):
            (m0, l0), (m1, l1) = carry[2 * pr], carry[2 * pr + 1]
            ot = acc_scr[pr] / jnp.where(row < MLA_V, l0, l1)
            o_ref[:, 2 * MLA_V * pr:2 * MLA_V * (pr + 1)] = ot.T
            lse_ref[pr, 0:1, :] = m0 + jnp.log2(l0)
            lse_ref[pr, 1:2, :] = m1 + jnp.log2(l1)

        if ng:
            @pl.when(step_no == steps - 1)
            def _():
                for arrival in gat.forward_arrivals():
                    arrival.wait_recv()
                for cp in gat.sends() + gat.forwards():
                    cp.wait_send()

    return pl.pallas_call(
        body, name="attn_fwd", grid=(MLA_HEADS // hps, nq),
        in_specs=[pl.BlockSpec((tq, hps * HEAD_PAD), lambda g, i: (i, g)),
                  pl.BlockSpec((T, hps * HEAD_PAD), lambda g, i: (0, g)),
                  pl.BlockSpec((hps * MLA_V, T), lambda g, i: (g, 0))] + [ANY] * ng,
        out_specs=[pl.BlockSpec((tq, hps * MLA_V), lambda g, i: (i, g)),
                   pl.BlockSpec((hps // 2, 2, tq), lambda g, i: (g, 0, i))] + [ANY] * ng,
        out_shape=[jax.ShapeDtypeStruct((T, MLA_WIDTH), F32), jax.ShapeDtypeStruct((MLA_HEADS // 2, 2, T), F32)]
        + _Gather.out_shapes(gather),
        scratch_shapes=[pltpu.VMEM((hps // 2, 2 * MLA_V, tq), F32)] + (_Gather.semaphores(gather) if ng else []),
        compiler_params=_params(("arbitrary", "arbitrary")),
    )(qb, kb, vt, *gather)


def _attn_bwd_t(qb, kb, kt, vb, dob, lse, dvec, send=(), tq=512, hps=4):
    T = qb.shape[0]
    nq = T // tq
    ns = len(send)
    steps = (MLA_HEADS // hps) * nq

    def body(q_ref, k_ref, kt_ref, v_ref, do_ref, lse_ref, d_ref, *rest):
        dqt_ref, dk_ref, dv_ref = rest[ns:ns + 3]
        va_scr, dv_scr = rest[2 * ns + 3:2 * ns + 5]
        j = pl.program_id(1)
        step_no = pl.program_id(0) * nq + j
        if ns:
            @pl.when(step_no == 0)
            def _():
                for cp in _chip_swap_copies(rest[:ns], rest[ns + 3:2 * ns + 3], *rest[2 * ns + 5:]):
                    cp.start()

        @pl.when(j == 0)
        def _():
            dqt_ref[...] = jnp.zeros_like(dqt_ref)

        lane = lax.broadcasted_iota(jnp.int32, (tq, 2 * MLA_V), 1)
        heads = [slice(HEAD_PAD * a, HEAD_PAD * (a + 1)) for a in range(hps)]
        pairs = [slice(2 * MLA_V * p, 2 * MLA_V * (p + 1)) for p in range(hps // 2)]
        for pr in range(hps // 2):
            vpair = v_ref[:, pairs[pr]]
            va_scr[2 * pr] = jnp.where(lane < MLA_V, vpair, jnp.zeros_like(vpair))
            va_scr[2 * pr + 1] = jnp.where(lane >= MLA_V, vpair, jnp.zeros_like(vpair))
        dk_ref[...] = jnp.zeros_like(dk_ref)
        dv_scr[...] = jnp.zeros_like(dv_scr)

        def step(i, masked):
            start = pl.multiple_of(i * tq, tq)
            rows = pl.ds(start, tq)
            scores = [_mm_nt(k_ref[:, heads[a]], q_ref[rows, heads[a]]) for a in range(hps)]
            dps = [_mm_nt(va_scr[a], do_ref[rows, pairs[a // 2]]) for a in range(hps)]
            for a in range(hps):
                pr, r = a // 2, a % 2
                p = jnp.exp2(scores[a] - lse_ref[pr, r:r + 1, rows])
                if masked:
                    kk = lax.broadcasted_iota(jnp.int32, (tq, tq), 0)
                    qq = lax.broadcasted_iota(jnp.int32, (tq, tq), 1)
                    p = jnp.where(kk <= qq, p, 0.0)
                ds = p * (dps[a] - d_ref[pr, r:r + 1, rows])
                dv_scr[a] += _mm(p, do_ref[rows, pairs[pr]])
                dk_ref[:, heads[a]] += _mm(ds, q_ref[rows, heads[a]])
                dqt_ref[heads[a], rows] += _mm(kt_ref[heads[a], :], ds)

        def loop_body(i, _):
            step(i, False)
            return 0

        step(j, True)
        lax.fori_loop(j + 1, nq, loop_body, 0)
        for pr in range(hps // 2):
            dv_ref[:, pairs[pr]] = jnp.where(lane < MLA_V, dv_scr[2 * pr], dv_scr[2 * pr + 1])
        dk_ref[...] = dk_ref[...] * (ATTN_SCALE / ATTN_SCALE_LOG2)

        if ns:
            @pl.when(step_no == steps - 1)
            def _():
                for cp in _chip_swap_copies(rest[:ns], rest[ns + 3:2 * ns + 3], *rest[2 * ns + 5:]):
                    cp.wait()

    stat = pl.BlockSpec((hps // 2, 2, T), lambda g, j: (g, 0, 0))
    return pl.pallas_call(
        body, name="attn_bwd", grid=(MLA_HEADS // hps, nq),
        in_specs=[pl.BlockSpec((T, hps * HEAD_PAD), lambda g, j: (0, g)),
                  pl.BlockSpec((tq, hps * HEAD_PAD), lambda g, j: (j, g)),
                  pl.BlockSpec((hps * HEAD_PAD, tq), lambda g, j: (g, j)),
                  pl.BlockSpec((tq, hps * MLA_V), lambda g, j: (j, g)),
                  pl.BlockSpec((T, hps * MLA_V), lambda g, j: (0, g)), stat, stat] + [ANY] * ns,
        out_specs=[pl.BlockSpec((hps * HEAD_PAD, T), lambda g, j: (g, 0)),
                   pl.BlockSpec((tq, hps * HEAD_PAD), lambda g, j: (j, g)),
                   pl.BlockSpec((tq, hps * MLA_V), lambda g, j: (j, g))] + [ANY] * ns,
        out_shape=[jax.ShapeDtypeStruct((MLA_HEADS * HEAD_PAD, T), F32),
                   jax.ShapeDtypeStruct((T, MLA_HEADS * HEAD_PAD), F32),
                   jax.ShapeDtypeStruct((T, MLA_WIDTH), F32)] + _chip_swap_shapes(send),
        scratch_shapes=[pltpu.VMEM((hps, tq, 2 * MLA_V), vb.dtype), pltpu.VMEM((hps, tq, 2 * MLA_V), F32)]
        + ([pltpu.SemaphoreType.DMA((3 * ns,)), pltpu.SemaphoreType.DMA((3 * ns,))] if ns else []),
        compiler_params=_params(("arbitrary", "arbitrary")),
    )(qb, kb, kt, vb, dob, lse, dvec, *send)


def _cumsum_rows(x):
    n = x.shape[0]
    row = lax.broadcasted_iota(jnp.int32, x.shape, 0)
    s = 1
    while s < n:
        x = x + jnp.where(row >= s, pltpu.roll(x, s, 0), 0.0)
        s *= 2
    return x


def _rev_cumsum_rows(x):
    n = x.shape[0]
    row = lax.broadcasted_iota(jnp.int32, x.shape, 0)
    s = 1
    while s < n:
        x = x + jnp.where(row < n - s, pltpu.roll(x, n - s, 0), 0.0)
        s *= 2
    return x


def _lb_from_logits(l):
    l0, l1 = l[0:1, :], l[1:2, :]
    m = jnp.maximum(l0, l1)
    e0, e1 = jnp.exp(l0 - m), jnp.exp(l1 - m)
    return e0 / (e0 + e1)


def _hgrn_gates(hq, hf, lb):
    sig_f = jax.nn.sigmoid(hf)
    f = lb + (1.0 - lb) * sig_f
    sig_q = jax.nn.sigmoid(hq)
    return sig_f, f, jnp.log(f), 1.0 - f, sig_q, hq * sig_q


def _hgrn_intra(q, kk, b, exact=False):
    row = lax.broadcasted_iota(jnp.int32, b.shape, 0)
    qs, ks, eqs, eks, a_rows = [], [], [], [], []
    for i in range(CHUNK // SUB):
        ref = b[SUB * i + SUB // 2:SUB * i + SUB // 2 + 1, :]
        eq = jnp.exp(b[SUB * i:SUB * (i + 1), :] - ref)
        ek = jnp.exp(jnp.where(row < SUB * (i + 1), ref - b, NEG_BIG))
        qi = q[SUB * i:SUB * (i + 1), :] * eq
        ki = kk * ek
        a_rows.append(_mm_nt(qi, ki, exact))
        qs.append(qi), ks.append(ki), eqs.append(eq), eks.append(ek)
    tt = lax.broadcasted_iota(jnp.int32, (CHUNK, CHUNK), 0)
    ss = lax.broadcasted_iota(jnp.int32, (CHUNK, CHUNK), 1)
    causal = ss <= tt
    a = jnp.where(causal, jnp.concatenate(a_rows, axis=0), 0.0)
    return a, causal, qs, ks, eqs, eks


def _hgrn_fwd(xph, lbl, tg=512):
    T = xph.shape[0]
    ng, ncg = T // tg, tg // CHUNK
    cols = [slice(HGRN_DIM * h, HGRN_DIM * (h + 1)) for h in range(HGRN_HEADS)]

    def body(lbl_ref, hq_ref, hf_ref, hi_ref, o_ref, st_ref, s_scr):
        @pl.when(pl.program_id(0) == 0)
        def _():
            s_scr[...] = jnp.zeros_like(s_scr)

        lb = _lb_from_logits(lbl_ref[...])

        def chunks(it, _):
            pre = []
            for k in range(HGRN_CPI):
                c = it * HGRN_CPI + k
                rows = pl.ds(pl.multiple_of(c * CHUNK, CHUNK), CHUNK)
                for cs in cols:
                    _, _, lf, kk, _, q = _hgrn_gates(hq_ref[rows, cs], hf_ref[rows, cs], lb[:, cs])
                    v = hi_ref[rows, cs]
                    b = _cumsum_rows(lf)
                    a = _hgrn_intra(q, kk, b)[0]
                    b_last = b[CHUNK - 1:CHUNK, :]
                    pre.append((c, rows, q * jnp.exp(b), a, v, jnp.exp(b_last), _mm_tn(v, kk * jnp.exp(b_last - b))))
            for i, (c, rows, qe, a, v, ebl, upd) in enumerate(pre):
                h = i % HGRN_HEADS
                st = s_scr[h]
                st_ref[h, c] = st
                o_ref[rows, cols[h]] = _mm_nt(qe, st) + _mm(a, v)
                s_scr[h] = st * ebl + upd
            return 0

        lax.fori_loop(0, ncg // HGRN_CPI, chunks, 0)

    col = lambda k: pl.BlockSpec((tg, HGRN_WIDTH), lambda g: (g, k))
    return pl.pallas_call(
        body, name="hgrn_fwd", grid=(ng,),
        in_specs=[pl.BlockSpec((2, HGRN_WIDTH), lambda g: (0, 0)), col(0), col(1), col(2)],
        out_specs=[col(0), pl.BlockSpec((HGRN_HEADS, ncg, HGRN_DIM, HGRN_DIM), lambda g: (0, g, 0, 0))],
        out_shape=[jax.ShapeDtypeStruct((T, HGRN_WIDTH), F32),
                   jax.ShapeDtypeStruct((HGRN_HEADS, T // CHUNK, HGRN_DIM, HGRN_DIM), F32)],
        scratch_shapes=[pltpu.VMEM((HGRN_HEADS, HGRN_DIM, HGRN_DIM), F32)],
        compiler_params=_params(("arbitrary",)),
    )(lbl, xph, xph, xph)


def _hgrn_bwd(xph, lbl, states, d_o, fill=(), tg=512):
    T = xph.shape[0]
    ng, ncg = T // tg, tg // CHUNK
    cols = [slice(HGRN_DIM * h, HGRN_DIM * (h + 1)) for h in range(HGRN_HEADS)]
    nsub = CHUNK // SUB
    nf = len(fill)

    def body(lbl_ref, hq_ref, hf_ref, hi_ref, st_ref, do_ref, *rest):
        dhq_ref, dhf_ref, dhi_ref, dlg_ref = rest[nf:nf + 4]
        ds_scr, dlb_scr = rest[2 * nf + 4:2 * nf + 6]
        fill_copies = lambda: _pair_fill_copies(rest[nf + 4:2 * nf + 4], *rest[2 * nf + 6:])
        g = pl.program_id(0)

        @pl.when(g == 0)
        def _():
            ds_scr[...] = jnp.zeros_like(ds_scr)
            dlb_scr[...] = jnp.zeros_like(dlb_scr)
            for cp in (fill_copies()[0] if nf else ()):
                cp.start()

        lb = _lb_from_logits(lbl_ref[...])

        def chunks(it, _):
            pre = []
            for k, h in ((k, h) for k in range(HGRN_CPI) for h in range(HGRN_HEADS)):
                cs = cols[h]
                c = ncg - 1 - (it * HGRN_CPI + k)
                rows = pl.ds(pl.multiple_of(c * CHUNK, CHUNK), CHUNK)
                hq = hq_ref[rows, cs]
                sig_f, f, lf, kk, sig_q, q = _hgrn_gates(hq, hf_ref[rows, cs], lb[:, cs])
                v = hi_ref[rows, cs]
                do = do_ref[rows, cs]
                b = _cumsum_rows(lf)
                eb = jnp.exp(b)
                a, causal, qs, ks, eqs, eks = _hgrn_intra(q, kk, b)
                b_last = b[CHUNK - 1:CHUNK, :]
                st = st_ref[h, c]
                pre.append(dict(h=h, cs=cs, rows=rows, hq=hq, sig_f=sig_f, f=f, kk=kk, sig_q=sig_q, q=q, v=v, eb=eb, qs=qs,
                                ks=ks, eqs=eqs,
                                eks=eks, ebl=jnp.exp(b_last), el=jnp.exp(b_last - b), st=st,
                                da=jnp.where(causal, _mm_nt(do, v, True), 0.0), dq=_mm(do, st, True) * eb,
                                dv=_mm_tn(a, do), dsu=_mm_tn(do, q * eb, True)))
            for w in pre:
                dq_rows = []
                dk = jnp.zeros_like(w["q"])
                for i in range(nsub):
                    dai = w["da"][SUB * i:SUB * (i + 1), :]
                    dq_rows.append(_mm(dai, w["ks"][i], True) * w["eqs"][i])
                    dk = dk + _mm_tn(dai, w["qs"][i], True) * w["eks"][i]
                w["dq"] = w["dq"] + jnp.concatenate(dq_rows, axis=0)
                w["dk"] = dk
            for w in pre:
                h, cs, rows = w["h"], w["cs"], w["rows"]
                kk, el, ebl, dst = w["kk"], w["el"], w["ebl"], ds_scr[h]
                dk_state = _mm(w["v"], dst, True) * el
                dk = w["dk"] + dk_state
                e_last = (ebl * jnp.sum(w["st"] * dst, axis=0, keepdims=True)
                          + jnp.sum(kk * dk_state, axis=0, keepdims=True))
                dlf = _rev_cumsum_rows(w["q"] * w["dq"] - kk * dk) + e_last
                ds_scr[h] = dst * ebl + w["dsu"]
                df = dlf / w["f"] - dk
                sig_f, sig_q = w["sig_f"], w["sig_q"]
                dhf_ref[rows, cs] = df * (1.0 - lb[:, cs]) * sig_f * (1.0 - sig_f)
                dlb_scr[:, cs] += jnp.sum(df * (1.0 - sig_f), axis=0, keepdims=True)
                dhq_ref[rows, cs] = w["dq"] * sig_q * (1.0 + w["hq"] * (1.0 - sig_q))
                dhi_ref[rows, cs] = w["dv"] + _mm_nt(kk * el, dst)
            return 0

        lax.fori_loop(0, ncg // HGRN_CPI, chunks, 0)

        @pl.when(g == ng - 1)
        def _():
            dl0 = dlb_scr[...] * lb * (1.0 - lb)
            dlg_ref[...] = jnp.concatenate([dl0, -dl0], axis=0)
            if nf:
                copies, waits = fill_copies()
                for w in waits:
                    w.wait_recv()
                for cp in copies:
                    cp.wait_send()

    col = lambda k: pl.BlockSpec((tg, HGRN_WIDTH), lambda g: (ng - 1 - g, k))
    logits = pl.BlockSpec((2, HGRN_WIDTH), lambda g: (0, 0))
    big = jax.ShapeDtypeStruct((T, HGRN_WIDTH), F32)
    n_in, n_out = 6, 4
    return pl.pallas_call(
        body, name="hgrn_bwd", grid=(ng,),
        in_specs=[logits, col(0), col(1), col(2),
                  pl.BlockSpec((HGRN_HEADS, ncg, HGRN_DIM, HGRN_DIM), lambda g: (0, ng - 1 - g, 0, 0)), col(0)] + [ANY] * nf,
        out_specs=[col(0), col(0), col(0), logits] + [ANY] * nf,
        out_shape=[big, big, big, jax.ShapeDtypeStruct((2, HGRN_WIDTH), F32)]
        + [jax.ShapeDtypeStruct(f.shape, f.dtype) for f in fill],
        input_output_aliases={n_in + k: n_out + k for k in range(nf)},
        scratch_shapes=[pltpu.VMEM((HGRN_HEADS, HGRN_DIM, HGRN_DIM), F32), pltpu.VMEM((1, HGRN_WIDTH), F32)]
        + ([pltpu.SemaphoreType.DMA((nf,)), pltpu.SemaphoreType.DMA((nf,))] if nf else []),
        compiler_params=_params(("arbitrary",)),
    )(lbl, xph, xph, xph, states, d_o, *fill)


def _proj_fwd(x, o_raw, oh_raw, xph, wout, w_mla, w_hg, w_post, w_fpre, tt=512):
    T = x.shape[0]

    def body(x_ref, o_ref, oh_ref, hg_ref, wout_ref, wmla_ref, whg_ref, wpost_ref, wfpre_ref,
             h1_ref, y1_ref, z_ref, mix_ref):
        om, _, _ = _grms_fwd(o_ref[...], wmla_ref[...], MLA_V)
        hg = hg_ref[...]
        ohn, _, _ = _grms_fwd(oh_ref[...], whg_ref[...], HGRN_DIM)
        mix = jnp.concatenate([om, ohn * (hg * jax.nn.sigmoid(hg))], axis=-1)
        mix_ref[...] = mix.astype(mix_ref.dtype)
        y1 = _mm(mix, wout_ref[...])
        y1_ref[...] = y1
        h1 = x_ref[...] + _rms_fwd(y1, wpost_ref[...])[0]
        h1_ref[...] = h1
        z_ref[...] = _rms_fwd(h1, wfpre_ref[...])[0].astype(z_ref.dtype)

    row = lambda w: pl.BlockSpec((tt, w), lambda i: (i, 0))
    full = lambda a: pl.BlockSpec(a.shape, lambda i: (0,) * a.ndim)
    sds = jax.ShapeDtypeStruct
    return pl.pallas_call(
        body, name="proj_fwd", grid=(T // tt,),
        in_specs=[row(D_MODEL), row(MLA_WIDTH), row(HGRN_WIDTH), pl.BlockSpec((tt, HGRN_WIDTH), lambda i: (i, 3)),
                  full(wout), full(w_mla), full(w_hg), full(w_post), full(w_fpre)],
        out_specs=[row(D_MODEL)] * 4,
        out_shape=[sds((T, D_MODEL), F32), sds((T, D_MODEL), F32), sds((T, D_MODEL), MXU_DTYPE),
                   sds((T, D_MODEL), MXU_DTYPE)],
        compiler_params=_params(("arbitrary",)),
    )(x, o_raw, oh_raw, xph, wout, w_mla, w_hg, w_post, w_fpre)


def _ffn_fwd(zb, h1, tgt, w_fpost, wg, wu, wd, tt=256):
    T = zb.shape[0]
    nj = N_CHIPS

    def body(z_ref, h1_ref, tgt_ref, wfpost_ref, wg_ref, wu_ref, wd_ref, g_ref, up_ref, dy2_ref, dh2_ref, loss_ref, dwf_ref):
        @pl.when(pl.program_id(0) == 0)
        def _():
            loss_ref[...] = jnp.zeros_like(loss_ref)
            dwf_ref[...] = jnp.zeros_like(dwf_ref)

        z = z_ref[...]
        gs = [_mm_nt(z, wg_ref[j]) for j in range(nj)]
        ups = [_mm_nt(z, wu_ref[j]) for j in range(nj)]
        y2 = jnp.zeros((tt, D_MODEL), F32)
        for j in range(nj):
            g_ref[j] = gs[j]
            up_ref[j] = ups[j]
            y2 = y2 + _mm(gs[j] * jax.nn.sigmoid(gs[j]) * ups[j], wd_ref[j])
        w = wfpost_ref[...]
        y2s, y2n, r2 = _rms_fwd(y2, w)
        e = h1_ref[...] + y2s - tgt_ref[...]
        loss_ref[...] += jnp.sum(e * e, axis=0, keepdims=True)
        dh2 = e * (1.0 / D_MODEL)
        dh2_ref[...] = dh2
        dy2, dwf = _rms_bwd(dh2, y2n, r2, w)
        dy2_ref[...] = dy2.astype(dy2_ref.dtype)
        dwf_ref[...] += dwf

    row = pl.BlockSpec((tt, D_MODEL), lambda i: (i, 0))
    vec = pl.BlockSpec((1, D_MODEL), lambda i: (0, 0))
    resident = pl.BlockSpec((nj, FF_SHARD, D_MODEL), lambda i: (0, 0, 0), pipeline_mode=pl.Buffered(1))
    act = pl.BlockSpec((nj, tt, FF_SHARD), lambda i: (0, i, 0))
    sds = jax.ShapeDtypeStruct
    return pl.pallas_call(
        body, name="ffn_fwd", grid=(T // tt,),
        in_specs=[row, row, row, vec, resident, resident, resident],
        out_specs=[act, act, row, row, vec, vec],
        out_shape=[sds((nj, T, FF_SHARD), F32), sds((nj, T, FF_SHARD), F32), sds((T, D_MODEL), MXU_DTYPE),
                   sds((T, D_MODEL), F32), sds((1, D_MODEL), F32), sds((1, D_MODEL), F32)],
        compiler_params=_params(("arbitrary",)),
    )(zb, h1, tgt, w_fpost, wg, wu, wd)


def _ffn_bwd(zb, g, up, dy2b, wg, wu, wd, tt=512):
    T = zb.shape[0]
    nj = N_CHIPS

    def body(z_ref, g_ref, up_ref, dy2_ref, wg_ref, wu_ref, wd_ref, dwg_ref, dwu_ref, dwd_ref, dz_ref, acc_ref):
        j, i = pl.program_id(0), pl.program_id(1)
        rows = pl.ds(pl.multiple_of(i * tt, tt), tt)

        @pl.when(i == 0)
        def _():
            dwg_ref[...] = jnp.zeros_like(dwg_ref)
            dwu_ref[...] = jnp.zeros_like(dwu_ref)
            dwd_ref[...] = jnp.zeros_like(dwd_ref)

        z, g_, up_, dy2 = z_ref[...], g_ref[0], up_ref[0], dy2_ref[...]
        sg = jax.nn.sigmoid(g_)
        act = g_ * sg
        dff = _mm_nt(dy2, wd_ref[0])
        dwd_ref[0] += _mm_tn(act * up_, dy2)
        dg = dff * up_ * sg * (1.0 + g_ * (1.0 - sg))
        dup = dff * act
        dwg_ref[0] += _mm_tn(dg, z)
        dwu_ref[0] += _mm_tn(dup, z)
        dz = _mm(dg, wg_ref[0]) + _mm(dup, wu_ref[0])

        @pl.when(j == 0)
        def _():
            acc_ref[rows, :] = dz

        @pl.when((j > 0) & (j < nj - 1))
        def _():
            acc_ref[rows, :] += dz

        @pl.when(j == nj - 1)
        def _():
            dz_ref[...] = acc_ref[rows, :] + dz

    row = pl.BlockSpec((tt, D_MODEL), lambda j, i: (i, 0))
    act = pl.BlockSpec((1, tt, FF_SHARD), lambda j, i: (j, i, 0))
    w_sh = pl.BlockSpec((1, FF_SHARD, D_MODEL), lambda j, i: (j, 0, 0))
    w_grad = jax.ShapeDtypeStruct((nj, FF_SHARD, D_MODEL), F32)
    return pl.pallas_call(
        body, name="ffn_bwd", grid=(nj, T // tt),
        in_specs=[row, act, act, row, w_sh, w_sh, w_sh],
        out_specs=[w_sh, w_sh, w_sh, pl.BlockSpec((tt, D_MODEL), lambda j, i: (jnp.where(j == nj - 1, i, 0), 0))],
        out_shape=[w_grad, w_grad, w_grad, jax.ShapeDtypeStruct((T, D_MODEL), F32)],
        scratch_shapes=[pltpu.VMEM((T, D_MODEL), F32)],
        compiler_params=_params(("arbitrary", "arbitrary"), vmem=FFN_BWD_VMEM),
    )(zb, g, up, dy2b, wg, wu, wd)


def _mid_bwd(dz, dh2, h1, y1, mixb, o_raw, oh_raw, xph, wout, w_fpre, w_post, w_mla, w_hg, swap=(), tt=512):
    T = dh2.shape[0]
    nsw = len(swap)
    n_in, n_out = 13, 10

    def body(*refs):
        (dz_ref, dh2_ref, h1_ref, y1_ref, mix_ref, o_ref, oh_ref, hg_ref, wout_ref, wfpre_ref, wpost_ref,
         wmla_ref, whg_ref) = refs[:n_in]
        (dh1_ref, dwout_ref, do_ref, doh_ref, dhg_ref, dvec_ref, dwfpre_ref, dwpost_ref, dwmla_ref,
         dwhg_ref) = refs[n_in + nsw:n_in + nsw + n_out]
        sems = refs[n_in + 2 * nsw + n_out + 1:]
        swap_copies = lambda: _pair_swap_copies(refs[n_in:n_in + nsw], refs[n_in + nsw + n_out:n_in + 2 * nsw + n_out],
                                                *sems)

        def wout_copies():
            _, _, c, _, sib, _ = _place()
            rw_ref, h = refs[n_in + 2 * nsw + n_out], D_MODEL // N_CHIPS // 2
            return [_rcopy(dwout_ref.at[pl.ds(pl.multiple_of((2 * k + 1 - c) * h, 8), h)], rw_ref.at[k], *sems, nsw + k, sib)
                    for k in range(N_CHIPS)]

        @pl.when(pl.program_id(0) == 0)
        def _():
            for r in (dwout_ref, dwfpre_ref, dwpost_ref, dwmla_ref, dwhg_ref):
                r[...] = jnp.zeros_like(r)
            for cp in (swap_copies() if nsw else ()):
                cp.start()

        dz = dz_ref[...]
        wfpre = wfpre_ref[...]
        _, h1n, r = _rms_fwd(h1_ref[...], wfpre)
        dh1_z, dwfpre = _rms_bwd(dz, h1n, r, wfpre)
        dwfpre_ref[...] += dwfpre
        dh1 = dh2_ref[...] + dh1_z
        dh1_ref[...] = dh1
        wpost = wpost_ref[...]
        _, y1n, r1 = _rms_fwd(y1_ref[...], wpost)
        dy1, dwpost = _rms_bwd(dh1, y1n, r1, wpost)
        dwpost_ref[...] += dwpost
        dmix = _mm_nt(dy1, wout_ref[...])
        dwout_ref[...] += _mm_tn(mix_ref[...], dy1)
        wmla = wmla_ref[...]
        o = o_ref[...]
        _, on, ro = _grms_fwd(o, wmla, MLA_V)
        d_o, dwmla = _grms_bwd(dmix[:, :MLA_WIDTH], on, ro, wmla, MLA_V)
        dwmla_ref[...] += dwmla
        do_ref[...] = d_o.astype(do_ref.dtype)
        hh = lax.broadcasted_iota(jnp.int32, (MLA_HEADS, MLA_WIDTH), 0)
        ll = lax.broadcasted_iota(jnp.int32, (MLA_HEADS, MLA_WIDTH), 1)
        sel = jnp.where((ll >= hh * MLA_V) & (ll < (hh + 1) * MLA_V), 1.0, 0.0)
        dvec_ref[...] = _mm_nt(sel, d_o * o, True)
        whg = whg_ref[...]
        hg = hg_ref[...]
        sg = jax.nn.sigmoid(hg)
        _, ohn, rh = _grms_fwd(oh_ref[...], whg, HGRN_DIM)
        dmh = dmix[:, MLA_WIDTH:]
        dhg_ref[...] = dmh * ohn * whg * sg * (1.0 + hg * (1.0 - sg))
        d_oh, dwhg = _grms_bwd(dmh * (hg * sg), ohn, rh, whg, HGRN_DIM)
        dwhg_ref[...] += dwhg
        doh_ref[...] = d_oh

        if nsw:
            @pl.when(pl.program_id(0) == T // tt - 1)
            def _():
                for cp in wout_copies():
                    cp.start()
                for cp in swap_copies() + wout_copies():
                    cp.wait()

    row = lambda w: pl.BlockSpec((tt, w), lambda i: (i, 0))
    full = lambda a: pl.BlockSpec(a.shape, lambda i: (0,) * a.ndim)
    vec = lambda w: pl.BlockSpec((1, w), lambda i: (0, 0))
    sds = jax.ShapeDtypeStruct
    return pl.pallas_call(
        body, name="mid_bwd", grid=(T // tt,),
        in_specs=[row(D_MODEL), row(D_MODEL), row(D_MODEL), row(D_MODEL),
                  row(D_MODEL), row(MLA_WIDTH), row(HGRN_WIDTH), pl.BlockSpec((tt, HGRN_WIDTH), lambda i: (i, 3)),
                  full(wout), vec(D_MODEL), vec(D_MODEL), vec(MLA_WIDTH), vec(HGRN_WIDTH)] + [ANY] * nsw,
        out_specs=[row(D_MODEL), full(wout), row(MLA_WIDTH), row(HGRN_WIDTH), row(HGRN_WIDTH),
                   pl.BlockSpec((MLA_HEADS, tt), lambda i: (0, i)),
                   vec(D_MODEL), vec(D_MODEL), vec(MLA_WIDTH), vec(HGRN_WIDTH)] + [ANY] * (nsw + 1 if nsw else 0),
        out_shape=[sds((T, D_MODEL), F32), sds(wout.shape, F32), sds((T, MLA_WIDTH), MXU_DTYPE), sds((T, HGRN_WIDTH), F32),
                   sds((T, HGRN_WIDTH), F32), sds((MLA_HEADS, T), F32),
                   sds((1, D_MODEL), F32), sds((1, D_MODEL), F32), sds((1, MLA_WIDTH), F32), sds((1, HGRN_WIDTH), F32)]
        + (_half_stack_shapes(list(swap) + [sds((N_CHIPS, D_MODEL // N_CHIPS, D_MODEL), F32)]) if nsw else []),
        scratch_shapes=[pltpu.SemaphoreType.DMA((nsw + N_CHIPS,)), pltpu.SemaphoreType.DMA((nsw + N_CHIPS,))] if nsw else [],
        compiler_params=_params(("arbitrary",)),
    )(dz, dh2, h1, y1, mixb, o_raw, oh_raw, xph, wout, w_fpre, w_post, w_mla, w_hg, *swap)


def _in_bwd(x, dh1, cq, ckv, dq, dk, dv, dhq, dhf, dhi, dhg, rc, rs, w_pre, win, qnw, wq, kvnw, wk, wv, tt=256):
    T = x.shape[0]

    def body(x_ref, dh1_ref, cq_ref, ckv_ref, dq_ref, dk_ref, dv_ref, dhq_ref, dhf_ref, dhi_ref, dhg_ref, rc_ref, rs_ref,
             wpre_ref, win_ref, qnw_ref, wq_ref, kvnw_ref, wk_ref, wv_ref,
             dx_ref, dwin_ref, dwq_ref, dwk_ref, dwv_ref, dwpre_ref, dqnw_ref, dkvnw_ref):
        @pl.when(pl.program_id(0) == 0)
        def _():
            for r in (dwin_ref, dwq_ref, dwk_ref, dwv_ref, dwpre_ref, dqnw_ref, dkvnw_ref):
                r[...] = jnp.zeros_like(r)

        def add_win_grad(r, first):
            for arr0, n, chip, row0 in _win_grad_segments():
                if first <= arr0 and arr0 + n <= first + r.shape[0]:
                    dwin_ref[chip, row0:row0 + n, :] += r[arr0 - first:arr0 - first + n]

        lo = Q_RANK + KV_RANK + HEAD_PAD
        dxp_h = jnp.concatenate([dhq_ref[...], dhf_ref[...], dhi_ref[...], dhg_ref[...]], axis=-1)
        du = _mm(dxp_h, win_ref[lo:, :])
        wpre = wpre_ref[...]
        u, xn, rx = _rms_fwd(x_ref[...], wpre)
        add_win_grad(_mm_tn(dxp_h, u), lo)
        c, sa, sb = _rope_tables(rc_ref[...], rs_ref[...])
        lane = lax.broadcasted_iota(jnp.int32, (tt, HEAD_PAD), 1)
        dk_all = dk_ref[...]
        dq_lin = []
        dkr = jnp.zeros((tt, HEAD_PAD), F32)
        for h in range(MLA_HEADS):
            sl = slice(HEAD_PAD * h, HEAD_PAD * (h + 1))
            dq_lin.append(_rope_bwd(dq_ref[sl, :].T * ATTN_SCALE, c, sa, sb))
            dkr = dkr + dk_all[:, sl]
        dq_lin = jnp.concatenate(dq_lin, axis=-1)
        dkr = jnp.where((lane >= MLA_NOPE) & (lane < MLA_QK), _rope_bwd(dkr, c, sa, sb), 0.0)
        qnw = qnw_ref[...]
        qn, cqn, rq = _rms_fwd(cq_ref[...], qnw)
        dwq_ref[...] += _mm_tn(qn, dq_lin)
        dcq, dqnw = _rms_bwd(_mm_nt(dq_lin, wq_ref[...]), cqn, rq, qnw)
        dqnw_ref[...] += dqnw
        kvnw = kvnw_ref[...]
        kvn, ckvn, rkv = _rms_fwd(ckv_ref[...], kvnw)
        dv_ = dv_ref[...]
        dwk_ref[...] += _mm_tn(kvn, dk_all)
        dwv_ref[...] += _mm_tn(kvn, dv_)
        dckv, dkvnw = _rms_bwd(_mm_nt(dk_all, wk_ref[...]) + _mm_nt(dv_, wv_ref[...]), ckvn, rkv, kvnw)
        dkvnw_ref[...] += dkvnw
        dxp_a = jnp.concatenate([dcq, dckv, dkr], axis=-1)
        add_win_grad(_mm_tn(dxp_a, u), 0)
        dx_u, dwpre = _rms_bwd(du + _mm(dxp_a, win_ref[:lo, :]), xn, rx, wpre)
        dwpre_ref[...] += dwpre
        dx_ref[...] = dh1_ref[...] + dx_u

    row = lambda w: pl.BlockSpec((tt, w), lambda i: (i, 0))
    full = lambda a: pl.BlockSpec(a.shape, lambda i: (0,) * a.ndim)
    sds = jax.ShapeDtypeStruct
    qk_w = MLA_HEADS * HEAD_PAD
    return pl.pallas_call(
        body, name="in_bwd", grid=(T // tt,),
        in_specs=[row(D_MODEL), row(D_MODEL), row(Q_RANK), row(KV_RANK), pl.BlockSpec((qk_w, tt), lambda i: (0, i)),
                  row(qk_w), row(MLA_WIDTH),
                  row(HGRN_WIDTH), row(HGRN_WIDTH), row(HGRN_WIDTH), row(HGRN_WIDTH), row(HEAD_PAD), row(HEAD_PAD),
                  full(w_pre), full(win), full(qnw), full(wq), full(kvnw), full(wk), full(wv)],
        out_specs=[row(D_MODEL), pl.BlockSpec(WIN_COMM_SHAPE, lambda i: (0, 0, 0)), full(wq), full(wk), full(wv),
                   full(w_pre), full(qnw), full(kvnw)],
        out_shape=[sds((T, D_MODEL), F32), sds(WIN_COMM_SHAPE, F32), sds(wq.shape, F32), sds(wk.shape, F32),
                   sds(wv.shape, F32), sds(w_pre.shape, F32), sds(qnw.shape, F32), sds(kvnw.shape, F32)],
        compiler_params=_params(("arbitrary",)),
    )(x, dh1, cq, ckv, dq, dk, dv, dhq, dhf, dhi, dhg, rc, rs, w_pre, win, qnw, wq, kvnw, wk, wv)


def _arrange_weights(win_t, wuq_full, wukv):
    dt = win_t.dtype
    z = lambda n: jnp.zeros((n, D_MODEL), dt)
    s2 = Q_RANK + KV_RANK
    win_arr = jnp.concatenate([win_t[:s2], z(MLA_NOPE), win_t[s2:s2 + MLA_ROPE], z(HEAD_PAD - MLA_QK),
                               win_t[s2 + MLA_ROPE:]], axis=0)
    wq_arr = jnp.pad(wuq_full, ((0, 0), (0, 0), (0, HEAD_PAD - MLA_QK))).reshape(Q_RANK, MLA_HEADS * HEAD_PAD)
    wk_arr = jnp.pad(wukv[:, :, :MLA_NOPE], ((0, 0), (0, 0), (0, HEAD_PAD - MLA_NOPE))).reshape(
        KV_RANK, MLA_HEADS * HEAD_PAD)
    wv_arr = wukv[:, :, MLA_NOPE:].reshape(KV_RANK, MLA_WIDTH)
    return win_arr, wq_arr, wk_arr, wv_arr


WIN_COMM_SHAPE = (N_CHIPS, -(-D_IN // N_CHIPS // 32) * 32, D_MODEL)


def _win_grad_segments():
    s2 = Q_RANK + KV_RANK
    runs = [(0, s2, 0), (s2, s2 + MLA_ROPE, MLA_NOPE), (s2 + MLA_ROPE, D_IN, HEAD_PAD - MLA_ROPE)]
    per = D_IN // N_CHIPS
    segs = []
    for lo, hi, shift in runs:
        for k in range(N_CHIPS):
            a, b = max(lo, per * k), min(hi, per * (k + 1))
            if a < b:
                segs.append((a + shift, b - a, k, a - per * k))
    return segs


def _unarrange_grads(dwq_arr, dwk_arr, dwv_arr):
    dwuq = dwq_arr.reshape(Q_RANK, MLA_HEADS, HEAD_PAD)[:, :, :MLA_QK]
    dwukv = jnp.concatenate([dwk_arr.reshape(KV_RANK, MLA_HEADS, HEAD_PAD)[:, :, :MLA_NOPE],
                             dwv_arr.reshape(KV_RANK, MLA_HEADS, MLA_V)], axis=-1)
    return dwuq, dwukv


def _rope_inv_freq():
    inv = 1.0 / (ROPE_THETA ** (jnp.arange(0, MLA_ROPE, 2, dtype=F32) / MLA_ROPE))
    z = lambda n: jnp.zeros((n,), F32)
    return jnp.concatenate([z(MLA_NOPE), inv, inv, z(HEAD_PAD - MLA_QK)]).reshape(1, HEAD_PAD)


def _local_step(x, pos, tgt, small, win_arr, wq_arr, wk_arr, wv_arr, late, place=None):
    invf = _rope_inv_freq()
    cq, ckv, xph, qb, kb, vb, kt, vt, rc, rs = _in_fwd(x, pos, invf, small["attn_pre_norm"], win_arr, small["mla_q_norm"],
                                               wq_arr, small["mla_kv_norm"], wk_arr, wv_arr)
    if place is None:
        o_raw, lse = _attn_fwd_t(qb, kb, vt)
        wout, wg, wu, wd = late
    else:
        o_raw, lse, *stacks = _attn_fwd_t(qb, kb, vt, gather=late)
        wout, wg, wu, wd = [lax.dynamic_update_slice(s, l[None], (place[1], 0, 0)) for s, l in zip(stacks, late)]
        wout = wout.reshape(D_MODEL, D_MODEL)
    oh_raw, states = _hgrn_fwd(xph, small["hgrn_lb_logits"])
    h1, y1, zb, mixb = _proj_fwd(x, o_raw, oh_raw, xph, wout, small["mla_out_norm"], small["hgrn_out_norm"],
                                 small["attn_post_norm"], small["ffn_pre_norm"])
    g, up, dy2b, dh2, loss_acc, d_fpost = _ffn_fwd(zb, h1, tgt, small["ffn_post_norm"], wg, wu, wd)
    dwg, dwu, dwd, dz = _ffn_bwd(zb, g, up, dy2b, wg, wu, wd)
    ffn_grads = [] if place is None else [dwg, dwu, dwd]
    dh1, dwout, d_o, d_oh, dhg, dvec, d_fpre, d_post, d_mla, d_hg, *ffn_rs = _mid_bwd(
        dz, dh2, h1, y1, mixb, o_raw, oh_raw, xph, wout, small["ffn_pre_norm"], small["attn_post_norm"],
        small["mla_out_norm"], small["hgrn_out_norm"], swap=ffn_grads)
    if ffn_grads:
        ffn_grads = ffn_grads + [dwout.reshape(N_CHIPS, D_MODEL // N_CHIPS, D_MODEL)]
    ffn_ps = _pair_sum(place, ffn_grads, ffn_rs, name="pair_sum_ffn") if ffn_grads else []
    dq, dk, dv, *ffn_ris = _attn_bwd_t(qb, kb, kt, vb, d_o, lse, dvec.reshape(lse.shape), send=ffn_ps)
    ffn_sums = _chip_sum(place, ffn_grads, ffn_rs, ffn_ris, name="chip_sum_ffn") if ffn_grads else []
    dhq, dhf, dhi, d_lbl, *ffn_final = _hgrn_bwd(xph, small["hgrn_lb_logits"], states, d_oh, fill=ffn_sums)
    dx, dwin4, dwq_arr, dwk_arr, dwv_arr, d_pre, d_qn, d_kvn = _in_bwd(
        x, dh1, cq, ckv, dq, dk, dv, dhq, dhf, dhi, dhg, rc, rs, small["attn_pre_norm"], win_arr,
        small["mla_q_norm"], wq_arr, small["mla_kv_norm"], wk_arr, wv_arr)
    dwuq, dwukv = _unarrange_grads(dwq_arr, dwk_arr, dwv_arr)
    loss = 0.5 * jnp.sum(loss_acc) * (1.0 / D_MODEL)
    grads = dict(attn_pre_norm=d_pre, w_in=dwin4, mla_q_norm=d_qn, mla_w_uq=dwuq, mla_kv_norm=d_kvn, mla_w_ukv=dwukv,
                 mla_out_norm=d_mla, hgrn_lb_logits=d_lbl, hgrn_out_norm=d_hg, w_out=dwout, attn_post_norm=d_post,
                 ffn_pre_norm=d_fpre, w_gate=dwg, w_up=dwu, w_down=dwd, ffn_post_norm=d_fpost)
    if place is None:
        return loss, dx, grads
    return loss, dx, grads, ffn_final


def _place():
    x, y, c = lax.axis_index("x"), lax.axis_index("y"), lax.axis_index("c")
    others = [(1 - x, y), (x, 1 - y), (1 - x, 1 - y)]
    return x, y, c, 2 * x + y, (x, y, 1 - c), others


def _half(ref, c, rows):
    return ref.at[pl.ds(pl.multiple_of(c * rows, 8), rows)]


def _rcopy(src, dst, send, recv, k, to):
    return pltpu.make_async_remote_copy(src_ref=src, dst_ref=dst, send_sem=send.at[k], recv_sem=recv.at[k],
                                        device_id=to, device_id_type=MESH)


class _Gather:
    def __init__(self, ins, outs, send, recv):
        self.ins, self.outs, self.send, self.recv = ins, outs, send, recv
        self.n = len(ins)
        self.halves = [r.shape[0] // 2 for r in ins]
        _, _, self.c, self.me, self.sib, self.others = _place()

    def _each(self):
        for j, (px, py) in enumerate(self.others):
            for a in range(self.n):
                yield j * self.n + a, a, 2 * px + py, (px, py, self.c)

    def sends(self):
        return [_rcopy(_half(self.ins[a], self.c, self.halves[a]), _half(self.outs[a].at[self.me], self.c, self.halves[a]),
                       self.send, self.recv, k, to) for k, a, _, to in self._each()]

    def arrivals(self):
        parts = [(k, _half(self.outs[a].at[chip], self.c, self.halves[a]), to) for k, a, chip, to in self._each()]
        return [_rcopy(p, p, self.send, self.recv, k, to) for k, p, to in parts]

    def forwards(self):
        parts = [(k, _half(self.outs[a].at[chip], self.c, self.halves[a])) for k, a, chip, _ in self._each()]
        return [_rcopy(p, p, self.send, self.recv, 3 * self.n + k, self.sib) for k, p in parts]

    def forward_arrivals(self):
        parts = [(k, _half(self.outs[a].at[chip], 1 - self.c, self.halves[a])) for k, a, chip, _ in self._each()]
        return [_rcopy(p, p, self.send, self.recv, 3 * self.n + k, self.sib) for k, p in parts]

    @staticmethod
    def out_shapes(arrs):
        return [jax.ShapeDtypeStruct((N_CHIPS,) + a.shape, a.dtype) for a in arrs]

    @staticmethod
    def semaphores(arrs):
        return [pltpu.SemaphoreType.DMA((6 * len(arrs),)), pltpu.SemaphoreType.DMA((6 * len(arrs),))]


def _gather_chips(arrs, name):
    n = len(arrs)

    def body(*refs):
        gat = _Gather(refs[:n], refs[n:2 * n], *refs[2 * n:])
        sends, forwards = gat.sends(), gat.forwards()
        for cp in sends:
            cp.start()
        for arrival, fw in zip(gat.arrivals(), forwards):
            arrival.wait_recv()
            fw.start()
        for arrival in gat.forward_arrivals():
            arrival.wait_recv()
        for cp in sends + forwards:
            cp.wait_send()

    return pl.pallas_call(body, name=name, in_specs=[ANY] * n, out_specs=[ANY] * n, out_shape=_Gather.out_shapes(arrs),
                          scratch_shapes=_Gather.semaphores(arrs))(*arrs)


def _pair_swap_copies(g_refs, r_refs, send, recv):
    _, _, c, _, sib, _ = _place()
    copies = []
    for a, (g, r) in enumerate(zip(g_refs, r_refs)):
        h = g.shape[1] // 2
        copies.append(_rcopy(g.at[:, pl.ds(pl.multiple_of((1 - c) * h, 8), h)], r, send, recv, a, sib))
    return copies


def _half_stack_shapes(gs, dtype=None):
    return [jax.ShapeDtypeStruct((N_CHIPS, g.shape[1] // 2, g.shape[2]), dtype or g.dtype) for g in gs]


def _pair_swap(gs, wholes):
    n, nw = len(gs), len(wholes)

    def body(*refs):
        ins, outs, (send, recv) = refs[:n + nw], refs[n + nw:2 * (n + nw)], refs[2 * (n + nw):]
        copies = _pair_swap_copies(ins[:n], outs[:n], send, recv)
        copies += [_rcopy(ins[n + k], outs[n + k], send, recv, n + k, _place()[4]) for k in range(nw)]
        for cp in copies:
            cp.start()
        for cp in copies:
            cp.wait()

    return pl.pallas_call(
        body, name="pair_swap", in_specs=[ANY] * (n + nw), out_specs=[ANY] * (n + nw),
        out_shape=_half_stack_shapes(gs) + [jax.ShapeDtypeStruct(w.shape, w.dtype) for w in wholes],
        scratch_shapes=[pltpu.SemaphoreType.DMA((n + nw,)), pltpu.SemaphoreType.DMA((n + nw,))],
    )(*gs, *wholes)


def _pair_sum(place, gs, rs, small=None, name="pair_sum"):
    n = len(gs)
    nb = 1

    def body(place_ref, *refs):
        g_refs, r_refs, p_refs = refs[:n], refs[n:2 * n], refs[-n - 1:-1] if small else refs[-n:]
        for a in range(n):
            p_refs[a][0] = (g_refs[a][0] + r_refs[a][0]).astype(p_refs[a].dtype)
        if small:
            @pl.when((pl.program_id(0) == 0) & (pl.program_id(1) == 0))
            def _():
                refs[-1][...] = refs[2 * n][...] + refs[2 * n + 1][...]

    in_specs, out_specs = [], []
    for g in gs:
        blk = (1, g.shape[1] // 2 // nb, g.shape[2])
        in_specs.append(pl.BlockSpec(blk, lambda i, k, p: (k, p[0] * nb + i, 0)))
    for g in gs:
        blk = (1, g.shape[1] // 2 // nb, g.shape[2])
        in_specs.append(pl.BlockSpec(blk, lambda i, k, p: (k, i, 0)))
        out_specs.append(pl.BlockSpec(blk, lambda i, k, p: (k, i, 0)))
    out_shape = _half_stack_shapes(gs, BF16)
    if small:
        sm_spec = pl.BlockSpec(small[0].shape, lambda i, k, p: (0, 0))
        in_specs += [sm_spec, sm_spec]
        out_specs.append(sm_spec)
        out_shape.append(jax.ShapeDtypeStruct(small[0].shape, F32))
    return pl.pallas_call(
        body, name=name,
        grid_spec=pltpu.PrefetchScalarGridSpec(num_scalar_prefetch=1, grid=(nb, N_CHIPS), in_specs=in_specs,
                                               out_specs=out_specs),
        out_shape=out_shape,
        compiler_params=_params(("arbitrary", "arbitrary")),
    )(place, *gs, *rs, *(small or ()))


def _chip_swap_copies(p_refs, ri_refs, send, recv):
    _, _, c, _, _, others = _place()
    n = len(p_refs)
    return [_rcopy(p_refs[a].at[2 * px + py], ri_refs[a].at[j], send, recv, j * n + a, (px, py, c))
            for j, (px, py) in enumerate(others) for a in range(n)]


def _chip_swap_shapes(ps):
    return [jax.ShapeDtypeStruct((3,) + p.shape[1:], p.dtype) for p in ps]


def _chip_swap(ps, pair):
    n = len(ps)

    def body(*refs):
        start, finish = _chip_swap_plan(refs[:n], refs[n], refs[n + 1:2 * n + 1], refs[2 * n + 1], *refs[2 * n + 2:])
        start()
        finish()

    return pl.pallas_call(
        body, name="chip_swap", in_specs=[ANY] * (n + 1), out_specs=[ANY] * (n + 1),
        out_shape=_chip_swap_out_shapes(ps, pair), scratch_shapes=_chip_swap_semaphores(n),
    )(*ps, pair)


def _chip_swap_plan(p_refs, pair_ref, ri_refs, sm4_ref, send, recv, lsem):
    n = len(p_refs)
    hs = SMALL_ROWS // 2
    x, y, c, me, sib, others = _place()
    local = pltpu.make_async_copy(pair_ref, sm4_ref.at[me], lsem.at[0])
    copies = _chip_swap_copies(p_refs, ri_refs, send, recv)
    arrivals = list(copies)
    for j, (px, py) in enumerate(others):
        copies.append(_rcopy(_half(pair_ref, c, hs), _half(sm4_ref.at[me], c, hs), send, recv, 3 * n + j, (px, py, c)))
        part = _half(sm4_ref.at[2 * px + py], c, hs)
        arrivals.append(_rcopy(part, part, send, recv, 3 * n + j, (px, py, c)))

    def start():
        local.start()
        for cp in copies:
            cp.start()

    def finish():
        for arrival in arrivals:
            arrival.wait_recv()
        for cp in copies:
            cp.wait_send()
        local.wait()

    return start, finish


def _chip_swap_out_shapes(ps, pair):
    return _chip_swap_shapes(ps) + [jax.ShapeDtypeStruct((N_CHIPS,) + pair.shape, pair.dtype)]


def _chip_swap_semaphores(n):
    k = 3 * (n + 1)
    return [pltpu.SemaphoreType.DMA((k,)), pltpu.SemaphoreType.DMA((k,)), pltpu.SemaphoreType.DMA((1,))]


def _chip_sum(place, gs, rs, ris, name="chip_sum"):
    n = len(gs)
    nb = 1

    def body(place_ref, *refs):
        g_refs, r_refs, ri_refs, o_refs = refs[:n], refs[n:2 * n], refs[2 * n:3 * n], refs[3 * n:]
        for a in range(n):
            ri = ri_refs[a]
            o_refs[a][...] = (g_refs[a][0] + r_refs[a][0]) + ri[0].astype(F32) + ri[1].astype(F32) + ri[2].astype(F32)

    in_specs, out_specs, out_shape = [], [], []
    for g in gs:
        blk = (1, g.shape[1] // 2 // nb, g.shape[2])
        in_specs.append(pl.BlockSpec(blk, lambda i, p: (p[1], p[0] * nb + i, 0)))
    for g in gs:
        blk = (1, g.shape[1] // 2 // nb, g.shape[2])
        in_specs.append(pl.BlockSpec(blk, lambda i, p: (p[1], i, 0)))
    for g in gs:
        rb = g.shape[1] // 2 // nb
        in_specs.append(pl.BlockSpec((3, rb, g.shape[2]), lambda i, p: (0, i, 0)))
        out_specs.append(pl.BlockSpec((rb, g.shape[2]), lambda i, p: (p[0] * nb + i, 0)))
        out_shape.append(jax.ShapeDtypeStruct(g.shape[1:], F32))
    return pl.pallas_call(
        body, name=name,
        grid_spec=pltpu.PrefetchScalarGridSpec(num_scalar_prefetch=1, grid=(nb,), in_specs=in_specs, out_specs=out_specs),
        out_shape=out_shape,
        compiler_params=_params(("arbitrary",)),
    )(place, *gs, *rs, *ris)


def _pair_fill_copies(g_refs, send, recv):
    _, _, c, _, sib, _ = _place()
    copies, waits = [], []
    for a, g in enumerate(g_refs):
        h = g.shape[0] // 2
        mine, theirs = _half(g, c, h), _half(g, 1 - c, h)
        copies.append(_rcopy(mine, mine, send, recv, a, sib))
        waits.append(_rcopy(theirs, theirs, send, recv, a, sib))
    return copies, waits


def _pair_fill(gfs, sm4):
    n = len(gfs)
    hs = SMALL_ROWS // 2

    def body(*refs):
        g_refs, sm4_ref = refs[n + 1:2 * n + 1], refs[2 * n + 1]
        send, recv = refs[2 * n + 2:]
        x, y, c, me, sib, others = _place()
        copies, waits = _pair_fill_copies(g_refs, send, recv)
        for j, (px, py) in enumerate(others):
            chip = 2 * px + py
            mine, theirs = _half(sm4_ref.at[chip], c, hs), _half(sm4_ref.at[chip], 1 - c, hs)
            copies.append(pltpu.make_async_remote_copy(src_ref=mine, dst_ref=mine, send_sem=send.at[n + j],
                                                       recv_sem=recv.at[n + j], device_id=sib, device_id_type=MESH))
            waits.append(pltpu.make_async_remote_copy(src_ref=theirs, dst_ref=theirs, send_sem=send.at[n + j],
                                                      recv_sem=recv.at[n + j], device_id=sib, device_id_type=MESH))
        for cp in copies:
            cp.start()
        for w in waits:
            w.wait_recv()
        for cp in copies:
            cp.wait_send()

    return pl.pallas_call(
        body, name="pair_fill", in_specs=[ANY] * (n + 1), out_specs=[ANY] * (n + 1),
        out_shape=[jax.ShapeDtypeStruct(g.shape, g.dtype) for g in gfs] + [jax.ShapeDtypeStruct(sm4.shape, sm4.dtype)],
        input_output_aliases={i: i for i in range(n + 1)},
        scratch_shapes=[pltpu.SemaphoreType.DMA((n + 3,)), pltpu.SemaphoreType.DMA((n + 3,))],
    )(*gfs, sm4)


def _adamw_math(w, g, m, v):
    m = ADAM_B1 * m + (1.0 - ADAM_B1) * g
    v = ADAM_B2 * v + (1.0 - ADAM_B2) * (g * g)
    m_hat = m / (1.0 - ADAM_B1 ** ADAM_STEP)
    v_hat = v / (1.0 - ADAM_B2 ** ADAM_STEP)
    return -ADAM_LR * (m_hat / (jnp.sqrt(v_hat) + ADAM_EPS) + ADAM_WD * w), m, v


def _adamw(items, steps, name):
    n = len(items)

    def body(*refs):
        for a in range(n):
            g = refs[4 * a + 1][...]
            d, mo, vo = _adamw_math(refs[4 * a][...], g, refs[4 * a + 2][...], refs[4 * a + 3][...])
            for out, val in zip(refs[4 * n + 4 * a:4 * n + 4 * a + 4], (g, d, mo, vo)):
                out[...] = val

    spec = lambda w: pl.BlockSpec((w.shape[0] // steps, w.shape[1]), lambda i: (i, 0))
    flat = pl.pallas_call(
        body, name=name, grid=(steps,), in_specs=[spec(it[0]) for it in items for _ in range(4)],
        out_specs=[spec(it[0]) for it in items for _ in range(4)],
        out_shape=[jax.ShapeDtypeStruct(it[0].shape, F32) for it in items for _ in range(4)],
        compiler_params=_params(("arbitrary",)),
    )(*[a for it in items for a in it])
    return [flat[4 * a:4 * a + 4] for a in range(n)]


def _adamw_small(sm4, wmv):
    views = SMALL_VIEWS[:-1]
    n = len(views)

    def body(sm4_ref, *refs):
        g_all = ((sm4_ref[0] + sm4_ref[1]) + sm4_ref[2]) + sm4_ref[3]
        for a, (name, rows, cols) in enumerate(views):
            row = SMALL_OFFSETS[name]
            g = g_all[row:row + rows, :cols]
            d, mo, vo = _adamw_math(refs[3 * a][...], g, refs[3 * a + 1][...], refs[3 * a + 2][...])
            for out, val in zip(refs[3 * n + 4 * a:3 * n + 4 * a + 4], (g, d, mo, vo)):
                out[...] = val
        row = SMALL_OFFSETS["loss"]
        refs[-1][...] = g_all[row:row + 1, :128]

    flat = pl.pallas_call(
        body, name="adamw_small",
        out_shape=[jax.ShapeDtypeStruct((rows, cols), F32) for _, rows, cols in views for _ in range(4)]
        + [jax.ShapeDtypeStruct((1, 128), F32)],
        compiler_params=pltpu.CompilerParams(vmem_limit_bytes=VMEM_LIMIT),
    )(sm4, *[a for t in wmv for a in t])
    return [flat[4 * a:4 * a + 4] for a in range(n)] + [flat[-1]]


SMALL_NAMES = ("attn_pre_norm", "mla_q_norm", "mla_kv_norm", "mla_w_ukv", "mla_out_norm", "hgrn_lb_logits",
               "hgrn_out_norm", "attn_post_norm", "ffn_pre_norm", "ffn_post_norm")
BIG_NAMES = ("w_in", "mla_w_uq", "w_out", "w_gate", "w_up", "w_down")
WEIGHT_NAMES = ("attn_pre_norm", "w_in", "mla_q_norm", "mla_w_uq", "mla_kv_norm", "mla_w_ukv", "mla_out_norm",
                "hgrn_lb_logits", "hgrn_out_norm", "w_out", "attn_post_norm", "ffn_pre_norm", "w_gate", "w_up", "w_down",
                "ffn_post_norm")


UQ_COMM_SHAPE = (192, 384)


def _pack_small(vals):
    parts, row = [], 0
    for name, rows, cols in sorted(SMALL_VIEWS, key=lambda view: SMALL_OFFSETS[view[0]]):
        assert SMALL_OFFSETS[name] == row
        parts.append(jnp.pad(vals[name].reshape(rows, cols), ((0, 0), (0, D_MODEL - cols))))
        row += rows
    parts.append(jnp.zeros((SMALL_ROWS - row, D_MODEL), F32))
    return jnp.concatenate(parts, axis=0)


def kernel(x, positions, attn_pre_norm, w_in, mla_q_norm, mla_w_uq, mla_kv_norm, mla_w_ukv, mla_out_norm, hgrn_lb_logits, hgrn_out_norm, w_out, attn_post_norm, ffn_pre_norm, w_gate, w_up, w_down, ffn_post_norm, loss_target, m_attn_pre_norm, m_w_in, m_mla_q_norm, m_mla_w_uq, m_mla_kv_norm, m_mla_w_ukv, m_mla_out_norm, m_hgrn_lb_logits, m_hgrn_out_norm, m_w_out, m_attn_post_norm, m_ffn_pre_norm, m_w_gate, m_w_up, m_w_down, m_ffn_post_norm, v_attn_pre_norm, v_w_in, v_mla_q_norm, v_mla_w_uq, v_mla_kv_norm, v_mla_w_ukv, v_mla_out_norm, v_hgrn_lb_logits, v_hgrn_out_norm, v_w_out, v_attn_post_norm, v_ffn_pre_norm, v_w_gate, v_w_up, v_w_down, v_ffn_post_norm):
    args = locals()
    W = {n: args[n] for n in WEIGHT_NAMES}
    M = {n: args["m_" + n] for n in WEIGHT_NAMES}
    V = {n: args["v_" + n] for n in WEIGHT_NAMES}
    T = x.shape[1]
    cx, cy, cc = lax.axis_index("x"), lax.axis_index("y"), lax.axis_index("c")

    win_rows = D_IN // N_CHIPS
    shard2d = {"w_in": (win_rows, D_MODEL), "mla_w_uq": (Q_RANK // N_CHIPS, MLA_HEADS * MLA_QK),
               "w_out": (D_MODEL // N_CHIPS, D_MODEL), "w_gate": (FF_SHARD, D_MODEL), "w_up": (FF_SHARD, D_MODEL),
               "w_down": (FF_SHARD, D_MODEL)}
    transposed = ("w_in", "w_gate", "w_up")
    to2d = lambda n, a: a[0].T if n in transposed else a.reshape(shard2d[n])
    from2d = lambda n, t: t.T[None] if n in transposed else t.reshape(W[n].shape)
    me = 2 * cx + cy
    place = jnp.stack([cc, me]).astype(jnp.int32)
    local_b = [to2d(n, W[n]).astype(BF16) for n in BIG_NAMES]
    local_b[0] = jnp.pad(local_b[0], ((0, WIN_COMM_SHAPE[1] - win_rows), (0, 0)))
    stacks = _gather_chips(local_b[:2], "gather_weights")
    win4, wuq4 = [lax.dynamic_update_slice(s, l[None], (me, 0, 0)) for s, l in zip(stacks, local_b)]
    win_t = win4[:, :win_rows].reshape(D_IN, D_MODEL)
    wuq_full = wuq4.reshape(Q_RANK, MLA_HEADS, MLA_QK)
    win_arr, wq_arr, wk_arr, wv_arr = _arrange_weights(win_t, wuq_full, mla_w_ukv[0].astype(BF16))
    small = {n: W[n][0] if n == "mla_w_ukv" else W[n].reshape(-1, W[n].shape[-1]) for n in SMALL_NAMES}

    loss_local, dx, grads, ffn_final = _local_step(x[0], positions.reshape(T, 1), loss_target[0], small, win_arr,
                                                           wq_arr, wk_arr, wv_arr, local_b[2:], place)

    gs = [grads["w_in"], grads["mla_w_uq"].reshape((N_CHIPS,) + UQ_COMM_SHAPE)]
    sm = _pack_small({**grads, "loss": loss_local})
    *rs, ssib = _pair_swap(gs, (sm,))
    *ps, pair = _pair_sum(place, gs, rs, small=(sm, ssib))
    ffn_names, rest_names = BIG_NAMES[3:], BIG_NAMES[:2]
    g2d = dict(zip(ffn_names + BIG_NAMES[2:3], ffn_final))
    adam_in = lambda names_: [(to2d(n, W[n]), g2d[n], to2d(n, M[n]), to2d(n, V[n])) for n in names_]
    early_names = ffn_names + BIG_NAMES[2:3]
    updates = dict(zip(early_names, _adamw(adam_in(early_names), 4, "adamw_ffn")))
    *ris, sm4 = _chip_swap(ps, pair)
    *gfin, smf = _pair_fill(_chip_sum(place, gs, rs, ris), sm4)

    g2d.update({n: gfin[k].reshape((-1,) + shard2d[n][1:]) for k, n in enumerate(rest_names)})
    updates.update(zip(rest_names, _adamw(adam_in(rest_names), 1, "adamw_w_in")))
    G, DW, NM, NV = {}, {}, {}, {}
    for n, outs in updates.items():
        G[n], DW[n], NM[n], NV[n] = (from2d(n, t) for t in outs)
    view2d = lambda n, a: a.reshape(next((r, c) for name, r, c in SMALL_VIEWS if name == n))
    *res, loss_row = _adamw_small(smf, [tuple(view2d(n, t[n]) for t in (W, M, V)) for n in SMALL_NAMES])
    for n, outs in zip(SMALL_NAMES, res):
        G[n], DW[n], NM[n], NV[n] = (t.reshape(W[n].shape) for t in outs)
    loss = loss_row[0, 0]
    return (loss, dx[None], *[G[n] for n in WEIGHT_NAMES], *[DW[n] for n in WEIGHT_NAMES],
            *[NM[n] for n in WEIGHT_NAMES], *[NV[n] for n in WEIGHT_NAMES])
```

```python
import jax
import jax.numpy as jnp
from jax import lax
from jax.experimental import pallas as pl
from jax.experimental.pallas import tpu as pltpu

F32 = jnp.float32
BF16 = jnp.bfloat16
MXU_DTYPE = BF16

D_MODEL = 1024
MLA_HEADS = 8
MLA_NOPE = 64
MLA_ROPE = 32
MLA_V = 64
MLA_QK = MLA_NOPE + MLA_ROPE
Q_RANK = 384
KV_RANK = 128
MLA_WIDTH = MLA_HEADS * MLA_V
HEAD_PAD = 128
HGRN_HEADS = 4
HGRN_DIM = 128
HGRN_WIDTH = HGRN_HEADS * HGRN_DIM
CHUNK = 64
SUB = 16
HGRN_CPI = 4
D_IN = Q_RANK + KV_RANK + MLA_ROPE + 4 * HGRN_WIDTH
D_IN_ARR = Q_RANK + KV_RANK + HEAD_PAD + 4 * HGRN_WIDTH
D_FF = 2816
N_CHIPS = 4
FF_SHARD = D_FF // N_CHIPS
EPS = 1e-6
ROPE_THETA = 10000.0
ATTN_SCALE = MLA_QK ** -0.5
ATTN_SCALE_LOG2 = ATTN_SCALE * 1.4426950408889634
NEG_BIG = -1e30

ADAM_LR = 0.001
ADAM_B1 = 0.9
ADAM_B2 = 0.999
ADAM_EPS = 1e-08
ADAM_WD = 0.01
ADAM_STEP = 10

VMEM_LIMIT = 56 * 1024 * 1024
FFN_BWD_VMEM = 62 * 1024 * 1024

SMALL_VIEWS = (("attn_pre_norm", 1, 1024), ("mla_q_norm", 1, 384), ("mla_kv_norm", 1, 128), ("mla_w_ukv", 128, 1024),
               ("mla_out_norm", 1, 512), ("hgrn_lb_logits", 2, 512), ("hgrn_out_norm", 1, 512),
               ("attn_post_norm", 1, 1024), ("ffn_pre_norm", 1, 1024), ("ffn_post_norm", 1, 1024), ("loss", 1, 1))
ROW_TILE = 8


def _small_layout():
    offsets, row = {}, 0
    for whole in (True, False):
        for name, rows, _ in SMALL_VIEWS:
            if (rows % ROW_TILE == 0) == whole:
                offsets[name] = row
                row += rows
    return offsets, -(-row // (2 * ROW_TILE)) * 2 * ROW_TILE


SMALL_OFFSETS, SMALL_ROWS = _small_layout()

MESH = pl.DeviceIdType.MESH
ANY = pl.BlockSpec(memory_space=pl.ANY)


def _dot(a, b, dims, exact):
    if exact:
        return lax.dot_general(a.astype(F32), b.astype(F32), (dims, ((), ())), precision=lax.Precision.HIGH,
                               preferred_element_type=F32)
    return lax.dot_general(a.astype(MXU_DTYPE), b.astype(MXU_DTYPE), (dims, ((), ())), preferred_element_type=F32)


def _mm(a, b, exact=False):
    return _dot(a, b, ((1,), (0,)), exact)


def _mm_nt(a, b, exact=False):
    return _dot(a, b, ((1,), (1,)), exact)


def _mm_tn(a, b, exact=False):
    return _dot(a, b, ((0,), (0,)), exact)


def _rms_fwd(x, w):
    r = lax.rsqrt(jnp.mean(x * x, axis=-1, keepdims=True) + EPS)
    xn = x * r
    return xn * w, xn, r


def _rms_bwd(dy, xn, r, w):
    dxn = dy * w
    dx = r * (dxn - xn * jnp.mean(dxn * xn, axis=-1, keepdims=True))
    dw = jnp.sum(dy * xn, axis=0, keepdims=True)
    return dx, dw


def _group_sums(v, gs):
    t, n = v.shape
    lane = lax.broadcasted_iota(jnp.int32, (t, 128), 1)
    out = []
    for p in range(n // 128):
        vb = v[:, 128 * p:128 * (p + 1)]
        if gs == 128:
            out.append(jnp.sum(vb, axis=-1, keepdims=True))
        else:
            out.append(jnp.sum(jnp.where(lane < 64, vb, 0.0), axis=-1, keepdims=True))
            out.append(jnp.sum(jnp.where(lane >= 64, vb, 0.0), axis=-1, keepdims=True))
    return out


def _group_bcast(sums, gs, t):
    lane = lax.broadcasted_iota(jnp.int32, (t, 128), 1)
    if gs == 128:
        return jnp.concatenate([jnp.broadcast_to(s, (t, 128)) for s in sums], axis=-1)
    return jnp.concatenate([jnp.where(lane < 64, sums[2 * p], sums[2 * p + 1]) for p in range(len(sums) // 2)],
                           axis=-1)


def _grms_fwd(x, w, gs):
    t = x.shape[0]
    r = lax.rsqrt(_group_bcast(_group_sums(x * x, gs), gs, t) * (1.0 / gs) + EPS)
    xn = x * r
    return xn * w, xn, r


def _grms_bwd(dy, xn, r, w, gs):
    t = dy.shape[0]
    dxn = dy * w
    dx = r * (dxn - xn * (_group_bcast(_group_sums(dxn * xn, gs), gs, t) * (1.0 / gs)))
    dw = jnp.sum(dy * xn, axis=0, keepdims=True)
    return dx, dw


def _rope_tables(c_tab, s_tab):
    lane = lax.broadcasted_iota(jnp.int32, c_tab.shape, 1)
    first = (lane >= MLA_NOPE) & (lane < MLA_NOPE + MLA_ROPE // 2)
    second = (lane >= MLA_NOPE + MLA_ROPE // 2) & (lane < MLA_QK)
    return c_tab, jnp.where(first, -s_tab, 0.0), jnp.where(second, s_tab, 0.0)


def _rope(v, c, sa, sb):
    return v * c + pltpu.roll(v, HEAD_PAD - MLA_ROPE // 2, 1) * sa + pltpu.roll(v, MLA_ROPE // 2, 1) * sb


def _rope_bwd(d, c, sa, sb):
    return d * c - pltpu.roll(d, HEAD_PAD - MLA_ROPE // 2, 1) * sa - pltpu.roll(d, MLA_ROPE // 2, 1) * sb


def _params(sem, vmem=VMEM_LIMIT):
    return pltpu.CompilerParams(dimension_semantics=sem, vmem_limit_bytes=vmem)


def _in_fwd(x, pos, invf, w_pre, win, qnw, wq, kvnw, wk, wv, tt=512):
    T = x.shape[0]

    def body(x_ref, pos_ref, invf_ref, wpre_ref, win_ref, qnw_ref, wq_ref, kvnw_ref, wk_ref, wv_ref,
             cq_ref, ckv_ref, xph_ref, q_ref, k_ref, v_ref, kt_ref, vt_ref, rc_ref, rs_ref):
        u, _, _ = _rms_fwd(x_ref[...], wpre_ref[...])
        lo = Q_RANK + KV_RANK + HEAD_PAD
        xp = _mm_nt(u, win_ref[:lo, :])
        xph_ref[...] = _mm_nt(u, win_ref[lo:, :])
        cq = xp[:, :Q_RANK]
        ckv = xp[:, Q_RANK:Q_RANK + KV_RANK]
        kr = xp[:, Q_RANK + KV_RANK:]
        cq_ref[...] = cq
        ckv_ref[...] = ckv
        ang = pos_ref[...].astype(F32) * invf_ref[...]
        c_tab = jnp.cos(ang)
        s_tab = jnp.sin(ang)
        rc_ref[...] = c_tab
        rs_ref[...] = s_tab
        c, sa, sb = _rope_tables(c_tab, s_tab)
        qn, _, _ = _rms_fwd(cq, qnw_ref[...])
        q = _mm(qn, wq_ref[...])
        kvn, _, _ = _rms_fwd(ckv, kvnw_ref[...])
        kn = _mm(kvn, wk_ref[...])
        v = _mm(kvn, wv_ref[...])
        v_ref[...] = v.astype(v_ref.dtype)
        vt_ref[...] = v.T.astype(vt_ref.dtype)
        krr = _rope(kr, c, sa, sb)
        for h in range(MLA_HEADS):
            sl = slice(HEAD_PAD * h, HEAD_PAD * (h + 1))
            q_ref[:, sl] = (_rope(q[:, sl], c, sa, sb) * ATTN_SCALE_LOG2).astype(q_ref.dtype)
            kh = kn[:, sl] + krr
            k_ref[:, sl] = kh.astype(k_ref.dtype)
            kt_ref[sl, :] = kh.T.astype(kt_ref.dtype)

    row = lambda w: pl.BlockSpec((tt, w), lambda i: (i, 0))
    full = lambda a: pl.BlockSpec(a.shape, lambda i: (0,) * a.ndim)
    qk_w = MLA_HEADS * HEAD_PAD
    return pl.pallas_call(
        body, name="in_fwd", grid=(T // tt,),
        in_specs=[row(D_MODEL), row(1), full(invf), full(w_pre), full(win), full(qnw), full(wq), full(kvnw),
                  full(wk), full(wv)],
        out_specs=[row(Q_RANK), row(KV_RANK), row(4 * HGRN_WIDTH), row(qk_w), row(qk_w), row(MLA_WIDTH),
                   pl.BlockSpec((qk_w, tt), lambda i: (0, i)), pl.BlockSpec((MLA_WIDTH, tt), lambda i: (0, i)),
                   row(HEAD_PAD), row(HEAD_PAD)],
        out_shape=[jax.ShapeDtypeStruct((T, Q_RANK), F32), jax.ShapeDtypeStruct((T, KV_RANK), F32),
                   jax.ShapeDtypeStruct((T, 4 * HGRN_WIDTH), F32), jax.ShapeDtypeStruct((T, qk_w), MXU_DTYPE),
                   jax.ShapeDtypeStruct((T, qk_w), MXU_DTYPE), jax.ShapeDtypeStruct((T, MLA_WIDTH), MXU_DTYPE),
                   jax.ShapeDtypeStruct((qk_w, T), MXU_DTYPE), jax.ShapeDtypeStruct((MLA_WIDTH, T), MXU_DTYPE),
                   jax.ShapeDtypeStruct((T, HEAD_PAD), F32), jax.ShapeDtypeStruct((T, HEAD_PAD), F32)],
        compiler_params=_params(("arbitrary",)),
    )(x, pos, invf, w_pre, win, qnw, wq, kvnw, wk, wv)


def _attn_fwd_t(qb, kb, vt, gather=(), tq=256, hps=8):
    T = qb.shape[0]
    nq = T // tq
    ng = len(gather)
    steps = (MLA_HEADS // hps) * nq
    pass_on = steps - 3

    def body(q_ref, k_ref, vt_ref, *rest):
        o_ref, lse_ref = rest[ng:ng + 2]
        acc_scr = rest[2 * ng + 2]
        qi = pl.program_id(1)
        step_no = pl.program_id(0) * nq + qi
        if ng:
            gat = _Gather(rest[:ng], rest[ng + 2:2 * ng + 2], *rest[2 * ng + 3:])

            @pl.when(step_no == 0)
            def _():
                for cp in gat.sends():
                    cp.start()

            @pl.when(step_no == pass_on)
            def _():
                for arrival in gat.arrivals():
                    arrival.wait_recv()
                for cp in gat.forwards():
                    cp.start()

        heads = [slice(HEAD_PAD * a, HEAD_PAD * (a + 1)) for a in range(hps)]
        acc_scr[...] = jnp.zeros_like(acc_scr)

        def step(j, carry, masked):
            start = pl.multiple_of(j * tq, tq)
            scores = [_mm_nt(k_ref[pl.ds(start, tq), heads[a]], q_ref[:, heads[a]]) for a in range(hps)]
            new, probs, alphas = [], [], []
            for a in range(hps):
                m, l = carry[a]
                s = scores[a]
                if masked:
                    kk = lax.broadcasted_iota(jnp.int32, (tq, tq), 0)
                    qq = lax.broadcasted_iota(jnp.int32, (tq, tq), 1)
                    s = jnp.where(kk <= qq, s, NEG_BIG)
                m_new = jnp.maximum(m, jnp.max(s, axis=0, keepdims=True))
                alpha = jnp.exp2(m - m_new)
                p = jnp.exp2(s - m_new)
                l = l * alpha + jnp.sum(p, axis=0, keepdims=True)
                new.append((m_new, l))
                probs.append(p.astype(MXU_DTYPE))
                alphas.append(alpha)
                if a % 2:
                    pr = a // 2
                    vtj = vt_ref[2 * MLA_V * pr:2 * MLA_V * (pr + 1), pl.ds(start, tq)]
                    none = jnp.zeros((MLA_V, tq), vtj.dtype)
                    pv = (_mm(jnp.concatenate([vtj[:MLA_V], none], axis=0), probs[a - 1])
                          + _mm(jnp.concatenate([none, vtj[MLA_V:]], axis=0), probs[a]))
                    acc_scr[pr] = acc_scr[pr] * jnp.where(row < MLA_V, alphas[a - 1], alphas[a]) + pv
            return tuple(new)

        row = lax.broadcasted_iota(jnp.int32, (2 * MLA_V, tq), 0)
        init = tuple((jnp.full((1, tq), NEG_BIG, F32), jnp.zeros((1, tq), F32)) for _ in range(hps))
        carry = lax.fori_loop(0, qi, lambda j, c: step(j, c, False), init)
        carry = step(qi, carry, True)
        for pr in range(hps // 2):
            (m0, l0), (m1, l1) = carry[2 * pr], carry[2 * pr + 1]
            ot = acc_scr[pr] / jnp.where(row < MLA_V, l0, l1)
            o_ref[:, 2 * MLA_V * pr:2 * MLA_V * (pr + 1)] = ot.T
            lse_ref[pr, 0:1, :] = m0 + jnp.log2(l0)
            lse_ref[pr, 1:2, :] = m1 + jnp.log2(l1)

        if ng:
            @pl.when(step_no == steps - 1)
            def _():
                for arrival in gat.forward_arrivals():
                    arrival.wait_recv()
                for cp in gat.sends() + gat.forwards():
                    cp.wait_send()

    return pl.pallas_call(
        body, name="attn_fwd", grid=(MLA_HEADS // hps, nq),
        in_specs=[pl.BlockSpec((tq, hps * HEAD_PAD), lambda g, i: (i, g)),
                  pl.BlockSpec((T, hps * HEAD_PAD), lambda g, i: (0, g)),
                  pl.BlockSpec((hps * MLA_V, T), lambda g, i: (g, 0))] + [ANY] * ng,
        out_specs=[pl.BlockSpec((tq, hps * MLA_V), lambda g, i: (i, g)),
                   pl.BlockSpec((hps // 2, 2, tq), lambda g, i: (g, 0, i))] + [ANY] * ng,
        out_shape=[jax.ShapeDtypeStruct((T, MLA_WIDTH), F32), jax.ShapeDtypeStruct((MLA_HEADS // 2, 2, T), F32)]
        + _Gather.out_shapes(gather),
        scratch_shapes=[pltpu.VMEM((hps // 2, 2 * MLA_V, tq), F32)] + (_Gather.semaphores(gather) if ng else []),
        compiler_params=_params(("arbitrary", "arbitrary")),
    )(qb, kb, vt, *gather)


def _attn_bwd_t(qb, kb, kt, vb, dob, lse, dvec, send=(), tq=512, hps=4):
    T = qb.shape[0]
    nq = T // tq
    ns = len(send)
    steps = (MLA_HEADS // hps) * nq

    def body(q_ref, k_ref, kt_ref, v_ref, do_ref, lse_ref, d_ref, *rest):
        dqt_ref, dk_ref, dv_ref = rest[ns:ns + 3]
        va_scr, dv_scr = rest[2 * ns + 3:2 * ns + 5]
        j = pl.program_id(1)
        step_no = pl.program_id(0) * nq + j
        if ns:
            @pl.when(step_no == 0)
            def _():
                for cp in _chip_swap_copies(rest[:ns], rest[ns + 3:2 * ns + 3], *rest[2 * ns + 5:]):
                    cp.start()

        @pl.when(j == 0)
        def _():
            dqt_ref[...] = jnp.zeros_like(dqt_ref)

        lane = lax.broadcasted_iota(jnp.int32, (tq, 2 * MLA_V), 1)
        heads = [slice(HEAD_PAD * a, HEAD_PAD * (a + 1)) for a in range(hps)]
        pairs = [slice(2 * MLA_V * p, 2 * MLA_V * (p + 1)) for p in range(hps // 2)]
        for pr in range(hps // 2):
            vpair = v_ref[:, pairs[pr]]
            va_scr[2 * pr] = jnp.where(lane < MLA_V, vpair, jnp.zeros_like(vpair))
            va_scr[2 * pr + 1] = jnp.where(lane >= MLA_V, vpair, jnp.zeros_like(vpair))
        dk_ref[...] = jnp.zeros_like(dk_ref)
        dv_scr[...] = jnp.zeros_like(dv_scr)

        def step(i, masked):
            start = pl.multiple_of(i * tq, tq)
            rows = pl.ds(start, tq)
            scores = [_mm_nt(k_ref[:, heads[a]], q_ref[rows, heads[a]]) for a in range(hps)]
            dps = [_mm_nt(va_scr[a], do_ref[rows, pairs[a // 2]]) for a in range(hps)]
            for a in range(hps):
                pr, r = a // 2, a % 2
                p = jnp.exp2(scores[a] - lse_ref[pr, r:r + 1, rows])
                if masked:
                    kk = lax.broadcasted_iota(jnp.int32, (tq, tq), 0)
                    qq = lax.broadcasted_iota(jnp.int32, (tq, tq), 1)
                    p = jnp.where(kk <= qq, p, 0.0)
                ds = p * (dps[a] - d_ref[pr, r:r + 1, rows])
                dv_scr[a] += _mm(p, do_ref[rows, pairs[pr]])
                dk_ref[:, heads[a]] += _mm(ds, q_ref[rows, heads[a]])
                dqt_ref[heads[a], rows] += _mm(kt_ref[heads[a], :], ds)

        def loop_body(i, _):
            step(i, False)
            return 0

        step(j, True)
        lax.fori_loop(j + 1, nq, loop_body, 0)
        for pr in range(hps // 2):
            dv_ref[:, pairs[pr]] = jnp.where(lane < MLA_V, dv_scr[2 * pr], dv_scr[2 * pr + 1])
        dk_ref[...] = dk_ref[...] * (ATTN_SCALE / ATTN_SCALE_LOG2)

        if ns:
            @pl.when(step_no == steps - 1)
            def _():
                for cp in _chip_swap_copies(rest[:ns], rest[ns + 3:2 * ns + 3], *rest[2 * ns + 5:]):
                    cp.wait()

    stat = pl.BlockSpec((hps // 2, 2, T), lambda g, j: (g, 0, 0))
    return pl.pallas_call(
        body, name="attn_bwd", grid=(MLA_HEADS // hps, nq),
        in_specs=[pl.BlockSpec((T, hps * HEAD_PAD), lambda g, j: (0, g)),
                  pl.BlockSpec((tq, hps * HEAD_PAD), lambda g, j: (j, g)),
                  pl.BlockSpec((hps * HEAD_PAD, tq), lambda g, j: (g, j)),
                  pl.BlockSpec((tq, hps * MLA_V), lambda g, j: (j, g)),
                  pl.BlockSpec((T, hps * MLA_V), lambda g, j: (0, g)), stat, stat] + [ANY] * ns,
        out_specs=[pl.BlockSpec((hps * HEAD_PAD, T), lambda g, j: (g, 0)),
                   pl.BlockSpec((tq, hps * HEAD_PAD), lambda g, j: (j, g)),
                   pl.BlockSpec((tq, hps * MLA_V), lambda g, j: (j, g))] + [ANY] * ns,
        out_shape=[jax.ShapeDtypeStruct((MLA_HEADS * HEAD_PAD, T), F32),
                   jax.ShapeDtypeStruct((T, MLA_HEADS * HEAD_PAD), F32),
                   jax.ShapeDtypeStruct((T, MLA_WIDTH), F32)] + _chip_swap_shapes(send),
        scratch_shapes=[pltpu.VMEM((hps, tq, 2 * MLA_V), vb.dtype), pltpu.VMEM((hps, tq, 2 * MLA_V), F32)]
        + ([pltpu.SemaphoreType.DMA((3 * ns,)), pltpu.SemaphoreType.DMA((3 * ns,))] if ns else []),
        compiler_params=_params(("arbitrary", "arbitrary")),
    )(qb, kb, kt, vb, dob, lse, dvec, *send)


def _cumsum_rows(x):
    n = x.shape[0]
    row = lax.broadcasted_iota(jnp.int32, x.shape, 0)
    s = 1
    while s < n:
        x = x + jnp.where(row >= s, pltpu.roll(x, s, 0), 0.0)
        s *= 2
    return x


def _rev_cumsum_rows(x):
    n = x.shape[0]
    row = lax.broadcasted_iota(jnp.int32, x.shape, 0)
    s = 1
    while s < n:
        x = x + jnp.where(row < n - s, pltpu.roll(x, n - s, 0), 0.0)
        s *= 2
    return x


def _lb_from_logits(l):
    l0, l1 = l[0:1, :], l[1:2, :]
    m = jnp.maximum(l0, l1)
    e0, e1 = jnp.exp(l0 - m), jnp.exp(l1 - m)
    return e0 / (e0 + e1)


def _hgrn_gates(hq, hf, lb):
    sig_f = jax.nn.sigmoid(hf)
    f = lb + (1.0 - lb) * sig_f
    sig_q = jax.nn.sigmoid(hq)
    return sig_f, f, jnp.log(f), 1.0 - f, sig_q, hq * sig_q


def _hgrn_intra(q, kk, b, exact=False):
    row = lax.broadcasted_iota(jnp.int32, b.shape, 0)
    qs, ks, eqs, eks, a_rows = [], [], [], [], []
    for i in range(CHUNK // SUB):
        ref = b[SUB * i + SUB // 2:SUB * i + SUB // 2 + 1, :]
        eq = jnp.exp(b[SUB * i:SUB * (i + 1), :] - ref)
        ek = jnp.exp(jnp.where(row < SUB * (i + 1), ref - b, NEG_BIG))
        qi = q[SUB * i:SUB * (i + 1), :] * eq
        ki = kk * ek
        a_rows.append(_mm_nt(qi, ki, exact))
        qs.append(qi), ks.append(ki), eqs.append(eq), eks.append(ek)
    tt = lax.broadcasted_iota(jnp.int32, (CHUNK, CHUNK), 0)
    ss = lax.broadcasted_iota(jnp.int32, (CHUNK, CHUNK), 1)
    causal = ss <= tt
    a = jnp.where(causal, jnp.concatenate(a_rows, axis=0), 0.0)
    return a, causal, qs, ks, eqs, eks


def _hgrn_fwd(xph, lbl, tg=512):
    T = xph.shape[0]
    ng, ncg = T // tg, tg // CHUNK
    cols = [slice(HGRN_DIM * h, HGRN_DIM * (h + 1)) for h in range(HGRN_HEADS)]

    def body(lbl_ref, hq_ref, hf_ref, hi_ref, o_ref, st_ref, s_scr):
        @pl.when(pl.program_id(0) == 0)
        def _():
            s_scr[...] = jnp.zeros_like(s_scr)

        lb = _lb_from_logits(lbl_ref[...])

        def chunks(it, _):
            pre = []
            for k in range(HGRN_CPI):
                c = it * HGRN_CPI + k
                rows = pl.ds(pl.multiple_of(c * CHUNK, CHUNK), CHUNK)
                for cs in cols:
                    _, _, lf, kk, _, q = _hgrn_gates(hq_ref[rows, cs], hf_ref[rows, cs], lb[:, cs])
                    v = hi_ref[rows, cs]
                    b = _cumsum_rows(lf)
                    a = _hgrn_intra(q, kk, b)[0]
                    b_last = b[CHUNK - 1:CHUNK, :]
                    pre.append((c, rows, q * jnp.exp(b), a, v, jnp.exp(b_last), _mm_tn(v, kk * jnp.exp(b_last - b))))
            for i, (c, rows, qe, a, v, ebl, upd) in enumerate(pre):
                h = i % HGRN_HEADS
                st = s_scr[h]
                st_ref[h, c] = st
                o_ref[rows, cols[h]] = _mm_nt(qe, st) + _mm(a, v)
                s_scr[h] = st * ebl + upd
            return 0

        lax.fori_loop(0, ncg // HGRN_CPI, chunks, 0)

    col = lambda k: pl.BlockSpec((tg, HGRN_WIDTH), lambda g: (g, k))
    return pl.pallas_call(
        body, name="hgrn_fwd", grid=(ng,),
        in_specs=[pl.BlockSpec((2, HGRN_WIDTH), lambda g: (0, 0)), col(0), col(1), col(2)],
        out_specs=[col(0), pl.BlockSpec((HGRN_HEADS, ncg, HGRN_DIM, HGRN_DIM), lambda g: (0, g, 0, 0))],
        out_shape=[jax.ShapeDtypeStruct((T, HGRN_WIDTH), F32),
                   jax.ShapeDtypeStruct((HGRN_HEADS, T // CHUNK, HGRN_DIM, HGRN_DIM), F32)],
        scratch_shapes=[pltpu.VMEM((HGRN_HEADS, HGRN_DIM, HGRN_DIM), F32)],
        compiler_params=_params(("arbitrary",)),
    )(lbl, xph, xph, xph)


def _hgrn_bwd(xph, lbl, states, d_o, fill=(), tg=512):
    T = xph.shape[0]
    ng, ncg = T // tg, tg // CHUNK
    cols = [slice(HGRN_DIM * h, HGRN_DIM * (h + 1)) for h in range(HGRN_HEADS)]
    nsub = CHUNK // SUB
    nf = len(fill)

    def body(lbl_ref, hq_ref, hf_ref, hi_ref, st_ref, do_ref, *rest):
        dhq_ref, dhf_ref, dhi_ref, dlg_ref = rest[nf:nf + 4]
        ds_scr, dlb_scr = rest[2 * nf + 4:2 * nf + 6]
        fill_copies = lambda: _pair_fill_copies(rest[nf + 4:2 * nf + 4], *rest[2 * nf + 6:])
        g = pl.program_id(0)

        @pl.when(g == 0)
        def _():
            ds_scr[...] = jnp.zeros_like(ds_scr)
            dlb_scr[...] = jnp.zeros_like(dlb_scr)
            for cp in (fill_copies()[0] if nf else ()):
                cp.start()

        lb = _lb_from_logits(lbl_ref[...])

        def chunks(it, _):
            pre = []
            for k, h in ((k, h) for k in range(HGRN_CPI) for h in range(HGRN_HEADS)):
                cs = cols[h]
                c = ncg - 1 - (it * HGRN_CPI + k)
                rows = pl.ds(pl.multiple_of(c * CHUNK, CHUNK), CHUNK)
                hq = hq_ref[rows, cs]
                sig_f, f, lf, kk, sig_q, q = _hgrn_gates(hq, hf_ref[rows, cs], lb[:, cs])
                v = hi_ref[rows, cs]
                do = do_ref[rows, cs]
                b = _cumsum_rows(lf)
                eb = jnp.exp(b)
                a, causal, qs, ks, eqs, eks = _hgrn_intra(q, kk, b)
                b_last = b[CHUNK - 1:CHUNK, :]
                st = st_ref[h, c]
                pre.append(dict(h=h, cs=cs, rows=rows, hq=hq, sig_f=sig_f, f=f, kk=kk, sig_q=sig_q, q=q, v=v, eb=eb, qs=qs,
                                ks=ks, eqs=eqs,
                                eks=eks, ebl=jnp.exp(b_last), el=jnp.exp(b_last - b), st=st,
                                da=jnp.where(causal, _mm_nt(do, v, True), 0.0), dq=_mm(do, st, True) * eb,
                                dv=_mm_tn(a, do), dsu=_mm_tn(do, q * eb, True)))
            for w in pre:
                dq_rows = []
                dk = jnp.zeros_like(w["q"])
                for i in range(nsub):
                    dai = w["da"][SUB * i:SUB * (i + 1), :]
                    dq_rows.append(_mm(dai, w["ks"][i], True) * w["eqs"][i])
                    dk = dk + _mm_tn(dai, w["qs"][i], True) * w["eks"][i]
                w["dq"] = w["dq"] + jnp.concatenate(dq_rows, axis=0)
                w["dk"] = dk
            for w in pre:
                h, cs, rows = w["h"], w["cs"], w["rows"]
                kk, el, ebl, dst = w["kk"], w["el"], w["ebl"], ds_scr[h]
                dk_state = _mm(w["v"], dst, True) * el
                dk = w["dk"] + dk_state
                e_last = (ebl * jnp.sum(w["st"] * dst, axis=0, keepdims=True)
                          + jnp.sum(kk * dk_state, axis=0, keepdims=True))
                dlf = _rev_cumsum_rows(w["q"] * w["dq"] - kk * dk) + e_last
                ds_scr[h] = dst * ebl + w["dsu"]
                df = dlf / w["f"] - dk
                sig_f, sig_q = w["sig_f"], w["sig_q"]
                dhf_ref[rows, cs] = df * (1.0 - lb[:, cs]) * sig_f * (1.0 - sig_f)
                dlb_scr[:, cs] += jnp.sum(df * (1.0 - sig_f), axis=0, keepdims=True)
                dhq_ref[rows, cs] = w["dq"] * sig_q * (1.0 + w["hq"] * (1.0 - sig_q))
                dhi_ref[rows, cs] = w["dv"] + _mm_nt(kk * el, dst)
            return 0

        lax.fori_loop(0, ncg // HGRN_CPI, chunks, 0)

        @pl.when(g == ng - 1)
        def _():
            dl0 = dlb_scr[...] * lb * (1.0 - lb)
            dlg_ref[...] = jnp.concatenate([dl0, -dl0], axis=0)
            if nf:
                copies, waits = fill_copies()
                for w in waits:
                    w.wait_recv()
                for cp in copies:
                    cp.wait_send()

    col = lambda k: pl.BlockSpec((tg, HGRN_WIDTH), lambda g: (ng - 1 - g, k))
    logits = pl.BlockSpec((2, HGRN_WIDTH), lambda g: (0, 0))
    big = jax.ShapeDtypeStruct((T, HGRN_WIDTH), F32)
    n_in, n_out = 6, 4
    return pl.pallas_call(
        body, name="hgrn_bwd", grid=(ng,),
        in_specs=[logits, col(0), col(1), col(2),
                  pl.BlockSpec((HGRN_HEADS, ncg, HGRN_DIM, HGRN_DIM), lambda g: (0, ng - 1 - g, 0, 0)), col(0)] + [ANY] * nf,
        out_specs=[col(0), col(0), col(0), logits] + [ANY] * nf,
        out_shape=[big, big, big, jax.ShapeDtypeStruct((2, HGRN_WIDTH), F32)]
        + [jax.ShapeDtypeStruct(f.shape, f.dtype) for f in fill],
        input_output_aliases={n_in + k: n_out + k for k in range(nf)},
        scratch_shapes=[pltpu.VMEM((HGRN_HEADS, HGRN_DIM, HGRN_DIM), F32), pltpu.VMEM((1, HGRN_WIDTH), F32)]
        + ([pltpu.SemaphoreType.DMA((nf,)), pltpu.SemaphoreType.DMA((nf,))] if nf else []),
        compiler_params=_params(("arbitrary",)),
    )(lbl, xph, xph, xph, states, d_o, *fill)


def _proj_fwd(x, o_raw, oh_raw, xph, wout, w_mla, w_hg, w_post, w_fpre, tt=512):
    T = x.shape[0]

    def body(x_ref, o_ref, oh_ref, hg_ref, wout_ref, wmla_ref, whg_ref, wpost_ref, wfpre_ref,
             h1_ref, y1_ref, z_ref, mix_ref):
        om, _, _ = _grms_fwd(o_ref[...], wmla_ref[...], MLA_V)
        hg = hg_ref[...]
        ohn, _, _ = _grms_fwd(oh_ref[...], whg_ref[...], HGRN_DIM)
        mix = jnp.concatenate([om, ohn * (hg * jax.nn.sigmoid(hg))], axis=-1)
        mix_ref[...] = mix.astype(mix_ref.dtype)
        y1 = _mm(mix, wout_ref[...])
        y1_ref[...] = y1
        h1 = x_ref[...] + _rms_fwd(y1, wpost_ref[...])[0]
        h1_ref[...] = h1
        z_ref[...] = _rms_fwd(h1, wfpre_ref[...])[0].astype(z_ref.dtype)

    row = lambda w: pl.BlockSpec((tt, w), lambda i: (i, 0))
    full = lambda a: pl.BlockSpec(a.shape, lambda i: (0,) * a.ndim)
    sds = jax.ShapeDtypeStruct
    return pl.pallas_call(
        body, name="proj_fwd", grid=(T // tt,),
        in_specs=[row(D_MODEL), row(MLA_WIDTH), row(HGRN_WIDTH), pl.BlockSpec((tt, HGRN_WIDTH), lambda i: (i, 3)),
                  full(wout), full(w_mla), full(w_hg), full(w_post), full(w_fpre)],
        out_specs=[row(D_MODEL)] * 4,
        out_shape=[sds((T, D_MODEL), F32), sds((T, D_MODEL), F32), sds((T, D_MODEL), MXU_DTYPE),
                   sds((T, D_MODEL), MXU_DTYPE)],
        compiler_params=_params(("arbitrary",)),
    )(x, o_raw, oh_raw, xph, wout, w_mla, w_hg, w_post, w_fpre)


def _ffn_fwd(zb, h1, tgt, w_fpost, wg, wu, wd, tt=256):
    T = zb.shape[0]
    nj = N_CHIPS

    def body(z_ref, h1_ref, tgt_ref, wfpost_ref, wg_ref, wu_ref, wd_ref, g_ref, up_ref, dy2_ref, dh2_ref, loss_ref, dwf_ref):
        @pl.when(pl.program_id(0) == 0)
        def _():
            loss_ref[...] = jnp.zeros_like(loss_ref)
            dwf_ref[...] = jnp.zeros_like(dwf_ref)

        z = z_ref[...]
        gs = [_mm_nt(z, wg_ref[j]) for j in range(nj)]
        ups = [_mm_nt(z, wu_ref[j]) for j in range(nj)]
        y2 = jnp.zeros((tt, D_MODEL), F32)
        for j in range(nj):
            g_ref[j] = gs[j]
            up_ref[j] = ups[j]
            y2 = y2 + _mm(gs[j] * jax.nn.sigmoid(gs[j]) * ups[j], wd_ref[j])
        w = wfpost_ref[...]
        y2s, y2n, r2 = _rms_fwd(y2, w)
        e = h1_ref[...] + y2s - tgt_ref[...]
        loss_ref[...] += jnp.sum(e * e, axis=0, keepdims=True)
        dh2 = e * (1.0 / D_MODEL)
        dh2_ref[...] = dh2
        dy2, dwf = _rms_bwd(dh2, y2n, r2, w)
        dy2_ref[...] = dy2.astype(dy2_ref.dtype)
        dwf_ref[...] += dwf

    row = pl.BlockSpec((tt, D_MODEL), lambda i: (i, 0))
    vec = pl.BlockSpec((1, D_MODEL), lambda i: (0, 0))
    resident = pl.BlockSpec((nj, FF_SHARD, D_MODEL), lambda i: (0, 0, 0), pipeline_mode=pl.Buffered(1))
    act = pl.BlockSpec((nj, tt, FF_SHARD), lambda i: (0, i, 0))
    sds = jax.ShapeDtypeStruct
    return pl.pallas_call(
        body, name="ffn_fwd", grid=(T // tt,),
        in_specs=[row, row, row, vec, resident, resident, resident],
        out_specs=[act, act, row, row, vec, vec],
        out_shape=[sds((nj, T, FF_SHARD), F32), sds((nj, T, FF_SHARD), F32), sds((T, D_MODEL), MXU_DTYPE),
                   sds((T, D_MODEL), F32), sds((1, D_MODEL), F32), sds((1, D_MODEL), F32)],
        compiler_params=_params(("arbitrary",)),
    )(zb, h1, tgt, w_fpost, wg, wu, wd)


def _ffn_bwd(zb, g, up, dy2b, wg, wu, wd, tt=512):
    T = zb.shape[0]
    nj = N_CHIPS

    def body(z_ref, g_ref, up_ref, dy2_ref, wg_ref, wu_ref, wd_ref, dwg_ref, dwu_ref, dwd_ref, dz_ref, acc_ref):
        j, i = pl.program_id(0), pl.program_id(1)
        rows = pl.ds(pl.multiple_of(i * tt, tt), tt)

        @pl.when(i == 0)
        def _():
            dwg_ref[...] = jnp.zeros_like(dwg_ref)
            dwu_ref[...] = jnp.zeros_like(dwu_ref)
            dwd_ref[...] = jnp.zeros_like(dwd_ref)

        z, g_, up_, dy2 = z_ref[...], g_ref[0], up_ref[0], dy2_ref[...]
        sg = jax.nn.sigmoid(g_)
        act = g_ * sg
        dff = _mm_nt(dy2, wd_ref[0])
        dwd_ref[0] += _mm_tn(act * up_, dy2)
        dg = dff * up_ * sg * (1.0 + g_ * (1.0 - sg))
        dup = dff * act
        dwg_ref[0] += _mm_tn(dg, z)
        dwu_ref[0] += _mm_tn(dup, z)
        dz = _mm(dg, wg_ref[0]) + _mm(dup, wu_ref[0])

        @pl.when(j == 0)
        def _():
            acc_ref[rows, :] = dz

        @pl.when((j > 0) & (j < nj - 1))
        def _():
            acc_ref[rows, :] += dz

        @pl.when(j == nj - 1)
        def _():
            dz_ref[...] = acc_ref[rows, :] + dz

    row = pl.BlockSpec((tt, D_MODEL), lambda j, i: (i, 0))
    act = pl.BlockSpec((1, tt, FF_SHARD), lambda j, i: (j, i, 0))
    w_sh = pl.BlockSpec((1, FF_SHARD, D_MODEL), lambda j, i: (j, 0, 0))
    w_grad = jax.ShapeDtypeStruct((nj, FF_SHARD, D_MODEL), F32)
    return pl.pallas_call(
        body, name="ffn_bwd", grid=(nj, T // tt),
        in_specs=[row, act, act, row, w_sh, w_sh, w_sh],
        out_specs=[w_sh, w_sh, w_sh, pl.BlockSpec((tt, D_MODEL), lambda j, i: (jnp.where(j == nj - 1, i, 0), 0))],
        out_shape=[w_grad, w_grad, w_grad, jax.ShapeDtypeStruct((T, D_MODEL), F32)],
        scratch_shapes=[pltpu.VMEM((T, D_MODEL), F32)],
        compiler_params=_params(("arbitrary", "arbitrary"), vmem=FFN_BWD_VMEM),
    )(zb, g, up, dy2b, wg, wu, wd)


def _mid_bwd(dz, dh2, h1, y1, mixb, o_raw, oh_raw, xph, wout, w_fpre, w_post, w_mla, w_hg, swap=(), tt=512):
    T = dh2.shape[0]
    nsw = len(swap)
    n_in, n_out = 13, 10

    def body(*refs):
        (dz_ref, dh2_ref, h1_ref, y1_ref, mix_ref, o_ref, oh_ref, hg_ref, wout_ref, wfpre_ref, wpost_ref,
         wmla_ref, whg_ref) = refs[:n_in]
        (dh1_ref, dwout_ref, do_ref, doh_ref, dhg_ref, dvec_ref, dwfpre_ref, dwpost_ref, dwmla_ref,
         dwhg_ref) = refs[n_in + nsw:n_in + nsw + n_out]
        sems = refs[n_in + 2 * nsw + n_out + 1:]
        swap_copies = lambda: _pair_swap_copies(refs[n_in:n_in + nsw], refs[n_in + nsw + n_out:n_in + 2 * nsw + n_out],
                                                *sems)

        def wout_copies():
            _, _, c, _, sib, _ = _place()
            rw_ref, h = refs[n_in + 2 * nsw + n_out], D_MODEL // N_CHIPS // 2
            return [_rcopy(dwout_ref.at[pl.ds(pl.multiple_of((2 * k + 1 - c) * h, 8), h)], rw_ref.at[k], *sems, nsw + k, sib)
                    for k in range(N_CHIPS)]

        @pl.when(pl.program_id(0) == 0)
        def _():
            for r in (dwout_ref, dwfpre_ref, dwpost_ref, dwmla_ref, dwhg_ref):
                r[...] = jnp.zeros_like(r)
            for cp in (swap_copies() if nsw else ()):
                cp.start()

        dz = dz_ref[...]
        wfpre = wfpre_ref[...]
        _, h1n, r = _rms_fwd(h1_ref[...], wfpre)
        dh1_z, dwfpre = _rms_bwd(dz, h1n, r, wfpre)
        dwfpre_ref[...] += dwfpre
        dh1 = dh2_ref[...] + dh1_z
        dh1_ref[...] = dh1
        wpost = wpost_ref[...]
        _, y1n, r1 = _rms_fwd(y1_ref[...], wpost)
        dy1, dwpost = _rms_bwd(dh1, y1n, r1, wpost)
        dwpost_ref[...] += dwpost
        dmix = _mm_nt(dy1, wout_ref[...])
        dwout_ref[...] += _mm_tn(mix_ref[...], dy1)
        wmla = wmla_ref[...]
        o = o_ref[...]
        _, on, ro = _grms_fwd(o, wmla, MLA_V)
        d_o, dwmla = _grms_bwd(dmix[:, :MLA_WIDTH], on, ro, wmla, MLA_V)
        dwmla_ref[...] += dwmla
        do_ref[...] = d_o.astype(do_ref.dtype)
        hh = lax.broadcasted_iota(jnp.int32, (MLA_HEADS, MLA_WIDTH), 0)
        ll = lax.broadcasted_iota(jnp.int32, (MLA_HEADS, MLA_WIDTH), 1)
        sel = jnp.where((ll >= hh * MLA_V) & (ll < (hh + 1) * MLA_V), 1.0, 0.0)
        dvec_ref[...] = _mm_nt(sel, d_o * o, True)
        whg = whg_ref[...]
        hg = hg_ref[...]
        sg = jax.nn.sigmoid(hg)
        _, ohn, rh = _grms_fwd(oh_ref[...], whg, HGRN_DIM)
        dmh = dmix[:, MLA_WIDTH:]
        dhg_ref[...] = dmh * ohn * whg * sg * (1.0 + hg * (1.0 - sg))
        d_oh, dwhg = _grms_bwd(dmh * (hg * sg), ohn, rh, whg, HGRN_DIM)
        dwhg_ref[...] += dwhg
        doh_ref[...] = d_oh

        if nsw:
            @pl.when(pl.program_id(0) == T // tt - 1)
            def _():
                for cp in wout_copies():
                    cp.start()
                for cp in swap_copies() + wout_copies():
                    cp.wait()

    row = lambda w: pl.BlockSpec((tt, w), lambda i: (i, 0))
    full = lambda a: pl.BlockSpec(a.shape, lambda i: (0,) * a.ndim)
    vec = lambda w: pl.BlockSpec((1, w), lambda i: (0, 0))
    sds = jax.ShapeDtypeStruct
    return pl.pallas_call(
        body, name="mid_bwd", grid=(T // tt,),
        in_specs=[row(D_MODEL), row(D_MODEL), row(D_MODEL), row(D_MODEL),
                  row(D_MODEL), row(MLA_WIDTH), row(HGRN_WIDTH), pl.BlockSpec((tt, HGRN_WIDTH), lambda i: (i, 3)),
                  full(wout), vec(D_MODEL), vec(D_MODEL), vec(MLA_WIDTH), vec(HGRN_WIDTH)] + [ANY] * nsw,
        out_specs=[row(D_MODEL), full(wout), row(MLA_WIDTH), row(HGRN_WIDTH), row(HGRN_WIDTH),
                   pl.BlockSpec((MLA_HEADS, tt), lambda i: (0, i)),
                   vec(D_MODEL), vec(D_MODEL), vec(MLA_WIDTH), vec(HGRN_WIDTH)] + [ANY] * (nsw + 1 if nsw else 0),
        out_shape=[sds((T, D_MODEL), F32), sds(wout.shape, F32), sds((T, MLA_WIDTH), MXU_DTYPE), sds((T, HGRN_WIDTH), F32),
                   sds((T, HGRN_WIDTH), F32), sds((MLA_HEADS, T), F32),
                   sds((1, D_MODEL), F32), sds((1, D_MODEL), F32), sds((1, MLA_WIDTH), F32), sds((1, HGRN_WIDTH), F32)]
        + (_half_stack_shapes(list(swap) + [sds((N_CHIPS, D_MODEL // N_CHIPS, D_MODEL), F32)]) if nsw else []),
        scratch_shapes=[pltpu.SemaphoreType.DMA((nsw + N_CHIPS,)), pltpu.SemaphoreType.DMA((nsw + N_CHIPS,))] if nsw else [],
        compiler_params=_params(("arbitrary",)),
    )(dz, dh2, h1, y1, mixb, o_raw, oh_raw, xph, wout, w_fpre, w_post, w_mla, w_hg, *swap)


def _in_bwd(x, dh1, cq, ckv, dq, dk, dv, dhq, dhf, dhi, dhg, rc, rs, w_pre, win, qnw, wq, kvnw, wk, wv, tt=256):
    T = x.shape[0]

    def body(x_ref, dh1_ref, cq_ref, ckv_ref, dq_ref, dk_ref, dv_ref, dhq_ref, dhf_ref, dhi_ref, dhg_ref, rc_ref, rs_ref,
             wpre_ref, win_ref, qnw_ref, wq_ref, kvnw_ref, wk_ref, wv_ref,
             dx_ref, dwin_ref, dwq_ref, dwk_ref, dwv_ref, dwpre_ref, dqnw_ref, dkvnw_ref):
        @pl.when(pl.program_id(0) == 0)
        def _():
            for r in (dwin_ref, dwq_ref, dwk_ref, dwv_ref, dwpre_ref, dqnw_ref, dkvnw_ref):
                r[...] = jnp.zeros_like(r)

        def add_win_grad(r, first):
            for arr0, n, chip, row0 in _win_grad_segments():
                if first <= arr0 and arr0 + n <= first + r.shape[0]:
                    dwin_ref[chip, row0:row0 + n, :] += r[arr0 - first:arr0 - first + n]

        lo = Q_RANK + KV_RANK + HEAD_PAD
        dxp_h = jnp.concatenate([dhq_ref[...], dhf_ref[...], dhi_ref[...], dhg_ref[...]], axis=-1)
        du = _mm(dxp_h, win_ref[lo:, :])
        wpre = wpre_ref[...]
        u, xn, rx = _rms_fwd(x_ref[...], wpre)
        add_win_grad(_mm_tn(dxp_h, u), lo)
        c, sa, sb = _rope_tables(rc_ref[...], rs_ref[...])
        lane = lax.broadcasted_iota(jnp.int32, (tt, HEAD_PAD), 1)
        dk_all = dk_ref[...]
        dq_lin = []
        dkr = jnp.zeros((tt, HEAD_PAD), F32)
        for h in range(MLA_HEADS):
            sl = slice(HEAD_PAD * h, HEAD_PAD * (h + 1))
            dq_lin.append(_rope_bwd(dq_ref[sl, :].T * ATTN_SCALE, c, sa, sb))
            dkr = dkr + dk_all[:, sl]
        dq_lin = jnp.concatenate(dq_lin, axis=-1)
        dkr = jnp.where((lane >= MLA_NOPE) & (lane < MLA_QK), _rope_bwd(dkr, c, sa, sb), 0.0)
        qnw = qnw_ref[...]
        qn, cqn, rq = _rms_fwd(cq_ref[...], qnw)
        dwq_ref[...] += _mm_tn(qn, dq_lin)
        dcq, dqnw = _rms_bwd(_mm_nt(dq_lin, wq_ref[...]), cqn, rq, qnw)
        dqnw_ref[...] += dqnw
        kvnw = kvnw_ref[...]
        kvn, ckvn, rkv = _rms_fwd(ckv_ref[...], kvnw)
        dv_ = dv_ref[...]
        dwk_ref[...] += _mm_tn(kvn, dk_all)
        dwv_ref[...] += _mm_tn(kvn, dv_)
        dckv, dkvnw = _rms_bwd(_mm_nt(dk_all, wk_ref[...]) + _mm_nt(dv_, wv_ref[...]), ckvn, rkv, kvnw)
        dkvnw_ref[...] += dkvnw
        dxp_a = jnp.concatenate([dcq, dckv, dkr], axis=-1)
        add_win_grad(_mm_tn(dxp_a, u), 0)
        dx_u, dwpre = _rms_bwd(du + _mm(dxp_a, win_ref[:lo, :]), xn, rx, wpre)
        dwpre_ref[...] += dwpre
        dx_ref[...] = dh1_ref[...] + dx_u

    row = lambda w: pl.BlockSpec((tt, w), lambda i: (i, 0))
    full = lambda a: pl.BlockSpec(a.shape, lambda i: (0,) * a.ndim)
    sds = jax.ShapeDtypeStruct
    qk_w = MLA_HEADS * HEAD_PAD
    return pl.pallas_call(
        body, name="in_bwd", grid=(T // tt,),
        in_specs=[row(D_MODEL), row(D_MODEL), row(Q_RANK), row(KV_RANK), pl.BlockSpec((qk_w, tt), lambda i: (0, i)),
                  row(qk_w), row(MLA_WIDTH),
                  row(HGRN_WIDTH), row(HGRN_WIDTH), row(HGRN_WIDTH), row(HGRN_WIDTH), row(HEAD_PAD), row(HEAD_PAD),
                  full(w_pre), full(win), full(qnw), full(wq), full(kvnw), full(wk), full(wv)],
        out_specs=[row(D_MODEL), pl.BlockSpec(WIN_COMM_SHAPE, lambda i: (0, 0, 0)), full(wq), full(wk), full(wv),
                   full(w_pre), full(qnw), full(kvnw)],
        out_shape=[sds((T, D_MODEL), F32), sds(WIN_COMM_SHAPE, F32), sds(wq.shape, F32), sds(wk.shape, F32),
                   sds(wv.shape, F32), sds(w_pre.shape, F32), sds(qnw.shape, F32), sds(kvnw.shape, F32)],
        compiler_params=_params(("arbitrary",)),
    )(x, dh1, cq, ckv, dq, dk, dv, dhq, dhf, dhi, dhg, rc, rs, w_pre, win, qnw, wq, kvnw, wk, wv)


def _arrange_weights(win_t, wuq_full, wukv):
    dt = win_t.dtype
    z = lambda n: jnp.zeros((n, D_MODEL), dt)
    s2 = Q_RANK + KV_RANK
    win_arr = jnp.concatenate([win_t[:s2], z(MLA_NOPE), win_t[s2:s2 + MLA_ROPE], z(HEAD_PAD - MLA_QK),
                               win_t[s2 + MLA_ROPE:]], axis=0)
    wq_arr = jnp.pad(wuq_full, ((0, 0), (0, 0), (0, HEAD_PAD - MLA_QK))).reshape(Q_RANK, MLA_HEADS * HEAD_PAD)
    wk_arr = jnp.pad(wukv[:, :, :MLA_NOPE], ((0, 0), (0, 0), (0, HEAD_PAD - MLA_NOPE))).reshape(
        KV_RANK, MLA_HEADS * HEAD_PAD)
    wv_arr = wukv[:, :, MLA_NOPE:].reshape(KV_RANK, MLA_WIDTH)
    return win_arr, wq_arr, wk_arr, wv_arr


WIN_COMM_SHAPE = (N_CHIPS, -(-D_IN // N_CHIPS // 32) * 32, D_MODEL)


def _win_grad_segments():
    s2 = Q_RANK + KV_RANK
    runs = [(0, s2, 0), (s2, s2 + MLA_ROPE, MLA_NOPE), (s2 + MLA_ROPE, D_IN, HEAD_PAD - MLA_ROPE)]
    per = D_IN // N_CHIPS
    segs = []
    for lo, hi, shift in runs:
        for k in range(N_CHIPS):
            a, b = max(lo, per * k), min(hi, per * (k + 1))
            if a < b:
                segs.append((a + shift, b - a, k, a - per * k))
    return segs


def _unarrange_grads(dwq_arr, dwk_arr, dwv_arr):
    dwuq = dwq_arr.reshape(Q_RANK, MLA_HEADS, HEAD_PAD)[:, :, :MLA_QK]
    dwukv = jnp.concatenate([dwk_arr.reshape(KV_RANK, MLA_HEADS, HEAD_PAD)[:, :, :MLA_NOPE],
                             dwv_arr.reshape(KV_RANK, MLA_HEADS, MLA_V)], axis=-1)
    return dwuq, dwukv


def _rope_inv_freq():
    inv = 1.0 / (ROPE_THETA ** (jnp.arange(0, MLA_ROPE, 2, dtype=F32) / MLA_ROPE))
    z = lambda n: jnp.zeros((n,), F32)
    return jnp.concatenate([z(MLA_NOPE), inv, inv, z(HEAD_PAD - MLA_QK)]).reshape(1, HEAD_PAD)


def _local_step(x, pos, tgt, small, win_arr, wq_arr, wk_arr, wv_arr, late, place=None):
    invf = _rope_inv_freq()
    cq, ckv, xph, qb, kb, vb, kt, vt, rc, rs = _in_fwd(x, pos, invf, small["attn_pre_norm"], win_arr, small["mla_q_norm"],
                                               wq_arr, small["mla_kv_norm"], wk_arr, wv_arr)
    if place is None:
        o_raw, lse = _attn_fwd_t(qb, kb, vt)
        wout, wg, wu, wd = late
    else:
        o_raw, lse, *stacks = _attn_fwd_t(qb, kb, vt, gather=late)
        wout, wg, wu, wd = [lax.dynamic_update_slice(s, l[None], (place[1], 0, 0)) for s, l in zip(stacks, late)]
        wout = wout.reshape(D_MODEL, D_MODEL)
    oh_raw, states = _hgrn_fwd(xph, small["hgrn_lb_logits"])
    h1, y1, zb, mixb = _proj_fwd(x, o_raw, oh_raw, xph, wout, small["mla_out_norm"], small["hgrn_out_norm"],
                                 small["attn_post_norm"], small["ffn_pre_norm"])
    g, up, dy2b, dh2, loss_acc, d_fpost = _ffn_fwd(zb, h1, tgt, small["ffn_post_norm"], wg, wu, wd)
    dwg, dwu, dwd, dz = _ffn_bwd(zb, g, up, dy2b, wg, wu, wd)
    ffn_grads = [] if place is None else [dwg, dwu, dwd]
    dh1, dwout, d_o, d_oh, dhg, dvec, d_fpre, d_post, d_mla, d_hg, *ffn_rs = _mid_bwd(
        dz, dh2, h1, y1, mixb, o_raw, oh_raw, xph, wout, small["ffn_pre_norm"], small["attn_post_norm"],
        small["mla_out_norm"], small["hgrn_out_norm"], swap=ffn_grads)
    if ffn_grads:
        ffn_grads = ffn_grads + [dwout.reshape(N_CHIPS, D_MODEL // N_CHIPS, D_MODEL)]
    ffn_ps = _pair_sum(place, ffn_grads, ffn_rs, name="pair_sum_ffn") if ffn_grads else []
    dq, dk, dv, *ffn_ris = _attn_bwd_t(qb, kb, kt, vb, d_o, lse, dvec.reshape(lse.shape), send=ffn_ps)
    ffn_sums = _chip_sum(place, ffn_grads, ffn_rs, ffn_ris, name="chip_sum_ffn") if ffn_grads else []
    dhq, dhf, dhi, d_lbl, *ffn_final = _hgrn_bwd(xph, small["hgrn_lb_logits"], states, d_oh, fill=ffn_sums)
    dx, dwin4, dwq_arr, dwk_arr, dwv_arr, d_pre, d_qn, d_kvn = _in_bwd(
        x, dh1, cq, ckv, dq, dk, dv, dhq, dhf, dhi, dhg, rc, rs, small["attn_pre_norm"], win_arr,
        small["mla_q_norm"], wq_arr, small["mla_kv_norm"], wk_arr, wv_arr)
    dwuq, dwukv = _unarrange_grads(dwq_arr, dwk_arr, dwv_arr)
    loss = 0.5 * jnp.sum(loss_acc) * (1.0 / D_MODEL)
    grads = dict(attn_pre_norm=d_pre, w_in=dwin4, mla_q_norm=d_qn, mla_w_uq=dwuq, mla_kv_norm=d_kvn, mla_w_ukv=dwukv,
                 mla_out_norm=d_mla, hgrn_lb_logits=d_lbl, hgrn_out_norm=d_hg, w_out=dwout, attn_post_norm=d_post,
                 ffn_pre_norm=d_fpre, w_gate=dwg, w_up=dwu, w_down=dwd, ffn_post_norm=d_fpost)
    if place is None:
        return loss, dx, grads
    return loss, dx, grads, ffn_final


def _place():
    x, y, c = lax.axis_index("x"), lax.axis_index("y"), lax.axis_index("c")
    others = [(1 - x, y), (x, 1 - y), (1 - x, 1 - y)]
    return x, y, c, 2 * x + y, (x, y, 1 - c), others


def _half(ref, c, rows):
    return ref.at[pl.ds(pl.multiple_of(c * rows, 8), rows)]


def _rcopy(src, dst, send, recv, k, to):
    return pltpu.make_async_remote_copy(src_ref=src, dst_ref=dst, send_sem=send.at[k], recv_sem=recv.at[k],
                                        device_id=to, device_id_type=MESH)


class _Gather:
    def __init__(self, ins, outs, send, recv):
        self.ins, self.outs, self.send, self.recv = ins, outs, send, recv
        self.n = len(ins)
        self.halves = [r.shape[0] // 2 for r in ins]
        _, _, self.c, self.me, self.sib, self.others = _place()

    def _each(self):
        for j, (px, py) in enumerate(self.others):
            for a in range(self.n):
                yield j * self.n + a, a, 2 * px + py, (px, py, self.c)

    def sends(self):
        return [_rcopy(_half(self.ins[a], self.c, self.halves[a]), _half(self.outs[a].at[self.me], self.c, self.halves[a]),
                       self.send, self.recv, k, to) for k, a, _, to in self._each()]

    def arrivals(self):
        parts = [(k, _half(self.outs[a].at[chip], self.c, self.halves[a]), to) for k, a, chip, to in self._each()]
        return [_rcopy(p, p, self.send, self.recv, k, to) for k, p, to in parts]

    def forwards(self):
        parts = [(k, _half(self.outs[a].at[chip], self.c, self.halves[a])) for k, a, chip, _ in self._each()]
        return [_rcopy(p, p, self.send, self.recv, 3 * self.n + k, self.sib) for k, p in parts]

    def forward_arrivals(self):
        parts = [(k, _half(self.outs[a].at[chip], 1 - self.c, self.halves[a])) for k, a, chip, _ in self._each()]
        return [_rcopy(p, p, self.send, self.recv, 3 * self.n + k, self.sib) for k, p in parts]

    @staticmethod
    def out_shapes(arrs):
        return [jax.ShapeDtypeStruct((N_CHIPS,) + a.shape, a.dtype) for a in arrs]

    @staticmethod
    def semaphores(arrs):
        return [pltpu.SemaphoreType.DMA((6 * len(arrs),)), pltpu.SemaphoreType.DMA((6 * len(arrs),))]


def _gather_chips(arrs, name):
    n = len(arrs)

    def body(*refs):
        gat = _Gather(refs[:n], refs[n:2 * n], *refs[2 * n:])
        sends, forwards = gat.sends(), gat.forwards()
        for cp in sends:
            cp.start()
        for arrival, fw in zip(gat.arrivals(), forwards):
            arrival.wait_recv()
            fw.start()
        for arrival in gat.forward_arrivals():
            arrival.wait_recv()
        for cp in sends + forwards:
            cp.wait_send()

    return pl.pallas_call(body, name=name, in_specs=[ANY] * n, out_specs=[ANY] * n, out_shape=_Gather.out_shapes(arrs),
                          scratch_shapes=_Gather.semaphores(arrs))(*arrs)


def _grad_blocks(gs):
    return 2 if all(g.shape[1] // 2 % 32 == 0 for g in gs) else 1


def _pair_swap_copies(g_refs, r_refs, send, recv):
    _, _, c, _, sib, _ = _place()
    copies = []
    for a, (g, r) in enumerate(zip(g_refs, r_refs)):
        h = g.shape[1] // 2
        copies.append(_rcopy(g.at[:, pl.ds(pl.multiple_of((1 - c) * h, 8), h)], r, send, recv, a, sib))
    return copies


def _half_stack_shapes(gs, dtype=None):
    return [jax.ShapeDtypeStruct((N_CHIPS, g.shape[1] // 2, g.shape[2]), dtype or g.dtype) for g in gs]


def _pair_swap(gs, wholes):
    n, nw = len(gs), len(wholes)

    def body(*refs):
        ins, outs, (send, recv) = refs[:n + nw], refs[n + nw:2 * (n + nw)], refs[2 * (n + nw):]
        copies = _pair_swap_copies(ins[:n], outs[:n], send, recv)
        copies += [_rcopy(ins[n + k], outs[n + k], send, recv, n + k, _place()[4]) for k in range(nw)]
        for cp in copies:
            cp.start()
        for cp in copies:
            cp.wait()

    return pl.pallas_call(
        body, name="pair_swap", in_specs=[ANY] * (n + nw), out_specs=[ANY] * (n + nw),
        out_shape=_half_stack_shapes(gs) + [jax.ShapeDtypeStruct(w.shape, w.dtype) for w in wholes],
        scratch_shapes=[pltpu.SemaphoreType.DMA((n + nw,)), pltpu.SemaphoreType.DMA((n + nw,))],
    )(*gs, *wholes)


def _pair_sum(place, gs, rs, small=None, name="pair_sum"):
    n = len(gs)
    nb = _grad_blocks(gs)

    def body(place_ref, *refs):
        g_refs, r_refs, p_refs = refs[:n], refs[n:2 * n], refs[-n - 1:-1] if small else refs[-n:]
        for a in range(n):
            p_refs[a][0] = (g_refs[a][0] + r_refs[a][0]).astype(p_refs[a].dtype)
        if small:
            @pl.when((pl.program_id(0) == 0) & (pl.program_id(1) == 0))
            def _():
                refs[-1][...] = refs[2 * n][...] + refs[2 * n + 1][...]

    chip = lambda k, p: lax.rem(p[1] + 1 + k, N_CHIPS)
    in_specs, out_specs = [], []
    for g in gs:
        blk = (1, g.shape[1] // 2 // nb, g.shape[2])
        in_specs.append(pl.BlockSpec(blk, lambda i, k, p: (chip(k, p), p[0] * nb + i, 0)))
    for g in gs:
        blk = (1, g.shape[1] // 2 // nb, g.shape[2])
        in_specs.append(pl.BlockSpec(blk, lambda i, k, p: (chip(k, p), i, 0)))
        out_specs.append(pl.BlockSpec(blk, lambda i, k, p: (chip(k, p), i, 0)))
    out_shape = _half_stack_shapes(gs, BF16)
    if small:
        sm_spec = pl.BlockSpec(small[0].shape, lambda i, k, p: (0, 0))
        in_specs += [sm_spec, sm_spec]
        out_specs.append(sm_spec)
        out_shape.append(jax.ShapeDtypeStruct(small[0].shape, F32))
    return pl.pallas_call(
        body, name=name,
        grid_spec=pltpu.PrefetchScalarGridSpec(num_scalar_prefetch=1, grid=(nb, N_CHIPS - 1), in_specs=in_specs,
                                               out_specs=out_specs),
        out_shape=out_shape,
        compiler_params=_params(("arbitrary", "arbitrary")),
    )(place, *gs, *rs, *(small or ()))


def _chip_swap_copies(p_refs, ri_refs, send, recv):
    _, _, c, _, _, others = _place()
    n = len(p_refs)
    return [_rcopy(p_refs[a].at[2 * px + py], ri_refs[a].at[j], send, recv, j * n + a, (px, py, c))
            for j, (px, py) in enumerate(others) for a in range(n)]


def _chip_swap_shapes(ps):
    return [jax.ShapeDtypeStruct((3,) + p.shape[1:], p.dtype) for p in ps]


def _chip_swap(ps, pair):
    n = len(ps)

    def body(*refs):
        start, finish = _chip_swap_plan(refs[:n], refs[n], refs[n + 1:2 * n + 1], refs[2 * n + 1], *refs[2 * n + 2:])
        start()
        finish()

    return pl.pallas_call(
        body, name="chip_swap", in_specs=[ANY] * (n + 1), out_specs=[ANY] * (n + 1),
        out_shape=_chip_swap_out_shapes(ps, pair), scratch_shapes=_chip_swap_semaphores(n),
    )(*ps, pair)


def _chip_swap_plan(p_refs, pair_ref, ri_refs, sm4_ref, send, recv, lsem):
    n = len(p_refs)
    hs = SMALL_ROWS // 2
    x, y, c, me, sib, others = _place()
    local = pltpu.make_async_copy(pair_ref, sm4_ref.at[me], lsem.at[0])
    copies = _chip_swap_copies(p_refs, ri_refs, send, recv)
    arrivals = list(copies)
    for j, (px, py) in enumerate(others):
        copies.append(_rcopy(_half(pair_ref, c, hs), _half(sm4_ref.at[me], c, hs), send, recv, 3 * n + j, (px, py, c)))
        part = _half(sm4_ref.at[2 * px + py], c, hs)
        arrivals.append(_rcopy(part, part, send, recv, 3 * n + j, (px, py, c)))

    def start():
        local.start()
        for cp in copies:
            cp.start()

    def finish():
        for arrival in arrivals:
            arrival.wait_recv()
        for cp in copies:
            cp.wait_send()
        local.wait()

    return start, finish


def _chip_swap_out_shapes(ps, pair):
    return _chip_swap_shapes(ps) + [jax.ShapeDtypeStruct((N_CHIPS,) + pair.shape, pair.dtype)]


def _chip_swap_semaphores(n):
    k = 3 * (n + 1)
    return [pltpu.SemaphoreType.DMA((k,)), pltpu.SemaphoreType.DMA((k,)), pltpu.SemaphoreType.DMA((1,))]


def _chip_sum(place, gs, rs, ris, name="chip_sum"):
    n = len(gs)
    nb = 1

    def body(place_ref, *refs):
        g_refs, r_refs, ri_refs, o_refs = refs[:n], refs[n:2 * n], refs[2 * n:3 * n], refs[3 * n:]
        for a in range(n):
            ri = ri_refs[a]
            o_refs[a][...] = (g_refs[a][0] + r_refs[a][0]) + ri[0].astype(F32) + ri[1].astype(F32) + ri[2].astype(F32)

    in_specs, out_specs, out_shape = [], [], []
    for g in gs:
        blk = (1, g.shape[1] // 2 // nb, g.shape[2])
        in_specs.append(pl.BlockSpec(blk, lambda i, p: (p[1], p[0] * nb + i, 0)))
    for g in gs:
        blk = (1, g.shape[1] // 2 // nb, g.shape[2])
        in_specs.append(pl.BlockSpec(blk, lambda i, p: (p[1], i, 0)))
    for g in gs:
        rb = g.shape[1] // 2 // nb
        in_specs.append(pl.BlockSpec((3, rb, g.shape[2]), lambda i, p: (0, i, 0)))
        out_specs.append(pl.BlockSpec((rb, g.shape[2]), lambda i, p: (p[0] * nb + i, 0)))
        out_shape.append(jax.ShapeDtypeStruct(g.shape[1:], F32))
    return pl.pallas_call(
        body, name=name,
        grid_spec=pltpu.PrefetchScalarGridSpec(num_scalar_prefetch=1, grid=(nb,), in_specs=in_specs, out_specs=out_specs),
        out_shape=out_shape,
        compiler_params=_params(("arbitrary",)),
    )(place, *gs, *rs, *ris)


def _pair_fill_copies(g_refs, send, recv):
    _, _, c, _, sib, _ = _place()
    copies, waits = [], []
    for a, g in enumerate(g_refs):
        h = g.shape[0] // 2
        mine, theirs = _half(g, c, h), _half(g, 1 - c, h)
        copies.append(_rcopy(mine, mine, send, recv, a, sib))
        waits.append(_rcopy(theirs, theirs, send, recv, a, sib))
    return copies, waits


def _pair_fill(gfs, sm4):
    n = len(gfs)
    hs = SMALL_ROWS // 2

    def body(*refs):
        g_refs, sm4_ref = refs[n + 1:2 * n + 1], refs[2 * n + 1]
        send, recv = refs[2 * n + 2:]
        x, y, c, me, sib, others = _place()
        copies, waits = _pair_fill_copies(g_refs, send, recv)
        for j, (px, py) in enumerate(others):
            chip = 2 * px + py
            mine, theirs = _half(sm4_ref.at[chip], c, hs), _half(sm4_ref.at[chip], 1 - c, hs)
            copies.append(pltpu.make_async_remote_copy(src_ref=mine, dst_ref=mine, send_sem=send.at[n + j],
                                                       recv_sem=recv.at[n + j], device_id=sib, device_id_type=MESH))
            waits.append(pltpu.make_async_remote_copy(src_ref=theirs, dst_ref=theirs, send_sem=send.at[n + j],
                                                      recv_sem=recv.at[n + j], device_id=sib, device_id_type=MESH))
        for cp in copies:
            cp.start()
        for w in waits:
            w.wait_recv()
        for cp in copies:
            cp.wait_send()

    return pl.pallas_call(
        body, name="pair_fill", in_specs=[ANY] * (n + 1), out_specs=[ANY] * (n + 1),
        out_shape=[jax.ShapeDtypeStruct(g.shape, g.dtype) for g in gfs] + [jax.ShapeDtypeStruct(sm4.shape, sm4.dtype)],
        input_output_aliases={i: i for i in range(n + 1)},
        scratch_shapes=[pltpu.SemaphoreType.DMA((n + 3,)), pltpu.SemaphoreType.DMA((n + 3,))],
    )(*gfs, sm4)


def _adamw_math(w, g, m, v):
    m = ADAM_B1 * m + (1.0 - ADAM_B1) * g
    v = ADAM_B2 * v + (1.0 - ADAM_B2) * (g * g)
    m_hat = m / (1.0 - ADAM_B1 ** ADAM_STEP)
    v_hat = v / (1.0 - ADAM_B2 ** ADAM_STEP)
    return -ADAM_LR * (m_hat / (jnp.sqrt(v_hat) + ADAM_EPS) + ADAM_WD * w), m, v


def _adamw(items, steps, name):
    n = len(items)

    def body(*refs):
        for a in range(n):
            g = refs[4 * a + 1][...]
            d, mo, vo = _adamw_math(refs[4 * a][...], g, refs[4 * a + 2][...], refs[4 * a + 3][...])
            for out, val in zip(refs[4 * n + 4 * a:4 * n + 4 * a + 4], (g, d, mo, vo)):
                out[...] = val

    spec = lambda w: pl.BlockSpec((w.shape[0] // steps, w.shape[1]), lambda i: (i, 0))
    flat = pl.pallas_call(
        body, name=name, grid=(steps,), in_specs=[spec(it[0]) for it in items for _ in range(4)],
        out_specs=[spec(it[0]) for it in items for _ in range(4)],
        out_shape=[jax.ShapeDtypeStruct(it[0].shape, F32) for it in items for _ in range(4)],
        compiler_params=_params(("arbitrary",)),
    )(*[a for it in items for a in it])
    return [flat[4 * a:4 * a + 4] for a in range(n)]


def _adamw_small(sm4, wmv):
    views = SMALL_VIEWS[:-1]
    n = len(views)

    def body(sm4_ref, *refs):
        g_all = ((sm4_ref[0] + sm4_ref[1]) + sm4_ref[2]) + sm4_ref[3]
        for a, (name, rows, cols) in enumerate(views):
            row = SMALL_OFFSETS[name]
            g = g_all[row:row + rows, :cols]
            d, mo, vo = _adamw_math(refs[3 * a][...], g, refs[3 * a + 1][...], refs[3 * a + 2][...])
            for out, val in zip(refs[3 * n + 4 * a:3 * n + 4 * a + 4], (g, d, mo, vo)):
                out[...] = val
        row = SMALL_OFFSETS["loss"]
        refs[-1][...] = g_all[row:row + 1, :128]

    flat = pl.pallas_call(
        body, name="adamw_small",
        out_shape=[jax.ShapeDtypeStruct((rows, cols), F32) for _, rows, cols in views for _ in range(4)]
        + [jax.ShapeDtypeStruct((1, 128), F32)],
        compiler_params=pltpu.CompilerParams(vmem_limit_bytes=VMEM_LIMIT),
    )(sm4, *[a for t in wmv for a in t])
    return [flat[4 * a:4 * a + 4] for a in range(n)] + [flat[-1]]


SMALL_NAMES = ("attn_pre_norm", "mla_q_norm", "mla_kv_norm", "mla_w_ukv", "mla_out_norm", "hgrn_lb_logits",
               "hgrn_out_norm", "attn_post_norm", "ffn_pre_norm", "ffn_post_norm")
BIG_NAMES = ("w_in", "mla_w_uq", "w_out", "w_gate", "w_up", "w_down")
WEIGHT_NAMES = ("attn_pre_norm", "w_in", "mla_q_norm", "mla_w_uq", "mla_kv_norm", "mla_w_ukv", "mla_out_norm",
                "hgrn_lb_logits", "hgrn_out_norm", "w_out", "attn_post_norm", "ffn_pre_norm", "w_gate", "w_up", "w_down",
                "ffn_post_norm")


UQ_COMM_SHAPE = (192, 384)


def _pack_small(vals):
    parts, row = [], 0
    for name, rows, cols in sorted(SMALL_VIEWS, key=lambda view: SMALL_OFFSETS[view[0]]):
        assert SMALL_OFFSETS[name] == row
        parts.append(jnp.pad(vals[name].reshape(rows, cols), ((0, 0), (0, D_MODEL - cols))))
        row += rows
    parts.append(jnp.zeros((SMALL_ROWS - row, D_MODEL), F32))
    return jnp.concatenate(parts, axis=0)


def kernel(x, positions, attn_pre_norm, w_in, mla_q_norm, mla_w_uq, mla_kv_norm, mla_w_ukv, mla_out_norm, hgrn_lb_logits, hgrn_out_norm, w_out, attn_post_norm, ffn_pre_norm, w_gate, w_up, w_down, ffn_post_norm, loss_target, m_attn_pre_norm, m_w_in, m_mla_q_norm, m_mla_w_uq, m_mla_kv_norm, m_mla_w_ukv, m_mla_out_norm, m_hgrn_lb_logits, m_hgrn_out_norm, m_w_out, m_attn_post_norm, m_ffn_pre_norm, m_w_gate, m_w_up, m_w_down, m_ffn_post_norm, v_attn_pre_norm, v_w_in, v_mla_q_norm, v_mla_w_uq, v_mla_kv_norm, v_mla_w_ukv, v_mla_out_norm, v_hgrn_lb_logits, v_hgrn_out_norm, v_w_out, v_attn_post_norm, v_ffn_pre_norm, v_w_gate, v_w_up, v_w_down, v_ffn_post_norm):
    args = locals()
    W = {n: args[n] for n in WEIGHT_NAMES}
    M = {n: args["m_" + n] for n in WEIGHT_NAMES}
    V = {n: args["v_" + n] for n in WEIGHT_NAMES}
    T = x.shape[1]
    cx, cy, cc = lax.axis_index("x"), lax.axis_index("y"), lax.axis_index("c")

    win_rows = D_IN // N_CHIPS
    shard2d = {"w_in": (win_rows, D_MODEL), "mla_w_uq": (Q_RANK // N_CHIPS, MLA_HEADS * MLA_QK),
               "w_out": (D_MODEL // N_CHIPS, D_MODEL), "w_gate": (FF_SHARD, D_MODEL), "w_up": (FF_SHARD, D_MODEL),
               "w_down": (FF_SHARD, D_MODEL)}
    transposed = ("w_in", "w_gate", "w_up")
    to2d = lambda n, a: a[0].T if n in transposed else a.reshape(shard2d[n])
    from2d = lambda n, t: t.T[None] if n in transposed else t.reshape(W[n].shape)
    me = 2 * cx + cy
    place = jnp.stack([cc, me]).astype(jnp.int32)
    local_b = [to2d(n, W[n]).astype(BF16) for n in BIG_NAMES]
    local_b[0] = jnp.pad(local_b[0], ((0, WIN_COMM_SHAPE[1] - win_rows), (0, 0)))
    stacks = _gather_chips(local_b[:2], "gather_weights")
    win4, wuq4 = [lax.dynamic_update_slice(s, l[None], (me, 0, 0)) for s, l in zip(stacks, local_b)]
    win_t = win4[:, :win_rows].reshape(D_IN, D_MODEL)
    wuq_full = wuq4.reshape(Q_RANK, MLA_HEADS, MLA_QK)
    win_arr, wq_arr, wk_arr, wv_arr = _arrange_weights(win_t, wuq_full, mla_w_ukv[0].astype(BF16))
    small = {n: W[n][0] if n == "mla_w_ukv" else W[n].reshape(-1, W[n].shape[-1]) for n in SMALL_NAMES}

    loss_local, dx, grads, ffn_final = _local_step(x[0], positions.reshape(T, 1), loss_target[0], small, win_arr,
                                                           wq_arr, wk_arr, wv_arr, local_b[2:], place)

    gs = [grads["w_in"], grads["mla_w_uq"].reshape((N_CHIPS,) + UQ_COMM_SHAPE)]
    sm = _pack_small({**grads, "loss": loss_local})
    *rs, ssib = _pair_swap(gs, (sm,))
    *ps, pair = _pair_sum(place, gs, rs, small=(sm, ssib))
    ffn_names, rest_names = BIG_NAMES[3:], BIG_NAMES[:2]
    g2d = dict(zip(ffn_names + BIG_NAMES[2:3], ffn_final))
    adam_in = lambda names_: [(to2d(n, W[n]), g2d[n], to2d(n, M[n]), to2d(n, V[n])) for n in names_]
    early_names = ffn_names + BIG_NAMES[2:3]
    updates = dict(zip(early_names, _adamw(adam_in(early_names), 4, "adamw_ffn")))
    *ris, sm4 = _chip_swap(ps, pair)
    *gfin, smf = _pair_fill(_chip_sum(place, gs, rs, ris), sm4)

    g2d.update({n: gfin[k].reshape((-1,) + shard2d[n][1:]) for k, n in enumerate(rest_names)})
    updates.update(zip(rest_names, _adamw(adam_in(rest_names), 1, "adamw_w_in")))
    G, DW, NM, NV = {}, {}, {}, {}
    for n, outs in updates.items():
        G[n], DW[n], NM[n], NV[n] = (from2d(n, t) for t in outs)
    view2d = lambda n, a: a.reshape(next((r, c) for name, r, c in SMALL_VIEWS if name == n))
    *res, loss_row = _adamw_small(smf, [tuple(view2d(n, t[n]) for t in (W, M, V)) for n in SMALL_NAMES])
    for n, outs in zip(SMALL_NAMES, res):
        G[n], DW[n], NM[n], NV[n] = (t.reshape(W[n].shape) for t in outs)
    loss = loss_row[0, 0]
    return (loss, dx[None], *[G[n] for n in WEIGHT_NAMES], *[DW[n] for n in WEIGHT_NAMES],
            *[NM[n] for n in WEIGHT_NAMES], *[NV[n] for n in WEIGHT_NAMES])
```

```python
import jax
import jax.numpy as jnp
from jax import lax
from jax.experimental import pallas as pl
from jax.experimental.pallas import tpu as pltpu

F32 = jnp.float32
BF16 = jnp.bfloat16
MXU_DTYPE = BF16

D_MODEL = 1024
MLA_HEADS = 8
MLA_NOPE = 64
MLA_ROPE = 32
MLA_V = 64
MLA_QK = MLA_NOPE + MLA_ROPE
Q_RANK = 384
KV_RANK = 128
MLA_WIDTH = MLA_HEADS * MLA_V
HEAD_PAD = 128
HGRN_HEADS = 4
HGRN_DIM = 128
HGRN_WIDTH = HGRN_HEADS * HGRN_DIM
CHUNK = 64
SUB = 16
HGRN_CPI = 4
D_IN = Q_RANK + KV_RANK + MLA_ROPE + 4 * HGRN_WIDTH
D_IN_ARR = Q_RANK + KV_RANK + HEAD_PAD + 4 * HGRN_WIDTH
D_FF = 2816
N_CHIPS = 4
FF_SHARD = D_FF // N_CHIPS
EPS = 1e-6
ROPE_THETA = 10000.0
ATTN_SCALE = MLA_QK ** -0.5
ATTN_SCALE_LOG2 = ATTN_SCALE * 1.4426950408889634
NEG_BIG = -1e30

ADAM_LR = 0.001
ADAM_B1 = 0.9
ADAM_B2 = 0.999
ADAM_EPS = 1e-08
ADAM_WD = 0.01
ADAM_STEP = 10

VMEM_LIMIT = 56 * 1024 * 1024
FFN_BWD_VMEM = 62 * 1024 * 1024

SMALL_VIEWS = (("attn_pre_norm", 1, 1024), ("mla_q_norm", 1, 384), ("mla_kv_norm", 1, 128), ("mla_w_ukv", 128, 1024),
               ("mla_out_norm", 1, 512), ("hgrn_lb_logits", 2, 512), ("hgrn_out_norm", 1, 512),
               ("attn_post_norm", 1, 1024), ("ffn_pre_norm", 1, 1024), ("ffn_post_norm", 1, 1024), ("loss", 1, 1))
ROW_TILE = 8


def _small_layout():
    offsets, row = {}, 0
    for whole in (True, False):
        for name, rows, _ in SMALL_VIEWS:
            if (rows % ROW_TILE == 0) == whole:
                offsets[name] = row
                row += rows
    return offsets, -(-row // (2 * ROW_TILE)) * 2 * ROW_TILE


SMALL_OFFSETS, SMALL_ROWS = _small_layout()

MESH = pl.DeviceIdType.MESH
ANY = pl.BlockSpec(memory_space=pl.ANY)


def _dot(a, b, dims, exact):
    if exact:
        return lax.dot_general(a.astype(F32), b.astype(F32), (dims, ((), ())), precision=lax.Precision.HIGH,
                               preferred_element_type=F32)
    return lax.dot_general(a.astype(MXU_DTYPE), b.astype(MXU_DTYPE), (dims, ((), ())), preferred_element_type=F32)


def _mm(a, b, exact=False):
    return _dot(a, b, ((1,), (0,)), exact)


def _mm_nt(a, b, exact=False):
    return _dot(a, b, ((1,), (1,)), exact)


def _mm_tn(a, b, exact=False):
    return _dot(a, b, ((0,), (0,)), exact)


def _rms_fwd(x, w):
    r = lax.rsqrt(jnp.mean(x * x, axis=-1, keepdims=True) + EPS)
    xn = x * r
    return xn * w, xn, r


def _rms_bwd(dy, xn, r, w):
    dxn = dy * w
    dx = r * (dxn - xn * jnp.mean(dxn * xn, axis=-1, keepdims=True))
    dw = jnp.sum(dy * xn, axis=0, keepdims=True)
    return dx, dw


def _group_sums(v, gs):
    t, n = v.shape
    lane = lax.broadcasted_iota(jnp.int32, (t, 128), 1)
    out = []
    for p in range(n // 128):
        vb = v[:, 128 * p:128 * (p + 1)]
        if gs == 128:
            out.append(jnp.sum(vb, axis=-1, keepdims=True))
        else:
            out.append(jnp.sum(jnp.where(lane < 64, vb, 0.0), axis=-1, keepdims=True))
            out.append(jnp.sum(jnp.where(lane >= 64, vb, 0.0), axis=-1, keepdims=True))
    return out


def _group_bcast(sums, gs, t):
    lane = lax.broadcasted_iota(jnp.int32, (t, 128), 1)
    if gs == 128:
        return jnp.concatenate([jnp.broadcast_to(s, (t, 128)) for s in sums], axis=-1)
    return jnp.concatenate([jnp.where(lane < 64, sums[2 * p], sums[2 * p + 1]) for p in range(len(sums) // 2)],
                           axis=-1)


def _grms_fwd(x, w, gs):
    t = x.shape[0]
    r = lax.rsqrt(_group_bcast(_group_sums(x * x, gs), gs, t) * (1.0 / gs) + EPS)
    xn = x * r
    return xn * w, xn, r


def _grms_bwd(dy, xn, r, w, gs):
    t = dy.shape[0]
    dxn = dy * w
    dx = r * (dxn - xn * (_group_bcast(_group_sums(dxn * xn, gs), gs, t) * (1.0 / gs)))
    dw = jnp.sum(dy * xn, axis=0, keepdims=True)
    return dx, dw


def _rope_tables(c_tab, s_tab):
    lane = lax.broadcasted_iota(jnp.int32, c_tab.shape, 1)
    first = (lane >= MLA_NOPE) & (lane < MLA_NOPE + MLA_ROPE // 2)
    second = (lane >= MLA_NOPE + MLA_ROPE // 2) & (lane < MLA_QK)
    return c_tab, jnp.where(first, -s_tab, 0.0), jnp.where(second, s_tab, 0.0)


def _rope(v, c, sa, sb):
    return v * c + pltpu.roll(v, HEAD_PAD - MLA_ROPE // 2, 1) * sa + pltpu.roll(v, MLA_ROPE // 2, 1) * sb


def _rope_bwd(d, c, sa, sb):
    return d * c - pltpu.roll(d, HEAD_PAD - MLA_ROPE // 2, 1) * sa - pltpu.roll(d, MLA_ROPE // 2, 1) * sb


def _params(sem, vmem=VMEM_LIMIT):
    return pltpu.CompilerParams(dimension_semantics=sem, vmem_limit_bytes=vmem)


def _in_fwd(x, pos, invf, w_pre, win, qnw, wq, kvnw, wk, wv, tt=512):
    T = x.shape[0]

    def body(x_ref, pos_ref, invf_ref, wpre_ref, win_ref, qnw_ref, wq_ref, kvnw_ref, wk_ref, wv_ref,
             cq_ref, ckv_ref, xph_ref, q_ref, k_ref, v_ref, kt_ref, vt_ref, rc_ref, rs_ref):
        u, _, _ = _rms_fwd(x_ref[...], wpre_ref[...])
        lo = Q_RANK + KV_RANK + HEAD_PAD
        xp = _mm_nt(u, win_ref[:lo, :])
        xph_ref[...] = _mm_nt(u, win_ref[lo:, :])
        cq = xp[:, :Q_RANK]
        ckv = xp[:, Q_RANK:Q_RANK + KV_RANK]
        kr = xp[:, Q_RANK + KV_RANK:]
        cq_ref[...] = cq
        ckv_ref[...] = ckv
        ang = pos_ref[...].astype(F32) * invf_ref[...]
        c_tab = jnp.cos(ang)
        s_tab = jnp.sin(ang)
        rc_ref[...] = c_tab
        rs_ref[...] = s_tab
        c, sa, sb = _rope_tables(c_tab, s_tab)
        qn, _, _ = _rms_fwd(cq, qnw_ref[...])
        q = _mm(qn, wq_ref[...])
        kvn, _, _ = _rms_fwd(ckv, kvnw_ref[...])
        kn = _mm(kvn, wk_ref[...])
        v = _mm(kvn, wv_ref[...])
        v_ref[...] = v.astype(v_ref.dtype)
        vt_ref[...] = v.T.astype(vt_ref.dtype)
        krr = _rope(kr, c, sa, sb)
        for h in range(MLA_HEADS):
            sl = slice(HEAD_PAD * h, HEAD_PAD * (h + 1))
            q_ref[:, sl] = (_rope(q[:, sl], c, sa, sb) * ATTN_SCALE_LOG2).astype(q_ref.dtype)
            kh = kn[:, sl] + krr
            k_ref[:, sl] = kh.astype(k_ref.dtype)
            kt_ref[sl, :] = kh.T.astype(kt_ref.dtype)

    row = lambda w: pl.BlockSpec((tt, w), lambda i: (i, 0))
    full = lambda a: pl.BlockSpec(a.shape, lambda i: (0,) * a.ndim)
    qk_w = MLA_HEADS * HEAD_PAD
    return pl.pallas_call(
        body, name="in_fwd", grid=(T // tt,),
        in_specs=[row(D_MODEL), row(1), full(invf), full(w_pre), full(win), full(qnw), full(wq), full(kvnw),
                  full(wk), full(wv)],
        out_specs=[row(Q_RANK), row(KV_RANK), row(4 * HGRN_WIDTH), row(qk_w), row(qk_w), row(MLA_WIDTH),
                   pl.BlockSpec((qk_w, tt), lambda i: (0, i)), pl.BlockSpec((MLA_WIDTH, tt), lambda i: (0, i)),
                   row(HEAD_PAD), row(HEAD_PAD)],
        out_shape=[jax.ShapeDtypeStruct((T, Q_RANK), F32), jax.ShapeDtypeStruct((T, KV_RANK), F32),
                   jax.ShapeDtypeStruct((T, 4 * HGRN_WIDTH), F32), jax.ShapeDtypeStruct((T, qk_w), MXU_DTYPE),
                   jax.ShapeDtypeStruct((T, qk_w), MXU_DTYPE), jax.ShapeDtypeStruct((T, MLA_WIDTH), MXU_DTYPE),
                   jax.ShapeDtypeStruct((qk_w, T), MXU_DTYPE), jax.ShapeDtypeStruct((MLA_WIDTH, T), MXU_DTYPE),
                   jax.ShapeDtypeStruct((T, HEAD_PAD), F32), jax.ShapeDtypeStruct((T, HEAD_PAD), F32)],
        compiler_params=_params(("arbitrary",)),
    )(x, pos, invf, w_pre, win, qnw, wq, kvnw, wk, wv)


def _attn_fwd_t(qb, kb, vt, gather=(), tq=256, hps=8):
    T = qb.shape[0]
    nq = T // tq
    ng = len(gather)
    steps = (MLA_HEADS // hps) * nq
    pass_on = steps - 3

    def body(q_ref, k_ref, vt_ref, *rest):
        o_ref, lse_ref = rest[ng:ng + 2]
        acc_scr = rest[2 * ng + 2]
        qi = pl.program_id(1)
        step_no = pl.program_id(0) * nq + qi
        if ng:
            gat = _Gather(rest[:ng], rest[ng + 2:2 * ng + 2], *rest[2 * ng + 3:])

            @pl.when(step_no == 0)
            def _():
                for cp in gat.sends():
                    cp.start()

            @pl.when(step_no == pass_on)
            def _():
                for arrival in gat.arrivals():
                    arrival.wait_recv()
                for cp in gat.forwards():
                    cp.start()

        heads = [slice(HEAD_PAD * a, HEAD_PAD * (a + 1)) for a in range(hps)]
        acc_scr[...] = jnp.zeros_like(acc_scr)

        def step(j, carry, masked):
            start = pl.multiple_of(j * tq, tq)
            scores = [_mm_nt(k_ref[pl.ds(start, tq), heads[a]], q_ref[:, heads[a]]) for a in range(hps)]
            new, probs, alphas = [], [], []
            for a in range(hps):
                m, l = carry[a]
                s = scores[a]
                if masked:
                    kk = lax.broadcasted_iota(jnp.int32, (tq, tq), 0)
                    qq = lax.broadcasted_iota(jnp.int32, (tq, tq), 1)
                    s = jnp.where(kk <= qq, s, NEG_BIG)
                m_new = jnp.maximum(m, jnp.max(s, axis=0, keepdims=True))
                alpha = jnp.exp2(m - m_new)
                p = jnp.exp2(s - m_new)
                l = l * alpha + jnp.sum(p, axis=0, keepdims=True)
                new.append((m_new, l))
                probs.append(p.astype(MXU_DTYPE))
                alphas.append(alpha)
                if a % 2:
                    pr = a // 2
                    vtj = vt_ref[2 * MLA_V * pr:2 * MLA_V * (pr + 1), pl.ds(start, tq)]
                    none = jnp.zeros((MLA_V, tq), vtj.dtype)
                    pv = (_mm(jnp.concatenate([vtj[:MLA_V], none], axis=0), probs[a - 1])
                          + _mm(jnp.concatenate([none, vtj[MLA_V:]], axis=0), probs[a]))
                    acc_scr[pr] = acc_scr[pr] * jnp.where(row < MLA_V, alphas[a - 1], alphas[a]) + pv
            return tuple(new)

        row = lax.broadcasted_iota(jnp.int32, (2 * MLA_V, tq), 0)
        init = tuple((jnp.full((1, tq), NEG_BIG, F32), jnp.zeros((1, tq), F32)) for _ in range(hps))
        carry = lax.fori_loop(0, qi, lambda j, c: step(j, c, False), init)
        carry = step(qi, carry, True)
        for pr in range(hps // 2):
            (m0, l0), (m1, l1) = carry[2 * pr], carry[2 * pr + 1]
            ot = acc_scr[pr] / jnp.where(row < MLA_V, l0, l1)
            o_ref[:, 2 * MLA_V * pr:2 * MLA_V * (pr + 1)] = ot.T
            lse_ref[pr, 0:1, :] = m0 + jnp.log2(l0)
            lse_ref[pr, 1:2, :] = m1 + jnp.log2(l1)

        if ng:
            @pl.when(step_no == steps - 1)
            def _():
                for arrival in gat.forward_arrivals():
                    arrival.wait_recv()
                for cp in gat.sends() + gat.forwards():
                    cp.wait_send()

    return pl.pallas_call(
        body, name="attn_fwd", grid=(MLA_HEADS // hps, nq),
        in_specs=[pl.BlockSpec((tq, hps * HEAD_PAD), lambda g, i: (i, g)),
                  pl.BlockSpec((T, hps * HEAD_PAD), lambda g, i: (0, g)),
                  pl.BlockSpec((hps * MLA_V, T), lambda g, i: (g, 0))] + [ANY] * ng,
        out_specs=[pl.BlockSpec((tq, hps * MLA_V), lambda g, i: (i, g)),
                   pl.BlockSpec((hps // 2, 2, tq), lambda g, i: (g, 0, i))] + [ANY] * ng,
        out_shape=[jax.ShapeDtypeStruct((T, MLA_WIDTH), F32), jax.ShapeDtypeStruct((MLA_HEADS // 2, 2, T), F32)]
        + _Gather.out_shapes(gather),
        scratch_shapes=[pltpu.VMEM((hps // 2, 2 * MLA_V, tq), F32)] + (_Gather.semaphores(gather) if ng else []),
        compiler_params=_params(("arbitrary", "arbitrary")),
    )(qb, kb, vt, *gather)


def _attn_bwd_t(qb, kb, kt, vb, dob, lse, dvec, send=(), tq=512, hps=4):
    T = qb.shape[0]
    nq = T // tq
    ns = len(send)
    steps = (MLA_HEADS // hps) * nq

    def body(q_ref, k_ref, kt_ref, v_ref, do_ref, lse_ref, d_ref, *rest):
        dqt_ref, dk_ref, dv_ref = rest[ns:ns + 3]
        va_scr, dv_scr = rest[2 * ns + 3:2 * ns + 5]
        j = pl.program_id(1)
        step_no = pl.program_id(0) * nq + j
        if ns:
            @pl.when(step_no == 0)
            def _():
                for cp in _chip_swap_copies(rest[:ns], rest[ns + 3:2 * ns + 3], *rest[2 * ns + 5:]):
                    cp.start()

        @pl.when(j == 0)
        def _():
            dqt_ref[...] = jnp.zeros_like(dqt_ref)

        lane = lax.broadcasted_iota(jnp.int32, (tq, 2 * MLA_V), 1)
        heads = [slice(HEAD_PAD * a, HEAD_PAD * (a + 1)) for a in range(hps)]
        pairs = [slice(2 * MLA_V * p, 2 * MLA_V * (p + 1)) for p in range(hps // 2)]
        for pr in range(hps // 2):
            vpair = v_ref[:, pairs[pr]]
            va_scr[2 * pr] = jnp.where(lane < MLA_V, vpair, jnp.zeros_like(vpair))
            va_scr[2 * pr + 1] = jnp.where(lane >= MLA_V, vpair, jnp.zeros_like(vpair))
        dk_ref[...] = jnp.zeros_like(dk_ref)
        dv_scr[...] = jnp.zeros_like(dv_scr)

        def step(i, masked):
            start = pl.multiple_of(i * tq, tq)
            rows = pl.ds(start, tq)
            scores = [_mm_nt(k_ref[:, heads[a]], q_ref[rows, heads[a]]) for a in range(hps)]
            dps = [_mm_nt(va_scr[a], do_ref[rows, pairs[a // 2]]) for a in range(hps)]
            for a in range(hps):
                pr, r = a // 2, a % 2
                p = jnp.exp2(scores[a] - lse_ref[pr, r:r + 1, rows])
                if masked:
                    kk = lax.broadcasted_iota(jnp.int32, (tq, tq), 0)
                    qq = lax.broadcasted_iota(jnp.int32, (tq, tq), 1)
                    p = jnp.where(kk <= qq, p, 0.0)
                ds = p * (dps[a] - d_ref[pr, r:r + 1, rows])
                dv_scr[a] += _mm(p, do_ref[rows, pairs[pr]])
                dk_ref[:, heads[a]] += _mm(ds, q_ref[rows, heads[a]])
                dqt_ref[heads[a], rows] += _mm(kt_ref[heads[a], :], ds)

        def loop_body(i, _):
            step(i, False)
            return 0

        step(j, True)
        lax.fori_loop(j + 1, nq, loop_body, 0)
        for pr in range(hps // 2):
            dv_ref[:, pairs[pr]] = jnp.where(lane < MLA_V, dv_scr[2 * pr], dv_scr[2 * pr + 1])
        dk_ref[...] = dk_ref[...] * (ATTN_SCALE / ATTN_SCALE_LOG2)

        if ns:
            @pl.when(step_no == steps - 1)
            def _():
                for cp in _chip_swap_copies(rest[:ns], rest[ns + 3:2 * ns + 3], *rest[2 * ns + 5:]):
                    cp.wait()

    stat = pl.BlockSpec((hps // 2, 2, T), lambda g, j: (g, 0, 0))
    return pl.pallas_call(
        body, name="attn_bwd", grid=(MLA_HEADS // hps, nq),
        in_specs=[pl.BlockSpec((T, hps * HEAD_PAD), lambda g, j: (0, g)),
                  pl.BlockSpec((tq, hps * HEAD_PAD), lambda g, j: (j, g)),
                  pl.BlockSpec((hps * HEAD_PAD, tq), lambda g, j: (g, j)),
                  pl.BlockSpec((tq, hps * MLA_V), lambda g, j: (j, g)),
                  pl.BlockSpec((T, hps * MLA_V), lambda g, j: (0, g)), stat, stat] + [ANY] * ns,
        out_specs=[pl.BlockSpec((hps * HEAD_PAD, T), lambda g, j: (g, 0)),
                   pl.BlockSpec((tq, hps * HEAD_PAD), lambda g, j: (j, g)),
                   pl.BlockSpec((tq, hps * MLA_V), lambda g, j: (j, g))] + [ANY] * ns,
        out_shape=[jax.ShapeDtypeStruct((MLA_HEADS * HEAD_PAD, T), F32),
                   jax.ShapeDtypeStruct((T, MLA_HEADS * HEAD_PAD), F32),
                   jax.ShapeDtypeStruct((T, MLA_WIDTH), F32)] + _chip_swap_shapes(send),
        scratch_shapes=[pltpu.VMEM((hps, tq, 2 * MLA_V), vb.dtype), pltpu.VMEM((hps, tq, 2 * MLA_V), F32)]
        + ([pltpu.SemaphoreType.DMA((3 * ns,)), pltpu.SemaphoreType.DMA((3 * ns,))] if ns else []),
        compiler_params=_params(("arbitrary", "arbitrary")),
    )(qb, kb, kt, vb, dob, lse, dvec, *send)


def _cumsum_rows(x):
    n = x.shape[0]
    row = lax.broadcasted_iota(jnp.int32, x.shape, 0)
    s = 1
    while s < n:
        x = x + jnp.where(row >= s, pltpu.roll(x, s, 0), 0.0)
        s *= 2
    return x


def _rev_cumsum_rows(x):
    n = x.shape[0]
    row = lax.broadcasted_iota(jnp.int32, x.shape, 0)
    s = 1
    while s < n:
        x = x + jnp.where(row < n - s, pltpu.roll(x, n - s, 0), 0.0)
        s *= 2
    return x


def _lb_from_logits(l):
    l0, l1 = l[0:1, :], l[1:2, :]
    m = jnp.maximum(l0, l1)
    e0, e1 = jnp.exp(l0 - m), jnp.exp(l1 - m)
    return e0 / (e0 + e1)


def _hgrn_gates(hq, hf, lb):
    sig_f = jax.nn.sigmoid(hf)
    f = lb + (1.0 - lb) * sig_f
    sig_q = jax.nn.sigmoid(hq)
    return sig_f, f, jnp.log(f), 1.0 - f, sig_q, hq * sig_q


def _hgrn_intra(q, kk, b, exact=False):
    row = lax.broadcasted_iota(jnp.int32, b.shape, 0)
    qs, ks, eqs, eks, a_rows = [], [], [], [], []
    for i in range(CHUNK // SUB):
        ref = b[SUB * i + SUB // 2:SUB * i + SUB // 2 + 1, :]
        eq = jnp.exp(b[SUB * i:SUB * (i + 1), :] - ref)
        ek = jnp.exp(jnp.where(row < SUB * (i + 1), ref - b, NEG_BIG))
        qi = q[SUB * i:SUB * (i + 1), :] * eq
        ki = kk * ek
        a_rows.append(_mm_nt(qi, ki, exact))
        qs.append(qi), ks.append(ki), eqs.append(eq), eks.append(ek)
    tt = lax.broadcasted_iota(jnp.int32, (CHUNK, CHUNK), 0)
    ss = lax.broadcasted_iota(jnp.int32, (CHUNK, CHUNK), 1)
    causal = ss <= tt
    a = jnp.where(causal, jnp.concatenate(a_rows, axis=0), 0.0)
    return a, causal, qs, ks, eqs, eks


def _hgrn_fwd(xph, lbl, tg=512):
    T = xph.shape[0]
    ng, ncg = T // tg, tg // CHUNK
    cols = [slice(HGRN_DIM * h, HGRN_DIM * (h + 1)) for h in range(HGRN_HEADS)]

    def body(lbl_ref, hq_ref, hf_ref, hi_ref, o_ref, st_ref, s_scr):
        @pl.when(pl.program_id(0) == 0)
        def _():
            s_scr[...] = jnp.zeros_like(s_scr)

        lb = _lb_from_logits(lbl_ref[...])

        def chunks(it, _):
            pre = []
            for k in range(HGRN_CPI):
                c = it * HGRN_CPI + k
                rows = pl.ds(pl.multiple_of(c * CHUNK, CHUNK), CHUNK)
                for cs in cols:
                    _, _, lf, kk, _, q = _hgrn_gates(hq_ref[rows, cs], hf_ref[rows, cs], lb[:, cs])
                    v = hi_ref[rows, cs]
                    b = _cumsum_rows(lf)
                    a = _hgrn_intra(q, kk, b)[0]
                    b_last = b[CHUNK - 1:CHUNK, :]
                    pre.append((c, rows, q * jnp.exp(b), a, v, jnp.exp(b_last), _mm_tn(v, kk * jnp.exp(b_last - b))))
            for i, (c, rows, qe, a, v, ebl, upd) in enumerate(pre):
                h = i % HGRN_HEADS
                st = s_scr[h]
                st_ref[h, c] = st
                o_ref[rows, cols[h]] = _mm_nt(qe, st) + _mm(a, v)
                s_scr[h] = st * ebl + upd
            return 0

        lax.fori_loop(0, ncg // HGRN_CPI, chunks, 0)

    col = lambda k: pl.BlockSpec((tg, HGRN_WIDTH), lambda g: (g, k))
    return pl.pallas_call(
        body, name="hgrn_fwd", grid=(ng,),
        in_specs=[pl.BlockSpec((2, HGRN_WIDTH), lambda g: (0, 0)), col(0), col(1), col(2)],
        out_specs=[col(0), pl.BlockSpec((HGRN_HEADS, ncg, HGRN_DIM, HGRN_DIM), lambda g: (0, g, 0, 0))],
        out_shape=[jax.ShapeDtypeStruct((T, HGRN_WIDTH), F32),
                   jax.ShapeDtypeStruct((HGRN_HEADS, T // CHUNK, HGRN_DIM, HGRN_DIM), F32)],
        scratch_shapes=[pltpu.VMEM((HGRN_HEADS, HGRN_DIM, HGRN_DIM), F32)],
        compiler_params=_params(("arbitrary",)),
    )(lbl, xph, xph, xph)


def _hgrn_bwd(xph, lbl, states, d_o, fill=(), tg=512):
    T = xph.shape[0]
    ng, ncg = T // tg, tg // CHUNK
    cols = [slice(HGRN_DIM * h, HGRN_DIM * (h + 1)) for h in range(HGRN_HEADS)]
    nsub = CHUNK // SUB
    nf = len(fill)

    def body(lbl_ref, hq_ref, hf_ref, hi_ref, st_ref, do_ref, *rest):
        dhq_ref, dhf_ref, dhi_ref, dlg_ref = rest[nf:nf + 4]
        ds_scr, dlb_scr = rest[2 * nf + 4:2 * nf + 6]
        fill_copies = lambda: _pair_fill_copies(rest[nf + 4:2 * nf + 4], *rest[2 * nf + 6:])
        g = pl.program_id(0)

        @pl.when(g == 0)
        def _():
            ds_scr[...] = jnp.zeros_like(ds_scr)
            dlb_scr[...] = jnp.zeros_like(dlb_scr)
            for cp in (fill_copies()[0] if nf else ()):
                cp.start()

        lb = _lb_from_logits(lbl_ref[...])

        def chunks(it, _):
            pre = []
            for k, h in ((k, h) for k in range(HGRN_CPI) for h in range(HGRN_HEADS)):
                cs = cols[h]
                c = ncg - 1 - (it * HGRN_CPI + k)
                rows = pl.ds(pl.multiple_of(c * CHUNK, CHUNK), CHUNK)
                hq = hq_ref[rows, cs]
                sig_f, f, lf, kk, sig_q, q = _hgrn_gates(hq, hf_ref[rows, cs], lb[:, cs])
                v = hi_ref[rows, cs]
                do = do_ref[rows, cs]
                b = _cumsum_rows(lf)
                eb = jnp.exp(b)
                a, causal, qs, ks, eqs, eks = _hgrn_intra(q, kk, b)
                b_last = b[CHUNK - 1:CHUNK, :]
                st = st_ref[h, c]
                pre.append(dict(h=h, cs=cs, rows=rows, hq=hq, sig_f=sig_f, f=f, kk=kk, sig_q=sig_q, q=q, v=v, eb=eb, qs=qs,
                                ks=ks, eqs=eqs,
                                eks=eks, ebl=jnp.exp(b_last), el=jnp.exp(b_last - b), st=st,
                                da=jnp.where(causal, _mm_nt(do, v, True), 0.0), dq=_mm(do, st, True) * eb,
                                dv=_mm_tn(a, do), dsu=_mm_tn(do, q * eb, True)))
            for w in pre:
                dq_rows = []
                dk = jnp.zeros_like(w["q"])
                for i in range(nsub):
                    dai = w["da"][SUB * i:SUB * (i + 1), :]
                    dq_rows.append(_mm(dai, w["ks"][i], True) * w["eqs"][i])
                    dk = dk + _mm_tn(dai, w["qs"][i], True) * w["eks"][i]
                w["dq"] = w["dq"] + jnp.concatenate(dq_rows, axis=0)
                w["dk"] = dk
            for w in pre:
                h, cs, rows = w["h"], w["cs"], w["rows"]
                kk, el, ebl, dst = w["kk"], w["el"], w["ebl"], ds_scr[h]
                dk_state = _mm(w["v"], dst, True) * el
                dk = w["dk"] + dk_state
                e_last = (ebl * jnp.sum(w["st"] * dst, axis=0, keepdims=True)
                          + jnp.sum(kk * dk_state, axis=0, keepdims=True))
                dlf = _rev_cumsum_rows(w["q"] * w["dq"] - kk * dk) + e_last
                ds_scr[h] = dst * ebl + w["dsu"]
                df = dlf / w["f"] - dk
                sig_f, sig_q = w["sig_f"], w["sig_q"]
                dhf_ref[rows, cs] = df * (1.0 - lb[:, cs]) * sig_f * (1.0 - sig_f)
                dlb_scr[:, cs] += jnp.sum(df * (1.0 - sig_f), axis=0, keepdims=True)
                dhq_ref[rows, cs] = w["dq"] * sig_q * (1.0 + w["hq"] * (1.0 - sig_q))
                dhi_ref[rows, cs] = w["dv"] + _mm_nt(kk * el, dst)
            return 0

        lax.fori_loop(0, ncg // HGRN_CPI, chunks, 0)

        @pl.when(g == ng - 1)
        def _():
            dl0 = dlb_scr[...] * lb * (1.0 - lb)
            dlg_ref[...] = jnp.concatenate([dl0, -dl0], axis=0)
            if nf:
                copies, waits = fill_copies()
                for w in waits:
                    w.wait_recv()
                for cp in copies:
                    cp.wait_send()

    col = lambda k: pl.BlockSpec((tg, HGRN_WIDTH), lambda g: (ng - 1 - g, k))
    logits = pl.BlockSpec((2, HGRN_WIDTH), lambda g: (0, 0))
    big = jax.ShapeDtypeStruct((T, HGRN_WIDTH), F32)
    n_in, n_out = 6, 4
    return pl.pallas_call(
        body, name="hgrn_bwd", grid=(ng,),
        in_specs=[logits, col(0), col(1), col(2),
                  pl.BlockSpec((HGRN_HEADS, ncg, HGRN_DIM, HGRN_DIM), lambda g: (0, ng - 1 - g, 0, 0)), col(0)] + [ANY] * nf,
        out_specs=[col(0), col(0), col(0), logits] + [ANY] * nf,
        out_shape=[big, big, big, jax.ShapeDtypeStruct((2, HGRN_WIDTH), F32)]
        + [jax.ShapeDtypeStruct(f.shape, f.dtype) for f in fill],
        input_output_aliases={n_in + k: n_out + k for k in range(nf)},
        scratch_shapes=[pltpu.VMEM((HGRN_HEADS, HGRN_DIM, HGRN_DIM), F32), pltpu.VMEM((1, HGRN_WIDTH), F32)]
        + ([pltpu.SemaphoreType.DMA((nf,)), pltpu.SemaphoreType.DMA((nf,))] if nf else []),
        compiler_params=_params(("arbitrary",)),
    )(lbl, xph, xph, xph, states, d_o, *fill)


def _proj_fwd(x, o_raw, oh_raw, xph, wout, w_mla, w_hg, w_post, w_fpre, tt=512):
    T = x.shape[0]

    def body(x_ref, o_ref, oh_ref, hg_ref, wout_ref, wmla_ref, whg_ref, wpost_ref, wfpre_ref,
             h1_ref, z_ref, mix_ref):
        om, _, _ = _grms_fwd(o_ref[...], wmla_ref[...], MLA_V)
        hg = hg_ref[...]
        ohn, _, _ = _grms_fwd(oh_ref[...], whg_ref[...], HGRN_DIM)
        mix = jnp.concatenate([om, ohn * (hg * jax.nn.sigmoid(hg))], axis=-1)
        mix_ref[...] = mix.astype(mix_ref.dtype)
        y1 = _mm(mix, wout_ref[...])
        h1 = x_ref[...] + _rms_fwd(y1, wpost_ref[...])[0]
        h1_ref[...] = h1
        z_ref[...] = _rms_fwd(h1, wfpre_ref[...])[0].astype(z_ref.dtype)

    row = lambda w: pl.BlockSpec((tt, w), lambda i: (i, 0))
    full = lambda a: pl.BlockSpec(a.shape, lambda i: (0,) * a.ndim)
    sds = jax.ShapeDtypeStruct
    return pl.pallas_call(
        body, name="proj_fwd", grid=(T // tt,),
        in_specs=[row(D_MODEL), row(MLA_WIDTH), row(HGRN_WIDTH), pl.BlockSpec((tt, HGRN_WIDTH), lambda i: (i, 3)),
                  full(wout), full(w_mla), full(w_hg), full(w_post), full(w_fpre)],
        out_specs=[row(D_MODEL)] * 3,
        out_shape=[sds((T, D_MODEL), F32), sds((T, D_MODEL), MXU_DTYPE), sds((T, D_MODEL), MXU_DTYPE)],
        compiler_params=_params(("arbitrary",)),
    )(x, o_raw, oh_raw, xph, wout, w_mla, w_hg, w_post, w_fpre)


def _ffn_fwd(zb, h1, tgt, w_fpost, wg, wu, wd, tt=256):
    T = zb.shape[0]
    nj = N_CHIPS

    def body(z_ref, h1_ref, tgt_ref, wfpost_ref, wg_ref, wu_ref, wd_ref, g_ref, up_ref, dy2_ref, dh2_ref, loss_ref, dwf_ref):
        @pl.when(pl.program_id(0) == 0)
        def _():
            loss_ref[...] = jnp.zeros_like(loss_ref)
            dwf_ref[...] = jnp.zeros_like(dwf_ref)

        z = z_ref[...]
        gs = [_mm_nt(z, wg_ref[j]) for j in range(nj)]
        ups = [_mm_nt(z, wu_ref[j]) for j in range(nj)]
        y2 = jnp.zeros((tt, D_MODEL), F32)
        for j in range(nj):
            g_ref[j] = gs[j]
            up_ref[j] = ups[j]
            y2 = y2 + _mm(gs[j] * jax.nn.sigmoid(gs[j]) * ups[j], wd_ref[j])
        w = wfpost_ref[...]
        y2s, y2n, r2 = _rms_fwd(y2, w)
        e = h1_ref[...] + y2s - tgt_ref[...]
        loss_ref[...] += jnp.sum(e * e, axis=0, keepdims=True)
        dh2 = e * (1.0 / D_MODEL)
        dh2_ref[...] = dh2
        dy2, dwf = _rms_bwd(dh2, y2n, r2, w)
        dy2_ref[...] = dy2.astype(dy2_ref.dtype)
        dwf_ref[...] += dwf

    row = pl.BlockSpec((tt, D_MODEL), lambda i: (i, 0))
    vec = pl.BlockSpec((1, D_MODEL), lambda i: (0, 0))
    resident = pl.BlockSpec((nj, FF_SHARD, D_MODEL), lambda i: (0, 0, 0), pipeline_mode=pl.Buffered(1))
    act = pl.BlockSpec((nj, tt, FF_SHARD), lambda i: (0, i, 0))
    sds = jax.ShapeDtypeStruct
    return pl.pallas_call(
        body, name="ffn_fwd", grid=(T // tt,),
        in_specs=[row, row, row, vec, resident, resident, resident],
        out_specs=[act, act, row, row, vec, vec],
        out_shape=[sds((nj, T, FF_SHARD), F32), sds((nj, T, FF_SHARD), F32), sds((T, D_MODEL), MXU_DTYPE),
                   sds((T, D_MODEL), F32), sds((1, D_MODEL), F32), sds((1, D_MODEL), F32)],
        compiler_params=_params(("arbitrary",)),
    )(zb, h1, tgt, w_fpost, wg, wu, wd)


def _ffn_bwd(zb, g, up, dy2b, wg, wu, wd, tt=512):
    T = zb.shape[0]
    nj = N_CHIPS

    def body(z_ref, g_ref, up_ref, dy2_ref, wg_ref, wu_ref, wd_ref, dwg_ref, dwu_ref, dwd_ref, dz_ref, acc_ref):
        j, i = pl.program_id(0), pl.program_id(1)
        rows = pl.ds(pl.multiple_of(i * tt, tt), tt)

        @pl.when(i == 0)
        def _():
            dwg_ref[...] = jnp.zeros_like(dwg_ref)
            dwu_ref[...] = jnp.zeros_like(dwu_ref)
            dwd_ref[...] = jnp.zeros_like(dwd_ref)

        z, g_, up_, dy2 = z_ref[...], g_ref[0], up_ref[0], dy2_ref[...]
        sg = jax.nn.sigmoid(g_)
        act = g_ * sg
        dff = _mm_nt(dy2, wd_ref[0])
        dwd_ref[0] += _mm_tn(act * up_, dy2)
        dg = dff * up_ * sg * (1.0 + g_ * (1.0 - sg))
        dup = dff * act
        dwg_ref[0] += _mm_tn(dg, z)
        dwu_ref[0] += _mm_tn(dup, z)
        dz = _mm(dg, wg_ref[0]) + _mm(dup, wu_ref[0])

        @pl.when(j == 0)
        def _():
            acc_ref[rows, :] = dz

        @pl.when((j > 0) & (j < nj - 1))
        def _():
            acc_ref[rows, :] += dz

        @pl.when(j == nj - 1)
        def _():
            dz_ref[...] = acc_ref[rows, :] + dz

    row = pl.BlockSpec((tt, D_MODEL), lambda j, i: (i, 0))
    act = pl.BlockSpec((1, tt, FF_SHARD), lambda j, i: (j, i, 0))
    w_sh = pl.BlockSpec((1, FF_SHARD, D_MODEL), lambda j, i: (j, 0, 0))
    w_grad = jax.ShapeDtypeStruct((nj, FF_SHARD, D_MODEL), F32)
    return pl.pallas_call(
        body, name="ffn_bwd", grid=(nj, T // tt),
        in_specs=[row, act, act, row, w_sh, w_sh, w_sh],
        out_specs=[w_sh, w_sh, w_sh, pl.BlockSpec((tt, D_MODEL), lambda j, i: (jnp.where(j == nj - 1, i, 0), 0))],
        out_shape=[w_grad, w_grad, w_grad, jax.ShapeDtypeStruct((T, D_MODEL), F32)],
        scratch_shapes=[pltpu.VMEM((T, D_MODEL), F32)],
        compiler_params=_params(("arbitrary", "arbitrary"), vmem=FFN_BWD_VMEM),
    )(zb, g, up, dy2b, wg, wu, wd)


def _mid_bwd(dz, dh2, h1, mixb, o_raw, oh_raw, xph, wout, w_fpre, w_post, w_mla, w_hg, swap=(), tt=512):
    T = dh2.shape[0]
    nsw = len(swap)
    n_in, n_out = 12, 10

    def body(*refs):
        (dz_ref, dh2_ref, h1_ref, mix_ref, o_ref, oh_ref, hg_ref, wout_ref, wfpre_ref, wpost_ref,
         wmla_ref, whg_ref) = refs[:n_in]
        (dh1_ref, dwout_ref, do_ref, doh_ref, dhg_ref, dvec_ref, dwfpre_ref, dwpost_ref, dwmla_ref,
         dwhg_ref) = refs[n_in + nsw:n_in + nsw + n_out]
        sems = refs[n_in + 2 * nsw + n_out + 1:]
        swap_copies = lambda: _pair_swap_copies(refs[n_in:n_in + nsw], refs[n_in + nsw + n_out:n_in + 2 * nsw + n_out],
                                                *sems)

        def wout_copies():
            _, _, c, _, sib, _ = _place()
            rw_ref, h = refs[n_in + 2 * nsw + n_out], D_MODEL // N_CHIPS // 2
            return [_rcopy(dwout_ref.at[pl.ds(pl.multiple_of((2 * k + 1 - c) * h, 8), h)], rw_ref.at[k], *sems, nsw + k, sib)
                    for k in range(N_CHIPS)]

        @pl.when(pl.program_id(0) == 0)
        def _():
            for r in (dwout_ref, dwfpre_ref, dwpost_ref, dwmla_ref, dwhg_ref):
                r[...] = jnp.zeros_like(r)
            for cp in (swap_copies() if nsw else ()):
                cp.start()

        dz = dz_ref[...]
        wfpre = wfpre_ref[...]
        _, h1n, r = _rms_fwd(h1_ref[...], wfpre)
        dh1_z, dwfpre = _rms_bwd(dz, h1n, r, wfpre)
        dwfpre_ref[...] += dwfpre
        dh1 = dh2_ref[...] + dh1_z
        dh1_ref[...] = dh1
        wpost = wpost_ref[...]
        _, y1n, r1 = _rms_fwd(_mm(mix_ref[...], wout_ref[...]), wpost)
        dy1, dwpost = _rms_bwd(dh1, y1n, r1, wpost)
        dwpost_ref[...] += dwpost
        dmix = _mm_nt(dy1, wout_ref[...])
        dwout_ref[...] += _mm_tn(mix_ref[...], dy1)
        wmla = wmla_ref[...]
        o = o_ref[...]
        _, on, ro = _grms_fwd(o, wmla, MLA_V)
        d_o, dwmla = _grms_bwd(dmix[:, :MLA_WIDTH], on, ro, wmla, MLA_V)
        dwmla_ref[...] += dwmla
        do_ref[...] = d_o.astype(do_ref.dtype)
        hh = lax.broadcasted_iota(jnp.int32, (MLA_HEADS, MLA_WIDTH), 0)
        ll = lax.broadcasted_iota(jnp.int32, (MLA_HEADS, MLA_WIDTH), 1)
        sel = jnp.where((ll >= hh * MLA_V) & (ll < (hh + 1) * MLA_V), 1.0, 0.0)
        dvec_ref[...] = _mm_nt(sel, d_o * o, True)
        whg = whg_ref[...]
        hg = hg_ref[...]
        sg = jax.nn.sigmoid(hg)
        _, ohn, rh = _grms_fwd(oh_ref[...], whg, HGRN_DIM)
        dmh = dmix[:, MLA_WIDTH:]
        dhg_ref[...] = dmh * ohn * whg * sg * (1.0 + hg * (1.0 - sg))
        d_oh, dwhg = _grms_bwd(dmh * (hg * sg), ohn, rh, whg, HGRN_DIM)
        dwhg_ref[...] += dwhg
        doh_ref[...] = d_oh

        if nsw:
            @pl.when(pl.program_id(0) == T // tt - 1)
            def _():
                for cp in wout_copies():
                    cp.start()
                for cp in swap_copies() + wout_copies():
                    cp.wait()

    row = lambda w: pl.BlockSpec((tt, w), lambda i: (i, 0))
    full = lambda a: pl.BlockSpec(a.shape, lambda i: (0,) * a.ndim)
    vec = lambda w: pl.BlockSpec((1, w), lambda i: (0, 0))
    sds = jax.ShapeDtypeStruct
    return pl.pallas_call(
        body, name="mid_bwd", grid=(T // tt,),
        in_specs=[row(D_MODEL), row(D_MODEL), row(D_MODEL),
                  row(D_MODEL), row(MLA_WIDTH), row(HGRN_WIDTH), pl.BlockSpec((tt, HGRN_WIDTH), lambda i: (i, 3)),
                  full(wout), vec(D_MODEL), vec(D_MODEL), vec(MLA_WIDTH), vec(HGRN_WIDTH)] + [ANY] * nsw,
        out_specs=[row(D_MODEL), full(wout), row(MLA_WIDTH), row(HGRN_WIDTH), row(HGRN_WIDTH),
                   pl.BlockSpec((MLA_HEADS, tt), lambda i: (0, i)),
                   vec(D_MODEL), vec(D_MODEL), vec(MLA_WIDTH), vec(HGRN_WIDTH)] + [ANY] * (nsw + 1 if nsw else 0),
        out_shape=[sds((T, D_MODEL), F32), sds(wout.shape, F32), sds((T, MLA_WIDTH), MXU_DTYPE), sds((T, HGRN_WIDTH), F32),
                   sds((T, HGRN_WIDTH), F32), sds((MLA_HEADS, T), F32),
                   sds((1, D_MODEL), F32), sds((1, D_MODEL), F32), sds((1, MLA_WIDTH), F32), sds((1, HGRN_WIDTH), F32)]
        + (_half_stack_shapes(list(swap) + [sds((N_CHIPS, D_MODEL // N_CHIPS, D_MODEL), F32)]) if nsw else []),
        scratch_shapes=[pltpu.SemaphoreType.DMA((nsw + N_CHIPS,)), pltpu.SemaphoreType.DMA((nsw + N_CHIPS,))] if nsw else [],
        compiler_params=_params(("arbitrary",)),
    )(dz, dh2, h1, mixb, o_raw, oh_raw, xph, wout, w_fpre, w_post, w_mla, w_hg, *swap)


def _in_bwd(x, dh1, cq, ckv, dq, dk, dv, dhq, dhf, dhi, dhg, rc, rs, w_pre, win, qnw, wq, kvnw, wk, wv, tt=256):
    T = x.shape[0]

    def body(x_ref, dh1_ref, cq_ref, ckv_ref, dq_ref, dk_ref, dv_ref, dhq_ref, dhf_ref, dhi_ref, dhg_ref, rc_ref, rs_ref,
             wpre_ref, win_ref, qnw_ref, wq_ref, kvnw_ref, wk_ref, wv_ref,
             dx_ref, dwin_ref, dwq_ref, dwk_ref, dwv_ref, dwpre_ref, dqnw_ref, dkvnw_ref):
        @pl.when(pl.program_id(0) == 0)
        def _():
            for r in (dwin_ref, dwq_ref, dwk_ref, dwv_ref, dwpre_ref, dqnw_ref, dkvnw_ref):
                r[...] = jnp.zeros_like(r)

        def add_win_grad(r, first):
            for arr0, n, chip, row0 in _win_grad_segments():
                if first <= arr0 and arr0 + n <= first + r.shape[0]:
                    dwin_ref[chip, row0:row0 + n, :] += r[arr0 - first:arr0 - first + n]

        lo = Q_RANK + KV_RANK + HEAD_PAD
        dxp_h = jnp.concatenate([dhq_ref[...], dhf_ref[...], dhi_ref[...], dhg_ref[...]], axis=-1)
        du = _mm(dxp_h, win_ref[lo:, :])
        wpre = wpre_ref[...]
        u, xn, rx = _rms_fwd(x_ref[...], wpre)
        add_win_grad(_mm_tn(dxp_h, u), lo)
        c, sa, sb = _rope_tables(rc_ref[...], rs_ref[...])
        lane = lax.broadcasted_iota(jnp.int32, (tt, HEAD_PAD), 1)
        dk_all = dk_ref[...]
        dq_lin = []
        dkr = jnp.zeros((tt, HEAD_PAD), F32)
        for h in range(MLA_HEADS):
            sl = slice(HEAD_PAD * h, HEAD_PAD * (h + 1))
            dq_lin.append(_rope_bwd(dq_ref[sl, :].T * ATTN_SCALE, c, sa, sb))
            dkr = dkr + dk_all[:, sl]
        dq_lin = jnp.concatenate(dq_lin, axis=-1)
        dkr = jnp.where((lane >= MLA_NOPE) & (lane < MLA_QK), _rope_bwd(dkr, c, sa, sb), 0.0)
        qnw = qnw_ref[...]
        qn, cqn, rq = _rms_fwd(cq_ref[...], qnw)
        dwq_ref[...] += _mm_tn(qn, dq_lin)
        dcq, dqnw = _rms_bwd(_mm_nt(dq_lin, wq_ref[...]), cqn, rq, qnw)
        dqnw_ref[...] += dqnw
        kvnw = kvnw_ref[...]
        kvn, ckvn, rkv = _rms_fwd(ckv_ref[...], kvnw)
        dv_ = dv_ref[...]
        dwk_ref[...] += _mm_tn(kvn, dk_all)
        dwv_ref[...] += _mm_tn(kvn, dv_)
        dckv, dkvnw = _rms_bwd(_mm_nt(dk_all, wk_ref[...]) + _mm_nt(dv_, wv_ref[...]), ckvn, rkv, kvnw)
        dkvnw_ref[...] += dkvnw
        dxp_a = jnp.concatenate([dcq, dckv, dkr], axis=-1)
        add_win_grad(_mm_tn(dxp_a, u), 0)
        dx_u, dwpre = _rms_bwd(du + _mm(dxp_a, win_ref[:lo, :]), xn, rx, wpre)
        dwpre_ref[...] += dwpre
        dx_ref[...] = dh1_ref[...] + dx_u

    row = lambda w: pl.BlockSpec((tt, w), lambda i: (i, 0))
    full = lambda a: pl.BlockSpec(a.shape, lambda i: (0,) * a.ndim)
    sds = jax.ShapeDtypeStruct
    qk_w = MLA_HEADS * HEAD_PAD
    return pl.pallas_call(
        body, name="in_bwd", grid=(T // tt,),
        in_specs=[row(D_MODEL), row(D_MODEL), row(Q_RANK), row(KV_RANK), pl.BlockSpec((qk_w, tt), lambda i: (0, i)),
                  row(qk_w), row(MLA_WIDTH),
                  row(HGRN_WIDTH), row(HGRN_WIDTH), row(HGRN_WIDTH), row(HGRN_WIDTH), row(HEAD_PAD), row(HEAD_PAD),
                  full(w_pre), full(win), full(qnw), full(wq), full(kvnw), full(wk), full(wv)],
        out_specs=[row(D_MODEL), pl.BlockSpec(WIN_COMM_SHAPE, lambda i: (0, 0, 0)), full(wq), full(wk), full(wv),
                   full(w_pre), full(qnw), full(kvnw)],
        out_shape=[sds((T, D_MODEL), F32), sds(WIN_COMM_SHAPE, F32), sds(wq.shape, F32), sds(wk.shape, F32),
                   sds(wv.shape, F32), sds(w_pre.shape, F32), sds(qnw.shape, F32), sds(kvnw.shape, F32)],
        compiler_params=_params(("arbitrary",)),
    )(x, dh1, cq, ckv, dq, dk, dv, dhq, dhf, dhi, dhg, rc, rs, w_pre, win, qnw, wq, kvnw, wk, wv)


def _arrange_weights(win_t, wuq_full, wukv):
    dt = win_t.dtype
    z = lambda n: jnp.zeros((n, D_MODEL), dt)
    s2 = Q_RANK + KV_RANK
    win_arr = jnp.concatenate([win_t[:s2], z(MLA_NOPE), win_t[s2:s2 + MLA_ROPE], z(HEAD_PAD - MLA_QK),
                               win_t[s2 + MLA_ROPE:]], axis=0)
    wq_arr = jnp.pad(wuq_full, ((0, 0), (0, 0), (0, HEAD_PAD - MLA_QK))).reshape(Q_RANK, MLA_HEADS * HEAD_PAD)
    wk_arr = jnp.pad(wukv[:, :, :MLA_NOPE], ((0, 0), (0, 0), (0, HEAD_PAD - MLA_NOPE))).reshape(
        KV_RANK, MLA_HEADS * HEAD_PAD)
    wv_arr = wukv[:, :, MLA_NOPE:].reshape(KV_RANK, MLA_WIDTH)
    return win_arr, wq_arr, wk_arr, wv_arr


WIN_COMM_SHAPE = (N_CHIPS, -(-D_IN // N_CHIPS // 32) * 32, D_MODEL)


def _win_grad_segments():
    s2 = Q_RANK + KV_RANK
    runs = [(0, s2, 0), (s2, s2 + MLA_ROPE, MLA_NOPE), (s2 + MLA_ROPE, D_IN, HEAD_PAD - MLA_ROPE)]
    per = D_IN // N_CHIPS
    segs = []
    for lo, hi, shift in runs:
        for k in range(N_CHIPS):
            a, b = max(lo, per * k), min(hi, per * (k + 1))
            if a < b:
                segs.append((a + shift, b - a, k, a - per * k))
    return segs


def _unarrange_grads(dwq_arr, dwk_arr, dwv_arr):
    dwuq = dwq_arr.reshape(Q_RANK, MLA_HEADS, HEAD_PAD)[:, :, :MLA_QK]
    dwukv = jnp.concatenate([dwk_arr.reshape(KV_RANK, MLA_HEADS, HEAD_PAD)[:, :, :MLA_NOPE],
                             dwv_arr.reshape(KV_RANK, MLA_HEADS, MLA_V)], axis=-1)
    return dwuq, dwukv


def _rope_inv_freq():
    inv = 1.0 / (ROPE_THETA ** (jnp.arange(0, MLA_ROPE, 2, dtype=F32) / MLA_ROPE))
    z = lambda n: jnp.zeros((n,), F32)
    return jnp.concatenate([z(MLA_NOPE), inv, inv, z(HEAD_PAD - MLA_QK)]).reshape(1, HEAD_PAD)


def _local_step(x, pos, tgt, small, win_arr, wq_arr, wk_arr, wv_arr, late, place=None):
    invf = _rope_inv_freq()
    cq, ckv, xph, qb, kb, vb, kt, vt, rc, rs = _in_fwd(x, pos, invf, small["attn_pre_norm"], win_arr, small["mla_q_norm"],
                                               wq_arr, small["mla_kv_norm"], wk_arr, wv_arr)
    if place is None:
        o_raw, lse = _attn_fwd_t(qb, kb, vt)
        wout, wg, wu, wd = late
    else:
        o_raw, lse, *stacks = _attn_fwd_t(qb, kb, vt, gather=late)
        wout, wg, wu, wd = [lax.dynamic_update_slice(s, l[None], (place[1], 0, 0)) for s, l in zip(stacks, late)]
        wout = wout.reshape(D_MODEL, D_MODEL)
    oh_raw, states = _hgrn_fwd(xph, small["hgrn_lb_logits"])
    h1, zb, mixb = _proj_fwd(x, o_raw, oh_raw, xph, wout, small["mla_out_norm"], small["hgrn_out_norm"],
                                 small["attn_post_norm"], small["ffn_pre_norm"])
    g, up, dy2b, dh2, loss_acc, d_fpost = _ffn_fwd(zb, h1, tgt, small["ffn_post_norm"], wg, wu, wd)
    dwg, dwu, dwd, dz = _ffn_bwd(zb, g, up, dy2b, wg, wu, wd)
    ffn_grads = [] if place is None else [dwg, dwu, dwd]
    dh1, dwout, d_o, d_oh, dhg, dvec, d_fpre, d_post, d_mla, d_hg, *ffn_rs = _mid_bwd(
        dz, dh2, h1, mixb, o_raw, oh_raw, xph, wout, small["ffn_pre_norm"], small["attn_post_norm"],
        small["mla_out_norm"], small["hgrn_out_norm"], swap=ffn_grads)
    if ffn_grads:
        ffn_grads = ffn_grads + [dwout.reshape(N_CHIPS, D_MODEL // N_CHIPS, D_MODEL)]
    ffn_ps = _pair_sum(place, ffn_grads, ffn_rs, name="pair_sum_ffn") if ffn_grads else []
    dq, dk, dv, *ffn_ris = _attn_bwd_t(qb, kb, kt, vb, d_o, lse, dvec.reshape(lse.shape), send=ffn_ps)
    ffn_sums = _chip_sum(place, ffn_grads, ffn_rs, ffn_ris, name="chip_sum_ffn") if ffn_grads else []
    dhq, dhf, dhi, d_lbl, *ffn_final = _hgrn_bwd(xph, small["hgrn_lb_logits"], states, d_oh, fill=ffn_sums)
    dx, dwin4, dwq_arr, dwk_arr, dwv_arr, d_pre, d_qn, d_kvn = _in_bwd(
        x, dh1, cq, ckv, dq, dk, dv, dhq, dhf, dhi, dhg, rc, rs, small["attn_pre_norm"], win_arr,
        small["mla_q_norm"], wq_arr, small["mla_kv_norm"], wk_arr, wv_arr)
    dwuq, dwukv = _unarrange_grads(dwq_arr, dwk_arr, dwv_arr)
    loss = 0.5 * jnp.sum(loss_acc) * (1.0 / D_MODEL)
    grads = dict(attn_pre_norm=d_pre, w_in=dwin4, mla_q_norm=d_qn, mla_w_uq=dwuq, mla_kv_norm=d_kvn, mla_w_ukv=dwukv,
                 mla_out_norm=d_mla, hgrn_lb_logits=d_lbl, hgrn_out_norm=d_hg, w_out=dwout, attn_post_norm=d_post,
                 ffn_pre_norm=d_fpre, w_gate=dwg, w_up=dwu, w_down=dwd, ffn_post_norm=d_fpost)
    if place is None:
        return loss, dx, grads
    return loss, dx, grads, ffn_final


def _place():
    x, y, c = lax.axis_index("x"), lax.axis_index("y"), lax.axis_index("c")
    others = [(1 - x, y), (x, 1 - y), (1 - x, 1 - y)]
    return x, y, c, 2 * x + y, (x, y, 1 - c), others


def _half(ref, c, rows):
    return ref.at[pl.ds(pl.multiple_of(c * rows, 8), rows)]


def _rcopy(src, dst, send, recv, k, to):
    return pltpu.make_async_remote_copy(src_ref=src, dst_ref=dst, send_sem=send.at[k], recv_sem=recv.at[k],
                                        device_id=to, device_id_type=MESH)


class _Gather:
    def __init__(self, ins, outs, send, recv):
        self.ins, self.outs, self.send, self.recv = ins, outs, send, recv
        self.n = len(ins)
        self.halves = [r.shape[0] // 2 for r in ins]
        _, _, self.c, self.me, self.sib, self.others = _place()

    def _each(self):
        for j, (px, py) in enumerate(self.others):
            for a in range(self.n):
                yield j * self.n + a, a, 2 * px + py, (px, py, self.c)

    def sends(self):
        return [_rcopy(_half(self.ins[a], self.c, self.halves[a]), _half(self.outs[a].at[self.me], self.c, self.halves[a]),
                       self.send, self.recv, k, to) for k, a, _, to in self._each()]

    def arrivals(self):
        parts = [(k, _half(self.outs[a].at[chip], self.c, self.halves[a]), to) for k, a, chip, to in self._each()]
        return [_rcopy(p, p, self.send, self.recv, k, to) for k, p, to in parts]

    def forwards(self):
        parts = [(k, _half(self.outs[a].at[chip], self.c, self.halves[a])) for k, a, chip, _ in self._each()]
        return [_rcopy(p, p, self.send, self.recv, 3 * self.n + k, self.sib) for k, p in parts]

    def forward_arrivals(self):
        parts = [(k, _half(self.outs[a].at[chip], 1 - self.c, self.halves[a])) for k, a, chip, _ in self._each()]
        return [_rcopy(p, p, self.send, self.recv, 3 * self.n + k, self.sib) for k, p in parts]

    @staticmethod
    def out_shapes(arrs):
        return [jax.ShapeDtypeStruct((N_CHIPS,) + a.shape, a.dtype) for a in arrs]

    @staticmethod
    def semaphores(arrs):
        return [pltpu.SemaphoreType.DMA((6 * len(arrs),)), pltpu.SemaphoreType.DMA((6 * len(arrs),))]


def _gather_chips(arrs, name):
    n = len(arrs)

    def body(*refs):
        gat = _Gather(refs[:n], refs[n:2 * n], *refs[2 * n:])
        sends, forwards = gat.sends(), gat.forwards()
        for cp in sends:
            cp.start()
        for arrival, fw in zip(gat.arrivals(), forwards):
            arrival.wait_recv()
            fw.start()
        for arrival in gat.forward_arrivals():
            arrival.wait_recv()
        for cp in sends + forwards:
            cp.wait_send()

    return pl.pallas_call(body, name=name, in_specs=[ANY] * n, out_specs=[ANY] * n, out_shape=_Gather.out_shapes(arrs),
                          scratch_shapes=_Gather.semaphores(arrs))(*arrs)


def _grad_blocks(gs):
    return 2 if all(g.shape[1] // 2 % 32 == 0 for g in gs) else 1


def _pair_swap_copies(g_refs, r_refs, send, recv):
    _, _, c, _, sib, _ = _place()
    copies = []
    for a, (g, r) in enumerate(zip(g_refs, r_refs)):
        h = g.shape[1] // 2
        copies.append(_rcopy(g.at[:, pl.ds(pl.multiple_of((1 - c) * h, 8), h)], r, send, recv, a, sib))
    return copies


def _half_stack_shapes(gs, dtype=None):
    return [jax.ShapeDtypeStruct((N_CHIPS, g.shape[1] // 2, g.shape[2]), dtype or g.dtype) for g in gs]


def _pair_swap(gs, wholes):
    n, nw = len(gs), len(wholes)

    def body(*refs):
        ins, outs, (send, recv) = refs[:n + nw], refs[n + nw:2 * (n + nw)], refs[2 * (n + nw):]
        copies = _pair_swap_copies(ins[:n], outs[:n], send, recv)
        copies += [_rcopy(ins[n + k], outs[n + k], send, recv, n + k, _place()[4]) for k in range(nw)]
        for cp in copies:
            cp.start()
        for cp in copies:
            cp.wait()

    return pl.pallas_call(
        body, name="pair_swap", in_specs=[ANY] * (n + nw), out_specs=[ANY] * (n + nw),
        out_shape=_half_stack_shapes(gs) + [jax.ShapeDtypeStruct(w.shape, w.dtype) for w in wholes],
        scratch_shapes=[pltpu.SemaphoreType.DMA((n + nw,)), pltpu.SemaphoreType.DMA((n + nw,))],
    )(*gs, *wholes)


def _pair_sum(place, gs, rs, small=None, name="pair_sum"):
    n = len(gs)
    nb = _grad_blocks(gs)

    def body(place_ref, *refs):
        g_refs, r_refs, p_refs = refs[:n], refs[n:2 * n], refs[-n - 1:-1] if small else refs[-n:]
        for a in range(n):
            p_refs[a][0] = (g_refs[a][0] + r_refs[a][0]).astype(p_refs[a].dtype)
        if small:
            @pl.when((pl.program_id(0) == 0) & (pl.program_id(1) == 0))
            def _():
                refs[-1][...] = refs[2 * n][...] + refs[2 * n + 1][...]

    chip = lambda k, p: lax.rem(p[1] + 1 + k, N_CHIPS)
    in_specs, out_specs = [], []
    for g in gs:
        blk = (1, g.shape[1] // 2 // nb, g.shape[2])
        in_specs.append(pl.BlockSpec(blk, lambda i, k, p: (chip(k, p), p[0] * nb + i, 0)))
    for g in gs:
        blk = (1, g.shape[1] // 2 // nb, g.shape[2])
        in_specs.append(pl.BlockSpec(blk, lambda i, k, p: (chip(k, p), i, 0)))
        out_specs.append(pl.BlockSpec(blk, lambda i, k, p: (chip(k, p), i, 0)))
    out_shape = _half_stack_shapes(gs, BF16)
    if small:
        sm_spec = pl.BlockSpec(small[0].shape, lambda i, k, p: (0, 0))
        in_specs += [sm_spec, sm_spec]
        out_specs.append(sm_spec)
        out_shape.append(jax.ShapeDtypeStruct(small[0].shape, F32))
    return pl.pallas_call(
        body, name=name,
        grid_spec=pltpu.PrefetchScalarGridSpec(num_scalar_prefetch=1, grid=(nb, N_CHIPS - 1), in_specs=in_specs,
                                               out_specs=out_specs),
        out_shape=out_shape,
        compiler_params=_params(("arbitrary", "arbitrary")),
    )(place, *gs, *rs, *(small or ()))


def _chip_swap_copies(p_refs, ri_refs, send, recv):
    _, _, c, _, _, others = _place()
    n = len(p_refs)
    return [_rcopy(p_refs[a].at[2 * px + py], ri_refs[a].at[j], send, recv, j * n + a, (px, py, c))
            for j, (px, py) in enumerate(others) for a in range(n)]


def _chip_swap_shapes(ps):
    return [jax.ShapeDtypeStruct((3,) + p.shape[1:], p.dtype) for p in ps]


def _chip_swap(ps, pair):
    n = len(ps)

    def body(*refs):
        start, finish = _chip_swap_plan(refs[:n], refs[n], refs[n + 1:2 * n + 1], refs[2 * n + 1], *refs[2 * n + 2:])
        start()
        finish()

    return pl.pallas_call(
        body, name="chip_swap", in_specs=[ANY] * (n + 1), out_specs=[ANY] * (n + 1),
        out_shape=_chip_swap_out_shapes(ps, pair), scratch_shapes=_chip_swap_semaphores(n),
    )(*ps, pair)


def _chip_swap_plan(p_refs, pair_ref, ri_refs, sm4_ref, send, recv, lsem):
    n = len(p_refs)
    hs = SMALL_ROWS // 2
    x, y, c, me, sib, others = _place()
    local = pltpu.make_async_copy(pair_ref, sm4_ref.at[me], lsem.at[0])
    copies = _chip_swap_copies(p_refs, ri_refs, send, recv)
    arrivals = list(copies)
    for j, (px, py) in enumerate(others):
        copies.append(_rcopy(_half(pair_ref, c, hs), _half(sm4_ref.at[me], c, hs), send, recv, 3 * n + j, (px, py, c)))
        part = _half(sm4_ref.at[2 * px + py], c, hs)
        arrivals.append(_rcopy(part, part, send, recv, 3 * n + j, (px, py, c)))

    def start():
        local.start()
        for cp in copies:
            cp.start()

    def finish():
        for arrival in arrivals:
            arrival.wait_recv()
        for cp in copies:
            cp.wait_send()
        local.wait()

    return start, finish


def _chip_swap_out_shapes(ps, pair):
    return _chip_swap_shapes(ps) + [jax.ShapeDtypeStruct((N_CHIPS,) + pair.shape, pair.dtype)]


def _chip_swap_semaphores(n):
    k = 3 * (n + 1)
    return [pltpu.SemaphoreType.DMA((k,)), pltpu.SemaphoreType.DMA((k,)), pltpu.SemaphoreType.DMA((1,))]


def _chip_sum(place, gs, rs, ris, name="chip_sum"):
    n = len(gs)
    nb = 1

    def body(place_ref, *refs):
        g_refs, r_refs, ri_refs, o_refs = refs[:n], refs[n:2 * n], refs[2 * n:3 * n], refs[3 * n:]
        for a in range(n):
            ri = ri_refs[a]
            o_refs[a][...] = (g_refs[a][0] + r_refs[a][0]) + ri[0].astype(F32) + ri[1].astype(F32) + ri[2].astype(F32)

    in_specs, out_specs, out_shape = [], [], []
    for g in gs:
        blk = (1, g.shape[1] // 2 // nb, g.shape[2])
        in_specs.append(pl.BlockSpec(blk, lambda i, p: (p[1], p[0] * nb + i, 0)))
    for g in gs:
        blk = (1, g.shape[1] // 2 // nb, g.shape[2])
        in_specs.append(pl.BlockSpec(blk, lambda i, p: (p[1], i, 0)))
    for g in gs:
        rb = g.shape[1] // 2 // nb
        in_specs.append(pl.BlockSpec((3, rb, g.shape[2]), lambda i, p: (0, i, 0)))
        out_specs.append(pl.BlockSpec((rb, g.shape[2]), lambda i, p: (p[0] * nb + i, 0)))
        out_shape.append(jax.ShapeDtypeStruct(g.shape[1:], F32))
    return pl.pallas_call(
        body, name=name,
        grid_spec=pltpu.PrefetchScalarGridSpec(num_scalar_prefetch=1, grid=(nb,), in_specs=in_specs, out_specs=out_specs),
        out_shape=out_shape,
        compiler_params=_params(("arbitrary",)),
    )(place, *gs, *rs, *ris)


def _pair_fill_copies(g_refs, send, recv):
    _, _, c, _, sib, _ = _place()
    copies, waits = [], []
    for a, g in enumerate(g_refs):
        h = g.shape[0] // 2
        mine, theirs = _half(g, c, h), _half(g, 1 - c, h)
        copies.append(_rcopy(mine, mine, send, recv, a, sib))
        waits.append(_rcopy(theirs, theirs, send, recv, a, sib))
    return copies, waits


def _pair_fill(gfs, sm4):
    n = len(gfs)
    hs = SMALL_ROWS // 2

    def body(*refs):
        g_refs, sm4_ref = refs[n + 1:2 * n + 1], refs[2 * n + 1]
        send, recv = refs[2 * n + 2:]
        x, y, c, me, sib, others = _place()
        copies, waits = _pair_fill_copies(g_refs, send, recv)
        for j, (px, py) in enumerate(others):
            chip = 2 * px + py
            mine, theirs = _half(sm4_ref.at[chip], c, hs), _half(sm4_ref.at[chip], 1 - c, hs)
            copies.append(pltpu.make_async_remote_copy(src_ref=mine, dst_ref=mine, send_sem=send.at[n + j],
                                                       recv_sem=recv.at[n + j], device_id=sib, device_id_type=MESH))
            waits.append(pltpu.make_async_remote_copy(src_ref=theirs, dst_ref=theirs, send_sem=send.at[n + j],
                                                      recv_sem=recv.at[n + j], device_id=sib, device_id_type=MESH))
        for cp in copies:
            cp.start()
        for w in waits:
            w.wait_recv()
        for cp in copies:
            cp.wait_send()

    return pl.pallas_call(
        body, name="pair_fill", in_specs=[ANY] * (n + 1), out_specs=[ANY] * (n + 1),
        out_shape=[jax.ShapeDtypeStruct(g.shape, g.dtype) for g in gfs] + [jax.ShapeDtypeStruct(sm4.shape, sm4.dtype)],
        input_output_aliases={i: i for i in range(n + 1)},
        scratch_shapes=[pltpu.SemaphoreType.DMA((n + 3,)), pltpu.SemaphoreType.DMA((n + 3,))],
    )(*gfs, sm4)


def _adamw_math(w, g, m, v):
    m = ADAM_B1 * m + (1.0 - ADAM_B1) * g
    v = ADAM_B2 * v + (1.0 - ADAM_B2) * (g * g)
    m_hat = m / (1.0 - ADAM_B1 ** ADAM_STEP)
    v_hat = v / (1.0 - ADAM_B2 ** ADAM_STEP)
    return -ADAM_LR * (m_hat / (jnp.sqrt(v_hat) + ADAM_EPS) + ADAM_WD * w), m, v


def _adamw(items, steps, name):
    n = len(items)

    def body(*refs):
        for a in range(n):
            g = refs[4 * a + 1][...]
            d, mo, vo = _adamw_math(refs[4 * a][...], g, refs[4 * a + 2][...], refs[4 * a + 3][...])
            for out, val in zip(refs[4 * n + 4 * a:4 * n + 4 * a + 4], (g, d, mo, vo)):
                out[...] = val

    spec = lambda w: pl.BlockSpec((w.shape[0] // steps, w.shape[1]), lambda i: (i, 0))
    flat = pl.pallas_call(
        body, name=name, grid=(steps,), in_specs=[spec(it[0]) for it in items for _ in range(4)],
        out_specs=[spec(it[0]) for it in items for _ in range(4)],
        out_shape=[jax.ShapeDtypeStruct(it[0].shape, F32) for it in items for _ in range(4)],
        compiler_params=_params(("arbitrary",)),
    )(*[a for it in items for a in it])
    return [flat[4 * a:4 * a + 4] for a in range(n)]


def _adamw_small(sm4, wmv):
    views = SMALL_VIEWS[:-1]
    n = len(views)

    def body(sm4_ref, *refs):
        g_all = ((sm4_ref[0] + sm4_ref[1]) + sm4_ref[2]) + sm4_ref[3]
        for a, (name, rows, cols) in enumerate(views):
            row = SMALL_OFFSETS[name]
            g = g_all[row:row + rows, :cols]
            d, mo, vo = _adamw_math(refs[3 * a][...], g, refs[3 * a + 1][...], refs[3 * a + 2][...])
            for out, val in zip(refs[3 * n + 4 * a:3 * n + 4 * a + 4], (g, d, mo, vo)):
                out[...] = val
        row = SMALL_OFFSETS["loss"]
        refs[-1][...] = g_all[row:row + 1, :128]

    flat = pl.pallas_call(
        body, name="adamw_small",
        out_shape=[jax.ShapeDtypeStruct((rows, cols), F32) for _, rows, cols in views for _ in range(4)]
        + [jax.ShapeDtypeStruct((1, 128), F32)],
        compiler_params=pltpu.CompilerParams(vmem_limit_bytes=VMEM_LIMIT),
    )(sm4, *[a for t in wmv for a in t])
    return [flat[4 * a:4 * a + 4] for a in range(n)] + [flat[-1]]


SMALL_NAMES = ("attn_pre_norm", "mla_q_norm", "mla_kv_norm", "mla_w_ukv", "mla_out_norm", "hgrn_lb_logits",
               "hgrn_out_norm", "attn_post_norm", "ffn_pre_norm", "ffn_post_norm")
BIG_NAMES = ("w_in", "mla_w_uq", "w_out", "w_gate", "w_up", "w_down")
WEIGHT_NAMES = ("attn_pre_norm", "w_in", "mla_q_norm", "mla_w_uq", "mla_kv_norm", "mla_w_ukv", "mla_out_norm",
                "hgrn_lb_logits", "hgrn_out_norm", "w_out", "attn_post_norm", "ffn_pre_norm", "w_gate", "w_up", "w_down",
                "ffn_post_norm")


UQ_COMM_SHAPE = (192, 384)


def _pack_small(vals):
    parts, row = [], 0
    for name, rows, cols in sorted(SMALL_VIEWS, key=lambda view: SMALL_OFFSETS[view[0]]):
        assert SMALL_OFFSETS[name] == row
        parts.append(jnp.pad(vals[name].reshape(rows, cols), ((0, 0), (0, D_MODEL - cols))))
        row += rows
    parts.append(jnp.zeros((SMALL_ROWS - row, D_MODEL), F32))
    return jnp.concatenate(parts, axis=0)


def kernel(x, positions, attn_pre_norm, w_in, mla_q_norm, mla_w_uq, mla_kv_norm, mla_w_ukv, mla_out_norm, hgrn_lb_logits, hgrn_out_norm, w_out, attn_post_norm, ffn_pre_norm, w_gate, w_up, w_down, ffn_post_norm, loss_target, m_attn_pre_norm, m_w_in, m_mla_q_norm, m_mla_w_uq, m_mla_kv_norm, m_mla_w_ukv, m_mla_out_norm, m_hgrn_lb_logits, m_hgrn_out_norm, m_w_out, m_attn_post_norm, m_ffn_pre_norm, m_w_gate, m_w_up, m_w_down, m_ffn_post_norm, v_attn_pre_norm, v_w_in, v_mla_q_norm, v_mla_w_uq, v_mla_kv_norm, v_mla_w_ukv, v_mla_out_norm, v_hgrn_lb_logits, v_hgrn_out_norm, v_w_out, v_attn_post_norm, v_ffn_pre_norm, v_w_gate, v_w_up, v_w_down, v_ffn_post_norm):
    args = locals()
    W = {n: args[n] for n in WEIGHT_NAMES}
    M = {n: args["m_" + n] for n in WEIGHT_NAMES}
    V = {n: args["v_" + n] for n in WEIGHT_NAMES}
    T = x.shape[1]
    cx, cy, cc = lax.axis_index("x"), lax.axis_index("y"), lax.axis_index("c")

    win_rows = D_IN // N_CHIPS
    shard2d = {"w_in": (win_rows, D_MODEL), "mla_w_uq": (Q_RANK // N_CHIPS, MLA_HEADS * MLA_QK),
               "w_out": (D_MODEL // N_CHIPS, D_MODEL), "w_gate": (FF_SHARD, D_MODEL), "w_up": (FF_SHARD, D_MODEL),
               "w_down": (FF_SHARD, D_MODEL)}
    transposed = ("w_in", "w_gate", "w_up")
    to2d = lambda n, a: a[0].T if n in transposed else a.reshape(shard2d[n])
    from2d = lambda n, t: t.T[None] if n in transposed else t.reshape(W[n].shape)
    me = 2 * cx + cy
    place = jnp.stack([cc, me]).astype(jnp.int32)
    local_b = [to2d(n, W[n]).astype(BF16) for n in BIG_NAMES]
    local_b[0] = jnp.pad(local_b[0], ((0, WIN_COMM_SHAPE[1] - win_rows), (0, 0)))
    stacks = _gather_chips(local_b[:2], "gather_weights")
    win4, wuq4 = [lax.dynamic_update_slice(s, l[None], (me, 0, 0)) for s, l in zip(stacks, local_b)]
    win_t = win4[:, :win_rows].reshape(D_IN, D_MODEL)
    wuq_full = wuq4.reshape(Q_RANK, MLA_HEADS, MLA_QK)
    win_arr, wq_arr, wk_arr, wv_arr = _arrange_weights(win_t, wuq_full, mla_w_ukv[0].astype(BF16))
    small = {n: W[n][0] if n == "mla_w_ukv" else W[n].reshape(-1, W[n].shape[-1]) for n in SMALL_NAMES}

    loss_local, dx, grads, ffn_final = _local_step(x[0], positions.reshape(T, 1), loss_target[0], small, win_arr,
                                                           wq_arr, wk_arr, wv_arr, local_b[2:], place)

    gs = [grads["w_in"], grads["mla_w_uq"].reshape((N_CHIPS,) + UQ_COMM_SHAPE)]
    sm = _pack_small({**grads, "loss": loss_local})
    *rs, ssib = _pair_swap(gs, (sm,))
    *ps, pair = _pair_sum(place, gs, rs, small=(sm, ssib))
    ffn_names, rest_names = BIG_NAMES[3:], BIG_NAMES[:2]
    g2d = dict(zip(ffn_names + BIG_NAMES[2:3], ffn_final))
    adam_in = lambda names_: [(to2d(n, W[n]), g2d[n], to2d(n, M[n]), to2d(n, V[n])) for n in names_]
    early_names = ffn_names + BIG_NAMES[2:3]
    updates = dict(zip(early_names, _adamw(adam_in(early_names), 4, "adamw_ffn")))
    *ris, sm4 = _chip_swap(ps, pair)
    *gfin, smf = _pair_fill(_chip_sum(place, gs, rs, ris), sm4)

    g2d.update({n: gfin[k].reshape((-1,) + shard2d[n][1:]) for k, n in enumerate(rest_names)})
    updates.update(zip(rest_names, _adamw(adam_in(rest_names), 1, "adamw_w_in")))
    G, DW, NM, NV = {}, {}, {}, {}
    for n, outs in updates.items():
        G[n], DW[n], NM[n], NV[n] = (from2d(n, t) for t in outs)
    view2d = lambda n, a: a.reshape(next((r, c) for name, r, c in SMALL_VIEWS if name == n))
    *res, loss_row = _adamw_small(smf, [tuple(view2d(n, t[n]) for t in (W, M, V)) for n in SMALL_NAMES])
    for n, outs in zip(SMALL_NAMES, res):
        G[n], DW[n], NM[n], NV[n] = (t.reshape(W[n].shape) for t in outs)
    loss = loss_row[0, 0]
    return (loss, dx[None], *[G[n] for n in WEIGHT_NAMES], *[DW[n] for n in WEIGHT_NAMES],
            *[NM[n] for n in WEIGHT_NAMES], *[NV[n] for n in WEIGHT_NAMES])
```

```python
import jax
import jax.numpy as jnp
from jax import lax
from jax.experimental import pallas as pl
from jax.experimental.pallas import tpu as pltpu

F32 = jnp.float32
BF16 = jnp.bfloat16
MXU_DTYPE = BF16

D_MODEL = 1024
MLA_HEADS = 8
MLA_NOPE = 64
MLA_ROPE = 32
MLA_V = 64
MLA_QK = MLA_NOPE + MLA_ROPE
Q_RANK = 384
KV_RANK = 128
MLA_WIDTH = MLA_HEADS * MLA_V
HEAD_PAD = 128
HGRN_HEADS = 4
HGRN_DIM = 128
HGRN_WIDTH = HGRN_HEADS * HGRN_DIM
CHUNK = 64
SUB = 16
HGRN_CPI = 4
D_IN = Q_RANK + KV_RANK + MLA_ROPE + 4 * HGRN_WIDTH
D_IN_ARR = Q_RANK + KV_RANK + HEAD_PAD + 4 * HGRN_WIDTH
D_FF = 2816
N_CHIPS = 4
FF_SHARD = D_FF // N_CHIPS
EPS = 1e-6
ROPE_THETA = 10000.0
ATTN_SCALE = MLA_QK ** -0.5
ATTN_SCALE_LOG2 = ATTN_SCALE * 1.4426950408889634
NEG_BIG = -1e30

ADAM_LR = 0.001
ADAM_B1 = 0.9
ADAM_B2 = 0.999
ADAM_EPS = 1e-08
ADAM_WD = 0.01
ADAM_STEP = 10

VMEM_LIMIT = 56 * 1024 * 1024
FFN_BWD_VMEM = 62 * 1024 * 1024

SMALL_VIEWS = (("attn_pre_norm", 1, 1024), ("mla_q_norm", 1, 384), ("mla_kv_norm", 1, 128), ("mla_w_ukv", 128, 1024),
               ("mla_out_norm", 1, 512), ("hgrn_lb_logits", 2, 512), ("hgrn_out_norm", 1, 512),
               ("attn_post_norm", 1, 1024), ("ffn_pre_norm", 1, 1024), ("ffn_post_norm", 1, 1024), ("loss", 1, 1))
ROW_TILE = 8


def _small_layout():
    offsets, row = {}, 0
    for whole in (True, False):
        for name, rows, _ in SMALL_VIEWS:
            if (rows % ROW_TILE == 0) == whole:
                offsets[name] = row
                row += rows
    return offsets, -(-row // (2 * ROW_TILE)) * 2 * ROW_TILE


SMALL_OFFSETS, SMALL_ROWS = _small_layout()

MESH = pl.DeviceIdType.MESH
ANY = pl.BlockSpec(memory_space=pl.ANY)


def _dot(a, b, dims, exact):
    if exact:
        return lax.dot_general(a.astype(F32), b.astype(F32), (dims, ((), ())), precision=lax.Precision.HIGH,
                               preferred_element_type=F32)
    return lax.dot_general(a.astype(MXU_DTYPE), b.astype(MXU_DTYPE), (dims, ((), ())), preferred_element_type=F32)


def _mm(a, b, exact=False):
    return _dot(a, b, ((1,), (0,)), exact)


def _mm_nt(a, b, exact=False):
    return _dot(a, b, ((1,), (1,)), exact)


def _mm_tn(a, b, exact=False):
    return _dot(a, b, ((0,), (0,)), exact)


def _rms_fwd(x, w):
    r = lax.rsqrt(jnp.mean(x * x, axis=-1, keepdims=True) + EPS)
    xn = x * r
    return xn * w, xn, r


def _rms_bwd(dy, xn, r, w):
    dxn = dy * w
    dx = r * (dxn - xn * jnp.mean(dxn * xn, axis=-1, keepdims=True))
    dw = jnp.sum(dy * xn, axis=0, keepdims=True)
    return dx, dw


def _group_sums(v, gs):
    t, n = v.shape
    lane = lax.broadcasted_iota(jnp.int32, (t, 128), 1)
    out = []
    for p in range(n // 128):
        vb = v[:, 128 * p:128 * (p + 1)]
        if gs == 128:
            out.append(jnp.sum(vb, axis=-1, keepdims=True))
        else:
            out.append(jnp.sum(jnp.where(lane < 64, vb, 0.0), axis=-1, keepdims=True))
            out.append(jnp.sum(jnp.where(lane >= 64, vb, 0.0), axis=-1, keepdims=True))
    return out


def _group_bcast(sums, gs, t):
    lane = lax.broadcasted_iota(jnp.int32, (t, 128), 1)
    if gs == 128:
        return jnp.concatenate([jnp.broadcast_to(s, (t, 128)) for s in sums], axis=-1)
    return jnp.concatenate([jnp.where(lane < 64, sums[2 * p], sums[2 * p + 1]) for p in range(len(sums) // 2)],
                           axis=-1)


def _grms_fwd(x, w, gs):
    t = x.shape[0]
    r = lax.rsqrt(_group_bcast(_group_sums(x * x, gs), gs, t) * (1.0 / gs) + EPS)
    xn = x * r
    return xn * w, xn, r


def _grms_bwd(dy, xn, r, w, gs):
    t = dy.shape[0]
    dxn = dy * w
    dx = r * (dxn - xn * (_group_bcast(_group_sums(dxn * xn, gs), gs, t) * (1.0 / gs)))
    dw = jnp.sum(dy * xn, axis=0, keepdims=True)
    return dx, dw


def _rope_tables(c_tab, s_tab):
    lane = lax.broadcasted_iota(jnp.int32, c_tab.shape, 1)
    first = (lane >= MLA_NOPE) & (lane < MLA_NOPE + MLA_ROPE // 2)
    second = (lane >= MLA_NOPE + MLA_ROPE // 2) & (lane < MLA_QK)
    return c_tab, jnp.where(first, -s_tab, 0.0), jnp.where(second, s_tab, 0.0)


def _rope(v, c, sa, sb):
    return v * c + pltpu.roll(v, HEAD_PAD - MLA_ROPE // 2, 1) * sa + pltpu.roll(v, MLA_ROPE // 2, 1) * sb


def _rope_bwd(d, c, sa, sb):
    return d * c - pltpu.roll(d, HEAD_PAD - MLA_ROPE // 2, 1) * sa - pltpu.roll(d, MLA_ROPE // 2, 1) * sb


def _params(sem, vmem=VMEM_LIMIT):
    return pltpu.CompilerParams(dimension_semantics=sem, vmem_limit_bytes=vmem)


def _in_fwd(x, pos, invf, w_pre, win, qnw, wq, kvnw, wk, wv, tt=512):
    T = x.shape[0]

    def body(x_ref, pos_ref, invf_ref, wpre_ref, win_ref, qnw_ref, wq_ref, kvnw_ref, wk_ref, wv_ref,
             cq_ref, ckv_ref, xph_ref, q_ref, k_ref, v_ref, kt_ref, vt_ref, rc_ref, rs_ref):
        u, _, _ = _rms_fwd(x_ref[...], wpre_ref[...])
        lo = Q_RANK + KV_RANK + HEAD_PAD
        xp = _mm_nt(u, win_ref[:lo, :])
        xph_ref[...] = _mm_nt(u, win_ref[lo:, :])
        cq = xp[:, :Q_RANK]
        ckv = xp[:, Q_RANK:Q_RANK + KV_RANK]
        kr = xp[:, Q_RANK + KV_RANK:]
        cq_ref[...] = cq
        ckv_ref[...] = ckv
        ang = pos_ref[...].astype(F32) * invf_ref[...]
        c_tab = jnp.cos(ang)
        s_tab = jnp.sin(ang)
        rc_ref[...] = c_tab
        rs_ref[...] = s_tab
        c, sa, sb = _rope_tables(c_tab, s_tab)
        qn, _, _ = _rms_fwd(cq, qnw_ref[...])
        q = _mm(qn, wq_ref[...])
        kvn, _, _ = _rms_fwd(ckv, kvnw_ref[...])
        kn = _mm(kvn, wk_ref[...])
        v = _mm(kvn, wv_ref[...])
        v_ref[...] = v.astype(v_ref.dtype)
        vt_ref[...] = v.T.astype(vt_ref.dtype)
        krr = _rope(kr, c, sa, sb)
        for h in range(MLA_HEADS):
            sl = slice(HEAD_PAD * h, HEAD_PAD * (h + 1))
            q_ref[:, sl] = (_rope(q[:, sl], c, sa, sb) * ATTN_SCALE_LOG2).astype(q_ref.dtype)
            kh = kn[:, sl] + krr
            k_ref[:, sl] = kh.astype(k_ref.dtype)
            kt_ref[sl, :] = kh.T.astype(kt_ref.dtype)

    row = lambda w: pl.BlockSpec((tt, w), lambda i: (i, 0))
    full = lambda a: pl.BlockSpec(a.shape, lambda i: (0,) * a.ndim)
    qk_w = MLA_HEADS * HEAD_PAD
    return pl.pallas_call(
        body, name="in_fwd", grid=(T // tt,),
        in_specs=[row(D_MODEL), row(1), full(invf), full(w_pre), full(win), full(qnw), full(wq), full(kvnw),
                  full(wk), full(wv)],
        out_specs=[row(Q_RANK), row(KV_RANK), row(4 * HGRN_WIDTH), row(qk_w), row(qk_w), row(MLA_WIDTH),
                   pl.BlockSpec((qk_w, tt), lambda i: (0, i)), pl.BlockSpec((MLA_WIDTH, tt), lambda i: (0, i)),
                   row(HEAD_PAD), row(HEAD_PAD)],
        out_shape=[jax.ShapeDtypeStruct((T, Q_RANK), F32), jax.ShapeDtypeStruct((T, KV_RANK), F32),
                   jax.ShapeDtypeStruct((T, 4 * HGRN_WIDTH), F32), jax.ShapeDtypeStruct((T, qk_w), MXU_DTYPE),
                   jax.ShapeDtypeStruct((T, qk_w), MXU_DTYPE), jax.ShapeDtypeStruct((T, MLA_WIDTH), MXU_DTYPE),
                   jax.ShapeDtypeStruct((qk_w, T), MXU_DTYPE), jax.ShapeDtypeStruct((MLA_WIDTH, T), MXU_DTYPE),
                   jax.ShapeDtypeStruct((T, HEAD_PAD), F32), jax.ShapeDtypeStruct((T, HEAD_PAD), F32)],
        compiler_params=_params(("arbitrary",)),
    )(x, pos, invf, w_pre, win, qnw, wq, kvnw, wk, wv)


def _attn_fwd_t(qb, kb, vt, gather=(), tq=256, hps=8):
    T = qb.shape[0]
    nq = T // tq
    ng = len(gather)
    steps = (MLA_HEADS // hps) * nq
    pass_on = steps - 3

    def body(q_ref, k_ref, vt_ref, *rest):
        o_ref, lse_ref = rest[ng:ng + 2]
        acc_scr = rest[2 * ng + 2]
        qi = pl.program_id(1)
        step_no = pl.program_id(0) * nq + qi
        if ng:
            gat = _Gather(rest[:ng], rest[ng + 2:2 * ng + 2], *rest[2 * ng + 3:])

            @pl.when(step_no == 0)
            def _():
                for cp in gat.sends():
                    cp.start()

            @pl.when(step_no == pass_on)
            def _():
                for arrival in gat.arrivals():
                    arrival.wait_recv()
                for cp in gat.forwards():
                    cp.start()

        heads = [slice(HEAD_PAD * a, HEAD_PAD * (a + 1)) for a in range(hps)]
        acc_scr[...] = jnp.zeros_like(acc_scr)

        def step(j, carry, masked):
            start = pl.multiple_of(j * tq, tq)
            scores = [_mm_nt(k_ref[pl.ds(start, tq), heads[a]], q_ref[:, heads[a]]) for a in range(hps)]
            new, probs, alphas = [], [], []
            for a in range(hps):
                m, l = carry[a]
                s = scores[a]
                if masked:
                    kk = lax.broadcasted_iota(jnp.int32, (tq, tq), 0)
                    qq = lax.broadcasted_iota(jnp.int32, (tq, tq), 1)
                    s = jnp.where(kk <= qq, s, NEG_BIG)
                m_new = jnp.maximum(m, jnp.max(s, axis=0, keepdims=True))
                alpha = jnp.exp2(m - m_new)
                p = jnp.exp2(s - m_new)
                l = l * alpha + jnp.sum(p, axis=0, keepdims=True)
                new.append((m_new, l))
                probs.append(p.astype(MXU_DTYPE))
                alphas.append(alpha)
                if a % 2:
                    pr = a // 2
                    vtj = vt_ref[2 * MLA_V * pr:2 * MLA_V * (pr + 1), pl.ds(start, tq)]
                    none = jnp.zeros((MLA_V, tq), vtj.dtype)
                    pv = (_mm(jnp.concatenate([vtj[:MLA_V], none], axis=0), probs[a - 1])
                          + _mm(jnp.concatenate([none, vtj[MLA_V:]], axis=0), probs[a]))
                    acc_scr[pr] = acc_scr[pr] * jnp.where(row < MLA_V, alphas[a - 1], alphas[a]) + pv
            return tuple(new)

        row = lax.broadcasted_iota(jnp.int32, (2 * MLA_V, tq), 0)
        init = tuple((jnp.full((1, tq), NEG_BIG, F32), jnp.zeros((1, tq), F32)) for _ in range(hps))
        carry = lax.fori_loop(0, qi, lambda j, c: step(j, c, False), init)
        carry = step(qi, carry, True)
        for pr in range(hps // 2):
            (m0, l0), (m1, l1) = carry[2 * pr], carry[2 * pr + 1]
            ot = acc_scr[pr] / jnp.where(row < MLA_V, l0, l1)
            o_ref[:, 2 * MLA_V * pr:2 * MLA_V * (pr + 1)] = ot.T
            lse_ref[pr, 0:1, :] = m0 + jnp.log2(l0)
            lse_ref[pr, 1:2, :] = m1 + jnp.log2(l1)

        if ng:
            @pl.when(step_no == steps - 1)
            def _():
                for arrival in gat.forward_arrivals():
                    arrival.wait_recv()
                for cp in gat.sends() + gat.forwards():
                    cp.wait_send()

    return pl.pallas_call(
        body, name="attn_fwd", grid=(MLA_HEADS // hps, nq),
        in_specs=[pl.BlockSpec((tq, hps * HEAD_PAD), lambda g, i: (i, g)),
                  pl.BlockSpec((T, hps * HEAD_PAD), lambda g, i: (0, g)),
                  pl.BlockSpec((hps * MLA_V, T), lambda g, i: (g, 0))] + [ANY] * ng,
        out_specs=[pl.BlockSpec((tq, hps * MLA_V), lambda g, i: (i, g)),
                   pl.BlockSpec((hps // 2, 2, tq), lambda g, i: (g, 0, i))] + [ANY] * ng,
        out_shape=[jax.ShapeDtypeStruct((T, MLA_WIDTH), F32), jax.ShapeDtypeStruct((MLA_HEADS // 2, 2, T), F32)]
        + _Gather.out_shapes(gather),
        scratch_shapes=[pltpu.VMEM((hps // 2, 2 * MLA_V, tq), F32)] + (_Gather.semaphores(gather) if ng else []),
        compiler_params=_params(("arbitrary", "arbitrary")),
    )(qb, kb, vt, *gather)


def _attn_bwd_t(qb, kb, kt, vb, dob, lse, dvec, send=(), tq=512, hps=4):
    T = qb.shape[0]
    nq = T // tq
    ns = len(send)
    steps = (MLA_HEADS // hps) * nq

    def body(q_ref, k_ref, kt_ref, v_ref, do_ref, lse_ref, d_ref, *rest):
        dqt_ref, dk_ref, dv_ref = rest[ns:ns + 3]
        va_scr, dv_scr = rest[2 * ns + 3:2 * ns + 5]
        j = pl.program_id(1)
        step_no = pl.program_id(0) * nq + j
        if ns:
            @pl.when(step_no == 0)
            def _():
                for cp in _chip_swap_copies(rest[:ns], rest[ns + 3:2 * ns + 3], *rest[2 * ns + 5:]):
                    cp.start()

        @pl.when(j == 0)
        def _():
            dqt_ref[...] = jnp.zeros_like(dqt_ref)

        lane = lax.broadcasted_iota(jnp.int32, (tq, 2 * MLA_V), 1)
        heads = [slice(HEAD_PAD * a, HEAD_PAD * (a + 1)) for a in range(hps)]
        pairs = [slice(2 * MLA_V * p, 2 * MLA_V * (p + 1)) for p in range(hps // 2)]
        for pr in range(hps // 2):
            vpair = v_ref[:, pairs[pr]]
            va_scr[2 * pr] = jnp.where(lane < MLA_V, vpair, jnp.zeros_like(vpair))
            va_scr[2 * pr + 1] = jnp.where(lane >= MLA_V, vpair, jnp.zeros_like(vpair))
        dk_ref[...] = jnp.zeros_like(dk_ref)
        dv_scr[...] = jnp.zeros_like(dv_scr)

        def step(i, masked):
            start = pl.multiple_of(i * tq, tq)
            rows = pl.ds(start, tq)
            scores = [_mm_nt(k_ref[:, heads[a]], q_ref[rows, heads[a]]) for a in range(hps)]
            dps = [_mm_nt(va_scr[a], do_ref[rows, pairs[a // 2]]) for a in range(hps)]
            for a in range(hps):
                pr, r = a // 2, a % 2
                p = jnp.exp2(scores[a] - lse_ref[pr, r:r + 1, rows])
                if masked:
                    kk = lax.broadcasted_iota(jnp.int32, (tq, tq), 0)
                    qq = lax.broadcasted_iota(jnp.int32, (tq, tq), 1)
                    p = jnp.where(kk <= qq, p, 0.0)
                ds = p * (dps[a] - d_ref[pr, r:r + 1, rows])
                dv_scr[a] += _mm(p, do_ref[rows, pairs[pr]])
                dk_ref[:, heads[a]] += _mm(ds, q_ref[rows, heads[a]])
                dqt_ref[heads[a], rows] += _mm(kt_ref[heads[a], :], ds)

        def loop_body(i, _):
            step(i, False)
            return 0

        step(j, True)
        lax.fori_loop(j + 1, nq, loop_body, 0)
        for pr in range(hps // 2):
            dv_ref[:, pairs[pr]] = jnp.where(lane < MLA_V, dv_scr[2 * pr], dv_scr[2 * pr + 1])
        dk_ref[...] = dk_ref[...] * (ATTN_SCALE / ATTN_SCALE_LOG2)

        if ns:
            @pl.when(step_no == steps - 1)
            def _():
                for cp in _chip_swap_copies(rest[:ns], rest[ns + 3:2 * ns + 3], *rest[2 * ns + 5:]):
                    cp.wait()

    stat = pl.BlockSpec((hps // 2, 2, T), lambda g, j: (g, 0, 0))
    return pl.pallas_call(
        body, name="attn_bwd", grid=(MLA_HEADS // hps, nq),
        in_specs=[pl.BlockSpec((T, hps * HEAD_PAD), lambda g, j: (0, g)),
                  pl.BlockSpec((tq, hps * HEAD_PAD), lambda g, j: (j, g)),
                  pl.BlockSpec((hps * HEAD_PAD, tq), lambda g, j: (g, j)),
                  pl.BlockSpec((tq, hps * MLA_V), lambda g, j: (j, g)),
                  pl.BlockSpec((T, hps * MLA_V), lambda g, j: (0, g)), stat, stat] + [ANY] * ns,
        out_specs=[pl.BlockSpec((hps * HEAD_PAD, T), lambda g, j: (g, 0)),
                   pl.BlockSpec((tq, hps * HEAD_PAD), lambda g, j: (j, g)),
                   pl.BlockSpec((tq, hps * MLA_V), lambda g, j: (j, g))] + [ANY] * ns,
        out_shape=[jax.ShapeDtypeStruct((MLA_HEADS * HEAD_PAD, T), F32),
                   jax.ShapeDtypeStruct((T, MLA_HEADS * HEAD_PAD), F32),
                   jax.ShapeDtypeStruct((T, MLA_WIDTH), F32)] + _chip_swap_shapes(send),
        scratch_shapes=[pltpu.VMEM((hps, tq, 2 * MLA_V), vb.dtype), pltpu.VMEM((hps, tq, 2 * MLA_V), F32)]
        + ([pltpu.SemaphoreType.DMA((3 * ns,)), pltpu.SemaphoreType.DMA((3 * ns,))] if ns else []),
        compiler_params=_params(("arbitrary", "arbitrary")),
    )(qb, kb, kt, vb, dob, lse, dvec, *send)


def _cumsum_rows(x):
    n = x.shape[0]
    row = lax.broadcasted_iota(jnp.int32, x.shape, 0)
    s = 1
    while s < n:
        x = x + jnp.where(row >= s, pltpu.roll(x, s, 0), 0.0)
        s *= 2
    return x


def _rev_cumsum_rows(x):
    n = x.shape[0]
    row = lax.broadcasted_iota(jnp.int32, x.shape, 0)
    s = 1
    while s < n:
        x = x + jnp.where(row < n - s, pltpu.roll(x, n - s, 0), 0.0)
        s *= 2
    return x


def _lb_from_logits(l):
    l0, l1 = l[0:1, :], l[1:2, :]
    m = jnp.maximum(l0, l1)
    e0, e1 = jnp.exp(l0 - m), jnp.exp(l1 - m)
    return e0 / (e0 + e1)


def _hgrn_gates(hq, hf, lb):
    sig_f = jax.nn.sigmoid(hf)
    f = lb + (1.0 - lb) * sig_f
    sig_q = jax.nn.sigmoid(hq)
    return sig_f, f, jnp.log(f), 1.0 - f, sig_q, hq * sig_q


def _hgrn_intra(q, kk, b, exact=False):
    row = lax.broadcasted_iota(jnp.int32, b.shape, 0)
    qs, ks, eqs, eks, a_rows = [], [], [], [], []
    for i in range(CHUNK // SUB):
        ref = b[SUB * i + SUB // 2:SUB * i + SUB // 2 + 1, :]
        eq = jnp.exp(b[SUB * i:SUB * (i + 1), :] - ref)
        ek = jnp.exp(jnp.where(row < SUB * (i + 1), ref - b, NEG_BIG))
        qi = q[SUB * i:SUB * (i + 1), :] * eq
        ki = kk * ek
        a_rows.append(_mm_nt(qi, ki, exact))
        qs.append(qi), ks.append(ki), eqs.append(eq), eks.append(ek)
    tt = lax.broadcasted_iota(jnp.int32, (CHUNK, CHUNK), 0)
    ss = lax.broadcasted_iota(jnp.int32, (CHUNK, CHUNK), 1)
    causal = ss <= tt
    a = jnp.where(causal, jnp.concatenate(a_rows, axis=0), 0.0)
    return a, causal, qs, ks, eqs, eks


def _hgrn_fwd(xph, lbl, tg=512):
    T = xph.shape[0]
    ng, ncg = T // tg, tg // CHUNK
    cols = [slice(HGRN_DIM * h, HGRN_DIM * (h + 1)) for h in range(HGRN_HEADS)]

    def body(lbl_ref, hq_ref, hf_ref, hi_ref, o_ref, st_ref, s_scr):
        @pl.when(pl.program_id(0) == 0)
        def _():
            s_scr[...] = jnp.zeros_like(s_scr)

        lb = _lb_from_logits(lbl_ref[...])

        def chunks(it, _):
            pre = []
            for k in range(HGRN_CPI):
                c = it * HGRN_CPI + k
                rows = pl.ds(pl.multiple_of(c * CHUNK, CHUNK), CHUNK)
                for cs in cols:
                    _, _, lf, kk, _, q = _hgrn_gates(hq_ref[rows, cs], hf_ref[rows, cs], lb[:, cs])
                    v = hi_ref[rows, cs]
                    b = _cumsum_rows(lf)
                    a = _hgrn_intra(q, kk, b)[0]
                    b_last = b[CHUNK - 1:CHUNK, :]
                    pre.append((c, rows, q * jnp.exp(b), a, v, jnp.exp(b_last), _mm_tn(v, kk * jnp.exp(b_last - b))))
            for i, (c, rows, qe, a, v, ebl, upd) in enumerate(pre):
                h = i % HGRN_HEADS
                st = s_scr[h]
                st_ref[h, c] = st
                o_ref[rows, cols[h]] = _mm_nt(qe, st) + _mm(a, v)
                s_scr[h] = st * ebl + upd
            return 0

        lax.fori_loop(0, ncg // HGRN_CPI, chunks, 0)

    col = lambda k: pl.BlockSpec((tg, HGRN_WIDTH), lambda g: (g, k))
    return pl.pallas_call(
        body, name="hgrn_fwd", grid=(ng,),
        in_specs=[pl.BlockSpec((2, HGRN_WIDTH), lambda g: (0, 0)), col(0), col(1), col(2)],
        out_specs=[col(0), pl.BlockSpec((HGRN_HEADS, ncg, HGRN_DIM, HGRN_DIM), lambda g: (0, g, 0, 0))],
        out_shape=[jax.ShapeDtypeStruct((T, HGRN_WIDTH), F32),
                   jax.ShapeDtypeStruct((HGRN_HEADS, T // CHUNK, HGRN_DIM, HGRN_DIM), F32)],
        scratch_shapes=[pltpu.VMEM((HGRN_HEADS, HGRN_DIM, HGRN_DIM), F32)],
        compiler_params=_params(("arbitrary",)),
    )(lbl, xph, xph, xph)


def _hgrn_bwd(xph, lbl, states, d_o, fill=(), tg=512):
    T = xph.shape[0]
    ng, ncg = T // tg, tg // CHUNK
    cols = [slice(HGRN_DIM * h, HGRN_DIM * (h + 1)) for h in range(HGRN_HEADS)]
    nsub = CHUNK // SUB
    nf = len(fill)

    def body(lbl_ref, hq_ref, hf_ref, hi_ref, st_ref, do_ref, *rest):
        dhq_ref, dhf_ref, dhi_ref, dlg_ref = rest[nf:nf + 4]
        ds_scr, dlb_scr = rest[2 * nf + 4:2 * nf + 6]
        fill_copies = lambda: _pair_fill_copies(rest[nf + 4:2 * nf + 4], *rest[2 * nf + 6:])
        g = pl.program_id(0)

        @pl.when(g == 0)
        def _():
            ds_scr[...] = jnp.zeros_like(ds_scr)
            dlb_scr[...] = jnp.zeros_like(dlb_scr)
            for cp in (fill_copies()[0] if nf else ()):
                cp.start()

        lb = _lb_from_logits(lbl_ref[...])

        def chunks(it, _):
            pre = []
            for k, h in ((k, h) for k in range(HGRN_CPI) for h in range(HGRN_HEADS)):
                cs = cols[h]
                c = ncg - 1 - (it * HGRN_CPI + k)
                rows = pl.ds(pl.multiple_of(c * CHUNK, CHUNK), CHUNK)
                hq = hq_ref[rows, cs]
                sig_f, f, lf, kk, sig_q, q = _hgrn_gates(hq, hf_ref[rows, cs], lb[:, cs])
                v = hi_ref[rows, cs]
                do = do_ref[rows, cs]
                b = _cumsum_rows(lf)
                eb = jnp.exp(b)
                a, causal, qs, ks, eqs, eks = _hgrn_intra(q, kk, b)
                b_last = b[CHUNK - 1:CHUNK, :]
                st = st_ref[h, c]
                pre.append(dict(h=h, cs=cs, rows=rows, hq=hq, sig_f=sig_f, f=f, kk=kk, sig_q=sig_q, q=q, v=v, eb=eb, qs=qs,
                                ks=ks, eqs=eqs,
                                eks=eks, ebl=jnp.exp(b_last), el=jnp.exp(b_last - b), st=st,
                                da=jnp.where(causal, _mm_nt(do, v, True), 0.0), dq=_mm(do, st, True) * eb,
                                dv=_mm_tn(a, do), dsu=_mm_tn(do, q * eb, True)))
            for w in pre:
                dq_rows = []
                dk = jnp.zeros_like(w["q"])
                for i in range(nsub):
                    dai = w["da"][SUB * i:SUB * (i + 1), :]
                    dq_rows.append(_mm(dai, w["ks"][i], True) * w["eqs"][i])
                    dk = dk + _mm_tn(dai, w["qs"][i], True) * w["eks"][i]
                w["dq"] = w["dq"] + jnp.concatenate(dq_rows, axis=0)
                w["dk"] = dk
            for w in pre:
                h, cs, rows = w["h"], w["cs"], w["rows"]
                kk, el, ebl, dst = w["kk"], w["el"], w["ebl"], ds_scr[h]
                dk_state = _mm(w["v"], dst, True) * el
                dk = w["dk"] + dk_state
                e_last = (ebl * jnp.sum(w["st"] * dst, axis=0, keepdims=True)
                          + jnp.sum(kk * dk_state, axis=0, keepdims=True))
                dlf = _rev_cumsum_rows(w["q"] * w["dq"] - kk * dk) + e_last
                ds_scr[h] = dst * ebl + w["dsu"]
                df = dlf / w["f"] - dk
                sig_f, sig_q = w["sig_f"], w["sig_q"]
                dhf_ref[rows, cs] = df * (1.0 - lb[:, cs]) * sig_f * (1.0 - sig_f)
                dlb_scr[:, cs] += jnp.sum(df * (1.0 - sig_f), axis=0, keepdims=True)
                dhq_ref[rows, cs] = w["dq"] * sig_q * (1.0 + w["hq"] * (1.0 - sig_q))
                dhi_ref[rows, cs] = w["dv"] + _mm_nt(kk * el, dst)
            return 0

        lax.fori_loop(0, ncg // HGRN_CPI, chunks, 0)

        @pl.when(g == ng - 1)
        def _():
            dl0 = dlb_scr[...] * lb * (1.0 - lb)
            dlg_ref[...] = jnp.concatenate([dl0, -dl0], axis=0)
            if nf:
                copies, waits = fill_copies()
                for w in waits:
                    w.wait_recv()
                for cp in copies:
                    cp.wait_send()

    col = lambda k: pl.BlockSpec((tg, HGRN_WIDTH), lambda g: (ng - 1 - g, k))
    logits = pl.BlockSpec((2, HGRN_WIDTH), lambda g: (0, 0))
    big = jax.ShapeDtypeStruct((T, HGRN_WIDTH), F32)
    n_in, n_out = 6, 4
    return pl.pallas_call(
        body, name="hgrn_bwd", grid=(ng,),
        in_specs=[logits, col(0), col(1), col(2),
                  pl.BlockSpec((HGRN_HEADS, ncg, HGRN_DIM, HGRN_DIM), lambda g: (0, ng - 1 - g, 0, 0)), col(0)] + [ANY] * nf,
        out_specs=[col(0), col(0), col(0), logits] + [ANY] * nf,
        out_shape=[big, big, big, jax.ShapeDtypeStruct((2, HGRN_WIDTH), F32)]
        + [jax.ShapeDtypeStruct(f.shape, f.dtype) for f in fill],
        input_output_aliases={n_in + k: n_out + k for k in range(nf)},
        scratch_shapes=[pltpu.VMEM((HGRN_HEADS, HGRN_DIM, HGRN_DIM), F32), pltpu.VMEM((1, HGRN_WIDTH), F32)]
        + ([pltpu.SemaphoreType.DMA((nf,)), pltpu.SemaphoreType.DMA((nf,))] if nf else []),
        compiler_params=_params(("arbitrary",)),
    )(lbl, xph, xph, xph, states, d_o, *fill)


def _proj_fwd(x, o_raw, oh_raw, xph, wout, w_mla, w_hg, w_post, w_fpre, tt=512):
    T = x.shape[0]

    def body(x_ref, o_ref, oh_ref, hg_ref, wout_ref, wmla_ref, whg_ref, wpost_ref, wfpre_ref,
             h1_ref, z_ref, mix_ref):
        om, _, _ = _grms_fwd(o_ref[...], wmla_ref[...], MLA_V)
        hg = hg_ref[...]
        ohn, _, _ = _grms_fwd(oh_ref[...], whg_ref[...], HGRN_DIM)
        mix = jnp.concatenate([om, ohn * (hg * jax.nn.sigmoid(hg))], axis=-1)
        mix_ref[...] = mix.astype(mix_ref.dtype)
        y1 = _mm(mix, wout_ref[...])
        h1 = x_ref[...] + _rms_fwd(y1, wpost_ref[...])[0]
        h1_ref[...] = h1
        z_ref[...] = _rms_fwd(h1, wfpre_ref[...])[0].astype(z_ref.dtype)

    row = lambda w: pl.BlockSpec((tt, w), lambda i: (i, 0))
    full = lambda a: pl.BlockSpec(a.shape, lambda i: (0,) * a.ndim)
    sds = jax.ShapeDtypeStruct
    return pl.pallas_call(
        body, name="proj_fwd", grid=(T // tt,),
        in_specs=[row(D_MODEL), row(MLA_WIDTH), row(HGRN_WIDTH), pl.BlockSpec((tt, HGRN_WIDTH), lambda i: (i, 3)),
                  full(wout), full(w_mla), full(w_hg), full(w_post), full(w_fpre)],
        out_specs=[row(D_MODEL)] * 3,
        out_shape=[sds((T, D_MODEL), F32), sds((T, D_MODEL), MXU_DTYPE), sds((T, D_MODEL), MXU_DTYPE)],
        compiler_params=_params(("arbitrary",)),
    )(x, o_raw, oh_raw, xph, wout, w_mla, w_hg, w_post, w_fpre)


def _ffn_fwd(zb, h1, tgt, w_fpost, wg, wu, wd, tt=256):
    T = zb.shape[0]
    nj = N_CHIPS

    def body(z_ref, h1_ref, tgt_ref, wfpost_ref, wg_ref, wu_ref, wd_ref, g_ref, up_ref, dy2_ref, dh2_ref, loss_ref, dwf_ref):
        @pl.when(pl.program_id(0) == 0)
        def _():
            loss_ref[...] = jnp.zeros_like(loss_ref)
            dwf_ref[...] = jnp.zeros_like(dwf_ref)

        z = z_ref[...]
        gs = [_mm_nt(z, wg_ref[j]) for j in range(nj)]
        ups = [_mm_nt(z, wu_ref[j]) for j in range(nj)]
        y2 = jnp.zeros((tt, D_MODEL), F32)
        for j in range(nj):
            g_ref[j] = gs[j]
            up_ref[j] = ups[j]
            y2 = y2 + _mm(gs[j] * jax.nn.sigmoid(gs[j]) * ups[j], wd_ref[j])
        w = wfpost_ref[...]
        y2s, y2n, r2 = _rms_fwd(y2, w)
        e = h1_ref[...] + y2s - tgt_ref[...]
        loss_ref[...] += jnp.sum(e * e, axis=0, keepdims=True)
        dh2 = e * (1.0 / D_MODEL)
        dh2_ref[...] = dh2
        dy2, dwf = _rms_bwd(dh2, y2n, r2, w)
        dy2_ref[...] = dy2.astype(dy2_ref.dtype)
        dwf_ref[...] += dwf

    row = pl.BlockSpec((tt, D_MODEL), lambda i: (i, 0))
    vec = pl.BlockSpec((1, D_MODEL), lambda i: (0, 0))
    resident = pl.BlockSpec((nj, FF_SHARD, D_MODEL), lambda i: (0, 0, 0), pipeline_mode=pl.Buffered(1))
    act = pl.BlockSpec((nj, tt, FF_SHARD), lambda i: (0, i, 0))
    sds = jax.ShapeDtypeStruct
    return pl.pallas_call(
        body, name="ffn_fwd", grid=(T // tt,),
        in_specs=[row, row, row, vec, resident, resident, resident],
        out_specs=[act, act, row, row, vec, vec],
        out_shape=[sds((nj, T, FF_SHARD), F32), sds((nj, T, FF_SHARD), F32), sds((T, D_MODEL), MXU_DTYPE),
                   sds((T, D_MODEL), F32), sds((1, D_MODEL), F32), sds((1, D_MODEL), F32)],
        compiler_params=_params(("arbitrary",)),
    )(zb, h1, tgt, w_fpost, wg, wu, wd)


def _ffn_bwd(zb, g, up, dy2b, wg, wu, wd, tt=512):
    T = zb.shape[0]
    nj = N_CHIPS

    def body(z_ref, g_ref, up_ref, dy2_ref, wg_ref, wu_ref, wd_ref, dwg_ref, dwu_ref, dwd_ref, dz_ref, acc_ref):
        j, i = pl.program_id(0), pl.program_id(1)
        rows = pl.ds(pl.multiple_of(i * tt, tt), tt)

        @pl.when(i == 0)
        def _():
            dwg_ref[...] = jnp.zeros_like(dwg_ref)
            dwu_ref[...] = jnp.zeros_like(dwu_ref)
            dwd_ref[...] = jnp.zeros_like(dwd_ref)

        z, g_, up_, dy2 = z_ref[...], g_ref[0], up_ref[0], dy2_ref[...]
        sg = jax.nn.sigmoid(g_)
        act = g_ * sg
        dff = _mm_nt(dy2, wd_ref[0])
        dwd_ref[0] += _mm_tn(act * up_, dy2)
        dg = dff * up_ * sg * (1.0 + g_ * (1.0 - sg))
        dup = dff * act
        dwg_ref[0] += _mm_tn(dg, z)
        dwu_ref[0] += _mm_tn(dup, z)
        dz = _mm(dg, wg_ref[0]) + _mm(dup, wu_ref[0])

        @pl.when(j == 0)
        def _():
            acc_ref[rows, :] = dz

        @pl.when((j > 0) & (j < nj - 1))
        def _():
            acc_ref[rows, :] += dz

        @pl.when(j == nj - 1)
        def _():
            dz_ref[...] = acc_ref[rows, :] + dz

    row = pl.BlockSpec((tt, D_MODEL), lambda j, i: (i, 0))
    act = pl.BlockSpec((1, tt, FF_SHARD), lambda j, i: (j, i, 0))
    w_sh = pl.BlockSpec((1, FF_SHARD, D_MODEL), lambda j, i: (j, 0, 0))
    w_grad = jax.ShapeDtypeStruct((nj, FF_SHARD, D_MODEL), F32)
    return pl.pallas_call(
        body, name="ffn_bwd", grid=(nj, T // tt),
        in_specs=[row, act, act, row, w_sh, w_sh, w_sh],
        out_specs=[w_sh, w_sh, w_sh, pl.BlockSpec((tt, D_MODEL), lambda j, i: (jnp.where(j == nj - 1, i, 0), 0))],
        out_shape=[w_grad, w_grad, w_grad, jax.ShapeDtypeStruct((T, D_MODEL), F32)],
        scratch_shapes=[pltpu.VMEM((T, D_MODEL), F32)],
        compiler_params=_params(("arbitrary", "arbitrary"), vmem=FFN_BWD_VMEM),
    )(zb, g, up, dy2b, wg, wu, wd)


def _mid_bwd(dz, dh2, h1, mixb, o_raw, oh_raw, xph, wout, w_fpre, w_post, w_mla, w_hg, swap=(), tt=512):
    T = dh2.shape[0]
    nsw = len(swap)
    n_in, n_out = 12, 10

    def body(*refs):
        (dz_ref, dh2_ref, h1_ref, mix_ref, o_ref, oh_ref, hg_ref, wout_ref, wfpre_ref, wpost_ref,
         wmla_ref, whg_ref) = refs[:n_in]
        (dh1_ref, dwout_ref, do_ref, doh_ref, dhg_ref, dvec_ref, dwfpre_ref, dwpost_ref, dwmla_ref,
         dwhg_ref) = refs[n_in + nsw:n_in + nsw + n_out]
        sems = refs[n_in + 2 * nsw + n_out + 1:]
        swap_copies = lambda: _pair_swap_copies(refs[n_in:n_in + nsw], refs[n_in + nsw + n_out:n_in + 2 * nsw + n_out],
                                                *sems)

        def wout_copies():
            _, _, c, _, sib, _ = _place()
            rw_ref, h = refs[n_in + 2 * nsw + n_out], D_MODEL // N_CHIPS // 2
            return [_rcopy(dwout_ref.at[pl.ds(pl.multiple_of((2 * k + 1 - c) * h, 8), h)], rw_ref.at[k], *sems, nsw + k, sib)
                    for k in range(N_CHIPS)]

        @pl.when(pl.program_id(0) == 0)
        def _():
            for r in (dwout_ref, dwfpre_ref, dwpost_ref, dwmla_ref, dwhg_ref):
                r[...] = jnp.zeros_like(r)
            for cp in (swap_copies() if nsw else ()):
                cp.start()

        dz = dz_ref[...]
        wfpre = wfpre_ref[...]
        _, h1n, r = _rms_fwd(h1_ref[...], wfpre)
        dh1_z, dwfpre = _rms_bwd(dz, h1n, r, wfpre)
        dwfpre_ref[...] += dwfpre
        dh1 = dh2_ref[...] + dh1_z
        dh1_ref[...] = dh1
        wpost = wpost_ref[...]
        _, y1n, r1 = _rms_fwd(_mm(mix_ref[...], wout_ref[...]), wpost)
        dy1, dwpost = _rms_bwd(dh1, y1n, r1, wpost)
        dwpost_ref[...] += dwpost
        dmix = _mm_nt(dy1, wout_ref[...])
        dwout_ref[...] += _mm_tn(mix_ref[...], dy1)
        wmla = wmla_ref[...]
        o = o_ref[...]
        _, on, ro = _grms_fwd(o, wmla, MLA_V)
        d_o, dwmla = _grms_bwd(dmix[:, :MLA_WIDTH], on, ro, wmla, MLA_V)
        dwmla_ref[...] += dwmla
        do_ref[...] = d_o.astype(do_ref.dtype)
        hh = lax.broadcasted_iota(jnp.int32, (MLA_HEADS, MLA_WIDTH), 0)
        ll = lax.broadcasted_iota(jnp.int32, (MLA_HEADS, MLA_WIDTH), 1)
        sel = jnp.where((ll >= hh * MLA_V) & (ll < (hh + 1) * MLA_V), 1.0, 0.0)
        dvec = _mm_nt(sel, d_o * o, True)
        for pr in range(MLA_HEADS // 2):
            dvec_ref[pr] = dvec[2 * pr:2 * pr + 2]
        whg = whg_ref[...]
        hg = hg_ref[...]
        sg = jax.nn.sigmoid(hg)
        _, ohn, rh = _grms_fwd(oh_ref[...], whg, HGRN_DIM)
        dmh = dmix[:, MLA_WIDTH:]
        dhg_ref[...] = dmh * ohn * whg * sg * (1.0 + hg * (1.0 - sg))
        d_oh, dwhg = _grms_bwd(dmh * (hg * sg), ohn, rh, whg, HGRN_DIM)
        dwhg_ref[...] += dwhg
        doh_ref[...] = d_oh

        if nsw:
            @pl.when(pl.program_id(0) == T // tt - 1)
            def _():
                for cp in wout_copies():
                    cp.start()
                for cp in swap_copies() + wout_copies():
                    cp.wait()

    row = lambda w: pl.BlockSpec((tt, w), lambda i: (i, 0))
    full = lambda a: pl.BlockSpec(a.shape, lambda i: (0,) * a.ndim)
    vec = lambda w: pl.BlockSpec((1, w), lambda i: (0, 0))
    sds = jax.ShapeDtypeStruct
    return pl.pallas_call(
        body, name="mid_bwd", grid=(T // tt,),
        in_specs=[row(D_MODEL), row(D_MODEL), row(D_MODEL),
                  row(D_MODEL), row(MLA_WIDTH), row(HGRN_WIDTH), pl.BlockSpec((tt, HGRN_WIDTH), lambda i: (i, 3)),
                  full(wout), vec(D_MODEL), vec(D_MODEL), vec(MLA_WIDTH), vec(HGRN_WIDTH)] + [ANY] * nsw,
        out_specs=[row(D_MODEL), full(wout), row(MLA_WIDTH), row(HGRN_WIDTH), row(HGRN_WIDTH),
                   pl.BlockSpec((MLA_HEADS // 2, 2, tt), lambda i: (0, 0, i)),
                   vec(D_MODEL), vec(D_MODEL), vec(MLA_WIDTH), vec(HGRN_WIDTH)] + [ANY] * (nsw + 1 if nsw else 0),
        out_shape=[sds((T, D_MODEL), F32), sds(wout.shape, F32), sds((T, MLA_WIDTH), MXU_DTYPE), sds((T, HGRN_WIDTH), F32),
                   sds((T, HGRN_WIDTH), F32), sds((MLA_HEADS // 2, 2, T), F32),
                   sds((1, D_MODEL), F32), sds((1, D_MODEL), F32), sds((1, MLA_WIDTH), F32), sds((1, HGRN_WIDTH), F32)]
        + (_half_stack_shapes(list(swap) + [sds((N_CHIPS, D_MODEL // N_CHIPS, D_MODEL), F32)]) if nsw else []),
        scratch_shapes=[pltpu.SemaphoreType.DMA((nsw + N_CHIPS,)), pltpu.SemaphoreType.DMA((nsw + N_CHIPS,))] if nsw else [],
        compiler_params=_params(("arbitrary",)),
    )(dz, dh2, h1, mixb, o_raw, oh_raw, xph, wout, w_fpre, w_post, w_mla, w_hg, *swap)


def _in_bwd(x, dh1, cq, ckv, dq, dk, dv, dhq, dhf, dhi, dhg, rc, rs, w_pre, win, qnw, wq, kvnw, wk, wv, tt=256):
    T = x.shape[0]

    def body(x_ref, dh1_ref, cq_ref, ckv_ref, dq_ref, dk_ref, dv_ref, dhq_ref, dhf_ref, dhi_ref, dhg_ref, rc_ref, rs_ref,
             wpre_ref, win_ref, qnw_ref, wq_ref, kvnw_ref, wk_ref, wv_ref,
             dx_ref, dwin_ref, dwq_ref, dwk_ref, dwv_ref, dwpre_ref, dqnw_ref, dkvnw_ref):
        @pl.when(pl.program_id(0) == 0)
        def _():
            for r in (dwin_ref, dwq_ref, dwk_ref, dwv_ref, dwpre_ref, dqnw_ref, dkvnw_ref):
                r[...] = jnp.zeros_like(r)

        def add_win_grad(r, first):
            for arr0, n, chip, row0 in _win_grad_segments():
                if first <= arr0 and arr0 + n <= first + r.shape[0]:
                    dwin_ref[chip, row0:row0 + n, :] += r[arr0 - first:arr0 - first + n]

        lo = Q_RANK + KV_RANK + HEAD_PAD
        dxp_h = jnp.concatenate([dhq_ref[...], dhf_ref[...], dhi_ref[...], dhg_ref[...]], axis=-1)
        du = _mm(dxp_h, win_ref[lo:, :])
        wpre = wpre_ref[...]
        u, xn, rx = _rms_fwd(x_ref[...], wpre)
        add_win_grad(_mm_tn(dxp_h, u), lo)
        c, sa, sb = _rope_tables(rc_ref[...], rs_ref[...])
        lane = lax.broadcasted_iota(jnp.int32, (tt, HEAD_PAD), 1)
        dk_all = dk_ref[...]
        dq_lin = []
        dkr = jnp.zeros((tt, HEAD_PAD), F32)
        for h in range(MLA_HEADS):
            sl = slice(HEAD_PAD * h, HEAD_PAD * (h + 1))
            dq_lin.append(_rope_bwd(dq_ref[sl, :].T * ATTN_SCALE, c, sa, sb))
            dkr = dkr + dk_all[:, sl]
        dq_lin = jnp.concatenate(dq_lin, axis=-1)
        dkr = jnp.where((lane >= MLA_NOPE) & (lane < MLA_QK), _rope_bwd(dkr, c, sa, sb), 0.0)
        qnw = qnw_ref[...]
        qn, cqn, rq = _rms_fwd(cq_ref[...], qnw)
        dwq_ref[...] += _mm_tn(qn, dq_lin)
        dcq, dqnw = _rms_bwd(_mm_nt(dq_lin, wq_ref[...]), cqn, rq, qnw)
        dqnw_ref[...] += dqnw
        kvnw = kvnw_ref[...]
        kvn, ckvn, rkv = _rms_fwd(ckv_ref[...], kvnw)
        dv_ = dv_ref[...]
        dwk_ref[...] += _mm_tn(kvn, dk_all)
        dwv_ref[...] += _mm_tn(kvn, dv_)
        dckv, dkvnw = _rms_bwd(_mm_nt(dk_all, wk_ref[...]) + _mm_nt(dv_, wv_ref[...]), ckvn, rkv, kvnw)
        dkvnw_ref[...] += dkvnw
        dxp_a = jnp.concatenate([dcq, dckv, dkr], axis=-1)
        add_win_grad(_mm_tn(dxp_a, u), 0)
        dx_u, dwpre = _rms_bwd(du + _mm(dxp_a, win_ref[:lo, :]), xn, rx, wpre)
        dwpre_ref[...] += dwpre
        dx_ref[...] = dh1_ref[...] + dx_u

    row = lambda w: pl.BlockSpec((tt, w), lambda i: (i, 0))
    full = lambda a: pl.BlockSpec(a.shape, lambda i: (0,) * a.ndim)
    sds = jax.ShapeDtypeStruct
    qk_w = MLA_HEADS * HEAD_PAD
    return pl.pallas_call(
        body, name="in_bwd", grid=(T // tt,),
        in_specs=[row(D_MODEL), row(D_MODEL), row(Q_RANK), row(KV_RANK), pl.BlockSpec((qk_w, tt), lambda i: (0, i)),
                  row(qk_w), row(MLA_WIDTH),
                  row(HGRN_WIDTH), row(HGRN_WIDTH), row(HGRN_WIDTH), row(HGRN_WIDTH), row(HEAD_PAD), row(HEAD_PAD),
                  full(w_pre), full(win), full(qnw), full(wq), full(kvnw), full(wk), full(wv)],
        out_specs=[row(D_MODEL), pl.BlockSpec(WIN_COMM_SHAPE, lambda i: (0, 0, 0)), full(wq), full(wk), full(wv),
                   full(w_pre), full(qnw), full(kvnw)],
        out_shape=[sds((T, D_MODEL), F32), sds(WIN_COMM_SHAPE, F32), sds(wq.shape, F32), sds(wk.shape, F32),
                   sds(wv.shape, F32), sds(w_pre.shape, F32), sds(qnw.shape, F32), sds(kvnw.shape, F32)],
        compiler_params=_params(("arbitrary",)),
    )(x, dh1, cq, ckv, dq, dk, dv, dhq, dhf, dhi, dhg, rc, rs, w_pre, win, qnw, wq, kvnw, wk, wv)


def _arrange_weights(win_t, wuq_full, wukv):
    dt = win_t.dtype
    z = lambda n: jnp.zeros((n, D_MODEL), dt)
    s2 = Q_RANK + KV_RANK
    win_arr = jnp.concatenate([win_t[:s2], z(MLA_NOPE), win_t[s2:s2 + MLA_ROPE], z(HEAD_PAD - MLA_QK),
                               win_t[s2 + MLA_ROPE:]], axis=0)
    wq_arr = jnp.pad(wuq_full, ((0, 0), (0, 0), (0, HEAD_PAD - MLA_QK))).reshape(Q_RANK, MLA_HEADS * HEAD_PAD)
    wk_arr = jnp.pad(wukv[:, :, :MLA_NOPE], ((0, 0), (0, 0), (0, HEAD_PAD - MLA_NOPE))).reshape(
        KV_RANK, MLA_HEADS * HEAD_PAD)
    wv_arr = wukv[:, :, MLA_NOPE:].reshape(KV_RANK, MLA_WIDTH)
    return win_arr, wq_arr, wk_arr, wv_arr


WIN_COMM_SHAPE = (N_CHIPS, -(-D_IN // N_CHIPS // 32) * 32, D_MODEL)


def _win_grad_segments():
    s2 = Q_RANK + KV_RANK
    runs = [(0, s2, 0), (s2, s2 + MLA_ROPE, MLA_NOPE), (s2 + MLA_ROPE, D_IN, HEAD_PAD - MLA_ROPE)]
    per = D_IN // N_CHIPS
    segs = []
    for lo, hi, shift in runs:
        for k in range(N_CHIPS):
            a, b = max(lo, per * k), min(hi, per * (k + 1))
            if a < b:
                segs.append((a + shift, b - a, k, a - per * k))
    return segs


def _unarrange_grads(dwq_arr, dwk_arr, dwv_arr):
    dwuq = dwq_arr.reshape(Q_RANK, MLA_HEADS, HEAD_PAD)[:, :, :MLA_QK]
    dwukv = jnp.concatenate([dwk_arr.reshape(KV_RANK, MLA_HEADS, HEAD_PAD)[:, :, :MLA_NOPE],
                             dwv_arr.reshape(KV_RANK, MLA_HEADS, MLA_V)], axis=-1)
    return dwuq, dwukv


def _rope_inv_freq():
    inv = 1.0 / (ROPE_THETA ** (jnp.arange(0, MLA_ROPE, 2, dtype=F32) / MLA_ROPE))
    z = lambda n: jnp.zeros((n,), F32)
    return jnp.concatenate([z(MLA_NOPE), inv, inv, z(HEAD_PAD - MLA_QK)]).reshape(1, HEAD_PAD)


def _local_step(x, pos, tgt, small, win_arr, wq_arr, wk_arr, wv_arr, late, place=None):
    invf = _rope_inv_freq()
    cq, ckv, xph, qb, kb, vb, kt, vt, rc, rs = _in_fwd(x, pos, invf, small["attn_pre_norm"], win_arr, small["mla_q_norm"],
                                               wq_arr, small["mla_kv_norm"], wk_arr, wv_arr)
    if place is None:
        o_raw, lse = _attn_fwd_t(qb, kb, vt)
        wout, wg, wu, wd = late
    else:
        o_raw, lse, *stacks = _attn_fwd_t(qb, kb, vt, gather=late)
        wout, wg, wu, wd = [lax.dynamic_update_slice(s, l[None], (place[1], 0, 0)) for s, l in zip(stacks, late)]
        wout = wout.reshape(D_MODEL, D_MODEL)
    oh_raw, states = _hgrn_fwd(xph, small["hgrn_lb_logits"])
    h1, zb, mixb = _proj_fwd(x, o_raw, oh_raw, xph, wout, small["mla_out_norm"], small["hgrn_out_norm"],
                                 small["attn_post_norm"], small["ffn_pre_norm"])
    g, up, dy2b, dh2, loss_acc, d_fpost = _ffn_fwd(zb, h1, tgt, small["ffn_post_norm"], wg, wu, wd)
    dwg, dwu, dwd, dz = _ffn_bwd(zb, g, up, dy2b, wg, wu, wd)
    ffn_grads = [] if place is None else [dwg, dwu, dwd]
    dh1, dwout, d_o, d_oh, dhg, dvec, d_fpre, d_post, d_mla, d_hg, *ffn_rs = _mid_bwd(
        dz, dh2, h1, mixb, o_raw, oh_raw, xph, wout, small["ffn_pre_norm"], small["attn_post_norm"],
        small["mla_out_norm"], small["hgrn_out_norm"], swap=ffn_grads)
    if ffn_grads:
        ffn_grads = ffn_grads + [dwout.reshape(N_CHIPS, D_MODEL // N_CHIPS, D_MODEL)]
    ffn_ps = _pair_sum(place, ffn_grads, ffn_rs, name="pair_sum_ffn") if ffn_grads else []
    dq, dk, dv, *ffn_ris = _attn_bwd_t(qb, kb, kt, vb, d_o, lse, dvec, send=ffn_ps)
    ffn_sums = _chip_sum(place, ffn_grads, ffn_rs, ffn_ris, name="chip_sum_ffn") if ffn_grads else []
    dhq, dhf, dhi, d_lbl, *ffn_final = _hgrn_bwd(xph, small["hgrn_lb_logits"], states, d_oh, fill=ffn_sums)
    dx, dwin4, dwq_arr, dwk_arr, dwv_arr, d_pre, d_qn, d_kvn = _in_bwd(
        x, dh1, cq, ckv, dq, dk, dv, dhq, dhf, dhi, dhg, rc, rs, small["attn_pre_norm"], win_arr,
        small["mla_q_norm"], wq_arr, small["mla_kv_norm"], wk_arr, wv_arr)
    dwuq, dwukv = _unarrange_grads(dwq_arr, dwk_arr, dwv_arr)
    loss = 0.5 * jnp.sum(loss_acc) * (1.0 / D_MODEL)
    grads = dict(attn_pre_norm=d_pre, w_in=dwin4, mla_q_norm=d_qn, mla_w_uq=dwuq, mla_kv_norm=d_kvn, mla_w_ukv=dwukv,
                 mla_out_norm=d_mla, hgrn_lb_logits=d_lbl, hgrn_out_norm=d_hg, w_out=dwout, attn_post_norm=d_post,
                 ffn_pre_norm=d_fpre, w_gate=dwg, w_up=dwu, w_down=dwd, ffn_post_norm=d_fpost)
    if place is None:
        return loss, dx, grads
    return loss, dx, grads, ffn_final


def _place():
    x, y, c = lax.axis_index("x"), lax.axis_index("y"), lax.axis_index("c")
    others = [(1 - x, y), (x, 1 - y), (1 - x, 1 - y)]
    return x, y, c, 2 * x + y, (x, y, 1 - c), others


def _half(ref, c, rows):
    return ref.at[pl.ds(pl.multiple_of(c * rows, 8), rows)]


def _rcopy(src, dst, send, recv, k, to):
    return pltpu.make_async_remote_copy(src_ref=src, dst_ref=dst, send_sem=send.at[k], recv_sem=recv.at[k],
                                        device_id=to, device_id_type=MESH)


class _Gather:
    def __init__(self, ins, outs, send, recv):
        self.ins, self.outs, self.send, self.recv = ins, outs, send, recv
        self.n = len(ins)
        self.halves = [r.shape[0] // 2 for r in ins]
        _, _, self.c, self.me, self.sib, self.others = _place()

    def _each(self):
        for j, (px, py) in enumerate(self.others):
            for a in range(self.n):
                yield j * self.n + a, a, 2 * px + py, (px, py, self.c)

    def sends(self):
        return [_rcopy(_half(self.ins[a], self.c, self.halves[a]), _half(self.outs[a].at[self.me], self.c, self.halves[a]),
                       self.send, self.recv, k, to) for k, a, _, to in self._each()]

    def arrivals(self):
        parts = [(k, _half(self.outs[a].at[chip], self.c, self.halves[a]), to) for k, a, chip, to in self._each()]
        return [_rcopy(p, p, self.send, self.recv, k, to) for k, p, to in parts]

    def forwards(self):
        parts = [(k, _half(self.outs[a].at[chip], self.c, self.halves[a])) for k, a, chip, _ in self._each()]
        return [_rcopy(p, p, self.send, self.recv, 3 * self.n + k, self.sib) for k, p in parts]

    def forward_arrivals(self):
        parts = [(k, _half(self.outs[a].at[chip], 1 - self.c, self.halves[a])) for k, a, chip, _ in self._each()]
        return [_rcopy(p, p, self.send, self.recv, 3 * self.n + k, self.sib) for k, p in parts]

    @staticmethod
    def out_shapes(arrs):
        return [jax.ShapeDtypeStruct((N_CHIPS,) + a.shape, a.dtype) for a in arrs]

    @staticmethod
    def semaphores(arrs):
        return [pltpu.SemaphoreType.DMA((6 * len(arrs),)), pltpu.SemaphoreType.DMA((6 * len(arrs),))]


def _gather_chips(arrs, name):
    n = len(arrs)

    def body(*refs):
        gat = _Gather(refs[:n], refs[n:2 * n], *refs[2 * n:])
        sends, forwards = gat.sends(), gat.forwards()
        for cp in sends:
            cp.start()
        for arrival, fw in zip(gat.arrivals(), forwards):
            arrival.wait_recv()
            fw.start()
        for arrival in gat.forward_arrivals():
            arrival.wait_recv()
        for cp in sends + forwards:
            cp.wait_send()

    return pl.pallas_call(body, name=name, in_specs=[ANY] * n, out_specs=[ANY] * n, out_shape=_Gather.out_shapes(arrs),
                          scratch_shapes=_Gather.semaphores(arrs))(*arrs)


def _grad_blocks(gs):
    return 2 if all(g.shape[1] // 2 % 32 == 0 for g in gs) else 1


def _pair_swap_copies(g_refs, r_refs, send, recv):
    _, _, c, _, sib, _ = _place()
    copies = []
    for a, (g, r) in enumerate(zip(g_refs, r_refs)):
        h = g.shape[1] // 2
        copies.append(_rcopy(g.at[:, pl.ds(pl.multiple_of((1 - c) * h, 8), h)], r, send, recv, a, sib))
    return copies


def _half_stack_shapes(gs, dtype=None):
    return [jax.ShapeDtypeStruct((N_CHIPS, g.shape[1] // 2, g.shape[2]), dtype or g.dtype) for g in gs]


def _pair_swap(gs, wholes):
    n, nw = len(gs), len(wholes)

    def body(*refs):
        ins, outs, (send, recv) = refs[:n + nw], refs[n + nw:2 * (n + nw)], refs[2 * (n + nw):]
        copies = _pair_swap_copies(ins[:n], outs[:n], send, recv)
        copies += [_rcopy(ins[n + k], outs[n + k], send, recv, n + k, _place()[4]) for k in range(nw)]
        for cp in copies:
            cp.start()
        for cp in copies:
            cp.wait()

    return pl.pallas_call(
        body, name="pair_swap", in_specs=[ANY] * (n + nw), out_specs=[ANY] * (n + nw),
        out_shape=_half_stack_shapes(gs) + [jax.ShapeDtypeStruct(w.shape, w.dtype) for w in wholes],
        scratch_shapes=[pltpu.SemaphoreType.DMA((n + nw,)), pltpu.SemaphoreType.DMA((n + nw,))],
    )(*gs, *wholes)


def _pair_sum(place, gs, rs, small=None, name="pair_sum"):
    n = len(gs)
    nb = _grad_blocks(gs)

    def body(place_ref, *refs):
        g_refs, r_refs, p_refs = refs[:n], refs[n:2 * n], refs[-n - 1:-1] if small else refs[-n:]
        for a in range(n):
            p_refs[a][0] = (g_refs[a][0] + r_refs[a][0]).astype(p_refs[a].dtype)
        if small:
            @pl.when((pl.program_id(0) == 0) & (pl.program_id(1) == 0))
            def _():
                refs[-1][...] = refs[2 * n][...] + refs[2 * n + 1][...]

    chip = lambda k, p: lax.rem(p[1] + 1 + k, N_CHIPS)
    in_specs, out_specs = [], []
    for g in gs:
        blk = (1, g.shape[1] // 2 // nb, g.shape[2])
        in_specs.append(pl.BlockSpec(blk, lambda i, k, p: (chip(k, p), p[0] * nb + i, 0)))
    for g in gs:
        blk = (1, g.shape[1] // 2 // nb, g.shape[2])
        in_specs.append(pl.BlockSpec(blk, lambda i, k, p: (chip(k, p), i, 0)))
        out_specs.append(pl.BlockSpec(blk, lambda i, k, p: (chip(k, p), i, 0)))
    out_shape = _half_stack_shapes(gs, BF16)
    if small:
        sm_spec = pl.BlockSpec(small[0].shape, lambda i, k, p: (0, 0))
        in_specs += [sm_spec, sm_spec]
        out_specs.append(sm_spec)
        out_shape.append(jax.ShapeDtypeStruct(small[0].shape, F32))
    return pl.pallas_call(
        body, name=name,
        grid_spec=pltpu.PrefetchScalarGridSpec(num_scalar_prefetch=1, grid=(nb, N_CHIPS - 1), in_specs=in_specs,
                                               out_specs=out_specs),
        out_shape=out_shape,
        compiler_params=_params(("arbitrary", "arbitrary")),
    )(place, *gs, *rs, *(small or ()))


def _chip_swap_copies(p_refs, ri_refs, send, recv):
    _, _, c, _, _, others = _place()
    n = len(p_refs)
    return [_rcopy(p_refs[a].at[2 * px + py], ri_refs[a].at[j], send, recv, j * n + a, (px, py, c))
            for j, (px, py) in enumerate(others) for a in range(n)]


def _chip_swap_shapes(ps):
    return [jax.ShapeDtypeStruct((3,) + p.shape[1:], p.dtype) for p in ps]


def _chip_swap(ps, pair):
    n = len(ps)

    def body(*refs):
        start, finish = _chip_swap_plan(refs[:n], refs[n], refs[n + 1:2 * n + 1], refs[2 * n + 1], *refs[2 * n + 2:])
        start()
        finish()

    return pl.pallas_call(
        body, name="chip_swap", in_specs=[ANY] * (n + 1), out_specs=[ANY] * (n + 1),
        out_shape=_chip_swap_out_shapes(ps, pair), scratch_shapes=_chip_swap_semaphores(n),
    )(*ps, pair)


def _chip_swap_plan(p_refs, pair_ref, ri_refs, sm4_ref, send, recv, lsem):
    n = len(p_refs)
    hs = SMALL_ROWS // 2
    x, y, c, me, sib, others = _place()
    local = pltpu.make_async_copy(pair_ref, sm4_ref.at[me], lsem.at[0])
    copies = _chip_swap_copies(p_refs, ri_refs, send, recv)
    arrivals = list(copies)
    for j, (px, py) in enumerate(others):
        copies.append(_rcopy(_half(pair_ref, c, hs), _half(sm4_ref.at[me], c, hs), send, recv, 3 * n + j, (px, py, c)))
        part = _half(sm4_ref.at[2 * px + py], c, hs)
        arrivals.append(_rcopy(part, part, send, recv, 3 * n + j, (px, py, c)))

    def start():
        local.start()
        for cp in copies:
            cp.start()

    def finish():
        for arrival in arrivals:
            arrival.wait_recv()
        for cp in copies:
            cp.wait_send()
        local.wait()

    return start, finish


def _chip_swap_out_shapes(ps, pair):
    return _chip_swap_shapes(ps) + [jax.ShapeDtypeStruct((N_CHIPS,) + pair.shape, pair.dtype)]


def _chip_swap_semaphores(n):
    k = 3 * (n + 1)
    return [pltpu.SemaphoreType.DMA((k,)), pltpu.SemaphoreType.DMA((k,)), pltpu.SemaphoreType.DMA((1,))]


def _chip_sum(place, gs, rs, ris, name="chip_sum"):
    n = len(gs)
    nb = 1

    def body(place_ref, *refs):
        g_refs, r_refs, ri_refs, o_refs = refs[:n], refs[n:2 * n], refs[2 * n:3 * n], refs[3 * n:]
        for a in range(n):
            ri = ri_refs[a]
            o_refs[a][...] = (g_refs[a][0] + r_refs[a][0]) + ri[0].astype(F32) + ri[1].astype(F32) + ri[2].astype(F32)

    in_specs, out_specs, out_shape = [], [], []
    for g in gs:
        blk = (1, g.shape[1] // 2 // nb, g.shape[2])
        in_specs.append(pl.BlockSpec(blk, lambda i, p: (p[1], p[0] * nb + i, 0)))
    for g in gs:
        blk = (1, g.shape[1] // 2 // nb, g.shape[2])
        in_specs.append(pl.BlockSpec(blk, lambda i, p: (p[1], i, 0)))
    for g in gs:
        rb = g.shape[1] // 2 // nb
        in_specs.append(pl.BlockSpec((3, rb, g.shape[2]), lambda i, p: (0, i, 0)))
        out_specs.append(pl.BlockSpec((rb, g.shape[2]), lambda i, p: (p[0] * nb + i, 0)))
        out_shape.append(jax.ShapeDtypeStruct(g.shape[1:], F32))
    return pl.pallas_call(
        body, name=name,
        grid_spec=pltpu.PrefetchScalarGridSpec(num_scalar_prefetch=1, grid=(nb,), in_specs=in_specs, out_specs=out_specs),
        out_shape=out_shape,
        compiler_params=_params(("arbitrary",)),
    )(place, *gs, *rs, *ris)


def _pair_fill_copies(g_refs, send, recv):
    _, _, c, _, sib, _ = _place()
    copies, waits = [], []
    for a, g in enumerate(g_refs):
        h = g.shape[0] // 2
        mine, theirs = _half(g, c, h), _half(g, 1 - c, h)
        copies.append(_rcopy(mine, mine, send, recv, a, sib))
        waits.append(_rcopy(theirs, theirs, send, recv, a, sib))
    return copies, waits


def _pair_fill(gfs, sm4):
    n = len(gfs)
    hs = SMALL_ROWS // 2

    def body(*refs):
        g_refs, sm4_ref = refs[n + 1:2 * n + 1], refs[2 * n + 1]
        send, recv = refs[2 * n + 2:]
        x, y, c, me, sib, others = _place()
        copies, waits = _pair_fill_copies(g_refs, send, recv)
        for j, (px, py) in enumerate(others):
            chip = 2 * px + py
            mine, theirs = _half(sm4_ref.at[chip], c, hs), _half(sm4_ref.at[chip], 1 - c, hs)
            copies.append(pltpu.make_async_remote_copy(src_ref=mine, dst_ref=mine, send_sem=send.at[n + j],
                                                       recv_sem=recv.at[n + j], device_id=sib, device_id_type=MESH))
            waits.append(pltpu.make_async_remote_copy(src_ref=theirs, dst_ref=theirs, send_sem=send.at[n + j],
                                                      recv_sem=recv.at[n + j], device_id=sib, device_id_type=MESH))
        for cp in copies:
            cp.start()
        for w in waits:
            w.wait_recv()
        for cp in copies:
            cp.wait_send()

    return pl.pallas_call(
        body, name="pair_fill", in_specs=[ANY] * (n + 1), out_specs=[ANY] * (n + 1),
        out_shape=[jax.ShapeDtypeStruct(g.shape, g.dtype) for g in gfs] + [jax.ShapeDtypeStruct(sm4.shape, sm4.dtype)],
        input_output_aliases={i: i for i in range(n + 1)},
        scratch_shapes=[pltpu.SemaphoreType.DMA((n + 3,)), pltpu.SemaphoreType.DMA((n + 3,))],
    )(*gfs, sm4)


def _adamw_math(w, g, m, v):
    m = ADAM_B1 * m + (1.0 - ADAM_B1) * g
    v = ADAM_B2 * v + (1.0 - ADAM_B2) * (g * g)
    m_hat = m / (1.0 - ADAM_B1 ** ADAM_STEP)
    v_hat = v / (1.0 - ADAM_B2 ** ADAM_STEP)
    return -ADAM_LR * (m_hat / (jnp.sqrt(v_hat) + ADAM_EPS) + ADAM_WD * w), m, v


def _adamw(items, steps, name):
    n = len(items)

    def body(*refs):
        for a in range(n):
            g = refs[4 * a + 1][...]
            d, mo, vo = _adamw_math(refs[4 * a][...], g, refs[4 * a + 2][...], refs[4 * a + 3][...])
            for out, val in zip(refs[4 * n + 4 * a:4 * n + 4 * a + 4], (g, d, mo, vo)):
                out[...] = val

    spec = lambda w: pl.BlockSpec((w.shape[0] // steps, w.shape[1]), lambda i: (i, 0))
    flat = pl.pallas_call(
        body, name=name, grid=(steps,), in_specs=[spec(it[0]) for it in items for _ in range(4)],
        out_specs=[spec(it[0]) for it in items for _ in range(4)],
        out_shape=[jax.ShapeDtypeStruct(it[0].shape, F32) for it in items for _ in range(4)],
        compiler_params=_params(("arbitrary",)),
    )(*[a for it in items for a in it])
    return [flat[4 * a:4 * a + 4] for a in range(n)]


def _adamw_small(sm4, wmv):
    views = SMALL_VIEWS[:-1]
    n = len(views)

    def body(sm4_ref, *refs):
        g_all = ((sm4_ref[0] + sm4_ref[1]) + sm4_ref[2]) + sm4_ref[3]
        for a, (name, rows, cols) in enumerate(views):
            row = SMALL_OFFSETS[name]
            g = g_all[row:row + rows, :cols]
            d, mo, vo = _adamw_math(refs[3 * a][...], g, refs[3 * a + 1][...], refs[3 * a + 2][...])
            for out, val in zip(refs[3 * n + 4 * a:3 * n + 4 * a + 4], (g, d, mo, vo)):
                out[...] = val
        row = SMALL_OFFSETS["loss"]
        refs[-1][...] = g_all[row:row + 1, :128]

    flat = pl.pallas_call(
        body, name="adamw_small",
        out_shape=[jax.ShapeDtypeStruct((rows, cols), F32) for _, rows, cols in views for _ in range(4)]
        + [jax.ShapeDtypeStruct((1, 128), F32)],
        compiler_params=pltpu.CompilerParams(vmem_limit_bytes=VMEM_LIMIT),
    )(sm4, *[a for t in wmv for a in t])
    return [flat[4 * a:4 * a + 4] for a in range(n)] + [flat[-1]]


SMALL_NAMES = ("attn_pre_norm", "mla_q_norm", "mla_kv_norm", "mla_w_ukv", "mla_out_norm", "hgrn_lb_logits",
               "hgrn_out_norm", "attn_post_norm", "ffn_pre_norm", "ffn_post_norm")
BIG_NAMES = ("w_in", "mla_w_uq", "w_out", "w_gate", "w_up", "w_down")
WEIGHT_NAMES = ("attn_pre_norm", "w_in", "mla_q_norm", "mla_w_uq", "mla_kv_norm", "mla_w_ukv", "mla_out_norm",
                "hgrn_lb_logits", "hgrn_out_norm", "w_out", "attn_post_norm", "ffn_pre_norm", "w_gate", "w_up", "w_down",
                "ffn_post_norm")


UQ_COMM_SHAPE = (192, 384)


def _pack_small(vals):
    parts, row = [], 0
    for name, rows, cols in sorted(SMALL_VIEWS, key=lambda view: SMALL_OFFSETS[view[0]]):
        assert SMALL_OFFSETS[name] == row
        parts.append(jnp.pad(vals[name].reshape(rows, cols), ((0, 0), (0, D_MODEL - cols))))
        row += rows
    parts.append(jnp.zeros((SMALL_ROWS - row, D_MODEL), F32))
    return jnp.concatenate(parts, axis=0)


def kernel(x, positions, attn_pre_norm, w_in, mla_q_norm, mla_w_uq, mla_kv_norm, mla_w_ukv, mla_out_norm, hgrn_lb_logits, hgrn_out_norm, w_out, attn_post_norm, ffn_pre_norm, w_gate, w_up, w_down, ffn_post_norm, loss_target, m_attn_pre_norm, m_w_in, m_mla_q_norm, m_mla_w_uq, m_mla_kv_norm, m_mla_w_ukv, m_mla_out_norm, m_hgrn_lb_logits, m_hgrn_out_norm, m_w_out, m_attn_post_norm, m_ffn_pre_norm, m_w_gate, m_w_up, m_w_down, m_ffn_post_norm, v_attn_pre_norm, v_w_in, v_mla_q_norm, v_mla_w_uq, v_mla_kv_norm, v_mla_w_ukv, v_mla_out_norm, v_hgrn_lb_logits, v_hgrn_out_norm, v_w_out, v_attn_post_norm, v_ffn_pre_norm, v_w_gate, v_w_up, v_w_down, v_ffn_post_norm):
    args = locals()
    W = {n: args[n] for n in WEIGHT_NAMES}
    M = {n: args["m_" + n] for n in WEIGHT_NAMES}
    V = {n: args["v_" + n] for n in WEIGHT_NAMES}
    T = x.shape[1]
    cx, cy, cc = lax.axis_index("x"), lax.axis_index("y"), lax.axis_index("c")

    win_rows = D_IN // N_CHIPS
    shard2d = {"w_in": (win_rows, D_MODEL), "mla_w_uq": (Q_RANK // N_CHIPS, MLA_HEADS * MLA_QK),
               "w_out": (D_MODEL // N_CHIPS, D_MODEL), "w_gate": (FF_SHARD, D_MODEL), "w_up": (FF_SHARD, D_MODEL),
               "w_down": (FF_SHARD, D_MODEL)}
    transposed = ("w_in", "w_gate", "w_up")
    to2d = lambda n, a: a[0].T if n in transposed else a.reshape(shard2d[n])
    from2d = lambda n, t: t.T[None] if n in transposed else t.reshape(W[n].shape)
    me = 2 * cx + cy
    place = jnp.stack([cc, me]).astype(jnp.int32)
    local_b = [to2d(n, W[n]).astype(BF16) for n in BIG_NAMES]
    local_b[0] = jnp.pad(local_b[0], ((0, WIN_COMM_SHAPE[1] - win_rows), (0, 0)))
    stacks = _gather_chips(local_b[:2], "gather_weights")
    win4, wuq4 = [lax.dynamic_update_slice(s, l[None], (me, 0, 0)) for s, l in zip(stacks, local_b)]
    win_t = win4[:, :win_rows].reshape(D_IN, D_MODEL)
    wuq_full = wuq4.reshape(Q_RANK, MLA_HEADS, MLA_QK)
    win_arr, wq_arr, wk_arr, wv_arr = _arrange_weights(win_t, wuq_full, mla_w_ukv[0].astype(BF16))
    small = {n: W[n][0] if n == "mla_w_ukv" else W[n].reshape(-1, W[n].shape[-1]) for n in SMALL_NAMES}

    loss_local, dx, grads, ffn_final = _local_step(x[0], positions.reshape(T, 1), loss_target[0], small, win_arr,
                                                           wq_arr, wk_arr, wv_arr, local_b[2:], place)

    gs = [grads["w_in"], grads["mla_w_uq"].reshape((N_CHIPS,) + UQ_COMM_SHAPE)]
    sm = _pack_small({**grads, "loss": loss_local})
    *rs, ssib = _pair_swap(gs, (sm,))
    *ps, pair = _pair_sum(place, gs, rs, small=(sm, ssib))
    ffn_names, rest_names = BIG_NAMES[3:], BIG_NAMES[:2]
    g2d = dict(zip(ffn_names + BIG_NAMES[2:3], ffn_final))
    adam_in = lambda names_: [(to2d(n, W[n]), g2d[n], to2d(n, M[n]), to2d(n, V[n])) for n in names_]
    early_names = ffn_names + BIG_NAMES[2:3]
    updates = dict(zip(early_names, _adamw(adam_in(early_names), 4, "adamw_ffn")))
    *ris, sm4 = _chip_swap(ps, pair)
    *gfin, smf = _pair_fill(_chip_sum(place, gs, rs, ris), sm4)

    g2d.update({n: gfin[k].reshape((-1,) + shard2d[n][1:]) for k, n in enumerate(rest_names)})
    updates.update(zip(rest_names, _adamw(adam_in(rest_names), 1, "adamw_w_in")))
    G, DW, NM, NV = {}, {}, {}, {}
    for n, outs in updates.items():
        G[n], DW[n], NM[n], NV[n] = (from2d(n, t) for t in outs)
    view2d = lambda n, a: a.reshape(next((r, c) for name, r, c in SMALL_VIEWS if name == n))
    *res, loss_row = _adamw_small(smf, [tuple(view2d(n, t[n]) for t in (W, M, V)) for n in SMALL_NAMES])
    for n, outs in zip(SMALL_NAMES, res):
        G[n], DW[n], NM[n], NV[n] = (t.reshape(W[n].shape) for t in outs)
    loss = loss_row[0, 0]
    return (loss, dx[None], *[G[n] for n in WEIGHT_NAMES], *[DW[n] for n in WEIGHT_NAMES],
            *[NM[n] for n in WEIGHT_NAMES], *[NV[n] for n in WEIGHT_NAMES])
```

```python
import jax
import jax.numpy as jnp
from jax import lax
from jax.experimental import pallas as pl
from jax.experimental.pallas import tpu as pltpu

F32 = jnp.float32
BF16 = jnp.bfloat16
MXU_DTYPE = BF16

D_MODEL = 1024
MLA_HEADS = 8
MLA_NOPE = 64
MLA_ROPE = 32
MLA_V = 64
MLA_QK = MLA_NOPE + MLA_ROPE
Q_RANK = 384
KV_RANK = 128
MLA_WIDTH = MLA_HEADS * MLA_V
HEAD_PAD = 128
HGRN_HEADS = 4
HGRN_DIM = 128
HGRN_WIDTH = HGRN_HEADS * HGRN_DIM
CHUNK = 64
SUB = 16
HGRN_CPI = 4
D_IN = Q_RANK + KV_RANK + MLA_ROPE + 4 * HGRN_WIDTH
D_IN_ARR = Q_RANK + KV_RANK + HEAD_PAD + 4 * HGRN_WIDTH
D_FF = 2816
N_CHIPS = 4
FF_SHARD = D_FF // N_CHIPS
EPS = 1e-6
ROPE_THETA = 10000.0
ATTN_SCALE = MLA_QK ** -0.5
ATTN_SCALE_LOG2 = ATTN_SCALE * 1.4426950408889634
NEG_BIG = -1e30

ADAM_LR = 0.001
ADAM_B1 = 0.9
ADAM_B2 = 0.999
ADAM_EPS = 1e-08
ADAM_WD = 0.01
ADAM_STEP = 10

VMEM_LIMIT = 56 * 1024 * 1024
FFN_BWD_VMEM = 62 * 1024 * 1024

SMALL_VIEWS = (("attn_pre_norm", 1, 1024), ("mla_q_norm", 1, 384), ("mla_kv_norm", 1, 128), ("mla_w_ukv", 128, 1024),
               ("mla_out_norm", 1, 512), ("hgrn_lb_logits", 2, 512), ("hgrn_out_norm", 1, 512),
               ("attn_post_norm", 1, 1024), ("ffn_pre_norm", 1, 1024), ("ffn_post_norm", 1, 1024), ("loss", 1, 1))
ROW_TILE = 8


def _small_layout():
    offsets, row = {}, 0
    for whole in (True, False):
        for name, rows, _ in SMALL_VIEWS:
            if (rows % ROW_TILE == 0) == whole:
                offsets[name] = row
                row += rows
    return offsets, -(-row // (2 * ROW_TILE)) * 2 * ROW_TILE


SMALL_OFFSETS, SMALL_ROWS = _small_layout()

MESH = pl.DeviceIdType.MESH
ANY = pl.BlockSpec(memory_space=pl.ANY)


def _dot(a, b, dims, exact):
    if exact:
        return lax.dot_general(a.astype(F32), b.astype(F32), (dims, ((), ())), precision=lax.Precision.HIGH,
                               preferred_element_type=F32)
    return lax.dot_general(a.astype(MXU_DTYPE), b.astype(MXU_DTYPE), (dims, ((), ())), preferred_element_type=F32)


def _mm(a, b, exact=False):
    return _dot(a, b, ((1,), (0,)), exact)


def _mm_nt(a, b, exact=False):
    return _dot(a, b, ((1,), (1,)), exact)


def _mm_tn(a, b, exact=False):
    return _dot(a, b, ((0,), (0,)), exact)


def _rms_fwd(x, w):
    r = lax.rsqrt(jnp.mean(x * x, axis=-1, keepdims=True) + EPS)
    xn = x * r
    return xn * w, xn, r


def _rms_bwd(dy, xn, r, w):
    dxn = dy * w
    dx = r * (dxn - xn * jnp.mean(dxn * xn, axis=-1, keepdims=True))
    dw = jnp.sum(dy * xn, axis=0, keepdims=True)
    return dx, dw


def _group_sums(v, gs):
    t, n = v.shape
    lane = lax.broadcasted_iota(jnp.int32, (t, 128), 1)
    out = []
    for p in range(n // 128):
        vb = v[:, 128 * p:128 * (p + 1)]
        if gs == 128:
            out.append(jnp.sum(vb, axis=-1, keepdims=True))
        else:
            out.append(jnp.sum(jnp.where(lane < 64, vb, 0.0), axis=-1, keepdims=True))
            out.append(jnp.sum(jnp.where(lane >= 64, vb, 0.0), axis=-1, keepdims=True))
    return out


def _group_bcast(sums, gs, t):
    lane = lax.broadcasted_iota(jnp.int32, (t, 128), 1)
    if gs == 128:
        return jnp.concatenate([jnp.broadcast_to(s, (t, 128)) for s in sums], axis=-1)
    return jnp.concatenate([jnp.where(lane < 64, sums[2 * p], sums[2 * p + 1]) for p in range(len(sums) // 2)],
                           axis=-1)


def _grms_fwd(x, w, gs):
    t = x.shape[0]
    r = lax.rsqrt(_group_bcast(_group_sums(x * x, gs), gs, t) * (1.0 / gs) + EPS)
    xn = x * r
    return xn * w, xn, r


def _grms_bwd(dy, xn, r, w, gs):
    t = dy.shape[0]
    dxn = dy * w
    dx = r * (dxn - xn * (_group_bcast(_group_sums(dxn * xn, gs), gs, t) * (1.0 / gs)))
    dw = jnp.sum(dy * xn, axis=0, keepdims=True)
    return dx, dw


def _rope_tables(c_tab, s_tab):
    lane = lax.broadcasted_iota(jnp.int32, c_tab.shape, 1)
    first = (lane >= MLA_NOPE) & (lane < MLA_NOPE + MLA_ROPE // 2)
    second = (lane >= MLA_NOPE + MLA_ROPE // 2) & (lane < MLA_QK)
    return c_tab, jnp.where(first, -s_tab, 0.0), jnp.where(second, s_tab, 0.0)


def _rope(v, c, sa, sb):
    return v * c + pltpu.roll(v, HEAD_PAD - MLA_ROPE // 2, 1) * sa + pltpu.roll(v, MLA_ROPE // 2, 1) * sb


def _rope_bwd(d, c, sa, sb):
    return d * c - pltpu.roll(d, HEAD_PAD - MLA_ROPE // 2, 1) * sa - pltpu.roll(d, MLA_ROPE // 2, 1) * sb


def _params(sem, vmem=VMEM_LIMIT):
    return pltpu.CompilerParams(dimension_semantics=sem, vmem_limit_bytes=vmem)


def _in_fwd(x, pos, invf, w_pre, win, qnw, wq, kvnw, wk, wv, tt=512):
    T = x.shape[0]

    def body(x_ref, pos_ref, invf_ref, wpre_ref, win_ref, qnw_ref, wq_ref, kvnw_ref, wk_ref, wv_ref,
             cq_ref, ckv_ref, xph_ref, q_ref, k_ref, v_ref, kt_ref, vt_ref, rc_ref, rs_ref):
        u, _, _ = _rms_fwd(x_ref[...], wpre_ref[...])
        lo = Q_RANK + KV_RANK + HEAD_PAD
        xp = _mm_nt(u, win_ref[:lo, :])
        xph_ref[...] = _mm_nt(u, win_ref[lo:, :])
        cq = xp[:, :Q_RANK]
        ckv = xp[:, Q_RANK:Q_RANK + KV_RANK]
        kr = xp[:, Q_RANK + KV_RANK:]
        cq_ref[...] = cq
        ckv_ref[...] = ckv
        ang = pos_ref[...].astype(F32) * invf_ref[...]
        c_tab = jnp.cos(ang)
        s_tab = jnp.sin(ang)
        rc_ref[...] = c_tab
        rs_ref[...] = s_tab
        c, sa, sb = _rope_tables(c_tab, s_tab)
        qn, _, _ = _rms_fwd(cq, qnw_ref[...])
        q = _mm(qn, wq_ref[...])
        kvn, _, _ = _rms_fwd(ckv, kvnw_ref[...])
        kn = _mm(kvn, wk_ref[...])
        v = _mm(kvn, wv_ref[...])
        v_ref[...] = v.astype(v_ref.dtype)
        vt_ref[...] = v.T.astype(vt_ref.dtype)
        krr = _rope(kr, c, sa, sb)
        for h in range(MLA_HEADS):
            sl = slice(HEAD_PAD * h, HEAD_PAD * (h + 1))
            q_ref[:, sl] = (_rope(q[:, sl], c, sa, sb) * ATTN_SCALE_LOG2).astype(q_ref.dtype)
            kh = kn[:, sl] + krr
            k_ref[:, sl] = kh.astype(k_ref.dtype)
            kt_ref[sl, :] = kh.T.astype(kt_ref.dtype)

    row = lambda w: pl.BlockSpec((tt, w), lambda i: (i, 0))
    full = lambda a: pl.BlockSpec(a.shape, lambda i: (0,) * a.ndim)
    qk_w = MLA_HEADS * HEAD_PAD
    return pl.pallas_call(
        body, name="in_fwd", grid=(T // tt,),
        in_specs=[row(D_MODEL), row(1), full(invf), full(w_pre), full(win), full(qnw), full(wq), full(kvnw),
                  full(wk), full(wv)],
        out_specs=[row(Q_RANK), row(KV_RANK), row(4 * HGRN_WIDTH), row(qk_w), row(qk_w), row(MLA_WIDTH),
                   pl.BlockSpec((qk_w, tt), lambda i: (0, i)), pl.BlockSpec((MLA_WIDTH, tt), lambda i: (0, i)),
                   row(HEAD_PAD), row(HEAD_PAD)],
        out_shape=[jax.ShapeDtypeStruct((T, Q_RANK), F32), jax.ShapeDtypeStruct((T, KV_RANK), F32),
                   jax.ShapeDtypeStruct((T, 4 * HGRN_WIDTH), F32), jax.ShapeDtypeStruct((T, qk_w), MXU_DTYPE),
                   jax.ShapeDtypeStruct((T, qk_w), MXU_DTYPE), jax.ShapeDtypeStruct((T, MLA_WIDTH), MXU_DTYPE),
                   jax.ShapeDtypeStruct((qk_w, T), MXU_DTYPE), jax.ShapeDtypeStruct((MLA_WIDTH, T), MXU_DTYPE),
                   jax.ShapeDtypeStruct((T, HEAD_PAD), F32), jax.ShapeDtypeStruct((T, HEAD_PAD), F32)],
        compiler_params=_params(("arbitrary",)),
    )(x, pos, invf, w_pre, win, qnw, wq, kvnw, wk, wv)


def _attn_fwd_t(qb, kb, vt, gather=(), tq=256, hps=8):
    T = qb.shape[0]
    nq = T // tq
    ng = len(gather)
    steps = (MLA_HEADS // hps) * nq
    pass_on = steps - 3

    def body(q_ref, k_ref, vt_ref, *rest):
        o_ref, lse_ref = rest[ng:ng + 2]
        acc_scr = rest[2 * ng + 2]
        qi = pl.program_id(1)
        step_no = pl.program_id(0) * nq + qi
        if ng:
            gat = _Gather(rest[:ng], rest[ng + 2:2 * ng + 2], *rest[2 * ng + 3:])

            @pl.when(step_no == 0)
            def _():
                for cp in gat.sends():
                    cp.start()

            @pl.when(step_no == pass_on)
            def _():
                for arrival in gat.arrivals():
                    arrival.wait_recv()
                for cp in gat.forwards():
                    cp.start()

        heads = [slice(HEAD_PAD * a, HEAD_PAD * (a + 1)) for a in range(hps)]
        acc_scr[...] = jnp.zeros_like(acc_scr)

        def step(j, carry, masked):
            start = pl.multiple_of(j * tq, tq)
            scores = [_mm_nt(k_ref[pl.ds(start, tq), heads[a]], q_ref[:, heads[a]]) for a in range(hps)]
            new, probs, alphas = [], [], []
            for a in range(hps):
                m, l = carry[a]
                s = scores[a]
                if masked:
                    kk = lax.broadcasted_iota(jnp.int32, (tq, tq), 0)
                    qq = lax.broadcasted_iota(jnp.int32, (tq, tq), 1)
                    s = jnp.where(kk <= qq, s, NEG_BIG)
                m_new = jnp.maximum(m, jnp.max(s, axis=0, keepdims=True))
                alpha = jnp.exp2(m - m_new)
                p = jnp.exp2(s - m_new)
                l = l * alpha + jnp.sum(p, axis=0, keepdims=True)
                new.append((m_new, l))
                probs.append(p.astype(MXU_DTYPE))
                alphas.append(alpha)
                if a % 2:
                    pr = a // 2
                    vtj = vt_ref[2 * MLA_V * pr:2 * MLA_V * (pr + 1), pl.ds(start, tq)]
                    none = jnp.zeros((MLA_V, tq), vtj.dtype)
                    pv = (_mm(jnp.concatenate([vtj[:MLA_V], none], axis=0), probs[a - 1])
                          + _mm(jnp.concatenate([none, vtj[MLA_V:]], axis=0), probs[a]))
                    acc_scr[pr] = acc_scr[pr] * jnp.where(row < MLA_V, alphas[a - 1], alphas[a]) + pv
            return tuple(new)

        row = lax.broadcasted_iota(jnp.int32, (2 * MLA_V, tq), 0)
        init = tuple((jnp.full((1, tq), NEG_BIG, F32), jnp.zeros((1, tq), F32)) for _ in range(hps))
        carry = lax.fori_loop(0, qi, lambda j, c: step(j, c, False), init)
        carry = step(qi, carry, True)
        for pr in range(hps // 2):
            (m0, l0), (m1, l1) = carry[2 * pr], carry[2 * pr + 1]
            ot = acc_scr[pr] / jnp.where(row < MLA_V, l0, l1)
            o_ref[:, 2 * MLA_V * pr:2 * MLA_V * (pr + 1)] = ot.T
            lse_ref[pr, 0:1, :] = m0 + jnp.log2(l0)
            lse_ref[pr, 1:2, :] = m1 + jnp.log2(l1)

        if ng:
            @pl.when(step_no == steps - 1)
            def _():
                for arrival in gat.forward_arrivals():
                    arrival.wait_recv()
                for cp in gat.sends() + gat.forwards():
                    cp.wait_send()

    return pl.pallas_call(
        body, name="attn_fwd", grid=(MLA_HEADS // hps, nq),
        in_specs=[pl.BlockSpec((tq, hps * HEAD_PAD), lambda g, i: (i, g)),
                  pl.BlockSpec((T, hps * HEAD_PAD), lambda g, i: (0, g)),
                  pl.BlockSpec((hps * MLA_V, T), lambda g, i: (g, 0))] + [ANY] * ng,
        out_specs=[pl.BlockSpec((tq, hps * MLA_V), lambda g, i: (i, g)),
                   pl.BlockSpec((hps // 2, 2, tq), lambda g, i: (g, 0, i))] + [ANY] * ng,
        out_shape=[jax.ShapeDtypeStruct((T, MLA_WIDTH), F32), jax.ShapeDtypeStruct((MLA_HEADS // 2, 2, T), F32)]
        + _Gather.out_shapes(gather),
        scratch_shapes=[pltpu.VMEM((hps // 2, 2 * MLA_V, tq), F32)] + (_Gather.semaphores(gather) if ng else []),
        compiler_params=_params(("arbitrary", "arbitrary")),
    )(qb, kb, vt, *gather)


def _attn_bwd_t(qb, kb, kt, vb, dob, lse, dvec, send=(), tq=512, hps=4):
    T = qb.shape[0]
    nq = T // tq
    ns = len(send)
    steps = (MLA_HEADS // hps) * nq

    def body(q_ref, k_ref, kt_ref, v_ref, do_ref, lse_ref, d_ref, *rest):
        dqt_ref, dk_ref, dv_ref = rest[ns:ns + 3]
        va_scr, dv_scr = rest[2 * ns + 3:2 * ns + 5]
        j = pl.program_id(1)
        step_no = pl.program_id(0) * nq + j
        if ns:
            @pl.when(step_no == 0)
            def _():
                for cp in _chip_swap_copies(rest[:ns], rest[ns + 3:2 * ns + 3], *rest[2 * ns + 5:]):
                    cp.start()

        @pl.when(j == 0)
        def _():
            dqt_ref[...] = jnp.zeros_like(dqt_ref)

        lane = lax.broadcasted_iota(jnp.int32, (tq, 2 * MLA_V), 1)
        heads = [slice(HEAD_PAD * a, HEAD_PAD * (a + 1)) for a in range(hps)]
        pairs = [slice(2 * MLA_V * p, 2 * MLA_V * (p + 1)) for p in range(hps // 2)]
        for pr in range(hps // 2):
            vpair = v_ref[:, pairs[pr]]
            va_scr[2 * pr] = jnp.where(lane < MLA_V, vpair, jnp.zeros_like(vpair))
            va_scr[2 * pr + 1] = jnp.where(lane >= MLA_V, vpair, jnp.zeros_like(vpair))
        dk_ref[...] = jnp.zeros_like(dk_ref)
        dv_scr[...] = jnp.zeros_like(dv_scr)

        def step(i, masked):
            start = pl.multiple_of(i * tq, tq)
            rows = pl.ds(start, tq)
            scores = [_mm_nt(k_ref[:, heads[a]], q_ref[rows, heads[a]]) for a in range(hps)]
            dps = [_mm_nt(va_scr[a], do_ref[rows, pairs[a // 2]]) for a in range(hps)]
            for a in range(hps):
                pr, r = a // 2, a % 2
                p = jnp.exp2(scores[a] - lse_ref[pr, r:r + 1, rows])
                if masked:
                    kk = lax.broadcasted_iota(jnp.int32, (tq, tq), 0)
                    qq = lax.broadcasted_iota(jnp.int32, (tq, tq), 1)
                    p = jnp.where(kk <= qq, p, 0.0)
                ds = p * (dps[a] - d_ref[pr, r:r + 1, rows])
                dv_scr[a] += _mm(p, do_ref[rows, pairs[pr]])
                dk_ref[:, heads[a]] += _mm(ds, q_ref[rows, heads[a]])
                dqt_ref[heads[a], rows] += _mm(kt_ref[heads[a], :], ds)

        def loop_body(i, _):
            step(i, False)
            return 0

        step(j, True)
        lax.fori_loop(j + 1, nq, loop_body, 0)
        for pr in range(hps // 2):
            dv_ref[:, pairs[pr]] = jnp.where(lane < MLA_V, dv_scr[2 * pr], dv_scr[2 * pr + 1])
        dk_ref[...] = dk_ref[...] * (ATTN_SCALE / ATTN_SCALE_LOG2)

        if ns:
            @pl.when(step_no == steps - 1)
            def _():
                for cp in _chip_swap_copies(rest[:ns], rest[ns + 3:2 * ns + 3], *rest[2 * ns + 5:]):
                    cp.wait()

    stat = pl.BlockSpec((hps // 2, 2, T), lambda g, j: (g, 0, 0))
    return pl.pallas_call(
        body, name="attn_bwd", grid=(MLA_HEADS // hps, nq),
        in_specs=[pl.BlockSpec((T, hps * HEAD_PAD), lambda g, j: (0, g)),
                  pl.BlockSpec((tq, hps * HEAD_PAD), lambda g, j: (j, g)),
                  pl.BlockSpec((hps * HEAD_PAD, tq), lambda g, j: (g, j)),
                  pl.BlockSpec((tq, hps * MLA_V), lambda g, j: (j, g)),
                  pl.BlockSpec((T, hps * MLA_V), lambda g, j: (0, g)), stat, stat] + [ANY] * ns,
        out_specs=[pl.BlockSpec((hps * HEAD_PAD, T), lambda g, j: (g, 0)),
                   pl.BlockSpec((tq, hps * HEAD_PAD), lambda g, j: (j, g)),
                   pl.BlockSpec((tq, hps * MLA_V), lambda g, j: (j, g))] + [ANY] * ns,
        out_shape=[jax.ShapeDtypeStruct((MLA_HEADS * HEAD_PAD, T), F32),
                   jax.ShapeDtypeStruct((T, MLA_HEADS * HEAD_PAD), F32),
                   jax.ShapeDtypeStruct((T, MLA_WIDTH), F32)] + _chip_swap_shapes(send),
        scratch_shapes=[pltpu.VMEM((hps, tq, 2 * MLA_V), vb.dtype), pltpu.VMEM((hps, tq, 2 * MLA_V), F32)]
        + ([pltpu.SemaphoreType.DMA((3 * ns,)), pltpu.SemaphoreType.DMA((3 * ns,))] if ns else []),
        compiler_params=_params(("arbitrary", "arbitrary")),
    )(qb, kb, kt, vb, dob, lse, dvec, *send)


def _cumsum_rows(x):
    n = x.shape[0]
    row = lax.broadcasted_iota(jnp.int32, x.shape, 0)
    s = 1
    while s < n:
        x = x + jnp.where(row >= s, pltpu.roll(x, s, 0), 0.0)
        s *= 2
    return x


def _rev_cumsum_rows(x):
    n = x.shape[0]
    row = lax.broadcasted_iota(jnp.int32, x.shape, 0)
    s = 1
    while s < n:
        x = x + jnp.where(row < n - s, pltpu.roll(x, n - s, 0), 0.0)
        s *= 2
    return x


def _lb_from_logits(l):
    l0, l1 = l[0:1, :], l[1:2, :]
    m = jnp.maximum(l0, l1)
    e0, e1 = jnp.exp(l0 - m), jnp.exp(l1 - m)
    return e0 / (e0 + e1)


def _hgrn_gates(hq, hf, lb):
    sig_f = jax.nn.sigmoid(hf)
    f = lb + (1.0 - lb) * sig_f
    sig_q = jax.nn.sigmoid(hq)
    return sig_f, f, jnp.log(f), 1.0 - f, sig_q, hq * sig_q


def _hgrn_intra(q, kk, b, exact=False):
    row = lax.broadcasted_iota(jnp.int32, b.shape, 0)
    qs, ks, eqs, eks, a_rows = [], [], [], [], []
    for i in range(CHUNK // SUB):
        ref = b[SUB * i + SUB // 2:SUB * i + SUB // 2 + 1, :]
        eq = jnp.exp(b[SUB * i:SUB * (i + 1), :] - ref)
        ek = jnp.exp(jnp.where(row < SUB * (i + 1), ref - b, NEG_BIG))
        qi = q[SUB * i:SUB * (i + 1), :] * eq
        ki = kk * ek
        a_rows.append(_mm_nt(qi, ki, exact))
        qs.append(qi), ks.append(ki), eqs.append(eq), eks.append(ek)
    tt = lax.broadcasted_iota(jnp.int32, (CHUNK, CHUNK), 0)
    ss = lax.broadcasted_iota(jnp.int32, (CHUNK, CHUNK), 1)
    causal = ss <= tt
    a = jnp.where(causal, jnp.concatenate(a_rows, axis=0), 0.0)
    return a, causal, qs, ks, eqs, eks


def _hgrn_fwd(xph, lbl, tg=512):
    T = xph.shape[0]
    ng, ncg = T // tg, tg // CHUNK
    cols = [slice(HGRN_DIM * h, HGRN_DIM * (h + 1)) for h in range(HGRN_HEADS)]

    def body(lbl_ref, hq_ref, hf_ref, hi_ref, o_ref, st_ref, s_scr):
        @pl.when(pl.program_id(0) == 0)
        def _():
            s_scr[...] = jnp.zeros_like(s_scr)

        lb = _lb_from_logits(lbl_ref[...])

        def chunks(it, _):
            pre = []
            for k in range(HGRN_CPI):
                c = it * HGRN_CPI + k
                rows = pl.ds(pl.multiple_of(c * CHUNK, CHUNK), CHUNK)
                for cs in cols:
                    _, _, lf, kk, _, q = _hgrn_gates(hq_ref[rows, cs], hf_ref[rows, cs], lb[:, cs])
                    v = hi_ref[rows, cs]
                    b = _cumsum_rows(lf)
                    a = _hgrn_intra(q, kk, b)[0]
                    b_last = b[CHUNK - 1:CHUNK, :]
                    pre.append((c, rows, q * jnp.exp(b), a, v, jnp.exp(b_last), _mm_tn(v, kk * jnp.exp(b_last - b))))
            for i, (c, rows, qe, a, v, ebl, upd) in enumerate(pre):
                h = i % HGRN_HEADS
                st = s_scr[h]
                st_ref[h, c] = st
                o_ref[rows, cols[h]] = _mm_nt(qe, st) + _mm(a, v)
                s_scr[h] = st * ebl + upd
            return 0

        lax.fori_loop(0, ncg // HGRN_CPI, chunks, 0)

    col = lambda k: pl.BlockSpec((tg, HGRN_WIDTH), lambda g: (g, k))
    return pl.pallas_call(
        body, name="hgrn_fwd", grid=(ng,),
        in_specs=[pl.BlockSpec((2, HGRN_WIDTH), lambda g: (0, 0)), col(0), col(1), col(2)],
        out_specs=[col(0), pl.BlockSpec((HGRN_HEADS, ncg, HGRN_DIM, HGRN_DIM), lambda g: (0, g, 0, 0))],
        out_shape=[jax.ShapeDtypeStruct((T, HGRN_WIDTH), F32),
                   jax.ShapeDtypeStruct((HGRN_HEADS, T // CHUNK, HGRN_DIM, HGRN_DIM), F32)],
        scratch_shapes=[pltpu.VMEM((HGRN_HEADS, HGRN_DIM, HGRN_DIM), F32)],
        compiler_params=_params(("arbitrary",)),
    )(lbl, xph, xph, xph)


def _hgrn_bwd(xph, lbl, states, d_o, fill=(), tg=512):
    T = xph.shape[0]
    ng, ncg = T // tg, tg // CHUNK
    cols = [slice(HGRN_DIM * h, HGRN_DIM * (h + 1)) for h in range(HGRN_HEADS)]
    nsub = CHUNK // SUB
    nf = len(fill)

    def body(lbl_ref, hq_ref, hf_ref, hi_ref, st_ref, do_ref, *rest):
        dhq_ref, dhf_ref, dhi_ref, dlg_ref = rest[nf:nf + 4]
        ds_scr, dlb_scr = rest[2 * nf + 4:2 * nf + 6]
        fill_copies = lambda: _pair_fill_copies(rest[nf + 4:2 * nf + 4], *rest[2 * nf + 6:])
        g = pl.program_id(0)

        @pl.when(g == 0)
        def _():
            ds_scr[...] = jnp.zeros_like(ds_scr)
            dlb_scr[...] = jnp.zeros_like(dlb_scr)
            for cp in (fill_copies()[0] if nf else ()):
                cp.start()

        lb = _lb_from_logits(lbl_ref[...])

        def chunks(it, _):
            pre = []
            for k, h in ((k, h) for k in range(HGRN_CPI) for h in range(HGRN_HEADS)):
                cs = cols[h]
                c = ncg - 1 - (it * HGRN_CPI + k)
                rows = pl.ds(pl.multiple_of(c * CHUNK, CHUNK), CHUNK)
                hq = hq_ref[rows, cs]
                sig_f, f, lf, kk, sig_q, q = _hgrn_gates(hq, hf_ref[rows, cs], lb[:, cs])
                v = hi_ref[rows, cs]
                do = do_ref[rows, cs]
                b = _cumsum_rows(lf)
                eb = jnp.exp(b)
                a, causal, qs, ks, eqs, eks = _hgrn_intra(q, kk, b)
                b_last = b[CHUNK - 1:CHUNK, :]
                st = st_ref[h, c]
                pre.append(dict(h=h, cs=cs, rows=rows, hq=hq, sig_f=sig_f, f=f, kk=kk, sig_q=sig_q, q=q, v=v, eb=eb, qs=qs,
                                ks=ks, eqs=eqs,
                                eks=eks, ebl=jnp.exp(b_last), el=jnp.exp(b_last - b), st=st,
                                da=jnp.where(causal, _mm_nt(do, v, True), 0.0), dq=_mm(do, st, True) * eb,
                                dv=_mm_tn(a, do), dsu=_mm_tn(do, q * eb, True)))
            for w in pre:
                dq_rows = []
                dk = jnp.zeros_like(w["q"])
                for i in range(nsub):
                    dai = w["da"][SUB * i:SUB * (i + 1), :]
                    dq_rows.append(_mm(dai, w["ks"][i], True) * w["eqs"][i])
                    dk = dk + _mm_tn(dai, w["qs"][i], True) * w["eks"][i]
                w["dq"] = w["dq"] + jnp.concatenate(dq_rows, axis=0)
                w["dk"] = dk
            for w in pre:
                h, cs, rows = w["h"], w["cs"], w["rows"]
                kk, el, ebl, dst = w["kk"], w["el"], w["ebl"], ds_scr[h]
                dk_state = _mm(w["v"], dst, True) * el
                dk = w["dk"] + dk_state
                e_last = (ebl * jnp.sum(w["st"] * dst, axis=0, keepdims=True)
                          + jnp.sum(kk * dk_state, axis=0, keepdims=True))
                dlf = _rev_cumsum_rows(w["q"] * w["dq"] - kk * dk) + e_last
                ds_scr[h] = dst * ebl + w["dsu"]
                df = dlf / w["f"] - dk
                sig_f, sig_q = w["sig_f"], w["sig_q"]
                dhf_ref[rows, cs] = df * (1.0 - lb[:, cs]) * sig_f * (1.0 - sig_f)
                dlb_scr[:, cs] += jnp.sum(df * (1.0 - sig_f), axis=0, keepdims=True)
                dhq_ref[rows, cs] = w["dq"] * sig_q * (1.0 + w["hq"] * (1.0 - sig_q))
                dhi_ref[rows, cs] = w["dv"] + _mm_nt(kk * el, dst)
            return 0

        lax.fori_loop(0, ncg // HGRN_CPI, chunks, 0)

        @pl.when(g == ng - 1)
        def _():
            dl0 = dlb_scr[...] * lb * (1.0 - lb)
            dlg_ref[...] = jnp.concatenate([dl0, -dl0], axis=0)
            if nf:
                copies, waits = fill_copies()
                for w in waits:
                    w.wait_recv()
                for cp in copies:
                    cp.wait_send()

    col = lambda k: pl.BlockSpec((tg, HGRN_WIDTH), lambda g: (ng - 1 - g, k))
    logits = pl.BlockSpec((2, HGRN_WIDTH), lambda g: (0, 0))
    big = jax.ShapeDtypeStruct((T, HGRN_WIDTH), F32)
    n_in, n_out = 6, 4
    return pl.pallas_call(
        body, name="hgrn_bwd", grid=(ng,),
        in_specs=[logits, col(0), col(1), col(2),
                  pl.BlockSpec((HGRN_HEADS, ncg, HGRN_DIM, HGRN_DIM), lambda g: (0, ng - 1 - g, 0, 0)), col(0)] + [ANY] * nf,
        out_specs=[col(0), col(0), col(0), logits] + [ANY] * nf,
        out_shape=[big, big, big, jax.ShapeDtypeStruct((2, HGRN_WIDTH), F32)]
        + [jax.ShapeDtypeStruct(f.shape, f.dtype) for f in fill],
        input_output_aliases={n_in + k: n_out + k for k in range(nf)},
        scratch_shapes=[pltpu.VMEM((HGRN_HEADS, HGRN_DIM, HGRN_DIM), F32), pltpu.VMEM((1, HGRN_WIDTH), F32)]
        + ([pltpu.SemaphoreType.DMA((nf,)), pltpu.SemaphoreType.DMA((nf,))] if nf else []),
        compiler_params=_params(("arbitrary",)),
    )(lbl, xph, xph, xph, states, d_o, *fill)


def _proj_fwd(x, o_raw, oh_raw, xph, wout, w_mla, w_hg, w_post, w_fpre, tt=512):
    T = x.shape[0]

    def body(x_ref, o_ref, oh_ref, hg_ref, wout_ref, wmla_ref, whg_ref, wpost_ref, wfpre_ref,
             h1_ref, z_ref, mix_ref):
        om, _, _ = _grms_fwd(o_ref[...], wmla_ref[...], MLA_V)
        hg = hg_ref[...]
        ohn, _, _ = _grms_fwd(oh_ref[...], whg_ref[...], HGRN_DIM)
        mix = jnp.concatenate([om, ohn * (hg * jax.nn.sigmoid(hg))], axis=-1)
        mix_ref[...] = mix.astype(mix_ref.dtype)
        y1 = _mm(mix, wout_ref[...])
        h1 = x_ref[...] + _rms_fwd(y1, wpost_ref[...])[0]
        h1_ref[...] = h1
        z_ref[...] = _rms_fwd(h1, wfpre_ref[...])[0].astype(z_ref.dtype)

    row = lambda w: pl.BlockSpec((tt, w), lambda i: (i, 0))
    full = lambda a: pl.BlockSpec(a.shape, lambda i: (0,) * a.ndim)
    sds = jax.ShapeDtypeStruct
    return pl.pallas_call(
        body, name="proj_fwd", grid=(T // tt,),
        in_specs=[row(D_MODEL), row(MLA_WIDTH), row(HGRN_WIDTH), pl.BlockSpec((tt, HGRN_WIDTH), lambda i: (i, 3)),
                  full(wout), full(w_mla), full(w_hg), full(w_post), full(w_fpre)],
        out_specs=[row(D_MODEL)] * 3,
        out_shape=[sds((T, D_MODEL), F32), sds((T, D_MODEL), MXU_DTYPE), sds((T, D_MODEL), MXU_DTYPE)],
        compiler_params=_params(("arbitrary",)),
    )(x, o_raw, oh_raw, xph, wout, w_mla, w_hg, w_post, w_fpre)


def _ffn_fwd(zb, h1, tgt, w_fpost, wg, wu, wd, tt=256):
    T = zb.shape[0]
    nj = N_CHIPS

    def body(z_ref, h1_ref, tgt_ref, wfpost_ref, wg_ref, wu_ref, wd_ref, g_ref, up_ref, dy2_ref, dh2_ref, loss_ref, dwf_ref):
        @pl.when(pl.program_id(0) == 0)
        def _():
            loss_ref[...] = jnp.zeros_like(loss_ref)
            dwf_ref[...] = jnp.zeros_like(dwf_ref)

        z = z_ref[...]
        gs = [_mm_nt(z, wg_ref[j]) for j in range(nj)]
        ups = [_mm_nt(z, wu_ref[j]) for j in range(nj)]
        y2 = jnp.zeros((tt, D_MODEL), F32)
        for j in range(nj):
            g_ref[j] = gs[j]
            up_ref[j] = ups[j]
            y2 = y2 + _mm(gs[j] * jax.nn.sigmoid(gs[j]) * ups[j], wd_ref[j])
        w = wfpost_ref[...]
        y2s, y2n, r2 = _rms_fwd(y2, w)
        e = h1_ref[...] + y2s - tgt_ref[...]
        loss_ref[...] += jnp.sum(e * e, axis=0, keepdims=True)
        dh2 = e * (1.0 / D_MODEL)
        dh2_ref[...] = dh2
        dy2, dwf = _rms_bwd(dh2, y2n, r2, w)
        dy2_ref[...] = dy2.astype(dy2_ref.dtype)
        dwf_ref[...] += dwf

    row = pl.BlockSpec((tt, D_MODEL), lambda i: (i, 0))
    vec = pl.BlockSpec((1, D_MODEL), lambda i: (0, 0))
    resident = pl.BlockSpec((nj, FF_SHARD, D_MODEL), lambda i: (0, 0, 0), pipeline_mode=pl.Buffered(1))
    act = pl.BlockSpec((nj, tt, FF_SHARD), lambda i: (0, i, 0))
    sds = jax.ShapeDtypeStruct
    return pl.pallas_call(
        body, name="ffn_fwd", grid=(T // tt,),
        in_specs=[row, row, row, vec, resident, resident, resident],
        out_specs=[act, act, row, row, vec, vec],
        out_shape=[sds((nj, T, FF_SHARD), F32), sds((nj, T, FF_SHARD), F32), sds((T, D_MODEL), MXU_DTYPE),
                   sds((T, D_MODEL), F32), sds((1, D_MODEL), F32), sds((1, D_MODEL), F32)],
        compiler_params=_params(("arbitrary",)),
    )(zb, h1, tgt, w_fpost, wg, wu, wd)


def _ffn_bwd(zb, g, up, dy2b, wg, wu, wd, tt=512):
    T = zb.shape[0]
    nj = N_CHIPS

    def body(z_ref, g_ref, up_ref, dy2_ref, wg_ref, wu_ref, wd_ref, dwg_ref, dwu_ref, dwd_ref, dz_ref, acc_ref):
        j, i = pl.program_id(0), pl.program_id(1)
        rows = pl.ds(pl.multiple_of(i * tt, tt), tt)

        @pl.when(i == 0)
        def _():
            dwg_ref[...] = jnp.zeros_like(dwg_ref)
            dwu_ref[...] = jnp.zeros_like(dwu_ref)
            dwd_ref[...] = jnp.zeros_like(dwd_ref)

        z, g_, up_, dy2 = z_ref[...], g_ref[0], up_ref[0], dy2_ref[...]
        sg = jax.nn.sigmoid(g_)
        act = g_ * sg
        dff = _mm_nt(dy2, wd_ref[0])
        dwd_ref[0] += _mm_tn(act * up_, dy2)
        dg = dff * up_ * sg * (1.0 + g_ * (1.0 - sg))
        dup = dff * act
        dwg_ref[0] += _mm_tn(dg, z)
        dwu_ref[0] += _mm_tn(dup, z)
        dz = _mm(dg, wg_ref[0]) + _mm(dup, wu_ref[0])

        @pl.when(j == 0)
        def _():
            acc_ref[rows, :] = dz

        @pl.when((j > 0) & (j < nj - 1))
        def _():
            acc_ref[rows, :] += dz

        @pl.when(j == nj - 1)
        def _():
            dz_ref[...] = acc_ref[rows, :] + dz

    row = pl.BlockSpec((tt, D_MODEL), lambda j, i: (i, 0))
    act = pl.BlockSpec((1, tt, FF_SHARD), lambda j, i: (j, i, 0))
    w_sh = pl.BlockSpec((1, FF_SHARD, D_MODEL), lambda j, i: (j, 0, 0))
    w_grad = jax.ShapeDtypeStruct((nj, FF_SHARD, D_MODEL), F32)
    return pl.pallas_call(
        body, name="ffn_bwd", grid=(nj, T // tt),
        in_specs=[row, act, act, row, w_sh, w_sh, w_sh],
        out_specs=[w_sh, w_sh, w_sh, pl.BlockSpec((tt, D_MODEL), lambda j, i: (jnp.where(j == nj - 1, i, 0), 0))],
        out_shape=[w_grad, w_grad, w_grad, jax.ShapeDtypeStruct((T, D_MODEL), F32)],
        scratch_shapes=[pltpu.VMEM((T, D_MODEL), F32)],
        compiler_params=_params(("arbitrary", "arbitrary"), vmem=FFN_BWD_VMEM),
    )(zb, g, up, dy2b, wg, wu, wd)


RING = 3


def _mid_bwd(dz, dh2, h1, mixb, o_raw, oh_raw, xph, wout, w_fpre, w_post, w_mla, w_hg, swap=(), tt=512):
    T = dh2.shape[0]
    nsw = len(swap)
    n_in, n_out = 12, 10

    def body(*refs):
        (dz_ref, dh2_ref, h1_ref, mix_ref, o_ref, oh_ref, hg_ref, wout_ref, wfpre_ref, wpost_ref,
         wmla_ref, whg_ref) = refs[:n_in]
        (dh1_ref, dwout_ref, do_ref, doh_ref, dhg_ref, dvec_ref, dwfpre_ref, dwpost_ref, dwmla_ref,
         dwhg_ref) = refs[n_in + nsw:n_in + nsw + n_out]
        scratch = refs[n_in + nsw + n_out + (nsw + 1 if nsw else 0):]
        ring_bufs, ring_sem, sems = scratch[:3], scratch[3], scratch[4:]
        step = pl.program_id(0)

        def ring_copy(a, s):
            slot = lax.rem(s, RING)
            return pltpu.make_async_copy(refs[a].at[pl.ds(pl.multiple_of(s * tt, tt), tt)], ring_bufs[a].at[slot],
                                         ring_sem.at[RING * a + slot])

        swap_copies = lambda: _pair_swap_copies(refs[n_in:n_in + nsw], refs[n_in + nsw + n_out:n_in + 2 * nsw + n_out],
                                                *sems)

        def wout_copies():
            _, _, c, _, sib, _ = _place()
            rw_ref, h = refs[n_in + 2 * nsw + n_out], D_MODEL // N_CHIPS // 2
            return [_rcopy(dwout_ref.at[pl.ds(pl.multiple_of((2 * k + 1 - c) * h, 8), h)], rw_ref.at[k], *sems, nsw + k, sib)
                    for k in range(N_CHIPS)]

        @pl.when(pl.program_id(0) == 0)
        def _():
            for r in (dwout_ref, dwfpre_ref, dwpost_ref, dwmla_ref, dwhg_ref):
                r[...] = jnp.zeros_like(r)
            for cp in (swap_copies() if nsw else ()):
                cp.start()
            for s0 in range(min(RING - 1, T // tt)):
                for a in range(3):
                    ring_copy(a, s0).start()

        @pl.when(step + RING - 1 < T // tt)
        def _():
            for a in range(3):
                ring_copy(a, step + RING - 1).start()

        for a in range(3):
            ring_copy(a, step).wait()
        slot = lax.rem(step, RING)
        dz = ring_bufs[0][slot]
        wfpre = wfpre_ref[...]
        _, h1n, r = _rms_fwd(ring_bufs[2][slot], wfpre)
        dh1_z, dwfpre = _rms_bwd(dz, h1n, r, wfpre)
        dwfpre_ref[...] += dwfpre
        dh1 = ring_bufs[1][slot] + dh1_z
        dh1_ref[...] = dh1
        wpost = wpost_ref[...]
        _, y1n, r1 = _rms_fwd(_mm(mix_ref[...], wout_ref[...]), wpost)
        dy1, dwpost = _rms_bwd(dh1, y1n, r1, wpost)
        dwpost_ref[...] += dwpost
        dmix = _mm_nt(dy1, wout_ref[...])
        dwout_ref[...] += _mm_tn(mix_ref[...], dy1)
        wmla = wmla_ref[...]
        o = o_ref[...]
        _, on, ro = _grms_fwd(o, wmla, MLA_V)
        d_o, dwmla = _grms_bwd(dmix[:, :MLA_WIDTH], on, ro, wmla, MLA_V)
        dwmla_ref[...] += dwmla
        do_ref[...] = d_o.astype(do_ref.dtype)
        hh = lax.broadcasted_iota(jnp.int32, (MLA_HEADS, MLA_WIDTH), 0)
        ll = lax.broadcasted_iota(jnp.int32, (MLA_HEADS, MLA_WIDTH), 1)
        sel = jnp.where((ll >= hh * MLA_V) & (ll < (hh + 1) * MLA_V), 1.0, 0.0)
        dvec_ref[...] = _mm_nt(sel, d_o * o, True)
        whg = whg_ref[...]
        hg = hg_ref[...]
        sg = jax.nn.sigmoid(hg)
        _, ohn, rh = _grms_fwd(oh_ref[...], whg, HGRN_DIM)
        dmh = dmix[:, MLA_WIDTH:]
        dhg_ref[...] = dmh * ohn * whg * sg * (1.0 + hg * (1.0 - sg))
        d_oh, dwhg = _grms_bwd(dmh * (hg * sg), ohn, rh, whg, HGRN_DIM)
        dwhg_ref[...] += dwhg
        doh_ref[...] = d_oh

        if nsw:
            @pl.when(pl.program_id(0) == T // tt - 1)
            def _():
                for cp in wout_copies():
                    cp.start()
                for cp in swap_copies() + wout_copies():
                    cp.wait()

    row = lambda w: pl.BlockSpec((tt, w), lambda i: (i, 0))
    full = lambda a: pl.BlockSpec(a.shape, lambda i: (0,) * a.ndim)
    vec = lambda w: pl.BlockSpec((1, w), lambda i: (0, 0))
    sds = jax.ShapeDtypeStruct
    return pl.pallas_call(
        body, name="mid_bwd", grid=(T // tt,),
        in_specs=[ANY, ANY, ANY,
                  row(D_MODEL), row(MLA_WIDTH), row(HGRN_WIDTH), pl.BlockSpec((tt, HGRN_WIDTH), lambda i: (i, 3)),
                  full(wout), vec(D_MODEL), vec(D_MODEL), vec(MLA_WIDTH), vec(HGRN_WIDTH)] + [ANY] * nsw,
        out_specs=[row(D_MODEL), full(wout), row(MLA_WIDTH), row(HGRN_WIDTH), row(HGRN_WIDTH),
                   pl.BlockSpec((MLA_HEADS, tt), lambda i: (0, i)),
                   vec(D_MODEL), vec(D_MODEL), vec(MLA_WIDTH), vec(HGRN_WIDTH)] + [ANY] * (nsw + 1 if nsw else 0),
        out_shape=[sds((T, D_MODEL), F32), sds(wout.shape, F32), sds((T, MLA_WIDTH), MXU_DTYPE), sds((T, HGRN_WIDTH), F32),
                   sds((T, HGRN_WIDTH), F32), sds((MLA_HEADS, T), F32),
                   sds((1, D_MODEL), F32), sds((1, D_MODEL), F32), sds((1, MLA_WIDTH), F32), sds((1, HGRN_WIDTH), F32)]
        + (_half_stack_shapes(list(swap) + [sds((N_CHIPS, D_MODEL // N_CHIPS, D_MODEL), F32)]) if nsw else []),
        scratch_shapes=[pltpu.VMEM((RING, tt, D_MODEL), F32)] * 3 + [pltpu.SemaphoreType.DMA((3 * RING,))]
        + ([pltpu.SemaphoreType.DMA((nsw + N_CHIPS,)), pltpu.SemaphoreType.DMA((nsw + N_CHIPS,))] if nsw else []),
        compiler_params=_params(("arbitrary",)),
    )(dz, dh2, h1, mixb, o_raw, oh_raw, xph, wout, w_fpre, w_post, w_mla, w_hg, *swap)


def _in_bwd(x, dh1, cq, ckv, dq, dk, dv, dhq, dhf, dhi, dhg, rc, rs, w_pre, win, qnw, wq, kvnw, wk, wv, tt=256):
    T = x.shape[0]

    def body(x_ref, dh1_ref, cq_ref, ckv_ref, dq_ref, dk_ref, dv_ref, dhq_ref, dhf_ref, dhi_ref, dhg_ref, rc_ref, rs_ref,
             wpre_ref, win_ref, qnw_ref, wq_ref, kvnw_ref, wk_ref, wv_ref,
             dx_ref, dwin_ref, dwq_ref, dwk_ref, dwv_ref, dwpre_ref, dqnw_ref, dkvnw_ref):
        @pl.when(pl.program_id(0) == 0)
        def _():
            for r in (dwin_ref, dwq_ref, dwk_ref, dwv_ref, dwpre_ref, dqnw_ref, dkvnw_ref):
                r[...] = jnp.zeros_like(r)

        def add_win_grad(r, first):
            for arr0, n, chip, row0 in _win_grad_segments():
                if first <= arr0 and arr0 + n <= first + r.shape[0]:
                    dwin_ref[chip, row0:row0 + n, :] += r[arr0 - first:arr0 - first + n]

        lo = Q_RANK + KV_RANK + HEAD_PAD
        dxp_h = jnp.concatenate([dhq_ref[...], dhf_ref[...], dhi_ref[...], dhg_ref[...]], axis=-1)
        du = _mm(dxp_h, win_ref[lo:, :])
        wpre = wpre_ref[...]
        u, xn, rx = _rms_fwd(x_ref[...], wpre)
        add_win_grad(_mm_tn(dxp_h, u), lo)
        c, sa, sb = _rope_tables(rc_ref[...], rs_ref[...])
        lane = lax.broadcasted_iota(jnp.int32, (tt, HEAD_PAD), 1)
        dk_all = dk_ref[...]
        dq_lin = []
        dkr = jnp.zeros((tt, HEAD_PAD), F32)
        for h in range(MLA_HEADS):
            sl = slice(HEAD_PAD * h, HEAD_PAD * (h + 1))
            dq_lin.append(_rope_bwd(dq_ref[sl, :].T * ATTN_SCALE, c, sa, sb))
            dkr = dkr + dk_all[:, sl]
        dq_lin = jnp.concatenate(dq_lin, axis=-1)
        dkr = jnp.where((lane >= MLA_NOPE) & (lane < MLA_QK), _rope_bwd(dkr, c, sa, sb), 0.0)
        qnw = qnw_ref[...]
        qn, cqn, rq = _rms_fwd(cq_ref[...], qnw)
        dwq_ref[...] += _mm_tn(qn, dq_lin)
        dcq, dqnw = _rms_bwd(_mm_nt(dq_lin, wq_ref[...]), cqn, rq, qnw)
        dqnw_ref[...] += dqnw
        kvnw = kvnw_ref[...]
        kvn, ckvn, rkv = _rms_fwd(ckv_ref[...], kvnw)
        dv_ = dv_ref[...]
        dwk_ref[...] += _mm_tn(kvn, dk_all)
        dwv_ref[...] += _mm_tn(kvn, dv_)
        dckv, dkvnw = _rms_bwd(_mm_nt(dk_all, wk_ref[...]) + _mm_nt(dv_, wv_ref[...]), ckvn, rkv, kvnw)
        dkvnw_ref[...] += dkvnw
        dxp_a = jnp.concatenate([dcq, dckv, dkr], axis=-1)
        add_win_grad(_mm_tn(dxp_a, u), 0)
        dx_u, dwpre = _rms_bwd(du + _mm(dxp_a, win_ref[:lo, :]), xn, rx, wpre)
        dwpre_ref[...] += dwpre
        dx_ref[...] = dh1_ref[...] + dx_u

    row = lambda w: pl.BlockSpec((tt, w), lambda i: (i, 0))
    full = lambda a: pl.BlockSpec(a.shape, lambda i: (0,) * a.ndim)
    sds = jax.ShapeDtypeStruct
    qk_w = MLA_HEADS * HEAD_PAD
    return pl.pallas_call(
        body, name="in_bwd", grid=(T // tt,),
        in_specs=[row(D_MODEL), row(D_MODEL), row(Q_RANK), row(KV_RANK), pl.BlockSpec((qk_w, tt), lambda i: (0, i)),
                  row(qk_w), row(MLA_WIDTH),
                  row(HGRN_WIDTH), row(HGRN_WIDTH), row(HGRN_WIDTH), row(HGRN_WIDTH), row(HEAD_PAD), row(HEAD_PAD),
                  full(w_pre), full(win), full(qnw), full(wq), full(kvnw), full(wk), full(wv)],
        out_specs=[row(D_MODEL), pl.BlockSpec(WIN_COMM_SHAPE, lambda i: (0, 0, 0)), full(wq), full(wk), full(wv),
                   full(w_pre), full(qnw), full(kvnw)],
        out_shape=[sds((T, D_MODEL), F32), sds(WIN_COMM_SHAPE, F32), sds(wq.shape, F32), sds(wk.shape, F32),
                   sds(wv.shape, F32), sds(w_pre.shape, F32), sds(qnw.shape, F32), sds(kvnw.shape, F32)],
        compiler_params=_params(("arbitrary",)),
    )(x, dh1, cq, ckv, dq, dk, dv, dhq, dhf, dhi, dhg, rc, rs, w_pre, win, qnw, wq, kvnw, wk, wv)


def _arrange_weights(win_t, wuq_full, wukv):
    dt = win_t.dtype
    z = lambda n: jnp.zeros((n, D_MODEL), dt)
    s2 = Q_RANK + KV_RANK
    win_arr = jnp.concatenate([win_t[:s2], z(MLA_NOPE), win_t[s2:s2 + MLA_ROPE], z(HEAD_PAD - MLA_QK),
                               win_t[s2 + MLA_ROPE:]], axis=0)
    wq_arr = jnp.pad(wuq_full, ((0, 0), (0, 0), (0, HEAD_PAD - MLA_QK))).reshape(Q_RANK, MLA_HEADS * HEAD_PAD)
    wk_arr = jnp.pad(wukv[:, :, :MLA_NOPE], ((0, 0), (0, 0), (0, HEAD_PAD - MLA_NOPE))).reshape(
        KV_RANK, MLA_HEADS * HEAD_PAD)
    wv_arr = wukv[:, :, MLA_NOPE:].reshape(KV_RANK, MLA_WIDTH)
    return win_arr, wq_arr, wk_arr, wv_arr


WIN_COMM_SHAPE = (N_CHIPS, -(-D_IN // N_CHIPS // 32) * 32, D_MODEL)


def _win_grad_segments():
    s2 = Q_RANK + KV_RANK
    runs = [(0, s2, 0), (s2, s2 + MLA_ROPE, MLA_NOPE), (s2 + MLA_ROPE, D_IN, HEAD_PAD - MLA_ROPE)]
    per = D_IN // N_CHIPS
    segs = []
    for lo, hi, shift in runs:
        for k in range(N_CHIPS):
            a, b = max(lo, per * k), min(hi, per * (k + 1))
            if a < b:
                segs.append((a + shift, b - a, k, a - per * k))
    return segs


def _unarrange_grads(dwq_arr, dwk_arr, dwv_arr):
    dwuq = dwq_arr.reshape(Q_RANK, MLA_HEADS, HEAD_PAD)[:, :, :MLA_QK]
    dwukv = jnp.concatenate([dwk_arr.reshape(KV_RANK, MLA_HEADS, HEAD_PAD)[:, :, :MLA_NOPE],
                             dwv_arr.reshape(KV_RANK, MLA_HEADS, MLA_V)], axis=-1)
    return dwuq, dwukv


def _rope_inv_freq():
    inv = 1.0 / (ROPE_THETA ** (jnp.arange(0, MLA_ROPE, 2, dtype=F32) / MLA_ROPE))
    z = lambda n: jnp.zeros((n,), F32)
    return jnp.concatenate([z(MLA_NOPE), inv, inv, z(HEAD_PAD - MLA_QK)]).reshape(1, HEAD_PAD)


def _local_step(x, pos, tgt, small, win_arr, wq_arr, wk_arr, wv_arr, late, place=None):
    invf = _rope_inv_freq()
    cq, ckv, xph, qb, kb, vb, kt, vt, rc, rs = _in_fwd(x, pos, invf, small["attn_pre_norm"], win_arr, small["mla_q_norm"],
                                               wq_arr, small["mla_kv_norm"], wk_arr, wv_arr)
    if place is None:
        o_raw, lse = _attn_fwd_t(qb, kb, vt)
        wout, wg, wu, wd = late
    else:
        o_raw, lse, *stacks = _attn_fwd_t(qb, kb, vt, gather=late)
        wout, wg, wu, wd = [lax.dynamic_update_slice(s, l[None], (place[1], 0, 0)) for s, l in zip(stacks, late)]
        wout = wout.reshape(D_MODEL, D_MODEL)
    oh_raw, states = _hgrn_fwd(xph, small["hgrn_lb_logits"])
    h1, zb, mixb = _proj_fwd(x, o_raw, oh_raw, xph, wout, small["mla_out_norm"], small["hgrn_out_norm"],
                                 small["attn_post_norm"], small["ffn_pre_norm"])
    g, up, dy2b, dh2, loss_acc, d_fpost = _ffn_fwd(zb, h1, tgt, small["ffn_post_norm"], wg, wu, wd)
    dwg, dwu, dwd, dz = _ffn_bwd(zb, g, up, dy2b, wg, wu, wd)
    ffn_grads = [] if place is None else [dwg, dwu, dwd]
    dh1, dwout, d_o, d_oh, dhg, dvec, d_fpre, d_post, d_mla, d_hg, *ffn_rs = _mid_bwd(
        dz, dh2, h1, mixb, o_raw, oh_raw, xph, wout, small["ffn_pre_norm"], small["attn_post_norm"],
        small["mla_out_norm"], small["hgrn_out_norm"], swap=ffn_grads)
    if ffn_grads:
        ffn_grads = ffn_grads + [dwout.reshape(N_CHIPS, D_MODEL // N_CHIPS, D_MODEL)]
    ffn_ps = _pair_sum(place, ffn_grads, ffn_rs, name="pair_sum_ffn") if ffn_grads else []
    dq, dk, dv, *ffn_ris = _attn_bwd_t(qb, kb, kt, vb, d_o, lse, dvec.reshape(lse.shape), send=ffn_ps)
    ffn_sums = _chip_sum(place, ffn_grads, ffn_rs, ffn_ris, name="chip_sum_ffn") if ffn_grads else []
    dhq, dhf, dhi, d_lbl, *ffn_final = _hgrn_bwd(xph, small["hgrn_lb_logits"], states, d_oh, fill=ffn_sums)
    dx, dwin4, dwq_arr, dwk_arr, dwv_arr, d_pre, d_qn, d_kvn = _in_bwd(
        x, dh1, cq, ckv, dq, dk, dv, dhq, dhf, dhi, dhg, rc, rs, small["attn_pre_norm"], win_arr,
        small["mla_q_norm"], wq_arr, small["mla_kv_norm"], wk_arr, wv_arr)
    dwuq, dwukv = _unarrange_grads(dwq_arr, dwk_arr, dwv_arr)
    loss = 0.5 * jnp.sum(loss_acc) * (1.0 / D_MODEL)
    grads = dict(attn_pre_norm=d_pre, w_in=dwin4, mla_q_norm=d_qn, mla_w_uq=dwuq, mla_kv_norm=d_kvn, mla_w_ukv=dwukv,
                 mla_out_norm=d_mla, hgrn_lb_logits=d_lbl, hgrn_out_norm=d_hg, w_out=dwout, attn_post_norm=d_post,
                 ffn_pre_norm=d_fpre, w_gate=dwg, w_up=dwu, w_down=dwd, ffn_post_norm=d_fpost)
    if place is None:
        return loss, dx, grads
    return loss, dx, grads, ffn_final


def _place():
    x, y, c = lax.axis_index("x"), lax.axis_index("y"), lax.axis_index("c")
    others = [(1 - x, y), (x, 1 - y), (1 - x, 1 - y)]
    return x, y, c, 2 * x + y, (x, y, 1 - c), others


def _half(ref, c, rows):
    return ref.at[pl.ds(pl.multiple_of(c * rows, 8), rows)]


def _rcopy(src, dst, send, recv, k, to):
    return pltpu.make_async_remote_copy(src_ref=src, dst_ref=dst, send_sem=send.at[k], recv_sem=recv.at[k],
                                        device_id=to, device_id_type=MESH)


class _Gather:
    def __init__(self, ins, outs, send, recv):
        self.ins, self.outs, self.send, self.recv = ins, outs, send, recv
        self.n = len(ins)
        self.halves = [r.shape[0] // 2 for r in ins]
        _, _, self.c, self.me, self.sib, self.others = _place()

    def _each(self):
        for j, (px, py) in enumerate(self.others):
            for a in range(self.n):
                yield j * self.n + a, a, 2 * px + py, (px, py, self.c)

    def sends(self):
        return [_rcopy(_half(self.ins[a], self.c, self.halves[a]), _half(self.outs[a].at[self.me], self.c, self.halves[a]),
                       self.send, self.recv, k, to) for k, a, _, to in self._each()]

    def arrivals(self):
        parts = [(k, _half(self.outs[a].at[chip], self.c, self.halves[a]), to) for k, a, chip, to in self._each()]
        return [_rcopy(p, p, self.send, self.recv, k, to) for k, p, to in parts]

    def forwards(self):
        parts = [(k, _half(self.outs[a].at[chip], self.c, self.halves[a])) for k, a, chip, _ in self._each()]
        return [_rcopy(p, p, self.send, self.recv, 3 * self.n + k, self.sib) for k, p in parts]

    def forward_arrivals(self):
        parts = [(k, _half(self.outs[a].at[chip], 1 - self.c, self.halves[a])) for k, a, chip, _ in self._each()]
        return [_rcopy(p, p, self.send, self.recv, 3 * self.n + k, self.sib) for k, p in parts]

    @staticmethod
    def out_shapes(arrs):
        return [jax.ShapeDtypeStruct((N_CHIPS,) + a.shape, a.dtype) for a in arrs]

    @staticmethod
    def semaphores(arrs):
        return [pltpu.SemaphoreType.DMA((6 * len(arrs),)), pltpu.SemaphoreType.DMA((6 * len(arrs),))]


def _gather_chips(arrs, name):
    n = len(arrs)

    def body(*refs):
        gat = _Gather(refs[:n], refs[n:2 * n], *refs[2 * n:])
        sends, forwards = gat.sends(), gat.forwards()
        for cp in sends:
            cp.start()
        for arrival, fw in zip(gat.arrivals(), forwards):
            arrival.wait_recv()
            fw.start()
        for arrival in gat.forward_arrivals():
            arrival.wait_recv()
        for cp in sends + forwards:
            cp.wait_send()

    return pl.pallas_call(body, name=name, in_specs=[ANY] * n, out_specs=[ANY] * n, out_shape=_Gather.out_shapes(arrs),
                          scratch_shapes=_Gather.semaphores(arrs))(*arrs)


def _grad_blocks(gs):
    return 2 if all(g.shape[1] // 2 % 32 == 0 for g in gs) else 1


def _pair_swap_copies(g_refs, r_refs, send, recv):
    _, _, c, _, sib, _ = _place()
    copies = []
    for a, (g, r) in enumerate(zip(g_refs, r_refs)):
        h = g.shape[1] // 2
        copies.append(_rcopy(g.at[:, pl.ds(pl.multiple_of((1 - c) * h, 8), h)], r, send, recv, a, sib))
    return copies


def _half_stack_shapes(gs, dtype=None):
    return [jax.ShapeDtypeStruct((N_CHIPS, g.shape[1] // 2, g.shape[2]), dtype or g.dtype) for g in gs]


def _pair_swap(gs, wholes):
    n, nw = len(gs), len(wholes)

    def body(*refs):
        ins, outs, (send, recv) = refs[:n + nw], refs[n + nw:2 * (n + nw)], refs[2 * (n + nw):]
        copies = _pair_swap_copies(ins[:n], outs[:n], send, recv)
        copies += [_rcopy(ins[n + k], outs[n + k], send, recv, n + k, _place()[4]) for k in range(nw)]
        for cp in copies:
            cp.start()
        for cp in copies:
            cp.wait()

    return pl.pallas_call(
        body, name="pair_swap", in_specs=[ANY] * (n + nw), out_specs=[ANY] * (n + nw),
        out_shape=_half_stack_shapes(gs) + [jax.ShapeDtypeStruct(w.shape, w.dtype) for w in wholes],
        scratch_shapes=[pltpu.SemaphoreType.DMA((n + nw,)), pltpu.SemaphoreType.DMA((n + nw,))],
    )(*gs, *wholes)


def _pair_sum(place, gs, rs, small=None, name="pair_sum"):
    n = len(gs)
    nb = _grad_blocks(gs)

    def body(place_ref, *refs):
        g_refs, r_refs, p_refs = refs[:n], refs[n:2 * n], refs[-n - 1:-1] if small else refs[-n:]
        for a in range(n):
            p_refs[a][0] = (g_refs[a][0] + r_refs[a][0]).astype(p_refs[a].dtype)
        if small:
            @pl.when((pl.program_id(0) == 0) & (pl.program_id(1) == 0))
            def _():
                refs[-1][...] = refs[2 * n][...] + refs[2 * n + 1][...]

    chip = lambda k, p: lax.rem(p[1] + 1 + k, N_CHIPS)
    in_specs, out_specs = [], []
    for g in gs:
        blk = (1, g.shape[1] // 2 // nb, g.shape[2])
        in_specs.append(pl.BlockSpec(blk, lambda i, k, p: (chip(k, p), p[0] * nb + i, 0)))
    for g in gs:
        blk = (1, g.shape[1] // 2 // nb, g.shape[2])
        in_specs.append(pl.BlockSpec(blk, lambda i, k, p: (chip(k, p), i, 0)))
        out_specs.append(pl.BlockSpec(blk, lambda i, k, p: (chip(k, p), i, 0)))
    out_shape = _half_stack_shapes(gs, BF16)
    if small:
        sm_spec = pl.BlockSpec(small[0].shape, lambda i, k, p: (0, 0))
        in_specs += [sm_spec, sm_spec]
        out_specs.append(sm_spec)
        out_shape.append(jax.ShapeDtypeStruct(small[0].shape, F32))
    return pl.pallas_call(
        body, name=name,
        grid_spec=pltpu.PrefetchScalarGridSpec(num_scalar_prefetch=1, grid=(nb, N_CHIPS - 1), in_specs=in_specs,
                                               out_specs=out_specs),
        out_shape=out_shape,
        compiler_params=_params(("arbitrary", "arbitrary")),
    )(place, *gs, *rs, *(small or ()))


def _chip_swap_copies(p_refs, ri_refs, send, recv):
    _, _, c, _, _, others = _place()
    n = len(p_refs)
    return [_rcopy(p_refs[a].at[2 * px + py], ri_refs[a].at[j], send, recv, j * n + a, (px, py, c))
            for j, (px, py) in enumerate(others) for a in range(n)]


def _chip_swap_shapes(ps):
    return [jax.ShapeDtypeStruct((3,) + p.shape[1:], p.dtype) for p in ps]


def _chip_swap(ps, pair):
    n = len(ps)

    def body(*refs):
        start, finish = _chip_swap_plan(refs[:n], refs[n], refs[n + 1:2 * n + 1], refs[2 * n + 1], *refs[2 * n + 2:])
        start()
        finish()

    return pl.pallas_call(
        body, name="chip_swap", in_specs=[ANY] * (n + 1), out_specs=[ANY] * (n + 1),
        out_shape=_chip_swap_out_shapes(ps, pair), scratch_shapes=_chip_swap_semaphores(n),
    )(*ps, pair)


def _chip_swap_plan(p_refs, pair_ref, ri_refs, sm4_ref, send, recv, lsem):
    n = len(p_refs)
    hs = SMALL_ROWS // 2
    x, y, c, me, sib, others = _place()
    local = pltpu.make_async_copy(pair_ref, sm4_ref.at[me], lsem.at[0])
    copies = _chip_swap_copies(p_refs, ri_refs, send, recv)
    arrivals = list(copies)
    for j, (px, py) in enumerate(others):
        copies.append(_rcopy(_half(pair_ref, c, hs), _half(sm4_ref.at[me], c, hs), send, recv, 3 * n + j, (px, py, c)))
        part = _half(sm4_ref.at[2 * px + py], c, hs)
        arrivals.append(_rcopy(part, part, send, recv, 3 * n + j, (px, py, c)))

    def start():
        local.start()
        for cp in copies:
            cp.start()

    def finish():
        for arrival in arrivals:
            arrival.wait_recv()
        for cp in copies:
            cp.wait_send()
        local.wait()

    return start, finish


def _chip_swap_out_shapes(ps, pair):
    return _chip_swap_shapes(ps) + [jax.ShapeDtypeStruct((N_CHIPS,) + pair.shape, pair.dtype)]


def _chip_swap_semaphores(n):
    k = 3 * (n + 1)
    return [pltpu.SemaphoreType.DMA((k,)), pltpu.SemaphoreType.DMA((k,)), pltpu.SemaphoreType.DMA((1,))]


def _chip_sum(place, gs, rs, ris, name="chip_sum"):
    n = len(gs)
    nb = 1

    def body(place_ref, *refs):
        g_refs, r_refs, ri_refs, o_refs = refs[:n], refs[n:2 * n], refs[2 * n:3 * n], refs[3 * n:]
        for a in range(n):
            ri = ri_refs[a]
            o_refs[a][...] = (g_refs[a][0] + r_refs[a][0]) + ri[0].astype(F32) + ri[1].astype(F32) + ri[2].astype(F32)

    in_specs, out_specs, out_shape = [], [], []
    for g in gs:
        blk = (1, g.shape[1] // 2 // nb, g.shape[2])
        in_specs.append(pl.BlockSpec(blk, lambda i, p: (p[1], p[0] * nb + i, 0)))
    for g in gs:
        blk = (1, g.shape[1] // 2 // nb, g.shape[2])
        in_specs.append(pl.BlockSpec(blk, lambda i, p: (p[1], i, 0)))
    for g in gs:
        rb = g.shape[1] // 2 // nb
        in_specs.append(pl.BlockSpec((3, rb, g.shape[2]), lambda i, p: (0, i, 0)))
        out_specs.append(pl.BlockSpec((rb, g.shape[2]), lambda i, p: (p[0] * nb + i, 0)))
        out_shape.append(jax.ShapeDtypeStruct(g.shape[1:], F32))
    return pl.pallas_call(
        body, name=name,
        grid_spec=pltpu.PrefetchScalarGridSpec(num_scalar_prefetch=1, grid=(nb,), in_specs=in_specs, out_specs=out_specs),
        out_shape=out_shape,
        compiler_params=_params(("arbitrary",)),
    )(place, *gs, *rs, *ris)


def _pair_fill_copies(g_refs, send, recv):
    _, _, c, _, sib, _ = _place()
    copies, waits = [], []
    for a, g in enumerate(g_refs):
        h = g.shape[0] // 2
        mine, theirs = _half(g, c, h), _half(g, 1 - c, h)
        copies.append(_rcopy(mine, mine, send, recv, a, sib))
        waits.append(_rcopy(theirs, theirs, send, recv, a, sib))
    return copies, waits


def _pair_fill(gfs, sm4):
    n = len(gfs)
    hs = SMALL_ROWS // 2

    def body(*refs):
        g_refs, sm4_ref = refs[n + 1:2 * n + 1], refs[2 * n + 1]
        send, recv = refs[2 * n + 2:]
        x, y, c, me, sib, others = _place()
        copies, waits = _pair_fill_copies(g_refs, send, recv)
        for j, (px, py) in enumerate(others):
            chip = 2 * px + py
            mine, theirs = _half(sm4_ref.at[chip], c, hs), _half(sm4_ref.at[chip], 1 - c, hs)
            copies.append(pltpu.make_async_remote_copy(src_ref=mine, dst_ref=mine, send_sem=send.at[n + j],
                                                       recv_sem=recv.at[n + j], device_id=sib, device_id_type=MESH))
            waits.append(pltpu.make_async_remote_copy(src_ref=theirs, dst_ref=theirs, send_sem=send.at[n + j],
                                                      recv_sem=recv.at[n + j], device_id=sib, device_id_type=MESH))
        for cp in copies:
            cp.start()
        for w in waits:
            w.wait_recv()
        for cp in copies:
            cp.wait_send()

    return pl.pallas_call(
        body, name="pair_fill", in_specs=[ANY] * (n + 1), out_specs=[ANY] * (n + 1),
        out_shape=[jax.ShapeDtypeStruct(g.shape, g.dtype) for g in gfs] + [jax.ShapeDtypeStruct(sm4.shape, sm4.dtype)],
        input_output_aliases={i: i for i in range(n + 1)},
        scratch_shapes=[pltpu.SemaphoreType.DMA((n + 3,)), pltpu.SemaphoreType.DMA((n + 3,))],
    )(*gfs, sm4)


def _adamw_math(w, g, m, v):
    m = ADAM_B1 * m + (1.0 - ADAM_B1) * g
    v = ADAM_B2 * v + (1.0 - ADAM_B2) * (g * g)
    m_hat = m / (1.0 - ADAM_B1 ** ADAM_STEP)
    v_hat = v / (1.0 - ADAM_B2 ** ADAM_STEP)
    return -ADAM_LR * (m_hat / (jnp.sqrt(v_hat) + ADAM_EPS) + ADAM_WD * w), m, v


def _adamw(items, steps, name):
    n = len(items)

    def body(*refs):
        for a in range(n):
            g = refs[4 * a + 1][...]
            d, mo, vo = _adamw_math(refs[4 * a][...], g, refs[4 * a + 2][...], refs[4 * a + 3][...])
            for out, val in zip(refs[4 * n + 4 * a:4 * n + 4 * a + 4], (g, d, mo, vo)):
                out[...] = val

    spec = lambda w: pl.BlockSpec((w.shape[0] // steps, w.shape[1]), lambda i: (i, 0))
    flat = pl.pallas_call(
        body, name=name, grid=(steps,), in_specs=[spec(it[0]) for it in items for _ in range(4)],
        out_specs=[spec(it[0]) for it in items for _ in range(4)],
        out_shape=[jax.ShapeDtypeStruct(it[0].shape, F32) for it in items for _ in range(4)],
        compiler_params=_params(("arbitrary",)),
    )(*[a for it in items for a in it])
    return [flat[4 * a:4 * a + 4] for a in range(n)]


def _adamw_small(sm4, wmv):
    views = SMALL_VIEWS[:-1]
    n = len(views)

    def body(sm4_ref, *refs):
        g_all = ((sm4_ref[0] + sm4_ref[1]) + sm4_ref[2]) + sm4_ref[3]
        for a, (name, rows, cols) in enumerate(views):
            row = SMALL_OFFSETS[name]
            g = g_all[row:row + rows, :cols]
            d, mo, vo = _adamw_math(refs[3 * a][...], g, refs[3 * a + 1][...], refs[3 * a + 2][...])
            for out, val in zip(refs[3 * n + 4 * a:3 * n + 4 * a + 4], (g, d, mo, vo)):
                out[...] = val
        row = SMALL_OFFSETS["loss"]
        refs[-1][...] = g_all[row:row + 1, :128]

    flat = pl.pallas_call(
        body, name="adamw_small",
        out_shape=[jax.ShapeDtypeStruct((rows, cols), F32) for _, rows, cols in views for _ in range(4)]
        + [jax.ShapeDtypeStruct((1, 128), F32)],
        compiler_params=pltpu.CompilerParams(vmem_limit_bytes=VMEM_LIMIT),
    )(sm4, *[a for t in wmv for a in t])
    return [flat[4 * a:4 * a + 4] for a in range(n)] + [flat[-1]]


SMALL_NAMES = ("attn_pre_norm", "mla_q_norm", "mla_kv_norm", "mla_w_ukv", "mla_out_norm", "hgrn_lb_logits",
               "hgrn_out_norm", "attn_post_norm", "ffn_pre_norm", "ffn_post_norm")
BIG_NAMES = ("w_in", "mla_w_uq", "w_out", "w_gate", "w_up", "w_down")
WEIGHT_NAMES = ("attn_pre_norm", "w_in", "mla_q_norm", "mla_w_uq", "mla_kv_norm", "mla_w_ukv", "mla_out_norm",
                "hgrn_lb_logits", "hgrn_out_norm", "w_out", "attn_post_norm", "ffn_pre_norm", "w_gate", "w_up", "w_down",
                "ffn_post_norm")


UQ_COMM_SHAPE = (192, 384)


def _pack_small(vals):
    parts, row = [], 0
    for name, rows, cols in sorted(SMALL_VIEWS, key=lambda view: SMALL_OFFSETS[view[0]]):
        assert SMALL_OFFSETS[name] == row
        parts.append(jnp.pad(vals[name].reshape(rows, cols), ((0, 0), (0, D_MODEL - cols))))
        row += rows
    parts.append(jnp.zeros((SMALL_ROWS - row, D_MODEL), F32))
    return jnp.concatenate(parts, axis=0)


def kernel(x, positions, attn_pre_norm, w_in, mla_q_norm, mla_w_uq, mla_kv_norm, mla_w_ukv, mla_out_norm, hgrn_lb_logits, hgrn_out_norm, w_out, attn_post_norm, ffn_pre_norm, w_gate, w_up, w_down, ffn_post_norm, loss_target, m_attn_pre_norm, m_w_in, m_mla_q_norm, m_mla_w_uq, m_mla_kv_norm, m_mla_w_ukv, m_mla_out_norm, m_hgrn_lb_logits, m_hgrn_out_norm, m_w_out, m_attn_post_norm, m_ffn_pre_norm, m_w_gate, m_w_up, m_w_down, m_ffn_post_norm, v_attn_pre_norm, v_w_in, v_mla_q_norm, v_mla_w_uq, v_mla_kv_norm, v_mla_w_ukv, v_mla_out_norm, v_hgrn_lb_logits, v_hgrn_out_norm, v_w_out, v_attn_post_norm, v_ffn_pre_norm, v_w_gate, v_w_up, v_w_down, v_ffn_post_norm):
    args = locals()
    W = {n: args[n] for n in WEIGHT_NAMES}
    M = {n: args["m_" + n] for n in WEIGHT_NAMES}
    V = {n: args["v_" + n] for n in WEIGHT_NAMES}
    T = x.shape[1]
    cx, cy, cc = lax.axis_index("x"), lax.axis_index("y"), lax.axis_index("c")

    win_rows = D_IN // N_CHIPS
    shard2d = {"w_in": (win_rows, D_MODEL), "mla_w_uq": (Q_RANK // N_CHIPS, MLA_HEADS * MLA_QK),
               "w_out": (D_MODEL // N_CHIPS, D_MODEL), "w_gate": (FF_SHARD, D_MODEL), "w_up": (FF_SHARD, D_MODEL),
               "w_down": (FF_SHARD, D_MODEL)}
    transposed = ("w_in", "w_gate", "w_up")
    to2d = lambda n, a: a[0].T if n in transposed else a.reshape(shard2d[n])
    from2d = lambda n, t: t.T[None] if n in transposed else t.reshape(W[n].shape)
    me = 2 * cx + cy
    place = jnp.stack([cc, me]).astype(jnp.int32)
    local_b = [to2d(n, W[n]).astype(BF16) for n in BIG_NAMES]
    local_b[0] = jnp.pad(local_b[0], ((0, WIN_COMM_SHAPE[1] - win_rows), (0, 0)))
    stacks = _gather_chips(local_b[:2], "gather_weights")
    win4, wuq4 = [lax.dynamic_update_slice(s, l[None], (me, 0, 0)) for s, l in zip(stacks, local_b)]
    win_t = win4[:, :win_rows].reshape(D_IN, D_MODEL)
    wuq_full = wuq4.reshape(Q_RANK, MLA_HEADS, MLA_QK)
    win_arr, wq_arr, wk_arr, wv_arr = _arrange_weights(win_t, wuq_full, mla_w_ukv[0].astype(BF16))
    small = {n: W[n][0] if n == "mla_w_ukv" else W[n].reshape(-1, W[n].shape[-1]) for n in SMALL_NAMES}

    loss_local, dx, grads, ffn_final = _local_step(x[0], positions.reshape(T, 1), loss_target[0], small, win_arr,
                                                           wq_arr, wk_arr, wv_arr, local_b[2:], place)

    gs = [grads["w_in"], grads["mla_w_uq"].reshape((N_CHIPS,) + UQ_COMM_SHAPE)]
    sm = _pack_small({**grads, "loss": loss_local})
    *rs, ssib = _pair_swap(gs, (sm,))
    *ps, pair = _pair_sum(place, gs, rs, small=(sm, ssib))
    ffn_names, rest_names = BIG_NAMES[3:], BIG_NAMES[:2]
    g2d = dict(zip(ffn_names + BIG_NAMES[2:3], ffn_final))
    adam_in = lambda names_: [(to2d(n, W[n]), g2d[n], to2d(n, M[n]), to2d(n, V[n])) for n in names_]
    early_names = ffn_names + BIG_NAMES[2:3]
    updates = dict(zip(early_names, _adamw(adam_in(early_names), 4, "adamw_ffn")))
    *ris, sm4 = _chip_swap(ps, pair)
    *gfin, smf = _pair_fill(_chip_sum(place, gs, rs, ris), sm4)

    g2d.update({n: gfin[k].reshape((-1,) + shard2d[n][1:]) for k, n in enumerate(rest_names)})
    updates.update(zip(rest_names, _adamw(adam_in(rest_names), 1, "adamw_w_in")))
    G, DW, NM, NV = {}, {}, {}, {}
    for n, outs in updates.items():
        G[n], DW[n], NM[n], NV[n] = (from2d(n, t) for t in outs)
    view2d = lambda n, a: a.reshape(next((r, c) for name, r, c in SMALL_VIEWS if name == n))
    *res, loss_row = _adamw_small(smf, [tuple(view2d(n, t[n]) for t in (W, M, V)) for n in SMALL_NAMES])
    for n, outs in zip(SMALL_NAMES, res):
        G[n], DW[n], NM[n], NV[n] = (t.reshape(W[n].shape) for t in outs)
    loss = loss_row[0, 0]
    return (loss, dx[None], *[G[n] for n in WEIGHT_NAMES], *[DW[n] for n in WEIGHT_NAMES],
            *[NM[n] for n in WEIGHT_NAMES], *[NV[n] for n in WEIGHT_NAMES])
```

```python
import jax
import jax.numpy as jnp
from jax import lax
from jax.experimental import pallas as pl
from jax.experimental.pallas import tpu as pltpu

F32 = jnp.float32
BF16 = jnp.bfloat16
MXU_DTYPE = BF16

D_MODEL = 1024
MLA_HEADS = 8
MLA_NOPE = 64
MLA_ROPE = 32
MLA_V = 64
MLA_QK = MLA_NOPE + MLA_ROPE
Q_RANK = 384
KV_RANK = 128
MLA_WIDTH = MLA_HEADS * MLA_V
HEAD_PAD = 128
HGRN_HEADS = 4
HGRN_DIM = 128
HGRN_WIDTH = HGRN_HEADS * HGRN_DIM
CHUNK = 64
SUB = 16
HGRN_CPI = 4
D_IN = Q_RANK + KV_RANK + MLA_ROPE + 4 * HGRN_WIDTH
D_IN_ARR = Q_RANK + KV_RANK + HEAD_PAD + 4 * HGRN_WIDTH
D_FF = 2816
N_CHIPS = 4
FF_SHARD = D_FF // N_CHIPS
EPS = 1e-6
ROPE_THETA = 10000.0
ATTN_SCALE = MLA_QK ** -0.5
ATTN_SCALE_LOG2 = ATTN_SCALE * 1.4426950408889634
NEG_BIG = -1e30

ADAM_LR = 0.001
ADAM_B1 = 0.9
ADAM_B2 = 0.999
ADAM_EPS = 1e-08
ADAM_WD = 0.01
ADAM_STEP = 10

VMEM_LIMIT = 56 * 1024 * 1024
FFN_BWD_VMEM = 62 * 1024 * 1024

SMALL_VIEWS = (("attn_pre_norm", 1, 1024), ("mla_q_norm", 1, 384), ("mla_kv_norm", 1, 128), ("mla_w_ukv", 128, 1024),
               ("mla_out_norm", 1, 512), ("hgrn_lb_logits", 2, 512), ("hgrn_out_norm", 1, 512),
               ("attn_post_norm", 1, 1024), ("ffn_pre_norm", 1, 1024), ("ffn_post_norm", 1, 1024), ("loss", 1, 1))
ROW_TILE = 8


def _small_layout():
    offsets, row = {}, 0
    for whole in (True, False):
        for name, rows, _ in SMALL_VIEWS:
            if (rows % ROW_TILE == 0) == whole:
                offsets[name] = row
                row += rows
    return offsets, -(-row // (2 * ROW_TILE)) * 2 * ROW_TILE


SMALL_OFFSETS, SMALL_ROWS = _small_layout()

MESH = pl.DeviceIdType.MESH
ANY = pl.BlockSpec(memory_space=pl.ANY)


def _dot(a, b, dims, exact):
    if exact:
        return lax.dot_general(a.astype(F32), b.astype(F32), (dims, ((), ())), precision=lax.Precision.HIGH,
                               preferred_element_type=F32)
    return lax.dot_general(a.astype(MXU_DTYPE), b.astype(MXU_DTYPE), (dims, ((), ())), preferred_element_type=F32)


def _mm(a, b, exact=False):
    return _dot(a, b, ((1,), (0,)), exact)


def _mm_nt(a, b, exact=False):
    return _dot(a, b, ((1,), (1,)), exact)


def _mm_tn(a, b, exact=False):
    return _dot(a, b, ((0,), (0,)), exact)


def _rms_fwd(x, w):
    r = lax.rsqrt(jnp.mean(x * x, axis=-1, keepdims=True) + EPS)
    xn = x * r
    return xn * w, xn, r


def _rms_bwd(dy, xn, r, w):
    dxn = dy * w
    dx = r * (dxn - xn * jnp.mean(dxn * xn, axis=-1, keepdims=True))
    dw = jnp.sum(dy * xn, axis=0, keepdims=True)
    return dx, dw


def _group_sums(v, gs):
    t, n = v.shape
    lane = lax.broadcasted_iota(jnp.int32, (t, 128), 1)
    out = []
    for p in range(n // 128):
        vb = v[:, 128 * p:128 * (p + 1)]
        if gs == 128:
            out.append(jnp.sum(vb, axis=-1, keepdims=True))
        else:
            out.append(jnp.sum(jnp.where(lane < 64, vb, 0.0), axis=-1, keepdims=True))
            out.append(jnp.sum(jnp.where(lane >= 64, vb, 0.0), axis=-1, keepdims=True))
    return out


def _group_bcast(sums, gs, t):
    lane = lax.broadcasted_iota(jnp.int32, (t, 128), 1)
    if gs == 128:
        return jnp.concatenate([jnp.broadcast_to(s, (t, 128)) for s in sums], axis=-1)
    return jnp.concatenate([jnp.where(lane < 64, sums[2 * p], sums[2 * p + 1]) for p in range(len(sums) // 2)],
                           axis=-1)


def _grms_fwd(x, w, gs):
    t = x.shape[0]
    r = lax.rsqrt(_group_bcast(_group_sums(x * x, gs), gs, t) * (1.0 / gs) + EPS)
    xn = x * r
    return xn * w, xn, r


def _grms_bwd(dy, xn, r, w, gs):
    t = dy.shape[0]
    dxn = dy * w
    dx = r * (dxn - xn * (_group_bcast(_group_sums(dxn * xn, gs), gs, t) * (1.0 / gs)))
    dw = jnp.sum(dy * xn, axis=0, keepdims=True)
    return dx, dw


def _rope_tables(c_tab, s_tab):
    lane = lax.broadcasted_iota(jnp.int32, c_tab.shape, 1)
    first = (lane >= MLA_NOPE) & (lane < MLA_NOPE + MLA_ROPE // 2)
    second = (lane >= MLA_NOPE + MLA_ROPE // 2) & (lane < MLA_QK)
    return c_tab, jnp.where(first, -s_tab, 0.0), jnp.where(second, s_tab, 0.0)


def _rope(v, c, sa, sb):
    return v * c + pltpu.roll(v, HEAD_PAD - MLA_ROPE // 2, 1) * sa + pltpu.roll(v, MLA_ROPE // 2, 1) * sb


def _rope_bwd(d, c, sa, sb):
    return d * c - pltpu.roll(d, HEAD_PAD - MLA_ROPE // 2, 1) * sa - pltpu.roll(d, MLA_ROPE // 2, 1) * sb


def _params(sem, vmem=VMEM_LIMIT):
    return pltpu.CompilerParams(dimension_semantics=sem, vmem_limit_bytes=vmem)


def _in_fwd(x, pos, invf, w_pre, win, qnw, wq, kvnw, wk, wv, tt=512):
    T = x.shape[0]

    def body(x_ref, pos_ref, invf_ref, wpre_ref, win_ref, qnw_ref, wq_ref, kvnw_ref, wk_ref, wv_ref,
             cq_ref, ckv_ref, xph_ref, q_ref, k_ref, v_ref, kt_ref, vt_ref, rc_ref, rs_ref):
        u, _, _ = _rms_fwd(x_ref[...], wpre_ref[...])
        lo = Q_RANK + KV_RANK + HEAD_PAD
        xp = _mm_nt(u, win_ref[:lo, :])
        xph_ref[...] = _mm_nt(u, win_ref[lo:, :])
        cq = xp[:, :Q_RANK]
        ckv = xp[:, Q_RANK:Q_RANK + KV_RANK]
        kr = xp[:, Q_RANK + KV_RANK:]
        cq_ref[...] = cq
        ckv_ref[...] = ckv
        ang = pos_ref[...].astype(F32) * invf_ref[...]
        c_tab = jnp.cos(ang)
        s_tab = jnp.sin(ang)
        rc_ref[...] = c_tab
        rs_ref[...] = s_tab
        c, sa, sb = _rope_tables(c_tab, s_tab)
        qn, _, _ = _rms_fwd(cq, qnw_ref[...])
        q = _mm(qn, wq_ref[...])
        kvn, _, _ = _rms_fwd(ckv, kvnw_ref[...])
        kn = _mm(kvn, wk_ref[...])
        v = _mm(kvn, wv_ref[...])
        v_ref[...] = v.astype(v_ref.dtype)
        vt_ref[...] = v.T.astype(vt_ref.dtype)
        krr = _rope(kr, c, sa, sb)
        for h in range(MLA_HEADS):
            sl = slice(HEAD_PAD * h, HEAD_PAD * (h + 1))
            q_ref[:, sl] = (_rope(q[:, sl], c, sa, sb) * ATTN_SCALE_LOG2).astype(q_ref.dtype)
            kh = kn[:, sl] + krr
            k_ref[:, sl] = kh.astype(k_ref.dtype)
            kt_ref[sl, :] = kh.T.astype(kt_ref.dtype)

    row = lambda w: pl.BlockSpec((tt, w), lambda i: (i, 0))
    full = lambda a: pl.BlockSpec(a.shape, lambda i: (0,) * a.ndim)
    qk_w = MLA_HEADS * HEAD_PAD
    return pl.pallas_call(
        body, name="in_fwd", grid=(T // tt,),
        in_specs=[row(D_MODEL), row(1), full(invf), full(w_pre), full(win), full(qnw), full(wq), full(kvnw),
                  full(wk), full(wv)],
        out_specs=[row(Q_RANK), row(KV_RANK), row(4 * HGRN_WIDTH), row(qk_w), row(qk_w), row(MLA_WIDTH),
                   pl.BlockSpec((qk_w, tt), lambda i: (0, i)), pl.BlockSpec((MLA_WIDTH, tt), lambda i: (0, i)),
                   row(HEAD_PAD), row(HEAD_PAD)],
        out_shape=[jax.ShapeDtypeStruct((T, Q_RANK), F32), jax.ShapeDtypeStruct((T, KV_RANK), F32),
                   jax.ShapeDtypeStruct((T, 4 * HGRN_WIDTH), F32), jax.ShapeDtypeStruct((T, qk_w), MXU_DTYPE),
                   jax.ShapeDtypeStruct((T, qk_w), MXU_DTYPE), jax.ShapeDtypeStruct((T, MLA_WIDTH), MXU_DTYPE),
                   jax.ShapeDtypeStruct((qk_w, T), MXU_DTYPE), jax.ShapeDtypeStruct((MLA_WIDTH, T), MXU_DTYPE),
                   jax.ShapeDtypeStruct((T, HEAD_PAD), F32), jax.ShapeDtypeStruct((T, HEAD_PAD), F32)],
        compiler_params=_params(("arbitrary",)),
    )(x, pos, invf, w_pre, win, qnw, wq, kvnw, wk, wv)


def _attn_fwd_t(qb, kb, vt, gather=(), tq=256, hps=8):
    T = qb.shape[0]
    nq = T // tq
    ng = len(gather)
    steps = (MLA_HEADS // hps) * nq
    pass_on = steps - 3

    def body(q_ref, k_ref, vt_ref, *rest):
        o_ref, lse_ref = rest[ng:ng + 2]
        acc_scr = rest[2 * ng + 2]
        qi = pl.program_id(1)
        step_no = pl.program_id(0) * nq + qi
        if ng:
            gat = _Gather(rest[:ng], rest[ng + 2:2 * ng + 2], *rest[2 * ng + 3:])

            @pl.when(step_no == 0)
            def _():
                for cp in gat.sends():
                    cp.start()

            @pl.when(step_no == pass_on)
            def _():
                for arrival in gat.arrivals():
                    arrival.wait_recv()
                for cp in gat.forwards():
                    cp.start()

        heads = [slice(HEAD_PAD * a, HEAD_PAD * (a + 1)) for a in range(hps)]
        acc_scr[...] = jnp.zeros_like(acc_scr)

        def step(j, carry, masked):
            start = pl.multiple_of(j * tq, tq)
            scores = [_mm_nt(k_ref[pl.ds(start, tq), heads[a]], q_ref[:, heads[a]]) for a in range(hps)]
            new, probs, alphas = [], [], []
            for a in range(hps):
                m, l = carry[a]
                s = scores[a]
                if masked:
                    kk = lax.broadcasted_iota(jnp.int32, (tq, tq), 0)
                    qq = lax.broadcasted_iota(jnp.int32, (tq, tq), 1)
                    s = jnp.where(kk <= qq, s, NEG_BIG)
                m_new = jnp.maximum(m, jnp.max(s, axis=0, keepdims=True))
                alpha = jnp.exp2(m - m_new)
                p = jnp.exp2(s - m_new)
                l = l * alpha + jnp.sum(p, axis=0, keepdims=True)
                new.append((m_new, l))
                probs.append(p.astype(MXU_DTYPE))
                alphas.append(alpha)
                if a % 2:
                    pr = a // 2
                    vtj = vt_ref[2 * MLA_V * pr:2 * MLA_V * (pr + 1), pl.ds(start, tq)]
                    none = jnp.zeros((MLA_V, tq), vtj.dtype)
                    pv = (_mm(jnp.concatenate([vtj[:MLA_V], none], axis=0), probs[a - 1])
                          + _mm(jnp.concatenate([none, vtj[MLA_V:]], axis=0), probs[a]))
                    acc_scr[pr] = acc_scr[pr] * jnp.where(row < MLA_V, alphas[a - 1], alphas[a]) + pv
            return tuple(new)

        row = lax.broadcasted_iota(jnp.int32, (2 * MLA_V, tq), 0)
        init = tuple((jnp.full((1, tq), NEG_BIG, F32), jnp.zeros((1, tq), F32)) for _ in range(hps))
        carry = lax.fori_loop(0, qi, lambda j, c: step(j, c, False), init)
        carry = step(qi, carry, True)
        for pr in range(hps // 2):
            (m0, l0), (m1, l1) = carry[2 * pr], carry[2 * pr + 1]
            ot = acc_scr[pr] / jnp.where(row < MLA_V, l0, l1)
            o_ref[:, 2 * MLA_V * pr:2 * MLA_V * (pr + 1)] = ot.T
            lse_ref[pr, 0:1, :] = m0 + jnp.log2(l0)
            lse_ref[pr, 1:2, :] = m1 + jnp.log2(l1)

        if ng:
            @pl.when(step_no == steps - 1)
            def _():
                for arrival in gat.forward_arrivals():
                    arrival.wait_recv()
                for cp in gat.sends() + gat.forwards():
                    cp.wait_send()

    return pl.pallas_call(
        body, name="attn_fwd", grid=(MLA_HEADS // hps, nq),
        in_specs=[pl.BlockSpec((tq, hps * HEAD_PAD), lambda g, i: (i, g)),
                  pl.BlockSpec((T, hps * HEAD_PAD), lambda g, i: (0, g)),
                  pl.BlockSpec((hps * MLA_V, T), lambda g, i: (g, 0))] + [ANY] * ng,
        out_specs=[pl.BlockSpec((tq, hps * MLA_V), lambda g, i: (i, g)),
                   pl.BlockSpec((hps // 2, 2, tq), lambda g, i: (g, 0, i))] + [ANY] * ng,
        out_shape=[jax.ShapeDtypeStruct((T, MLA_WIDTH), F32), jax.ShapeDtypeStruct((MLA_HEADS // 2, 2, T), F32)]
        + _Gather.out_shapes(gather),
        scratch_shapes=[pltpu.VMEM((hps // 2, 2 * MLA_V, tq), F32)] + (_Gather.semaphores(gather) if ng else []),
        compiler_params=_params(("arbitrary", "arbitrary")),
    )(qb, kb, vt, *gather)


def _attn_bwd_t(qb, kb, kt, vb, dob, lse, dvec, send=(), tq=512, hps=4):
    T = qb.shape[0]
    nq = T // tq
    ns = len(send)
    steps = (MLA_HEADS // hps) * nq

    def body(q_ref, k_ref, kt_ref, v_ref, do_ref, lse_ref, d_ref, *rest):
        dqt_ref, dk_ref, dv_ref = rest[ns:ns + 3]
        va_scr, dv_scr = rest[2 * ns + 3:2 * ns + 5]
        j = pl.program_id(1)
        step_no = pl.program_id(0) * nq + j
        if ns:
            @pl.when(step_no == 0)
            def _():
                for cp in _chip_swap_copies(rest[:ns], rest[ns + 3:2 * ns + 3], *rest[2 * ns + 5:]):
                    cp.start()

        @pl.when(j == 0)
        def _():
            dqt_ref[...] = jnp.zeros_like(dqt_ref)

        lane = lax.broadcasted_iota(jnp.int32, (tq, 2 * MLA_V), 1)
        heads = [slice(HEAD_PAD * a, HEAD_PAD * (a + 1)) for a in range(hps)]
        pairs = [slice(2 * MLA_V * p, 2 * MLA_V * (p + 1)) for p in range(hps // 2)]
        for pr in range(hps // 2):
            vpair = v_ref[:, pairs[pr]]
            va_scr[2 * pr] = jnp.where(lane < MLA_V, vpair, jnp.zeros_like(vpair))
            va_scr[2 * pr + 1] = jnp.where(lane >= MLA_V, vpair, jnp.zeros_like(vpair))
        dk_ref[...] = jnp.zeros_like(dk_ref)
        dv_scr[...] = jnp.zeros_like(dv_scr)

        def step(i, masked):
            start = pl.multiple_of(i * tq, tq)
            rows = pl.ds(start, tq)
            scores = [_mm_nt(k_ref[:, heads[a]], q_ref[rows, heads[a]]) for a in range(hps)]
            dps = [_mm_nt(va_scr[a], do_ref[rows, pairs[a // 2]]) for a in range(hps)]
            for a in range(hps):
                pr, r = a // 2, a % 2
                p = jnp.exp2(scores[a] - lse_ref[pr, r:r + 1, rows])
                if masked:
                    kk = lax.broadcasted_iota(jnp.int32, (tq, tq), 0)
                    qq = lax.broadcasted_iota(jnp.int32, (tq, tq), 1)
                    p = jnp.where(kk <= qq, p, 0.0)
                ds = p * (dps[a] - d_ref[pr, r:r + 1, rows])
                dv_scr[a] += _mm(p, do_ref[rows, pairs[pr]])
                dk_ref[:, heads[a]] += _mm(ds, q_ref[rows, heads[a]])
                dqt_ref[heads[a], rows] += _mm(kt_ref[heads[a], :], ds)

        def loop_body(i, _):
            step(i, False)
            return 0

        step(j, True)
        lax.fori_loop(j + 1, nq, loop_body, 0)
        for pr in range(hps // 2):
            dv_ref[:, pairs[pr]] = jnp.where(lane < MLA_V, dv_scr[2 * pr], dv_scr[2 * pr + 1])
        dk_ref[...] = dk_ref[...] * (ATTN_SCALE / ATTN_SCALE_LOG2)

        if ns:
            @pl.when(step_no == steps - 1)
            def _():
                for cp in _chip_swap_copies(rest[:ns], rest[ns + 3:2 * ns + 3], *rest[2 * ns + 5:]):
                    cp.wait()

    stat = pl.BlockSpec((hps // 2, 2, T), lambda g, j: (g, 0, 0))
    return pl.pallas_call(
        body, name="attn_bwd", grid=(MLA_HEADS // hps, nq),
        in_specs=[pl.BlockSpec((T, hps * HEAD_PAD), lambda g, j: (0, g)),
                  pl.BlockSpec((tq, hps * HEAD_PAD), lambda g, j: (j, g)),
                  pl.BlockSpec((hps * HEAD_PAD, tq), lambda g, j: (g, j)),
                  pl.BlockSpec((tq, hps * MLA_V), lambda g, j: (j, g)),
                  pl.BlockSpec((T, hps * MLA_V), lambda g, j: (0, g)), stat, stat] + [ANY] * ns,
        out_specs=[pl.BlockSpec((hps * HEAD_PAD, T), lambda g, j: (g, 0)),
                   pl.BlockSpec((tq, hps * HEAD_PAD), lambda g, j: (j, g)),
                   pl.BlockSpec((tq, hps * MLA_V), lambda g, j: (j, g))] + [ANY] * ns,
        out_shape=[jax.ShapeDtypeStruct((MLA_HEADS * HEAD_PAD, T), F32),
                   jax.ShapeDtypeStruct((T, MLA_HEADS * HEAD_PAD), F32),
                   jax.ShapeDtypeStruct((T, MLA_WIDTH), F32)] + _chip_swap_shapes(send),
        scratch_shapes=[pltpu.VMEM((hps, tq, 2 * MLA_V), vb.dtype), pltpu.VMEM((hps, tq, 2 * MLA_V), F32)]
        + ([pltpu.SemaphoreType.DMA((3 * ns,)), pltpu.SemaphoreType.DMA((3 * ns,))] if ns else []),
        compiler_params=_params(("arbitrary", "arbitrary")),
    )(qb, kb, kt, vb, dob, lse, dvec, *send)


def _cumsum_rows(x):
    n = x.shape[0]
    row = lax.broadcasted_iota(jnp.int32, x.shape, 0)
    s = 1
    while s < n:
        x = x + jnp.where(row >= s, pltpu.roll(x, s, 0), 0.0)
        s *= 2
    return x


def _rev_cumsum_rows(x):
    n = x.shape[0]
    row = lax.broadcasted_iota(jnp.int32, x.shape, 0)
    s = 1
    while s < n:
        x = x + jnp.where(row < n - s, pltpu.roll(x, n - s, 0), 0.0)
        s *= 2
    return x


def _lb_from_logits(l):
    l0, l1 = l[0:1, :], l[1:2, :]
    m = jnp.maximum(l0, l1)
    e0, e1 = jnp.exp(l0 - m), jnp.exp(l1 - m)
    return e0 / (e0 + e1)


def _hgrn_gates(hq, hf, lb):
    sig_f = jax.nn.sigmoid(hf)
    f = lb + (1.0 - lb) * sig_f
    sig_q = jax.nn.sigmoid(hq)
    return sig_f, f, jnp.log(f), 1.0 - f, sig_q, hq * sig_q


def _hgrn_intra(q, kk, b, exact=False):
    row = lax.broadcasted_iota(jnp.int32, b.shape, 0)
    qs, ks, eqs, eks, a_rows = [], [], [], [], []
    for i in range(CHUNK // SUB):
        ref = b[SUB * i + SUB // 2:SUB * i + SUB // 2 + 1, :]
        eq = jnp.exp(b[SUB * i:SUB * (i + 1), :] - ref)
        ek = jnp.exp(jnp.where(row < SUB * (i + 1), ref - b, NEG_BIG))
        qi = q[SUB * i:SUB * (i + 1), :] * eq
        ki = kk * ek
        a_rows.append(_mm_nt(qi, ki, exact))
        qs.append(qi), ks.append(ki), eqs.append(eq), eks.append(ek)
    tt = lax.broadcasted_iota(jnp.int32, (CHUNK, CHUNK), 0)
    ss = lax.broadcasted_iota(jnp.int32, (CHUNK, CHUNK), 1)
    causal = ss <= tt
    a = jnp.where(causal, jnp.concatenate(a_rows, axis=0), 0.0)
    return a, causal, qs, ks, eqs, eks


def _hgrn_fwd(xph, lbl, tg=512):
    T = xph.shape[0]
    ng, ncg = T // tg, tg // CHUNK
    cols = [slice(HGRN_DIM * h, HGRN_DIM * (h + 1)) for h in range(HGRN_HEADS)]

    def body(lbl_ref, hq_ref, hf_ref, hi_ref, o_ref, st_ref, s_scr):
        @pl.when(pl.program_id(0) == 0)
        def _():
            s_scr[...] = jnp.zeros_like(s_scr)

        lb = _lb_from_logits(lbl_ref[...])

        def chunks(it, _):
            pre = []
            for k in range(HGRN_CPI):
                c = it * HGRN_CPI + k
                rows = pl.ds(pl.multiple_of(c * CHUNK, CHUNK), CHUNK)
                for cs in cols:
                    _, _, lf, kk, _, q = _hgrn_gates(hq_ref[rows, cs], hf_ref[rows, cs], lb[:, cs])
                    v = hi_ref[rows, cs]
                    b = _cumsum_rows(lf)
                    a = _hgrn_intra(q, kk, b)[0]
                    b_last = b[CHUNK - 1:CHUNK, :]
                    pre.append((c, rows, q * jnp.exp(b), a, v, jnp.exp(b_last), _mm_tn(v, kk * jnp.exp(b_last - b))))
            for i, (c, rows, qe, a, v, ebl, upd) in enumerate(pre):
                h = i % HGRN_HEADS
                st = s_scr[h]
                st_ref[h, c] = st
                o_ref[rows, cols[h]] = _mm_nt(qe, st) + _mm(a, v)
                s_scr[h] = st * ebl + upd
            return 0

        lax.fori_loop(0, ncg // HGRN_CPI, chunks, 0)

    col = lambda k: pl.BlockSpec((tg, HGRN_WIDTH), lambda g: (g, k))
    return pl.pallas_call(
        body, name="hgrn_fwd", grid=(ng,),
        in_specs=[pl.BlockSpec((2, HGRN_WIDTH), lambda g: (0, 0)), col(0), col(1), col(2)],
        out_specs=[col(0), pl.BlockSpec((HGRN_HEADS, ncg, HGRN_DIM, HGRN_DIM), lambda g: (0, g, 0, 0))],
        out_shape=[jax.ShapeDtypeStruct((T, HGRN_WIDTH), F32),
                   jax.ShapeDtypeStruct((HGRN_HEADS, T // CHUNK, HGRN_DIM, HGRN_DIM), F32)],
        scratch_shapes=[pltpu.VMEM((HGRN_HEADS, HGRN_DIM, HGRN_DIM), F32)],
        compiler_params=_params(("arbitrary",)),
    )(lbl, xph, xph, xph)


def _hgrn_bwd(xph, lbl, states, d_o, fill=(), tg=512):
    T = xph.shape[0]
    ng, ncg = T // tg, tg // CHUNK
    cols = [slice(HGRN_DIM * h, HGRN_DIM * (h + 1)) for h in range(HGRN_HEADS)]
    nsub = CHUNK // SUB
    nf = len(fill)

    def body(lbl_ref, hq_ref, hf_ref, hi_ref, st_ref, do_ref, *rest):
        dhq_ref, dhf_ref, dhi_ref, dlg_ref = rest[nf:nf + 4]
        ds_scr, dlb_scr = rest[2 * nf + 4:2 * nf + 6]
        fill_copies = lambda: _pair_fill_copies(rest[nf + 4:2 * nf + 4], *rest[2 * nf + 6:])
        g = pl.program_id(0)

        @pl.when(g == 0)
        def _():
            ds_scr[...] = jnp.zeros_like(ds_scr)
            dlb_scr[...] = jnp.zeros_like(dlb_scr)
            for cp in (fill_copies()[0] if nf else ()):
                cp.start()

        lb = _lb_from_logits(lbl_ref[...])

        def chunks(it, _):
            pre = []
            for k, h in ((k, h) for k in range(HGRN_CPI) for h in range(HGRN_HEADS)):
                cs = cols[h]
                c = ncg - 1 - (it * HGRN_CPI + k)
                rows = pl.ds(pl.multiple_of(c * CHUNK, CHUNK), CHUNK)
                hq = hq_ref[rows, cs]
                sig_f, f, lf, kk, sig_q, q = _hgrn_gates(hq, hf_ref[rows, cs], lb[:, cs])
                v = hi_ref[rows, cs]
                do = do_ref[rows, cs]
                b = _cumsum_rows(lf)
                eb = jnp.exp(b)
                a, causal, qs, ks, eqs, eks = _hgrn_intra(q, kk, b)
                b_last = b[CHUNK - 1:CHUNK, :]
                st = st_ref[h, c]
                pre.append(dict(h=h, cs=cs, rows=rows, hq=hq, sig_f=sig_f, f=f, kk=kk, sig_q=sig_q, q=q, v=v, eb=eb, qs=qs,
                                ks=ks, eqs=eqs,
                                eks=eks, ebl=jnp.exp(b_last), el=jnp.exp(b_last - b), st=st,
                                da=jnp.where(causal, _mm_nt(do, v, True), 0.0), dq=_mm(do, st, True) * eb,
                                dv=_mm_tn(a, do), dsu=_mm_tn(do, q * eb, True)))
            for w in pre:
                dq_rows = []
                dk = jnp.zeros_like(w["q"])
                for i in range(nsub):
                    dai = w["da"][SUB * i:SUB * (i + 1), :]
                    dq_rows.append(_mm(dai, w["ks"][i], True) * w["eqs"][i])
                    dk = dk + _mm_tn(dai, w["qs"][i], True) * w["eks"][i]
                w["dq"] = w["dq"] + jnp.concatenate(dq_rows, axis=0)
                w["dk"] = dk
            for w in pre:
                h, cs, rows = w["h"], w["cs"], w["rows"]
                kk, el, ebl, dst = w["kk"], w["el"], w["ebl"], ds_scr[h]
                dk_state = _mm(w["v"], dst, True) * el
                dk = w["dk"] + dk_state
                e_last = (ebl * jnp.sum(w["st"] * dst, axis=0, keepdims=True)
                          + jnp.sum(kk * dk_state, axis=0, keepdims=True))
                dlf = _rev_cumsum_rows(w["q"] * w["dq"] - kk * dk) + e_last
                ds_scr[h] = dst * ebl + w["dsu"]
                df = dlf / w["f"] - dk
                sig_f, sig_q = w["sig_f"], w["sig_q"]
                dhf_ref[rows, cs] = df * (1.0 - lb[:, cs]) * sig_f * (1.0 - sig_f)
                dlb_scr[:, cs] += jnp.sum(df * (1.0 - sig_f), axis=0, keepdims=True)
                dhq_ref[rows, cs] = w["dq"] * sig_q * (1.0 + w["hq"] * (1.0 - sig_q))
                dhi_ref[rows, cs] = w["dv"] + _mm_nt(kk * el, dst)
            return 0

        lax.fori_loop(0, ncg // HGRN_CPI, chunks, 0)

        @pl.when(g == ng - 1)
        def _():
            dl0 = dlb_scr[...] * lb * (1.0 - lb)
            dlg_ref[...] = jnp.concatenate([dl0, -dl0], axis=0)
            if nf:
                copies, waits = fill_copies()
                for w in waits:
                    w.wait_recv()
                for cp in copies:
                    cp.wait_send()

    col = lambda k: pl.BlockSpec((tg, HGRN_WIDTH), lambda g: (ng - 1 - g, k))
    logits = pl.BlockSpec((2, HGRN_WIDTH), lambda g: (0, 0))
    big = jax.ShapeDtypeStruct((T, HGRN_WIDTH), F32)
    n_in, n_out = 6, 4
    return pl.pallas_call(
        body, name="hgrn_bwd", grid=(ng,),
        in_specs=[logits, col(0), col(1), col(2),
                  pl.BlockSpec((HGRN_HEADS, ncg, HGRN_DIM, HGRN_DIM), lambda g: (0, ng - 1 - g, 0, 0)), col(0)] + [ANY] * nf,
        out_specs=[col(0), col(0), col(0), logits] + [ANY] * nf,
        out_shape=[big, big, big, jax.ShapeDtypeStruct((2, HGRN_WIDTH), F32)]
        + [jax.ShapeDtypeStruct(f.shape, f.dtype) for f in fill],
        input_output_aliases={n_in + k: n_out + k for k in range(nf)},
        scratch_shapes=[pltpu.VMEM((HGRN_HEADS, HGRN_DIM, HGRN_DIM), F32), pltpu.VMEM((1, HGRN_WIDTH), F32)]
        + ([pltpu.SemaphoreType.DMA((nf,)), pltpu.SemaphoreType.DMA((nf,))] if nf else []),
        compiler_params=_params(("arbitrary",)),
    )(lbl, xph, xph, xph, states, d_o, *fill)


def _proj_fwd(x, o_raw, oh_raw, xph, wout, w_mla, w_hg, w_post, w_fpre, tt=512):
    T = x.shape[0]

    def body(x_ref, o_ref, oh_ref, hg_ref, wout_ref, wmla_ref, whg_ref, wpost_ref, wfpre_ref,
             h1_ref, z_ref, mix_ref):
        om, _, _ = _grms_fwd(o_ref[...], wmla_ref[...], MLA_V)
        hg = hg_ref[...]
        ohn, _, _ = _grms_fwd(oh_ref[...], whg_ref[...], HGRN_DIM)
        mix = jnp.concatenate([om, ohn * (hg * jax.nn.sigmoid(hg))], axis=-1)
        mix_ref[...] = mix.astype(mix_ref.dtype)
        y1 = _mm(mix, wout_ref[...])
        h1 = x_ref[...] + _rms_fwd(y1, wpost_ref[...])[0]
        h1_ref[...] = h1
        z_ref[...] = _rms_fwd(h1, wfpre_ref[...])[0].astype(z_ref.dtype)

    row = lambda w: pl.BlockSpec((tt, w), lambda i: (i, 0))
    full = lambda a: pl.BlockSpec(a.shape, lambda i: (0,) * a.ndim)
    sds = jax.ShapeDtypeStruct
    return pl.pallas_call(
        body, name="proj_fwd", grid=(T // tt,),
        in_specs=[row(D_MODEL), row(MLA_WIDTH), row(HGRN_WIDTH), pl.BlockSpec((tt, HGRN_WIDTH), lambda i: (i, 3)),
                  full(wout), full(w_mla), full(w_hg), full(w_post), full(w_fpre)],
        out_specs=[row(D_MODEL)] * 3,
        out_shape=[sds((T, D_MODEL), F32), sds((T, D_MODEL), MXU_DTYPE), sds((T, D_MODEL), MXU_DTYPE)],
        compiler_params=_params(("arbitrary",)),
    )(x, o_raw, oh_raw, xph, wout, w_mla, w_hg, w_post, w_fpre)


def _ffn_fwd(zb, h1, tgt, w_fpost, wg, wu, wd, tt=256):
    T = zb.shape[0]
    nj = N_CHIPS

    def body(z_ref, h1_ref, tgt_ref, wfpost_ref, wg_ref, wu_ref, wd_ref, g_ref, up_ref, dy2_ref, dh2_ref, loss_ref, dwf_ref):
        @pl.when(pl.program_id(0) == 0)
        def _():
            loss_ref[...] = jnp.zeros_like(loss_ref)
            dwf_ref[...] = jnp.zeros_like(dwf_ref)

        z = z_ref[...]
        gs = [_mm_nt(z, wg_ref[j]) for j in range(nj)]
        ups = [_mm_nt(z, wu_ref[j]) for j in range(nj)]
        y2 = jnp.zeros((tt, D_MODEL), F32)
        for j in range(nj):
            g_ref[j] = gs[j]
            up_ref[j] = ups[j]
            y2 = y2 + _mm(gs[j] * jax.nn.sigmoid(gs[j]) * ups[j], wd_ref[j])
        w = wfpost_ref[...]
        y2s, y2n, r2 = _rms_fwd(y2, w)
        e = h1_ref[...] + y2s - tgt_ref[...]
        loss_ref[...] += jnp.sum(e * e, axis=0, keepdims=True)
        dh2 = e * (1.0 / D_MODEL)
        dh2_ref[...] = dh2
        dy2, dwf = _rms_bwd(dh2, y2n, r2, w)
        dy2_ref[...] = dy2.astype(dy2_ref.dtype)
        dwf_ref[...] += dwf

    row = pl.BlockSpec((tt, D_MODEL), lambda i: (i, 0))
    vec = pl.BlockSpec((1, D_MODEL), lambda i: (0, 0))
    resident = pl.BlockSpec((nj, FF_SHARD, D_MODEL), lambda i: (0, 0, 0), pipeline_mode=pl.Buffered(1))
    act = pl.BlockSpec((nj, tt, FF_SHARD), lambda i: (0, i, 0))
    sds = jax.ShapeDtypeStruct
    return pl.pallas_call(
        body, name="ffn_fwd", grid=(T // tt,),
        in_specs=[row, row, row, vec, resident, resident, resident],
        out_specs=[act, act, row, row, vec, vec],
        out_shape=[sds((nj, T, FF_SHARD), F32), sds((nj, T, FF_SHARD), F32), sds((T, D_MODEL), MXU_DTYPE),
                   sds((T, D_MODEL), F32), sds((1, D_MODEL), F32), sds((1, D_MODEL), F32)],
        compiler_params=_params(("arbitrary",)),
    )(zb, h1, tgt, w_fpost, wg, wu, wd)


def _ffn_bwd(zb, g, up, dy2b, wg, wu, wd, tt=512):
    T = zb.shape[0]
    nj = N_CHIPS

    def body(z_ref, g_ref, up_ref, dy2_ref, wg_ref, wu_ref, wd_ref, dwg_ref, dwu_ref, dwd_ref, dz_ref, acc_ref):
        j, i = pl.program_id(0), pl.program_id(1)
        rows = pl.ds(pl.multiple_of(i * tt, tt), tt)

        @pl.when(i == 0)
        def _():
            dwg_ref[...] = jnp.zeros_like(dwg_ref)
            dwu_ref[...] = jnp.zeros_like(dwu_ref)
            dwd_ref[...] = jnp.zeros_like(dwd_ref)

        z, g_, up_, dy2 = z_ref[...], g_ref[0], up_ref[0], dy2_ref[...]
        sg = jax.nn.sigmoid(g_)
        act = g_ * sg
        dff = _mm_nt(dy2, wd_ref[0])
        dwd_ref[0] += _mm_tn(act * up_, dy2)
        dg = dff * up_ * sg * (1.0 + g_ * (1.0 - sg))
        dup = dff * act
        dwg_ref[0] += _mm_tn(dg, z)
        dwu_ref[0] += _mm_tn(dup, z)
        dz = _mm(dg, wg_ref[0]) + _mm(dup, wu_ref[0])

        @pl.when(j == 0)
        def _():
            acc_ref[rows, :] = dz

        @pl.when((j > 0) & (j < nj - 1))
        def _():
            acc_ref[rows, :] += dz

        @pl.when(j == nj - 1)
        def _():
            dz_ref[...] = acc_ref[rows, :] + dz

    row = pl.BlockSpec((tt, D_MODEL), lambda j, i: (i, 0))
    act = pl.BlockSpec((1, tt, FF_SHARD), lambda j, i: (j, i, 0))
    w_sh = pl.BlockSpec((1, FF_SHARD, D_MODEL), lambda j, i: (j, 0, 0))
    w_grad = jax.ShapeDtypeStruct((nj, FF_SHARD, D_MODEL), F32)
    return pl.pallas_call(
        body, name="ffn_bwd", grid=(nj, T // tt),
        in_specs=[row, act, act, row, w_sh, w_sh, w_sh],
        out_specs=[w_sh, w_sh, w_sh, pl.BlockSpec((tt, D_MODEL), lambda j, i: (jnp.where(j == nj - 1, i, 0), 0))],
        out_shape=[w_grad, w_grad, w_grad, jax.ShapeDtypeStruct((T, D_MODEL), F32)],
        scratch_shapes=[pltpu.VMEM((T, D_MODEL), F32)],
        compiler_params=_params(("arbitrary", "arbitrary"), vmem=FFN_BWD_VMEM),
    )(zb, g, up, dy2b, wg, wu, wd)


RING = 3


def _mid_bwd(dz, dh2, h1, mixb, o_raw, oh_raw, xph, wout, w_fpre, w_post, w_mla, w_hg, swap=(), tt=512):
    T = dh2.shape[0]
    nsw = len(swap)
    n_in, n_out = 12, 10
    ring_streams = [(D_MODEL, F32)] * 3 + [(D_MODEL, mixb.dtype), (MLA_WIDTH, F32), (HGRN_WIDTH, F32), (HGRN_WIDTH, F32)]
    nr = len(ring_streams)

    def body(*refs):
        (dz_ref, dh2_ref, h1_ref, mix_ref, o_ref, oh_ref, hg_ref, wout_ref, wfpre_ref, wpost_ref,
         wmla_ref, whg_ref) = refs[:n_in]
        (dh1_ref, dwout_ref, do_ref, doh_ref, dhg_ref, dvec_ref, dwfpre_ref, dwpost_ref, dwmla_ref,
         dwhg_ref) = refs[n_in + nsw:n_in + nsw + n_out]
        scratch = refs[n_in + nsw + n_out + (nsw + 1 if nsw else 0):]
        ring_bufs, ring_sem, sems = scratch[:nr], scratch[nr], scratch[nr + 1:]
        step = pl.program_id(0)

        def ring_copy(a, s):
            slot = lax.rem(s, RING)
            src = refs[a].at[pl.ds(pl.multiple_of(s * tt, tt), tt)]
            if a == nr - 1:
                src = refs[a].at[pl.ds(pl.multiple_of(s * tt, tt), tt), pl.ds(3 * HGRN_WIDTH, HGRN_WIDTH)]
            return pltpu.make_async_copy(src, ring_bufs[a].at[slot], ring_sem.at[RING * a + slot])

        swap_copies = lambda: _pair_swap_copies(refs[n_in:n_in + nsw], refs[n_in + nsw + n_out:n_in + 2 * nsw + n_out],
                                                *sems)

        def wout_copies():
            _, _, c, _, sib, _ = _place()
            rw_ref, h = refs[n_in + 2 * nsw + n_out], D_MODEL // N_CHIPS // 2
            return [_rcopy(dwout_ref.at[pl.ds(pl.multiple_of((2 * k + 1 - c) * h, 8), h)], rw_ref.at[k], *sems, nsw + k, sib)
                    for k in range(N_CHIPS)]

        @pl.when(pl.program_id(0) == 0)
        def _():
            for r in (dwout_ref, dwfpre_ref, dwpost_ref, dwmla_ref, dwhg_ref):
                r[...] = jnp.zeros_like(r)
            for cp in (swap_copies() if nsw else ()):
                cp.start()
            for s0 in range(min(RING - 1, T // tt)):
                for a in range(nr):
                    ring_copy(a, s0).start()

        @pl.when(step + RING - 1 < T // tt)
        def _():
            for a in range(nr):
                ring_copy(a, step + RING - 1).start()

        for a in range(nr):
            ring_copy(a, step).wait()
        slot = lax.rem(step, RING)
        dz = ring_bufs[0][slot]
        wfpre = wfpre_ref[...]
        _, h1n, r = _rms_fwd(ring_bufs[2][slot], wfpre)
        dh1_z, dwfpre = _rms_bwd(dz, h1n, r, wfpre)
        dwfpre_ref[...] += dwfpre
        dh1 = ring_bufs[1][slot] + dh1_z
        dh1_ref[...] = dh1
        wpost = wpost_ref[...]
        mixb_ = ring_bufs[3][slot]
        _, y1n, r1 = _rms_fwd(_mm(mixb_, wout_ref[...]), wpost)
        dy1, dwpost = _rms_bwd(dh1, y1n, r1, wpost)
        dwpost_ref[...] += dwpost
        dmix = _mm_nt(dy1, wout_ref[...])
        dwout_ref[...] += _mm_tn(mixb_, dy1)
        wmla = wmla_ref[...]
        o = ring_bufs[4][slot]
        _, on, ro = _grms_fwd(o, wmla, MLA_V)
        d_o, dwmla = _grms_bwd(dmix[:, :MLA_WIDTH], on, ro, wmla, MLA_V)
        dwmla_ref[...] += dwmla
        do_ref[...] = d_o.astype(do_ref.dtype)
        hh = lax.broadcasted_iota(jnp.int32, (MLA_HEADS, MLA_WIDTH), 0)
        ll = lax.broadcasted_iota(jnp.int32, (MLA_HEADS, MLA_WIDTH), 1)
        sel = jnp.where((ll >= hh * MLA_V) & (ll < (hh + 1) * MLA_V), 1.0, 0.0)
        dvec_ref[...] = _mm_nt(sel, d_o * o, True)
        whg = whg_ref[...]
        hg = ring_bufs[6][slot]
        sg = jax.nn.sigmoid(hg)
        _, ohn, rh = _grms_fwd(ring_bufs[5][slot], whg, HGRN_DIM)
        dmh = dmix[:, MLA_WIDTH:]
        dhg_ref[...] = dmh * ohn * whg * sg * (1.0 + hg * (1.0 - sg))
        d_oh, dwhg = _grms_bwd(dmh * (hg * sg), ohn, rh, whg, HGRN_DIM)
        dwhg_ref[...] += dwhg
        doh_ref[...] = d_oh

        if nsw:
            @pl.when(pl.program_id(0) == T // tt - 1)
            def _():
                for cp in wout_copies():
                    cp.start()
                for cp in swap_copies() + wout_copies():
                    cp.wait()

    row = lambda w: pl.BlockSpec((tt, w), lambda i: (i, 0))
    full = lambda a: pl.BlockSpec(a.shape, lambda i: (0,) * a.ndim)
    vec = lambda w: pl.BlockSpec((1, w), lambda i: (0, 0))
    sds = jax.ShapeDtypeStruct
    return pl.pallas_call(
        body, name="mid_bwd", grid=(T // tt,),
        in_specs=[ANY] * nr + [full(wout), vec(D_MODEL), vec(D_MODEL), vec(MLA_WIDTH), vec(HGRN_WIDTH)] + [ANY] * nsw,
        out_specs=[row(D_MODEL), full(wout), row(MLA_WIDTH), row(HGRN_WIDTH), row(HGRN_WIDTH),
                   pl.BlockSpec((MLA_HEADS, tt), lambda i: (0, i)),
                   vec(D_MODEL), vec(D_MODEL), vec(MLA_WIDTH), vec(HGRN_WIDTH)] + [ANY] * (nsw + 1 if nsw else 0),
        out_shape=[sds((T, D_MODEL), F32), sds(wout.shape, F32), sds((T, MLA_WIDTH), MXU_DTYPE), sds((T, HGRN_WIDTH), F32),
                   sds((T, HGRN_WIDTH), F32), sds((MLA_HEADS, T), F32),
                   sds((1, D_MODEL), F32), sds((1, D_MODEL), F32), sds((1, MLA_WIDTH), F32), sds((1, HGRN_WIDTH), F32)]
        + (_half_stack_shapes(list(swap) + [sds((N_CHIPS, D_MODEL // N_CHIPS, D_MODEL), F32)]) if nsw else []),
        scratch_shapes=[pltpu.VMEM((RING, tt, w), dt) for w, dt in ring_streams] + [pltpu.SemaphoreType.DMA((nr * RING,))]
        + ([pltpu.SemaphoreType.DMA((nsw + N_CHIPS,)), pltpu.SemaphoreType.DMA((nsw + N_CHIPS,))] if nsw else []),
        compiler_params=_params(("arbitrary",)),
    )(dz, dh2, h1, mixb, o_raw, oh_raw, xph, wout, w_fpre, w_post, w_mla, w_hg, *swap)


def _in_bwd(x, dh1, cq, ckv, dq, dk, dv, dhq, dhf, dhi, dhg, rc, rs, w_pre, win, qnw, wq, kvnw, wk, wv, tt=256):
    T = x.shape[0]

    def body(x_ref, dh1_ref, cq_ref, ckv_ref, dq_ref, dk_ref, dv_ref, dhq_ref, dhf_ref, dhi_ref, dhg_ref, rc_ref, rs_ref,
             wpre_ref, win_ref, qnw_ref, wq_ref, kvnw_ref, wk_ref, wv_ref,
             dx_ref, dwin_ref, dwq_ref, dwk_ref, dwv_ref, dwpre_ref, dqnw_ref, dkvnw_ref):
        @pl.when(pl.program_id(0) == 0)
        def _():
            for r in (dwin_ref, dwq_ref, dwk_ref, dwv_ref, dwpre_ref, dqnw_ref, dkvnw_ref):
                r[...] = jnp.zeros_like(r)

        def add_win_grad(r, first):
            for arr0, n, chip, row0 in _win_grad_segments():
                if first <= arr0 and arr0 + n <= first + r.shape[0]:
                    dwin_ref[chip, row0:row0 + n, :] += r[arr0 - first:arr0 - first + n]

        lo = Q_RANK + KV_RANK + HEAD_PAD
        dxp_h = jnp.concatenate([dhq_ref[...], dhf_ref[...], dhi_ref[...], dhg_ref[...]], axis=-1)
        du = _mm(dxp_h, win_ref[lo:, :])
        wpre = wpre_ref[...]
        u, xn, rx = _rms_fwd(x_ref[...], wpre)
        add_win_grad(_mm_tn(dxp_h, u), lo)
        c, sa, sb = _rope_tables(rc_ref[...], rs_ref[...])
        lane = lax.broadcasted_iota(jnp.int32, (tt, HEAD_PAD), 1)
        dk_all = dk_ref[...]
        dq_lin = []
        dkr = jnp.zeros((tt, HEAD_PAD), F32)
        for h in range(MLA_HEADS):
            sl = slice(HEAD_PAD * h, HEAD_PAD * (h + 1))
            dq_lin.append(_rope_bwd(dq_ref[sl, :].T * ATTN_SCALE, c, sa, sb))
            dkr = dkr + dk_all[:, sl]
        dq_lin = jnp.concatenate(dq_lin, axis=-1)
        dkr = jnp.where((lane >= MLA_NOPE) & (lane < MLA_QK), _rope_bwd(dkr, c, sa, sb), 0.0)
        qnw = qnw_ref[...]
        qn, cqn, rq = _rms_fwd(cq_ref[...], qnw)
        dwq_ref[...] += _mm_tn(qn, dq_lin)
        dcq, dqnw = _rms_bwd(_mm_nt(dq_lin, wq_ref[...]), cqn, rq, qnw)
        dqnw_ref[...] += dqnw
        kvnw = kvnw_ref[...]
        kvn, ckvn, rkv = _rms_fwd(ckv_ref[...], kvnw)
        dv_ = dv_ref[...]
        dwk_ref[...] += _mm_tn(kvn, dk_all)
        dwv_ref[...] += _mm_tn(kvn, dv_)
        dckv, dkvnw = _rms_bwd(_mm_nt(dk_all, wk_ref[...]) + _mm_nt(dv_, wv_ref[...]), ckvn, rkv, kvnw)
        dkvnw_ref[...] += dkvnw
        dxp_a = jnp.concatenate([dcq, dckv, dkr], axis=-1)
        add_win_grad(_mm_tn(dxp_a, u), 0)
        dx_u, dwpre = _rms_bwd(du + _mm(dxp_a, win_ref[:lo, :]), xn, rx, wpre)
        dwpre_ref[...] += dwpre
        dx_ref[...] = dh1_ref[...] + dx_u

    row = lambda w: pl.BlockSpec((tt, w), lambda i: (i, 0))
    full = lambda a: pl.BlockSpec(a.shape, lambda i: (0,) * a.ndim)
    sds = jax.ShapeDtypeStruct
    qk_w = MLA_HEADS * HEAD_PAD
    return pl.pallas_call(
        body, name="in_bwd", grid=(T // tt,),
        in_specs=[row(D_MODEL), row(D_MODEL), row(Q_RANK), row(KV_RANK), pl.BlockSpec((qk_w, tt), lambda i: (0, i)),
                  row(qk_w), row(MLA_WIDTH),
                  row(HGRN_WIDTH), row(HGRN_WIDTH), row(HGRN_WIDTH), row(HGRN_WIDTH), row(HEAD_PAD), row(HEAD_PAD),
                  full(w_pre), full(win), full(qnw), full(wq), full(kvnw), full(wk), full(wv)],
        out_specs=[row(D_MODEL), pl.BlockSpec(WIN_COMM_SHAPE, lambda i: (0, 0, 0)), full(wq), full(wk), full(wv),
                   full(w_pre), full(qnw), full(kvnw)],
        out_shape=[sds((T, D_MODEL), F32), sds(WIN_COMM_SHAPE, F32), sds(wq.shape, F32), sds(wk.shape, F32),
                   sds(wv.shape, F32), sds(w_pre.shape, F32), sds(qnw.shape, F32), sds(kvnw.shape, F32)],
        compiler_params=_params(("arbitrary",)),
    )(x, dh1, cq, ckv, dq, dk, dv, dhq, dhf, dhi, dhg, rc, rs, w_pre, win, qnw, wq, kvnw, wk, wv)


def _arrange_weights(win_t, wuq_full, wukv):
    dt = win_t.dtype
    z = lambda n: jnp.zeros((n, D_MODEL), dt)
    s2 = Q_RANK + KV_RANK
    win_arr = jnp.concatenate([win_t[:s2], z(MLA_NOPE), win_t[s2:s2 + MLA_ROPE], z(HEAD_PAD - MLA_QK),
                               win_t[s2 + MLA_ROPE:]], axis=0)
    wq_arr = jnp.pad(wuq_full, ((0, 0), (0, 0), (0, HEAD_PAD - MLA_QK))).reshape(Q_RANK, MLA_HEADS * HEAD_PAD)
    wk_arr = jnp.pad(wukv[:, :, :MLA_NOPE], ((0, 0), (0, 0), (0, HEAD_PAD - MLA_NOPE))).reshape(
        KV_RANK, MLA_HEADS * HEAD_PAD)
    wv_arr = wukv[:, :, MLA_NOPE:].reshape(KV_RANK, MLA_WIDTH)
    return win_arr, wq_arr, wk_arr, wv_arr


WIN_COMM_SHAPE = (N_CHIPS, -(-D_IN // N_CHIPS // 32) * 32, D_MODEL)


def _win_grad_segments():
    s2 = Q_RANK + KV_RANK
    runs = [(0, s2, 0), (s2, s2 + MLA_ROPE, MLA_NOPE), (s2 + MLA_ROPE, D_IN, HEAD_PAD - MLA_ROPE)]
    per = D_IN // N_CHIPS
    segs = []
    for lo, hi, shift in runs:
        for k in range(N_CHIPS):
            a, b = max(lo, per * k), min(hi, per * (k + 1))
            if a < b:
                segs.append((a + shift, b - a, k, a - per * k))
    return segs


def _unarrange_grads(dwq_arr, dwk_arr, dwv_arr):
    dwuq = dwq_arr.reshape(Q_RANK, MLA_HEADS, HEAD_PAD)[:, :, :MLA_QK]
    dwukv = jnp.concatenate([dwk_arr.reshape(KV_RANK, MLA_HEADS, HEAD_PAD)[:, :, :MLA_NOPE],
                             dwv_arr.reshape(KV_RANK, MLA_HEADS, MLA_V)], axis=-1)
    return dwuq, dwukv


def _rope_inv_freq():
    inv = 1.0 / (ROPE_THETA ** (jnp.arange(0, MLA_ROPE, 2, dtype=F32) / MLA_ROPE))
    z = lambda n: jnp.zeros((n,), F32)
    return jnp.concatenate([z(MLA_NOPE), inv, inv, z(HEAD_PAD - MLA_QK)]).reshape(1, HEAD_PAD)


def _local_step(x, pos, tgt, small, win_arr, wq_arr, wk_arr, wv_arr, late, place=None):
    invf = _rope_inv_freq()
    cq, ckv, xph, qb, kb, vb, kt, vt, rc, rs = _in_fwd(x, pos, invf, small["attn_pre_norm"], win_arr, small["mla_q_norm"],
                                               wq_arr, small["mla_kv_norm"], wk_arr, wv_arr)
    if place is None:
        o_raw, lse = _attn_fwd_t(qb, kb, vt)
        wout, wg, wu, wd = late
    else:
        o_raw, lse, *stacks = _attn_fwd_t(qb, kb, vt, gather=late)
        wout, wg, wu, wd = [lax.dynamic_update_slice(s, l[None], (place[1], 0, 0)) for s, l in zip(stacks, late)]
        wout = wout.reshape(D_MODEL, D_MODEL)
    oh_raw, states = _hgrn_fwd(xph, small["hgrn_lb_logits"])
    h1, zb, mixb = _proj_fwd(x, o_raw, oh_raw, xph, wout, small["mla_out_norm"], small["hgrn_out_norm"],
                                 small["attn_post_norm"], small["ffn_pre_norm"])
    g, up, dy2b, dh2, loss_acc, d_fpost = _ffn_fwd(zb, h1, tgt, small["ffn_post_norm"], wg, wu, wd)
    dwg, dwu, dwd, dz = _ffn_bwd(zb, g, up, dy2b, wg, wu, wd)
    ffn_grads = [] if place is None else [dwg, dwu, dwd]
    dh1, dwout, d_o, d_oh, dhg, dvec, d_fpre, d_post, d_mla, d_hg, *ffn_rs = _mid_bwd(
        dz, dh2, h1, mixb, o_raw, oh_raw, xph, wout, small["ffn_pre_norm"], small["attn_post_norm"],
        small["mla_out_norm"], small["hgrn_out_norm"], swap=ffn_grads)
    if ffn_grads:
        ffn_grads = ffn_grads + [dwout.reshape(N_CHIPS, D_MODEL // N_CHIPS, D_MODEL)]
    ffn_ps = _pair_sum(place, ffn_grads, ffn_rs, name="pair_sum_ffn") if ffn_grads else []
    dq, dk, dv, *ffn_ris = _attn_bwd_t(qb, kb, kt, vb, d_o, lse, dvec.reshape(lse.shape), send=ffn_ps)
    ffn_sums = _chip_sum(place, ffn_grads, ffn_rs, ffn_ris, name="chip_sum_ffn") if ffn_grads else []
    dhq, dhf, dhi, d_lbl, *ffn_final = _hgrn_bwd(xph, small["hgrn_lb_logits"], states, d_oh, fill=ffn_sums)
    dx, dwin4, dwq_arr, dwk_arr, dwv_arr, d_pre, d_qn, d_kvn = _in_bwd(
        x, dh1, cq, ckv, dq, dk, dv, dhq, dhf, dhi, dhg, rc, rs, small["attn_pre_norm"], win_arr,
        small["mla_q_norm"], wq_arr, small["mla_kv_norm"], wk_arr, wv_arr)
    dwuq, dwukv = _unarrange_grads(dwq_arr, dwk_arr, dwv_arr)
    loss = 0.5 * jnp.sum(loss_acc) * (1.0 / D_MODEL)
    grads = dict(attn_pre_norm=d_pre, w_in=dwin4, mla_q_norm=d_qn, mla_w_uq=dwuq, mla_kv_norm=d_kvn, mla_w_ukv=dwukv,
                 mla_out_norm=d_mla, hgrn_lb_logits=d_lbl, hgrn_out_norm=d_hg, w_out=dwout, attn_post_norm=d_post,
                 ffn_pre_norm=d_fpre, w_gate=dwg, w_up=dwu, w_down=dwd, ffn_post_norm=d_fpost)
    if place is None:
        return loss, dx, grads
    return loss, dx, grads, ffn_final


def _place():
    x, y, c = lax.axis_index("x"), lax.axis_index("y"), lax.axis_index("c")
    others = [(1 - x, y), (x, 1 - y), (1 - x, 1 - y)]
    return x, y, c, 2 * x + y, (x, y, 1 - c), others


def _half(ref, c, rows):
    return ref.at[pl.ds(pl.multiple_of(c * rows, 8), rows)]


def _rcopy(src, dst, send, recv, k, to):
    return pltpu.make_async_remote_copy(src_ref=src, dst_ref=dst, send_sem=send.at[k], recv_sem=recv.at[k],
                                        device_id=to, device_id_type=MESH)


class _Gather:
    def __init__(self, ins, outs, send, recv):
        self.ins, self.outs, self.send, self.recv = ins, outs, send, recv
        self.n = len(ins)
        self.halves = [r.shape[0] // 2 for r in ins]
        _, _, self.c, self.me, self.sib, self.others = _place()

    def _each(self):
        for j, (px, py) in enumerate(self.others):
            for a in range(self.n):
                yield j * self.n + a, a, 2 * px + py, (px, py, self.c)

    def sends(self):
        return [_rcopy(_half(self.ins[a], self.c, self.halves[a]), _half(self.outs[a].at[self.me], self.c, self.halves[a]),
                       self.send, self.recv, k, to) for k, a, _, to in self._each()]

    def arrivals(self):
        parts = [(k, _half(self.outs[a].at[chip], self.c, self.halves[a]), to) for k, a, chip, to in self._each()]
        return [_rcopy(p, p, self.send, self.recv, k, to) for k, p, to in parts]

    def forwards(self):
        parts = [(k, _half(self.outs[a].at[chip], self.c, self.halves[a])) for k, a, chip, _ in self._each()]
        return [_rcopy(p, p, self.send, self.recv, 3 * self.n + k, self.sib) for k, p in parts]

    def forward_arrivals(self):
        parts = [(k, _half(self.outs[a].at[chip], 1 - self.c, self.halves[a])) for k, a, chip, _ in self._each()]
        return [_rcopy(p, p, self.send, self.recv, 3 * self.n + k, self.sib) for k, p in parts]

    @staticmethod
    def out_shapes(arrs):
        return [jax.ShapeDtypeStruct((N_CHIPS,) + a.shape, a.dtype) for a in arrs]

    @staticmethod
    def semaphores(arrs):
        return [pltpu.SemaphoreType.DMA((6 * len(arrs),)), pltpu.SemaphoreType.DMA((6 * len(arrs),))]


def _gather_chips(arrs, name):
    n = len(arrs)

    def body(*refs):
        gat = _Gather(refs[:n], refs[n:2 * n], *refs[2 * n:])
        sends, forwards = gat.sends(), gat.forwards()
        for cp in sends:
            cp.start()
        for arrival, fw in zip(gat.arrivals(), forwards):
            arrival.wait_recv()
            fw.start()
        for arrival in gat.forward_arrivals():
            arrival.wait_recv()
        for cp in sends + forwards:
            cp.wait_send()

    return pl.pallas_call(body, name=name, in_specs=[ANY] * n, out_specs=[ANY] * n, out_shape=_Gather.out_shapes(arrs),
                          scratch_shapes=_Gather.semaphores(arrs))(*arrs)


def _grad_blocks(gs):
    return 2 if all(g.shape[1] // 2 % 32 == 0 for g in gs) else 1


def _pair_swap_copies(g_refs, r_refs, send, recv):
    _, _, c, _, sib, _ = _place()
    copies = []
    for a, (g, r) in enumerate(zip(g_refs, r_refs)):
        h = g.shape[1] // 2
        copies.append(_rcopy(g.at[:, pl.ds(pl.multiple_of((1 - c) * h, 8), h)], r, send, recv, a, sib))
    return copies


def _half_stack_shapes(gs, dtype=None):
    return [jax.ShapeDtypeStruct((N_CHIPS, g.shape[1] // 2, g.shape[2]), dtype or g.dtype) for g in gs]


def _pair_swap(gs, wholes):
    n, nw = len(gs), len(wholes)

    def body(*refs):
        ins, outs, (send, recv) = refs[:n + nw], refs[n + nw:2 * (n + nw)], refs[2 * (n + nw):]
        copies = _pair_swap_copies(ins[:n], outs[:n], send, recv)
        copies += [_rcopy(ins[n + k], outs[n + k], send, recv, n + k, _place()[4]) for k in range(nw)]
        for cp in copies:
            cp.start()
        for cp in copies:
            cp.wait()

    return pl.pallas_call(
        body, name="pair_swap", in_specs=[ANY] * (n + nw), out_specs=[ANY] * (n + nw),
        out_shape=_half_stack_shapes(gs) + [jax.ShapeDtypeStruct(w.shape, w.dtype) for w in wholes],
        scratch_shapes=[pltpu.SemaphoreType.DMA((n + nw,)), pltpu.SemaphoreType.DMA((n + nw,))],
    )(*gs, *wholes)


def _pair_sum(place, gs, rs, small=None, name="pair_sum"):
    n = len(gs)
    nb = _grad_blocks(gs)

    def body(place_ref, *refs):
        g_refs, r_refs, p_refs = refs[:n], refs[n:2 * n], refs[-n - 1:-1] if small else refs[-n:]
        for a in range(n):
            p_refs[a][0] = (g_refs[a][0] + r_refs[a][0]).astype(p_refs[a].dtype)
        if small:
            @pl.when((pl.program_id(0) == 0) & (pl.program_id(1) == 0))
            def _():
                refs[-1][...] = refs[2 * n][...] + refs[2 * n + 1][...]

    chip = lambda k, p: lax.rem(p[1] + 1 + k, N_CHIPS)
    in_specs, out_specs = [], []
    for g in gs:
        blk = (1, g.shape[1] // 2 // nb, g.shape[2])
        in_specs.append(pl.BlockSpec(blk, lambda i, k, p: (chip(k, p), p[0] * nb + i, 0)))
    for g in gs:
        blk = (1, g.shape[1] // 2 // nb, g.shape[2])
        in_specs.append(pl.BlockSpec(blk, lambda i, k, p: (chip(k, p), i, 0)))
        out_specs.append(pl.BlockSpec(blk, lambda i, k, p: (chip(k, p), i, 0)))
    out_shape = _half_stack_shapes(gs, BF16)
    if small:
        sm_spec = pl.BlockSpec(small[0].shape, lambda i, k, p: (0, 0))
        in_specs += [sm_spec, sm_spec]
        out_specs.append(sm_spec)
        out_shape.append(jax.ShapeDtypeStruct(small[0].shape, F32))
    return pl.pallas_call(
        body, name=name,
        grid_spec=pltpu.PrefetchScalarGridSpec(num_scalar_prefetch=1, grid=(nb, N_CHIPS - 1), in_specs=in_specs,
                                               out_specs=out_specs),
        out_shape=out_shape,
        compiler_params=_params(("arbitrary", "arbitrary")),
    )(place, *gs, *rs, *(small or ()))


def _chip_swap_copies(p_refs, ri_refs, send, recv):
    _, _, c, _, _, others = _place()
    n = len(p_refs)
    return [_rcopy(p_refs[a].at[2 * px + py], ri_refs[a].at[j], send, recv, j * n + a, (px, py, c))
            for j, (px, py) in enumerate(others) for a in range(n)]


def _chip_swap_shapes(ps):
    return [jax.ShapeDtypeStruct((3,) + p.shape[1:], p.dtype) for p in ps]


def _chip_swap(ps, pair):
    n = len(ps)

    def body(*refs):
        start, finish = _chip_swap_plan(refs[:n], refs[n], refs[n + 1:2 * n + 1], refs[2 * n + 1], *refs[2 * n + 2:])
        start()
        finish()

    return pl.pallas_call(
        body, name="chip_swap", in_specs=[ANY] * (n + 1), out_specs=[ANY] * (n + 1),
        out_shape=_chip_swap_out_shapes(ps, pair), scratch_shapes=_chip_swap_semaphores(n),
    )(*ps, pair)


def _chip_swap_plan(p_refs, pair_ref, ri_refs, sm4_ref, send, recv, lsem):
    n = len(p_refs)
    hs = SMALL_ROWS // 2
    x, y, c, me, sib, others = _place()
    local = pltpu.make_async_copy(pair_ref, sm4_ref.at[me], lsem.at[0])
    copies = _chip_swap_copies(p_refs, ri_refs, send, recv)
    arrivals = list(copies)
    for j, (px, py) in enumerate(others):
        copies.append(_rcopy(_half(pair_ref, c, hs), _half(sm4_ref.at[me], c, hs), send, recv, 3 * n + j, (px, py, c)))
        part = _half(sm4_ref.at[2 * px + py], c, hs)
        arrivals.append(_rcopy(part, part, send, recv, 3 * n + j, (px, py, c)))

    def start():
        local.start()
        for cp in copies:
            cp.start()

    def finish():
        for arrival in arrivals:
            arrival.wait_recv()
        for cp in copies:
            cp.wait_send()
        local.wait()

    return start, finish


def _chip_swap_out_shapes(ps, pair):
    return _chip_swap_shapes(ps) + [jax.ShapeDtypeStruct((N_CHIPS,) + pair.shape, pair.dtype)]


def _chip_swap_semaphores(n):
    k = 3 * (n + 1)
    return [pltpu.SemaphoreType.DMA((k,)), pltpu.SemaphoreType.DMA((k,)), pltpu.SemaphoreType.DMA((1,))]


def _chip_sum(place, gs, rs, ris, name="chip_sum"):
    n = len(gs)
    nb = 1

    def body(place_ref, *refs):
        g_refs, r_refs, ri_refs, o_refs = refs[:n], refs[n:2 * n], refs[2 * n:3 * n], refs[3 * n:]
        for a in range(n):
            ri = ri_refs[a]
            o_refs[a][...] = (g_refs[a][0] + r_refs[a][0]) + ri[0].astype(F32) + ri[1].astype(F32) + ri[2].astype(F32)

    in_specs, out_specs, out_shape = [], [], []
    for g in gs:
        blk = (1, g.shape[1] // 2 // nb, g.shape[2])
        in_specs.append(pl.BlockSpec(blk, lambda i, p: (p[1], p[0] * nb + i, 0)))
    for g in gs:
        blk = (1, g.shape[1] // 2 // nb, g.shape[2])
        in_specs.append(pl.BlockSpec(blk, lambda i, p: (p[1], i, 0)))
    for g in gs:
        rb = g.shape[1] // 2 // nb
        in_specs.append(pl.BlockSpec((3, rb, g.shape[2]), lambda i, p: (0, i, 0)))
        out_specs.append(pl.BlockSpec((rb, g.shape[2]), lambda i, p: (p[0] * nb + i, 0)))
        out_shape.append(jax.ShapeDtypeStruct(g.shape[1:], F32))
    return pl.pallas_call(
        body, name=name,
        grid_spec=pltpu.PrefetchScalarGridSpec(num_scalar_prefetch=1, grid=(nb,), in_specs=in_specs, out_specs=out_specs),
        out_shape=out_shape,
        compiler_params=_params(("arbitrary",)),
    )(place, *gs, *rs, *ris)


def _pair_fill_copies(g_refs, send, recv):
    _, _, c, _, sib, _ = _place()
    copies, waits = [], []
    for a, g in enumerate(g_refs):
        h = g.shape[0] // 2
        mine, theirs = _half(g, c, h), _half(g, 1 - c, h)
        copies.append(_rcopy(mine, mine, send, recv, a, sib))
        waits.append(_rcopy(theirs, theirs, send, recv, a, sib))
    return copies, waits


def _pair_fill(gfs, sm4):
    n = len(gfs)
    hs = SMALL_ROWS // 2

    def body(*refs):
        g_refs, sm4_ref = refs[n + 1:2 * n + 1], refs[2 * n + 1]
        send, recv = refs[2 * n + 2:]
        x, y, c, me, sib, others = _place()
        copies, waits = _pair_fill_copies(g_refs, send, recv)
        for j, (px, py) in enumerate(others):
            chip = 2 * px + py
            mine, theirs = _half(sm4_ref.at[chip], c, hs), _half(sm4_ref.at[chip], 1 - c, hs)
            copies.append(pltpu.make_async_remote_copy(src_ref=mine, dst_ref=mine, send_sem=send.at[n + j],
                                                       recv_sem=recv.at[n + j], device_id=sib, device_id_type=MESH))
            waits.append(pltpu.make_async_remote_copy(src_ref=theirs, dst_ref=theirs, send_sem=send.at[n + j],
                                                      recv_sem=recv.at[n + j], device_id=sib, device_id_type=MESH))
        for cp in copies:
            cp.start()
        for w in waits:
            w.wait_recv()
        for cp in copies:
            cp.wait_send()

    return pl.pallas_call(
        body, name="pair_fill", in_specs=[ANY] * (n + 1), out_specs=[ANY] * (n + 1),
        out_shape=[jax.ShapeDtypeStruct(g.shape, g.dtype) for g in gfs] + [jax.ShapeDtypeStruct(sm4.shape, sm4.dtype)],
        input_output_aliases={i: i for i in range(n + 1)},
        scratch_shapes=[pltpu.SemaphoreType.DMA((n + 3,)), pltpu.SemaphoreType.DMA((n + 3,))],
    )(*gfs, sm4)


def _adamw_math(w, g, m, v):
    m = ADAM_B1 * m + (1.0 - ADAM_B1) * g
    v = ADAM_B2 * v + (1.0 - ADAM_B2) * (g * g)
    m_hat = m / (1.0 - ADAM_B1 ** ADAM_STEP)
    v_hat = v / (1.0 - ADAM_B2 ** ADAM_STEP)
    return -ADAM_LR * (m_hat / (jnp.sqrt(v_hat) + ADAM_EPS) + ADAM_WD * w), m, v


def _adamw(items, steps, name):
    n = len(items)

    def body(*refs):
        for a in range(n):
            g = refs[4 * a + 1][...]
            d, mo, vo = _adamw_math(refs[4 * a][...], g, refs[4 * a + 2][...], refs[4 * a + 3][...])
            for out, val in zip(refs[4 * n + 4 * a:4 * n + 4 * a + 4], (g, d, mo, vo)):
                out[...] = val

    spec = lambda w: pl.BlockSpec((w.shape[0] // steps, w.shape[1]), lambda i: (i, 0))
    flat = pl.pallas_call(
        body, name=name, grid=(steps,), in_specs=[spec(it[0]) for it in items for _ in range(4)],
        out_specs=[spec(it[0]) for it in items for _ in range(4)],
        out_shape=[jax.ShapeDtypeStruct(it[0].shape, F32) for it in items for _ in range(4)],
        compiler_params=_params(("arbitrary",)),
    )(*[a for it in items for a in it])
    return [flat[4 * a:4 * a + 4] for a in range(n)]


def _adamw_small(sm4, wmv):
    views = SMALL_VIEWS[:-1]
    n = len(views)

    def body(sm4_ref, *refs):
        g_all = ((sm4_ref[0] + sm4_ref[1]) + sm4_ref[2]) + sm4_ref[3]
        for a, (name, rows, cols) in enumerate(views):
            row = SMALL_OFFSETS[name]
            g = g_all[row:row + rows, :cols]
            d, mo, vo = _adamw_math(refs[3 * a][...], g, refs[3 * a + 1][...], refs[3 * a + 2][...])
            for out, val in zip(refs[3 * n + 4 * a:3 * n + 4 * a + 4], (g, d, mo, vo)):
                out[...] = val
        row = SMALL_OFFSETS["loss"]
        refs[-1][...] = g_all[row:row + 1, :128]

    flat = pl.pallas_call(
        body, name="adamw_small",
        out_shape=[jax.ShapeDtypeStruct((rows, cols), F32) for _, rows, cols in views for _ in range(4)]
        + [jax.ShapeDtypeStruct((1, 128), F32)],
        compiler_params=pltpu.CompilerParams(vmem_limit_bytes=VMEM_LIMIT),
    )(sm4, *[a for t in wmv for a in t])
    return [flat[4 * a:4 * a + 4] for a in range(n)] + [flat[-1]]


SMALL_NAMES = ("attn_pre_norm", "mla_q_norm", "mla_kv_norm", "mla_w_ukv", "mla_out_norm", "hgrn_lb_logits",
               "hgrn_out_norm", "attn_post_norm", "ffn_pre_norm", "ffn_post_norm")
BIG_NAMES = ("w_in", "mla_w_uq", "w_out", "w_gate", "w_up", "w_down")
WEIGHT_NAMES = ("attn_pre_norm", "w_in", "mla_q_norm", "mla_w_uq", "mla_kv_norm", "mla_w_ukv", "mla_out_norm",
                "hgrn_lb_logits", "hgrn_out_norm", "w_out", "attn_post_norm", "ffn_pre_norm", "w_gate", "w_up", "w_down",
                "ffn_post_norm")


UQ_COMM_SHAPE = (192, 384)


def _pack_small(vals):
    parts, row = [], 0
    for name, rows, cols in sorted(SMALL_VIEWS, key=lambda view: SMALL_OFFSETS[view[0]]):
        assert SMALL_OFFSETS[name] == row
        parts.append(jnp.pad(vals[name].reshape(rows, cols), ((0, 0), (0, D_MODEL - cols))))
        row += rows
    parts.append(jnp.zeros((SMALL_ROWS - row, D_MODEL), F32))
    return jnp.concatenate(parts, axis=0)


def kernel(x, positions, attn_pre_norm, w_in, mla_q_norm, mla_w_uq, mla_kv_norm, mla_w_ukv, mla_out_norm, hgrn_lb_logits, hgrn_out_norm, w_out, attn_post_norm, ffn_pre_norm, w_gate, w_up, w_down, ffn_post_norm, loss_target, m_attn_pre_norm, m_w_in, m_mla_q_norm, m_mla_w_uq, m_mla_kv_norm, m_mla_w_ukv, m_mla_out_norm, m_hgrn_lb_logits, m_hgrn_out_norm, m_w_out, m_attn_post_norm, m_ffn_pre_norm, m_w_gate, m_w_up, m_w_down, m_ffn_post_norm, v_attn_pre_norm, v_w_in, v_mla_q_norm, v_mla_w_uq, v_mla_kv_norm, v_mla_w_ukv, v_mla_out_norm, v_hgrn_lb_logits, v_hgrn_out_norm, v_w_out, v_attn_post_norm, v_ffn_pre_norm, v_w_gate, v_w_up, v_w_down, v_ffn_post_norm):
    args = locals()
    W = {n: args[n] for n in WEIGHT_NAMES}
    M = {n: args["m_" + n] for n in WEIGHT_NAMES}
    V = {n: args["v_" + n] for n in WEIGHT_NAMES}
    T = x.shape[1]
    cx, cy, cc = lax.axis_index("x"), lax.axis_index("y"), lax.axis_index("c")

    win_rows = D_IN // N_CHIPS
    shard2d = {"w_in": (win_rows, D_MODEL), "mla_w_uq": (Q_RANK // N_CHIPS, MLA_HEADS * MLA_QK),
               "w_out": (D_MODEL // N_CHIPS, D_MODEL), "w_gate": (FF_SHARD, D_MODEL), "w_up": (FF_SHARD, D_MODEL),
               "w_down": (FF_SHARD, D_MODEL)}
    transposed = ("w_in", "w_gate", "w_up")
    to2d = lambda n, a: a[0].T if n in transposed else a.reshape(shard2d[n])
    from2d = lambda n, t: t.T[None] if n in transposed else t.reshape(W[n].shape)
    me = 2 * cx + cy
    place = jnp.stack([cc, me]).astype(jnp.int32)
    local_b = [to2d(n, W[n]).astype(BF16) for n in BIG_NAMES]
    local_b[0] = jnp.pad(local_b[0], ((0, WIN_COMM_SHAPE[1] - win_rows), (0, 0)))
    stacks = _gather_chips(local_b[:2], "gather_weights")
    win4, wuq4 = [lax.dynamic_update_slice(s, l[None], (me, 0, 0)) for s, l in zip(stacks, local_b)]
    win_t = win4[:, :win_rows].reshape(D_IN, D_MODEL)
    wuq_full = wuq4.reshape(Q_RANK, MLA_HEADS, MLA_QK)
    win_arr, wq_arr, wk_arr, wv_arr = _arrange_weights(win_t, wuq_full, mla_w_ukv[0].astype(BF16))
    small = {n: W[n][0] if n == "mla_w_ukv" else W[n].reshape(-1, W[n].shape[-1]) for n in SMALL_NAMES}

    loss_local, dx, grads, ffn_final = _local_step(x[0], positions.reshape(T, 1), loss_target[0], small, win_arr,
                                                           wq_arr, wk_arr, wv_arr, local_b[2:], place)

    gs = [grads["w_in"], grads["mla_w_uq"].reshape((N_CHIPS,) + UQ_COMM_SHAPE)]
    sm = _pack_small({**grads, "loss": loss_local})
    *rs, ssib = _pair_swap(gs, (sm,))
    *ps, pair = _pair_sum(place, gs, rs, small=(sm, ssib))
    ffn_names, rest_names = BIG_NAMES[3:], BIG_NAMES[:2]
    g2d = dict(zip(ffn_names + BIG_NAMES[2:3], ffn_final))
    adam_in = lambda names_: [(to2d(n, W[n]), g2d[n], to2d(n, M[n]), to2d(n, V[n])) for n in names_]
    early_names = ffn_names + BIG_NAMES[2:3]
    updates = dict(zip(early_names, _adamw(adam_in(early_names), 4, "adamw_ffn")))
    *ris, sm4 = _chip_swap(ps, pair)
    *gfin, smf = _pair_fill(_chip_sum(place, gs, rs, ris), sm4)

    g2d.update({n: gfin[k].reshape((-1,) + shard2d[n][1:]) for k, n in enumerate(rest_names)})
    updates.update(zip(rest_names, _adamw(adam_in(rest_names), 1, "adamw_w_in")))
    G, DW, NM, NV = {}, {}, {}, {}
    for n, outs in updates.items():
        G[n], DW[n], NM[n], NV[n] = (from2d(n, t) for t in outs)
    view2d = lambda n, a: a.reshape(next((r, c) for name, r, c in SMALL_VIEWS if name == n))
    *res, loss_row = _adamw_small(smf, [tuple(view2d(n, t[n]) for t in (W, M, V)) for n in SMALL_NAMES])
    for n, outs in zip(SMALL_NAMES, res):
        G[n], DW[n], NM[n], NV[n] = (t.reshape(W[n].shape) for t in outs)
    loss = loss_row[0, 0]
    return (loss, dx[None], *[G[n] for n in WEIGHT_NAMES], *[DW[n] for n in WEIGHT_NAMES],
            *[NM[n] for n in WEIGHT_NAMES], *[NV[n] for n in WEIGHT_NAMES])
```
